```python
import jax, jax.numpy as jnp
from jax import lax
import numpy as np

D_MODEL = 1024
BATCH = 8
SEQ = 4096
DEPTH = 1

CHUNK = 64
N_META = 16
Q_BLOCK = 128
N_PAD = Q_BLOCK - N_META
PREFIX = N_PAD + N_META
MIX_WIDTH = D_MODEL
RET_HEADS = 4
RET_DV = (MIX_WIDTH // 2) // RET_HEADS
RET_DK = RET_DV // 2
FOX_HEADS = 8
FOX_DH = (MIX_WIDTH // 2) // FOX_HEADS
D_FF = 2816
CONV_W = 3
ROPE_BASE = 10000.0
EPS = 1e-6
NEG = -1e30

RET_QK = RET_HEADS * RET_DK
RET_V = RET_HEADS * RET_DV
FOX_W = FOX_HEADS * FOX_DH
SPLIT_POINTS = (RET_QK, 2 * RET_QK, 2 * RET_QK + RET_V, 2 * RET_QK + 2 * RET_V,
                2 * RET_QK + 2 * RET_V + FOX_W, 2 * RET_QK + 2 * RET_V + 2 * FOX_W,
                2 * RET_QK + 2 * RET_V + 3 * FOX_W)
IN_WIDTH = 2 * RET_QK + 2 * RET_V + 3 * FOX_W + FOX_HEADS

kernel_name = "hymba_retention_fox_convffn_block"


def rms_norm(x, g):
    xf = x.astype(jnp.float32)
    y = xf * lax.rsqrt(jnp.mean(xf * xf, axis=-1, keepdims=True) + EPS)
    return (y * g.astype(jnp.float32)).astype(x.dtype)


def rotary(x, pos):
    half = x.shape[-1] // 2
    inv = 1.0 / (ROPE_BASE ** (jnp.arange(half, dtype=jnp.float32) / half))
    ang = pos.astype(jnp.float32)[:, None] * inv[None, :]
    cos = jnp.cos(ang)[None, :, None, :]
    sin = jnp.sin(ang)[None, :, None, :]
    x1 = x[..., :half].astype(jnp.float32)
    x2 = x[..., half:].astype(jnp.float32)
    return jnp.concatenate([x1 * cos - x2 * sin, x1 * sin + x2 * cos], axis=-1)


def retention(q, k, v, valid):
    B, L, H, dk = q.shape
    dv = v.shape[-1]
    nc = L // CHUNK
    f32 = jnp.float32
    log_g = jnp.log1p(-jnp.exp2(-5.0 - jnp.arange(H, dtype=f32)))
    pos = jnp.arange(L)
    qr = rotary(q, pos)
    kr = rotary(k, pos) * (dk ** -0.5) * valid.astype(f32)[None, :, None, None]
    qc = qr.reshape(B, nc, CHUNK, H, dk)
    kc = kr.reshape(B, nc, CHUNK, H, dk)
    vc = v.astype(f32).reshape(B, nc, CHUNK, H, dv)
    n = jnp.arange(CHUNK, dtype=f32)
    d_intra = jnp.exp(jnp.abs(n[:, None] - n[None, :])[None] * log_g[:, None, None])
    s = jnp.einsum('bcnhd,bcmhd->bchnm', qc, kc) * d_intra[None, None]
    intra = jnp.einsum('bchnm,bcmhe->bcnhe', s, vc)
    w_k = jnp.exp((CHUNK - 1.0 - n)[:, None] * log_g[None, :])
    u = jnp.einsum('bcmhd,bcmhe->bchde', kc * w_k[None, None, :, :, None], vc)
    g_chunk = jnp.exp(CHUNK * log_g)[None, :, None, None]

    def step(r, u_i):
        return g_chunk * r + u_i, r

    _, r_prev = lax.scan(step, jnp.zeros((B, H, dk, dv), f32), jnp.moveaxis(u, 1, 0))
    r_prev = jnp.moveaxis(r_prev, 0, 1)
    w_q = jnp.exp((n + 1.0)[:, None] * log_g[None, :])
    inter = jnp.einsum('bcnhd,bchde->bcnhe', qc * w_q[None, None, :, :, None], r_prev)
    return (intra + inter).reshape(B, L, H, dv)


def forgetting_attention(q, k, v, log_f, valid):
    B, L, H, dh = q.shape
    f32 = jnp.float32
    scale = dh ** -0.5
    c = jnp.cumsum(log_f.astype(f32), axis=1).transpose(0, 2, 1)
    qf, kf, vf = q.astype(f32), k.astype(f32), v.astype(f32)
    pos = jnp.arange(L)
    outs = []
    for blk in range(L // Q_BLOCK):
        q0, q1 = blk * Q_BLOCK, (blk + 1) * Q_BLOCK
        logits = jnp.einsum('bqhd,bkhd->bhqk', qf[:, q0:q1], kf[:, :q1]) * scale
        bias = c[:, :, q0:q1, None] - c[:, :, None, :q1]
        mask = (pos[None, :q1] <= pos[q0:q1, None]) & valid[None, :q1]
        logits = jnp.where(mask[None, None], logits + bias, NEG)
        p = jax.nn.softmax(logits, axis=-1)
        outs.append(jnp.einsum('bhqk,bkhd->bqhd', p, vf[:, :q1]))
    return jnp.concatenate(outs, axis=1)


def hybrid_mixer(h, w_in, forget_b, ret_norm_g, w_out, valid):
    B, L, _ = h.shape
    proj = jnp.einsum('bld,de->ble', h, w_in)
    rq, rk, rv, rg, fq, fk, fv, ff = jnp.split(proj, SPLIT_POINTS, axis=-1)
    o_r = retention(rq.reshape(B, L, RET_HEADS, RET_DK), rk.reshape(B, L, RET_HEADS, RET_DK),
                    rv.reshape(B, L, RET_HEADS, RET_DV), valid)
    o_r = o_r * lax.rsqrt(jnp.mean(o_r * o_r, axis=-1, keepdims=True) + EPS)
    o_r = o_r.reshape(B, L, RET_V) * ret_norm_g.astype(jnp.float32) * jax.nn.silu(rg.astype(jnp.float32))
    log_f = jax.nn.log_sigmoid(ff.astype(jnp.float32) + forget_b.astype(jnp.float32))
    o_f = forgetting_attention(fq.reshape(B, L, FOX_HEADS, FOX_DH), fk.reshape(B, L, FOX_HEADS, FOX_DH),
                               fv.reshape(B, L, FOX_HEADS, FOX_DH), log_f, valid).reshape(B, L, FOX_W)
    mixed = jnp.concatenate([o_r, o_f], axis=-1).astype(h.dtype)
    return jnp.einsum('ble,ed->bld', mixed, w_out)


def conv_ffn(h, w_up, conv_w, conv_b, w_down, valid):
    L = h.shape[1]
    up = jnp.einsum('bld,df->blf', h, w_up)
    a, b = jnp.split(up, 2, axis=-1)
    a = a * valid.astype(a.dtype)[None, :, None]
    a_pad = jnp.pad(a, ((0, 0), (CONV_W - 1, 0), (0, 0)))
    acc = conv_b
    for j in range(CONV_W):
        acc = acc + a_pad[:, j:j + L] * conv_w[j]
    return jnp.einsum('blf,fd->bld', jax.nn.silu(acc) * b, w_down)


def _fwd_setup_inputs(seed: int = 0) -> dict:
    key = jax.random.key(seed)
    ks = jax.random.split(key, 13)
    f32 = jnp.float32
    return {
        "x": jax.random.normal(ks[0], (BATCH, SEQ, D_MODEL), f32),
        "meta_tokens": jax.random.normal(ks[1], (N_META, D_MODEL), f32),
        "attn_norm_g": 1.0 + 0.02 * jax.random.normal(ks[2], (DEPTH, D_MODEL), f32),
        "w_in": jax.random.normal(ks[3], (DEPTH, D_MODEL, IN_WIDTH), f32) * D_MODEL ** -0.5,
        "fox_forget_b": jax.random.uniform(ks[4], (DEPTH, FOX_HEADS), f32, 1.0, 5.0),
        "ret_norm_g": 1.0 + 0.02 * jax.random.normal(ks[5], (DEPTH, RET_V), f32),
        "w_out": jax.random.normal(ks[6], (DEPTH, MIX_WIDTH, D_MODEL), f32) * MIX_WIDTH ** -0.5,
        "ffn_norm_g": 1.0 + 0.02 * jax.random.normal(ks[7], (DEPTH, D_MODEL), f32),
        "w_up": jax.random.normal(ks[8], (DEPTH, D_MODEL, 2 * D_FF), f32) * D_MODEL ** -0.5,
        "conv_w": jax.random.normal(ks[9], (DEPTH, CONV_W, D_FF), f32) * CONV_W ** -0.5,
        "conv_b": 0.02 * jax.random.normal(ks[10], (DEPTH, D_FF), f32),
        "w_down": jax.random.normal(ks[11], (DEPTH, D_FF, D_MODEL), f32) * D_FF ** -0.5,
        "final_norm_g": 1.0 + 0.02 * jax.random.normal(ks[12], (D_MODEL,), f32),
    }


def _fwd_reference(x, meta_tokens, attn_norm_g, w_in, fox_forget_b, ret_norm_g, w_out,
              ffn_norm_g, w_up, conv_w, conv_b, w_down, final_norm_g):
    B = x.shape[0]
    pad = jnp.zeros((B, N_PAD, D_MODEL), x.dtype)
    meta = jnp.broadcast_to(meta_tokens.astype(x.dtype)[None], (B, N_META, D_MODEL))
    h = jnp.concatenate([pad, meta, x], axis=1)
    valid = jnp.arange(h.shape[1]) >= N_PAD
    for layer in range(DEPTH):
        h = h + hybrid_mixer(rms_norm(h, attn_norm_g[layer]), w_in[layer], fox_forget_b[layer],
                             ret_norm_g[layer], w_out[layer], valid)
        h = h + conv_ffn(rms_norm(h, ffn_norm_g[layer]), w_up[layer], conv_w[layer], conv_b[layer],
                         w_down[layer], valid)
    return rms_norm(h, final_norm_g)[:, PREFIX:]


import jax as _jax
import jax.numpy as _jnp

TWIN_FORMAT = 'train_step'
FWD_PARAMS = ['x', 'meta_tokens', 'attn_norm_g', 'w_in', 'fox_forget_b', 'ret_norm_g', 'w_out', 'ffn_norm_g', 'w_up', 'conv_w', 'conv_b', 'w_down', 'final_norm_g']
TWIN_WEIGHTS = ['meta_tokens', 'attn_norm_g', 'w_in', 'fox_forget_b', 'ret_norm_g', 'w_out', 'ffn_norm_g', 'w_up', 'conv_w', 'conv_b', 'w_down', 'final_norm_g']
TWIN_DIFF_INPUT = 'x'
TWIN_INPUTS = ['x', 'meta_tokens', 'attn_norm_g', 'w_in', 'fox_forget_b', 'ret_norm_g', 'w_out', 'ffn_norm_g', 'w_up', 'conv_w', 'conv_b', 'w_down', 'final_norm_g', 'loss_target', 'm_meta_tokens', 'm_attn_norm_g', 'm_w_in', 'm_fox_forget_b', 'm_ret_norm_g', 'm_w_out', 'm_ffn_norm_g', 'm_w_up', 'm_conv_w', 'm_conv_b', 'm_w_down', 'm_final_norm_g', 'v_meta_tokens', 'v_attn_norm_g', 'v_w_in', 'v_fox_forget_b', 'v_ret_norm_g', 'v_w_out', 'v_ffn_norm_g', 'v_w_up', 'v_conv_w', 'v_conv_b', 'v_w_down', 'v_final_norm_g']
TWIN_OUTPUTS = ['loss', 'grad_x', 'grad_meta_tokens', 'grad_attn_norm_g', 'grad_w_in', 'grad_fox_forget_b', 'grad_ret_norm_g', 'grad_w_out', 'grad_ffn_norm_g', 'grad_w_up', 'grad_conv_w', 'grad_conv_b', 'grad_w_down', 'grad_final_norm_g', 'delta_meta_tokens', 'delta_attn_norm_g', 'delta_w_in', 'delta_fox_forget_b', 'delta_ret_norm_g', 'delta_w_out', 'delta_ffn_norm_g', 'delta_w_up', 'delta_conv_w', 'delta_conv_b', 'delta_w_down', 'delta_final_norm_g', 'new_m_meta_tokens', 'new_m_attn_norm_g', 'new_m_w_in', 'new_m_fox_forget_b', 'new_m_ret_norm_g', 'new_m_w_out', 'new_m_ffn_norm_g', 'new_m_w_up', 'new_m_conv_w', 'new_m_conv_b', 'new_m_w_down', 'new_m_final_norm_g', 'new_v_meta_tokens', 'new_v_attn_norm_g', 'new_v_w_in', 'new_v_fox_forget_b', 'new_v_ret_norm_g', 'new_v_w_out', 'new_v_ffn_norm_g', 'new_v_w_up', 'new_v_conv_w', 'new_v_conv_b', 'new_v_w_down', 'new_v_final_norm_g']
TWIN_LEAF_KINDS = {'loss': 'loss', 'grad_x': 'grad_x', 'grad_meta_tokens': 'grad_w', 'grad_attn_norm_g': 'grad_w', 'grad_w_in': 'grad_w', 'grad_fox_forget_b': 'grad_w', 'grad_ret_norm_g': 'grad_w', 'grad_w_out': 'grad_w', 'grad_ffn_norm_g': 'grad_w', 'grad_w_up': 'grad_w', 'grad_conv_w': 'grad_w', 'grad_conv_b': 'grad_w', 'grad_w_down': 'grad_w', 'grad_final_norm_g': 'grad_w', 'delta_meta_tokens': 'delta_w', 'delta_attn_norm_g': 'delta_w', 'delta_w_in': 'delta_w', 'delta_fox_forget_b': 'delta_w', 'delta_ret_norm_g': 'delta_w', 'delta_w_out': 'delta_w', 'delta_ffn_norm_g': 'delta_w', 'delta_w_up': 'delta_w', 'delta_conv_w': 'delta_w', 'delta_conv_b': 'delta_w', 'delta_w_down': 'delta_w', 'delta_final_norm_g': 'delta_w', 'new_m_meta_tokens': 'new_m', 'new_m_attn_norm_g': 'new_m', 'new_m_w_in': 'new_m', 'new_m_fox_forget_b': 'new_m', 'new_m_ret_norm_g': 'new_m', 'new_m_w_out': 'new_m', 'new_m_ffn_norm_g': 'new_m', 'new_m_w_up': 'new_m', 'new_m_conv_w': 'new_m', 'new_m_conv_b': 'new_m', 'new_m_w_down': 'new_m', 'new_m_final_norm_g': 'new_m', 'new_v_meta_tokens': 'new_v', 'new_v_attn_norm_g': 'new_v', 'new_v_w_in': 'new_v', 'new_v_fox_forget_b': 'new_v', 'new_v_ret_norm_g': 'new_v', 'new_v_w_out': 'new_v', 'new_v_ffn_norm_g': 'new_v', 'new_v_w_up': 'new_v', 'new_v_conv_w': 'new_v', 'new_v_conv_b': 'new_v', 'new_v_w_down': 'new_v', 'new_v_final_norm_g': 'new_v'}


def _forward(args):
    return _fwd_reference(*[args[k] for k in FWD_PARAMS])


def _output_shape():
    out = _jax.eval_shape(lambda: _forward(_fwd_setup_inputs(0)))
    return out.shape, out.dtype

N_MICROBATCH = 1
ADAM_LR = 0.001
ADAM_B1 = 0.9
ADAM_B2 = 0.999
ADAM_EPS = 1e-08
ADAM_WD = 0.01
ADAM_STEP = 10
PER_EXAMPLE_BATCH_AXIS = {'x': 0, 'loss_target': 0}
SHARED_INPUTS = []
_WEIGHT_DTYPES = {'meta_tokens': _jnp.float32, 'attn_norm_g': _jnp.float32, 'w_in': _jnp.float32, 'fox_forget_b': _jnp.float32, 'ret_norm_g': _jnp.float32, 'w_out': _jnp.float32, 'ffn_norm_g': _jnp.float32, 'w_up': _jnp.float32, 'conv_w': _jnp.float32, 'conv_b': _jnp.float32, 'w_down': _jnp.float32, 'final_norm_g': _jnp.float32}
MOMENT_SCALE = {'meta_tokens': 6.378942e-03, 'attn_norm_g': 1.694219e-01, 'w_in': 9.608770e-02, 'fox_forget_b': 2.275179e-01, 'ret_norm_g': 1.103079e-01, 'w_out': 8.784188e-02, 'ffn_norm_g': 1.258280e-01, 'w_up': 5.236853e-02, 'conv_w': 5.265906e-02, 'conv_b': 4.962219e-02, 'w_down': 8.558007e-02, 'final_norm_g': 3.200234e+01}


def _to_microbatches(a, axis):
    t = _jnp.moveaxis(a, axis, 0)
    t = t.reshape((N_MICROBATCH, t.shape[0] // N_MICROBATCH) + t.shape[1:])
    return _jnp.moveaxis(t, 1, axis + 1)


def setup_inputs(seed: int = 0) -> dict:
    inp = _fwd_setup_inputs(seed)
    key = _jax.random.fold_in(_jax.random.key(seed), 7919)
    shape, _ = _output_shape()
    out = dict(inp)
    out["loss_target"] = _jax.random.normal(_jax.random.fold_in(key, 0), shape, _jnp.float32)
    for i, name in enumerate(TWIN_WEIGHTS):
        w = inp[name].astype(_jnp.float32)
        if MOMENT_SCALE is None:
            s = _jnp.sqrt(_jnp.mean(_jnp.square(w)) + 1e-30)
        else:
            s = MOMENT_SCALE[name]
        km, kv = _jax.random.split(_jax.random.fold_in(key, i + 1))
        out[name] = w
        out["m_" + name] = s * _jax.random.normal(km, w.shape, _jnp.float32)
        out["v_" + name] = (s * s) * _jax.random.uniform(kv, w.shape, _jnp.float32, 0.5, 1.5)
    if N_MICROBATCH > 1:
        for name, axis in PER_EXAMPLE_BATCH_AXIS.items():
            out[name] = _to_microbatches(out[name], axis)
    return {'x': out['x'], 'meta_tokens': out['meta_tokens'], 'attn_norm_g': out['attn_norm_g'], 'w_in': out['w_in'], 'fox_forget_b': out['fox_forget_b'], 'ret_norm_g': out['ret_norm_g'], 'w_out': out['w_out'], 'ffn_norm_g': out['ffn_norm_g'], 'w_up': out['w_up'], 'conv_w': out['conv_w'], 'conv_b': out['conv_b'], 'w_down': out['w_down'], 'final_norm_g': out['final_norm_g'], 'loss_target': out['loss_target'], 'm_meta_tokens': out['m_meta_tokens'], 'm_attn_norm_g': out['m_attn_norm_g'], 'm_w_in': out['m_w_in'], 'm_fox_forget_b': out['m_fox_forget_b'], 'm_ret_norm_g': out['m_ret_norm_g'], 'm_w_out': out['m_w_out'], 'm_ffn_norm_g': out['m_ffn_norm_g'], 'm_w_up': out['m_w_up'], 'm_conv_w': out['m_conv_w'], 'm_conv_b': out['m_conv_b'], 'm_w_down': out['m_w_down'], 'm_final_norm_g': out['m_final_norm_g'], 'v_meta_tokens': out['v_meta_tokens'], 'v_attn_norm_g': out['v_attn_norm_g'], 'v_w_in': out['v_w_in'], 'v_fox_forget_b': out['v_fox_forget_b'], 'v_ret_norm_g': out['v_ret_norm_g'], 'v_w_out': out['v_w_out'], 'v_ffn_norm_g': out['v_ffn_norm_g'], 'v_w_up': out['v_w_up'], 'v_conv_w': out['v_conv_w'], 'v_conv_b': out['v_conv_b'], 'v_w_down': out['v_w_down'], 'v_final_norm_g': out['v_final_norm_g']}


def _loss(weights, diff, rest, loss_target):
    with _jax.named_scope("forward"):
        args = {**rest, TWIN_DIFF_INPUT: diff, **{k: w.astype(_WEIGHT_DTYPES[k]) for k, w in weights.items()}}
        y = _forward(args)
    with _jax.named_scope("loss_head"):
        err = _jnp.square(y.astype(_jnp.float32) - loss_target)
        return 0.5 * _jnp.sum(_jnp.mean(err, axis=-1)) if err.ndim else 0.5 * err


def _adamw(w, g, m, v):
    m = ADAM_B1 * m + (1.0 - ADAM_B1) * g
    v = ADAM_B2 * v + (1.0 - ADAM_B2) * _jnp.square(g)
    m_hat = m / (1.0 - ADAM_B1 ** ADAM_STEP)
    v_hat = v / (1.0 - ADAM_B2 ** ADAM_STEP)
    delta = -ADAM_LR * (m_hat / (_jnp.sqrt(v_hat) + ADAM_EPS) + ADAM_WD * w)
    return delta, m, v


def reference(x, meta_tokens, attn_norm_g, w_in, fox_forget_b, ret_norm_g, w_out, ffn_norm_g, w_up, conv_w, conv_b, w_down, final_norm_g, loss_target, m_meta_tokens, m_attn_norm_g, m_w_in, m_fox_forget_b, m_ret_norm_g, m_w_out, m_ffn_norm_g, m_w_up, m_conv_w, m_conv_b, m_w_down, m_final_norm_g, v_meta_tokens, v_attn_norm_g, v_w_in, v_fox_forget_b, v_ret_norm_g, v_w_out, v_ffn_norm_g, v_w_up, v_conv_w, v_conv_b, v_w_down, v_final_norm_g):
    given = dict(x=x, meta_tokens=meta_tokens, attn_norm_g=attn_norm_g, w_in=w_in, fox_forget_b=fox_forget_b, ret_norm_g=ret_norm_g, w_out=w_out, ffn_norm_g=ffn_norm_g, w_up=w_up, conv_w=conv_w, conv_b=conv_b, w_down=w_down, final_norm_g=final_norm_g, loss_target=loss_target, m_meta_tokens=m_meta_tokens, m_attn_norm_g=m_attn_norm_g, m_w_in=m_w_in, m_fox_forget_b=m_fox_forget_b, m_ret_norm_g=m_ret_norm_g, m_w_out=m_w_out, m_ffn_norm_g=m_ffn_norm_g, m_w_up=m_w_up, m_conv_w=m_conv_w, m_conv_b=m_conv_b, m_w_down=m_w_down, m_final_norm_g=m_final_norm_g, v_meta_tokens=v_meta_tokens, v_attn_norm_g=v_attn_norm_g, v_w_in=v_w_in, v_fox_forget_b=v_fox_forget_b, v_ret_norm_g=v_ret_norm_g, v_w_out=v_w_out, v_ffn_norm_g=v_ffn_norm_g, v_w_up=v_w_up, v_conv_w=v_conv_w, v_conv_b=v_conv_b, v_w_down=v_w_down, v_final_norm_g=v_final_norm_g)
    weights = {n: given[n] for n in TWIN_WEIGHTS}
    shared = {n: given[n] for n in SHARED_INPUTS}
    per_example = {n: given[n] for n in ['x']}
    grad_fn = _jax.value_and_grad(_loss, argnums=(0, 1))

    def one_microbatch(ex, loss_target):
        ex = dict(ex)
        diff = ex.pop(TWIN_DIFF_INPUT)
        return grad_fn(weights, diff, {**shared, **ex}, loss_target)

    if N_MICROBATCH == 1:
        loss, (grad_w, grad_x) = one_microbatch(per_example, given["loss_target"])
    else:
        def body(carry, xs):
            loss_sum, grad_sum = carry
            l_k, (gw_k, gx_k) = one_microbatch(xs[0], xs[1])
            with _jax.named_scope("update"):
                return (loss_sum + l_k, _jax.tree.map(_jnp.add, grad_sum, gw_k)), gx_k

        init = (_jnp.zeros((), _jnp.float32), _jax.tree.map(_jnp.zeros_like, weights))
        (loss, grad_w), grad_x = _jax.lax.scan(body, init, (per_example, given["loss_target"]))
    with _jax.named_scope("update"):
        delta_w, new_m, new_v = {}, {}, {}
        for n in TWIN_WEIGHTS:
            delta_w[n], new_m[n], new_v[n] = _adamw(weights[n], grad_w[n], given["m_" + n], given["v_" + n])
    return (loss, grad_x, *[grad_w[n] for n in TWIN_WEIGHTS], *[delta_w[n] for n in TWIN_WEIGHTS],
            *[new_m[n] for n in TWIN_WEIGHTS], *[new_v[n] for n in TWIN_WEIGHTS])
```

```python
import numpy as np
import jax
import jax.numpy as jnp
from jax import lax
from jax.experimental import pallas as pl
from jax.experimental.pallas import tpu as pltpu

F32 = jnp.float32
BF16 = jnp.bfloat16

D_MODEL = 1024
N_META = 16
N_PAD = 112
PREFIX = 128
RET_HEADS = 4
RET_DK = 64
RET_DV = 128
FOX_HEADS = 8
FOX_DH = 64
D_FF = 2816
ROPE_BASE = 10000.0
EPS = 1e-6
NEG = -1e30
RET_QK = RET_HEADS * RET_DK
RET_V = RET_HEADS * RET_DV
FOX_W = FOX_HEADS * FOX_DH
IN_WIDTH = 2 * RET_QK + 2 * RET_V + 3 * FOX_W + FOX_HEADS
IN_PAD = 3200
FF_COL_BLOCK = (IN_WIDTH - FOX_HEADS) // 128
QK_SCALE = 0.125

ADAM_LR = 0.001
ADAM_B1 = 0.9
ADAM_B2 = 0.999
ADAM_EPS = 1e-08
ADAM_WD = 0.01
ADAM_STEP = 10

N_DEV = 8
LANE = 128
ROW_TILE = 128
TOK_TILE = 384

NN = (((1,), (0,)), ((), ()))
NT = (((1,), (1,)), ((), ()))
TN = (((0,), (0,)), ((), ()))


def _pcall(body, **kw):
    return pl.pallas_call(body, **kw)


def _params(*sem):
    return pltpu.CompilerParams(dimension_semantics=sem)


def _dot(a, b, dims=NN):
    return lax.dot_general(a, b, dims, preferred_element_type=F32)


def _sigmoid(x):
    return 1.0 / (1.0 + jnp.exp(-x))


def _matmul(a, b, *, mode, grid, a_spec, b_spec, o_spec, out_shape, name, add=None, add_spec=None):
    dims = {"nn": NN, "nt": NT, "tn": TN}[mode]
    nk = grid[2]
    has_add = add is not None

    def body(*refs):
        if has_add:
            a_ref, b_ref, add_ref, o_ref = refs[:4]
        else:
            a_ref, b_ref, o_ref = refs[:3]
        part = _dot(a_ref[...].astype(BF16), b_ref[...].astype(BF16), dims)

        def finish(acc):
            if has_add:
                acc = acc + add_ref[...]
            o_ref[...] = acc.astype(o_ref.dtype)

        if nk == 1:
            finish(part)
        else:
            acc_ref = refs[-1]
            k = pl.program_id(2)

            @pl.when(k == 0)
            def _():
                acc_ref[...] = part

            @pl.when(k > 0)
            def _():
                acc_ref[...] += part

            @pl.when(k == nk - 1)
            def _():
                finish(acc_ref[...])

    in_specs = [a_spec, b_spec] + ([add_spec] if has_add else [])
    args = (a, b) + ((add,) if has_add else ())
    scratch = [] if nk == 1 else [pltpu.VMEM(tuple(d for d in o_spec.block_shape if d is not None), F32)]
    return _pcall(
        body, name=name, grid=grid, in_specs=in_specs, out_specs=o_spec, out_shape=out_shape,
        scratch_shapes=scratch, compiler_params=_params("parallel", "parallel", "arbitrary"),
    )(*args)


def _mm_simple(a, b, *, mode, tm, tn, tk, out_dtype, name, add=None):
    if mode == "tn":
        K, M = a.shape
    else:
        M, K = a.shape
    N = b.shape[0] if mode == "nt" else b.shape[1]
    grid = (M // tm, N // tn, K // tk)
    a_spec = pl.BlockSpec((tk, tm), lambda i, j, k: (k, i)) if mode == "tn" else pl.BlockSpec((tm, tk), lambda i, j, k: (i, k))
    b_spec = pl.BlockSpec((tn, tk), lambda i, j, k: (j, k)) if mode == "nt" else pl.BlockSpec((tk, tn), lambda i, j, k: (k, j))
    o_spec = pl.BlockSpec((tm, tn), lambda i, j, k: (i, j))
    return _matmul(a, b, mode=mode, grid=grid, a_spec=a_spec, b_spec=b_spec, o_spec=o_spec,
                   out_shape=jax.ShapeDtypeStruct((M, N), out_dtype), name=name, add=add,
                   add_spec=o_spec if add is not None else None)


def _prep_norm(x, meta, gain, name):
    seq, d = x.shape
    t = seq + PREFIX
    nb = t // ROW_TILE

    def body(x_ref, meta_ref, g_ref, h_ref, n_ref):
        i = pl.program_id(0)

        @pl.when(i == 0)
        def _():
            h_ref[0:N_PAD, :] = jnp.zeros((N_PAD, d), F32)
            h_ref[N_PAD:ROW_TILE, :] = meta_ref[...]

        @pl.when(i > 0)
        def _():
            h_ref[...] = x_ref[...]

        h = h_ref[...]
        r = lax.rsqrt(jnp.mean(h * h, axis=-1, keepdims=True) + EPS)
        n_ref[...] = (h * r * g_ref[...]).astype(BF16)

    return _pcall(
        body, name=name, grid=(nb,),
        in_specs=[pl.BlockSpec((ROW_TILE, d), lambda i: (jnp.maximum(i - 1, 0), 0)),
                  pl.BlockSpec((N_META, d), lambda i: (0, 0)),
                  pl.BlockSpec((1, d), lambda i: (0, 0))],
        out_specs=[pl.BlockSpec((ROW_TILE, d), lambda i: (i, 0)), pl.BlockSpec((ROW_TILE, d), lambda i: (i, 0))],
        out_shape=[jax.ShapeDtypeStruct((t, d), F32), jax.ShapeDtypeStruct((t, d), BF16)],
        compiler_params=_params("parallel"),
    )(x, meta, gain)


def _rmsnorm(h, gain, name):
    t, d = h.shape

    def body(h_ref, g_ref, n_ref):
        x = h_ref[...]
        r = lax.rsqrt(jnp.mean(x * x, axis=-1, keepdims=True) + EPS)
        n_ref[...] = (x * r * g_ref[...]).astype(BF16)

    return _pcall(
        body, name=name, grid=(t // TOK_TILE,),
        in_specs=[pl.BlockSpec((TOK_TILE, d), lambda i: (i, 0)), pl.BlockSpec((1, d), lambda i: (0, 0))],
        out_specs=pl.BlockSpec((TOK_TILE, d), lambda i: (i, 0)),
        out_shape=jax.ShapeDtypeStruct((t, d), BF16),
        compiler_params=_params("parallel"),
    )(h, gain)


def _rmsnorm_bwd(dn, h, gain, dres, name):
    t, d = h.shape

    def body(dn_ref, h_ref, g_ref, dres_ref, dh_ref, gg_ref):
        i = pl.program_id(0)
        x = h_ref[...]
        r = lax.rsqrt(jnp.mean(x * x, axis=-1, keepdims=True) + EPS)
        xhat = x * r
        dy = dn_ref[...]
        u = dy * g_ref[...]
        dh_ref[...] = dres_ref[...] + r * (u - xhat * jnp.mean(u * xhat, axis=-1, keepdims=True))
        part = jnp.sum(dy * xhat, axis=0, keepdims=True)

        @pl.when(i == 0)
        def _():
            gg_ref[...] = part

        @pl.when(i > 0)
        def _():
            gg_ref[...] += part

    return _pcall(
        body, name=name, grid=(t // TOK_TILE,),
        in_specs=[pl.BlockSpec((TOK_TILE, d), lambda i: (i, 0)), pl.BlockSpec((TOK_TILE, d), lambda i: (i, 0)),
                  pl.BlockSpec((1, d), lambda i: (0, 0)), pl.BlockSpec((TOK_TILE, d), lambda i: (i, 0))],
        out_specs=[pl.BlockSpec((TOK_TILE, d), lambda i: (i, 0)), pl.BlockSpec((1, d), lambda i: (0, 0))],
        out_shape=[jax.ShapeDtypeStruct((t, d), F32), jax.ShapeDtypeStruct((1, d), F32)],
        compiler_params=_params("arbitrary"),
    )(dn, h, gain, dres)


def _loss_bwd(h2, target, gain, name):
    t, d = h2.shape
    nb = t // ROW_TILE

    def body(h_ref, tgt_ref, g_ref, loss_ref, dh_ref, gg_ref):
        i = pl.program_id(0)

        @pl.when(i == 0)
        def _():
            loss_ref[...] = jnp.zeros_like(loss_ref)
            gg_ref[...] = jnp.zeros_like(gg_ref)
            dh_ref[...] = jnp.zeros_like(dh_ref)

        @pl.when(i > 0)
        def _():
            x = h_ref[...]
            r = lax.rsqrt(jnp.mean(x * x, axis=-1, keepdims=True) + EPS)
            xhat = x * r
            g = g_ref[...]
            err = xhat * g - tgt_ref[...]
            loss_ref[...] += 0.5 * jnp.sum(jnp.mean(err * err, axis=-1, keepdims=True))
            dy = err * (1.0 / d)
            u = dy * g
            dh_ref[...] = r * (u - xhat * jnp.mean(u * xhat, axis=-1, keepdims=True))
            gg_ref[...] += jnp.sum(dy * xhat, axis=0, keepdims=True)

    return _pcall(
        body, name=name, grid=(nb,),
        in_specs=[pl.BlockSpec((ROW_TILE, d), lambda i: (i, 0)),
                  pl.BlockSpec((ROW_TILE, d), lambda i: (jnp.maximum(i - 1, 0), 0)),
                  pl.BlockSpec((1, d), lambda i: (0, 0))],
        out_specs=[pl.BlockSpec((8, LANE), lambda i: (0, 0)), pl.BlockSpec((ROW_TILE, d), lambda i: (i, 0)),
                   pl.BlockSpec((1, d), lambda i: (0, 0))],
        out_shape=[jax.ShapeDtypeStruct((8, LANE), F32), jax.ShapeDtypeStruct((t, d), F32),
                   jax.ShapeDtypeStruct((1, d), F32)],
        compiler_params=_params("arbitrary"),
    )(h2, target, gain)


def _ret_consts(bk):
    gam = 1.0 - 2.0 ** (-5.0 - np.arange(RET_HEADS))
    n = np.arange(bk)
    same_or_earlier_chunk = (n[None, :] // 64) <= (n[:, None] // 64)
    w = gam[:, None, None] ** np.abs(n[:, None] - n[None, :])[None] * same_or_earlier_chunk[None]
    wq = gam[:, None] ** (n[None, :] + 1.0)
    wk = gam[:, None] ** (bk - 1.0 - n[None, :])
    mask = (np.arange(RET_QK)[None, :] // RET_DK) == np.arange(RET_HEADS)[:, None]
    return (jnp.asarray(w, F32), jnp.asarray(wq[:, :, None], F32), jnp.asarray(wk[:, :, None], F32),
            jnp.asarray(mask[:, None, :], F32), [float(g ** bk) for g in gam])


def _rope_tables(t):
    half = RET_DK // 2
    inv = 1.0 / (ROPE_BASE ** (jnp.arange(half, dtype=F32) / half))
    ang = jnp.arange(t).astype(F32)[:, None] * inv[None, :]
    cos, sin = jnp.cos(ang), jnp.sin(ang)
    return (jnp.tile(jnp.concatenate([cos, cos], axis=1), (1, RET_HEADS)),
            jnp.tile(jnp.concatenate([-sin, sin], axis=1), (1, RET_HEADS)))


def _swap_halves(x):
    outs = []
    for s in range(x.shape[1] // LANE):
        xs = x[:, LANE * s:LANE * (s + 1)]
        lane = lax.broadcasted_iota(jnp.int32, xs.shape, 1)
        outs.append(jnp.where((lane & 32) == 0, pltpu.roll(xs, LANE - 32, axis=1), pltpu.roll(xs, 32, axis=1)))
    return outs[0] if len(outs) == 1 else jnp.concatenate(outs, axis=1)


def _rope(x, cos, sin_signed):
    return x * cos + _swap_halves(x) * sin_signed


def _rope_t(dx, cos, sin_signed):
    return dx * cos + _swap_halves(dx * sin_signed)


def _ret_fwd(proj, cos, sin, gain, name):
    t = proj.shape[0]
    bk = TOK_TILE
    nb = t // bk
    w, wq, wk, mask, g_blk = _ret_consts(bk)

    def body(q_ref, k_ref, v_ref, rg_ref, cos_ref, sin_ref, w_ref, wq_ref, wk_ref, mask_ref, gain_ref,
             opre_ref, og_ref, st_ref, r_ref):
        i = pl.program_id(0)

        @pl.when(i == 0)
        def _():
            r_ref[...] = jnp.zeros_like(r_ref)

        c, s = cos_ref[...], sin_ref[...]
        valid = ((i * bk + lax.broadcasted_iota(jnp.int32, (bk, 1), 0)) >= N_PAD).astype(F32)
        qr = _rope(q_ref[...], c, s)
        kr = _rope(k_ref[...], c, s) * QK_SCALE * valid
        kb = kr.astype(BF16)
        for h in range(RET_HEADS):
            hm = mask_ref[h]
            cols = slice(RET_DV * h, RET_DV * (h + 1))
            vh = v_ref[:, cols].astype(BF16)
            r_prev = r_ref[h]
            st_ref[0, h] = r_prev
            sm = _dot((qr * hm).astype(BF16), kb, NT) * w_ref[h]
            o = _dot(sm.astype(BF16), vh) + _dot((qr * (hm * wq_ref[h])).astype(BF16), r_prev.astype(BF16))
            r_ref[h] = g_blk[h] * r_prev + _dot((kr * wk_ref[h]).astype(BF16), vh, TN)
            opre_ref[:, cols] = o
            rstd = lax.rsqrt(jnp.mean(o * o, axis=-1, keepdims=True) + EPS)
            rg = rg_ref[:, cols]
            og_ref[:, cols] = (o * rstd * gain_ref[:, cols] * (rg * _sigmoid(rg))).astype(BF16)

    full = lambda shape: pl.BlockSpec(shape, lambda i: (0,) * len(shape))
    return _pcall(
        body, name=name, grid=(nb,),
        in_specs=[pl.BlockSpec((bk, RET_QK), lambda i: (i, 0)), pl.BlockSpec((bk, RET_QK), lambda i: (i, 1)),
                  pl.BlockSpec((bk, RET_V), lambda i: (i, 1)), pl.BlockSpec((bk, RET_V), lambda i: (i, 2)),
                  pl.BlockSpec((bk, RET_QK), lambda i: (i, 0)), pl.BlockSpec((bk, RET_QK), lambda i: (i, 0)),
                  full((RET_HEADS, bk, bk)), full((RET_HEADS, bk, 1)), full((RET_HEADS, bk, 1)),
                  full((RET_HEADS, 1, RET_QK)), full((1, RET_V))],
        out_specs=[pl.BlockSpec((bk, RET_V), lambda i: (i, 0)), pl.BlockSpec((bk, RET_V), lambda i: (i, 0)),
                   pl.BlockSpec((1, RET_HEADS, RET_QK, RET_DV), lambda i: (i, 0, 0, 0))],
        out_shape=[jax.ShapeDtypeStruct((t, RET_V), F32), jax.ShapeDtypeStruct((t, RET_V), BF16),
                   jax.ShapeDtypeStruct((nb, RET_HEADS, RET_QK, RET_DV), F32)],
        scratch_shapes=[pltpu.VMEM((RET_HEADS, RET_QK, RET_DV), F32)],
        compiler_params=_params("arbitrary"),
    )(proj, proj, proj, proj, cos, sin, w, wq, wk, mask, gain)


def _ret_bwd(proj, cos, sin, gain, dmixed, opre, states, name):
    t = proj.shape[0]
    bk = TOK_TILE
    nb = t // bk
    w, wq, wk, mask, g_blk = _ret_consts(bk)

    def body(q_ref, k_ref, v_ref, rg_ref, cos_ref, sin_ref, w_ref, wq_ref, wk_ref, mask_ref, gain_ref,
             dog_ref, opre_ref, st_ref, dq_ref, dk_ref, dv_ref, drg_ref, gg_ref, dr_ref):
        step = pl.program_id(0)
        i = nb - 1 - step

        @pl.when(step == 0)
        def _():
            dr_ref[...] = jnp.zeros_like(dr_ref)
            gg_ref[...] = jnp.zeros_like(gg_ref)

        c, s = cos_ref[...], sin_ref[...]
        valid = ((i * bk + lax.broadcasted_iota(jnp.int32, (bk, 1), 0)) >= N_PAD).astype(F32)
        qr = _rope(q_ref[...], c, s)
        kr = _rope(k_ref[...], c, s) * QK_SCALE * valid
        kb = kr.astype(BF16)
        dqr = jnp.zeros((bk, RET_QK), F32)
        dkr = jnp.zeros((bk, RET_QK), F32)
        for h in range(RET_HEADS):
            hm = mask_ref[h]
            cols = slice(RET_DV * h, RET_DV * (h + 1))
            vh = v_ref[:, cols].astype(BF16)
            o = opre_ref[:, cols]
            rstd = lax.rsqrt(jnp.mean(o * o, axis=-1, keepdims=True) + EPS)
            xhat = o * rstd
            rg = rg_ref[:, cols]
            sg = _sigmoid(rg)
            gate = rg * sg
            gn = gain_ref[:, cols]
            dog = dog_ref[:, cols]
            drg_ref[:, cols] = (dog * xhat * gn * (sg * (1.0 + rg * (1.0 - sg)))).astype(BF16)
            gg_ref[:, cols] += jnp.sum(dog * xhat * gate, axis=0, keepdims=True)
            dxh = dog * gn * gate
            do = (rstd * (dxh - xhat * jnp.mean(dxh * xhat, axis=-1, keepdims=True))).astype(BF16)
            qm = (qr * hm).astype(BF16)
            qw = (qr * (hm * wq_ref[h])).astype(BF16)
            kw = (kr * wk_ref[h]).astype(BF16)
            wh = w_ref[h]
            sm = (_dot(qm, kb, NT) * wh).astype(BF16)
            ds = (_dot(do, vh, NT) * wh).astype(BF16)
            dr = dr_ref[h]
            drb = dr.astype(BF16)
            dv_ref[:, cols] = (_dot(sm, do, TN) + _dot(kw, drb)).astype(BF16)
            dqr = dqr + _dot(ds, kb) * hm + _dot(do, st_ref[0, h].astype(BF16), NT) * (hm * wq_ref[h])
            dkr = dkr + _dot(ds, qm, TN) + _dot(vh, drb, NT) * wk_ref[h]
            dr_ref[h] = g_blk[h] * dr + _dot(qw, do, TN)
        dq_ref[...] = _rope_t(dqr, c, s).astype(BF16)
        dk_ref[...] = _rope_t(dkr * (QK_SCALE * valid), c, s).astype(BF16)

    full = lambda shape: pl.BlockSpec(shape, lambda i: (0,) * len(shape))
    rev = lambda col: (lambda i: (nb - 1 - i, col))
    return _pcall(
        body, name=name, grid=(nb,),
        in_specs=[pl.BlockSpec((bk, RET_QK), rev(0)), pl.BlockSpec((bk, RET_QK), rev(1)),
                  pl.BlockSpec((bk, RET_V), rev(1)), pl.BlockSpec((bk, RET_V), rev(2)),
                  pl.BlockSpec((bk, RET_QK), rev(0)), pl.BlockSpec((bk, RET_QK), rev(0)),
                  full((RET_HEADS, bk, bk)), full((RET_HEADS, bk, 1)), full((RET_HEADS, bk, 1)),
                  full((RET_HEADS, 1, RET_QK)), full((1, RET_V)),
                  pl.BlockSpec((bk, RET_V), rev(0)), pl.BlockSpec((bk, RET_V), rev(0)),
                  pl.BlockSpec((1, RET_HEADS, RET_QK, RET_DV), lambda i: (nb - 1 - i, 0, 0, 0))],
        out_specs=[pl.BlockSpec((bk, RET_QK), rev(0)), pl.BlockSpec((bk, RET_QK), rev(0)),
                   pl.BlockSpec((bk, RET_V), rev(0)), pl.BlockSpec((bk, RET_V), rev(0)),
                   pl.BlockSpec((1, RET_V), lambda i: (0, 0))],
        out_shape=[jax.ShapeDtypeStruct((t, RET_QK), BF16), jax.ShapeDtypeStruct((t, RET_QK), BF16),
                   jax.ShapeDtypeStruct((t, RET_V), BF16), jax.ShapeDtypeStruct((t, RET_V), BF16),
                   jax.ShapeDtypeStruct((1, RET_V), F32)],
        scratch_shapes=[pltpu.VMEM((RET_HEADS, RET_QK, RET_DV), F32)],
        compiler_params=_params("arbitrary"),
    )(proj, proj, proj, proj, cos, sin, w, wq, wk, mask, gain, dmixed, opre, states)


def _forget_cumsum(proj, bias, name):
    t = proj.shape[0]
    nb = t // ROW_TILE
    tril = jnp.asarray(np.tril(np.ones((ROW_TILE, ROW_TILE))), F32)

    def body(z_ref, b_ref, tril_ref, c_ref, carry_ref):
        i = pl.program_id(0)

        @pl.when(i == 0)
        def _():
            carry_ref[...] = jnp.zeros_like(carry_ref)

        z = z_ref[...] + b_ref[...]
        logf = jnp.minimum(z, 0.0) - jnp.log(1.0 + jnp.exp(-jnp.abs(z)))
        c = lax.dot_general(tril_ref[...], logf, NN, precision=lax.Precision.HIGHEST,
                            preferred_element_type=F32) + carry_ref[...]
        c_ref[...] = c
        carry_ref[...] = c[ROW_TILE - 1:ROW_TILE, :]

    return _pcall(
        body, name=name, grid=(nb,),
        in_specs=[pl.BlockSpec((ROW_TILE, LANE), lambda i: (i, FF_COL_BLOCK)), pl.BlockSpec((1, LANE), lambda i: (0, 0)),
                  pl.BlockSpec((ROW_TILE, ROW_TILE), lambda i: (0, 0))],
        out_specs=pl.BlockSpec((ROW_TILE, LANE), lambda i: (i, 0)),
        out_shape=jax.ShapeDtypeStruct((t, LANE), F32),
        scratch_shapes=[pltpu.VMEM((1, LANE), F32)],
        compiler_params=_params("arbitrary"),
    )(proj, bias, tril)


def _forget_cumsum_bwd(proj, bias, drs, dcs, name):
    t = proj.shape[0]
    nb = t // ROW_TILE
    triu = jnp.asarray(np.triu(np.ones((ROW_TILE, ROW_TILE))), F32)

    def body(z_ref, b_ref, triu_ref, drs_ref, dcs_ref, dz_ref, gb_ref, carry_ref):
        step = pl.program_id(0)

        @pl.when(step == 0)
        def _():
            carry_ref[...] = jnp.zeros_like(carry_ref)
            gb_ref[...] = jnp.zeros_like(gb_ref)

        dlogf = lax.dot_general(triu_ref[...], drs_ref[...] - dcs_ref[...], NN, precision=lax.Precision.HIGHEST,
                                preferred_element_type=F32) + carry_ref[...]
        carry_ref[...] = dlogf[0:1, :]
        z = z_ref[...] + b_ref[...]
        dz = dlogf / (1.0 + jnp.exp(z))
        dz_ref[...] = dz.astype(BF16)
        gb_ref[...] += jnp.sum(dz, axis=0, keepdims=True)

    return _pcall(
        body, name=name, grid=(nb,),
        in_specs=[pl.BlockSpec((ROW_TILE, LANE), lambda i: (nb - 1 - i, FF_COL_BLOCK)),
                  pl.BlockSpec((1, LANE), lambda i: (0, 0)),
                  pl.BlockSpec((ROW_TILE, ROW_TILE), lambda i: (0, 0)),
                  pl.BlockSpec((ROW_TILE, LANE), lambda i: (nb - 1 - i, 0)),
                  pl.BlockSpec((ROW_TILE, LANE), lambda i: (nb - 1 - i, 0))],
        out_specs=[pl.BlockSpec((ROW_TILE, LANE), lambda i: (nb - 1 - i, 0)), pl.BlockSpec((1, LANE), lambda i: (0, 0))],
        out_shape=[jax.ShapeDtypeStruct((t, LANE), BF16), jax.ShapeDtypeStruct((1, LANE), F32)],
        scratch_shapes=[pltpu.VMEM((1, LANE), F32)],
        compiler_params=_params("arbitrary"),
    )(proj, bias, triu, drs, dcs)


def _fox_mask(i, j, tq):
    row = i * tq + lax.broadcasted_iota(jnp.int32, (tq, tq), 0)
    col = j * tq + lax.broadcasted_iota(jnp.int32, (tq, tq), 1)
    return (col <= row) & (col >= N_PAD)


def _fox_fwd(q, k, v, ct, cs, name):
    nh, nq, tq, dh = q.shape

    def body(q_ref, k_ref, v_ref, ct_ref, cs_ref, o_ref, lse_ref):
        i = pl.program_id(1)
        qb = q_ref[...]
        ct_i = ct_ref[...]

        def step(j, carry):
            m, l, acc = carry
            s = _dot(qb, k_ref[j], NT) * QK_SCALE + (ct_i - cs_ref[j])
            s = jnp.where(_fox_mask(i, j, tq), s, NEG)
            m_new = jnp.maximum(m, jnp.max(s, axis=-1, keepdims=True))
            alpha = jnp.exp(m - m_new)
            p = jnp.exp(s - m_new)
            l = alpha * l + jnp.sum(p, axis=-1, keepdims=True)
            acc = alpha * acc + _dot(p.astype(BF16), v_ref[j])
            return m_new, l, acc

        init = (jnp.full((tq, 1), NEG, F32), jnp.zeros((tq, 1), F32), jnp.zeros((tq, dh), F32))
        m, l, acc = lax.fori_loop(0, i + 1, step, init)
        o_ref[...] = acc / l
        lse_ref[...] = m + jnp.log(l)

    blk = lambda last2: pl.BlockSpec((None, None) + last2, lambda h, i: (h, i, 0, 0))
    whole = lambda last2: pl.BlockSpec((None, nq) + last2, lambda h, i: (h, 0, 0, 0))
    return _pcall(
        body, name=name, grid=(nh, nq),
        in_specs=[blk((tq, dh)), whole((tq, dh)), whole((tq, dh)), blk((tq, 1)), whole((1, tq))],
        out_specs=[blk((tq, dh)), blk((tq, 1))],
        out_shape=[jax.ShapeDtypeStruct((nh, nq, tq, dh), F32), jax.ShapeDtypeStruct((nh, nq, tq, 1), F32)],
        compiler_params=_params("parallel", "parallel"),
    )(q, k, v, ct, cs)


def _fox_bwd(q, k, v, ct, cs, o, lse, do, name):
    nh, nq, tq, dh = q.shape

    def body(q_ref, do_ref, o_ref, lse_ref, ct_ref, k_ref, v_ref, cs_ref, dq_ref, drs_ref, dk_ref, dv_ref, dcs_ref):
        j = pl.program_id(1)

        @pl.when(j == 0)
        def _():
            dq_ref[...] = jnp.zeros_like(dq_ref)
            drs_ref[...] = jnp.zeros_like(drs_ref)

        kb, vb, cs_j = k_ref[...], v_ref[...], cs_ref[...]

        def step(i, carry):
            dk, dv, dcs = carry
            qb = q_ref[i]
            do_i = do_ref[i]
            dob = do_i.astype(BF16)
            delta = jnp.sum(do_i * o_ref[i], axis=-1, keepdims=True)
            s = _dot(qb, kb, NT) * QK_SCALE + (ct_ref[i] - cs_j)
            p = jnp.where(_fox_mask(i, j, tq), jnp.exp(s - lse_ref[i]), 0.0)
            ds = p * (_dot(dob, vb, NT) - delta)
            dsb = ds.astype(BF16)
            dv = dv + _dot(p.astype(BF16), dob, TN)
            dk = dk + _dot(dsb, qb, TN) * QK_SCALE
            dq_ref[i] += _dot(dsb, kb) * QK_SCALE
            drs_ref[i] += jnp.sum(ds, axis=1, keepdims=True)
            dcs = dcs + jnp.sum(ds, axis=0, keepdims=True)
            return dk, dv, dcs

        init = (jnp.zeros((tq, dh), F32), jnp.zeros((tq, dh), F32), jnp.zeros((1, tq), F32))
        dk, dv, dcs = lax.fori_loop(j, nq, step, init)
        dk_ref[...] = dk
        dv_ref[...] = dv
        dcs_ref[...] = dcs

    blk = lambda last2: pl.BlockSpec((None, None) + last2, lambda h, j: (h, j, 0, 0))
    whole = lambda last2: pl.BlockSpec((None, nq) + last2, lambda h, j: (h, 0, 0, 0))
    return _pcall(
        body, name=name, grid=(nh, nq),
        in_specs=[whole((tq, dh)), whole((tq, dh)), whole((tq, dh)), whole((tq, 1)), whole((tq, 1)),
                  blk((tq, dh)), blk((tq, dh)), blk((1, tq))],
        out_specs=[whole((tq, dh)), whole((tq, 1)), blk((tq, dh)), blk((tq, dh)), blk((1, tq))],
        out_shape=[jax.ShapeDtypeStruct((nh, nq, tq, dh), F32), jax.ShapeDtypeStruct((nh, nq, tq, 1), F32),
                   jax.ShapeDtypeStruct((nh, nq, tq, dh), F32),
                   jax.ShapeDtypeStruct((nh, nq, tq, dh), F32), jax.ShapeDtypeStruct((nh, nq, 1, tq), F32)],
        compiler_params=_params("parallel", "arbitrary"),
    )(q, do, o, lse, ct, k, v, cs)


HALO = 8


def _rows_ext(ref, r0, rows, t, before, after):
    lo, hi = r0 - before, r0 + rows + after
    parts = []
    if lo < 0:
        parts.append(jnp.zeros((-lo, LANE), F32))
    parts.append(ref[max(lo, 0):min(hi, t), :].astype(F32))
    if hi > t:
        parts.append(jnp.zeros((hi - t, LANE), F32))
    return parts[0] if len(parts) == 1 else jnp.concatenate(parts, axis=0)


def _conv_taps(a_ext, r0_ext, cw_ref, cb_ref):
    n = a_ext.shape[0]
    if r0_ext < N_PAD:
        row = r0_ext + lax.broadcasted_iota(jnp.int32, (n, 1), 0)
        a_ext = jnp.where(row >= N_PAD, a_ext, 0.0)
    a1 = pltpu.roll(a_ext, 1, axis=0)
    a2 = pltpu.roll(a_ext, 2, axis=0)
    acc = cb_ref[...] + a2 * cw_ref[0:1, :] + a1 * cw_ref[1:2, :] + a_ext * cw_ref[2:3, :]
    return a_ext, a1, a2, acc


def _conv_gate_fwd(up, conv_w8, conv_b, name):
    _, t, f = up.shape
    rows = TOK_TILE

    def body(a_ref, b_ref, cw_ref, cb_ref, g_ref):
        for r0 in range(0, t, rows):
            a_ext = _rows_ext(a_ref, r0, rows, t, HALO, 0)
            _, _, _, acc = _conv_taps(a_ext, r0 - HALO, cw_ref, cb_ref)
            acc = acc[HALO:, :]
            g_ref[r0:r0 + rows, :] = (acc * _sigmoid(acc) * b_ref[r0:r0 + rows, :]).astype(BF16)

    return _pcall(
        body, name=name, grid=(f // LANE,),
        in_specs=[pl.BlockSpec((None, t, LANE), lambda j: (0, 0, j)), pl.BlockSpec((None, t, LANE), lambda j: (1, 0, j)),
                  pl.BlockSpec((8, LANE), lambda j: (0, j)), pl.BlockSpec((1, LANE), lambda j: (0, j))],
        out_specs=pl.BlockSpec((t, LANE), lambda j: (0, j)),
        out_shape=jax.ShapeDtypeStruct((t, f), BF16),
        compiler_params=_params("parallel"),
    )(up, up, conv_w8, conv_b)


def _conv_gate_bwd(up, conv_w8, conv_b, dg, name):
    _, t, f = up.shape
    rows = TOK_TILE

    def body(a_ref, b_ref, cw_ref, cb_ref, dg_ref, dup_ref, gcw_ref, gcb_ref):
        gw = [jnp.zeros((1, LANE), F32) for _ in range(3)]
        gb = jnp.zeros((1, LANE), F32)
        for r0 in range(0, t, rows):
            a_ext = _rows_ext(a_ref, r0, rows, t, HALO, HALO)
            b_ext = _rows_ext(b_ref, r0, rows, t, HALO, HALO)
            dg_ext = _rows_ext(dg_ref, r0, rows, t, HALO, HALO)
            a0, a1, a2, acc = _conv_taps(a_ext, r0 - HALO, cw_ref, cb_ref)
            sg = _sigmoid(acc)
            dacc = dg_ext * b_ext * (sg * (1.0 + acc * (1.0 - sg)))
            n = dacc.shape[0]
            da = (dacc * cw_ref[2:3, :] + pltpu.roll(dacc, n - 1, axis=0) * cw_ref[1:2, :]
                  + pltpu.roll(dacc, n - 2, axis=0) * cw_ref[0:1, :])
            core = slice(HALO, HALO + rows)
            da = da[core, :]
            if r0 < N_PAD:
                row = r0 + lax.broadcasted_iota(jnp.int32, (rows, 1), 0)
                da = jnp.where(row >= N_PAD, da, 0.0)
            dup_ref[0, r0:r0 + rows, :] = da.astype(BF16)
            dup_ref[1, r0:r0 + rows, :] = (dg_ext * acc * sg)[core, :].astype(BF16)
            dacc_c = dacc[core, :]
            gw[0] = gw[0] + jnp.sum(dacc_c * a2[core, :], axis=0, keepdims=True)
            gw[1] = gw[1] + jnp.sum(dacc_c * a1[core, :], axis=0, keepdims=True)
            gw[2] = gw[2] + jnp.sum(dacc_c * a0[core, :], axis=0, keepdims=True)
            gb = gb + jnp.sum(dacc_c, axis=0, keepdims=True)
        gcw_ref[...] = jnp.zeros((8, LANE), F32)
        for tap in range(3):
            gcw_ref[tap:tap + 1, :] = gw[tap]
        gcb_ref[...] = gb

    return _pcall(
        body, name=name, grid=(f // LANE,),
        in_specs=[pl.BlockSpec((None, t, LANE), lambda j: (0, 0, j)), pl.BlockSpec((None, t, LANE), lambda j: (1, 0, j)),
                  pl.BlockSpec((8, LANE), lambda j: (0, j)), pl.BlockSpec((1, LANE), lambda j: (0, j)),
                  pl.BlockSpec((t, LANE), lambda j: (0, j))],
        out_specs=[pl.BlockSpec((2, t, LANE), lambda j: (0, 0, j)), pl.BlockSpec((8, LANE), lambda j: (0, j)),
                   pl.BlockSpec((1, LANE), lambda j: (0, j))],
        out_shape=[jax.ShapeDtypeStruct((2, t, f), BF16), jax.ShapeDtypeStruct((8, f), F32),
                   jax.ShapeDtypeStruct((1, f), F32)],
        compiler_params=_params("parallel"),
    )(up, up, conv_w8, conv_b, dg)


def _exchange(arrays, kinds, name):
    n = len(arrays)
    npeer = N_DEV - 1

    def body(*refs):
        ins, outs = refs[:n], refs[n:2 * n]
        send_sems, recv_sems, local_sems = refs[2 * n:]
        x, y, c = lax.axis_index("x"), lax.axis_index("y"), lax.axis_index("c")
        me = 4 * x + 2 * y + c
        copies, locals_ = [], []
        for a in range(n):
            gather = kinds[a] == "gather"
            own = pltpu.make_async_copy(ins[a] if gather else ins[a].at[me], outs[a].at[me], local_sems.at[a])
            own.start()
            locals_.append(own)
            for d in range(1, N_DEV):
                px = 1 - x if d & 4 else x
                py = 1 - y if d & 2 else y
                pc = 1 - c if d & 1 else c
                src = ins[a] if gather else ins[a].at[4 * px + 2 * py + pc]
                cp = pltpu.make_async_remote_copy(
                    src_ref=src, dst_ref=outs[a].at[me],
                    send_sem=send_sems.at[a * npeer + d - 1], recv_sem=recv_sems.at[a * npeer + d - 1],
                    device_id=(px, py, pc), device_id_type=pl.DeviceIdType.MESH)
                cp.start()
                copies.append(cp)
        for cp in copies:
            cp.wait_recv()
        for cp in copies:
            cp.wait_send()
        for own in locals_:
            own.wait()

    out_shape = [jax.ShapeDtypeStruct((N_DEV,) + (a.shape if k == "gather" else a.shape[1:]), a.dtype)
                 for a, k in zip(arrays, kinds)]
    return _pcall(
        body, name=name,
        in_specs=[pl.BlockSpec(memory_space=pl.ANY)] * n,
        out_specs=[pl.BlockSpec(memory_space=pl.ANY)] * n,
        out_shape=out_shape,
        scratch_shapes=[pltpu.SemaphoreType.DMA((n * npeer,)), pltpu.SemaphoreType.DMA((n * npeer,)),
                        pltpu.SemaphoreType.DMA((n,))],
        compiler_params=pltpu.CompilerParams(has_side_effects=True),
    )(*arrays)


def _sum_slots(slots, name, rows_tile):
    nd, r, c = slots.shape

    def body(s_ref, o_ref):
        acc = s_ref[0].astype(F32)
        for p in range(1, nd):
            acc = acc + s_ref[p].astype(F32)
        o_ref[...] = acc

    return _pcall(
        body, name=name, grid=(r // rows_tile,),
        in_specs=[pl.BlockSpec((nd, rows_tile, c), lambda i: (0, i, 0))],
        out_specs=pl.BlockSpec((rows_tile, c), lambda i: (i, 0)),
        out_shape=jax.ShapeDtypeStruct((r, c), F32),
        compiler_params=_params("parallel"),
    )(slots)


def _adamw(w, g, m, v, name, rows_tile):
    r, c = w.shape

    def body(w_ref, g_ref, m_ref, v_ref, d_ref, nm_ref, nv_ref):
        gr = g_ref[...]
        nm = ADAM_B1 * m_ref[...] + (1.0 - ADAM_B1) * gr
        nv = ADAM_B2 * v_ref[...] + (1.0 - ADAM_B2) * (gr * gr)
        m_hat = nm / (1.0 - ADAM_B1 ** ADAM_STEP)
        v_hat = nv / (1.0 - ADAM_B2 ** ADAM_STEP)
        d_ref[...] = -ADAM_LR * (m_hat / (jnp.sqrt(v_hat) + ADAM_EPS) + ADAM_WD * w_ref[...])
        nm_ref[...] = nm
        nv_ref[...] = nv

    spec = pl.BlockSpec((rows_tile, c), lambda i: (i, 0))
    shp = jax.ShapeDtypeStruct((r, c), F32)
    return _pcall(
        body, name=name, grid=(r // rows_tile,), in_specs=[spec] * 4, out_specs=[spec] * 3, out_shape=[shp] * 3,
        compiler_params=_params("parallel"),
    )(w, g, m, v)


def _heads_major(x2d, nq, tq):
    t = x2d.shape[0]
    return x2d.reshape(t, FOX_HEADS, FOX_DH).transpose(1, 0, 2).reshape(FOX_HEADS, nq, tq, FOX_DH)


def _tokens_major(x4d):
    nh, nq, tq, dh = x4d.shape
    return x4d.reshape(nh, nq * tq, dh).transpose(1, 0, 2).reshape(nq * tq, nh * dh)


def _local_step(x, target, meta, attn_g, w_in_t, fox_b, ret_g, w_out, ffn_g, w_up_t, conv_w8, conv_b, w_down, final_g):
    seq, d = x.shape
    t = seq + PREFIX
    tm = TOK_TILE
    nq = t // tm
    fox_b128 = jnp.pad(fox_b, ((0, 0), (0, LANE - FOX_HEADS)))

    h0, n1 = _prep_norm(x, meta, attn_g, "prep_norm")
    proj = _mm_simple(n1, w_in_t, mode="nt", tm=tm, tn=640, tk=d, out_dtype=F32, name="mm_in")
    cos, sin = _rope_tables(t)
    o_pre, o_ret, states = _ret_fwd(proj, cos, sin, ret_g, "ret_fwd")
    c = _forget_cumsum(proj, fox_b128, "forget_cumsum")
    c_heads = c[:, :FOX_HEADS].T
    ct = c_heads.reshape(FOX_HEADS, nq, tm, 1)
    cs = c_heads.reshape(FOX_HEADS, nq, 1, tm)
    f0 = 2 * RET_QK + 2 * RET_V
    fq = _heads_major(proj[:, f0:f0 + FOX_W].astype(BF16), nq, tm)
    fk = _heads_major(proj[:, f0 + FOX_W:f0 + 2 * FOX_W].astype(BF16), nq, tm)
    fv = _heads_major(proj[:, f0 + 2 * FOX_W:f0 + 3 * FOX_W].astype(BF16), nq, tm)
    o_fox, lse = _fox_fwd(fq, fk, fv, ct, cs, "fox_fwd")
    mixed = jnp.concatenate([o_ret, _tokens_major(o_fox).astype(BF16)], axis=1)
    h1 = _mm_simple(mixed, w_out, mode="nn", tm=tm, tn=d, tk=d, out_dtype=F32, name="mm_out", add=h0)
    n2 = _rmsnorm(h1, ffn_g, "ffn_norm")
    nf = D_FF // 1408
    up = _matmul(
        n2, w_up_t, mode="nt", grid=(nq, 2 * nf, 1),
        a_spec=pl.BlockSpec((tm, d), lambda i, j, k: (i, 0)),
        b_spec=pl.BlockSpec((None, 1408, d), lambda i, j, k: (j // nf, j % nf, 0)),
        o_spec=pl.BlockSpec((None, tm, 1408), lambda i, j, k: (j // nf, i, j % nf)),
        out_shape=jax.ShapeDtypeStruct((2, t, D_FF), F32), name="mm_up")
    g = _conv_gate_fwd(up, conv_w8, conv_b, "conv_gate_fwd")
    h2 = _mm_simple(g, w_down, mode="nn", tm=tm, tn=d, tk=D_FF, out_dtype=F32, name="mm_down", add=h1)

    loss_tile, dh2, g_final = _loss_bwd(h2, target, final_g, "loss_bwd")
    dg = _mm_simple(dh2, w_down, mode="nt", tm=tm, tn=1408, tk=d, out_dtype=F32, name="mm_dg")
    gw_down = _mm_simple(g, dh2, mode="tn", tm=1408, tn=d, tk=tm, out_dtype=BF16, name="mm_gw_down")
    dup, g_conv_w8, g_conv_b = _conv_gate_bwd(up, conv_w8, conv_b, dg, "conv_gate_bwd")
    dn2 = _matmul(
        dup, w_up_t, mode="nn", grid=(nq, 1, 2 * nf),
        a_spec=pl.BlockSpec((None, tm, 1408), lambda i, j, k: (k // nf, i, k % nf)),
        b_spec=pl.BlockSpec((None, 1408, d), lambda i, j, k: (k // nf, k % nf, 0)),
        o_spec=pl.BlockSpec((tm, d), lambda i, j, k: (i, 0)),
        out_shape=jax.ShapeDtypeStruct((t, d), F32), name="mm_dn2")
    gw_up_t = _matmul(
        dup, n2, mode="tn", grid=(2 * nf, 1, nq),
        a_spec=pl.BlockSpec((None, tm, 1408), lambda i, j, k: (i // nf, k, i % nf)),
        b_spec=pl.BlockSpec((tm, d), lambda i, j, k: (k, 0)),
        o_spec=pl.BlockSpec((1408, d), lambda i, j, k: (i, 0)),
        out_shape=jax.ShapeDtypeStruct((2 * D_FF, d), BF16), name="mm_gw_up")
    dh1, g_ffn = _rmsnorm_bwd(dn2, h1, ffn_g, dh2, "ffn_norm_bwd")

    dmixed = _mm_simple(dh1, w_out, mode="nt", tm=tm, tn=d, tk=d, out_dtype=F32, name="mm_dmixed")
    gw_out = _mm_simple(mixed, dh1, mode="tn", tm=512, tn=d, tk=tm, out_dtype=BF16, name="mm_gw_out")
    dq, dk, dv, drg, g_ret = _ret_bwd(proj, cos, sin, ret_g, dmixed, o_pre, states, "ret_bwd")
    do_fox = _heads_major(dmixed[:, RET_V:], nq, tm)
    dfq, drs, dfk, dfv, dcs = _fox_bwd(fq, fk, fv, ct, cs, o_fox, lse, do_fox, "fox_bwd")
    lanes128 = lambda a: jnp.pad(a.reshape(FOX_HEADS, t).T, ((0, 0), (0, LANE - FOX_HEADS)))
    dff, g_fox_b = _forget_cumsum_bwd(proj, fox_b128, lanes128(drs), lanes128(dcs), "forget_cumsum_bwd")
    head_cols = lax.broadcasted_iota(jnp.int32, (1, LANE), 1) < FOX_HEADS
    dproj = jnp.concatenate(
        [dq, dk, dv, drg, _tokens_major(dfq).astype(BF16), _tokens_major(dfk).astype(BF16),
         _tokens_major(dfv).astype(BF16), jnp.where(head_cols, dff, jnp.zeros_like(dff))], axis=1)
    dn1 = _mm_simple(dproj, w_in_t, mode="nn", tm=tm, tn=d, tk=640, out_dtype=F32, name="mm_dn1")
    gw_in_t = _mm_simple(dproj, n1, mode="tn", tm=640, tn=d, tk=tm, out_dtype=BF16, name="mm_gw_in")
    dh0, g_attn = _rmsnorm_bwd(dn1, h0, attn_g, dh1, "attn_norm_bwd")

    grads = dict(meta=dh0[N_PAD:PREFIX], attn_g=g_attn, w_in_t=gw_in_t, fox_b=g_fox_b[:, :FOX_HEADS], ret_g=g_ret,
                 w_out=gw_out, ffn_g=g_ffn, w_up_t=gw_up_t, conv_w=g_conv_w8[:3], conv_b=g_conv_b, w_down=gw_down,
                 final_g=g_final)
    return loss_tile, dh0[PREFIX:], grads


def _pack(pieces, width=LANE):
    rows = []
    for p in pieces:
        flat = p.reshape(-1)
        pad = (-flat.shape[0]) % width
        rows.append(jnp.pad(flat, (0, pad)).reshape(-1, width))
    out = jnp.concatenate(rows, axis=0)
    return jnp.pad(out, ((0, (-out.shape[0]) % 8), (0, 0)))


def _unpack(packed, shapes, width=LANE):
    outs, r = [], 0
    for shp in shapes:
        size = int(np.prod(shp))
        nrows = -(-size // width)
        outs.append(packed[r:r + nrows].reshape(-1)[:size].reshape(shp))
        r += nrows
    return outs


def kernel(x, meta_tokens, attn_norm_g, w_in, fox_forget_b, ret_norm_g, w_out, ffn_norm_g, w_up, conv_w, conv_b, w_down, final_norm_g, loss_target, m_meta_tokens, m_attn_norm_g, m_w_in, m_fox_forget_b, m_ret_norm_g, m_w_out, m_ffn_norm_g, m_w_up, m_conv_w, m_conv_b, m_w_down, m_final_norm_g, v_meta_tokens, v_attn_norm_g, v_w_in, v_fox_forget_b, v_ret_norm_g, v_w_out, v_ffn_norm_g, v_w_up, v_conv_w, v_conv_b, v_w_down, v_final_norm_g):
    d = D_MODEL
    me = 4 * lax.axis_index("x") + 2 * lax.axis_index("y") + lax.axis_index("c")
    in_blk = IN_WIDTH // N_DEV
    in_blk_pad = 400
    up_blk = 2 * D_FF // N_DEV
    down_blk = D_FF // N_DEV
    cw_blk = D_FF // N_DEV

    w_in_loc = jnp.pad(w_in[0].T.astype(BF16), ((0, in_blk_pad - in_blk), (0, 0)))
    cw_loc = jnp.pad(conv_w[0], ((0, 5), (0, 384 - cw_blk)))
    g_in, g_out, g_up, g_down, g_meta, g_cw = _exchange(
        [w_in_loc, w_out[0].astype(BF16), w_up[0].T.astype(BF16), w_down[0].astype(BF16), meta_tokens, cw_loc],
        ["gather"] * 6, "gather_weights")
    w_in_t = jnp.pad(g_in[:, :in_blk].reshape(IN_WIDTH, d), ((0, IN_PAD - IN_WIDTH), (0, 0)))
    w_out_f = g_out.reshape(d, d)
    w_up_t = g_up.reshape(2, D_FF, d)
    w_down_f = g_down.reshape(D_FF, d)
    meta_f = g_meta.transpose(1, 0, 2).reshape(N_META, d)
    conv_w8 = jnp.pad(g_cw[:, :3, :cw_blk].transpose(1, 0, 2).reshape(3, D_FF), ((0, 5), (0, 0)))

    loss_tile, grad_x, gr = _local_step(
        x[0], loss_target[0], meta_f, attn_norm_g, w_in_t, fox_forget_b, ret_norm_g, w_out_f, ffn_norm_g,
        w_up_t, conv_w8, conv_b, w_down_f, final_norm_g.reshape(1, d))

    s_in = jnp.pad(gr["w_in_t"][:IN_WIDTH].reshape(N_DEV, in_blk, d), ((0, 0), (0, in_blk_pad - in_blk), (0, 0)))
    small_shapes = [(1, LANE), (1, d), (1, FOX_HEADS), (1, RET_V), (1, d), (1, D_FF), (1, d), (N_META, d), (3, D_FF)]
    small = _pack([loss_tile[0:1], gr["attn_g"], gr["fox_b"], gr["ret_g"], gr["ffn_g"], gr["conv_b"], gr["final_g"],
                   gr["meta"], gr["conv_w"]])
    r_in, r_out, r_up, r_down, r_small = _exchange(
        [s_in, gr["w_out"].reshape(N_DEV, d // N_DEV, d), gr["w_up_t"].reshape(N_DEV, up_blk, d),
         gr["w_down"].reshape(N_DEV, down_blk, d), small],
        ["scatter"] * 4 + ["gather"], "exchange_grads")
    g_w_in = _sum_slots(r_in, "sum_w_in", in_blk_pad)[:in_blk].T
    g_w_out = _sum_slots(r_out, "sum_w_out", d // N_DEV)
    g_w_up = _sum_slots(r_up, "sum_w_up", up_blk).T
    g_w_down = _sum_slots(r_down, "sum_w_down", down_blk)
    s_all = _sum_slots(r_small, "sum_small", r_small.shape[1])
    (loss_row, g_attn, g_fox_b, g_ret, g_ffn, g_conv_b, g_final, g_meta_full, g_cw_full) = _unpack(s_all, small_shapes)
    loss = loss_row[0, 0]
    g_meta_loc = lax.dynamic_slice(g_meta_full, (0, me * (d // N_DEV)), (N_META, d // N_DEV))
    g_cw_loc = lax.dynamic_slice(g_cw_full, (0, me * cw_blk), (3, cw_blk))

    d_w_in, m_w_in_n, v_w_in_n = _adamw(w_in[0], g_w_in, m_w_in[0], v_w_in[0], "adamw_w_in", 128)
    d_w_out, m_w_out_n, v_w_out_n = _adamw(w_out[0], g_w_out, m_w_out[0], v_w_out[0], "adamw_w_out", 128)
    d_w_up, m_w_up_n, v_w_up_n = _adamw(w_up[0], g_w_up, m_w_up[0], v_w_up[0], "adamw_w_up", 128)
    d_w_down, m_w_down_n, v_w_down_n = _adamw(w_down[0], g_w_down, m_w_down[0], v_w_down[0], "adamw_w_down", down_blk)
    sm_grads = [g_meta_loc, g_attn, g_fox_b, g_ret, g_ffn, g_cw_loc, g_conv_b, g_final.reshape(d)]
    sm_w = [meta_tokens, attn_norm_g, fox_forget_b, ret_norm_g, ffn_norm_g, conv_w[0], conv_b, final_norm_g]
    sm_m = [m_meta_tokens, m_attn_norm_g, m_fox_forget_b, m_ret_norm_g, m_ffn_norm_g, m_conv_w[0], m_conv_b, m_final_norm_g]
    sm_v = [v_meta_tokens, v_attn_norm_g, v_fox_forget_b, v_ret_norm_g, v_ffn_norm_g, v_conv_w[0], v_conv_b, v_final_norm_g]
    sm_shapes = [a.shape for a in sm_w]
    pk = [_pack(lst) for lst in (sm_w, sm_grads, sm_m, sm_v)]
    sm_d, sm_nm, sm_nv = _adamw(pk[0], pk[1], pk[2], pk[3], "adamw_small", pk[0].shape[0])
    dl = _unpack(sm_d, sm_shapes)
    ml = _unpack(sm_nm, sm_shapes)
    vl = _unpack(sm_nv, sm_shapes)

    def by_weight(meta_, attn_, w_in_, fox_, ret_, w_out_, ffn_, w_up_, cw_, cb_, w_down_, final_):
        return (meta_, attn_, w_in_[None], fox_, ret_, w_out_[None], ffn_, w_up_[None], cw_[None], cb_, w_down_[None], final_)

    grads_out = by_weight(g_meta_loc, g_attn, g_w_in, g_fox_b, g_ret, g_w_out, g_ffn, g_w_up, g_cw_loc, g_conv_b,
                          g_w_down, g_final.reshape(d))
    delta_out = by_weight(dl[0], dl[1], d_w_in, dl[2], dl[3], d_w_out, dl[4], d_w_up, dl[5], dl[6], d_w_down, dl[7])
    m_out = by_weight(ml[0], ml[1], m_w_in_n, ml[2], ml[3], m_w_out_n, ml[4], m_w_up_n, ml[5], ml[6], m_w_down_n, ml[7])
    v_out = by_weight(vl[0], vl[1], v_w_in_n, vl[2], vl[3], v_w_out_n, vl[4], v_w_up_n, vl[5], vl[6], v_w_down_n, vl[7])
    return (loss, grad_x[None]) + grads_out + delta_out + m_out + v_out
```

```python
import numpy as np
import jax
import jax.numpy as jnp
from jax import lax
from jax.experimental import pallas as pl
from jax.experimental.pallas import tpu as pltpu

F32 = jnp.float32
BF16 = jnp.bfloat16

D_MODEL = 1024
N_META = 16
N_PAD = 112
PREFIX = 128
RET_HEADS = 4
RET_DK = 64
RET_DV = 128
FOX_HEADS = 8
FOX_DH = 64
D_FF = 2816
ROPE_BASE = 10000.0
EPS = 1e-6
NEG = -1e30
RET_QK = RET_HEADS * RET_DK
RET_V = RET_HEADS * RET_DV
FOX_W = FOX_HEADS * FOX_DH
IN_WIDTH = 2 * RET_QK + 2 * RET_V + 3 * FOX_W + FOX_HEADS
IN_PAD = 3200
FF_COL_BLOCK = (IN_WIDTH - FOX_HEADS) // 128
QK_SCALE = 0.125

ADAM_LR = 0.001
ADAM_B1 = 0.9
ADAM_B2 = 0.999
ADAM_EPS = 1e-08
ADAM_WD = 0.01
ADAM_STEP = 10

N_DEV = 8
LANE = 128
ROW_TILE = 128
TOK_TILE = 384

NN = (((1,), (0,)), ((), ()))
NT = (((1,), (1,)), ((), ()))
TN = (((0,), (0,)), ((), ()))


def _pcall(body, **kw):
    return pl.pallas_call(body, **kw)


def _params(*sem):
    return pltpu.CompilerParams(dimension_semantics=sem)


def _dot(a, b, dims=NN):
    return lax.dot_general(a, b, dims, preferred_element_type=F32)


def _sigmoid(x):
    return 1.0 / (1.0 + jnp.exp(-x))


def _matmul(a, b, *, mode, grid, a_spec, b_spec, o_spec, out_shape, name, add=None, add_spec=None):
    dims = {"nn": NN, "nt": NT, "tn": TN}[mode]
    nk = grid[2]
    has_add = add is not None

    def body(*refs):
        if has_add:
            a_ref, b_ref, add_ref, o_ref = refs[:4]
        else:
            a_ref, b_ref, o_ref = refs[:3]
        part = _dot(a_ref[...].astype(BF16), b_ref[...].astype(BF16), dims)

        def finish(acc):
            if has_add:
                acc = acc + add_ref[...]
            o_ref[...] = acc.astype(o_ref.dtype)

        if nk == 1:
            finish(part)
        else:
            acc_ref = refs[-1]
            k = pl.program_id(2)

            @pl.when(k == 0)
            def _():
                acc_ref[...] = part

            @pl.when(k > 0)
            def _():
                acc_ref[...] += part

            @pl.when(k == nk - 1)
            def _():
                finish(acc_ref[...])

    in_specs = [a_spec, b_spec] + ([add_spec] if has_add else [])
    args = (a, b) + ((add,) if has_add else ())
    scratch = [] if nk == 1 else [pltpu.VMEM(tuple(d for d in o_spec.block_shape if d is not None), F32)]
    return _pcall(
        body, name=name, grid=grid, in_specs=in_specs, out_specs=o_spec, out_shape=out_shape,
        scratch_shapes=scratch, compiler_params=_params("parallel", "parallel", "arbitrary"),
    )(*args)


def _mm_simple(a, b, *, mode, tm, tn, tk, out_dtype, name, add=None):
    if mode == "tn":
        K, M = a.shape
    else:
        M, K = a.shape
    N = b.shape[0] if mode == "nt" else b.shape[1]
    grid = (M // tm, N // tn, K // tk)
    a_spec = pl.BlockSpec((tk, tm), lambda i, j, k: (k, i)) if mode == "tn" else pl.BlockSpec((tm, tk), lambda i, j, k: (i, k))
    b_spec = pl.BlockSpec((tn, tk), lambda i, j, k: (j, k)) if mode == "nt" else pl.BlockSpec((tk, tn), lambda i, j, k: (k, j))
    o_spec = pl.BlockSpec((tm, tn), lambda i, j, k: (i, j))
    return _matmul(a, b, mode=mode, grid=grid, a_spec=a_spec, b_spec=b_spec, o_spec=o_spec,
                   out_shape=jax.ShapeDtypeStruct((M, N), out_dtype), name=name, add=add,
                   add_spec=o_spec if add is not None else None)


def _prep_norm(x, meta, gain, name):
    seq, d = x.shape
    t = seq + PREFIX
    nb = t // ROW_TILE

    def body(x_ref, meta_ref, g_ref, h_ref, n_ref):
        i = pl.program_id(0)

        @pl.when(i == 0)
        def _():
            h_ref[0:N_PAD, :] = jnp.zeros((N_PAD, d), F32)
            h_ref[N_PAD:ROW_TILE, :] = meta_ref[...]

        @pl.when(i > 0)
        def _():
            h_ref[...] = x_ref[...]

        h = h_ref[...]
        r = lax.rsqrt(jnp.mean(h * h, axis=-1, keepdims=True) + EPS)
        n_ref[...] = (h * r * g_ref[...]).astype(BF16)

    return _pcall(
        body, name=name, grid=(nb,),
        in_specs=[pl.BlockSpec((ROW_TILE, d), lambda i: (jnp.maximum(i - 1, 0), 0)),
                  pl.BlockSpec((N_META, d), lambda i: (0, 0)),
                  pl.BlockSpec((1, d), lambda i: (0, 0))],
        out_specs=[pl.BlockSpec((ROW_TILE, d), lambda i: (i, 0)), pl.BlockSpec((ROW_TILE, d), lambda i: (i, 0))],
        out_shape=[jax.ShapeDtypeStruct((t, d), F32), jax.ShapeDtypeStruct((t, d), BF16)],
        compiler_params=_params("parallel"),
    )(x, meta, gain)


def _rmsnorm(h, gain, name):
    t, d = h.shape

    def body(h_ref, g_ref, n_ref):
        x = h_ref[...]
        r = lax.rsqrt(jnp.mean(x * x, axis=-1, keepdims=True) + EPS)
        n_ref[...] = (x * r * g_ref[...]).astype(BF16)

    return _pcall(
        body, name=name, grid=(t // TOK_TILE,),
        in_specs=[pl.BlockSpec((TOK_TILE, d), lambda i: (i, 0)), pl.BlockSpec((1, d), lambda i: (0, 0))],
        out_specs=pl.BlockSpec((TOK_TILE, d), lambda i: (i, 0)),
        out_shape=jax.ShapeDtypeStruct((t, d), BF16),
        compiler_params=_params("parallel"),
    )(h, gain)


def _rmsnorm_bwd(dn, h, gain, dres, name):
    t, d = h.shape

    def body(dn_ref, h_ref, g_ref, dres_ref, dh_ref, gg_ref):
        i = pl.program_id(0)
        x = h_ref[...]
        r = lax.rsqrt(jnp.mean(x * x, axis=-1, keepdims=True) + EPS)
        xhat = x * r
        dy = dn_ref[...]
        u = dy * g_ref[...]
        dh_ref[...] = dres_ref[...] + r * (u - xhat * jnp.mean(u * xhat, axis=-1, keepdims=True))
        part = jnp.sum(dy * xhat, axis=0, keepdims=True)

        @pl.when(i == 0)
        def _():
            gg_ref[...] = part

        @pl.when(i > 0)
        def _():
            gg_ref[...] += part

    return _pcall(
        body, name=name, grid=(t // TOK_TILE,),
        in_specs=[pl.BlockSpec((TOK_TILE, d), lambda i: (i, 0)), pl.BlockSpec((TOK_TILE, d), lambda i: (i, 0)),
                  pl.BlockSpec((1, d), lambda i: (0, 0)), pl.BlockSpec((TOK_TILE, d), lambda i: (i, 0))],
        out_specs=[pl.BlockSpec((TOK_TILE, d), lambda i: (i, 0)), pl.BlockSpec((1, d), lambda i: (0, 0))],
        out_shape=[jax.ShapeDtypeStruct((t, d), F32), jax.ShapeDtypeStruct((1, d), F32)],
        compiler_params=_params("arbitrary"),
    )(dn, h, gain, dres)


def _loss_bwd(h2, target, gain, name):
    t, d = h2.shape
    nb = t // ROW_TILE

    def body(h_ref, tgt_ref, g_ref, loss_ref, dh_ref, gg_ref):
        i = pl.program_id(0)

        @pl.when(i == 0)
        def _():
            loss_ref[...] = jnp.zeros_like(loss_ref)
            gg_ref[...] = jnp.zeros_like(gg_ref)
            dh_ref[...] = jnp.zeros_like(dh_ref)

        @pl.when(i > 0)
        def _():
            x = h_ref[...]
            r = lax.rsqrt(jnp.mean(x * x, axis=-1, keepdims=True) + EPS)
            xhat = x * r
            g = g_ref[...]
            err = xhat * g - tgt_ref[...]
            loss_ref[...] += 0.5 * jnp.sum(jnp.mean(err * err, axis=-1, keepdims=True))
            dy = err * (1.0 / d)
            u = dy * g
            dh_ref[...] = r * (u - xhat * jnp.mean(u * xhat, axis=-1, keepdims=True))
            gg_ref[...] += jnp.sum(dy * xhat, axis=0, keepdims=True)

    return _pcall(
        body, name=name, grid=(nb,),
        in_specs=[pl.BlockSpec((ROW_TILE, d), lambda i: (i, 0)),
                  pl.BlockSpec((ROW_TILE, d), lambda i: (jnp.maximum(i - 1, 0), 0)),
                  pl.BlockSpec((1, d), lambda i: (0, 0))],
        out_specs=[pl.BlockSpec((8, LANE), lambda i: (0, 0)), pl.BlockSpec((ROW_TILE, d), lambda i: (i, 0)),
                   pl.BlockSpec((1, d), lambda i: (0, 0))],
        out_shape=[jax.ShapeDtypeStruct((8, LANE), F32), jax.ShapeDtypeStruct((t, d), F32),
                   jax.ShapeDtypeStruct((1, d), F32)],
        compiler_params=_params("arbitrary"),
    )(h2, target, gain)


def _ret_consts(bk):
    gam = 1.0 - 2.0 ** (-5.0 - np.arange(RET_HEADS))
    n = np.arange(bk)
    same_or_earlier_chunk = (n[None, :] // 64) <= (n[:, None] // 64)
    w = gam[:, None, None] ** np.abs(n[:, None] - n[None, :])[None] * same_or_earlier_chunk[None]
    wq = gam[:, None] ** (n[None, :] + 1.0)
    wk = gam[:, None] ** (bk - 1.0 - n[None, :])
    mask = (np.arange(RET_QK)[None, :] // RET_DK) == np.arange(RET_HEADS)[:, None]
    return (jnp.asarray(w, F32), jnp.asarray(wq[:, :, None], F32), jnp.asarray(wk[:, :, None], F32),
            jnp.asarray(mask[:, None, :], F32), [float(g ** bk) for g in gam])


def _rope_tables(t):
    half = RET_DK // 2
    inv = 1.0 / (ROPE_BASE ** (jnp.arange(half, dtype=F32) / half))
    ang = jnp.arange(t).astype(F32)[:, None] * inv[None, :]
    cos, sin = jnp.cos(ang), jnp.sin(ang)
    return (jnp.tile(jnp.concatenate([cos, cos], axis=1), (1, RET_HEADS)),
            jnp.tile(jnp.concatenate([-sin, sin], axis=1), (1, RET_HEADS)))


def _swap_halves(x):
    outs = []
    for s in range(x.shape[1] // LANE):
        xs = x[:, LANE * s:LANE * (s + 1)]
        lane = lax.broadcasted_iota(jnp.int32, xs.shape, 1)
        outs.append(jnp.where((lane & 32) == 0, pltpu.roll(xs, LANE - 32, axis=1), pltpu.roll(xs, 32, axis=1)))
    return outs[0] if len(outs) == 1 else jnp.concatenate(outs, axis=1)


def _rope(x, cos, sin_signed):
    return x * cos + _swap_halves(x) * sin_signed


def _rope_t(dx, cos, sin_signed):
    return dx * cos + _swap_halves(dx * sin_signed)


def _ret_fwd(proj, cos, sin, gain, name):
    t = proj.shape[0]
    bk = TOK_TILE
    nb = t // bk
    w, wq, wk, mask, g_blk = _ret_consts(bk)

    def body(q_ref, k_ref, v_ref, rg_ref, cos_ref, sin_ref, w_ref, wq_ref, wk_ref, mask_ref, gain_ref,
             opre_ref, og_ref, st_ref, r_ref):
        i = pl.program_id(0)

        @pl.when(i == 0)
        def _():
            r_ref[...] = jnp.zeros_like(r_ref)

        c, s = cos_ref[...], sin_ref[...]
        valid = ((i * bk + lax.broadcasted_iota(jnp.int32, (bk, 1), 0)) >= N_PAD).astype(F32)
        qr = _rope(q_ref[...], c, s)
        kr = _rope(k_ref[...], c, s) * QK_SCALE * valid
        kb = kr.astype(BF16)
        for h in range(RET_HEADS):
            hm = mask_ref[h]
            cols = slice(RET_DV * h, RET_DV * (h + 1))
            vh = v_ref[:, cols].astype(BF16)
            r_prev = r_ref[h]
            st_ref[0, h] = r_prev
            sm = _dot((qr * hm).astype(BF16), kb, NT) * w_ref[h]
            o = _dot(sm.astype(BF16), vh) + _dot((qr * (hm * wq_ref[h])).astype(BF16), r_prev.astype(BF16))
            r_ref[h] = g_blk[h] * r_prev + _dot((kr * wk_ref[h]).astype(BF16), vh, TN)
            opre_ref[:, cols] = o
            rstd = lax.rsqrt(jnp.mean(o * o, axis=-1, keepdims=True) + EPS)
            rg = rg_ref[:, cols]
            og_ref[:, cols] = (o * rstd * gain_ref[:, cols] * (rg * _sigmoid(rg))).astype(BF16)

    full = lambda shape: pl.BlockSpec(shape, lambda i: (0,) * len(shape))
    return _pcall(
        body, name=name, grid=(nb,),
        in_specs=[pl.BlockSpec((bk, RET_QK), lambda i: (i, 0)), pl.BlockSpec((bk, RET_QK), lambda i: (i, 1)),
                  pl.BlockSpec((bk, RET_V), lambda i: (i, 1)), pl.BlockSpec((bk, RET_V), lambda i: (i, 2)),
                  pl.BlockSpec((bk, RET_QK), lambda i: (i, 0)), pl.BlockSpec((bk, RET_QK), lambda i: (i, 0)),
                  full((RET_HEADS, bk, bk)), full((RET_HEADS, bk, 1)), full((RET_HEADS, bk, 1)),
                  full((RET_HEADS, 1, RET_QK)), full((1, RET_V))],
        out_specs=[pl.BlockSpec((bk, RET_V), lambda i: (i, 0)), pl.BlockSpec((bk, RET_V), lambda i: (i, 0)),
                   pl.BlockSpec((1, RET_HEADS, RET_QK, RET_DV), lambda i: (i, 0, 0, 0))],
        out_shape=[jax.ShapeDtypeStruct((t, RET_V), F32), jax.ShapeDtypeStruct((t, RET_V + FOX_W), BF16),
                   jax.ShapeDtypeStruct((nb, RET_HEADS, RET_QK, RET_DV), F32)],
        scratch_shapes=[pltpu.VMEM((RET_HEADS, RET_QK, RET_DV), F32)],
        compiler_params=_params("arbitrary"),
    )(proj, proj, proj, proj, cos, sin, w, wq, wk, mask, gain)


def _ret_bwd(proj, cos, sin, gain, dmixed, opre, states, name):
    t = proj.shape[0]
    bk = TOK_TILE
    nb = t // bk
    w, wq, wk, mask, g_blk = _ret_consts(bk)
    v0, g0 = 2 * RET_QK, 2 * RET_QK + RET_V

    def body(q_ref, k_ref, v_ref, rg_ref, cos_ref, sin_ref, w_ref, wq_ref, wk_ref, mask_ref, gain_ref,
             dog_ref, opre_ref, st_ref, dp_ref, gg_ref, dr_ref):
        step = pl.program_id(0)
        i = nb - 1 - step

        @pl.when(step == 0)
        def _():
            dr_ref[...] = jnp.zeros_like(dr_ref)
            gg_ref[...] = jnp.zeros_like(gg_ref)

        c, s = cos_ref[...], sin_ref[...]
        valid = ((i * bk + lax.broadcasted_iota(jnp.int32, (bk, 1), 0)) >= N_PAD).astype(F32)
        qr = _rope(q_ref[...], c, s)
        kr = _rope(k_ref[...], c, s) * QK_SCALE * valid
        kb = kr.astype(BF16)
        dqr = jnp.zeros((bk, RET_QK), F32)
        dkr = jnp.zeros((bk, RET_QK), F32)
        for h in range(RET_HEADS):
            hm = mask_ref[h]
            cols = slice(RET_DV * h, RET_DV * (h + 1))
            vh = v_ref[:, cols].astype(BF16)
            o = opre_ref[:, cols]
            rstd = lax.rsqrt(jnp.mean(o * o, axis=-1, keepdims=True) + EPS)
            xhat = o * rstd
            rg = rg_ref[:, cols]
            sg = _sigmoid(rg)
            gate = rg * sg
            gn = gain_ref[:, cols]
            dog = dog_ref[:, cols]
            dp_ref[:, g0 + RET_DV * h:g0 + RET_DV * (h + 1)] = (
                dog * xhat * gn * (sg * (1.0 + rg * (1.0 - sg)))).astype(BF16)
            gg_ref[:, cols] += jnp.sum(dog * xhat * gate, axis=0, keepdims=True)
            dxh = dog * gn * gate
            do = (rstd * (dxh - xhat * jnp.mean(dxh * xhat, axis=-1, keepdims=True))).astype(BF16)
            qm = (qr * hm).astype(BF16)
            qw = (qr * (hm * wq_ref[h])).astype(BF16)
            kw = (kr * wk_ref[h]).astype(BF16)
            wh = w_ref[h]
            sm = (_dot(qm, kb, NT) * wh).astype(BF16)
            ds = (_dot(do, vh, NT) * wh).astype(BF16)
            dr = dr_ref[h]
            drb = dr.astype(BF16)
            dp_ref[:, v0 + RET_DV * h:v0 + RET_DV * (h + 1)] = (_dot(sm, do, TN) + _dot(kw, drb)).astype(BF16)
            dqr = dqr + _dot(ds, kb) * hm + _dot(do, st_ref[0, h].astype(BF16), NT) * (hm * wq_ref[h])
            dkr = dkr + _dot(ds, qm, TN) + _dot(vh, drb, NT) * wk_ref[h]
            dr_ref[h] = g_blk[h] * dr + _dot(qw, do, TN)
        dp_ref[:, 0:RET_QK] = _rope_t(dqr, c, s).astype(BF16)
        dp_ref[:, RET_QK:2 * RET_QK] = _rope_t(dkr * (QK_SCALE * valid), c, s).astype(BF16)

    full = lambda shape: pl.BlockSpec(shape, lambda i: (0,) * len(shape))
    rev = lambda col: (lambda i: (nb - 1 - i, col))
    return _pcall(
        body, name=name, grid=(nb,),
        in_specs=[pl.BlockSpec((bk, RET_QK), rev(0)), pl.BlockSpec((bk, RET_QK), rev(1)),
                  pl.BlockSpec((bk, RET_V), rev(1)), pl.BlockSpec((bk, RET_V), rev(2)),
                  pl.BlockSpec((bk, RET_QK), rev(0)), pl.BlockSpec((bk, RET_QK), rev(0)),
                  full((RET_HEADS, bk, bk)), full((RET_HEADS, bk, 1)), full((RET_HEADS, bk, 1)),
                  full((RET_HEADS, 1, RET_QK)), full((1, RET_V)),
                  pl.BlockSpec((bk, RET_V), rev(0)), pl.BlockSpec((bk, RET_V), rev(0)),
                  pl.BlockSpec((1, RET_HEADS, RET_QK, RET_DV), lambda i: (nb - 1 - i, 0, 0, 0))],
        out_specs=[pl.BlockSpec((bk, g0 + RET_V), rev(0)), pl.BlockSpec((1, RET_V), lambda i: (0, 0))],
        out_shape=[jax.ShapeDtypeStruct((t, IN_PAD), BF16), jax.ShapeDtypeStruct((1, RET_V), F32)],
        scratch_shapes=[pltpu.VMEM((RET_HEADS, RET_QK, RET_DV), F32)],
        compiler_params=_params("arbitrary"),
    )(proj, proj, proj, proj, cos, sin, w, wq, wk, mask, gain, dmixed, opre, states)


def _forget_cumsum(proj, bias, name):
    t = proj.shape[0]
    nb = t // ROW_TILE
    tril = jnp.asarray(np.tril(np.ones((ROW_TILE, ROW_TILE))), F32)

    def body(z_ref, b_ref, tril_ref, c_ref, carry_ref):
        i = pl.program_id(0)

        @pl.when(i == 0)
        def _():
            carry_ref[...] = jnp.zeros_like(carry_ref)

        z = z_ref[...] + b_ref[...]
        logf = jnp.minimum(z, 0.0) - jnp.log(1.0 + jnp.exp(-jnp.abs(z)))
        c = lax.dot_general(tril_ref[...], logf, NN, precision=lax.Precision.HIGHEST,
                            preferred_element_type=F32) + carry_ref[...]
        c_ref[...] = c
        carry_ref[...] = c[ROW_TILE - 1:ROW_TILE, :]

    return _pcall(
        body, name=name, grid=(nb,),
        in_specs=[pl.BlockSpec((ROW_TILE, LANE), lambda i: (i, FF_COL_BLOCK)), pl.BlockSpec((1, LANE), lambda i: (0, 0)),
                  pl.BlockSpec((ROW_TILE, ROW_TILE), lambda i: (0, 0))],
        out_specs=pl.BlockSpec((ROW_TILE, LANE), lambda i: (i, 0)),
        out_shape=jax.ShapeDtypeStruct((t, LANE), F32),
        scratch_shapes=[pltpu.VMEM((1, LANE), F32)],
        compiler_params=_params("arbitrary"),
    )(proj, bias, tril)


def _forget_cumsum_bwd(proj, bias, drs, dcs, dproj, name):
    t = proj.shape[0]
    nb = t // ROW_TILE
    triu = jnp.asarray(np.triu(np.ones((ROW_TILE, ROW_TILE))), F32)

    def body(z_ref, b_ref, triu_ref, drs_ref, dcs_ref, dproj_in, dz_ref, gb_ref, carry_ref):
        step = pl.program_id(0)

        @pl.when(step == 0)
        def _():
            carry_ref[...] = jnp.zeros_like(carry_ref)
            gb_ref[...] = jnp.zeros_like(gb_ref)

        dlogf = lax.dot_general(triu_ref[...], drs_ref[...] - dcs_ref[...], NN, precision=lax.Precision.HIGHEST,
                                preferred_element_type=F32) + carry_ref[...]
        carry_ref[...] = dlogf[0:1, :]
        z = z_ref[...] + b_ref[...]
        is_head = lax.broadcasted_iota(jnp.int32, (ROW_TILE, LANE), 1) < FOX_HEADS
        dz = jnp.where(is_head, dlogf / (1.0 + jnp.exp(z)), 0.0)
        dz_ref[...] = dz.astype(BF16)
        gb_ref[...] += jnp.sum(dz, axis=0, keepdims=True)

    return _pcall(
        body, name=name, grid=(nb,),
        in_specs=[pl.BlockSpec((ROW_TILE, LANE), lambda i: (nb - 1 - i, FF_COL_BLOCK)),
                  pl.BlockSpec((1, LANE), lambda i: (0, 0)),
                  pl.BlockSpec((ROW_TILE, ROW_TILE), lambda i: (0, 0)),
                  pl.BlockSpec((ROW_TILE, LANE), lambda i: (nb - 1 - i, 0)),
                  pl.BlockSpec((ROW_TILE, LANE), lambda i: (nb - 1 - i, 0)),
                  pl.BlockSpec(memory_space=pl.ANY)],
        out_specs=[pl.BlockSpec((ROW_TILE, LANE), lambda i: (nb - 1 - i, FF_COL_BLOCK)),
                   pl.BlockSpec((1, LANE), lambda i: (0, 0))],
        out_shape=[jax.ShapeDtypeStruct(dproj.shape, BF16), jax.ShapeDtypeStruct((1, LANE), F32)],
        input_output_aliases={5: 0},
        scratch_shapes=[pltpu.VMEM((1, LANE), F32)],
        compiler_params=_params("arbitrary"),
    )(proj, bias, triu, drs, dcs, dproj)


FOX_PAIRS = FOX_HEADS // 2
L_ONE_Q = FOX_DH
L_ONE_K = FOX_DH + 3
L_LSE = FOX_DH + 4


def _split3(x):
    hi = x.astype(BF16).astype(F32)
    r = x - hi
    mid = r.astype(BF16).astype(F32)
    return hi, mid, r - mid


def _head_to_low(slab, e):
    return slab if e == 0 else pltpu.roll(slab, FOX_DH, axis=1)


def _pair(a, b, low):
    return jnp.where(low, a, pltpu.roll(b, FOX_DH, axis=1))


def _fox_prep(proj, c, name):
    t = proj.shape[0]
    tq = TOK_TILE

    def body(p_ref, c_ref, qa_ref, ka_ref, va_ref):
        i = pl.program_id(0)
        lane = lax.broadcasted_iota(jnp.int32, (tq, LANE), 1)
        low = lane < FOX_DH
        live = (i * tq + lax.broadcasted_iota(jnp.int32, (tq, 1), 0)) >= N_PAD
        q_tail = jnp.where(lane < L_ONE_Q + 3, 1.0, 0.0)
        k_ones = (lane >= L_ONE_K) & (lane < L_ONE_K + 4)
        v_tail = jnp.where(lane < FOX_DH + 2, 1.0, 0.0)
        for pair in range(FOX_PAIRS):
            base = 3 * LANE * pair
            for e in range(2):
                h = 2 * pair + e
                q = _head_to_low(p_ref[:, base:base + LANE], e)
                k = _head_to_low(p_ref[:, base + LANE:base + 2 * LANE], e)
                v = _head_to_low(p_ref[:, base + 2 * LANE:base + 3 * LANE], e)
                hi, mid, lo = _split3(jnp.where(live, -c_ref[:, h:h + 1], NEG))
                ka = jnp.where(low, k, jnp.where(k_ones, 1.0, 0.0))
                ka = jnp.where(lane == L_ONE_Q, hi, jnp.where(lane == L_ONE_Q + 1, mid, jnp.where(lane == L_ONE_Q + 2, lo, ka)))
                qa_ref[h] = jnp.where(low, q * QK_SCALE, q_tail).astype(BF16)
                ka_ref[h] = ka.astype(BF16)
                va_ref[h] = jnp.where(low, v, v_tail).astype(BF16)

    out = jax.ShapeDtypeStruct((FOX_HEADS, t, LANE), BF16)
    ospec = pl.BlockSpec((FOX_HEADS, tq, LANE), lambda i: (0, i, 0))
    return _pcall(
        body, name=name, grid=(t // tq,),
        in_specs=[pl.BlockSpec((tq, 3 * FOX_W), lambda i: (i, 1)), pl.BlockSpec((tq, LANE), lambda i: (i, 0))],
        out_specs=[ospec, ospec, ospec], out_shape=[out, out, out],
        compiler_params=_params("parallel"),
    )(proj, c)


def _fox_fwd(qa, ka, va, mixed, name):
    nh, nq, tq, _ = qa.shape
    t = nq * tq

    def body(qa_ref, ka_ref, va_ref, mixed_in, mixed_ref, o_ref, lse_ref):
        i = pl.program_id(1)
        lane = lax.broadcasted_iota(jnp.int32, (tq, LANE), 1)
        causal = lax.broadcasted_iota(jnp.int32, (tq, tq), 1) <= lax.broadcasted_iota(jnp.int32, (tq, tq), 0)
        outs, lses = [], []
        for e in range(2):
            q = qa_ref[e]

            def step(j, carry, diagonal):
                m, acc = carry
                s = _dot(q, ka_ref[e, j], NT)
                if diagonal:
                    s = jnp.where(causal, s, NEG)
                m_new = jnp.maximum(m, jnp.max(s, axis=-1, keepdims=True))
                p = jnp.exp(s - m_new)
                return m_new, jnp.exp(m - m_new) * acc + _dot(p.astype(BF16), va_ref[e, j])

            carry = (jnp.full((tq, 1), NEG, F32), jnp.zeros((tq, LANE), F32))
            carry = lax.fori_loop(0, i, lambda j, cr: step(j, cr, False), carry)
            m, acc = step(i, carry, True)
            l = acc[:, FOX_DH:FOX_DH + 1]
            outs.append(acc / l)
            lses.append(m + jnp.log(l))
        o_pair = _pair(outs[0], outs[1], lane < FOX_DH)
        mixed_ref[...] = o_pair.astype(BF16)
        o_ref[...] = o_pair
        lse_ref[...] = jnp.where(lane == 0, lses[0], jnp.where(lane == 1, lses[1], 0.0))

    whole = pl.BlockSpec((2, nq, tq, LANE), lambda hp, i: (hp, 0, 0, 0))
    return _pcall(
        body, name=name, grid=(FOX_PAIRS, nq),
        in_specs=[pl.BlockSpec((2, None, tq, LANE), lambda hp, i: (hp, i, 0, 0)), whole, whole,
                  pl.BlockSpec(memory_space=pl.ANY)],
        out_specs=[pl.BlockSpec((tq, LANE), lambda hp, i: (i, RET_V // LANE + hp)),
                   pl.BlockSpec((tq, LANE), lambda hp, i: (i, hp)),
                   pl.BlockSpec((None, tq, LANE), lambda hp, i: (hp, i, 0))],
        out_shape=[jax.ShapeDtypeStruct(mixed.shape, BF16), jax.ShapeDtypeStruct((t, FOX_W), F32),
                   jax.ShapeDtypeStruct((FOX_PAIRS, t, LANE), F32)],
        input_output_aliases={3: 0},
        compiler_params=_params("parallel", "parallel"),
    )(qa, ka, va, mixed)


def _fox_prep_bwd(dmixed, o_fox, lse, qa, name):
    t = dmixed.shape[0]
    tq = TOK_TILE

    def body(dm_ref, o_ref, lse_ref, qa_ref, qab_ref, doa_ref):
        i = pl.program_id(0)
        lane = lax.broadcasted_iota(jnp.int32, (tq, LANE), 1)
        low = lane < FOX_DH
        live = (i * tq + lax.broadcasted_iota(jnp.int32, (tq, 1), 0)) >= N_PAD
        for pair in range(FOX_PAIRS):
            cols = slice(LANE * pair, LANE * (pair + 1))
            d_slab = dm_ref[:, cols]
            prod = d_slab * o_ref[:, cols]
            for e in range(2):
                h = 2 * pair + e
                nd = -jnp.sum(jnp.where(low, _head_to_low(prod, e), 0.0), axis=-1, keepdims=True)
                nd_hi = nd.astype(BF16).astype(F32)
                doa = jnp.where(low, _head_to_low(d_slab, e), 0.0)
                doa = jnp.where(lane == FOX_DH, nd_hi, jnp.where(lane == FOX_DH + 1, nd - nd_hi, doa))
                doa_ref[h] = doa.astype(BF16)
                hi, mid, lo = _split3(jnp.where(live, -lse_ref[pair][:, e:e + 1], 0.0))
                qab = qa_ref[h].astype(F32)
                qab = jnp.where(lane == L_LSE, hi, jnp.where(lane == L_LSE + 1, mid, jnp.where(lane == L_LSE + 2, lo, qab)))
                qab_ref[h] = qab.astype(BF16)

    out = jax.ShapeDtypeStruct((FOX_HEADS, t, LANE), BF16)
    hspec = pl.BlockSpec((FOX_HEADS, tq, LANE), lambda i: (0, i, 0))
    return _pcall(
        body, name=name, grid=(t // tq,),
        in_specs=[pl.BlockSpec((tq, FOX_W), lambda i: (i, 1)), pl.BlockSpec((tq, FOX_W), lambda i: (i, 0)),
                  pl.BlockSpec((FOX_PAIRS, tq, LANE), lambda i: (0, i, 0)), hspec],
        out_specs=[hspec, hspec], out_shape=[out, out],
        compiler_params=_params("parallel"),
    )(dmixed, o_fox, lse, qa)


def _fox_bwd(qab, doa, ka, va, dproj, name):
    nh, nq, tq, _ = qab.shape
    t = nq * tq
    pair0 = (2 * RET_QK + 2 * RET_V) // (3 * LANE)

    def body(qab_ref, doa_ref, ka_ref, va_ref, dproj_in, dp_ref, drs_ref, dcs_ref, dq_ref):
        hp, j = pl.program_id(0), pl.program_id(1)

        @pl.when((hp == 0) & (j == 0))
        def _():
            drs_ref[...] = jnp.zeros_like(drs_ref)
            dcs_ref[...] = jnp.zeros_like(dcs_ref)

        @pl.when(j == 0)
        def _():
            dq_ref[...] = jnp.zeros_like(dq_ref)

        lane = lax.broadcasted_iota(jnp.int32, (tq, LANE), 1)
        low = lane < FOX_DH
        causal = lax.broadcasted_iota(jnp.int32, (tq, tq), 1) <= lax.broadcasted_iota(jnp.int32, (tq, tq), 0)

        def by_head(a, b, col):
            return jnp.where(lane == 2 * hp, a[:, col:col + 1], jnp.where(lane == 2 * hp + 1, b[:, col:col + 1], 0.0))

        dks, dvs = [], []
        for e in range(2):
            kb, vb = ka_ref[e], va_ref[e]

            def step(i, carry, diagonal):
                dk, dv = carry
                qb, dob = qab_ref[e, i], doa_ref[e, i]
                p = jnp.exp(_dot(qb, kb, NT))
                if diagonal:
                    p = jnp.where(causal, p, 0.0)
                ds = (p * _dot(dob, vb, NT)).astype(BF16)
                dq_ref[e, i] += _dot(ds, kb)
                return dk + _dot(ds, qb, TN), dv + _dot(p.astype(BF16), dob, TN)

            carry = step(j, (jnp.zeros((tq, LANE), F32), jnp.zeros((tq, LANE), F32)), True)
            dk, dv = lax.fori_loop(j + 1, nq, lambda i, cr: step(i, cr, False), carry)
            dks.append(dk)
            dvs.append(dv)
        rows = pl.ds(pl.multiple_of(j * tq, tq), tq)
        dp_ref[rows, LANE:2 * LANE] = _pair(dks[0], dks[1], low).astype(BF16)
        dp_ref[rows, 2 * LANE:3 * LANE] = _pair(dvs[0], dvs[1], low).astype(BF16)
        dcs_ref[rows, :] += by_head(dks[0], dks[1], L_ONE_Q)

        @pl.when(j == nq - 1)
        def _():
            for blk in range(nq):
                r = slice(blk * tq, (blk + 1) * tq)
                a, b = dq_ref[0, blk], dq_ref[1, blk]
                dp_ref[r, 0:LANE] = (_pair(a, b, low) * QK_SCALE).astype(BF16)
                drs_ref[r, :] += by_head(a, b, L_ONE_K)

    whole = pl.BlockSpec((2, nq, tq, LANE), lambda hp, j: (hp, 0, 0, 0))
    blk = pl.BlockSpec((2, None, tq, LANE), lambda hp, j: (hp, j, 0, 0))
    sums = pl.BlockSpec((t, LANE), lambda hp, j: (0, 0))
    return _pcall(
        body, name=name, grid=(FOX_PAIRS, nq),
        in_specs=[whole, whole, blk, blk, pl.BlockSpec(memory_space=pl.ANY)],
        out_specs=[pl.BlockSpec((t, 3 * LANE), lambda hp, j: (0, pair0 + hp)), sums, sums],
        out_shape=[jax.ShapeDtypeStruct(dproj.shape, BF16), jax.ShapeDtypeStruct((t, LANE), F32),
                   jax.ShapeDtypeStruct((t, LANE), F32)],
        input_output_aliases={4: 0},
        scratch_shapes=[pltpu.VMEM((2, nq, tq, LANE), F32)],
        compiler_params=_params("arbitrary", "arbitrary"),
    )(qab, doa, ka, va, dproj)


HALO = 8


def _rows_ext(ref, r0, rows, t, before, after):
    lo, hi = r0 - before, r0 + rows + after
    parts = []
    if lo < 0:
        parts.append(jnp.zeros((-lo, LANE), F32))
    parts.append(ref[max(lo, 0):min(hi, t), :].astype(F32))
    if hi > t:
        parts.append(jnp.zeros((hi - t, LANE), F32))
    return parts[0] if len(parts) == 1 else jnp.concatenate(parts, axis=0)


def _conv_taps(a_ext, r0_ext, cw_ref, cb_ref):
    n = a_ext.shape[0]
    if r0_ext < N_PAD:
        row = r0_ext + lax.broadcasted_iota(jnp.int32, (n, 1), 0)
        a_ext = jnp.where(row >= N_PAD, a_ext, 0.0)
    a1 = pltpu.roll(a_ext, 1, axis=0)
    a2 = pltpu.roll(a_ext, 2, axis=0)
    acc = cb_ref[...] + a2 * cw_ref[0:1, :] + a1 * cw_ref[1:2, :] + a_ext * cw_ref[2:3, :]
    return a_ext, a1, a2, acc


def _conv_gate_fwd(up, conv_w8, conv_b, name):
    _, t, f = up.shape
    rows = TOK_TILE

    def body(a_ref, b_ref, cw_ref, cb_ref, g_ref):
        for r0 in range(0, t, rows):
            a_ext = _rows_ext(a_ref, r0, rows, t, HALO, 0)
            _, _, _, acc = _conv_taps(a_ext, r0 - HALO, cw_ref, cb_ref)
            acc = acc[HALO:, :]
            g_ref[r0:r0 + rows, :] = (acc * _sigmoid(acc) * b_ref[r0:r0 + rows, :]).astype(BF16)

    return _pcall(
        body, name=name, grid=(f // LANE,),
        in_specs=[pl.BlockSpec((None, t, LANE), lambda j: (0, 0, j)), pl.BlockSpec((None, t, LANE), lambda j: (1, 0, j)),
                  pl.BlockSpec((8, LANE), lambda j: (0, j)), pl.BlockSpec((1, LANE), lambda j: (0, j))],
        out_specs=pl.BlockSpec((t, LANE), lambda j: (0, j)),
        out_shape=jax.ShapeDtypeStruct((t, f), BF16),
        compiler_params=_params("parallel"),
    )(up, up, conv_w8, conv_b)


def _conv_gate_bwd(up, conv_w8, conv_b, dg, name):
    _, t, f = up.shape
    rows = TOK_TILE

    def body(a_ref, b_ref, cw_ref, cb_ref, dg_ref, dup_ref, gcw_ref, gcb_ref):
        gw = [jnp.zeros((1, LANE), F32) for _ in range(3)]
        gb = jnp.zeros((1, LANE), F32)
        for r0 in range(0, t, rows):
            a_ext = _rows_ext(a_ref, r0, rows, t, HALO, HALO)
            b_ext = _rows_ext(b_ref, r0, rows, t, HALO, HALO)
            dg_ext = _rows_ext(dg_ref, r0, rows, t, HALO, HALO)
            a0, a1, a2, acc = _conv_taps(a_ext, r0 - HALO, cw_ref, cb_ref)
            sg = _sigmoid(acc)
            dacc = dg_ext * b_ext * (sg * (1.0 + acc * (1.0 - sg)))
            n = dacc.shape[0]
            da = (dacc * cw_ref[2:3, :] + pltpu.roll(dacc, n - 1, axis=0) * cw_ref[1:2, :]
                  + pltpu.roll(dacc, n - 2, axis=0) * cw_ref[0:1, :])
            core = slice(HALO, HALO + rows)
            da = da[core, :]
            if r0 < N_PAD:
                row = r0 + lax.broadcasted_iota(jnp.int32, (rows, 1), 0)
                da = jnp.where(row >= N_PAD, da, 0.0)
            dup_ref[0, r0:r0 + rows, :] = da.astype(BF16)
            dup_ref[1, r0:r0 + rows, :] = (dg_ext * acc * sg)[core, :].astype(BF16)
            dacc_c = dacc[core, :]
            gw[0] = gw[0] + jnp.sum(dacc_c * a2[core, :], axis=0, keepdims=True)
            gw[1] = gw[1] + jnp.sum(dacc_c * a1[core, :], axis=0, keepdims=True)
            gw[2] = gw[2] + jnp.sum(dacc_c * a0[core, :], axis=0, keepdims=True)
            gb = gb + jnp.sum(dacc_c, axis=0, keepdims=True)
        gcw_ref[...] = jnp.zeros((8, LANE), F32)
        for tap in range(3):
            gcw_ref[tap:tap + 1, :] = gw[tap]
        gcb_ref[...] = gb

    return _pcall(
        body, name=name, grid=(f // LANE,),
        in_specs=[pl.BlockSpec((None, t, LANE), lambda j: (0, 0, j)), pl.BlockSpec((None, t, LANE), lambda j: (1, 0, j)),
                  pl.BlockSpec((8, LANE), lambda j: (0, j)), pl.BlockSpec((1, LANE), lambda j: (0, j)),
                  pl.BlockSpec((t, LANE), lambda j: (0, j))],
        out_specs=[pl.BlockSpec((2, t, LANE), lambda j: (0, 0, j)), pl.BlockSpec((8, LANE), lambda j: (0, j)),
                   pl.BlockSpec((1, LANE), lambda j: (0, j))],
        out_shape=[jax.ShapeDtypeStruct((2, t, f), BF16), jax.ShapeDtypeStruct((8, f), F32),
                   jax.ShapeDtypeStruct((1, f), F32)],
        compiler_params=_params("parallel"),
    )(up, up, conv_w8, conv_b, dg)


def _exchange(arrays, kinds, name):
    n = len(arrays)
    npeer = N_DEV - 1

    def body(*refs):
        ins, outs = refs[:n], refs[n:2 * n]
        send_sems, recv_sems, local_sems = refs[2 * n:]
        x, y, c = lax.axis_index("x"), lax.axis_index("y"), lax.axis_index("c")
        me = 4 * x + 2 * y + c
        copies, locals_ = [], []
        for a in range(n):
            gather = kinds[a] == "gather"
            own = pltpu.make_async_copy(ins[a] if gather else ins[a].at[me], outs[a].at[me], local_sems.at[a])
            own.start()
            locals_.append(own)
            for d in range(1, N_DEV):
                px = 1 - x if d & 4 else x
                py = 1 - y if d & 2 else y
                pc = 1 - c if d & 1 else c
                src = ins[a] if gather else ins[a].at[4 * px + 2 * py + pc]
                cp = pltpu.make_async_remote_copy(
                    src_ref=src, dst_ref=outs[a].at[me],
                    send_sem=send_sems.at[a * npeer + d - 1], recv_sem=recv_sems.at[a * npeer + d - 1],
                    device_id=(px, py, pc), device_id_type=pl.DeviceIdType.MESH)
                cp.start()
                copies.append(cp)
        for cp in copies:
            cp.wait_recv()
        for cp in copies:
            cp.wait_send()
        for own in locals_:
            own.wait()

    out_shape = [jax.ShapeDtypeStruct((N_DEV,) + (a.shape if k == "gather" else a.shape[1:]), a.dtype)
                 for a, k in zip(arrays, kinds)]
    return _pcall(
        body, name=name,
        in_specs=[pl.BlockSpec(memory_space=pl.ANY)] * n,
        out_specs=[pl.BlockSpec(memory_space=pl.ANY)] * n,
        out_shape=out_shape,
        scratch_shapes=[pltpu.SemaphoreType.DMA((n * npeer,)), pltpu.SemaphoreType.DMA((n * npeer,)),
                        pltpu.SemaphoreType.DMA((n,))],
        compiler_params=pltpu.CompilerParams(has_side_effects=True),
    )(*arrays)


def _sum_slots(slots, name, rows_tile):
    nd, r, c = slots.shape

    def body(s_ref, o_ref):
        acc = s_ref[0].astype(F32)
        for p in range(1, nd):
            acc = acc + s_ref[p].astype(F32)
        o_ref[...] = acc

    return _pcall(
        body, name=name, grid=(r // rows_tile,),
        in_specs=[pl.BlockSpec((nd, rows_tile, c), lambda i: (0, i, 0))],
        out_specs=pl.BlockSpec((rows_tile, c), lambda i: (i, 0)),
        out_shape=jax.ShapeDtypeStruct((r, c), F32),
        compiler_params=_params("parallel"),
    )(slots)


def _adamw(w, g, m, v, name, rows_tile):
    r, c = w.shape

    def body(w_ref, g_ref, m_ref, v_ref, d_ref, nm_ref, nv_ref):
        gr = g_ref[...]
        nm = ADAM_B1 * m_ref[...] + (1.0 - ADAM_B1) * gr
        nv = ADAM_B2 * v_ref[...] + (1.0 - ADAM_B2) * (gr * gr)
        m_hat = nm / (1.0 - ADAM_B1 ** ADAM_STEP)
        v_hat = nv / (1.0 - ADAM_B2 ** ADAM_STEP)
        d_ref[...] = -ADAM_LR * (m_hat / (jnp.sqrt(v_hat) + ADAM_EPS) + ADAM_WD * w_ref[...])
        nm_ref[...] = nm
        nv_ref[...] = nv

    spec = pl.BlockSpec((rows_tile, c), lambda i: (i, 0))
    shp = jax.ShapeDtypeStruct((r, c), F32)
    return _pcall(
        body, name=name, grid=(r // rows_tile,), in_specs=[spec] * 4, out_specs=[spec] * 3, out_shape=[shp] * 3,
        compiler_params=_params("parallel"),
    )(w, g, m, v)


F0 = 2 * RET_QK + 2 * RET_V


def _to_internal_rows(w_t):
    parts = [w_t[:F0]]
    for pair in range(FOX_PAIRS):
        for group in range(3):
            lo = F0 + group * FOX_W + LANE * pair
            parts.append(w_t[lo:lo + LANE])
    parts.append(w_t[F0 + 3 * FOX_W:])
    parts.append(jnp.zeros((IN_PAD - IN_WIDTH, w_t.shape[1]), w_t.dtype))
    return jnp.concatenate(parts, axis=0)


def _from_internal_rows(g_t):
    parts = [g_t[:F0]]
    for group in range(3):
        for pair in range(FOX_PAIRS):
            lo = F0 + 3 * LANE * pair + LANE * group
            parts.append(g_t[lo:lo + LANE])
    parts.append(g_t[F0 + 3 * FOX_W:F0 + 3 * FOX_W + FOX_HEADS])
    return jnp.concatenate(parts, axis=0)


def _local_step(x, target, meta, attn_g, w_in_t, fox_b, ret_g, w_out, ffn_g, w_up_t, conv_w8, conv_b, w_down, final_g):
    seq, d = x.shape
    t = seq + PREFIX
    tm = TOK_TILE
    nq = t // tm
    fox_b128 = jnp.pad(fox_b, ((0, 0), (0, LANE - FOX_HEADS)))

    h0, n1 = _prep_norm(x, meta, attn_g, "prep_norm")
    proj = _mm_simple(n1, w_in_t, mode="nt", tm=tm, tn=640, tk=d, out_dtype=F32, name="mm_in")
    cos, sin = _rope_tables(t)
    o_pre, mixed, states = _ret_fwd(proj, cos, sin, ret_g, "ret_fwd")
    c = _forget_cumsum(proj, fox_b128, "forget_cumsum")
    qa, ka, va = _fox_prep(proj, c, "fox_prep")
    by_block = lambda a: a.reshape(FOX_HEADS, nq, tm, LANE)
    mixed, o_fox, lse = _fox_fwd(by_block(qa), by_block(ka), by_block(va), mixed, "fox_fwd")
    h1 = _mm_simple(mixed, w_out, mode="nn", tm=tm, tn=d, tk=d, out_dtype=F32, name="mm_out", add=h0)
    n2 = _rmsnorm(h1, ffn_g, "ffn_norm")
    nf = D_FF // 1408
    up = _matmul(
        n2, w_up_t, mode="nt", grid=(nq, 2 * nf, 1),
        a_spec=pl.BlockSpec((tm, d), lambda i, j, k: (i, 0)),
        b_spec=pl.BlockSpec((None, 1408, d), lambda i, j, k: (j // nf, j % nf, 0)),
        o_spec=pl.BlockSpec((None, tm, 1408), lambda i, j, k: (j // nf, i, j % nf)),
        out_shape=jax.ShapeDtypeStruct((2, t, D_FF), F32), name="mm_up")
    g = _conv_gate_fwd(up, conv_w8, conv_b, "conv_gate_fwd")
    h2 = _mm_simple(g, w_down, mode="nn", tm=tm, tn=d, tk=D_FF, out_dtype=F32, name="mm_down", add=h1)

    loss_tile, dh2, g_final = _loss_bwd(h2, target, final_g, "loss_bwd")
    dg = _mm_simple(dh2, w_down, mode="nt", tm=tm, tn=1408, tk=d, out_dtype=F32, name="mm_dg")
    gw_down = _mm_simple(g, dh2, mode="tn", tm=1408, tn=d, tk=tm, out_dtype=BF16, name="mm_gw_down")
    dup, g_conv_w8, g_conv_b = _conv_gate_bwd(up, conv_w8, conv_b, dg, "conv_gate_bwd")
    dn2 = _matmul(
        dup, w_up_t, mode="nn", grid=(nq, 1, 2 * nf),
        a_spec=pl.BlockSpec((None, tm, 1408), lambda i, j, k: (k // nf, i, k % nf)),
        b_spec=pl.BlockSpec((None, 1408, d), lambda i, j, k: (k // nf, k % nf, 0)),
        o_spec=pl.BlockSpec((tm, d), lambda i, j, k: (i, 0)),
        out_shape=jax.ShapeDtypeStruct((t, d), F32), name="mm_dn2")
    gw_up_t = _matmul(
        dup, n2, mode="tn", grid=(2 * nf, 1, nq),
        a_spec=pl.BlockSpec((None, tm, 1408), lambda i, j, k: (i // nf, k, i % nf)),
        b_spec=pl.BlockSpec((tm, d), lambda i, j, k: (k, 0)),
        o_spec=pl.BlockSpec((1408, d), lambda i, j, k: (i, 0)),
        out_shape=jax.ShapeDtypeStruct((2 * D_FF, d), BF16), name="mm_gw_up")
    dh1, g_ffn = _rmsnorm_bwd(dn2, h1, ffn_g, dh2, "ffn_norm_bwd")

    dmixed = _mm_simple(dh1, w_out, mode="nt", tm=tm, tn=d, tk=d, out_dtype=F32, name="mm_dmixed")
    gw_out = _mm_simple(mixed, dh1, mode="tn", tm=512, tn=d, tk=tm, out_dtype=BF16, name="mm_gw_out")
    dproj, g_ret = _ret_bwd(proj, cos, sin, ret_g, dmixed, o_pre, states, "ret_bwd")
    qab, doa = _fox_prep_bwd(dmixed, o_fox, lse, qa, "fox_prep_bwd")
    dproj, drs, dcs = _fox_bwd(by_block(qab), by_block(doa), by_block(ka), by_block(va), dproj, "fox_bwd")
    dproj, g_fox_b = _forget_cumsum_bwd(proj, fox_b128, drs, dcs, dproj, "forget_cumsum_bwd")
    dn1 = _mm_simple(dproj, w_in_t, mode="nn", tm=tm, tn=d, tk=640, out_dtype=F32, name="mm_dn1")
    gw_in_t = _mm_simple(dproj, n1, mode="tn", tm=640, tn=d, tk=tm, out_dtype=BF16, name="mm_gw_in")
    dh0, g_attn = _rmsnorm_bwd(dn1, h0, attn_g, dh1, "attn_norm_bwd")

    grads = dict(meta=dh0[N_PAD:PREFIX], attn_g=g_attn, w_in_t=gw_in_t, fox_b=g_fox_b[:, :FOX_HEADS], ret_g=g_ret,
                 w_out=gw_out, ffn_g=g_ffn, w_up_t=gw_up_t, conv_w=g_conv_w8[:3], conv_b=g_conv_b, w_down=gw_down,
                 final_g=g_final)
    return loss_tile, dh0[PREFIX:], grads


def _pack(pieces, width=LANE):
    rows = []
    for p in pieces:
        flat = p.reshape(-1)
        pad = (-flat.shape[0]) % width
        rows.append(jnp.pad(flat, (0, pad)).reshape(-1, width))
    out = jnp.concatenate(rows, axis=0)
    return jnp.pad(out, ((0, (-out.shape[0]) % 8), (0, 0)))


def _unpack(packed, shapes, width=LANE):
    outs, r = [], 0
    for shp in shapes:
        size = int(np.prod(shp))
        nrows = -(-size // width)
        outs.append(packed[r:r + nrows].reshape(-1)[:size].reshape(shp))
        r += nrows
    return outs


def kernel(x, meta_tokens, attn_norm_g, w_in, fox_forget_b, ret_norm_g, w_out, ffn_norm_g, w_up, conv_w, conv_b, w_down, final_norm_g, loss_target, m_meta_tokens, m_attn_norm_g, m_w_in, m_fox_forget_b, m_ret_norm_g, m_w_out, m_ffn_norm_g, m_w_up, m_conv_w, m_conv_b, m_w_down, m_final_norm_g, v_meta_tokens, v_attn_norm_g, v_w_in, v_fox_forget_b, v_ret_norm_g, v_w_out, v_ffn_norm_g, v_w_up, v_conv_w, v_conv_b, v_w_down, v_final_norm_g):
    d = D_MODEL
    me = 4 * lax.axis_index("x") + 2 * lax.axis_index("y") + lax.axis_index("c")
    in_blk = IN_WIDTH // N_DEV
    in_blk_pad = 400
    up_blk = 2 * D_FF // N_DEV
    down_blk = D_FF // N_DEV
    cw_blk = D_FF // N_DEV

    w_in_loc = jnp.pad(w_in[0].T.astype(BF16), ((0, in_blk_pad - in_blk), (0, 0)))
    cw_loc = jnp.pad(conv_w[0], ((0, 5), (0, 384 - cw_blk)))
    g_in, g_out, g_up, g_down, g_meta, g_cw = _exchange(
        [w_in_loc, w_out[0].astype(BF16), w_up[0].T.astype(BF16), w_down[0].astype(BF16), meta_tokens, cw_loc],
        ["gather"] * 6, "gather_weights")
    w_in_t = _to_internal_rows(g_in[:, :in_blk].reshape(IN_WIDTH, d))
    w_out_f = g_out.reshape(d, d)
    w_up_t = g_up.reshape(2, D_FF, d)
    w_down_f = g_down.reshape(D_FF, d)
    meta_f = g_meta.transpose(1, 0, 2).reshape(N_META, d)
    conv_w8 = jnp.pad(g_cw[:, :3, :cw_blk].transpose(1, 0, 2).reshape(3, D_FF), ((0, 5), (0, 0)))

    loss_tile, grad_x, gr = _local_step(
        x[0], loss_target[0], meta_f, attn_norm_g, w_in_t, fox_forget_b, ret_norm_g, w_out_f, ffn_norm_g,
        w_up_t, conv_w8, conv_b, w_down_f, final_norm_g.reshape(1, d))

    s_in = jnp.pad(_from_internal_rows(gr["w_in_t"]).reshape(N_DEV, in_blk, d), ((0, 0), (0, in_blk_pad - in_blk), (0, 0)))
    small_shapes = [(1, LANE), (1, d), (1, FOX_HEADS), (1, RET_V), (1, d), (1, D_FF), (1, d), (N_META, d), (3, D_FF)]
    small = _pack([loss_tile[0:1], gr["attn_g"], gr["fox_b"], gr["ret_g"], gr["ffn_g"], gr["conv_b"], gr["final_g"],
                   gr["meta"], gr["conv_w"]])
    r_in, r_out, r_up, r_down, r_small = _exchange(
        [s_in, gr["w_out"].reshape(N_DEV, d // N_DEV, d), gr["w_up_t"].reshape(N_DEV, up_blk, d),
         gr["w_down"].reshape(N_DEV, down_blk, d), small],
        ["scatter"] * 4 + ["gather"], "exchange_grads")
    g_w_in = _sum_slots(r_in, "sum_w_in", in_blk_pad)[:in_blk].T
    g_w_out = _sum_slots(r_out, "sum_w_out", d // N_DEV)
    g_w_up = _sum_slots(r_up, "sum_w_up", up_blk).T
    g_w_down = _sum_slots(r_down, "sum_w_down", down_blk)
    s_all = _sum_slots(r_small, "sum_small", r_small.shape[1])
    (loss_row, g_attn, g_fox_b, g_ret, g_ffn, g_conv_b, g_final, g_meta_full, g_cw_full) = _unpack(s_all, small_shapes)
    loss = loss_row[0, 0]
    g_meta_loc = lax.dynamic_slice(g_meta_full, (0, me * (d // N_DEV)), (N_META, d // N_DEV))
    g_cw_loc = lax.dynamic_slice(g_cw_full, (0, me * cw_blk), (3, cw_blk))

    d_w_in, m_w_in_n, v_w_in_n = _adamw(w_in[0], g_w_in, m_w_in[0], v_w_in[0], "adamw_w_in", 128)
    d_w_out, m_w_out_n, v_w_out_n = _adamw(w_out[0], g_w_out, m_w_out[0], v_w_out[0], "adamw_w_out", 128)
    d_w_up, m_w_up_n, v_w_up_n = _adamw(w_up[0], g_w_up, m_w_up[0], v_w_up[0], "adamw_w_up", 128)
    d_w_down, m_w_down_n, v_w_down_n = _adamw(w_down[0], g_w_down, m_w_down[0], v_w_down[0], "adamw_w_down", down_blk)
    sm_grads = [g_meta_loc, g_attn, g_fox_b, g_ret, g_ffn, g_cw_loc, g_conv_b, g_final.reshape(d)]
    sm_w = [meta_tokens, attn_norm_g, fox_forget_b, ret_norm_g, ffn_norm_g, conv_w[0], conv_b, final_norm_g]
    sm_m = [m_meta_tokens, m_attn_norm_g, m_fox_forget_b, m_ret_norm_g, m_ffn_norm_g, m_conv_w[0], m_conv_b, m_final_norm_g]
    sm_v = [v_meta_tokens, v_attn_norm_g, v_fox_forget_b, v_ret_norm_g, v_ffn_norm_g, v_conv_w[0], v_conv_b, v_final_norm_g]
    sm_shapes = [a.shape for a in sm_w]
    pk = [_pack(lst) for lst in (sm_w, sm_grads, sm_m, sm_v)]
    sm_d, sm_nm, sm_nv = _adamw(pk[0], pk[1], pk[2], pk[3], "adamw_small", pk[0].shape[0])
    dl = _unpack(sm_d, sm_shapes)
    ml = _unpack(sm_nm, sm_shapes)
    vl = _unpack(sm_nv, sm_shapes)

    def by_weight(meta_, attn_, w_in_, fox_, ret_, w_out_, ffn_, w_up_, cw_, cb_, w_down_, final_):
        return (meta_, attn_, w_in_[None], fox_, ret_, w_out_[None], ffn_, w_up_[None], cw_[None], cb_, w_down_[None], final_)

    grads_out = by_weight(g_meta_loc, g_attn, g_w_in, g_fox_b, g_ret, g_w_out, g_ffn, g_w_up, g_cw_loc, g_conv_b,
                          g_w_down, g_final.reshape(d))
    delta_out = by_weight(dl[0], dl[1], d_w_in, dl[2], dl[3], d_w_out, dl[4], d_w_up, dl[5], dl[6], d_w_down, dl[7])
    m_out = by_weight(ml[0], ml[1], m_w_in_n, ml[2], ml[3], m_w_out_n, ml[4], m_w_up_n, ml[5], ml[6], m_w_down_n, ml[7])
    v_out = by_weight(vl[0], vl[1], v_w_in_n, vl[2], vl[3], v_w_out_n, vl[4], v_w_up_n, vl[5], vl[6], v_w_down_n, vl[7])
    return (loss, grad_x[None]) + grads_out + delta_out + m_out + v_out
```

```python
import numpy as np
import jax
import jax.numpy as jnp
from jax import lax
from jax.experimental import pallas as pl
from jax.experimental.pallas import tpu as pltpu

F32 = jnp.float32
BF16 = jnp.bfloat16

D_MODEL = 1024
N_META = 16
N_PAD = 112
PREFIX = 128
RET_HEADS = 4
RET_DK = 64
RET_DV = 128
FOX_HEADS = 8
FOX_DH = 64
D_FF = 2816
ROPE_BASE = 10000.0
EPS = 1e-6
NEG = -1e30
RET_QK = RET_HEADS * RET_DK
RET_V = RET_HEADS * RET_DV
FOX_W = FOX_HEADS * FOX_DH
IN_WIDTH = 2 * RET_QK + 2 * RET_V + 3 * FOX_W + FOX_HEADS
IN_PAD = 3200
FF_COL_BLOCK = (IN_WIDTH - FOX_HEADS) // 128
QK_SCALE = 0.125

ADAM_LR = 0.001
ADAM_B1 = 0.9
ADAM_B2 = 0.999
ADAM_EPS = 1e-08
ADAM_WD = 0.01
ADAM_STEP = 10

N_DEV = 8
LANE = 128
ROW_TILE = 128
TOK_TILE = 384

NN = (((1,), (0,)), ((), ()))
NT = (((1,), (1,)), ((), ()))
TN = (((0,), (0,)), ((), ()))


def _pcall(body, **kw):
    return pl.pallas_call(body, **kw)


def _params(*sem):
    return pltpu.CompilerParams(dimension_semantics=sem)


def _dot(a, b, dims=NN):
    return lax.dot_general(a, b, dims, preferred_element_type=F32)


def _sigmoid(x):
    return 1.0 / (1.0 + jnp.exp(-x))


def _matmul(a, b, *, mode, grid, a_spec, b_spec, o_spec, out_shape, name, add=None, add_spec=None):
    dims = {"nn": NN, "nt": NT, "tn": TN}[mode]
    nk = grid[2]
    has_add = add is not None

    def body(*refs):
        if has_add:
            a_ref, b_ref, add_ref, o_ref = refs[:4]
        else:
            a_ref, b_ref, o_ref = refs[:3]
        part = _dot(a_ref[...].astype(BF16), b_ref[...].astype(BF16), dims)

        def finish(acc):
            if has_add:
                acc = acc + add_ref[...]
            o_ref[...] = acc.astype(o_ref.dtype)

        if nk == 1:
            finish(part)
        else:
            acc_ref = refs[-1]
            k = pl.program_id(2)

            @pl.when(k == 0)
            def _():
                acc_ref[...] = part

            @pl.when(k > 0)
            def _():
                acc_ref[...] += part

            @pl.when(k == nk - 1)
            def _():
                finish(acc_ref[...])

    in_specs = [a_spec, b_spec] + ([add_spec] if has_add else [])
    args = (a, b) + ((add,) if has_add else ())
    scratch = [] if nk == 1 else [pltpu.VMEM(tuple(d for d in o_spec.block_shape if d is not None), F32)]
    return _pcall(
        body, name=name, grid=grid, in_specs=in_specs, out_specs=o_spec, out_shape=out_shape,
        scratch_shapes=scratch, compiler_params=_params("parallel", "parallel", "arbitrary"),
    )(*args)


def _mm_simple(a, b, *, mode, tm, tn, tk, out_dtype, name, add=None):
    if mode == "tn":
        K, M = a.shape
    else:
        M, K = a.shape
    N = b.shape[0] if mode == "nt" else b.shape[1]
    grid = (M // tm, N // tn, K // tk)
    a_spec = pl.BlockSpec((tk, tm), lambda i, j, k: (k, i)) if mode == "tn" else pl.BlockSpec((tm, tk), lambda i, j, k: (i, k))
    b_spec = pl.BlockSpec((tn, tk), lambda i, j, k: (j, k)) if mode == "nt" else pl.BlockSpec((tk, tn), lambda i, j, k: (k, j))
    o_spec = pl.BlockSpec((tm, tn), lambda i, j, k: (i, j))
    return _matmul(a, b, mode=mode, grid=grid, a_spec=a_spec, b_spec=b_spec, o_spec=o_spec,
                   out_shape=jax.ShapeDtypeStruct((M, N), out_dtype), name=name, add=add,
                   add_spec=o_spec if add is not None else None)


def _prep_norm(x, meta, gain, name):
    seq, d = x.shape
    t = seq + PREFIX
    nb = t // ROW_TILE

    def body(x_ref, meta_ref, g_ref, h_ref, n_ref):
        i = pl.program_id(0)

        @pl.when(i == 0)
        def _():
            h_ref[0:N_PAD, :] = jnp.zeros((N_PAD, d), F32)
            h_ref[N_PAD:ROW_TILE, :] = meta_ref[...]

        @pl.when(i > 0)
        def _():
            h_ref[...] = x_ref[...]

        h = h_ref[...]
        r = lax.rsqrt(jnp.mean(h * h, axis=-1, keepdims=True) + EPS)
        n_ref[...] = (h * r * g_ref[...]).astype(BF16)

    return _pcall(
        body, name=name, grid=(nb,),
        in_specs=[pl.BlockSpec((ROW_TILE, d), lambda i: (jnp.maximum(i - 1, 0), 0)),
                  pl.BlockSpec((N_META, d), lambda i: (0, 0)),
                  pl.BlockSpec((1, d), lambda i: (0, 0))],
        out_specs=[pl.BlockSpec((ROW_TILE, d), lambda i: (i, 0)), pl.BlockSpec((ROW_TILE, d), lambda i: (i, 0))],
        out_shape=[jax.ShapeDtypeStruct((t, d), F32), jax.ShapeDtypeStruct((t, d), BF16)],
        compiler_params=_params("parallel"),
    )(x, meta, gain)


def _rmsnorm(h, gain, name):
    t, d = h.shape

    def body(h_ref, g_ref, n_ref):
        x = h_ref[...]
        r = lax.rsqrt(jnp.mean(x * x, axis=-1, keepdims=True) + EPS)
        n_ref[...] = (x * r * g_ref[...]).astype(BF16)

    return _pcall(
        body, name=name, grid=(t // TOK_TILE,),
        in_specs=[pl.BlockSpec((TOK_TILE, d), lambda i: (i, 0)), pl.BlockSpec((1, d), lambda i: (0, 0))],
        out_specs=pl.BlockSpec((TOK_TILE, d), lambda i: (i, 0)),
        out_shape=jax.ShapeDtypeStruct((t, d), BF16),
        compiler_params=_params("parallel"),
    )(h, gain)


def _rmsnorm_bwd(dn, h, gain, dres, name):
    t, d = h.shape

    def body(dn_ref, h_ref, g_ref, dres_ref, dh_ref, gg_ref):
        i = pl.program_id(0)
        x = h_ref[...]
        r = lax.rsqrt(jnp.mean(x * x, axis=-1, keepdims=True) + EPS)
        xhat = x * r
        dy = dn_ref[...]
        u = dy * g_ref[...]
        dh_ref[...] = dres_ref[...] + r * (u - xhat * jnp.mean(u * xhat, axis=-1, keepdims=True))
        part = jnp.sum(dy * xhat, axis=0, keepdims=True)

        @pl.when(i == 0)
        def _():
            gg_ref[...] = part

        @pl.when(i > 0)
        def _():
            gg_ref[...] += part

    return _pcall(
        body, name=name, grid=(t // TOK_TILE,),
        in_specs=[pl.BlockSpec((TOK_TILE, d), lambda i: (i, 0)), pl.BlockSpec((TOK_TILE, d), lambda i: (i, 0)),
                  pl.BlockSpec((1, d), lambda i: (0, 0)), pl.BlockSpec((TOK_TILE, d), lambda i: (i, 0))],
        out_specs=[pl.BlockSpec((TOK_TILE, d), lambda i: (i, 0)), pl.BlockSpec((1, d), lambda i: (0, 0))],
        out_shape=[jax.ShapeDtypeStruct((t, d), F32), jax.ShapeDtypeStruct((1, d), F32)],
        compiler_params=_params("arbitrary"),
    )(dn, h, gain, dres)


def _loss_bwd(h2, target, gain, name):
    t, d = h2.shape
    nb = t // ROW_TILE

    def body(h_ref, tgt_ref, g_ref, loss_ref, dh_ref, gg_ref):
        i = pl.program_id(0)

        @pl.when(i == 0)
        def _():
            loss_ref[...] = jnp.zeros_like(loss_ref)
            gg_ref[...] = jnp.zeros_like(gg_ref)
            dh_ref[...] = jnp.zeros_like(dh_ref)

        @pl.when(i > 0)
        def _():
            x = h_ref[...]
            r = lax.rsqrt(jnp.mean(x * x, axis=-1, keepdims=True) + EPS)
            xhat = x * r
            g = g_ref[...]
            err = xhat * g - tgt_ref[...]
            loss_ref[...] += 0.5 * jnp.sum(jnp.mean(err * err, axis=-1, keepdims=True))
            dy = err * (1.0 / d)
            u = dy * g
            dh_ref[...] = r * (u - xhat * jnp.mean(u * xhat, axis=-1, keepdims=True))
            gg_ref[...] += jnp.sum(dy * xhat, axis=0, keepdims=True)

    return _pcall(
        body, name=name, grid=(nb,),
        in_specs=[pl.BlockSpec((ROW_TILE, d), lambda i: (i, 0)),
                  pl.BlockSpec((ROW_TILE, d), lambda i: (jnp.maximum(i - 1, 0), 0)),
                  pl.BlockSpec((1, d), lambda i: (0, 0))],
        out_specs=[pl.BlockSpec((8, LANE), lambda i: (0, 0)), pl.BlockSpec((ROW_TILE, d), lambda i: (i, 0)),
                   pl.BlockSpec((1, d), lambda i: (0, 0))],
        out_shape=[jax.ShapeDtypeStruct((8, LANE), F32), jax.ShapeDtypeStruct((t, d), F32),
                   jax.ShapeDtypeStruct((1, d), F32)],
        compiler_params=_params("arbitrary"),
    )(h2, target, gain)


def _ret_consts(bk):
    gam = 1.0 - 2.0 ** (-5.0 - np.arange(RET_HEADS))
    n = np.arange(bk)
    same_or_earlier_chunk = (n[None, :] // 64) <= (n[:, None] // 64)
    w = gam[:, None, None] ** np.abs(n[:, None] - n[None, :])[None] * same_or_earlier_chunk[None]
    wq = gam[:, None] ** (n[None, :] + 1.0)
    wk = gam[:, None] ** (bk - 1.0 - n[None, :])
    mask = (np.arange(RET_QK)[None, :] // RET_DK) == np.arange(RET_HEADS)[:, None]
    return (jnp.asarray(w, F32), jnp.asarray(wq[:, :, None], F32), jnp.asarray(wk[:, :, None], F32),
            jnp.asarray(mask[:, None, :], F32), [float(g ** bk) for g in gam])


def _rope_tables(t):
    half = RET_DK // 2
    inv = 1.0 / (ROPE_BASE ** (jnp.arange(half, dtype=F32) / half))
    ang = jnp.arange(t).astype(F32)[:, None] * inv[None, :]
    cos, sin = jnp.cos(ang), jnp.sin(ang)
    return (jnp.tile(jnp.concatenate([cos, cos], axis=1), (1, RET_HEADS)),
            jnp.tile(jnp.concatenate([-sin, sin], axis=1), (1, RET_HEADS)))


def _swap_halves(x):
    outs = []
    for s in range(x.shape[1] // LANE):
        xs = x[:, LANE * s:LANE * (s + 1)]
        lane = lax.broadcasted_iota(jnp.int32, xs.shape, 1)
        outs.append(jnp.where((lane & 32) == 0, pltpu.roll(xs, LANE - 32, axis=1), pltpu.roll(xs, 32, axis=1)))
    return outs[0] if len(outs) == 1 else jnp.concatenate(outs, axis=1)


def _rope(x, cos, sin_signed):
    return x * cos + _swap_halves(x) * sin_signed


def _rope_t(dx, cos, sin_signed):
    return dx * cos + _swap_halves(dx * sin_signed)


def _ret_fwd(proj, cos, sin, gain, name):
    t = proj.shape[0]
    bk = TOK_TILE
    nb = t // bk
    w, wq, wk, mask, g_blk = _ret_consts(bk)

    def body(q_ref, k_ref, v_ref, rg_ref, cos_ref, sin_ref, w_ref, wq_ref, wk_ref, mask_ref, gain_ref,
             opre_ref, og_ref, st_ref, r_ref):
        i = pl.program_id(0)

        @pl.when(i == 0)
        def _():
            r_ref[...] = jnp.zeros_like(r_ref)

        c, s = cos_ref[...], sin_ref[...]
        valid = ((i * bk + lax.broadcasted_iota(jnp.int32, (bk, 1), 0)) >= N_PAD).astype(F32)
        qr = _rope(q_ref[...], c, s)
        kr = _rope(k_ref[...], c, s) * QK_SCALE * valid
        kb = kr.astype(BF16)
        for h in range(RET_HEADS):
            hm = mask_ref[h]
            cols = slice(RET_DV * h, RET_DV * (h + 1))
            vh = v_ref[:, cols].astype(BF16)
            r_prev = r_ref[h]
            st_ref[0, h] = r_prev
            sm = _dot((qr * hm).astype(BF16), kb, NT) * w_ref[h]
            o = _dot(sm.astype(BF16), vh) + _dot((qr * (hm * wq_ref[h])).astype(BF16), r_prev.astype(BF16))
            r_ref[h] = g_blk[h] * r_prev + _dot((kr * wk_ref[h]).astype(BF16), vh, TN)
            opre_ref[:, cols] = o
            rstd = lax.rsqrt(jnp.mean(o * o, axis=-1, keepdims=True) + EPS)
            rg = rg_ref[:, cols]
            og_ref[:, cols] = (o * rstd * gain_ref[:, cols] * (rg * _sigmoid(rg))).astype(BF16)

    full = lambda shape: pl.BlockSpec(shape, lambda i: (0,) * len(shape))
    return _pcall(
        body, name=name, grid=(nb,),
        in_specs=[pl.BlockSpec((bk, RET_QK), lambda i: (i, 0)), pl.BlockSpec((bk, RET_QK), lambda i: (i, 1)),
                  pl.BlockSpec((bk, RET_V), lambda i: (i, 1)), pl.BlockSpec((bk, RET_V), lambda i: (i, 2)),
                  pl.BlockSpec((bk, RET_QK), lambda i: (i, 0)), pl.BlockSpec((bk, RET_QK), lambda i: (i, 0)),
                  full((RET_HEADS, bk, bk)), full((RET_HEADS, bk, 1)), full((RET_HEADS, bk, 1)),
                  full((RET_HEADS, 1, RET_QK)), full((1, RET_V))],
        out_specs=[pl.BlockSpec((bk, RET_V), lambda i: (i, 0)), pl.BlockSpec((bk, RET_V), lambda i: (i, 0)),
                   pl.BlockSpec((1, RET_HEADS, RET_QK, RET_DV), lambda i: (i, 0, 0, 0))],
        out_shape=[jax.ShapeDtypeStruct((t, RET_V), F32), jax.ShapeDtypeStruct((t, RET_V + FOX_W), BF16),
                   jax.ShapeDtypeStruct((nb, RET_HEADS, RET_QK, RET_DV), F32)],
        scratch_shapes=[pltpu.VMEM((RET_HEADS, RET_QK, RET_DV), F32)],
        compiler_params=_params("arbitrary"),
    )(proj, proj, proj, proj, cos, sin, w, wq, wk, mask, gain)


def _ret_bwd(proj, cos, sin, gain, dmixed, opre, states, name):
    t = proj.shape[0]
    bk = TOK_TILE
    nb = t // bk
    w, wq, wk, mask, g_blk = _ret_consts(bk)
    v0, g0 = 2 * RET_QK, 2 * RET_QK + RET_V

    def body(q_ref, k_ref, v_ref, rg_ref, cos_ref, sin_ref, w_ref, wq_ref, wk_ref, mask_ref, gain_ref,
             dog_ref, opre_ref, st_ref, dp_ref, gg_ref, dr_ref):
        step = pl.program_id(0)
        i = nb - 1 - step

        @pl.when(step == 0)
        def _():
            dr_ref[...] = jnp.zeros_like(dr_ref)
            gg_ref[...] = jnp.zeros_like(gg_ref)

        c, s = cos_ref[...], sin_ref[...]
        valid = ((i * bk + lax.broadcasted_iota(jnp.int32, (bk, 1), 0)) >= N_PAD).astype(F32)
        qr = _rope(q_ref[...], c, s)
        kr = _rope(k_ref[...], c, s) * QK_SCALE * valid
        kb = kr.astype(BF16)
        dqr = jnp.zeros((bk, RET_QK), F32)
        dkr = jnp.zeros((bk, RET_QK), F32)
        for h in range(RET_HEADS):
            hm = mask_ref[h]
            cols = slice(RET_DV * h, RET_DV * (h + 1))
            vh = v_ref[:, cols].astype(BF16)
            o = opre_ref[:, cols]
            rstd = lax.rsqrt(jnp.mean(o * o, axis=-1, keepdims=True) + EPS)
            xhat = o * rstd
            rg = rg_ref[:, cols]
            sg = _sigmoid(rg)
            gate = rg * sg
            gn = gain_ref[:, cols]
            dog = dog_ref[:, cols]
            dp_ref[:, g0 + RET_DV * h:g0 + RET_DV * (h + 1)] = (
                dog * xhat * gn * (sg * (1.0 + rg * (1.0 - sg)))).astype(BF16)
            gg_ref[:, cols] += jnp.sum(dog * xhat * gate, axis=0, keepdims=True)
            dxh = dog * gn * gate
            do = (rstd * (dxh - xhat * jnp.mean(dxh * xhat, axis=-1, keepdims=True))).astype(BF16)
            qm = (qr * hm).astype(BF16)
            qw = (qr * (hm * wq_ref[h])).astype(BF16)
            kw = (kr * wk_ref[h]).astype(BF16)
            wh = w_ref[h]
            sm = (_dot(qm, kb, NT) * wh).astype(BF16)
            ds = (_dot(do, vh, NT) * wh).astype(BF16)
            dr = dr_ref[h]
            drb = dr.astype(BF16)
            dp_ref[:, v0 + RET_DV * h:v0 + RET_DV * (h + 1)] = (_dot(sm, do, TN) + _dot(kw, drb)).astype(BF16)
            dqr = dqr + _dot(ds, kb) * hm + _dot(do, st_ref[0, h].astype(BF16), NT) * (hm * wq_ref[h])
            dkr = dkr + _dot(ds, qm, TN) + _dot(vh, drb, NT) * wk_ref[h]
            dr_ref[h] = g_blk[h] * dr + _dot(qw, do, TN)
        dp_ref[:, 0:RET_QK] = _rope_t(dqr, c, s).astype(BF16)
        dp_ref[:, RET_QK:2 * RET_QK] = _rope_t(dkr * (QK_SCALE * valid), c, s).astype(BF16)

    full = lambda shape: pl.BlockSpec(shape, lambda i: (0,) * len(shape))
    rev = lambda col: (lambda i: (nb - 1 - i, col))
    return _pcall(
        body, name=name, grid=(nb,),
        in_specs=[pl.BlockSpec((bk, RET_QK), rev(0)), pl.BlockSpec((bk, RET_QK), rev(1)),
                  pl.BlockSpec((bk, RET_V), rev(1)), pl.BlockSpec((bk, RET_V), rev(2)),
                  pl.BlockSpec((bk, RET_QK), rev(0)), pl.BlockSpec((bk, RET_QK), rev(0)),
                  full((RET_HEADS, bk, bk)), full((RET_HEADS, bk, 1)), full((RET_HEADS, bk, 1)),
                  full((RET_HEADS, 1, RET_QK)), full((1, RET_V)),
                  pl.BlockSpec((bk, RET_V), rev(0)), pl.BlockSpec((bk, RET_V), rev(0)),
                  pl.BlockSpec((1, RET_HEADS, RET_QK, RET_DV), lambda i: (nb - 1 - i, 0, 0, 0))],
        out_specs=[pl.BlockSpec((bk, g0 + RET_V), rev(0)), pl.BlockSpec((1, RET_V), lambda i: (0, 0))],
        out_shape=[jax.ShapeDtypeStruct((t, IN_PAD), BF16), jax.ShapeDtypeStruct((1, RET_V), F32)],
        scratch_shapes=[pltpu.VMEM((RET_HEADS, RET_QK, RET_DV), F32)],
        compiler_params=_params("arbitrary"),
    )(proj, proj, proj, proj, cos, sin, w, wq, wk, mask, gain, dmixed, opre, states)


def _forget_cumsum(proj, bias, name):
    t = proj.shape[0]
    nb = t // ROW_TILE
    tril = jnp.asarray(np.tril(np.ones((ROW_TILE, ROW_TILE))), F32)

    def body(z_ref, b_ref, tril_ref, c_ref, carry_ref):
        i = pl.program_id(0)

        @pl.when(i == 0)
        def _():
            carry_ref[...] = jnp.zeros_like(carry_ref)

        z = z_ref[...] + b_ref[...]
        logf = jnp.minimum(z, 0.0) - jnp.log(1.0 + jnp.exp(-jnp.abs(z)))
        c = lax.dot_general(tril_ref[...], logf, NN, precision=lax.Precision.HIGHEST,
                            preferred_element_type=F32) + carry_ref[...]
        c_ref[...] = c
        carry_ref[...] = c[ROW_TILE - 1:ROW_TILE, :]

    return _pcall(
        body, name=name, grid=(nb,),
        in_specs=[pl.BlockSpec((ROW_TILE, LANE), lambda i: (i, FF_COL_BLOCK)), pl.BlockSpec((1, LANE), lambda i: (0, 0)),
                  pl.BlockSpec((ROW_TILE, ROW_TILE), lambda i: (0, 0))],
        out_specs=pl.BlockSpec((ROW_TILE, LANE), lambda i: (i, 0)),
        out_shape=jax.ShapeDtypeStruct((t, LANE), F32),
        scratch_shapes=[pltpu.VMEM((1, LANE), F32)],
        compiler_params=_params("arbitrary"),
    )(proj, bias, tril)


def _forget_cumsum_bwd(proj, bias, drs, dcs, dproj, name):
    t = proj.shape[0]
    nb = t // ROW_TILE
    triu = jnp.asarray(np.triu(np.ones((ROW_TILE, ROW_TILE))), F32)

    def body(z_ref, b_ref, triu_ref, drs_ref, dcs_ref, dproj_in, dz_ref, gb_ref, carry_ref):
        step = pl.program_id(0)

        @pl.when(step == 0)
        def _():
            carry_ref[...] = jnp.zeros_like(carry_ref)
            gb_ref[...] = jnp.zeros_like(gb_ref)

        dlogf = lax.dot_general(triu_ref[...], drs_ref[...] - dcs_ref[...], NN, precision=lax.Precision.HIGHEST,
                                preferred_element_type=F32) + carry_ref[...]
        carry_ref[...] = dlogf[0:1, :]
        z = z_ref[...] + b_ref[...]
        is_head = lax.broadcasted_iota(jnp.int32, (ROW_TILE, LANE), 1) < FOX_HEADS
        dz = jnp.where(is_head, dlogf / (1.0 + jnp.exp(z)), 0.0)
        dz_ref[...] = dz.astype(BF16)
        gb_ref[...] += jnp.sum(dz, axis=0, keepdims=True)

    return _pcall(
        body, name=name, grid=(nb,),
        in_specs=[pl.BlockSpec((ROW_TILE, LANE), lambda i: (nb - 1 - i, FF_COL_BLOCK)),
                  pl.BlockSpec((1, LANE), lambda i: (0, 0)),
                  pl.BlockSpec((ROW_TILE, ROW_TILE), lambda i: (0, 0)),
                  pl.BlockSpec((ROW_TILE, LANE), lambda i: (nb - 1 - i, 0)),
                  pl.BlockSpec((ROW_TILE, LANE), lambda i: (nb - 1 - i, 0)),
                  pl.BlockSpec(memory_space=pl.ANY)],
        out_specs=[pl.BlockSpec((ROW_TILE, LANE), lambda i: (nb - 1 - i, FF_COL_BLOCK)),
                   pl.BlockSpec((1, LANE), lambda i: (0, 0))],
        out_shape=[jax.ShapeDtypeStruct(dproj.shape, BF16), jax.ShapeDtypeStruct((1, LANE), F32)],
        input_output_aliases={5: 0},
        scratch_shapes=[pltpu.VMEM((1, LANE), F32)],
        compiler_params=_params("arbitrary"),
    )(proj, bias, triu, drs, dcs, dproj)


FOX_PAIRS = FOX_HEADS // 2
L_ONE_Q = FOX_DH
L_ONE_K = FOX_DH + 3
L_LSE = FOX_DH + 4


def _split3(x):
    hi = x.astype(BF16).astype(F32)
    r = x - hi
    mid = r.astype(BF16).astype(F32)
    return hi, mid, r - mid


def _head_to_low(slab, e):
    return slab if e == 0 else pltpu.roll(slab, FOX_DH, axis=1)


def _pair(a, b, low):
    return jnp.where(low, a, pltpu.roll(b, FOX_DH, axis=1))


def _fox_prep(proj, c, name):
    t = proj.shape[0]
    tq = TOK_TILE

    def body(p_ref, c_ref, qa_ref, ka_ref, va_ref):
        i = pl.program_id(0)
        lane = lax.broadcasted_iota(jnp.int32, (tq, LANE), 1)
        low = lane < FOX_DH
        live = (i * tq + lax.broadcasted_iota(jnp.int32, (tq, 1), 0)) >= N_PAD
        q_tail = jnp.where(lane < L_ONE_Q + 3, 1.0, 0.0)
        k_ones = (lane >= L_ONE_K) & (lane < L_ONE_K + 4)
        v_tail = jnp.where(lane < FOX_DH + 2, 1.0, 0.0)
        for pair in range(FOX_PAIRS):
            base = 3 * LANE * pair
            for e in range(2):
                h = 2 * pair + e
                q = _head_to_low(p_ref[:, base:base + LANE], e)
                k = _head_to_low(p_ref[:, base + LANE:base + 2 * LANE], e)
                v = _head_to_low(p_ref[:, base + 2 * LANE:base + 3 * LANE], e)
                hi, mid, lo = _split3(jnp.where(live, -c_ref[:, h:h + 1], NEG))
                ka = jnp.where(low, k, jnp.where(k_ones, 1.0, 0.0))
                ka = jnp.where(lane == L_ONE_Q, hi, jnp.where(lane == L_ONE_Q + 1, mid, jnp.where(lane == L_ONE_Q + 2, lo, ka)))
                qa_ref[h] = jnp.where(low, q * QK_SCALE, q_tail).astype(BF16)
                ka_ref[h] = ka.astype(BF16)
                va_ref[h] = jnp.where(low, v, v_tail).astype(BF16)

    out = jax.ShapeDtypeStruct((FOX_HEADS, t, LANE), BF16)
    ospec = pl.BlockSpec((FOX_HEADS, tq, LANE), lambda i: (0, i, 0))
    return _pcall(
        body, name=name, grid=(t // tq,),
        in_specs=[pl.BlockSpec((tq, 3 * FOX_W), lambda i: (i, 1)), pl.BlockSpec((tq, LANE), lambda i: (i, 0))],
        out_specs=[ospec, ospec, ospec], out_shape=[out, out, out],
        compiler_params=_params("parallel"),
    )(proj, c)


STEP_PAIRS = 2
STEP_HEADS = 2 * STEP_PAIRS
FOX_GROUPS = FOX_PAIRS // STEP_PAIRS


def _blockdiag(a, b):
    z = jnp.zeros_like(a)
    return jnp.concatenate([jnp.concatenate([a, z], axis=1), jnp.concatenate([z, b], axis=1)], axis=0)


def _fox_fwd(qa, ka, va, mixed, name):
    nh, nq, tq, _ = qa.shape
    t = nq * tq

    def body(qa_ref, ka_ref, va_ref, mixed_in, mixed_ref, o_ref, lse_ref):
        i = pl.program_id(1)
        lane = lax.broadcasted_iota(jnp.int32, (tq, LANE), 1)
        causal = lax.broadcasted_iota(jnp.int32, (tq, tq), 1) <= lax.broadcasted_iota(jnp.int32, (tq, tq), 0)
        qps = [jnp.concatenate([qa_ref[2 * c], qa_ref[2 * c + 1]], axis=1) for c in range(STEP_PAIRS)]

        def step(j, carry, diagonal):
            scores = [_dot(qps[c], _blockdiag(ka_ref[2 * c, j], ka_ref[2 * c + 1, j]), NT) for c in range(STEP_PAIRS)]
            new = []
            for c in range(STEP_PAIRS):
                ms, acc = carry[c]
                ps, ms_new, alphas = [], [], []
                for e in range(2):
                    s = scores[c][:, e * tq:(e + 1) * tq]
                    if diagonal:
                        s = jnp.where(causal, s, NEG)
                    m_new = jnp.maximum(ms[e], jnp.max(s, axis=-1, keepdims=True))
                    ps.append(jnp.exp(s - m_new).astype(BF16))
                    ms_new.append(m_new)
                    alphas.append(jnp.broadcast_to(jnp.exp(ms[e] - m_new), (tq, LANE)))
                pv = _dot(jnp.concatenate(ps, axis=1), _blockdiag(va_ref[2 * c, j], va_ref[2 * c + 1, j]))
                new.append((tuple(ms_new), jnp.concatenate(alphas, axis=1) * acc + pv))
            return tuple(new)

        m0 = jnp.full((tq, 1), NEG, F32)
        init = tuple(((m0, m0), jnp.zeros((tq, 2 * LANE), F32)) for _ in range(STEP_PAIRS))
        carry = lax.fori_loop(0, i, lambda j, cr: step(j, cr, False), init)
        o_pairs = []
        lse = jnp.zeros((tq, LANE), F32)
        for c, (ms, acc) in enumerate(step(i, carry, True)):
            outs = []
            for e in range(2):
                half = acc[:, e * LANE:(e + 1) * LANE]
                l = half[:, FOX_DH:FOX_DH + 1]
                outs.append(half / l)
                lse = jnp.where(lane == 2 * c + e, ms[e] + jnp.log(l), lse)
            o_pairs.append(_pair(outs[0], outs[1], lane < FOX_DH))
        o_all = jnp.concatenate(o_pairs, axis=1)
        mixed_ref[...] = o_all.astype(BF16)
        o_ref[...] = o_all
        lse_ref[...] = lse

    width = STEP_PAIRS * LANE
    whole = pl.BlockSpec((STEP_HEADS, nq, tq, LANE), lambda g, i: (g, 0, 0, 0), pipeline_mode=pl.Buffered(1))
    return _pcall(
        body, name=name, grid=(FOX_GROUPS, nq),
        in_specs=[pl.BlockSpec((STEP_HEADS, None, tq, LANE), lambda g, i: (g, i, 0, 0)), whole, whole,
                  pl.BlockSpec(memory_space=pl.ANY)],
        out_specs=[pl.BlockSpec((tq, width), lambda g, i: (i, RET_V // width + g)),
                   pl.BlockSpec((tq, width), lambda g, i: (i, g)),
                   pl.BlockSpec((None, tq, LANE), lambda g, i: (g, i, 0))],
        out_shape=[jax.ShapeDtypeStruct(mixed.shape, BF16), jax.ShapeDtypeStruct((t, FOX_W), F32),
                   jax.ShapeDtypeStruct((FOX_GROUPS, t, LANE), F32)],
        input_output_aliases={3: 0},
        compiler_params=_params("parallel", "parallel"),
    )(qa, ka, va, mixed)


def _fox_prep_bwd(dmixed, o_fox, lse, qa, name):
    t = dmixed.shape[0]
    tq = TOK_TILE

    def body(dm_ref, o_ref, lse_ref, qa_ref, qab_ref, doa_ref):
        i = pl.program_id(0)
        lane = lax.broadcasted_iota(jnp.int32, (tq, LANE), 1)
        low = lane < FOX_DH
        live = (i * tq + lax.broadcasted_iota(jnp.int32, (tq, 1), 0)) >= N_PAD
        for pair in range(FOX_PAIRS):
            cols = slice(LANE * pair, LANE * (pair + 1))
            d_slab = dm_ref[:, cols]
            prod = d_slab * o_ref[:, cols]
            for e in range(2):
                h = 2 * pair + e
                nd = -jnp.sum(jnp.where(low, _head_to_low(prod, e), 0.0), axis=-1, keepdims=True)
                nd_hi = nd.astype(BF16).astype(F32)
                doa = jnp.where(low, _head_to_low(d_slab, e), 0.0)
                doa = jnp.where(lane == FOX_DH, nd_hi, jnp.where(lane == FOX_DH + 1, nd - nd_hi, doa))
                doa_ref[h] = doa.astype(BF16)
                lse_h = lse_ref[h // STEP_HEADS][:, h % STEP_HEADS:h % STEP_HEADS + 1]
                hi, mid, lo = _split3(jnp.where(live, -lse_h, 0.0))
                qab = qa_ref[h].astype(F32)
                qab = jnp.where(lane == L_LSE, hi, jnp.where(lane == L_LSE + 1, mid, jnp.where(lane == L_LSE + 2, lo, qab)))
                qab_ref[h] = qab.astype(BF16)

    out = jax.ShapeDtypeStruct((FOX_HEADS, t, LANE), BF16)
    hspec = pl.BlockSpec((FOX_HEADS, tq, LANE), lambda i: (0, i, 0))
    return _pcall(
        body, name=name, grid=(t // tq,),
        in_specs=[pl.BlockSpec((tq, FOX_W), lambda i: (i, 1)), pl.BlockSpec((tq, FOX_W), lambda i: (i, 0)),
                  pl.BlockSpec((FOX_GROUPS, tq, LANE), lambda i: (0, i, 0)), hspec],
        out_specs=[hspec, hspec], out_shape=[out, out],
        compiler_params=_params("parallel"),
    )(dmixed, o_fox, lse, qa)


def _fox_bwd(qab, doa, ka, va, dproj, name):
    nh, nq, tq, _ = qab.shape
    t = nq * tq
    slab = 3 * LANE * STEP_PAIRS
    group0 = (2 * RET_QK + 2 * RET_V) // slab

    def body(qab_ref, doa_ref, ka_ref, va_ref, dproj_in, dp_ref, drs_ref, dcs_ref, dq_ref):
        g, j = pl.program_id(0), pl.program_id(1)

        @pl.when((g == 0) & (j == 0))
        def _():
            drs_ref[...] = jnp.zeros_like(drs_ref)
            dcs_ref[...] = jnp.zeros_like(dcs_ref)

        @pl.when(j == 0)
        def _():
            dq_ref[...] = jnp.zeros_like(dq_ref)

        lane = lax.broadcasted_iota(jnp.int32, (tq, LANE), 1)
        low = lane < FOX_DH
        key_le_query = lax.broadcasted_iota(jnp.int32, (tq, tq), 0) <= lax.broadcasted_iota(jnp.int32, (tq, tq), 1)

        def by_head(c, a, b, col):
            h = STEP_HEADS * g + 2 * c
            return jnp.where(lane == h, a[:, col:col + 1], jnp.where(lane == h + 1, b[:, col:col + 1], 0.0))

        kbs = [ka_ref[h] for h in range(STEP_HEADS)]
        vbs = [va_ref[h] for h in range(STEP_HEADS)]

        def step(i, carry, diagonal):
            qbs = [qab_ref[h, i] for h in range(STEP_HEADS)]
            dobs = [doa_ref[h, i] for h in range(STEP_HEADS)]
            st = [_dot(kbs[h], qbs[h], NT) for h in range(STEP_HEADS)]
            dpt = [_dot(vbs[h], dobs[h], NT) for h in range(STEP_HEADS)]
            new = []
            for h in range(STEP_HEADS):
                p = jnp.exp(st[h])
                if diagonal:
                    p = jnp.where(key_le_query, p, 0.0)
                ds = (p * dpt[h]).astype(BF16)
                dq_ref[h, i] += _dot(ds, kbs[h], TN)
                dk, dv = carry[h]
                new.append((dk + _dot(ds, qbs[h]), dv + _dot(p.astype(BF16), dobs[h])))
            return tuple(new)

        zero = jnp.zeros((tq, LANE), F32)
        carry = step(j, tuple((zero, zero) for _ in range(STEP_HEADS)), True)
        carry = lax.fori_loop(j + 1, nq, lambda i, cr: step(i, cr, False), carry)
        rows = pl.ds(pl.multiple_of(j * tq, tq), tq)
        for c in range(STEP_PAIRS):
            (dka, dva), (dkb, dvb) = carry[2 * c], carry[2 * c + 1]
            c0 = 3 * LANE * c
            dp_ref[rows, c0 + LANE:c0 + 2 * LANE] = _pair(dka, dkb, low).astype(BF16)
            dp_ref[rows, c0 + 2 * LANE:c0 + 3 * LANE] = _pair(dva, dvb, low).astype(BF16)
            dcs_ref[rows, :] += by_head(c, dka, dkb, L_ONE_Q)

        @pl.when(j == nq - 1)
        def _():
            for c in range(STEP_PAIRS):
                for blk in range(nq):
                    r = slice(blk * tq, (blk + 1) * tq)
                    a, b = dq_ref[2 * c, blk], dq_ref[2 * c + 1, blk]
                    dp_ref[r, 3 * LANE * c:3 * LANE * c + LANE] = (_pair(a, b, low) * QK_SCALE).astype(BF16)
                    drs_ref[r, :] += by_head(c, a, b, L_ONE_K)

    whole = pl.BlockSpec((STEP_HEADS, nq, tq, LANE), lambda g, j: (g, 0, 0, 0), pipeline_mode=pl.Buffered(1))
    blk = pl.BlockSpec((STEP_HEADS, None, tq, LANE), lambda g, j: (g, j, 0, 0))
    sums = pl.BlockSpec((t, LANE), lambda g, j: (0, 0), pipeline_mode=pl.Buffered(1))
    return _pcall(
        body, name=name, grid=(FOX_GROUPS, nq),
        in_specs=[whole, whole, blk, blk, pl.BlockSpec(memory_space=pl.ANY)],
        out_specs=[pl.BlockSpec((t, slab), lambda g, j: (0, group0 + g)), sums, sums],
        out_shape=[jax.ShapeDtypeStruct(dproj.shape, BF16), jax.ShapeDtypeStruct((t, LANE), F32),
                   jax.ShapeDtypeStruct((t, LANE), F32)],
        input_output_aliases={4: 0},
        scratch_shapes=[pltpu.VMEM((STEP_HEADS, nq, tq, LANE), F32)],
        compiler_params=_params("arbitrary", "arbitrary"),
    )(qab, doa, ka, va, dproj)


HALO = 8


def _rows_ext(ref, r0, rows, t, before, after):
    lo, hi = r0 - before, r0 + rows + after
    parts = []
    if lo < 0:
        parts.append(jnp.zeros((-lo, LANE), F32))
    parts.append(ref[max(lo, 0):min(hi, t), :].astype(F32))
    if hi > t:
        parts.append(jnp.zeros((hi - t, LANE), F32))
    return parts[0] if len(parts) == 1 else jnp.concatenate(parts, axis=0)


def _conv_taps(a_ext, r0_ext, cw_ref, cb_ref):
    n = a_ext.shape[0]
    if r0_ext < N_PAD:
        row = r0_ext + lax.broadcasted_iota(jnp.int32, (n, 1), 0)
        a_ext = jnp.where(row >= N_PAD, a_ext, 0.0)
    a1 = pltpu.roll(a_ext, 1, axis=0)
    a2 = pltpu.roll(a_ext, 2, axis=0)
    acc = cb_ref[...] + a2 * cw_ref[0:1, :] + a1 * cw_ref[1:2, :] + a_ext * cw_ref[2:3, :]
    return a_ext, a1, a2, acc


def _conv_gate_fwd(up, conv_w8, conv_b, name):
    _, t, f = up.shape
    rows = TOK_TILE

    def body(a_ref, b_ref, cw_ref, cb_ref, g_ref):
        for r0 in range(0, t, rows):
            a_ext = _rows_ext(a_ref, r0, rows, t, HALO, 0)
            _, _, _, acc = _conv_taps(a_ext, r0 - HALO, cw_ref, cb_ref)
            acc = acc[HALO:, :]
            g_ref[r0:r0 + rows, :] = (acc * _sigmoid(acc) * b_ref[r0:r0 + rows, :]).astype(BF16)

    return _pcall(
        body, name=name, grid=(f // LANE,),
        in_specs=[pl.BlockSpec((None, t, LANE), lambda j: (0, 0, j)), pl.BlockSpec((None, t, LANE), lambda j: (1, 0, j)),
                  pl.BlockSpec((8, LANE), lambda j: (0, j)), pl.BlockSpec((1, LANE), lambda j: (0, j))],
        out_specs=pl.BlockSpec((t, LANE), lambda j: (0, j)),
        out_shape=jax.ShapeDtypeStruct((t, f), BF16),
        compiler_params=_params("parallel"),
    )(up, up, conv_w8, conv_b)


def _conv_gate_bwd(up, conv_w8, conv_b, dg, name):
    _, t, f = up.shape
    rows = TOK_TILE

    def body(a_ref, b_ref, cw_ref, cb_ref, dg_ref, dup_ref, gcw_ref, gcb_ref):
        gw = [jnp.zeros((1, LANE), F32) for _ in range(3)]
        gb = jnp.zeros((1, LANE), F32)
        for r0 in range(0, t, rows):
            a_ext = _rows_ext(a_ref, r0, rows, t, HALO, HALO)
            b_ext = _rows_ext(b_ref, r0, rows, t, HALO, HALO)
            dg_ext = _rows_ext(dg_ref, r0, rows, t, HALO, HALO)
            a0, a1, a2, acc = _conv_taps(a_ext, r0 - HALO, cw_ref, cb_ref)
            sg = _sigmoid(acc)
            dacc = dg_ext * b_ext * (sg * (1.0 + acc * (1.0 - sg)))
            n = dacc.shape[0]
            da = (dacc * cw_ref[2:3, :] + pltpu.roll(dacc, n - 1, axis=0) * cw_ref[1:2, :]
                  + pltpu.roll(dacc, n - 2, axis=0) * cw_ref[0:1, :])
            core = slice(HALO, HALO + rows)
            da = da[core, :]
            if r0 < N_PAD:
                row = r0 + lax.broadcasted_iota(jnp.int32, (rows, 1), 0)
                da = jnp.where(row >= N_PAD, da, 0.0)
            dup_ref[0, r0:r0 + rows, :] = da.astype(BF16)
            dup_ref[1, r0:r0 + rows, :] = (dg_ext * acc * sg)[core, :].astype(BF16)
            dacc_c = dacc[core, :]
            gw[0] = gw[0] + jnp.sum(dacc_c * a2[core, :], axis=0, keepdims=True)
            gw[1] = gw[1] + jnp.sum(dacc_c * a1[core, :], axis=0, keepdims=True)
            gw[2] = gw[2] + jnp.sum(dacc_c * a0[core, :], axis=0, keepdims=True)
            gb = gb + jnp.sum(dacc_c, axis=0, keepdims=True)
        gcw_ref[...] = jnp.zeros((8, LANE), F32)
        for tap in range(3):
            gcw_ref[tap:tap + 1, :] = gw[tap]
        gcb_ref[...] = gb

    return _pcall(
        body, name=name, grid=(f // LANE,),
        in_specs=[pl.BlockSpec((None, t, LANE), lambda j: (0, 0, j)), pl.BlockSpec((None, t, LANE), lambda j: (1, 0, j)),
                  pl.BlockSpec((8, LANE), lambda j: (0, j)), pl.BlockSpec((1, LANE), lambda j: (0, j)),
                  pl.BlockSpec((t, LANE), lambda j: (0, j))],
        out_specs=[pl.BlockSpec((2, t, LANE), lambda j: (0, 0, j)), pl.BlockSpec((8, LANE), lambda j: (0, j)),
                   pl.BlockSpec((1, LANE), lambda j: (0, j))],
        out_shape=[jax.ShapeDtypeStruct((2, t, f), BF16), jax.ShapeDtypeStruct((8, f), F32),
                   jax.ShapeDtypeStruct((1, f), F32)],
        compiler_params=_params("parallel"),
    )(up, up, conv_w8, conv_b, dg)


def _exchange(arrays, kinds, name):
    n = len(arrays)
    npeer = N_DEV - 1

    def body(*refs):
        ins, outs = refs[:n], refs[n:2 * n]
        send_sems, recv_sems, local_sems = refs[2 * n:]
        x, y, c = lax.axis_index("x"), lax.axis_index("y"), lax.axis_index("c")
        me = 4 * x + 2 * y + c
        copies, locals_ = [], []
        for a in range(n):
            gather = kinds[a] == "gather"
            own = pltpu.make_async_copy(ins[a] if gather else ins[a].at[me], outs[a].at[me], local_sems.at[a])
            own.start()
            locals_.append(own)
            for d in range(1, N_DEV):
                px = 1 - x if d & 4 else x
                py = 1 - y if d & 2 else y
                pc = 1 - c if d & 1 else c
                src = ins[a] if gather else ins[a].at[4 * px + 2 * py + pc]
                cp = pltpu.make_async_remote_copy(
                    src_ref=src, dst_ref=outs[a].at[me],
                    send_sem=send_sems.at[a * npeer + d - 1], recv_sem=recv_sems.at[a * npeer + d - 1],
                    device_id=(px, py, pc), device_id_type=pl.DeviceIdType.MESH)
                cp.start()
                copies.append(cp)
        for cp in copies:
            cp.wait_recv()
        for cp in copies:
            cp.wait_send()
        for own in locals_:
            own.wait()

    out_shape = [jax.ShapeDtypeStruct((N_DEV,) + (a.shape if k == "gather" else a.shape[1:]), a.dtype)
                 for a, k in zip(arrays, kinds)]
    return _pcall(
        body, name=name,
        in_specs=[pl.BlockSpec(memory_space=pl.ANY)] * n,
        out_specs=[pl.BlockSpec(memory_space=pl.ANY)] * n,
        out_shape=out_shape,
        scratch_shapes=[pltpu.SemaphoreType.DMA((n * npeer,)), pltpu.SemaphoreType.DMA((n * npeer,)),
                        pltpu.SemaphoreType.DMA((n,))],
        compiler_params=pltpu.CompilerParams(has_side_effects=True),
    )(*arrays)


def _sum_slots(slots, name, rows_tile):
    nd, r, c = slots.shape

    def body(s_ref, o_ref):
        acc = s_ref[0].astype(F32)
        for p in range(1, nd):
            acc = acc + s_ref[p].astype(F32)
        o_ref[...] = acc

    return _pcall(
        body, name=name, grid=(r // rows_tile,),
        in_specs=[pl.BlockSpec((nd, rows_tile, c), lambda i: (0, i, 0))],
        out_specs=pl.BlockSpec((rows_tile, c), lambda i: (i, 0)),
        out_shape=jax.ShapeDtypeStruct((r, c), F32),
        compiler_params=_params("parallel"),
    )(slots)


def _adamw(w, g, m, v, name, rows_tile):
    r, c = w.shape

    def body(w_ref, g_ref, m_ref, v_ref, d_ref, nm_ref, nv_ref):
        gr = g_ref[...]
        nm = ADAM_B1 * m_ref[...] + (1.0 - ADAM_B1) * gr
        nv = ADAM_B2 * v_ref[...] + (1.0 - ADAM_B2) * (gr * gr)
        m_hat = nm / (1.0 - ADAM_B1 ** ADAM_STEP)
        v_hat = nv / (1.0 - ADAM_B2 ** ADAM_STEP)
        d_ref[...] = -ADAM_LR * (m_hat / (jnp.sqrt(v_hat) + ADAM_EPS) + ADAM_WD * w_ref[...])
        nm_ref[...] = nm
        nv_ref[...] = nv

    spec = pl.BlockSpec((rows_tile, c), lambda i: (i, 0))
    shp = jax.ShapeDtypeStruct((r, c), F32)
    return _pcall(
        body, name=name, grid=(r // rows_tile,), in_specs=[spec] * 4, out_specs=[spec] * 3, out_shape=[shp] * 3,
        compiler_params=_params("parallel"),
    )(w, g, m, v)


F0 = 2 * RET_QK + 2 * RET_V


def _to_internal_rows(w_t):
    parts = [w_t[:F0]]
    for pair in range(FOX_PAIRS):
        for group in range(3):
            lo = F0 + group * FOX_W + LANE * pair
            parts.append(w_t[lo:lo + LANE])
    parts.append(w_t[F0 + 3 * FOX_W:])
    parts.append(jnp.zeros((IN_PAD - IN_WIDTH, w_t.shape[1]), w_t.dtype))
    return jnp.concatenate(parts, axis=0)


def _from_internal_rows(g_t):
    parts = [g_t[:F0]]
    for group in range(3):
        for pair in range(FOX_PAIRS):
            lo = F0 + 3 * LANE * pair + LANE * group
            parts.append(g_t[lo:lo + LANE])
    parts.append(g_t[F0 + 3 * FOX_W:F0 + 3 * FOX_W + FOX_HEADS])
    return jnp.concatenate(parts, axis=0)


def _local_step(x, target, meta, attn_g, w_in_t, fox_b, ret_g, w_out, ffn_g, w_up_t, conv_w8, conv_b, w_down, final_g):
    seq, d = x.shape
    t = seq + PREFIX
    tm = TOK_TILE
    nq = t // tm
    fox_b128 = jnp.pad(fox_b, ((0, 0), (0, LANE - FOX_HEADS)))

    h0, n1 = _prep_norm(x, meta, attn_g, "prep_norm")
    proj = _mm_simple(n1, w_in_t, mode="nt", tm=tm, tn=640, tk=d, out_dtype=F32, name="mm_in")
    cos, sin = _rope_tables(t)
    o_pre, mixed, states = _ret_fwd(proj, cos, sin, ret_g, "ret_fwd")
    c = _forget_cumsum(proj, fox_b128, "forget_cumsum")
    qa, ka, va = _fox_prep(proj, c, "fox_prep")
    by_block = lambda a: a.reshape(FOX_HEADS, nq, tm, LANE)
    mixed, o_fox, lse = _fox_fwd(by_block(qa), by_block(ka), by_block(va), mixed, "fox_fwd")
    h1 = _mm_simple(mixed, w_out, mode="nn", tm=tm, tn=d, tk=d, out_dtype=F32, name="mm_out", add=h0)
    n2 = _rmsnorm(h1, ffn_g, "ffn_norm")
    nf = D_FF // 1408
    up = _matmul(
        n2, w_up_t, mode="nt", grid=(nq, 2 * nf, 1),
        a_spec=pl.BlockSpec((tm, d), lambda i, j, k: (i, 0)),
        b_spec=pl.BlockSpec((None, 1408, d), lambda i, j, k: (j // nf, j % nf, 0)),
        o_spec=pl.BlockSpec((None, tm, 1408), lambda i, j, k: (j // nf, i, j % nf)),
        out_shape=jax.ShapeDtypeStruct((2, t, D_FF), F32), name="mm_up")
    g = _conv_gate_fwd(up, conv_w8, conv_b, "conv_gate_fwd")
    h2 = _mm_simple(g, w_down, mode="nn", tm=tm, tn=d, tk=D_FF, out_dtype=F32, name="mm_down", add=h1)

    loss_tile, dh2, g_final = _loss_bwd(h2, target, final_g, "loss_bwd")
    dg = _mm_simple(dh2, w_down, mode="nt", tm=tm, tn=1408, tk=d, out_dtype=F32, name="mm_dg")
    gw_down = _mm_simple(g, dh2, mode="tn", tm=1408, tn=d, tk=tm, out_dtype=BF16, name="mm_gw_down")
    dup, g_conv_w8, g_conv_b = _conv_gate_bwd(up, conv_w8, conv_b, dg, "conv_gate_bwd")
    dn2 = _matmul(
        dup, w_up_t, mode="nn", grid=(nq, 1, 2 * nf),
        a_spec=pl.BlockSpec((None, tm, 1408), lambda i, j, k: (k // nf, i, k % nf)),
        b_spec=pl.BlockSpec((None, 1408, d), lambda i, j, k: (k // nf, k % nf, 0)),
        o_spec=pl.BlockSpec((tm, d), lambda i, j, k: (i, 0)),
        out_shape=jax.ShapeDtypeStruct((t, d), F32), name="mm_dn2")
    gw_up_t = _matmul(
        dup, n2, mode="tn", grid=(2 * nf, 1, nq),
        a_spec=pl.BlockSpec((None, tm, 1408), lambda i, j, k: (i // nf, k, i % nf)),
        b_spec=pl.BlockSpec((tm, d), lambda i, j, k: (k, 0)),
        o_spec=pl.BlockSpec((1408, d), lambda i, j, k: (i, 0)),
        out_shape=jax.ShapeDtypeStruct((2 * D_FF, d), BF16), name="mm_gw_up")
    dh1, g_ffn = _rmsnorm_bwd(dn2, h1, ffn_g, dh2, "ffn_norm_bwd")

    dmixed = _mm_simple(dh1, w_out, mode="nt", tm=tm, tn=d, tk=d, out_dtype=F32, name="mm_dmixed")
    gw_out = _mm_simple(mixed, dh1, mode="tn", tm=512, tn=d, tk=tm, out_dtype=BF16, name="mm_gw_out")
    dproj, g_ret = _ret_bwd(proj, cos, sin, ret_g, dmixed, o_pre, states, "ret_bwd")
    qab, doa = _fox_prep_bwd(dmixed, o_fox, lse, qa, "fox_prep_bwd")
    dproj, drs, dcs = _fox_bwd(by_block(qab), by_block(doa), by_block(ka), by_block(va), dproj, "fox_bwd")
    dproj, g_fox_b = _forget_cumsum_bwd(proj, fox_b128, drs, dcs, dproj, "forget_cumsum_bwd")
    dn1 = _mm_simple(dproj, w_in_t, mode="nn", tm=tm, tn=d, tk=640, out_dtype=F32, name="mm_dn1")
    gw_in_t = _mm_simple(dproj, n1, mode="tn", tm=640, tn=d, tk=tm, out_dtype=BF16, name="mm_gw_in")
    dh0, g_attn = _rmsnorm_bwd(dn1, h0, attn_g, dh1, "attn_norm_bwd")

    grads = dict(meta=dh0[N_PAD:PREFIX], attn_g=g_attn, w_in_t=gw_in_t, fox_b=g_fox_b[:, :FOX_HEADS], ret_g=g_ret,
                 w_out=gw_out, ffn_g=g_ffn, w_up_t=gw_up_t, conv_w=g_conv_w8[:3], conv_b=g_conv_b, w_down=gw_down,
                 final_g=g_final)
    return loss_tile, dh0[PREFIX:], grads


def _pack(pieces, width=LANE):
    rows = []
    for p in pieces:
        flat = p.reshape(-1)
        pad = (-flat.shape[0]) % width
        rows.append(jnp.pad(flat, (0, pad)).reshape(-1, width))
    out = jnp.concatenate(rows, axis=0)
    return jnp.pad(out, ((0, (-out.shape[0]) % 8), (0, 0)))


def _unpack(packed, shapes, width=LANE):
    outs, r = [], 0
    for shp in shapes:
        size = int(np.prod(shp))
        nrows = -(-size // width)
        outs.append(packed[r:r + nrows].reshape(-1)[:size].reshape(shp))
        r += nrows
    return outs


def kernel(x, meta_tokens, attn_norm_g, w_in, fox_forget_b, ret_norm_g, w_out, ffn_norm_g, w_up, conv_w, conv_b, w_down, final_norm_g, loss_target, m_meta_tokens, m_attn_norm_g, m_w_in, m_fox_forget_b, m_ret_norm_g, m_w_out, m_ffn_norm_g, m_w_up, m_conv_w, m_conv_b, m_w_down, m_final_norm_g, v_meta_tokens, v_attn_norm_g, v_w_in, v_fox_forget_b, v_ret_norm_g, v_w_out, v_ffn_norm_g, v_w_up, v_conv_w, v_conv_b, v_w_down, v_final_norm_g):
    d = D_MODEL
    me = 4 * lax.axis_index("x") + 2 * lax.axis_index("y") + lax.axis_index("c")
    in_blk = IN_WIDTH // N_DEV
    in_blk_pad = 400
    up_blk = 2 * D_FF // N_DEV
    down_blk = D_FF // N_DEV
    cw_blk = D_FF // N_DEV

    w_in_loc = jnp.pad(w_in[0].T.astype(BF16), ((0, in_blk_pad - in_blk), (0, 0)))
    cw_loc = jnp.pad(conv_w[0], ((0, 5), (0, 384 - cw_blk)))
    g_in, g_out, g_up, g_down, g_meta, g_cw = _exchange(
        [w_in_loc, w_out[0].astype(BF16), w_up[0].T.astype(BF16), w_down[0].astype(BF16), meta_tokens, cw_loc],
        ["gather"] * 6, "gather_weights")
    w_in_t = _to_internal_rows(g_in[:, :in_blk].reshape(IN_WIDTH, d))
    w_out_f = g_out.reshape(d, d)
    w_up_t = g_up.reshape(2, D_FF, d)
    w_down_f = g_down.reshape(D_FF, d)
    meta_f = g_meta.transpose(1, 0, 2).reshape(N_META, d)
    conv_w8 = jnp.pad(g_cw[:, :3, :cw_blk].transpose(1, 0, 2).reshape(3, D_FF), ((0, 5), (0, 0)))

    loss_tile, grad_x, gr = _local_step(
        x[0], loss_target[0], meta_f, attn_norm_g, w_in_t, fox_forget_b, ret_norm_g, w_out_f, ffn_norm_g,
        w_up_t, conv_w8, conv_b, w_down_f, final_norm_g.reshape(1, d))

    s_in = jnp.pad(_from_internal_rows(gr["w_in_t"]).reshape(N_DEV, in_blk, d), ((0, 0), (0, in_blk_pad - in_blk), (0, 0)))
    small_shapes = [(1, LANE), (1, d), (1, FOX_HEADS), (1, RET_V), (1, d), (1, D_FF), (1, d), (N_META, d), (3, D_FF)]
    small = _pack([loss_tile[0:1], gr["attn_g"], gr["fox_b"], gr["ret_g"], gr["ffn_g"], gr["conv_b"], gr["final_g"],
                   gr["meta"], gr["conv_w"]])
    r_in, r_out, r_up, r_down, r_small = _exchange(
        [s_in, gr["w_out"].reshape(N_DEV, d // N_DEV, d), gr["w_up_t"].reshape(N_DEV, up_blk, d),
         gr["w_down"].reshape(N_DEV, down_blk, d), small],
        ["scatter"] * 4 + ["gather"], "exchange_grads")
    g_w_in = _sum_slots(r_in, "sum_w_in", in_blk_pad)[:in_blk].T
    g_w_out = _sum_slots(r_out, "sum_w_out", d // N_DEV)
    g_w_up = _sum_slots(r_up, "sum_w_up", up_blk).T
    g_w_down = _sum_slots(r_down, "sum_w_down", down_blk)
    s_all = _sum_slots(r_small, "sum_small", r_small.shape[1])
    (loss_row, g_attn, g_fox_b, g_ret, g_ffn, g_conv_b, g_final, g_meta_full, g_cw_full) = _unpack(s_all, small_shapes)
    loss = loss_row[0, 0]
    g_meta_loc = lax.dynamic_slice(g_meta_full, (0, me * (d // N_DEV)), (N_META, d // N_DEV))
    g_cw_loc = lax.dynamic_slice(g_cw_full, (0, me * cw_blk), (3, cw_blk))

    d_w_in, m_w_in_n, v_w_in_n = _adamw(w_in[0], g_w_in, m_w_in[0], v_w_in[0], "adamw_w_in", 128)
    d_w_out, m_w_out_n, v_w_out_n = _adamw(w_out[0], g_w_out, m_w_out[0], v_w_out[0], "adamw_w_out", 128)
    d_w_up, m_w_up_n, v_w_up_n = _adamw(w_up[0], g_w_up, m_w_up[0], v_w_up[0], "adamw_w_up", 128)
    d_w_down, m_w_down_n, v_w_down_n = _adamw(w_down[0], g_w_down, m_w_down[0], v_w_down[0], "adamw_w_down", down_blk)
    sm_grads = [g_meta_loc, g_attn, g_fox_b, g_ret, g_ffn, g_cw_loc, g_conv_b, g_final.reshape(d)]
    sm_w = [meta_tokens, attn_norm_g, fox_forget_b, ret_norm_g, ffn_norm_g, conv_w[0], conv_b, final_norm_g]
    sm_m = [m_meta_tokens, m_attn_norm_g, m_fox_forget_b, m_ret_norm_g, m_ffn_norm_g, m_conv_w[0], m_conv_b, m_final_norm_g]
    sm_v = [v_meta_tokens, v_attn_norm_g, v_fox_forget_b, v_ret_norm_g, v_ffn_norm_g, v_conv_w[0], v_conv_b, v_final_norm_g]
    sm_shapes = [a.shape for a in sm_w]
    pk = [_pack(lst) for lst in (sm_w, sm_grads, sm_m, sm_v)]
    sm_d, sm_nm, sm_nv = _adamw(pk[0], pk[1], pk[2], pk[3], "adamw_small", pk[0].shape[0])
    dl = _unpack(sm_d, sm_shapes)
    ml = _unpack(sm_nm, sm_shapes)
    vl = _unpack(sm_nv, sm_shapes)

    def by_weight(meta_, attn_, w_in_, fox_, ret_, w_out_, ffn_, w_up_, cw_, cb_, w_down_, final_):
        return (meta_, attn_, w_in_[None], fox_, ret_, w_out_[None], ffn_, w_up_[None], cw_[None], cb_, w_down_[None], final_)

    grads_out = by_weight(g_meta_loc, g_attn, g_w_in, g_fox_b, g_ret, g_w_out, g_ffn, g_w_up, g_cw_loc, g_conv_b,
                          g_w_down, g_final.reshape(d))
    delta_out = by_weight(dl[0], dl[1], d_w_in, dl[2], dl[3], d_w_out, dl[4], d_w_up, dl[5], dl[6], d_w_down, dl[7])
    m_out = by_weight(ml[0], ml[1], m_w_in_n, ml[2], ml[3], m_w_out_n, ml[4], m_w_up_n, ml[5], ml[6], m_w_down_n, ml[7])
    v_out = by_weight(vl[0], vl[1], v_w_in_n, vl[2], vl[3], v_w_out_n, vl[4], v_w_up_n, vl[5], vl[6], v_w_down_n, vl[7])
    return (loss, grad_x[None]) + grads_out + delta_out + m_out + v_out
```

```python
import numpy as np
import jax
import jax.numpy as jnp
from jax import lax
from jax.experimental import pallas as pl
from jax.experimental.pallas import tpu as pltpu

F32 = jnp.float32
BF16 = jnp.bfloat16

D_MODEL = 1024
N_META = 16
N_PAD = 112
PREFIX = 128
RET_HEADS = 4
RET_DK = 64
RET_DV = 128
FOX_HEADS = 8
FOX_DH = 64
D_FF = 2816
ROPE_BASE = 10000.0
EPS = 1e-6
NEG = -1e30
RET_QK = RET_HEADS * RET_DK
RET_V = RET_HEADS * RET_DV
FOX_W = FOX_HEADS * FOX_DH
IN_WIDTH = 2 * RET_QK + 2 * RET_V + 3 * FOX_W + FOX_HEADS
IN_PAD = 3200
FF_COL_BLOCK = (IN_WIDTH - FOX_HEADS) // 128
QK_SCALE = 0.125

ADAM_LR = 0.001
ADAM_B1 = 0.9
ADAM_B2 = 0.999
ADAM_EPS = 1e-08
ADAM_WD = 0.01
ADAM_STEP = 10

N_DEV = 8
LANE = 128
ROW_TILE = 128
TOK_TILE = 384

NN = (((1,), (0,)), ((), ()))
NT = (((1,), (1,)), ((), ()))
TN = (((0,), (0,)), ((), ()))


def _pcall(body, **kw):
    return pl.pallas_call(body, **kw)


def _params(*sem):
    return pltpu.CompilerParams(dimension_semantics=sem)


def _dot(a, b, dims=NN):
    return lax.dot_general(a, b, dims, preferred_element_type=F32)


def _sigmoid(x):
    return 1.0 / (1.0 + jnp.exp(-x))


def _matmul(a, b, *, mode, grid, a_spec, b_spec, o_spec, out_shape, name, add=None, add_spec=None):
    dims = {"nn": NN, "nt": NT, "tn": TN}[mode]
    nk = grid[2]
    has_add = add is not None

    def body(*refs):
        if has_add:
            a_ref, b_ref, add_ref, o_ref = refs[:4]
        else:
            a_ref, b_ref, o_ref = refs[:3]
        part = _dot(a_ref[...].astype(BF16), b_ref[...].astype(BF16), dims)

        def finish(acc):
            if has_add:
                acc = acc + add_ref[...]
            o_ref[...] = acc.astype(o_ref.dtype)

        if nk == 1:
            finish(part)
        else:
            acc_ref = refs[-1]
            k = pl.program_id(2)

            @pl.when(k == 0)
            def _():
                acc_ref[...] = part

            @pl.when(k > 0)
            def _():
                acc_ref[...] += part

            @pl.when(k == nk - 1)
            def _():
                finish(acc_ref[...])

    in_specs = [a_spec, b_spec] + ([add_spec] if has_add else [])
    args = (a, b) + ((add,) if has_add else ())
    scratch = [] if nk == 1 else [pltpu.VMEM(tuple(d for d in o_spec.block_shape if d is not None), F32)]
    return _pcall(
        body, name=name, grid=grid, in_specs=in_specs, out_specs=o_spec, out_shape=out_shape,
        scratch_shapes=scratch, compiler_params=_params("parallel", "parallel", "arbitrary"),
    )(*args)


def _mm_simple(a, b, *, mode, tm, tn, tk, out_dtype, name, add=None):
    if mode == "tn":
        K, M = a.shape
    else:
        M, K = a.shape
    N = b.shape[0] if mode == "nt" else b.shape[1]
    grid = (M // tm, N // tn, K // tk)
    a_spec = pl.BlockSpec((tk, tm), lambda i, j, k: (k, i)) if mode == "tn" else pl.BlockSpec((tm, tk), lambda i, j, k: (i, k))
    b_spec = pl.BlockSpec((tn, tk), lambda i, j, k: (j, k)) if mode == "nt" else pl.BlockSpec((tk, tn), lambda i, j, k: (k, j))
    o_spec = pl.BlockSpec((tm, tn), lambda i, j, k: (i, j))
    return _matmul(a, b, mode=mode, grid=grid, a_spec=a_spec, b_spec=b_spec, o_spec=o_spec,
                   out_shape=jax.ShapeDtypeStruct((M, N), out_dtype), name=name, add=add,
                   add_spec=o_spec if add is not None else None)


def _prep_norm(x, meta, gain, name):
    seq, d = x.shape
    t = seq + PREFIX
    nb = t // ROW_TILE

    def body(x_ref, meta_ref, g_ref, h_ref, n_ref):
        i = pl.program_id(0)

        @pl.when(i == 0)
        def _():
            h_ref[0:N_PAD, :] = jnp.zeros((N_PAD, d), F32)
            h_ref[N_PAD:ROW_TILE, :] = meta_ref[...]

        @pl.when(i > 0)
        def _():
            h_ref[...] = x_ref[...]

        h = h_ref[...]
        r = lax.rsqrt(jnp.mean(h * h, axis=-1, keepdims=True) + EPS)
        n_ref[...] = (h * r * g_ref[...]).astype(BF16)

    return _pcall(
        body, name=name, grid=(nb,),
        in_specs=[pl.BlockSpec((ROW_TILE, d), lambda i: (jnp.maximum(i - 1, 0), 0)),
                  pl.BlockSpec((N_META, d), lambda i: (0, 0)),
                  pl.BlockSpec((1, d), lambda i: (0, 0))],
        out_specs=[pl.BlockSpec((ROW_TILE, d), lambda i: (i, 0)), pl.BlockSpec((ROW_TILE, d), lambda i: (i, 0))],
        out_shape=[jax.ShapeDtypeStruct((t, d), F32), jax.ShapeDtypeStruct((t, d), BF16)],
        compiler_params=_params("parallel"),
    )(x, meta, gain)


def _rmsnorm(h, gain, name):
    t, d = h.shape

    def body(h_ref, g_ref, n_ref):
        x = h_ref[...]
        r = lax.rsqrt(jnp.mean(x * x, axis=-1, keepdims=True) + EPS)
        n_ref[...] = (x * r * g_ref[...]).astype(BF16)

    return _pcall(
        body, name=name, grid=(t // TOK_TILE,),
        in_specs=[pl.BlockSpec((TOK_TILE, d), lambda i: (i, 0)), pl.BlockSpec((1, d), lambda i: (0, 0))],
        out_specs=pl.BlockSpec((TOK_TILE, d), lambda i: (i, 0)),
        out_shape=jax.ShapeDtypeStruct((t, d), BF16),
        compiler_params=_params("parallel"),
    )(h, gain)


def _rmsnorm_bwd(dn, h, gain, dres, name):
    t, d = h.shape

    def body(dn_ref, h_ref, g_ref, dres_ref, dh_ref, gg_ref):
        i = pl.program_id(0)
        x = h_ref[...]
        r = lax.rsqrt(jnp.mean(x * x, axis=-1, keepdims=True) + EPS)
        xhat = x * r
        dy = dn_ref[...]
        u = dy * g_ref[...]
        dh_ref[...] = dres_ref[...] + r * (u - xhat * jnp.mean(u * xhat, axis=-1, keepdims=True))
        part = jnp.sum(dy * xhat, axis=0, keepdims=True)

        @pl.when(i == 0)
        def _():
            gg_ref[...] = part

        @pl.when(i > 0)
        def _():
            gg_ref[...] += part

    return _pcall(
        body, name=name, grid=(t // TOK_TILE,),
        in_specs=[pl.BlockSpec((TOK_TILE, d), lambda i: (i, 0)), pl.BlockSpec((TOK_TILE, d), lambda i: (i, 0)),
                  pl.BlockSpec((1, d), lambda i: (0, 0)), pl.BlockSpec((TOK_TILE, d), lambda i: (i, 0))],
        out_specs=[pl.BlockSpec((TOK_TILE, d), lambda i: (i, 0)), pl.BlockSpec((1, d), lambda i: (0, 0))],
        out_shape=[jax.ShapeDtypeStruct((t, d), F32), jax.ShapeDtypeStruct((1, d), F32)],
        compiler_params=_params("arbitrary"),
    )(dn, h, gain, dres)


def _loss_bwd(h2, target, gain, name):
    t, d = h2.shape
    nb = t // ROW_TILE

    def body(h_ref, tgt_ref, g_ref, loss_ref, dh_ref, gg_ref):
        i = pl.program_id(0)

        @pl.when(i == 0)
        def _():
            loss_ref[...] = jnp.zeros_like(loss_ref)
            gg_ref[...] = jnp.zeros_like(gg_ref)
            dh_ref[...] = jnp.zeros_like(dh_ref)

        @pl.when(i > 0)
        def _():
            x = h_ref[...]
            r = lax.rsqrt(jnp.mean(x * x, axis=-1, keepdims=True) + EPS)
            xhat = x * r
            g = g_ref[...]
            err = xhat * g - tgt_ref[...]
            loss_ref[...] += 0.5 * jnp.sum(jnp.mean(err * err, axis=-1, keepdims=True))
            dy = err * (1.0 / d)
            u = dy * g
            dh_ref[...] = r * (u - xhat * jnp.mean(u * xhat, axis=-1, keepdims=True))
            gg_ref[...] += jnp.sum(dy * xhat, axis=0, keepdims=True)

    return _pcall(
        body, name=name, grid=(nb,),
        in_specs=[pl.BlockSpec((ROW_TILE, d), lambda i: (i, 0)),
                  pl.BlockSpec((ROW_TILE, d), lambda i: (jnp.maximum(i - 1, 0), 0)),
                  pl.BlockSpec((1, d), lambda i: (0, 0))],
        out_specs=[pl.BlockSpec((8, LANE), lambda i: (0, 0)), pl.BlockSpec((ROW_TILE, d), lambda i: (i, 0)),
                   pl.BlockSpec((1, d), lambda i: (0, 0))],
        out_shape=[jax.ShapeDtypeStruct((8, LANE), F32), jax.ShapeDtypeStruct((t, d), F32),
                   jax.ShapeDtypeStruct((1, d), F32)],
        compiler_params=_params("arbitrary"),
    )(h2, target, gain)


def _ret_consts(bk):
    gam = 1.0 - 2.0 ** (-5.0 - np.arange(RET_HEADS))
    n = np.arange(bk)
    same_or_earlier_chunk = (n[None, :] // 64) <= (n[:, None] // 64)
    w = gam[:, None, None] ** np.abs(n[:, None] - n[None, :])[None] * same_or_earlier_chunk[None]
    wq = gam[:, None] ** (n[None, :] + 1.0)
    wk = gam[:, None] ** (bk - 1.0 - n[None, :])
    mask = (np.arange(RET_QK)[None, :] // RET_DK) == np.arange(RET_HEADS)[:, None]
    return (jnp.asarray(w, F32), jnp.asarray(wq[:, :, None], F32), jnp.asarray(wk[:, :, None], F32),
            jnp.asarray(mask[:, None, :], F32), [float(g ** bk) for g in gam])


def _rope_tables(t):
    half = RET_DK // 2
    inv = 1.0 / (ROPE_BASE ** (jnp.arange(half, dtype=F32) / half))
    ang = jnp.arange(t).astype(F32)[:, None] * inv[None, :]
    cos, sin = jnp.cos(ang), jnp.sin(ang)
    return (jnp.tile(jnp.concatenate([cos, cos], axis=1), (1, RET_HEADS)),
            jnp.tile(jnp.concatenate([-sin, sin], axis=1), (1, RET_HEADS)))


def _swap_halves(x):
    outs = []
    for s in range(x.shape[1] // LANE):
        xs = x[:, LANE * s:LANE * (s + 1)]
        lane = lax.broadcasted_iota(jnp.int32, xs.shape, 1)
        outs.append(jnp.where((lane & 32) == 0, pltpu.roll(xs, LANE - 32, axis=1), pltpu.roll(xs, 32, axis=1)))
    return outs[0] if len(outs) == 1 else jnp.concatenate(outs, axis=1)


def _rope(x, cos, sin_signed):
    return x * cos + _swap_halves(x) * sin_signed


def _rope_t(dx, cos, sin_signed):
    return dx * cos + _swap_halves(dx * sin_signed)


def _ret_fwd(proj, cos, sin, gain, name):
    t = proj.shape[0]
    bk = TOK_TILE
    nb = t // bk
    w, wq, wk, mask, g_blk = _ret_consts(bk)

    def body(q_ref, k_ref, v_ref, rg_ref, cos_ref, sin_ref, w_ref, wq_ref, wk_ref, mask_ref, gain_ref,
             opre_ref, og_ref, st_ref, r_ref):
        i = pl.program_id(0)

        @pl.when(i == 0)
        def _():
            r_ref[...] = jnp.zeros_like(r_ref)

        c, s = cos_ref[...], sin_ref[...]
        valid = ((i * bk + lax.broadcasted_iota(jnp.int32, (bk, 1), 0)) >= N_PAD).astype(F32)
        qr = _rope(q_ref[...], c, s)
        kr = _rope(k_ref[...], c, s) * QK_SCALE * valid
        kb = kr.astype(BF16)
        for h in range(RET_HEADS):
            hm = mask_ref[h]
            cols = slice(RET_DV * h, RET_DV * (h + 1))
            vh = v_ref[:, cols].astype(BF16)
            r_prev = r_ref[h]
            st_ref[0, h] = r_prev
            sm = _dot((qr * hm).astype(BF16), kb, NT) * w_ref[h]
            o = _dot(sm.astype(BF16), vh) + _dot((qr * (hm * wq_ref[h])).astype(BF16), r_prev.astype(BF16))
            r_ref[h] = g_blk[h] * r_prev + _dot((kr * wk_ref[h]).astype(BF16), vh, TN)
            opre_ref[:, cols] = o
            rstd = lax.rsqrt(jnp.mean(o * o, axis=-1, keepdims=True) + EPS)
            rg = rg_ref[:, cols]
            og_ref[:, cols] = (o * rstd * gain_ref[:, cols] * (rg * _sigmoid(rg))).astype(BF16)

    full = lambda shape: pl.BlockSpec(shape, lambda i: (0,) * len(shape))
    return _pcall(
        body, name=name, grid=(nb,),
        in_specs=[pl.BlockSpec((bk, RET_QK), lambda i: (i, 0)), pl.BlockSpec((bk, RET_QK), lambda i: (i, 1)),
                  pl.BlockSpec((bk, RET_V), lambda i: (i, 1)), pl.BlockSpec((bk, RET_V), lambda i: (i, 2)),
                  pl.BlockSpec((bk, RET_QK), lambda i: (i, 0)), pl.BlockSpec((bk, RET_QK), lambda i: (i, 0)),
                  full((RET_HEADS, bk, bk)), full((RET_HEADS, bk, 1)), full((RET_HEADS, bk, 1)),
                  full((RET_HEADS, 1, RET_QK)), full((1, RET_V))],
        out_specs=[pl.BlockSpec((bk, RET_V), lambda i: (i, 0)), pl.BlockSpec((bk, RET_V), lambda i: (i, 0)),
                   pl.BlockSpec((1, RET_HEADS, RET_QK, RET_DV), lambda i: (i, 0, 0, 0))],
        out_shape=[jax.ShapeDtypeStruct((t, RET_V), F32), jax.ShapeDtypeStruct((t, RET_V + FOX_W), BF16),
                   jax.ShapeDtypeStruct((nb, RET_HEADS, RET_QK, RET_DV), F32)],
        scratch_shapes=[pltpu.VMEM((RET_HEADS, RET_QK, RET_DV), F32)],
        compiler_params=_params("arbitrary"),
    )(proj, proj, proj, proj, cos, sin, w, wq, wk, mask, gain)


def _ret_bwd(proj, cos, sin, gain, dmixed, opre, states, name):
    t = proj.shape[0]
    bk = TOK_TILE
    nb = t // bk
    w, wq, wk, mask, g_blk = _ret_consts(bk)
    v0, g0 = 2 * RET_QK, 2 * RET_QK + RET_V

    def body(q_ref, k_ref, v_ref, rg_ref, cos_ref, sin_ref, w_ref, wq_ref, wk_ref, mask_ref, gain_ref,
             dog_ref, opre_ref, st_ref, dp_ref, gg_ref, dr_ref):
        step = pl.program_id(0)
        i = nb - 1 - step

        @pl.when(step == 0)
        def _():
            dr_ref[...] = jnp.zeros_like(dr_ref)
            gg_ref[...] = jnp.zeros_like(gg_ref)

        c, s = cos_ref[...], sin_ref[...]
        valid = ((i * bk + lax.broadcasted_iota(jnp.int32, (bk, 1), 0)) >= N_PAD).astype(F32)
        qr = _rope(q_ref[...], c, s)
        kr = _rope(k_ref[...], c, s) * QK_SCALE * valid
        kb = kr.astype(BF16)
        dqr = jnp.zeros((bk, RET_QK), F32)
        dkr = jnp.zeros((bk, RET_QK), F32)
        for h in range(RET_HEADS):
            hm = mask_ref[h]
            cols = slice(RET_DV * h, RET_DV * (h + 1))
            vh = v_ref[:, cols].astype(BF16)
            o = opre_ref[:, cols]
            rstd = lax.rsqrt(jnp.mean(o * o, axis=-1, keepdims=True) + EPS)
            xhat = o * rstd
            rg = rg_ref[:, cols]
            sg = _sigmoid(rg)
            gate = rg * sg
            gn = gain_ref[:, cols]
            dog = dog_ref[:, cols]
            dp_ref[:, g0 + RET_DV * h:g0 + RET_DV * (h + 1)] = (
                dog * xhat * gn * (sg * (1.0 + rg * (1.0 - sg)))).astype(BF16)
            gg_ref[:, cols] += jnp.sum(dog * xhat * gate, axis=0, keepdims=True)
            dxh = dog * gn * gate
            do = (rstd * (dxh - xhat * jnp.mean(dxh * xhat, axis=-1, keepdims=True))).astype(BF16)
            qm = (qr * hm).astype(BF16)
            qw = (qr * (hm * wq_ref[h])).astype(BF16)
            kw = (kr * wk_ref[h]).astype(BF16)
            wh = w_ref[h]
            sm = (_dot(qm, kb, NT) * wh).astype(BF16)
            ds = (_dot(do, vh, NT) * wh).astype(BF16)
            dr = dr_ref[h]
            drb = dr.astype(BF16)
            dp_ref[:, v0 + RET_DV * h:v0 + RET_DV * (h + 1)] = (_dot(sm, do, TN) + _dot(kw, drb)).astype(BF16)
            dqr = dqr + _dot(ds, kb) * hm + _dot(do, st_ref[0, h].astype(BF16), NT) * (hm * wq_ref[h])
            dkr = dkr + _dot(ds, qm, TN) + _dot(vh, drb, NT) * wk_ref[h]
            dr_ref[h] = g_blk[h] * dr + _dot(qw, do, TN)
        dp_ref[:, 0:RET_QK] = _rope_t(dqr, c, s).astype(BF16)
        dp_ref[:, RET_QK:2 * RET_QK] = _rope_t(dkr * (QK_SCALE * valid), c, s).astype(BF16)

    full = lambda shape: pl.BlockSpec(shape, lambda i: (0,) * len(shape))
    rev = lambda col: (lambda i: (nb - 1 - i, col))
    return _pcall(
        body, name=name, grid=(nb,),
        in_specs=[pl.BlockSpec((bk, RET_QK), rev(0)), pl.BlockSpec((bk, RET_QK), rev(1)),
                  pl.BlockSpec((bk, RET_V), rev(1)), pl.BlockSpec((bk, RET_V), rev(2)),
                  pl.BlockSpec((bk, RET_QK), rev(0)), pl.BlockSpec((bk, RET_QK), rev(0)),
                  full((RET_HEADS, bk, bk)), full((RET_HEADS, bk, 1)), full((RET_HEADS, bk, 1)),
                  full((RET_HEADS, 1, RET_QK)), full((1, RET_V)),
                  pl.BlockSpec((bk, RET_V), rev(0)), pl.BlockSpec((bk, RET_V), rev(0)),
                  pl.BlockSpec((1, RET_HEADS, RET_QK, RET_DV), lambda i: (nb - 1 - i, 0, 0, 0))],
        out_specs=[pl.BlockSpec((bk, g0 + RET_V), rev(0)), pl.BlockSpec((1, RET_V), lambda i: (0, 0))],
        out_shape=[jax.ShapeDtypeStruct((t, IN_PAD), BF16), jax.ShapeDtypeStruct((1, RET_V), F32)],
        scratch_shapes=[pltpu.VMEM((RET_HEADS, RET_QK, RET_DV), F32)],
        compiler_params=_params("arbitrary"),
    )(proj, proj, proj, proj, cos, sin, w, wq, wk, mask, gain, dmixed, opre, states)


def _forget_cumsum(proj, bias, name):
    t = proj.shape[0]
    nb = t // ROW_TILE
    tril = jnp.asarray(np.tril(np.ones((ROW_TILE, ROW_TILE))), F32)

    def body(z_ref, b_ref, tril_ref, c_ref, carry_ref):
        i = pl.program_id(0)

        @pl.when(i == 0)
        def _():
            carry_ref[...] = jnp.zeros_like(carry_ref)

        z = z_ref[...] + b_ref[...]
        logf = jnp.minimum(z, 0.0) - jnp.log(1.0 + jnp.exp(-jnp.abs(z)))
        c = lax.dot_general(tril_ref[...], logf, NN, precision=lax.Precision.HIGHEST,
                            preferred_element_type=F32) + carry_ref[...]
        c_ref[...] = c
        carry_ref[...] = c[ROW_TILE - 1:ROW_TILE, :]

    return _pcall(
        body, name=name, grid=(nb,),
        in_specs=[pl.BlockSpec((ROW_TILE, LANE), lambda i: (i, FF_COL_BLOCK)), pl.BlockSpec((1, LANE), lambda i: (0, 0)),
                  pl.BlockSpec((ROW_TILE, ROW_TILE), lambda i: (0, 0))],
        out_specs=pl.BlockSpec((ROW_TILE, LANE), lambda i: (i, 0)),
        out_shape=jax.ShapeDtypeStruct((t, LANE), F32),
        scratch_shapes=[pltpu.VMEM((1, LANE), F32)],
        compiler_params=_params("arbitrary"),
    )(proj, bias, tril)


def _forget_cumsum_bwd(proj, bias, drs, dcs, dproj, name):
    t = proj.shape[0]
    nb = t // ROW_TILE
    triu = jnp.asarray(np.triu(np.ones((ROW_TILE, ROW_TILE))), F32)

    def body(z_ref, b_ref, triu_ref, drs_ref, dcs_ref, dproj_in, dz_ref, gb_ref, carry_ref):
        step = pl.program_id(0)

        @pl.when(step == 0)
        def _():
            carry_ref[...] = jnp.zeros_like(carry_ref)
            gb_ref[...] = jnp.zeros_like(gb_ref)

        dlogf = lax.dot_general(triu_ref[...], drs_ref[...] - dcs_ref[...], NN, precision=lax.Precision.HIGHEST,
                                preferred_element_type=F32) + carry_ref[...]
        carry_ref[...] = dlogf[0:1, :]
        z = z_ref[...] + b_ref[...]
        is_head = lax.broadcasted_iota(jnp.int32, (ROW_TILE, LANE), 1) < FOX_HEADS
        dz = jnp.where(is_head, dlogf / (1.0 + jnp.exp(z)), 0.0)
        dz_ref[...] = dz.astype(BF16)
        gb_ref[...] += jnp.sum(dz, axis=0, keepdims=True)

    return _pcall(
        body, name=name, grid=(nb,),
        in_specs=[pl.BlockSpec((ROW_TILE, LANE), lambda i: (nb - 1 - i, FF_COL_BLOCK)),
                  pl.BlockSpec((1, LANE), lambda i: (0, 0)),
                  pl.BlockSpec((ROW_TILE, ROW_TILE), lambda i: (0, 0)),
                  pl.BlockSpec((ROW_TILE, LANE), lambda i: (nb - 1 - i, 0)),
                  pl.BlockSpec((ROW_TILE, LANE), lambda i: (nb - 1 - i, 0)),
                  pl.BlockSpec(memory_space=pl.ANY)],
        out_specs=[pl.BlockSpec((ROW_TILE, LANE), lambda i: (nb - 1 - i, FF_COL_BLOCK)),
                   pl.BlockSpec((1, LANE), lambda i: (0, 0))],
        out_shape=[jax.ShapeDtypeStruct(dproj.shape, BF16), jax.ShapeDtypeStruct((1, LANE), F32)],
        input_output_aliases={5: 0},
        scratch_shapes=[pltpu.VMEM((1, LANE), F32)],
        compiler_params=_params("arbitrary"),
    )(proj, bias, triu, drs, dcs, dproj)


FOX_PAIRS = FOX_HEADS // 2
L_ONE_Q = FOX_DH
L_ONE_K = FOX_DH + 3
L_LSE = FOX_DH + 4


def _split3(x):
    hi = x.astype(BF16).astype(F32)
    r = x - hi
    mid = r.astype(BF16).astype(F32)
    return hi, mid, r - mid


def _head_to_low(slab, e):
    return slab if e == 0 else pltpu.roll(slab, FOX_DH, axis=1)


def _pair(a, b, low):
    return jnp.where(low, a, pltpu.roll(b, FOX_DH, axis=1))


def _fox_prep(proj, c, name):
    t = proj.shape[0]
    tq = TOK_TILE

    def body(p_ref, c_ref, qa_ref, ka_ref, va_ref):
        i = pl.program_id(0)
        lane = lax.broadcasted_iota(jnp.int32, (tq, LANE), 1)
        low = lane < FOX_DH
        live = (i * tq + lax.broadcasted_iota(jnp.int32, (tq, 1), 0)) >= N_PAD
        q_tail = jnp.where(lane < L_ONE_Q + 3, 1.0, 0.0)
        k_ones = (lane >= L_ONE_K) & (lane < L_ONE_K + 4)
        v_tail = jnp.where(lane < FOX_DH + 2, 1.0, 0.0)
        for pair in range(FOX_PAIRS):
            base = 3 * LANE * pair
            for e in range(2):
                h = 2 * pair + e
                q = _head_to_low(p_ref[:, base:base + LANE], e)
                k = _head_to_low(p_ref[:, base + LANE:base + 2 * LANE], e)
                v = _head_to_low(p_ref[:, base + 2 * LANE:base + 3 * LANE], e)
                hi, mid, lo = _split3(jnp.where(live, -c_ref[:, h:h + 1], NEG))
                ka = jnp.where(low, k, jnp.where(k_ones, 1.0, 0.0))
                ka = jnp.where(lane == L_ONE_Q, hi, jnp.where(lane == L_ONE_Q + 1, mid, jnp.where(lane == L_ONE_Q + 2, lo, ka)))
                qa_ref[h] = jnp.where(low, q * QK_SCALE, q_tail).astype(BF16)
                ka_ref[h] = ka.astype(BF16)
                va_ref[h] = jnp.where(low, v, v_tail).astype(BF16)

    out = jax.ShapeDtypeStruct((FOX_HEADS, t, LANE), BF16)
    ospec = pl.BlockSpec((FOX_HEADS, tq, LANE), lambda i: (0, i, 0))
    return _pcall(
        body, name=name, grid=(t // tq,),
        in_specs=[pl.BlockSpec((tq, 3 * FOX_W), lambda i: (i, 1)), pl.BlockSpec((tq, LANE), lambda i: (i, 0))],
        out_specs=[ospec, ospec, ospec], out_shape=[out, out, out],
        compiler_params=_params("parallel"),
    )(proj, c)


STEP_PAIRS = 2
STEP_HEADS = 2 * STEP_PAIRS
FOX_GROUPS = FOX_PAIRS // STEP_PAIRS


def _blockdiag(a, b):
    z = jnp.zeros_like(a)
    return jnp.concatenate([jnp.concatenate([a, z], axis=1), jnp.concatenate([z, b], axis=1)], axis=0)


def _fox_fwd(qa, ka, va, mixed, name):
    nh, nq, tq, _ = qa.shape
    t = nq * tq

    def body(qa_ref, ka_ref, va_ref, mixed_in, mixed_ref, o_ref, lse_ref):
        i = pl.program_id(1)
        lane = lax.broadcasted_iota(jnp.int32, (tq, LANE), 1)
        causal = lax.broadcasted_iota(jnp.int32, (tq, tq), 1) <= lax.broadcasted_iota(jnp.int32, (tq, tq), 0)
        qps = [jnp.concatenate([qa_ref[2 * c], qa_ref[2 * c + 1]], axis=1) for c in range(STEP_PAIRS)]

        def step(j, carry, diagonal):
            scores = [_dot(qps[c], _blockdiag(ka_ref[2 * c, j], ka_ref[2 * c + 1, j]), NT) for c in range(STEP_PAIRS)]
            new = []
            for c in range(STEP_PAIRS):
                ms, acc = carry[c]
                ps, ms_new, alphas = [], [], []
                for e in range(2):
                    s = scores[c][:, e * tq:(e + 1) * tq]
                    if diagonal:
                        s = jnp.where(causal, s, NEG)
                    m_new = jnp.maximum(ms[e], jnp.max(s, axis=-1, keepdims=True))
                    ps.append(jnp.exp(s - m_new).astype(BF16))
                    ms_new.append(m_new)
                    alphas.append(jnp.broadcast_to(jnp.exp(ms[e] - m_new), (tq, LANE)))
                pv = _dot(jnp.concatenate(ps, axis=1), _blockdiag(va_ref[2 * c, j], va_ref[2 * c + 1, j]))
                new.append((tuple(ms_new), jnp.concatenate(alphas, axis=1) * acc + pv))
            return tuple(new)

        m0 = jnp.full((tq, 1), NEG, F32)
        init = tuple(((m0, m0), jnp.zeros((tq, 2 * LANE), F32)) for _ in range(STEP_PAIRS))
        carry = lax.fori_loop(0, i, lambda j, cr: step(j, cr, False), init)
        o_pairs = []
        lse = jnp.zeros((tq, LANE), F32)
        for c, (ms, acc) in enumerate(step(i, carry, True)):
            outs = []
            for e in range(2):
                half = acc[:, e * LANE:(e + 1) * LANE]
                l = half[:, FOX_DH:FOX_DH + 1]
                outs.append(half / l)
                lse = jnp.where(lane == 2 * c + e, ms[e] + jnp.log(l), lse)
            o_pairs.append(_pair(outs[0], outs[1], lane < FOX_DH))
        o_all = jnp.concatenate(o_pairs, axis=1)
        mixed_ref[...] = o_all.astype(BF16)
        o_ref[...] = o_all
        lse_ref[...] = lse

    width = STEP_PAIRS * LANE
    whole = pl.BlockSpec((STEP_HEADS, nq, tq, LANE), lambda g, i: (g, 0, 0, 0), pipeline_mode=pl.Buffered(1))
    return _pcall(
        body, name=name, grid=(FOX_GROUPS, nq),
        in_specs=[pl.BlockSpec((STEP_HEADS, None, tq, LANE), lambda g, i: (g, i, 0, 0)), whole, whole,
                  pl.BlockSpec(memory_space=pl.ANY)],
        out_specs=[pl.BlockSpec((tq, width), lambda g, i: (i, RET_V // width + g)),
                   pl.BlockSpec((tq, width), lambda g, i: (i, g)),
                   pl.BlockSpec((None, tq, LANE), lambda g, i: (g, i, 0))],
        out_shape=[jax.ShapeDtypeStruct(mixed.shape, BF16), jax.ShapeDtypeStruct((t, FOX_W), F32),
                   jax.ShapeDtypeStruct((FOX_GROUPS, t, LANE), F32)],
        input_output_aliases={3: 0},
        compiler_params=_params("parallel", "parallel"),
    )(qa, ka, va, mixed)


def _fox_prep_bwd(dmixed, o_fox, lse, qa, name):
    t = dmixed.shape[0]
    tq = TOK_TILE

    def body(dm_ref, o_ref, lse_ref, qa_ref, qab_ref, doa_ref):
        i = pl.program_id(0)
        lane = lax.broadcasted_iota(jnp.int32, (tq, LANE), 1)
        low = lane < FOX_DH
        live = (i * tq + lax.broadcasted_iota(jnp.int32, (tq, 1), 0)) >= N_PAD
        for pair in range(FOX_PAIRS):
            cols = slice(LANE * pair, LANE * (pair + 1))
            d_slab = dm_ref[:, cols]
            prod = d_slab * o_ref[:, cols]
            for e in range(2):
                h = 2 * pair + e
                nd = -jnp.sum(jnp.where(low, _head_to_low(prod, e), 0.0), axis=-1, keepdims=True)
                nd_hi = nd.astype(BF16).astype(F32)
                doa = jnp.where(low, _head_to_low(d_slab, e), 0.0)
                doa = jnp.where(lane == FOX_DH, nd_hi, jnp.where(lane == FOX_DH + 1, nd - nd_hi, doa))
                doa_ref[h] = doa.astype(BF16)
                lse_h = lse_ref[h // STEP_HEADS][:, h % STEP_HEADS:h % STEP_HEADS + 1]
                hi, mid, lo = _split3(jnp.where(live, -lse_h, 0.0))
                qab = qa_ref[h].astype(F32)
                qab = jnp.where(lane == L_LSE, hi, jnp.where(lane == L_LSE + 1, mid, jnp.where(lane == L_LSE + 2, lo, qab)))
                qab_ref[h] = qab.astype(BF16)

    out = jax.ShapeDtypeStruct((FOX_HEADS, t, LANE), BF16)
    hspec = pl.BlockSpec((FOX_HEADS, tq, LANE), lambda i: (0, i, 0))
    return _pcall(
        body, name=name, grid=(t // tq,),
        in_specs=[pl.BlockSpec((tq, FOX_W), lambda i: (i, 1)), pl.BlockSpec((tq, FOX_W), lambda i: (i, 0)),
                  pl.BlockSpec((FOX_GROUPS, tq, LANE), lambda i: (0, i, 0)), hspec],
        out_specs=[hspec, hspec], out_shape=[out, out],
        compiler_params=_params("parallel"),
    )(dmixed, o_fox, lse, qa)


def _fox_bwd(qab, doa, ka, va, dproj, name):
    nh, nq, tq, _ = qab.shape
    t = nq * tq
    slab = 3 * LANE * STEP_PAIRS
    group0 = (2 * RET_QK + 2 * RET_V) // slab

    def body(qab_ref, doa_ref, ka_ref, va_ref, dproj_in, dp_ref, drs_ref, dcs_ref, dq_ref):
        g, j = pl.program_id(0), pl.program_id(1)

        @pl.when((g == 0) & (j == 0))
        def _():
            drs_ref[...] = jnp.zeros_like(drs_ref)
            dcs_ref[...] = jnp.zeros_like(dcs_ref)

        @pl.when(j == 0)
        def _():
            dq_ref[...] = jnp.zeros_like(dq_ref)

        lane = lax.broadcasted_iota(jnp.int32, (tq, LANE), 1)
        low = lane < FOX_DH
        key_le_query = lax.broadcasted_iota(jnp.int32, (tq, tq), 0) <= lax.broadcasted_iota(jnp.int32, (tq, tq), 1)

        def by_head(c, a, b, col):
            h = STEP_HEADS * g + 2 * c
            return jnp.where(lane == h, a[:, col:col + 1], jnp.where(lane == h + 1, b[:, col:col + 1], 0.0))

        kbs = [ka_ref[h] for h in range(STEP_HEADS)]
        vbs = [va_ref[h] for h in range(STEP_HEADS)]

        def step(i, carry, diagonal):
            qbs = [qab_ref[h, i] for h in range(STEP_HEADS)]
            dobs = [doa_ref[h, i] for h in range(STEP_HEADS)]
            st = [_dot(kbs[h], qbs[h], NT) for h in range(STEP_HEADS)]
            dpt = [_dot(vbs[h], dobs[h], NT) for h in range(STEP_HEADS)]
            new = []
            for h in range(STEP_HEADS):
                p = jnp.exp(st[h])
                if diagonal:
                    p = jnp.where(key_le_query, p, 0.0)
                ds = (p * dpt[h]).astype(BF16)
                dq_ref[h, i] += _dot(ds, kbs[h], TN)
                dk, dv = carry[h]
                new.append((dk + _dot(ds, qbs[h]), dv + _dot(p.astype(BF16), dobs[h])))
            return tuple(new)

        zero = jnp.zeros((tq, LANE), F32)
        carry = step(j, tuple((zero, zero) for _ in range(STEP_HEADS)), True)
        carry = lax.fori_loop(j + 1, nq, lambda i, cr: step(i, cr, False), carry)
        rows = pl.ds(pl.multiple_of(j * tq, tq), tq)
        for c in range(STEP_PAIRS):
            (dka, dva), (dkb, dvb) = carry[2 * c], carry[2 * c + 1]
            c0 = 3 * LANE * c
            dp_ref[rows, c0 + LANE:c0 + 2 * LANE] = _pair(dka, dkb, low).astype(BF16)
            dp_ref[rows, c0 + 2 * LANE:c0 + 3 * LANE] = _pair(dva, dvb, low).astype(BF16)
            dcs_ref[rows, :] += by_head(c, dka, dkb, L_ONE_Q)

        @pl.when(j == nq - 1)
        def _():
            for c in range(STEP_PAIRS):
                for blk in range(nq):
                    r = slice(blk * tq, (blk + 1) * tq)
                    a, b = dq_ref[2 * c, blk], dq_ref[2 * c + 1, blk]
                    dp_ref[r, 3 * LANE * c:3 * LANE * c + LANE] = (_pair(a, b, low) * QK_SCALE).astype(BF16)
                    drs_ref[r, :] += by_head(c, a, b, L_ONE_K)

    whole = pl.BlockSpec((STEP_HEADS, nq, tq, LANE), lambda g, j: (g, 0, 0, 0), pipeline_mode=pl.Buffered(1))
    blk = pl.BlockSpec((STEP_HEADS, None, tq, LANE), lambda g, j: (g, j, 0, 0))
    sums = pl.BlockSpec((t, LANE), lambda g, j: (0, 0), pipeline_mode=pl.Buffered(1))
    return _pcall(
        body, name=name, grid=(FOX_GROUPS, nq),
        in_specs=[whole, whole, blk, blk, pl.BlockSpec(memory_space=pl.ANY)],
        out_specs=[pl.BlockSpec((t, slab), lambda g, j: (0, group0 + g)), sums, sums],
        out_shape=[jax.ShapeDtypeStruct(dproj.shape, BF16), jax.ShapeDtypeStruct((t, LANE), F32),
                   jax.ShapeDtypeStruct((t, LANE), F32)],
        input_output_aliases={4: 0},
        scratch_shapes=[pltpu.VMEM((STEP_HEADS, nq, tq, LANE), F32)],
        compiler_params=_params("arbitrary", "arbitrary"),
    )(qab, doa, ka, va, dproj)


HALO = 8


def _rows_ext(ref, r0, rows, t, before, after):
    lo, hi = r0 - before, r0 + rows + after
    parts = []
    if lo < 0:
        parts.append(jnp.zeros((-lo, LANE), F32))
    parts.append(ref[max(lo, 0):min(hi, t), :].astype(F32))
    if hi > t:
        parts.append(jnp.zeros((hi - t, LANE), F32))
    return parts[0] if len(parts) == 1 else jnp.concatenate(parts, axis=0)


def _conv_taps(a_ext, r0_ext, cw_ref, cb_ref):
    n = a_ext.shape[0]
    if r0_ext < N_PAD:
        row = r0_ext + lax.broadcasted_iota(jnp.int32, (n, 1), 0)
        a_ext = jnp.where(row >= N_PAD, a_ext, 0.0)
    a1 = pltpu.roll(a_ext, 1, axis=0)
    a2 = pltpu.roll(a_ext, 2, axis=0)
    acc = cb_ref[...] + a2 * cw_ref[0:1, :] + a1 * cw_ref[1:2, :] + a_ext * cw_ref[2:3, :]
    return a_ext, a1, a2, acc


def _conv_gate_fwd(up, conv_w8, conv_b, name):
    _, t, f = up.shape
    rows = TOK_TILE

    def body(a_ref, b_ref, cw_ref, cb_ref, g_ref):
        for r0 in range(0, t, rows):
            a_ext = _rows_ext(a_ref, r0, rows, t, HALO, 0)
            _, _, _, acc = _conv_taps(a_ext, r0 - HALO, cw_ref, cb_ref)
            acc = acc[HALO:, :]
            g_ref[r0:r0 + rows, :] = (acc * _sigmoid(acc) * b_ref[r0:r0 + rows, :]).astype(BF16)

    return _pcall(
        body, name=name, grid=(f // LANE,),
        in_specs=[pl.BlockSpec((None, t, LANE), lambda j: (0, 0, j)), pl.BlockSpec((None, t, LANE), lambda j: (1, 0, j)),
                  pl.BlockSpec((8, LANE), lambda j: (0, j)), pl.BlockSpec((1, LANE), lambda j: (0, j))],
        out_specs=pl.BlockSpec((t, LANE), lambda j: (0, j)),
        out_shape=jax.ShapeDtypeStruct((t, f), BF16),
        compiler_params=_params("parallel"),
    )(up, up, conv_w8, conv_b)


def _conv_gate_bwd(up, conv_w8, conv_b, dg, name):
    _, t, f = up.shape
    rows = TOK_TILE

    def body(a_ref, b_ref, cw_ref, cb_ref, dg_ref, dup_ref, gcw_ref, gcb_ref):
        gw = [jnp.zeros((1, LANE), F32) for _ in range(3)]
        gb = jnp.zeros((1, LANE), F32)
        for r0 in range(0, t, rows):
            a_ext = _rows_ext(a_ref, r0, rows, t, HALO, HALO)
            b_ext = _rows_ext(b_ref, r0, rows, t, HALO, HALO)
            dg_ext = _rows_ext(dg_ref, r0, rows, t, HALO, HALO)
            a0, a1, a2, acc = _conv_taps(a_ext, r0 - HALO, cw_ref, cb_ref)
            sg = _sigmoid(acc)
            dacc = dg_ext * b_ext * (sg * (1.0 + acc * (1.0 - sg)))
            n = dacc.shape[0]
            da = (dacc * cw_ref[2:3, :] + pltpu.roll(dacc, n - 1, axis=0) * cw_ref[1:2, :]
                  + pltpu.roll(dacc, n - 2, axis=0) * cw_ref[0:1, :])
            core = slice(HALO, HALO + rows)
            da = da[core, :]
            if r0 < N_PAD:
                row = r0 + lax.broadcasted_iota(jnp.int32, (rows, 1), 0)
                da = jnp.where(row >= N_PAD, da, 0.0)
            dup_ref[0, r0:r0 + rows, :] = da.astype(BF16)
            dup_ref[1, r0:r0 + rows, :] = (dg_ext * acc * sg)[core, :].astype(BF16)
            dacc_c = dacc[core, :]
            gw[0] = gw[0] + jnp.sum(dacc_c * a2[core, :], axis=0, keepdims=True)
            gw[1] = gw[1] + jnp.sum(dacc_c * a1[core, :], axis=0, keepdims=True)
            gw[2] = gw[2] + jnp.sum(dacc_c * a0[core, :], axis=0, keepdims=True)
            gb = gb + jnp.sum(dacc_c, axis=0, keepdims=True)
        gcw_ref[...] = jnp.zeros((8, LANE), F32)
        for tap in range(3):
            gcw_ref[tap:tap + 1, :] = gw[tap]
        gcb_ref[...] = gb

    return _pcall(
        body, name=name, grid=(f // LANE,),
        in_specs=[pl.BlockSpec((None, t, LANE), lambda j: (0, 0, j)), pl.BlockSpec((None, t, LANE), lambda j: (1, 0, j)),
                  pl.BlockSpec((8, LANE), lambda j: (0, j)), pl.BlockSpec((1, LANE), lambda j: (0, j)),
                  pl.BlockSpec((t, LANE), lambda j: (0, j))],
        out_specs=[pl.BlockSpec((2, t, LANE), lambda j: (0, 0, j)), pl.BlockSpec((8, LANE), lambda j: (0, j)),
                   pl.BlockSpec((1, LANE), lambda j: (0, j))],
        out_shape=[jax.ShapeDtypeStruct((2, t, f), BF16), jax.ShapeDtypeStruct((8, f), F32),
                   jax.ShapeDtypeStruct((1, f), F32)],
        compiler_params=_params("parallel"),
    )(up, up, conv_w8, conv_b, dg)


def _exchange(arrays, kinds, name):
    n = len(arrays)
    npeer = N_DEV - 1

    def body(*refs):
        ins, outs = refs[:n], refs[n:2 * n]
        send_sems, recv_sems, local_sems = refs[2 * n:]
        x, y, c = lax.axis_index("x"), lax.axis_index("y"), lax.axis_index("c")
        me = 4 * x + 2 * y + c
        copies, locals_ = [], []
        for a in range(n):
            gather = kinds[a] == "gather"
            own = pltpu.make_async_copy(ins[a] if gather else ins[a].at[me], outs[a].at[me], local_sems.at[a])
            own.start()
            locals_.append(own)
            for d in range(1, N_DEV):
                px = 1 - x if d & 4 else x
                py = 1 - y if d & 2 else y
                pc = 1 - c if d & 1 else c
                src = ins[a] if gather else ins[a].at[4 * px + 2 * py + pc]
                cp = pltpu.make_async_remote_copy(
                    src_ref=src, dst_ref=outs[a].at[me],
                    send_sem=send_sems.at[a * npeer + d - 1], recv_sem=recv_sems.at[a * npeer + d - 1],
                    device_id=(px, py, pc), device_id_type=pl.DeviceIdType.MESH)
                cp.start()
                copies.append(cp)
        for cp in copies:
            cp.wait_recv()
        for cp in copies:
            cp.wait_send()
        for own in locals_:
            own.wait()

    out_shape = [jax.ShapeDtypeStruct((N_DEV,) + (a.shape if k == "gather" else a.shape[1:]), a.dtype)
                 for a, k in zip(arrays, kinds)]
    return _pcall(
        body, name=name,
        in_specs=[pl.BlockSpec(memory_space=pl.ANY)] * n,
        out_specs=[pl.BlockSpec(memory_space=pl.ANY)] * n,
        out_shape=out_shape,
        scratch_shapes=[pltpu.SemaphoreType.DMA((n * npeer,)), pltpu.SemaphoreType.DMA((n * npeer,)),
                        pltpu.SemaphoreType.DMA((n,))],
        compiler_params=pltpu.CompilerParams(has_side_effects=True),
    )(*arrays)


def _peer_copies(srcs, lands, kinds, send_sems, recv_sems):
    x, y, c = lax.axis_index("x"), lax.axis_index("y"), lax.axis_index("c")
    me = 4 * x + 2 * y + c
    copies = []
    for a in range(len(srcs)):
        for d in range(1, N_DEV):
            px = 1 - x if d & 4 else x
            py = 1 - y if d & 2 else y
            pc = 1 - c if d & 1 else c
            k = a * (N_DEV - 1) + d - 1
            copies.append(pltpu.make_async_remote_copy(
                src_ref=srcs[a] if kinds[a] == "gather" else srcs[a].at[4 * px + 2 * py + pc], dst_ref=lands[a].at[me],
                send_sem=send_sems.at[k], recv_sem=recv_sems.at[k],
                device_id=(px, py, pc), device_id_type=pl.DeviceIdType.MESH))
    return copies


def _exchange_start(arrays, kinds, name):
    n = len(arrays)
    nsem = n * (N_DEV - 1)
    hbm = pl.BlockSpec(memory_space=pltpu.HBM)
    sem = pl.BlockSpec(memory_space=pltpu.SEMAPHORE)
    land_shapes = [(N_DEV,) + (a.shape if k == "gather" else a.shape[1:]) for a, k in zip(arrays, kinds)]

    def body(*refs):
        srcs, lands = refs[:n], refs[n:2 * n]
        send_sems, recv_sems = refs[2 * n], refs[2 * n + 1]
        token = refs[-1]
        for cp in _peer_copies(srcs, lands, kinds, send_sems, recv_sems):
            cp.start()
        token[...] = jnp.zeros_like(token)

    operands = [pltpu.with_memory_space_constraint(a, pltpu.HBM) for a in arrays]
    operands += [pltpu.with_memory_space_constraint(lax.empty(s, a.dtype), pltpu.HBM) for s, a in zip(land_shapes, arrays)]
    out = _pcall(
        body, name=name,
        in_specs=[hbm] * (2 * n),
        out_specs=[sem, sem] + [hbm] * (2 * n) + [pl.BlockSpec(memory_space=pltpu.VMEM)],
        out_shape=[pltpu.SemaphoreType.DMA((nsem,)), pltpu.SemaphoreType.DMA((nsem,))]
        + [pltpu.HBM(a.shape, a.dtype) for a in arrays]
        + [pltpu.HBM(s, a.dtype) for s, a in zip(land_shapes, arrays)]
        + [jax.ShapeDtypeStruct((8, LANE), F32)],
        input_output_aliases={k: 2 + k for k in range(2 * n)},
        compiler_params=pltpu.CompilerParams(has_side_effects=pltpu.SideEffectType.DATAFLOW_SIDE_EFFECTING),
    )(*operands)
    return out[0], out[1], list(out[2:2 + n]), list(out[2 + n:2 + 2 * n]), out[-1]


def _exchange_wait(started, kinds, after, name):
    send_sems, recv_sems, srcs, lands, _ = started
    n = len(srcs)
    hbm = pl.BlockSpec(memory_space=pltpu.HBM)
    sem = pl.BlockSpec(memory_space=pltpu.SEMAPHORE)

    def body(*refs):
        src_refs, land_refs = refs[:n], refs[n:2 * n]
        copies = _peer_copies(src_refs, land_refs, kinds, refs[2 * n], refs[2 * n + 1])
        for cp in copies:
            cp.wait_send()
        for cp in copies:
            cp.wait_recv()

    out = _pcall(
        body, name=name,
        in_specs=[hbm] * (2 * n) + [sem, sem, pl.BlockSpec(memory_space=pl.ANY)],
        out_specs=[hbm] * (2 * n),
        out_shape=[pltpu.HBM(a.shape, a.dtype) for a in srcs + lands],
        input_output_aliases={k: k for k in range(2 * n)},
        compiler_params=pltpu.CompilerParams(has_side_effects=pltpu.SideEffectType.DATAFLOW_SIDE_EFFECTING),
    )(*srcs, *lands, send_sems, recv_sems, after)
    return list(out[n:])


def _with_own_slot(land, own, me):
    return lax.dynamic_update_slice(land, own[None], (me,) + (0,) * own.ndim)


def _sum_slots(slots, name, rows_tile):
    nd, r, c = slots.shape

    def body(s_ref, o_ref):
        acc = s_ref[0].astype(F32)
        for p in range(1, nd):
            acc = acc + s_ref[p].astype(F32)
        o_ref[...] = acc

    return _pcall(
        body, name=name, grid=(r // rows_tile,),
        in_specs=[pl.BlockSpec((nd, rows_tile, c), lambda i: (0, i, 0))],
        out_specs=pl.BlockSpec((rows_tile, c), lambda i: (i, 0)),
        out_shape=jax.ShapeDtypeStruct((r, c), F32),
        compiler_params=_params("parallel"),
    )(slots)


def _adamw(w, g, m, v, name, rows_tile):
    r, c = w.shape

    def body(w_ref, g_ref, m_ref, v_ref, d_ref, nm_ref, nv_ref):
        gr = g_ref[...]
        nm = ADAM_B1 * m_ref[...] + (1.0 - ADAM_B1) * gr
        nv = ADAM_B2 * v_ref[...] + (1.0 - ADAM_B2) * (gr * gr)
        m_hat = nm / (1.0 - ADAM_B1 ** ADAM_STEP)
        v_hat = nv / (1.0 - ADAM_B2 ** ADAM_STEP)
        d_ref[...] = -ADAM_LR * (m_hat / (jnp.sqrt(v_hat) + ADAM_EPS) + ADAM_WD * w_ref[...])
        nm_ref[...] = nm
        nv_ref[...] = nv

    spec = pl.BlockSpec((rows_tile, c), lambda i: (i, 0))
    shp = jax.ShapeDtypeStruct((r, c), F32)
    return _pcall(
        body, name=name, grid=(r // rows_tile,), in_specs=[spec] * 4, out_specs=[spec] * 3, out_shape=[shp] * 3,
        compiler_params=_params("parallel"),
    )(w, g, m, v)


F0 = 2 * RET_QK + 2 * RET_V


def _to_internal_rows(w_t):
    parts = [w_t[:F0]]
    for pair in range(FOX_PAIRS):
        for group in range(3):
            lo = F0 + group * FOX_W + LANE * pair
            parts.append(w_t[lo:lo + LANE])
    parts.append(w_t[F0 + 3 * FOX_W:])
    parts.append(jnp.zeros((IN_PAD - IN_WIDTH, w_t.shape[1]), w_t.dtype))
    return jnp.concatenate(parts, axis=0)


def _from_internal_rows(g_t):
    parts = [g_t[:F0]]
    for group in range(3):
        for pair in range(FOX_PAIRS):
            lo = F0 + 3 * LANE * pair + LANE * group
            parts.append(g_t[lo:lo + LANE])
    parts.append(g_t[F0 + 3 * FOX_W:F0 + 3 * FOX_W + FOX_HEADS])
    return jnp.concatenate(parts, axis=0)


def _local_step(x, target, meta, attn_g, w_in_t, fox_b, ret_g, ffn_g, conv_w8, conv_b, final_g,
                late_weights, ffn_grads_ready, out_grad_ready):
    seq, d = x.shape
    t = seq + PREFIX
    tm = TOK_TILE
    nq = t // tm
    fox_b128 = jnp.pad(fox_b, ((0, 0), (0, LANE - FOX_HEADS)))

    h0, n1 = _prep_norm(x, meta, attn_g, "prep_norm")
    proj = _mm_simple(n1, w_in_t, mode="nt", tm=tm, tn=640, tk=d, out_dtype=F32, name="mm_in")
    cos, sin = _rope_tables(t)
    o_pre, mixed, states = _ret_fwd(proj, cos, sin, ret_g, "ret_fwd")
    c = _forget_cumsum(proj, fox_b128, "forget_cumsum")
    qa, ka, va = _fox_prep(proj, c, "fox_prep")
    by_block = lambda a: a.reshape(FOX_HEADS, nq, tm, LANE)
    mixed, o_fox, lse = _fox_fwd(by_block(qa), by_block(ka), by_block(va), mixed, "fox_fwd")
    w_out, w_up_t, w_down = late_weights(o_fox)
    h1 = _mm_simple(mixed, w_out, mode="nn", tm=tm, tn=d, tk=d, out_dtype=F32, name="mm_out", add=h0)
    n2 = _rmsnorm(h1, ffn_g, "ffn_norm")
    nf = D_FF // 1408
    up = _matmul(
        n2, w_up_t, mode="nt", grid=(nq, 2 * nf, 1),
        a_spec=pl.BlockSpec((tm, d), lambda i, j, k: (i, 0)),
        b_spec=pl.BlockSpec((None, 1408, d), lambda i, j, k: (j // nf, j % nf, 0)),
        o_spec=pl.BlockSpec((None, tm, 1408), lambda i, j, k: (j // nf, i, j % nf)),
        out_shape=jax.ShapeDtypeStruct((2, t, D_FF), F32), name="mm_up")
    g = _conv_gate_fwd(up, conv_w8, conv_b, "conv_gate_fwd")
    h2 = _mm_simple(g, w_down, mode="nn", tm=tm, tn=d, tk=D_FF, out_dtype=F32, name="mm_down", add=h1)

    loss_tile, dh2, g_final = _loss_bwd(h2, target, final_g, "loss_bwd")
    dg = _mm_simple(dh2, w_down, mode="nt", tm=tm, tn=1408, tk=d, out_dtype=F32, name="mm_dg")
    gw_down = _mm_simple(g, dh2, mode="tn", tm=1408, tn=d, tk=tm, out_dtype=BF16, name="mm_gw_down")
    dup, g_conv_w8, g_conv_b = _conv_gate_bwd(up, conv_w8, conv_b, dg, "conv_gate_bwd")
    dn2 = _matmul(
        dup, w_up_t, mode="nn", grid=(nq, 1, 2 * nf),
        a_spec=pl.BlockSpec((None, tm, 1408), lambda i, j, k: (k // nf, i, k % nf)),
        b_spec=pl.BlockSpec((None, 1408, d), lambda i, j, k: (k // nf, k % nf, 0)),
        o_spec=pl.BlockSpec((tm, d), lambda i, j, k: (i, 0)),
        out_shape=jax.ShapeDtypeStruct((t, d), F32), name="mm_dn2")
    gw_up_t = _matmul(
        dup, n2, mode="tn", grid=(2 * nf, 1, nq),
        a_spec=pl.BlockSpec((None, tm, 1408), lambda i, j, k: (i // nf, k, i % nf)),
        b_spec=pl.BlockSpec((tm, d), lambda i, j, k: (k, 0)),
        o_spec=pl.BlockSpec((1408, d), lambda i, j, k: (i, 0)),
        out_shape=jax.ShapeDtypeStruct((2 * D_FF, d), BF16), name="mm_gw_up")
    dh1, g_ffn = _rmsnorm_bwd(dn2, h1, ffn_g + ffn_grads_ready(gw_down, gw_up_t), dh2, "ffn_norm_bwd")

    dmixed = _mm_simple(dh1, w_out, mode="nt", tm=tm, tn=d, tk=d, out_dtype=F32, name="mm_dmixed")
    gw_out = _mm_simple(mixed, dh1, mode="tn", tm=512, tn=d, tk=tm, out_dtype=BF16, name="mm_gw_out")
    dproj, g_ret = _ret_bwd(proj, cos, sin, ret_g + out_grad_ready(gw_out), dmixed, o_pre, states, "ret_bwd")
    qab, doa = _fox_prep_bwd(dmixed, o_fox, lse, qa, "fox_prep_bwd")
    dproj, drs, dcs = _fox_bwd(by_block(qab), by_block(doa), by_block(ka), by_block(va), dproj, "fox_bwd")
    dproj, g_fox_b = _forget_cumsum_bwd(proj, fox_b128, drs, dcs, dproj, "forget_cumsum_bwd")
    dn1 = _mm_simple(dproj, w_in_t, mode="nn", tm=tm, tn=d, tk=640, out_dtype=F32, name="mm_dn1")
    gw_in_t = _mm_simple(dproj, n1, mode="tn", tm=640, tn=d, tk=tm, out_dtype=BF16, name="mm_gw_in")
    dh0, g_attn = _rmsnorm_bwd(dn1, h0, attn_g, dh1, "attn_norm_bwd")

    grads = dict(meta=dh0[N_PAD:PREFIX], attn_g=g_attn, w_in_t=gw_in_t, fox_b=g_fox_b[:, :FOX_HEADS], ret_g=g_ret,
                 ffn_g=g_ffn, conv_w=g_conv_w8[:3], conv_b=g_conv_b, final_g=g_final)
    return loss_tile, dh0[PREFIX:], grads


def _pack(pieces, width=LANE):
    rows = []
    for p in pieces:
        flat = p.reshape(-1)
        pad = (-flat.shape[0]) % width
        rows.append(jnp.pad(flat, (0, pad)).reshape(-1, width))
    out = jnp.concatenate(rows, axis=0)
    return jnp.pad(out, ((0, (-out.shape[0]) % 8), (0, 0)))


def _unpack(packed, shapes, width=LANE):
    outs, r = [], 0
    for shp in shapes:
        size = int(np.prod(shp))
        nrows = -(-size // width)
        outs.append(packed[r:r + nrows].reshape(-1)[:size].reshape(shp))
        r += nrows
    return outs


def kernel(x, meta_tokens, attn_norm_g, w_in, fox_forget_b, ret_norm_g, w_out, ffn_norm_g, w_up, conv_w, conv_b, w_down, final_norm_g, loss_target, m_meta_tokens, m_attn_norm_g, m_w_in, m_fox_forget_b, m_ret_norm_g, m_w_out, m_ffn_norm_g, m_w_up, m_conv_w, m_conv_b, m_w_down, m_final_norm_g, v_meta_tokens, v_attn_norm_g, v_w_in, v_fox_forget_b, v_ret_norm_g, v_w_out, v_ffn_norm_g, v_w_up, v_conv_w, v_conv_b, v_w_down, v_final_norm_g):
    d = D_MODEL
    me = 4 * lax.axis_index("x") + 2 * lax.axis_index("y") + lax.axis_index("c")
    in_blk = IN_WIDTH // N_DEV
    in_blk_pad = 400
    up_blk = 2 * D_FF // N_DEV
    down_blk = D_FF // N_DEV
    cw_blk = D_FF // N_DEV

    w_in_loc = jnp.pad(w_in[0].T.astype(BF16), ((0, in_blk_pad - in_blk), (0, 0)))
    cw_loc = jnp.pad(conv_w[0], ((0, 5), (0, 384 - cw_blk)))
    g_in, g_meta, g_cw = _exchange([w_in_loc, meta_tokens, cw_loc], ["gather"] * 3, "gather_first")
    rest_loc = [w_out[0].astype(BF16), w_up[0].T.astype(BF16), w_down[0].astype(BF16)]
    rest = _exchange_start(rest_loc, ["gather"] * 3, "gather_rest_start")
    w_in_t = _to_internal_rows(g_in[:, :in_blk].reshape(IN_WIDTH, d))
    meta_f = g_meta.transpose(1, 0, 2).reshape(N_META, d)
    conv_w8 = jnp.pad(g_cw[:, :3, :cw_blk].transpose(1, 0, 2).reshape(3, D_FF), ((0, 5), (0, 0)))
    pending = {}

    def late_weights(after):
        g_out, g_up, g_down = [_with_own_slot(land, own, me) for land, own in
                               zip(_exchange_wait(rest, ["gather"] * 3, after, "gather_rest_wait"), rest_loc)]
        return g_out.reshape(d, d), g_up.reshape(2, D_FF, d), g_down.reshape(D_FF, d)

    def ffn_grads_ready(gw_down, gw_up_t):
        pending["ffn_own"] = [gw_down.reshape(N_DEV, down_blk, d), gw_up_t.reshape(N_DEV, up_blk, d)]
        pending["ffn"] = _exchange_start(pending["ffn_own"], ["scatter"] * 2, "grads_ffn_start")
        return pending["ffn"][-1][0:1, 0:1]

    def out_grad_ready(gw_out):
        pending["out_own"] = [gw_out.reshape(N_DEV, d // N_DEV, d)]
        pending["out"] = _exchange_start(pending["out_own"], ["scatter"], "grads_out_start")
        return pending["out"][-1][0:1, 0:1]

    loss_tile, grad_x, gr = _local_step(
        x[0], loss_target[0], meta_f, attn_norm_g + rest[-1][0:1, 0:1], w_in_t, fox_forget_b, ret_norm_g, ffn_norm_g,
        conv_w8, conv_b, final_norm_g.reshape(1, d), late_weights, ffn_grads_ready, out_grad_ready)

    s_in = jnp.pad(_from_internal_rows(gr["w_in_t"]).reshape(N_DEV, in_blk, d), ((0, 0), (0, in_blk_pad - in_blk), (0, 0)))
    small_shapes = [(1, LANE), (1, d), (1, FOX_HEADS), (1, RET_V), (1, d), (1, D_FF), (1, d), (N_META, d), (3, D_FF)]
    small = _pack([loss_tile[0:1], gr["attn_g"], gr["fox_b"], gr["ret_g"], gr["ffn_g"], gr["conv_b"], gr["final_g"],
                   gr["meta"], gr["conv_w"]])
    r_in, r_small = _exchange([s_in, small], ["scatter", "gather"], "exchange_grads")
    own_block = lambda a: lax.dynamic_index_in_dim(a, me, axis=0, keepdims=False)
    r_down, r_up = [_with_own_slot(land, own_block(own), me) for land, own in
                    zip(_exchange_wait(pending["ffn"], ["scatter"] * 2, r_in, "grads_ffn_wait"), pending["ffn_own"])]
    (r_out,) = [_with_own_slot(land, own_block(own), me) for land, own in
                zip(_exchange_wait(pending["out"], ["scatter"], r_in, "grads_out_wait"), pending["out_own"])]
    g_w_in = _sum_slots(r_in, "sum_w_in", in_blk_pad)[:in_blk].T
    g_w_out = _sum_slots(r_out, "sum_w_out", d // N_DEV)
    g_w_up = _sum_slots(r_up, "sum_w_up", up_blk).T
    g_w_down = _sum_slots(r_down, "sum_w_down", down_blk)
    s_all = _sum_slots(r_small, "sum_small", r_small.shape[1])
    (loss_row, g_attn, g_fox_b, g_ret, g_ffn, g_conv_b, g_final, g_meta_full, g_cw_full) = _unpack(s_all, small_shapes)
    loss = loss_row[0, 0]
    g_meta_loc = lax.dynamic_slice(g_meta_full, (0, me * (d // N_DEV)), (N_META, d // N_DEV))
    g_cw_loc = lax.dynamic_slice(g_cw_full, (0, me * cw_blk), (3, cw_blk))

    d_w_in, m_w_in_n, v_w_in_n = _adamw(w_in[0], g_w_in, m_w_in[0], v_w_in[0], "adamw_w_in", 128)
    d_w_out, m_w_out_n, v_w_out_n = _adamw(w_out[0], g_w_out, m_w_out[0], v_w_out[0], "adamw_w_out", 128)
    d_w_up, m_w_up_n, v_w_up_n = _adamw(w_up[0], g_w_up, m_w_up[0], v_w_up[0], "adamw_w_up", 128)
    d_w_down, m_w_down_n, v_w_down_n = _adamw(w_down[0], g_w_down, m_w_down[0], v_w_down[0], "adamw_w_down", down_blk)
    sm_grads = [g_meta_loc, g_attn, g_fox_b, g_ret, g_ffn, g_cw_loc, g_conv_b, g_final.reshape(d)]
    sm_w = [meta_tokens, attn_norm_g, fox_forget_b, ret_norm_g, ffn_norm_g, conv_w[0], conv_b, final_norm_g]
    sm_m = [m_meta_tokens, m_attn_norm_g, m_fox_forget_b, m_ret_norm_g, m_ffn_norm_g, m_conv_w[0], m_conv_b, m_final_norm_g]
    sm_v = [v_meta_tokens, v_attn_norm_g, v_fox_forget_b, v_ret_norm_g, v_ffn_norm_g, v_conv_w[0], v_conv_b, v_final_norm_g]
    sm_shapes = [a.shape for a in sm_w]
    pk = [_pack(lst) for lst in (sm_w, sm_grads, sm_m, sm_v)]
    sm_d, sm_nm, sm_nv = _adamw(pk[0], pk[1], pk[2], pk[3], "adamw_small", pk[0].shape[0])
    dl = _unpack(sm_d, sm_shapes)
    ml = _unpack(sm_nm, sm_shapes)
    vl = _unpack(sm_nv, sm_shapes)

    def by_weight(meta_, attn_, w_in_, fox_, ret_, w_out_, ffn_, w_up_, cw_, cb_, w_down_, final_):
        return (meta_, attn_, w_in_[None], fox_, ret_, w_out_[None], ffn_, w_up_[None], cw_[None], cb_, w_down_[None], final_)

    grads_out = by_weight(g_meta_loc, g_attn, g_w_in, g_fox_b, g_ret, g_w_out, g_ffn, g_w_up, g_cw_loc, g_conv_b,
                          g_w_down, g_final.reshape(d))
    delta_out = by_weight(dl[0], dl[1], d_w_in, dl[2], dl[3], d_w_out, dl[4], d_w_up, dl[5], dl[6], d_w_down, dl[7])
    m_out = by_weight(ml[0], ml[1], m_w_in_n, ml[2], ml[3], m_w_out_n, ml[4], m_w_up_n, ml[5], ml[6], m_w_down_n, ml[7])
    v_out = by_weight(vl[0], vl[1], v_w_in_n, vl[2], vl[3], v_w_out_n, vl[4], v_w_up_n, vl[5], vl[6], v_w_down_n, vl[7])
    return (loss, grad_x[None]) + grads_out + delta_out + m_out + v_out
```

```python
import numpy as np
import jax
import jax.numpy as jnp
from jax import lax
from jax.experimental import pallas as pl
from jax.experimental.pallas import tpu as pltpu

F32 = jnp.float32
BF16 = jnp.bfloat16

D_MODEL = 1024
N_META = 16
N_PAD = 112
PREFIX = 128
RET_HEADS = 4
RET_DK = 64
RET_DV = 128
FOX_HEADS = 8
FOX_DH = 64
D_FF = 2816
ROPE_BASE = 10000.0
EPS = 1e-6
NEG = -1e30
RET_QK = RET_HEADS * RET_DK
RET_V = RET_HEADS * RET_DV
FOX_W = FOX_HEADS * FOX_DH
IN_WIDTH = 2 * RET_QK + 2 * RET_V + 3 * FOX_W + FOX_HEADS
IN_PAD = 3200
FF_COL_BLOCK = (IN_WIDTH - FOX_HEADS) // 128
QK_SCALE = 0.125

ADAM_LR = 0.001
ADAM_B1 = 0.9
ADAM_B2 = 0.999
ADAM_EPS = 1e-08
ADAM_WD = 0.01
ADAM_STEP = 10

N_DEV = 8
LANE = 128
ROW_TILE = 128
TOK_TILE = 384

NN = (((1,), (0,)), ((), ()))
NT = (((1,), (1,)), ((), ()))
TN = (((0,), (0,)), ((), ()))


def _pcall(body, **kw):
    return pl.pallas_call(body, **kw)


def _params(*sem):
    return pltpu.CompilerParams(dimension_semantics=sem)


def _dot(a, b, dims=NN):
    return lax.dot_general(a, b, dims, preferred_element_type=F32)


def _sigmoid(x):
    return 1.0 / (1.0 + jnp.exp(-x))


def _matmul(a, b, *, mode, grid, a_spec, b_spec, o_spec, out_shape, name, add=None, add_spec=None):
    dims = {"nn": NN, "nt": NT, "tn": TN}[mode]
    nk = grid[2]
    has_add = add is not None
    a_list, b_list = (list(a), list(b)) if isinstance(a, (list, tuple)) else ([a], [b])
    a_specs, b_specs = (list(a_spec), list(b_spec)) if isinstance(a_spec, (list, tuple)) else ([a_spec], [b_spec])
    nt = len(a_list)

    def body(*refs):
        a_refs, b_refs = refs[:nt], refs[nt:2 * nt]
        if has_add:
            add_ref, o_ref = refs[2 * nt:2 * nt + 2]
        else:
            o_ref = refs[2 * nt]
        part = _dot(a_refs[0][...].astype(BF16), b_refs[0][...].astype(BF16), dims)
        for ar, br in zip(a_refs[1:], b_refs[1:]):
            part = part + _dot(ar[...].astype(BF16), br[...].astype(BF16), dims)

        def finish(acc):
            if has_add:
                acc = acc + add_ref[...]
            o_ref[...] = acc.astype(o_ref.dtype)

        if nk == 1:
            finish(part)
        else:
            acc_ref = refs[-1]
            k = pl.program_id(2)

            @pl.when(k == 0)
            def _():
                acc_ref[...] = part

            @pl.when(k > 0)
            def _():
                acc_ref[...] += part

            @pl.when(k == nk - 1)
            def _():
                finish(acc_ref[...])

    in_specs = a_specs + b_specs + ([add_spec] if has_add else [])
    args = tuple(a_list) + tuple(b_list) + ((add,) if has_add else ())
    scratch = [] if nk == 1 else [pltpu.VMEM(tuple(d for d in o_spec.block_shape if d is not None), F32)]
    return _pcall(
        body, name=name, grid=grid, in_specs=in_specs, out_specs=o_spec, out_shape=out_shape,
        scratch_shapes=scratch, compiler_params=_params("parallel", "parallel", "arbitrary"),
    )(*args)


def _mm_simple(a, b, *, mode, tm, tn, tk, out_dtype, name, add=None):
    if mode == "tn":
        K, M = a.shape
    else:
        M, K = a.shape
    N = b.shape[0] if mode == "nt" else b.shape[1]
    grid = (M // tm, N // tn, K // tk)
    resident = dict(pipeline_mode=pl.Buffered(1)) if (tn == N and tk == K) else {}
    a_spec = pl.BlockSpec((tk, tm), lambda i, j, k: (k, i)) if mode == "tn" else pl.BlockSpec((tm, tk), lambda i, j, k: (i, k))
    b_spec = (pl.BlockSpec((tn, tk), lambda i, j, k: (j, k), **resident) if mode == "nt"
              else pl.BlockSpec((tk, tn), lambda i, j, k: (k, j), **resident))
    o_spec = pl.BlockSpec((tm, tn), lambda i, j, k: (i, j))
    return _matmul(a, b, mode=mode, grid=grid, a_spec=a_spec, b_spec=b_spec, o_spec=o_spec,
                   out_shape=jax.ShapeDtypeStruct((M, N), out_dtype), name=name, add=add,
                   add_spec=o_spec if add is not None else None)


def _prep_norm(x, meta, gain, name):
    seq, d = x.shape
    t = seq + PREFIX
    nb = t // ROW_TILE

    def body(x_ref, meta_ref, g_ref, h_ref, n_ref):
        i = pl.program_id(0)

        @pl.when(i == 0)
        def _():
            h_ref[0:N_PAD, :] = jnp.zeros((N_PAD, d), F32)
            h_ref[N_PAD:ROW_TILE, :] = meta_ref[...]

        @pl.when(i > 0)
        def _():
            h_ref[...] = x_ref[...]

        h = h_ref[...]
        r = lax.rsqrt(jnp.mean(h * h, axis=-1, keepdims=True) + EPS)
        n_ref[...] = (h * r * g_ref[...]).astype(BF16)

    return _pcall(
        body, name=name, grid=(nb,),
        in_specs=[pl.BlockSpec((ROW_TILE, d), lambda i: (jnp.maximum(i - 1, 0), 0)),
                  pl.BlockSpec((N_META, d), lambda i: (0, 0)),
                  pl.BlockSpec((1, d), lambda i: (0, 0))],
        out_specs=[pl.BlockSpec((ROW_TILE, d), lambda i: (i, 0)), pl.BlockSpec((ROW_TILE, d), lambda i: (i, 0))],
        out_shape=[jax.ShapeDtypeStruct((t, d), F32), jax.ShapeDtypeStruct((t, d), BF16)],
        compiler_params=_params("parallel"),
    )(x, meta, gain)


def _rmsnorm(h, gain, name):
    t, d = h.shape

    def body(h_ref, g_ref, n_ref):
        x = h_ref[...]
        r = lax.rsqrt(jnp.mean(x * x, axis=-1, keepdims=True) + EPS)
        n_ref[...] = (x * r * g_ref[...]).astype(BF16)

    return _pcall(
        body, name=name, grid=(t // TOK_TILE,),
        in_specs=[pl.BlockSpec((TOK_TILE, d), lambda i: (i, 0)), pl.BlockSpec((1, d), lambda i: (0, 0))],
        out_specs=pl.BlockSpec((TOK_TILE, d), lambda i: (i, 0)),
        out_shape=jax.ShapeDtypeStruct((t, d), BF16),
        compiler_params=_params("parallel"),
    )(h, gain)


def _rmsnorm_bwd(dn, h, gain, dres, name):
    t, d = h.shape

    def body(dn_ref, h_ref, g_ref, dres_ref, dh_ref, gg_ref):
        i = pl.program_id(0)
        x = h_ref[...]
        r = lax.rsqrt(jnp.mean(x * x, axis=-1, keepdims=True) + EPS)
        xhat = x * r
        dy = dn_ref[...]
        u = dy * g_ref[...]
        dh_ref[...] = dres_ref[...] + r * (u - xhat * jnp.mean(u * xhat, axis=-1, keepdims=True))
        part = jnp.sum(dy * xhat, axis=0, keepdims=True)

        @pl.when(i == 0)
        def _():
            gg_ref[...] = part

        @pl.when(i > 0)
        def _():
            gg_ref[...] += part

    return _pcall(
        body, name=name, grid=(t // TOK_TILE,),
        in_specs=[pl.BlockSpec((TOK_TILE, d), lambda i: (i, 0)), pl.BlockSpec((TOK_TILE, d), lambda i: (i, 0)),
                  pl.BlockSpec((1, d), lambda i: (0, 0)), pl.BlockSpec((TOK_TILE, d), lambda i: (i, 0))],
        out_specs=[pl.BlockSpec((TOK_TILE, d), lambda i: (i, 0)), pl.BlockSpec((1, d), lambda i: (0, 0))],
        out_shape=[jax.ShapeDtypeStruct((t, d), F32), jax.ShapeDtypeStruct((1, d), F32)],
        compiler_params=_params("arbitrary"),
    )(dn, h, gain, dres)


def _loss_bwd(h2, target, gain, name):
    t, d = h2.shape
    nb = t // ROW_TILE

    def body(h_ref, tgt_ref, g_ref, loss_ref, dh_ref, gg_ref):
        i = pl.program_id(0)

        @pl.when(i == 0)
        def _():
            loss_ref[...] = jnp.zeros_like(loss_ref)
            gg_ref[...] = jnp.zeros_like(gg_ref)
            dh_ref[...] = jnp.zeros_like(dh_ref)

        @pl.when(i > 0)
        def _():
            x = h_ref[...]
            r = lax.rsqrt(jnp.mean(x * x, axis=-1, keepdims=True) + EPS)
            xhat = x * r
            g = g_ref[...]
            err = xhat * g - tgt_ref[...]
            loss_ref[...] += 0.5 * jnp.sum(jnp.mean(err * err, axis=-1, keepdims=True))
            dy = err * (1.0 / d)
            u = dy * g
            dh_ref[...] = r * (u - xhat * jnp.mean(u * xhat, axis=-1, keepdims=True))
            gg_ref[...] += jnp.sum(dy * xhat, axis=0, keepdims=True)

    return _pcall(
        body, name=name, grid=(nb,),
        in_specs=[pl.BlockSpec((ROW_TILE, d), lambda i: (i, 0)),
                  pl.BlockSpec((ROW_TILE, d), lambda i: (jnp.maximum(i - 1, 0), 0)),
                  pl.BlockSpec((1, d), lambda i: (0, 0))],
        out_specs=[pl.BlockSpec((8, LANE), lambda i: (0, 0)), pl.BlockSpec((ROW_TILE, d), lambda i: (i, 0)),
                   pl.BlockSpec((1, d), lambda i: (0, 0))],
        out_shape=[jax.ShapeDtypeStruct((8, LANE), F32), jax.ShapeDtypeStruct((t, d), F32),
                   jax.ShapeDtypeStruct((1, d), F32)],
        compiler_params=_params("arbitrary"),
    )(h2, target, gain)


def _ret_consts(bk):
    gam = 1.0 - 2.0 ** (-5.0 - np.arange(RET_HEADS))
    n = np.arange(bk)
    same_or_earlier_chunk = (n[None, :] // 64) <= (n[:, None] // 64)
    w = gam[:, None, None] ** np.abs(n[:, None] - n[None, :])[None] * same_or_earlier_chunk[None]
    wq = gam[:, None] ** (n[None, :] + 1.0)
    wk = gam[:, None] ** (bk - 1.0 - n[None, :])
    mask = (np.arange(RET_QK)[None, :] // RET_DK) == np.arange(RET_HEADS)[:, None]
    return (jnp.asarray(w, F32), jnp.asarray(wq[:, :, None], F32), jnp.asarray(wk[:, :, None], F32),
            jnp.asarray(mask[:, None, :], F32), [float(g ** bk) for g in gam])


def _rope_tables(t):
    half = RET_DK // 2
    inv = 1.0 / (ROPE_BASE ** (jnp.arange(half, dtype=F32) / half))
    ang = jnp.arange(t).astype(F32)[:, None] * inv[None, :]
    cos, sin = jnp.cos(ang), jnp.sin(ang)
    return (jnp.tile(jnp.concatenate([cos, cos], axis=1), (1, RET_HEADS)),
            jnp.tile(jnp.concatenate([-sin, sin], axis=1), (1, RET_HEADS)))


def _swap_halves(x):
    outs = []
    for s in range(x.shape[1] // LANE):
        xs = x[:, LANE * s:LANE * (s + 1)]
        lane = lax.broadcasted_iota(jnp.int32, xs.shape, 1)
        outs.append(jnp.where((lane & 32) == 0, pltpu.roll(xs, LANE - 32, axis=1), pltpu.roll(xs, 32, axis=1)))
    return outs[0] if len(outs) == 1 else jnp.concatenate(outs, axis=1)


def _rope(x, cos, sin_signed):
    return x * cos + _swap_halves(x) * sin_signed


def _rope_t(dx, cos, sin_signed):
    return dx * cos + _swap_halves(dx * sin_signed)


def _ret_fwd(proj, cos, sin, gain, name):
    t = proj.shape[0]
    bk = TOK_TILE
    nb = t // bk
    w, wq, wk, mask, g_blk = _ret_consts(bk)

    def body(q_ref, k_ref, v_ref, rg_ref, cos_ref, sin_ref, w_ref, wq_ref, wk_ref, mask_ref, gain_ref,
             opre_ref, og_ref, st_ref, r_ref):
        i = pl.program_id(0)

        @pl.when(i == 0)
        def _():
            r_ref[...] = jnp.zeros_like(r_ref)

        c, s = cos_ref[...], sin_ref[...]
        valid = ((i * bk + lax.broadcasted_iota(jnp.int32, (bk, 1), 0)) >= N_PAD).astype(F32)
        qr = _rope(q_ref[...], c, s)
        kr = _rope(k_ref[...], c, s) * QK_SCALE * valid
        kb = kr.astype(BF16)
        for h in range(RET_HEADS):
            hm = mask_ref[h]
            cols = slice(RET_DV * h, RET_DV * (h + 1))
            vh = v_ref[:, cols].astype(BF16)
            r_prev = r_ref[h]
            st_ref[0, h] = r_prev
            sm = _dot((qr * hm).astype(BF16), kb, NT) * w_ref[h]
            o = _dot(sm.astype(BF16), vh) + _dot((qr * (hm * wq_ref[h])).astype(BF16), r_prev.astype(BF16))
            r_ref[h] = g_blk[h] * r_prev + _dot((kr * wk_ref[h]).astype(BF16), vh, TN)
            opre_ref[:, cols] = o
            rstd = lax.rsqrt(jnp.mean(o * o, axis=-1, keepdims=True) + EPS)
            rg = rg_ref[:, cols]
            og_ref[:, cols] = (o * rstd * gain_ref[:, cols] * (rg * _sigmoid(rg))).astype(BF16)

    full = lambda shape: pl.BlockSpec(shape, lambda i: (0,) * len(shape))
    return _pcall(
        body, name=name, grid=(nb,),
        in_specs=[pl.BlockSpec((bk, RET_QK), lambda i: (i, 0)), pl.BlockSpec((bk, RET_QK), lambda i: (i, 1)),
                  pl.BlockSpec((bk, RET_V), lambda i: (i, 1)), pl.BlockSpec((bk, RET_V), lambda i: (i, 2)),
                  pl.BlockSpec((bk, RET_QK), lambda i: (i, 0)), pl.BlockSpec((bk, RET_QK), lambda i: (i, 0)),
                  full((RET_HEADS, bk, bk)), full((RET_HEADS, bk, 1)), full((RET_HEADS, bk, 1)),
                  full((RET_HEADS, 1, RET_QK)), full((1, RET_V))],
        out_specs=[pl.BlockSpec((bk, RET_V), lambda i: (i, 0)), pl.BlockSpec((bk, RET_V), lambda i: (i, 0)),
                   pl.BlockSpec((1, RET_HEADS, RET_QK, RET_DV), lambda i: (i, 0, 0, 0))],
        out_shape=[jax.ShapeDtypeStruct((t, RET_V), F32), jax.ShapeDtypeStruct((t, RET_V + FOX_W), BF16),
                   jax.ShapeDtypeStruct((nb, RET_HEADS, RET_QK, RET_DV), F32)],
        scratch_shapes=[pltpu.VMEM((RET_HEADS, RET_QK, RET_DV), F32)],
        compiler_params=_params("arbitrary"),
    )(proj, proj, proj, proj, cos, sin, w, wq, wk, mask, gain)


def _ret_bwd(proj, cos, sin, gain, dmixed, opre, states, name):
    t = proj.shape[0]
    bk = TOK_TILE
    nb = t // bk
    w, wq, wk, mask, g_blk = _ret_consts(bk)
    v0, g0 = 2 * RET_QK, 2 * RET_QK + RET_V

    def body(q_ref, k_ref, v_ref, rg_ref, cos_ref, sin_ref, w_ref, wq_ref, wk_ref, mask_ref, gain_ref,
             dog_ref, opre_ref, st_ref, dp_ref, gg_ref, dr_ref):
        step = pl.program_id(0)
        i = nb - 1 - step

        @pl.when(step == 0)
        def _():
            dr_ref[...] = jnp.zeros_like(dr_ref)
            gg_ref[...] = jnp.zeros_like(gg_ref)

        c, s = cos_ref[...], sin_ref[...]
        valid = ((i * bk + lax.broadcasted_iota(jnp.int32, (bk, 1), 0)) >= N_PAD).astype(F32)
        qr = _rope(q_ref[...], c, s)
        kr = _rope(k_ref[...], c, s) * QK_SCALE * valid
        kb = kr.astype(BF16)
        dqr = jnp.zeros((bk, RET_QK), F32)
        dkr = jnp.zeros((bk, RET_QK), F32)
        for h in range(RET_HEADS):
            hm = mask_ref[h]
            cols = slice(RET_DV * h, RET_DV * (h + 1))
            vh = v_ref[:, cols].astype(BF16)
            o = opre_ref[:, cols]
            rstd = lax.rsqrt(jnp.mean(o * o, axis=-1, keepdims=True) + EPS)
            xhat = o * rstd
            rg = rg_ref[:, cols]
            sg = _sigmoid(rg)
            gate = rg * sg
            gn = gain_ref[:, cols]
            dog = dog_ref[:, cols]
            dp_ref[:, g0 + RET_DV * h:g0 + RET_DV * (h + 1)] = (
                dog * xhat * gn * (sg * (1.0 + rg * (1.0 - sg)))).astype(BF16)
            gg_ref[:, cols] += jnp.sum(dog * xhat * gate, axis=0, keepdims=True)
            dxh = dog * gn * gate
            do = (rstd * (dxh - xhat * jnp.mean(dxh * xhat, axis=-1, keepdims=True))).astype(BF16)
            qm = (qr * hm).astype(BF16)
            qw = (qr * (hm * wq_ref[h])).astype(BF16)
            kw = (kr * wk_ref[h]).astype(BF16)
            wh = w_ref[h]
            sm = (_dot(qm, kb, NT) * wh).astype(BF16)
            ds = (_dot(do, vh, NT) * wh).astype(BF16)
            dr = dr_ref[h]
            drb = dr.astype(BF16)
            dp_ref[:, v0 + RET_DV * h:v0 + RET_DV * (h + 1)] = (_dot(sm, do, TN) + _dot(kw, drb)).astype(BF16)
            dqr = dqr + _dot(ds, kb) * hm + _dot(do, st_ref[0, h].astype(BF16), NT) * (hm * wq_ref[h])
            dkr = dkr + _dot(ds, qm, TN) + _dot(vh, drb, NT) * wk_ref[h]
            dr_ref[h] = g_blk[h] * dr + _dot(qw, do, TN)
        dp_ref[:, 0:RET_QK] = _rope_t(dqr, c, s).astype(BF16)
        dp_ref[:, RET_QK:2 * RET_QK] = _rope_t(dkr * (QK_SCALE * valid), c, s).astype(BF16)

    full = lambda shape: pl.BlockSpec(shape, lambda i: (0,) * len(shape))
    rev = lambda col: (lambda i: (nb - 1 - i, col))
    return _pcall(
        body, name=name, grid=(nb,),
        in_specs=[pl.BlockSpec((bk, RET_QK), rev(0)), pl.BlockSpec((bk, RET_QK), rev(1)),
                  pl.BlockSpec((bk, RET_V), rev(1)), pl.BlockSpec((bk, RET_V), rev(2)),
                  pl.BlockSpec((bk, RET_QK), rev(0)), pl.BlockSpec((bk, RET_QK), rev(0)),
                  full((RET_HEADS, bk, bk)), full((RET_HEADS, bk, 1)), full((RET_HEADS, bk, 1)),
                  full((RET_HEADS, 1, RET_QK)), full((1, RET_V)),
                  pl.BlockSpec((bk, RET_V), rev(0)), pl.BlockSpec((bk, RET_V), rev(0)),
                  pl.BlockSpec((1, RET_HEADS, RET_QK, RET_DV), lambda i: (nb - 1 - i, 0, 0, 0))],
        out_specs=[pl.BlockSpec((bk, g0 + RET_V), rev(0)), pl.BlockSpec((1, RET_V), lambda i: (0, 0))],
        out_shape=[jax.ShapeDtypeStruct((t, IN_PAD), BF16), jax.ShapeDtypeStruct((1, RET_V), F32)],
        scratch_shapes=[pltpu.VMEM((RET_HEADS, RET_QK, RET_DV), F32)],
        compiler_params=_params("arbitrary"),
    )(proj, proj, proj, proj, cos, sin, w, wq, wk, mask, gain, dmixed, opre, states)


def _forget_cumsum(proj, bias, name):
    t = proj.shape[0]
    nb = t // ROW_TILE
    tril = jnp.asarray(np.tril(np.ones((ROW_TILE, ROW_TILE))), F32)

    def body(z_ref, b_ref, tril_ref, c_ref, carry_ref):
        i = pl.program_id(0)

        @pl.when(i == 0)
        def _():
            carry_ref[...] = jnp.zeros_like(carry_ref)

        z = z_ref[...] + b_ref[...]
        logf = jnp.minimum(z, 0.0) - jnp.log(1.0 + jnp.exp(-jnp.abs(z)))
        c = lax.dot_general(tril_ref[...], logf, NN, precision=lax.Precision.HIGHEST,
                            preferred_element_type=F32) + carry_ref[...]
        c_ref[...] = c
        carry_ref[...] = c[ROW_TILE - 1:ROW_TILE, :]

    return _pcall(
        body, name=name, grid=(nb,),
        in_specs=[pl.BlockSpec((ROW_TILE, LANE), lambda i: (i, FF_COL_BLOCK)), pl.BlockSpec((1, LANE), lambda i: (0, 0)),
                  pl.BlockSpec((ROW_TILE, ROW_TILE), lambda i: (0, 0))],
        out_specs=pl.BlockSpec((ROW_TILE, LANE), lambda i: (i, 0)),
        out_shape=jax.ShapeDtypeStruct((t, LANE), F32),
        scratch_shapes=[pltpu.VMEM((1, LANE), F32)],
        compiler_params=_params("arbitrary"),
    )(proj, bias, tril)


def _forget_cumsum_bwd(proj, bias, drs, dcs, dproj, name):
    t = proj.shape[0]
    nb = t // ROW_TILE
    triu = jnp.asarray(np.triu(np.ones((ROW_TILE, ROW_TILE))), F32)

    def body(z_ref, b_ref, triu_ref, drs_ref, dcs_ref, dproj_in, dz_ref, gb_ref, carry_ref):
        step = pl.program_id(0)

        @pl.when(step == 0)
        def _():
            carry_ref[...] = jnp.zeros_like(carry_ref)
            gb_ref[...] = jnp.zeros_like(gb_ref)

        dlogf = lax.dot_general(triu_ref[...], drs_ref[...] - dcs_ref[...], NN, precision=lax.Precision.HIGHEST,
                                preferred_element_type=F32) + carry_ref[...]
        carry_ref[...] = dlogf[0:1, :]
        z = z_ref[...] + b_ref[...]
        is_head = lax.broadcasted_iota(jnp.int32, (ROW_TILE, LANE), 1) < FOX_HEADS
        dz = jnp.where(is_head, dlogf / (1.0 + jnp.exp(z)), 0.0)
        dz_ref[...] = dz.astype(BF16)
        gb_ref[...] += jnp.sum(dz, axis=0, keepdims=True)

    return _pcall(
        body, name=name, grid=(nb,),
        in_specs=[pl.BlockSpec((ROW_TILE, LANE), lambda i: (nb - 1 - i, FF_COL_BLOCK)),
                  pl.BlockSpec((1, LANE), lambda i: (0, 0)),
                  pl.BlockSpec((ROW_TILE, ROW_TILE), lambda i: (0, 0)),
                  pl.BlockSpec((ROW_TILE, LANE), lambda i: (nb - 1 - i, 0)),
                  pl.BlockSpec((ROW_TILE, LANE), lambda i: (nb - 1 - i, 0)),
                  pl.BlockSpec(memory_space=pl.ANY)],
        out_specs=[pl.BlockSpec((ROW_TILE, LANE), lambda i: (nb - 1 - i, FF_COL_BLOCK)),
                   pl.BlockSpec((1, LANE), lambda i: (0, 0))],
        out_shape=[jax.ShapeDtypeStruct(dproj.shape, BF16), jax.ShapeDtypeStruct((1, LANE), F32)],
        input_output_aliases={5: 0},
        scratch_shapes=[pltpu.VMEM((1, LANE), F32)],
        compiler_params=_params("arbitrary"),
    )(proj, bias, triu, drs, dcs, dproj)


FOX_PAIRS = FOX_HEADS // 2
L_ONE_Q = FOX_DH
L_ONE_K = FOX_DH + 3
L_LSE = FOX_DH + 4


def _split3(x):
    hi = x.astype(BF16).astype(F32)
    r = x - hi
    mid = r.astype(BF16).astype(F32)
    return hi, mid, r - mid


def _head_to_low(slab, e):
    return slab if e == 0 else pltpu.roll(slab, FOX_DH, axis=1)


def _pair(a, b, low):
    return jnp.where(low, a, pltpu.roll(b, FOX_DH, axis=1))


def _fox_prep(proj, c, name):
    t = proj.shape[0]
    tq = TOK_TILE

    def body(p_ref, c_ref, qa_ref, ka_ref, va_ref):
        i = pl.program_id(0)
        lane = lax.broadcasted_iota(jnp.int32, (tq, LANE), 1)
        low = lane < FOX_DH
        live = (i * tq + lax.broadcasted_iota(jnp.int32, (tq, 1), 0)) >= N_PAD
        q_tail = jnp.where(lane < L_ONE_Q + 3, 1.0, 0.0)
        k_ones = (lane >= L_ONE_K) & (lane < L_ONE_K + 4)
        v_tail = jnp.where(lane < FOX_DH + 2, 1.0, 0.0)
        for pair in range(FOX_PAIRS):
            base = 3 * LANE * pair
            for e in range(2):
                h = 2 * pair + e
                q = _head_to_low(p_ref[:, base:base + LANE], e)
                k = _head_to_low(p_ref[:, base + LANE:base + 2 * LANE], e)
                v = _head_to_low(p_ref[:, base + 2 * LANE:base + 3 * LANE], e)
                hi, mid, lo = _split3(jnp.where(live, -c_ref[:, h:h + 1], NEG))
                ka = jnp.where(low, k, jnp.where(k_ones, 1.0, 0.0))
                ka = jnp.where(lane == L_ONE_Q, hi, jnp.where(lane == L_ONE_Q + 1, mid, jnp.where(lane == L_ONE_Q + 2, lo, ka)))
                qa_ref[h] = jnp.where(low, q * QK_SCALE, q_tail).astype(BF16)
                ka_ref[h] = ka.astype(BF16)
                va_ref[h] = jnp.where(low, v, v_tail).astype(BF16)

    out = jax.ShapeDtypeStruct((FOX_HEADS, t, LANE), BF16)
    ospec = pl.BlockSpec((FOX_HEADS, tq, LANE), lambda i: (0, i, 0))
    return _pcall(
        body, name=name, grid=(t // tq,),
        in_specs=[pl.BlockSpec((tq, 3 * FOX_W), lambda i: (i, 1)), pl.BlockSpec((tq, LANE), lambda i: (i, 0))],
        out_specs=[ospec, ospec, ospec], out_shape=[out, out, out],
        compiler_params=_params("parallel"),
    )(proj, c)


STEP_PAIRS = 2
STEP_HEADS = 2 * STEP_PAIRS
FOX_GROUPS = FOX_PAIRS // STEP_PAIRS


def _blockdiag(a, b):
    z = jnp.zeros_like(a)
    return jnp.concatenate([jnp.concatenate([a, z], axis=1), jnp.concatenate([z, b], axis=1)], axis=0)


def _fox_fwd(qa, ka, va, mixed, name):
    nh, nq, tq, _ = qa.shape
    t = nq * tq

    def body(qa_ref, ka_ref, va_ref, mixed_in, mixed_ref, o_ref, lse_ref):
        i = pl.program_id(1)
        lane = lax.broadcasted_iota(jnp.int32, (tq, LANE), 1)
        causal = lax.broadcasted_iota(jnp.int32, (tq, tq), 1) <= lax.broadcasted_iota(jnp.int32, (tq, tq), 0)
        qps = [jnp.concatenate([qa_ref[2 * c], qa_ref[2 * c + 1]], axis=1) for c in range(STEP_PAIRS)]

        def step(j, carry, diagonal):
            scores = [_dot(qps[c], _blockdiag(ka_ref[2 * c, j], ka_ref[2 * c + 1, j]), NT) for c in range(STEP_PAIRS)]
            new = []
            for c in range(STEP_PAIRS):
                ms, acc = carry[c]
                ps, ms_new, alphas = [], [], []
                for e in range(2):
                    s = scores[c][:, e * tq:(e + 1) * tq]
                    if diagonal:
                        s = jnp.where(causal, s, NEG)
                    m_new = jnp.maximum(ms[e], jnp.max(s, axis=-1, keepdims=True))
                    ps.append(jnp.exp(s - m_new).astype(BF16))
                    ms_new.append(m_new)
                    alphas.append(jnp.broadcast_to(jnp.exp(ms[e] - m_new), (tq, LANE)))
                pv = _dot(jnp.concatenate(ps, axis=1), _blockdiag(va_ref[2 * c, j], va_ref[2 * c + 1, j]))
                new.append((tuple(ms_new), jnp.concatenate(alphas, axis=1) * acc + pv))
            return tuple(new)

        m0 = jnp.full((tq, 1), NEG, F32)
        init = tuple(((m0, m0), jnp.zeros((tq, 2 * LANE), F32)) for _ in range(STEP_PAIRS))
        carry = lax.fori_loop(0, i, lambda j, cr: step(j, cr, False), init)
        o_pairs = []
        lse = jnp.zeros((tq, LANE), F32)
        for c, (ms, acc) in enumerate(step(i, carry, True)):
            outs = []
            for e in range(2):
                half = acc[:, e * LANE:(e + 1) * LANE]
                l = half[:, FOX_DH:FOX_DH + 1]
                outs.append(half / l)
                lse = jnp.where(lane == 2 * c + e, ms[e] + jnp.log(l), lse)
            o_pairs.append(_pair(outs[0], outs[1], lane < FOX_DH))
        o_all = jnp.concatenate(o_pairs, axis=1)
        mixed_ref[...] = o_all.astype(BF16)
        o_ref[...] = o_all
        lse_ref[...] = lse

    width = STEP_PAIRS * LANE
    whole = pl.BlockSpec((STEP_HEADS, nq, tq, LANE), lambda g, i: (g, 0, 0, 0), pipeline_mode=pl.Buffered(1))
    return _pcall(
        body, name=name, grid=(FOX_GROUPS, nq),
        in_specs=[pl.BlockSpec((STEP_HEADS, None, tq, LANE), lambda g, i: (g, i, 0, 0)), whole, whole,
                  pl.BlockSpec(memory_space=pl.ANY)],
        out_specs=[pl.BlockSpec((tq, width), lambda g, i: (i, RET_V // width + g)),
                   pl.BlockSpec((tq, width), lambda g, i: (i, g)),
                   pl.BlockSpec((None, tq, LANE), lambda g, i: (g, i, 0))],
        out_shape=[jax.ShapeDtypeStruct(mixed.shape, BF16), jax.ShapeDtypeStruct((t, FOX_W), F32),
                   jax.ShapeDtypeStruct((FOX_GROUPS, t, LANE), F32)],
        input_output_aliases={3: 0},
        compiler_params=_params("parallel", "parallel"),
    )(qa, ka, va, mixed)


def _fox_prep_bwd(dmixed, o_fox, lse, qa, name):
    t = dmixed.shape[0]
    tq = TOK_TILE

    def body(dm_ref, o_ref, lse_ref, qa_ref, qab_ref, doa_ref):
        i = pl.program_id(0)
        lane = lax.broadcasted_iota(jnp.int32, (tq, LANE), 1)
        low = lane < FOX_DH
        live = (i * tq + lax.broadcasted_iota(jnp.int32, (tq, 1), 0)) >= N_PAD
        for pair in range(FOX_PAIRS):
            cols = slice(LANE * pair, LANE * (pair + 1))
            d_slab = dm_ref[:, cols]
            prod = d_slab * o_ref[:, cols]
            for e in range(2):
                h = 2 * pair + e
                nd = -jnp.sum(jnp.where(low, _head_to_low(prod, e), 0.0), axis=-1, keepdims=True)
                nd_hi = nd.astype(BF16).astype(F32)
                doa = jnp.where(low, _head_to_low(d_slab, e), 0.0)
                doa = jnp.where(lane == FOX_DH, nd_hi, jnp.where(lane == FOX_DH + 1, nd - nd_hi, doa))
                doa_ref[h] = doa.astype(BF16)
                lse_h = lse_ref[h // STEP_HEADS][:, h % STEP_HEADS:h % STEP_HEADS + 1]
                hi, mid, lo = _split3(jnp.where(live, -lse_h, 0.0))
                qab = qa_ref[h].astype(F32)
                qab = jnp.where(lane == L_LSE, hi, jnp.where(lane == L_LSE + 1, mid, jnp.where(lane == L_LSE + 2, lo, qab)))
                qab_ref[h] = qab.astype(BF16)

    out = jax.ShapeDtypeStruct((FOX_HEADS, t, LANE), BF16)
    hspec = pl.BlockSpec((FOX_HEADS, tq, LANE), lambda i: (0, i, 0))
    return _pcall(
        body, name=name, grid=(t // tq,),
        in_specs=[pl.BlockSpec((tq, FOX_W), lambda i: (i, 1)), pl.BlockSpec((tq, FOX_W), lambda i: (i, 0)),
                  pl.BlockSpec((FOX_GROUPS, tq, LANE), lambda i: (0, i, 0)), hspec],
        out_specs=[hspec, hspec], out_shape=[out, out],
        compiler_params=_params("parallel"),
    )(dmixed, o_fox, lse, qa)


def _fox_bwd(qab, doa, ka, va, dproj, name):
    nh, nq, tq, _ = qab.shape
    t = nq * tq
    slab = 3 * LANE * STEP_PAIRS
    group0 = (2 * RET_QK + 2 * RET_V) // slab

    def body(qab_ref, doa_ref, ka_ref, va_ref, dproj_in, dp_ref, drs_ref, dcs_ref, dq_ref):
        g, j = pl.program_id(0), pl.program_id(1)

        @pl.when((g == 0) & (j == 0))
        def _():
            drs_ref[...] = jnp.zeros_like(drs_ref)
            dcs_ref[...] = jnp.zeros_like(dcs_ref)

        @pl.when(j == 0)
        def _():
            dq_ref[...] = jnp.zeros_like(dq_ref)

        lane = lax.broadcasted_iota(jnp.int32, (tq, LANE), 1)
        low = lane < FOX_DH
        key_le_query = lax.broadcasted_iota(jnp.int32, (tq, tq), 0) <= lax.broadcasted_iota(jnp.int32, (tq, tq), 1)

        def by_head(c, a, b, col):
            h = STEP_HEADS * g + 2 * c
            return jnp.where(lane == h, a[:, col:col + 1], jnp.where(lane == h + 1, b[:, col:col + 1], 0.0))

        kbs = [ka_ref[h] for h in range(STEP_HEADS)]
        vbs = [va_ref[h] for h in range(STEP_HEADS)]

        def step(i, carry, diagonal):
            qbs = [qab_ref[h, i] for h in range(STEP_HEADS)]
            dobs = [doa_ref[h, i] for h in range(STEP_HEADS)]
            st = [_dot(kbs[h], qbs[h], NT) for h in range(STEP_HEADS)]
            dpt = [_dot(vbs[h], dobs[h], NT) for h in range(STEP_HEADS)]
            new = []
            for h in range(STEP_HEADS):
                p = jnp.exp(st[h])
                if diagonal:
                    p = jnp.where(key_le_query, p, 0.0)
                ds = (p * dpt[h]).astype(BF16)
                dq_ref[h, i] += _dot(ds, kbs[h], TN)
                dk, dv = carry[h]
                new.append((dk + _dot(ds, qbs[h]), dv + _dot(p.astype(BF16), dobs[h])))
            return tuple(new)

        zero = jnp.zeros((tq, LANE), F32)
        carry = step(j, tuple((zero, zero) for _ in range(STEP_HEADS)), True)
        carry = lax.fori_loop(j + 1, nq, lambda i, cr: step(i, cr, False), carry)
        rows = pl.ds(pl.multiple_of(j * tq, tq), tq)
        for c in range(STEP_PAIRS):
            (dka, dva), (dkb, dvb) = carry[2 * c], carry[2 * c + 1]
            c0 = 3 * LANE * c
            dp_ref[rows, c0 + LANE:c0 + 2 * LANE] = _pair(dka, dkb, low).astype(BF16)
            dp_ref[rows, c0 + 2 * LANE:c0 + 3 * LANE] = _pair(dva, dvb, low).astype(BF16)
            dcs_ref[rows, :] += by_head(c, dka, dkb, L_ONE_Q)

        @pl.when(j == nq - 1)
        def _():
            for c in range(STEP_PAIRS):
                for blk in range(nq):
                    r = slice(blk * tq, (blk + 1) * tq)
                    a, b = dq_ref[2 * c, blk], dq_ref[2 * c + 1, blk]
                    dp_ref[r, 3 * LANE * c:3 * LANE * c + LANE] = (_pair(a, b, low) * QK_SCALE).astype(BF16)
                    drs_ref[r, :] += by_head(c, a, b, L_ONE_K)

    whole = pl.BlockSpec((STEP_HEADS, nq, tq, LANE), lambda g, j: (g, 0, 0, 0), pipeline_mode=pl.Buffered(1))
    blk = pl.BlockSpec((STEP_HEADS, None, tq, LANE), lambda g, j: (g, j, 0, 0))
    sums = pl.BlockSpec((t, LANE), lambda g, j: (0, 0), pipeline_mode=pl.Buffered(1))
    return _pcall(
        body, name=name, grid=(FOX_GROUPS, nq),
        in_specs=[whole, whole, blk, blk, pl.BlockSpec(memory_space=pl.ANY)],
        out_specs=[pl.BlockSpec((t, slab), lambda g, j: (0, group0 + g)), sums, sums],
        out_shape=[jax.ShapeDtypeStruct(dproj.shape, BF16), jax.ShapeDtypeStruct((t, LANE), F32),
                   jax.ShapeDtypeStruct((t, LANE), F32)],
        input_output_aliases={4: 0},
        scratch_shapes=[pltpu.VMEM((STEP_HEADS, nq, tq, LANE), F32)],
        compiler_params=_params("arbitrary", "arbitrary"),
    )(qab, doa, ka, va, dproj)


HALO = 8


def _rows_ext(ref, r0, rows, t, before, after):
    lo, hi = r0 - before, r0 + rows + after
    parts = []
    if lo < 0:
        parts.append(jnp.zeros((-lo, LANE), F32))
    parts.append(ref[max(lo, 0):min(hi, t), :].astype(F32))
    if hi > t:
        parts.append(jnp.zeros((hi - t, LANE), F32))
    return parts[0] if len(parts) == 1 else jnp.concatenate(parts, axis=0)


def _conv_taps(a_ext, r0_ext, cw_ref, cb_ref):
    n = a_ext.shape[0]
    if r0_ext < N_PAD:
        row = r0_ext + lax.broadcasted_iota(jnp.int32, (n, 1), 0)
        a_ext = jnp.where(row >= N_PAD, a_ext, 0.0)
    a1 = pltpu.roll(a_ext, 1, axis=0)
    a2 = pltpu.roll(a_ext, 2, axis=0)
    acc = cb_ref[...] + a2 * cw_ref[0:1, :] + a1 * cw_ref[1:2, :] + a_ext * cw_ref[2:3, :]
    return a_ext, a1, a2, acc


def _conv_gate_fwd(up, conv_w8, conv_b, name):
    _, t, f = up.shape
    rows = TOK_TILE

    def body(a_ref, b_ref, cw_ref, cb_ref, g_ref):
        for r0 in range(0, t, rows):
            a_ext = _rows_ext(a_ref, r0, rows, t, HALO, 0)
            _, _, _, acc = _conv_taps(a_ext, r0 - HALO, cw_ref, cb_ref)
            acc = acc[HALO:, :]
            g_ref[r0:r0 + rows, :] = (acc * _sigmoid(acc) * b_ref[r0:r0 + rows, :]).astype(BF16)

    return _pcall(
        body, name=name, grid=(f // LANE,),
        in_specs=[pl.BlockSpec((None, t, LANE), lambda j: (0, 0, j)), pl.BlockSpec((None, t, LANE), lambda j: (1, 0, j)),
                  pl.BlockSpec((8, LANE), lambda j: (0, j)), pl.BlockSpec((1, LANE), lambda j: (0, j))],
        out_specs=pl.BlockSpec((t, LANE), lambda j: (0, j)),
        out_shape=jax.ShapeDtypeStruct((t, f), BF16),
        compiler_params=_params("parallel"),
    )(up, up, conv_w8, conv_b)


def _conv_gate_bwd(up, conv_w8, conv_b, dg, name):
    _, t, f = up.shape
    rows = TOK_TILE

    def body(a_ref, b_ref, cw_ref, cb_ref, dg_ref, dup_ref, gcw_ref, gcb_ref):
        gw = [jnp.zeros((1, LANE), F32) for _ in range(3)]
        gb = jnp.zeros((1, LANE), F32)
        for r0 in range(0, t, rows):
            a_ext = _rows_ext(a_ref, r0, rows, t, HALO, HALO)
            b_ext = _rows_ext(b_ref, r0, rows, t, HALO, HALO)
            dg_ext = _rows_ext(dg_ref, r0, rows, t, HALO, HALO)
            a0, a1, a2, acc = _conv_taps(a_ext, r0 - HALO, cw_ref, cb_ref)
            sg = _sigmoid(acc)
            dacc = dg_ext * b_ext * (sg * (1.0 + acc * (1.0 - sg)))
            n = dacc.shape[0]
            da = (dacc * cw_ref[2:3, :] + pltpu.roll(dacc, n - 1, axis=0) * cw_ref[1:2, :]
                  + pltpu.roll(dacc, n - 2, axis=0) * cw_ref[0:1, :])
            core = slice(HALO, HALO + rows)
            da = da[core, :]
            if r0 < N_PAD:
                row = r0 + lax.broadcasted_iota(jnp.int32, (rows, 1), 0)
                da = jnp.where(row >= N_PAD, da, 0.0)
            dup_ref[0, r0:r0 + rows, :] = da.astype(BF16)
            dup_ref[1, r0:r0 + rows, :] = (dg_ext * acc * sg)[core, :].astype(BF16)
            dacc_c = dacc[core, :]
            gw[0] = gw[0] + jnp.sum(dacc_c * a2[core, :], axis=0, keepdims=True)
            gw[1] = gw[1] + jnp.sum(dacc_c * a1[core, :], axis=0, keepdims=True)
            gw[2] = gw[2] + jnp.sum(dacc_c * a0[core, :], axis=0, keepdims=True)
            gb = gb + jnp.sum(dacc_c, axis=0, keepdims=True)
        gcw_ref[...] = jnp.zeros((8, LANE), F32)
        for tap in range(3):
            gcw_ref[tap:tap + 1, :] = gw[tap]
        gcb_ref[...] = gb

    return _pcall(
        body, name=name, grid=(f // LANE,),
        in_specs=[pl.BlockSpec((None, t, LANE), lambda j: (0, 0, j)), pl.BlockSpec((None, t, LANE), lambda j: (1, 0, j)),
                  pl.BlockSpec((8, LANE), lambda j: (0, j)), pl.BlockSpec((1, LANE), lambda j: (0, j)),
                  pl.BlockSpec((t, LANE), lambda j: (0, j))],
        out_specs=[pl.BlockSpec((2, t, LANE), lambda j: (0, 0, j)), pl.BlockSpec((8, LANE), lambda j: (0, j)),
                   pl.BlockSpec((1, LANE), lambda j: (0, j))],
        out_shape=[jax.ShapeDtypeStruct((2, t, f), BF16), jax.ShapeDtypeStruct((8, f), F32),
                   jax.ShapeDtypeStruct((1, f), F32)],
        compiler_params=_params("parallel"),
    )(up, up, conv_w8, conv_b, dg)


def _exchange(arrays, kinds, name):
    n = len(arrays)
    npeer = N_DEV - 1

    def body(*refs):
        ins, outs = refs[:n], refs[n:2 * n]
        send_sems, recv_sems, local_sems = refs[2 * n:]
        x, y, c = lax.axis_index("x"), lax.axis_index("y"), lax.axis_index("c")
        me = 4 * x + 2 * y + c
        copies, locals_ = [], []
        for a in range(n):
            gather = kinds[a] == "gather"
            own = pltpu.make_async_copy(ins[a] if gather else ins[a].at[me], outs[a].at[me], local_sems.at[a])
            own.start()
            locals_.append(own)
            for d in range(1, N_DEV):
                px = 1 - x if d & 4 else x
                py = 1 - y if d & 2 else y
                pc = 1 - c if d & 1 else c
                src = ins[a] if gather else ins[a].at[4 * px + 2 * py + pc]
                cp = pltpu.make_async_remote_copy(
                    src_ref=src, dst_ref=outs[a].at[me],
                    send_sem=send_sems.at[a * npeer + d - 1], recv_sem=recv_sems.at[a * npeer + d - 1],
                    device_id=(px, py, pc), device_id_type=pl.DeviceIdType.MESH)
                cp.start()
                copies.append(cp)
        for cp in copies:
            cp.wait_recv()
        for cp in copies:
            cp.wait_send()
        for own in locals_:
            own.wait()

    out_shape = [jax.ShapeDtypeStruct((N_DEV,) + (a.shape if k == "gather" else a.shape[1:]), a.dtype)
                 for a, k in zip(arrays, kinds)]
    return _pcall(
        body, name=name,
        in_specs=[pl.BlockSpec(memory_space=pl.ANY)] * n,
        out_specs=[pl.BlockSpec(memory_space=pl.ANY)] * n,
        out_shape=out_shape,
        scratch_shapes=[pltpu.SemaphoreType.DMA((n * npeer,)), pltpu.SemaphoreType.DMA((n * npeer,)),
                        pltpu.SemaphoreType.DMA((n,))],
        compiler_params=pltpu.CompilerParams(has_side_effects=True),
    )(*arrays)


def _peer_copies(srcs, lands, kinds, send_sems, recv_sems):
    x, y, c = lax.axis_index("x"), lax.axis_index("y"), lax.axis_index("c")
    me = 4 * x + 2 * y + c
    copies = []
    for a in range(len(srcs)):
        for d in range(1, N_DEV):
            px = 1 - x if d & 4 else x
            py = 1 - y if d & 2 else y
            pc = 1 - c if d & 1 else c
            k = a * (N_DEV - 1) + d - 1
            copies.append(pltpu.make_async_remote_copy(
                src_ref=srcs[a] if kinds[a] == "gather" else srcs[a].at[4 * px + 2 * py + pc], dst_ref=lands[a].at[me],
                send_sem=send_sems.at[k], recv_sem=recv_sems.at[k],
                device_id=(px, py, pc), device_id_type=pl.DeviceIdType.MESH))
    return copies


def _exchange_start(arrays, kinds, name):
    n = len(arrays)
    nsem = n * (N_DEV - 1)
    hbm = pl.BlockSpec(memory_space=pltpu.HBM)
    sem = pl.BlockSpec(memory_space=pltpu.SEMAPHORE)
    land_shapes = [(N_DEV,) + (a.shape if k == "gather" else a.shape[1:]) for a, k in zip(arrays, kinds)]

    def body(*refs):
        srcs, lands = refs[:n], refs[n:2 * n]
        send_sems, recv_sems = refs[2 * n], refs[2 * n + 1]
        token = refs[-1]
        for cp in _peer_copies(srcs, lands, kinds, send_sems, recv_sems):
            cp.start()
        token[...] = jnp.zeros_like(token)

    operands = [pltpu.with_memory_space_constraint(a, pltpu.HBM) for a in arrays]
    operands += [pltpu.with_memory_space_constraint(lax.empty(s, a.dtype), pltpu.HBM) for s, a in zip(land_shapes, arrays)]
    out = _pcall(
        body, name=name,
        in_specs=[hbm] * (2 * n),
        out_specs=[sem, sem] + [hbm] * (2 * n) + [pl.BlockSpec(memory_space=pltpu.VMEM)],
        out_shape=[pltpu.SemaphoreType.DMA((nsem,)), pltpu.SemaphoreType.DMA((nsem,))]
        + [pltpu.HBM(a.shape, a.dtype) for a in arrays]
        + [pltpu.HBM(s, a.dtype) for s, a in zip(land_shapes, arrays)]
        + [jax.ShapeDtypeStruct((8, LANE), F32)],
        input_output_aliases={k: 2 + k for k in range(2 * n)},
        compiler_params=pltpu.CompilerParams(has_side_effects=pltpu.SideEffectType.DATAFLOW_SIDE_EFFECTING),
    )(*operands)
    return out[0], out[1], list(out[2:2 + n]), list(out[2 + n:2 + 2 * n]), out[-1]


def _exchange_wait(started, kinds, after, name):
    send_sems, recv_sems, srcs, lands, _ = started
    n = len(srcs)
    hbm = pl.BlockSpec(memory_space=pltpu.HBM)
    sem = pl.BlockSpec(memory_space=pltpu.SEMAPHORE)

    def body(*refs):
        src_refs, land_refs = refs[:n], refs[n:2 * n]
        copies = _peer_copies(src_refs, land_refs, kinds, refs[2 * n], refs[2 * n + 1])
        for cp in copies:
            cp.wait_send()
        for cp in copies:
            cp.wait_recv()

    out = _pcall(
        body, name=name,
        in_specs=[hbm] * (2 * n) + [sem, sem, pl.BlockSpec(memory_space=pl.ANY)],
        out_specs=[hbm] * (2 * n),
        out_shape=[pltpu.HBM(a.shape, a.dtype) for a in srcs + lands],
        input_output_aliases={k: k for k in range(2 * n)},
        compiler_params=pltpu.CompilerParams(has_side_effects=pltpu.SideEffectType.DATAFLOW_SIDE_EFFECTING),
    )(*srcs, *lands, send_sems, recv_sems, after)
    return list(out[n:])


def _with_own_slot(land, own, me):
    return lax.dynamic_update_slice(land, own[None], (me,) + (0,) * own.ndim)


def _sum_slots(slots, name, rows_tile):
    nd, r, c = slots.shape

    def body(s_ref, o_ref):
        acc = s_ref[0].astype(F32)
        for p in range(1, nd):
            acc = acc + s_ref[p].astype(F32)
        o_ref[...] = acc

    return _pcall(
        body, name=name, grid=(r // rows_tile,),
        in_specs=[pl.BlockSpec((nd, rows_tile, c), lambda i: (0, i, 0))],
        out_specs=pl.BlockSpec((rows_tile, c), lambda i: (i, 0)),
        out_shape=jax.ShapeDtypeStruct((r, c), F32),
        compiler_params=_params("parallel"),
    )(slots)


def _adamw(w, g, m, v, name, rows_tile):
    r, c = w.shape

    def body(w_ref, g_ref, m_ref, v_ref, d_ref, nm_ref, nv_ref):
        gr = g_ref[...]
        nm = ADAM_B1 * m_ref[...] + (1.0 - ADAM_B1) * gr
        nv = ADAM_B2 * v_ref[...] + (1.0 - ADAM_B2) * (gr * gr)
        m_hat = nm / (1.0 - ADAM_B1 ** ADAM_STEP)
        v_hat = nv / (1.0 - ADAM_B2 ** ADAM_STEP)
        d_ref[...] = -ADAM_LR * (m_hat / (jnp.sqrt(v_hat) + ADAM_EPS) + ADAM_WD * w_ref[...])
        nm_ref[...] = nm
        nv_ref[...] = nv

    spec = pl.BlockSpec((rows_tile, c), lambda i: (i, 0))
    shp = jax.ShapeDtypeStruct((r, c), F32)
    return _pcall(
        body, name=name, grid=(r // rows_tile,), in_specs=[spec] * 4, out_specs=[spec] * 3, out_shape=[shp] * 3,
        compiler_params=_params("parallel"),
    )(w, g, m, v)


F0 = 2 * RET_QK + 2 * RET_V


def _to_internal_rows(w_t):
    parts = [w_t[:F0]]
    for pair in range(FOX_PAIRS):
        for group in range(3):
            lo = F0 + group * FOX_W + LANE * pair
            parts.append(w_t[lo:lo + LANE])
    parts.append(w_t[F0 + 3 * FOX_W:])
    parts.append(jnp.zeros((IN_PAD - IN_WIDTH, w_t.shape[1]), w_t.dtype))
    return jnp.concatenate(parts, axis=0)


def _from_internal_rows(g_t):
    parts = [g_t[:F0]]
    for group in range(3):
        for pair in range(FOX_PAIRS):
            lo = F0 + 3 * LANE * pair + LANE * group
            parts.append(g_t[lo:lo + LANE])
    parts.append(g_t[F0 + 3 * FOX_W:F0 + 3 * FOX_W + FOX_HEADS])
    return jnp.concatenate(parts, axis=0)


def _local_step(x, target, meta, attn_g, fox_b, ret_g, ffn_g, conv_w8, conv_b, final_g,
                first_weight, late_weights, ffn_grads_ready, out_grad_ready, in_grad_ready):
    seq, d = x.shape
    t = seq + PREFIX
    tm = TOK_TILE
    nq = t // tm
    fox_b128 = jnp.pad(fox_b, ((0, 0), (0, LANE - FOX_HEADS)))

    h0, n1 = _prep_norm(x, meta, attn_g, "prep_norm")
    w_in_t = first_weight(n1)
    proj = _mm_simple(n1, w_in_t, mode="nt", tm=tm, tn=IN_PAD, tk=d, out_dtype=F32, name="mm_in")
    cos, sin = _rope_tables(t)
    o_pre, mixed, states = _ret_fwd(proj, cos, sin, ret_g, "ret_fwd")
    c = _forget_cumsum(proj, fox_b128, "forget_cumsum")
    qa, ka, va = _fox_prep(proj, c, "fox_prep")
    by_block = lambda a: a.reshape(FOX_HEADS, nq, tm, LANE)
    mixed, o_fox, lse = _fox_fwd(by_block(qa), by_block(ka), by_block(va), mixed, "fox_fwd")
    w_out, w_up_t, w_down = late_weights(o_fox)
    h1 = _mm_simple(mixed, w_out, mode="nn", tm=tm, tn=d, tk=d, out_dtype=F32, name="mm_out", add=h0)
    n2 = _rmsnorm(h1, ffn_g, "ffn_norm")
    nf = D_FF // 1408
    up = _matmul(
        n2, w_up_t, mode="nt", grid=(2 * nf, nq, 1),
        a_spec=pl.BlockSpec((tm, d), lambda j, i, k: (i, 0)),
        b_spec=pl.BlockSpec((None, 1408, d), lambda j, i, k: (j // nf, j % nf, 0)),
        o_spec=pl.BlockSpec((None, tm, 1408), lambda j, i, k: (j // nf, i, j % nf)),
        out_shape=jax.ShapeDtypeStruct((2, t, D_FF), F32), name="mm_up")
    g = _conv_gate_fwd(up, conv_w8, conv_b, "conv_gate_fwd")
    h2 = _mm_simple(g, w_down, mode="nn", tm=tm, tn=d, tk=D_FF, out_dtype=F32, name="mm_down", add=h1)

    loss_tile, dh2, g_final = _loss_bwd(h2, target, final_g, "loss_bwd")
    tkw = 1408 if t % 1408 == 0 else tm
    dg = _mm_simple(dh2, w_down, mode="nt", tm=tm, tn=D_FF, tk=d, out_dtype=F32, name="mm_dg")
    gw_down = _mm_simple(g, dh2, mode="tn", tm=1408, tn=d, tk=tkw, out_dtype=BF16, name="mm_gw_down")
    dup, g_conv_w8, g_conv_b = _conv_gate_bwd(up, conv_w8, conv_b, dg, "conv_gate_bwd")
    half = lambda p: pl.BlockSpec((None, tm, D_FF), lambda i, j, k: (p, i, 0))
    half_w = lambda p: pl.BlockSpec((None, D_FF, d), lambda i, j, k: (p, 0, 0), pipeline_mode=pl.Buffered(1))
    dn2 = _matmul(
        [dup, dup], [w_up_t, w_up_t], mode="nn", grid=(nq, 1, 1),
        a_spec=[half(0), half(1)], b_spec=[half_w(0), half_w(1)],
        o_spec=pl.BlockSpec((tm, d), lambda i, j, k: (i, 0)),
        out_shape=jax.ShapeDtypeStruct((t, d), F32), name="mm_dn2")
    gw_up_t = _matmul(
        dup, n2, mode="tn", grid=(2 * nf, 1, t // tkw),
        a_spec=pl.BlockSpec((None, tkw, 1408), lambda i, j, k: (i // nf, k, i % nf)),
        b_spec=pl.BlockSpec((tkw, d), lambda i, j, k: (k, 0)),
        o_spec=pl.BlockSpec((1408, d), lambda i, j, k: (i, 0)),
        out_shape=jax.ShapeDtypeStruct((2 * D_FF, d), BF16), name="mm_gw_up")
    dh1, g_ffn = _rmsnorm_bwd(dn2, h1, ffn_g + ffn_grads_ready(gw_down, gw_up_t), dh2, "ffn_norm_bwd")

    dmixed = _mm_simple(dh1, w_out, mode="nt", tm=tm, tn=d, tk=d, out_dtype=F32, name="mm_dmixed")
    gw_out = _mm_simple(mixed, dh1, mode="tn", tm=d, tn=d, tk=tkw, out_dtype=BF16, name="mm_gw_out")
    dproj, g_ret = _ret_bwd(proj, cos, sin, ret_g + out_grad_ready(gw_out), dmixed, o_pre, states, "ret_bwd")
    qab, doa = _fox_prep_bwd(dmixed, o_fox, lse, qa, "fox_prep_bwd")
    dproj, drs, dcs = _fox_bwd(by_block(qab), by_block(doa), by_block(ka), by_block(va), dproj, "fox_bwd")
    dproj, g_fox_b = _forget_cumsum_bwd(proj, fox_b128, drs, dcs, dproj, "forget_cumsum_bwd")
    dn1 = _mm_simple(dproj, w_in_t, mode="nn", tm=tm, tn=d, tk=IN_PAD, out_dtype=F32, name="mm_dn1")
    gw_in_t = _mm_simple(dproj, n1, mode="tn", tm=640, tn=d, tk=tkw, out_dtype=BF16, name="mm_gw_in")
    dh0, g_attn = _rmsnorm_bwd(dn1, h0, attn_g + in_grad_ready(gw_in_t), dh1, "attn_norm_bwd")

    grads = dict(meta=dh0[N_PAD:PREFIX], attn_g=g_attn, fox_b=g_fox_b[:, :FOX_HEADS], ret_g=g_ret,
                 ffn_g=g_ffn, conv_w=g_conv_w8[:3], conv_b=g_conv_b, final_g=g_final)
    return loss_tile, dh0[PREFIX:], grads


def _pack(pieces, width=LANE):
    rows = []
    for p in pieces:
        flat = p.reshape(-1)
        pad = (-flat.shape[0]) % width
        rows.append(jnp.pad(flat, (0, pad)).reshape(-1, width))
    out = jnp.concatenate(rows, axis=0)
    return jnp.pad(out, ((0, (-out.shape[0]) % 8), (0, 0)))


def _unpack(packed, shapes, width=LANE):
    outs, r = [], 0
    for shp in shapes:
        size = int(np.prod(shp))
        nrows = -(-size // width)
        outs.append(packed[r:r + nrows].reshape(-1)[:size].reshape(shp))
        r += nrows
    return outs


def kernel(x, meta_tokens, attn_norm_g, w_in, fox_forget_b, ret_norm_g, w_out, ffn_norm_g, w_up, conv_w, conv_b, w_down, final_norm_g, loss_target, m_meta_tokens, m_attn_norm_g, m_w_in, m_fox_forget_b, m_ret_norm_g, m_w_out, m_ffn_norm_g, m_w_up, m_conv_w, m_conv_b, m_w_down, m_final_norm_g, v_meta_tokens, v_attn_norm_g, v_w_in, v_fox_forget_b, v_ret_norm_g, v_w_out, v_ffn_norm_g, v_w_up, v_conv_w, v_conv_b, v_w_down, v_final_norm_g):
    d = D_MODEL
    me = 4 * lax.axis_index("x") + 2 * lax.axis_index("y") + lax.axis_index("c")
    in_blk = IN_WIDTH // N_DEV
    in_blk_pad = 400
    up_blk = 2 * D_FF // N_DEV
    down_blk = D_FF // N_DEV
    cw_blk = D_FF // N_DEV

    w_in_loc = jnp.pad(w_in[0].T.astype(BF16), ((0, in_blk_pad - in_blk), (0, 0)))
    cw_loc = jnp.pad(conv_w[0], ((0, 5), (0, 384 - cw_blk)))
    g_meta, g_cw = _exchange([meta_tokens, cw_loc], ["gather"] * 2, "gather_small")
    first = _exchange_start([w_in_loc], ["gather"], "gather_in_start")
    rest_loc = [(w_out[0] + first[-1][0:1, 0:1]).astype(BF16), w_up[0].T.astype(BF16), w_down[0].astype(BF16)]
    rest = _exchange_start(rest_loc, ["gather"] * 3, "gather_rest_start")
    meta_f = g_meta.transpose(1, 0, 2).reshape(N_META, d)
    conv_w8 = jnp.pad(g_cw[:, :3, :cw_blk].transpose(1, 0, 2).reshape(3, D_FF), ((0, 5), (0, 0)))
    pending = {}

    def first_weight(after):
        (land,) = _exchange_wait(first, ["gather"], after, "gather_in_wait")
        return _to_internal_rows(_with_own_slot(land, w_in_loc, me)[:, :in_blk].reshape(IN_WIDTH, d))

    def in_grad_ready(gw_in_t):
        blocks = _from_internal_rows(gw_in_t).reshape(N_DEV, in_blk, d)
        pending["in_own"] = [jnp.pad(blocks, ((0, 0), (0, in_blk_pad - in_blk), (0, 0)))]
        pending["in"] = _exchange_start(pending["in_own"], ["scatter"], "grads_in_start")
        return pending["in"][-1][0:1, 0:1]

    def late_weights(after):
        g_out, g_up, g_down = [_with_own_slot(land, own, me) for land, own in
                               zip(_exchange_wait(rest, ["gather"] * 3, after, "gather_rest_wait"), rest_loc)]
        return g_out.reshape(d, d), g_up.reshape(2, D_FF, d), g_down.reshape(D_FF, d)

    def ffn_grads_ready(gw_down, gw_up_t):
        pending["ffn_own"] = [gw_down.reshape(N_DEV, down_blk, d), gw_up_t.reshape(N_DEV, up_blk, d)]
        pending["ffn"] = _exchange_start(pending["ffn_own"], ["scatter"] * 2, "grads_ffn_start")
        return pending["ffn"][-1][0:1, 0:1]

    def out_grad_ready(gw_out):
        pending["out_own"] = [gw_out.reshape(N_DEV, d // N_DEV, d)]
        pending["out"] = _exchange_start(pending["out_own"], ["scatter"], "grads_out_start")
        return pending["out"][-1][0:1, 0:1]

    loss_tile, grad_x, gr = _local_step(
        x[0], loss_target[0], meta_f, attn_norm_g + rest[-1][0:1, 0:1], fox_forget_b, ret_norm_g, ffn_norm_g,
        conv_w8, conv_b, final_norm_g.reshape(1, d), first_weight, late_weights, ffn_grads_ready, out_grad_ready,
        in_grad_ready)

    small_shapes = [(1, LANE), (1, d), (1, FOX_HEADS), (1, RET_V), (1, d), (1, D_FF), (1, d), (N_META, d), (3, D_FF)]
    small = _pack([loss_tile[0:1], gr["attn_g"], gr["fox_b"], gr["ret_g"], gr["ffn_g"], gr["conv_b"], gr["final_g"],
                   gr["meta"], gr["conv_w"]])
    (r_small,) = _exchange([small], ["gather"], "exchange_small")
    own_block = lambda a: lax.dynamic_index_in_dim(a, me, axis=0, keepdims=False)
    r_down, r_up = [_with_own_slot(land, own_block(own), me) for land, own in
                    zip(_exchange_wait(pending["ffn"], ["scatter"] * 2, r_small, "grads_ffn_wait"), pending["ffn_own"])]
    (r_out,) = [_with_own_slot(land, own_block(own), me) for land, own in
                zip(_exchange_wait(pending["out"], ["scatter"], r_small, "grads_out_wait"), pending["out_own"])]
    g_w_out = _sum_slots(r_out, "sum_w_out", d // N_DEV)
    g_w_up = _sum_slots(r_up, "sum_w_up", up_blk).T
    g_w_down = _sum_slots(r_down, "sum_w_down", down_blk)
    s_all = _sum_slots(r_small, "sum_small", r_small.shape[1])
    (loss_row, g_attn, g_fox_b, g_ret, g_ffn, g_conv_b, g_final, g_meta_full, g_cw_full) = _unpack(s_all, small_shapes)
    loss = loss_row[0, 0]
    g_meta_loc = lax.dynamic_slice(g_meta_full, (0, me * (d // N_DEV)), (N_META, d // N_DEV))
    g_cw_loc = lax.dynamic_slice(g_cw_full, (0, me * cw_blk), (3, cw_blk))

    d_w_out, m_w_out_n, v_w_out_n = _adamw(w_out[0], g_w_out, m_w_out[0], v_w_out[0], "adamw_w_out", 128)
    d_w_up, m_w_up_n, v_w_up_n = _adamw(w_up[0], g_w_up, m_w_up[0], v_w_up[0], "adamw_w_up", 128)
    d_w_down, m_w_down_n, v_w_down_n = _adamw(w_down[0], g_w_down, m_w_down[0], v_w_down[0], "adamw_w_down", down_blk)
    (r_in,) = [_with_own_slot(land, own_block(own), me) for land, own in
               zip(_exchange_wait(pending["in"], ["scatter"], d_w_up, "grads_in_wait"), pending["in_own"])]
    g_w_in = _sum_slots(r_in, "sum_w_in", in_blk_pad)[:in_blk].T
    d_w_in, m_w_in_n, v_w_in_n = _adamw(w_in[0], g_w_in, m_w_in[0], v_w_in[0], "adamw_w_in", 128)
    sm_grads = [g_meta_loc, g_attn, g_fox_b, g_ret, g_ffn, g_cw_loc, g_conv_b, g_final.reshape(d)]
    sm_w = [meta_tokens, attn_norm_g, fox_forget_b, ret_norm_g, ffn_norm_g, conv_w[0], conv_b, final_norm_g]
    sm_m = [m_meta_tokens, m_attn_norm_g, m_fox_forget_b, m_ret_norm_g, m_ffn_norm_g, m_conv_w[0], m_conv_b, m_final_norm_g]
    sm_v = [v_meta_tokens, v_attn_norm_g, v_fox_forget_b, v_ret_norm_g, v_ffn_norm_g, v_conv_w[0], v_conv_b, v_final_norm_g]
    sm_shapes = [a.shape for a in sm_w]
    pk = [_pack(lst) for lst in (sm_w, sm_grads, sm_m, sm_v)]
    sm_d, sm_nm, sm_nv = _adamw(pk[0], pk[1], pk[2], pk[3], "adamw_small", pk[0].shape[0])
    dl = _unpack(sm_d, sm_shapes)
    ml = _unpack(sm_nm, sm_shapes)
    vl = _unpack(sm_nv, sm_shapes)

    def by_weight(meta_, attn_, w_in_, fox_, ret_, w_out_, ffn_, w_up_, cw_, cb_, w_down_, final_):
        return (meta_, attn_, w_in_[None], fox_, ret_, w_out_[None], ffn_, w_up_[None], cw_[None], cb_, w_down_[None], final_)

    grads_out = by_weight(g_meta_loc, g_attn, g_w_in, g_fox_b, g_ret, g_w_out, g_ffn, g_w_up, g_cw_loc, g_conv_b,
                          g_w_down, g_final.reshape(d))
    delta_out = by_weight(dl[0], dl[1], d_w_in, dl[2], dl[3], d_w_out, dl[4], d_w_up, dl[5], dl[6], d_w_down, dl[7])
    m_out = by_weight(ml[0], ml[1], m_w_in_n, ml[2], ml[3], m_w_out_n, ml[4], m_w_up_n, ml[5], ml[6], m_w_down_n, ml[7])
    v_out = by_weight(vl[0], vl[1], v_w_in_n, vl[2], vl[3], v_w_out_n, vl[4], v_w_up_n, vl[5], vl[6], v_w_down_n, vl[7])
    return (loss, grad_x[None]) + grads_out + delta_out + m_out + v_out
```

```python
import numpy as np
import jax
import jax.numpy as jnp
from jax import lax
from jax.experimental import pallas as pl
from jax.experimental.pallas import tpu as pltpu

F32 = jnp.float32
BF16 = jnp.bfloat16

D_MODEL = 1024
N_META = 16
N_PAD = 112
PREFIX = 128
RET_HEADS = 4
RET_DK = 64
RET_DV = 128
FOX_HEADS = 8
FOX_DH = 64
D_FF = 2816
ROPE_BASE = 10000.0
EPS = 1e-6
NEG = -1e30
RET_QK = RET_HEADS * RET_DK
RET_V = RET_HEADS * RET_DV
FOX_W = FOX_HEADS * FOX_DH
IN_WIDTH = 2 * RET_QK + 2 * RET_V + 3 * FOX_W + FOX_HEADS
IN_PAD = 3200
FF_COL_BLOCK = (IN_WIDTH - FOX_HEADS) // 128
QK_SCALE = 0.125

ADAM_LR = 0.001
ADAM_B1 = 0.9
ADAM_B2 = 0.999
ADAM_EPS = 1e-08
ADAM_WD = 0.01
ADAM_STEP = 10

N_DEV = 8
LANE = 128
ROW_TILE = 128
TOK_TILE = 384

NN = (((1,), (0,)), ((), ()))
NT = (((1,), (1,)), ((), ()))
TN = (((0,), (0,)), ((), ()))


def _pcall(body, **kw):
    return pl.pallas_call(body, **kw)


def _params(*sem):
    return pltpu.CompilerParams(dimension_semantics=sem)


def _dot(a, b, dims=NN):
    return lax.dot_general(a, b, dims, preferred_element_type=F32)


def _sigmoid(x):
    return 0.5 * jnp.tanh(0.5 * x) + 0.5


def _matmul(a, b, *, mode, grid, a_spec, b_spec, o_spec, out_shape, name, add=None, add_spec=None):
    dims = {"nn": NN, "nt": NT, "tn": TN}[mode]
    nk = grid[2]
    has_add = add is not None
    a_list, b_list = (list(a), list(b)) if isinstance(a, (list, tuple)) else ([a], [b])
    a_specs, b_specs = (list(a_spec), list(b_spec)) if isinstance(a_spec, (list, tuple)) else ([a_spec], [b_spec])
    nt = len(a_list)

    def body(*refs):
        a_refs, b_refs = refs[:nt], refs[nt:2 * nt]
        if has_add:
            add_ref, o_ref = refs[2 * nt:2 * nt + 2]
        else:
            o_ref = refs[2 * nt]
        part = _dot(a_refs[0][...].astype(BF16), b_refs[0][...].astype(BF16), dims)
        for ar, br in zip(a_refs[1:], b_refs[1:]):
            part = part + _dot(ar[...].astype(BF16), br[...].astype(BF16), dims)

        def finish(acc):
            if has_add:
                acc = acc + add_ref[...]
            o_ref[...] = acc.astype(o_ref.dtype)

        if nk == 1:
            finish(part)
        else:
            acc_ref = refs[-1]
            k = pl.program_id(2)

            @pl.when(k == 0)
            def _():
                acc_ref[...] = part

            @pl.when(k > 0)
            def _():
                acc_ref[...] += part

            @pl.when(k == nk - 1)
            def _():
                finish(acc_ref[...])

    in_specs = a_specs + b_specs + ([add_spec] if has_add else [])
    args = tuple(a_list) + tuple(b_list) + ((add,) if has_add else ())
    scratch = [] if nk == 1 else [pltpu.VMEM(tuple(d for d in o_spec.block_shape if d is not None), F32)]
    return _pcall(
        body, name=name, grid=grid, in_specs=in_specs, out_specs=o_spec, out_shape=out_shape,
        scratch_shapes=scratch, compiler_params=_params("parallel", "parallel", "arbitrary"),
    )(*args)


def _mm_simple(a, b, *, mode, tm, tn, tk, out_dtype, name, add=None):
    if mode == "tn":
        K, M = a.shape
    else:
        M, K = a.shape
    N = b.shape[0] if mode == "nt" else b.shape[1]
    grid = (M // tm, N // tn, K // tk)
    resident = dict(pipeline_mode=pl.Buffered(1)) if (tn == N and tk == K) else {}
    a_spec = pl.BlockSpec((tk, tm), lambda i, j, k: (k, i)) if mode == "tn" else pl.BlockSpec((tm, tk), lambda i, j, k: (i, k))
    b_spec = (pl.BlockSpec((tn, tk), lambda i, j, k: (j, k), **resident) if mode == "nt"
              else pl.BlockSpec((tk, tn), lambda i, j, k: (k, j), **resident))
    o_spec = pl.BlockSpec((tm, tn), lambda i, j, k: (i, j))
    return _matmul(a, b, mode=mode, grid=grid, a_spec=a_spec, b_spec=b_spec, o_spec=o_spec,
                   out_shape=jax.ShapeDtypeStruct((M, N), out_dtype), name=name, add=add,
                   add_spec=o_spec if add is not None else None)


def _prep_norm(x, meta, gain, name):
    seq, d = x.shape
    t = seq + PREFIX

    def body(xa_ref, xb_ref, xc_ref, meta_ref, g_ref, h_ref, n_ref):
        i = pl.program_id(0)

        @pl.when(i == 0)
        def _():
            h_ref[0:N_PAD, :] = jnp.zeros((N_PAD, d), F32)
            h_ref[N_PAD:ROW_TILE, :] = meta_ref[...]

        @pl.when(i > 0)
        def _():
            h_ref[0:ROW_TILE, :] = xa_ref[...]

        h_ref[ROW_TILE:2 * ROW_TILE, :] = xb_ref[...]
        h_ref[2 * ROW_TILE:3 * ROW_TILE, :] = xc_ref[...]
        h = h_ref[...]
        r = lax.rsqrt(jnp.mean(h * h, axis=-1, keepdims=True) + EPS)
        n_ref[...] = (h * r * g_ref[...]).astype(BF16)

    return _pcall(
        body, name=name, grid=(t // TOK_TILE,),
        in_specs=_shifted_row_specs(d) + [pl.BlockSpec((N_META, d), lambda i: (0, 0)), pl.BlockSpec((1, d), lambda i: (0, 0))],
        out_specs=[pl.BlockSpec((TOK_TILE, d), lambda i: (i, 0)), pl.BlockSpec((TOK_TILE, d), lambda i: (i, 0))],
        out_shape=[jax.ShapeDtypeStruct((t, d), F32), jax.ShapeDtypeStruct((t, d), BF16)],
        compiler_params=_params("parallel"),
    )(x, x, x, meta, gain)


def _shifted_row_specs(d):
    blocks_per_tile = TOK_TILE // ROW_TILE
    return [pl.BlockSpec((ROW_TILE, d), lambda i, r=r: (jnp.maximum(blocks_per_tile * i + r, 0), 0)) for r in (-1, 0, 1)]


def _rmsnorm(h, gain, name):
    t, d = h.shape

    def body(h_ref, g_ref, n_ref):
        x = h_ref[...]
        r = lax.rsqrt(jnp.mean(x * x, axis=-1, keepdims=True) + EPS)
        n_ref[...] = (x * r * g_ref[...]).astype(BF16)

    return _pcall(
        body, name=name, grid=(t // TOK_TILE,),
        in_specs=[pl.BlockSpec((TOK_TILE, d), lambda i: (i, 0)), pl.BlockSpec((1, d), lambda i: (0, 0))],
        out_specs=pl.BlockSpec((TOK_TILE, d), lambda i: (i, 0)),
        out_shape=jax.ShapeDtypeStruct((t, d), BF16),
        compiler_params=_params("parallel"),
    )(h, gain)


def _rmsnorm_bwd(dn, h, gain, dres, name):
    t, d = h.shape

    def body(dn_ref, h_ref, g_ref, dres_ref, dh_ref, gg_ref):
        i = pl.program_id(0)
        x = h_ref[...]
        r = lax.rsqrt(jnp.mean(x * x, axis=-1, keepdims=True) + EPS)
        xhat = x * r
        dy = dn_ref[...]
        u = dy * g_ref[...]
        dh_ref[...] = dres_ref[...] + r * (u - xhat * jnp.mean(u * xhat, axis=-1, keepdims=True))
        part = jnp.sum(dy * xhat, axis=0, keepdims=True)

        @pl.when(i == 0)
        def _():
            gg_ref[...] = part

        @pl.when(i > 0)
        def _():
            gg_ref[...] += part

    return _pcall(
        body, name=name, grid=(t // TOK_TILE,),
        in_specs=[pl.BlockSpec((TOK_TILE, d), lambda i: (i, 0)), pl.BlockSpec((TOK_TILE, d), lambda i: (i, 0)),
                  pl.BlockSpec((1, d), lambda i: (0, 0)), pl.BlockSpec((TOK_TILE, d), lambda i: (i, 0))],
        out_specs=[pl.BlockSpec((TOK_TILE, d), lambda i: (i, 0)), pl.BlockSpec((1, d), lambda i: (0, 0))],
        out_shape=[jax.ShapeDtypeStruct((t, d), F32), jax.ShapeDtypeStruct((1, d), F32)],
        compiler_params=_params("arbitrary"),
    )(dn, h, gain, dres)


def _loss_bwd(h2, target, gain, name):
    t, d = h2.shape

    def body(h_ref, ta_ref, tb_ref, tc_ref, g_ref, loss_ref, dh_ref, gg_ref):
        i = pl.program_id(0)

        @pl.when(i == 0)
        def _():
            loss_ref[...] = jnp.zeros_like(loss_ref)
            gg_ref[...] = jnp.zeros_like(gg_ref)

        x = h_ref[...]
        r = lax.rsqrt(jnp.mean(x * x, axis=-1, keepdims=True) + EPS)
        xhat = x * r
        g = g_ref[...]
        tgt = jnp.concatenate([ta_ref[...], tb_ref[...], tc_ref[...]], axis=0)
        counted = (i * TOK_TILE + lax.broadcasted_iota(jnp.int32, (TOK_TILE, 1), 0)) >= PREFIX
        err = jnp.where(counted, xhat * g - tgt, 0.0)
        loss_ref[...] += 0.5 * jnp.sum(jnp.mean(err * err, axis=-1, keepdims=True))
        dy = err * (1.0 / d)
        u = dy * g
        dh_ref[...] = r * (u - xhat * jnp.mean(u * xhat, axis=-1, keepdims=True))
        gg_ref[...] += jnp.sum(dy * xhat, axis=0, keepdims=True)

    return _pcall(
        body, name=name, grid=(t // TOK_TILE,),
        in_specs=[pl.BlockSpec((TOK_TILE, d), lambda i: (i, 0))] + _shifted_row_specs(d) + [pl.BlockSpec((1, d), lambda i: (0, 0))],
        out_specs=[pl.BlockSpec((8, LANE), lambda i: (0, 0)), pl.BlockSpec((TOK_TILE, d), lambda i: (i, 0)),
                   pl.BlockSpec((1, d), lambda i: (0, 0))],
        out_shape=[jax.ShapeDtypeStruct((8, LANE), F32), jax.ShapeDtypeStruct((t, d), F32),
                   jax.ShapeDtypeStruct((1, d), F32)],
        compiler_params=_params("arbitrary"),
    )(h2, target, target, target, gain)


def _ret_consts(bk):
    gam = 1.0 - 2.0 ** (-5.0 - np.arange(RET_HEADS))
    n = np.arange(bk)
    same_or_earlier_chunk = (n[None, :] // 64) <= (n[:, None] // 64)
    w = gam[:, None, None] ** np.abs(n[:, None] - n[None, :])[None] * same_or_earlier_chunk[None]
    wq = gam[:, None] ** (n[None, :] + 1.0)
    wk = gam[:, None] ** (bk - 1.0 - n[None, :])
    mask = (np.arange(RET_QK)[None, :] // RET_DK) == np.arange(RET_HEADS)[:, None]
    return (jnp.asarray(w, F32), jnp.asarray(wq[:, :, None], F32), jnp.asarray(wk[:, :, None], F32),
            jnp.asarray(mask[:, None, :], F32), [float(g ** bk) for g in gam])


def _rope_tables(t):
    half = RET_DK // 2
    inv = 1.0 / (ROPE_BASE ** (jnp.arange(half, dtype=F32) / half))
    ang = jnp.arange(t).astype(F32)[:, None] * inv[None, :]
    cos, sin = jnp.cos(ang), jnp.sin(ang)
    return (jnp.tile(jnp.concatenate([cos, cos], axis=1), (1, RET_HEADS)),
            jnp.tile(jnp.concatenate([-sin, sin], axis=1), (1, RET_HEADS)))


def _swap_halves(x):
    outs = []
    for s in range(x.shape[1] // LANE):
        xs = x[:, LANE * s:LANE * (s + 1)]
        lane = lax.broadcasted_iota(jnp.int32, xs.shape, 1)
        outs.append(jnp.where((lane & 32) == 0, pltpu.roll(xs, LANE - 32, axis=1), pltpu.roll(xs, 32, axis=1)))
    return outs[0] if len(outs) == 1 else jnp.concatenate(outs, axis=1)


def _rope(x, cos, sin_signed):
    return x * cos + _swap_halves(x) * sin_signed


def _rope_t(dx, cos, sin_signed):
    return dx * cos + _swap_halves(dx * sin_signed)


def _ret_fwd(proj, cos, sin, gain, name):
    t = proj.shape[0]
    bk = TOK_TILE
    nb = t // bk
    w, wq, wk, mask, g_blk = _ret_consts(bk)

    def body(q_ref, k_ref, v_ref, rg_ref, cos_ref, sin_ref, w_ref, wq_ref, wk_ref, mask_ref, gain_ref,
             opre_ref, og_ref, st_ref, r_ref):
        i = pl.program_id(0)

        @pl.when(i == 0)
        def _():
            r_ref[...] = jnp.zeros_like(r_ref)

        c, s = cos_ref[...], sin_ref[...]
        valid = ((i * bk + lax.broadcasted_iota(jnp.int32, (bk, 1), 0)) >= N_PAD).astype(F32)
        qr = _rope(q_ref[...], c, s)
        kr = _rope(k_ref[...], c, s) * QK_SCALE * valid
        kb = kr.astype(BF16)
        for h in range(RET_HEADS):
            hm = mask_ref[h]
            cols = slice(RET_DV * h, RET_DV * (h + 1))
            vh = v_ref[:, cols].astype(BF16)
            r_prev = r_ref[h]
            st_ref[0, h] = r_prev
            sm = _dot((qr * hm).astype(BF16), kb, NT) * w_ref[h]
            o = _dot(sm.astype(BF16), vh) + _dot((qr * (hm * wq_ref[h])).astype(BF16), r_prev.astype(BF16))
            r_ref[h] = g_blk[h] * r_prev + _dot((kr * wk_ref[h]).astype(BF16), vh, TN)
            opre_ref[:, cols] = o
            rstd = lax.rsqrt(jnp.mean(o * o, axis=-1, keepdims=True) + EPS)
            rg = rg_ref[:, cols]
            og_ref[:, cols] = (o * rstd * gain_ref[:, cols] * (rg * _sigmoid(rg))).astype(BF16)

    full = lambda shape: pl.BlockSpec(shape, lambda i: (0,) * len(shape))
    return _pcall(
        body, name=name, grid=(nb,),
        in_specs=[pl.BlockSpec((bk, RET_QK), lambda i: (i, 0)), pl.BlockSpec((bk, RET_QK), lambda i: (i, 1)),
                  pl.BlockSpec((bk, RET_V), lambda i: (i, 1)), pl.BlockSpec((bk, RET_V), lambda i: (i, 2)),
                  pl.BlockSpec((bk, RET_QK), lambda i: (i, 0)), pl.BlockSpec((bk, RET_QK), lambda i: (i, 0)),
                  full((RET_HEADS, bk, bk)), full((RET_HEADS, bk, 1)), full((RET_HEADS, bk, 1)),
                  full((RET_HEADS, 1, RET_QK)), full((1, RET_V))],
        out_specs=[pl.BlockSpec((bk, RET_V), lambda i: (i, 0)), pl.BlockSpec((bk, RET_V), lambda i: (i, 0)),
                   pl.BlockSpec((1, RET_HEADS, RET_QK, RET_DV), lambda i: (i, 0, 0, 0))],
        out_shape=[jax.ShapeDtypeStruct((t, RET_V), F32), jax.ShapeDtypeStruct((t, RET_V + FOX_W), BF16),
                   jax.ShapeDtypeStruct((nb, RET_HEADS, RET_QK, RET_DV), F32)],
        scratch_shapes=[pltpu.VMEM((RET_HEADS, RET_QK, RET_DV), F32)],
        compiler_params=_params("arbitrary"),
    )(proj, proj, proj, proj, cos, sin, w, wq, wk, mask, gain)


def _ret_bwd(proj, cos, sin, gain, dmixed, opre, states, name):
    t = proj.shape[0]
    bk = TOK_TILE
    nb = t // bk
    w, wq, wk, mask, g_blk = _ret_consts(bk)
    v0, g0 = 2 * RET_QK, 2 * RET_QK + RET_V

    def body(q_ref, k_ref, v_ref, rg_ref, cos_ref, sin_ref, w_ref, wq_ref, wk_ref, mask_ref, gain_ref,
             dog_ref, opre_ref, st_ref, dp_ref, gg_ref, dr_ref):
        step = pl.program_id(0)
        i = nb - 1 - step

        @pl.when(step == 0)
        def _():
            dr_ref[...] = jnp.zeros_like(dr_ref)
            gg_ref[...] = jnp.zeros_like(gg_ref)

        c, s = cos_ref[...], sin_ref[...]
        valid = ((i * bk + lax.broadcasted_iota(jnp.int32, (bk, 1), 0)) >= N_PAD).astype(F32)
        qr = _rope(q_ref[...], c, s)
        kr = _rope(k_ref[...], c, s) * QK_SCALE * valid
        kb = kr.astype(BF16)
        dqr = jnp.zeros((bk, RET_QK), F32)
        dkr = jnp.zeros((bk, RET_QK), F32)
        for h in range(RET_HEADS):
            hm = mask_ref[h]
            cols = slice(RET_DV * h, RET_DV * (h + 1))
            vh = v_ref[:, cols].astype(BF16)
            o = opre_ref[:, cols]
            rstd = lax.rsqrt(jnp.mean(o * o, axis=-1, keepdims=True) + EPS)
            xhat = o * rstd
            rg = rg_ref[:, cols]
            sg = _sigmoid(rg)
            gate = rg * sg
            gn = gain_ref[:, cols]
            dog = dog_ref[:, cols]
            dp_ref[:, g0 + RET_DV * h:g0 + RET_DV * (h + 1)] = (
                dog * xhat * gn * (sg * (1.0 + rg * (1.0 - sg)))).astype(BF16)
            gg_ref[:, cols] += jnp.sum(dog * xhat * gate, axis=0, keepdims=True)
            dxh = dog * gn * gate
            do = (rstd * (dxh - xhat * jnp.mean(dxh * xhat, axis=-1, keepdims=True))).astype(BF16)
            qm = (qr * hm).astype(BF16)
            qw = (qr * (hm * wq_ref[h])).astype(BF16)
            kw = (kr * wk_ref[h]).astype(BF16)
            wh = w_ref[h]
            sm = (_dot(qm, kb, NT) * wh).astype(BF16)
            ds = (_dot(do, vh, NT) * wh).astype(BF16)
            dr = dr_ref[h]
            drb = dr.astype(BF16)
            dp_ref[:, v0 + RET_DV * h:v0 + RET_DV * (h + 1)] = (_dot(sm, do, TN) + _dot(kw, drb)).astype(BF16)
            dqr = dqr + _dot(ds, kb) * hm + _dot(do, st_ref[0, h].astype(BF16), NT) * (hm * wq_ref[h])
            dkr = dkr + _dot(ds, qm, TN) + _dot(vh, drb, NT) * wk_ref[h]
            dr_ref[h] = g_blk[h] * dr + _dot(qw, do, TN)
        dp_ref[:, 0:RET_QK] = _rope_t(dqr, c, s).astype(BF16)
        dp_ref[:, RET_QK:2 * RET_QK] = _rope_t(dkr * (QK_SCALE * valid), c, s).astype(BF16)

    full = lambda shape: pl.BlockSpec(shape, lambda i: (0,) * len(shape))
    rev = lambda col: (lambda i: (nb - 1 - i, col))
    return _pcall(
        body, name=name, grid=(nb,),
        in_specs=[pl.BlockSpec((bk, RET_QK), rev(0)), pl.BlockSpec((bk, RET_QK), rev(1)),
                  pl.BlockSpec((bk, RET_V), rev(1)), pl.BlockSpec((bk, RET_V), rev(2)),
                  pl.BlockSpec((bk, RET_QK), rev(0)), pl.BlockSpec((bk, RET_QK), rev(0)),
                  full((RET_HEADS, bk, bk)), full((RET_HEADS, bk, 1)), full((RET_HEADS, bk, 1)),
                  full((RET_HEADS, 1, RET_QK)), full((1, RET_V)),
                  pl.BlockSpec((bk, RET_V), rev(0)), pl.BlockSpec((bk, RET_V), rev(0)),
                  pl.BlockSpec((1, RET_HEADS, RET_QK, RET_DV), lambda i: (nb - 1 - i, 0, 0, 0))],
        out_specs=[pl.BlockSpec((bk, g0 + RET_V), rev(0)), pl.BlockSpec((1, RET_V), lambda i: (0, 0))],
        out_shape=[jax.ShapeDtypeStruct((t, IN_PAD), BF16), jax.ShapeDtypeStruct((1, RET_V), F32)],
        scratch_shapes=[pltpu.VMEM((RET_HEADS, RET_QK, RET_DV), F32)],
        compiler_params=_params("arbitrary"),
    )(proj, proj, proj, proj, cos, sin, w, wq, wk, mask, gain, dmixed, opre, states)


def _forget_cumsum(proj, bias, name):
    t = proj.shape[0]
    rt = TOK_TILE
    nb = t // rt
    tril = jnp.asarray(np.tril(np.ones((rt, rt))), F32)

    def body(z_ref, b_ref, tril_ref, c_ref, carry_ref):
        i = pl.program_id(0)

        @pl.when(i == 0)
        def _():
            carry_ref[...] = jnp.zeros_like(carry_ref)

        z = z_ref[...] + b_ref[...]
        logf = jnp.minimum(z, 0.0) - jnp.log(1.0 + jnp.exp(-jnp.abs(z)))
        c = lax.dot_general(tril_ref[...], logf, NN, precision=lax.Precision.HIGHEST,
                            preferred_element_type=F32) + carry_ref[...]
        c_ref[...] = c
        carry_ref[...] = c[rt - 1:rt, :]

    return _pcall(
        body, name=name, grid=(nb,),
        in_specs=[pl.BlockSpec((rt, LANE), lambda i: (i, FF_COL_BLOCK)), pl.BlockSpec((1, LANE), lambda i: (0, 0)),
                  pl.BlockSpec((rt, rt), lambda i: (0, 0))],
        out_specs=pl.BlockSpec((rt, LANE), lambda i: (i, 0)),
        out_shape=jax.ShapeDtypeStruct((t, LANE), F32),
        scratch_shapes=[pltpu.VMEM((1, LANE), F32)],
        compiler_params=_params("arbitrary"),
    )(proj, bias, tril)


def _forget_cumsum_bwd(proj, bias, drs, dcs, dproj, name):
    t = proj.shape[0]
    rt = TOK_TILE
    nb = t // rt
    triu = jnp.asarray(np.triu(np.ones((rt, rt))), F32)

    def body(z_ref, b_ref, triu_ref, drs_ref, dcs_ref, dproj_in, dz_ref, gb_ref, carry_ref):
        step = pl.program_id(0)

        @pl.when(step == 0)
        def _():
            carry_ref[...] = jnp.zeros_like(carry_ref)
            gb_ref[...] = jnp.zeros_like(gb_ref)

        dlogf = lax.dot_general(triu_ref[...], drs_ref[...] - dcs_ref[...], NN, precision=lax.Precision.HIGHEST,
                                preferred_element_type=F32) + carry_ref[...]
        carry_ref[...] = dlogf[0:1, :]
        z = z_ref[...] + b_ref[...]
        is_head = lax.broadcasted_iota(jnp.int32, (rt, LANE), 1) < FOX_HEADS
        dz = jnp.where(is_head, dlogf / (1.0 + jnp.exp(z)), 0.0)
        dz_ref[...] = dz.astype(BF16)
        gb_ref[...] += jnp.sum(dz, axis=0, keepdims=True)

    return _pcall(
        body, name=name, grid=(nb,),
        in_specs=[pl.BlockSpec((rt, LANE), lambda i: (nb - 1 - i, FF_COL_BLOCK)),
                  pl.BlockSpec((1, LANE), lambda i: (0, 0)),
                  pl.BlockSpec((rt, rt), lambda i: (0, 0)),
                  pl.BlockSpec((rt, LANE), lambda i: (nb - 1 - i, 0)),
                  pl.BlockSpec((rt, LANE), lambda i: (nb - 1 - i, 0)),
                  pl.BlockSpec(memory_space=pl.ANY)],
        out_specs=[pl.BlockSpec((rt, LANE), lambda i: (nb - 1 - i, FF_COL_BLOCK)),
                   pl.BlockSpec((1, LANE), lambda i: (0, 0))],
        out_shape=[jax.ShapeDtypeStruct(dproj.shape, BF16), jax.ShapeDtypeStruct((1, LANE), F32)],
        input_output_aliases={5: 0},
        scratch_shapes=[pltpu.VMEM((1, LANE), F32)],
        compiler_params=_params("arbitrary"),
    )(proj, bias, triu, drs, dcs, dproj)


FOX_PAIRS = FOX_HEADS // 2
L_ONE_Q = FOX_DH
L_ONE_K = FOX_DH + 3
L_LSE = FOX_DH + 4


def _split3(x):
    hi = x.astype(BF16).astype(F32)
    r = x - hi
    mid = r.astype(BF16).astype(F32)
    return hi, mid, r - mid


def _head_to_low(slab, e):
    return slab if e == 0 else pltpu.roll(slab, FOX_DH, axis=1)


def _pair(a, b, low):
    return jnp.where(low, a, pltpu.roll(b, FOX_DH, axis=1))


def _fox_prep(proj, c, name):
    t = proj.shape[0]
    tq = TOK_TILE

    def body(p_ref, c_ref, qa_ref, ka_ref, va_ref):
        i = pl.program_id(0)
        lane = lax.broadcasted_iota(jnp.int32, (tq, LANE), 1)
        low = lane < FOX_DH
        live = (i * tq + lax.broadcasted_iota(jnp.int32, (tq, 1), 0)) >= N_PAD
        q_tail = jnp.where(lane < L_ONE_Q + 3, 1.0, 0.0)
        k_ones = (lane >= L_ONE_K) & (lane < L_ONE_K + 4)
        v_tail = jnp.where(lane < FOX_DH + 2, 1.0, 0.0)
        for pair in range(FOX_PAIRS):
            base = 3 * LANE * pair
            for e in range(2):
                h = 2 * pair + e
                q = _head_to_low(p_ref[:, base:base + LANE], e)
                k = _head_to_low(p_ref[:, base + LANE:base + 2 * LANE], e)
                v = _head_to_low(p_ref[:, base + 2 * LANE:base + 3 * LANE], e)
                hi, mid, lo = _split3(jnp.where(live, -c_ref[:, h:h + 1], NEG))
                ka = jnp.where(low, k, jnp.where(k_ones, 1.0, 0.0))
                ka = jnp.where(lane == L_ONE_Q, hi, jnp.where(lane == L_ONE_Q + 1, mid, jnp.where(lane == L_ONE_Q + 2, lo, ka)))
                qa_ref[h] = jnp.where(low, q * QK_SCALE, q_tail).astype(BF16)
                ka_ref[h] = ka.astype(BF16)
                va_ref[h] = jnp.where(low, v, v_tail).astype(BF16)

    out = jax.ShapeDtypeStruct((FOX_HEADS, t, LANE), BF16)
    ospec = pl.BlockSpec((FOX_HEADS, tq, LANE), lambda i: (0, i, 0))
    return _pcall(
        body, name=name, grid=(t // tq,),
        in_specs=[pl.BlockSpec((tq, 3 * FOX_W), lambda i: (i, 1)), pl.BlockSpec((tq, LANE), lambda i: (i, 0))],
        out_specs=[ospec, ospec, ospec], out_shape=[out, out, out],
        compiler_params=_params("parallel"),
    )(proj, c)


STEP_PAIRS = 2
STEP_HEADS = 2 * STEP_PAIRS
FOX_GROUPS = FOX_PAIRS // STEP_PAIRS


def _blockdiag(a, b):
    z = jnp.zeros_like(a)
    return jnp.concatenate([jnp.concatenate([a, z], axis=1), jnp.concatenate([z, b], axis=1)], axis=0)


def _fox_fwd(qa, ka, va, mixed, name):
    nh, nq, tq, _ = qa.shape
    t = nq * tq

    def body(qa_ref, ka_ref, va_ref, mixed_in, mixed_ref, o_ref, lse_ref):
        i = pl.program_id(1)
        lane = lax.broadcasted_iota(jnp.int32, (tq, LANE), 1)
        causal = lax.broadcasted_iota(jnp.int32, (tq, tq), 1) <= lax.broadcasted_iota(jnp.int32, (tq, tq), 0)
        qps = [jnp.concatenate([qa_ref[2 * c], qa_ref[2 * c + 1]], axis=1) for c in range(STEP_PAIRS)]

        def step(j, carry, diagonal):
            scores = [_dot(qps[c], _blockdiag(ka_ref[2 * c, j], ka_ref[2 * c + 1, j]), NT) for c in range(STEP_PAIRS)]
            new = []
            for c in range(STEP_PAIRS):
                ms, acc = carry[c]
                ps, ms_new, alphas = [], [], []
                for e in range(2):
                    s = scores[c][:, e * tq:(e + 1) * tq]
                    if diagonal:
                        s = jnp.where(causal, s, NEG)
                    m_new = jnp.maximum(ms[e], jnp.max(s, axis=-1, keepdims=True))
                    ps.append(jnp.exp(s - m_new).astype(BF16))
                    ms_new.append(m_new)
                    alphas.append(jnp.broadcast_to(jnp.exp(ms[e] - m_new), (tq, LANE)))
                pv = _dot(jnp.concatenate(ps, axis=1), _blockdiag(va_ref[2 * c, j], va_ref[2 * c + 1, j]))
                new.append((tuple(ms_new), jnp.concatenate(alphas, axis=1) * acc + pv))
            return tuple(new)

        m0 = jnp.full((tq, 1), NEG, F32)
        init = tuple(((m0, m0), jnp.zeros((tq, 2 * LANE), F32)) for _ in range(STEP_PAIRS))
        carry = lax.fori_loop(0, i, lambda j, cr: step(j, cr, False), init)
        o_pairs = []
        lse = jnp.zeros((tq, LANE), F32)
        for c, (ms, acc) in enumerate(step(i, carry, True)):
            outs = []
            for e in range(2):
                half = acc[:, e * LANE:(e + 1) * LANE]
                l = half[:, FOX_DH:FOX_DH + 1]
                outs.append(half / l)
                lse = jnp.where(lane == 2 * c + e, ms[e] + jnp.log(l), lse)
            o_pairs.append(_pair(outs[0], outs[1], lane < FOX_DH))
        o_all = jnp.concatenate(o_pairs, axis=1)
        mixed_ref[...] = o_all.astype(BF16)
        o_ref[...] = o_all
        lse_ref[...] = lse

    width = STEP_PAIRS * LANE
    whole = pl.BlockSpec((STEP_HEADS, nq, tq, LANE), lambda g, i: (g, 0, 0, 0), pipeline_mode=pl.Buffered(1))
    return _pcall(
        body, name=name, grid=(FOX_GROUPS, nq),
        in_specs=[pl.BlockSpec((STEP_HEADS, None, tq, LANE), lambda g, i: (g, i, 0, 0)), whole, whole,
                  pl.BlockSpec(memory_space=pl.ANY)],
        out_specs=[pl.BlockSpec((tq, width), lambda g, i: (i, RET_V // width + g)),
                   pl.BlockSpec((tq, width), lambda g, i: (i, g)),
                   pl.BlockSpec((None, tq, LANE), lambda g, i: (g, i, 0))],
        out_shape=[jax.ShapeDtypeStruct(mixed.shape, BF16), jax.ShapeDtypeStruct((t, FOX_W), F32),
                   jax.ShapeDtypeStruct((FOX_GROUPS, t, LANE), F32)],
        input_output_aliases={3: 0},
        compiler_params=_params("parallel", "parallel"),
    )(qa, ka, va, mixed)


def _fox_prep_bwd(dmixed, o_fox, lse, qa, name):
    t = dmixed.shape[0]
    tq = TOK_TILE

    def body(dm_ref, o_ref, lse_ref, qa_ref, qab_ref, doa_ref):
        i = pl.program_id(0)
        lane = lax.broadcasted_iota(jnp.int32, (tq, LANE), 1)
        low = lane < FOX_DH
        live = (i * tq + lax.broadcasted_iota(jnp.int32, (tq, 1), 0)) >= N_PAD
        for pair in range(FOX_PAIRS):
            cols = slice(LANE * pair, LANE * (pair + 1))
            d_slab = dm_ref[:, cols]
            prod = d_slab * o_ref[:, cols]
            for e in range(2):
                h = 2 * pair + e
                nd = -jnp.sum(jnp.where(low, _head_to_low(prod, e), 0.0), axis=-1, keepdims=True)
                nd_hi = nd.astype(BF16).astype(F32)
                doa = jnp.where(low, _head_to_low(d_slab, e), 0.0)
                doa = jnp.where(lane == FOX_DH, nd_hi, jnp.where(lane == FOX_DH + 1, nd - nd_hi, doa))
                doa_ref[h] = doa.astype(BF16)
                lse_h = lse_ref[h // STEP_HEADS][:, h % STEP_HEADS:h % STEP_HEADS + 1]
                hi, mid, lo = _split3(jnp.where(live, -lse_h, 0.0))
                qab = qa_ref[h].astype(F32)
                qab = jnp.where(lane == L_LSE, hi, jnp.where(lane == L_LSE + 1, mid, jnp.where(lane == L_LSE + 2, lo, qab)))
                qab_ref[h] = qab.astype(BF16)

    out = jax.ShapeDtypeStruct((FOX_HEADS, t, LANE), BF16)
    hspec = pl.BlockSpec((FOX_HEADS, tq, LANE), lambda i: (0, i, 0))
    return _pcall(
        body, name=name, grid=(t // tq,),
        in_specs=[pl.BlockSpec((tq, FOX_W), lambda i: (i, 1)), pl.BlockSpec((tq, FOX_W), lambda i: (i, 0)),
                  pl.BlockSpec((FOX_GROUPS, tq, LANE), lambda i: (0, i, 0)), hspec],
        out_specs=[hspec, hspec], out_shape=[out, out],
        compiler_params=_params("parallel"),
    )(dmixed, o_fox, lse, qa)


def _fox_bwd(qab, doa, ka, va, dproj, name):
    nh, nq, tq, _ = qab.shape
    t = nq * tq
    slab = 3 * LANE * STEP_PAIRS
    group0 = (2 * RET_QK + 2 * RET_V) // slab

    def body(qab_ref, doa_ref, ka_ref, va_ref, dproj_in, dp_ref, drs_ref, dcs_ref, dq_ref):
        g, j = pl.program_id(0), pl.program_id(1)

        @pl.when((g == 0) & (j == 0))
        def _():
            drs_ref[...] = jnp.zeros_like(drs_ref)
            dcs_ref[...] = jnp.zeros_like(dcs_ref)

        @pl.when(j == 0)
        def _():
            dq_ref[...] = jnp.zeros_like(dq_ref)

        lane = lax.broadcasted_iota(jnp.int32, (tq, LANE), 1)
        low = lane < FOX_DH
        key_le_query = lax.broadcasted_iota(jnp.int32, (tq, tq), 0) <= lax.broadcasted_iota(jnp.int32, (tq, tq), 1)

        def by_head(c, a, b, col):
            h = STEP_HEADS * g + 2 * c
            return jnp.where(lane == h, a[:, col:col + 1], jnp.where(lane == h + 1, b[:, col:col + 1], 0.0))

        kbs = [ka_ref[h] for h in range(STEP_HEADS)]
        vbs = [va_ref[h] for h in range(STEP_HEADS)]

        def step(i, carry, diagonal):
            qbs = [qab_ref[h, i] for h in range(STEP_HEADS)]
            dobs = [doa_ref[h, i] for h in range(STEP_HEADS)]
            st = [_dot(kbs[h], qbs[h], NT) for h in range(STEP_HEADS)]
            dpt = [_dot(vbs[h], dobs[h], NT) for h in range(STEP_HEADS)]
            new = []
            for h in range(STEP_HEADS):
                p = jnp.exp(st[h])
                if diagonal:
                    p = jnp.where(key_le_query, p, 0.0)
                ds = (p * dpt[h]).astype(BF16)
                dq_ref[h, i] += _dot(ds, kbs[h], TN)
                dk, dv = carry[h]
                new.append((dk + _dot(ds, qbs[h]), dv + _dot(p.astype(BF16), dobs[h])))
            return tuple(new)

        zero = jnp.zeros((tq, LANE), F32)
        carry = step(j, tuple((zero, zero) for _ in range(STEP_HEADS)), True)
        carry = lax.fori_loop(j + 1, nq, lambda i, cr: step(i, cr, False), carry)
        rows = pl.ds(pl.multiple_of(j * tq, tq), tq)
        for c in range(STEP_PAIRS):
            (dka, dva), (dkb, dvb) = carry[2 * c], carry[2 * c + 1]
            c0 = 3 * LANE * c
            dp_ref[rows, c0 + LANE:c0 + 2 * LANE] = _pair(dka, dkb, low).astype(BF16)
            dp_ref[rows, c0 + 2 * LANE:c0 + 3 * LANE] = _pair(dva, dvb, low).astype(BF16)
            dcs_ref[rows, :] += by_head(c, dka, dkb, L_ONE_Q)

        @pl.when(j == nq - 1)
        def _():
            for c in range(STEP_PAIRS):
                for blk in range(nq):
                    r = slice(blk * tq, (blk + 1) * tq)
                    a, b = dq_ref[2 * c, blk], dq_ref[2 * c + 1, blk]
                    dp_ref[r, 3 * LANE * c:3 * LANE * c + LANE] = (_pair(a, b, low) * QK_SCALE).astype(BF16)
                    drs_ref[r, :] += by_head(c, a, b, L_ONE_K)

    whole = pl.BlockSpec((STEP_HEADS, nq, tq, LANE), lambda g, j: (g, 0, 0, 0), pipeline_mode=pl.Buffered(1))
    blk = pl.BlockSpec((STEP_HEADS, None, tq, LANE), lambda g, j: (g, j, 0, 0))
    sums = pl.BlockSpec((t, LANE), lambda g, j: (0, 0), pipeline_mode=pl.Buffered(1))
    return _pcall(
        body, name=name, grid=(FOX_GROUPS, nq),
        in_specs=[whole, whole, blk, blk, pl.BlockSpec(memory_space=pl.ANY)],
        out_specs=[pl.BlockSpec((t, slab), lambda g, j: (0, group0 + g)), sums, sums],
        out_shape=[jax.ShapeDtypeStruct(dproj.shape, BF16), jax.ShapeDtypeStruct((t, LANE), F32),
                   jax.ShapeDtypeStruct((t, LANE), F32)],
        input_output_aliases={4: 0},
        scratch_shapes=[pltpu.VMEM((STEP_HEADS, nq, tq, LANE), F32)],
        compiler_params=_params("arbitrary", "arbitrary"),
    )(qab, doa, ka, va, dproj)


HALO = 8


def _rows_ext(ref, r0, rows, t, before, after):
    lo, hi = r0 - before, r0 + rows + after
    parts = []
    if lo < 0:
        parts.append(jnp.zeros((-lo, LANE), F32))
    parts.append(ref[max(lo, 0):min(hi, t), :].astype(F32))
    if hi > t:
        parts.append(jnp.zeros((hi - t, LANE), F32))
    return parts[0] if len(parts) == 1 else jnp.concatenate(parts, axis=0)


def _conv_taps(a_ext, r0_ext, cw_ref, cb_ref):
    n = a_ext.shape[0]
    if r0_ext < N_PAD:
        row = r0_ext + lax.broadcasted_iota(jnp.int32, (n, 1), 0)
        a_ext = jnp.where(row >= N_PAD, a_ext, 0.0)
    a1 = pltpu.roll(a_ext, 1, axis=0)
    a2 = pltpu.roll(a_ext, 2, axis=0)
    acc = cb_ref[...] + a2 * cw_ref[0:1, :] + a1 * cw_ref[1:2, :] + a_ext * cw_ref[2:3, :]
    return a_ext, a1, a2, acc


def _conv_gate_fwd(up, conv_w8, conv_b, name):
    _, t, f = up.shape
    rows = TOK_TILE

    def body(a_ref, b_ref, cw_ref, cb_ref, g_ref):
        for r0 in range(0, t, rows):
            a_ext = _rows_ext(a_ref, r0, rows, t, HALO, 0)
            _, _, _, acc = _conv_taps(a_ext, r0 - HALO, cw_ref, cb_ref)
            acc = acc[HALO:, :]
            g_ref[r0:r0 + rows, :] = (acc * _sigmoid(acc) * b_ref[r0:r0 + rows, :]).astype(BF16)

    return _pcall(
        body, name=name, grid=(f // LANE,),
        in_specs=[pl.BlockSpec((None, t, LANE), lambda j: (0, 0, j)), pl.BlockSpec((None, t, LANE), lambda j: (1, 0, j)),
                  pl.BlockSpec((8, LANE), lambda j: (0, j)), pl.BlockSpec((1, LANE), lambda j: (0, j))],
        out_specs=pl.BlockSpec((t, LANE), lambda j: (0, j)),
        out_shape=jax.ShapeDtypeStruct((t, f), BF16),
        compiler_params=_params("parallel"),
    )(up, up, conv_w8, conv_b)


def _conv_gate_bwd(up, conv_w8, conv_b, dg, name):
    _, t, f = up.shape
    rows = TOK_TILE

    def body(a_ref, b_ref, cw_ref, cb_ref, dg_ref, dup_ref, gcw_ref, gcb_ref):
        gw = [jnp.zeros((1, LANE), F32) for _ in range(3)]
        gb = jnp.zeros((1, LANE), F32)
        for r0 in range(0, t, rows):
            a_ext = _rows_ext(a_ref, r0, rows, t, HALO, HALO)
            b_ext = _rows_ext(b_ref, r0, rows, t, HALO, HALO)
            dg_ext = _rows_ext(dg_ref, r0, rows, t, HALO, HALO)
            a0, a1, a2, acc = _conv_taps(a_ext, r0 - HALO, cw_ref, cb_ref)
            sg = _sigmoid(acc)
            dacc = dg_ext * b_ext * (sg * (1.0 + acc * (1.0 - sg)))
            n = dacc.shape[0]
            da = (dacc * cw_ref[2:3, :] + pltpu.roll(dacc, n - 1, axis=0) * cw_ref[1:2, :]
                  + pltpu.roll(dacc, n - 2, axis=0) * cw_ref[0:1, :])
            core = slice(HALO, HALO + rows)
            da = da[core, :]
            if r0 < N_PAD:
                row = r0 + lax.broadcasted_iota(jnp.int32, (rows, 1), 0)
                da = jnp.where(row >= N_PAD, da, 0.0)
            dup_ref[0, r0:r0 + rows, :] = da.astype(BF16)
            dup_ref[1, r0:r0 + rows, :] = (dg_ext * acc * sg)[core, :].astype(BF16)
            dacc_c = dacc[core, :]
            gw[0] = gw[0] + jnp.sum(dacc_c * a2[core, :], axis=0, keepdims=True)
            gw[1] = gw[1] + jnp.sum(dacc_c * a1[core, :], axis=0, keepdims=True)
            gw[2] = gw[2] + jnp.sum(dacc_c * a0[core, :], axis=0, keepdims=True)
            gb = gb + jnp.sum(dacc_c, axis=0, keepdims=True)
        gcw_ref[...] = jnp.zeros((8, LANE), F32)
        for tap in range(3):
            gcw_ref[tap:tap + 1, :] = gw[tap]
        gcb_ref[...] = gb

    return _pcall(
        body, name=name, grid=(f // LANE,),
        in_specs=[pl.BlockSpec((None, t, LANE), lambda j: (0, 0, j)), pl.BlockSpec((None, t, LANE), lambda j: (1, 0, j)),
                  pl.BlockSpec((8, LANE), lambda j: (0, j)), pl.BlockSpec((1, LANE), lambda j: (0, j)),
                  pl.BlockSpec((t, LANE), lambda j: (0, j))],
        out_specs=[pl.BlockSpec((2, t, LANE), lambda j: (0, 0, j)), pl.BlockSpec((8, LANE), lambda j: (0, j)),
                   pl.BlockSpec((1, LANE), lambda j: (0, j))],
        out_shape=[jax.ShapeDtypeStruct((2, t, f), BF16), jax.ShapeDtypeStruct((8, f), F32),
                   jax.ShapeDtypeStruct((1, f), F32)],
        compiler_params=_params("parallel"),
    )(up, up, conv_w8, conv_b, dg)


def _exchange(arrays, kinds, name):
    n = len(arrays)
    npeer = N_DEV - 1

    def body(*refs):
        ins, outs = refs[:n], refs[n:2 * n]
        send_sems, recv_sems, local_sems = refs[2 * n:]
        x, y, c = lax.axis_index("x"), lax.axis_index("y"), lax.axis_index("c")
        me = 4 * x + 2 * y + c
        copies, locals_ = [], []
        for a in range(n):
            gather = kinds[a] == "gather"
            own = pltpu.make_async_copy(ins[a] if gather else ins[a].at[me], outs[a].at[me], local_sems.at[a])
            own.start()
            locals_.append(own)
            for d in range(1, N_DEV):
                px = 1 - x if d & 4 else x
                py = 1 - y if d & 2 else y
                pc = 1 - c if d & 1 else c
                src = ins[a] if gather else ins[a].at[4 * px + 2 * py + pc]
                cp = pltpu.make_async_remote_copy(
                    src_ref=src, dst_ref=outs[a].at[me],
                    send_sem=send_sems.at[a * npeer + d - 1], recv_sem=recv_sems.at[a * npeer + d - 1],
                    device_id=(px, py, pc), device_id_type=pl.DeviceIdType.MESH)
                cp.start()
                copies.append(cp)
        for cp in copies:
            cp.wait_recv()
        for cp in copies:
            cp.wait_send()
        for own in locals_:
            own.wait()

    out_shape = [jax.ShapeDtypeStruct((N_DEV,) + (a.shape if k == "gather" else a.shape[1:]), a.dtype)
                 for a, k in zip(arrays, kinds)]
    return _pcall(
        body, name=name,
        in_specs=[pl.BlockSpec(memory_space=pl.ANY)] * n,
        out_specs=[pl.BlockSpec(memory_space=pl.ANY)] * n,
        out_shape=out_shape,
        scratch_shapes=[pltpu.SemaphoreType.DMA((n * npeer,)), pltpu.SemaphoreType.DMA((n * npeer,)),
                        pltpu.SemaphoreType.DMA((n,))],
        compiler_params=pltpu.CompilerParams(has_side_effects=True),
    )(*arrays)


def _peer_copies(srcs, lands, kinds, send_sems, recv_sems):
    x, y, c = lax.axis_index("x"), lax.axis_index("y"), lax.axis_index("c")
    me = 4 * x + 2 * y + c
    copies = []
    for a in range(len(srcs)):
        for d in range(1, N_DEV):
            px = 1 - x if d & 4 else x
            py = 1 - y if d & 2 else y
            pc = 1 - c if d & 1 else c
            k = a * (N_DEV - 1) + d - 1
            copies.append(pltpu.make_async_remote_copy(
                src_ref=srcs[a] if kinds[a] == "gather" else srcs[a].at[4 * px + 2 * py + pc], dst_ref=lands[a].at[me],
                send_sem=send_sems.at[k], recv_sem=recv_sems.at[k],
                device_id=(px, py, pc), device_id_type=pl.DeviceIdType.MESH))
    return copies


def _exchange_start(arrays, kinds, name):
    n = len(arrays)
    nsem = n * (N_DEV - 1)
    hbm = pl.BlockSpec(memory_space=pltpu.HBM)
    sem = pl.BlockSpec(memory_space=pltpu.SEMAPHORE)
    land_shapes = [(N_DEV,) + (a.shape if k == "gather" else a.shape[1:]) for a, k in zip(arrays, kinds)]

    def body(*refs):
        srcs, lands = refs[:n], refs[n:2 * n]
        send_sems, recv_sems = refs[2 * n], refs[2 * n + 1]
        token = refs[-1]
        for cp in _peer_copies(srcs, lands, kinds, send_sems, recv_sems):
            cp.start()
        token[...] = jnp.zeros_like(token)

    operands = [pltpu.with_memory_space_constraint(a, pltpu.HBM) for a in arrays]
    operands += [pltpu.with_memory_space_constraint(lax.empty(s, a.dtype), pltpu.HBM) for s, a in zip(land_shapes, arrays)]
    out = _pcall(
        body, name=name,
        in_specs=[hbm] * (2 * n),
        out_specs=[sem, sem] + [hbm] * (2 * n) + [pl.BlockSpec(memory_space=pltpu.VMEM)],
        out_shape=[pltpu.SemaphoreType.DMA((nsem,)), pltpu.SemaphoreType.DMA((nsem,))]
        + [pltpu.HBM(a.shape, a.dtype) for a in arrays]
        + [pltpu.HBM(s, a.dtype) for s, a in zip(land_shapes, arrays)]
        + [jax.ShapeDtypeStruct((8, LANE), F32)],
        input_output_aliases={k: 2 + k for k in range(2 * n)},
        compiler_params=pltpu.CompilerParams(has_side_effects=pltpu.SideEffectType.DATAFLOW_SIDE_EFFECTING),
    )(*operands)
    return out[0], out[1], list(out[2:2 + n]), list(out[2 + n:2 + 2 * n]), out[-1]


def _exchange_wait(started, kinds, after, name):
    send_sems, recv_sems, srcs, lands, _ = started
    n = len(srcs)
    hbm = pl.BlockSpec(memory_space=pltpu.HBM)
    sem = pl.BlockSpec(memory_space=pltpu.SEMAPHORE)

    def body(*refs):
        src_refs, land_refs = refs[:n], refs[n:2 * n]
        copies = _peer_copies(src_refs, land_refs, kinds, refs[2 * n], refs[2 * n + 1])
        for cp in copies:
            cp.wait_send()
        for cp in copies:
            cp.wait_recv()

    out = _pcall(
        body, name=name,
        in_specs=[hbm] * (2 * n) + [sem, sem, pl.BlockSpec(memory_space=pl.ANY)],
        out_specs=[hbm] * (2 * n),
        out_shape=[pltpu.HBM(a.shape, a.dtype) for a in srcs + lands],
        input_output_aliases={k: k for k in range(2 * n)},
        compiler_params=pltpu.CompilerParams(has_side_effects=pltpu.SideEffectType.DATAFLOW_SIDE_EFFECTING),
    )(*srcs, *lands, send_sems, recv_sems, after)
    return list(out[n:])


def _with_own_slot(land, own, me):
    return lax.dynamic_update_slice(land, own[None], (me,) + (0,) * own.ndim)


def _sum_slots(slots, name, rows_tile):
    nd, r, c = slots.shape

    def body(s_ref, o_ref):
        acc = s_ref[0].astype(F32)
        for p in range(1, nd):
            acc = acc + s_ref[p].astype(F32)
        o_ref[...] = acc

    return _pcall(
        body, name=name, grid=(r // rows_tile,),
        in_specs=[pl.BlockSpec((nd, rows_tile, c), lambda i: (0, i, 0))],
        out_specs=pl.BlockSpec((rows_tile, c), lambda i: (i, 0)),
        out_shape=jax.ShapeDtypeStruct((r, c), F32),
        compiler_params=_params("parallel"),
    )(slots)


def _adamw(w, g, m, v, name, rows_tile):
    r, c = w.shape

    def body(w_ref, g_ref, m_ref, v_ref, d_ref, nm_ref, nv_ref):
        gr = g_ref[...]
        nm = ADAM_B1 * m_ref[...] + (1.0 - ADAM_B1) * gr
        nv = ADAM_B2 * v_ref[...] + (1.0 - ADAM_B2) * (gr * gr)
        m_hat = nm / (1.0 - ADAM_B1 ** ADAM_STEP)
        v_hat = nv / (1.0 - ADAM_B2 ** ADAM_STEP)
        d_ref[...] = -ADAM_LR * (m_hat / (jnp.sqrt(v_hat) + ADAM_EPS) + ADAM_WD * w_ref[...])
        nm_ref[...] = nm
        nv_ref[...] = nv

    spec = pl.BlockSpec((rows_tile, c), lambda i: (i, 0))
    shp = jax.ShapeDtypeStruct((r, c), F32)
    return _pcall(
        body, name=name, grid=(r // rows_tile,), in_specs=[spec] * 4, out_specs=[spec] * 3, out_shape=[shp] * 3,
        compiler_params=_params("parallel"),
    )(w, g, m, v)


F0 = 2 * RET_QK + 2 * RET_V


def _to_internal_rows(w_t):
    cols = w_t.shape[1]
    fox = w_t[F0:F0 + 3 * FOX_W].reshape(3, FOX_PAIRS, LANE, cols).transpose(1, 0, 2, 3).reshape(3 * FOX_W, cols)
    tail = jnp.zeros((IN_PAD - IN_WIDTH, cols), w_t.dtype)
    return jnp.concatenate([w_t[:F0], fox, w_t[F0 + 3 * FOX_W:], tail], axis=0)


def _from_internal_rows(g_t):
    cols = g_t.shape[1]
    fox = g_t[F0:F0 + 3 * FOX_W].reshape(FOX_PAIRS, 3, LANE, cols).transpose(1, 0, 2, 3).reshape(3 * FOX_W, cols)
    return jnp.concatenate([g_t[:F0], fox, g_t[F0 + 3 * FOX_W:F0 + 3 * FOX_W + FOX_HEADS]], axis=0)


def _local_step(x, target, meta, attn_g, fox_b, ret_g, ffn_g, conv_w8, conv_b, final_g,
                first_weight, late_weights, ffn_grads_ready, out_grad_ready, in_grad_ready):
    seq, d = x.shape
    t = seq + PREFIX
    tm = TOK_TILE
    nq = t // tm
    fox_b128 = jnp.pad(fox_b, ((0, 0), (0, LANE - FOX_HEADS)))

    h0, n1 = _prep_norm(x, meta, attn_g, "prep_norm")
    w_in_t = first_weight(n1)
    proj = _mm_simple(n1, w_in_t, mode="nt", tm=tm, tn=IN_PAD, tk=d, out_dtype=F32, name="mm_in")
    cos, sin = _rope_tables(t)
    o_pre, mixed, states = _ret_fwd(proj, cos, sin, ret_g, "ret_fwd")
    c = _forget_cumsum(proj, fox_b128, "forget_cumsum")
    qa, ka, va = _fox_prep(proj, c, "fox_prep")
    by_block = lambda a: a.reshape(FOX_HEADS, nq, tm, LANE)
    mixed, o_fox, lse = _fox_fwd(by_block(qa), by_block(ka), by_block(va), mixed, "fox_fwd")
    w_out, w_up_t, w_down = late_weights(o_fox)
    h1 = _mm_simple(mixed, w_out, mode="nn", tm=tm, tn=d, tk=d, out_dtype=F32, name="mm_out", add=h0)
    n2 = _rmsnorm(h1, ffn_g, "ffn_norm")
    nf = D_FF // 1408
    up = _matmul(
        n2, w_up_t, mode="nt", grid=(2 * nf, nq, 1),
        a_spec=pl.BlockSpec((tm, d), lambda j, i, k: (i, 0)),
        b_spec=pl.BlockSpec((None, 1408, d), lambda j, i, k: (j // nf, j % nf, 0)),
        o_spec=pl.BlockSpec((None, tm, 1408), lambda j, i, k: (j // nf, i, j % nf)),
        out_shape=jax.ShapeDtypeStruct((2, t, D_FF), F32), name="mm_up")
    g = _conv_gate_fwd(up, conv_w8, conv_b, "conv_gate_fwd")
    h2 = _mm_simple(g, w_down, mode="nn", tm=tm, tn=d, tk=D_FF, out_dtype=F32, name="mm_down", add=h1)

    loss_tile, dh2, g_final = _loss_bwd(h2, target, final_g, "loss_bwd")
    tkw = 1408 if t % 1408 == 0 else tm
    dg = _mm_simple(dh2, w_down, mode="nt", tm=tm, tn=D_FF, tk=d, out_dtype=F32, name="mm_dg")
    gw_down = _mm_simple(g, dh2, mode="tn", tm=1408, tn=d, tk=tkw, out_dtype=BF16, name="mm_gw_down")
    dup, g_conv_w8, g_conv_b = _conv_gate_bwd(up, conv_w8, conv_b, dg, "conv_gate_bwd")
    half = lambda p: pl.BlockSpec((None, tm, D_FF), lambda i, j, k: (p, i, 0))
    half_w = lambda p: pl.BlockSpec((None, D_FF, d), lambda i, j, k: (p, 0, 0), pipeline_mode=pl.Buffered(1))
    dn2 = _matmul(
        [dup, dup], [w_up_t, w_up_t], mode="nn", grid=(nq, 1, 1),
        a_spec=[half(0), half(1)], b_spec=[half_w(0), half_w(1)],
        o_spec=pl.BlockSpec((tm, d), lambda i, j, k: (i, 0)),
        out_shape=jax.ShapeDtypeStruct((t, d), F32), name="mm_dn2")
    gw_up_t = _matmul(
        dup, n2, mode="tn", grid=(2 * nf, 1, t // tkw),
        a_spec=pl.BlockSpec((None, tkw, 1408), lambda i, j, k: (i // nf, k, i % nf)),
        b_spec=pl.BlockSpec((tkw, d), lambda i, j, k: (k, 0)),
        o_spec=pl.BlockSpec((1408, d), lambda i, j, k: (i, 0)),
        out_shape=jax.ShapeDtypeStruct((2 * D_FF, d), BF16), name="mm_gw_up")
    dh1, g_ffn = _rmsnorm_bwd(dn2, h1, ffn_g + ffn_grads_ready(gw_down, gw_up_t), dh2, "ffn_norm_bwd")

    dmixed = _mm_simple(dh1, w_out, mode="nt", tm=tm, tn=d, tk=d, out_dtype=F32, name="mm_dmixed")
    gw_out = _mm_simple(mixed, dh1, mode="tn", tm=d, tn=d, tk=tkw, out_dtype=BF16, name="mm_gw_out")
    dproj, g_ret = _ret_bwd(proj, cos, sin, ret_g + out_grad_ready(gw_out), dmixed, o_pre, states, "ret_bwd")
    qab, doa = _fox_prep_bwd(dmixed, o_fox, lse, qa, "fox_prep_bwd")
    dproj, drs, dcs = _fox_bwd(by_block(qab), by_block(doa), by_block(ka), by_block(va), dproj, "fox_bwd")
    dproj, g_fox_b = _forget_cumsum_bwd(proj, fox_b128, drs, dcs, dproj, "forget_cumsum_bwd")
    gw_in_t = _mm_simple(dproj, n1, mode="tn", tm=640, tn=d, tk=tkw, out_dtype=BF16, name="mm_gw_in")
    sent = in_grad_ready(gw_in_t)
    dn1 = _mm_simple(dproj, w_in_t, mode="nn", tm=tm, tn=d, tk=IN_PAD, out_dtype=F32, name="mm_dn1")
    dh0, g_attn = _rmsnorm_bwd(dn1, h0, attn_g + sent, dh1, "attn_norm_bwd")

    grads = dict(meta=dh0[N_PAD:PREFIX], attn_g=g_attn, fox_b=g_fox_b[:, :FOX_HEADS], ret_g=g_ret,
                 ffn_g=g_ffn, conv_w=g_conv_w8[:3], conv_b=g_conv_b, final_g=g_final)
    return loss_tile, dh0[PREFIX:], grads


def _pack(pieces, width=LANE):
    rows = []
    for p in pieces:
        flat = p.reshape(-1)
        pad = (-flat.shape[0]) % width
        rows.append(jnp.pad(flat, (0, pad)).reshape(-1, width))
    out = jnp.concatenate(rows, axis=0)
    return jnp.pad(out, ((0, (-out.shape[0]) % 8), (0, 0)))


def _unpack(packed, shapes, width=LANE):
    outs, r = [], 0
    for shp in shapes:
        size = int(np.prod(shp))
        nrows = -(-size // width)
        outs.append(packed[r:r + nrows].reshape(-1)[:size].reshape(shp))
        r += nrows
    return outs


def kernel(x, meta_tokens, attn_norm_g, w_in, fox_forget_b, ret_norm_g, w_out, ffn_norm_g, w_up, conv_w, conv_b, w_down, final_norm_g, loss_target, m_meta_tokens, m_attn_norm_g, m_w_in, m_fox_forget_b, m_ret_norm_g, m_w_out, m_ffn_norm_g, m_w_up, m_conv_w, m_conv_b, m_w_down, m_final_norm_g, v_meta_tokens, v_attn_norm_g, v_w_in, v_fox_forget_b, v_ret_norm_g, v_w_out, v_ffn_norm_g, v_w_up, v_conv_w, v_conv_b, v_w_down, v_final_norm_g):
    d = D_MODEL
    me = 4 * lax.axis_index("x") + 2 * lax.axis_index("y") + lax.axis_index("c")
    in_blk = IN_WIDTH // N_DEV
    in_blk_pad = 400
    up_blk = 2 * D_FF // N_DEV
    down_blk = D_FF // N_DEV
    cw_blk = D_FF // N_DEV

    w_in_loc = jnp.pad(w_in[0].T.astype(BF16), ((0, in_blk_pad - in_blk), (0, 0)))
    cw_loc = jnp.pad(conv_w[0], ((0, 5), (0, 384 - cw_blk)))
    g_meta, g_cw = _exchange([meta_tokens, cw_loc], ["gather"] * 2, "gather_small")
    first = _exchange_start([w_in_loc], ["gather"], "gather_in_start")
    rest_loc = [(w_out[0] + first[-1][0:1, 0:1]).astype(BF16), w_up[0].T.astype(BF16), w_down[0].astype(BF16)]
    rest = _exchange_start(rest_loc, ["gather"] * 3, "gather_rest_start")
    meta_f = g_meta.transpose(1, 0, 2).reshape(N_META, d)
    conv_w8 = jnp.pad(g_cw[:, :3, :cw_blk].transpose(1, 0, 2).reshape(3, D_FF), ((0, 5), (0, 0)))
    pending = {}

    def first_weight(after):
        (land,) = _exchange_wait(first, ["gather"], after, "gather_in_wait")
        return _to_internal_rows(_with_own_slot(land, w_in_loc, me)[:, :in_blk].reshape(IN_WIDTH, d))

    def in_grad_ready(gw_in_t):
        blocks = _from_internal_rows(gw_in_t).reshape(N_DEV, in_blk, d)
        pending["in_own"] = [jnp.pad(blocks, ((0, 0), (0, in_blk_pad - in_blk), (0, 0)))]
        pending["in"] = _exchange_start(pending["in_own"], ["scatter"], "grads_in_start")
        return pending["in"][-1][0:1, 0:1]

    def late_weights(after):
        g_out, g_up, g_down = [_with_own_slot(land, own, me) for land, own in
                               zip(_exchange_wait(rest, ["gather"] * 3, after, "gather_rest_wait"), rest_loc)]
        return g_out.reshape(d, d), g_up.reshape(2, D_FF, d), g_down.reshape(D_FF, d)

    def ffn_grads_ready(gw_down, gw_up_t):
        pending["ffn_own"] = [gw_down.reshape(N_DEV, down_blk, d), gw_up_t.reshape(N_DEV, up_blk, d)]
        pending["ffn"] = _exchange_start(pending["ffn_own"], ["scatter"] * 2, "grads_ffn_start")
        return pending["ffn"][-1][0:1, 0:1]

    def out_grad_ready(gw_out):
        pending["out_own"] = [gw_out.reshape(N_DEV, d // N_DEV, d)]
        pending["out"] = _exchange_start(pending["out_own"], ["scatter"], "grads_out_start")
        return pending["out"][-1][0:1, 0:1]

    loss_tile, grad_x, gr = _local_step(
        x[0], loss_target[0], meta_f, attn_norm_g + rest[-1][0:1, 0:1], fox_forget_b, ret_norm_g, ffn_norm_g,
        conv_w8, conv_b, final_norm_g.reshape(1, d), first_weight, late_weights, ffn_grads_ready, out_grad_ready,
        in_grad_ready)

    small_shapes = [(1, LANE), (1, d), (1, FOX_HEADS), (1, RET_V), (1, d), (1, D_FF), (1, d), (N_META, d), (3, D_FF)]
    small = _pack([loss_tile[0:1], gr["attn_g"], gr["fox_b"], gr["ret_g"], gr["ffn_g"], gr["conv_b"], gr["final_g"],
                   gr["meta"], gr["conv_w"]])
    (r_small,) = _exchange([small], ["gather"], "exchange_small")
    own_block = lambda a: lax.dynamic_index_in_dim(a, me, axis=0, keepdims=False)
    r_down, r_up = [_with_own_slot(land, own_block(own), me) for land, own in
                    zip(_exchange_wait(pending["ffn"], ["scatter"] * 2, r_small, "grads_ffn_wait"), pending["ffn_own"])]
    (r_out,) = [_with_own_slot(land, own_block(own), me) for land, own in
                zip(_exchange_wait(pending["out"], ["scatter"], r_small, "grads_out_wait"), pending["out_own"])]
    g_w_out = _sum_slots(r_out, "sum_w_out", d // N_DEV)
    g_w_up = _sum_slots(r_up, "sum_w_up", up_blk).T
    g_w_down = _sum_slots(r_down, "sum_w_down", down_blk)
    s_all = _sum_slots(r_small, "sum_small", r_small.shape[1])
    (loss_row, g_attn, g_fox_b, g_ret, g_ffn, g_conv_b, g_final, g_meta_full, g_cw_full) = _unpack(s_all, small_shapes)
    loss = loss_row[0, 0]
    g_meta_loc = lax.dynamic_slice(g_meta_full, (0, me * (d // N_DEV)), (N_META, d // N_DEV))
    g_cw_loc = lax.dynamic_slice(g_cw_full, (0, me * cw_blk), (3, cw_blk))

    d_w_out, m_w_out_n, v_w_out_n = _adamw(w_out[0], g_w_out, m_w_out[0], v_w_out[0], "adamw_w_out", 128)
    d_w_up, m_w_up_n, v_w_up_n = _adamw(w_up[0], g_w_up, m_w_up[0], v_w_up[0], "adamw_w_up", 128)
    d_w_down, m_w_down_n, v_w_down_n = _adamw(w_down[0], g_w_down, m_w_down[0], v_w_down[0], "adamw_w_down", down_blk)
    (r_in,) = [_with_own_slot(land, own_block(own), me) for land, own in
               zip(_exchange_wait(pending["in"], ["scatter"], d_w_up, "grads_in_wait"), pending["in_own"])]
    g_w_in = _sum_slots(r_in, "sum_w_in", in_blk_pad)[:in_blk].T
    d_w_in, m_w_in_n, v_w_in_n = _adamw(w_in[0], g_w_in, m_w_in[0], v_w_in[0], "adamw_w_in", 128)
    sm_grads = [g_meta_loc, g_attn, g_fox_b, g_ret, g_ffn, g_cw_loc, g_conv_b, g_final.reshape(d)]
    sm_w = [meta_tokens, attn_norm_g, fox_forget_b, ret_norm_g, ffn_norm_g, conv_w[0], conv_b, final_norm_g]
    sm_m = [m_meta_tokens, m_attn_norm_g, m_fox_forget_b, m_ret_norm_g, m_ffn_norm_g, m_conv_w[0], m_conv_b, m_final_norm_g]
    sm_v = [v_meta_tokens, v_attn_norm_g, v_fox_forget_b, v_ret_norm_g, v_ffn_norm_g, v_conv_w[0], v_conv_b, v_final_norm_g]
    sm_shapes = [a.shape for a in sm_w]
    pk = [_pack(lst) for lst in (sm_w, sm_grads, sm_m, sm_v)]
    sm_d, sm_nm, sm_nv = _adamw(pk[0], pk[1], pk[2], pk[3], "adamw_small", pk[0].shape[0])
    dl = _unpack(sm_d, sm_shapes)
    ml = _unpack(sm_nm, sm_shapes)
    vl = _unpack(sm_nv, sm_shapes)

    def by_weight(meta_, attn_, w_in_, fox_, ret_, w_out_, ffn_, w_up_, cw_, cb_, w_down_, final_):
        return (meta_, attn_, w_in_[None], fox_, ret_, w_out_[None], ffn_, w_up_[None], cw_[None], cb_, w_down_[None], final_)

    grads_out = by_weight(g_meta_loc, g_attn, g_w_in, g_fox_b, g_ret, g_w_out, g_ffn, g_w_up, g_cw_loc, g_conv_b,
                          g_w_down, g_final.reshape(d))
    delta_out = by_weight(dl[0], dl[1], d_w_in, dl[2], dl[3], d_w_out, dl[4], d_w_up, dl[5], dl[6], d_w_down, dl[7])
    m_out = by_weight(ml[0], ml[1], m_w_in_n, ml[2], ml[3], m_w_out_n, ml[4], m_w_up_n, ml[5], ml[6], m_w_down_n, ml[7])
    v_out = by_weight(vl[0], vl[1], v_w_in_n, vl[2], vl[3], v_w_out_n, vl[4], v_w_up_n, vl[5], vl[6], v_w_down_n, vl[7])
    return (loss, grad_x[None]) + grads_out + delta_out + m_out + v_out
```

```python
import numpy as np
import jax
import jax.numpy as jnp
from jax import lax
from jax.experimental import pallas as pl
from jax.experimental.pallas import tpu as pltpu

F32 = jnp.float32
BF16 = jnp.bfloat16

D_MODEL = 1024
N_META = 16
N_PAD = 112
PREFIX = 128
RET_HEADS = 4
RET_DK = 64
RET_DV = 128
FOX_HEADS = 8
FOX_DH = 64
D_FF = 2816
ROPE_BASE = 10000.0
EPS = 1e-6
NEG = -1e30
RET_QK = RET_HEADS * RET_DK
RET_V = RET_HEADS * RET_DV
FOX_W = FOX_HEADS * FOX_DH
IN_WIDTH = 2 * RET_QK + 2 * RET_V + 3 * FOX_W + FOX_HEADS
IN_PAD = 3200
FF_COL_BLOCK = (IN_WIDTH - FOX_HEADS) // 128
QK_SCALE = 0.125

ADAM_LR = 0.001
ADAM_B1 = 0.9
ADAM_B2 = 0.999
ADAM_EPS = 1e-08
ADAM_WD = 0.01
ADAM_STEP = 10

N_DEV = 8
LANE = 128
ROW_TILE = 128
TOK_TILE = 384

NN = (((1,), (0,)), ((), ()))
NT = (((1,), (1,)), ((), ()))
TN = (((0,), (0,)), ((), ()))


def _pcall(body, **kw):
    return pl.pallas_call(body, **kw)


def _params(*sem):
    return pltpu.CompilerParams(dimension_semantics=sem)


def _dot(a, b, dims=NN):
    return lax.dot_general(a, b, dims, preferred_element_type=F32)


def _sigmoid(x):
    return 0.5 * jnp.tanh(0.5 * x) + 0.5


def _matmul(a, b, *, mode, grid, a_spec, b_spec, o_spec, out_shape, name, add=None, add_spec=None, after=None):
    dims = {"nn": NN, "nt": NT, "tn": TN}[mode]
    nk = grid[2]
    has_add = add is not None
    a_list, b_list = (list(a), list(b)) if isinstance(a, (list, tuple)) else ([a], [b])
    a_specs, b_specs = (list(a_spec), list(b_spec)) if isinstance(a_spec, (list, tuple)) else ([a_spec], [b_spec])
    nt = len(a_list)
    n_in = 2 * nt + int(has_add) + int(after is not None)

    def body(*refs):
        a_refs, b_refs = refs[:nt], refs[nt:2 * nt]
        add_ref = refs[2 * nt] if has_add else None
        o_ref = refs[n_in]
        part = _dot(a_refs[0][...].astype(BF16), b_refs[0][...].astype(BF16), dims)
        for ar, br in zip(a_refs[1:], b_refs[1:]):
            part = part + _dot(ar[...].astype(BF16), br[...].astype(BF16), dims)

        def finish(acc):
            if has_add:
                acc = acc + add_ref[...]
            o_ref[...] = acc.astype(o_ref.dtype)

        if nk == 1:
            finish(part)
        else:
            acc_ref = refs[-1]
            k = pl.program_id(2)

            @pl.when(k == 0)
            def _():
                acc_ref[...] = part

            @pl.when(k > 0)
            def _():
                acc_ref[...] += part

            @pl.when(k == nk - 1)
            def _():
                finish(acc_ref[...])

    in_specs = a_specs + b_specs + ([add_spec] if has_add else [])
    args = tuple(a_list) + tuple(b_list) + ((add,) if has_add else ())
    if after is not None:
        in_specs, args = in_specs + [pl.BlockSpec(memory_space=pl.ANY)], args + (after,)
    scratch = [] if nk == 1 else [pltpu.VMEM(tuple(d for d in o_spec.block_shape if d is not None), F32)]
    return _pcall(
        body, name=name, grid=grid, in_specs=in_specs, out_specs=o_spec, out_shape=out_shape,
        scratch_shapes=scratch, compiler_params=_params("parallel", "parallel", "arbitrary"),
    )(*args)


def _mm_simple(a, b, *, mode, tm, tn, tk, out_dtype, name, add=None, after=None):
    if mode == "tn":
        K, M = a.shape
    else:
        M, K = a.shape
    N = b.shape[0] if mode == "nt" else b.shape[1]
    grid = (M // tm, N // tn, K // tk)
    resident = dict(pipeline_mode=pl.Buffered(1)) if (tn == N and tk == K) else {}
    a_spec = pl.BlockSpec((tk, tm), lambda i, j, k: (k, i)) if mode == "tn" else pl.BlockSpec((tm, tk), lambda i, j, k: (i, k))
    b_spec = (pl.BlockSpec((tn, tk), lambda i, j, k: (j, k), **resident) if mode == "nt"
              else pl.BlockSpec((tk, tn), lambda i, j, k: (k, j), **resident))
    o_spec = pl.BlockSpec((tm, tn), lambda i, j, k: (i, j))
    return _matmul(a, b, mode=mode, grid=grid, a_spec=a_spec, b_spec=b_spec, o_spec=o_spec,
                   out_shape=jax.ShapeDtypeStruct((M, N), out_dtype), name=name, add=add,
                   add_spec=o_spec if add is not None else None, after=after)


def _prep_norm(x, meta, gain, name):
    seq, d = x.shape
    t = seq + PREFIX

    def body(xa_ref, xb_ref, xc_ref, meta_ref, g_ref, h_ref, n_ref):
        i = pl.program_id(0)

        @pl.when(i == 0)
        def _():
            h_ref[0:N_PAD, :] = jnp.zeros((N_PAD, d), F32)
            h_ref[N_PAD:ROW_TILE, :] = meta_ref[...]

        @pl.when(i > 0)
        def _():
            h_ref[0:ROW_TILE, :] = xa_ref[...]

        h_ref[ROW_TILE:2 * ROW_TILE, :] = xb_ref[...]
        h_ref[2 * ROW_TILE:3 * ROW_TILE, :] = xc_ref[...]
        h = h_ref[...]
        r = lax.rsqrt(jnp.mean(h * h, axis=-1, keepdims=True) + EPS)
        n_ref[...] = (h * r * g_ref[...]).astype(BF16)

    return _pcall(
        body, name=name, grid=(t // TOK_TILE,),
        in_specs=_shifted_row_specs(d) + [pl.BlockSpec((N_META, d), lambda i: (0, 0)), pl.BlockSpec((1, d), lambda i: (0, 0))],
        out_specs=[pl.BlockSpec((TOK_TILE, d), lambda i: (i, 0)), pl.BlockSpec((TOK_TILE, d), lambda i: (i, 0))],
        out_shape=[jax.ShapeDtypeStruct((t, d), F32), jax.ShapeDtypeStruct((t, d), BF16)],
        compiler_params=_params("parallel"),
    )(x, x, x, meta, gain)


def _shifted_row_specs(d):
    blocks_per_tile = TOK_TILE // ROW_TILE
    return [pl.BlockSpec((ROW_TILE, d), lambda i, r=r: (jnp.maximum(blocks_per_tile * i + r, 0), 0)) for r in (-1, 0, 1)]


def _rmsnorm(h, gain, name):
    t, d = h.shape

    def body(h_ref, g_ref, n_ref):
        x = h_ref[...]
        r = lax.rsqrt(jnp.mean(x * x, axis=-1, keepdims=True) + EPS)
        n_ref[...] = (x * r * g_ref[...]).astype(BF16)

    return _pcall(
        body, name=name, grid=(t // TOK_TILE,),
        in_specs=[pl.BlockSpec((TOK_TILE, d), lambda i: (i, 0)), pl.BlockSpec((1, d), lambda i: (0, 0))],
        out_specs=pl.BlockSpec((TOK_TILE, d), lambda i: (i, 0)),
        out_shape=jax.ShapeDtypeStruct((t, d), BF16),
        compiler_params=_params("parallel"),
    )(h, gain)


def _rmsnorm_bwd(dn, h, gain, dres, name):
    t, d = h.shape

    def body(dn_ref, h_ref, g_ref, dres_ref, dh_ref, gg_ref):
        i = pl.program_id(0)
        x = h_ref[...]
        r = lax.rsqrt(jnp.mean(x * x, axis=-1, keepdims=True) + EPS)
        xhat = x * r
        dy = dn_ref[...]
        u = dy * g_ref[...]
        dh_ref[...] = dres_ref[...] + r * (u - xhat * jnp.mean(u * xhat, axis=-1, keepdims=True))
        part = jnp.sum(dy * xhat, axis=0, keepdims=True)

        @pl.when(i == 0)
        def _():
            gg_ref[...] = part

        @pl.when(i > 0)
        def _():
            gg_ref[...] += part

    return _pcall(
        body, name=name, grid=(t // TOK_TILE,),
        in_specs=[pl.BlockSpec((TOK_TILE, d), lambda i: (i, 0)), pl.BlockSpec((TOK_TILE, d), lambda i: (i, 0)),
                  pl.BlockSpec((1, d), lambda i: (0, 0)), pl.BlockSpec((TOK_TILE, d), lambda i: (i, 0))],
        out_specs=[pl.BlockSpec((TOK_TILE, d), lambda i: (i, 0)), pl.BlockSpec((1, d), lambda i: (0, 0))],
        out_shape=[jax.ShapeDtypeStruct((t, d), F32), jax.ShapeDtypeStruct((1, d), F32)],
        compiler_params=_params("arbitrary"),
    )(dn, h, gain, dres)


def _loss_bwd(h2, target, gain, name):
    t, d = h2.shape

    def body(h_ref, ta_ref, tb_ref, tc_ref, g_ref, loss_ref, dh_ref, gg_ref):
        i = pl.program_id(0)

        @pl.when(i == 0)
        def _():
            loss_ref[...] = jnp.zeros_like(loss_ref)
            gg_ref[...] = jnp.zeros_like(gg_ref)

        x = h_ref[...]
        r = lax.rsqrt(jnp.mean(x * x, axis=-1, keepdims=True) + EPS)
        xhat = x * r
        g = g_ref[...]
        tgt = jnp.concatenate([ta_ref[...], tb_ref[...], tc_ref[...]], axis=0)
        counted = (i * TOK_TILE + lax.broadcasted_iota(jnp.int32, (TOK_TILE, 1), 0)) >= PREFIX
        err = jnp.where(counted, xhat * g - tgt, 0.0)
        loss_ref[...] += 0.5 * jnp.sum(jnp.mean(err * err, axis=-1, keepdims=True))
        dy = err * (1.0 / d)
        u = dy * g
        dh_ref[...] = r * (u - xhat * jnp.mean(u * xhat, axis=-1, keepdims=True))
        gg_ref[...] += jnp.sum(dy * xhat, axis=0, keepdims=True)

    return _pcall(
        body, name=name, grid=(t // TOK_TILE,),
        in_specs=[pl.BlockSpec((TOK_TILE, d), lambda i: (i, 0))] + _shifted_row_specs(d) + [pl.BlockSpec((1, d), lambda i: (0, 0))],
        out_specs=[pl.BlockSpec((8, LANE), lambda i: (0, 0)), pl.BlockSpec((TOK_TILE, d), lambda i: (i, 0)),
                   pl.BlockSpec((1, d), lambda i: (0, 0))],
        out_shape=[jax.ShapeDtypeStruct((8, LANE), F32), jax.ShapeDtypeStruct((t, d), F32),
                   jax.ShapeDtypeStruct((1, d), F32)],
        compiler_params=_params("arbitrary"),
    )(h2, target, target, target, gain)


def _ret_consts(bk):
    gam = 1.0 - 2.0 ** (-5.0 - np.arange(RET_HEADS))
    n = np.arange(bk)
    same_or_earlier_chunk = (n[None, :] // 64) <= (n[:, None] // 64)
    w = gam[:, None, None] ** np.abs(n[:, None] - n[None, :])[None] * same_or_earlier_chunk[None]
    wq = gam[:, None] ** (n[None, :] + 1.0)
    wk = gam[:, None] ** (bk - 1.0 - n[None, :])
    mask = (np.arange(RET_QK)[None, :] // RET_DK) == np.arange(RET_HEADS)[:, None]
    return (jnp.asarray(w, F32), jnp.asarray(wq[:, :, None], F32), jnp.asarray(wk[:, :, None], F32),
            jnp.asarray(mask[:, None, :], F32), [float(g ** bk) for g in gam])


def _rope_tables(t):
    half = RET_DK // 2
    inv = 1.0 / (ROPE_BASE ** (jnp.arange(half, dtype=F32) / half))
    ang = jnp.arange(t).astype(F32)[:, None] * inv[None, :]
    cos, sin = jnp.cos(ang), jnp.sin(ang)
    return (jnp.tile(jnp.concatenate([cos, cos], axis=1), (1, RET_HEADS)),
            jnp.tile(jnp.concatenate([-sin, sin], axis=1), (1, RET_HEADS)))


def _swap_halves(x):
    outs = []
    for s in range(x.shape[1] // LANE):
        xs = x[:, LANE * s:LANE * (s + 1)]
        lane = lax.broadcasted_iota(jnp.int32, xs.shape, 1)
        outs.append(jnp.where((lane & 32) == 0, pltpu.roll(xs, LANE - 32, axis=1), pltpu.roll(xs, 32, axis=1)))
    return outs[0] if len(outs) == 1 else jnp.concatenate(outs, axis=1)


def _rope(x, cos, sin_signed):
    return x * cos + _swap_halves(x) * sin_signed


def _rope_t(dx, cos, sin_signed):
    return dx * cos + _swap_halves(dx * sin_signed)


def _ret_fwd(proj, cos, sin, gain, name):
    t = proj.shape[0]
    bk = TOK_TILE
    nb = t // bk
    w, wq, wk, mask, g_blk = _ret_consts(bk)

    def body(q_ref, k_ref, v_ref, rg_ref, cos_ref, sin_ref, w_ref, wq_ref, wk_ref, mask_ref, gain_ref,
             opre_ref, og_ref, st_ref, r_ref):
        i = pl.program_id(0)

        @pl.when(i == 0)
        def _():
            r_ref[...] = jnp.zeros_like(r_ref)

        c, s = cos_ref[...], sin_ref[...]
        valid = ((i * bk + lax.broadcasted_iota(jnp.int32, (bk, 1), 0)) >= N_PAD).astype(F32)
        qr = _rope(q_ref[...], c, s)
        kr = _rope(k_ref[...], c, s) * QK_SCALE * valid
        kb = kr.astype(BF16)
        for h in range(RET_HEADS):
            hm = mask_ref[h]
            cols = slice(RET_DV * h, RET_DV * (h + 1))
            vh = v_ref[:, cols].astype(BF16)
            r_prev = r_ref[h]
            st_ref[0, h] = r_prev
            sm = _dot((qr * hm).astype(BF16), kb, NT) * w_ref[h]
            o = _dot(sm.astype(BF16), vh) + _dot((qr * (hm * wq_ref[h])).astype(BF16), r_prev.astype(BF16))
            r_ref[h] = g_blk[h] * r_prev + _dot((kr * wk_ref[h]).astype(BF16), vh, TN)
            opre_ref[:, cols] = o
            rstd = lax.rsqrt(jnp.mean(o * o, axis=-1, keepdims=True) + EPS)
            rg = rg_ref[:, cols]
            og_ref[:, cols] = (o * rstd * gain_ref[:, cols] * (rg * _sigmoid(rg))).astype(BF16)

    full = lambda shape: pl.BlockSpec(shape, lambda i: (0,) * len(shape))
    return _pcall(
        body, name=name, grid=(nb,),
        in_specs=[pl.BlockSpec((bk, RET_QK), lambda i: (i, 0)), pl.BlockSpec((bk, RET_QK), lambda i: (i, 1)),
                  pl.BlockSpec((bk, RET_V), lambda i: (i, 1)), pl.BlockSpec((bk, RET_V), lambda i: (i, 2)),
                  pl.BlockSpec((bk, RET_QK), lambda i: (i, 0)), pl.BlockSpec((bk, RET_QK), lambda i: (i, 0)),
                  full((RET_HEADS, bk, bk)), full((RET_HEADS, bk, 1)), full((RET_HEADS, bk, 1)),
                  full((RET_HEADS, 1, RET_QK)), full((1, RET_V))],
        out_specs=[pl.BlockSpec((bk, RET_V), lambda i: (i, 0)), pl.BlockSpec((bk, RET_V), lambda i: (i, 0)),
                   pl.BlockSpec((1, RET_HEADS, RET_QK, RET_DV), lambda i: (i, 0, 0, 0))],
        out_shape=[jax.ShapeDtypeStruct((t, RET_V), F32), jax.ShapeDtypeStruct((t, RET_V + FOX_W), BF16),
                   jax.ShapeDtypeStruct((nb, RET_HEADS, RET_QK, RET_DV), F32)],
        scratch_shapes=[pltpu.VMEM((RET_HEADS, RET_QK, RET_DV), F32)],
        compiler_params=_params("arbitrary"),
    )(proj, proj, proj, proj, cos, sin, w, wq, wk, mask, gain)


def _ret_bwd(proj, cos, sin, gain, dmixed, opre, states, name):
    t = proj.shape[0]
    bk = TOK_TILE
    nb = t // bk
    w, wq, wk, mask, g_blk = _ret_consts(bk)
    v0, g0 = 2 * RET_QK, 2 * RET_QK + RET_V

    def body(q_ref, k_ref, v_ref, rg_ref, cos_ref, sin_ref, w_ref, wq_ref, wk_ref, mask_ref, gain_ref,
             dog_ref, opre_ref, st_ref, dp_ref, gg_ref, dr_ref):
        step = pl.program_id(0)
        i = nb - 1 - step

        @pl.when(step == 0)
        def _():
            dr_ref[...] = jnp.zeros_like(dr_ref)
            gg_ref[...] = jnp.zeros_like(gg_ref)

        c, s = cos_ref[...], sin_ref[...]
        valid = ((i * bk + lax.broadcasted_iota(jnp.int32, (bk, 1), 0)) >= N_PAD).astype(F32)
        qr = _rope(q_ref[...], c, s)
        kr = _rope(k_ref[...], c, s) * QK_SCALE * valid
        kb = kr.astype(BF16)
        dqr = jnp.zeros((bk, RET_QK), F32)
        dkr = jnp.zeros((bk, RET_QK), F32)
        for h in range(RET_HEADS):
            hm = mask_ref[h]
            cols = slice(RET_DV * h, RET_DV * (h + 1))
            vh = v_ref[:, cols].astype(BF16)
            o = opre_ref[:, cols]
            rstd = lax.rsqrt(jnp.mean(o * o, axis=-1, keepdims=True) + EPS)
            xhat = o * rstd
            rg = rg_ref[:, cols]
            sg = _sigmoid(rg)
            gate = rg * sg
            gn = gain_ref[:, cols]
            dog = dog_ref[:, cols]
            dp_ref[:, g0 + RET_DV * h:g0 + RET_DV * (h + 1)] = (
                dog * xhat * gn * (sg * (1.0 + rg * (1.0 - sg)))).astype(BF16)
            gg_ref[:, cols] += jnp.sum(dog * xhat * gate, axis=0, keepdims=True)
            dxh = dog * gn * gate
            do = (rstd * (dxh - xhat * jnp.mean(dxh * xhat, axis=-1, keepdims=True))).astype(BF16)
            qm = (qr * hm).astype(BF16)
            qw = (qr * (hm * wq_ref[h])).astype(BF16)
            kw = (kr * wk_ref[h]).astype(BF16)
            wh = w_ref[h]
            sm = (_dot(qm, kb, NT) * wh).astype(BF16)
            ds = (_dot(do, vh, NT) * wh).astype(BF16)
            dr = dr_ref[h]
            drb = dr.astype(BF16)
            dp_ref[:, v0 + RET_DV * h:v0 + RET_DV * (h + 1)] = (_dot(sm, do, TN) + _dot(kw, drb)).astype(BF16)
            dqr = dqr + _dot(ds, kb) * hm + _dot(do, st_ref[0, h].astype(BF16), NT) * (hm * wq_ref[h])
            dkr = dkr + _dot(ds, qm, TN) + _dot(vh, drb, NT) * wk_ref[h]
            dr_ref[h] = g_blk[h] * dr + _dot(qw, do, TN)
        dp_ref[:, 0:RET_QK] = _rope_t(dqr, c, s).astype(BF16)
        dp_ref[:, RET_QK:2 * RET_QK] = _rope_t(dkr * (QK_SCALE * valid), c, s).astype(BF16)

    full = lambda shape: pl.BlockSpec(shape, lambda i: (0,) * len(shape))
    rev = lambda col: (lambda i: (nb - 1 - i, col))
    return _pcall(
        body, name=name, grid=(nb,),
        in_specs=[pl.BlockSpec((bk, RET_QK), rev(0)), pl.BlockSpec((bk, RET_QK), rev(1)),
                  pl.BlockSpec((bk, RET_V), rev(1)), pl.BlockSpec((bk, RET_V), rev(2)),
                  pl.BlockSpec((bk, RET_QK), rev(0)), pl.BlockSpec((bk, RET_QK), rev(0)),
                  full((RET_HEADS, bk, bk)), full((RET_HEADS, bk, 1)), full((RET_HEADS, bk, 1)),
                  full((RET_HEADS, 1, RET_QK)), full((1, RET_V)),
                  pl.BlockSpec((bk, RET_V), rev(0)), pl.BlockSpec((bk, RET_V), rev(0)),
                  pl.BlockSpec((1, RET_HEADS, RET_QK, RET_DV), lambda i: (nb - 1 - i, 0, 0, 0))],
        out_specs=[pl.BlockSpec((bk, g0 + RET_V), rev(0)), pl.BlockSpec((1, RET_V), lambda i: (0, 0))],
        out_shape=[jax.ShapeDtypeStruct((t, IN_PAD), BF16), jax.ShapeDtypeStruct((1, RET_V), F32)],
        scratch_shapes=[pltpu.VMEM((RET_HEADS, RET_QK, RET_DV), F32)],
        compiler_params=_params("arbitrary"),
    )(proj, proj, proj, proj, cos, sin, w, wq, wk, mask, gain, dmixed, opre, states)


def _forget_cumsum(proj, bias, name):
    t = proj.shape[0]
    rt = TOK_TILE
    nb = t // rt
    tril = jnp.asarray(np.tril(np.ones((rt, rt))), F32)

    def body(z_ref, b_ref, tril_ref, c_ref, carry_ref):
        i = pl.program_id(0)

        @pl.when(i == 0)
        def _():
            carry_ref[...] = jnp.zeros_like(carry_ref)

        z = z_ref[...] + b_ref[...]
        logf = jnp.minimum(z, 0.0) - jnp.log(1.0 + jnp.exp(-jnp.abs(z)))
        c = lax.dot_general(tril_ref[...], logf, NN, precision=lax.Precision.HIGHEST,
                            preferred_element_type=F32) + carry_ref[...]
        c_ref[...] = c
        carry_ref[...] = c[rt - 1:rt, :]

    return _pcall(
        body, name=name, grid=(nb,),
        in_specs=[pl.BlockSpec((rt, LANE), lambda i: (i, FF_COL_BLOCK)), pl.BlockSpec((1, LANE), lambda i: (0, 0)),
                  pl.BlockSpec((rt, rt), lambda i: (0, 0))],
        out_specs=pl.BlockSpec((rt, LANE), lambda i: (i, 0)),
        out_shape=jax.ShapeDtypeStruct((t, LANE), F32),
        scratch_shapes=[pltpu.VMEM((1, LANE), F32)],
        compiler_params=_params("arbitrary"),
    )(proj, bias, tril)


def _forget_cumsum_bwd(proj, bias, drs, dcs, dproj, name):
    t = proj.shape[0]
    rt = TOK_TILE
    nb = t // rt
    triu = jnp.asarray(np.triu(np.ones((rt, rt))), F32)

    def body(z_ref, b_ref, triu_ref, drs_ref, dcs_ref, dproj_in, dz_ref, gb_ref, carry_ref):
        step = pl.program_id(0)

        @pl.when(step == 0)
        def _():
            carry_ref[...] = jnp.zeros_like(carry_ref)
            gb_ref[...] = jnp.zeros_like(gb_ref)

        dlogf = lax.dot_general(triu_ref[...], drs_ref[...] - dcs_ref[...], NN, precision=lax.Precision.HIGHEST,
                                preferred_element_type=F32) + carry_ref[...]
        carry_ref[...] = dlogf[0:1, :]
        z = z_ref[...] + b_ref[...]
        is_head = lax.broadcasted_iota(jnp.int32, (rt, LANE), 1) < FOX_HEADS
        dz = jnp.where(is_head, dlogf / (1.0 + jnp.exp(z)), 0.0)
        dz_ref[...] = dz.astype(BF16)
        gb_ref[...] += jnp.sum(dz, axis=0, keepdims=True)

    return _pcall(
        body, name=name, grid=(nb,),
        in_specs=[pl.BlockSpec((rt, LANE), lambda i: (nb - 1 - i, FF_COL_BLOCK)),
                  pl.BlockSpec((1, LANE), lambda i: (0, 0)),
                  pl.BlockSpec((rt, rt), lambda i: (0, 0)),
                  pl.BlockSpec((rt, LANE), lambda i: (nb - 1 - i, 0)),
                  pl.BlockSpec((rt, LANE), lambda i: (nb - 1 - i, 0)),
                  pl.BlockSpec(memory_space=pl.ANY)],
        out_specs=[pl.BlockSpec((rt, LANE), lambda i: (nb - 1 - i, FF_COL_BLOCK)),
                   pl.BlockSpec((1, LANE), lambda i: (0, 0))],
        out_shape=[jax.ShapeDtypeStruct(dproj.shape, BF16), jax.ShapeDtypeStruct((1, LANE), F32)],
        input_output_aliases={5: 0},
        scratch_shapes=[pltpu.VMEM((1, LANE), F32)],
        compiler_params=_params("arbitrary"),
    )(proj, bias, triu, drs, dcs, dproj)


FOX_PAIRS = FOX_HEADS // 2
L_ONE_Q = FOX_DH
L_ONE_K = FOX_DH + 3
L_LSE = FOX_DH + 4


def _split3(x):
    hi = x.astype(BF16).astype(F32)
    r = x - hi
    mid = r.astype(BF16).astype(F32)
    return hi, mid, r - mid


def _head_to_low(slab, e):
    return slab if e == 0 else pltpu.roll(slab, FOX_DH, axis=1)


def _pair(a, b, low):
    return jnp.where(low, a, pltpu.roll(b, FOX_DH, axis=1))


def _fox_prep(proj, c, name):
    t = proj.shape[0]
    tq = TOK_TILE

    def body(p_ref, c_ref, qa_ref, ka_ref, va_ref):
        i = pl.program_id(0)
        lane = lax.broadcasted_iota(jnp.int32, (tq, LANE), 1)
        low = lane < FOX_DH
        live = (i * tq + lax.broadcasted_iota(jnp.int32, (tq, 1), 0)) >= N_PAD
        q_tail = jnp.where(lane < L_ONE_Q + 3, 1.0, 0.0)
        k_ones = (lane >= L_ONE_K) & (lane < L_ONE_K + 4)
        v_tail = jnp.where(lane < FOX_DH + 2, 1.0, 0.0)
        for pair in range(FOX_PAIRS):
            base = 3 * LANE * pair
            for e in range(2):
                h = 2 * pair + e
                q = _head_to_low(p_ref[:, base:base + LANE], e)
                k = _head_to_low(p_ref[:, base + LANE:base + 2 * LANE], e)
                v = _head_to_low(p_ref[:, base + 2 * LANE:base + 3 * LANE], e)
                hi, mid, lo = _split3(jnp.where(live, -c_ref[:, h:h + 1], NEG))
                ka = jnp.where(low, k, jnp.where(k_ones, 1.0, 0.0))
                ka = jnp.where(lane == L_ONE_Q, hi, jnp.where(lane == L_ONE_Q + 1, mid, jnp.where(lane == L_ONE_Q + 2, lo, ka)))
                qa_ref[h] = jnp.where(low, q * QK_SCALE, q_tail).astype(BF16)
                ka_ref[h] = ka.astype(BF16)
                va_ref[h] = jnp.where(low, v, v_tail).astype(BF16)

    out = jax.ShapeDtypeStruct((FOX_HEADS, t, LANE), BF16)
    ospec = pl.BlockSpec((FOX_HEADS, tq, LANE), lambda i: (0, i, 0))
    return _pcall(
        body, name=name, grid=(t // tq,),
        in_specs=[pl.BlockSpec((tq, 3 * FOX_W), lambda i: (i, 1)), pl.BlockSpec((tq, LANE), lambda i: (i, 0))],
        out_specs=[ospec, ospec, ospec], out_shape=[out, out, out],
        compiler_params=_params("parallel"),
    )(proj, c)


STEP_PAIRS = 2
STEP_HEADS = 2 * STEP_PAIRS
FOX_GROUPS = FOX_PAIRS // STEP_PAIRS


def _blockdiag(a, b):
    z = jnp.zeros_like(a)
    return jnp.concatenate([jnp.concatenate([a, z], axis=1), jnp.concatenate([z, b], axis=1)], axis=0)


def _fox_fwd(qa, ka, va, mixed, name):
    nh, nq, tq, _ = qa.shape
    t = nq * tq

    def body(qa_ref, ka_ref, va_ref, mixed_in, mixed_ref, o_ref, lse_ref):
        i = pl.program_id(1)
        lane = lax.broadcasted_iota(jnp.int32, (tq, LANE), 1)
        causal = lax.broadcasted_iota(jnp.int32, (tq, tq), 1) <= lax.broadcasted_iota(jnp.int32, (tq, tq), 0)
        qps = [jnp.concatenate([qa_ref[2 * c], qa_ref[2 * c + 1]], axis=1) for c in range(STEP_PAIRS)]

        def step(j, carry, diagonal):
            scores = [_dot(qps[c], _blockdiag(ka_ref[2 * c, j], ka_ref[2 * c + 1, j]), NT) for c in range(STEP_PAIRS)]
            new = []
            for c in range(STEP_PAIRS):
                ms, acc = carry[c]
                ps, ms_new, alphas = [], [], []
                for e in range(2):
                    s = scores[c][:, e * tq:(e + 1) * tq]
                    if diagonal:
                        s = jnp.where(causal, s, NEG)
                    m_new = jnp.maximum(ms[e], jnp.max(s, axis=-1, keepdims=True))
                    ps.append(jnp.exp(s - m_new).astype(BF16))
                    ms_new.append(m_new)
                    alphas.append(jnp.broadcast_to(jnp.exp(ms[e] - m_new), (tq, LANE)))
                pv = _dot(jnp.concatenate(ps, axis=1), _blockdiag(va_ref[2 * c, j], va_ref[2 * c + 1, j]))
                new.append((tuple(ms_new), jnp.concatenate(alphas, axis=1) * acc + pv))
            return tuple(new)

        m0 = jnp.full((tq, 1), NEG, F32)
        init = tuple(((m0, m0), jnp.zeros((tq, 2 * LANE), F32)) for _ in range(STEP_PAIRS))
        carry = lax.fori_loop(0, i, lambda j, cr: step(j, cr, False), init)
        o_pairs = []
        lse = jnp.zeros((tq, LANE), F32)
        for c, (ms, acc) in enumerate(step(i, carry, True)):
            outs = []
            for e in range(2):
                half = acc[:, e * LANE:(e + 1) * LANE]
                l = half[:, FOX_DH:FOX_DH + 1]
                outs.append(half / l)
                lse = jnp.where(lane == 2 * c + e, ms[e] + jnp.log(l), lse)
            o_pairs.append(_pair(outs[0], outs[1], lane < FOX_DH))
        o_all = jnp.concatenate(o_pairs, axis=1)
        mixed_ref[...] = o_all.astype(BF16)
        o_ref[...] = o_all
        lse_ref[...] = lse

    width = STEP_PAIRS * LANE
    whole = pl.BlockSpec((STEP_HEADS, nq, tq, LANE), lambda g, i: (g, 0, 0, 0), pipeline_mode=pl.Buffered(1))
    return _pcall(
        body, name=name, grid=(FOX_GROUPS, nq),
        in_specs=[pl.BlockSpec((STEP_HEADS, None, tq, LANE), lambda g, i: (g, i, 0, 0)), whole, whole,
                  pl.BlockSpec(memory_space=pl.ANY)],
        out_specs=[pl.BlockSpec((tq, width), lambda g, i: (i, RET_V // width + g)),
                   pl.BlockSpec((tq, width), lambda g, i: (i, g)),
                   pl.BlockSpec((None, tq, LANE), lambda g, i: (g, i, 0))],
        out_shape=[jax.ShapeDtypeStruct(mixed.shape, BF16), jax.ShapeDtypeStruct((t, FOX_W), F32),
                   jax.ShapeDtypeStruct((FOX_GROUPS, t, LANE), F32)],
        input_output_aliases={3: 0},
        compiler_params=_params("parallel", "parallel"),
    )(qa, ka, va, mixed)


def _fox_prep_bwd(dmixed, o_fox, lse, qa, name):
    t = dmixed.shape[0]
    tq = TOK_TILE

    def body(dm_ref, o_ref, lse_ref, qa_ref, qab_ref, doa_ref):
        i = pl.program_id(0)
        lane = lax.broadcasted_iota(jnp.int32, (tq, LANE), 1)
        low = lane < FOX_DH
        live = (i * tq + lax.broadcasted_iota(jnp.int32, (tq, 1), 0)) >= N_PAD
        for pair in range(FOX_PAIRS):
            cols = slice(LANE * pair, LANE * (pair + 1))
            d_slab = dm_ref[:, cols]
            prod = d_slab * o_ref[:, cols]
            for e in range(2):
                h = 2 * pair + e
                nd = -jnp.sum(jnp.where(low, _head_to_low(prod, e), 0.0), axis=-1, keepdims=True)
                nd_hi = nd.astype(BF16).astype(F32)
                doa = jnp.where(low, _head_to_low(d_slab, e), 0.0)
                doa = jnp.where(lane == FOX_DH, nd_hi, jnp.where(lane == FOX_DH + 1, nd - nd_hi, doa))
                doa_ref[h] = doa.astype(BF16)
                lse_h = lse_ref[h // STEP_HEADS][:, h % STEP_HEADS:h % STEP_HEADS + 1]
                hi, mid, lo = _split3(jnp.where(live, -lse_h, 0.0))
                qab = qa_ref[h].astype(F32)
                qab = jnp.where(lane == L_LSE, hi, jnp.where(lane == L_LSE + 1, mid, jnp.where(lane == L_LSE + 2, lo, qab)))
                qab_ref[h] = qab.astype(BF16)

    out = jax.ShapeDtypeStruct((FOX_HEADS, t, LANE), BF16)
    hspec = pl.BlockSpec((FOX_HEADS, tq, LANE), lambda i: (0, i, 0))
    return _pcall(
        body, name=name, grid=(t // tq,),
        in_specs=[pl.BlockSpec((tq, FOX_W), lambda i: (i, 1)), pl.BlockSpec((tq, FOX_W), lambda i: (i, 0)),
                  pl.BlockSpec((FOX_GROUPS, tq, LANE), lambda i: (0, i, 0)), hspec],
        out_specs=[hspec, hspec], out_shape=[out, out],
        compiler_params=_params("parallel"),
    )(dmixed, o_fox, lse, qa)


def _fox_bwd(qab, doa, ka, va, dproj, name):
    nh, nq, tq, _ = qab.shape
    t = nq * tq
    slab = 3 * LANE * STEP_PAIRS
    group0 = (2 * RET_QK + 2 * RET_V) // slab

    def body(qab_ref, doa_ref, ka_ref, va_ref, dproj_in, dp_ref, drs_ref, dcs_ref, dq_ref):
        g, j = pl.program_id(0), pl.program_id(1)

        @pl.when((g == 0) & (j == 0))
        def _():
            drs_ref[...] = jnp.zeros_like(drs_ref)
            dcs_ref[...] = jnp.zeros_like(dcs_ref)

        @pl.when(j == 0)
        def _():
            dq_ref[...] = jnp.zeros_like(dq_ref)

        lane = lax.broadcasted_iota(jnp.int32, (tq, LANE), 1)
        low = lane < FOX_DH
        key_le_query = lax.broadcasted_iota(jnp.int32, (tq, tq), 0) <= lax.broadcasted_iota(jnp.int32, (tq, tq), 1)

        def by_head(c, a, b, col):
            h = STEP_HEADS * g + 2 * c
            return jnp.where(lane == h, a[:, col:col + 1], jnp.where(lane == h + 1, b[:, col:col + 1], 0.0))

        kbs = [ka_ref[h] for h in range(STEP_HEADS)]
        vbs = [va_ref[h] for h in range(STEP_HEADS)]

        def step(i, carry, diagonal):
            qbs = [qab_ref[h, i] for h in range(STEP_HEADS)]
            dobs = [doa_ref[h, i] for h in range(STEP_HEADS)]
            st = [_dot(kbs[h], qbs[h], NT) for h in range(STEP_HEADS)]
            dpt = [_dot(vbs[h], dobs[h], NT) for h in range(STEP_HEADS)]
            new = []
            for h in range(STEP_HEADS):
                p = jnp.exp(st[h])
                if diagonal:
                    p = jnp.where(key_le_query, p, 0.0)
                ds = (p * dpt[h]).astype(BF16)
                dq_ref[h, i] += _dot(ds, kbs[h], TN)
                dk, dv = carry[h]
                new.append((dk + _dot(ds, qbs[h]), dv + _dot(p.astype(BF16), dobs[h])))
            return tuple(new)

        zero = jnp.zeros((tq, LANE), F32)
        carry = step(j, tuple((zero, zero) for _ in range(STEP_HEADS)), True)
        carry = lax.fori_loop(j + 1, nq, lambda i, cr: step(i, cr, False), carry)
        rows = pl.ds(pl.multiple_of(j * tq, tq), tq)
        for c in range(STEP_PAIRS):
            (dka, dva), (dkb, dvb) = carry[2 * c], carry[2 * c + 1]
            c0 = 3 * LANE * c
            dp_ref[rows, c0 + LANE:c0 + 2 * LANE] = _pair(dka, dkb, low).astype(BF16)
            dp_ref[rows, c0 + 2 * LANE:c0 + 3 * LANE] = _pair(dva, dvb, low).astype(BF16)
            dcs_ref[rows, :] += by_head(c, dka, dkb, L_ONE_Q)

        @pl.when(j == nq - 1)
        def _():
            for c in range(STEP_PAIRS):
                for blk in range(nq):
                    r = slice(blk * tq, (blk + 1) * tq)
                    a, b = dq_ref[2 * c, blk], dq_ref[2 * c + 1, blk]
                    dp_ref[r, 3 * LANE * c:3 * LANE * c + LANE] = (_pair(a, b, low) * QK_SCALE).astype(BF16)
                    drs_ref[r, :] += by_head(c, a, b, L_ONE_K)

    whole = pl.BlockSpec((STEP_HEADS, nq, tq, LANE), lambda g, j: (g, 0, 0, 0), pipeline_mode=pl.Buffered(1))
    blk = pl.BlockSpec((STEP_HEADS, None, tq, LANE), lambda g, j: (g, j, 0, 0))
    sums = pl.BlockSpec((t, LANE), lambda g, j: (0, 0), pipeline_mode=pl.Buffered(1))
    return _pcall(
        body, name=name, grid=(FOX_GROUPS, nq),
        in_specs=[whole, whole, blk, blk, pl.BlockSpec(memory_space=pl.ANY)],
        out_specs=[pl.BlockSpec((t, slab), lambda g, j: (0, group0 + g)), sums, sums],
        out_shape=[jax.ShapeDtypeStruct(dproj.shape, BF16), jax.ShapeDtypeStruct((t, LANE), F32),
                   jax.ShapeDtypeStruct((t, LANE), F32)],
        input_output_aliases={4: 0},
        scratch_shapes=[pltpu.VMEM((STEP_HEADS, nq, tq, LANE), F32)],
        compiler_params=_params("arbitrary", "arbitrary"),
    )(qab, doa, ka, va, dproj)


HALO = 8


def _rows_ext(ref, r0, rows, t, before, after):
    lo, hi = r0 - before, r0 + rows + after
    parts = []
    if lo < 0:
        parts.append(jnp.zeros((-lo, LANE), F32))
    parts.append(ref[max(lo, 0):min(hi, t), :].astype(F32))
    if hi > t:
        parts.append(jnp.zeros((hi - t, LANE), F32))
    return parts[0] if len(parts) == 1 else jnp.concatenate(parts, axis=0)


def _conv_taps(a_ext, r0_ext, cw_ref, cb_ref):
    n = a_ext.shape[0]
    if r0_ext < N_PAD:
        row = r0_ext + lax.broadcasted_iota(jnp.int32, (n, 1), 0)
        a_ext = jnp.where(row >= N_PAD, a_ext, 0.0)
    a1 = pltpu.roll(a_ext, 1, axis=0)
    a2 = pltpu.roll(a_ext, 2, axis=0)
    acc = cb_ref[...] + a2 * cw_ref[0:1, :] + a1 * cw_ref[1:2, :] + a_ext * cw_ref[2:3, :]
    return a_ext, a1, a2, acc


def _conv_gate_fwd(up, conv_w8, conv_b, name):
    _, t, f = up.shape
    rows = TOK_TILE

    def body(a_ref, b_ref, cw_ref, cb_ref, g_ref):
        for r0 in range(0, t, rows):
            a_ext = _rows_ext(a_ref, r0, rows, t, HALO, 0)
            _, _, _, acc = _conv_taps(a_ext, r0 - HALO, cw_ref, cb_ref)
            acc = acc[HALO:, :]
            g_ref[r0:r0 + rows, :] = (acc * _sigmoid(acc) * b_ref[r0:r0 + rows, :]).astype(BF16)

    return _pcall(
        body, name=name, grid=(f // LANE,),
        in_specs=[pl.BlockSpec((None, t, LANE), lambda j: (0, 0, j)), pl.BlockSpec((None, t, LANE), lambda j: (1, 0, j)),
                  pl.BlockSpec((8, LANE), lambda j: (0, j)), pl.BlockSpec((1, LANE), lambda j: (0, j))],
        out_specs=pl.BlockSpec((t, LANE), lambda j: (0, j)),
        out_shape=jax.ShapeDtypeStruct((t, f), BF16),
        compiler_params=_params("parallel"),
    )(up, up, conv_w8, conv_b)


def _conv_gate_bwd(up, conv_w8, conv_b, dg, name):
    _, t, f = up.shape
    rows = TOK_TILE

    def body(a_ref, b_ref, cw_ref, cb_ref, dg_ref, dup_ref, gcw_ref, gcb_ref):
        gw = [jnp.zeros((1, LANE), F32) for _ in range(3)]
        gb = jnp.zeros((1, LANE), F32)
        for r0 in range(0, t, rows):
            a_ext = _rows_ext(a_ref, r0, rows, t, HALO, HALO)
            b_ext = _rows_ext(b_ref, r0, rows, t, HALO, HALO)
            dg_ext = _rows_ext(dg_ref, r0, rows, t, HALO, HALO)
            a0, a1, a2, acc = _conv_taps(a_ext, r0 - HALO, cw_ref, cb_ref)
            sg = _sigmoid(acc)
            dacc = dg_ext * b_ext * (sg * (1.0 + acc * (1.0 - sg)))
            n = dacc.shape[0]
            da = (dacc * cw_ref[2:3, :] + pltpu.roll(dacc, n - 1, axis=0) * cw_ref[1:2, :]
                  + pltpu.roll(dacc, n - 2, axis=0) * cw_ref[0:1, :])
            core = slice(HALO, HALO + rows)
            da = da[core, :]
            if r0 < N_PAD:
                row = r0 + lax.broadcasted_iota(jnp.int32, (rows, 1), 0)
                da = jnp.where(row >= N_PAD, da, 0.0)
            dup_ref[0, r0:r0 + rows, :] = da.astype(BF16)
            dup_ref[1, r0:r0 + rows, :] = (dg_ext * acc * sg)[core, :].astype(BF16)
            dacc_c = dacc[core, :]
            gw[0] = gw[0] + jnp.sum(dacc_c * a2[core, :], axis=0, keepdims=True)
            gw[1] = gw[1] + jnp.sum(dacc_c * a1[core, :], axis=0, keepdims=True)
            gw[2] = gw[2] + jnp.sum(dacc_c * a0[core, :], axis=0, keepdims=True)
            gb = gb + jnp.sum(dacc_c, axis=0, keepdims=True)
        gcw_ref[...] = jnp.zeros((8, LANE), F32)
        for tap in range(3):
            gcw_ref[tap:tap + 1, :] = gw[tap]
        gcb_ref[...] = gb

    return _pcall(
        body, name=name, grid=(f // LANE,),
        in_specs=[pl.BlockSpec((None, t, LANE), lambda j: (0, 0, j)), pl.BlockSpec((None, t, LANE), lambda j: (1, 0, j)),
                  pl.BlockSpec((8, LANE), lambda j: (0, j)), pl.BlockSpec((1, LANE), lambda j: (0, j)),
                  pl.BlockSpec((t, LANE), lambda j: (0, j))],
        out_specs=[pl.BlockSpec((2, t, LANE), lambda j: (0, 0, j)), pl.BlockSpec((8, LANE), lambda j: (0, j)),
                   pl.BlockSpec((1, LANE), lambda j: (0, j))],
        out_shape=[jax.ShapeDtypeStruct((2, t, f), BF16), jax.ShapeDtypeStruct((8, f), F32),
                   jax.ShapeDtypeStruct((1, f), F32)],
        compiler_params=_params("parallel"),
    )(up, up, conv_w8, conv_b, dg)


def _exchange(arrays, kinds, name):
    n = len(arrays)
    npeer = N_DEV - 1

    def body(*refs):
        ins, outs = refs[:n], refs[n:2 * n]
        send_sems, recv_sems, local_sems = refs[2 * n:]
        x, y, c = lax.axis_index("x"), lax.axis_index("y"), lax.axis_index("c")
        me = 4 * x + 2 * y + c
        copies, locals_ = [], []
        for a in range(n):
            gather = kinds[a] == "gather"
            own = pltpu.make_async_copy(ins[a] if gather else ins[a].at[me], outs[a].at[me], local_sems.at[a])
            own.start()
            locals_.append(own)
            for d in range(1, N_DEV):
                px = 1 - x if d & 4 else x
                py = 1 - y if d & 2 else y
                pc = 1 - c if d & 1 else c
                src = ins[a] if gather else ins[a].at[4 * px + 2 * py + pc]
                cp = pltpu.make_async_remote_copy(
                    src_ref=src, dst_ref=outs[a].at[me],
                    send_sem=send_sems.at[a * npeer + d - 1], recv_sem=recv_sems.at[a * npeer + d - 1],
                    device_id=(px, py, pc), device_id_type=pl.DeviceIdType.MESH)
                cp.start()
                copies.append(cp)
        for cp in copies:
            cp.wait_recv()
        for cp in copies:
            cp.wait_send()
        for own in locals_:
            own.wait()

    out_shape = [jax.ShapeDtypeStruct((N_DEV,) + (a.shape if k == "gather" else a.shape[1:]), a.dtype)
                 for a, k in zip(arrays, kinds)]
    return _pcall(
        body, name=name,
        in_specs=[pl.BlockSpec(memory_space=pl.ANY)] * n,
        out_specs=[pl.BlockSpec(memory_space=pl.ANY)] * n,
        out_shape=out_shape,
        scratch_shapes=[pltpu.SemaphoreType.DMA((n * npeer,)), pltpu.SemaphoreType.DMA((n * npeer,)),
                        pltpu.SemaphoreType.DMA((n,))],
        compiler_params=pltpu.CompilerParams(has_side_effects=True),
    )(*arrays)


def _peer_copies(srcs, lands, kinds, send_sems, recv_sems):
    x, y, c = lax.axis_index("x"), lax.axis_index("y"), lax.axis_index("c")
    me = 4 * x + 2 * y + c
    copies = []
    for a in range(len(srcs)):
        for d in range(1, N_DEV):
            px = 1 - x if d & 4 else x
            py = 1 - y if d & 2 else y
            pc = 1 - c if d & 1 else c
            k = a * (N_DEV - 1) + d - 1
            copies.append(pltpu.make_async_remote_copy(
                src_ref=srcs[a] if kinds[a] == "gather" else srcs[a].at[4 * px + 2 * py + pc], dst_ref=lands[a].at[me],
                send_sem=send_sems.at[k], recv_sem=recv_sems.at[k],
                device_id=(px, py, pc), device_id_type=pl.DeviceIdType.MESH))
    return copies


def _exchange_start(arrays, kinds, name):
    n = len(arrays)
    nsem = n * (N_DEV - 1)
    hbm = pl.BlockSpec(memory_space=pltpu.HBM)
    sem = pl.BlockSpec(memory_space=pltpu.SEMAPHORE)
    land_shapes = [(N_DEV,) + (a.shape if k == "gather" else a.shape[1:]) for a, k in zip(arrays, kinds)]

    def body(*refs):
        srcs, lands = refs[:n], refs[n:2 * n]
        send_sems, recv_sems = refs[2 * n], refs[2 * n + 1]
        token = refs[-1]
        for cp in _peer_copies(srcs, lands, kinds, send_sems, recv_sems):
            cp.start()
        token[...] = jnp.zeros_like(token)

    operands = [pltpu.with_memory_space_constraint(a, pltpu.HBM) for a in arrays]
    operands += [pltpu.with_memory_space_constraint(lax.empty(s, a.dtype), pltpu.HBM) for s, a in zip(land_shapes, arrays)]
    out = _pcall(
        body, name=name,
        in_specs=[hbm] * (2 * n),
        out_specs=[sem, sem] + [hbm] * (2 * n) + [pl.BlockSpec(memory_space=pltpu.VMEM)],
        out_shape=[pltpu.SemaphoreType.DMA((nsem,)), pltpu.SemaphoreType.DMA((nsem,))]
        + [pltpu.HBM(a.shape, a.dtype) for a in arrays]
        + [pltpu.HBM(s, a.dtype) for s, a in zip(land_shapes, arrays)]
        + [jax.ShapeDtypeStruct((8, LANE), F32)],
        input_output_aliases={k: 2 + k for k in range(2 * n)},
        compiler_params=pltpu.CompilerParams(has_side_effects=pltpu.SideEffectType.DATAFLOW_SIDE_EFFECTING),
    )(*operands)
    return out[0], out[1], list(out[2:2 + n]), list(out[2 + n:2 + 2 * n]), out[-1]


def _exchange_wait(started, kinds, after, name):
    send_sems, recv_sems, srcs, lands, _ = started
    n = len(srcs)
    hbm = pl.BlockSpec(memory_space=pltpu.HBM)
    sem = pl.BlockSpec(memory_space=pltpu.SEMAPHORE)

    def body(*refs):
        src_refs, land_refs = refs[:n], refs[n:2 * n]
        copies = _peer_copies(src_refs, land_refs, kinds, refs[2 * n], refs[2 * n + 1])
        for cp in copies:
            cp.wait_send()
        for cp in copies:
            cp.wait_recv()

    out = _pcall(
        body, name=name,
        in_specs=[hbm] * (2 * n) + [sem, sem, pl.BlockSpec(memory_space=pl.ANY)],
        out_specs=[hbm] * (2 * n),
        out_shape=[pltpu.HBM(a.shape, a.dtype) for a in srcs + lands],
        input_output_aliases={k: k for k in range(2 * n)},
        compiler_params=pltpu.CompilerParams(has_side_effects=pltpu.SideEffectType.DATAFLOW_SIDE_EFFECTING),
    )(*srcs, *lands, send_sems, recv_sems, after)
    me = 4 * lax.axis_index("x") + 2 * lax.axis_index("y") + lax.axis_index("c")
    filled = []
    for src, land, kind in zip(out[:n], out[n:], kinds):
        own = src if kind == "gather" else lax.dynamic_index_in_dim(src, me, axis=0, keepdims=False)
        filled.append(lax.dynamic_update_slice(land, own[None], (me,) + (0,) * own.ndim))
    return filled


def _sum_slots(slots, name, rows_tile):
    nd, r, c = slots.shape

    def body(s_ref, o_ref):
        acc = s_ref[0].astype(F32)
        for p in range(1, nd):
            acc = acc + s_ref[p].astype(F32)
        o_ref[...] = acc

    return _pcall(
        body, name=name, grid=(r // rows_tile,),
        in_specs=[pl.BlockSpec((nd, rows_tile, c), lambda i: (0, i, 0))],
        out_specs=pl.BlockSpec((rows_tile, c), lambda i: (i, 0)),
        out_shape=jax.ShapeDtypeStruct((r, c), F32),
        compiler_params=_params("parallel"),
    )(slots)


def _sum_slots_small(slot_arrays, name):
    n = len(slot_arrays)

    def body(*refs):
        for s_ref, o_ref in zip(refs[:n], refs[n:]):
            acc = s_ref[0]
            for p in range(1, s_ref.shape[0]):
                acc = acc + s_ref[p]
            o_ref[...] = acc

    return _pcall(body, name=name, out_shape=[jax.ShapeDtypeStruct(a.shape[1:], F32) for a in slot_arrays])(*slot_arrays)


def _adamw_update(w_ref, g_ref, m_ref, v_ref, d_ref, nm_ref, nv_ref):
    gr = g_ref[...]
    nm = ADAM_B1 * m_ref[...] + (1.0 - ADAM_B1) * gr
    nv = ADAM_B2 * v_ref[...] + (1.0 - ADAM_B2) * (gr * gr)
    m_hat = nm / (1.0 - ADAM_B1 ** ADAM_STEP)
    v_hat = nv / (1.0 - ADAM_B2 ** ADAM_STEP)
    d_ref[...] = -ADAM_LR * (m_hat / (jnp.sqrt(v_hat) + ADAM_EPS) + ADAM_WD * w_ref[...])
    nm_ref[...] = nm
    nv_ref[...] = nv


def _adamw_small(ws, gs, ms, vs, name):
    n = len(ws)

    def body(*refs):
        ins, outs = refs[:4 * n], refs[4 * n:]
        for k in range(n):
            _adamw_update(ins[k], ins[n + k], ins[2 * n + k], ins[3 * n + k], outs[k], outs[n + k], outs[2 * n + k])

    shapes = [jax.ShapeDtypeStruct(w.shape, F32) for w in ws]
    out = _pcall(body, name=name, out_shape=shapes * 3)(*ws, *gs, *ms, *vs)
    return list(out[:n]), list(out[n:2 * n]), list(out[2 * n:])


def _adamw(w, g, m, v, name, rows_tile):
    r, c = w.shape
    body = lambda *refs: _adamw_update(*refs)
    spec = pl.BlockSpec((rows_tile, c), lambda i: (i, 0))
    shp = jax.ShapeDtypeStruct((r, c), F32)
    return _pcall(
        body, name=name, grid=(r // rows_tile,), in_specs=[spec] * 4, out_specs=[spec] * 3, out_shape=[shp] * 3,
        compiler_params=_params("parallel"),
    )(w, g, m, v)


F0 = 2 * RET_QK + 2 * RET_V


def _to_internal_rows(w_t):
    cols = w_t.shape[1]
    fox = w_t[F0:F0 + 3 * FOX_W].reshape(3, FOX_PAIRS, LANE, cols).transpose(1, 0, 2, 3).reshape(3 * FOX_W, cols)
    tail = jnp.zeros((IN_PAD - IN_WIDTH, cols), w_t.dtype)
    return jnp.concatenate([w_t[:F0], fox, w_t[F0 + 3 * FOX_W:], tail], axis=0)


def _from_internal_rows(g_t):
    cols = g_t.shape[1]
    fox = g_t[F0:F0 + 3 * FOX_W].reshape(FOX_PAIRS, 3, LANE, cols).transpose(1, 0, 2, 3).reshape(3 * FOX_W, cols)
    return jnp.concatenate([g_t[:F0], fox, g_t[F0 + 3 * FOX_W:F0 + 3 * FOX_W + FOX_HEADS]], axis=0)


def _local_step(x, target, meta, attn_g, fox_b, ret_g, ffn_g, conv_w8, conv_b, final_g,
                first_weight, late_weights, ffn_grads_ready, out_grad_ready, in_grad_ready):
    seq, d = x.shape
    t = seq + PREFIX
    tm = TOK_TILE
    nq = t // tm
    fox_b128 = jnp.pad(fox_b, ((0, 0), (0, LANE - FOX_HEADS)))

    h0, n1 = _prep_norm(x, meta, attn_g, "prep_norm")
    w_in_t = first_weight(n1)
    proj = _mm_simple(n1, w_in_t, mode="nt", tm=tm, tn=IN_PAD, tk=d, out_dtype=F32, name="mm_in")
    cos, sin = _rope_tables(t)
    o_pre, mixed, states = _ret_fwd(proj, cos, sin, ret_g, "ret_fwd")
    c = _forget_cumsum(proj, fox_b128, "forget_cumsum")
    qa, ka, va = _fox_prep(proj, c, "fox_prep")
    by_block = lambda a: a.reshape(FOX_HEADS, nq, tm, LANE)
    mixed, o_fox, lse = _fox_fwd(by_block(qa), by_block(ka), by_block(va), mixed, "fox_fwd")
    w_out, w_up_t, w_down = late_weights(o_fox)
    h1 = _mm_simple(mixed, w_out, mode="nn", tm=tm, tn=d, tk=d, out_dtype=F32, name="mm_out", add=h0)
    n2 = _rmsnorm(h1, ffn_g, "ffn_norm")
    nf = D_FF // 1408
    up = _matmul(
        n2, w_up_t, mode="nt", grid=(2 * nf, nq, 1),
        a_spec=pl.BlockSpec((tm, d), lambda j, i, k: (i, 0)),
        b_spec=pl.BlockSpec((None, 1408, d), lambda j, i, k: (j // nf, j % nf, 0)),
        o_spec=pl.BlockSpec((None, tm, 1408), lambda j, i, k: (j // nf, i, j % nf)),
        out_shape=jax.ShapeDtypeStruct((2, t, D_FF), F32), name="mm_up")
    g = _conv_gate_fwd(up, conv_w8, conv_b, "conv_gate_fwd")
    h2 = _mm_simple(g, w_down, mode="nn", tm=tm, tn=d, tk=D_FF, out_dtype=F32, name="mm_down", add=h1)

    loss_tile, dh2, g_final = _loss_bwd(h2, target, final_g, "loss_bwd")
    tkw = 1408 if t % 1408 == 0 else tm
    dg = _mm_simple(dh2, w_down, mode="nt", tm=tm, tn=D_FF, tk=d, out_dtype=F32, name="mm_dg")
    gw_down = _mm_simple(g, dh2, mode="tn", tm=1408, tn=d, tk=tkw, out_dtype=BF16, name="mm_gw_down")
    dup, g_conv_w8, g_conv_b = _conv_gate_bwd(up, conv_w8, conv_b, dg, "conv_gate_bwd")
    half = lambda p: pl.BlockSpec((None, tm, D_FF), lambda i, j, k: (p, i, 0))
    half_w = lambda p: pl.BlockSpec((None, D_FF, d), lambda i, j, k: (p, 0, 0), pipeline_mode=pl.Buffered(1))
    dn2 = _matmul(
        [dup, dup], [w_up_t, w_up_t], mode="nn", grid=(nq, 1, 1),
        a_spec=[half(0), half(1)], b_spec=[half_w(0), half_w(1)],
        o_spec=pl.BlockSpec((tm, d), lambda i, j, k: (i, 0)),
        out_shape=jax.ShapeDtypeStruct((t, d), F32), name="mm_dn2")
    gw_up_t = _matmul(
        dup, n2, mode="tn", grid=(2 * nf, 1, t // tkw),
        a_spec=pl.BlockSpec((None, tkw, 1408), lambda i, j, k: (i // nf, k, i % nf)),
        b_spec=pl.BlockSpec((tkw, d), lambda i, j, k: (k, 0)),
        o_spec=pl.BlockSpec((1408, d), lambda i, j, k: (i, 0)),
        out_shape=jax.ShapeDtypeStruct((2 * D_FF, d), BF16), name="mm_gw_up")
    dh1, g_ffn = _rmsnorm_bwd(dn2, h1, ffn_g + ffn_grads_ready(gw_down, gw_up_t), dh2, "ffn_norm_bwd")

    dmixed = _mm_simple(dh1, w_out, mode="nt", tm=tm, tn=d, tk=d, out_dtype=F32, name="mm_dmixed")
    gw_out = _mm_simple(mixed, dh1, mode="tn", tm=d, tn=d, tk=tkw, out_dtype=BF16, name="mm_gw_out")
    dproj, g_ret = _ret_bwd(proj, cos, sin, ret_g + out_grad_ready(gw_out), dmixed, o_pre, states, "ret_bwd")
    qab, doa = _fox_prep_bwd(dmixed, o_fox, lse, qa, "fox_prep_bwd")
    dproj, drs, dcs = _fox_bwd(by_block(qab), by_block(doa), by_block(ka), by_block(va), dproj, "fox_bwd")
    dproj, g_fox_b = _forget_cumsum_bwd(proj, fox_b128, drs, dcs, dproj, "forget_cumsum_bwd")
    gw_in_t = _mm_simple(dproj, n1, mode="tn", tm=640, tn=d, tk=tkw, out_dtype=BF16, name="mm_gw_in")
    sent = in_grad_ready(gw_in_t)
    dn1 = _mm_simple(dproj, w_in_t, mode="nn", tm=tm, tn=d, tk=IN_PAD, out_dtype=F32, name="mm_dn1", after=sent)
    dh0, g_attn = _rmsnorm_bwd(dn1, h0, attn_g, dh1, "attn_norm_bwd")

    grads = dict(meta=dh0[N_PAD:PREFIX], attn_g=g_attn, fox_b=g_fox_b, ret_g=g_ret,
                 ffn_g=g_ffn, conv_w=g_conv_w8, conv_b=g_conv_b, final_g=g_final)
    return loss_tile, dh0[PREFIX:], grads


def kernel(x, meta_tokens, attn_norm_g, w_in, fox_forget_b, ret_norm_g, w_out, ffn_norm_g, w_up, conv_w, conv_b, w_down, final_norm_g, loss_target, m_meta_tokens, m_attn_norm_g, m_w_in, m_fox_forget_b, m_ret_norm_g, m_w_out, m_ffn_norm_g, m_w_up, m_conv_w, m_conv_b, m_w_down, m_final_norm_g, v_meta_tokens, v_attn_norm_g, v_w_in, v_fox_forget_b, v_ret_norm_g, v_w_out, v_ffn_norm_g, v_w_up, v_conv_w, v_conv_b, v_w_down, v_final_norm_g):
    d = D_MODEL
    me = 4 * lax.axis_index("x") + 2 * lax.axis_index("y") + lax.axis_index("c")
    in_blk = IN_WIDTH // N_DEV
    in_blk_pad = 400
    up_blk = 2 * D_FF // N_DEV
    down_blk = D_FF // N_DEV
    cw_blk = D_FF // N_DEV

    w_in_loc = jnp.pad(w_in[0].T.astype(BF16), ((0, in_blk_pad - in_blk), (0, 0)))
    cw_loc = jnp.pad(conv_w[0], ((0, 5), (0, 384 - cw_blk)))
    g_meta, g_cw = _exchange([meta_tokens, cw_loc], ["gather"] * 2, "gather_small")
    first = _exchange_start([w_in_loc], ["gather"], "gather_in_start")
    rest_loc = [(w_out[0] + first[-1][0:1, 0:1]).astype(BF16), w_up[0].T.astype(BF16), w_down[0].astype(BF16)]
    rest = _exchange_start(rest_loc, ["gather"] * 3, "gather_rest_start")
    meta_f = g_meta.transpose(1, 0, 2).reshape(N_META, d)
    conv_w8 = jnp.pad(g_cw[:, :3, :cw_blk].transpose(1, 0, 2).reshape(3, D_FF), ((0, 5), (0, 0)))
    pending = {}

    def first_weight(after):
        (g_in,) = _exchange_wait(first, ["gather"], after, "gather_in_wait")
        return _to_internal_rows(g_in[:, :in_blk].reshape(IN_WIDTH, d))

    def in_grad_ready(gw_in_t):
        blocks = _from_internal_rows(gw_in_t).reshape(N_DEV, in_blk, d)
        blocks = jnp.pad(blocks, ((0, 0), (0, in_blk_pad - in_blk), (0, 0)))
        pending["in"] = _exchange_start([blocks], ["scatter"], "grads_in_start")
        return pending["in"][-1][0:1, 0:1]

    def late_weights(after):
        g_out, g_up, g_down = _exchange_wait(rest, ["gather"] * 3, after, "gather_rest_wait")
        return g_out.reshape(d, d), g_up.reshape(2, D_FF, d), g_down.reshape(D_FF, d)

    def ffn_grads_ready(gw_down, gw_up_t):
        blocks = [gw_down.reshape(N_DEV, down_blk, d), gw_up_t.reshape(N_DEV, up_blk, d)]
        pending["ffn"] = _exchange_start(blocks, ["scatter"] * 2, "grads_ffn_start")
        return pending["ffn"][-1][0:1, 0:1]

    def out_grad_ready(gw_out):
        pending["out"] = _exchange_start([gw_out.reshape(N_DEV, d // N_DEV, d)], ["scatter"], "grads_out_start")
        return pending["out"][-1][0:1, 0:1]

    loss_tile, grad_x, gr = _local_step(
        x[0], loss_target[0], meta_f, attn_norm_g + rest[-1][0:1, 0:1], fox_forget_b, ret_norm_g, ffn_norm_g,
        conv_w8, conv_b, final_norm_g.reshape(1, d), first_weight, late_weights, ffn_grads_ready, out_grad_ready,
        in_grad_ready)

    small = [loss_tile, gr["attn_g"], gr["fox_b"], gr["ret_g"], gr["ffn_g"], gr["conv_b"], gr["final_g"],
             gr["meta"], gr["conv_w"]]
    r_small = _exchange(small, ["gather"] * len(small), "exchange_small")
    r_down, r_up = _exchange_wait(pending["ffn"], ["scatter"] * 2, r_small[0], "grads_ffn_wait")
    (r_out,) = _exchange_wait(pending["out"], ["scatter"], r_small[0], "grads_out_wait")
    g_w_out = _sum_slots(r_out, "sum_w_out", d // N_DEV)
    g_w_up = _sum_slots(r_up, "sum_w_up", up_blk).T
    g_w_down = _sum_slots(r_down, "sum_w_down", down_blk)
    (loss_all, g_attn, g_fox_b128, g_ret, g_ffn, g_conv_b, g_final, g_meta_full, g_cw_full) = _sum_slots_small(
        r_small, "sum_small")
    loss = loss_all[0, 0]
    g_fox_b = g_fox_b128[:, :FOX_HEADS]
    g_meta_loc = lax.dynamic_slice(g_meta_full, (0, me * (d // N_DEV)), (N_META, d // N_DEV))
    g_cw_loc = lax.dynamic_slice(g_cw_full, (0, me * cw_blk), (3, cw_blk))

    d_w_out, m_w_out_n, v_w_out_n = _adamw(w_out[0], g_w_out, m_w_out[0], v_w_out[0], "adamw_w_out", 128)
    d_w_up, m_w_up_n, v_w_up_n = _adamw(w_up[0], g_w_up, m_w_up[0], v_w_up[0], "adamw_w_up", 128)
    d_w_down, m_w_down_n, v_w_down_n = _adamw(w_down[0], g_w_down, m_w_down[0], v_w_down[0], "adamw_w_down", down_blk)
    (r_in,) = _exchange_wait(pending["in"], ["scatter"], d_w_up, "grads_in_wait")
    g_w_in = _sum_slots(r_in, "sum_w_in", in_blk_pad)[:in_blk].T
    d_w_in, m_w_in_n, v_w_in_n = _adamw(w_in[0], g_w_in, m_w_in[0], v_w_in[0], "adamw_w_in", 128)
    row = lambda a: a.reshape(1, d)
    sm_grads = [g_meta_loc, g_attn, g_fox_b, g_ret, g_ffn, g_cw_loc, g_conv_b, g_final]
    sm_w = [meta_tokens, attn_norm_g, fox_forget_b, ret_norm_g, ffn_norm_g, conv_w[0], conv_b, row(final_norm_g)]
    sm_m = [m_meta_tokens, m_attn_norm_g, m_fox_forget_b, m_ret_norm_g, m_ffn_norm_g, m_conv_w[0], m_conv_b,
            row(m_final_norm_g)]
    sm_v = [v_meta_tokens, v_attn_norm_g, v_fox_forget_b, v_ret_norm_g, v_ffn_norm_g, v_conv_w[0], v_conv_b,
            row(v_final_norm_g)]
    dl, ml, vl = [lst[:7] + [lst[7].reshape(d)] for lst in _adamw_small(sm_w, sm_grads, sm_m, sm_v, "adamw_small")]

    def by_weight(meta_, attn_, w_in_, fox_, ret_, w_out_, ffn_, w_up_, cw_, cb_, w_down_, final_):
        return (meta_, attn_, w_in_[None], fox_, ret_, w_out_[None], ffn_, w_up_[None], cw_[None], cb_, w_down_[None], final_)

    grads_out = by_weight(g_meta_loc, g_attn, g_w_in, g_fox_b, g_ret, g_w_out, g_ffn, g_w_up, g_cw_loc, g_conv_b,
                          g_w_down, g_final.reshape(d))
    delta_out = by_weight(dl[0], dl[1], d_w_in, dl[2], dl[3], d_w_out, dl[4], d_w_up, dl[5], dl[6], d_w_down, dl[7])
    m_out = by_weight(ml[0], ml[1], m_w_in_n, ml[2], ml[3], m_w_out_n, ml[4], m_w_up_n, ml[5], ml[6], m_w_down_n, ml[7])
    v_out = by_weight(vl[0], vl[1], v_w_in_n, vl[2], vl[3], v_w_out_n, vl[4], v_w_up_n, vl[5], vl[6], v_w_down_n, vl[7])
    return (loss, grad_x[None]) + grads_out + delta_out + m_out + v_out
```

```python
import numpy as np
import jax
import jax.numpy as jnp
from jax import lax
from jax.experimental import pallas as pl
from jax.experimental.pallas import tpu as pltpu

F32 = jnp.float32
BF16 = jnp.bfloat16

D_MODEL = 1024
N_META = 16
N_PAD = 112
PREFIX = 128
RET_HEADS = 4
RET_DK = 64
RET_DV = 128
FOX_HEADS = 8
FOX_DH = 64
D_FF = 2816
ROPE_BASE = 10000.0
EPS = 1e-6
NEG = -1e30
RET_QK = RET_HEADS * RET_DK
RET_V = RET_HEADS * RET_DV
FOX_W = FOX_HEADS * FOX_DH
IN_WIDTH = 2 * RET_QK + 2 * RET_V + 3 * FOX_W + FOX_HEADS
IN_PAD = 3200
FF_COL_BLOCK = (IN_WIDTH - FOX_HEADS) // 128
QK_SCALE = 0.125

ADAM_LR = 0.001
ADAM_B1 = 0.9
ADAM_B2 = 0.999
ADAM_EPS = 1e-08
ADAM_WD = 0.01
ADAM_STEP = 10

N_DEV = 8
LANE = 128
ROW_TILE = 128
TOK_TILE = 384

NN = (((1,), (0,)), ((), ()))
NT = (((1,), (1,)), ((), ()))
TN = (((0,), (0,)), ((), ()))


def _pcall(body, **kw):
    return pl.pallas_call(body, **kw)


def _params(*sem):
    return pltpu.CompilerParams(dimension_semantics=sem)


def _dot(a, b, dims=NN):
    return lax.dot_general(a, b, dims, preferred_element_type=F32)


def _sigmoid(x):
    return 0.5 * jnp.tanh(0.5 * x) + 0.5


def _matmul(a, b, *, mode, grid, a_spec, b_spec, o_spec, out_shape, name, add=None, add_spec=None, after=None):
    dims = {"nn": NN, "nt": NT, "tn": TN}[mode]
    nk = grid[2]
    has_add = add is not None
    a_list, b_list = (list(a), list(b)) if isinstance(a, (list, tuple)) else ([a], [b])
    a_specs, b_specs = (list(a_spec), list(b_spec)) if isinstance(a_spec, (list, tuple)) else ([a_spec], [b_spec])
    nt = len(a_list)
    n_in = 2 * nt + int(has_add) + int(after is not None)

    def body(*refs):
        a_refs, b_refs = refs[:nt], refs[nt:2 * nt]
        add_ref = refs[2 * nt] if has_add else None
        o_ref = refs[n_in]
        part = _dot(a_refs[0][...].astype(BF16), b_refs[0][...].astype(BF16), dims)
        for ar, br in zip(a_refs[1:], b_refs[1:]):
            part = part + _dot(ar[...].astype(BF16), br[...].astype(BF16), dims)

        def finish(acc):
            if has_add:
                acc = acc + add_ref[...]
            o_ref[...] = acc.astype(o_ref.dtype)

        if nk == 1:
            finish(part)
        else:
            acc_ref = refs[-1]
            k = pl.program_id(2)

            @pl.when(k == 0)
            def _():
                acc_ref[...] = part

            @pl.when(k > 0)
            def _():
                acc_ref[...] += part

            @pl.when(k == nk - 1)
            def _():
                finish(acc_ref[...])

    in_specs = a_specs + b_specs + ([add_spec] if has_add else [])
    args = tuple(a_list) + tuple(b_list) + ((add,) if has_add else ())
    if after is not None:
        in_specs, args = in_specs + [pl.BlockSpec(memory_space=pl.ANY)], args + (after,)
    scratch = [] if nk == 1 else [pltpu.VMEM(tuple(d for d in o_spec.block_shape if d is not None), F32)]
    return _pcall(
        body, name=name, grid=grid, in_specs=in_specs, out_specs=o_spec, out_shape=out_shape,
        scratch_shapes=scratch, compiler_params=_params("parallel", "parallel", "arbitrary"),
    )(*args)


def _mm_simple(a, b, *, mode, tm, tn, tk, out_dtype, name, add=None, after=None):
    if mode == "tn":
        K, M = a.shape
    else:
        M, K = a.shape
    N = b.shape[0] if mode == "nt" else b.shape[1]
    grid = (M // tm, N // tn, K // tk)
    resident = dict(pipeline_mode=pl.Buffered(1)) if (tn == N and tk == K) else {}
    a_spec = pl.BlockSpec((tk, tm), lambda i, j, k: (k, i)) if mode == "tn" else pl.BlockSpec((tm, tk), lambda i, j, k: (i, k))
    b_spec = (pl.BlockSpec((tn, tk), lambda i, j, k: (j, k), **resident) if mode == "nt"
              else pl.BlockSpec((tk, tn), lambda i, j, k: (k, j), **resident))
    o_spec = pl.BlockSpec((tm, tn), lambda i, j, k: (i, j))
    return _matmul(a, b, mode=mode, grid=grid, a_spec=a_spec, b_spec=b_spec, o_spec=o_spec,
                   out_shape=jax.ShapeDtypeStruct((M, N), out_dtype), name=name, add=add,
                   add_spec=o_spec if add is not None else None, after=after)


def _prep_norm(x, meta, gain, name):
    seq, d = x.shape
    t = seq + PREFIX

    def body(xa_ref, xb_ref, xc_ref, meta_ref, g_ref, h_ref, n_ref):
        i = pl.program_id(0)

        @pl.when(i == 0)
        def _():
            h_ref[0:N_PAD, :] = jnp.zeros((N_PAD, d), F32)
            h_ref[N_PAD:ROW_TILE, :] = meta_ref[...]

        @pl.when(i > 0)
        def _():
            h_ref[0:ROW_TILE, :] = xa_ref[...]

        h_ref[ROW_TILE:2 * ROW_TILE, :] = xb_ref[...]
        h_ref[2 * ROW_TILE:3 * ROW_TILE, :] = xc_ref[...]
        h = h_ref[...]
        r = lax.rsqrt(jnp.mean(h * h, axis=-1, keepdims=True) + EPS)
        n_ref[...] = (h * r * g_ref[...]).astype(BF16)

    return _pcall(
        body, name=name, grid=(t // TOK_TILE,),
        in_specs=_shifted_row_specs(d) + [pl.BlockSpec((N_META, d), lambda i: (0, 0)), pl.BlockSpec((1, d), lambda i: (0, 0))],
        out_specs=[pl.BlockSpec((TOK_TILE, d), lambda i: (i, 0)), pl.BlockSpec((TOK_TILE, d), lambda i: (i, 0))],
        out_shape=[jax.ShapeDtypeStruct((t, d), F32), jax.ShapeDtypeStruct((t, d), BF16)],
        compiler_params=_params("parallel"),
    )(x, x, x, meta, gain)


def _shifted_row_specs(d):
    blocks_per_tile = TOK_TILE // ROW_TILE
    return [pl.BlockSpec((ROW_TILE, d), lambda i, r=r: (jnp.maximum(blocks_per_tile * i + r, 0), 0)) for r in (-1, 0, 1)]


def _rmsnorm(h, gain, name):
    t, d = h.shape

    def body(h_ref, g_ref, n_ref):
        x = h_ref[...]
        r = lax.rsqrt(jnp.mean(x * x, axis=-1, keepdims=True) + EPS)
        n_ref[...] = (x * r * g_ref[...]).astype(BF16)

    return _pcall(
        body, name=name, grid=(t // TOK_TILE,),
        in_specs=[pl.BlockSpec((TOK_TILE, d), lambda i: (i, 0)), pl.BlockSpec((1, d), lambda i: (0, 0))],
        out_specs=pl.BlockSpec((TOK_TILE, d), lambda i: (i, 0)),
        out_shape=jax.ShapeDtypeStruct((t, d), BF16),
        compiler_params=_params("parallel"),
    )(h, gain)


def _rmsnorm_bwd(dn, h, gain, dres, name):
    t, d = h.shape

    def body(dn_ref, h_ref, g_ref, dres_ref, dh_ref, gg_ref):
        i = pl.program_id(0)
        x = h_ref[...]
        r = lax.rsqrt(jnp.mean(x * x, axis=-1, keepdims=True) + EPS)
        xhat = x * r
        dy = dn_ref[...]
        u = dy * g_ref[...]
        dh_ref[...] = dres_ref[...] + r * (u - xhat * jnp.mean(u * xhat, axis=-1, keepdims=True))
        part = jnp.sum(dy * xhat, axis=0, keepdims=True)

        @pl.when(i == 0)
        def _():
            gg_ref[...] = part

        @pl.when(i > 0)
        def _():
            gg_ref[...] += part

    return _pcall(
        body, name=name, grid=(t // TOK_TILE,),
        in_specs=[pl.BlockSpec((TOK_TILE, d), lambda i: (i, 0)), pl.BlockSpec((TOK_TILE, d), lambda i: (i, 0)),
                  pl.BlockSpec((1, d), lambda i: (0, 0)), pl.BlockSpec((TOK_TILE, d), lambda i: (i, 0))],
        out_specs=[pl.BlockSpec((TOK_TILE, d), lambda i: (i, 0)), pl.BlockSpec((1, d), lambda i: (0, 0))],
        out_shape=[jax.ShapeDtypeStruct((t, d), F32), jax.ShapeDtypeStruct((1, d), F32)],
        compiler_params=_params("arbitrary"),
    )(dn, h, gain, dres)


def _loss_bwd(h2, target, gain, name):
    t, d = h2.shape

    def body(h_ref, ta_ref, tb_ref, tc_ref, g_ref, loss_ref, dh_ref, gg_ref):
        i = pl.program_id(0)

        @pl.when(i == 0)
        def _():
            loss_ref[...] = jnp.zeros_like(loss_ref)
            gg_ref[...] = jnp.zeros_like(gg_ref)

        x = h_ref[...]
        r = lax.rsqrt(jnp.mean(x * x, axis=-1, keepdims=True) + EPS)
        xhat = x * r
        g = g_ref[...]
        tgt = jnp.concatenate([ta_ref[...], tb_ref[...], tc_ref[...]], axis=0)
        counted = (i * TOK_TILE + lax.broadcasted_iota(jnp.int32, (TOK_TILE, 1), 0)) >= PREFIX
        err = jnp.where(counted, xhat * g - tgt, 0.0)
        loss_ref[...] += 0.5 * jnp.sum(jnp.mean(err * err, axis=-1, keepdims=True))
        dy = err * (1.0 / d)
        u = dy * g
        dh_ref[...] = r * (u - xhat * jnp.mean(u * xhat, axis=-1, keepdims=True))
        gg_ref[...] += jnp.sum(dy * xhat, axis=0, keepdims=True)

    return _pcall(
        body, name=name, grid=(t // TOK_TILE,),
        in_specs=[pl.BlockSpec((TOK_TILE, d), lambda i: (i, 0))] + _shifted_row_specs(d) + [pl.BlockSpec((1, d), lambda i: (0, 0))],
        out_specs=[pl.BlockSpec((8, LANE), lambda i: (0, 0)), pl.BlockSpec((TOK_TILE, d), lambda i: (i, 0)),
                   pl.BlockSpec((1, d), lambda i: (0, 0))],
        out_shape=[jax.ShapeDtypeStruct((8, LANE), F32), jax.ShapeDtypeStruct((t, d), F32),
                   jax.ShapeDtypeStruct((1, d), F32)],
        compiler_params=_params("arbitrary"),
    )(h2, target, target, target, gain)


def _ret_consts(bk):
    gam = 1.0 - 2.0 ** (-5.0 - np.arange(RET_HEADS))
    n = np.arange(bk)
    same_or_earlier_chunk = (n[None, :] // 64) <= (n[:, None] // 64)
    w = gam[:, None, None] ** np.abs(n[:, None] - n[None, :])[None] * same_or_earlier_chunk[None]
    wq = gam[:, None] ** (n[None, :] + 1.0)
    wk = gam[:, None] ** (bk - 1.0 - n[None, :])
    mask = (np.arange(RET_QK)[None, :] // RET_DK) == np.arange(RET_HEADS)[:, None]
    return (jnp.asarray(w, F32), jnp.asarray(wq[:, :, None], F32), jnp.asarray(wk[:, :, None], F32),
            jnp.asarray(mask[:, None, :], F32), [float(g ** bk) for g in gam])


def _rope_tables(t):
    half = RET_DK // 2
    inv = 1.0 / (ROPE_BASE ** (jnp.arange(half, dtype=F32) / half))
    ang = jnp.arange(t).astype(F32)[:, None] * inv[None, :]
    cos, sin = jnp.cos(ang), jnp.sin(ang)
    return (jnp.tile(jnp.concatenate([cos, cos], axis=1), (1, RET_HEADS)),
            jnp.tile(jnp.concatenate([-sin, sin], axis=1), (1, RET_HEADS)))


def _swap_halves(x):
    outs = []
    for s in range(x.shape[1] // LANE):
        xs = x[:, LANE * s:LANE * (s + 1)]
        lane = lax.broadcasted_iota(jnp.int32, xs.shape, 1)
        outs.append(jnp.where((lane & 32) == 0, pltpu.roll(xs, LANE - 32, axis=1), pltpu.roll(xs, 32, axis=1)))
    return outs[0] if len(outs) == 1 else jnp.concatenate(outs, axis=1)


def _rope(x, cos, sin_signed):
    return x * cos + _swap_halves(x) * sin_signed


def _rope_t(dx, cos, sin_signed):
    return dx * cos + _swap_halves(dx * sin_signed)


def _ret_fwd(proj, cos, sin, gain, name):
    t = proj.shape[0]
    bk = TOK_TILE
    nb = t // bk
    w, wq, wk, mask, g_blk = _ret_consts(bk)

    def body(q_ref, k_ref, v_ref, rg_ref, cos_ref, sin_ref, w_ref, wq_ref, wk_ref, mask_ref, gain_ref,
             opre_ref, og_ref, st_ref, r_ref):
        i = pl.program_id(0)

        @pl.when(i == 0)
        def _():
            r_ref[...] = jnp.zeros_like(r_ref)

        c, s = cos_ref[...], sin_ref[...]
        valid = ((i * bk + lax.broadcasted_iota(jnp.int32, (bk, 1), 0)) >= N_PAD).astype(F32)
        qr = _rope(q_ref[...], c, s)
        kr = _rope(k_ref[...], c, s) * QK_SCALE * valid
        kb = kr.astype(BF16)
        for h in range(RET_HEADS):
            hm = mask_ref[h]
            cols = slice(RET_DV * h, RET_DV * (h + 1))
            vh = v_ref[:, cols].astype(BF16)
            r_prev = r_ref[h]
            st_ref[0, h] = r_prev
            sm = _dot((qr * hm).astype(BF16), kb, NT) * w_ref[h]
            o = _dot(sm.astype(BF16), vh) + _dot((qr * (hm * wq_ref[h])).astype(BF16), r_prev.astype(BF16))
            r_ref[h] = g_blk[h] * r_prev + _dot((kr * wk_ref[h]).astype(BF16), vh, TN)
            opre_ref[:, cols] = o
            rstd = lax.rsqrt(jnp.mean(o * o, axis=-1, keepdims=True) + EPS)
            rg = rg_ref[:, cols]
            og_ref[:, cols] = (o * rstd * gain_ref[:, cols] * (rg * _sigmoid(rg))).astype(BF16)

    full = lambda shape: pl.BlockSpec(shape, lambda i: (0,) * len(shape))
    return _pcall(
        body, name=name, grid=(nb,),
        in_specs=[pl.BlockSpec((bk, RET_QK), lambda i: (i, 0)), pl.BlockSpec((bk, RET_QK), lambda i: (i, 1)),
                  pl.BlockSpec((bk, RET_V), lambda i: (i, 1)), pl.BlockSpec((bk, RET_V), lambda i: (i, 2)),
                  pl.BlockSpec((bk, RET_QK), lambda i: (i, 0)), pl.BlockSpec((bk, RET_QK), lambda i: (i, 0)),
                  full((RET_HEADS, bk, bk)), full((RET_HEADS, bk, 1)), full((RET_HEADS, bk, 1)),
                  full((RET_HEADS, 1, RET_QK)), full((1, RET_V))],
        out_specs=[pl.BlockSpec((bk, RET_V), lambda i: (i, 0)), pl.BlockSpec((bk, RET_V), lambda i: (i, 0)),
                   pl.BlockSpec((1, RET_HEADS, RET_QK, RET_DV), lambda i: (i, 0, 0, 0))],
        out_shape=[jax.ShapeDtypeStruct((t, RET_V), F32), jax.ShapeDtypeStruct((t, RET_V + FOX_W), BF16),
                   jax.ShapeDtypeStruct((nb, RET_HEADS, RET_QK, RET_DV), F32)],
        scratch_shapes=[pltpu.VMEM((RET_HEADS, RET_QK, RET_DV), F32)],
        compiler_params=_params("arbitrary"),
    )(proj, proj, proj, proj, cos, sin, w, wq, wk, mask, gain)


def _ret_bwd(proj, cos, sin, gain, dmixed, opre, states, name):
    t = proj.shape[0]
    bk = TOK_TILE
    nb = t // bk
    w, wq, wk, mask, g_blk = _ret_consts(bk)
    v0, g0 = 2 * RET_QK, 2 * RET_QK + RET_V

    def body(q_ref, k_ref, v_ref, rg_ref, cos_ref, sin_ref, w_ref, wq_ref, wk_ref, mask_ref, gain_ref,
             dog_ref, opre_ref, st_ref, dp_ref, gg_ref, dr_ref):
        step = pl.program_id(0)
        i = nb - 1 - step

        @pl.when(step == 0)
        def _():
            dr_ref[...] = jnp.zeros_like(dr_ref)
            gg_ref[...] = jnp.zeros_like(gg_ref)

        c, s = cos_ref[...], sin_ref[...]
        valid = ((i * bk + lax.broadcasted_iota(jnp.int32, (bk, 1), 0)) >= N_PAD).astype(F32)
        qr = _rope(q_ref[...], c, s)
        kr = _rope(k_ref[...], c, s) * QK_SCALE * valid
        kb = kr.astype(BF16)
        dqr = jnp.zeros((bk, RET_QK), F32)
        dkr = jnp.zeros((bk, RET_QK), F32)
        for h in range(RET_HEADS):
            hm = mask_ref[h]
            cols = slice(RET_DV * h, RET_DV * (h + 1))
            vh = v_ref[:, cols].astype(BF16)
            o = opre_ref[:, cols]
            rstd = lax.rsqrt(jnp.mean(o * o, axis=-1, keepdims=True) + EPS)
            xhat = o * rstd
            rg = rg_ref[:, cols]
            sg = _sigmoid(rg)
            gate = rg * sg
            gn = gain_ref[:, cols]
            dog = dog_ref[:, cols]
            dp_ref[:, g0 + RET_DV * h:g0 + RET_DV * (h + 1)] = (
                dog * xhat * gn * (sg * (1.0 + rg * (1.0 - sg)))).astype(BF16)
            gg_ref[:, cols] += jnp.sum(dog * xhat * gate, axis=0, keepdims=True)
            dxh = dog * gn * gate
            do = (rstd * (dxh - xhat * jnp.mean(dxh * xhat, axis=-1, keepdims=True))).astype(BF16)
            qm = (qr * hm).astype(BF16)
            qw = (qr * (hm * wq_ref[h])).astype(BF16)
            kw = (kr * wk_ref[h]).astype(BF16)
            wh = w_ref[h]
            sm = (_dot(qm, kb, NT) * wh).astype(BF16)
            ds = (_dot(do, vh, NT) * wh).astype(BF16)
            dr = dr_ref[h]
            drb = dr.astype(BF16)
            dp_ref[:, v0 + RET_DV * h:v0 + RET_DV * (h + 1)] = (_dot(sm, do, TN) + _dot(kw, drb)).astype(BF16)
            dqr = dqr + _dot(ds, kb) * hm + _dot(do, st_ref[0, h].astype(BF16), NT) * (hm * wq_ref[h])
            dkr = dkr + _dot(ds, qm, TN) + _dot(vh, drb, NT) * wk_ref[h]
            dr_ref[h] = g_blk[h] * dr + _dot(qw, do, TN)
        dp_ref[:, 0:RET_QK] = _rope_t(dqr, c, s).astype(BF16)
        dp_ref[:, RET_QK:2 * RET_QK] = _rope_t(dkr * (QK_SCALE * valid), c, s).astype(BF16)

    full = lambda shape: pl.BlockSpec(shape, lambda i: (0,) * len(shape))
    rev = lambda col: (lambda i: (nb - 1 - i, col))
    return _pcall(
        body, name=name, grid=(nb,),
        in_specs=[pl.BlockSpec((bk, RET_QK), rev(0)), pl.BlockSpec((bk, RET_QK), rev(1)),
                  pl.BlockSpec((bk, RET_V), rev(1)), pl.BlockSpec((bk, RET_V), rev(2)),
                  pl.BlockSpec((bk, RET_QK), rev(0)), pl.BlockSpec((bk, RET_QK), rev(0)),
                  full((RET_HEADS, bk, bk)), full((RET_HEADS, bk, 1)), full((RET_HEADS, bk, 1)),
                  full((RET_HEADS, 1, RET_QK)), full((1, RET_V)),
                  pl.BlockSpec((bk, RET_V), rev(0)), pl.BlockSpec((bk, RET_V), rev(0)),
                  pl.BlockSpec((1, RET_HEADS, RET_QK, RET_DV), lambda i: (nb - 1 - i, 0, 0, 0))],
        out_specs=[pl.BlockSpec((bk, g0 + RET_V), rev(0)), pl.BlockSpec((1, RET_V), lambda i: (0, 0))],
        out_shape=[jax.ShapeDtypeStruct((t, IN_PAD), BF16), jax.ShapeDtypeStruct((1, RET_V), F32)],
        scratch_shapes=[pltpu.VMEM((RET_HEADS, RET_QK, RET_DV), F32)],
        compiler_params=_params("arbitrary"),
    )(proj, proj, proj, proj, cos, sin, w, wq, wk, mask, gain, dmixed, opre, states)


def _forget_cumsum(proj, bias, name):
    t = proj.shape[0]
    rt = TOK_TILE
    nb = t // rt
    tril = jnp.asarray(np.tril(np.ones((rt, rt))), F32)

    def body(z_ref, b_ref, tril_ref, c_ref, carry_ref):
        i = pl.program_id(0)

        @pl.when(i == 0)
        def _():
            carry_ref[...] = jnp.zeros_like(carry_ref)

        z = z_ref[...] + b_ref[...]
        logf = jnp.minimum(z, 0.0) - jnp.log(1.0 + jnp.exp(-jnp.abs(z)))
        c = lax.dot_general(tril_ref[...], logf, NN, precision=lax.Precision.HIGHEST,
                            preferred_element_type=F32) + carry_ref[...]
        c_ref[...] = c
        carry_ref[...] = c[rt - 1:rt, :]

    return _pcall(
        body, name=name, grid=(nb,),
        in_specs=[pl.BlockSpec((rt, LANE), lambda i: (i, FF_COL_BLOCK)), pl.BlockSpec((1, LANE), lambda i: (0, 0)),
                  pl.BlockSpec((rt, rt), lambda i: (0, 0))],
        out_specs=pl.BlockSpec((rt, LANE), lambda i: (i, 0)),
        out_shape=jax.ShapeDtypeStruct((t, LANE), F32),
        scratch_shapes=[pltpu.VMEM((1, LANE), F32)],
        compiler_params=_params("arbitrary"),
    )(proj, bias, tril)


def _forget_cumsum_bwd(proj, bias, drs, dcs, dproj, name):
    t = proj.shape[0]
    rt = TOK_TILE
    nb = t // rt
    triu = jnp.asarray(np.triu(np.ones((rt, rt))), F32)

    def body(z_ref, b_ref, triu_ref, drs_ref, dcs_ref, dproj_in, dz_ref, gb_ref, carry_ref):
        step = pl.program_id(0)

        @pl.when(step == 0)
        def _():
            carry_ref[...] = jnp.zeros_like(carry_ref)
            gb_ref[...] = jnp.zeros_like(gb_ref)

        dlogf = lax.dot_general(triu_ref[...], drs_ref[...] - dcs_ref[...], NN, precision=lax.Precision.HIGHEST,
                                preferred_element_type=F32) + carry_ref[...]
        carry_ref[...] = dlogf[0:1, :]
        z = z_ref[...] + b_ref[...]
        is_head = lax.broadcasted_iota(jnp.int32, (rt, LANE), 1) < FOX_HEADS
        dz = jnp.where(is_head, dlogf / (1.0 + jnp.exp(z)), 0.0)
        dz_ref[...] = dz.astype(BF16)
        gb_ref[...] += jnp.sum(dz, axis=0, keepdims=True)

    return _pcall(
        body, name=name, grid=(nb,),
        in_specs=[pl.BlockSpec((rt, LANE), lambda i: (nb - 1 - i, FF_COL_BLOCK)),
                  pl.BlockSpec((1, LANE), lambda i: (0, 0)),
                  pl.BlockSpec((rt, rt), lambda i: (0, 0)),
                  pl.BlockSpec((rt, LANE), lambda i: (nb - 1 - i, 0)),
                  pl.BlockSpec((rt, LANE), lambda i: (nb - 1 - i, 0)),
                  pl.BlockSpec(memory_space=pl.ANY)],
        out_specs=[pl.BlockSpec((rt, LANE), lambda i: (nb - 1 - i, FF_COL_BLOCK)),
                   pl.BlockSpec((1, LANE), lambda i: (0, 0))],
        out_shape=[jax.ShapeDtypeStruct(dproj.shape, BF16), jax.ShapeDtypeStruct((1, LANE), F32)],
        input_output_aliases={5: 0},
        scratch_shapes=[pltpu.VMEM((1, LANE), F32)],
        compiler_params=_params("arbitrary"),
    )(proj, bias, triu, drs, dcs, dproj)


FOX_PAIRS = FOX_HEADS // 2
L_ONE_Q = FOX_DH
L_ONE_K = FOX_DH + 3
L_LSE = FOX_DH + 4


def _split3(x):
    hi = x.astype(BF16).astype(F32)
    r = x - hi
    mid = r.astype(BF16).astype(F32)
    return hi, mid, r - mid


def _head_to_low(slab, e):
    return slab if e == 0 else pltpu.roll(slab, FOX_DH, axis=1)


def _pair(a, b, low):
    return jnp.where(low, a, pltpu.roll(b, FOX_DH, axis=1))


def _fox_prep(proj, c, name):
    t = proj.shape[0]
    tq = TOK_TILE

    def body(p_ref, c_ref, qa_ref, ka_ref, va_ref):
        i = pl.program_id(0)
        lane = lax.broadcasted_iota(jnp.int32, (tq, LANE), 1)
        low = lane < FOX_DH
        live = (i * tq + lax.broadcasted_iota(jnp.int32, (tq, 1), 0)) >= N_PAD
        q_tail = jnp.where(lane < L_ONE_Q + 3, 1.0, 0.0)
        k_ones = (lane >= L_ONE_K) & (lane < L_ONE_K + 4)
        v_tail = jnp.where(lane < FOX_DH + 2, 1.0, 0.0)
        for pair in range(FOX_PAIRS):
            base = 3 * LANE * pair
            for e in range(2):
                h = 2 * pair + e
                q = _head_to_low(p_ref[:, base:base + LANE], e)
                k = _head_to_low(p_ref[:, base + LANE:base + 2 * LANE], e)
                v = _head_to_low(p_ref[:, base + 2 * LANE:base + 3 * LANE], e)
                hi, mid, lo = _split3(jnp.where(live, -c_ref[:, h:h + 1], NEG))
                ka = jnp.where(low, k, jnp.where(k_ones, 1.0, 0.0))
                ka = jnp.where(lane == L_ONE_Q, hi, jnp.where(lane == L_ONE_Q + 1, mid, jnp.where(lane == L_ONE_Q + 2, lo, ka)))
                qa_ref[h] = jnp.where(low, q * QK_SCALE, q_tail).astype(BF16)
                ka_ref[h] = ka.astype(BF16)
                va_ref[h] = jnp.where(low, v, v_tail).astype(BF16)

    out = jax.ShapeDtypeStruct((FOX_HEADS, t, LANE), BF16)
    ospec = pl.BlockSpec((FOX_HEADS, tq, LANE), lambda i: (0, i, 0))
    return _pcall(
        body, name=name, grid=(t // tq,),
        in_specs=[pl.BlockSpec((tq, 3 * FOX_W), lambda i: (i, 1)), pl.BlockSpec((tq, LANE), lambda i: (i, 0))],
        out_specs=[ospec, ospec, ospec], out_shape=[out, out, out],
        compiler_params=_params("parallel"),
    )(proj, c)


STEP_PAIRS = 2
STEP_HEADS = 2 * STEP_PAIRS
FOX_GROUPS = FOX_PAIRS // STEP_PAIRS


def _blockdiag(a, b):
    z = jnp.zeros_like(a)
    return jnp.concatenate([jnp.concatenate([a, z], axis=1), jnp.concatenate([z, b], axis=1)], axis=0)


def _fox_fwd(qa, ka, va, mixed, name):
    nh, nq, tq, _ = qa.shape
    t = nq * tq

    def body(qa_ref, ka_ref, va_ref, mixed_in, mixed_ref, o_ref, lse_ref):
        i = pl.program_id(1)
        lane = lax.broadcasted_iota(jnp.int32, (tq, LANE), 1)
        causal = lax.broadcasted_iota(jnp.int32, (tq, tq), 1) <= lax.broadcasted_iota(jnp.int32, (tq, tq), 0)
        qps = [jnp.concatenate([qa_ref[2 * c], qa_ref[2 * c + 1]], axis=1) for c in range(STEP_PAIRS)]

        def logits(j):
            return [_dot(qps[c], _blockdiag(ka_ref[2 * c, j], ka_ref[2 * c + 1, j]), NT) for c in range(STEP_PAIRS)]

        def update(j, scores, carry, diagonal):
            new = []
            for c in range(STEP_PAIRS):
                ms, acc = carry[c]
                ps, ms_new, alphas = [], [], []
                for e in range(2):
                    s = scores[c][:, e * tq:(e + 1) * tq]
                    if diagonal:
                        s = jnp.where(causal, s, NEG)
                    m_new = jnp.maximum(ms[e], jnp.max(s, axis=-1, keepdims=True))
                    ps.append(jnp.exp(s - m_new).astype(BF16))
                    ms_new.append(m_new)
                    alphas.append(jnp.broadcast_to(jnp.exp(ms[e] - m_new), (tq, LANE)))
                pv = _dot(jnp.concatenate(ps, axis=1), _blockdiag(va_ref[2 * c, j], va_ref[2 * c + 1, j]))
                new.append((tuple(ms_new), jnp.concatenate(alphas, axis=1) * acc + pv))
            return tuple(new)

        m0 = jnp.full((tq, 1), NEG, F32)
        init = tuple(((m0, m0), jnp.zeros((tq, 2 * LANE), F32)) for _ in range(STEP_PAIRS))
        carry = lax.fori_loop(0, i, lambda j, cr: update(j, logits(j), cr, False), init)
        o_pairs = []
        lse = jnp.zeros((tq, LANE), F32)
        for c, (ms, acc) in enumerate(update(i, logits(i), carry, True)):
            outs = []
            for e in range(2):
                half = acc[:, e * LANE:(e + 1) * LANE]
                l = half[:, FOX_DH:FOX_DH + 1]
                outs.append(half / l)
                lse = jnp.where(lane == 2 * c + e, ms[e] + jnp.log(l), lse)
            o_pairs.append(_pair(outs[0], outs[1], lane < FOX_DH))
        o_all = jnp.concatenate(o_pairs, axis=1)
        mixed_ref[...] = o_all.astype(BF16)
        o_ref[...] = o_all
        lse_ref[...] = lse

    width = STEP_PAIRS * LANE
    whole = pl.BlockSpec((STEP_HEADS, nq, tq, LANE), lambda g, i: (g, 0, 0, 0), pipeline_mode=pl.Buffered(1))
    return _pcall(
        body, name=name, grid=(FOX_GROUPS, nq),
        in_specs=[pl.BlockSpec((STEP_HEADS, None, tq, LANE), lambda g, i: (g, i, 0, 0)), whole, whole,
                  pl.BlockSpec(memory_space=pl.ANY)],
        out_specs=[pl.BlockSpec((tq, width), lambda g, i: (i, RET_V // width + g)),
                   pl.BlockSpec((tq, width), lambda g, i: (i, g)),
                   pl.BlockSpec((None, tq, LANE), lambda g, i: (g, i, 0))],
        out_shape=[jax.ShapeDtypeStruct(mixed.shape, BF16), jax.ShapeDtypeStruct((t, FOX_W), F32),
                   jax.ShapeDtypeStruct((FOX_GROUPS, t, LANE), F32)],
        input_output_aliases={3: 0},
        compiler_params=_params("parallel", "parallel"),
    )(qa, ka, va, mixed)


def _fox_prep_bwd(dmixed, o_fox, lse, qa, name):
    t = dmixed.shape[0]
    tq = TOK_TILE

    def body(dm_ref, o_ref, lse_ref, qa_ref, qab_ref, doa_ref):
        i = pl.program_id(0)
        lane = lax.broadcasted_iota(jnp.int32, (tq, LANE), 1)
        low = lane < FOX_DH
        live = (i * tq + lax.broadcasted_iota(jnp.int32, (tq, 1), 0)) >= N_PAD
        for pair in range(FOX_PAIRS):
            cols = slice(LANE * pair, LANE * (pair + 1))
            d_slab = dm_ref[:, cols]
            prod = d_slab * o_ref[:, cols]
            for e in range(2):
                h = 2 * pair + e
                nd = -jnp.sum(jnp.where(low, _head_to_low(prod, e), 0.0), axis=-1, keepdims=True)
                nd_hi = nd.astype(BF16).astype(F32)
                doa = jnp.where(low, _head_to_low(d_slab, e), 0.0)
                doa = jnp.where(lane == FOX_DH, nd_hi, jnp.where(lane == FOX_DH + 1, nd - nd_hi, doa))
                doa_ref[h] = doa.astype(BF16)
                lse_h = lse_ref[h // STEP_HEADS][:, h % STEP_HEADS:h % STEP_HEADS + 1]
                hi, mid, lo = _split3(jnp.where(live, -lse_h, 0.0))
                qab = qa_ref[h].astype(F32)
                qab = jnp.where(lane == L_LSE, hi, jnp.where(lane == L_LSE + 1, mid, jnp.where(lane == L_LSE + 2, lo, qab)))
                qab_ref[h] = qab.astype(BF16)

    out = jax.ShapeDtypeStruct((FOX_HEADS, t, LANE), BF16)
    hspec = pl.BlockSpec((FOX_HEADS, tq, LANE), lambda i: (0, i, 0))
    return _pcall(
        body, name=name, grid=(t // tq,),
        in_specs=[pl.BlockSpec((tq, FOX_W), lambda i: (i, 1)), pl.BlockSpec((tq, FOX_W), lambda i: (i, 0)),
                  pl.BlockSpec((FOX_GROUPS, tq, LANE), lambda i: (0, i, 0)), hspec],
        out_specs=[hspec, hspec], out_shape=[out, out],
        compiler_params=_params("parallel"),
    )(dmixed, o_fox, lse, qa)


def _fox_bwd(qab, doa, ka, va, dproj, name):
    nh, nq, tq, _ = qab.shape
    t = nq * tq
    slab = 3 * LANE * STEP_PAIRS
    group0 = (2 * RET_QK + 2 * RET_V) // slab

    def body(qab_ref, doa_ref, ka_ref, va_ref, dproj_in, dp_ref, drs_ref, dcs_ref, dq_ref):
        g, j = pl.program_id(0), pl.program_id(1)

        @pl.when((g == 0) & (j == 0))
        def _():
            drs_ref[...] = jnp.zeros_like(drs_ref)
            dcs_ref[...] = jnp.zeros_like(dcs_ref)

        @pl.when(j == 0)
        def _():
            dq_ref[...] = jnp.zeros_like(dq_ref)

        lane = lax.broadcasted_iota(jnp.int32, (tq, LANE), 1)
        low = lane < FOX_DH
        key_le_query = lax.broadcasted_iota(jnp.int32, (tq, tq), 0) <= lax.broadcasted_iota(jnp.int32, (tq, tq), 1)

        def by_head(c, a, b, col):
            h = STEP_HEADS * g + 2 * c
            return jnp.where(lane == h, a[:, col:col + 1], jnp.where(lane == h + 1, b[:, col:col + 1], 0.0))

        kbs = [ka_ref[h] for h in range(STEP_HEADS)]
        vbs = [va_ref[h] for h in range(STEP_HEADS)]

        def step(i, carry, diagonal):
            qbs = [qab_ref[h, i] for h in range(STEP_HEADS)]
            dobs = [doa_ref[h, i] for h in range(STEP_HEADS)]
            st = [_dot(kbs[h], qbs[h], NT) for h in range(STEP_HEADS)]
            dpt = [_dot(vbs[h], dobs[h], NT) for h in range(STEP_HEADS)]
            new = []
            for h in range(STEP_HEADS):
                p = jnp.exp(st[h])
                if diagonal:
                    p = jnp.where(key_le_query, p, 0.0)
                ds = (p * dpt[h]).astype(BF16)
                dq_ref[h, i] += _dot(ds, kbs[h], TN)
                dk, dv = carry[h]
                new.append((dk + _dot(ds, qbs[h]), dv + _dot(p.astype(BF16), dobs[h])))
            return tuple(new)

        zero = jnp.zeros((tq, LANE), F32)
        carry = step(j, tuple((zero, zero) for _ in range(STEP_HEADS)), True)
        carry = lax.fori_loop(j + 1, nq, lambda i, cr: step(i, cr, False), carry)
        rows = pl.ds(pl.multiple_of(j * tq, tq), tq)
        for c in range(STEP_PAIRS):
            (dka, dva), (dkb, dvb) = carry[2 * c], carry[2 * c + 1]
            c0 = 3 * LANE * c
            dp_ref[rows, c0 + LANE:c0 + 2 * LANE] = _pair(dka, dkb, low).astype(BF16)
            dp_ref[rows, c0 + 2 * LANE:c0 + 3 * LANE] = _pair(dva, dvb, low).astype(BF16)
            dcs_ref[rows, :] += by_head(c, dka, dkb, L_ONE_Q)

        @pl.when(j == nq - 1)
        def _():
            for c in range(STEP_PAIRS):
                for blk in range(nq):
                    r = slice(blk * tq, (blk + 1) * tq)
                    a, b = dq_ref[2 * c, blk], dq_ref[2 * c + 1, blk]
                    dp_ref[r, 3 * LANE * c:3 * LANE * c + LANE] = (_pair(a, b, low) * QK_SCALE).astype(BF16)
                    drs_ref[r, :] += by_head(c, a, b, L_ONE_K)

    whole = pl.BlockSpec((STEP_HEADS, nq, tq, LANE), lambda g, j: (g, 0, 0, 0), pipeline_mode=pl.Buffered(1))
    blk = pl.BlockSpec((STEP_HEADS, None, tq, LANE), lambda g, j: (g, j, 0, 0))
    sums = pl.BlockSpec((t, LANE), lambda g, j: (0, 0), pipeline_mode=pl.Buffered(1))
    return _pcall(
        body, name=name, grid=(FOX_GROUPS, nq),
        in_specs=[whole, whole, blk, blk, pl.BlockSpec(memory_space=pl.ANY)],
        out_specs=[pl.BlockSpec((t, slab), lambda g, j: (0, group0 + g)), sums, sums],
        out_shape=[jax.ShapeDtypeStruct(dproj.shape, BF16), jax.ShapeDtypeStruct((t, LANE), F32),
                   jax.ShapeDtypeStruct((t, LANE), F32)],
        input_output_aliases={4: 0},
        scratch_shapes=[pltpu.VMEM((STEP_HEADS, nq, tq, LANE), F32)],
        compiler_params=_params("arbitrary", "arbitrary"),
    )(qab, doa, ka, va, dproj)


HALO = 8


def _rows_ext(ref, r0, rows, t, before, after):
    lo, hi = r0 - before, r0 + rows + after
    parts = []
    if lo < 0:
        parts.append(jnp.zeros((-lo, LANE), F32))
    parts.append(ref[max(lo, 0):min(hi, t), :].astype(F32))
    if hi > t:
        parts.append(jnp.zeros((hi - t, LANE), F32))
    return parts[0] if len(parts) == 1 else jnp.concatenate(parts, axis=0)


def _conv_taps(a_ext, r0_ext, cw_ref, cb_ref):
    n = a_ext.shape[0]
    if r0_ext < N_PAD:
        row = r0_ext + lax.broadcasted_iota(jnp.int32, (n, 1), 0)
        a_ext = jnp.where(row >= N_PAD, a_ext, 0.0)
    a1 = pltpu.roll(a_ext, 1, axis=0)
    a2 = pltpu.roll(a_ext, 2, axis=0)
    acc = cb_ref[...] + a2 * cw_ref[0:1, :] + a1 * cw_ref[1:2, :] + a_ext * cw_ref[2:3, :]
    return a_ext, a1, a2, acc


def _conv_gate_fwd(up, conv_w8, conv_b, name):
    _, t, f = up.shape
    rows = TOK_TILE

    def body(a_ref, b_ref, cw_ref, cb_ref, g_ref):
        for r0 in range(0, t, rows):
            a_ext = _rows_ext(a_ref, r0, rows, t, HALO, 0)
            _, _, _, acc = _conv_taps(a_ext, r0 - HALO, cw_ref, cb_ref)
            acc = acc[HALO:, :]
            g_ref[r0:r0 + rows, :] = (acc * _sigmoid(acc) * b_ref[r0:r0 + rows, :]).astype(BF16)

    return _pcall(
        body, name=name, grid=(f // LANE,),
        in_specs=[pl.BlockSpec((None, t, LANE), lambda j: (0, 0, j)), pl.BlockSpec((None, t, LANE), lambda j: (1, 0, j)),
                  pl.BlockSpec((8, LANE), lambda j: (0, j)), pl.BlockSpec((1, LANE), lambda j: (0, j))],
        out_specs=pl.BlockSpec((t, LANE), lambda j: (0, j)),
        out_shape=jax.ShapeDtypeStruct((t, f), BF16),
        compiler_params=_params("parallel"),
    )(up, up, conv_w8, conv_b)


def _conv_gate_bwd(up, conv_w8, conv_b, dg, name):
    _, t, f = up.shape
    rows = TOK_TILE

    def body(a_ref, b_ref, cw_ref, cb_ref, dg_ref, dup_ref, gcw_ref, gcb_ref):
        gw = [jnp.zeros((1, LANE), F32) for _ in range(3)]
        gb = jnp.zeros((1, LANE), F32)
        for r0 in range(0, t, rows):
            a_ext = _rows_ext(a_ref, r0, rows, t, HALO, HALO)
            b_ext = _rows_ext(b_ref, r0, rows, t, HALO, HALO)
            dg_ext = _rows_ext(dg_ref, r0, rows, t, HALO, HALO)
            a0, a1, a2, acc = _conv_taps(a_ext, r0 - HALO, cw_ref, cb_ref)
            sg = _sigmoid(acc)
            dacc = dg_ext * b_ext * (sg * (1.0 + acc * (1.0 - sg)))
            n = dacc.shape[0]
            da = (dacc * cw_ref[2:3, :] + pltpu.roll(dacc, n - 1, axis=0) * cw_ref[1:2, :]
                  + pltpu.roll(dacc, n - 2, axis=0) * cw_ref[0:1, :])
            core = slice(HALO, HALO + rows)
            da = da[core, :]
            if r0 < N_PAD:
                row = r0 + lax.broadcasted_iota(jnp.int32, (rows, 1), 0)
                da = jnp.where(row >= N_PAD, da, 0.0)
            dup_ref[0, r0:r0 + rows, :] = da.astype(BF16)
            dup_ref[1, r0:r0 + rows, :] = (dg_ext * acc * sg)[core, :].astype(BF16)
            dacc_c = dacc[core, :]
            gw[0] = gw[0] + jnp.sum(dacc_c * a2[core, :], axis=0, keepdims=True)
            gw[1] = gw[1] + jnp.sum(dacc_c * a1[core, :], axis=0, keepdims=True)
            gw[2] = gw[2] + jnp.sum(dacc_c * a0[core, :], axis=0, keepdims=True)
            gb = gb + jnp.sum(dacc_c, axis=0, keepdims=True)
        gcw_ref[...] = jnp.zeros((8, LANE), F32)
        for tap in range(3):
            gcw_ref[tap:tap + 1, :] = gw[tap]
        gcb_ref[...] = gb

    return _pcall(
        body, name=name, grid=(f // LANE,),
        in_specs=[pl.BlockSpec((None, t, LANE), lambda j: (0, 0, j)), pl.BlockSpec((None, t, LANE), lambda j: (1, 0, j)),
                  pl.BlockSpec((8, LANE), lambda j: (0, j)), pl.BlockSpec((1, LANE), lambda j: (0, j)),
                  pl.BlockSpec((t, LANE), lambda j: (0, j))],
        out_specs=[pl.BlockSpec((2, t, LANE), lambda j: (0, 0, j)), pl.BlockSpec((8, LANE), lambda j: (0, j)),
                   pl.BlockSpec((1, LANE), lambda j: (0, j))],
        out_shape=[jax.ShapeDtypeStruct((2, t, f), BF16), jax.ShapeDtypeStruct((8, f), F32),
                   jax.ShapeDtypeStruct((1, f), F32)],
        compiler_params=_params("parallel"),
    )(up, up, conv_w8, conv_b, dg)


def _exchange(arrays, kinds, name):
    n = len(arrays)
    npeer = N_DEV - 1

    def body(*refs):
        ins, outs = refs[:n], refs[n:2 * n]
        send_sems, recv_sems, local_sems = refs[2 * n:]
        x, y, c = lax.axis_index("x"), lax.axis_index("y"), lax.axis_index("c")
        me = 4 * x + 2 * y + c
        copies, locals_ = [], []
        for a in range(n):
            gather = kinds[a] == "gather"
            own = pltpu.make_async_copy(ins[a] if gather else ins[a].at[me], outs[a].at[me], local_sems.at[a])
            own.start()
            locals_.append(own)
            for d in range(1, N_DEV):
                px = 1 - x if d & 4 else x
                py = 1 - y if d & 2 else y
                pc = 1 - c if d & 1 else c
                src = ins[a] if gather else ins[a].at[4 * px + 2 * py + pc]
                cp = pltpu.make_async_remote_copy(
                    src_ref=src, dst_ref=outs[a].at[me],
                    send_sem=send_sems.at[a * npeer + d - 1], recv_sem=recv_sems.at[a * npeer + d - 1],
                    device_id=(px, py, pc), device_id_type=pl.DeviceIdType.MESH)
                cp.start()
                copies.append(cp)
        for cp in copies:
            cp.wait_recv()
        for cp in copies:
            cp.wait_send()
        for own in locals_:
            own.wait()

    out_shape = [jax.ShapeDtypeStruct((N_DEV,) + (a.shape if k == "gather" else a.shape[1:]), a.dtype)
                 for a, k in zip(arrays, kinds)]
    return _pcall(
        body, name=name,
        in_specs=[pl.BlockSpec(memory_space=pl.ANY)] * n,
        out_specs=[pl.BlockSpec(memory_space=pl.ANY)] * n,
        out_shape=out_shape,
        scratch_shapes=[pltpu.SemaphoreType.DMA((n * npeer,)), pltpu.SemaphoreType.DMA((n * npeer,)),
                        pltpu.SemaphoreType.DMA((n,))],
        compiler_params=pltpu.CompilerParams(has_side_effects=True),
    )(*arrays)


def _peer_copies(srcs, lands, kinds, send_sems, recv_sems):
    x, y, c = lax.axis_index("x"), lax.axis_index("y"), lax.axis_index("c")
    me = 4 * x + 2 * y + c
    copies = []
    for a in range(len(srcs)):
        for d in range(1, N_DEV):
            px = 1 - x if d & 4 else x
            py = 1 - y if d & 2 else y
            pc = 1 - c if d & 1 else c
            k = a * (N_DEV - 1) + d - 1
            copies.append(pltpu.make_async_remote_copy(
                src_ref=srcs[a] if kinds[a] == "gather" else srcs[a].at[4 * px + 2 * py + pc], dst_ref=lands[a].at[me],
                send_sem=send_sems.at[k], recv_sem=recv_sems.at[k],
                device_id=(px, py, pc), device_id_type=pl.DeviceIdType.MESH))
    return copies


def _exchange_start(arrays, kinds, name, after=None):
    n = len(arrays)
    nsem = n * (N_DEV - 1)
    hbm = pl.BlockSpec(memory_space=pltpu.HBM)
    sem = pl.BlockSpec(memory_space=pltpu.SEMAPHORE)
    land_shapes = [(N_DEV,) + (a.shape if k == "gather" else a.shape[1:]) for a, k in zip(arrays, kinds)]

    n_in = 2 * n + int(after is not None)

    def body(*refs):
        srcs, lands = refs[:n], refs[n:2 * n]
        send_sems, recv_sems = refs[n_in], refs[n_in + 1]
        token = refs[-1]
        for cp in _peer_copies(srcs, lands, kinds, send_sems, recv_sems):
            cp.start()
        token[...] = jnp.zeros_like(token)

    operands = [pltpu.with_memory_space_constraint(a, pltpu.HBM) for a in arrays]
    operands += [pltpu.with_memory_space_constraint(lax.empty(s, a.dtype), pltpu.HBM) for s, a in zip(land_shapes, arrays)]
    operands += [] if after is None else [after]
    out = _pcall(
        body, name=name,
        in_specs=[hbm] * (2 * n) + ([] if after is None else [pl.BlockSpec(memory_space=pl.ANY)]),
        out_specs=[sem, sem] + [hbm] * (2 * n) + [pl.BlockSpec(memory_space=pltpu.VMEM)],
        out_shape=[pltpu.SemaphoreType.DMA((nsem,)), pltpu.SemaphoreType.DMA((nsem,))]
        + [pltpu.HBM(a.shape, a.dtype) for a in arrays]
        + [pltpu.HBM(s, a.dtype) for s, a in zip(land_shapes, arrays)]
        + [jax.ShapeDtypeStruct((8, LANE), F32)],
        input_output_aliases={k: 2 + k for k in range(2 * n)},
        compiler_params=pltpu.CompilerParams(has_side_effects=pltpu.SideEffectType.DATAFLOW_SIDE_EFFECTING),
    )(*operands)
    return out[0], out[1], list(out[2:2 + n]), list(out[2 + n:2 + 2 * n]), out[-1]


def _exchange_wait(started, kinds, after, name):
    send_sems, recv_sems, srcs, lands, _ = started
    n = len(srcs)
    hbm = pl.BlockSpec(memory_space=pltpu.HBM)
    sem = pl.BlockSpec(memory_space=pltpu.SEMAPHORE)

    def body(*refs):
        src_refs, land_refs = refs[:n], refs[n:2 * n]
        copies = _peer_copies(src_refs, land_refs, kinds, refs[2 * n], refs[2 * n + 1])
        for cp in copies:
            cp.wait_send()
        for cp in copies:
            cp.wait_recv()

    out = _pcall(
        body, name=name,
        in_specs=[hbm] * (2 * n) + [sem, sem, pl.BlockSpec(memory_space=pl.ANY)],
        out_specs=[hbm] * (2 * n),
        out_shape=[pltpu.HBM(a.shape, a.dtype) for a in srcs + lands],
        input_output_aliases={k: k for k in range(2 * n)},
        compiler_params=pltpu.CompilerParams(has_side_effects=pltpu.SideEffectType.DATAFLOW_SIDE_EFFECTING),
    )(*srcs, *lands, send_sems, recv_sems, after)
    me = 4 * lax.axis_index("x") + 2 * lax.axis_index("y") + lax.axis_index("c")
    filled = []
    for src, land, kind in zip(out[:n], out[n:], kinds):
        own = src if kind == "gather" else lax.dynamic_index_in_dim(src, me, axis=0, keepdims=False)
        filled.append(lax.dynamic_update_slice(land, own[None], (me,) + (0,) * own.ndim))
    return filled


def _sum_slots(slots, name, rows_tile):
    nd, r, c = slots.shape

    def body(s_ref, o_ref):
        acc = s_ref[0].astype(F32)
        for p in range(1, nd):
            acc = acc + s_ref[p].astype(F32)
        o_ref[...] = acc

    return _pcall(
        body, name=name, grid=(r // rows_tile,),
        in_specs=[pl.BlockSpec((nd, rows_tile, c), lambda i: (0, i, 0))],
        out_specs=pl.BlockSpec((rows_tile, c), lambda i: (i, 0)),
        out_shape=jax.ShapeDtypeStruct((r, c), F32),
        compiler_params=_params("parallel"),
    )(slots)


def _sum_slots_small(slot_arrays, name):
    n = len(slot_arrays)

    def body(*refs):
        for s_ref, o_ref in zip(refs[:n], refs[n:]):
            acc = s_ref[0]
            for p in range(1, s_ref.shape[0]):
                acc = acc + s_ref[p]
            o_ref[...] = acc

    return _pcall(body, name=name, out_shape=[jax.ShapeDtypeStruct(a.shape[1:], F32) for a in slot_arrays])(*slot_arrays)


def _adamw_update(w_ref, g_ref, m_ref, v_ref, d_ref, nm_ref, nv_ref):
    gr = g_ref[...]
    nm = ADAM_B1 * m_ref[...] + (1.0 - ADAM_B1) * gr
    nv = ADAM_B2 * v_ref[...] + (1.0 - ADAM_B2) * (gr * gr)
    m_hat = nm / (1.0 - ADAM_B1 ** ADAM_STEP)
    v_hat = nv / (1.0 - ADAM_B2 ** ADAM_STEP)
    d_ref[...] = -ADAM_LR * (m_hat / (jnp.sqrt(v_hat) + ADAM_EPS) + ADAM_WD * w_ref[...])
    nm_ref[...] = nm
    nv_ref[...] = nv


def _adamw_small(ws, gs, ms, vs, name):
    n = len(ws)

    def body(*refs):
        ins, outs = refs[:4 * n], refs[4 * n:]
        for k in range(n):
            _adamw_update(ins[k], ins[n + k], ins[2 * n + k], ins[3 * n + k], outs[k], outs[n + k], outs[2 * n + k])

    shapes = [jax.ShapeDtypeStruct(w.shape, F32) for w in ws]
    out = _pcall(body, name=name, out_shape=shapes * 3)(*ws, *gs, *ms, *vs)
    return list(out[:n]), list(out[n:2 * n]), list(out[2 * n:])


def _adamw(w, g, m, v, name, rows_tile):
    r, c = w.shape
    body = lambda *refs: _adamw_update(*refs)
    spec = pl.BlockSpec((rows_tile, c), lambda i: (i, 0))
    shp = jax.ShapeDtypeStruct((r, c), F32)
    return _pcall(
        body, name=name, grid=(r // rows_tile,), in_specs=[spec] * 4, out_specs=[spec] * 3, out_shape=[shp] * 3,
        compiler_params=_params("parallel"),
    )(w, g, m, v)


F0 = 2 * RET_QK + 2 * RET_V


def _to_internal_rows(w_t):
    cols = w_t.shape[1]
    fox = w_t[F0:F0 + 3 * FOX_W].reshape(3, FOX_PAIRS, LANE, cols).transpose(1, 0, 2, 3).reshape(3 * FOX_W, cols)
    tail = jnp.zeros((IN_PAD - IN_WIDTH, cols), w_t.dtype)
    return jnp.concatenate([w_t[:F0], fox, w_t[F0 + 3 * FOX_W:], tail], axis=0)


def _from_internal_rows(g_t):
    cols = g_t.shape[1]
    fox = g_t[F0:F0 + 3 * FOX_W].reshape(FOX_PAIRS, 3, LANE, cols).transpose(1, 0, 2, 3).reshape(3 * FOX_W, cols)
    return jnp.concatenate([g_t[:F0], fox, g_t[F0 + 3 * FOX_W:F0 + 3 * FOX_W + FOX_HEADS]], axis=0)


def _local_step(x, target, meta, attn_g, fox_b, ret_g, ffn_g, conv_w8, conv_b, final_g,
                first_weight, late_weights, ffn_grads_ready, out_grad_ready, in_grad_ready):
    seq, d = x.shape
    t = seq + PREFIX
    tm = TOK_TILE
    nq = t // tm
    fox_b128 = jnp.pad(fox_b, ((0, 0), (0, LANE - FOX_HEADS)))

    h0, n1 = _prep_norm(x, meta, attn_g, "prep_norm")
    w_in_t = first_weight(n1)
    proj = _mm_simple(n1, w_in_t, mode="nt", tm=tm, tn=IN_PAD, tk=d, out_dtype=F32, name="mm_in")
    cos, sin = _rope_tables(t)
    o_pre, mixed, states = _ret_fwd(proj, cos, sin, ret_g, "ret_fwd")
    c = _forget_cumsum(proj, fox_b128, "forget_cumsum")
    qa, ka, va = _fox_prep(proj, c, "fox_prep")
    by_block = lambda a: a.reshape(FOX_HEADS, nq, tm, LANE)
    mixed, o_fox, lse = _fox_fwd(by_block(qa), by_block(ka), by_block(va), mixed, "fox_fwd")
    w_out, w_up_t, w_down = late_weights(o_fox)
    h1 = _mm_simple(mixed, w_out, mode="nn", tm=tm, tn=d, tk=d, out_dtype=F32, name="mm_out", add=h0)
    n2 = _rmsnorm(h1, ffn_g, "ffn_norm")
    nf = D_FF // 1408
    up = _matmul(
        n2, w_up_t, mode="nt", grid=(2 * nf, nq, 1),
        a_spec=pl.BlockSpec((tm, d), lambda j, i, k: (i, 0)),
        b_spec=pl.BlockSpec((None, 1408, d), lambda j, i, k: (j // nf, j % nf, 0)),
        o_spec=pl.BlockSpec((None, tm, 1408), lambda j, i, k: (j // nf, i, j % nf)),
        out_shape=jax.ShapeDtypeStruct((2, t, D_FF), F32), name="mm_up")
    g = _conv_gate_fwd(up, conv_w8, conv_b, "conv_gate_fwd")
    h2 = _mm_simple(g, w_down, mode="nn", tm=tm, tn=d, tk=D_FF, out_dtype=F32, name="mm_down", add=h1)

    loss_tile, dh2, g_final = _loss_bwd(h2, target, final_g, "loss_bwd")
    tkw = 1408 if t % 1408 == 0 else tm
    dg = _mm_simple(dh2, w_down, mode="nt", tm=tm, tn=D_FF, tk=d, out_dtype=F32, name="mm_dg")
    gw_down = _mm_simple(g, dh2, mode="tn", tm=1408, tn=d, tk=tkw, out_dtype=BF16, name="mm_gw_down")
    dup, g_conv_w8, g_conv_b = _conv_gate_bwd(up, conv_w8, conv_b, dg, "conv_gate_bwd")
    half = lambda p: pl.BlockSpec((None, tm, D_FF), lambda i, j, k: (p, i, 0))
    half_w = lambda p: pl.BlockSpec((None, D_FF, d), lambda i, j, k: (p, 0, 0), pipeline_mode=pl.Buffered(1))
    dn2 = _matmul(
        [dup, dup], [w_up_t, w_up_t], mode="nn", grid=(nq, 1, 1),
        a_spec=[half(0), half(1)], b_spec=[half_w(0), half_w(1)],
        o_spec=pl.BlockSpec((tm, d), lambda i, j, k: (i, 0)),
        out_shape=jax.ShapeDtypeStruct((t, d), F32), name="mm_dn2")
    gw_up_t = _matmul(
        dup, n2, mode="tn", grid=(2 * nf, 1, t // tkw),
        a_spec=pl.BlockSpec((None, tkw, 1408), lambda i, j, k: (i // nf, k, i % nf)),
        b_spec=pl.BlockSpec((tkw, d), lambda i, j, k: (k, 0)),
        o_spec=pl.BlockSpec((1408, d), lambda i, j, k: (i, 0)),
        out_shape=jax.ShapeDtypeStruct((2 * D_FF, d), BF16), name="mm_gw_up")
    dh1, g_ffn = _rmsnorm_bwd(dn2, h1, ffn_g + ffn_grads_ready(gw_down, gw_up_t), dh2, "ffn_norm_bwd")

    dmixed = _mm_simple(dh1, w_out, mode="nt", tm=tm, tn=d, tk=d, out_dtype=F32, name="mm_dmixed")
    gw_out = _mm_simple(mixed, dh1, mode="tn", tm=d, tn=d, tk=tkw, out_dtype=BF16, name="mm_gw_out")
    dproj, g_ret = _ret_bwd(proj, cos, sin, ret_g + out_grad_ready(gw_out), dmixed, o_pre, states, "ret_bwd")
    qab, doa = _fox_prep_bwd(dmixed, o_fox, lse, qa, "fox_prep_bwd")
    dproj, drs, dcs = _fox_bwd(by_block(qab), by_block(doa), by_block(ka), by_block(va), dproj, "fox_bwd")
    dproj, g_fox_b = _forget_cumsum_bwd(proj, fox_b128, drs, dcs, dproj, "forget_cumsum_bwd")
    gw_in_t = _mm_simple(dproj, n1, mode="tn", tm=640, tn=d, tk=tkw, out_dtype=BF16, name="mm_gw_in")
    sent = in_grad_ready(gw_in_t)
    dn1 = _mm_simple(dproj, w_in_t, mode="nn", tm=tm, tn=d, tk=IN_PAD, out_dtype=F32, name="mm_dn1", after=sent)
    dh0, g_attn = _rmsnorm_bwd(dn1, h0, attn_g, dh1, "attn_norm_bwd")

    grads = dict(meta=dh0[N_PAD:PREFIX], attn_g=g_attn, fox_b=g_fox_b, ret_g=g_ret,
                 ffn_g=g_ffn, conv_w=g_conv_w8, conv_b=g_conv_b, final_g=g_final)
    return loss_tile, dh0[PREFIX:], grads


def kernel(x, meta_tokens, attn_norm_g, w_in, fox_forget_b, ret_norm_g, w_out, ffn_norm_g, w_up, conv_w, conv_b, w_down, final_norm_g, loss_target, m_meta_tokens, m_attn_norm_g, m_w_in, m_fox_forget_b, m_ret_norm_g, m_w_out, m_ffn_norm_g, m_w_up, m_conv_w, m_conv_b, m_w_down, m_final_norm_g, v_meta_tokens, v_attn_norm_g, v_w_in, v_fox_forget_b, v_ret_norm_g, v_w_out, v_ffn_norm_g, v_w_up, v_conv_w, v_conv_b, v_w_down, v_final_norm_g):
    d = D_MODEL
    me = 4 * lax.axis_index("x") + 2 * lax.axis_index("y") + lax.axis_index("c")
    in_blk = IN_WIDTH // N_DEV
    in_blk_pad = 400
    up_blk = 2 * D_FF // N_DEV
    down_blk = D_FF // N_DEV
    cw_blk = D_FF // N_DEV

    w_in_loc = jnp.pad(w_in[0].T.astype(BF16), ((0, in_blk_pad - in_blk), (0, 0)))
    cw_loc = jnp.pad(conv_w[0], ((0, 5), (0, 384 - cw_blk)))
    g_meta, g_cw = _exchange([meta_tokens, cw_loc], ["gather"] * 2, "gather_small")
    first = _exchange_start([w_in_loc], ["gather"], "gather_in_start", after=g_meta)
    rest_loc = [(w_out[0] + first[-1][0:1, 0:1]).astype(BF16), w_up[0].T.astype(BF16), w_down[0].astype(BF16)]
    rest = _exchange_start(rest_loc, ["gather"] * 3, "gather_rest_start")
    meta_f = g_meta.transpose(1, 0, 2).reshape(N_META, d)
    conv_w8 = jnp.pad(g_cw[:, :3, :cw_blk].transpose(1, 0, 2).reshape(3, D_FF), ((0, 5), (0, 0)))
    pending = {}

    def first_weight(after):
        (g_in,) = _exchange_wait(first, ["gather"], after, "gather_in_wait")
        return _to_internal_rows(g_in[:, :in_blk].reshape(IN_WIDTH, d))

    def in_grad_ready(gw_in_t):
        blocks = _from_internal_rows(gw_in_t).reshape(N_DEV, in_blk, d)
        blocks = jnp.pad(blocks, ((0, 0), (0, in_blk_pad - in_blk), (0, 0)))
        pending["in"] = _exchange_start([blocks], ["scatter"], "grads_in_start")
        return pending["in"][-1][0:1, 0:1]

    def late_weights(after):
        g_out, g_up, g_down = _exchange_wait(rest, ["gather"] * 3, after, "gather_rest_wait")
        return g_out.reshape(d, d), g_up.reshape(2, D_FF, d), g_down.reshape(D_FF, d)

    def ffn_grads_ready(gw_down, gw_up_t):
        blocks = [gw_down.reshape(N_DEV, down_blk, d), gw_up_t.reshape(N_DEV, up_blk, d)]
        pending["ffn"] = _exchange_start(blocks, ["scatter"] * 2, "grads_ffn_start")
        return pending["ffn"][-1][0:1, 0:1]

    def out_grad_ready(gw_out):
        pending["out"] = _exchange_start([gw_out.reshape(N_DEV, d // N_DEV, d)], ["scatter"], "grads_out_start")
        return pending["out"][-1][0:1, 0:1]

    loss_tile, grad_x, gr = _local_step(
        x[0], loss_target[0], meta_f, attn_norm_g + rest[-1][0:1, 0:1], fox_forget_b, ret_norm_g, ffn_norm_g,
        conv_w8, conv_b, final_norm_g.reshape(1, d), first_weight, late_weights, ffn_grads_ready, out_grad_ready,
        in_grad_ready)

    small = [loss_tile, gr["attn_g"], gr["fox_b"], gr["ret_g"], gr["ffn_g"], gr["conv_b"], gr["final_g"],
             gr["meta"], gr["conv_w"]]
    r_small = _exchange(small, ["gather"] * len(small), "exchange_small")
    r_down, r_up = _exchange_wait(pending["ffn"], ["scatter"] * 2, r_small[0], "grads_ffn_wait")
    (r_out,) = _exchange_wait(pending["out"], ["scatter"], r_small[0], "grads_out_wait")
    g_w_out = _sum_slots(r_out, "sum_w_out", d // N_DEV)
    g_w_up = _sum_slots(r_up, "sum_w_up", up_blk).T
    g_w_down = _sum_slots(r_down, "sum_w_down", down_blk)
    (loss_all, g_attn, g_fox_b128, g_ret, g_ffn, g_conv_b, g_final, g_meta_full, g_cw_full) = _sum_slots_small(
        r_small, "sum_small")
    loss = loss_all[0, 0]
    g_fox_b = g_fox_b128[:, :FOX_HEADS]
    g_meta_loc = lax.dynamic_slice(g_meta_full, (0, me * (d // N_DEV)), (N_META, d // N_DEV))
    g_cw_loc = lax.dynamic_slice(g_cw_full, (0, me * cw_blk), (3, cw_blk))

    d_w_out, m_w_out_n, v_w_out_n = _adamw(w_out[0], g_w_out, m_w_out[0], v_w_out[0], "adamw_w_out", 128)
    d_w_up, m_w_up_n, v_w_up_n = _adamw(w_up[0], g_w_up, m_w_up[0], v_w_up[0], "adamw_w_up", 128)
    d_w_down, m_w_down_n, v_w_down_n = _adamw(w_down[0], g_w_down, m_w_down[0], v_w_down[0], "adamw_w_down", down_blk)
    (r_in,) = _exchange_wait(pending["in"], ["scatter"], d_w_up, "grads_in_wait")
    g_w_in = _sum_slots(r_in, "sum_w_in", in_blk_pad)[:in_blk].T
    d_w_in, m_w_in_n, v_w_in_n = _adamw(w_in[0], g_w_in, m_w_in[0], v_w_in[0], "adamw_w_in", 128)
    row = lambda a: a.reshape(1, d)
    sm_grads = [g_meta_loc, g_attn, g_fox_b, g_ret, g_ffn, g_cw_loc, g_conv_b, g_final]
    sm_w = [meta_tokens, attn_norm_g, fox_forget_b, ret_norm_g, ffn_norm_g, conv_w[0], conv_b, row(final_norm_g)]
    sm_m = [m_meta_tokens, m_attn_norm_g, m_fox_forget_b, m_ret_norm_g, m_ffn_norm_g, m_conv_w[0], m_conv_b,
            row(m_final_norm_g)]
    sm_v = [v_meta_tokens, v_attn_norm_g, v_fox_forget_b, v_ret_norm_g, v_ffn_norm_g, v_conv_w[0], v_conv_b,
            row(v_final_norm_g)]
    dl, ml, vl = [lst[:7] + [lst[7].reshape(d)] for lst in _adamw_small(sm_w, sm_grads, sm_m, sm_v, "adamw_small")]

    def by_weight(meta_, attn_, w_in_, fox_, ret_, w_out_, ffn_, w_up_, cw_, cb_, w_down_, final_):
        return (meta_, attn_, w_in_[None], fox_, ret_, w_out_[None], ffn_, w_up_[None], cw_[None], cb_, w_down_[None], final_)

    grads_out = by_weight(g_meta_loc, g_attn, g_w_in, g_fox_b, g_ret, g_w_out, g_ffn, g_w_up, g_cw_loc, g_conv_b,
                          g_w_down, g_final.reshape(d))
    delta_out = by_weight(dl[0], dl[1], d_w_in, dl[2], dl[3], d_w_out, dl[4], d_w_up, dl[5], dl[6], d_w_down, dl[7])
    m_out = by_weight(ml[0], ml[1], m_w_in_n, ml[2], ml[3], m_w_out_n, ml[4], m_w_up_n, ml[5], ml[6], m_w_down_n, ml[7])
    v_out = by_weight(vl[0], vl[1], v_w_in_n, vl[2], vl[3], v_w_out_n, vl[4], v_w_up_n, vl[5], vl[6], v_w_down_n, vl[7])
    return (loss, grad_x[None]) + grads_out + delta_out + m_out + v_out
```

```python
import numpy as np
import jax
import jax.numpy as jnp
from jax import lax
from jax.experimental import pallas as pl
from jax.experimental.pallas import tpu as pltpu

F32 = jnp.float32
BF16 = jnp.bfloat16

D_MODEL = 1024
N_META = 16
N_PAD = 112
PREFIX = 128
RET_HEADS = 4
RET_DK = 64
RET_DV = 128
FOX_HEADS = 8
FOX_DH = 64
D_FF = 2816
ROPE_BASE = 10000.0
EPS = 1e-6
NEG = -1e30
RET_QK = RET_HEADS * RET_DK
RET_V = RET_HEADS * RET_DV
FOX_W = FOX_HEADS * FOX_DH
IN_WIDTH = 2 * RET_QK + 2 * RET_V + 3 * FOX_W + FOX_HEADS
IN_PAD = 3200
FF_COL_BLOCK = (IN_WIDTH - FOX_HEADS) // 128
QK_SCALE = 0.125

ADAM_LR = 0.001
ADAM_B1 = 0.9
ADAM_B2 = 0.999
ADAM_EPS = 1e-08
ADAM_WD = 0.01
ADAM_STEP = 10

N_DEV = 8
LANE = 128
ROW_TILE = 128
TOK_TILE = 384

NN = (((1,), (0,)), ((), ()))
NT = (((1,), (1,)), ((), ()))
TN = (((0,), (0,)), ((), ()))


def _pcall(body, **kw):
    return pl.pallas_call(body, **kw)


def _params(*sem):
    return pltpu.CompilerParams(dimension_semantics=sem)


def _dot(a, b, dims=NN):
    return lax.dot_general(a, b, dims, preferred_element_type=F32)


def _sigmoid(x):
    return 0.5 * jnp.tanh(0.5 * x) + 0.5


def _matmul(a, b, *, mode, grid, a_spec, b_spec, o_spec, out_shape, name, add=None, add_spec=None, after=None):
    dims = {"nn": NN, "nt": NT, "tn": TN}[mode]
    nk = grid[2]
    has_add = add is not None
    a_list, b_list = (list(a), list(b)) if isinstance(a, (list, tuple)) else ([a], [b])
    a_specs, b_specs = (list(a_spec), list(b_spec)) if isinstance(a_spec, (list, tuple)) else ([a_spec], [b_spec])
    nt = len(a_list)
    n_in = 2 * nt + int(has_add) + int(after is not None)

    def body(*refs):
        a_refs, b_refs = refs[:nt], refs[nt:2 * nt]
        add_ref = refs[2 * nt] if has_add else None
        o_ref = refs[n_in]
        part = _dot(a_refs[0][...].astype(BF16), b_refs[0][...].astype(BF16), dims)
        for ar, br in zip(a_refs[1:], b_refs[1:]):
            part = part + _dot(ar[...].astype(BF16), br[...].astype(BF16), dims)

        def finish(acc):
            if has_add:
                acc = acc + add_ref[...]
            o_ref[...] = acc.astype(o_ref.dtype)

        if nk == 1:
            finish(part)
        else:
            acc_ref = refs[-1]
            k = pl.program_id(2)

            @pl.when(k == 0)
            def _():
                acc_ref[...] = part

            @pl.when(k > 0)
            def _():
                acc_ref[...] += part

            @pl.when(k == nk - 1)
            def _():
                finish(acc_ref[...])

    in_specs = a_specs + b_specs + ([add_spec] if has_add else [])
    args = tuple(a_list) + tuple(b_list) + ((add,) if has_add else ())
    if after is not None:
        in_specs, args = in_specs + [pl.BlockSpec(memory_space=pl.ANY)], args + (after,)
    scratch = [] if nk == 1 else [pltpu.VMEM(tuple(d for d in o_spec.block_shape if d is not None), F32)]
    return _pcall(
        body, name=name, grid=grid, in_specs=in_specs, out_specs=o_spec, out_shape=out_shape,
        scratch_shapes=scratch, compiler_params=_params("parallel", "parallel", "arbitrary"),
    )(*args)


def _mm_simple(a, b, *, mode, tm, tn, tk, out_dtype, name, add=None, after=None):
    if mode == "tn":
        K, M = a.shape
    else:
        M, K = a.shape
    N = b.shape[0] if mode == "nt" else b.shape[1]
    grid = (M // tm, N // tn, K // tk)
    resident = dict(pipeline_mode=pl.Buffered(1)) if (tn == N and tk == K) else {}
    a_spec = pl.BlockSpec((tk, tm), lambda i, j, k: (k, i)) if mode == "tn" else pl.BlockSpec((tm, tk), lambda i, j, k: (i, k))
    b_spec = (pl.BlockSpec((tn, tk), lambda i, j, k: (j, k), **resident) if mode == "nt"
              else pl.BlockSpec((tk, tn), lambda i, j, k: (k, j), **resident))
    o_spec = pl.BlockSpec((tm, tn), lambda i, j, k: (i, j))
    return _matmul(a, b, mode=mode, grid=grid, a_spec=a_spec, b_spec=b_spec, o_spec=o_spec,
                   out_shape=jax.ShapeDtypeStruct((M, N), out_dtype), name=name, add=add,
                   add_spec=o_spec if add is not None else None, after=after)


def _prep_norm(x, meta, gain, name):
    seq, d = x.shape
    t = seq + PREFIX

    def body(xa_ref, xb_ref, xc_ref, meta_ref, g_ref, h_ref, n_ref):
        i = pl.program_id(0)

        @pl.when(i == 0)
        def _():
            h_ref[0:N_PAD, :] = jnp.zeros((N_PAD, d), F32)
            h_ref[N_PAD:ROW_TILE, :] = meta_ref[...]

        @pl.when(i > 0)
        def _():
            h_ref[0:ROW_TILE, :] = xa_ref[...]

        h_ref[ROW_TILE:2 * ROW_TILE, :] = xb_ref[...]
        h_ref[2 * ROW_TILE:3 * ROW_TILE, :] = xc_ref[...]
        h = h_ref[...]
        r = lax.rsqrt(jnp.mean(h * h, axis=-1, keepdims=True) + EPS)
        n_ref[...] = (h * r * g_ref[...]).astype(BF16)

    return _pcall(
        body, name=name, grid=(t // TOK_TILE,),
        in_specs=_shifted_row_specs(d) + [pl.BlockSpec((N_META, d), lambda i: (0, 0)), pl.BlockSpec((1, d), lambda i: (0, 0))],
        out_specs=[pl.BlockSpec((TOK_TILE, d), lambda i: (i, 0)), pl.BlockSpec((TOK_TILE, d), lambda i: (i, 0))],
        out_shape=[jax.ShapeDtypeStruct((t, d), F32), jax.ShapeDtypeStruct((t, d), BF16)],
        compiler_params=_params("parallel"),
    )(x, x, x, meta, gain)


def _shifted_row_specs(d):
    blocks_per_tile = TOK_TILE // ROW_TILE
    return [pl.BlockSpec((ROW_TILE, d), lambda i, r=r: (jnp.maximum(blocks_per_tile * i + r, 0), 0)) for r in (-1, 0, 1)]


def _rmsnorm(h, gain, name):
    t, d = h.shape

    def body(h_ref, g_ref, n_ref):
        x = h_ref[...]
        r = lax.rsqrt(jnp.mean(x * x, axis=-1, keepdims=True) + EPS)
        n_ref[...] = (x * r * g_ref[...]).astype(BF16)

    return _pcall(
        body, name=name, grid=(t // TOK_TILE,),
        in_specs=[pl.BlockSpec((TOK_TILE, d), lambda i: (i, 0)), pl.BlockSpec((1, d), lambda i: (0, 0))],
        out_specs=pl.BlockSpec((TOK_TILE, d), lambda i: (i, 0)),
        out_shape=jax.ShapeDtypeStruct((t, d), BF16),
        compiler_params=_params("parallel"),
    )(h, gain)


def _rmsnorm_bwd(dn, h, gain, dres, name):
    t, d = h.shape

    def body(dn_ref, h_ref, g_ref, dres_ref, dh_ref, gg_ref):
        i = pl.program_id(0)
        x = h_ref[...]
        r = lax.rsqrt(jnp.mean(x * x, axis=-1, keepdims=True) + EPS)
        xhat = x * r
        dy = dn_ref[...]
        u = dy * g_ref[...]
        dh_ref[...] = dres_ref[...] + r * (u - xhat * jnp.mean(u * xhat, axis=-1, keepdims=True))
        part = jnp.sum(dy * xhat, axis=0, keepdims=True)

        @pl.when(i == 0)
        def _():
            gg_ref[...] = part

        @pl.when(i > 0)
        def _():
            gg_ref[...] += part

    return _pcall(
        body, name=name, grid=(t // TOK_TILE,),
        in_specs=[pl.BlockSpec((TOK_TILE, d), lambda i: (i, 0)), pl.BlockSpec((TOK_TILE, d), lambda i: (i, 0)),
                  pl.BlockSpec((1, d), lambda i: (0, 0)), pl.BlockSpec((TOK_TILE, d), lambda i: (i, 0))],
        out_specs=[pl.BlockSpec((TOK_TILE, d), lambda i: (i, 0)), pl.BlockSpec((1, d), lambda i: (0, 0))],
        out_shape=[jax.ShapeDtypeStruct((t, d), F32), jax.ShapeDtypeStruct((1, d), F32)],
        compiler_params=_params("arbitrary"),
    )(dn, h, gain, dres)


def _loss_bwd(h2, target, gain, name):
    t, d = h2.shape

    def body(h_ref, ta_ref, tb_ref, tc_ref, g_ref, loss_ref, dh_ref, gg_ref):
        i = pl.program_id(0)

        @pl.when(i == 0)
        def _():
            loss_ref[...] = jnp.zeros_like(loss_ref)
            gg_ref[...] = jnp.zeros_like(gg_ref)

        x = h_ref[...]
        r = lax.rsqrt(jnp.mean(x * x, axis=-1, keepdims=True) + EPS)
        xhat = x * r
        g = g_ref[...]
        tgt = jnp.concatenate([ta_ref[...], tb_ref[...], tc_ref[...]], axis=0)
        counted = (i * TOK_TILE + lax.broadcasted_iota(jnp.int32, (TOK_TILE, 1), 0)) >= PREFIX
        err = jnp.where(counted, xhat * g - tgt, 0.0)
        loss_ref[...] += 0.5 * jnp.sum(jnp.mean(err * err, axis=-1, keepdims=True))
        dy = err * (1.0 / d)
        u = dy * g
        dh_ref[...] = r * (u - xhat * jnp.mean(u * xhat, axis=-1, keepdims=True))
        gg_ref[...] += jnp.sum(dy * xhat, axis=0, keepdims=True)

    return _pcall(
        body, name=name, grid=(t // TOK_TILE,),
        in_specs=[pl.BlockSpec((TOK_TILE, d), lambda i: (i, 0))] + _shifted_row_specs(d) + [pl.BlockSpec((1, d), lambda i: (0, 0))],
        out_specs=[pl.BlockSpec((8, LANE), lambda i: (0, 0)), pl.BlockSpec((TOK_TILE, d), lambda i: (i, 0)),
                   pl.BlockSpec((1, d), lambda i: (0, 0))],
        out_shape=[jax.ShapeDtypeStruct((8, LANE), F32), jax.ShapeDtypeStruct((t, d), F32),
                   jax.ShapeDtypeStruct((1, d), F32)],
        compiler_params=_params("arbitrary"),
    )(h2, target, target, target, gain)


def _ret_consts(bk):
    gam = 1.0 - 2.0 ** (-5.0 - np.arange(RET_HEADS))
    n = np.arange(bk)
    same_or_earlier_chunk = (n[None, :] // 64) <= (n[:, None] // 64)
    w = gam[:, None, None] ** np.abs(n[:, None] - n[None, :])[None] * same_or_earlier_chunk[None]
    wq = gam[:, None] ** (n[None, :] + 1.0)
    wk = gam[:, None] ** (bk - 1.0 - n[None, :])
    mask = (np.arange(RET_QK)[None, :] // RET_DK) == np.arange(RET_HEADS)[:, None]
    return (jnp.asarray(w, F32), jnp.asarray(wq[:, :, None], F32), jnp.asarray(wk[:, :, None], F32),
            jnp.asarray(mask[:, None, :], F32), [float(g ** bk) for g in gam])


def _rope_tables(t):
    half = RET_DK // 2
    inv = 1.0 / (ROPE_BASE ** (jnp.arange(half, dtype=F32) / half))
    ang = jnp.arange(t).astype(F32)[:, None] * inv[None, :]
    cos, sin = jnp.cos(ang), jnp.sin(ang)
    return (jnp.tile(jnp.concatenate([cos, cos], axis=1), (1, RET_HEADS)),
            jnp.tile(jnp.concatenate([-sin, sin], axis=1), (1, RET_HEADS)))


def _swap_halves(x):
    outs = []
    for s in range(x.shape[1] // LANE):
        xs = x[:, LANE * s:LANE * (s + 1)]
        lane = lax.broadcasted_iota(jnp.int32, xs.shape, 1)
        outs.append(jnp.where((lane & 32) == 0, pltpu.roll(xs, LANE - 32, axis=1), pltpu.roll(xs, 32, axis=1)))
    return outs[0] if len(outs) == 1 else jnp.concatenate(outs, axis=1)


def _rope(x, cos, sin_signed):
    return x * cos + _swap_halves(x) * sin_signed


def _rope_t(dx, cos, sin_signed):
    return dx * cos + _swap_halves(dx * sin_signed)


def _ret_fwd(proj, cos, sin, gain, name):
    t = proj.shape[0]
    bk = TOK_TILE
    nb = t // bk
    w, wq, wk, mask, g_blk = _ret_consts(bk)

    def body(q_ref, k_ref, v_ref, rg_ref, cos_ref, sin_ref, w_ref, wq_ref, wk_ref, mask_ref, gain_ref,
             opre_ref, og_ref, st_ref, r_ref):
        i = pl.program_id(0)

        @pl.when(i == 0)
        def _():
            r_ref[...] = jnp.zeros_like(r_ref)

        c, s = cos_ref[...], sin_ref[...]
        valid = ((i * bk + lax.broadcasted_iota(jnp.int32, (bk, 1), 0)) >= N_PAD).astype(F32)
        qr = _rope(q_ref[...], c, s)
        kr = _rope(k_ref[...], c, s) * QK_SCALE * valid
        kb = kr.astype(BF16)
        for h in range(RET_HEADS):
            hm = mask_ref[h]
            cols = slice(RET_DV * h, RET_DV * (h + 1))
            vh = v_ref[:, cols].astype(BF16)
            r_prev = r_ref[h]
            st_ref[0, h] = r_prev
            sm = _dot((qr * hm).astype(BF16), kb, NT) * w_ref[h]
            o = _dot(sm.astype(BF16), vh) + _dot((qr * (hm * wq_ref[h])).astype(BF16), r_prev.astype(BF16))
            r_ref[h] = g_blk[h] * r_prev + _dot((kr * wk_ref[h]).astype(BF16), vh, TN)
            opre_ref[:, cols] = o
            rstd = lax.rsqrt(jnp.mean(o * o, axis=-1, keepdims=True) + EPS)
            rg = rg_ref[:, cols]
            og_ref[:, cols] = (o * rstd * gain_ref[:, cols] * (rg * _sigmoid(rg))).astype(BF16)

    full = lambda shape: pl.BlockSpec(shape, lambda i: (0,) * len(shape))
    return _pcall(
        body, name=name, grid=(nb,),
        in_specs=[pl.BlockSpec((bk, RET_QK), lambda i: (i, 0)), pl.BlockSpec((bk, RET_QK), lambda i: (i, 1)),
                  pl.BlockSpec((bk, RET_V), lambda i: (i, 1)), pl.BlockSpec((bk, RET_V), lambda i: (i, 2)),
                  pl.BlockSpec((bk, RET_QK), lambda i: (i, 0)), pl.BlockSpec((bk, RET_QK), lambda i: (i, 0)),
                  full((RET_HEADS, bk, bk)), full((RET_HEADS, bk, 1)), full((RET_HEADS, bk, 1)),
                  full((RET_HEADS, 1, RET_QK)), full((1, RET_V))],
        out_specs=[pl.BlockSpec((bk, RET_V), lambda i: (i, 0)), pl.BlockSpec((bk, RET_V), lambda i: (i, 0)),
                   pl.BlockSpec((1, RET_HEADS, RET_QK, RET_DV), lambda i: (i, 0, 0, 0))],
        out_shape=[jax.ShapeDtypeStruct((t, RET_V), F32), jax.ShapeDtypeStruct((t, RET_V + FOX_W), BF16),
                   jax.ShapeDtypeStruct((nb, RET_HEADS, RET_QK, RET_DV), F32)],
        scratch_shapes=[pltpu.VMEM((RET_HEADS, RET_QK, RET_DV), F32)],
        compiler_params=_params("arbitrary"),
    )(proj, proj, proj, proj, cos, sin, w, wq, wk, mask, gain)


def _ret_bwd(proj, cos, sin, gain, dmixed, opre, states, name):
    t = proj.shape[0]
    bk = TOK_TILE
    nb = t // bk
    w, wq, wk, mask, g_blk = _ret_consts(bk)
    v0, g0 = 2 * RET_QK, 2 * RET_QK + RET_V

    def body(q_ref, k_ref, v_ref, rg_ref, cos_ref, sin_ref, w_ref, wq_ref, wk_ref, mask_ref, gain_ref,
             dog_ref, opre_ref, st_ref, dp_ref, gg_ref, dr_ref):
        step = pl.program_id(0)
        i = nb - 1 - step

        @pl.when(step == 0)
        def _():
            dr_ref[...] = jnp.zeros_like(dr_ref)
            gg_ref[...] = jnp.zeros_like(gg_ref)

        c, s = cos_ref[...], sin_ref[...]
        valid = ((i * bk + lax.broadcasted_iota(jnp.int32, (bk, 1), 0)) >= N_PAD).astype(F32)
        qr = _rope(q_ref[...], c, s)
        kr = _rope(k_ref[...], c, s) * QK_SCALE * valid
        kb = kr.astype(BF16)
        dqr = jnp.zeros((bk, RET_QK), F32)
        dkr = jnp.zeros((bk, RET_QK), F32)
        for h in range(RET_HEADS):
            hm = mask_ref[h]
            cols = slice(RET_DV * h, RET_DV * (h + 1))
            vh = v_ref[:, cols].astype(BF16)
            o = opre_ref[:, cols]
            rstd = lax.rsqrt(jnp.mean(o * o, axis=-1, keepdims=True) + EPS)
            xhat = o * rstd
            rg = rg_ref[:, cols]
            sg = _sigmoid(rg)
            gate = rg * sg
            gn = gain_ref[:, cols]
            dog = dog_ref[:, cols]
            dp_ref[:, g0 + RET_DV * h:g0 + RET_DV * (h + 1)] = (
                dog * xhat * gn * (sg * (1.0 + rg * (1.0 - sg)))).astype(BF16)
            gg_ref[:, cols] += jnp.sum(dog * xhat * gate, axis=0, keepdims=True)
            dxh = dog * gn * gate
            do = (rstd * (dxh - xhat * jnp.mean(dxh * xhat, axis=-1, keepdims=True))).astype(BF16)
            qm = (qr * hm).astype(BF16)
            qw = (qr * (hm * wq_ref[h])).astype(BF16)
            kw = (kr * wk_ref[h]).astype(BF16)
            wh = w_ref[h]
            sm = (_dot(qm, kb, NT) * wh).astype(BF16)
            ds = (_dot(do, vh, NT) * wh).astype(BF16)
            dr = dr_ref[h]
            drb = dr.astype(BF16)
            dp_ref[:, v0 + RET_DV * h:v0 + RET_DV * (h + 1)] = (_dot(sm, do, TN) + _dot(kw, drb)).astype(BF16)
            dqr = dqr + _dot(ds, kb) * hm + _dot(do, st_ref[0, h].astype(BF16), NT) * (hm * wq_ref[h])
            dkr = dkr + _dot(ds, qm, TN) + _dot(vh, drb, NT) * wk_ref[h]
            dr_ref[h] = g_blk[h] * dr + _dot(qw, do, TN)
        dp_ref[:, 0:RET_QK] = _rope_t(dqr, c, s).astype(BF16)
        dp_ref[:, RET_QK:2 * RET_QK] = _rope_t(dkr * (QK_SCALE * valid), c, s).astype(BF16)

    full = lambda shape: pl.BlockSpec(shape, lambda i: (0,) * len(shape))
    rev = lambda col: (lambda i: (nb - 1 - i, col))
    return _pcall(
        body, name=name, grid=(nb,),
        in_specs=[pl.BlockSpec((bk, RET_QK), rev(0)), pl.BlockSpec((bk, RET_QK), rev(1)),
                  pl.BlockSpec((bk, RET_V), rev(1)), pl.BlockSpec((bk, RET_V), rev(2)),
                  pl.BlockSpec((bk, RET_QK), rev(0)), pl.BlockSpec((bk, RET_QK), rev(0)),
                  full((RET_HEADS, bk, bk)), full((RET_HEADS, bk, 1)), full((RET_HEADS, bk, 1)),
                  full((RET_HEADS, 1, RET_QK)), full((1, RET_V)),
                  pl.BlockSpec((bk, RET_V), rev(0)), pl.BlockSpec((bk, RET_V), rev(0)),
                  pl.BlockSpec((1, RET_HEADS, RET_QK, RET_DV), lambda i: (nb - 1 - i, 0, 0, 0))],
        out_specs=[pl.BlockSpec((bk, g0 + RET_V), rev(0)), pl.BlockSpec((1, RET_V), lambda i: (0, 0))],
        out_shape=[jax.ShapeDtypeStruct((t, IN_PAD), BF16), jax.ShapeDtypeStruct((1, RET_V), F32)],
        scratch_shapes=[pltpu.VMEM((RET_HEADS, RET_QK, RET_DV), F32)],
        compiler_params=_params("arbitrary"),
    )(proj, proj, proj, proj, cos, sin, w, wq, wk, mask, gain, dmixed, opre, states)


def _forget_cumsum(proj, bias, name):
    t = proj.shape[0]
    rt = TOK_TILE
    nb = t // rt
    tril = jnp.asarray(np.tril(np.ones((rt, rt))), F32)

    def body(z_ref, b_ref, tril_ref, c_ref, carry_ref):
        i = pl.program_id(0)

        @pl.when(i == 0)
        def _():
            carry_ref[...] = jnp.zeros_like(carry_ref)

        z = z_ref[...] + b_ref[...]
        logf = jnp.minimum(z, 0.0) - jnp.log(1.0 + jnp.exp(-jnp.abs(z)))
        c = lax.dot_general(tril_ref[...], logf, NN, precision=lax.Precision.HIGHEST,
                            preferred_element_type=F32) + carry_ref[...]
        c_ref[...] = c
        carry_ref[...] = c[rt - 1:rt, :]

    return _pcall(
        body, name=name, grid=(nb,),
        in_specs=[pl.BlockSpec((rt, LANE), lambda i: (i, FF_COL_BLOCK)), pl.BlockSpec((1, LANE), lambda i: (0, 0)),
                  pl.BlockSpec((rt, rt), lambda i: (0, 0))],
        out_specs=pl.BlockSpec((rt, LANE), lambda i: (i, 0)),
        out_shape=jax.ShapeDtypeStruct((t, LANE), F32),
        scratch_shapes=[pltpu.VMEM((1, LANE), F32)],
        compiler_params=_params("arbitrary"),
    )(proj, bias, tril)


def _forget_cumsum_bwd(proj, bias, drs, dcs, dproj, name):
    t = proj.shape[0]
    rt = TOK_TILE
    nb = t // rt
    triu = jnp.asarray(np.triu(np.ones((rt, rt))), F32)

    def body(z_ref, b_ref, triu_ref, drs_ref, dcs_ref, dproj_in, dz_ref, gb_ref, carry_ref):
        step = pl.program_id(0)

        @pl.when(step == 0)
        def _():
            carry_ref[...] = jnp.zeros_like(carry_ref)
            gb_ref[...] = jnp.zeros_like(gb_ref)

        dlogf = lax.dot_general(triu_ref[...], drs_ref[...] - dcs_ref[...], NN, precision=lax.Precision.HIGHEST,
                                preferred_element_type=F32) + carry_ref[...]
        carry_ref[...] = dlogf[0:1, :]
        z = z_ref[...] + b_ref[...]
        is_head = lax.broadcasted_iota(jnp.int32, (rt, LANE), 1) < FOX_HEADS
        dz = jnp.where(is_head, dlogf / (1.0 + jnp.exp(z)), 0.0)
        dz_ref[...] = dz.astype(BF16)
        gb_ref[...] += jnp.sum(dz, axis=0, keepdims=True)

    return _pcall(
        body, name=name, grid=(nb,),
        in_specs=[pl.BlockSpec((rt, LANE), lambda i: (nb - 1 - i, FF_COL_BLOCK)),
                  pl.BlockSpec((1, LANE), lambda i: (0, 0)),
                  pl.BlockSpec((rt, rt), lambda i: (0, 0)),
                  pl.BlockSpec((rt, LANE), lambda i: (nb - 1 - i, 0)),
                  pl.BlockSpec((rt, LANE), lambda i: (nb - 1 - i, 0)),
                  pl.BlockSpec(memory_space=pl.ANY)],
        out_specs=[pl.BlockSpec((rt, LANE), lambda i: (nb - 1 - i, FF_COL_BLOCK)),
                   pl.BlockSpec((1, LANE), lambda i: (0, 0))],
        out_shape=[jax.ShapeDtypeStruct(dproj.shape, BF16), jax.ShapeDtypeStruct((1, LANE), F32)],
        input_output_aliases={5: 0},
        scratch_shapes=[pltpu.VMEM((1, LANE), F32)],
        compiler_params=_params("arbitrary"),
    )(proj, bias, triu, drs, dcs, dproj)


FOX_PAIRS = FOX_HEADS // 2
L_ONE_Q = FOX_DH
L_ONE_K = FOX_DH + 3
L_LSE = FOX_DH + 4


def _split3(x):
    hi = x.astype(BF16).astype(F32)
    r = x - hi
    mid = r.astype(BF16).astype(F32)
    return hi, mid, r - mid


def _head_to_low(slab, e):
    return slab if e == 0 else pltpu.roll(slab, FOX_DH, axis=1)


def _pair(a, b, low):
    return jnp.where(low, a, pltpu.roll(b, FOX_DH, axis=1))


def _fox_prep(proj, c, name):
    t = proj.shape[0]
    tq = TOK_TILE

    def body(p_ref, c_ref, qa_ref, ka_ref, va_ref):
        i = pl.program_id(0)
        lane = lax.broadcasted_iota(jnp.int32, (tq, LANE), 1)
        low = lane < FOX_DH
        live = (i * tq + lax.broadcasted_iota(jnp.int32, (tq, 1), 0)) >= N_PAD
        q_tail = jnp.where(lane < L_ONE_Q + 3, 1.0, 0.0)
        k_ones = (lane >= L_ONE_K) & (lane < L_ONE_K + 4)
        v_tail = jnp.where(lane < FOX_DH + 2, 1.0, 0.0)
        for pair in range(FOX_PAIRS):
            base = 3 * LANE * pair
            for e in range(2):
                h = 2 * pair + e
                q = _head_to_low(p_ref[:, base:base + LANE], e)
                k = _head_to_low(p_ref[:, base + LANE:base + 2 * LANE], e)
                v = _head_to_low(p_ref[:, base + 2 * LANE:base + 3 * LANE], e)
                hi, mid, lo = _split3(jnp.where(live, -c_ref[:, h:h + 1], NEG))
                ka = jnp.where(low, k, jnp.where(k_ones, 1.0, 0.0))
                ka = jnp.where(lane == L_ONE_Q, hi, jnp.where(lane == L_ONE_Q + 1, mid, jnp.where(lane == L_ONE_Q + 2, lo, ka)))
                qa_ref[h] = jnp.where(low, q * QK_SCALE, q_tail).astype(BF16)
                ka_ref[h] = ka.astype(BF16)
                va_ref[h] = jnp.where(low, v, v_tail).astype(BF16)

    out = jax.ShapeDtypeStruct((FOX_HEADS, t, LANE), BF16)
    ospec = pl.BlockSpec((FOX_HEADS, tq, LANE), lambda i: (0, i, 0))
    return _pcall(
        body, name=name, grid=(t // tq,),
        in_specs=[pl.BlockSpec((tq, 3 * FOX_W), lambda i: (i, 1)), pl.BlockSpec((tq, LANE), lambda i: (i, 0))],
        out_specs=[ospec, ospec, ospec], out_shape=[out, out, out],
        compiler_params=_params("parallel"),
    )(proj, c)


STEP_PAIRS = 2
STEP_HEADS = 2 * STEP_PAIRS
FOX_GROUPS = FOX_PAIRS // STEP_PAIRS


def _blockdiag(a, b):
    z = jnp.zeros_like(a)
    return jnp.concatenate([jnp.concatenate([a, z], axis=1), jnp.concatenate([z, b], axis=1)], axis=0)


def _fox_fwd(qa, ka, va, mixed, name):
    nh, nq, tq, _ = qa.shape
    t = nq * tq

    def body(qa_ref, ka_ref, va_ref, mixed_in, mixed_ref, o_ref, lse_ref):
        i = pl.program_id(1)
        lane = lax.broadcasted_iota(jnp.int32, (tq, LANE), 1)
        causal = lax.broadcasted_iota(jnp.int32, (tq, tq), 1) <= lax.broadcasted_iota(jnp.int32, (tq, tq), 0)
        qps = [jnp.concatenate([qa_ref[2 * c], qa_ref[2 * c + 1]], axis=1) for c in range(STEP_PAIRS)]

        def logits(j):
            return [_dot(qps[c], _blockdiag(ka_ref[2 * c, j], ka_ref[2 * c + 1, j]), NT) for c in range(STEP_PAIRS)]

        def update(j, scores, carry, diagonal):
            new = []
            for c in range(STEP_PAIRS):
                ms, acc = carry[c]
                ps, ms_new, alphas = [], [], []
                for e in range(2):
                    s = scores[c][:, e * tq:(e + 1) * tq]
                    if diagonal:
                        s = jnp.where(causal, s, NEG)
                    m_new = jnp.maximum(ms[e], jnp.max(s, axis=-1, keepdims=True))
                    ps.append(jnp.exp(s - m_new).astype(BF16))
                    ms_new.append(m_new)
                    alphas.append(jnp.broadcast_to(jnp.exp(ms[e] - m_new), (tq, LANE)))
                pv = _dot(jnp.concatenate(ps, axis=1), _blockdiag(va_ref[2 * c, j], va_ref[2 * c + 1, j]))
                new.append((tuple(ms_new), jnp.concatenate(alphas, axis=1) * acc + pv))
            return tuple(new)

        m0 = jnp.full((tq, 1), NEG, F32)
        init = tuple(((m0, m0), jnp.zeros((tq, 2 * LANE), F32)) for _ in range(STEP_PAIRS))
        carry = lax.fori_loop(0, i, lambda j, cr: update(j, logits(j), cr, False), init)
        o_pairs = []
        lse = jnp.zeros((tq, LANE), F32)
        for c, (ms, acc) in enumerate(update(i, logits(i), carry, True)):
            outs = []
            for e in range(2):
                half = acc[:, e * LANE:(e + 1) * LANE]
                l = half[:, FOX_DH:FOX_DH + 1]
                outs.append(half / l)
                lse = jnp.where(lane == 2 * c + e, ms[e] + jnp.log(l), lse)
            o_pairs.append(_pair(outs[0], outs[1], lane < FOX_DH))
        o_all = jnp.concatenate(o_pairs, axis=1)
        mixed_ref[...] = o_all.astype(BF16)
        o_ref[...] = o_all
        lse_ref[...] = lse

    width = STEP_PAIRS * LANE
    whole = pl.BlockSpec((STEP_HEADS, nq, tq, LANE), lambda g, i: (g, 0, 0, 0), pipeline_mode=pl.Buffered(1))
    return _pcall(
        body, name=name, grid=(FOX_GROUPS, nq),
        in_specs=[pl.BlockSpec((STEP_HEADS, None, tq, LANE), lambda g, i: (g, i, 0, 0)), whole, whole,
                  pl.BlockSpec(memory_space=pl.ANY)],
        out_specs=[pl.BlockSpec((tq, width), lambda g, i: (i, RET_V // width + g)),
                   pl.BlockSpec((tq, width), lambda g, i: (i, g)),
                   pl.BlockSpec((None, tq, LANE), lambda g, i: (g, i, 0))],
        out_shape=[jax.ShapeDtypeStruct(mixed.shape, BF16), jax.ShapeDtypeStruct((t, FOX_W), F32),
                   jax.ShapeDtypeStruct((FOX_GROUPS, t, LANE), F32)],
        input_output_aliases={3: 0},
        compiler_params=_params("parallel", "parallel"),
    )(qa, ka, va, mixed)


def _fox_prep_bwd(dmixed, o_fox, lse, qa, name):
    t = dmixed.shape[0]
    tq = TOK_TILE

    def body(dm_ref, o_ref, lse_ref, qa_ref, qab_ref, doa_ref):
        i = pl.program_id(0)
        lane = lax.broadcasted_iota(jnp.int32, (tq, LANE), 1)
        low = lane < FOX_DH
        live = (i * tq + lax.broadcasted_iota(jnp.int32, (tq, 1), 0)) >= N_PAD
        for pair in range(FOX_PAIRS):
            cols = slice(LANE * pair, LANE * (pair + 1))
            d_slab = dm_ref[:, cols]
            prod = d_slab * o_ref[:, cols]
            for e in range(2):
                h = 2 * pair + e
                nd = -jnp.sum(jnp.where(low, _head_to_low(prod, e), 0.0), axis=-1, keepdims=True)
                nd_hi = nd.astype(BF16).astype(F32)
                doa = jnp.where(low, _head_to_low(d_slab, e), 0.0)
                doa = jnp.where(lane == FOX_DH, nd_hi, jnp.where(lane == FOX_DH + 1, nd - nd_hi, doa))
                doa_ref[h] = doa.astype(BF16)
                lse_h = lse_ref[h // STEP_HEADS][:, h % STEP_HEADS:h % STEP_HEADS + 1]
                hi, mid, lo = _split3(jnp.where(live, -lse_h, 0.0))
                qab = qa_ref[h].astype(F32)
                qab = jnp.where(lane == L_LSE, hi, jnp.where(lane == L_LSE + 1, mid, jnp.where(lane == L_LSE + 2, lo, qab)))
                qab_ref[h] = qab.astype(BF16)

    out = jax.ShapeDtypeStruct((FOX_HEADS, t, LANE), BF16)
    hspec = pl.BlockSpec((FOX_HEADS, tq, LANE), lambda i: (0, i, 0))
    return _pcall(
        body, name=name, grid=(t // tq,),
        in_specs=[pl.BlockSpec((tq, FOX_W), lambda i: (i, 1)), pl.BlockSpec((tq, FOX_W), lambda i: (i, 0)),
                  pl.BlockSpec((FOX_GROUPS, tq, LANE), lambda i: (0, i, 0)), hspec],
        out_specs=[hspec, hspec], out_shape=[out, out],
        compiler_params=_params("parallel"),
    )(dmixed, o_fox, lse, qa)


def _fox_bwd(qab, doa, ka, va, dproj, name):
    nh, nq, tq, _ = qab.shape
    t = nq * tq
    slab = 3 * LANE * STEP_PAIRS
    group0 = (2 * RET_QK + 2 * RET_V) // slab

    def body(qab_ref, doa_ref, ka_ref, va_ref, dproj_in, dp_ref, drs_ref, dcs_ref, dq_ref):
        g, j = pl.program_id(0), pl.program_id(1)

        @pl.when((g == 0) & (j == 0))
        def _():
            drs_ref[...] = jnp.zeros_like(drs_ref)
            dcs_ref[...] = jnp.zeros_like(dcs_ref)

        @pl.when(j == 0)
        def _():
            dq_ref[...] = jnp.zeros_like(dq_ref)

        lane = lax.broadcasted_iota(jnp.int32, (tq, LANE), 1)
        low = lane < FOX_DH
        key_le_query = lax.broadcasted_iota(jnp.int32, (tq, tq), 0) <= lax.broadcasted_iota(jnp.int32, (tq, tq), 1)

        def by_head(c, a, b, col):
            h = STEP_HEADS * g + 2 * c
            return jnp.where(lane == h, a[:, col:col + 1], jnp.where(lane == h + 1, b[:, col:col + 1], 0.0))

        kbs = [ka_ref[h] for h in range(STEP_HEADS)]
        vbs = [va_ref[h] for h in range(STEP_HEADS)]

        def step(i, carry, diagonal):
            qbs = [qab_ref[h, i] for h in range(STEP_HEADS)]
            dobs = [doa_ref[h, i] for h in range(STEP_HEADS)]
            st = [_dot(kbs[h], qbs[h], NT) for h in range(STEP_HEADS)]
            dpt = [_dot(vbs[h], dobs[h], NT) for h in range(STEP_HEADS)]
            new = []
            for h in range(STEP_HEADS):
                p = jnp.exp(st[h])
                if diagonal:
                    p = jnp.where(key_le_query, p, 0.0)
                ds = (p * dpt[h]).astype(BF16)
                dq_ref[h, i] += _dot(ds, kbs[h], TN)
                dk, dv = carry[h]
                new.append((dk + _dot(ds, qbs[h]), dv + _dot(p.astype(BF16), dobs[h])))
            return tuple(new)

        zero = jnp.zeros((tq, LANE), F32)
        carry = step(j, tuple((zero, zero) for _ in range(STEP_HEADS)), True)
        carry = lax.fori_loop(j + 1, nq, lambda i, cr: step(i, cr, False), carry)
        rows = pl.ds(pl.multiple_of(j * tq, tq), tq)
        for c in range(STEP_PAIRS):
            (dka, dva), (dkb, dvb) = carry[2 * c], carry[2 * c + 1]
            c0 = 3 * LANE * c
            dp_ref[rows, c0 + LANE:c0 + 2 * LANE] = _pair(dka, dkb, low).astype(BF16)
            dp_ref[rows, c0 + 2 * LANE:c0 + 3 * LANE] = _pair(dva, dvb, low).astype(BF16)
            dcs_ref[rows, :] += by_head(c, dka, dkb, L_ONE_Q)

        @pl.when(j == nq - 1)
        def _():
            for c in range(STEP_PAIRS):
                for blk in range(nq):
                    r = slice(blk * tq, (blk + 1) * tq)
                    a, b = dq_ref[2 * c, blk], dq_ref[2 * c + 1, blk]
                    dp_ref[r, 3 * LANE * c:3 * LANE * c + LANE] = (_pair(a, b, low) * QK_SCALE).astype(BF16)
                    drs_ref[r, :] += by_head(c, a, b, L_ONE_K)

    whole = pl.BlockSpec((STEP_HEADS, nq, tq, LANE), lambda g, j: (g, 0, 0, 0), pipeline_mode=pl.Buffered(1))
    blk = pl.BlockSpec((STEP_HEADS, None, tq, LANE), lambda g, j: (g, j, 0, 0))
    sums = pl.BlockSpec((t, LANE), lambda g, j: (0, 0), pipeline_mode=pl.Buffered(1))
    return _pcall(
        body, name=name, grid=(FOX_GROUPS, nq),
        in_specs=[whole, whole, blk, blk, pl.BlockSpec(memory_space=pl.ANY)],
        out_specs=[pl.BlockSpec((t, slab), lambda g, j: (0, group0 + g)), sums, sums],
        out_shape=[jax.ShapeDtypeStruct(dproj.shape, BF16), jax.ShapeDtypeStruct((t, LANE), F32),
                   jax.ShapeDtypeStruct((t, LANE), F32)],
        input_output_aliases={4: 0},
        scratch_shapes=[pltpu.VMEM((STEP_HEADS, nq, tq, LANE), F32)],
        compiler_params=_params("arbitrary", "arbitrary"),
    )(qab, doa, ka, va, dproj)


HALO = 8


def _rows_ext(ref, r0, rows, t, before, after):
    lo, hi = r0 - before, r0 + rows + after
    parts = []
    if lo < 0:
        parts.append(jnp.zeros((-lo, LANE), F32))
    parts.append(ref[max(lo, 0):min(hi, t), :].astype(F32))
    if hi > t:
        parts.append(jnp.zeros((hi - t, LANE), F32))
    return parts[0] if len(parts) == 1 else jnp.concatenate(parts, axis=0)


def _conv_taps(a_ext, r0_ext, cw_ref, cb_ref):
    n = a_ext.shape[0]
    if r0_ext < N_PAD:
        row = r0_ext + lax.broadcasted_iota(jnp.int32, (n, 1), 0)
        a_ext = jnp.where(row >= N_PAD, a_ext, 0.0)
    a1 = pltpu.roll(a_ext, 1, axis=0)
    a2 = pltpu.roll(a_ext, 2, axis=0)
    acc = cb_ref[...] + a2 * cw_ref[0:1, :] + a1 * cw_ref[1:2, :] + a_ext * cw_ref[2:3, :]
    return a_ext, a1, a2, acc


def _conv_gate_fwd(up, conv_w8, conv_b, name):
    _, t, f = up.shape
    rows = TOK_TILE

    def body(a_ref, b_ref, cw_ref, cb_ref, g_ref):
        for r0 in range(0, t, rows):
            a_ext = _rows_ext(a_ref, r0, rows, t, HALO, 0)
            _, _, _, acc = _conv_taps(a_ext, r0 - HALO, cw_ref, cb_ref)
            acc = acc[HALO:, :]
            g_ref[r0:r0 + rows, :] = (acc * _sigmoid(acc) * b_ref[r0:r0 + rows, :]).astype(BF16)

    return _pcall(
        body, name=name, grid=(f // LANE,),
        in_specs=[pl.BlockSpec((None, t, LANE), lambda j: (0, 0, j)), pl.BlockSpec((None, t, LANE), lambda j: (1, 0, j)),
                  pl.BlockSpec((8, LANE), lambda j: (0, j)), pl.BlockSpec((1, LANE), lambda j: (0, j))],
        out_specs=pl.BlockSpec((t, LANE), lambda j: (0, j)),
        out_shape=jax.ShapeDtypeStruct((t, f), BF16),
        compiler_params=_params("parallel"),
    )(up, up, conv_w8, conv_b)


def _conv_gate_bwd(up, conv_w8, conv_b, dg, name):
    _, t, f = up.shape
    rows = TOK_TILE

    def body(a_ref, b_ref, cw_ref, cb_ref, dg_ref, dup_ref, gcw_ref, gcb_ref):
        gw = [jnp.zeros((1, LANE), F32) for _ in range(3)]
        gb = jnp.zeros((1, LANE), F32)
        for r0 in range(0, t, rows):
            a_ext = _rows_ext(a_ref, r0, rows, t, HALO, HALO)
            b_ext = _rows_ext(b_ref, r0, rows, t, HALO, HALO)
            dg_ext = _rows_ext(dg_ref, r0, rows, t, HALO, HALO)
            a0, a1, a2, acc = _conv_taps(a_ext, r0 - HALO, cw_ref, cb_ref)
            sg = _sigmoid(acc)
            dacc = dg_ext * b_ext * (sg * (1.0 + acc * (1.0 - sg)))
            n = dacc.shape[0]
            da = (dacc * cw_ref[2:3, :] + pltpu.roll(dacc, n - 1, axis=0) * cw_ref[1:2, :]
                  + pltpu.roll(dacc, n - 2, axis=0) * cw_ref[0:1, :])
            core = slice(HALO, HALO + rows)
            da = da[core, :]
            if r0 < N_PAD:
                row = r0 + lax.broadcasted_iota(jnp.int32, (rows, 1), 0)
                da = jnp.where(row >= N_PAD, da, 0.0)
            dup_ref[0, r0:r0 + rows, :] = da.astype(BF16)
            dup_ref[1, r0:r0 + rows, :] = (dg_ext * acc * sg)[core, :].astype(BF16)
            dacc_c = dacc[core, :]
            gw[0] = gw[0] + jnp.sum(dacc_c * a2[core, :], axis=0, keepdims=True)
            gw[1] = gw[1] + jnp.sum(dacc_c * a1[core, :], axis=0, keepdims=True)
            gw[2] = gw[2] + jnp.sum(dacc_c * a0[core, :], axis=0, keepdims=True)
            gb = gb + jnp.sum(dacc_c, axis=0, keepdims=True)
        gcw_ref[...] = jnp.zeros((8, LANE), F32)
        for tap in range(3):
            gcw_ref[tap:tap + 1, :] = gw[tap]
        gcb_ref[...] = gb

    return _pcall(
        body, name=name, grid=(f // LANE,),
        in_specs=[pl.BlockSpec((None, t, LANE), lambda j: (0, 0, j)), pl.BlockSpec((None, t, LANE), lambda j: (1, 0, j)),
                  pl.BlockSpec((8, LANE), lambda j: (0, j)), pl.BlockSpec((1, LANE), lambda j: (0, j)),
                  pl.BlockSpec((t, LANE), lambda j: (0, j))],
        out_specs=[pl.BlockSpec((2, t, LANE), lambda j: (0, 0, j)), pl.BlockSpec((8, LANE), lambda j: (0, j)),
                   pl.BlockSpec((1, LANE), lambda j: (0, j))],
        out_shape=[jax.ShapeDtypeStruct((2, t, f), BF16), jax.ShapeDtypeStruct((8, f), F32),
                   jax.ShapeDtypeStruct((1, f), F32)],
        compiler_params=_params("parallel"),
    )(up, up, conv_w8, conv_b, dg)


def _exchange(arrays, kinds, name):
    n = len(arrays)
    npeer = N_DEV - 1

    def body(*refs):
        ins, outs = refs[:n], refs[n:2 * n]
        send_sems, recv_sems, local_sems = refs[2 * n:]
        x, y, c = lax.axis_index("x"), lax.axis_index("y"), lax.axis_index("c")
        me = 4 * x + 2 * y + c
        copies, locals_ = [], []
        for a in range(n):
            gather = kinds[a] == "gather"
            own = pltpu.make_async_copy(ins[a] if gather else ins[a].at[me], outs[a].at[me], local_sems.at[a])
            own.start()
            locals_.append(own)
            for d in range(1, N_DEV):
                px = 1 - x if d & 4 else x
                py = 1 - y if d & 2 else y
                pc = 1 - c if d & 1 else c
                src = ins[a] if gather else ins[a].at[4 * px + 2 * py + pc]
                cp = pltpu.make_async_remote_copy(
                    src_ref=src, dst_ref=outs[a].at[me],
                    send_sem=send_sems.at[a * npeer + d - 1], recv_sem=recv_sems.at[a * npeer + d - 1],
                    device_id=(px, py, pc), device_id_type=pl.DeviceIdType.MESH)
                cp.start()
                copies.append(cp)
        for cp in copies:
            cp.wait_recv()
        for cp in copies:
            cp.wait_send()
        for own in locals_:
            own.wait()

    out_shape = [jax.ShapeDtypeStruct((N_DEV,) + (a.shape if k == "gather" else a.shape[1:]), a.dtype)
                 for a, k in zip(arrays, kinds)]
    return _pcall(
        body, name=name,
        in_specs=[pl.BlockSpec(memory_space=pl.ANY)] * n,
        out_specs=[pl.BlockSpec(memory_space=pl.ANY)] * n,
        out_shape=out_shape,
        scratch_shapes=[pltpu.SemaphoreType.DMA((n * npeer,)), pltpu.SemaphoreType.DMA((n * npeer,)),
                        pltpu.SemaphoreType.DMA((n,))],
        compiler_params=pltpu.CompilerParams(has_side_effects=True),
    )(*arrays)


def _peer_copies(srcs, lands, kinds, send_sems, recv_sems):
    x, y, c = lax.axis_index("x"), lax.axis_index("y"), lax.axis_index("c")
    me = 4 * x + 2 * y + c
    copies = []
    for a in range(len(srcs)):
        for d in range(1, N_DEV):
            px = 1 - x if d & 4 else x
            py = 1 - y if d & 2 else y
            pc = 1 - c if d & 1 else c
            k = a * (N_DEV - 1) + d - 1
            copies.append(pltpu.make_async_remote_copy(
                src_ref=srcs[a] if kinds[a] == "gather" else srcs[a].at[4 * px + 2 * py + pc], dst_ref=lands[a].at[me],
                send_sem=send_sems.at[k], recv_sem=recv_sems.at[k],
                device_id=(px, py, pc), device_id_type=pl.DeviceIdType.MESH))
    return copies


def _exchange_start(arrays, kinds, name, after=None):
    n = len(arrays)
    nsem = n * (N_DEV - 1)
    hbm = pl.BlockSpec(memory_space=pltpu.HBM)
    sem = pl.BlockSpec(memory_space=pltpu.SEMAPHORE)
    land_shapes = [(N_DEV,) + (a.shape if k == "gather" else a.shape[1:]) for a, k in zip(arrays, kinds)]

    n_in = 2 * n + int(after is not None)

    def body(*refs):
        srcs, lands = refs[:n], refs[n:2 * n]
        send_sems, recv_sems = refs[n_in], refs[n_in + 1]
        token = refs[-1]
        for cp in _peer_copies(srcs, lands, kinds, send_sems, recv_sems):
            cp.start()
        token[...] = jnp.zeros_like(token)

    operands = [pltpu.with_memory_space_constraint(a, pltpu.HBM) for a in arrays]
    operands += [pltpu.with_memory_space_constraint(lax.empty(s, a.dtype), pltpu.HBM) for s, a in zip(land_shapes, arrays)]
    operands += [] if after is None else [after]
    out = _pcall(
        body, name=name,
        in_specs=[hbm] * (2 * n) + ([] if after is None else [pl.BlockSpec(memory_space=pl.ANY)]),
        out_specs=[sem, sem] + [hbm] * (2 * n) + [pl.BlockSpec(memory_space=pltpu.VMEM)],
        out_shape=[pltpu.SemaphoreType.DMA((nsem,)), pltpu.SemaphoreType.DMA((nsem,))]
        + [pltpu.HBM(a.shape, a.dtype) for a in arrays]
        + [pltpu.HBM(s, a.dtype) for s, a in zip(land_shapes, arrays)]
        + [jax.ShapeDtypeStruct((8, LANE), F32)],
        input_output_aliases={k: 2 + k for k in range(2 * n)},
        compiler_params=pltpu.CompilerParams(has_side_effects=pltpu.SideEffectType.DATAFLOW_SIDE_EFFECTING),
    )(*operands)
    return out[0], out[1], list(out[2:2 + n]), list(out[2 + n:2 + 2 * n]), out[-1]


def _exchange_wait(started, kinds, after, name):
    send_sems, recv_sems, srcs, lands, _ = started
    n = len(srcs)
    hbm = pl.BlockSpec(memory_space=pltpu.HBM)
    sem = pl.BlockSpec(memory_space=pltpu.SEMAPHORE)

    def body(*refs):
        src_refs, land_refs = refs[:n], refs[n:2 * n]
        copies = _peer_copies(src_refs, land_refs, kinds, refs[2 * n], refs[2 * n + 1])
        for cp in copies:
            cp.wait_send()
        for cp in copies:
            cp.wait_recv()

    out = _pcall(
        body, name=name,
        in_specs=[hbm] * (2 * n) + [sem, sem, pl.BlockSpec(memory_space=pl.ANY)],
        out_specs=[hbm] * (2 * n),
        out_shape=[pltpu.HBM(a.shape, a.dtype) for a in srcs + lands],
        input_output_aliases={k: k for k in range(2 * n)},
        compiler_params=pltpu.CompilerParams(has_side_effects=pltpu.SideEffectType.DATAFLOW_SIDE_EFFECTING),
    )(*srcs, *lands, send_sems, recv_sems, after)
    me = 4 * lax.axis_index("x") + 2 * lax.axis_index("y") + lax.axis_index("c")
    filled = []
    for src, land, kind in zip(out[:n], out[n:], kinds):
        own = src if kind == "gather" else lax.dynamic_index_in_dim(src, me, axis=0, keepdims=False)
        filled.append(lax.dynamic_update_slice(land, own[None], (me,) + (0,) * own.ndim))
    return filled


def _sum_slots(slots, name, rows_tile):
    nd, r, c = slots.shape

    def body(s_ref, o_ref):
        acc = s_ref[0].astype(F32)
        for p in range(1, nd):
            acc = acc + s_ref[p].astype(F32)
        o_ref[...] = acc

    return _pcall(
        body, name=name, grid=(r // rows_tile,),
        in_specs=[pl.BlockSpec((nd, rows_tile, c), lambda i: (0, i, 0))],
        out_specs=pl.BlockSpec((rows_tile, c), lambda i: (i, 0)),
        out_shape=jax.ShapeDtypeStruct((r, c), F32),
        compiler_params=_params("parallel"),
    )(slots)


def _sum_slots_small(slot_arrays, name):
    n = len(slot_arrays)

    def body(*refs):
        for s_ref, o_ref in zip(refs[:n], refs[n:]):
            acc = s_ref[0]
            for p in range(1, s_ref.shape[0]):
                acc = acc + s_ref[p]
            o_ref[...] = acc

    return _pcall(body, name=name, out_shape=[jax.ShapeDtypeStruct(a.shape[1:], F32) for a in slot_arrays])(*slot_arrays)


def _adamw_update(w_ref, g_ref, m_ref, v_ref, d_ref, nm_ref, nv_ref):
    gr = g_ref[...]
    nm = ADAM_B1 * m_ref[...] + (1.0 - ADAM_B1) * gr
    nv = ADAM_B2 * v_ref[...] + (1.0 - ADAM_B2) * (gr * gr)
    m_hat = nm / (1.0 - ADAM_B1 ** ADAM_STEP)
    v_hat = nv / (1.0 - ADAM_B2 ** ADAM_STEP)
    d_ref[...] = -ADAM_LR * (m_hat / (jnp.sqrt(v_hat) + ADAM_EPS) + ADAM_WD * w_ref[...])
    nm_ref[...] = nm
    nv_ref[...] = nv


def _adamw_small(ws, gs, ms, vs, name):
    n = len(ws)

    def body(*refs):
        ins, outs = refs[:4 * n], refs[4 * n:]
        for k in range(n):
            _adamw_update(ins[k], ins[n + k], ins[2 * n + k], ins[3 * n + k], outs[k], outs[n + k], outs[2 * n + k])

    shapes = [jax.ShapeDtypeStruct(w.shape, F32) for w in ws]
    out = _pcall(body, name=name, out_shape=shapes * 3)(*ws, *gs, *ms, *vs)
    return list(out[:n]), list(out[n:2 * n]), list(out[2 * n:])


def _adamw(w, g, m, v, name, rows_tile):
    _, r, c = w.shape
    body = lambda *refs: _adamw_update(*refs)
    spec3 = pl.BlockSpec((None, rows_tile, c), lambda i: (0, i, 0))
    spec2 = pl.BlockSpec((rows_tile, c), lambda i: (i, 0))
    shp = jax.ShapeDtypeStruct((1, r, c), F32)
    return _pcall(
        body, name=name, grid=(r // rows_tile,), in_specs=[spec3, spec2, spec3, spec3], out_specs=[spec3] * 3,
        out_shape=[shp] * 3, compiler_params=_params("parallel"),
    )(w, g, m, v)


F0 = 2 * RET_QK + 2 * RET_V


def _to_internal_rows(w_t):
    cols = w_t.shape[1]
    fox = w_t[F0:F0 + 3 * FOX_W].reshape(3, FOX_PAIRS, LANE, cols).transpose(1, 0, 2, 3).reshape(3 * FOX_W, cols)
    tail = jnp.zeros((IN_PAD - IN_WIDTH, cols), w_t.dtype)
    return jnp.concatenate([w_t[:F0], fox, w_t[F0 + 3 * FOX_W:], tail], axis=0)


def _from_internal_rows(g_t):
    cols = g_t.shape[1]
    fox = g_t[F0:F0 + 3 * FOX_W].reshape(FOX_PAIRS, 3, LANE, cols).transpose(1, 0, 2, 3).reshape(3 * FOX_W, cols)
    return jnp.concatenate([g_t[:F0], fox, g_t[F0 + 3 * FOX_W:F0 + 3 * FOX_W + FOX_HEADS]], axis=0)


def _local_step(x, target, meta, attn_g, fox_b, ret_g, ffn_g, conv_w8, conv_b, final_g,
                first_weight, late_weights, ffn_grads_ready, out_grad_ready, in_grad_ready):
    seq, d = x.shape
    t = seq + PREFIX
    tm = TOK_TILE
    nq = t // tm
    fox_b128 = jnp.pad(fox_b, ((0, 0), (0, LANE - FOX_HEADS)))

    h0, n1 = _prep_norm(x, meta, attn_g, "prep_norm")
    w_in_t = first_weight(n1)
    proj = _mm_simple(n1, w_in_t, mode="nt", tm=tm, tn=IN_PAD, tk=d, out_dtype=F32, name="mm_in")
    cos, sin = _rope_tables(t)
    o_pre, mixed, states = _ret_fwd(proj, cos, sin, ret_g, "ret_fwd")
    c = _forget_cumsum(proj, fox_b128, "forget_cumsum")
    qa, ka, va = _fox_prep(proj, c, "fox_prep")
    by_block = lambda a: a.reshape(FOX_HEADS, nq, tm, LANE)
    mixed, o_fox, lse = _fox_fwd(by_block(qa), by_block(ka), by_block(va), mixed, "fox_fwd")
    w_out, w_up_t, w_down = late_weights(o_fox)
    h1 = _mm_simple(mixed, w_out, mode="nn", tm=tm, tn=d, tk=d, out_dtype=F32, name="mm_out", add=h0)
    n2 = _rmsnorm(h1, ffn_g, "ffn_norm")
    nf = D_FF // 1408
    up = _matmul(
        n2, w_up_t, mode="nt", grid=(2 * nf, nq, 1),
        a_spec=pl.BlockSpec((tm, d), lambda j, i, k: (i, 0)),
        b_spec=pl.BlockSpec((None, 1408, d), lambda j, i, k: (j // nf, j % nf, 0)),
        o_spec=pl.BlockSpec((None, tm, 1408), lambda j, i, k: (j // nf, i, j % nf)),
        out_shape=jax.ShapeDtypeStruct((2, t, D_FF), F32), name="mm_up")
    g = _conv_gate_fwd(up, conv_w8, conv_b, "conv_gate_fwd")
    h2 = _mm_simple(g, w_down, mode="nn", tm=tm, tn=d, tk=D_FF, out_dtype=F32, name="mm_down", add=h1)

    loss_tile, dh2, g_final = _loss_bwd(h2, target, final_g, "loss_bwd")
    tkw = 1408 if t % 1408 == 0 else tm
    dg = _mm_simple(dh2, w_down, mode="nt", tm=tm, tn=D_FF, tk=d, out_dtype=F32, name="mm_dg")
    gw_down = _mm_simple(g, dh2, mode="tn", tm=1408, tn=d, tk=tkw, out_dtype=BF16, name="mm_gw_down")
    dup, g_conv_w8, g_conv_b = _conv_gate_bwd(up, conv_w8, conv_b, dg, "conv_gate_bwd")
    half = lambda p: pl.BlockSpec((None, tm, D_FF), lambda i, j, k: (p, i, 0))
    half_w = lambda p: pl.BlockSpec((None, D_FF, d), lambda i, j, k: (p, 0, 0), pipeline_mode=pl.Buffered(1))
    dn2 = _matmul(
        [dup, dup], [w_up_t, w_up_t], mode="nn", grid=(nq, 1, 1),
        a_spec=[half(0), half(1)], b_spec=[half_w(0), half_w(1)],
        o_spec=pl.BlockSpec((tm, d), lambda i, j, k: (i, 0)),
        out_shape=jax.ShapeDtypeStruct((t, d), F32), name="mm_dn2")
    gw_up_t = _matmul(
        dup, n2, mode="tn", grid=(2 * nf, 1, t // tkw),
        a_spec=pl.BlockSpec((None, tkw, 1408), lambda i, j, k: (i // nf, k, i % nf)),
        b_spec=pl.BlockSpec((tkw, d), lambda i, j, k: (k, 0)),
        o_spec=pl.BlockSpec((1408, d), lambda i, j, k: (i, 0)),
        out_shape=jax.ShapeDtypeStruct((2 * D_FF, d), BF16), name="mm_gw_up")
    dh1, g_ffn = _rmsnorm_bwd(dn2, h1, ffn_g + ffn_grads_ready(gw_down, gw_up_t), dh2, "ffn_norm_bwd")

    dmixed = _mm_simple(dh1, w_out, mode="nt", tm=tm, tn=d, tk=d, out_dtype=F32, name="mm_dmixed")
    gw_out = _mm_simple(mixed, dh1, mode="tn", tm=d, tn=d, tk=tkw, out_dtype=BF16, name="mm_gw_out")
    dproj, g_ret = _ret_bwd(proj, cos, sin, ret_g + out_grad_ready(gw_out), dmixed, o_pre, states, "ret_bwd")
    qab, doa = _fox_prep_bwd(dmixed, o_fox, lse, qa, "fox_prep_bwd")
    dproj, drs, dcs = _fox_bwd(by_block(qab), by_block(doa), by_block(ka), by_block(va), dproj, "fox_bwd")
    dproj, g_fox_b = _forget_cumsum_bwd(proj, fox_b128, drs, dcs, dproj, "forget_cumsum_bwd")
    gw_in_t = _mm_simple(dproj, n1, mode="tn", tm=640, tn=d, tk=tkw, out_dtype=BF16, name="mm_gw_in")
    sent = in_grad_ready(gw_in_t)
    dn1 = _mm_simple(dproj, w_in_t, mode="nn", tm=tm, tn=d, tk=IN_PAD, out_dtype=F32, name="mm_dn1", after=sent)
    dh0, g_attn = _rmsnorm_bwd(dn1, h0, attn_g, dh1, "attn_norm_bwd")

    grads = dict(meta=dh0[N_PAD:PREFIX], attn_g=g_attn, fox_b=g_fox_b, ret_g=g_ret,
                 ffn_g=g_ffn, conv_w=g_conv_w8, conv_b=g_conv_b, final_g=g_final)
    return loss_tile, dh0[PREFIX:], grads


def kernel(x, meta_tokens, attn_norm_g, w_in, fox_forget_b, ret_norm_g, w_out, ffn_norm_g, w_up, conv_w, conv_b, w_down, final_norm_g, loss_target, m_meta_tokens, m_attn_norm_g, m_w_in, m_fox_forget_b, m_ret_norm_g, m_w_out, m_ffn_norm_g, m_w_up, m_conv_w, m_conv_b, m_w_down, m_final_norm_g, v_meta_tokens, v_attn_norm_g, v_w_in, v_fox_forget_b, v_ret_norm_g, v_w_out, v_ffn_norm_g, v_w_up, v_conv_w, v_conv_b, v_w_down, v_final_norm_g):
    d = D_MODEL
    me = 4 * lax.axis_index("x") + 2 * lax.axis_index("y") + lax.axis_index("c")
    in_blk = IN_WIDTH // N_DEV
    in_blk_pad = 400
    up_blk = 2 * D_FF // N_DEV
    down_blk = D_FF // N_DEV
    cw_blk = D_FF // N_DEV

    w_in_loc = jnp.pad(w_in[0].T.astype(BF16), ((0, in_blk_pad - in_blk), (0, 0)))
    cw_loc = jnp.pad(conv_w[0], ((0, 5), (0, 384 - cw_blk)))
    g_meta, g_cw = _exchange([meta_tokens, cw_loc], ["gather"] * 2, "gather_small")
    first = _exchange_start([w_in_loc], ["gather"], "gather_in_start", after=g_meta)
    rest_loc = [(w_out[0] + first[-1][0:1, 0:1]).astype(BF16), w_up[0].T.astype(BF16), w_down[0].astype(BF16)]
    rest = _exchange_start(rest_loc, ["gather"] * 3, "gather_rest_start")
    meta_f = g_meta.transpose(1, 0, 2).reshape(N_META, d)
    conv_w8 = jnp.pad(g_cw[:, :3, :cw_blk].transpose(1, 0, 2).reshape(3, D_FF), ((0, 5), (0, 0)))
    pending = {}

    def first_weight(after):
        (g_in,) = _exchange_wait(first, ["gather"], after, "gather_in_wait")
        return _to_internal_rows(g_in[:, :in_blk].reshape(IN_WIDTH, d))

    def in_grad_ready(gw_in_t):
        blocks = _from_internal_rows(gw_in_t).reshape(N_DEV, in_blk, d)
        blocks = jnp.pad(blocks, ((0, 0), (0, in_blk_pad - in_blk), (0, 0)))
        pending["in"] = _exchange_start([blocks], ["scatter"], "grads_in_start")
        return pending["in"][-1][0:1, 0:1]

    def late_weights(after):
        g_out, g_up, g_down = _exchange_wait(rest, ["gather"] * 3, after, "gather_rest_wait")
        return g_out.reshape(d, d), g_up.reshape(2, D_FF, d), g_down.reshape(D_FF, d)

    def ffn_grads_ready(gw_down, gw_up_t):
        blocks = [gw_down.reshape(N_DEV, down_blk, d), gw_up_t.reshape(N_DEV, up_blk, d)]
        pending["ffn"] = _exchange_start(blocks, ["scatter"] * 2, "grads_ffn_start")
        return pending["ffn"][-1][0:1, 0:1]

    def out_grad_ready(gw_out):
        pending["out"] = _exchange_start([gw_out.reshape(N_DEV, d // N_DEV, d)], ["scatter"], "grads_out_start")
        return pending["out"][-1][0:1, 0:1]

    loss_tile, grad_x, gr = _local_step(
        x[0], loss_target[0], meta_f, attn_norm_g + rest[-1][0:1, 0:1], fox_forget_b, ret_norm_g, ffn_norm_g,
        conv_w8, conv_b, final_norm_g.reshape(1, d), first_weight, late_weights, ffn_grads_ready, out_grad_ready,
        in_grad_ready)

    small = [loss_tile, gr["attn_g"], gr["fox_b"], gr["ret_g"], gr["ffn_g"], gr["conv_b"], gr["final_g"],
             gr["meta"], gr["conv_w"]]
    r_small = _exchange(small, ["gather"] * len(small), "exchange_small")
    r_down, r_up = _exchange_wait(pending["ffn"], ["scatter"] * 2, r_small[0], "grads_ffn_wait")
    (r_out,) = _exchange_wait(pending["out"], ["scatter"], r_small[0], "grads_out_wait")
    g_w_out = _sum_slots(r_out, "sum_w_out", d // N_DEV)
    g_w_up = _sum_slots(r_up, "sum_w_up", up_blk).T
    g_w_down = _sum_slots(r_down, "sum_w_down", down_blk)
    (loss_all, g_attn, g_fox_b128, g_ret, g_ffn, g_conv_b, g_final, g_meta_full, g_cw_full) = _sum_slots_small(
        r_small, "sum_small")
    loss = loss_all[0, 0]
    g_fox_b = g_fox_b128[:, :FOX_HEADS]
    g_meta_loc = lax.dynamic_slice(g_meta_full, (0, me * (d // N_DEV)), (N_META, d // N_DEV))
    g_cw_loc = lax.dynamic_slice(g_cw_full, (0, me * cw_blk), (3, cw_blk))

    d_w_out, m_w_out_n, v_w_out_n = _adamw(w_out, g_w_out, m_w_out, v_w_out, "adamw_w_out", 128)
    d_w_up, m_w_up_n, v_w_up_n = _adamw(w_up, g_w_up, m_w_up, v_w_up, "adamw_w_up", 128)
    d_w_down, m_w_down_n, v_w_down_n = _adamw(w_down, g_w_down, m_w_down, v_w_down, "adamw_w_down", down_blk)
    (r_in,) = _exchange_wait(pending["in"], ["scatter"], d_w_up, "grads_in_wait")
    g_w_in = _sum_slots(r_in, "sum_w_in", in_blk_pad)[:in_blk].T
    d_w_in, m_w_in_n, v_w_in_n = _adamw(w_in, g_w_in, m_w_in, v_w_in, "adamw_w_in", 128)
    row = lambda a: a.reshape(1, d)
    sm_grads = [g_meta_loc, g_attn, g_fox_b, g_ret, g_ffn, g_cw_loc, g_conv_b, g_final]
    sm_w = [meta_tokens, attn_norm_g, fox_forget_b, ret_norm_g, ffn_norm_g, conv_w[0], conv_b, row(final_norm_g)]
    sm_m = [m_meta_tokens, m_attn_norm_g, m_fox_forget_b, m_ret_norm_g, m_ffn_norm_g, m_conv_w[0], m_conv_b,
            row(m_final_norm_g)]
    sm_v = [v_meta_tokens, v_attn_norm_g, v_fox_forget_b, v_ret_norm_g, v_ffn_norm_g, v_conv_w[0], v_conv_b,
            row(v_final_norm_g)]
    dl, ml, vl = [lst[:7] + [lst[7].reshape(d)] for lst in _adamw_small(sm_w, sm_grads, sm_m, sm_v, "adamw_small")]

    def by_weight(meta_, attn_, w_in_, fox_, ret_, w_out_, ffn_, w_up_, cw_, cb_, w_down_, final_):
        return (meta_, attn_, w_in_, fox_, ret_, w_out_, ffn_, w_up_, cw_[None], cb_, w_down_, final_)

    grads_out = by_weight(g_meta_loc, g_attn, g_w_in[None], g_fox_b, g_ret, g_w_out[None], g_ffn, g_w_up[None], g_cw_loc,
                          g_conv_b, g_w_down[None], g_final.reshape(d))
    delta_out = by_weight(dl[0], dl[1], d_w_in, dl[2], dl[3], d_w_out, dl[4], d_w_up, dl[5], dl[6], d_w_down, dl[7])
    m_out = by_weight(ml[0], ml[1], m_w_in_n, ml[2], ml[3], m_w_out_n, ml[4], m_w_up_n, ml[5], ml[6], m_w_down_n, ml[7])
    v_out = by_weight(vl[0], vl[1], v_w_in_n, vl[2], vl[3], v_w_out_n, vl[4], v_w_up_n, vl[5], vl[6], v_w_down_n, vl[7])
    return (loss, grad_x[None]) + grads_out + delta_out + m_out + v_out
```

```python
import numpy as np
import jax
import jax.numpy as jnp
from jax import lax
from jax.experimental import pallas as pl
from jax.experimental.pallas import tpu as pltpu

F32 = jnp.float32
BF16 = jnp.bfloat16

D_MODEL = 1024
N_META = 16
N_PAD = 112
PREFIX = 128
RET_HEADS = 4
RET_DK = 64
RET_DV = 128
FOX_HEADS = 8
FOX_DH = 64
D_FF = 2816
ROPE_BASE = 10000.0
EPS = 1e-6
NEG = -1e30
RET_QK = RET_HEADS * RET_DK
RET_V = RET_HEADS * RET_DV
FOX_W = FOX_HEADS * FOX_DH
IN_WIDTH = 2 * RET_QK + 2 * RET_V + 3 * FOX_W + FOX_HEADS
IN_PAD = 3200
FF_COL_BLOCK = (IN_WIDTH - FOX_HEADS) // 128
QK_SCALE = 0.125

ADAM_LR = 0.001
ADAM_B1 = 0.9
ADAM_B2 = 0.999
ADAM_EPS = 1e-08
ADAM_WD = 0.01
ADAM_STEP = 10

N_DEV = 8
LANE = 128
ROW_TILE = 128
TOK_TILE = 384

NN = (((1,), (0,)), ((), ()))
NT = (((1,), (1,)), ((), ()))
TN = (((0,), (0,)), ((), ()))


def _pcall(body, **kw):
    return pl.pallas_call(body, **kw)


def _params(*sem):
    return pltpu.CompilerParams(dimension_semantics=sem)


def _dot(a, b, dims=NN):
    return lax.dot_general(a, b, dims, preferred_element_type=F32)


def _sigmoid(x):
    return 0.5 * jnp.tanh(0.5 * x) + 0.5


def _matmul(a, b, *, mode, grid, a_spec, b_spec, o_spec, out_shape, name, add=None, add_spec=None, after=None):
    dims = {"nn": NN, "nt": NT, "tn": TN}[mode]
    nk = grid[2]
    has_add = add is not None
    a_list, b_list = (list(a), list(b)) if isinstance(a, (list, tuple)) else ([a], [b])
    a_specs, b_specs = (list(a_spec), list(b_spec)) if isinstance(a_spec, (list, tuple)) else ([a_spec], [b_spec])
    nt = len(a_list)
    n_in = 2 * nt + int(has_add) + int(after is not None)

    def body(*refs):
        a_refs, b_refs = refs[:nt], refs[nt:2 * nt]
        add_ref = refs[2 * nt] if has_add else None
        o_ref = refs[n_in]
        part = _dot(a_refs[0][...].astype(BF16), b_refs[0][...].astype(BF16), dims)
        for ar, br in zip(a_refs[1:], b_refs[1:]):
            part = part + _dot(ar[...].astype(BF16), br[...].astype(BF16), dims)

        def finish(acc):
            if has_add:
                acc = acc + add_ref[...]
            o_ref[...] = acc.astype(o_ref.dtype)

        if nk == 1:
            finish(part)
        else:
            acc_ref = refs[-1]
            k = pl.program_id(2)

            @pl.when(k == 0)
            def _():
                acc_ref[...] = part

            @pl.when(k > 0)
            def _():
                acc_ref[...] += part

            @pl.when(k == nk - 1)
            def _():
                finish(acc_ref[...])

    in_specs = a_specs + b_specs + ([add_spec] if has_add else [])
    args = tuple(a_list) + tuple(b_list) + ((add,) if has_add else ())
    if after is not None:
        in_specs, args = in_specs + [pl.BlockSpec(memory_space=pl.ANY)], args + (after,)
    scratch = [] if nk == 1 else [pltpu.VMEM(tuple(d for d in o_spec.block_shape if d is not None), F32)]
    return _pcall(
        body, name=name, grid=grid, in_specs=in_specs, out_specs=o_spec, out_shape=out_shape,
        scratch_shapes=scratch, compiler_params=_params("parallel", "parallel", "arbitrary"),
    )(*args)


def _mm_simple(a, b, *, mode, tm, tn, tk, out_dtype, name, add=None, after=None):
    if mode == "tn":
        K, M = a.shape
    else:
        M, K = a.shape
    N = b.shape[0] if mode == "nt" else b.shape[1]
    grid = (M // tm, N // tn, K // tk)
    resident = dict(pipeline_mode=pl.Buffered(1)) if (tn == N and tk == K) else {}
    a_spec = pl.BlockSpec((tk, tm), lambda i, j, k: (k, i)) if mode == "tn" else pl.BlockSpec((tm, tk), lambda i, j, k: (i, k))
    b_spec = (pl.BlockSpec((tn, tk), lambda i, j, k: (j, k), **resident) if mode == "nt"
              else pl.BlockSpec((tk, tn), lambda i, j, k: (k, j), **resident))
    o_spec = pl.BlockSpec((tm, tn), lambda i, j, k: (i, j))
    return _matmul(a, b, mode=mode, grid=grid, a_spec=a_spec, b_spec=b_spec, o_spec=o_spec,
                   out_shape=jax.ShapeDtypeStruct((M, N), out_dtype), name=name, add=add,
                   add_spec=o_spec if add is not None else None, after=after)


def _prep_norm(x, meta, gain, name):
    seq, d = x.shape
    t = seq + PREFIX

    def body(xa_ref, xb_ref, xc_ref, meta_ref, g_ref, h_ref, n_ref):
        i = pl.program_id(0)

        @pl.when(i == 0)
        def _():
            h_ref[0:N_PAD, :] = jnp.zeros((N_PAD, d), F32)
            h_ref[N_PAD:ROW_TILE, :] = meta_ref[...]

        @pl.when(i > 0)
        def _():
            h_ref[0:ROW_TILE, :] = xa_ref[...]

        h_ref[ROW_TILE:2 * ROW_TILE, :] = xb_ref[...]
        h_ref[2 * ROW_TILE:3 * ROW_TILE, :] = xc_ref[...]
        h = h_ref[...]
        r = lax.rsqrt(jnp.mean(h * h, axis=-1, keepdims=True) + EPS)
        n_ref[...] = (h * r * g_ref[...]).astype(BF16)

    return _pcall(
        body, name=name, grid=(t // TOK_TILE,),
        in_specs=_shifted_row_specs(d) + [pl.BlockSpec((N_META, d), lambda i: (0, 0)), pl.BlockSpec((1, d), lambda i: (0, 0))],
        out_specs=[pl.BlockSpec((TOK_TILE, d), lambda i: (i, 0)), pl.BlockSpec((TOK_TILE, d), lambda i: (i, 0))],
        out_shape=[jax.ShapeDtypeStruct((t, d), F32), jax.ShapeDtypeStruct((t, d), BF16)],
        compiler_params=_params("parallel"),
    )(x, x, x, meta, gain)


def _shifted_row_specs(d):
    blocks_per_tile = TOK_TILE // ROW_TILE
    return [pl.BlockSpec((ROW_TILE, d), lambda i, r=r: (jnp.maximum(blocks_per_tile * i + r, 0), 0)) for r in (-1, 0, 1)]


def _rmsnorm(h, gain, name):
    t, d = h.shape

    def body(h_ref, g_ref, n_ref):
        x = h_ref[...]
        r = lax.rsqrt(jnp.mean(x * x, axis=-1, keepdims=True) + EPS)
        n_ref[...] = (x * r * g_ref[...]).astype(BF16)

    return _pcall(
        body, name=name, grid=(t // TOK_TILE,),
        in_specs=[pl.BlockSpec((TOK_TILE, d), lambda i: (i, 0)), pl.BlockSpec((1, d), lambda i: (0, 0))],
        out_specs=pl.BlockSpec((TOK_TILE, d), lambda i: (i, 0)),
        out_shape=jax.ShapeDtypeStruct((t, d), BF16),
        compiler_params=_params("parallel"),
    )(h, gain)


def _rmsnorm_bwd(dn, h, gain, dres, name):
    t, d = h.shape

    def body(dn_ref, h_ref, g_ref, dres_ref, dh_ref, gg_ref):
        i = pl.program_id(0)
        x = h_ref[...]
        r = lax.rsqrt(jnp.mean(x * x, axis=-1, keepdims=True) + EPS)
        xhat = x * r
        dy = dn_ref[...]
        u = dy * g_ref[...]
        dh_ref[...] = dres_ref[...] + r * (u - xhat * jnp.mean(u * xhat, axis=-1, keepdims=True))
        part = jnp.sum(dy * xhat, axis=0, keepdims=True)

        @pl.when(i == 0)
        def _():
            gg_ref[...] = part

        @pl.when(i > 0)
        def _():
            gg_ref[...] += part

    return _pcall(
        body, name=name, grid=(t // TOK_TILE,),
        in_specs=[pl.BlockSpec((TOK_TILE, d), lambda i: (i, 0)), pl.BlockSpec((TOK_TILE, d), lambda i: (i, 0)),
                  pl.BlockSpec((1, d), lambda i: (0, 0)), pl.BlockSpec((TOK_TILE, d), lambda i: (i, 0))],
        out_specs=[pl.BlockSpec((TOK_TILE, d), lambda i: (i, 0)), pl.BlockSpec((1, d), lambda i: (0, 0))],
        out_shape=[jax.ShapeDtypeStruct((t, d), F32), jax.ShapeDtypeStruct((1, d), F32)],
        compiler_params=_params("arbitrary"),
    )(dn, h, gain, dres)


def _loss_bwd(h2, target, gain, name):
    t, d = h2.shape

    def body(h_ref, ta_ref, tb_ref, tc_ref, g_ref, loss_ref, dh_ref, dhb_ref, gg_ref):
        i = pl.program_id(0)

        @pl.when(i == 0)
        def _():
            loss_ref[...] = jnp.zeros_like(loss_ref)
            gg_ref[...] = jnp.zeros_like(gg_ref)

        x = h_ref[...]
        r = lax.rsqrt(jnp.mean(x * x, axis=-1, keepdims=True) + EPS)
        xhat = x * r
        g = g_ref[...]
        tgt = jnp.concatenate([ta_ref[...], tb_ref[...], tc_ref[...]], axis=0)
        counted = (i * TOK_TILE + lax.broadcasted_iota(jnp.int32, (TOK_TILE, 1), 0)) >= PREFIX
        err = jnp.where(counted, xhat * g - tgt, 0.0)
        loss_ref[...] += 0.5 * jnp.sum(jnp.mean(err * err, axis=-1, keepdims=True))
        dy = err * (1.0 / d)
        u = dy * g
        dh = r * (u - xhat * jnp.mean(u * xhat, axis=-1, keepdims=True))
        dh_ref[...] = dh
        dhb_ref[...] = dh.astype(BF16)
        gg_ref[...] += jnp.sum(dy * xhat, axis=0, keepdims=True)

    tile = pl.BlockSpec((TOK_TILE, d), lambda i: (i, 0))
    return _pcall(
        body, name=name, grid=(t // TOK_TILE,),
        in_specs=[tile] + _shifted_row_specs(d) + [pl.BlockSpec((1, d), lambda i: (0, 0))],
        out_specs=[pl.BlockSpec((8, LANE), lambda i: (0, 0)), tile, tile, pl.BlockSpec((1, d), lambda i: (0, 0))],
        out_shape=[jax.ShapeDtypeStruct((8, LANE), F32), jax.ShapeDtypeStruct((t, d), F32),
                   jax.ShapeDtypeStruct((t, d), BF16), jax.ShapeDtypeStruct((1, d), F32)],
        compiler_params=_params("arbitrary"),
    )(h2, target, target, target, gain)


def _ret_consts(bk):
    gam = 1.0 - 2.0 ** (-5.0 - np.arange(RET_HEADS))
    n = np.arange(bk)
    same_or_earlier_chunk = (n[None, :] // 64) <= (n[:, None] // 64)
    w = gam[:, None, None] ** np.abs(n[:, None] - n[None, :])[None] * same_or_earlier_chunk[None]
    wq = gam[:, None] ** (n[None, :] + 1.0)
    wk = gam[:, None] ** (bk - 1.0 - n[None, :])
    mask = (np.arange(RET_QK)[None, :] // RET_DK) == np.arange(RET_HEADS)[:, None]
    return (jnp.asarray(w, F32), jnp.asarray(wq[:, :, None], F32), jnp.asarray(wk[:, :, None], F32),
            jnp.asarray(mask[:, None, :], F32), [float(g ** bk) for g in gam])


def _rope_tables(t):
    half = RET_DK // 2
    inv = 1.0 / (ROPE_BASE ** (jnp.arange(half, dtype=F32) / half))
    ang = jnp.arange(t).astype(F32)[:, None] * inv[None, :]
    cos, sin = jnp.cos(ang), jnp.sin(ang)
    return (jnp.tile(jnp.concatenate([cos, cos], axis=1), (1, RET_HEADS)),
            jnp.tile(jnp.concatenate([-sin, sin], axis=1), (1, RET_HEADS)))


def _swap_halves(x):
    outs = []
    for s in range(x.shape[1] // LANE):
        xs = x[:, LANE * s:LANE * (s + 1)]
        lane = lax.broadcasted_iota(jnp.int32, xs.shape, 1)
        outs.append(jnp.where((lane & 32) == 0, pltpu.roll(xs, LANE - 32, axis=1), pltpu.roll(xs, 32, axis=1)))
    return outs[0] if len(outs) == 1 else jnp.concatenate(outs, axis=1)


def _rope(x, cos, sin_signed):
    return x * cos + _swap_halves(x) * sin_signed


def _rope_t(dx, cos, sin_signed):
    return dx * cos + _swap_halves(dx * sin_signed)


def _ret_fwd(proj, cos, sin, gain, name):
    t = proj.shape[0]
    bk = TOK_TILE
    nb = t // bk
    w, wq, wk, mask, g_blk = _ret_consts(bk)

    def body(q_ref, k_ref, v_ref, rg_ref, cos_ref, sin_ref, w_ref, wq_ref, wk_ref, mask_ref, gain_ref,
             opre_ref, og_ref, st_ref, r_ref):
        i = pl.program_id(0)

        @pl.when(i == 0)
        def _():
            r_ref[...] = jnp.zeros_like(r_ref)

        c, s = cos_ref[...], sin_ref[...]
        valid = ((i * bk + lax.broadcasted_iota(jnp.int32, (bk, 1), 0)) >= N_PAD).astype(F32)
        qr = _rope(q_ref[...], c, s)
        kr = _rope(k_ref[...], c, s) * QK_SCALE * valid
        kb = kr.astype(BF16)
        for h in range(RET_HEADS):
            hm = mask_ref[h]
            cols = slice(RET_DV * h, RET_DV * (h + 1))
            vh = v_ref[:, cols].astype(BF16)
            r_prev = r_ref[h]
            st_ref[0, h] = r_prev
            sm = _dot((qr * hm).astype(BF16), kb, NT) * w_ref[h]
            o = _dot(sm.astype(BF16), vh) + _dot((qr * (hm * wq_ref[h])).astype(BF16), r_prev.astype(BF16))
            r_ref[h] = g_blk[h] * r_prev + _dot((kr * wk_ref[h]).astype(BF16), vh, TN)
            opre_ref[:, cols] = o
            rstd = lax.rsqrt(jnp.mean(o * o, axis=-1, keepdims=True) + EPS)
            rg = rg_ref[:, cols]
            og_ref[:, cols] = (o * rstd * gain_ref[:, cols] * (rg * _sigmoid(rg))).astype(BF16)

    full = lambda shape: pl.BlockSpec(shape, lambda i: (0,) * len(shape))
    return _pcall(
        body, name=name, grid=(nb,),
        in_specs=[pl.BlockSpec((bk, RET_QK), lambda i: (i, 0)), pl.BlockSpec((bk, RET_QK), lambda i: (i, 1)),
                  pl.BlockSpec((bk, RET_V), lambda i: (i, 1)), pl.BlockSpec((bk, RET_V), lambda i: (i, 2)),
                  pl.BlockSpec((bk, RET_QK), lambda i: (i, 0)), pl.BlockSpec((bk, RET_QK), lambda i: (i, 0)),
                  full((RET_HEADS, bk, bk)), full((RET_HEADS, bk, 1)), full((RET_HEADS, bk, 1)),
                  full((RET_HEADS, 1, RET_QK)), full((1, RET_V))],
        out_specs=[pl.BlockSpec((bk, RET_V), lambda i: (i, 0)), pl.BlockSpec((bk, RET_V), lambda i: (i, 0)),
                   pl.BlockSpec((1, RET_HEADS, RET_QK, RET_DV), lambda i: (i, 0, 0, 0))],
        out_shape=[jax.ShapeDtypeStruct((t, RET_V), F32), jax.ShapeDtypeStruct((t, RET_V + FOX_W), BF16),
                   jax.ShapeDtypeStruct((nb, RET_HEADS, RET_QK, RET_DV), F32)],
        scratch_shapes=[pltpu.VMEM((RET_HEADS, RET_QK, RET_DV), F32)],
        compiler_params=_params("arbitrary"),
    )(proj, proj, proj, proj, cos, sin, w, wq, wk, mask, gain)


def _ret_bwd(proj, cos, sin, gain, dmixed, opre, states, name):
    t = proj.shape[0]
    bk = TOK_TILE
    nb = t // bk
    w, wq, wk, mask, g_blk = _ret_consts(bk)
    v0, g0 = 2 * RET_QK, 2 * RET_QK + RET_V

    def body(q_ref, k_ref, v_ref, rg_ref, cos_ref, sin_ref, w_ref, wq_ref, wk_ref, mask_ref, gain_ref,
             dog_ref, opre_ref, st_ref, dp_ref, gg_ref, dr_ref):
        step = pl.program_id(0)
        i = nb - 1 - step

        @pl.when(step == 0)
        def _():
            dr_ref[...] = jnp.zeros_like(dr_ref)
            gg_ref[...] = jnp.zeros_like(gg_ref)

        c, s = cos_ref[...], sin_ref[...]
        valid = ((i * bk + lax.broadcasted_iota(jnp.int32, (bk, 1), 0)) >= N_PAD).astype(F32)
        qr = _rope(q_ref[...], c, s)
        kr = _rope(k_ref[...], c, s) * QK_SCALE * valid
        kb = kr.astype(BF16)
        dqr = jnp.zeros((bk, RET_QK), F32)
        dkr = jnp.zeros((bk, RET_QK), F32)
        for h in range(RET_HEADS):
            hm = mask_ref[h]
            cols = slice(RET_DV * h, RET_DV * (h + 1))
            vh = v_ref[:, cols].astype(BF16)
            o = opre_ref[:, cols]
            rstd = lax.rsqrt(jnp.mean(o * o, axis=-1, keepdims=True) + EPS)
            xhat = o * rstd
            rg = rg_ref[:, cols]
            sg = _sigmoid(rg)
            gate = rg * sg
            gn = gain_ref[:, cols]
            dog = dog_ref[:, cols]
            dp_ref[:, g0 + RET_DV * h:g0 + RET_DV * (h + 1)] = (
                dog * xhat * gn * (sg * (1.0 + rg * (1.0 - sg)))).astype(BF16)
            gg_ref[:, cols] += jnp.sum(dog * xhat * gate, axis=0, keepdims=True)
            dxh = dog * gn * gate
            do = (rstd * (dxh - xhat * jnp.mean(dxh * xhat, axis=-1, keepdims=True))).astype(BF16)
            qm = (qr * hm).astype(BF16)
            qw = (qr * (hm * wq_ref[h])).astype(BF16)
            kw = (kr * wk_ref[h]).astype(BF16)
            wh = w_ref[h]
            sm = (_dot(qm, kb, NT) * wh).astype(BF16)
            ds = (_dot(do, vh, NT) * wh).astype(BF16)
            dr = dr_ref[h]
            drb = dr.astype(BF16)
            dp_ref[:, v0 + RET_DV * h:v0 + RET_DV * (h + 1)] = (_dot(sm, do, TN) + _dot(kw, drb)).astype(BF16)
            dqr = dqr + _dot(ds, kb) * hm + _dot(do, st_ref[0, h].astype(BF16), NT) * (hm * wq_ref[h])
            dkr = dkr + _dot(ds, qm, TN) + _dot(vh, drb, NT) * wk_ref[h]
            dr_ref[h] = g_blk[h] * dr + _dot(qw, do, TN)
        dp_ref[:, 0:RET_QK] = _rope_t(dqr, c, s).astype(BF16)
        dp_ref[:, RET_QK:2 * RET_QK] = _rope_t(dkr * (QK_SCALE * valid), c, s).astype(BF16)

    full = lambda shape: pl.BlockSpec(shape, lambda i: (0,) * len(shape))
    rev = lambda col: (lambda i: (nb - 1 - i, col))
    return _pcall(
        body, name=name, grid=(nb,),
        in_specs=[pl.BlockSpec((bk, RET_QK), rev(0)), pl.BlockSpec((bk, RET_QK), rev(1)),
                  pl.BlockSpec((bk, RET_V), rev(1)), pl.BlockSpec((bk, RET_V), rev(2)),
                  pl.BlockSpec((bk, RET_QK), rev(0)), pl.BlockSpec((bk, RET_QK), rev(0)),
                  full((RET_HEADS, bk, bk)), full((RET_HEADS, bk, 1)), full((RET_HEADS, bk, 1)),
                  full((RET_HEADS, 1, RET_QK)), full((1, RET_V)),
                  pl.BlockSpec((bk, RET_V), rev(0)), pl.BlockSpec((bk, RET_V), rev(0)),
                  pl.BlockSpec((1, RET_HEADS, RET_QK, RET_DV), lambda i: (nb - 1 - i, 0, 0, 0))],
        out_specs=[pl.BlockSpec((bk, g0 + RET_V), rev(0)), pl.BlockSpec((1, RET_V), lambda i: (0, 0))],
        out_shape=[jax.ShapeDtypeStruct((t, IN_PAD), BF16), jax.ShapeDtypeStruct((1, RET_V), F32)],
        scratch_shapes=[pltpu.VMEM((RET_HEADS, RET_QK, RET_DV), F32)],
        compiler_params=_params("arbitrary"),
    )(proj, proj, proj, proj, cos, sin, w, wq, wk, mask, gain, dmixed, opre, states)


def _forget_cumsum(proj, bias, name):
    t = proj.shape[0]
    rt = TOK_TILE
    nb = t // rt
    tril = jnp.asarray(np.tril(np.ones((rt, rt))), F32)

    def body(z_ref, b_ref, tril_ref, c_ref, carry_ref):
        i = pl.program_id(0)

        @pl.when(i == 0)
        def _():
            carry_ref[...] = jnp.zeros_like(carry_ref)

        z = z_ref[...] + b_ref[...]
        logf = jnp.minimum(z, 0.0) - jnp.log(1.0 + jnp.exp(-jnp.abs(z)))
        c = lax.dot_general(tril_ref[...], logf, NN, precision=lax.Precision.HIGHEST,
                            preferred_element_type=F32) + carry_ref[...]
        c_ref[...] = c
        carry_ref[...] = c[rt - 1:rt, :]

    return _pcall(
        body, name=name, grid=(nb,),
        in_specs=[pl.BlockSpec((rt, LANE), lambda i: (i, FF_COL_BLOCK)), pl.BlockSpec((1, LANE), lambda i: (0, 0)),
                  pl.BlockSpec((rt, rt), lambda i: (0, 0))],
        out_specs=pl.BlockSpec((rt, LANE), lambda i: (i, 0)),
        out_shape=jax.ShapeDtypeStruct((t, LANE), F32),
        scratch_shapes=[pltpu.VMEM((1, LANE), F32)],
        compiler_params=_params("arbitrary"),
    )(proj, bias, tril)


def _forget_cumsum_bwd(proj, bias, drs, dcs, dproj, name):
    t = proj.shape[0]
    rt = TOK_TILE
    nb = t // rt
    triu = jnp.asarray(np.triu(np.ones((rt, rt))), F32)

    def body(z_ref, b_ref, triu_ref, drs_ref, dcs_ref, dproj_in, dz_ref, gb_ref, carry_ref):
        step = pl.program_id(0)

        @pl.when(step == 0)
        def _():
            carry_ref[...] = jnp.zeros_like(carry_ref)
            gb_ref[...] = jnp.zeros_like(gb_ref)

        dlogf = lax.dot_general(triu_ref[...], drs_ref[...] - dcs_ref[...], NN, precision=lax.Precision.HIGHEST,
                                preferred_element_type=F32) + carry_ref[...]
        carry_ref[...] = dlogf[0:1, :]
        z = z_ref[...] + b_ref[...]
        is_head = lax.broadcasted_iota(jnp.int32, (rt, LANE), 1) < FOX_HEADS
        dz = jnp.where(is_head, dlogf / (1.0 + jnp.exp(z)), 0.0)
        dz_ref[...] = dz.astype(BF16)
        gb_ref[...] += jnp.sum(dz, axis=0, keepdims=True)

    return _pcall(
        body, name=name, grid=(nb,),
        in_specs=[pl.BlockSpec((rt, LANE), lambda i: (nb - 1 - i, FF_COL_BLOCK)),
                  pl.BlockSpec((1, LANE), lambda i: (0, 0)),
                  pl.BlockSpec((rt, rt), lambda i: (0, 0)),
                  pl.BlockSpec((rt, LANE), lambda i: (nb - 1 - i, 0)),
                  pl.BlockSpec((rt, LANE), lambda i: (nb - 1 - i, 0)),
                  pl.BlockSpec(memory_space=pl.ANY)],
        out_specs=[pl.BlockSpec((rt, LANE), lambda i: (nb - 1 - i, FF_COL_BLOCK)),
                   pl.BlockSpec((1, LANE), lambda i: (0, 0))],
        out_shape=[jax.ShapeDtypeStruct(dproj.shape, BF16), jax.ShapeDtypeStruct((1, LANE), F32)],
        input_output_aliases={5: 0},
        scratch_shapes=[pltpu.VMEM((1, LANE), F32)],
        compiler_params=_params("arbitrary"),
    )(proj, bias, triu, drs, dcs, dproj)


FOX_PAIRS = FOX_HEADS // 2
L_ONE_Q = FOX_DH
L_ONE_K = FOX_DH + 3
L_LSE = FOX_DH + 4


def _split3(x):
    hi = x.astype(BF16).astype(F32)
    r = x - hi
    mid = r.astype(BF16).astype(F32)
    return hi, mid, r - mid


def _head_to_low(slab, e):
    return slab if e == 0 else pltpu.roll(slab, FOX_DH, axis=1)


def _pair(a, b, low):
    return jnp.where(low, a, pltpu.roll(b, FOX_DH, axis=1))


def _fox_prep(proj, c, name):
    t = proj.shape[0]
    tq = TOK_TILE

    def body(p_ref, c_ref, qa_ref, ka_ref, va_ref):
        i = pl.program_id(0)
        lane = lax.broadcasted_iota(jnp.int32, (tq, LANE), 1)
        low = lane < FOX_DH
        live = (i * tq + lax.broadcasted_iota(jnp.int32, (tq, 1), 0)) >= N_PAD
        q_tail = jnp.where(lane < L_ONE_Q + 3, 1.0, 0.0)
        k_ones = (lane >= L_ONE_K) & (lane < L_ONE_K + 4)
        v_tail = jnp.where(lane < FOX_DH + 2, 1.0, 0.0)
        for pair in range(FOX_PAIRS):
            base = 3 * LANE * pair
            for e in range(2):
                h = 2 * pair + e
                q = _head_to_low(p_ref[:, base:base + LANE], e)
                k = _head_to_low(p_ref[:, base + LANE:base + 2 * LANE], e)
                v = _head_to_low(p_ref[:, base + 2 * LANE:base + 3 * LANE], e)
                hi, mid, lo = _split3(jnp.where(live, -c_ref[:, h:h + 1], NEG))
                ka = jnp.where(low, k, jnp.where(k_ones, 1.0, 0.0))
                ka = jnp.where(lane == L_ONE_Q, hi, jnp.where(lane == L_ONE_Q + 1, mid, jnp.where(lane == L_ONE_Q + 2, lo, ka)))
                qa_ref[h] = jnp.where(low, q * QK_SCALE, q_tail).astype(BF16)
                ka_ref[h] = ka.astype(BF16)
                va_ref[h] = jnp.where(low, v, v_tail).astype(BF16)

    out = jax.ShapeDtypeStruct((FOX_HEADS, t, LANE), BF16)
    ospec = pl.BlockSpec((FOX_HEADS, tq, LANE), lambda i: (0, i, 0))
    return _pcall(
        body, name=name, grid=(t // tq,),
        in_specs=[pl.BlockSpec((tq, 3 * FOX_W), lambda i: (i, 1)), pl.BlockSpec((tq, LANE), lambda i: (i, 0))],
        out_specs=[ospec, ospec, ospec], out_shape=[out, out, out],
        compiler_params=_params("parallel"),
    )(proj, c)


STEP_PAIRS = 2
STEP_HEADS = 2 * STEP_PAIRS
FOX_GROUPS = FOX_PAIRS // STEP_PAIRS
FWD_PAIRS = 4
FWD_HEADS = 2 * FWD_PAIRS
FWD_GROUPS = FOX_PAIRS // FWD_PAIRS


def _blockdiag(a, b):
    z = jnp.zeros_like(a)
    return jnp.concatenate([jnp.concatenate([a, z], axis=1), jnp.concatenate([z, b], axis=1)], axis=0)


def _fox_fwd(qa, ka, va, mixed, name):
    nh, nq, tq, _ = qa.shape
    t = nq * tq

    def body(qa_ref, ka_ref, va_ref, mixed_in, mixed_ref, o_ref, lse_ref):
        i = pl.program_id(1)
        lane = lax.broadcasted_iota(jnp.int32, (tq, LANE), 1)
        causal = lax.broadcasted_iota(jnp.int32, (tq, tq), 1) <= lax.broadcasted_iota(jnp.int32, (tq, tq), 0)
        qps = [jnp.concatenate([qa_ref[2 * c], qa_ref[2 * c + 1]], axis=1) for c in range(FWD_PAIRS)]

        def logits(j):
            return [_dot(qps[c], _blockdiag(ka_ref[2 * c, j], ka_ref[2 * c + 1, j]), NT) for c in range(FWD_PAIRS)]

        def update(j, scores, carry, diagonal):
            new = []
            for c in range(FWD_PAIRS):
                ms, acc = carry[c]
                ps, ms_new, alphas = [], [], []
                for e in range(2):
                    s = scores[c][:, e * tq:(e + 1) * tq]
                    if diagonal:
                        s = jnp.where(causal, s, NEG)
                    m_new = jnp.maximum(ms[e], jnp.max(s, axis=-1, keepdims=True))
                    ps.append(jnp.exp(s - m_new).astype(BF16))
                    ms_new.append(m_new)
                    alphas.append(jnp.broadcast_to(jnp.exp(ms[e] - m_new), (tq, LANE)))
                pv = _dot(jnp.concatenate(ps, axis=1), _blockdiag(va_ref[2 * c, j], va_ref[2 * c + 1, j]))
                new.append((tuple(ms_new), jnp.concatenate(alphas, axis=1) * acc + pv))
            return tuple(new)

        m0 = jnp.full((tq, 1), NEG, F32)
        init = tuple(((m0, m0), jnp.zeros((tq, 2 * LANE), F32)) for _ in range(FWD_PAIRS))
        carry = lax.fori_loop(0, i, lambda j, cr: update(j, logits(j), cr, False), init)
        o_pairs = []
        lse = jnp.zeros((tq, LANE), F32)
        for c, (ms, acc) in enumerate(update(i, logits(i), carry, True)):
            outs = []
            for e in range(2):
                half = acc[:, e * LANE:(e + 1) * LANE]
                l = half[:, FOX_DH:FOX_DH + 1]
                outs.append(half / l)
                lse = jnp.where(lane == 2 * c + e, ms[e] + jnp.log(l), lse)
            o_pairs.append(_pair(outs[0], outs[1], lane < FOX_DH))
        o_all = jnp.concatenate(o_pairs, axis=1)
        mixed_ref[...] = o_all.astype(BF16)
        o_ref[...] = o_all
        lse_ref[...] = lse

    width = FWD_PAIRS * LANE
    whole = pl.BlockSpec((FWD_HEADS, nq, tq, LANE), lambda g, i: (g, 0, 0, 0), pipeline_mode=pl.Buffered(1))
    return _pcall(
        body, name=name, grid=(FWD_GROUPS, nq),
        in_specs=[pl.BlockSpec((FWD_HEADS, None, tq, LANE), lambda g, i: (g, i, 0, 0)), whole, whole,
                  pl.BlockSpec(memory_space=pl.ANY)],
        out_specs=[pl.BlockSpec((tq, width), lambda g, i: (i, RET_V // width + g)),
                   pl.BlockSpec((tq, width), lambda g, i: (i, g)),
                   pl.BlockSpec((None, tq, LANE), lambda g, i: (g, i, 0))],
        out_shape=[jax.ShapeDtypeStruct(mixed.shape, BF16), jax.ShapeDtypeStruct((t, FOX_W), F32),
                   jax.ShapeDtypeStruct((FWD_GROUPS, t, LANE), F32)],
        input_output_aliases={3: 0},
        compiler_params=_params("parallel", "parallel"),
    )(qa, ka, va, mixed)


def _fox_prep_bwd(dmixed, o_fox, lse, qa, name):
    t = dmixed.shape[0]
    tq = TOK_TILE

    def body(dm_ref, o_ref, lse_ref, qa_ref, qab_ref, doa_ref):
        i = pl.program_id(0)
        lane = lax.broadcasted_iota(jnp.int32, (tq, LANE), 1)
        low = lane < FOX_DH
        live = (i * tq + lax.broadcasted_iota(jnp.int32, (tq, 1), 0)) >= N_PAD
        for pair in range(FOX_PAIRS):
            cols = slice(LANE * pair, LANE * (pair + 1))
            d_slab = dm_ref[:, cols]
            prod = d_slab * o_ref[:, cols]
            for e in range(2):
                h = 2 * pair + e
                nd = -jnp.sum(jnp.where(low, _head_to_low(prod, e), 0.0), axis=-1, keepdims=True)
                nd_hi = nd.astype(BF16).astype(F32)
                doa = jnp.where(low, _head_to_low(d_slab, e), 0.0)
                doa = jnp.where(lane == FOX_DH, nd_hi, jnp.where(lane == FOX_DH + 1, nd - nd_hi, doa))
                doa_ref[h] = doa.astype(BF16)
                lse_h = lse_ref[h // FWD_HEADS][:, h % FWD_HEADS:h % FWD_HEADS + 1]
                hi, mid, lo = _split3(jnp.where(live, -lse_h, 0.0))
                qab = qa_ref[h].astype(F32)
                qab = jnp.where(lane == L_LSE, hi, jnp.where(lane == L_LSE + 1, mid, jnp.where(lane == L_LSE + 2, lo, qab)))
                qab_ref[h] = qab.astype(BF16)

    out = jax.ShapeDtypeStruct((FOX_HEADS, t, LANE), BF16)
    hspec = pl.BlockSpec((FOX_HEADS, tq, LANE), lambda i: (0, i, 0))
    return _pcall(
        body, name=name, grid=(t // tq,),
        in_specs=[pl.BlockSpec((tq, FOX_W), lambda i: (i, 1)), pl.BlockSpec((tq, FOX_W), lambda i: (i, 0)),
                  pl.BlockSpec((FWD_GROUPS, tq, LANE), lambda i: (0, i, 0)), hspec],
        out_specs=[hspec, hspec], out_shape=[out, out],
        compiler_params=_params("parallel"),
    )(dmixed, o_fox, lse, qa)


def _fox_bwd(qab, doa, ka, va, dproj, name):
    nh, nq, tq, _ = qab.shape
    t = nq * tq
    slab = 3 * LANE * STEP_PAIRS
    group0 = (2 * RET_QK + 2 * RET_V) // slab

    def body(qab_ref, doa_ref, ka_ref, va_ref, dproj_in, dp_ref, drs_ref, dcs_ref, dq_ref):
        g, j = pl.program_id(0), pl.program_id(1)

        @pl.when((g == 0) & (j == 0))
        def _():
            drs_ref[...] = jnp.zeros_like(drs_ref)
            dcs_ref[...] = jnp.zeros_like(dcs_ref)

        @pl.when(j == 0)
        def _():
            dq_ref[...] = jnp.zeros_like(dq_ref)

        lane = lax.broadcasted_iota(jnp.int32, (tq, LANE), 1)
        low = lane < FOX_DH
        key_le_query = lax.broadcasted_iota(jnp.int32, (tq, tq), 0) <= lax.broadcasted_iota(jnp.int32, (tq, tq), 1)

        def by_head(c, a, b, col):
            h = STEP_HEADS * g + 2 * c
            return jnp.where(lane == h, a[:, col:col + 1], jnp.where(lane == h + 1, b[:, col:col + 1], 0.0))

        kbs = [ka_ref[h] for h in range(STEP_HEADS)]
        vbs = [va_ref[h] for h in range(STEP_HEADS)]

        def step(i, carry, diagonal):
            qbs = [qab_ref[h, i] for h in range(STEP_HEADS)]
            dobs = [doa_ref[h, i] for h in range(STEP_HEADS)]
            st = [_dot(kbs[h], qbs[h], NT) for h in range(STEP_HEADS)]
            dpt = [_dot(vbs[h], dobs[h], NT) for h in range(STEP_HEADS)]
            new = []
            for h in range(STEP_HEADS):
                p = jnp.exp(st[h])
                if diagonal:
                    p = jnp.where(key_le_query, p, 0.0)
                ds = (p * dpt[h]).astype(BF16)
                dq_ref[h, i] += _dot(ds, kbs[h], TN)
                dk, dv = carry[h]
                new.append((dk + _dot(ds, qbs[h]), dv + _dot(p.astype(BF16), dobs[h])))
            return tuple(new)

        zero = jnp.zeros((tq, LANE), F32)
        carry = step(j, tuple((zero, zero) for _ in range(STEP_HEADS)), True)
        carry = lax.fori_loop(j + 1, nq, lambda i, cr: step(i, cr, False), carry)
        rows = pl.ds(pl.multiple_of(j * tq, tq), tq)
        for c in range(STEP_PAIRS):
            (dka, dva), (dkb, dvb) = carry[2 * c], carry[2 * c + 1]
            c0 = 3 * LANE * c
            dp_ref[rows, c0 + LANE:c0 + 2 * LANE] = _pair(dka, dkb, low).astype(BF16)
            dp_ref[rows, c0 + 2 * LANE:c0 + 3 * LANE] = _pair(dva, dvb, low).astype(BF16)
            dcs_ref[rows, :] += by_head(c, dka, dkb, L_ONE_Q)

        @pl.when(j == nq - 1)
        def _():
            for c in range(STEP_PAIRS):
                for blk in range(nq):
                    r = slice(blk * tq, (blk + 1) * tq)
                    a, b = dq_ref[2 * c, blk], dq_ref[2 * c + 1, blk]
                    dp_ref[r, 3 * LANE * c:3 * LANE * c + LANE] = (_pair(a, b, low) * QK_SCALE).astype(BF16)
                    drs_ref[r, :] += by_head(c, a, b, L_ONE_K)

    whole = pl.BlockSpec((STEP_HEADS, nq, tq, LANE), lambda g, j: (g, 0, 0, 0), pipeline_mode=pl.Buffered(1))
    blk = pl.BlockSpec((STEP_HEADS, None, tq, LANE), lambda g, j: (g, j, 0, 0))
    sums = pl.BlockSpec((t, LANE), lambda g, j: (0, 0), pipeline_mode=pl.Buffered(1))
    return _pcall(
        body, name=name, grid=(FOX_GROUPS, nq),
        in_specs=[whole, whole, blk, blk, pl.BlockSpec(memory_space=pl.ANY)],
        out_specs=[pl.BlockSpec((t, slab), lambda g, j: (0, group0 + g)), sums, sums],
        out_shape=[jax.ShapeDtypeStruct(dproj.shape, BF16), jax.ShapeDtypeStruct((t, LANE), F32),
                   jax.ShapeDtypeStruct((t, LANE), F32)],
        input_output_aliases={4: 0},
        scratch_shapes=[pltpu.VMEM((STEP_HEADS, nq, tq, LANE), F32)],
        compiler_params=_params("arbitrary", "arbitrary"),
    )(qab, doa, ka, va, dproj)


HALO = 8


def _rows_ext(ref, r0, rows, t, before, after):
    lo, hi = r0 - before, r0 + rows + after
    width = ref.shape[-1]
    parts = []
    if lo < 0:
        parts.append(jnp.zeros((-lo, width), F32))
    parts.append(ref[max(lo, 0):min(hi, t), :].astype(F32))
    if hi > t:
        parts.append(jnp.zeros((hi - t, width), F32))
    return parts[0] if len(parts) == 1 else jnp.concatenate(parts, axis=0)


def _conv_taps(a_ext, r0_ext, cw_ref, cb_ref):
    n = a_ext.shape[0]
    if r0_ext < N_PAD:
        row = r0_ext + lax.broadcasted_iota(jnp.int32, (n, 1), 0)
        a_ext = jnp.where(row >= N_PAD, a_ext, 0.0)
    a1 = pltpu.roll(a_ext, 1, axis=0)
    a2 = pltpu.roll(a_ext, 2, axis=0)
    acc = cb_ref[...] + a2 * cw_ref[0:1, :] + a1 * cw_ref[1:2, :] + a_ext * cw_ref[2:3, :]
    return a_ext, a1, a2, acc


FF_COLS = 256


def _up_conv_fwd(n2, w_up_t, conv_w8, conv_b, name):
    t, d = n2.shape
    f = w_up_t.shape[1]
    rows = TOK_TILE
    starts = list(range(0, t, rows))

    def body(n_ref, wa_ref, wb_ref, cw_ref, cb_ref, up_ref, g_ref):
        wa, wb = wa_ref[...], wb_ref[...]

        def project(r0):
            n_rows = n_ref[r0:r0 + rows, :]
            up_ref[0, r0:r0 + rows, :] = _dot(n_rows, wa, NT)
            up_ref[1, r0:r0 + rows, :] = _dot(n_rows, wb, NT)

        def activate(r0):
            a_ext = _rows_ext(up_ref.at[0], r0, rows, t, HALO, 0)
            _, _, _, acc = _conv_taps(a_ext, r0 - HALO, cw_ref, cb_ref)
            acc = acc[HALO:, :]
            g_ref[r0:r0 + rows, :] = (acc * _sigmoid(acc) * up_ref[1, r0:r0 + rows, :]).astype(BF16)

        project(starts[0])
        for r0, r_next in zip(starts, starts[1:] + [None]):
            if r_next is not None:
                project(r_next)
            activate(r0)

    return _pcall(
        body, name=name, grid=(f // FF_COLS,),
        in_specs=[pl.BlockSpec((t, d), lambda j: (0, 0), pipeline_mode=pl.Buffered(1)),
                  pl.BlockSpec((None, FF_COLS, d), lambda j: (0, j, 0)), pl.BlockSpec((None, FF_COLS, d), lambda j: (1, j, 0)),
                  pl.BlockSpec((8, FF_COLS), lambda j: (0, j)), pl.BlockSpec((1, FF_COLS), lambda j: (0, j))],
        out_specs=[pl.BlockSpec((2, t, FF_COLS), lambda j: (0, 0, j)), pl.BlockSpec((t, FF_COLS), lambda j: (0, j))],
        out_shape=[jax.ShapeDtypeStruct((2, t, f), F32), jax.ShapeDtypeStruct((t, f), BF16)],
        compiler_params=_params("parallel"),
    )(n2, w_up_t, w_up_t, conv_w8, conv_b)


def _dg_conv_bwd(up, conv_w8, conv_b, dh2, w_down, name):
    _, t, f = up.shape
    d = dh2.shape[1]
    rows = TOK_TILE
    starts = list(range(0, t, rows))

    def body(a_ref, b_ref, cw_ref, cb_ref, dh_ref, wd_ref, dup_ref, gcw_ref, gcb_ref, dg_ref):
        wd = wd_ref[...]

        def project(r0):
            dg_ref[r0:r0 + rows, :] = _dot(dh_ref[r0:r0 + rows, :], wd, NT)

        gw = [jnp.zeros((1, FF_COLS), F32) for _ in range(3)]
        gb = jnp.zeros((1, FF_COLS), F32)
        project(starts[0])
        for r0, r_next in zip(starts, starts[1:] + [None]):
            if r_next is not None:
                project(r_next)
            a_ext = _rows_ext(a_ref, r0, rows, t, HALO, HALO)
            b_ext = _rows_ext(b_ref, r0, rows, t, HALO, HALO)
            dg_ext = _rows_ext(dg_ref, r0, rows, t, HALO, HALO)
            a0, a1, a2, acc = _conv_taps(a_ext, r0 - HALO, cw_ref, cb_ref)
            sg = _sigmoid(acc)
            dacc = dg_ext * b_ext * (sg * (1.0 + acc * (1.0 - sg)))
            n = dacc.shape[0]
            da = (dacc * cw_ref[2:3, :] + pltpu.roll(dacc, n - 1, axis=0) * cw_ref[1:2, :]
                  + pltpu.roll(dacc, n - 2, axis=0) * cw_ref[0:1, :])
            core = slice(HALO, HALO + rows)
            da = da[core, :]
            if r0 < N_PAD:
                row = r0 + lax.broadcasted_iota(jnp.int32, (rows, 1), 0)
                da = jnp.where(row >= N_PAD, da, 0.0)
            dup_ref[0, r0:r0 + rows, :] = da.astype(BF16)
            dup_ref[1, r0:r0 + rows, :] = (dg_ext * acc * sg)[core, :].astype(BF16)
            dacc_c = dacc[core, :]
            gw[0] = gw[0] + jnp.sum(dacc_c * a2[core, :], axis=0, keepdims=True)
            gw[1] = gw[1] + jnp.sum(dacc_c * a1[core, :], axis=0, keepdims=True)
            gw[2] = gw[2] + jnp.sum(dacc_c * a0[core, :], axis=0, keepdims=True)
            gb = gb + jnp.sum(dacc_c, axis=0, keepdims=True)
        gcw_ref[...] = jnp.zeros((8, FF_COLS), F32)
        for tap in range(3):
            gcw_ref[tap:tap + 1, :] = gw[tap]
        gcb_ref[...] = gb

    return _pcall(
        body, name=name, grid=(f // FF_COLS,),
        in_specs=[pl.BlockSpec((None, t, FF_COLS), lambda j: (0, 0, j)), pl.BlockSpec((None, t, FF_COLS), lambda j: (1, 0, j)),
                  pl.BlockSpec((8, FF_COLS), lambda j: (0, j)), pl.BlockSpec((1, FF_COLS), lambda j: (0, j)),
                  pl.BlockSpec((t, d), lambda j: (0, 0), pipeline_mode=pl.Buffered(1)),
                  pl.BlockSpec((FF_COLS, d), lambda j: (j, 0))],
        out_specs=[pl.BlockSpec((2, t, FF_COLS), lambda j: (0, 0, j)), pl.BlockSpec((8, FF_COLS), lambda j: (0, j)),
                   pl.BlockSpec((1, FF_COLS), lambda j: (0, j))],
        out_shape=[jax.ShapeDtypeStruct((2, t, f), BF16), jax.ShapeDtypeStruct((8, f), F32),
                   jax.ShapeDtypeStruct((1, f), F32)],
        scratch_shapes=[pltpu.VMEM((t, FF_COLS), F32)],
        compiler_params=_params("parallel"),
    )(up, up, conv_w8, conv_b, dh2, w_down)


def _exchange(arrays, kinds, name):
    n = len(arrays)
    npeer = N_DEV - 1

    def body(*refs):
        ins, outs = refs[:n], refs[n:2 * n]
        send_sems, recv_sems, local_sems = refs[2 * n:]
        x, y, c = lax.axis_index("x"), lax.axis_index("y"), lax.axis_index("c")
        me = 4 * x + 2 * y + c
        copies, locals_ = [], []
        for a in range(n):
            gather = kinds[a] == "gather"
            own = pltpu.make_async_copy(ins[a] if gather else ins[a].at[me], outs[a].at[me], local_sems.at[a])
            own.start()
            locals_.append(own)
            for d in range(1, N_DEV):
                px = 1 - x if d & 4 else x
                py = 1 - y if d & 2 else y
                pc = 1 - c if d & 1 else c
                src = ins[a] if gather else ins[a].at[4 * px + 2 * py + pc]
                cp = pltpu.make_async_remote_copy(
                    src_ref=src, dst_ref=outs[a].at[me],
                    send_sem=send_sems.at[a * npeer + d - 1], recv_sem=recv_sems.at[a * npeer + d - 1],
                    device_id=(px, py, pc), device_id_type=pl.DeviceIdType.MESH)
                cp.start()
                copies.append(cp)
        for cp in copies:
            cp.wait_recv()
        for cp in copies:
            cp.wait_send()
        for own in locals_:
            own.wait()

    out_shape = [jax.ShapeDtypeStruct((N_DEV,) + (a.shape if k == "gather" else a.shape[1:]), a.dtype)
                 for a, k in zip(arrays, kinds)]
    return _pcall(
        body, name=name,
        in_specs=[pl.BlockSpec(memory_space=pl.ANY)] * n,
        out_specs=[pl.BlockSpec(memory_space=pl.ANY)] * n,
        out_shape=out_shape,
        scratch_shapes=[pltpu.SemaphoreType.DMA((n * npeer,)), pltpu.SemaphoreType.DMA((n * npeer,)),
                        pltpu.SemaphoreType.DMA((n,))],
        compiler_params=pltpu.CompilerParams(has_side_effects=True),
    )(*arrays)


def _peer_copies(srcs, lands, kinds, send_sems, recv_sems):
    x, y, c = lax.axis_index("x"), lax.axis_index("y"), lax.axis_index("c")
    me = 4 * x + 2 * y + c
    copies = []
    for a in range(len(srcs)):
        for d in range(1, N_DEV):
            px = 1 - x if d & 4 else x
            py = 1 - y if d & 2 else y
            pc = 1 - c if d & 1 else c
            k = a * (N_DEV - 1) + d - 1
            copies.append(pltpu.make_async_remote_copy(
                src_ref=srcs[a] if kinds[a] == "gather" else srcs[a].at[4 * px + 2 * py + pc], dst_ref=lands[a].at[me],
                send_sem=send_sems.at[k], recv_sem=recv_sems.at[k],
                device_id=(px, py, pc), device_id_type=pl.DeviceIdType.MESH))
    return copies


def _exchange_start(arrays, kinds, name, after=None):
    n = len(arrays)
    nsem = n * (N_DEV - 1)
    hbm = pl.BlockSpec(memory_space=pltpu.HBM)
    sem = pl.BlockSpec(memory_space=pltpu.SEMAPHORE)
    land_shapes = [(N_DEV,) + (a.shape if k == "gather" else a.shape[1:]) for a, k in zip(arrays, kinds)]

    n_in = 2 * n + int(after is not None)

    def body(*refs):
        srcs, lands = refs[:n], refs[n:2 * n]
        send_sems, recv_sems = refs[n_in], refs[n_in + 1]
        token = refs[-1]
        for cp in _peer_copies(srcs, lands, kinds, send_sems, recv_sems):
            cp.start()
        token[...] = jnp.zeros_like(token)

    operands = [pltpu.with_memory_space_constraint(a, pltpu.HBM) for a in arrays]
    operands += [pltpu.with_memory_space_constraint(lax.empty(s, a.dtype), pltpu.HBM) for s, a in zip(land_shapes, arrays)]
    operands += [] if after is None else [after]
    out = _pcall(
        body, name=name,
        in_specs=[hbm] * (2 * n) + ([] if after is None else [pl.BlockSpec(memory_space=pl.ANY)]),
        out_specs=[sem, sem] + [hbm] * (2 * n) + [pl.BlockSpec(memory_space=pltpu.VMEM)],
        out_shape=[pltpu.SemaphoreType.DMA((nsem,)), pltpu.SemaphoreType.DMA((nsem,))]
        + [pltpu.HBM(a.shape, a.dtype) for a in arrays]
        + [pltpu.HBM(s, a.dtype) for s, a in zip(land_shapes, arrays)]
        + [jax.ShapeDtypeStruct((8, LANE), F32)],
        input_output_aliases={k: 2 + k for k in range(2 * n)},
        compiler_params=pltpu.CompilerParams(has_side_effects=pltpu.SideEffectType.DATAFLOW_SIDE_EFFECTING),
    )(*operands)
    return out[0], out[1], list(out[2:2 + n]), list(out[2 + n:2 + 2 * n]), out[-1]


def _exchange_wait(started, kinds, after, name):
    send_sems, recv_sems, srcs, lands, _ = started
    n = len(srcs)
    hbm = pl.BlockSpec(memory_space=pltpu.HBM)
    sem = pl.BlockSpec(memory_space=pltpu.SEMAPHORE)

    def body(*refs):
        src_refs, land_refs = refs[:n], refs[n:2 * n]
        copies = _peer_copies(src_refs, land_refs, kinds, refs[2 * n], refs[2 * n + 1])
        for cp in copies:
            cp.wait_send()
        for cp in copies:
            cp.wait_recv()

    out = _pcall(
        body, name=name,
        in_specs=[hbm] * (2 * n) + [sem, sem, pl.BlockSpec(memory_space=pl.ANY)],
        out_specs=[hbm] * (2 * n),
        out_shape=[pltpu.HBM(a.shape, a.dtype) for a in srcs + lands],
        input_output_aliases={k: k for k in range(2 * n)},
        compiler_params=pltpu.CompilerParams(has_side_effects=pltpu.SideEffectType.DATAFLOW_SIDE_EFFECTING),
    )(*srcs, *lands, send_sems, recv_sems, after)
    me = 4 * lax.axis_index("x") + 2 * lax.axis_index("y") + lax.axis_index("c")
    filled = []
    for src, land, kind in zip(out[:n], out[n:], kinds):
        own = src if kind == "gather" else lax.dynamic_index_in_dim(src, me, axis=0, keepdims=False)
        filled.append(lax.dynamic_update_slice(land, own[None], (me,) + (0,) * own.ndim))
    return filled


def _sum_slots(slots, name, rows_tile):
    nd, r, c = slots.shape

    def body(s_ref, o_ref):
        acc = s_ref[0].astype(F32)
        for p in range(1, nd):
            acc = acc + s_ref[p].astype(F32)
        o_ref[...] = acc

    return _pcall(
        body, name=name, grid=(r // rows_tile,),
        in_specs=[pl.BlockSpec((nd, rows_tile, c), lambda i: (0, i, 0))],
        out_specs=pl.BlockSpec((rows_tile, c), lambda i: (i, 0)),
        out_shape=jax.ShapeDtypeStruct((r, c), F32),
        compiler_params=_params("parallel"),
    )(slots)


def _sum_slots_small(slot_arrays, name):
    n = len(slot_arrays)

    def body(*refs):
        for s_ref, o_ref in zip(refs[:n], refs[n:]):
            acc = s_ref[0]
            for p in range(1, s_ref.shape[0]):
                acc = acc + s_ref[p]
            o_ref[...] = acc

    return _pcall(body, name=name, out_shape=[jax.ShapeDtypeStruct(a.shape[1:], F32) for a in slot_arrays])(*slot_arrays)


def _adamw_update(w_ref, g_ref, m_ref, v_ref, d_ref, nm_ref, nv_ref):
    gr = g_ref[...]
    nm = ADAM_B1 * m_ref[...] + (1.0 - ADAM_B1) * gr
    nv = ADAM_B2 * v_ref[...] + (1.0 - ADAM_B2) * (gr * gr)
    m_hat = nm / (1.0 - ADAM_B1 ** ADAM_STEP)
    v_hat = nv / (1.0 - ADAM_B2 ** ADAM_STEP)
    d_ref[...] = -ADAM_LR * (m_hat / (jnp.sqrt(v_hat) + ADAM_EPS) + ADAM_WD * w_ref[...])
    nm_ref[...] = nm
    nv_ref[...] = nv


def _adamw_small(ws, gs, ms, vs, name):
    n = len(ws)

    def body(*refs):
        ins, outs = refs[:4 * n], refs[4 * n:]
        for k in range(n):
            _adamw_update(ins[k], ins[n + k], ins[2 * n + k], ins[3 * n + k], outs[k], outs[n + k], outs[2 * n + k])

    shapes = [jax.ShapeDtypeStruct(w.shape, F32) for w in ws]
    out = _pcall(body, name=name, out_shape=shapes * 3)(*ws, *gs, *ms, *vs)
    return list(out[:n]), list(out[n:2 * n]), list(out[2 * n:])


def _adamw(w, g, m, v, name, rows_tile):
    _, r, c = w.shape
    body = lambda *refs: _adamw_update(*refs)
    spec3 = pl.BlockSpec((None, rows_tile, c), lambda i: (0, i, 0))
    spec2 = pl.BlockSpec((rows_tile, c), lambda i: (i, 0))
    shp = jax.ShapeDtypeStruct((1, r, c), F32)
    return _pcall(
        body, name=name, grid=(r // rows_tile,), in_specs=[spec3, spec2, spec3, spec3], out_specs=[spec3] * 3,
        out_shape=[shp] * 3, compiler_params=_params("parallel"),
    )(w, g, m, v)


F0 = 2 * RET_QK + 2 * RET_V


def _to_internal_rows(w_t):
    cols = w_t.shape[1]
    fox = w_t[F0:F0 + 3 * FOX_W].reshape(3, FOX_PAIRS, LANE, cols).transpose(1, 0, 2, 3).reshape(3 * FOX_W, cols)
    tail = jnp.zeros((IN_PAD - IN_WIDTH, cols), w_t.dtype)
    return jnp.concatenate([w_t[:F0], fox, w_t[F0 + 3 * FOX_W:], tail], axis=0)


def _from_internal_rows(g_t):
    cols = g_t.shape[1]
    fox = g_t[F0:F0 + 3 * FOX_W].reshape(FOX_PAIRS, 3, LANE, cols).transpose(1, 0, 2, 3).reshape(3 * FOX_W, cols)
    return jnp.concatenate([g_t[:F0], fox, g_t[F0 + 3 * FOX_W:F0 + 3 * FOX_W + FOX_HEADS]], axis=0)


def _local_step(x, target, meta, attn_g, fox_b, ret_g, ffn_g, conv_w8, conv_b, final_g,
                first_weight, late_weights, ffn_grads_ready, out_grad_ready, in_grad_ready):
    seq, d = x.shape
    t = seq + PREFIX
    tm = TOK_TILE
    nq = t // tm
    fox_b128 = jnp.pad(fox_b, ((0, 0), (0, LANE - FOX_HEADS)))

    h0, n1 = _prep_norm(x, meta, attn_g, "prep_norm")
    w_in_t = first_weight(n1)
    proj = _mm_simple(n1, w_in_t, mode="nt", tm=tm, tn=IN_PAD, tk=d, out_dtype=F32, name="mm_in")
    cos, sin = _rope_tables(t)
    o_pre, mixed, states = _ret_fwd(proj, cos, sin, ret_g, "ret_fwd")
    c = _forget_cumsum(proj, fox_b128, "forget_cumsum")
    qa, ka, va = _fox_prep(proj, c, "fox_prep")
    by_block = lambda a: a.reshape(FOX_HEADS, nq, tm, LANE)
    mixed, o_fox, lse = _fox_fwd(by_block(qa), by_block(ka), by_block(va), mixed, "fox_fwd")
    w_out, w_up_t, w_down = late_weights(o_fox)
    h1 = _mm_simple(mixed, w_out, mode="nn", tm=tm, tn=d, tk=d, out_dtype=F32, name="mm_out", add=h0)
    n2 = _rmsnorm(h1, ffn_g, "ffn_norm")
    nf = D_FF // 1408
    up, g = _up_conv_fwd(n2, w_up_t, conv_w8, conv_b, "up_conv_fwd")
    h2 = _mm_simple(g, w_down, mode="nn", tm=tm, tn=d, tk=D_FF, out_dtype=F32, name="mm_down", add=h1)

    loss_tile, dh2, dh2_b, g_final = _loss_bwd(h2, target, final_g, "loss_bwd")
    tkw = 1408 if t % 1408 == 0 else tm
    gw_down = _mm_simple(g, dh2_b, mode="tn", tm=1408, tn=d, tk=tkw, out_dtype=BF16, name="mm_gw_down")
    dup, g_conv_w8, g_conv_b = _dg_conv_bwd(up, conv_w8, conv_b, dh2_b, w_down, "dg_conv_bwd")
    half = lambda p: pl.BlockSpec((None, tm, D_FF), lambda i, j, k: (p, i, 0))
    half_w = lambda p: pl.BlockSpec((None, D_FF, d), lambda i, j, k: (p, 0, 0), pipeline_mode=pl.Buffered(1))
    dn2 = _matmul(
        [dup, dup], [w_up_t, w_up_t], mode="nn", grid=(nq, 1, 1),
        a_spec=[half(0), half(1)], b_spec=[half_w(0), half_w(1)],
        o_spec=pl.BlockSpec((tm, d), lambda i, j, k: (i, 0)),
        out_shape=jax.ShapeDtypeStruct((t, d), F32), name="mm_dn2")
    gw_up_t = _matmul(
        dup, n2, mode="tn", grid=(2 * nf, 1, t // tkw),
        a_spec=pl.BlockSpec((None, tkw, 1408), lambda i, j, k: (i // nf, k, i % nf)),
        b_spec=pl.BlockSpec((tkw, d), lambda i, j, k: (k, 0)),
        o_spec=pl.BlockSpec((1408, d), lambda i, j, k: (i, 0)),
        out_shape=jax.ShapeDtypeStruct((2 * D_FF, d), BF16), name="mm_gw_up")
    dh1, g_ffn = _rmsnorm_bwd(dn2, h1, ffn_g + ffn_grads_ready(gw_down, gw_up_t), dh2, "ffn_norm_bwd")

    dmixed = _mm_simple(dh1, w_out, mode="nt", tm=tm, tn=d, tk=d, out_dtype=F32, name="mm_dmixed")
    gw_out = _mm_simple(mixed, dh1, mode="tn", tm=d, tn=d, tk=tkw, out_dtype=BF16, name="mm_gw_out")
    dproj, g_ret = _ret_bwd(proj, cos, sin, ret_g + out_grad_ready(gw_out), dmixed, o_pre, states, "ret_bwd")
    qab, doa = _fox_prep_bwd(dmixed, o_fox, lse, qa, "fox_prep_bwd")
    dproj, drs, dcs = _fox_bwd(by_block(qab), by_block(doa), by_block(ka), by_block(va), dproj, "fox_bwd")
    dproj, g_fox_b = _forget_cumsum_bwd(proj, fox_b128, drs, dcs, dproj, "forget_cumsum_bwd")
    gw_in_t = _mm_simple(dproj, n1, mode="tn", tm=640, tn=d, tk=tkw, out_dtype=BF16, name="mm_gw_in")
    sent = in_grad_ready(gw_in_t)
    dn1 = _mm_simple(dproj, w_in_t, mode="nn", tm=tm, tn=d, tk=IN_PAD, out_dtype=F32, name="mm_dn1", after=sent)
    dh0, g_attn = _rmsnorm_bwd(dn1, h0, attn_g, dh1, "attn_norm_bwd")

    grads = dict(meta=dh0[N_PAD:PREFIX], attn_g=g_attn, fox_b=g_fox_b, ret_g=g_ret,
                 ffn_g=g_ffn, conv_w=g_conv_w8, conv_b=g_conv_b, final_g=g_final)
    return loss_tile, dh0[PREFIX:], grads


def kernel(x, meta_tokens, attn_norm_g, w_in, fox_forget_b, ret_norm_g, w_out, ffn_norm_g, w_up, conv_w, conv_b, w_down, final_norm_g, loss_target, m_meta_tokens, m_attn_norm_g, m_w_in, m_fox_forget_b, m_ret_norm_g, m_w_out, m_ffn_norm_g, m_w_up, m_conv_w, m_conv_b, m_w_down, m_final_norm_g, v_meta_tokens, v_attn_norm_g, v_w_in, v_fox_forget_b, v_ret_norm_g, v_w_out, v_ffn_norm_g, v_w_up, v_conv_w, v_conv_b, v_w_down, v_final_norm_g):
    d = D_MODEL
    me = 4 * lax.axis_index("x") + 2 * lax.axis_index("y") + lax.axis_index("c")
    in_blk = IN_WIDTH // N_DEV
    in_blk_pad = 400
    up_blk = 2 * D_FF // N_DEV
    down_blk = D_FF // N_DEV
    cw_blk = D_FF // N_DEV

    w_in_loc = jnp.pad(w_in[0].T.astype(BF16), ((0, in_blk_pad - in_blk), (0, 0)))
    cw_loc = jnp.pad(conv_w[0], ((0, 5), (0, 384 - cw_blk)))
    g_meta, g_cw = _exchange([meta_tokens, cw_loc], ["gather"] * 2, "gather_small")
    first = _exchange_start([w_in_loc], ["gather"], "gather_in_start", after=g_meta)
    rest_loc = [(w_out[0] + first[-1][0:1, 0:1]).astype(BF16), w_up[0].T.astype(BF16), w_down[0].astype(BF16)]
    rest = _exchange_start(rest_loc, ["gather"] * 3, "gather_rest_start")
    meta_f = g_meta.transpose(1, 0, 2).reshape(N_META, d)
    conv_w8 = jnp.pad(g_cw[:, :3, :cw_blk].transpose(1, 0, 2).reshape(3, D_FF), ((0, 5), (0, 0)))
    pending = {}

    def first_weight(after):
        (g_in,) = _exchange_wait(first, ["gather"], after, "gather_in_wait")
        return _to_internal_rows(g_in[:, :in_blk].reshape(IN_WIDTH, d))

    def in_grad_ready(gw_in_t):
        blocks = _from_internal_rows(gw_in_t).reshape(N_DEV, in_blk, d)
        blocks = jnp.pad(blocks, ((0, 0), (0, in_blk_pad - in_blk), (0, 0)))
        pending["in"] = _exchange_start([blocks], ["scatter"], "grads_in_start")
        return pending["in"][-1][0:1, 0:1]

    def late_weights(after):
        g_out, g_up, g_down = _exchange_wait(rest, ["gather"] * 3, after, "gather_rest_wait")
        return g_out.reshape(d, d), g_up.reshape(2, D_FF, d), g_down.reshape(D_FF, d)

    def ffn_grads_ready(gw_down, gw_up_t):
        blocks = [gw_down.reshape(N_DEV, down_blk, d), gw_up_t.reshape(N_DEV, up_blk, d)]
        pending["ffn"] = _exchange_start(blocks, ["scatter"] * 2, "grads_ffn_start")
        return pending["ffn"][-1][0:1, 0:1]

    def out_grad_ready(gw_out):
        pending["out"] = _exchange_start([gw_out.reshape(N_DEV, d // N_DEV, d)], ["scatter"], "grads_out_start")
        return pending["out"][-1][0:1, 0:1]

    loss_tile, grad_x, gr = _local_step(
        x[0], loss_target[0], meta_f, attn_norm_g + rest[-1][0:1, 0:1], fox_forget_b, ret_norm_g, ffn_norm_g,
        conv_w8, conv_b, final_norm_g.reshape(1, d), first_weight, late_weights, ffn_grads_ready, out_grad_ready,
        in_grad_ready)

    small = [loss_tile, gr["attn_g"], gr["fox_b"], gr["ret_g"], gr["ffn_g"], gr["conv_b"], gr["final_g"],
             gr["meta"], gr["conv_w"]]
    r_small = _exchange(small, ["gather"] * len(small), "exchange_small")
    r_down, r_up = _exchange_wait(pending["ffn"], ["scatter"] * 2, r_small[0], "grads_ffn_wait")
    (r_out,) = _exchange_wait(pending["out"], ["scatter"], r_small[0], "grads_out_wait")
    g_w_out = _sum_slots(r_out, "sum_w_out", d // N_DEV)
    g_w_up = _sum_slots(r_up, "sum_w_up", up_blk).T
    g_w_down = _sum_slots(r_down, "sum_w_down", down_blk)
    (loss_all, g_attn, g_fox_b128, g_ret, g_ffn, g_conv_b, g_final, g_meta_full, g_cw_full) = _sum_slots_small(
        r_small, "sum_small")
    loss = loss_all[0, 0]
    g_fox_b = g_fox_b128[:, :FOX_HEADS]
    g_meta_loc = lax.dynamic_slice(g_meta_full, (0, me * (d // N_DEV)), (N_META, d // N_DEV))
    g_cw_loc = lax.dynamic_slice(g_cw_full, (0, me * cw_blk), (3, cw_blk))

    d_w_out, m_w_out_n, v_w_out_n = _adamw(w_out, g_w_out, m_w_out, v_w_out, "adamw_w_out", 128)
    d_w_up, m_w_up_n, v_w_up_n = _adamw(w_up, g_w_up, m_w_up, v_w_up, "adamw_w_up", 128)
    d_w_down, m_w_down_n, v_w_down_n = _adamw(w_down, g_w_down, m_w_down, v_w_down, "adamw_w_down", down_blk)
    (r_in,) = _exchange_wait(pending["in"], ["scatter"], d_w_up, "grads_in_wait")
    g_w_in = _sum_slots(r_in, "sum_w_in", in_blk_pad)[:in_blk].T
    d_w_in, m_w_in_n, v_w_in_n = _adamw(w_in, g_w_in, m_w_in, v_w_in, "adamw_w_in", 128)
    row = lambda a: a.reshape(1, d)
    sm_grads = [g_meta_loc, g_attn, g_fox_b, g_ret, g_ffn, g_cw_loc, g_conv_b, g_final]
    sm_w = [meta_tokens, attn_norm_g, fox_forget_b, ret_norm_g, ffn_norm_g, conv_w[0], conv_b, row(final_norm_g)]
    sm_m = [m_meta_tokens, m_attn_norm_g, m_fox_forget_b, m_ret_norm_g, m_ffn_norm_g, m_conv_w[0], m_conv_b,
            row(m_final_norm_g)]
    sm_v = [v_meta_tokens, v_attn_norm_g, v_fox_forget_b, v_ret_norm_g, v_ffn_norm_g, v_conv_w[0], v_conv_b,
            row(v_final_norm_g)]
    dl, ml, vl = [lst[:7] + [lst[7].reshape(d)] for lst in _adamw_small(sm_w, sm_grads, sm_m, sm_v, "adamw_small")]

    def by_weight(meta_, attn_, w_in_, fox_, ret_, w_out_, ffn_, w_up_, cw_, cb_, w_down_, final_):
        return (meta_, attn_, w_in_, fox_, ret_, w_out_, ffn_, w_up_, cw_[None], cb_, w_down_, final_)

    grads_out = by_weight(g_meta_loc, g_attn, g_w_in[None], g_fox_b, g_ret, g_w_out[None], g_ffn, g_w_up[None], g_cw_loc,
                          g_conv_b, g_w_down[None], g_final.reshape(d))
    delta_out = by_weight(dl[0], dl[1], d_w_in, dl[2], dl[3], d_w_out, dl[4], d_w_up, dl[5], dl[6], d_w_down, dl[7])
    m_out = by_weight(ml[0], ml[1], m_w_in_n, ml[2], ml[3], m_w_out_n, ml[4], m_w_up_n, ml[5], ml[6], m_w_down_n, ml[7])
    v_out = by_weight(vl[0], vl[1], v_w_in_n, vl[2], vl[3], v_w_out_n, vl[4], v_w_up_n, vl[5], vl[6], v_w_down_n, vl[7])
    return (loss, grad_x[None]) + grads_out + delta_out + m_out + v_out
```

```python
import numpy as np
import jax
import jax.numpy as jnp
from jax import lax
from jax.experimental import pallas as pl
from jax.experimental.pallas import tpu as pltpu

F32 = jnp.float32
BF16 = jnp.bfloat16

D_MODEL = 1024
N_META = 16
N_PAD = 112
PREFIX = 128
RET_HEADS = 4
RET_DK = 64
RET_DV = 128
FOX_HEADS = 8
FOX_DH = 64
D_FF = 2816
ROPE_BASE = 10000.0
EPS = 1e-6
NEG = -1e30
RET_QK = RET_HEADS * RET_DK
RET_V = RET_HEADS * RET_DV
FOX_W = FOX_HEADS * FOX_DH
IN_WIDTH = 2 * RET_QK + 2 * RET_V + 3 * FOX_W + FOX_HEADS
IN_PAD = 3200
FF_COL_BLOCK = (IN_WIDTH - FOX_HEADS) // 128
QK_SCALE = 0.125

ADAM_LR = 0.001
ADAM_B1 = 0.9
ADAM_B2 = 0.999
ADAM_EPS = 1e-08
ADAM_WD = 0.01
ADAM_STEP = 10

N_DEV = 8
LANE = 128
ROW_TILE = 128
TOK_TILE = 384

NN = (((1,), (0,)), ((), ()))
NT = (((1,), (1,)), ((), ()))
TN = (((0,), (0,)), ((), ()))


def _pcall(body, **kw):
    return pl.pallas_call(body, **kw)


def _params(*sem):
    return pltpu.CompilerParams(dimension_semantics=sem)


def _dot(a, b, dims=NN):
    return lax.dot_general(a, b, dims, preferred_element_type=F32)


def _sigmoid(x):
    return 0.5 * jnp.tanh(0.5 * x) + 0.5


def _matmul(a, b, *, mode, grid, a_spec, b_spec, o_spec, out_shape, name, add=None, add_spec=None, after=None):
    dims = {"nn": NN, "nt": NT, "tn": TN}[mode]
    nk = grid[2]
    has_add = add is not None
    a_list, b_list = (list(a), list(b)) if isinstance(a, (list, tuple)) else ([a], [b])
    a_specs, b_specs = (list(a_spec), list(b_spec)) if isinstance(a_spec, (list, tuple)) else ([a_spec], [b_spec])
    nt = len(a_list)
    n_in = 2 * nt + int(has_add) + int(after is not None)

    def body(*refs):
        a_refs, b_refs = refs[:nt], refs[nt:2 * nt]
        add_ref = refs[2 * nt] if has_add else None
        o_ref = refs[n_in]
        part = _dot(a_refs[0][...].astype(BF16), b_refs[0][...].astype(BF16), dims)
        for ar, br in zip(a_refs[1:], b_refs[1:]):
            part = part + _dot(ar[...].astype(BF16), br[...].astype(BF16), dims)

        def finish(acc):
            if has_add:
                acc = acc + add_ref[...]
            o_ref[...] = acc.astype(o_ref.dtype)

        if nk == 1:
            finish(part)
        else:
            acc_ref = refs[-1]
            k = pl.program_id(2)

            @pl.when(k == 0)
            def _():
                acc_ref[...] = part

            @pl.when(k > 0)
            def _():
                acc_ref[...] += part

            @pl.when(k == nk - 1)
            def _():
                finish(acc_ref[...])

    in_specs = a_specs + b_specs + ([add_spec] if has_add else [])
    args = tuple(a_list) + tuple(b_list) + ((add,) if has_add else ())
    if after is not None:
        in_specs, args = in_specs + [pl.BlockSpec(memory_space=pl.ANY)], args + (after,)
    scratch = [] if nk == 1 else [pltpu.VMEM(tuple(d for d in o_spec.block_shape if d is not None), F32)]
    return _pcall(
        body, name=name, grid=grid, in_specs=in_specs, out_specs=o_spec, out_shape=out_shape,
        scratch_shapes=scratch, compiler_params=_params("parallel", "parallel", "arbitrary"),
    )(*args)


def _mm_simple(a, b, *, mode, tm, tn, tk, out_dtype, name, add=None, after=None):
    if mode == "tn":
        K, M = a.shape
    else:
        M, K = a.shape
    N = b.shape[0] if mode == "nt" else b.shape[1]
    grid = (M // tm, N // tn, K // tk)
    resident = dict(pipeline_mode=pl.Buffered(1)) if (tn == N and tk == K) else {}
    a_spec = pl.BlockSpec((tk, tm), lambda i, j, k: (k, i)) if mode == "tn" else pl.BlockSpec((tm, tk), lambda i, j, k: (i, k))
    b_spec = (pl.BlockSpec((tn, tk), lambda i, j, k: (j, k), **resident) if mode == "nt"
              else pl.BlockSpec((tk, tn), lambda i, j, k: (k, j), **resident))
    o_spec = pl.BlockSpec((tm, tn), lambda i, j, k: (i, j))
    return _matmul(a, b, mode=mode, grid=grid, a_spec=a_spec, b_spec=b_spec, o_spec=o_spec,
                   out_shape=jax.ShapeDtypeStruct((M, N), out_dtype), name=name, add=add,
                   add_spec=o_spec if add is not None else None, after=after)


def _matmul_rows(a_list, a_specs, b_list, b_specs, extras, extra_specs, out_specs, out_shape, epilogue, *,
                 mode, steps, name, after=None):
    dims = {"nn": NN, "nt": NT}[mode]
    nt, ne = len(a_list), len(extras)
    n_in = 2 * nt + ne + int(after is not None)

    def body(*refs):
        acc = _dot(refs[0][...].astype(BF16), refs[nt][...].astype(BF16), dims)
        for k in range(1, nt):
            acc = acc + _dot(refs[k][...].astype(BF16), refs[nt + k][...].astype(BF16), dims)
        epilogue(pl.program_id(0), acc, refs[2 * nt:2 * nt + ne], refs[n_in:])

    in_specs = list(a_specs) + list(b_specs) + list(extra_specs)
    args = tuple(a_list) + tuple(b_list) + tuple(extras)
    if after is not None:
        in_specs, args = in_specs + [pl.BlockSpec(memory_space=pl.ANY)], args + (after,)
    return _pcall(body, name=name, grid=(steps,), in_specs=in_specs, out_specs=out_specs, out_shape=out_shape,
                  compiler_params=_params("arbitrary"))(*args)


def _rms_bwd_tile(dy, x, gain, dres):
    r = lax.rsqrt(jnp.mean(x * x, axis=-1, keepdims=True) + EPS)
    xhat = x * r
    u = dy * gain
    return dres + r * (u - xhat * jnp.mean(u * xhat, axis=-1, keepdims=True)), jnp.sum(dy * xhat, axis=0, keepdims=True)


def _loss_tile(i, x, tgt, gain):
    d = x.shape[-1]
    r = lax.rsqrt(jnp.mean(x * x, axis=-1, keepdims=True) + EPS)
    xhat = x * r
    counted = (i * TOK_TILE + lax.broadcasted_iota(jnp.int32, (TOK_TILE, 1), 0)) >= PREFIX
    err = jnp.where(counted, xhat * gain - tgt, 0.0)
    dy = err * (1.0 / d)
    u = dy * gain
    dh = r * (u - xhat * jnp.mean(u * xhat, axis=-1, keepdims=True))
    return 0.5 * jnp.sum(jnp.mean(err * err, axis=-1, keepdims=True)), dh, jnp.sum(dy * xhat, axis=0, keepdims=True)


def _accumulate(ref, i, part):
    @pl.when(i == 0)
    def _():
        ref[...] = part

    @pl.when(i > 0)
    def _():
        ref[...] += part


def _prep_norm(x, meta, gain, name):
    seq, d = x.shape
    t = seq + PREFIX

    def body(xa_ref, xb_ref, xc_ref, meta_ref, g_ref, h_ref, n_ref):
        i = pl.program_id(0)

        @pl.when(i == 0)
        def _():
            h_ref[0:N_PAD, :] = jnp.zeros((N_PAD, d), F32)
            h_ref[N_PAD:ROW_TILE, :] = meta_ref[...]

        @pl.when(i > 0)
        def _():
            h_ref[0:ROW_TILE, :] = xa_ref[...]

        h_ref[ROW_TILE:2 * ROW_TILE, :] = xb_ref[...]
        h_ref[2 * ROW_TILE:3 * ROW_TILE, :] = xc_ref[...]
        h = h_ref[...]
        r = lax.rsqrt(jnp.mean(h * h, axis=-1, keepdims=True) + EPS)
        n_ref[...] = (h * r * g_ref[...]).astype(BF16)

    return _pcall(
        body, name=name, grid=(t // TOK_TILE,),
        in_specs=_shifted_row_specs(d) + [pl.BlockSpec((N_META, d), lambda i: (0, 0)), pl.BlockSpec((1, d), lambda i: (0, 0))],
        out_specs=[pl.BlockSpec((TOK_TILE, d), lambda i: (i, 0)), pl.BlockSpec((TOK_TILE, d), lambda i: (i, 0))],
        out_shape=[jax.ShapeDtypeStruct((t, d), F32), jax.ShapeDtypeStruct((t, d), BF16)],
        compiler_params=_params("parallel"),
    )(x, x, x, meta, gain)


def _shifted_row_specs(d):
    blocks_per_tile = TOK_TILE // ROW_TILE
    return [pl.BlockSpec((ROW_TILE, d), lambda i, r=r: (jnp.maximum(blocks_per_tile * i + r, 0), 0)) for r in (-1, 0, 1)]


def _rmsnorm(h, gain, name):
    t, d = h.shape

    def body(h_ref, g_ref, n_ref):
        x = h_ref[...]
        r = lax.rsqrt(jnp.mean(x * x, axis=-1, keepdims=True) + EPS)
        n_ref[...] = (x * r * g_ref[...]).astype(BF16)

    return _pcall(
        body, name=name, grid=(t // TOK_TILE,),
        in_specs=[pl.BlockSpec((TOK_TILE, d), lambda i: (i, 0)), pl.BlockSpec((1, d), lambda i: (0, 0))],
        out_specs=pl.BlockSpec((TOK_TILE, d), lambda i: (i, 0)),
        out_shape=jax.ShapeDtypeStruct((t, d), BF16),
        compiler_params=_params("parallel"),
    )(h, gain)


def _rmsnorm_bwd(dn, h, gain, dres, name):
    t, d = h.shape

    def body(dn_ref, h_ref, g_ref, dres_ref, dh_ref, gg_ref):
        i = pl.program_id(0)
        x = h_ref[...]
        r = lax.rsqrt(jnp.mean(x * x, axis=-1, keepdims=True) + EPS)
        xhat = x * r
        dy = dn_ref[...]
        u = dy * g_ref[...]
        dh_ref[...] = dres_ref[...] + r * (u - xhat * jnp.mean(u * xhat, axis=-1, keepdims=True))
        part = jnp.sum(dy * xhat, axis=0, keepdims=True)

        @pl.when(i == 0)
        def _():
            gg_ref[...] = part

        @pl.when(i > 0)
        def _():
            gg_ref[...] += part

    return _pcall(
        body, name=name, grid=(t // TOK_TILE,),
        in_specs=[pl.BlockSpec((TOK_TILE, d), lambda i: (i, 0)), pl.BlockSpec((TOK_TILE, d), lambda i: (i, 0)),
                  pl.BlockSpec((1, d), lambda i: (0, 0)), pl.BlockSpec((TOK_TILE, d), lambda i: (i, 0))],
        out_specs=[pl.BlockSpec((TOK_TILE, d), lambda i: (i, 0)), pl.BlockSpec((1, d), lambda i: (0, 0))],
        out_shape=[jax.ShapeDtypeStruct((t, d), F32), jax.ShapeDtypeStruct((1, d), F32)],
        compiler_params=_params("arbitrary"),
    )(dn, h, gain, dres)


def _loss_bwd(h2, target, gain, name):
    t, d = h2.shape

    def body(h_ref, ta_ref, tb_ref, tc_ref, g_ref, loss_ref, dh_ref, dhb_ref, gg_ref):
        i = pl.program_id(0)

        @pl.when(i == 0)
        def _():
            loss_ref[...] = jnp.zeros_like(loss_ref)
            gg_ref[...] = jnp.zeros_like(gg_ref)

        x = h_ref[...]
        r = lax.rsqrt(jnp.mean(x * x, axis=-1, keepdims=True) + EPS)
        xhat = x * r
        g = g_ref[...]
        tgt = jnp.concatenate([ta_ref[...], tb_ref[...], tc_ref[...]], axis=0)
        counted = (i * TOK_TILE + lax.broadcasted_iota(jnp.int32, (TOK_TILE, 1), 0)) >= PREFIX
        err = jnp.where(counted, xhat * g - tgt, 0.0)
        loss_ref[...] += 0.5 * jnp.sum(jnp.mean(err * err, axis=-1, keepdims=True))
        dy = err * (1.0 / d)
        u = dy * g
        dh = r * (u - xhat * jnp.mean(u * xhat, axis=-1, keepdims=True))
        dh_ref[...] = dh
        dhb_ref[...] = dh.astype(BF16)
        gg_ref[...] += jnp.sum(dy * xhat, axis=0, keepdims=True)

    tile = pl.BlockSpec((TOK_TILE, d), lambda i: (i, 0))
    return _pcall(
        body, name=name, grid=(t // TOK_TILE,),
        in_specs=[tile] + _shifted_row_specs(d) + [pl.BlockSpec((1, d), lambda i: (0, 0))],
        out_specs=[pl.BlockSpec((8, LANE), lambda i: (0, 0)), tile, tile, pl.BlockSpec((1, d), lambda i: (0, 0))],
        out_shape=[jax.ShapeDtypeStruct((8, LANE), F32), jax.ShapeDtypeStruct((t, d), F32),
                   jax.ShapeDtypeStruct((t, d), BF16), jax.ShapeDtypeStruct((1, d), F32)],
        compiler_params=_params("arbitrary"),
    )(h2, target, target, target, gain)


def _ret_consts(bk):
    gam = 1.0 - 2.0 ** (-5.0 - np.arange(RET_HEADS))
    n = np.arange(bk)
    same_or_earlier_chunk = (n[None, :] // 64) <= (n[:, None] // 64)
    w = gam[:, None, None] ** np.abs(n[:, None] - n[None, :])[None] * same_or_earlier_chunk[None]
    wq = gam[:, None] ** (n[None, :] + 1.0)
    wk = gam[:, None] ** (bk - 1.0 - n[None, :])
    mask = (np.arange(RET_QK)[None, :] // RET_DK) == np.arange(RET_HEADS)[:, None]
    return (jnp.asarray(w, F32), jnp.asarray(wq[:, :, None], F32), jnp.asarray(wk[:, :, None], F32),
            jnp.asarray(mask[:, None, :], F32), [float(g ** bk) for g in gam])


def _rope_tables(t):
    half = RET_DK // 2
    inv = 1.0 / (ROPE_BASE ** (jnp.arange(half, dtype=F32) / half))
    ang = jnp.arange(t).astype(F32)[:, None] * inv[None, :]
    cos, sin = jnp.cos(ang), jnp.sin(ang)
    return (jnp.tile(jnp.concatenate([cos, cos], axis=1), (1, RET_HEADS)),
            jnp.tile(jnp.concatenate([-sin, sin], axis=1), (1, RET_HEADS)))


def _swap_halves(x):
    outs = []
    for s in range(x.shape[1] // LANE):
        xs = x[:, LANE * s:LANE * (s + 1)]
        lane = lax.broadcasted_iota(jnp.int32, xs.shape, 1)
        outs.append(jnp.where((lane & 32) == 0, pltpu.roll(xs, LANE - 32, axis=1), pltpu.roll(xs, 32, axis=1)))
    return outs[0] if len(outs) == 1 else jnp.concatenate(outs, axis=1)


def _rope(x, cos, sin_signed):
    return x * cos + _swap_halves(x) * sin_signed


def _rope_t(dx, cos, sin_signed):
    return dx * cos + _swap_halves(dx * sin_signed)


def _ret_fwd(proj, cos, sin, gain, name):
    t = proj.shape[0]
    bk = TOK_TILE
    nb = t // bk
    w, wq, wk, mask, g_blk = _ret_consts(bk)

    def body(q_ref, k_ref, v_ref, rg_ref, cos_ref, sin_ref, w_ref, wq_ref, wk_ref, mask_ref, gain_ref,
             opre_ref, og_ref, st_ref, r_ref):
        i = pl.program_id(0)

        @pl.when(i == 0)
        def _():
            r_ref[...] = jnp.zeros_like(r_ref)

        c, s = cos_ref[...], sin_ref[...]
        valid = ((i * bk + lax.broadcasted_iota(jnp.int32, (bk, 1), 0)) >= N_PAD).astype(F32)
        qr = _rope(q_ref[...], c, s)
        kr = _rope(k_ref[...], c, s) * QK_SCALE * valid
        kb = kr.astype(BF16)
        for h in range(RET_HEADS):
            hm = mask_ref[h]
            cols = slice(RET_DV * h, RET_DV * (h + 1))
            vh = v_ref[:, cols].astype(BF16)
            r_prev = r_ref[h]
            st_ref[0, h] = r_prev
            sm = _dot((qr * hm).astype(BF16), kb, NT) * w_ref[h]
            o = _dot(sm.astype(BF16), vh) + _dot((qr * (hm * wq_ref[h])).astype(BF16), r_prev.astype(BF16))
            r_ref[h] = g_blk[h] * r_prev + _dot((kr * wk_ref[h]).astype(BF16), vh, TN)
            opre_ref[:, cols] = o
            rstd = lax.rsqrt(jnp.mean(o * o, axis=-1, keepdims=True) + EPS)
            rg = rg_ref[:, cols]
            og_ref[:, cols] = (o * rstd * gain_ref[:, cols] * (rg * _sigmoid(rg))).astype(BF16)

    full = lambda shape: pl.BlockSpec(shape, lambda i: (0,) * len(shape))
    return _pcall(
        body, name=name, grid=(nb,),
        in_specs=[pl.BlockSpec((bk, RET_QK), lambda i: (i, 0)), pl.BlockSpec((bk, RET_QK), lambda i: (i, 1)),
                  pl.BlockSpec((bk, RET_V), lambda i: (i, 1)), pl.BlockSpec((bk, RET_V), lambda i: (i, 2)),
                  pl.BlockSpec((bk, RET_QK), lambda i: (i, 0)), pl.BlockSpec((bk, RET_QK), lambda i: (i, 0)),
                  full((RET_HEADS, bk, bk)), full((RET_HEADS, bk, 1)), full((RET_HEADS, bk, 1)),
                  full((RET_HEADS, 1, RET_QK)), full((1, RET_V))],
        out_specs=[pl.BlockSpec((bk, RET_V), lambda i: (i, 0)), pl.BlockSpec((bk, RET_V), lambda i: (i, 0)),
                   pl.BlockSpec((1, RET_HEADS, RET_QK, RET_DV), lambda i: (i, 0, 0, 0))],
        out_shape=[jax.ShapeDtypeStruct((t, RET_V), F32), jax.ShapeDtypeStruct((t, RET_V + FOX_W), BF16),
                   jax.ShapeDtypeStruct((nb, RET_HEADS, RET_QK, RET_DV), F32)],
        scratch_shapes=[pltpu.VMEM((RET_HEADS, RET_QK, RET_DV), F32)],
        compiler_params=_params("arbitrary"),
    )(proj, proj, proj, proj, cos, sin, w, wq, wk, mask, gain)


def _ret_bwd(proj, cos, sin, gain, dmixed, opre, states, name):
    t = proj.shape[0]
    bk = TOK_TILE
    nb = t // bk
    w, wq, wk, mask, g_blk = _ret_consts(bk)
    v0, g0 = 2 * RET_QK, 2 * RET_QK + RET_V

    def body(q_ref, k_ref, v_ref, rg_ref, cos_ref, sin_ref, w_ref, wq_ref, wk_ref, mask_ref, gain_ref,
             dog_ref, opre_ref, st_ref, dp_ref, gg_ref, dr_ref):
        step = pl.program_id(0)
        i = nb - 1 - step

        @pl.when(step == 0)
        def _():
            dr_ref[...] = jnp.zeros_like(dr_ref)
            gg_ref[...] = jnp.zeros_like(gg_ref)

        c, s = cos_ref[...], sin_ref[...]
        valid = ((i * bk + lax.broadcasted_iota(jnp.int32, (bk, 1), 0)) >= N_PAD).astype(F32)
        qr = _rope(q_ref[...], c, s)
        kr = _rope(k_ref[...], c, s) * QK_SCALE * valid
        kb = kr.astype(BF16)
        dqr = jnp.zeros((bk, RET_QK), F32)
        dkr = jnp.zeros((bk, RET_QK), F32)
        for h in range(RET_HEADS):
            hm = mask_ref[h]
            cols = slice(RET_DV * h, RET_DV * (h + 1))
            vh = v_ref[:, cols].astype(BF16)
            o = opre_ref[:, cols]
            rstd = lax.rsqrt(jnp.mean(o * o, axis=-1, keepdims=True) + EPS)
            xhat = o * rstd
            rg = rg_ref[:, cols]
            sg = _sigmoid(rg)
            gate = rg * sg
            gn = gain_ref[:, cols]
            dog = dog_ref[:, cols]
            dp_ref[:, g0 + RET_DV * h:g0 + RET_DV * (h + 1)] = (
                dog * xhat * gn * (sg * (1.0 + rg * (1.0 - sg)))).astype(BF16)
            gg_ref[:, cols] += jnp.sum(dog * xhat * gate, axis=0, keepdims=True)
            dxh = dog * gn * gate
            do = (rstd * (dxh - xhat * jnp.mean(dxh * xhat, axis=-1, keepdims=True))).astype(BF16)
            qm = (qr * hm).astype(BF16)
            qw = (qr * (hm * wq_ref[h])).astype(BF16)
            kw = (kr * wk_ref[h]).astype(BF16)
            wh = w_ref[h]
            sm = (_dot(qm, kb, NT) * wh).astype(BF16)
            ds = (_dot(do, vh, NT) * wh).astype(BF16)
            dr = dr_ref[h]
            drb = dr.astype(BF16)
            dp_ref[:, v0 + RET_DV * h:v0 + RET_DV * (h + 1)] = (_dot(sm, do, TN) + _dot(kw, drb)).astype(BF16)
            dqr = dqr + _dot(ds, kb) * hm + _dot(do, st_ref[0, h].astype(BF16), NT) * (hm * wq_ref[h])
            dkr = dkr + _dot(ds, qm, TN) + _dot(vh, drb, NT) * wk_ref[h]
            dr_ref[h] = g_blk[h] * dr + _dot(qw, do, TN)
        dp_ref[:, 0:RET_QK] = _rope_t(dqr, c, s).astype(BF16)
        dp_ref[:, RET_QK:2 * RET_QK] = _rope_t(dkr * (QK_SCALE * valid), c, s).astype(BF16)

    full = lambda shape: pl.BlockSpec(shape, lambda i: (0,) * len(shape))
    rev = lambda col: (lambda i: (nb - 1 - i, col))
    return _pcall(
        body, name=name, grid=(nb,),
        in_specs=[pl.BlockSpec((bk, RET_QK), rev(0)), pl.BlockSpec((bk, RET_QK), rev(1)),
                  pl.BlockSpec((bk, RET_V), rev(1)), pl.BlockSpec((bk, RET_V), rev(2)),
                  pl.BlockSpec((bk, RET_QK), rev(0)), pl.BlockSpec((bk, RET_QK), rev(0)),
                  full((RET_HEADS, bk, bk)), full((RET_HEADS, bk, 1)), full((RET_HEADS, bk, 1)),
                  full((RET_HEADS, 1, RET_QK)), full((1, RET_V)),
                  pl.BlockSpec((bk, RET_V), rev(0)), pl.BlockSpec((bk, RET_V), rev(0)),
                  pl.BlockSpec((1, RET_HEADS, RET_QK, RET_DV), lambda i: (nb - 1 - i, 0, 0, 0))],
        out_specs=[pl.BlockSpec((bk, g0 + RET_V), rev(0)), pl.BlockSpec((1, RET_V), lambda i: (0, 0))],
        out_shape=[jax.ShapeDtypeStruct((t, IN_PAD), BF16), jax.ShapeDtypeStruct((1, RET_V), F32)],
        scratch_shapes=[pltpu.VMEM((RET_HEADS, RET_QK, RET_DV), F32)],
        compiler_params=_params("arbitrary"),
    )(proj, proj, proj, proj, cos, sin, w, wq, wk, mask, gain, dmixed, opre, states)


def _forget_cumsum(proj, bias, name):
    t = proj.shape[0]
    rt = TOK_TILE
    nb = t // rt
    tril = jnp.asarray(np.tril(np.ones((rt, rt))), F32)

    def body(z_ref, b_ref, tril_ref, c_ref, carry_ref):
        i = pl.program_id(0)

        @pl.when(i == 0)
        def _():
            carry_ref[...] = jnp.zeros_like(carry_ref)

        z = z_ref[...] + b_ref[...]
        logf = jnp.minimum(z, 0.0) - jnp.log(1.0 + jnp.exp(-jnp.abs(z)))
        c = lax.dot_general(tril_ref[...], logf, NN, precision=lax.Precision.HIGHEST,
                            preferred_element_type=F32) + carry_ref[...]
        c_ref[...] = c
        carry_ref[...] = c[rt - 1:rt, :]

    return _pcall(
        body, name=name, grid=(nb,),
        in_specs=[pl.BlockSpec((rt, LANE), lambda i: (i, FF_COL_BLOCK)), pl.BlockSpec((1, LANE), lambda i: (0, 0)),
                  pl.BlockSpec((rt, rt), lambda i: (0, 0))],
        out_specs=pl.BlockSpec((rt, LANE), lambda i: (i, 0)),
        out_shape=jax.ShapeDtypeStruct((t, LANE), F32),
        scratch_shapes=[pltpu.VMEM((1, LANE), F32)],
        compiler_params=_params("arbitrary"),
    )(proj, bias, tril)


def _forget_cumsum_bwd(proj, bias, drs, dcs, dproj, name):
    t = proj.shape[0]
    rt = TOK_TILE
    nb = t // rt
    triu = jnp.asarray(np.triu(np.ones((rt, rt))), F32)

    def body(z_ref, b_ref, triu_ref, drs_ref, dcs_ref, dproj_in, dz_ref, gb_ref, carry_ref):
        step = pl.program_id(0)

        @pl.when(step == 0)
        def _():
            carry_ref[...] = jnp.zeros_like(carry_ref)
            gb_ref[...] = jnp.zeros_like(gb_ref)

        dlogf = lax.dot_general(triu_ref[...], drs_ref[...] - dcs_ref[...], NN, precision=lax.Precision.HIGHEST,
                                preferred_element_type=F32) + carry_ref[...]
        carry_ref[...] = dlogf[0:1, :]
        z = z_ref[...] + b_ref[...]
        is_head = lax.broadcasted_iota(jnp.int32, (rt, LANE), 1) < FOX_HEADS
        dz = jnp.where(is_head, dlogf / (1.0 + jnp.exp(z)), 0.0)
        dz_ref[...] = dz.astype(BF16)
        gb_ref[...] += jnp.sum(dz, axis=0, keepdims=True)

    return _pcall(
        body, name=name, grid=(nb,),
        in_specs=[pl.BlockSpec((rt, LANE), lambda i: (nb - 1 - i, FF_COL_BLOCK)),
                  pl.BlockSpec((1, LANE), lambda i: (0, 0)),
                  pl.BlockSpec((rt, rt), lambda i: (0, 0)),
                  pl.BlockSpec((rt, LANE), lambda i: (nb - 1 - i, 0)),
                  pl.BlockSpec((rt, LANE), lambda i: (nb - 1 - i, 0)),
                  pl.BlockSpec(memory_space=pl.ANY)],
        out_specs=[pl.BlockSpec((rt, LANE), lambda i: (nb - 1 - i, FF_COL_BLOCK)),
                   pl.BlockSpec((1, LANE), lambda i: (0, 0))],
        out_shape=[jax.ShapeDtypeStruct(dproj.shape, BF16), jax.ShapeDtypeStruct((1, LANE), F32)],
        input_output_aliases={5: 0},
        scratch_shapes=[pltpu.VMEM((1, LANE), F32)],
        compiler_params=_params("arbitrary"),
    )(proj, bias, triu, drs, dcs, dproj)


FOX_PAIRS = FOX_HEADS // 2
L_ONE_Q = FOX_DH
L_ONE_K = FOX_DH + 3
L_LSE = FOX_DH + 4


def _split3(x):
    hi = x.astype(BF16).astype(F32)
    r = x - hi
    mid = r.astype(BF16).astype(F32)
    return hi, mid, r - mid


def _head_to_low(slab, e):
    return slab if e == 0 else pltpu.roll(slab, FOX_DH, axis=1)


def _pair(a, b, low):
    return jnp.where(low, a, pltpu.roll(b, FOX_DH, axis=1))


def _fox_prep(proj, c, name):
    t = proj.shape[0]
    tq = TOK_TILE

    def body(p_ref, c_ref, qa_ref, ka_ref, va_ref):
        i = pl.program_id(0)
        lane = lax.broadcasted_iota(jnp.int32, (tq, LANE), 1)
        low = lane < FOX_DH
        live = (i * tq + lax.broadcasted_iota(jnp.int32, (tq, 1), 0)) >= N_PAD
        q_tail = jnp.where(lane < L_ONE_Q + 3, 1.0, 0.0)
        k_ones = (lane >= L_ONE_K) & (lane < L_ONE_K + 4)
        v_tail = jnp.where(lane < FOX_DH + 2, 1.0, 0.0)
        for pair in range(FOX_PAIRS):
            base = 3 * LANE * pair
            for e in range(2):
                h = 2 * pair + e
                q = _head_to_low(p_ref[:, base:base + LANE], e)
                k = _head_to_low(p_ref[:, base + LANE:base + 2 * LANE], e)
                v = _head_to_low(p_ref[:, base + 2 * LANE:base + 3 * LANE], e)
                hi, mid, lo = _split3(jnp.where(live, -c_ref[:, h:h + 1], NEG))
                ka = jnp.where(low, k, jnp.where(k_ones, 1.0, 0.0))
                ka = jnp.where(lane == L_ONE_Q, hi, jnp.where(lane == L_ONE_Q + 1, mid, jnp.where(lane == L_ONE_Q + 2, lo, ka)))
                qa_ref[h] = jnp.where(low, q * QK_SCALE, q_tail).astype(BF16)
                ka_ref[h] = ka.astype(BF16)
                va_ref[h] = jnp.where(low, v, v_tail).astype(BF16)

    out = jax.ShapeDtypeStruct((FOX_HEADS, t, LANE), BF16)
    ospec = pl.BlockSpec((FOX_HEADS, tq, LANE), lambda i: (0, i, 0))
    return _pcall(
        body, name=name, grid=(t // tq,),
        in_specs=[pl.BlockSpec((tq, 3 * FOX_W), lambda i: (i, 1)), pl.BlockSpec((tq, LANE), lambda i: (i, 0))],
        out_specs=[ospec, ospec, ospec], out_shape=[out, out, out],
        compiler_params=_params("parallel"),
    )(proj, c)


STEP_PAIRS = 2
STEP_HEADS = 2 * STEP_PAIRS
FOX_GROUPS = FOX_PAIRS // STEP_PAIRS
FWD_PAIRS = 4
FWD_HEADS = 2 * FWD_PAIRS
FWD_GROUPS = FOX_PAIRS // FWD_PAIRS


def _blockdiag(a, b):
    z = jnp.zeros_like(a)
    return jnp.concatenate([jnp.concatenate([a, z], axis=1), jnp.concatenate([z, b], axis=1)], axis=0)


def _fox_fwd(qa, ka, va, mixed, name):
    nh, nq, tq, _ = qa.shape
    t = nq * tq

    def body(qa_ref, ka_ref, va_ref, mixed_in, mixed_ref, o_ref, lse_ref):
        i = pl.program_id(1)
        lane = lax.broadcasted_iota(jnp.int32, (tq, LANE), 1)
        causal = lax.broadcasted_iota(jnp.int32, (tq, tq), 1) <= lax.broadcasted_iota(jnp.int32, (tq, tq), 0)
        qps = [jnp.concatenate([qa_ref[2 * c], qa_ref[2 * c + 1]], axis=1) for c in range(FWD_PAIRS)]

        def logits(j):
            return [_dot(qps[c], _blockdiag(ka_ref[2 * c, j], ka_ref[2 * c + 1, j]), NT) for c in range(FWD_PAIRS)]

        def update(j, scores, carry, diagonal):
            new = []
            for c in range(FWD_PAIRS):
                ms, acc = carry[c]
                ps, ms_new, alphas = [], [], []
                for e in range(2):
                    s = scores[c][:, e * tq:(e + 1) * tq]
                    if diagonal:
                        s = jnp.where(causal, s, NEG)
                    m_new = jnp.maximum(ms[e], jnp.max(s, axis=-1, keepdims=True))
                    ps.append(jnp.exp(s - m_new).astype(BF16))
                    ms_new.append(m_new)
                    alphas.append(jnp.broadcast_to(jnp.exp(ms[e] - m_new), (tq, LANE)))
                pv = _dot(jnp.concatenate(ps, axis=1), _blockdiag(va_ref[2 * c, j], va_ref[2 * c + 1, j]))
                new.append((tuple(ms_new), jnp.concatenate(alphas, axis=1) * acc + pv))
            return tuple(new)

        m0 = jnp.full((tq, 1), NEG, F32)
        init = tuple(((m0, m0), jnp.zeros((tq, 2 * LANE), F32)) for _ in range(FWD_PAIRS))
        carry = lax.fori_loop(0, i, lambda j, cr: update(j, logits(j), cr, False), init)
        o_pairs = []
        lse = jnp.zeros((tq, LANE), F32)
        for c, (ms, acc) in enumerate(update(i, logits(i), carry, True)):
            outs = []
            for e in range(2):
                half = acc[:, e * LANE:(e + 1) * LANE]
                l = half[:, FOX_DH:FOX_DH + 1]
                outs.append(half / l)
                lse = jnp.where(lane == 2 * c + e, ms[e] + jnp.log(l), lse)
            o_pairs.append(_pair(outs[0], outs[1], lane < FOX_DH))
        o_all = jnp.concatenate(o_pairs, axis=1)
        mixed_ref[...] = o_all.astype(BF16)
        o_ref[...] = o_all
        lse_ref[...] = lse

    width = FWD_PAIRS * LANE
    whole = pl.BlockSpec((FWD_HEADS, nq, tq, LANE), lambda g, i: (g, 0, 0, 0), pipeline_mode=pl.Buffered(1))
    return _pcall(
        body, name=name, grid=(FWD_GROUPS, nq),
        in_specs=[pl.BlockSpec((FWD_HEADS, None, tq, LANE), lambda g, i: (g, i, 0, 0)), whole, whole,
                  pl.BlockSpec(memory_space=pl.ANY)],
        out_specs=[pl.BlockSpec((tq, width), lambda g, i: (i, RET_V // width + g)),
                   pl.BlockSpec((tq, width), lambda g, i: (i, g)),
                   pl.BlockSpec((None, tq, LANE), lambda g, i: (g, i, 0))],
        out_shape=[jax.ShapeDtypeStruct(mixed.shape, BF16), jax.ShapeDtypeStruct((t, FOX_W), F32),
                   jax.ShapeDtypeStruct((FWD_GROUPS, t, LANE), F32)],
        input_output_aliases={3: 0},
        compiler_params=_params("parallel", "parallel"),
    )(qa, ka, va, mixed)


def _fox_prep_bwd(dmixed, o_fox, lse, qa, name):
    t = dmixed.shape[0]
    tq = TOK_TILE

    def body(dm_ref, o_ref, lse_ref, qa_ref, qab_ref, doa_ref):
        i = pl.program_id(0)
        lane = lax.broadcasted_iota(jnp.int32, (tq, LANE), 1)
        low = lane < FOX_DH
        live = (i * tq + lax.broadcasted_iota(jnp.int32, (tq, 1), 0)) >= N_PAD
        for pair in range(FOX_PAIRS):
            cols = slice(LANE * pair, LANE * (pair + 1))
            d_slab = dm_ref[:, cols]
            prod = d_slab * o_ref[:, cols]
            for e in range(2):
                h = 2 * pair + e
                nd = -jnp.sum(jnp.where(low, _head_to_low(prod, e), 0.0), axis=-1, keepdims=True)
                nd_hi = nd.astype(BF16).astype(F32)
                doa = jnp.where(low, _head_to_low(d_slab, e), 0.0)
                doa = jnp.where(lane == FOX_DH, nd_hi, jnp.where(lane == FOX_DH + 1, nd - nd_hi, doa))
                doa_ref[h] = doa.astype(BF16)
                lse_h = lse_ref[h // FWD_HEADS][:, h % FWD_HEADS:h % FWD_HEADS + 1]
                hi, mid, lo = _split3(jnp.where(live, -lse_h, 0.0))
                qab = qa_ref[h].astype(F32)
                qab = jnp.where(lane == L_LSE, hi, jnp.where(lane == L_LSE + 1, mid, jnp.where(lane == L_LSE + 2, lo, qab)))
                qab_ref[h] = qab.astype(BF16)

    out = jax.ShapeDtypeStruct((FOX_HEADS, t, LANE), BF16)
    hspec = pl.BlockSpec((FOX_HEADS, tq, LANE), lambda i: (0, i, 0))
    return _pcall(
        body, name=name, grid=(t // tq,),
        in_specs=[pl.BlockSpec((tq, FOX_W), lambda i: (i, 1)), pl.BlockSpec((tq, FOX_W), lambda i: (i, 0)),
                  pl.BlockSpec((FWD_GROUPS, tq, LANE), lambda i: (0, i, 0)), hspec],
        out_specs=[hspec, hspec], out_shape=[out, out],
        compiler_params=_params("parallel"),
    )(dmixed, o_fox, lse, qa)


def _fox_bwd(qab, doa, ka, va, dproj, name):
    nh, nq, tq, _ = qab.shape
    t = nq * tq
    slab = 3 * LANE * STEP_PAIRS
    group0 = (2 * RET_QK + 2 * RET_V) // slab

    def body(qab_ref, doa_ref, ka_ref, va_ref, dproj_in, dp_ref, drs_ref, dcs_ref, dq_ref):
        g, j = pl.program_id(0), pl.program_id(1)

        @pl.when((g == 0) & (j == 0))
        def _():
            drs_ref[...] = jnp.zeros_like(drs_ref)
            dcs_ref[...] = jnp.zeros_like(dcs_ref)

        @pl.when(j == 0)
        def _():
            dq_ref[...] = jnp.zeros_like(dq_ref)

        lane = lax.broadcasted_iota(jnp.int32, (tq, LANE), 1)
        low = lane < FOX_DH
        key_le_query = lax.broadcasted_iota(jnp.int32, (tq, tq), 0) <= lax.broadcasted_iota(jnp.int32, (tq, tq), 1)

        def by_head(c, a, b, col):
            h = STEP_HEADS * g + 2 * c
            return jnp.where(lane == h, a[:, col:col + 1], jnp.where(lane == h + 1, b[:, col:col + 1], 0.0))

        kbs = [ka_ref[h] for h in range(STEP_HEADS)]
        vbs = [va_ref[h] for h in range(STEP_HEADS)]

        def step(i, carry, diagonal):
            qbs = [qab_ref[h, i] for h in range(STEP_HEADS)]
            dobs = [doa_ref[h, i] for h in range(STEP_HEADS)]
            st = [_dot(kbs[h], qbs[h], NT) for h in range(STEP_HEADS)]
            dpt = [_dot(vbs[h], dobs[h], NT) for h in range(STEP_HEADS)]
            new = []
            for h in range(STEP_HEADS):
                p = jnp.exp(st[h])
                if diagonal:
                    p = jnp.where(key_le_query, p, 0.0)
                ds = (p * dpt[h]).astype(BF16)
                dq_ref[h, i] += _dot(ds, kbs[h], TN)
                dk, dv = carry[h]
                new.append((dk + _dot(ds, qbs[h]), dv + _dot(p.astype(BF16), dobs[h])))
            return tuple(new)

        zero = jnp.zeros((tq, LANE), F32)
        carry = step(j, tuple((zero, zero) for _ in range(STEP_HEADS)), True)
        carry = lax.fori_loop(j + 1, nq, lambda i, cr: step(i, cr, False), carry)
        rows = pl.ds(pl.multiple_of(j * tq, tq), tq)
        for c in range(STEP_PAIRS):
            (dka, dva), (dkb, dvb) = carry[2 * c], carry[2 * c + 1]
            c0 = 3 * LANE * c
            dp_ref[rows, c0 + LANE:c0 + 2 * LANE] = _pair(dka, dkb, low).astype(BF16)
            dp_ref[rows, c0 + 2 * LANE:c0 + 3 * LANE] = _pair(dva, dvb, low).astype(BF16)
            dcs_ref[rows, :] += by_head(c, dka, dkb, L_ONE_Q)

        @pl.when(j == nq - 1)
        def _():
            for c in range(STEP_PAIRS):
                for blk in range(nq):
                    r = slice(blk * tq, (blk + 1) * tq)
                    a, b = dq_ref[2 * c, blk], dq_ref[2 * c + 1, blk]
                    dp_ref[r, 3 * LANE * c:3 * LANE * c + LANE] = (_pair(a, b, low) * QK_SCALE).astype(BF16)
                    drs_ref[r, :] += by_head(c, a, b, L_ONE_K)

    whole = pl.BlockSpec((STEP_HEADS, nq, tq, LANE), lambda g, j: (g, 0, 0, 0), pipeline_mode=pl.Buffered(1))
    blk = pl.BlockSpec((STEP_HEADS, None, tq, LANE), lambda g, j: (g, j, 0, 0))
    sums = pl.BlockSpec((t, LANE), lambda g, j: (0, 0), pipeline_mode=pl.Buffered(1))
    return _pcall(
        body, name=name, grid=(FOX_GROUPS, nq),
        in_specs=[whole, whole, blk, blk, pl.BlockSpec(memory_space=pl.ANY)],
        out_specs=[pl.BlockSpec((t, slab), lambda g, j: (0, group0 + g)), sums, sums],
        out_shape=[jax.ShapeDtypeStruct(dproj.shape, BF16), jax.ShapeDtypeStruct((t, LANE), F32),
                   jax.ShapeDtypeStruct((t, LANE), F32)],
        input_output_aliases={4: 0},
        scratch_shapes=[pltpu.VMEM((STEP_HEADS, nq, tq, LANE), F32)],
        compiler_params=_params("arbitrary", "arbitrary"),
    )(qab, doa, ka, va, dproj)


HALO = 8


def _rows_ext(ref, r0, rows, t, before, after):
    lo, hi = r0 - before, r0 + rows + after
    width = ref.shape[-1]
    parts = []
    if lo < 0:
        parts.append(jnp.zeros((-lo, width), F32))
    parts.append(ref[max(lo, 0):min(hi, t), :].astype(F32))
    if hi > t:
        parts.append(jnp.zeros((hi - t, width), F32))
    return parts[0] if len(parts) == 1 else jnp.concatenate(parts, axis=0)


def _conv_taps(a_ext, r0_ext, cw_ref, cb_ref):
    n = a_ext.shape[0]
    if r0_ext < N_PAD:
        row = r0_ext + lax.broadcasted_iota(jnp.int32, (n, 1), 0)
        a_ext = jnp.where(row >= N_PAD, a_ext, 0.0)
    a1 = pltpu.roll(a_ext, 1, axis=0)
    a2 = pltpu.roll(a_ext, 2, axis=0)
    acc = cb_ref[...] + a2 * cw_ref[0:1, :] + a1 * cw_ref[1:2, :] + a_ext * cw_ref[2:3, :]
    return a_ext, a1, a2, acc


FF_COLS = 256


def _up_conv_fwd(n2, w_up_t, conv_w8, conv_b, name):
    t, d = n2.shape
    f = w_up_t.shape[1]
    rows = TOK_TILE
    starts = list(range(0, t, rows))

    def body(n_ref, wa_ref, wb_ref, cw_ref, cb_ref, up_ref, g_ref):
        wa, wb = wa_ref[...], wb_ref[...]

        def project(r0):
            n_rows = n_ref[r0:r0 + rows, :]
            up_ref[0, r0:r0 + rows, :] = _dot(n_rows, wa, NT)
            up_ref[1, r0:r0 + rows, :] = _dot(n_rows, wb, NT)

        def activate(r0):
            a_ext = _rows_ext(up_ref.at[0], r0, rows, t, HALO, 0)
            _, _, _, acc = _conv_taps(a_ext, r0 - HALO, cw_ref, cb_ref)
            acc = acc[HALO:, :]
            g_ref[r0:r0 + rows, :] = (acc * _sigmoid(acc) * up_ref[1, r0:r0 + rows, :]).astype(BF16)

        project(starts[0])
        for r0, r_next in zip(starts, starts[1:] + [None]):
            if r_next is not None:
                project(r_next)
            activate(r0)

    return _pcall(
        body, name=name, grid=(f // FF_COLS,),
        in_specs=[pl.BlockSpec((t, d), lambda j: (0, 0), pipeline_mode=pl.Buffered(1)),
                  pl.BlockSpec((None, FF_COLS, d), lambda j: (0, j, 0)), pl.BlockSpec((None, FF_COLS, d), lambda j: (1, j, 0)),
                  pl.BlockSpec((8, FF_COLS), lambda j: (0, j)), pl.BlockSpec((1, FF_COLS), lambda j: (0, j))],
        out_specs=[pl.BlockSpec((2, t, FF_COLS), lambda j: (0, 0, j)), pl.BlockSpec((t, FF_COLS), lambda j: (0, j))],
        out_shape=[jax.ShapeDtypeStruct((2, t, f), F32), jax.ShapeDtypeStruct((t, f), BF16)],
        compiler_params=_params("parallel"),
    )(n2, w_up_t, w_up_t, conv_w8, conv_b)


def _dg_conv_bwd(up, conv_w8, conv_b, dh2, w_down, name):
    _, t, f = up.shape
    d = dh2.shape[1]
    rows = TOK_TILE
    starts = list(range(0, t, rows))

    def body(a_ref, b_ref, cw_ref, cb_ref, dh_ref, wd_ref, dup_ref, gcw_ref, gcb_ref, dg_ref):
        wd = wd_ref[...]

        def project(r0):
            dg_ref[r0:r0 + rows, :] = _dot(dh_ref[r0:r0 + rows, :], wd, NT)

        gw = [jnp.zeros((1, FF_COLS), F32) for _ in range(3)]
        gb = jnp.zeros((1, FF_COLS), F32)
        project(starts[0])
        for r0, r_next in zip(starts, starts[1:] + [None]):
            if r_next is not None:
                project(r_next)
            a_ext = _rows_ext(a_ref, r0, rows, t, HALO, HALO)
            b_ext = _rows_ext(b_ref, r0, rows, t, HALO, HALO)
            dg_ext = _rows_ext(dg_ref, r0, rows, t, HALO, HALO)
            a0, a1, a2, acc = _conv_taps(a_ext, r0 - HALO, cw_ref, cb_ref)
            sg = _sigmoid(acc)
            dacc = dg_ext * b_ext * (sg * (1.0 + acc * (1.0 - sg)))
            n = dacc.shape[0]
            da = (dacc * cw_ref[2:3, :] + pltpu.roll(dacc, n - 1, axis=0) * cw_ref[1:2, :]
                  + pltpu.roll(dacc, n - 2, axis=0) * cw_ref[0:1, :])
            core = slice(HALO, HALO + rows)
            da = da[core, :]
            if r0 < N_PAD:
                row = r0 + lax.broadcasted_iota(jnp.int32, (rows, 1), 0)
                da = jnp.where(row >= N_PAD, da, 0.0)
            dup_ref[0, r0:r0 + rows, :] = da.astype(BF16)
            dup_ref[1, r0:r0 + rows, :] = (dg_ext * acc * sg)[core, :].astype(BF16)
            dacc_c = dacc[core, :]
            gw[0] = gw[0] + jnp.sum(dacc_c * a2[core, :], axis=0, keepdims=True)
            gw[1] = gw[1] + jnp.sum(dacc_c * a1[core, :], axis=0, keepdims=True)
            gw[2] = gw[2] + jnp.sum(dacc_c * a0[core, :], axis=0, keepdims=True)
            gb = gb + jnp.sum(dacc_c, axis=0, keepdims=True)
        gcw_ref[...] = jnp.zeros((8, FF_COLS), F32)
        for tap in range(3):
            gcw_ref[tap:tap + 1, :] = gw[tap]
        gcb_ref[...] = gb

    return _pcall(
        body, name=name, grid=(f // FF_COLS,),
        in_specs=[pl.BlockSpec((None, t, FF_COLS), lambda j: (0, 0, j)), pl.BlockSpec((None, t, FF_COLS), lambda j: (1, 0, j)),
                  pl.BlockSpec((8, FF_COLS), lambda j: (0, j)), pl.BlockSpec((1, FF_COLS), lambda j: (0, j)),
                  pl.BlockSpec((t, d), lambda j: (0, 0), pipeline_mode=pl.Buffered(1)),
                  pl.BlockSpec((FF_COLS, d), lambda j: (j, 0))],
        out_specs=[pl.BlockSpec((2, t, FF_COLS), lambda j: (0, 0, j)), pl.BlockSpec((8, FF_COLS), lambda j: (0, j)),
                   pl.BlockSpec((1, FF_COLS), lambda j: (0, j))],
        out_shape=[jax.ShapeDtypeStruct((2, t, f), BF16), jax.ShapeDtypeStruct((8, f), F32),
                   jax.ShapeDtypeStruct((1, f), F32)],
        scratch_shapes=[pltpu.VMEM((t, FF_COLS), F32)],
        compiler_params=_params("parallel"),
    )(up, up, conv_w8, conv_b, dh2, w_down)


def _exchange(arrays, kinds, name):
    n = len(arrays)
    npeer = N_DEV - 1

    def body(*refs):
        ins, outs = refs[:n], refs[n:2 * n]
        send_sems, recv_sems, local_sems = refs[2 * n:]
        x, y, c = lax.axis_index("x"), lax.axis_index("y"), lax.axis_index("c")
        me = 4 * x + 2 * y + c
        copies, locals_ = [], []
        for a in range(n):
            gather = kinds[a] == "gather"
            own = pltpu.make_async_copy(ins[a] if gather else ins[a].at[me], outs[a].at[me], local_sems.at[a])
            own.start()
            locals_.append(own)
            for d in range(1, N_DEV):
                px = 1 - x if d & 4 else x
                py = 1 - y if d & 2 else y
                pc = 1 - c if d & 1 else c
                src = ins[a] if gather else ins[a].at[4 * px + 2 * py + pc]
                cp = pltpu.make_async_remote_copy(
                    src_ref=src, dst_ref=outs[a].at[me],
                    send_sem=send_sems.at[a * npeer + d - 1], recv_sem=recv_sems.at[a * npeer + d - 1],
                    device_id=(px, py, pc), device_id_type=pl.DeviceIdType.MESH)
                cp.start()
                copies.append(cp)
        for cp in copies:
            cp.wait_recv()
        for cp in copies:
            cp.wait_send()
        for own in locals_:
            own.wait()

    out_shape = [jax.ShapeDtypeStruct((N_DEV,) + (a.shape if k == "gather" else a.shape[1:]), a.dtype)
                 for a, k in zip(arrays, kinds)]
    return _pcall(
        body, name=name,
        in_specs=[pl.BlockSpec(memory_space=pl.ANY)] * n,
        out_specs=[pl.BlockSpec(memory_space=pl.ANY)] * n,
        out_shape=out_shape,
        scratch_shapes=[pltpu.SemaphoreType.DMA((n * npeer,)), pltpu.SemaphoreType.DMA((n * npeer,)),
                        pltpu.SemaphoreType.DMA((n,))],
        compiler_params=pltpu.CompilerParams(has_side_effects=True),
    )(*arrays)


def _peer_copies(srcs, lands, kinds, send_sems, recv_sems):
    x, y, c = lax.axis_index("x"), lax.axis_index("y"), lax.axis_index("c")
    me = 4 * x + 2 * y + c
    copies = []
    for a in range(len(srcs)):
        for d in range(1, N_DEV):
            px = 1 - x if d & 4 else x
            py = 1 - y if d & 2 else y
            pc = 1 - c if d & 1 else c
            k = a * (N_DEV - 1) + d - 1
            copies.append(pltpu.make_async_remote_copy(
                src_ref=srcs[a] if kinds[a] == "gather" else srcs[a].at[4 * px + 2 * py + pc], dst_ref=lands[a].at[me],
                send_sem=send_sems.at[k], recv_sem=recv_sems.at[k],
                device_id=(px, py, pc), device_id_type=pl.DeviceIdType.MESH))
    return copies


def _exchange_start(arrays, kinds, name, after=None):
    n = len(arrays)
    nsem = n * (N_DEV - 1)
    hbm = pl.BlockSpec(memory_space=pltpu.HBM)
    sem = pl.BlockSpec(memory_space=pltpu.SEMAPHORE)
    land_shapes = [(N_DEV,) + (a.shape if k == "gather" else a.shape[1:]) for a, k in zip(arrays, kinds)]

    n_in = 2 * n + int(after is not None)

    def body(*refs):
        srcs, lands = refs[:n], refs[n:2 * n]
        send_sems, recv_sems = refs[n_in], refs[n_in + 1]
        token = refs[-1]
        for cp in _peer_copies(srcs, lands, kinds, send_sems, recv_sems):
            cp.start()
        token[...] = jnp.zeros_like(token)

    operands = [pltpu.with_memory_space_constraint(a, pltpu.HBM) for a in arrays]
    operands += [pltpu.with_memory_space_constraint(lax.empty(s, a.dtype), pltpu.HBM) for s, a in zip(land_shapes, arrays)]
    operands += [] if after is None else [after]
    out = _pcall(
        body, name=name,
        in_specs=[hbm] * (2 * n) + ([] if after is None else [pl.BlockSpec(memory_space=pl.ANY)]),
        out_specs=[sem, sem] + [hbm] * (2 * n) + [pl.BlockSpec(memory_space=pltpu.VMEM)],
        out_shape=[pltpu.SemaphoreType.DMA((nsem,)), pltpu.SemaphoreType.DMA((nsem,))]
        + [pltpu.HBM(a.shape, a.dtype) for a in arrays]
        + [pltpu.HBM(s, a.dtype) for s, a in zip(land_shapes, arrays)]
        + [jax.ShapeDtypeStruct((8, LANE), F32)],
        input_output_aliases={k: 2 + k for k in range(2 * n)},
        compiler_params=pltpu.CompilerParams(has_side_effects=pltpu.SideEffectType.DATAFLOW_SIDE_EFFECTING),
    )(*operands)
    return out[0], out[1], list(out[2:2 + n]), list(out[2 + n:2 + 2 * n]), out[-1]


def _exchange_wait(started, kinds, after, name):
    send_sems, recv_sems, srcs, lands, _ = started
    n = len(srcs)
    hbm = pl.BlockSpec(memory_space=pltpu.HBM)
    sem = pl.BlockSpec(memory_space=pltpu.SEMAPHORE)

    def body(*refs):
        src_refs, land_refs = refs[:n], refs[n:2 * n]
        copies = _peer_copies(src_refs, land_refs, kinds, refs[2 * n], refs[2 * n + 1])
        for cp in copies:
            cp.wait_send()
        for cp in copies:
            cp.wait_recv()

    out = _pcall(
        body, name=name,
        in_specs=[hbm] * (2 * n) + [sem, sem, pl.BlockSpec(memory_space=pl.ANY)],
        out_specs=[hbm] * (2 * n),
        out_shape=[pltpu.HBM(a.shape, a.dtype) for a in srcs + lands],
        input_output_aliases={k: k for k in range(2 * n)},
        compiler_params=pltpu.CompilerParams(has_side_effects=pltpu.SideEffectType.DATAFLOW_SIDE_EFFECTING),
    )(*srcs, *lands, send_sems, recv_sems, after)
    me = 4 * lax.axis_index("x") + 2 * lax.axis_index("y") + lax.axis_index("c")
    filled = []
    for src, land, kind in zip(out[:n], out[n:], kinds):
        own = src if kind == "gather" else lax.dynamic_index_in_dim(src, me, axis=0, keepdims=False)
        filled.append(lax.dynamic_update_slice(land, own[None], (me,) + (0,) * own.ndim))
    return filled


def _sum_slots(slots, name, rows_tile):
    nd, r, c = slots.shape

    def body(s_ref, o_ref):
        acc = s_ref[0].astype(F32)
        for p in range(1, nd):
            acc = acc + s_ref[p].astype(F32)
        o_ref[...] = acc

    return _pcall(
        body, name=name, grid=(r // rows_tile,),
        in_specs=[pl.BlockSpec((nd, rows_tile, c), lambda i: (0, i, 0))],
        out_specs=pl.BlockSpec((rows_tile, c), lambda i: (i, 0)),
        out_shape=jax.ShapeDtypeStruct((r, c), F32),
        compiler_params=_params("parallel"),
    )(slots)


def _sum_slots_small(slot_arrays, name):
    n = len(slot_arrays)

    def body(*refs):
        for s_ref, o_ref in zip(refs[:n], refs[n:]):
            acc = s_ref[0]
            for p in range(1, s_ref.shape[0]):
                acc = acc + s_ref[p]
            o_ref[...] = acc

    return _pcall(body, name=name, out_shape=[jax.ShapeDtypeStruct(a.shape[1:], F32) for a in slot_arrays])(*slot_arrays)


def _adamw_update(w_ref, g_ref, m_ref, v_ref, d_ref, nm_ref, nv_ref):
    gr = g_ref[...]
    nm = ADAM_B1 * m_ref[...] + (1.0 - ADAM_B1) * gr
    nv = ADAM_B2 * v_ref[...] + (1.0 - ADAM_B2) * (gr * gr)
    m_hat = nm / (1.0 - ADAM_B1 ** ADAM_STEP)
    v_hat = nv / (1.0 - ADAM_B2 ** ADAM_STEP)
    d_ref[...] = -ADAM_LR * (m_hat / (jnp.sqrt(v_hat) + ADAM_EPS) + ADAM_WD * w_ref[...])
    nm_ref[...] = nm
    nv_ref[...] = nv


def _adamw_small(ws, gs, ms, vs, name):
    n = len(ws)

    def body(*refs):
        ins, outs = refs[:4 * n], refs[4 * n:]
        for k in range(n):
            _adamw_update(ins[k], ins[n + k], ins[2 * n + k], ins[3 * n + k], outs[k], outs[n + k], outs[2 * n + k])

    shapes = [jax.ShapeDtypeStruct(w.shape, F32) for w in ws]
    out = _pcall(body, name=name, out_shape=shapes * 3)(*ws, *gs, *ms, *vs)
    return list(out[:n]), list(out[n:2 * n]), list(out[2 * n:])


def _adamw(w, g, m, v, name, rows_tile):
    _, r, c = w.shape
    body = lambda *refs: _adamw_update(*refs)
    spec3 = pl.BlockSpec((None, rows_tile, c), lambda i: (0, i, 0))
    spec2 = pl.BlockSpec((rows_tile, c), lambda i: (i, 0))
    shp = jax.ShapeDtypeStruct((1, r, c), F32)
    return _pcall(
        body, name=name, grid=(r // rows_tile,), in_specs=[spec3, spec2, spec3, spec3], out_specs=[spec3] * 3,
        out_shape=[shp] * 3, compiler_params=_params("parallel"),
    )(w, g, m, v)


F0 = 2 * RET_QK + 2 * RET_V


def _to_internal_rows(w_t):
    cols = w_t.shape[1]
    fox = w_t[F0:F0 + 3 * FOX_W].reshape(3, FOX_PAIRS, LANE, cols).transpose(1, 0, 2, 3).reshape(3 * FOX_W, cols)
    tail = jnp.zeros((IN_PAD - IN_WIDTH, cols), w_t.dtype)
    return jnp.concatenate([w_t[:F0], fox, w_t[F0 + 3 * FOX_W:], tail], axis=0)


def _from_internal_rows(g_t):
    cols = g_t.shape[1]
    fox = g_t[F0:F0 + 3 * FOX_W].reshape(FOX_PAIRS, 3, LANE, cols).transpose(1, 0, 2, 3).reshape(3 * FOX_W, cols)
    return jnp.concatenate([g_t[:F0], fox, g_t[F0 + 3 * FOX_W:F0 + 3 * FOX_W + FOX_HEADS]], axis=0)


def _local_step(x, target, meta, attn_g, fox_b, ret_g, ffn_g, conv_w8, conv_b, final_g,
                first_weight, late_weights, ffn_grads_ready, out_grad_ready, in_grad_ready):
    seq, d = x.shape
    t = seq + PREFIX
    tm = TOK_TILE
    nq = t // tm
    fox_b128 = jnp.pad(fox_b, ((0, 0), (0, LANE - FOX_HEADS)))

    h0, n1 = _prep_norm(x, meta, attn_g, "prep_norm")
    w_in_t = first_weight(n1)
    proj = _mm_simple(n1, w_in_t, mode="nt", tm=tm, tn=IN_PAD, tk=d, out_dtype=F32, name="mm_in")
    cos, sin = _rope_tables(t)
    o_pre, mixed, states = _ret_fwd(proj, cos, sin, ret_g, "ret_fwd")
    c = _forget_cumsum(proj, fox_b128, "forget_cumsum")
    qa, ka, va = _fox_prep(proj, c, "fox_prep")
    by_block = lambda a: a.reshape(FOX_HEADS, nq, tm, LANE)
    mixed, o_fox, lse = _fox_fwd(by_block(qa), by_block(ka), by_block(va), mixed, "fox_fwd")
    w_out, w_up_t, w_down = late_weights(o_fox)
    tile = pl.BlockSpec((tm, d), lambda i: (i, 0))
    row_vec = pl.BlockSpec((1, d), lambda i: (0, 0))
    resident = lambda shape: pl.BlockSpec(shape, lambda i: (0,) * len(shape), pipeline_mode=pl.Buffered(1))
    acts = lambda dtype: jax.ShapeDtypeStruct((t, d), dtype)
    vec = jax.ShapeDtypeStruct((1, d), F32)

    def residual_and_norm(i, acc, ins, outs):
        h = acc + ins[0][...]
        outs[0][...] = h
        outs[1][...] = (h * lax.rsqrt(jnp.mean(h * h, axis=-1, keepdims=True) + EPS) * ins[1][...]).astype(BF16)

    h1, n2 = _matmul_rows([mixed], [tile], [w_out], [resident((d, d))], [h0, ffn_g], [tile, row_vec],
                          [tile, tile], [acts(F32), acts(BF16)], residual_and_norm, mode="nn", steps=nq, name="mm_out_norm")
    nf = D_FF // 1408
    up, g = _up_conv_fwd(n2, w_up_t, conv_w8, conv_b, "up_conv_fwd")

    def residual_loss_bwd(i, acc, ins, outs):
        loss_ref, dh_ref, dhb_ref, gg_ref = outs
        part, dh, gg = _loss_tile(i, acc + ins[0][...], jnp.concatenate([ins[1][...], ins[2][...], ins[3][...]], axis=0),
                                  ins[4][...])
        _accumulate(loss_ref, i, jnp.broadcast_to(part, loss_ref.shape))
        dh_ref[...] = dh
        dhb_ref[...] = dh.astype(BF16)
        _accumulate(gg_ref, i, gg)

    loss_tile, dh2, dh2_b, g_final = _matmul_rows(
        [g], [pl.BlockSpec((tm, D_FF), lambda i: (i, 0))], [w_down], [resident((D_FF, d))],
        [h1, target, target, target, final_g], [tile] + _shifted_row_specs(d) + [row_vec],
        [pl.BlockSpec((8, LANE), lambda i: (0, 0)), tile, tile, row_vec],
        [jax.ShapeDtypeStruct((8, LANE), F32), acts(F32), acts(BF16), vec], residual_loss_bwd,
        mode="nn", steps=nq, name="mm_down_loss")

    tkw = 1408 if t % 1408 == 0 else tm
    gw_down = _mm_simple(g, dh2_b, mode="tn", tm=1408, tn=d, tk=tkw, out_dtype=BF16, name="mm_gw_down")
    dup, g_conv_w8, g_conv_b = _dg_conv_bwd(up, conv_w8, conv_b, dh2_b, w_down, "dg_conv_bwd")

    def norm_bwd(i, acc, ins, outs):
        dh, gg = _rms_bwd_tile(acc, ins[0][...], ins[1][...], ins[2][...])
        outs[0][...] = dh
        _accumulate(outs[1], i, gg)

    half = lambda p: pl.BlockSpec((None, tm, D_FF), lambda i: (p, i, 0))
    half_w = lambda p: pl.BlockSpec((None, D_FF, d), lambda i: (p, 0, 0), pipeline_mode=pl.Buffered(1))
    gw_up_t = _matmul(
        dup, n2, mode="tn", grid=(2 * nf, 1, t // tkw),
        a_spec=pl.BlockSpec((None, tkw, 1408), lambda i, j, k: (i // nf, k, i % nf)),
        b_spec=pl.BlockSpec((tkw, d), lambda i, j, k: (k, 0)),
        o_spec=pl.BlockSpec((1408, d), lambda i, j, k: (i, 0)),
        out_shape=jax.ShapeDtypeStruct((2 * D_FF, d), BF16), name="mm_gw_up")
    dh1, g_ffn = _matmul_rows(
        [dup, dup], [half(0), half(1)], [w_up_t, w_up_t], [half_w(0), half_w(1)],
        [h1, ffn_g, dh2], [tile, row_vec, tile], [tile, row_vec], [acts(F32), vec], norm_bwd,
        mode="nn", steps=nq, name="mm_dn2_norm_bwd", after=ffn_grads_ready(gw_down, gw_up_t))

    dmixed = _mm_simple(dh1, w_out, mode="nt", tm=tm, tn=d, tk=d, out_dtype=F32, name="mm_dmixed")
    gw_out = _mm_simple(mixed, dh1, mode="tn", tm=d, tn=d, tk=tkw, out_dtype=BF16, name="mm_gw_out")
    dproj, g_ret = _ret_bwd(proj, cos, sin, ret_g + out_grad_ready(gw_out), dmixed, o_pre, states, "ret_bwd")
    qab, doa = _fox_prep_bwd(dmixed, o_fox, lse, qa, "fox_prep_bwd")
    dproj, drs, dcs = _fox_bwd(by_block(qab), by_block(doa), by_block(ka), by_block(va), dproj, "fox_bwd")
    dproj, g_fox_b = _forget_cumsum_bwd(proj, fox_b128, drs, dcs, dproj, "forget_cumsum_bwd")
    gw_in_t = _mm_simple(dproj, n1, mode="tn", tm=640, tn=d, tk=tkw, out_dtype=BF16, name="mm_gw_in")
    sent = in_grad_ready(gw_in_t)
    dh0, g_attn = _matmul_rows(
        [dproj], [pl.BlockSpec((tm, IN_PAD), lambda i: (i, 0))], [w_in_t], [resident((IN_PAD, d))],
        [h0, attn_g, dh1], [tile, row_vec, tile], [tile, row_vec], [acts(F32), vec], norm_bwd,
        mode="nn", steps=nq, name="mm_dn1_norm_bwd", after=sent)

    grads = dict(meta=dh0[N_PAD:PREFIX], attn_g=g_attn, fox_b=g_fox_b, ret_g=g_ret,
                 ffn_g=g_ffn, conv_w=g_conv_w8, conv_b=g_conv_b, final_g=g_final)
    return loss_tile, dh0[PREFIX:], grads


def kernel(x, meta_tokens, attn_norm_g, w_in, fox_forget_b, ret_norm_g, w_out, ffn_norm_g, w_up, conv_w, conv_b, w_down, final_norm_g, loss_target, m_meta_tokens, m_attn_norm_g, m_w_in, m_fox_forget_b, m_ret_norm_g, m_w_out, m_ffn_norm_g, m_w_up, m_conv_w, m_conv_b, m_w_down, m_final_norm_g, v_meta_tokens, v_attn_norm_g, v_w_in, v_fox_forget_b, v_ret_norm_g, v_w_out, v_ffn_norm_g, v_w_up, v_conv_w, v_conv_b, v_w_down, v_final_norm_g):
    d = D_MODEL
    me = 4 * lax.axis_index("x") + 2 * lax.axis_index("y") + lax.axis_index("c")
    in_blk = IN_WIDTH // N_DEV
    in_blk_pad = 400
    up_blk = 2 * D_FF // N_DEV
    down_blk = D_FF // N_DEV
    cw_blk = D_FF // N_DEV

    w_in_loc = jnp.pad(w_in[0].T.astype(BF16), ((0, in_blk_pad - in_blk), (0, 0)))
    cw_loc = jnp.pad(conv_w[0], ((0, 5), (0, 384 - cw_blk)))
    g_meta, g_cw = _exchange([meta_tokens, cw_loc], ["gather"] * 2, "gather_small")
    first = _exchange_start([w_in_loc], ["gather"], "gather_in_start", after=g_meta)
    rest_loc = [(w_out[0] + first[-1][0:1, 0:1]).astype(BF16), w_up[0].T.astype(BF16), w_down[0].astype(BF16)]
    rest = _exchange_start(rest_loc, ["gather"] * 3, "gather_rest_start")
    meta_f = g_meta.transpose(1, 0, 2).reshape(N_META, d)
    conv_w8 = jnp.pad(g_cw[:, :3, :cw_blk].transpose(1, 0, 2).reshape(3, D_FF), ((0, 5), (0, 0)))
    pending = {}

    def first_weight(after):
        (g_in,) = _exchange_wait(first, ["gather"], after, "gather_in_wait")
        return _to_internal_rows(g_in[:, :in_blk].reshape(IN_WIDTH, d))

    def in_grad_ready(gw_in_t):
        blocks = _from_internal_rows(gw_in_t).reshape(N_DEV, in_blk, d)
        blocks = jnp.pad(blocks, ((0, 0), (0, in_blk_pad - in_blk), (0, 0)))
        pending["in"] = _exchange_start([blocks], ["scatter"], "grads_in_start")
        return pending["in"][-1][0:1, 0:1]

    def late_weights(after):
        g_out, g_up, g_down = _exchange_wait(rest, ["gather"] * 3, after, "gather_rest_wait")
        return g_out.reshape(d, d), g_up.reshape(2, D_FF, d), g_down.reshape(D_FF, d)

    def ffn_grads_ready(gw_down, gw_up_t):
        blocks = [gw_down.reshape(N_DEV, down_blk, d), gw_up_t.reshape(N_DEV, up_blk, d)]
        pending["ffn"] = _exchange_start(blocks, ["scatter"] * 2, "grads_ffn_start")
        return pending["ffn"][-1][0:1, 0:1]

    def out_grad_ready(gw_out):
        pending["out"] = _exchange_start([gw_out.reshape(N_DEV, d // N_DEV, d)], ["scatter"], "grads_out_start")
        return pending["out"][-1][0:1, 0:1]

    loss_tile, grad_x, gr = _local_step(
        x[0], loss_target[0], meta_f, attn_norm_g + rest[-1][0:1, 0:1], fox_forget_b, ret_norm_g, ffn_norm_g,
        conv_w8, conv_b, final_norm_g.reshape(1, d), first_weight, late_weights, ffn_grads_ready, out_grad_ready,
        in_grad_ready)

    small = [loss_tile, gr["attn_g"], gr["fox_b"], gr["ret_g"], gr["ffn_g"], gr["conv_b"], gr["final_g"],
             gr["meta"], gr["conv_w"]]
    r_small = _exchange(small, ["gather"] * len(small), "exchange_small")
    r_down, r_up = _exchange_wait(pending["ffn"], ["scatter"] * 2, r_small[0], "grads_ffn_wait")
    (r_out,) = _exchange_wait(pending["out"], ["scatter"], r_small[0], "grads_out_wait")
    g_w_out = _sum_slots(r_out, "sum_w_out", d // N_DEV)
    g_w_up = _sum_slots(r_up, "sum_w_up", up_blk).T
    g_w_down = _sum_slots(r_down, "sum_w_down", down_blk)
    (loss_all, g_attn, g_fox_b128, g_ret, g_ffn, g_conv_b, g_final, g_meta_full, g_cw_full) = _sum_slots_small(
        r_small, "sum_small")
    loss = loss_all[0, 0]
    g_fox_b = g_fox_b128[:, :FOX_HEADS]
    g_meta_loc = lax.dynamic_slice(g_meta_full, (0, me * (d // N_DEV)), (N_META, d // N_DEV))
    g_cw_loc = lax.dynamic_slice(g_cw_full, (0, me * cw_blk), (3, cw_blk))

    d_w_out, m_w_out_n, v_w_out_n = _adamw(w_out, g_w_out, m_w_out, v_w_out, "adamw_w_out", 128)
    d_w_up, m_w_up_n, v_w_up_n = _adamw(w_up, g_w_up, m_w_up, v_w_up, "adamw_w_up", 128)
    d_w_down, m_w_down_n, v_w_down_n = _adamw(w_down, g_w_down, m_w_down, v_w_down, "adamw_w_down", down_blk)
    (r_in,) = _exchange_wait(pending["in"], ["scatter"], d_w_up, "grads_in_wait")
    g_w_in = _sum_slots(r_in, "sum_w_in", in_blk_pad)[:in_blk].T
    d_w_in, m_w_in_n, v_w_in_n = _adamw(w_in, g_w_in, m_w_in, v_w_in, "adamw_w_in", 128)
    row = lambda a: a.reshape(1, d)
    sm_grads = [g_meta_loc, g_attn, g_fox_b, g_ret, g_ffn, g_cw_loc, g_conv_b, g_final]
    sm_w = [meta_tokens, attn_norm_g, fox_forget_b, ret_norm_g, ffn_norm_g, conv_w[0], conv_b, row(final_norm_g)]
    sm_m = [m_meta_tokens, m_attn_norm_g, m_fox_forget_b, m_ret_norm_g, m_ffn_norm_g, m_conv_w[0], m_conv_b,
            row(m_final_norm_g)]
    sm_v = [v_meta_tokens, v_attn_norm_g, v_fox_forget_b, v_ret_norm_g, v_ffn_norm_g, v_conv_w[0], v_conv_b,
            row(v_final_norm_g)]
    dl, ml, vl = [lst[:7] + [lst[7].reshape(d)] for lst in _adamw_small(sm_w, sm_grads, sm_m, sm_v, "adamw_small")]

    def by_weight(meta_, attn_, w_in_, fox_, ret_, w_out_, ffn_, w_up_, cw_, cb_, w_down_, final_):
        return (meta_, attn_, w_in_, fox_, ret_, w_out_, ffn_, w_up_, cw_[None], cb_, w_down_, final_)

    grads_out = by_weight(g_meta_loc, g_attn, g_w_in[None], g_fox_b, g_ret, g_w_out[None], g_ffn, g_w_up[None], g_cw_loc,
                          g_conv_b, g_w_down[None], g_final.reshape(d))
    delta_out = by_weight(dl[0], dl[1], d_w_in, dl[2], dl[3], d_w_out, dl[4], d_w_up, dl[5], dl[6], d_w_down, dl[7])
    m_out = by_weight(ml[0], ml[1], m_w_in_n, ml[2], ml[3], m_w_out_n, ml[4], m_w_up_n, ml[5], ml[6], m_w_down_n, ml[7])
    v_out = by_weight(vl[0], vl[1], v_w_in_n, vl[2], vl[3], v_w_out_n, vl[4], v_w_up_n, vl[5], vl[6], v_w_down_n, vl[7])
    return (loss, grad_x[None]) + grads_out + delta_out + m_out + v_out
```

```python
import numpy as np
import jax
import jax.numpy as jnp
from jax import lax
from jax.experimental import pallas as pl
from jax.experimental.pallas import tpu as pltpu

F32 = jnp.float32
BF16 = jnp.bfloat16

D_MODEL = 1024
N_META = 16
N_PAD = 112
PREFIX = 128
RET_HEADS = 4
RET_DK = 64
RET_DV = 128
FOX_HEADS = 8
FOX_DH = 64
D_FF = 2816
ROPE_BASE = 10000.0
EPS = 1e-6
NEG = -1e30
RET_QK = RET_HEADS * RET_DK
RET_V = RET_HEADS * RET_DV
FOX_W = FOX_HEADS * FOX_DH
IN_WIDTH = 2 * RET_QK + 2 * RET_V + 3 * FOX_W + FOX_HEADS
IN_PAD = 3200
FF_COL_BLOCK = (IN_WIDTH - FOX_HEADS) // 128
QK_SCALE = 0.125

ADAM_LR = 0.001
ADAM_B1 = 0.9
ADAM_B2 = 0.999
ADAM_EPS = 1e-08
ADAM_WD = 0.01
ADAM_STEP = 10

N_DEV = 8
LANE = 128
ROW_TILE = 128
TOK_TILE = 384

NN = (((1,), (0,)), ((), ()))
NT = (((1,), (1,)), ((), ()))
TN = (((0,), (0,)), ((), ()))


def _pcall(body, **kw):
    return pl.pallas_call(body, **kw)


def _params(*sem):
    return pltpu.CompilerParams(dimension_semantics=sem)


def _dot(a, b, dims=NN):
    return lax.dot_general(a, b, dims, preferred_element_type=F32)


def _sigmoid(x):
    return 0.5 * jnp.tanh(0.5 * x) + 0.5


def _matmul(a, b, *, mode, grid, a_spec, b_spec, o_spec, out_shape, name, add=None, add_spec=None, after=None):
    dims = {"nn": NN, "nt": NT, "tn": TN}[mode]
    nk = grid[2]
    has_add = add is not None
    a_list, b_list = (list(a), list(b)) if isinstance(a, (list, tuple)) else ([a], [b])
    a_specs, b_specs = (list(a_spec), list(b_spec)) if isinstance(a_spec, (list, tuple)) else ([a_spec], [b_spec])
    nt = len(a_list)
    n_in = 2 * nt + int(has_add) + int(after is not None)

    def body(*refs):
        a_refs, b_refs = refs[:nt], refs[nt:2 * nt]
        add_ref = refs[2 * nt] if has_add else None
        o_ref = refs[n_in]
        part = _dot(a_refs[0][...].astype(BF16), b_refs[0][...].astype(BF16), dims)
        for ar, br in zip(a_refs[1:], b_refs[1:]):
            part = part + _dot(ar[...].astype(BF16), br[...].astype(BF16), dims)

        def finish(acc):
            if has_add:
                acc = acc + add_ref[...]
            o_ref[...] = acc.astype(o_ref.dtype)

        if nk == 1:
            finish(part)
        else:
            acc_ref = refs[-1]
            k = pl.program_id(2)

            @pl.when(k == 0)
            def _():
                acc_ref[...] = part

            @pl.when(k > 0)
            def _():
                acc_ref[...] += part

            @pl.when(k == nk - 1)
            def _():
                finish(acc_ref[...])

    in_specs = a_specs + b_specs + ([add_spec] if has_add else [])
    args = tuple(a_list) + tuple(b_list) + ((add,) if has_add else ())
    if after is not None:
        in_specs, args = in_specs + [pl.BlockSpec(memory_space=pl.ANY)], args + (after,)
    scratch = [] if nk == 1 else [pltpu.VMEM(tuple(d for d in o_spec.block_shape if d is not None), F32)]
    return _pcall(
        body, name=name, grid=grid, in_specs=in_specs, out_specs=o_spec, out_shape=out_shape,
        scratch_shapes=scratch, compiler_params=_params("parallel", "parallel", "arbitrary"),
    )(*args)


def _mm_simple(a, b, *, mode, tm, tn, tk, out_dtype, name, add=None, after=None):
    if mode == "tn":
        K, M = a.shape
    else:
        M, K = a.shape
    N = b.shape[0] if mode == "nt" else b.shape[1]
    grid = (M // tm, N // tn, K // tk)
    resident = dict(pipeline_mode=pl.Buffered(1)) if (tn == N and tk == K) else {}
    a_spec = pl.BlockSpec((tk, tm), lambda i, j, k: (k, i)) if mode == "tn" else pl.BlockSpec((tm, tk), lambda i, j, k: (i, k))
    b_spec = (pl.BlockSpec((tn, tk), lambda i, j, k: (j, k), **resident) if mode == "nt"
              else pl.BlockSpec((tk, tn), lambda i, j, k: (k, j), **resident))
    o_spec = pl.BlockSpec((tm, tn), lambda i, j, k: (i, j))
    return _matmul(a, b, mode=mode, grid=grid, a_spec=a_spec, b_spec=b_spec, o_spec=o_spec,
                   out_shape=jax.ShapeDtypeStruct((M, N), out_dtype), name=name, add=add,
                   add_spec=o_spec if add is not None else None, after=after)


def _matmul_rows(a_list, a_specs, b_list, b_specs, extras, extra_specs, out_specs, out_shape, epilogue, *,
                 mode, steps, name, after=None):
    dims = {"nn": NN, "nt": NT}[mode]
    nt, ne = len(a_list), len(extras)
    n_in = 2 * nt + ne + int(after is not None)

    def body(*refs):
        acc = _dot(refs[0][...].astype(BF16), refs[nt][...].astype(BF16), dims)
        for k in range(1, nt):
            acc = acc + _dot(refs[k][...].astype(BF16), refs[nt + k][...].astype(BF16), dims)
        epilogue(pl.program_id(0), acc, refs[2 * nt:2 * nt + ne], refs[n_in:])

    in_specs = list(a_specs) + list(b_specs) + list(extra_specs)
    args = tuple(a_list) + tuple(b_list) + tuple(extras)
    if after is not None:
        in_specs, args = in_specs + [pl.BlockSpec(memory_space=pl.ANY)], args + (after,)
    return _pcall(body, name=name, grid=(steps,), in_specs=in_specs, out_specs=out_specs, out_shape=out_shape,
                  compiler_params=_params("arbitrary"))(*args)


def _rms_bwd_tile(dy, x, gain, dres):
    r = lax.rsqrt(jnp.mean(x * x, axis=-1, keepdims=True) + EPS)
    xhat = x * r
    u = dy * gain
    return dres + r * (u - xhat * jnp.mean(u * xhat, axis=-1, keepdims=True)), jnp.sum(dy * xhat, axis=0, keepdims=True)


def _loss_tile(i, x, tgt, gain):
    d = x.shape[-1]
    r = lax.rsqrt(jnp.mean(x * x, axis=-1, keepdims=True) + EPS)
    xhat = x * r
    counted = (i * TOK_TILE + lax.broadcasted_iota(jnp.int32, (TOK_TILE, 1), 0)) >= PREFIX
    err = jnp.where(counted, xhat * gain - tgt, 0.0)
    dy = err * (1.0 / d)
    u = dy * gain
    dh = r * (u - xhat * jnp.mean(u * xhat, axis=-1, keepdims=True))
    return 0.5 * jnp.sum(jnp.mean(err * err, axis=-1, keepdims=True)), dh, jnp.sum(dy * xhat, axis=0, keepdims=True)


def _accumulate(ref, i, part):
    @pl.when(i == 0)
    def _():
        ref[...] = part

    @pl.when(i > 0)
    def _():
        ref[...] += part


def _prep_norm(x, meta, gain, name):
    seq, d = x.shape
    t = seq + PREFIX

    def body(xa_ref, xb_ref, xc_ref, meta_ref, g_ref, h_ref, n_ref):
        i = pl.program_id(0)

        @pl.when(i == 0)
        def _():
            h_ref[0:N_PAD, :] = jnp.zeros((N_PAD, d), F32)
            h_ref[N_PAD:ROW_TILE, :] = meta_ref[...]

        @pl.when(i > 0)
        def _():
            h_ref[0:ROW_TILE, :] = xa_ref[...]

        h_ref[ROW_TILE:2 * ROW_TILE, :] = xb_ref[...]
        h_ref[2 * ROW_TILE:3 * ROW_TILE, :] = xc_ref[...]
        h = h_ref[...]
        r = lax.rsqrt(jnp.mean(h * h, axis=-1, keepdims=True) + EPS)
        n_ref[...] = (h * r * g_ref[...]).astype(BF16)

    return _pcall(
        body, name=name, grid=(t // TOK_TILE,),
        in_specs=_shifted_row_specs(d) + [pl.BlockSpec((N_META, d), lambda i: (0, 0)), pl.BlockSpec((1, d), lambda i: (0, 0))],
        out_specs=[pl.BlockSpec((TOK_TILE, d), lambda i: (i, 0)), pl.BlockSpec((TOK_TILE, d), lambda i: (i, 0))],
        out_shape=[jax.ShapeDtypeStruct((t, d), F32), jax.ShapeDtypeStruct((t, d), BF16)],
        compiler_params=_params("parallel"),
    )(x, x, x, meta, gain)


def _shifted_row_specs(d):
    blocks_per_tile = TOK_TILE // ROW_TILE
    return [pl.BlockSpec((ROW_TILE, d), lambda i, r=r: (jnp.maximum(blocks_per_tile * i + r, 0), 0)) for r in (-1, 0, 1)]


def _rmsnorm(h, gain, name):
    t, d = h.shape

    def body(h_ref, g_ref, n_ref):
        x = h_ref[...]
        r = lax.rsqrt(jnp.mean(x * x, axis=-1, keepdims=True) + EPS)
        n_ref[...] = (x * r * g_ref[...]).astype(BF16)

    return _pcall(
        body, name=name, grid=(t // TOK_TILE,),
        in_specs=[pl.BlockSpec((TOK_TILE, d), lambda i: (i, 0)), pl.BlockSpec((1, d), lambda i: (0, 0))],
        out_specs=pl.BlockSpec((TOK_TILE, d), lambda i: (i, 0)),
        out_shape=jax.ShapeDtypeStruct((t, d), BF16),
        compiler_params=_params("parallel"),
    )(h, gain)


def _rmsnorm_bwd(dn, h, gain, dres, name):
    t, d = h.shape

    def body(dn_ref, h_ref, g_ref, dres_ref, dh_ref, gg_ref):
        i = pl.program_id(0)
        x = h_ref[...]
        r = lax.rsqrt(jnp.mean(x * x, axis=-1, keepdims=True) + EPS)
        xhat = x * r
        dy = dn_ref[...]
        u = dy * g_ref[...]
        dh_ref[...] = dres_ref[...] + r * (u - xhat * jnp.mean(u * xhat, axis=-1, keepdims=True))
        part = jnp.sum(dy * xhat, axis=0, keepdims=True)

        @pl.when(i == 0)
        def _():
            gg_ref[...] = part

        @pl.when(i > 0)
        def _():
            gg_ref[...] += part

    return _pcall(
        body, name=name, grid=(t // TOK_TILE,),
        in_specs=[pl.BlockSpec((TOK_TILE, d), lambda i: (i, 0)), pl.BlockSpec((TOK_TILE, d), lambda i: (i, 0)),
                  pl.BlockSpec((1, d), lambda i: (0, 0)), pl.BlockSpec((TOK_TILE, d), lambda i: (i, 0))],
        out_specs=[pl.BlockSpec((TOK_TILE, d), lambda i: (i, 0)), pl.BlockSpec((1, d), lambda i: (0, 0))],
        out_shape=[jax.ShapeDtypeStruct((t, d), F32), jax.ShapeDtypeStruct((1, d), F32)],
        compiler_params=_params("arbitrary"),
    )(dn, h, gain, dres)


def _loss_bwd(h2, target, gain, name):
    t, d = h2.shape

    def body(h_ref, ta_ref, tb_ref, tc_ref, g_ref, loss_ref, dh_ref, dhb_ref, gg_ref):
        i = pl.program_id(0)

        @pl.when(i == 0)
        def _():
            loss_ref[...] = jnp.zeros_like(loss_ref)
            gg_ref[...] = jnp.zeros_like(gg_ref)

        x = h_ref[...]
        r = lax.rsqrt(jnp.mean(x * x, axis=-1, keepdims=True) + EPS)
        xhat = x * r
        g = g_ref[...]
        tgt = jnp.concatenate([ta_ref[...], tb_ref[...], tc_ref[...]], axis=0)
        counted = (i * TOK_TILE + lax.broadcasted_iota(jnp.int32, (TOK_TILE, 1), 0)) >= PREFIX
        err = jnp.where(counted, xhat * g - tgt, 0.0)
        loss_ref[...] += 0.5 * jnp.sum(jnp.mean(err * err, axis=-1, keepdims=True))
        dy = err * (1.0 / d)
        u = dy * g
        dh = r * (u - xhat * jnp.mean(u * xhat, axis=-1, keepdims=True))
        dh_ref[...] = dh
        dhb_ref[...] = dh.astype(BF16)
        gg_ref[...] += jnp.sum(dy * xhat, axis=0, keepdims=True)

    tile = pl.BlockSpec((TOK_TILE, d), lambda i: (i, 0))
    return _pcall(
        body, name=name, grid=(t // TOK_TILE,),
        in_specs=[tile] + _shifted_row_specs(d) + [pl.BlockSpec((1, d), lambda i: (0, 0))],
        out_specs=[pl.BlockSpec((8, LANE), lambda i: (0, 0)), tile, tile, pl.BlockSpec((1, d), lambda i: (0, 0))],
        out_shape=[jax.ShapeDtypeStruct((8, LANE), F32), jax.ShapeDtypeStruct((t, d), F32),
                   jax.ShapeDtypeStruct((t, d), BF16), jax.ShapeDtypeStruct((1, d), F32)],
        compiler_params=_params("arbitrary"),
    )(h2, target, target, target, gain)


def _ret_consts(bk):
    gam = 1.0 - 2.0 ** (-5.0 - np.arange(RET_HEADS))
    n = np.arange(bk)
    same_or_earlier_chunk = (n[None, :] // 64) <= (n[:, None] // 64)
    w = gam[:, None, None] ** np.abs(n[:, None] - n[None, :])[None] * same_or_earlier_chunk[None]
    wq = gam[:, None] ** (n[None, :] + 1.0)
    wk = gam[:, None] ** (bk - 1.0 - n[None, :])
    mask = (np.arange(RET_QK)[None, :] // RET_DK) == np.arange(RET_HEADS)[:, None]
    return (jnp.asarray(w, F32), jnp.asarray(wq[:, :, None], F32), jnp.asarray(wk[:, :, None], F32),
            jnp.asarray(mask[:, None, :], F32), [float(g ** bk) for g in gam])


def _rope_tables(t):
    half = RET_DK // 2
    inv = 1.0 / (ROPE_BASE ** (jnp.arange(half, dtype=F32) / half))
    ang = jnp.arange(t).astype(F32)[:, None] * inv[None, :]
    cos, sin = jnp.cos(ang), jnp.sin(ang)
    return (jnp.tile(jnp.concatenate([cos, cos], axis=1), (1, RET_HEADS)),
            jnp.tile(jnp.concatenate([-sin, sin], axis=1), (1, RET_HEADS)))


def _swap_halves(x):
    outs = []
    for s in range(x.shape[1] // LANE):
        xs = x[:, LANE * s:LANE * (s + 1)]
        lane = lax.broadcasted_iota(jnp.int32, xs.shape, 1)
        outs.append(jnp.where((lane & 32) == 0, pltpu.roll(xs, LANE - 32, axis=1), pltpu.roll(xs, 32, axis=1)))
    return outs[0] if len(outs) == 1 else jnp.concatenate(outs, axis=1)


def _rope(x, cos, sin_signed):
    return x * cos + _swap_halves(x) * sin_signed


def _rope_t(dx, cos, sin_signed):
    return dx * cos + _swap_halves(dx * sin_signed)


def _ret_fwd(proj, cos, sin, gain, name):
    t = proj.shape[0]
    bk = TOK_TILE
    nb = t // bk
    w, wq, wk, mask, g_blk = _ret_consts(bk)

    def body(q_ref, k_ref, v_ref, rg_ref, cos_ref, sin_ref, w_ref, wq_ref, wk_ref, mask_ref, gain_ref,
             opre_ref, og_ref, st_ref, r_ref):
        i = pl.program_id(0)

        @pl.when(i == 0)
        def _():
            r_ref[...] = jnp.zeros_like(r_ref)

        c, s = cos_ref[...], sin_ref[...]
        valid = ((i * bk + lax.broadcasted_iota(jnp.int32, (bk, 1), 0)) >= N_PAD).astype(F32)
        qr = _rope(q_ref[...], c, s)
        kr = _rope(k_ref[...], c, s) * QK_SCALE * valid
        kb = kr.astype(BF16)
        for h in range(RET_HEADS):
            hm = mask_ref[h]
            cols = slice(RET_DV * h, RET_DV * (h + 1))
            vh = v_ref[:, cols].astype(BF16)
            r_prev = r_ref[h]
            st_ref[0, h] = r_prev
            sm = _dot((qr * hm).astype(BF16), kb, NT) * w_ref[h]
            o = _dot(sm.astype(BF16), vh) + _dot((qr * (hm * wq_ref[h])).astype(BF16), r_prev.astype(BF16))
            r_ref[h] = g_blk[h] * r_prev + _dot((kr * wk_ref[h]).astype(BF16), vh, TN)
            opre_ref[:, cols] = o
            rstd = lax.rsqrt(jnp.mean(o * o, axis=-1, keepdims=True) + EPS)
            rg = rg_ref[:, cols]
            og_ref[:, cols] = (o * rstd * gain_ref[:, cols] * (rg * _sigmoid(rg))).astype(BF16)

    full = lambda shape: pl.BlockSpec(shape, lambda i: (0,) * len(shape))
    return _pcall(
        body, name=name, grid=(nb,),
        in_specs=[pl.BlockSpec((bk, RET_QK), lambda i: (i, 0)), pl.BlockSpec((bk, RET_QK), lambda i: (i, 1)),
                  pl.BlockSpec((bk, RET_V), lambda i: (i, 1)), pl.BlockSpec((bk, RET_V), lambda i: (i, 2)),
                  pl.BlockSpec((bk, RET_QK), lambda i: (i, 0)), pl.BlockSpec((bk, RET_QK), lambda i: (i, 0)),
                  full((RET_HEADS, bk, bk)), full((RET_HEADS, bk, 1)), full((RET_HEADS, bk, 1)),
                  full((RET_HEADS, 1, RET_QK)), full((1, RET_V))],
        out_specs=[pl.BlockSpec((bk, RET_V), lambda i: (i, 0)), pl.BlockSpec((bk, RET_V), lambda i: (i, 0)),
                   pl.BlockSpec((1, RET_HEADS, RET_QK, RET_DV), lambda i: (i, 0, 0, 0))],
        out_shape=[jax.ShapeDtypeStruct((t, RET_V), F32), jax.ShapeDtypeStruct((t, RET_V + FOX_W), BF16),
                   jax.ShapeDtypeStruct((nb, RET_HEADS, RET_QK, RET_DV), F32)],
        scratch_shapes=[pltpu.VMEM((RET_HEADS, RET_QK, RET_DV), F32)],
        compiler_params=_params("arbitrary"),
    )(proj, proj, proj, proj, cos, sin, w, wq, wk, mask, gain)


def _ret_bwd(proj, cos, sin, gain, dmixed, opre, states, name):
    t = proj.shape[0]
    bk = TOK_TILE
    nb = t // bk
    w, wq, wk, mask, g_blk = _ret_consts(bk)
    v0, g0 = 2 * RET_QK, 2 * RET_QK + RET_V

    def body(q_ref, k_ref, v_ref, rg_ref, cos_ref, sin_ref, w_ref, wq_ref, wk_ref, mask_ref, gain_ref,
             dog_ref, opre_ref, st_ref, dp_ref, gg_ref, dr_ref):
        step = pl.program_id(0)
        i = nb - 1 - step

        @pl.when(step == 0)
        def _():
            dr_ref[...] = jnp.zeros_like(dr_ref)
            gg_ref[...] = jnp.zeros_like(gg_ref)

        c, s = cos_ref[...], sin_ref[...]
        valid = ((i * bk + lax.broadcasted_iota(jnp.int32, (bk, 1), 0)) >= N_PAD).astype(F32)
        qr = _rope(q_ref[...], c, s)
        kr = _rope(k_ref[...], c, s) * QK_SCALE * valid
        kb = kr.astype(BF16)
        dqr = jnp.zeros((bk, RET_QK), F32)
        dkr = jnp.zeros((bk, RET_QK), F32)
        for h in range(RET_HEADS):
            hm = mask_ref[h]
            cols = slice(RET_DV * h, RET_DV * (h + 1))
            vh = v_ref[:, cols].astype(BF16)
            o = opre_ref[:, cols]
            rstd = lax.rsqrt(jnp.mean(o * o, axis=-1, keepdims=True) + EPS)
            xhat = o * rstd
            rg = rg_ref[:, cols]
            sg = _sigmoid(rg)
            gate = rg * sg
            gn = gain_ref[:, cols]
            dog = dog_ref[:, cols]
            dp_ref[:, g0 + RET_DV * h:g0 + RET_DV * (h + 1)] = (
                dog * xhat * gn * (sg * (1.0 + rg * (1.0 - sg)))).astype(BF16)
            gg_ref[:, cols] += jnp.sum(dog * xhat * gate, axis=0, keepdims=True)
            dxh = dog * gn * gate
            do = (rstd * (dxh - xhat * jnp.mean(dxh * xhat, axis=-1, keepdims=True))).astype(BF16)
            qm = (qr * hm).astype(BF16)
            qw = (qr * (hm * wq_ref[h])).astype(BF16)
            kw = (kr * wk_ref[h]).astype(BF16)
            wh = w_ref[h]
            sm = (_dot(qm, kb, NT) * wh).astype(BF16)
            ds = (_dot(do, vh, NT) * wh).astype(BF16)
            dr = dr_ref[h]
            drb = dr.astype(BF16)
            dp_ref[:, v0 + RET_DV * h:v0 + RET_DV * (h + 1)] = (_dot(sm, do, TN) + _dot(kw, drb)).astype(BF16)
            dqr = dqr + _dot(ds, kb) * hm + _dot(do, st_ref[0, h].astype(BF16), NT) * (hm * wq_ref[h])
            dkr = dkr + _dot(ds, qm, TN) + _dot(vh, drb, NT) * wk_ref[h]
            dr_ref[h] = g_blk[h] * dr + _dot(qw, do, TN)
        dp_ref[:, 0:RET_QK] = _rope_t(dqr, c, s).astype(BF16)
        dp_ref[:, RET_QK:2 * RET_QK] = _rope_t(dkr * (QK_SCALE * valid), c, s).astype(BF16)

    full = lambda shape: pl.BlockSpec(shape, lambda i: (0,) * len(shape))
    rev = lambda col: (lambda i: (nb - 1 - i, col))
    return _pcall(
        body, name=name, grid=(nb,),
        in_specs=[pl.BlockSpec((bk, RET_QK), rev(0)), pl.BlockSpec((bk, RET_QK), rev(1)),
                  pl.BlockSpec((bk, RET_V), rev(1)), pl.BlockSpec((bk, RET_V), rev(2)),
                  pl.BlockSpec((bk, RET_QK), rev(0)), pl.BlockSpec((bk, RET_QK), rev(0)),
                  full((RET_HEADS, bk, bk)), full((RET_HEADS, bk, 1)), full((RET_HEADS, bk, 1)),
                  full((RET_HEADS, 1, RET_QK)), full((1, RET_V)),
                  pl.BlockSpec((bk, RET_V), rev(0)), pl.BlockSpec((bk, RET_V), rev(0)),
                  pl.BlockSpec((1, RET_HEADS, RET_QK, RET_DV), lambda i: (nb - 1 - i, 0, 0, 0))],
        out_specs=[pl.BlockSpec((bk, g0 + RET_V), rev(0)), pl.BlockSpec((1, RET_V), lambda i: (0, 0))],
        out_shape=[jax.ShapeDtypeStruct((t, IN_PAD), BF16), jax.ShapeDtypeStruct((1, RET_V), F32)],
        scratch_shapes=[pltpu.VMEM((RET_HEADS, RET_QK, RET_DV), F32)],
        compiler_params=_params("arbitrary"),
    )(proj, proj, proj, proj, cos, sin, w, wq, wk, mask, gain, dmixed, opre, states)


def _forget_cumsum(proj, bias, name):
    t = proj.shape[0]
    rt = TOK_TILE
    nb = t // rt
    tril = jnp.asarray(np.tril(np.ones((rt, rt))), F32)

    def body(z_ref, b_ref, tril_ref, c_ref, carry_ref):
        i = pl.program_id(0)

        @pl.when(i == 0)
        def _():
            carry_ref[...] = jnp.zeros_like(carry_ref)

        z = z_ref[...] + b_ref[...]
        logf = jnp.minimum(z, 0.0) - jnp.log(1.0 + jnp.exp(-jnp.abs(z)))
        c = lax.dot_general(tril_ref[...], logf, NN, precision=lax.Precision.HIGHEST,
                            preferred_element_type=F32) + carry_ref[...]
        c_ref[...] = c
        carry_ref[...] = c[rt - 1:rt, :]

    return _pcall(
        body, name=name, grid=(nb,),
        in_specs=[pl.BlockSpec((rt, LANE), lambda i: (i, FF_COL_BLOCK)), pl.BlockSpec((1, LANE), lambda i: (0, 0)),
                  pl.BlockSpec((rt, rt), lambda i: (0, 0))],
        out_specs=pl.BlockSpec((rt, LANE), lambda i: (i, 0)),
        out_shape=jax.ShapeDtypeStruct((t, LANE), F32),
        scratch_shapes=[pltpu.VMEM((1, LANE), F32)],
        compiler_params=_params("arbitrary"),
    )(proj, bias, tril)


def _forget_cumsum_bwd(proj, bias, drs, dcs, dproj, name):
    t = proj.shape[0]
    rt = TOK_TILE
    nb = t // rt
    triu = jnp.asarray(np.triu(np.ones((rt, rt))), F32)

    def body(z_ref, b_ref, triu_ref, drs_ref, dcs_ref, dproj_in, dz_ref, gb_ref, carry_ref):
        step = pl.program_id(0)

        @pl.when(step == 0)
        def _():
            carry_ref[...] = jnp.zeros_like(carry_ref)
            gb_ref[...] = jnp.zeros_like(gb_ref)

        dlogf = lax.dot_general(triu_ref[...], drs_ref[...] - dcs_ref[...], NN, precision=lax.Precision.HIGHEST,
                                preferred_element_type=F32) + carry_ref[...]
        carry_ref[...] = dlogf[0:1, :]
        z = z_ref[...] + b_ref[...]
        is_head = lax.broadcasted_iota(jnp.int32, (rt, LANE), 1) < FOX_HEADS
        dz = jnp.where(is_head, dlogf / (1.0 + jnp.exp(z)), 0.0)
        dz_ref[...] = dz.astype(BF16)
        gb_ref[...] += jnp.sum(dz, axis=0, keepdims=True)

    return _pcall(
        body, name=name, grid=(nb,),
        in_specs=[pl.BlockSpec((rt, LANE), lambda i: (nb - 1 - i, FF_COL_BLOCK)),
                  pl.BlockSpec((1, LANE), lambda i: (0, 0)),
                  pl.BlockSpec((rt, rt), lambda i: (0, 0)),
                  pl.BlockSpec((rt, LANE), lambda i: (nb - 1 - i, 0)),
                  pl.BlockSpec((rt, LANE), lambda i: (nb - 1 - i, 0)),
                  pl.BlockSpec(memory_space=pl.ANY)],
        out_specs=[pl.BlockSpec((rt, LANE), lambda i: (nb - 1 - i, FF_COL_BLOCK)),
                   pl.BlockSpec((1, LANE), lambda i: (0, 0))],
        out_shape=[jax.ShapeDtypeStruct(dproj.shape, BF16), jax.ShapeDtypeStruct((1, LANE), F32)],
        input_output_aliases={5: 0},
        scratch_shapes=[pltpu.VMEM((1, LANE), F32)],
        compiler_params=_params("arbitrary"),
    )(proj, bias, triu, drs, dcs, dproj)


FOX_PAIRS = FOX_HEADS // 2
L_ONE_Q = FOX_DH
L_ONE_K = FOX_DH + 3
L_LSE = FOX_DH + 4


def _split3(x):
    hi = x.astype(BF16).astype(F32)
    r = x - hi
    mid = r.astype(BF16).astype(F32)
    return hi, mid, r - mid


def _head_to_low(slab, e):
    return slab if e == 0 else pltpu.roll(slab, FOX_DH, axis=1)


def _pair(a, b, low):
    return jnp.where(low, a, pltpu.roll(b, FOX_DH, axis=1))


def _fox_prep(proj, c, name):
    t = proj.shape[0]
    tq = TOK_TILE

    def body(p_ref, c_ref, qa_ref, ka_ref, va_ref):
        i = pl.program_id(0)
        lane = lax.broadcasted_iota(jnp.int32, (tq, LANE), 1)
        low = lane < FOX_DH
        live = (i * tq + lax.broadcasted_iota(jnp.int32, (tq, 1), 0)) >= N_PAD
        q_tail = jnp.where(lane < L_ONE_Q + 3, 1.0, 0.0)
        k_ones = (lane >= L_ONE_K) & (lane < L_ONE_K + 4)
        v_tail = jnp.where(lane < FOX_DH + 2, 1.0, 0.0)
        for pair in range(FOX_PAIRS):
            base = 3 * LANE * pair
            for e in range(2):
                h = 2 * pair + e
                q = _head_to_low(p_ref[:, base:base + LANE], e)
                k = _head_to_low(p_ref[:, base + LANE:base + 2 * LANE], e)
                v = _head_to_low(p_ref[:, base + 2 * LANE:base + 3 * LANE], e)
                hi, mid, lo = _split3(jnp.where(live, -c_ref[:, h:h + 1], NEG))
                ka = jnp.where(low, k, jnp.where(k_ones, 1.0, 0.0))
                ka = jnp.where(lane == L_ONE_Q, hi, jnp.where(lane == L_ONE_Q + 1, mid, jnp.where(lane == L_ONE_Q + 2, lo, ka)))
                qa_ref[h] = jnp.where(low, q * QK_SCALE, q_tail).astype(BF16)
                ka_ref[h] = ka.astype(BF16)
                va_ref[h] = jnp.where(low, v, v_tail).astype(BF16)

    out = jax.ShapeDtypeStruct((FOX_HEADS, t, LANE), BF16)
    ospec = pl.BlockSpec((FOX_HEADS, tq, LANE), lambda i: (0, i, 0))
    return _pcall(
        body, name=name, grid=(t // tq,),
        in_specs=[pl.BlockSpec((tq, 3 * FOX_W), lambda i: (i, 1)), pl.BlockSpec((tq, LANE), lambda i: (i, 0))],
        out_specs=[ospec, ospec, ospec], out_shape=[out, out, out],
        compiler_params=_params("parallel"),
    )(proj, c)


STEP_PAIRS = 2
STEP_HEADS = 2 * STEP_PAIRS
FOX_GROUPS = FOX_PAIRS // STEP_PAIRS
FWD_PAIRS = 4
FWD_HEADS = 2 * FWD_PAIRS
FWD_GROUPS = FOX_PAIRS // FWD_PAIRS


def _blockdiag(a, b):
    z = jnp.zeros_like(a)
    return jnp.concatenate([jnp.concatenate([a, z], axis=1), jnp.concatenate([z, b], axis=1)], axis=0)


def _fox_fwd(qa, ka, va, mixed, name):
    nh, nq, tq, _ = qa.shape
    t = nq * tq

    def body(qa_ref, ka_ref, va_ref, mixed_in, mixed_ref, o_ref, lse_ref):
        i = pl.program_id(1)
        lane = lax.broadcasted_iota(jnp.int32, (tq, LANE), 1)
        causal = lax.broadcasted_iota(jnp.int32, (tq, tq), 1) <= lax.broadcasted_iota(jnp.int32, (tq, tq), 0)
        qps = [jnp.concatenate([qa_ref[2 * c], qa_ref[2 * c + 1]], axis=1) for c in range(FWD_PAIRS)]

        def logits(j):
            return [_dot(qps[c], _blockdiag(ka_ref[2 * c, j], ka_ref[2 * c + 1, j]), NT) for c in range(FWD_PAIRS)]

        def update(j, scores, carry, diagonal):
            new = []
            for c in range(FWD_PAIRS):
                ms, acc = carry[c]
                ps, ms_new, alphas = [], [], []
                for e in range(2):
                    s = scores[c][:, e * tq:(e + 1) * tq]
                    if diagonal:
                        s = jnp.where(causal, s, NEG)
                    m_new = jnp.maximum(ms[e], jnp.max(s, axis=-1, keepdims=True))
                    ps.append(jnp.exp(s - m_new).astype(BF16))
                    ms_new.append(m_new)
                    alphas.append(jnp.broadcast_to(jnp.exp(ms[e] - m_new), (tq, LANE)))
                pv = _dot(jnp.concatenate(ps, axis=1), _blockdiag(va_ref[2 * c, j], va_ref[2 * c + 1, j]))
                new.append((tuple(ms_new), jnp.concatenate(alphas, axis=1) * acc + pv))
            return tuple(new)

        m0 = jnp.full((tq, 1), NEG, F32)
        init = tuple(((m0, m0), jnp.zeros((tq, 2 * LANE), F32)) for _ in range(FWD_PAIRS))
        carry = lax.fori_loop(0, i, lambda j, cr: update(j, logits(j), cr, False), init)
        o_pairs = []
        lse = jnp.zeros((tq, LANE), F32)
        for c, (ms, acc) in enumerate(update(i, logits(i), carry, True)):
            outs = []
            for e in range(2):
                half = acc[:, e * LANE:(e + 1) * LANE]
                l = half[:, FOX_DH:FOX_DH + 1]
                outs.append(half / l)
                lse = jnp.where(lane == 2 * c + e, ms[e] + jnp.log(l), lse)
            o_pairs.append(_pair(outs[0], outs[1], lane < FOX_DH))
        o_all = jnp.concatenate(o_pairs, axis=1)
        mixed_ref[...] = o_all.astype(BF16)
        o_ref[...] = o_all
        lse_ref[...] = lse

    width = FWD_PAIRS * LANE
    whole = pl.BlockSpec((FWD_HEADS, nq, tq, LANE), lambda g, i: (g, 0, 0, 0), pipeline_mode=pl.Buffered(1))
    return _pcall(
        body, name=name, grid=(FWD_GROUPS, nq),
        in_specs=[pl.BlockSpec((FWD_HEADS, None, tq, LANE), lambda g, i: (g, i, 0, 0)), whole, whole,
                  pl.BlockSpec(memory_space=pl.ANY)],
        out_specs=[pl.BlockSpec((tq, width), lambda g, i: (i, RET_V // width + g)),
                   pl.BlockSpec((tq, width), lambda g, i: (i, g)),
                   pl.BlockSpec((None, tq, LANE), lambda g, i: (g, i, 0))],
        out_shape=[jax.ShapeDtypeStruct(mixed.shape, BF16), jax.ShapeDtypeStruct((t, FOX_W), F32),
                   jax.ShapeDtypeStruct((FWD_GROUPS, t, LANE), F32)],
        input_output_aliases={3: 0},
        compiler_params=_params("parallel", "parallel"),
    )(qa, ka, va, mixed)


def _fox_prep_bwd(dmixed, o_fox, lse, qa, name):
    t = dmixed.shape[0]
    tq = TOK_TILE

    def body(dm_ref, o_ref, lse_ref, qa_ref, qab_ref, doa_ref):
        i = pl.program_id(0)
        lane = lax.broadcasted_iota(jnp.int32, (tq, LANE), 1)
        low = lane < FOX_DH
        live = (i * tq + lax.broadcasted_iota(jnp.int32, (tq, 1), 0)) >= N_PAD
        for pair in range(FOX_PAIRS):
            cols = slice(LANE * pair, LANE * (pair + 1))
            d_slab = dm_ref[:, cols]
            prod = d_slab * o_ref[:, cols]
            for e in range(2):
                h = 2 * pair + e
                nd = -jnp.sum(jnp.where(low, _head_to_low(prod, e), 0.0), axis=-1, keepdims=True)
                nd_hi = nd.astype(BF16).astype(F32)
                doa = jnp.where(low, _head_to_low(d_slab, e), 0.0)
                doa = jnp.where(lane == FOX_DH, nd_hi, jnp.where(lane == FOX_DH + 1, nd - nd_hi, doa))
                doa_ref[h] = doa.astype(BF16)
                lse_h = lse_ref[h // FWD_HEADS][:, h % FWD_HEADS:h % FWD_HEADS + 1]
                hi, mid, lo = _split3(jnp.where(live, -lse_h, 0.0))
                qab = qa_ref[h].astype(F32)
                qab = jnp.where(lane == L_LSE, hi, jnp.where(lane == L_LSE + 1, mid, jnp.where(lane == L_LSE + 2, lo, qab)))
                qab_ref[h] = qab.astype(BF16)

    out = jax.ShapeDtypeStruct((FOX_HEADS, t, LANE), BF16)
    hspec = pl.BlockSpec((FOX_HEADS, tq, LANE), lambda i: (0, i, 0))
    return _pcall(
        body, name=name, grid=(t // tq,),
        in_specs=[pl.BlockSpec((tq, FOX_W), lambda i: (i, 1)), pl.BlockSpec((tq, FOX_W), lambda i: (i, 0)),
                  pl.BlockSpec((FWD_GROUPS, tq, LANE), lambda i: (0, i, 0)), hspec],
        out_specs=[hspec, hspec], out_shape=[out, out],
        compiler_params=_params("parallel"),
    )(dmixed, o_fox, lse, qa)


def _fox_bwd(qab, doa, ka, va, dproj, name):
    nh, nq, tq, _ = qab.shape
    t = nq * tq
    slab = 3 * LANE * STEP_PAIRS
    group0 = (2 * RET_QK + 2 * RET_V) // slab

    def body(qab_ref, doa_ref, ka_ref, va_ref, dproj_in, dp_ref, drs_ref, dcs_ref, dq_ref):
        g, j = pl.program_id(0), pl.program_id(1)

        @pl.when((g == 0) & (j == 0))
        def _():
            drs_ref[...] = jnp.zeros_like(drs_ref)
            dcs_ref[...] = jnp.zeros_like(dcs_ref)

        @pl.when(j == 0)
        def _():
            dq_ref[...] = jnp.zeros_like(dq_ref)

        lane = lax.broadcasted_iota(jnp.int32, (tq, LANE), 1)
        low = lane < FOX_DH
        key_le_query = lax.broadcasted_iota(jnp.int32, (tq, tq), 0) <= lax.broadcasted_iota(jnp.int32, (tq, tq), 1)

        def by_head(c, a, b, col):
            h = STEP_HEADS * g + 2 * c
            return jnp.where(lane == h, a[:, col:col + 1], jnp.where(lane == h + 1, b[:, col:col + 1], 0.0))

        kbs = [ka_ref[h] for h in range(STEP_HEADS)]
        vbs = [va_ref[h] for h in range(STEP_HEADS)]

        def step(i, carry, diagonal):
            qbs = [qab_ref[h, i] for h in range(STEP_HEADS)]
            dobs = [doa_ref[h, i] for h in range(STEP_HEADS)]
            st = [_dot(kbs[h], qbs[h], NT) for h in range(STEP_HEADS)]
            dpt = [_dot(vbs[h], dobs[h], NT) for h in range(STEP_HEADS)]
            new = []
            for h in range(STEP_HEADS):
                p = jnp.exp(st[h])
                if diagonal:
                    p = jnp.where(key_le_query, p, 0.0)
                ds = (p * dpt[h]).astype(BF16)
                dq_ref[h, i] += _dot(ds, kbs[h], TN)
                dk, dv = carry[h]
                new.append((dk + _dot(ds, qbs[h]), dv + _dot(p.astype(BF16), dobs[h])))
            return tuple(new)

        zero = jnp.zeros((tq, LANE), F32)
        carry = step(j, tuple((zero, zero) for _ in range(STEP_HEADS)), True)
        carry = lax.fori_loop(j + 1, nq, lambda i, cr: step(i, cr, False), carry)
        rows = pl.ds(pl.multiple_of(j * tq, tq), tq)
        for c in range(STEP_PAIRS):
            (dka, dva), (dkb, dvb) = carry[2 * c], carry[2 * c + 1]
            c0 = 3 * LANE * c
            dp_ref[rows, c0 + LANE:c0 + 2 * LANE] = _pair(dka, dkb, low).astype(BF16)
            dp_ref[rows, c0 + 2 * LANE:c0 + 3 * LANE] = _pair(dva, dvb, low).astype(BF16)
            dcs_ref[rows, :] += by_head(c, dka, dkb, L_ONE_Q)

        @pl.when(j == nq - 1)
        def _():
            for c in range(STEP_PAIRS):
                for blk in range(nq):
                    r = slice(blk * tq, (blk + 1) * tq)
                    a, b = dq_ref[2 * c, blk], dq_ref[2 * c + 1, blk]
                    dp_ref[r, 3 * LANE * c:3 * LANE * c + LANE] = (_pair(a, b, low) * QK_SCALE).astype(BF16)
                    drs_ref[r, :] += by_head(c, a, b, L_ONE_K)

    whole = pl.BlockSpec((STEP_HEADS, nq, tq, LANE), lambda g, j: (g, 0, 0, 0), pipeline_mode=pl.Buffered(1))
    blk = pl.BlockSpec((STEP_HEADS, None, tq, LANE), lambda g, j: (g, j, 0, 0))
    sums = pl.BlockSpec((t, LANE), lambda g, j: (0, 0), pipeline_mode=pl.Buffered(1))
    return _pcall(
        body, name=name, grid=(FOX_GROUPS, nq),
        in_specs=[whole, whole, blk, blk, pl.BlockSpec(memory_space=pl.ANY)],
        out_specs=[pl.BlockSpec((t, slab), lambda g, j: (0, group0 + g)), sums, sums],
        out_shape=[jax.ShapeDtypeStruct(dproj.shape, BF16), jax.ShapeDtypeStruct((t, LANE), F32),
                   jax.ShapeDtypeStruct((t, LANE), F32)],
        input_output_aliases={4: 0},
        scratch_shapes=[pltpu.VMEM((STEP_HEADS, nq, tq, LANE), F32)],
        compiler_params=_params("arbitrary", "arbitrary"),
    )(qab, doa, ka, va, dproj)


HALO = 8


def _rows_ext(ref, r0, rows, t, before, after):
    lo, hi = r0 - before, r0 + rows + after
    width = ref.shape[-1]
    parts = []
    if lo < 0:
        parts.append(jnp.zeros((-lo, width), F32))
    parts.append(ref[max(lo, 0):min(hi, t), :].astype(F32))
    if hi > t:
        parts.append(jnp.zeros((hi - t, width), F32))
    return parts[0] if len(parts) == 1 else jnp.concatenate(parts, axis=0)


def _conv_taps(a_ext, r0_ext, cw_ref, cb_ref):
    n = a_ext.shape[0]
    if r0_ext < N_PAD:
        row = r0_ext + lax.broadcasted_iota(jnp.int32, (n, 1), 0)
        a_ext = jnp.where(row >= N_PAD, a_ext, 0.0)
    a1 = pltpu.roll(a_ext, 1, axis=0)
    a2 = pltpu.roll(a_ext, 2, axis=0)
    acc = cb_ref[...] + a2 * cw_ref[0:1, :] + a1 * cw_ref[1:2, :] + a_ext * cw_ref[2:3, :]
    return a_ext, a1, a2, acc


FF_COLS = 256


def _up_conv_fwd(n2, w_up_t, conv_w8, conv_b, name):
    t, d = n2.shape
    f = w_up_t.shape[1]
    rows = TOK_TILE
    starts = list(range(0, t, rows))

    def body(n_ref, wa_ref, wb_ref, cw_ref, cb_ref, up_ref, g_ref):
        wa, wb = wa_ref[...], wb_ref[...]

        def project(r0):
            n_rows = n_ref[r0:r0 + rows, :]
            up_ref[0, r0:r0 + rows, :] = _dot(n_rows, wa, NT)
            up_ref[1, r0:r0 + rows, :] = _dot(n_rows, wb, NT)

        def activate(r0):
            a_ext = _rows_ext(up_ref.at[0], r0, rows, t, HALO, 0)
            _, _, _, acc = _conv_taps(a_ext, r0 - HALO, cw_ref, cb_ref)
            acc = acc[HALO:, :]
            g_ref[r0:r0 + rows, :] = (acc * _sigmoid(acc) * up_ref[1, r0:r0 + rows, :]).astype(BF16)

        project(starts[0])
        for r0, r_next in zip(starts, starts[1:] + [None]):
            if r_next is not None:
                project(r_next)
            activate(r0)

    return _pcall(
        body, name=name, grid=(f // FF_COLS,),
        in_specs=[pl.BlockSpec((t, d), lambda j: (0, 0), pipeline_mode=pl.Buffered(1)),
                  pl.BlockSpec((None, FF_COLS, d), lambda j: (0, j, 0)), pl.BlockSpec((None, FF_COLS, d), lambda j: (1, j, 0)),
                  pl.BlockSpec((8, FF_COLS), lambda j: (0, j)), pl.BlockSpec((1, FF_COLS), lambda j: (0, j))],
        out_specs=[pl.BlockSpec((2, t, FF_COLS), lambda j: (0, 0, j)), pl.BlockSpec((t, FF_COLS), lambda j: (0, j))],
        out_shape=[jax.ShapeDtypeStruct((2, t, f), F32), jax.ShapeDtypeStruct((t, f), BF16)],
        compiler_params=_params("parallel"),
    )(n2, w_up_t, w_up_t, conv_w8, conv_b)


def _dg_conv_bwd(up, conv_w8, conv_b, dh2, w_down, name):
    _, t, f = up.shape
    d = dh2.shape[1]
    rows = TOK_TILE
    starts = list(range(0, t, rows))

    def body(a_ref, b_ref, cw_ref, cb_ref, dh_ref, wd_ref, dup_ref, gcw_ref, gcb_ref, dg_ref):
        wd = wd_ref[...]

        def project(r0):
            dg_ref[r0:r0 + rows, :] = _dot(dh_ref[r0:r0 + rows, :], wd, NT)

        gw = [jnp.zeros((1, FF_COLS), F32) for _ in range(3)]
        gb = jnp.zeros((1, FF_COLS), F32)
        project(starts[0])
        for r0, r_next in zip(starts, starts[1:] + [None]):
            if r_next is not None:
                project(r_next)
            a_ext = _rows_ext(a_ref, r0, rows, t, HALO, HALO)
            b_ext = _rows_ext(b_ref, r0, rows, t, HALO, HALO)
            dg_ext = _rows_ext(dg_ref, r0, rows, t, HALO, HALO)
            a0, a1, a2, acc = _conv_taps(a_ext, r0 - HALO, cw_ref, cb_ref)
            sg = _sigmoid(acc)
            dacc = dg_ext * b_ext * (sg * (1.0 + acc * (1.0 - sg)))
            n = dacc.shape[0]
            da = (dacc * cw_ref[2:3, :] + pltpu.roll(dacc, n - 1, axis=0) * cw_ref[1:2, :]
                  + pltpu.roll(dacc, n - 2, axis=0) * cw_ref[0:1, :])
            core = slice(HALO, HALO + rows)
            da = da[core, :]
            if r0 < N_PAD:
                row = r0 + lax.broadcasted_iota(jnp.int32, (rows, 1), 0)
                da = jnp.where(row >= N_PAD, da, 0.0)
            dup_ref[0, r0:r0 + rows, :] = da.astype(BF16)
            dup_ref[1, r0:r0 + rows, :] = (dg_ext * acc * sg)[core, :].astype(BF16)
            dacc_c = dacc[core, :]
            gw[0] = gw[0] + jnp.sum(dacc_c * a2[core, :], axis=0, keepdims=True)
            gw[1] = gw[1] + jnp.sum(dacc_c * a1[core, :], axis=0, keepdims=True)
            gw[2] = gw[2] + jnp.sum(dacc_c * a0[core, :], axis=0, keepdims=True)
            gb = gb + jnp.sum(dacc_c, axis=0, keepdims=True)
        gcw_ref[...] = jnp.zeros((8, FF_COLS), F32)
        for tap in range(3):
            gcw_ref[tap:tap + 1, :] = gw[tap]
        gcb_ref[...] = gb

    return _pcall(
        body, name=name, grid=(f // FF_COLS,),
        in_specs=[pl.BlockSpec((None, t, FF_COLS), lambda j: (0, 0, j)), pl.BlockSpec((None, t, FF_COLS), lambda j: (1, 0, j)),
                  pl.BlockSpec((8, FF_COLS), lambda j: (0, j)), pl.BlockSpec((1, FF_COLS), lambda j: (0, j)),
                  pl.BlockSpec((t, d), lambda j: (0, 0), pipeline_mode=pl.Buffered(1)),
                  pl.BlockSpec((FF_COLS, d), lambda j: (j, 0))],
        out_specs=[pl.BlockSpec((2, t, FF_COLS), lambda j: (0, 0, j)), pl.BlockSpec((8, FF_COLS), lambda j: (0, j)),
                   pl.BlockSpec((1, FF_COLS), lambda j: (0, j))],
        out_shape=[jax.ShapeDtypeStruct((2, t, f), BF16), jax.ShapeDtypeStruct((8, f), F32),
                   jax.ShapeDtypeStruct((1, f), F32)],
        scratch_shapes=[pltpu.VMEM((t, FF_COLS), F32)],
        compiler_params=_params("parallel"),
    )(up, up, conv_w8, conv_b, dh2, w_down)


def _exchange(arrays, kinds, name):
    n = len(arrays)
    npeer = N_DEV - 1

    def body(*refs):
        ins, outs = refs[:n], refs[n:2 * n]
        send_sems, recv_sems, local_sems = refs[2 * n:]
        x, y, c = lax.axis_index("x"), lax.axis_index("y"), lax.axis_index("c")
        me = 4 * x + 2 * y + c
        copies, locals_ = [], []
        for a in range(n):
            gather = kinds[a] == "gather"
            own = pltpu.make_async_copy(ins[a] if gather else ins[a].at[me], outs[a].at[me], local_sems.at[a])
            own.start()
            locals_.append(own)
            for d in range(1, N_DEV):
                px = 1 - x if d & 4 else x
                py = 1 - y if d & 2 else y
                pc = 1 - c if d & 1 else c
                src = ins[a] if gather else ins[a].at[4 * px + 2 * py + pc]
                cp = pltpu.make_async_remote_copy(
                    src_ref=src, dst_ref=outs[a].at[me],
                    send_sem=send_sems.at[a * npeer + d - 1], recv_sem=recv_sems.at[a * npeer + d - 1],
                    device_id=(px, py, pc), device_id_type=pl.DeviceIdType.MESH)
                cp.start()
                copies.append(cp)
        for cp in copies:
            cp.wait_recv()
        for cp in copies:
            cp.wait_send()
        for own in locals_:
            own.wait()

    out_shape = [jax.ShapeDtypeStruct((N_DEV,) + (a.shape if k == "gather" else a.shape[1:]), a.dtype)
                 for a, k in zip(arrays, kinds)]
    return _pcall(
        body, name=name,
        in_specs=[pl.BlockSpec(memory_space=pl.ANY)] * n,
        out_specs=[pl.BlockSpec(memory_space=pl.ANY)] * n,
        out_shape=out_shape,
        scratch_shapes=[pltpu.SemaphoreType.DMA((n * npeer,)), pltpu.SemaphoreType.DMA((n * npeer,)),
                        pltpu.SemaphoreType.DMA((n,))],
        compiler_params=pltpu.CompilerParams(has_side_effects=True),
    )(*arrays)


def _peer_copies(srcs, lands, kinds, send_sems, recv_sems):
    x, y, c = lax.axis_index("x"), lax.axis_index("y"), lax.axis_index("c")
    me = 4 * x + 2 * y + c
    copies = []
    for a in range(len(srcs)):
        for d in range(1, N_DEV):
            px = 1 - x if d & 4 else x
            py = 1 - y if d & 2 else y
            pc = 1 - c if d & 1 else c
            k = a * (N_DEV - 1) + d - 1
            copies.append(pltpu.make_async_remote_copy(
                src_ref=srcs[a] if kinds[a] == "gather" else srcs[a].at[4 * px + 2 * py + pc], dst_ref=lands[a].at[me],
                send_sem=send_sems.at[k], recv_sem=recv_sems.at[k],
                device_id=(px, py, pc), device_id_type=pl.DeviceIdType.MESH))
    return copies


def _exchange_start(arrays, kinds, name, after=None):
    n = len(arrays)
    nsem = n * (N_DEV - 1)
    hbm = pl.BlockSpec(memory_space=pltpu.HBM)
    sem = pl.BlockSpec(memory_space=pltpu.SEMAPHORE)
    land_shapes = [(N_DEV,) + (a.shape if k == "gather" else a.shape[1:]) for a, k in zip(arrays, kinds)]

    n_in = 2 * n + int(after is not None)

    def body(*refs):
        srcs, lands = refs[:n], refs[n:2 * n]
        send_sems, recv_sems = refs[n_in], refs[n_in + 1]
        token = refs[-1]
        for cp in _peer_copies(srcs, lands, kinds, send_sems, recv_sems):
            cp.start()
        token[...] = jnp.zeros_like(token)

    operands = [pltpu.with_memory_space_constraint(a, pltpu.HBM) for a in arrays]
    operands += [pltpu.with_memory_space_constraint(lax.empty(s, a.dtype), pltpu.HBM) for s, a in zip(land_shapes, arrays)]
    operands += [] if after is None else [after]
    out = _pcall(
        body, name=name,
        in_specs=[hbm] * (2 * n) + ([] if after is None else [pl.BlockSpec(memory_space=pl.ANY)]),
        out_specs=[sem, sem] + [hbm] * (2 * n) + [pl.BlockSpec(memory_space=pltpu.VMEM)],
        out_shape=[pltpu.SemaphoreType.DMA((nsem,)), pltpu.SemaphoreType.DMA((nsem,))]
        + [pltpu.HBM(a.shape, a.dtype) for a in arrays]
        + [pltpu.HBM(s, a.dtype) for s, a in zip(land_shapes, arrays)]
        + [jax.ShapeDtypeStruct((8, LANE), F32)],
        input_output_aliases={k: 2 + k for k in range(2 * n)},
        compiler_params=pltpu.CompilerParams(has_side_effects=pltpu.SideEffectType.DATAFLOW_SIDE_EFFECTING),
    )(*operands)
    return out[0], out[1], list(out[2:2 + n]), list(out[2 + n:2 + 2 * n]), out[-1]


def _exchange_wait(started, kinds, after, name):
    send_sems, recv_sems, srcs, lands, _ = started
    n = len(srcs)
    hbm = pl.BlockSpec(memory_space=pltpu.HBM)
    sem = pl.BlockSpec(memory_space=pltpu.SEMAPHORE)

    def body(*refs):
        src_refs, land_refs = refs[:n], refs[n:2 * n]
        copies = _peer_copies(src_refs, land_refs, kinds, refs[2 * n], refs[2 * n + 1])
        for cp in copies:
            cp.wait_send()
        for cp in copies:
            cp.wait_recv()

    out = _pcall(
        body, name=name,
        in_specs=[hbm] * (2 * n) + [sem, sem, pl.BlockSpec(memory_space=pl.ANY)],
        out_specs=[hbm] * (2 * n),
        out_shape=[pltpu.HBM(a.shape, a.dtype) for a in srcs + lands],
        input_output_aliases={k: k for k in range(2 * n)},
        compiler_params=pltpu.CompilerParams(has_side_effects=pltpu.SideEffectType.DATAFLOW_SIDE_EFFECTING),
    )(*srcs, *lands, send_sems, recv_sems, after)
    me = 4 * lax.axis_index("x") + 2 * lax.axis_index("y") + lax.axis_index("c")
    filled = []
    for src, land, kind in zip(out[:n], out[n:], kinds):
        own = src if kind == "gather" else lax.dynamic_index_in_dim(src, me, axis=0, keepdims=False)
        filled.append(lax.dynamic_update_slice(land, own[None], (me,) + (0,) * own.ndim))
    return filled


def _sum_slots(slots, name, rows_tile):
    nd, r, c = slots.shape

    def body(s_ref, o_ref):
        acc = s_ref[0].astype(F32)
        for p in range(1, nd):
            acc = acc + s_ref[p].astype(F32)
        o_ref[...] = acc

    return _pcall(
        body, name=name, grid=(r // rows_tile,),
        in_specs=[pl.BlockSpec((nd, rows_tile, c), lambda i: (0, i, 0))],
        out_specs=pl.BlockSpec((rows_tile, c), lambda i: (i, 0)),
        out_shape=jax.ShapeDtypeStruct((r, c), F32),
        compiler_params=_params("parallel"),
    )(slots)


def _sum_slots_small(slot_arrays, name):
    n = len(slot_arrays)

    def body(*refs):
        for s_ref, o_ref in zip(refs[:n], refs[n:]):
            acc = s_ref[0]
            for p in range(1, s_ref.shape[0]):
                acc = acc + s_ref[p]
            o_ref[...] = acc

    return _pcall(body, name=name, out_shape=[jax.ShapeDtypeStruct(a.shape[1:], F32) for a in slot_arrays])(*slot_arrays)


def _adamw_update(w_ref, g_ref, m_ref, v_ref, d_ref, nm_ref, nv_ref):
    gr = g_ref[...]
    nm = ADAM_B1 * m_ref[...] + (1.0 - ADAM_B1) * gr
    nv = ADAM_B2 * v_ref[...] + (1.0 - ADAM_B2) * (gr * gr)
    m_hat = nm / (1.0 - ADAM_B1 ** ADAM_STEP)
    v_hat = nv / (1.0 - ADAM_B2 ** ADAM_STEP)
    d_ref[...] = -ADAM_LR * (m_hat / (jnp.sqrt(v_hat) + ADAM_EPS) + ADAM_WD * w_ref[...])
    nm_ref[...] = nm
    nv_ref[...] = nv


def _adamw_small(ws, gs, ms, vs, name):
    n = len(ws)

    def body(*refs):
        ins, outs = refs[:4 * n], refs[4 * n:]
        for k in range(n):
            _adamw_update(ins[k], ins[n + k], ins[2 * n + k], ins[3 * n + k], outs[k], outs[n + k], outs[2 * n + k])

    shapes = [jax.ShapeDtypeStruct(w.shape, F32) for w in ws]
    out = _pcall(body, name=name, out_shape=shapes * 3)(*ws, *gs, *ms, *vs)
    return list(out[:n]), list(out[n:2 * n]), list(out[2 * n:])


def _adamw(w, g, m, v, name, rows_tile):
    r, c = w.shape
    body = lambda *refs: _adamw_update(*refs)
    spec = pl.BlockSpec((rows_tile, c), lambda i: (i, 0))
    shp = jax.ShapeDtypeStruct((r, c), F32)
    return _pcall(
        body, name=name, grid=(r // rows_tile,), in_specs=[spec] * 4, out_specs=[spec] * 3, out_shape=[shp] * 3,
        compiler_params=_params("parallel"),
    )(w, g, m, v)


F0 = 2 * RET_QK + 2 * RET_V


def _to_internal_rows(w_t):
    cols = w_t.shape[1]
    fox = w_t[F0:F0 + 3 * FOX_W].reshape(3, FOX_PAIRS, LANE, cols).transpose(1, 0, 2, 3).reshape(3 * FOX_W, cols)
    tail = jnp.zeros((IN_PAD - IN_WIDTH, cols), w_t.dtype)
    return jnp.concatenate([w_t[:F0], fox, w_t[F0 + 3 * FOX_W:], tail], axis=0)


def _from_internal_rows(g_t):
    cols = g_t.shape[1]
    fox = g_t[F0:F0 + 3 * FOX_W].reshape(FOX_PAIRS, 3, LANE, cols).transpose(1, 0, 2, 3).reshape(3 * FOX_W, cols)
    return jnp.concatenate([g_t[:F0], fox, g_t[F0 + 3 * FOX_W:F0 + 3 * FOX_W + FOX_HEADS]], axis=0)


def _local_step(x, target, meta, attn_g, fox_b, ret_g, ffn_g, conv_w8, conv_b, final_g,
                first_weight, late_weights, ffn_grads_ready, out_grad_ready, in_grad_ready):
    seq, d = x.shape
    t = seq + PREFIX
    tm = TOK_TILE
    nq = t // tm
    fox_b128 = jnp.pad(fox_b, ((0, 0), (0, LANE - FOX_HEADS)))

    h0, n1 = _prep_norm(x, meta, attn_g, "prep_norm")
    w_in_t = first_weight(n1)
    proj = _mm_simple(n1, w_in_t, mode="nt", tm=tm, tn=IN_PAD, tk=d, out_dtype=F32, name="mm_in")
    cos, sin = _rope_tables(t)
    o_pre, mixed, states = _ret_fwd(proj, cos, sin, ret_g, "ret_fwd")
    c = _forget_cumsum(proj, fox_b128, "forget_cumsum")
    qa, ka, va = _fox_prep(proj, c, "fox_prep")
    by_block = lambda a: a.reshape(FOX_HEADS, nq, tm, LANE)
    mixed, o_fox, lse = _fox_fwd(by_block(qa), by_block(ka), by_block(va), mixed, "fox_fwd")
    w_out, w_up_t, w_down = late_weights(o_fox)
    tile = pl.BlockSpec((tm, d), lambda i: (i, 0))
    row_vec = pl.BlockSpec((1, d), lambda i: (0, 0))
    resident = lambda shape: pl.BlockSpec(shape, lambda i: (0,) * len(shape), pipeline_mode=pl.Buffered(1))
    acts = lambda dtype: jax.ShapeDtypeStruct((t, d), dtype)
    vec = jax.ShapeDtypeStruct((1, d), F32)

    def residual_and_norm(i, acc, ins, outs):
        h = acc + ins[0][...]
        outs[0][...] = h
        outs[1][...] = (h * lax.rsqrt(jnp.mean(h * h, axis=-1, keepdims=True) + EPS) * ins[1][...]).astype(BF16)

    h1, n2 = _matmul_rows([mixed], [tile], [w_out], [resident((d, d))], [h0, ffn_g], [tile, row_vec],
                          [tile, tile], [acts(F32), acts(BF16)], residual_and_norm, mode="nn", steps=nq, name="mm_out_norm")
    nf = D_FF // 1408
    up, g = _up_conv_fwd(n2, w_up_t, conv_w8, conv_b, "up_conv_fwd")

    def residual_loss_bwd(i, acc, ins, outs):
        loss_ref, dh_ref, dhb_ref, gg_ref = outs
        part, dh, gg = _loss_tile(i, acc + ins[0][...], jnp.concatenate([ins[1][...], ins[2][...], ins[3][...]], axis=0),
                                  ins[4][...])
        _accumulate(loss_ref, i, jnp.broadcast_to(part, loss_ref.shape))
        dh_ref[...] = dh
        dhb_ref[...] = dh.astype(BF16)
        _accumulate(gg_ref, i, gg)

    loss_tile, dh2, dh2_b, g_final = _matmul_rows(
        [g], [pl.BlockSpec((tm, D_FF), lambda i: (i, 0))], [w_down], [resident((D_FF, d))],
        [h1, target, target, target, final_g], [tile] + _shifted_row_specs(d) + [row_vec],
        [pl.BlockSpec((8, LANE), lambda i: (0, 0)), tile, tile, row_vec],
        [jax.ShapeDtypeStruct((8, LANE), F32), acts(F32), acts(BF16), vec], residual_loss_bwd,
        mode="nn", steps=nq, name="mm_down_loss")

    tkw = 1408 if t % 1408 == 0 else tm
    gw_down = _mm_simple(g, dh2_b, mode="tn", tm=1408, tn=d, tk=tkw, out_dtype=BF16, name="mm_gw_down")
    dup, g_conv_w8, g_conv_b = _dg_conv_bwd(up, conv_w8, conv_b, dh2_b, w_down, "dg_conv_bwd")

    def norm_bwd(i, acc, ins, outs):
        dh, gg = _rms_bwd_tile(acc, ins[0][...], ins[1][...], ins[2][...])
        outs[0][...] = dh
        _accumulate(outs[1], i, gg)

    half = lambda p: pl.BlockSpec((None, tm, D_FF), lambda i: (p, i, 0))
    half_w = lambda p: pl.BlockSpec((None, D_FF, d), lambda i: (p, 0, 0), pipeline_mode=pl.Buffered(1))
    gw_up_t = _matmul(
        dup, n2, mode="tn", grid=(2 * nf, 1, t // tkw),
        a_spec=pl.BlockSpec((None, tkw, 1408), lambda i, j, k: (i // nf, k, i % nf)),
        b_spec=pl.BlockSpec((tkw, d), lambda i, j, k: (k, 0)),
        o_spec=pl.BlockSpec((1408, d), lambda i, j, k: (i, 0)),
        out_shape=jax.ShapeDtypeStruct((2 * D_FF, d), BF16), name="mm_gw_up")
    dh1, g_ffn = _matmul_rows(
        [dup, dup], [half(0), half(1)], [w_up_t, w_up_t], [half_w(0), half_w(1)],
        [h1, ffn_g, dh2], [tile, row_vec, tile], [tile, row_vec], [acts(F32), vec], norm_bwd,
        mode="nn", steps=nq, name="mm_dn2_norm_bwd", after=ffn_grads_ready(gw_down, gw_up_t))

    dmixed = _mm_simple(dh1, w_out, mode="nt", tm=tm, tn=d, tk=d, out_dtype=F32, name="mm_dmixed")
    gw_out = _mm_simple(mixed, dh1, mode="tn", tm=d, tn=d, tk=tkw, out_dtype=BF16, name="mm_gw_out")
    dproj, g_ret = _ret_bwd(proj, cos, sin, ret_g + out_grad_ready(gw_out), dmixed, o_pre, states, "ret_bwd")
    qab, doa = _fox_prep_bwd(dmixed, o_fox, lse, qa, "fox_prep_bwd")
    dproj, drs, dcs = _fox_bwd(by_block(qab), by_block(doa), by_block(ka), by_block(va), dproj, "fox_bwd")
    dproj, g_fox_b = _forget_cumsum_bwd(proj, fox_b128, drs, dcs, dproj, "forget_cumsum_bwd")
    gw_in_t = _mm_simple(dproj, n1, mode="tn", tm=640, tn=d, tk=tkw, out_dtype=BF16, name="mm_gw_in")
    sent = in_grad_ready(gw_in_t)
    dh0, g_attn = _matmul_rows(
        [dproj], [pl.BlockSpec((tm, IN_PAD), lambda i: (i, 0))], [w_in_t], [resident((IN_PAD, d))],
        [h0, attn_g, dh1], [tile, row_vec, tile], [tile, row_vec], [acts(F32), vec], norm_bwd,
        mode="nn", steps=nq, name="mm_dn1_norm_bwd", after=sent)

    grads = dict(meta=dh0[N_PAD:PREFIX], attn_g=g_attn, fox_b=g_fox_b, ret_g=g_ret,
                 ffn_g=g_ffn, conv_w=g_conv_w8, conv_b=g_conv_b, final_g=g_final)
    return loss_tile, dh0[PREFIX:], grads


def kernel(x, meta_tokens, attn_norm_g, w_in, fox_forget_b, ret_norm_g, w_out, ffn_norm_g, w_up, conv_w, conv_b, w_down, final_norm_g, loss_target, m_meta_tokens, m_attn_norm_g, m_w_in, m_fox_forget_b, m_ret_norm_g, m_w_out, m_ffn_norm_g, m_w_up, m_conv_w, m_conv_b, m_w_down, m_final_norm_g, v_meta_tokens, v_attn_norm_g, v_w_in, v_fox_forget_b, v_ret_norm_g, v_w_out, v_ffn_norm_g, v_w_up, v_conv_w, v_conv_b, v_w_down, v_final_norm_g):
    d = D_MODEL
    me = 4 * lax.axis_index("x") + 2 * lax.axis_index("y") + lax.axis_index("c")
    in_blk = IN_WIDTH // N_DEV
    in_blk_pad = 400
    up_blk = 2 * D_FF // N_DEV
    down_blk = D_FF // N_DEV
    cw_blk = D_FF // N_DEV

    w_in_loc = jnp.pad(w_in[0].T.astype(BF16), ((0, in_blk_pad - in_blk), (0, 0)))
    cw_loc = jnp.pad(conv_w[0], ((0, 5), (0, 384 - cw_blk)))
    g_meta, g_cw = _exchange([meta_tokens, cw_loc], ["gather"] * 2, "gather_small")
    first = _exchange_start([w_in_loc], ["gather"], "gather_in_start", after=g_meta)
    rest_loc = [(w_out[0] + first[-1][0:1, 0:1]).astype(BF16), w_up[0].T.astype(BF16), w_down[0].astype(BF16)]
    rest = _exchange_start(rest_loc, ["gather"] * 3, "gather_rest_start")
    meta_f = g_meta.transpose(1, 0, 2).reshape(N_META, d)
    conv_w8 = jnp.pad(g_cw[:, :3, :cw_blk].transpose(1, 0, 2).reshape(3, D_FF), ((0, 5), (0, 0)))
    pending = {}

    def first_weight(after):
        (g_in,) = _exchange_wait(first, ["gather"], after, "gather_in_wait")
        return _to_internal_rows(g_in[:, :in_blk].reshape(IN_WIDTH, d))

    def in_grad_ready(gw_in_t):
        blocks = _from_internal_rows(gw_in_t).reshape(N_DEV, in_blk, d)
        blocks = jnp.pad(blocks, ((0, 0), (0, in_blk_pad - in_blk), (0, 0)))
        pending["in"] = _exchange_start([blocks], ["scatter"], "grads_in_start")
        return pending["in"][-1][0:1, 0:1]

    def late_weights(after):
        g_out, g_up, g_down = _exchange_wait(rest, ["gather"] * 3, after, "gather_rest_wait")
        return g_out.reshape(d, d), g_up.reshape(2, D_FF, d), g_down.reshape(D_FF, d)

    def ffn_grads_ready(gw_down, gw_up_t):
        blocks = [gw_down.reshape(N_DEV, down_blk, d), gw_up_t.reshape(N_DEV, up_blk, d)]
        pending["ffn"] = _exchange_start(blocks, ["scatter"] * 2, "grads_ffn_start")
        return pending["ffn"][-1][0:1, 0:1]

    def out_grad_ready(gw_out):
        pending["out"] = _exchange_start([gw_out.reshape(N_DEV, d // N_DEV, d)], ["scatter"], "grads_out_start")
        return pending["out"][-1][0:1, 0:1]

    loss_tile, grad_x, gr = _local_step(
        x[0], loss_target[0], meta_f, attn_norm_g + rest[-1][0:1, 0:1], fox_forget_b, ret_norm_g, ffn_norm_g,
        conv_w8, conv_b, final_norm_g.reshape(1, d), first_weight, late_weights, ffn_grads_ready, out_grad_ready,
        in_grad_ready)

    small = [loss_tile, gr["attn_g"], gr["fox_b"], gr["ret_g"], gr["ffn_g"], gr["conv_b"], gr["final_g"],
             gr["meta"], gr["conv_w"]]
    r_small = _exchange(small, ["gather"] * len(small), "exchange_small")
    r_down, r_up = _exchange_wait(pending["ffn"], ["scatter"] * 2, r_small[0], "grads_ffn_wait")
    (r_out,) = _exchange_wait(pending["out"], ["scatter"], r_small[0], "grads_out_wait")
    g_w_out = _sum_slots(r_out, "sum_w_out", d // N_DEV)
    g_w_up_t = _sum_slots(r_up, "sum_w_up", up_blk)
    g_w_down = _sum_slots(r_down, "sum_w_down", down_blk)
    (loss_all, g_attn, g_fox_b128, g_ret, g_ffn, g_conv_b, g_final, g_meta_full, g_cw_full) = _sum_slots_small(
        r_small, "sum_small")
    loss = loss_all[0, 0]
    g_fox_b = g_fox_b128[:, :FOX_HEADS]
    g_meta_loc = lax.dynamic_slice(g_meta_full, (0, me * (d // N_DEV)), (N_META, d // N_DEV))
    g_cw_loc = lax.dynamic_slice(g_cw_full, (0, me * cw_blk), (3, cw_blk))

    as_t = lambda a: a[0].T
    from_t = lambda a: a.T[None]
    d_w_out, m_w_out_n, v_w_out_n = [a[None] for a in _adamw(w_out[0], g_w_out, m_w_out[0], v_w_out[0], "adamw_w_out", 128)]
    up_t = _adamw(as_t(w_up), g_w_up_t, as_t(m_w_up), as_t(v_w_up), "adamw_w_up", up_blk // 2)
    d_w_up, m_w_up_n, v_w_up_n = [from_t(a) for a in up_t]
    d_w_down, m_w_down_n, v_w_down_n = [a[None] for a in _adamw(w_down[0], g_w_down, m_w_down[0], v_w_down[0],
                                                                "adamw_w_down", down_blk)]
    (r_in,) = _exchange_wait(pending["in"], ["scatter"], up_t[0], "grads_in_wait")
    g_w_in_t = _sum_slots(r_in, "sum_w_in", in_blk_pad)[:in_blk]
    d_w_in, m_w_in_n, v_w_in_n = [from_t(a) for a in _adamw(as_t(w_in), g_w_in_t, as_t(m_w_in), as_t(v_w_in),
                                                            "adamw_w_in", in_blk)]
    g_w_in, g_w_up = g_w_in_t.T, g_w_up_t.T
    row = lambda a: a.reshape(1, d)
    sm_grads = [g_meta_loc, g_attn, g_fox_b, g_ret, g_ffn, g_cw_loc, g_conv_b, g_final]
    sm_w = [meta_tokens, attn_norm_g, fox_forget_b, ret_norm_g, ffn_norm_g, conv_w[0], conv_b, row(final_norm_g)]
    sm_m = [m_meta_tokens, m_attn_norm_g, m_fox_forget_b, m_ret_norm_g, m_ffn_norm_g, m_conv_w[0], m_conv_b,
            row(m_final_norm_g)]
    sm_v = [v_meta_tokens, v_attn_norm_g, v_fox_forget_b, v_ret_norm_g, v_ffn_norm_g, v_conv_w[0], v_conv_b,
            row(v_final_norm_g)]
    dl, ml, vl = [lst[:7] + [lst[7].reshape(d)] for lst in _adamw_small(sm_w, sm_grads, sm_m, sm_v, "adamw_small")]

    def by_weight(meta_, attn_, w_in_, fox_, ret_, w_out_, ffn_, w_up_, cw_, cb_, w_down_, final_):
        return (meta_, attn_, w_in_, fox_, ret_, w_out_, ffn_, w_up_, cw_[None], cb_, w_down_, final_)

    grads_out = by_weight(g_meta_loc, g_attn, g_w_in[None], g_fox_b, g_ret, g_w_out[None], g_ffn, g_w_up[None], g_cw_loc,
                          g_conv_b, g_w_down[None], g_final.reshape(d))
    delta_out = by_weight(dl[0], dl[1], d_w_in, dl[2], dl[3], d_w_out, dl[4], d_w_up, dl[5], dl[6], d_w_down, dl[7])
    m_out = by_weight(ml[0], ml[1], m_w_in_n, ml[2], ml[3], m_w_out_n, ml[4], m_w_up_n, ml[5], ml[6], m_w_down_n, ml[7])
    v_out = by_weight(vl[0], vl[1], v_w_in_n, vl[2], vl[3], v_w_out_n, vl[4], v_w_up_n, vl[5], vl[6], v_w_down_n, vl[7])
    return (loss, grad_x[None]) + grads_out + delta_out + m_out + v_out
```

```python
import numpy as np
import jax
import jax.numpy as jnp
from jax import lax
from jax.experimental import pallas as pl
from jax.experimental.pallas import tpu as pltpu

F32 = jnp.float32
BF16 = jnp.bfloat16

D_MODEL = 1024
N_META = 16
N_PAD = 112
PREFIX = 128
RET_HEADS = 4
RET_DK = 64
RET_DV = 128
FOX_HEADS = 8
FOX_DH = 64
D_FF = 2816
ROPE_BASE = 10000.0
EPS = 1e-6
NEG = -1e30
RET_QK = RET_HEADS * RET_DK
RET_V = RET_HEADS * RET_DV
FOX_W = FOX_HEADS * FOX_DH
IN_WIDTH = 2 * RET_QK + 2 * RET_V + 3 * FOX_W + FOX_HEADS
IN_PAD = 3200
FF_COL_BLOCK = (IN_WIDTH - FOX_HEADS) // 128
QK_SCALE = 0.125

ADAM_LR = 0.001
ADAM_B1 = 0.9
ADAM_B2 = 0.999
ADAM_EPS = 1e-08
ADAM_WD = 0.01
ADAM_STEP = 10

N_DEV = 8
LANE = 128
ROW_TILE = 128
TOK_TILE = 384

NN = (((1,), (0,)), ((), ()))
NT = (((1,), (1,)), ((), ()))
TN = (((0,), (0,)), ((), ()))


def _pcall(body, **kw):
    return pl.pallas_call(body, **kw)


def _params(*sem):
    return pltpu.CompilerParams(dimension_semantics=sem)


def _dot(a, b, dims=NN):
    return lax.dot_general(a, b, dims, preferred_element_type=F32)


def _sigmoid(x):
    return 0.5 * jnp.tanh(0.5 * x) + 0.5


def _matmul(a, b, *, mode, grid, a_spec, b_spec, o_spec, out_shape, name, add=None, add_spec=None, after=None):
    dims = {"nn": NN, "nt": NT, "tn": TN}[mode]
    nk = grid[2]
    has_add = add is not None
    a_list, b_list = (list(a), list(b)) if isinstance(a, (list, tuple)) else ([a], [b])
    a_specs, b_specs = (list(a_spec), list(b_spec)) if isinstance(a_spec, (list, tuple)) else ([a_spec], [b_spec])
    nt = len(a_list)
    n_in = 2 * nt + int(has_add) + int(after is not None)

    def body(*refs):
        a_refs, b_refs = refs[:nt], refs[nt:2 * nt]
        add_ref = refs[2 * nt] if has_add else None
        o_ref = refs[n_in]
        part = _dot(a_refs[0][...].astype(BF16), b_refs[0][...].astype(BF16), dims)
        for ar, br in zip(a_refs[1:], b_refs[1:]):
            part = part + _dot(ar[...].astype(BF16), br[...].astype(BF16), dims)

        def finish(acc):
            if has_add:
                acc = acc + add_ref[...]
            o_ref[...] = acc.astype(o_ref.dtype)

        if nk == 1:
            finish(part)
        else:
            acc_ref = refs[-1]
            k = pl.program_id(2)

            @pl.when(k == 0)
            def _():
                acc_ref[...] = part

            @pl.when(k > 0)
            def _():
                acc_ref[...] += part

            @pl.when(k == nk - 1)
            def _():
                finish(acc_ref[...])

    in_specs = a_specs + b_specs + ([add_spec] if has_add else [])
    args = tuple(a_list) + tuple(b_list) + ((add,) if has_add else ())
    if after is not None:
        in_specs, args = in_specs + [pl.BlockSpec(memory_space=pl.ANY)], args + (after,)
    scratch = [] if nk == 1 else [pltpu.VMEM(tuple(d for d in o_spec.block_shape if d is not None), F32)]
    return _pcall(
        body, name=name, grid=grid, in_specs=in_specs, out_specs=o_spec, out_shape=out_shape,
        scratch_shapes=scratch, compiler_params=_params("parallel", "parallel", "arbitrary"),
    )(*args)


def _mm_simple(a, b, *, mode, tm, tn, tk, out_dtype, name, add=None, after=None):
    if mode == "tn":
        K, M = a.shape
    else:
        M, K = a.shape
    N = b.shape[0] if mode == "nt" else b.shape[1]
    grid = (M // tm, N // tn, K // tk)
    resident = dict(pipeline_mode=pl.Buffered(1)) if (tn == N and tk == K) else {}
    a_spec = pl.BlockSpec((tk, tm), lambda i, j, k: (k, i)) if mode == "tn" else pl.BlockSpec((tm, tk), lambda i, j, k: (i, k))
    b_spec = (pl.BlockSpec((tn, tk), lambda i, j, k: (j, k), **resident) if mode == "nt"
              else pl.BlockSpec((tk, tn), lambda i, j, k: (k, j), **resident))
    o_spec = pl.BlockSpec((tm, tn), lambda i, j, k: (i, j))
    return _matmul(a, b, mode=mode, grid=grid, a_spec=a_spec, b_spec=b_spec, o_spec=o_spec,
                   out_shape=jax.ShapeDtypeStruct((M, N), out_dtype), name=name, add=add,
                   add_spec=o_spec if add is not None else None, after=after)


def _matmul_rows(a_list, a_specs, b_list, b_specs, extras, extra_specs, out_specs, out_shape, epilogue, *,
                 mode, steps, name, after=None):
    dims = {"nn": NN, "nt": NT}[mode]
    nt, ne = len(a_list), len(extras)
    n_in = 2 * nt + ne + int(after is not None)

    def body(*refs):
        acc = _dot(refs[0][...].astype(BF16), refs[nt][...].astype(BF16), dims)
        for k in range(1, nt):
            acc = acc + _dot(refs[k][...].astype(BF16), refs[nt + k][...].astype(BF16), dims)
        epilogue(pl.program_id(0), acc, refs[2 * nt:2 * nt + ne], refs[n_in:])

    in_specs = list(a_specs) + list(b_specs) + list(extra_specs)
    args = tuple(a_list) + tuple(b_list) + tuple(extras)
    if after is not None:
        in_specs, args = in_specs + [pl.BlockSpec(memory_space=pl.ANY)], args + (after,)
    return _pcall(body, name=name, grid=(steps,), in_specs=in_specs, out_specs=out_specs, out_shape=out_shape,
                  compiler_params=_params("arbitrary"))(*args)


def _rms_bwd_tile(dy, x, gain, dres):
    r = lax.rsqrt(jnp.mean(x * x, axis=-1, keepdims=True) + EPS)
    xhat = x * r
    u = dy * gain
    return dres + r * (u - xhat * jnp.mean(u * xhat, axis=-1, keepdims=True)), jnp.sum(dy * xhat, axis=0, keepdims=True)


def _loss_tile(i, x, tgt, gain):
    d = x.shape[-1]
    r = lax.rsqrt(jnp.mean(x * x, axis=-1, keepdims=True) + EPS)
    xhat = x * r
    counted = (i * TOK_TILE + lax.broadcasted_iota(jnp.int32, (TOK_TILE, 1), 0)) >= PREFIX
    err = jnp.where(counted, xhat * gain - tgt, 0.0)
    dy = err * (1.0 / d)
    u = dy * gain
    dh = r * (u - xhat * jnp.mean(u * xhat, axis=-1, keepdims=True))
    return 0.5 * jnp.sum(jnp.mean(err * err, axis=-1, keepdims=True)), dh, jnp.sum(dy * xhat, axis=0, keepdims=True)


def _accumulate(ref, i, part):
    @pl.when(i == 0)
    def _():
        ref[...] = part

    @pl.when(i > 0)
    def _():
        ref[...] += part


def _prep_norm(x, meta, gain, name):
    seq, d = x.shape
    t = seq + PREFIX

    def body(xa_ref, xb_ref, xc_ref, meta_ref, g_ref, h_ref, n_ref):
        i = pl.program_id(0)

        @pl.when(i == 0)
        def _():
            h_ref[0:N_PAD, :] = jnp.zeros((N_PAD, d), F32)
            h_ref[N_PAD:ROW_TILE, :] = meta_ref[...]

        @pl.when(i > 0)
        def _():
            h_ref[0:ROW_TILE, :] = xa_ref[...]

        h_ref[ROW_TILE:2 * ROW_TILE, :] = xb_ref[...]
        h_ref[2 * ROW_TILE:3 * ROW_TILE, :] = xc_ref[...]
        h = h_ref[...]
        r = lax.rsqrt(jnp.mean(h * h, axis=-1, keepdims=True) + EPS)
        n_ref[...] = (h * r * g_ref[...]).astype(BF16)

    return _pcall(
        body, name=name, grid=(t // TOK_TILE,),
        in_specs=_shifted_row_specs(d) + [pl.BlockSpec((N_META, d), lambda i: (0, 0)), pl.BlockSpec((1, d), lambda i: (0, 0))],
        out_specs=[pl.BlockSpec((TOK_TILE, d), lambda i: (i, 0)), pl.BlockSpec((TOK_TILE, d), lambda i: (i, 0))],
        out_shape=[jax.ShapeDtypeStruct((t, d), F32), jax.ShapeDtypeStruct((t, d), BF16)],
        compiler_params=_params("parallel"),
    )(x, x, x, meta, gain)


def _shifted_row_specs(d):
    blocks_per_tile = TOK_TILE // ROW_TILE
    return [pl.BlockSpec((ROW_TILE, d), lambda i, r=r: (jnp.maximum(blocks_per_tile * i + r, 0), 0)) for r in (-1, 0, 1)]


def _ret_consts(bk):
    gam = 1.0 - 2.0 ** (-5.0 - np.arange(RET_HEADS))
    n = np.arange(bk)
    same_or_earlier_chunk = (n[None, :] // 64) <= (n[:, None] // 64)
    w = gam[:, None, None] ** np.abs(n[:, None] - n[None, :])[None] * same_or_earlier_chunk[None]
    wq = gam[:, None] ** (n[None, :] + 1.0)
    wk = gam[:, None] ** (bk - 1.0 - n[None, :])
    mask = (np.arange(RET_QK)[None, :] // RET_DK) == np.arange(RET_HEADS)[:, None]
    return (jnp.asarray(w, F32), jnp.asarray(wq[:, :, None], F32), jnp.asarray(wk[:, :, None], F32),
            jnp.asarray(mask[:, None, :], F32), [float(g ** bk) for g in gam])


def _rope_tables(t):
    half = RET_DK // 2
    inv = 1.0 / (ROPE_BASE ** (jnp.arange(half, dtype=F32) / half))
    ang = jnp.arange(t).astype(F32)[:, None] * inv[None, :]
    cos, sin = jnp.cos(ang), jnp.sin(ang)
    return (jnp.tile(jnp.concatenate([cos, cos], axis=1), (1, RET_HEADS)),
            jnp.tile(jnp.concatenate([-sin, sin], axis=1), (1, RET_HEADS)))


def _swap_halves(x):
    outs = []
    for s in range(x.shape[1] // LANE):
        xs = x[:, LANE * s:LANE * (s + 1)]
        lane = lax.broadcasted_iota(jnp.int32, xs.shape, 1)
        outs.append(jnp.where((lane & 32) == 0, pltpu.roll(xs, LANE - 32, axis=1), pltpu.roll(xs, 32, axis=1)))
    return outs[0] if len(outs) == 1 else jnp.concatenate(outs, axis=1)


def _rope(x, cos, sin_signed):
    return x * cos + _swap_halves(x) * sin_signed


def _rope_t(dx, cos, sin_signed):
    return dx * cos + _swap_halves(dx * sin_signed)


def _ret_fwd(proj, cos, sin, gain, name):
    t = proj.shape[0]
    bk = TOK_TILE
    nb = t // bk
    w, wq, wk, mask, g_blk = _ret_consts(bk)

    def body(q_ref, k_ref, v_ref, rg_ref, cos_ref, sin_ref, w_ref, wq_ref, wk_ref, mask_ref, gain_ref,
             opre_ref, og_ref, st_ref, r_ref):
        i = pl.program_id(0)

        @pl.when(i == 0)
        def _():
            r_ref[...] = jnp.zeros_like(r_ref)

        c, s = cos_ref[...], sin_ref[...]
        valid = ((i * bk + lax.broadcasted_iota(jnp.int32, (bk, 1), 0)) >= N_PAD).astype(F32)
        qr = _rope(q_ref[...], c, s)
        kr = _rope(k_ref[...], c, s) * QK_SCALE * valid
        kb = kr.astype(BF16)
        for h in range(RET_HEADS):
            hm = mask_ref[h]
            cols = slice(RET_DV * h, RET_DV * (h + 1))
            vh = v_ref[:, cols].astype(BF16)
            r_prev = r_ref[h]
            st_ref[0, h] = r_prev
            sm = _dot((qr * hm).astype(BF16), kb, NT) * w_ref[h]
            o = _dot(sm.astype(BF16), vh) + _dot((qr * (hm * wq_ref[h])).astype(BF16), r_prev.astype(BF16))
            r_ref[h] = g_blk[h] * r_prev + _dot((kr * wk_ref[h]).astype(BF16), vh, TN)
            opre_ref[:, cols] = o
            rstd = lax.rsqrt(jnp.mean(o * o, axis=-1, keepdims=True) + EPS)
            rg = rg_ref[:, cols]
            og_ref[:, cols] = (o * rstd * gain_ref[:, cols] * (rg * _sigmoid(rg))).astype(BF16)

    full = lambda shape: pl.BlockSpec(shape, lambda i: (0,) * len(shape))
    return _pcall(
        body, name=name, grid=(nb,),
        in_specs=[pl.BlockSpec((bk, RET_QK), lambda i: (i, 0)), pl.BlockSpec((bk, RET_QK), lambda i: (i, 1)),
                  pl.BlockSpec((bk, RET_V), lambda i: (i, 1)), pl.BlockSpec((bk, RET_V), lambda i: (i, 2)),
                  pl.BlockSpec((bk, RET_QK), lambda i: (i, 0)), pl.BlockSpec((bk, RET_QK), lambda i: (i, 0)),
                  full((RET_HEADS, bk, bk)), full((RET_HEADS, bk, 1)), full((RET_HEADS, bk, 1)),
                  full((RET_HEADS, 1, RET_QK)), full((1, RET_V))],
        out_specs=[pl.BlockSpec((bk, RET_V), lambda i: (i, 0)), pl.BlockSpec((bk, RET_V), lambda i: (i, 0)),
                   pl.BlockSpec((1, RET_HEADS, RET_QK, RET_DV), lambda i: (i, 0, 0, 0))],
        out_shape=[jax.ShapeDtypeStruct((t, RET_V), F32), jax.ShapeDtypeStruct((t, RET_V + FOX_W), BF16),
                   jax.ShapeDtypeStruct((nb, RET_HEADS, RET_QK, RET_DV), F32)],
        scratch_shapes=[pltpu.VMEM((RET_HEADS, RET_QK, RET_DV), F32)],
        compiler_params=_params("arbitrary"),
    )(proj, proj, proj, proj, cos, sin, w, wq, wk, mask, gain)


def _ret_bwd(proj, cos, sin, gain, dmixed, opre, states, name):
    t = proj.shape[0]
    bk = TOK_TILE
    nb = t // bk
    w, wq, wk, mask, g_blk = _ret_consts(bk)
    v0, g0 = 2 * RET_QK, 2 * RET_QK + RET_V

    def body(q_ref, k_ref, v_ref, rg_ref, cos_ref, sin_ref, w_ref, wq_ref, wk_ref, mask_ref, gain_ref,
             dog_ref, opre_ref, st_ref, dp_ref, gg_ref, dr_ref):
        step = pl.program_id(0)
        i = nb - 1 - step

        @pl.when(step == 0)
        def _():
            dr_ref[...] = jnp.zeros_like(dr_ref)
            gg_ref[...] = jnp.zeros_like(gg_ref)

        c, s = cos_ref[...], sin_ref[...]
        valid = ((i * bk + lax.broadcasted_iota(jnp.int32, (bk, 1), 0)) >= N_PAD).astype(F32)
        qr = _rope(q_ref[...], c, s)
        kr = _rope(k_ref[...], c, s) * QK_SCALE * valid
        kb = kr.astype(BF16)
        dqr = jnp.zeros((bk, RET_QK), F32)
        dkr = jnp.zeros((bk, RET_QK), F32)
        for h in range(RET_HEADS):
            hm = mask_ref[h]
            cols = slice(RET_DV * h, RET_DV * (h + 1))
            vh = v_ref[:, cols].astype(BF16)
            o = opre_ref[:, cols]
            rstd = lax.rsqrt(jnp.mean(o * o, axis=-1, keepdims=True) + EPS)
            xhat = o * rstd
            rg = rg_ref[:, cols]
            sg = _sigmoid(rg)
            gate = rg * sg
            gn = gain_ref[:, cols]
            dog = dog_ref[:, cols]
            dp_ref[:, g0 + RET_DV * h:g0 + RET_DV * (h + 1)] = (
                dog * xhat * gn * (sg * (1.0 + rg * (1.0 - sg)))).astype(BF16)
            gg_ref[:, cols] += jnp.sum(dog * xhat * gate, axis=0, keepdims=True)
            dxh = dog * gn * gate
            do = (rstd * (dxh - xhat * jnp.mean(dxh * xhat, axis=-1, keepdims=True))).astype(BF16)
            qm = (qr * hm).astype(BF16)
            qw = (qr * (hm * wq_ref[h])).astype(BF16)
            kw = (kr * wk_ref[h]).astype(BF16)
            wh = w_ref[h]
            sm = (_dot(qm, kb, NT) * wh).astype(BF16)
            ds = (_dot(do, vh, NT) * wh).astype(BF16)
            dr = dr_ref[h]
            drb = dr.astype(BF16)
            dp_ref[:, v0 + RET_DV * h:v0 + RET_DV * (h + 1)] = (_dot(sm, do, TN) + _dot(kw, drb)).astype(BF16)
            dqr = dqr + _dot(ds, kb) * hm + _dot(do, st_ref[0, h].astype(BF16), NT) * (hm * wq_ref[h])
            dkr = dkr + _dot(ds, qm, TN) + _dot(vh, drb, NT) * wk_ref[h]
            dr_ref[h] = g_blk[h] * dr + _dot(qw, do, TN)
        dp_ref[:, 0:RET_QK] = _rope_t(dqr, c, s).astype(BF16)
        dp_ref[:, RET_QK:2 * RET_QK] = _rope_t(dkr * (QK_SCALE * valid), c, s).astype(BF16)

    full = lambda shape: pl.BlockSpec(shape, lambda i: (0,) * len(shape))
    rev = lambda col: (lambda i: (nb - 1 - i, col))
    return _pcall(
        body, name=name, grid=(nb,),
        in_specs=[pl.BlockSpec((bk, RET_QK), rev(0)), pl.BlockSpec((bk, RET_QK), rev(1)),
                  pl.BlockSpec((bk, RET_V), rev(1)), pl.BlockSpec((bk, RET_V), rev(2)),
                  pl.BlockSpec((bk, RET_QK), rev(0)), pl.BlockSpec((bk, RET_QK), rev(0)),
                  full((RET_HEADS, bk, bk)), full((RET_HEADS, bk, 1)), full((RET_HEADS, bk, 1)),
                  full((RET_HEADS, 1, RET_QK)), full((1, RET_V)),
                  pl.BlockSpec((bk, RET_V), rev(0)), pl.BlockSpec((bk, RET_V), rev(0)),
                  pl.BlockSpec((1, RET_HEADS, RET_QK, RET_DV), lambda i: (nb - 1 - i, 0, 0, 0))],
        out_specs=[pl.BlockSpec((bk, g0 + RET_V), rev(0)), pl.BlockSpec((1, RET_V), lambda i: (0, 0))],
        out_shape=[jax.ShapeDtypeStruct((t, IN_PAD), BF16), jax.ShapeDtypeStruct((1, RET_V), F32)],
        scratch_shapes=[pltpu.VMEM((RET_HEADS, RET_QK, RET_DV), F32)],
        compiler_params=_params("arbitrary"),
    )(proj, proj, proj, proj, cos, sin, w, wq, wk, mask, gain, dmixed, opre, states)


def _forget_cumsum(proj, bias, name):
    t = proj.shape[0]
    rt = TOK_TILE
    nb = t // rt
    tril = jnp.asarray(np.tril(np.ones((rt, rt))), F32)

    def body(z_ref, b_ref, tril_ref, c_ref, carry_ref):
        i = pl.program_id(0)

        @pl.when(i == 0)
        def _():
            carry_ref[...] = jnp.zeros_like(carry_ref)

        z = z_ref[...] + b_ref[...]
        logf = jnp.minimum(z, 0.0) - jnp.log(1.0 + jnp.exp(-jnp.abs(z)))
        c = lax.dot_general(tril_ref[...], logf, NN, precision=lax.Precision.HIGHEST,
                            preferred_element_type=F32) + carry_ref[...]
        c_ref[...] = c
        carry_ref[...] = c[rt - 1:rt, :]

    return _pcall(
        body, name=name, grid=(nb,),
        in_specs=[pl.BlockSpec((rt, LANE), lambda i: (i, FF_COL_BLOCK)), pl.BlockSpec((1, LANE), lambda i: (0, 0)),
                  pl.BlockSpec((rt, rt), lambda i: (0, 0))],
        out_specs=pl.BlockSpec((rt, LANE), lambda i: (i, 0)),
        out_shape=jax.ShapeDtypeStruct((t, LANE), F32),
        scratch_shapes=[pltpu.VMEM((1, LANE), F32)],
        compiler_params=_params("arbitrary"),
    )(proj, bias, tril)


def _forget_cumsum_bwd(proj, bias, drs, dcs, dproj, name):
    t = proj.shape[0]
    rt = TOK_TILE
    nb = t // rt
    triu = jnp.asarray(np.triu(np.ones((rt, rt))), F32)

    def body(z_ref, b_ref, triu_ref, drs_ref, dcs_ref, dproj_in, dz_ref, gb_ref, carry_ref):
        step = pl.program_id(0)

        @pl.when(step == 0)
        def _():
            carry_ref[...] = jnp.zeros_like(carry_ref)
            gb_ref[...] = jnp.zeros_like(gb_ref)

        dlogf = lax.dot_general(triu_ref[...], drs_ref[...] - dcs_ref[...], NN, precision=lax.Precision.HIGHEST,
                                preferred_element_type=F32) + carry_ref[...]
        carry_ref[...] = dlogf[0:1, :]
        z = z_ref[...] + b_ref[...]
        is_head = lax.broadcasted_iota(jnp.int32, (rt, LANE), 1) < FOX_HEADS
        dz = jnp.where(is_head, dlogf / (1.0 + jnp.exp(z)), 0.0)
        dz_ref[...] = dz.astype(BF16)
        gb_ref[...] += jnp.sum(dz, axis=0, keepdims=True)

    return _pcall(
        body, name=name, grid=(nb,),
        in_specs=[pl.BlockSpec((rt, LANE), lambda i: (nb - 1 - i, FF_COL_BLOCK)),
                  pl.BlockSpec((1, LANE), lambda i: (0, 0)),
                  pl.BlockSpec((rt, rt), lambda i: (0, 0)),
                  pl.BlockSpec((rt, LANE), lambda i: (nb - 1 - i, 0)),
                  pl.BlockSpec((rt, LANE), lambda i: (nb - 1 - i, 0)),
                  pl.BlockSpec(memory_space=pl.ANY)],
        out_specs=[pl.BlockSpec((rt, LANE), lambda i: (nb - 1 - i, FF_COL_BLOCK)),
                   pl.BlockSpec((1, LANE), lambda i: (0, 0))],
        out_shape=[jax.ShapeDtypeStruct(dproj.shape, BF16), jax.ShapeDtypeStruct((1, LANE), F32)],
        input_output_aliases={5: 0},
        scratch_shapes=[pltpu.VMEM((1, LANE), F32)],
        compiler_params=_params("arbitrary"),
    )(proj, bias, triu, drs, dcs, dproj)


FOX_PAIRS = FOX_HEADS // 2
L_ONE_Q = FOX_DH
L_ONE_K = FOX_DH + 3
L_LSE = FOX_DH + 4


def _split3(x):
    hi = x.astype(BF16).astype(F32)
    r = x - hi
    mid = r.astype(BF16).astype(F32)
    return hi, mid, r - mid


def _head_to_low(slab, e):
    return slab if e == 0 else pltpu.roll(slab, FOX_DH, axis=1)


def _pair(a, b, low):
    return jnp.where(low, a, pltpu.roll(b, FOX_DH, axis=1))


def _fox_prep(proj, c, name):
    t = proj.shape[0]
    tq = TOK_TILE

    def body(p_ref, c_ref, qa_ref, ka_ref, va_ref):
        i = pl.program_id(0)
        lane = lax.broadcasted_iota(jnp.int32, (tq, LANE), 1)
        low = lane < FOX_DH
        live = (i * tq + lax.broadcasted_iota(jnp.int32, (tq, 1), 0)) >= N_PAD
        q_tail = jnp.where(lane < L_ONE_Q + 3, 1.0, 0.0)
        k_ones = (lane >= L_ONE_K) & (lane < L_ONE_K + 4)
        v_tail = jnp.where(lane < FOX_DH + 2, 1.0, 0.0)
        for pair in range(FOX_PAIRS):
            base = 3 * LANE * pair
            for e in range(2):
                h = 2 * pair + e
                q = _head_to_low(p_ref[:, base:base + LANE], e)
                k = _head_to_low(p_ref[:, base + LANE:base + 2 * LANE], e)
                v = _head_to_low(p_ref[:, base + 2 * LANE:base + 3 * LANE], e)
                hi, mid, lo = _split3(jnp.where(live, -c_ref[:, h:h + 1], NEG))
                ka = jnp.where(low, k, jnp.where(k_ones, 1.0, 0.0))
                ka = jnp.where(lane == L_ONE_Q, hi, jnp.where(lane == L_ONE_Q + 1, mid, jnp.where(lane == L_ONE_Q + 2, lo, ka)))
                qa_ref[h] = jnp.where(low, q * QK_SCALE, q_tail).astype(BF16)
                ka_ref[h] = ka.astype(BF16)
                va_ref[h] = jnp.where(low, v, v_tail).astype(BF16)

    out = jax.ShapeDtypeStruct((FOX_HEADS, t, LANE), BF16)
    ospec = pl.BlockSpec((FOX_HEADS, tq, LANE), lambda i: (0, i, 0))
    return _pcall(
        body, name=name, grid=(t // tq,),
        in_specs=[pl.BlockSpec((tq, 3 * FOX_W), lambda i: (i, 1)), pl.BlockSpec((tq, LANE), lambda i: (i, 0))],
        out_specs=[ospec, ospec, ospec], out_shape=[out, out, out],
        compiler_params=_params("parallel"),
    )(proj, c)


STEP_PAIRS = 2
STEP_HEADS = 2 * STEP_PAIRS
FOX_GROUPS = FOX_PAIRS // STEP_PAIRS
FWD_PAIRS = 4
FWD_HEADS = 2 * FWD_PAIRS
FWD_GROUPS = FOX_PAIRS // FWD_PAIRS


def _blockdiag(a, b):
    z = jnp.zeros_like(a)
    return jnp.concatenate([jnp.concatenate([a, z], axis=1), jnp.concatenate([z, b], axis=1)], axis=0)


def _fox_fwd(qa, ka, va, mixed, name):
    nh, nq, tq, _ = qa.shape
    t = nq * tq

    def body(qa_ref, ka_ref, va_ref, mixed_in, mixed_ref, o_ref, lse_ref):
        i = pl.program_id(1)
        lane = lax.broadcasted_iota(jnp.int32, (tq, LANE), 1)
        causal = lax.broadcasted_iota(jnp.int32, (tq, tq), 1) <= lax.broadcasted_iota(jnp.int32, (tq, tq), 0)
        qps = [jnp.concatenate([qa_ref[2 * c], qa_ref[2 * c + 1]], axis=1) for c in range(FWD_PAIRS)]

        def logits(j):
            return [_dot(qps[c], _blockdiag(ka_ref[2 * c, j], ka_ref[2 * c + 1, j]), NT) for c in range(FWD_PAIRS)]

        def update(j, scores, carry, diagonal):
            new = []
            for c in range(FWD_PAIRS):
                ms, acc = carry[c]
                ps, ms_new, alphas = [], [], []
                for e in range(2):
                    s = scores[c][:, e * tq:(e + 1) * tq]
                    if diagonal:
                        s = jnp.where(causal, s, NEG)
                    m_new = jnp.maximum(ms[e], jnp.max(s, axis=-1, keepdims=True))
                    ps.append(jnp.exp(s - m_new).astype(BF16))
                    ms_new.append(m_new)
                    alphas.append(jnp.broadcast_to(jnp.exp(ms[e] - m_new), (tq, LANE)))
                pv = _dot(jnp.concatenate(ps, axis=1), _blockdiag(va_ref[2 * c, j], va_ref[2 * c + 1, j]))
                new.append((tuple(ms_new), jnp.concatenate(alphas, axis=1) * acc + pv))
            return tuple(new)

        m0 = jnp.full((tq, 1), NEG, F32)
        init = tuple(((m0, m0), jnp.zeros((tq, 2 * LANE), F32)) for _ in range(FWD_PAIRS))
        carry = lax.fori_loop(0, i, lambda j, cr: update(j, logits(j), cr, False), init)
        o_pairs = []
        lse = jnp.zeros((tq, LANE), F32)
        for c, (ms, acc) in enumerate(update(i, logits(i), carry, True)):
            outs = []
            for e in range(2):
                half = acc[:, e * LANE:(e + 1) * LANE]
                l = half[:, FOX_DH:FOX_DH + 1]
                outs.append(half / l)
                lse = jnp.where(lane == 2 * c + e, ms[e] + jnp.log(l), lse)
            o_pairs.append(_pair(outs[0], outs[1], lane < FOX_DH))
        o_all = jnp.concatenate(o_pairs, axis=1)
        mixed_ref[...] = o_all.astype(BF16)
        o_ref[...] = o_all
        lse_ref[...] = lse

    width = FWD_PAIRS * LANE
    whole = pl.BlockSpec((FWD_HEADS, nq, tq, LANE), lambda g, i: (g, 0, 0, 0), pipeline_mode=pl.Buffered(1))
    return _pcall(
        body, name=name, grid=(FWD_GROUPS, nq),
        in_specs=[pl.BlockSpec((FWD_HEADS, None, tq, LANE), lambda g, i: (g, i, 0, 0)), whole, whole,
                  pl.BlockSpec(memory_space=pl.ANY)],
        out_specs=[pl.BlockSpec((tq, width), lambda g, i: (i, RET_V // width + g)),
                   pl.BlockSpec((tq, width), lambda g, i: (i, g)),
                   pl.BlockSpec((None, tq, LANE), lambda g, i: (g, i, 0))],
        out_shape=[jax.ShapeDtypeStruct(mixed.shape, BF16), jax.ShapeDtypeStruct((t, FOX_W), F32),
                   jax.ShapeDtypeStruct((FWD_GROUPS, t, LANE), F32)],
        input_output_aliases={3: 0},
        compiler_params=_params("parallel", "parallel"),
    )(qa, ka, va, mixed)


def _fox_prep_bwd(dmixed, o_fox, lse, qa, name):
    t = dmixed.shape[0]
    tq = TOK_TILE

    def body(dm_ref, o_ref, lse_ref, qa_ref, qab_ref, doa_ref):
        i = pl.program_id(0)
        lane = lax.broadcasted_iota(jnp.int32, (tq, LANE), 1)
        low = lane < FOX_DH
        live = (i * tq + lax.broadcasted_iota(jnp.int32, (tq, 1), 0)) >= N_PAD
        for pair in range(FOX_PAIRS):
            cols = slice(LANE * pair, LANE * (pair + 1))
            d_slab = dm_ref[:, cols]
            prod = d_slab * o_ref[:, cols]
            for e in range(2):
                h = 2 * pair + e
                nd = -jnp.sum(jnp.where(low, _head_to_low(prod, e), 0.0), axis=-1, keepdims=True)
                nd_hi = nd.astype(BF16).astype(F32)
                doa = jnp.where(low, _head_to_low(d_slab, e), 0.0)
                doa = jnp.where(lane == FOX_DH, nd_hi, jnp.where(lane == FOX_DH + 1, nd - nd_hi, doa))
                doa_ref[h] = doa.astype(BF16)
                lse_h = lse_ref[h // FWD_HEADS][:, h % FWD_HEADS:h % FWD_HEADS + 1]
                hi, mid, lo = _split3(jnp.where(live, -lse_h, 0.0))
                qab = qa_ref[h].astype(F32)
                qab = jnp.where(lane == L_LSE, hi, jnp.where(lane == L_LSE + 1, mid, jnp.where(lane == L_LSE + 2, lo, qab)))
                qab_ref[h] = qab.astype(BF16)

    out = jax.ShapeDtypeStruct((FOX_HEADS, t, LANE), BF16)
    hspec = pl.BlockSpec((FOX_HEADS, tq, LANE), lambda i: (0, i, 0))
    return _pcall(
        body, name=name, grid=(t // tq,),
        in_specs=[pl.BlockSpec((tq, FOX_W), lambda i: (i, 1)), pl.BlockSpec((tq, FOX_W), lambda i: (i, 0)),
                  pl.BlockSpec((FWD_GROUPS, tq, LANE), lambda i: (0, i, 0)), hspec],
        out_specs=[hspec, hspec], out_shape=[out, out],
        compiler_params=_params("parallel"),
    )(dmixed, o_fox, lse, qa)


def _fox_bwd(qab, doa, ka, va, dproj, name):
    nh, nq, tq, _ = qab.shape
    t = nq * tq
    slab = 3 * LANE * STEP_PAIRS
    group0 = (2 * RET_QK + 2 * RET_V) // slab

    def body(qab_ref, doa_ref, ka_ref, va_ref, dproj_in, dp_ref, drs_ref, dcs_ref, dq_ref):
        g, j = pl.program_id(0), pl.program_id(1)

        @pl.when((g == 0) & (j == 0))
        def _():
            drs_ref[...] = jnp.zeros_like(drs_ref)
            dcs_ref[...] = jnp.zeros_like(dcs_ref)

        @pl.when(j == 0)
        def _():
            dq_ref[...] = jnp.zeros_like(dq_ref)

        lane = lax.broadcasted_iota(jnp.int32, (tq, LANE), 1)
        low = lane < FOX_DH
        key_le_query = lax.broadcasted_iota(jnp.int32, (tq, tq), 0) <= lax.broadcasted_iota(jnp.int32, (tq, tq), 1)

        def by_head(c, a, b, col):
            h = STEP_HEADS * g + 2 * c
            return jnp.where(lane == h, a[:, col:col + 1], jnp.where(lane == h + 1, b[:, col:col + 1], 0.0))

        kbs = [ka_ref[h] for h in range(STEP_HEADS)]
        vbs = [va_ref[h] for h in range(STEP_HEADS)]

        def step(i, carry, diagonal):
            qbs = [qab_ref[h, i] for h in range(STEP_HEADS)]
            dobs = [doa_ref[h, i] for h in range(STEP_HEADS)]
            st = [_dot(kbs[h], qbs[h], NT) for h in range(STEP_HEADS)]
            dpt = [_dot(vbs[h], dobs[h], NT) for h in range(STEP_HEADS)]
            new = []
            for h in range(STEP_HEADS):
                p = jnp.exp(st[h])
                if diagonal:
                    p = jnp.where(key_le_query, p, 0.0)
                ds = (p * dpt[h]).astype(BF16)
                dq_ref[h, i] += _dot(ds, kbs[h], TN)
                dk, dv = carry[h]
                new.append((dk + _dot(ds, qbs[h]), dv + _dot(p.astype(BF16), dobs[h])))
            return tuple(new)

        zero = jnp.zeros((tq, LANE), F32)
        carry = step(j, tuple((zero, zero) for _ in range(STEP_HEADS)), True)
        carry = lax.fori_loop(j + 1, nq, lambda i, cr: step(i, cr, False), carry)
        rows = pl.ds(pl.multiple_of(j * tq, tq), tq)
        for c in range(STEP_PAIRS):
            (dka, dva), (dkb, dvb) = carry[2 * c], carry[2 * c + 1]
            c0 = 3 * LANE * c
            dp_ref[rows, c0 + LANE:c0 + 2 * LANE] = _pair(dka, dkb, low).astype(BF16)
            dp_ref[rows, c0 + 2 * LANE:c0 + 3 * LANE] = _pair(dva, dvb, low).astype(BF16)
            dcs_ref[rows, :] += by_head(c, dka, dkb, L_ONE_Q)

        @pl.when(j == nq - 1)
        def _():
            for c in range(STEP_PAIRS):
                for blk in range(nq):
                    r = slice(blk * tq, (blk + 1) * tq)
                    a, b = dq_ref[2 * c, blk], dq_ref[2 * c + 1, blk]
                    dp_ref[r, 3 * LANE * c:3 * LANE * c + LANE] = (_pair(a, b, low) * QK_SCALE).astype(BF16)
                    drs_ref[r, :] += by_head(c, a, b, L_ONE_K)

    whole = pl.BlockSpec((STEP_HEADS, nq, tq, LANE), lambda g, j: (g, 0, 0, 0), pipeline_mode=pl.Buffered(1))
    blk = pl.BlockSpec((STEP_HEADS, None, tq, LANE), lambda g, j: (g, j, 0, 0))
    sums = pl.BlockSpec((t, LANE), lambda g, j: (0, 0), pipeline_mode=pl.Buffered(1))
    return _pcall(
        body, name=name, grid=(FOX_GROUPS, nq),
        in_specs=[whole, whole, blk, blk, pl.BlockSpec(memory_space=pl.ANY)],
        out_specs=[pl.BlockSpec((t, slab), lambda g, j: (0, group0 + g)), sums, sums],
        out_shape=[jax.ShapeDtypeStruct(dproj.shape, BF16), jax.ShapeDtypeStruct((t, LANE), F32),
                   jax.ShapeDtypeStruct((t, LANE), F32)],
        input_output_aliases={4: 0},
        scratch_shapes=[pltpu.VMEM((STEP_HEADS, nq, tq, LANE), F32)],
        compiler_params=_params("arbitrary", "arbitrary"),
    )(qab, doa, ka, va, dproj)


HALO = 8


def _rows_ext(ref, r0, rows, t, before, after):
    lo, hi = r0 - before, r0 + rows + after
    width = ref.shape[-1]
    parts = []
    if lo < 0:
        parts.append(jnp.zeros((-lo, width), F32))
    parts.append(ref[max(lo, 0):min(hi, t), :].astype(F32))
    if hi > t:
        parts.append(jnp.zeros((hi - t, width), F32))
    return parts[0] if len(parts) == 1 else jnp.concatenate(parts, axis=0)


def _conv_taps(a_ext, r0_ext, cw_ref, cb_ref):
    n = a_ext.shape[0]
    if r0_ext < N_PAD:
        row = r0_ext + lax.broadcasted_iota(jnp.int32, (n, 1), 0)
        a_ext = jnp.where(row >= N_PAD, a_ext, 0.0)
    a1 = pltpu.roll(a_ext, 1, axis=0)
    a2 = pltpu.roll(a_ext, 2, axis=0)
    acc = cb_ref[...] + a2 * cw_ref[0:1, :] + a1 * cw_ref[1:2, :] + a_ext * cw_ref[2:3, :]
    return a_ext, a1, a2, acc


FF_COLS = 256


def _up_conv_fwd(n2, w_up_t, conv_w8, conv_b, name):
    t, d = n2.shape
    f = w_up_t.shape[1]
    rows = TOK_TILE
    starts = list(range(0, t, rows))

    def body(n_ref, wa_ref, wb_ref, cw_ref, cb_ref, up_ref, g_ref):
        wa, wb = wa_ref[...], wb_ref[...]

        def project(r0):
            n_rows = n_ref[r0:r0 + rows, :]
            up_ref[0, r0:r0 + rows, :] = _dot(n_rows, wa, NT)
            up_ref[1, r0:r0 + rows, :] = _dot(n_rows, wb, NT)

        def activate(r0):
            a_ext = _rows_ext(up_ref.at[0], r0, rows, t, HALO, 0)
            _, _, _, acc = _conv_taps(a_ext, r0 - HALO, cw_ref, cb_ref)
            acc = acc[HALO:, :]
            g_ref[r0:r0 + rows, :] = (acc * _sigmoid(acc) * up_ref[1, r0:r0 + rows, :]).astype(BF16)

        project(starts[0])
        for r0, r_next in zip(starts, starts[1:] + [None]):
            if r_next is not None:
                project(r_next)
            activate(r0)

    return _pcall(
        body, name=name, grid=(f // FF_COLS,),
        in_specs=[pl.BlockSpec((t, d), lambda j: (0, 0), pipeline_mode=pl.Buffered(1)),
                  pl.BlockSpec((None, FF_COLS, d), lambda j: (0, j, 0)), pl.BlockSpec((None, FF_COLS, d), lambda j: (1, j, 0)),
                  pl.BlockSpec((8, FF_COLS), lambda j: (0, j)), pl.BlockSpec((1, FF_COLS), lambda j: (0, j))],
        out_specs=[pl.BlockSpec((2, t, FF_COLS), lambda j: (0, 0, j)), pl.BlockSpec((t, FF_COLS), lambda j: (0, j))],
        out_shape=[jax.ShapeDtypeStruct((2, t, f), F32), jax.ShapeDtypeStruct((t, f), BF16)],
        compiler_params=_params("parallel"),
    )(n2, w_up_t, w_up_t, conv_w8, conv_b)


def _dg_conv_bwd(up, conv_w8, conv_b, dh2, w_down, name):
    _, t, f = up.shape
    d = dh2.shape[1]
    rows = TOK_TILE
    starts = list(range(0, t, rows))

    def body(a_ref, b_ref, cw_ref, cb_ref, dh_ref, wd_ref, dup_ref, gcw_ref, gcb_ref, dg_ref):
        wd = wd_ref[...]

        def project(r0):
            dg_ref[r0:r0 + rows, :] = _dot(dh_ref[r0:r0 + rows, :], wd, NT)

        gw = [jnp.zeros((1, FF_COLS), F32) for _ in range(3)]
        gb = jnp.zeros((1, FF_COLS), F32)
        project(starts[0])
        for r0, r_next in zip(starts, starts[1:] + [None]):
            if r_next is not None:
                project(r_next)
            a_ext = _rows_ext(a_ref, r0, rows, t, HALO, HALO)
            b_ext = _rows_ext(b_ref, r0, rows, t, HALO, HALO)
            dg_ext = _rows_ext(dg_ref, r0, rows, t, HALO, HALO)
            a0, a1, a2, acc = _conv_taps(a_ext, r0 - HALO, cw_ref, cb_ref)
            sg = _sigmoid(acc)
            dacc = dg_ext * b_ext * (sg * (1.0 + acc * (1.0 - sg)))
            n = dacc.shape[0]
            da = (dacc * cw_ref[2:3, :] + pltpu.roll(dacc, n - 1, axis=0) * cw_ref[1:2, :]
                  + pltpu.roll(dacc, n - 2, axis=0) * cw_ref[0:1, :])
            core = slice(HALO, HALO + rows)
            da = da[core, :]
            if r0 < N_PAD:
                row = r0 + lax.broadcasted_iota(jnp.int32, (rows, 1), 0)
                da = jnp.where(row >= N_PAD, da, 0.0)
            dup_ref[0, r0:r0 + rows, :] = da.astype(BF16)
            dup_ref[1, r0:r0 + rows, :] = (dg_ext * acc * sg)[core, :].astype(BF16)
            dacc_c = dacc[core, :]
            gw[0] = gw[0] + jnp.sum(dacc_c * a2[core, :], axis=0, keepdims=True)
            gw[1] = gw[1] + jnp.sum(dacc_c * a1[core, :], axis=0, keepdims=True)
            gw[2] = gw[2] + jnp.sum(dacc_c * a0[core, :], axis=0, keepdims=True)
            gb = gb + jnp.sum(dacc_c, axis=0, keepdims=True)
        gcw_ref[...] = jnp.zeros((8, FF_COLS), F32)
        for tap in range(3):
            gcw_ref[tap:tap + 1, :] = gw[tap]
        gcb_ref[...] = gb

    return _pcall(
        body, name=name, grid=(f // FF_COLS,),
        in_specs=[pl.BlockSpec((None, t, FF_COLS), lambda j: (0, 0, j)), pl.BlockSpec((None, t, FF_COLS), lambda j: (1, 0, j)),
                  pl.BlockSpec((8, FF_COLS), lambda j: (0, j)), pl.BlockSpec((1, FF_COLS), lambda j: (0, j)),
                  pl.BlockSpec((t, d), lambda j: (0, 0), pipeline_mode=pl.Buffered(1)),
                  pl.BlockSpec((FF_COLS, d), lambda j: (j, 0))],
        out_specs=[pl.BlockSpec((2, t, FF_COLS), lambda j: (0, 0, j)), pl.BlockSpec((8, FF_COLS), lambda j: (0, j)),
                   pl.BlockSpec((1, FF_COLS), lambda j: (0, j))],
        out_shape=[jax.ShapeDtypeStruct((2, t, f), BF16), jax.ShapeDtypeStruct((8, f), F32),
                   jax.ShapeDtypeStruct((1, f), F32)],
        scratch_shapes=[pltpu.VMEM((t, FF_COLS), F32)],
        compiler_params=_params("parallel"),
    )(up, up, conv_w8, conv_b, dh2, w_down)


def _exchange(arrays, kinds, name, after=None):
    n = len(arrays)
    npeer = N_DEV - 1
    n_in = n + int(after is not None)

    def body(*refs):
        ins, outs = refs[:n], refs[n_in:n_in + n]
        send_sems, recv_sems, local_sems = refs[n_in + n:]
        x, y, c = lax.axis_index("x"), lax.axis_index("y"), lax.axis_index("c")
        me = 4 * x + 2 * y + c
        copies, locals_ = [], []
        for a in range(n):
            gather = kinds[a] == "gather"
            own = pltpu.make_async_copy(ins[a] if gather else ins[a].at[me], outs[a].at[me], local_sems.at[a])
            own.start()
            locals_.append(own)
            for d in range(1, N_DEV):
                px = 1 - x if d & 4 else x
                py = 1 - y if d & 2 else y
                pc = 1 - c if d & 1 else c
                src = ins[a] if gather else ins[a].at[4 * px + 2 * py + pc]
                cp = pltpu.make_async_remote_copy(
                    src_ref=src, dst_ref=outs[a].at[me],
                    send_sem=send_sems.at[a * npeer + d - 1], recv_sem=recv_sems.at[a * npeer + d - 1],
                    device_id=(px, py, pc), device_id_type=pl.DeviceIdType.MESH)
                cp.start()
                copies.append(cp)
        for cp in copies:
            cp.wait_recv()
        for cp in copies:
            cp.wait_send()
        for own in locals_:
            own.wait()

    out_shape = [jax.ShapeDtypeStruct((N_DEV,) + (a.shape if k == "gather" else a.shape[1:]), a.dtype)
                 for a, k in zip(arrays, kinds)]
    return _pcall(
        body, name=name,
        in_specs=[pl.BlockSpec(memory_space=pl.ANY)] * n_in,
        out_specs=[pl.BlockSpec(memory_space=pl.ANY)] * n,
        out_shape=out_shape,
        scratch_shapes=[pltpu.SemaphoreType.DMA((n * npeer,)), pltpu.SemaphoreType.DMA((n * npeer,)),
                        pltpu.SemaphoreType.DMA((n,))],
        compiler_params=pltpu.CompilerParams(has_side_effects=True),
    )(*arrays, *([] if after is None else [after]))


def _peer_copies(srcs, lands, kinds, send_sems, recv_sems):
    x, y, c = lax.axis_index("x"), lax.axis_index("y"), lax.axis_index("c")
    me = 4 * x + 2 * y + c
    copies = []
    for a in range(len(srcs)):
        for d in range(1, N_DEV):
            px = 1 - x if d & 4 else x
            py = 1 - y if d & 2 else y
            pc = 1 - c if d & 1 else c
            k = a * (N_DEV - 1) + d - 1
            copies.append(pltpu.make_async_remote_copy(
                src_ref=srcs[a] if kinds[a] == "gather" else srcs[a].at[4 * px + 2 * py + pc], dst_ref=lands[a].at[me],
                send_sem=send_sems.at[k], recv_sem=recv_sems.at[k],
                device_id=(px, py, pc), device_id_type=pl.DeviceIdType.MESH))
    return copies


def _exchange_start(arrays, kinds, name, after=None):
    n = len(arrays)
    nsem = n * (N_DEV - 1)
    hbm = pl.BlockSpec(memory_space=pltpu.HBM)
    sem = pl.BlockSpec(memory_space=pltpu.SEMAPHORE)
    land_shapes = [(N_DEV,) + (a.shape if k == "gather" else a.shape[1:]) for a, k in zip(arrays, kinds)]

    n_in = 2 * n + int(after is not None)

    def body(*refs):
        srcs, lands = refs[:n], refs[n:2 * n]
        send_sems, recv_sems = refs[n_in], refs[n_in + 1]
        token = refs[-1]
        for cp in _peer_copies(srcs, lands, kinds, send_sems, recv_sems):
            cp.start()
        token[...] = jnp.zeros_like(token)

    operands = [pltpu.with_memory_space_constraint(a, pltpu.HBM) for a in arrays]
    operands += [pltpu.with_memory_space_constraint(lax.empty(s, a.dtype), pltpu.HBM) for s, a in zip(land_shapes, arrays)]
    operands += [] if after is None else [after]
    out = _pcall(
        body, name=name,
        in_specs=[hbm] * (2 * n) + ([] if after is None else [pl.BlockSpec(memory_space=pl.ANY)]),
        out_specs=[sem, sem] + [hbm] * (2 * n) + [pl.BlockSpec(memory_space=pltpu.VMEM)],
        out_shape=[pltpu.SemaphoreType.DMA((nsem,)), pltpu.SemaphoreType.DMA((nsem,))]
        + [pltpu.HBM(a.shape, a.dtype) for a in arrays]
        + [pltpu.HBM(s, a.dtype) for s, a in zip(land_shapes, arrays)]
        + [jax.ShapeDtypeStruct((8, LANE), F32)],
        input_output_aliases={k: 2 + k for k in range(2 * n)},
        compiler_params=pltpu.CompilerParams(has_side_effects=pltpu.SideEffectType.DATAFLOW_SIDE_EFFECTING),
    )(*operands)
    return out[0], out[1], list(out[2:2 + n]), list(out[2 + n:2 + 2 * n]), out[-1]


def _exchange_wait(started, kinds, after, name):
    send_sems, recv_sems, srcs, lands, _ = started
    n = len(srcs)
    hbm = pl.BlockSpec(memory_space=pltpu.HBM)
    sem = pl.BlockSpec(memory_space=pltpu.SEMAPHORE)

    def body(*refs):
        src_refs, land_refs = refs[:n], refs[n:2 * n]
        copies = _peer_copies(src_refs, land_refs, kinds, refs[2 * n], refs[2 * n + 1])
        for cp in copies:
            cp.wait_send()
        for cp in copies:
            cp.wait_recv()

    out = _pcall(
        body, name=name,
        in_specs=[hbm] * (2 * n) + [sem, sem, pl.BlockSpec(memory_space=pl.ANY)],
        out_specs=[hbm] * (2 * n),
        out_shape=[pltpu.HBM(a.shape, a.dtype) for a in srcs + lands],
        input_output_aliases={k: k for k in range(2 * n)},
        compiler_params=pltpu.CompilerParams(has_side_effects=pltpu.SideEffectType.DATAFLOW_SIDE_EFFECTING),
    )(*srcs, *lands, send_sems, recv_sems, after)
    me = 4 * lax.axis_index("x") + 2 * lax.axis_index("y") + lax.axis_index("c")
    filled = []
    for src, land, kind in zip(out[:n], out[n:], kinds):
        own = src if kind == "gather" else lax.dynamic_index_in_dim(src, me, axis=0, keepdims=False)
        filled.append(lax.dynamic_update_slice(land, own[None], (me,) + (0,) * own.ndim))
    return filled


def _sum_slots(slots, name, rows_tile):
    nd, r, c = slots.shape

    def body(s_ref, o_ref):
        acc = s_ref[0].astype(F32)
        for p in range(1, nd):
            acc = acc + s_ref[p].astype(F32)
        o_ref[...] = acc

    return _pcall(
        body, name=name, grid=(r // rows_tile,),
        in_specs=[pl.BlockSpec((nd, rows_tile, c), lambda i: (0, i, 0))],
        out_specs=pl.BlockSpec((rows_tile, c), lambda i: (i, 0)),
        out_shape=jax.ShapeDtypeStruct((r, c), F32),
        compiler_params=_params("parallel"),
    )(slots)


def _sum_slots_small(slot_arrays, name):
    n = len(slot_arrays)

    def body(*refs):
        for s_ref, o_ref in zip(refs[:n], refs[n:]):
            acc = s_ref[0]
            for p in range(1, s_ref.shape[0]):
                acc = acc + s_ref[p]
            o_ref[...] = acc

    return _pcall(body, name=name, out_shape=[jax.ShapeDtypeStruct(a.shape[1:], F32) for a in slot_arrays])(*slot_arrays)


def _adamw_update(w_ref, g_ref, m_ref, v_ref, d_ref, nm_ref, nv_ref):
    gr = g_ref[...]
    nm = ADAM_B1 * m_ref[...] + (1.0 - ADAM_B1) * gr
    nv = ADAM_B2 * v_ref[...] + (1.0 - ADAM_B2) * (gr * gr)
    m_hat = nm / (1.0 - ADAM_B1 ** ADAM_STEP)
    v_hat = nv / (1.0 - ADAM_B2 ** ADAM_STEP)
    d_ref[...] = -ADAM_LR * (m_hat / (jnp.sqrt(v_hat) + ADAM_EPS) + ADAM_WD * w_ref[...])
    nm_ref[...] = nm
    nv_ref[...] = nv


def _adamw_small(ws, gs, ms, vs, name):
    n = len(ws)

    def body(*refs):
        ins, outs = refs[:4 * n], refs[4 * n:]
        for k in range(n):
            _adamw_update(ins[k], ins[n + k], ins[2 * n + k], ins[3 * n + k], outs[k], outs[n + k], outs[2 * n + k])

    shapes = [jax.ShapeDtypeStruct(w.shape, F32) for w in ws]
    out = _pcall(body, name=name, out_shape=shapes * 3)(*ws, *gs, *ms, *vs)
    return list(out[:n]), list(out[n:2 * n]), list(out[2 * n:])


def _adamw(w, g, m, v, name, rows_tile):
    r, c = w.shape
    body = lambda *refs: _adamw_update(*refs)
    spec = pl.BlockSpec((rows_tile, c), lambda i: (i, 0))
    shp = jax.ShapeDtypeStruct((r, c), F32)
    return _pcall(
        body, name=name, grid=(r // rows_tile,), in_specs=[spec] * 4, out_specs=[spec] * 3, out_shape=[shp] * 3,
        compiler_params=_params("parallel"),
    )(w, g, m, v)


F0 = 2 * RET_QK + 2 * RET_V


def _to_internal_rows(w_t):
    cols = w_t.shape[1]
    fox = w_t[F0:F0 + 3 * FOX_W].reshape(3, FOX_PAIRS, LANE, cols).transpose(1, 0, 2, 3).reshape(3 * FOX_W, cols)
    tail = jnp.zeros((IN_PAD - IN_WIDTH, cols), w_t.dtype)
    return jnp.concatenate([w_t[:F0], fox, w_t[F0 + 3 * FOX_W:], tail], axis=0)


def _from_internal_rows(g_t):
    cols = g_t.shape[1]
    fox = g_t[F0:F0 + 3 * FOX_W].reshape(FOX_PAIRS, 3, LANE, cols).transpose(1, 0, 2, 3).reshape(3 * FOX_W, cols)
    return jnp.concatenate([g_t[:F0], fox, g_t[F0 + 3 * FOX_W:F0 + 3 * FOX_W + FOX_HEADS]], axis=0)


def _local_step(x, target, meta, attn_g, fox_b, ret_g, ffn_g, conv_w8, conv_b, final_g,
                first_weight, late_weights, ffn_grads_ready, out_grad_ready, in_grad_ready):
    seq, d = x.shape
    t = seq + PREFIX
    tm = TOK_TILE
    nq = t // tm
    fox_b128 = jnp.pad(fox_b, ((0, 0), (0, LANE - FOX_HEADS)))

    h0, n1 = _prep_norm(x, meta, attn_g, "prep_norm")
    w_in_t = first_weight(n1)
    proj = _mm_simple(n1, w_in_t, mode="nt", tm=tm, tn=IN_PAD, tk=d, out_dtype=F32, name="mm_in")
    cos, sin = _rope_tables(t)
    o_pre, mixed, states = _ret_fwd(proj, cos, sin, ret_g, "ret_fwd")
    c = _forget_cumsum(proj, fox_b128, "forget_cumsum")
    qa, ka, va = _fox_prep(proj, c, "fox_prep")
    by_block = lambda a: a.reshape(FOX_HEADS, nq, tm, LANE)
    mixed, o_fox, lse = _fox_fwd(by_block(qa), by_block(ka), by_block(va), mixed, "fox_fwd")
    w_out, w_up_t, w_down = late_weights(o_fox)
    tile = pl.BlockSpec((tm, d), lambda i: (i, 0))
    row_vec = pl.BlockSpec((1, d), lambda i: (0, 0))
    resident = lambda shape: pl.BlockSpec(shape, lambda i: (0,) * len(shape), pipeline_mode=pl.Buffered(1))
    acts = lambda dtype: jax.ShapeDtypeStruct((t, d), dtype)
    vec = jax.ShapeDtypeStruct((1, d), F32)

    def residual_and_norm(i, acc, ins, outs):
        h = acc + ins[0][...]
        outs[0][...] = h
        outs[1][...] = (h * lax.rsqrt(jnp.mean(h * h, axis=-1, keepdims=True) + EPS) * ins[1][...]).astype(BF16)

    h1, n2 = _matmul_rows([mixed], [tile], [w_out], [resident((d, d))], [h0, ffn_g], [tile, row_vec],
                          [tile, tile], [acts(F32), acts(BF16)], residual_and_norm, mode="nn", steps=nq, name="mm_out_norm")
    nf = D_FF // 1408
    up, g = _up_conv_fwd(n2, w_up_t, conv_w8, conv_b, "up_conv_fwd")

    def residual_loss_bwd(i, acc, ins, outs):
        loss_ref, dh_ref, dhb_ref, gg_ref = outs
        part, dh, gg = _loss_tile(i, acc + ins[0][...], jnp.concatenate([ins[1][...], ins[2][...], ins[3][...]], axis=0),
                                  ins[4][...])
        _accumulate(loss_ref, i, jnp.broadcast_to(part, loss_ref.shape))
        dh_ref[...] = dh
        dhb_ref[...] = dh.astype(BF16)
        _accumulate(gg_ref, i, gg)

    loss_tile, dh2, dh2_b, g_final = _matmul_rows(
        [g], [pl.BlockSpec((tm, D_FF), lambda i: (i, 0))], [w_down], [resident((D_FF, d))],
        [h1, target, target, target, final_g], [tile] + _shifted_row_specs(d) + [row_vec],
        [pl.BlockSpec((8, LANE), lambda i: (0, 0)), tile, tile, row_vec],
        [jax.ShapeDtypeStruct((8, LANE), F32), acts(F32), acts(BF16), vec], residual_loss_bwd,
        mode="nn", steps=nq, name="mm_down_loss")

    tkw = 1408 if t % 1408 == 0 else tm
    gw_down = _mm_simple(g, dh2_b, mode="tn", tm=1408, tn=d, tk=tkw, out_dtype=BF16, name="mm_gw_down")
    dup, g_conv_w8, g_conv_b = _dg_conv_bwd(up, conv_w8, conv_b, dh2_b, w_down, "dg_conv_bwd")

    def norm_bwd(i, acc, ins, outs):
        dh, gg = _rms_bwd_tile(acc, ins[0][...], ins[1][...], ins[2][...])
        outs[0][...] = dh
        _accumulate(outs[1], i, gg)

    half = lambda p: pl.BlockSpec((None, tm, D_FF), lambda i: (p, i, 0))
    half_w = lambda p: pl.BlockSpec((None, D_FF, d), lambda i: (p, 0, 0), pipeline_mode=pl.Buffered(1))
    gw_up_t = _matmul(
        dup, n2, mode="tn", grid=(2 * nf, 1, t // tkw),
        a_spec=pl.BlockSpec((None, tkw, 1408), lambda i, j, k: (i // nf, k, i % nf)),
        b_spec=pl.BlockSpec((tkw, d), lambda i, j, k: (k, 0)),
        o_spec=pl.BlockSpec((1408, d), lambda i, j, k: (i, 0)),
        out_shape=jax.ShapeDtypeStruct((2 * D_FF, d), BF16), name="mm_gw_up")
    def norm_bwd_and_mixer_grad(i, acc, ins, outs):
        dh, gg = _rms_bwd_tile(acc, ins[0][...], ins[1][...], ins[2][...])
        outs[0][...] = dh
        _accumulate(outs[1], i, gg)
        outs[2][...] = _dot(dh.astype(BF16), ins[3][...], NT)

    dh1, g_ffn, dmixed = _matmul_rows(
        [dup, dup], [half(0), half(1)], [w_up_t, w_up_t], [half_w(0), half_w(1)],
        [h1, ffn_g, dh2, w_out], [tile, row_vec, tile, resident((d, d))], [tile, row_vec, tile],
        [acts(F32), vec, acts(F32)], norm_bwd_and_mixer_grad,
        mode="nn", steps=nq, name="mm_dn2_norm_bwd", after=ffn_grads_ready(gw_down, gw_up_t))
    gw_out = _mm_simple(mixed, dh1, mode="tn", tm=d, tn=d, tk=tkw, out_dtype=BF16, name="mm_gw_out")
    dproj, g_ret = _ret_bwd(proj, cos, sin, ret_g + out_grad_ready(gw_out), dmixed, o_pre, states, "ret_bwd")
    qab, doa = _fox_prep_bwd(dmixed, o_fox, lse, qa, "fox_prep_bwd")
    dproj, drs, dcs = _fox_bwd(by_block(qab), by_block(doa), by_block(ka), by_block(va), dproj, "fox_bwd")
    dproj, g_fox_b = _forget_cumsum_bwd(proj, fox_b128, drs, dcs, dproj, "forget_cumsum_bwd")
    gw_in_t = _mm_simple(dproj, n1, mode="tn", tm=640, tn=d, tk=tkw, out_dtype=BF16, name="mm_gw_in")
    sent = in_grad_ready(gw_in_t)
    dh0, g_attn = _matmul_rows(
        [dproj], [pl.BlockSpec((tm, IN_PAD), lambda i: (i, 0))], [w_in_t], [resident((IN_PAD, d))],
        [h0, attn_g, dh1], [tile, row_vec, tile], [tile, row_vec], [acts(F32), vec], norm_bwd,
        mode="nn", steps=nq, name="mm_dn1_norm_bwd", after=sent)

    grads = dict(meta=dh0[N_PAD:PREFIX], attn_g=g_attn, fox_b=g_fox_b, ret_g=g_ret,
                 ffn_g=g_ffn, conv_w=g_conv_w8, conv_b=g_conv_b, final_g=g_final)
    return loss_tile, dh0[PREFIX:], grads


def kernel(x, meta_tokens, attn_norm_g, w_in, fox_forget_b, ret_norm_g, w_out, ffn_norm_g, w_up, conv_w, conv_b, w_down, final_norm_g, loss_target, m_meta_tokens, m_attn_norm_g, m_w_in, m_fox_forget_b, m_ret_norm_g, m_w_out, m_ffn_norm_g, m_w_up, m_conv_w, m_conv_b, m_w_down, m_final_norm_g, v_meta_tokens, v_attn_norm_g, v_w_in, v_fox_forget_b, v_ret_norm_g, v_w_out, v_ffn_norm_g, v_w_up, v_conv_w, v_conv_b, v_w_down, v_final_norm_g):
    d = D_MODEL
    me = 4 * lax.axis_index("x") + 2 * lax.axis_index("y") + lax.axis_index("c")
    in_blk = IN_WIDTH // N_DEV
    in_blk_pad = 400
    up_blk = 2 * D_FF // N_DEV
    down_blk = D_FF // N_DEV
    cw_blk = D_FF // N_DEV

    w_in_loc = jnp.pad(w_in[0].T.astype(BF16), ((0, in_blk_pad - in_blk), (0, 0)))
    cw_loc = jnp.pad(conv_w[0], ((0, 5), (0, 384 - cw_blk)))
    g_meta, g_cw = _exchange([meta_tokens, cw_loc], ["gather"] * 2, "gather_small")
    first = _exchange_start([w_in_loc], ["gather"], "gather_in_start", after=g_meta)
    rest_loc = [(w_out[0] + first[-1][0:1, 0:1]).astype(BF16), w_up[0].T.astype(BF16), w_down[0].astype(BF16)]
    rest = _exchange_start(rest_loc, ["gather"] * 3, "gather_rest_start")
    meta_f = g_meta.transpose(1, 0, 2).reshape(N_META, d)
    conv_w8 = jnp.pad(g_cw[:, :3, :cw_blk].transpose(1, 0, 2).reshape(3, D_FF), ((0, 5), (0, 0)))
    pending = {}

    def first_weight(after):
        (g_in,) = _exchange_wait(first, ["gather"], after, "gather_in_wait")
        return _to_internal_rows(g_in[:, :in_blk].reshape(IN_WIDTH, d))

    def in_grad_ready(gw_in_t):
        blocks = _from_internal_rows(gw_in_t).reshape(N_DEV, in_blk, d)
        blocks = jnp.pad(blocks, ((0, 0), (0, in_blk_pad - in_blk), (0, 0)))
        pending["in"] = _exchange_start([blocks], ["scatter"], "grads_in_start")
        return pending["in"][-1][0:1, 0:1]

    def late_weights(after):
        g_out, g_up, g_down = _exchange_wait(rest, ["gather"] * 3, after, "gather_rest_wait")
        return g_out.reshape(d, d), g_up.reshape(2, D_FF, d), g_down.reshape(D_FF, d)

    def ffn_grads_ready(gw_down, gw_up_t):
        blocks = [gw_down.reshape(N_DEV, down_blk, d), gw_up_t.reshape(N_DEV, up_blk, d)]
        pending["ffn"] = _exchange_start(blocks, ["scatter"] * 2, "grads_ffn_start")
        return pending["ffn"][-1][0:1, 0:1]

    def out_grad_ready(gw_out):
        pending["out"] = _exchange_start([gw_out.reshape(N_DEV, d // N_DEV, d)], ["scatter"], "grads_out_start")
        return pending["out"][-1][0:1, 0:1]

    loss_tile, grad_x, gr = _local_step(
        x[0], loss_target[0], meta_f, attn_norm_g + rest[-1][0:1, 0:1], fox_forget_b, ret_norm_g, ffn_norm_g,
        conv_w8, conv_b, final_norm_g.reshape(1, d), first_weight, late_weights, ffn_grads_ready, out_grad_ready,
        in_grad_ready)

    r_down, r_up = _exchange_wait(pending["ffn"], ["scatter"] * 2, grad_x, "grads_ffn_wait")
    (r_out,) = _exchange_wait(pending["out"], ["scatter"], grad_x, "grads_out_wait")
    g_w_out = _sum_slots(r_out, "sum_w_out", d // N_DEV)
    g_w_up_t = _sum_slots(r_up, "sum_w_up", up_blk)
    g_w_down = _sum_slots(r_down, "sum_w_down", down_blk)
    as_t = lambda a: a[0].T
    from_t = lambda a: a.T[None]
    d_w_out, m_w_out_n, v_w_out_n = [a[None] for a in _adamw(w_out[0], g_w_out, m_w_out[0], v_w_out[0], "adamw_w_out", 128)]
    up_t = _adamw(as_t(w_up), g_w_up_t, as_t(m_w_up), as_t(v_w_up), "adamw_w_up", up_blk // 2)
    d_w_up, m_w_up_n, v_w_up_n = [from_t(a) for a in up_t]
    d_w_down, m_w_down_n, v_w_down_n = [a[None] for a in _adamw(w_down[0], g_w_down, m_w_down[0], v_w_down[0],
                                                                "adamw_w_down", down_blk)]

    small = [loss_tile, gr["attn_g"], gr["fox_b"], gr["ret_g"], gr["ffn_g"], gr["conv_b"], gr["final_g"],
             gr["meta"], gr["conv_w"]]
    r_small = _exchange(small, ["gather"] * len(small), "exchange_small", after=up_t[0])
    (loss_all, g_attn, g_fox_b128, g_ret, g_ffn, g_conv_b, g_final, g_meta_full, g_cw_full) = _sum_slots_small(
        r_small, "sum_small")
    loss = loss_all[0, 0]
    g_fox_b = g_fox_b128[:, :FOX_HEADS]
    g_meta_loc = lax.dynamic_slice(g_meta_full, (0, me * (d // N_DEV)), (N_META, d // N_DEV))
    g_cw_loc = lax.dynamic_slice(g_cw_full, (0, me * cw_blk), (3, cw_blk))

    (r_in,) = _exchange_wait(pending["in"], ["scatter"], r_small[0], "grads_in_wait")
    g_w_in_t = _sum_slots(r_in, "sum_w_in", in_blk_pad)[:in_blk]
    d_w_in, m_w_in_n, v_w_in_n = [from_t(a) for a in _adamw(as_t(w_in), g_w_in_t, as_t(m_w_in), as_t(v_w_in),
                                                            "adamw_w_in", in_blk)]
    g_w_in, g_w_up = g_w_in_t.T, g_w_up_t.T
    row = lambda a: a.reshape(1, d)
    sm_grads = [g_meta_loc, g_attn, g_fox_b, g_ret, g_ffn, g_cw_loc, g_conv_b, g_final]
    sm_w = [meta_tokens, attn_norm_g, fox_forget_b, ret_norm_g, ffn_norm_g, conv_w[0], conv_b, row(final_norm_g)]
    sm_m = [m_meta_tokens, m_attn_norm_g, m_fox_forget_b, m_ret_norm_g, m_ffn_norm_g, m_conv_w[0], m_conv_b,
            row(m_final_norm_g)]
    sm_v = [v_meta_tokens, v_attn_norm_g, v_fox_forget_b, v_ret_norm_g, v_ffn_norm_g, v_conv_w[0], v_conv_b,
            row(v_final_norm_g)]
    dl, ml, vl = [lst[:7] + [lst[7].reshape(d)] for lst in _adamw_small(sm_w, sm_grads, sm_m, sm_v, "adamw_small")]

    def by_weight(meta_, attn_, w_in_, fox_, ret_, w_out_, ffn_, w_up_, cw_, cb_, w_down_, final_):
        return (meta_, attn_, w_in_, fox_, ret_, w_out_, ffn_, w_up_, cw_[None], cb_, w_down_, final_)

    grads_out = by_weight(g_meta_loc, g_attn, g_w_in[None], g_fox_b, g_ret, g_w_out[None], g_ffn, g_w_up[None], g_cw_loc,
                          g_conv_b, g_w_down[None], g_final.reshape(d))
    delta_out = by_weight(dl[0], dl[1], d_w_in, dl[2], dl[3], d_w_out, dl[4], d_w_up, dl[5], dl[6], d_w_down, dl[7])
    m_out = by_weight(ml[0], ml[1], m_w_in_n, ml[2], ml[3], m_w_out_n, ml[4], m_w_up_n, ml[5], ml[6], m_w_down_n, ml[7])
    v_out = by_weight(vl[0], vl[1], v_w_in_n, vl[2], vl[3], v_w_out_n, vl[4], v_w_up_n, vl[5], vl[6], v_w_down_n, vl[7])
    return (loss, grad_x[None]) + grads_out + delta_out + m_out + v_out
```

```python
import numpy as np
import jax
import jax.numpy as jnp
from jax import lax
from jax.experimental import pallas as pl
from jax.experimental.pallas import tpu as pltpu

F32 = jnp.float32
BF16 = jnp.bfloat16

D_MODEL = 1024
N_META = 16
N_PAD = 112
PREFIX = 128
RET_HEADS = 4
RET_DK = 64
RET_DV = 128
FOX_HEADS = 8
FOX_DH = 64
D_FF = 2816
ROPE_BASE = 10000.0
EPS = 1e-6
NEG = -1e30
RET_QK = RET_HEADS * RET_DK
RET_V = RET_HEADS * RET_DV
FOX_W = FOX_HEADS * FOX_DH
IN_WIDTH = 2 * RET_QK + 2 * RET_V + 3 * FOX_W + FOX_HEADS
IN_PAD = 3200
FF_COL_BLOCK = (IN_WIDTH - FOX_HEADS) // 128
QK_SCALE = 0.125

ADAM_LR = 0.001
ADAM_B1 = 0.9
ADAM_B2 = 0.999
ADAM_EPS = 1e-08
ADAM_WD = 0.01
ADAM_STEP = 10

N_DEV = 8
LANE = 128
ROW_TILE = 128
TOK_TILE = 384

NN = (((1,), (0,)), ((), ()))
NT = (((1,), (1,)), ((), ()))
TN = (((0,), (0,)), ((), ()))


def _pcall(body, **kw):
    return pl.pallas_call(body, **kw)


def _params(*sem):
    return pltpu.CompilerParams(dimension_semantics=sem)


def _dot(a, b, dims=NN):
    return lax.dot_general(a, b, dims, preferred_element_type=F32)


def _sigmoid(x):
    return 0.5 * jnp.tanh(0.5 * x) + 0.5


def _matmul(a, b, *, mode, grid, a_spec, b_spec, o_spec, out_shape, name, add=None, add_spec=None, after=None):
    dims = {"nn": NN, "nt": NT, "tn": TN}[mode]
    nk = grid[2]
    has_add = add is not None
    a_list, b_list = (list(a), list(b)) if isinstance(a, (list, tuple)) else ([a], [b])
    a_specs, b_specs = (list(a_spec), list(b_spec)) if isinstance(a_spec, (list, tuple)) else ([a_spec], [b_spec])
    nt = len(a_list)
    n_in = 2 * nt + int(has_add) + int(after is not None)

    def body(*refs):
        a_refs, b_refs = refs[:nt], refs[nt:2 * nt]
        add_ref = refs[2 * nt] if has_add else None
        o_ref = refs[n_in]
        part = _dot(a_refs[0][...].astype(BF16), b_refs[0][...].astype(BF16), dims)
        for ar, br in zip(a_refs[1:], b_refs[1:]):
            part = part + _dot(ar[...].astype(BF16), br[...].astype(BF16), dims)

        def finish(acc):
            if has_add:
                acc = acc + add_ref[...]
            o_ref[...] = acc.astype(o_ref.dtype)

        if nk == 1:
            finish(part)
        else:
            acc_ref = refs[-1]
            k = pl.program_id(2)

            @pl.when(k == 0)
            def _():
                acc_ref[...] = part

            @pl.when(k > 0)
            def _():
                acc_ref[...] += part

            @pl.when(k == nk - 1)
            def _():
                finish(acc_ref[...])

    in_specs = a_specs + b_specs + ([add_spec] if has_add else [])
    args = tuple(a_list) + tuple(b_list) + ((add,) if has_add else ())
    if after is not None:
        in_specs, args = in_specs + [pl.BlockSpec(memory_space=pl.ANY)], args + (after,)
    scratch = [] if nk == 1 else [pltpu.VMEM(tuple(d for d in o_spec.block_shape if d is not None), F32)]
    return _pcall(
        body, name=name, grid=grid, in_specs=in_specs, out_specs=o_spec, out_shape=out_shape,
        scratch_shapes=scratch, compiler_params=_params("parallel", "parallel", "arbitrary"),
    )(*args)


def _mm_simple(a, b, *, mode, tm, tn, tk, out_dtype, name, add=None, after=None):
    if mode == "tn":
        K, M = a.shape
    else:
        M, K = a.shape
    N = b.shape[0] if mode == "nt" else b.shape[1]
    grid = (M // tm, N // tn, K // tk)
    resident = dict(pipeline_mode=pl.Buffered(1)) if (tn == N and tk == K) else {}
    a_spec = pl.BlockSpec((tk, tm), lambda i, j, k: (k, i)) if mode == "tn" else pl.BlockSpec((tm, tk), lambda i, j, k: (i, k))
    b_spec = (pl.BlockSpec((tn, tk), lambda i, j, k: (j, k), **resident) if mode == "nt"
              else pl.BlockSpec((tk, tn), lambda i, j, k: (k, j), **resident))
    o_spec = pl.BlockSpec((tm, tn), lambda i, j, k: (i, j))
    return _matmul(a, b, mode=mode, grid=grid, a_spec=a_spec, b_spec=b_spec, o_spec=o_spec,
                   out_shape=jax.ShapeDtypeStruct((M, N), out_dtype), name=name, add=add,
                   add_spec=o_spec if add is not None else None, after=after)


def _matmul_rows(a_list, a_specs, b_list, b_specs, extras, extra_specs, out_specs, out_shape, epilogue, *,
                 mode, steps, name, after=None, scratch=()):
    dims = {"nn": NN, "nt": NT}[mode]
    nt, ne = len(a_list), len(extras)
    n_in = 2 * nt + ne + int(after is not None)

    def body(*refs):
        acc = _dot(refs[0][...].astype(BF16), refs[nt][...].astype(BF16), dims)
        for k in range(1, nt):
            acc = acc + _dot(refs[k][...].astype(BF16), refs[nt + k][...].astype(BF16), dims)
        epilogue(pl.program_id(0), acc, refs[2 * nt:2 * nt + ne], refs[n_in:])

    in_specs = list(a_specs) + list(b_specs) + list(extra_specs)
    args = tuple(a_list) + tuple(b_list) + tuple(extras)
    if after is not None:
        in_specs, args = in_specs + [pl.BlockSpec(memory_space=pl.ANY)], args + (after,)
    return _pcall(body, name=name, grid=(steps,), in_specs=in_specs, out_specs=out_specs, out_shape=out_shape,
                  scratch_shapes=list(scratch), compiler_params=_params("arbitrary"))(*args)


def _rms_bwd_tile(dy, x, gain, dres):
    r = lax.rsqrt(jnp.mean(x * x, axis=-1, keepdims=True) + EPS)
    xhat = x * r
    u = dy * gain
    return dres + r * (u - xhat * jnp.mean(u * xhat, axis=-1, keepdims=True)), jnp.sum(dy * xhat, axis=0, keepdims=True)


def _loss_tile(i, x, tgt, gain):
    d = x.shape[-1]
    r = lax.rsqrt(jnp.mean(x * x, axis=-1, keepdims=True) + EPS)
    xhat = x * r
    counted = (i * TOK_TILE + lax.broadcasted_iota(jnp.int32, (TOK_TILE, 1), 0)) >= PREFIX
    err = jnp.where(counted, xhat * gain - tgt, 0.0)
    dy = err * (1.0 / d)
    u = dy * gain
    dh = r * (u - xhat * jnp.mean(u * xhat, axis=-1, keepdims=True))
    return 0.5 * jnp.sum(jnp.mean(err * err, axis=-1, keepdims=True)), dh, jnp.sum(dy * xhat, axis=0, keepdims=True)


def _accumulate(ref, i, part):
    @pl.when(i == 0)
    def _():
        ref[...] = part

    @pl.when(i > 0)
    def _():
        ref[...] += part


def _prep_norm(x, meta, gain, name):
    seq, d = x.shape
    t = seq + PREFIX

    def body(xa_ref, xb_ref, xc_ref, meta_ref, g_ref, h_ref, n_ref):
        i = pl.program_id(0)

        @pl.when(i == 0)
        def _():
            h_ref[0:N_PAD, :] = jnp.zeros((N_PAD, d), F32)
            h_ref[N_PAD:ROW_TILE, :] = meta_ref[...]

        @pl.when(i > 0)
        def _():
            h_ref[0:ROW_TILE, :] = xa_ref[...]

        h_ref[ROW_TILE:2 * ROW_TILE, :] = xb_ref[...]
        h_ref[2 * ROW_TILE:3 * ROW_TILE, :] = xc_ref[...]
        h = h_ref[...]
        r = lax.rsqrt(jnp.mean(h * h, axis=-1, keepdims=True) + EPS)
        n_ref[...] = (h * r * g_ref[...]).astype(BF16)

    return _pcall(
        body, name=name, grid=(t // TOK_TILE,),
        in_specs=_shifted_row_specs(d) + [pl.BlockSpec((N_META, d), lambda i: (0, 0)), pl.BlockSpec((1, d), lambda i: (0, 0))],
        out_specs=[pl.BlockSpec((TOK_TILE, d), lambda i: (i, 0)), pl.BlockSpec((TOK_TILE, d), lambda i: (i, 0))],
        out_shape=[jax.ShapeDtypeStruct((t, d), F32), jax.ShapeDtypeStruct((t, d), BF16)],
        compiler_params=_params("parallel"),
    )(x, x, x, meta, gain)


def _shifted_row_specs(d):
    blocks_per_tile = TOK_TILE // ROW_TILE
    return [pl.BlockSpec((ROW_TILE, d), lambda i, r=r: (jnp.maximum(blocks_per_tile * i + r, 0), 0)) for r in (-1, 0, 1)]


def _ret_consts(bk):
    gam = 1.0 - 2.0 ** (-5.0 - np.arange(RET_HEADS))
    n = np.arange(bk)
    same_or_earlier_chunk = (n[None, :] // 64) <= (n[:, None] // 64)
    w = gam[:, None, None] ** np.abs(n[:, None] - n[None, :])[None] * same_or_earlier_chunk[None]
    wq = gam[:, None] ** (n[None, :] + 1.0)
    wk = gam[:, None] ** (bk - 1.0 - n[None, :])
    mask = (np.arange(RET_QK)[None, :] // RET_DK) == np.arange(RET_HEADS)[:, None]
    return (jnp.asarray(w, F32), jnp.asarray(wq[:, :, None], F32), jnp.asarray(wk[:, :, None], F32),
            jnp.asarray(mask[:, None, :], F32), [float(g ** bk) for g in gam])


def _rope_tables(t):
    half = RET_DK // 2
    inv = 1.0 / (ROPE_BASE ** (jnp.arange(half, dtype=F32) / half))
    ang = jnp.arange(t).astype(F32)[:, None] * inv[None, :]
    cos, sin = jnp.cos(ang), jnp.sin(ang)
    return (jnp.tile(jnp.concatenate([cos, cos], axis=1), (1, RET_HEADS)),
            jnp.tile(jnp.concatenate([-sin, sin], axis=1), (1, RET_HEADS)))


def _swap_halves(x):
    outs = []
    for s in range(x.shape[1] // LANE):
        xs = x[:, LANE * s:LANE * (s + 1)]
        lane = lax.broadcasted_iota(jnp.int32, xs.shape, 1)
        outs.append(jnp.where((lane & 32) == 0, pltpu.roll(xs, LANE - 32, axis=1), pltpu.roll(xs, 32, axis=1)))
    return outs[0] if len(outs) == 1 else jnp.concatenate(outs, axis=1)


def _rope(x, cos, sin_signed):
    return x * cos + _swap_halves(x) * sin_signed


def _rope_t(dx, cos, sin_signed):
    return dx * cos + _swap_halves(dx * sin_signed)


def _ret_fwd(proj, cos, sin, gain, name):
    t = proj.shape[0]
    bk = TOK_TILE
    nb = t // bk
    w, wq, wk, mask, g_blk = _ret_consts(bk)

    def body(q_ref, k_ref, v_ref, rg_ref, cos_ref, sin_ref, w_ref, wq_ref, wk_ref, mask_ref, gain_ref,
             opre_ref, og_ref, st_ref, r_ref):
        i = pl.program_id(0)

        @pl.when(i == 0)
        def _():
            r_ref[...] = jnp.zeros_like(r_ref)

        c, s = cos_ref[...], sin_ref[...]
        valid = ((i * bk + lax.broadcasted_iota(jnp.int32, (bk, 1), 0)) >= N_PAD).astype(F32)
        qr = _rope(q_ref[...], c, s)
        kr = _rope(k_ref[...], c, s) * QK_SCALE * valid
        kb = kr.astype(BF16)
        for h in range(RET_HEADS):
            hm = mask_ref[h]
            cols = slice(RET_DV * h, RET_DV * (h + 1))
            vh = v_ref[:, cols].astype(BF16)
            r_prev = r_ref[h]
            st_ref[0, h] = r_prev
            sm = _dot((qr * hm).astype(BF16), kb, NT) * w_ref[h]
            o = _dot(sm.astype(BF16), vh) + _dot((qr * (hm * wq_ref[h])).astype(BF16), r_prev.astype(BF16))
            r_ref[h] = g_blk[h] * r_prev + _dot((kr * wk_ref[h]).astype(BF16), vh, TN)
            opre_ref[:, cols] = o
            rstd = lax.rsqrt(jnp.mean(o * o, axis=-1, keepdims=True) + EPS)
            rg = rg_ref[:, cols]
            og_ref[:, cols] = (o * rstd * gain_ref[:, cols] * (rg * _sigmoid(rg))).astype(BF16)

    full = lambda shape: pl.BlockSpec(shape, lambda i: (0,) * len(shape))
    return _pcall(
        body, name=name, grid=(nb,),
        in_specs=[pl.BlockSpec((bk, RET_QK), lambda i: (i, 0)), pl.BlockSpec((bk, RET_QK), lambda i: (i, 1)),
                  pl.BlockSpec((bk, RET_V), lambda i: (i, 1)), pl.BlockSpec((bk, RET_V), lambda i: (i, 2)),
                  pl.BlockSpec((bk, RET_QK), lambda i: (i, 0)), pl.BlockSpec((bk, RET_QK), lambda i: (i, 0)),
                  full((RET_HEADS, bk, bk)), full((RET_HEADS, bk, 1)), full((RET_HEADS, bk, 1)),
                  full((RET_HEADS, 1, RET_QK)), full((1, RET_V))],
        out_specs=[pl.BlockSpec((bk, RET_V), lambda i: (i, 0)), pl.BlockSpec((bk, RET_V), lambda i: (i, 0)),
                   pl.BlockSpec((1, RET_HEADS, RET_QK, RET_DV), lambda i: (i, 0, 0, 0))],
        out_shape=[jax.ShapeDtypeStruct((t, RET_V), F32), jax.ShapeDtypeStruct((t, RET_V + FOX_W), BF16),
                   jax.ShapeDtypeStruct((nb, RET_HEADS, RET_QK, RET_DV), F32)],
        scratch_shapes=[pltpu.VMEM((RET_HEADS, RET_QK, RET_DV), F32)],
        compiler_params=_params("arbitrary"),
    )(proj, proj, proj, proj, cos, sin, w, wq, wk, mask, gain)


def _ret_bwd(proj, cos, sin, gain, dmixed, opre, states, name):
    t = proj.shape[0]
    bk = TOK_TILE
    nb = t // bk
    w, wq, wk, mask, g_blk = _ret_consts(bk)
    v0, g0 = 2 * RET_QK, 2 * RET_QK + RET_V

    def body(q_ref, k_ref, v_ref, rg_ref, cos_ref, sin_ref, w_ref, wq_ref, wk_ref, mask_ref, gain_ref,
             dog_ref, opre_ref, st_ref, dp_ref, gg_ref, dr_ref):
        step = pl.program_id(0)
        i = nb - 1 - step

        @pl.when(step == 0)
        def _():
            dr_ref[...] = jnp.zeros_like(dr_ref)
            gg_ref[...] = jnp.zeros_like(gg_ref)

        c, s = cos_ref[...], sin_ref[...]
        valid = ((i * bk + lax.broadcasted_iota(jnp.int32, (bk, 1), 0)) >= N_PAD).astype(F32)
        qr = _rope(q_ref[...], c, s)
        kr = _rope(k_ref[...], c, s) * QK_SCALE * valid
        kb = kr.astype(BF16)
        dqr = jnp.zeros((bk, RET_QK), F32)
        dkr = jnp.zeros((bk, RET_QK), F32)
        for h in range(RET_HEADS):
            hm = mask_ref[h]
            cols = slice(RET_DV * h, RET_DV * (h + 1))
            vh = v_ref[:, cols].astype(BF16)
            o = opre_ref[:, cols]
            rstd = lax.rsqrt(jnp.mean(o * o, axis=-1, keepdims=True) + EPS)
            xhat = o * rstd
            rg = rg_ref[:, cols]
            sg = _sigmoid(rg)
            gate = rg * sg
            gn = gain_ref[:, cols]
            dog = dog_ref[:, cols]
            dp_ref[:, g0 + RET_DV * h:g0 + RET_DV * (h + 1)] = (
                dog * xhat * gn * (sg * (1.0 + rg * (1.0 - sg)))).astype(BF16)
            gg_ref[:, cols] += jnp.sum(dog * xhat * gate, axis=0, keepdims=True)
            dxh = dog * gn * gate
            do = (rstd * (dxh - xhat * jnp.mean(dxh * xhat, axis=-1, keepdims=True))).astype(BF16)
            qm = (qr * hm).astype(BF16)
            qw = (qr * (hm * wq_ref[h])).astype(BF16)
            kw = (kr * wk_ref[h]).astype(BF16)
            wh = w_ref[h]
            sm = (_dot(qm, kb, NT) * wh).astype(BF16)
            ds = (_dot(do, vh, NT) * wh).astype(BF16)
            dr = dr_ref[h]
            drb = dr.astype(BF16)
            dp_ref[:, v0 + RET_DV * h:v0 + RET_DV * (h + 1)] = (_dot(sm, do, TN) + _dot(kw, drb)).astype(BF16)
            dqr = dqr + _dot(ds, kb) * hm + _dot(do, st_ref[0, h].astype(BF16), NT) * (hm * wq_ref[h])
            dkr = dkr + _dot(ds, qm, TN) + _dot(vh, drb, NT) * wk_ref[h]
            dr_ref[h] = g_blk[h] * dr + _dot(qw, do, TN)
        dp_ref[:, 0:RET_QK] = _rope_t(dqr, c, s).astype(BF16)
        dp_ref[:, RET_QK:2 * RET_QK] = _rope_t(dkr * (QK_SCALE * valid), c, s).astype(BF16)

    full = lambda shape: pl.BlockSpec(shape, lambda i: (0,) * len(shape))
    rev = lambda col: (lambda i: (nb - 1 - i, col))
    return _pcall(
        body, name=name, grid=(nb,),
        in_specs=[pl.BlockSpec((bk, RET_QK), rev(0)), pl.BlockSpec((bk, RET_QK), rev(1)),
                  pl.BlockSpec((bk, RET_V), rev(1)), pl.BlockSpec((bk, RET_V), rev(2)),
                  pl.BlockSpec((bk, RET_QK), rev(0)), pl.BlockSpec((bk, RET_QK), rev(0)),
                  full((RET_HEADS, bk, bk)), full((RET_HEADS, bk, 1)), full((RET_HEADS, bk, 1)),
                  full((RET_HEADS, 1, RET_QK)), full((1, RET_V)),
                  pl.BlockSpec((bk, RET_V), rev(0)), pl.BlockSpec((bk, RET_V), rev(0)),
                  pl.BlockSpec((1, RET_HEADS, RET_QK, RET_DV), lambda i: (nb - 1 - i, 0, 0, 0))],
        out_specs=[pl.BlockSpec((bk, g0 + RET_V), rev(0)), pl.BlockSpec((1, RET_V), lambda i: (0, 0))],
        out_shape=[jax.ShapeDtypeStruct((t, IN_PAD), BF16), jax.ShapeDtypeStruct((1, RET_V), F32)],
        scratch_shapes=[pltpu.VMEM((RET_HEADS, RET_QK, RET_DV), F32)],
        compiler_params=_params("arbitrary"),
    )(proj, proj, proj, proj, cos, sin, w, wq, wk, mask, gain, dmixed, opre, states)


def _forget_cumsum(proj, bias, name):
    t = proj.shape[0]
    rt = TOK_TILE
    nb = t // rt
    tril = jnp.asarray(np.tril(np.ones((rt, rt))), F32)

    def body(z_ref, b_ref, tril_ref, c_ref, carry_ref):
        i = pl.program_id(0)

        @pl.when(i == 0)
        def _():
            carry_ref[...] = jnp.zeros_like(carry_ref)

        z = z_ref[...] + b_ref[...]
        logf = jnp.minimum(z, 0.0) - jnp.log(1.0 + jnp.exp(-jnp.abs(z)))
        c = lax.dot_general(tril_ref[...], logf, NN, precision=lax.Precision.HIGHEST,
                            preferred_element_type=F32) + carry_ref[...]
        c_ref[...] = c
        carry_ref[...] = c[rt - 1:rt, :]

    return _pcall(
        body, name=name, grid=(nb,),
        in_specs=[pl.BlockSpec((rt, LANE), lambda i: (i, FF_COL_BLOCK)), pl.BlockSpec((1, LANE), lambda i: (0, 0)),
                  pl.BlockSpec((rt, rt), lambda i: (0, 0))],
        out_specs=pl.BlockSpec((rt, LANE), lambda i: (i, 0)),
        out_shape=jax.ShapeDtypeStruct((t, LANE), F32),
        scratch_shapes=[pltpu.VMEM((1, LANE), F32)],
        compiler_params=_params("arbitrary"),
    )(proj, bias, tril)


def _forget_cumsum_bwd(proj, bias, drs, dcs, dproj, name):
    t = proj.shape[0]
    rt = TOK_TILE
    nb = t // rt
    triu = jnp.asarray(np.triu(np.ones((rt, rt))), F32)

    def body(z_ref, b_ref, triu_ref, drs_ref, dcs_ref, dproj_in, dz_ref, gb_ref, carry_ref):
        step = pl.program_id(0)

        @pl.when(step == 0)
        def _():
            carry_ref[...] = jnp.zeros_like(carry_ref)
            gb_ref[...] = jnp.zeros_like(gb_ref)

        dlogf = lax.dot_general(triu_ref[...], drs_ref[...] - dcs_ref[...], NN, precision=lax.Precision.HIGHEST,
                                preferred_element_type=F32) + carry_ref[...]
        carry_ref[...] = dlogf[0:1, :]
        z = z_ref[...] + b_ref[...]
        is_head = lax.broadcasted_iota(jnp.int32, (rt, LANE), 1) < FOX_HEADS
        dz = jnp.where(is_head, dlogf / (1.0 + jnp.exp(z)), 0.0)
        dz_ref[...] = dz.astype(BF16)
        gb_ref[...] += jnp.sum(dz, axis=0, keepdims=True)

    return _pcall(
        body, name=name, grid=(nb,),
        in_specs=[pl.BlockSpec((rt, LANE), lambda i: (nb - 1 - i, FF_COL_BLOCK)),
                  pl.BlockSpec((1, LANE), lambda i: (0, 0)),
                  pl.BlockSpec((rt, rt), lambda i: (0, 0)),
                  pl.BlockSpec((rt, LANE), lambda i: (nb - 1 - i, 0)),
                  pl.BlockSpec((rt, LANE), lambda i: (nb - 1 - i, 0)),
                  pl.BlockSpec(memory_space=pl.ANY)],
        out_specs=[pl.BlockSpec((rt, LANE), lambda i: (nb - 1 - i, FF_COL_BLOCK)),
                   pl.BlockSpec((1, LANE), lambda i: (0, 0))],
        out_shape=[jax.ShapeDtypeStruct(dproj.shape, BF16), jax.ShapeDtypeStruct((1, LANE), F32)],
        input_output_aliases={5: 0},
        scratch_shapes=[pltpu.VMEM((1, LANE), F32)],
        compiler_params=_params("arbitrary"),
    )(proj, bias, triu, drs, dcs, dproj)


FOX_PAIRS = FOX_HEADS // 2
L_ONE_Q = FOX_DH
L_ONE_K = FOX_DH + 3
L_LSE = FOX_DH + 4


def _split3(x):
    hi = x.astype(BF16).astype(F32)
    r = x - hi
    mid = r.astype(BF16).astype(F32)
    return hi, mid, r - mid


def _head_to_low(slab, e):
    return slab if e == 0 else pltpu.roll(slab, FOX_DH, axis=1)


def _pair(a, b, low):
    return jnp.where(low, a, pltpu.roll(b, FOX_DH, axis=1))


def _fox_prep(proj, c, name):
    t = proj.shape[0]
    tq = TOK_TILE

    def body(p_ref, c_ref, qa_ref, ka_ref, va_ref):
        i = pl.program_id(0)
        lane = lax.broadcasted_iota(jnp.int32, (tq, LANE), 1)
        low = lane < FOX_DH
        live = (i * tq + lax.broadcasted_iota(jnp.int32, (tq, 1), 0)) >= N_PAD
        q_tail = jnp.where(lane < L_ONE_Q + 3, 1.0, 0.0)
        k_ones = (lane >= L_ONE_K) & (lane < L_ONE_K + 4)
        v_tail = jnp.where(lane < FOX_DH + 2, 1.0, 0.0)
        for pair in range(FOX_PAIRS):
            base = 3 * LANE * pair
            for e in range(2):
                h = 2 * pair + e
                q = _head_to_low(p_ref[:, base:base + LANE], e)
                k = _head_to_low(p_ref[:, base + LANE:base + 2 * LANE], e)
                v = _head_to_low(p_ref[:, base + 2 * LANE:base + 3 * LANE], e)
                hi, mid, lo = _split3(jnp.where(live, -c_ref[:, h:h + 1], NEG))
                ka = jnp.where(low, k, jnp.where(k_ones, 1.0, 0.0))
                ka = jnp.where(lane == L_ONE_Q, hi, jnp.where(lane == L_ONE_Q + 1, mid, jnp.where(lane == L_ONE_Q + 2, lo, ka)))
                qa_ref[h] = jnp.where(low, q * QK_SCALE, q_tail).astype(BF16)
                ka_ref[h] = ka.astype(BF16)
                va_ref[h] = jnp.where(low, v, v_tail).astype(BF16)

    out = jax.ShapeDtypeStruct((FOX_HEADS, t, LANE), BF16)
    ospec = pl.BlockSpec((FOX_HEADS, tq, LANE), lambda i: (0, i, 0))
    return _pcall(
        body, name=name, grid=(t // tq,),
        in_specs=[pl.BlockSpec((tq, 3 * FOX_W), lambda i: (i, 1)), pl.BlockSpec((tq, LANE), lambda i: (i, 0))],
        out_specs=[ospec, ospec, ospec], out_shape=[out, out, out],
        compiler_params=_params("parallel"),
    )(proj, c)


STEP_PAIRS = 2
STEP_HEADS = 2 * STEP_PAIRS
FOX_GROUPS = FOX_PAIRS // STEP_PAIRS
FWD_PAIRS = 4
FWD_HEADS = 2 * FWD_PAIRS
FWD_GROUPS = FOX_PAIRS // FWD_PAIRS


def _blockdiag(a, b):
    z = jnp.zeros_like(a)
    return jnp.concatenate([jnp.concatenate([a, z], axis=1), jnp.concatenate([z, b], axis=1)], axis=0)


def _fox_fwd(qa, ka, va, mixed, name):
    nh, nq, tq, _ = qa.shape
    t = nq * tq

    def body(qa_ref, ka_ref, va_ref, mixed_in, mixed_ref, o_ref, lse_ref):
        i = pl.program_id(1)
        lane = lax.broadcasted_iota(jnp.int32, (tq, LANE), 1)
        causal = lax.broadcasted_iota(jnp.int32, (tq, tq), 1) <= lax.broadcasted_iota(jnp.int32, (tq, tq), 0)
        qps = [jnp.concatenate([qa_ref[2 * c], qa_ref[2 * c + 1]], axis=1) for c in range(FWD_PAIRS)]

        def logits(j):
            return [_dot(qps[c], _blockdiag(ka_ref[2 * c, j], ka_ref[2 * c + 1, j]), NT) for c in range(FWD_PAIRS)]

        def update(j, scores, carry, diagonal):
            new = []
            for c in range(FWD_PAIRS):
                ms, acc = carry[c]
                ps, ms_new, alphas = [], [], []
                for e in range(2):
                    s = scores[c][:, e * tq:(e + 1) * tq]
                    if diagonal:
                        s = jnp.where(causal, s, NEG)
                    m_new = jnp.maximum(ms[e], jnp.max(s, axis=-1, keepdims=True))
                    ps.append(jnp.exp(s - m_new).astype(BF16))
                    ms_new.append(m_new)
                    alphas.append(jnp.broadcast_to(jnp.exp(ms[e] - m_new), (tq, LANE)))
                pv = _dot(jnp.concatenate(ps, axis=1), _blockdiag(va_ref[2 * c, j], va_ref[2 * c + 1, j]))
                new.append((tuple(ms_new), jnp.concatenate(alphas, axis=1) * acc + pv))
            return tuple(new)

        m0 = jnp.full((tq, 1), NEG, F32)
        init = tuple(((m0, m0), jnp.zeros((tq, 2 * LANE), F32)) for _ in range(FWD_PAIRS))
        carry = lax.fori_loop(0, i, lambda j, cr: update(j, logits(j), cr, False), init)
        o_pairs = []
        lse = jnp.zeros((tq, LANE), F32)
        for c, (ms, acc) in enumerate(update(i, logits(i), carry, True)):
            outs = []
            for e in range(2):
                half = acc[:, e * LANE:(e + 1) * LANE]
                l = half[:, FOX_DH:FOX_DH + 1]
                outs.append(half / l)
                lse = jnp.where(lane == 2 * c + e, ms[e] + jnp.log(l), lse)
            o_pairs.append(_pair(outs[0], outs[1], lane < FOX_DH))
        o_all = jnp.concatenate(o_pairs, axis=1)
        mixed_ref[...] = o_all.astype(BF16)
        o_ref[...] = o_all
        lse_ref[...] = lse

    width = FWD_PAIRS * LANE
    whole = pl.BlockSpec((FWD_HEADS, nq, tq, LANE), lambda g, i: (g, 0, 0, 0), pipeline_mode=pl.Buffered(1))
    return _pcall(
        body, name=name, grid=(FWD_GROUPS, nq),
        in_specs=[pl.BlockSpec((FWD_HEADS, None, tq, LANE), lambda g, i: (g, i, 0, 0)), whole, whole,
                  pl.BlockSpec(memory_space=pl.ANY)],
        out_specs=[pl.BlockSpec((tq, width), lambda g, i: (i, RET_V // width + g)),
                   pl.BlockSpec((tq, width), lambda g, i: (i, g)),
                   pl.BlockSpec((None, tq, LANE), lambda g, i: (g, i, 0))],
        out_shape=[jax.ShapeDtypeStruct(mixed.shape, BF16), jax.ShapeDtypeStruct((t, FOX_W), F32),
                   jax.ShapeDtypeStruct((FWD_GROUPS, t, LANE), F32)],
        input_output_aliases={3: 0},
        compiler_params=_params("parallel", "parallel"),
    )(qa, ka, va, mixed)


def _fox_prep_bwd(dmixed, o_fox, lse, qa, name):
    t = dmixed.shape[0]
    tq = TOK_TILE

    def body(dm_ref, o_ref, lse_ref, qa_ref, qab_ref, doa_ref):
        i = pl.program_id(0)
        lane = lax.broadcasted_iota(jnp.int32, (tq, LANE), 1)
        low = lane < FOX_DH
        live = (i * tq + lax.broadcasted_iota(jnp.int32, (tq, 1), 0)) >= N_PAD
        for pair in range(FOX_PAIRS):
            cols = slice(LANE * pair, LANE * (pair + 1))
            d_slab = dm_ref[:, cols]
            prod = d_slab * o_ref[:, cols]
            for e in range(2):
                h = 2 * pair + e
                nd = -jnp.sum(jnp.where(low, _head_to_low(prod, e), 0.0), axis=-1, keepdims=True)
                nd_hi = nd.astype(BF16).astype(F32)
                doa = jnp.where(low, _head_to_low(d_slab, e), 0.0)
                doa = jnp.where(lane == FOX_DH, nd_hi, jnp.where(lane == FOX_DH + 1, nd - nd_hi, doa))
                doa_ref[h] = doa.astype(BF16)
                lse_h = lse_ref[h // FWD_HEADS][:, h % FWD_HEADS:h % FWD_HEADS + 1]
                hi, mid, lo = _split3(jnp.where(live, -lse_h, 0.0))
                qab = qa_ref[h].astype(F32)
                qab = jnp.where(lane == L_LSE, hi, jnp.where(lane == L_LSE + 1, mid, jnp.where(lane == L_LSE + 2, lo, qab)))
                qab_ref[h] = qab.astype(BF16)

    out = jax.ShapeDtypeStruct((FOX_HEADS, t, LANE), BF16)
    hspec = pl.BlockSpec((FOX_HEADS, tq, LANE), lambda i: (0, i, 0))
    return _pcall(
        body, name=name, grid=(t // tq,),
        in_specs=[pl.BlockSpec((tq, FOX_W), lambda i: (i, 1)), pl.BlockSpec((tq, FOX_W), lambda i: (i, 0)),
                  pl.BlockSpec((FWD_GROUPS, tq, LANE), lambda i: (0, i, 0)), hspec],
        out_specs=[hspec, hspec], out_shape=[out, out],
        compiler_params=_params("parallel"),
    )(dmixed, o_fox, lse, qa)


def _fox_bwd(qab, doa, ka, va, dproj, name):
    nh, nq, tq, _ = qab.shape
    t = nq * tq
    slab = 3 * LANE * STEP_PAIRS
    group0 = (2 * RET_QK + 2 * RET_V) // slab

    def body(qab_ref, doa_ref, ka_ref, va_ref, dproj_in, dp_ref, drs_ref, dcs_ref, dq_ref):
        g, j = pl.program_id(0), pl.program_id(1)

        @pl.when((g == 0) & (j == 0))
        def _():
            drs_ref[...] = jnp.zeros_like(drs_ref)
            dcs_ref[...] = jnp.zeros_like(dcs_ref)

        @pl.when(j == 0)
        def _():
            dq_ref[...] = jnp.zeros_like(dq_ref)

        lane = lax.broadcasted_iota(jnp.int32, (tq, LANE), 1)
        low = lane < FOX_DH
        key_le_query = lax.broadcasted_iota(jnp.int32, (tq, tq), 0) <= lax.broadcasted_iota(jnp.int32, (tq, tq), 1)

        def by_head(c, a, b, col):
            h = STEP_HEADS * g + 2 * c
            return jnp.where(lane == h, a[:, col:col + 1], jnp.where(lane == h + 1, b[:, col:col + 1], 0.0))

        kbs = [ka_ref[h] for h in range(STEP_HEADS)]
        vbs = [va_ref[h] for h in range(STEP_HEADS)]

        def step(i, carry, diagonal):
            qbs = [qab_ref[h, i] for h in range(STEP_HEADS)]
            dobs = [doa_ref[h, i] for h in range(STEP_HEADS)]
            st = [_dot(kbs[h], qbs[h], NT) for h in range(STEP_HEADS)]
            dpt = [_dot(vbs[h], dobs[h], NT) for h in range(STEP_HEADS)]
            new = []
            for h in range(STEP_HEADS):
                p = jnp.exp(st[h])
                if diagonal:
                    p = jnp.where(key_le_query, p, 0.0)
                ds = (p * dpt[h]).astype(BF16)
                dq_ref[h, i] += _dot(ds, kbs[h], TN)
                dk, dv = carry[h]
                new.append((dk + _dot(ds, qbs[h]), dv + _dot(p.astype(BF16), dobs[h])))
            return tuple(new)

        zero = jnp.zeros((tq, LANE), F32)
        carry = step(j, tuple((zero, zero) for _ in range(STEP_HEADS)), True)
        carry = lax.fori_loop(j + 1, nq, lambda i, cr: step(i, cr, False), carry)
        rows = pl.ds(pl.multiple_of(j * tq, tq), tq)
        for c in range(STEP_PAIRS):
            (dka, dva), (dkb, dvb) = carry[2 * c], carry[2 * c + 1]
            c0 = 3 * LANE * c
            dp_ref[rows, c0 + LANE:c0 + 2 * LANE] = _pair(dka, dkb, low).astype(BF16)
            dp_ref[rows, c0 + 2 * LANE:c0 + 3 * LANE] = _pair(dva, dvb, low).astype(BF16)
            dcs_ref[rows, :] += by_head(c, dka, dkb, L_ONE_Q)

        @pl.when(j == nq - 1)
        def _():
            for c in range(STEP_PAIRS):
                for blk in range(nq):
                    r = slice(blk * tq, (blk + 1) * tq)
                    a, b = dq_ref[2 * c, blk], dq_ref[2 * c + 1, blk]
                    dp_ref[r, 3 * LANE * c:3 * LANE * c + LANE] = (_pair(a, b, low) * QK_SCALE).astype(BF16)
                    drs_ref[r, :] += by_head(c, a, b, L_ONE_K)

    whole = pl.BlockSpec((STEP_HEADS, nq, tq, LANE), lambda g, j: (g, 0, 0, 0), pipeline_mode=pl.Buffered(1))
    blk = pl.BlockSpec((STEP_HEADS, None, tq, LANE), lambda g, j: (g, j, 0, 0))
    sums = pl.BlockSpec((t, LANE), lambda g, j: (0, 0), pipeline_mode=pl.Buffered(1))
    return _pcall(
        body, name=name, grid=(FOX_GROUPS, nq),
        in_specs=[whole, whole, blk, blk, pl.BlockSpec(memory_space=pl.ANY)],
        out_specs=[pl.BlockSpec((t, slab), lambda g, j: (0, group0 + g)), sums, sums],
        out_shape=[jax.ShapeDtypeStruct(dproj.shape, BF16), jax.ShapeDtypeStruct((t, LANE), F32),
                   jax.ShapeDtypeStruct((t, LANE), F32)],
        input_output_aliases={4: 0},
        scratch_shapes=[pltpu.VMEM((STEP_HEADS, nq, tq, LANE), F32)],
        compiler_params=_params("arbitrary", "arbitrary"),
    )(qab, doa, ka, va, dproj)


HALO = 8


def _rows_ext(ref, r0, rows, t, before, after):
    lo, hi = r0 - before, r0 + rows + after
    width = ref.shape[-1]
    parts = []
    if lo < 0:
        parts.append(jnp.zeros((-lo, width), F32))
    parts.append(ref[max(lo, 0):min(hi, t), :].astype(F32))
    if hi > t:
        parts.append(jnp.zeros((hi - t, width), F32))
    return parts[0] if len(parts) == 1 else jnp.concatenate(parts, axis=0)


def _conv_taps(a_ext, r0_ext, cw_ref, cb_ref):
    n = a_ext.shape[0]
    if r0_ext < N_PAD:
        row = r0_ext + lax.broadcasted_iota(jnp.int32, (n, 1), 0)
        a_ext = jnp.where(row >= N_PAD, a_ext, 0.0)
    a1 = pltpu.roll(a_ext, 1, axis=0)
    a2 = pltpu.roll(a_ext, 2, axis=0)
    acc = cb_ref[...] + a2 * cw_ref[0:1, :] + a1 * cw_ref[1:2, :] + a_ext * cw_ref[2:3, :]
    return a_ext, a1, a2, acc


FF_COLS = 256


def _up_conv_fwd(n2, w_up_t, conv_w8, conv_b, name):
    t, d = n2.shape
    f = w_up_t.shape[1]
    rows = TOK_TILE
    starts = list(range(0, t, rows))

    def body(n_ref, wa_ref, wb_ref, cw_ref, cb_ref, up_ref, g_ref):
        wa, wb = wa_ref[...], wb_ref[...]

        def project(r0):
            n_rows = n_ref[r0:r0 + rows, :]
            up_ref[0, r0:r0 + rows, :] = _dot(n_rows, wa, NT)
            up_ref[1, r0:r0 + rows, :] = _dot(n_rows, wb, NT)

        def activate(r0):
            a_ext = _rows_ext(up_ref.at[0], r0, rows, t, HALO, 0)
            _, _, _, acc = _conv_taps(a_ext, r0 - HALO, cw_ref, cb_ref)
            acc = acc[HALO:, :]
            g_ref[r0:r0 + rows, :] = (acc * _sigmoid(acc) * up_ref[1, r0:r0 + rows, :]).astype(BF16)

        project(starts[0])
        for r0, r_next in zip(starts, starts[1:] + [None]):
            if r_next is not None:
                project(r_next)
            activate(r0)

    return _pcall(
        body, name=name, grid=(f // FF_COLS,),
        in_specs=[pl.BlockSpec((t, d), lambda j: (0, 0), pipeline_mode=pl.Buffered(1)),
                  pl.BlockSpec((None, FF_COLS, d), lambda j: (0, j, 0)), pl.BlockSpec((None, FF_COLS, d), lambda j: (1, j, 0)),
                  pl.BlockSpec((8, FF_COLS), lambda j: (0, j)), pl.BlockSpec((1, FF_COLS), lambda j: (0, j))],
        out_specs=[pl.BlockSpec((2, t, FF_COLS), lambda j: (0, 0, j)), pl.BlockSpec((t, FF_COLS), lambda j: (0, j))],
        out_shape=[jax.ShapeDtypeStruct((2, t, f), F32), jax.ShapeDtypeStruct((t, f), BF16)],
        compiler_params=_params("parallel"),
    )(n2, w_up_t, w_up_t, conv_w8, conv_b)


def _dg_conv_bwd(up, conv_w8, conv_b, dh2, w_down, name):
    _, t, f = up.shape
    d = dh2.shape[1]
    rows = TOK_TILE
    starts = list(range(0, t, rows))

    def body(a_ref, b_ref, cw_ref, cb_ref, dh_ref, wd_ref, dup_ref, gcw_ref, gcb_ref, dg_ref):
        wd = wd_ref[...]

        def project(r0):
            dg_ref[r0:r0 + rows, :] = _dot(dh_ref[r0:r0 + rows, :], wd, NT)

        gw = [jnp.zeros((1, FF_COLS), F32) for _ in range(3)]
        gb = jnp.zeros((1, FF_COLS), F32)
        project(starts[0])
        for r0, r_next in zip(starts, starts[1:] + [None]):
            if r_next is not None:
                project(r_next)
            a_ext = _rows_ext(a_ref, r0, rows, t, HALO, HALO)
            b_ext = _rows_ext(b_ref, r0, rows, t, HALO, HALO)
            dg_ext = _rows_ext(dg_ref, r0, rows, t, HALO, HALO)
            a0, a1, a2, acc = _conv_taps(a_ext, r0 - HALO, cw_ref, cb_ref)
            sg = _sigmoid(acc)
            dacc = dg_ext * b_ext * (sg * (1.0 + acc * (1.0 - sg)))
            n = dacc.shape[0]
            da = (dacc * cw_ref[2:3, :] + pltpu.roll(dacc, n - 1, axis=0) * cw_ref[1:2, :]
                  + pltpu.roll(dacc, n - 2, axis=0) * cw_ref[0:1, :])
            core = slice(HALO, HALO + rows)
            da = da[core, :]
            if r0 < N_PAD:
                row = r0 + lax.broadcasted_iota(jnp.int32, (rows, 1), 0)
                da = jnp.where(row >= N_PAD, da, 0.0)
            dup_ref[0, r0:r0 + rows, :] = da.astype(BF16)
            dup_ref[1, r0:r0 + rows, :] = (dg_ext * acc * sg)[core, :].astype(BF16)
            dacc_c = dacc[core, :]
            gw[0] = gw[0] + jnp.sum(dacc_c * a2[core, :], axis=0, keepdims=True)
            gw[1] = gw[1] + jnp.sum(dacc_c * a1[core, :], axis=0, keepdims=True)
            gw[2] = gw[2] + jnp.sum(dacc_c * a0[core, :], axis=0, keepdims=True)
            gb = gb + jnp.sum(dacc_c, axis=0, keepdims=True)
        gcw_ref[...] = jnp.zeros((8, FF_COLS), F32)
        for tap in range(3):
            gcw_ref[tap:tap + 1, :] = gw[tap]
        gcb_ref[...] = gb

    return _pcall(
        body, name=name, grid=(f // FF_COLS,),
        in_specs=[pl.BlockSpec((None, t, FF_COLS), lambda j: (0, 0, j)), pl.BlockSpec((None, t, FF_COLS), lambda j: (1, 0, j)),
                  pl.BlockSpec((8, FF_COLS), lambda j: (0, j)), pl.BlockSpec((1, FF_COLS), lambda j: (0, j)),
                  pl.BlockSpec((t, d), lambda j: (0, 0), pipeline_mode=pl.Buffered(1)),
                  pl.BlockSpec((FF_COLS, d), lambda j: (j, 0))],
        out_specs=[pl.BlockSpec((2, t, FF_COLS), lambda j: (0, 0, j)), pl.BlockSpec((8, FF_COLS), lambda j: (0, j)),
                   pl.BlockSpec((1, FF_COLS), lambda j: (0, j))],
        out_shape=[jax.ShapeDtypeStruct((2, t, f), BF16), jax.ShapeDtypeStruct((8, f), F32),
                   jax.ShapeDtypeStruct((1, f), F32)],
        scratch_shapes=[pltpu.VMEM((t, FF_COLS), F32)],
        compiler_params=_params("parallel"),
    )(up, up, conv_w8, conv_b, dh2, w_down)


def _exchange(arrays, kinds, name, after=None):
    n = len(arrays)
    npeer = N_DEV - 1
    n_in = n + int(after is not None)

    def body(*refs):
        ins, outs = refs[:n], refs[n_in:n_in + n]
        send_sems, recv_sems, local_sems = refs[n_in + n:]
        x, y, c = lax.axis_index("x"), lax.axis_index("y"), lax.axis_index("c")
        me = 4 * x + 2 * y + c
        copies, locals_ = [], []
        for a in range(n):
            gather = kinds[a] == "gather"
            own = pltpu.make_async_copy(ins[a] if gather else ins[a].at[me], outs[a].at[me], local_sems.at[a])
            own.start()
            locals_.append(own)
            for d in range(1, N_DEV):
                px = 1 - x if d & 4 else x
                py = 1 - y if d & 2 else y
                pc = 1 - c if d & 1 else c
                src = ins[a] if gather else ins[a].at[4 * px + 2 * py + pc]
                cp = pltpu.make_async_remote_copy(
                    src_ref=src, dst_ref=outs[a].at[me],
                    send_sem=send_sems.at[a * npeer + d - 1], recv_sem=recv_sems.at[a * npeer + d - 1],
                    device_id=(px, py, pc), device_id_type=pl.DeviceIdType.MESH)
                cp.start()
                copies.append(cp)
        for cp in copies:
            cp.wait_recv()
        for cp in copies:
            cp.wait_send()
        for own in locals_:
            own.wait()

    out_shape = [jax.ShapeDtypeStruct((N_DEV,) + (a.shape if k == "gather" else a.shape[1:]), a.dtype)
                 for a, k in zip(arrays, kinds)]
    return _pcall(
        body, name=name,
        in_specs=[pl.BlockSpec(memory_space=pl.ANY)] * n_in,
        out_specs=[pl.BlockSpec(memory_space=pl.ANY)] * n,
        out_shape=out_shape,
        scratch_shapes=[pltpu.SemaphoreType.DMA((n * npeer,)), pltpu.SemaphoreType.DMA((n * npeer,)),
                        pltpu.SemaphoreType.DMA((n,))],
        compiler_params=pltpu.CompilerParams(has_side_effects=True),
    )(*arrays, *([] if after is None else [after]))


def _peer_copies(srcs, lands, kinds, send_sems, recv_sems):
    x, y, c = lax.axis_index("x"), lax.axis_index("y"), lax.axis_index("c")
    me = 4 * x + 2 * y + c
    copies = []
    for a in range(len(srcs)):
        for d in range(1, N_DEV):
            px = 1 - x if d & 4 else x
            py = 1 - y if d & 2 else y
            pc = 1 - c if d & 1 else c
            k = a * (N_DEV - 1) + d - 1
            copies.append(pltpu.make_async_remote_copy(
                src_ref=srcs[a] if kinds[a] == "gather" else srcs[a].at[4 * px + 2 * py + pc], dst_ref=lands[a].at[me],
                send_sem=send_sems.at[k], recv_sem=recv_sems.at[k],
                device_id=(px, py, pc), device_id_type=pl.DeviceIdType.MESH))
    return copies


def _exchange_start(arrays, kinds, name, after=None):
    n = len(arrays)
    nsem = n * (N_DEV - 1)
    hbm = pl.BlockSpec(memory_space=pltpu.HBM)
    sem = pl.BlockSpec(memory_space=pltpu.SEMAPHORE)
    land_shapes = [(N_DEV,) + (a.shape if k == "gather" else a.shape[1:]) for a, k in zip(arrays, kinds)]

    n_in = 2 * n + int(after is not None)

    def body(*refs):
        srcs, lands = refs[:n], refs[n:2 * n]
        send_sems, recv_sems = refs[n_in], refs[n_in + 1]
        token = refs[-1]
        for cp in _peer_copies(srcs, lands, kinds, send_sems, recv_sems):
            cp.start()
        token[...] = jnp.zeros_like(token)

    operands = [pltpu.with_memory_space_constraint(a, pltpu.HBM) for a in arrays]
    operands += [pltpu.with_memory_space_constraint(lax.empty(s, a.dtype), pltpu.HBM) for s, a in zip(land_shapes, arrays)]
    operands += [] if after is None else [after]
    out = _pcall(
        body, name=name,
        in_specs=[hbm] * (2 * n) + ([] if after is None else [pl.BlockSpec(memory_space=pl.ANY)]),
        out_specs=[sem, sem] + [hbm] * (2 * n) + [pl.BlockSpec(memory_space=pltpu.VMEM)],
        out_shape=[pltpu.SemaphoreType.DMA((nsem,)), pltpu.SemaphoreType.DMA((nsem,))]
        + [pltpu.HBM(a.shape, a.dtype) for a in arrays]
        + [pltpu.HBM(s, a.dtype) for s, a in zip(land_shapes, arrays)]
        + [jax.ShapeDtypeStruct((8, LANE), F32)],
        input_output_aliases={k: 2 + k for k in range(2 * n)},
        compiler_params=pltpu.CompilerParams(has_side_effects=pltpu.SideEffectType.DATAFLOW_SIDE_EFFECTING),
    )(*operands)
    return out[0], out[1], list(out[2:2 + n]), list(out[2 + n:2 + 2 * n]), out[-1]


def _exchange_wait(started, kinds, after, name):
    send_sems, recv_sems, srcs, lands, _ = started
    n = len(srcs)
    hbm = pl.BlockSpec(memory_space=pltpu.HBM)
    sem = pl.BlockSpec(memory_space=pltpu.SEMAPHORE)

    def body(*refs):
        src_refs, land_refs = refs[:n], refs[n:2 * n]
        copies = _peer_copies(src_refs, land_refs, kinds, refs[2 * n], refs[2 * n + 1])
        for cp in copies:
            cp.wait_send()
        for cp in copies:
            cp.wait_recv()

    out = _pcall(
        body, name=name,
        in_specs=[hbm] * (2 * n) + [sem, sem, pl.BlockSpec(memory_space=pl.ANY)],
        out_specs=[hbm] * (2 * n),
        out_shape=[pltpu.HBM(a.shape, a.dtype) for a in srcs + lands],
        input_output_aliases={k: k for k in range(2 * n)},
        compiler_params=pltpu.CompilerParams(has_side_effects=pltpu.SideEffectType.DATAFLOW_SIDE_EFFECTING),
    )(*srcs, *lands, send_sems, recv_sems, after)
    me = 4 * lax.axis_index("x") + 2 * lax.axis_index("y") + lax.axis_index("c")
    filled = []
    for src, land, kind in zip(out[:n], out[n:], kinds):
        own = src if kind == "gather" else lax.dynamic_index_in_dim(src, me, axis=0, keepdims=False)
        filled.append(lax.dynamic_update_slice(land, own[None], (me,) + (0,) * own.ndim))
    return filled


def _sum_slots(slots, name, rows_tile):
    nd, r, c = slots.shape

    def body(s_ref, o_ref):
        acc = s_ref[0].astype(F32)
        for p in range(1, nd):
            acc = acc + s_ref[p].astype(F32)
        o_ref[...] = acc

    return _pcall(
        body, name=name, grid=(r // rows_tile,),
        in_specs=[pl.BlockSpec((nd, rows_tile, c), lambda i: (0, i, 0))],
        out_specs=pl.BlockSpec((rows_tile, c), lambda i: (i, 0)),
        out_shape=jax.ShapeDtypeStruct((r, c), F32),
        compiler_params=_params("parallel"),
    )(slots)


def _sum_slots_small(slot_arrays, name):
    n = len(slot_arrays)

    def body(*refs):
        for s_ref, o_ref in zip(refs[:n], refs[n:]):
            acc = s_ref[0]
            for p in range(1, s_ref.shape[0]):
                acc = acc + s_ref[p]
            o_ref[...] = acc

    return _pcall(body, name=name, out_shape=[jax.ShapeDtypeStruct(a.shape[1:], F32) for a in slot_arrays])(*slot_arrays)


def _adamw_update(w_ref, g_ref, m_ref, v_ref, d_ref, nm_ref, nv_ref):
    gr = g_ref[...]
    nm = ADAM_B1 * m_ref[...] + (1.0 - ADAM_B1) * gr
    nv = ADAM_B2 * v_ref[...] + (1.0 - ADAM_B2) * (gr * gr)
    m_hat = nm / (1.0 - ADAM_B1 ** ADAM_STEP)
    v_hat = nv / (1.0 - ADAM_B2 ** ADAM_STEP)
    d_ref[...] = -ADAM_LR * (m_hat / (jnp.sqrt(v_hat) + ADAM_EPS) + ADAM_WD * w_ref[...])
    nm_ref[...] = nm
    nv_ref[...] = nv


def _adamw_small(ws, gs, ms, vs, name):
    n = len(ws)

    def body(*refs):
        ins, outs = refs[:4 * n], refs[4 * n:]
        for k in range(n):
            _adamw_update(ins[k], ins[n + k], ins[2 * n + k], ins[3 * n + k], outs[k], outs[n + k], outs[2 * n + k])

    shapes = [jax.ShapeDtypeStruct(w.shape, F32) for w in ws]
    out = _pcall(body, name=name, out_shape=shapes * 3)(*ws, *gs, *ms, *vs)
    return list(out[:n]), list(out[n:2 * n]), list(out[2 * n:])


def _adamw(w, g, m, v, name, rows_tile):
    r, c = w.shape
    body = lambda *refs: _adamw_update(*refs)
    spec = pl.BlockSpec((rows_tile, c), lambda i: (i, 0))
    shp = jax.ShapeDtypeStruct((r, c), F32)
    return _pcall(
        body, name=name, grid=(r // rows_tile,), in_specs=[spec] * 4, out_specs=[spec] * 3, out_shape=[shp] * 3,
        compiler_params=_params("parallel"),
    )(w, g, m, v)


F0 = 2 * RET_QK + 2 * RET_V


def _to_internal_rows(w_t):
    cols = w_t.shape[1]
    fox = w_t[F0:F0 + 3 * FOX_W].reshape(3, FOX_PAIRS, LANE, cols).transpose(1, 0, 2, 3).reshape(3 * FOX_W, cols)
    tail = jnp.zeros((IN_PAD - IN_WIDTH, cols), w_t.dtype)
    return jnp.concatenate([w_t[:F0], fox, w_t[F0 + 3 * FOX_W:], tail], axis=0)


def _from_internal_rows(g_t):
    cols = g_t.shape[1]
    fox = g_t[F0:F0 + 3 * FOX_W].reshape(FOX_PAIRS, 3, LANE, cols).transpose(1, 0, 2, 3).reshape(3 * FOX_W, cols)
    return jnp.concatenate([g_t[:F0], fox, g_t[F0 + 3 * FOX_W:F0 + 3 * FOX_W + FOX_HEADS]], axis=0)


def _local_step(x, target, meta, attn_g, fox_b, ret_g, ffn_g, conv_w8, conv_b, final_g,
                first_weight, late_weights, ffn_grads_ready, out_grad_ready, in_grad_ready):
    seq, d = x.shape
    t = seq + PREFIX
    tm = TOK_TILE
    nq = t // tm
    fox_b128 = jnp.pad(fox_b, ((0, 0), (0, LANE - FOX_HEADS)))

    h0, n1 = _prep_norm(x, meta, attn_g, "prep_norm")
    w_in_t = first_weight(n1)
    proj = _mm_simple(n1, w_in_t, mode="nt", tm=tm, tn=IN_PAD, tk=d, out_dtype=F32, name="mm_in")
    cos, sin = _rope_tables(t)
    o_pre, mixed, states = _ret_fwd(proj, cos, sin, ret_g, "ret_fwd")
    c = _forget_cumsum(proj, fox_b128, "forget_cumsum")
    qa, ka, va = _fox_prep(proj, c, "fox_prep")
    by_block = lambda a: a.reshape(FOX_HEADS, nq, tm, LANE)
    mixed, o_fox, lse = _fox_fwd(by_block(qa), by_block(ka), by_block(va), mixed, "fox_fwd")
    w_out, w_up_t, w_down = late_weights(o_fox)
    tile = pl.BlockSpec((tm, d), lambda i: (i, 0))
    row_vec = pl.BlockSpec((1, d), lambda i: (0, 0))
    resident = lambda shape: pl.BlockSpec(shape, lambda i: (0,) * len(shape), pipeline_mode=pl.Buffered(1))
    acts = lambda dtype: jax.ShapeDtypeStruct((t, d), dtype)
    vec = jax.ShapeDtypeStruct((1, d), F32)

    def residual_and_norm(i, acc, ins, outs):
        h = acc + ins[0][...]
        outs[0][...] = h
        outs[1][...] = (h * lax.rsqrt(jnp.mean(h * h, axis=-1, keepdims=True) + EPS) * ins[1][...]).astype(BF16)

    h1, n2 = _matmul_rows([mixed], [tile], [w_out], [resident((d, d))], [h0, ffn_g], [tile, row_vec],
                          [tile, tile], [acts(F32), acts(BF16)], residual_and_norm, mode="nn", steps=nq, name="mm_out_norm")
    nf = D_FF // 1408
    up, g = _up_conv_fwd(n2, w_up_t, conv_w8, conv_b, "up_conv_fwd")

    def residual_loss_bwd(i, acc, ins, outs):
        loss_ref, dh_ref, dhb_ref, gg_ref = outs
        part, dh, gg = _loss_tile(i, acc + ins[0][...], jnp.concatenate([ins[1][...], ins[2][...], ins[3][...]], axis=0),
                                  ins[4][...])
        _accumulate(loss_ref, i, jnp.broadcast_to(part, loss_ref.shape))
        dh_ref[...] = dh
        dhb_ref[...] = dh.astype(BF16)
        _accumulate(gg_ref, i, gg)

    loss_tile, dh2, dh2_b, g_final = _matmul_rows(
        [g], [pl.BlockSpec((tm, D_FF), lambda i: (i, 0))], [w_down], [resident((D_FF, d))],
        [h1, target, target, target, final_g], [tile] + _shifted_row_specs(d) + [row_vec],
        [pl.BlockSpec((8, LANE), lambda i: (0, 0)), tile, tile, row_vec],
        [jax.ShapeDtypeStruct((8, LANE), F32), acts(F32), acts(BF16), vec], residual_loss_bwd,
        mode="nn", steps=nq, name="mm_down_loss")

    tkw = 1408 if t % 1408 == 0 else tm
    gw_down = _mm_simple(g, dh2_b, mode="tn", tm=1408, tn=d, tk=tkw, out_dtype=BF16, name="mm_gw_down")
    dup, g_conv_w8, g_conv_b = _dg_conv_bwd(up, conv_w8, conv_b, dh2_b, w_down, "dg_conv_bwd")

    half = lambda p: pl.BlockSpec((None, tm, D_FF), lambda i: (p, i, 0))
    half_w = lambda p: pl.BlockSpec((None, D_FF, d), lambda i: (p, 0, 0), pipeline_mode=pl.Buffered(1))
    gw_up_t = _matmul(
        dup, n2, mode="tn", grid=(2 * nf, 1, t // tkw),
        a_spec=pl.BlockSpec((None, tkw, 1408), lambda i, j, k: (i // nf, k, i % nf)),
        b_spec=pl.BlockSpec((tkw, d), lambda i, j, k: (k, 0)),
        o_spec=pl.BlockSpec((1408, d), lambda i, j, k: (i, 0)),
        out_shape=jax.ShapeDtypeStruct((2 * D_FF, d), BF16), name="mm_gw_up")
    def norm_bwd_and_mixer_grad(i, acc, ins, outs):
        dh, gg = _rms_bwd_tile(acc, ins[0][...], ins[1][...], ins[2][...])
        outs[0][...] = dh
        _accumulate(outs[1], i, gg)
        outs[2][...] = _dot(dh.astype(BF16), ins[3][...], NT)

    dh1, g_ffn, dmixed = _matmul_rows(
        [dup, dup], [half(0), half(1)], [w_up_t, w_up_t], [half_w(0), half_w(1)],
        [h1, ffn_g, dh2, w_out], [tile, row_vec, tile, resident((d, d))], [tile, row_vec, tile],
        [acts(F32), vec, acts(F32)], norm_bwd_and_mixer_grad,
        mode="nn", steps=nq, name="mm_dn2_norm_bwd", after=ffn_grads_ready(gw_down, gw_up_t))
    gw_out = _mm_simple(mixed, dh1, mode="tn", tm=d, tn=d, tk=tkw, out_dtype=BF16, name="mm_gw_out")
    dproj, g_ret = _ret_bwd(proj, cos, sin, ret_g + out_grad_ready(gw_out), dmixed, o_pre, states, "ret_bwd")
    qab, doa = _fox_prep_bwd(dmixed, o_fox, lse, qa, "fox_prep_bwd")
    dproj, drs, dcs = _fox_bwd(by_block(qab), by_block(doa), by_block(ka), by_block(va), dproj, "fox_bwd")
    dproj, g_fox_b = _forget_cumsum_bwd(proj, fox_b128, drs, dcs, dproj, "forget_cumsum_bwd")
    gw_in_t = _mm_simple(dproj, n1, mode="tn", tm=640, tn=d, tk=tkw, out_dtype=BF16, name="mm_gw_in")
    sent = in_grad_ready(gw_in_t)
    def input_grads(i, acc, ins, outs):
        gx_ref, gmeta_ref, gg_ref, buf_ref, sem = outs
        dh, gg = _rms_bwd_tile(acc, ins[0][...], ins[1][...], ins[2][...])
        _accumulate(gg_ref, i, gg)
        buf_ref[...] = dh

        @pl.when(i == 0)
        def _():
            gmeta_ref[...] = dh[N_PAD:PREFIX, :]
            first = pltpu.make_async_copy(buf_ref.at[pl.ds(PREFIX, tm - PREFIX)], gx_ref.at[pl.ds(0, tm - PREFIX)], sem)
            first.start()
            first.wait()

        @pl.when(i > 0)
        def _():
            rows = pl.ds(pl.multiple_of(i * tm - PREFIX, PREFIX), tm)
            rest = pltpu.make_async_copy(buf_ref, gx_ref.at[rows], sem)
            rest.start()
            rest.wait()

    grad_x, g_meta, g_attn = _matmul_rows(
        [dproj], [pl.BlockSpec((tm, IN_PAD), lambda i: (i, 0))], [w_in_t], [resident((IN_PAD, d))],
        [h0, attn_g, dh1], [tile, row_vec, tile],
        [pl.BlockSpec(memory_space=pl.ANY), pl.BlockSpec((N_META, d), lambda i: (0, 0)), row_vec],
        [jax.ShapeDtypeStruct((seq, d), F32), jax.ShapeDtypeStruct((N_META, d), F32), vec], input_grads,
        mode="nn", steps=nq, name="mm_dn1_norm_bwd", after=sent,
        scratch=[pltpu.VMEM((tm, d), F32), pltpu.SemaphoreType.DMA(())])

    grads = dict(meta=g_meta, attn_g=g_attn, fox_b=g_fox_b, ret_g=g_ret,
                 ffn_g=g_ffn, conv_w=g_conv_w8, conv_b=g_conv_b, final_g=g_final)
    return loss_tile, grad_x, grads


def kernel(x, meta_tokens, attn_norm_g, w_in, fox_forget_b, ret_norm_g, w_out, ffn_norm_g, w_up, conv_w, conv_b, w_down, final_norm_g, loss_target, m_meta_tokens, m_attn_norm_g, m_w_in, m_fox_forget_b, m_ret_norm_g, m_w_out, m_ffn_norm_g, m_w_up, m_conv_w, m_conv_b, m_w_down, m_final_norm_g, v_meta_tokens, v_attn_norm_g, v_w_in, v_fox_forget_b, v_ret_norm_g, v_w_out, v_ffn_norm_g, v_w_up, v_conv_w, v_conv_b, v_w_down, v_final_norm_g):
    d = D_MODEL
    me = 4 * lax.axis_index("x") + 2 * lax.axis_index("y") + lax.axis_index("c")
    in_blk = IN_WIDTH // N_DEV
    in_blk_pad = 400
    up_blk = 2 * D_FF // N_DEV
    down_blk = D_FF // N_DEV
    cw_blk = D_FF // N_DEV

    w_in_loc = jnp.pad(w_in[0].T.astype(BF16), ((0, in_blk_pad - in_blk), (0, 0)))
    cw_loc = jnp.pad(conv_w[0], ((0, 5), (0, 384 - cw_blk)))
    g_meta, g_cw = _exchange([meta_tokens, cw_loc], ["gather"] * 2, "gather_small")
    first = _exchange_start([w_in_loc], ["gather"], "gather_in_start", after=g_meta)
    rest_loc = [(w_out[0] + first[-1][0:1, 0:1]).astype(BF16), w_up[0].T.astype(BF16), w_down[0].astype(BF16)]
    rest = _exchange_start(rest_loc, ["gather"] * 3, "gather_rest_start")
    meta_f = g_meta.transpose(1, 0, 2).reshape(N_META, d)
    conv_w8 = jnp.pad(g_cw[:, :3, :cw_blk].transpose(1, 0, 2).reshape(3, D_FF), ((0, 5), (0, 0)))
    pending = {}

    def first_weight(after):
        (g_in,) = _exchange_wait(first, ["gather"], after, "gather_in_wait")
        return _to_internal_rows(g_in[:, :in_blk].reshape(IN_WIDTH, d))

    def in_grad_ready(gw_in_t):
        blocks = _from_internal_rows(gw_in_t).reshape(N_DEV, in_blk, d)
        blocks = jnp.pad(blocks, ((0, 0), (0, in_blk_pad - in_blk), (0, 0)))
        pending["in"] = _exchange_start([blocks], ["scatter"], "grads_in_start")
        return pending["in"][-1][0:1, 0:1]

    def late_weights(after):
        g_out, g_up, g_down = _exchange_wait(rest, ["gather"] * 3, after, "gather_rest_wait")
        return g_out.reshape(d, d), g_up.reshape(2, D_FF, d), g_down.reshape(D_FF, d)

    def ffn_grads_ready(gw_down, gw_up_t):
        blocks = [gw_down.reshape(N_DEV, down_blk, d), gw_up_t.reshape(N_DEV, up_blk, d)]
        pending["ffn"] = _exchange_start(blocks, ["scatter"] * 2, "grads_ffn_start")
        return pending["ffn"][-1][0:1, 0:1]

    def out_grad_ready(gw_out):
        pending["out"] = _exchange_start([gw_out.reshape(N_DEV, d // N_DEV, d)], ["scatter"], "grads_out_start")
        return pending["out"][-1][0:1, 0:1]

    loss_tile, grad_x, gr = _local_step(
        x[0], loss_target[0], meta_f, attn_norm_g + rest[-1][0:1, 0:1], fox_forget_b, ret_norm_g, ffn_norm_g,
        conv_w8, conv_b, final_norm_g.reshape(1, d), first_weight, late_weights, ffn_grads_ready, out_grad_ready,
        in_grad_ready)

    r_down, r_up = _exchange_wait(pending["ffn"], ["scatter"] * 2, grad_x, "grads_ffn_wait")
    (r_out,) = _exchange_wait(pending["out"], ["scatter"], grad_x, "grads_out_wait")
    g_w_out = _sum_slots(r_out, "sum_w_out", d // N_DEV)
    g_w_up_t = _sum_slots(r_up, "sum_w_up", up_blk)
    g_w_down = _sum_slots(r_down, "sum_w_down", down_blk)
    as_t = lambda a: a[0].T
    from_t = lambda a: a.T[None]
    d_w_out, m_w_out_n, v_w_out_n = [a[None] for a in _adamw(w_out[0], g_w_out, m_w_out[0], v_w_out[0], "adamw_w_out", 128)]
    up_t = _adamw(as_t(w_up), g_w_up_t, as_t(m_w_up), as_t(v_w_up), "adamw_w_up", up_blk // 2)
    d_w_up, m_w_up_n, v_w_up_n = [from_t(a) for a in up_t]
    d_w_down, m_w_down_n, v_w_down_n = [a[None] for a in _adamw(w_down[0], g_w_down, m_w_down[0], v_w_down[0],
                                                                "adamw_w_down", down_blk)]

    small = [loss_tile, gr["attn_g"], gr["fox_b"], gr["ret_g"], gr["ffn_g"], gr["conv_b"], gr["final_g"],
             gr["meta"], gr["conv_w"]]
    r_small = _exchange(small, ["gather"] * len(small), "exchange_small", after=up_t[0])
    (loss_all, g_attn, g_fox_b128, g_ret, g_ffn, g_conv_b, g_final, g_meta_full, g_cw_full) = _sum_slots_small(
        r_small, "sum_small")
    loss = loss_all[0, 0]
    g_fox_b = g_fox_b128[:, :FOX_HEADS]
    g_meta_loc = lax.dynamic_slice(g_meta_full, (0, me * (d // N_DEV)), (N_META, d // N_DEV))
    g_cw_loc = lax.dynamic_slice(g_cw_full, (0, me * cw_blk), (3, cw_blk))

    (r_in,) = _exchange_wait(pending["in"], ["scatter"], r_small[0], "grads_in_wait")
    g_w_in_t = _sum_slots(r_in, "sum_w_in", in_blk_pad)[:in_blk]
    d_w_in, m_w_in_n, v_w_in_n = [from_t(a) for a in _adamw(as_t(w_in), g_w_in_t, as_t(m_w_in), as_t(v_w_in),
                                                            "adamw_w_in", in_blk)]
    g_w_in, g_w_up = g_w_in_t.T, g_w_up_t.T
    row = lambda a: a.reshape(1, d)
    sm_grads = [g_meta_loc, g_attn, g_fox_b, g_ret, g_ffn, g_cw_loc, g_conv_b, g_final]
    sm_w = [meta_tokens, attn_norm_g, fox_forget_b, ret_norm_g, ffn_norm_g, conv_w[0], conv_b, row(final_norm_g)]
    sm_m = [m_meta_tokens, m_attn_norm_g, m_fox_forget_b, m_ret_norm_g, m_ffn_norm_g, m_conv_w[0], m_conv_b,
            row(m_final_norm_g)]
    sm_v = [v_meta_tokens, v_attn_norm_g, v_fox_forget_b, v_ret_norm_g, v_ffn_norm_g, v_conv_w[0], v_conv_b,
            row(v_final_norm_g)]
    dl, ml, vl = [lst[:7] + [lst[7].reshape(d)] for lst in _adamw_small(sm_w, sm_grads, sm_m, sm_v, "adamw_small")]

    def by_weight(meta_, attn_, w_in_, fox_, ret_, w_out_, ffn_, w_up_, cw_, cb_, w_down_, final_):
        return (meta_, attn_, w_in_, fox_, ret_, w_out_, ffn_, w_up_, cw_[None], cb_, w_down_, final_)

    grads_out = by_weight(g_meta_loc, g_attn, g_w_in[None], g_fox_b, g_ret, g_w_out[None], g_ffn, g_w_up[None], g_cw_loc,
                          g_conv_b, g_w_down[None], g_final.reshape(d))
    delta_out = by_weight(dl[0], dl[1], d_w_in, dl[2], dl[3], d_w_out, dl[4], d_w_up, dl[5], dl[6], d_w_down, dl[7])
    m_out = by_weight(ml[0], ml[1], m_w_in_n, ml[2], ml[3], m_w_out_n, ml[4], m_w_up_n, ml[5], ml[6], m_w_down_n, ml[7])
    v_out = by_weight(vl[0], vl[1], v_w_in_n, vl[2], vl[3], v_w_out_n, vl[4], v_w_up_n, vl[5], vl[6], v_w_down_n, vl[7])
    return (loss, grad_x[None]) + grads_out + delta_out + m_out + v_out
```

```python
import numpy as np
import jax
import jax.numpy as jnp
from jax import lax
from jax.experimental import pallas as pl
from jax.experimental.pallas import tpu as pltpu

F32 = jnp.float32
BF16 = jnp.bfloat16

D_MODEL = 1024
N_META = 16
N_PAD = 112
PREFIX = 128
RET_HEADS = 4
RET_DK = 64
RET_DV = 128
FOX_HEADS = 8
FOX_DH = 64
D_FF = 2816
ROPE_BASE = 10000.0
EPS = 1e-6
NEG = -1e30
RET_QK = RET_HEADS * RET_DK
RET_V = RET_HEADS * RET_DV
FOX_W = FOX_HEADS * FOX_DH
IN_WIDTH = 2 * RET_QK + 2 * RET_V + 3 * FOX_W + FOX_HEADS
IN_PAD = 3200
FF_COL_BLOCK = (IN_WIDTH - FOX_HEADS) // 128
QK_SCALE = 0.125

ADAM_LR = 0.001
ADAM_B1 = 0.9
ADAM_B2 = 0.999
ADAM_EPS = 1e-08
ADAM_WD = 0.01
ADAM_STEP = 10

N_DEV = 8
LANE = 128
ROW_TILE = 128
TOK_TILE = 384

NN = (((1,), (0,)), ((), ()))
NT = (((1,), (1,)), ((), ()))
TN = (((0,), (0,)), ((), ()))


def _pcall(body, **kw):
    return pl.pallas_call(body, **kw)


def _params(*sem):
    return pltpu.CompilerParams(dimension_semantics=sem)


def _dot(a, b, dims=NN):
    return lax.dot_general(a, b, dims, preferred_element_type=F32)


def _sigmoid(x):
    return 0.5 * jnp.tanh(0.5 * x) + 0.5


def _matmul(a, b, *, mode, grid, a_spec, b_spec, o_spec, out_shape, name, add=None, add_spec=None, after=None):
    dims = {"nn": NN, "nt": NT, "tn": TN}[mode]
    nk = grid[2]
    has_add = add is not None
    a_list, b_list = (list(a), list(b)) if isinstance(a, (list, tuple)) else ([a], [b])
    a_specs, b_specs = (list(a_spec), list(b_spec)) if isinstance(a_spec, (list, tuple)) else ([a_spec], [b_spec])
    nt = len(a_list)
    n_in = 2 * nt + int(has_add) + int(after is not None)

    def body(*refs):
        a_refs, b_refs = refs[:nt], refs[nt:2 * nt]
        add_ref = refs[2 * nt] if has_add else None
        o_ref = refs[n_in]
        part = _dot(a_refs[0][...].astype(BF16), b_refs[0][...].astype(BF16), dims)
        for ar, br in zip(a_refs[1:], b_refs[1:]):
            part = part + _dot(ar[...].astype(BF16), br[...].astype(BF16), dims)

        def finish(acc):
            if has_add:
                acc = acc + add_ref[...]
            o_ref[...] = acc.astype(o_ref.dtype)

        if nk == 1:
            finish(part)
        else:
            acc_ref = refs[-1]
            k = pl.program_id(2)

            @pl.when(k == 0)
            def _():
                acc_ref[...] = part

            @pl.when(k > 0)
            def _():
                acc_ref[...] += part

            @pl.when(k == nk - 1)
            def _():
                finish(acc_ref[...])

    in_specs = a_specs + b_specs + ([add_spec] if has_add else [])
    args = tuple(a_list) + tuple(b_list) + ((add,) if has_add else ())
    if after is not None:
        in_specs, args = in_specs + [pl.BlockSpec(memory_space=pl.ANY)], args + (after,)
    scratch = [] if nk == 1 else [pltpu.VMEM(tuple(d for d in o_spec.block_shape if d is not None), F32)]
    return _pcall(
        body, name=name, grid=grid, in_specs=in_specs, out_specs=o_spec, out_shape=out_shape,
        scratch_shapes=scratch, compiler_params=_params("parallel", "parallel", "arbitrary"),
    )(*args)


def _mm_simple(a, b, *, mode, tm, tn, tk, out_dtype, name, add=None, after=None):
    if mode == "tn":
        K, M = a.shape
    else:
        M, K = a.shape
    N = b.shape[0] if mode == "nt" else b.shape[1]
    grid = (M // tm, N // tn, K // tk)
    resident = dict(pipeline_mode=pl.Buffered(1)) if (tn == N and tk == K) else {}
    a_spec = pl.BlockSpec((tk, tm), lambda i, j, k: (k, i)) if mode == "tn" else pl.BlockSpec((tm, tk), lambda i, j, k: (i, k))
    b_spec = (pl.BlockSpec((tn, tk), lambda i, j, k: (j, k), **resident) if mode == "nt"
              else pl.BlockSpec((tk, tn), lambda i, j, k: (k, j), **resident))
    o_spec = pl.BlockSpec((tm, tn), lambda i, j, k: (i, j))
    return _matmul(a, b, mode=mode, grid=grid, a_spec=a_spec, b_spec=b_spec, o_spec=o_spec,
                   out_shape=jax.ShapeDtypeStruct((M, N), out_dtype), name=name, add=add,
                   add_spec=o_spec if add is not None else None, after=after)


def _matmul_rows(a_list, a_specs, b_list, b_specs, extras, extra_specs, out_specs, out_shape, epilogue, *,
                 mode, steps, name, after=None, scratch=()):
    dims = {"nn": NN, "nt": NT}[mode]
    nt, ne = len(a_list), len(extras)
    n_in = 2 * nt + ne + int(after is not None)

    def body(*refs):
        acc = _dot(refs[0][...].astype(BF16), refs[nt][...].astype(BF16), dims)
        for k in range(1, nt):
            acc = acc + _dot(refs[k][...].astype(BF16), refs[nt + k][...].astype(BF16), dims)
        epilogue(pl.program_id(0), acc, refs[2 * nt:2 * nt + ne], refs[n_in:])

    in_specs = list(a_specs) + list(b_specs) + list(extra_specs)
    args = tuple(a_list) + tuple(b_list) + tuple(extras)
    if after is not None:
        in_specs, args = in_specs + [pl.BlockSpec(memory_space=pl.ANY)], args + (after,)
    return _pcall(body, name=name, grid=(steps,), in_specs=in_specs, out_specs=out_specs, out_shape=out_shape,
                  scratch_shapes=list(scratch), compiler_params=_params("arbitrary"))(*args)


def _rms_bwd_tile(dy, x, gain, dres):
    r = lax.rsqrt(jnp.mean(x * x, axis=-1, keepdims=True) + EPS)
    xhat = x * r
    u = dy * gain
    return dres + r * (u - xhat * jnp.mean(u * xhat, axis=-1, keepdims=True)), jnp.sum(dy * xhat, axis=0, keepdims=True)


def _loss_tile(i, x, tgt, gain):
    d = x.shape[-1]
    r = lax.rsqrt(jnp.mean(x * x, axis=-1, keepdims=True) + EPS)
    xhat = x * r
    counted = (i * TOK_TILE + lax.broadcasted_iota(jnp.int32, (TOK_TILE, 1), 0)) >= PREFIX
    err = jnp.where(counted, xhat * gain - tgt, 0.0)
    dy = err * (1.0 / d)
    u = dy * gain
    dh = r * (u - xhat * jnp.mean(u * xhat, axis=-1, keepdims=True))
    return 0.5 * jnp.sum(jnp.mean(err * err, axis=-1, keepdims=True)), dh, jnp.sum(dy * xhat, axis=0, keepdims=True)


def _accumulate(ref, i, part):
    @pl.when(i == 0)
    def _():
        ref[...] = part

    @pl.when(i > 0)
    def _():
        ref[...] += part


def _prep_norm(x, meta, gain, name):
    seq, d = x.shape
    t = seq + PREFIX

    def body(xa_ref, xb_ref, xc_ref, meta_ref, g_ref, h_ref, n_ref):
        i = pl.program_id(0)

        @pl.when(i == 0)
        def _():
            h_ref[0:N_PAD, :] = jnp.zeros((N_PAD, d), F32)
            h_ref[N_PAD:ROW_TILE, :] = meta_ref[...]

        @pl.when(i > 0)
        def _():
            h_ref[0:ROW_TILE, :] = xa_ref[...]

        h_ref[ROW_TILE:2 * ROW_TILE, :] = xb_ref[...]
        h_ref[2 * ROW_TILE:3 * ROW_TILE, :] = xc_ref[...]
        h = h_ref[...]
        r = lax.rsqrt(jnp.mean(h * h, axis=-1, keepdims=True) + EPS)
        n_ref[...] = (h * r * g_ref[...]).astype(BF16)

    return _pcall(
        body, name=name, grid=(t // TOK_TILE,),
        in_specs=_shifted_row_specs(d) + [pl.BlockSpec((N_META, d), lambda i: (0, 0)), pl.BlockSpec((1, d), lambda i: (0, 0))],
        out_specs=[pl.BlockSpec((TOK_TILE, d), lambda i: (i, 0)), pl.BlockSpec((TOK_TILE, d), lambda i: (i, 0))],
        out_shape=[jax.ShapeDtypeStruct((t, d), F32), jax.ShapeDtypeStruct((t, d), BF16)],
        compiler_params=_params("parallel"),
    )(x, x, x, meta, gain)


def _shifted_row_specs(d):
    blocks_per_tile = TOK_TILE // ROW_TILE
    return [pl.BlockSpec((ROW_TILE, d), lambda i, r=r: (jnp.maximum(blocks_per_tile * i + r, 0), 0)) for r in (-1, 0, 1)]


def _ret_consts(bk):
    gam = 1.0 - 2.0 ** (-5.0 - np.arange(RET_HEADS))
    n = np.arange(bk)
    same_or_earlier_chunk = (n[None, :] // 64) <= (n[:, None] // 64)
    w = gam[:, None, None] ** np.abs(n[:, None] - n[None, :])[None] * same_or_earlier_chunk[None]
    wq = gam[:, None] ** (n[None, :] + 1.0)
    wk = gam[:, None] ** (bk - 1.0 - n[None, :])
    mask = (np.arange(RET_QK)[None, :] // RET_DK) == np.arange(RET_HEADS)[:, None]
    return (jnp.asarray(w, F32), jnp.asarray(wq[:, :, None], F32), jnp.asarray(wk[:, :, None], F32),
            jnp.asarray(mask[:, None, :], F32), [float(g ** bk) for g in gam])


def _rope_tables(t):
    half = RET_DK // 2
    inv = 1.0 / (ROPE_BASE ** (jnp.arange(half, dtype=F32) / half))
    ang = jnp.arange(t).astype(F32)[:, None] * inv[None, :]
    cos, sin = jnp.cos(ang), jnp.sin(ang)
    return (jnp.tile(jnp.concatenate([cos, cos], axis=1), (1, RET_HEADS)),
            jnp.tile(jnp.concatenate([-sin, sin], axis=1), (1, RET_HEADS)))


def _swap_halves(x):
    outs = []
    for s in range(x.shape[1] // LANE):
        xs = x[:, LANE * s:LANE * (s + 1)]
        lane = lax.broadcasted_iota(jnp.int32, xs.shape, 1)
        outs.append(jnp.where((lane & 32) == 0, pltpu.roll(xs, LANE - 32, axis=1), pltpu.roll(xs, 32, axis=1)))
    return outs[0] if len(outs) == 1 else jnp.concatenate(outs, axis=1)


def _rope(x, cos, sin_signed):
    return x * cos + _swap_halves(x) * sin_signed


def _rope_t(dx, cos, sin_signed):
    return dx * cos + _swap_halves(dx * sin_signed)


def _ret_fwd(proj, cos, sin, gain, name):
    t = proj.shape[0]
    bk = TOK_TILE
    nb = t // bk
    w, wq, wk, mask, g_blk = _ret_consts(bk)

    def body(q_ref, k_ref, v_ref, rg_ref, cos_ref, sin_ref, w_ref, wq_ref, wk_ref, mask_ref, gain_ref,
             opre_ref, og_ref, st_ref, r_ref):
        i = pl.program_id(0)

        @pl.when(i == 0)
        def _():
            r_ref[...] = jnp.zeros_like(r_ref)

        c, s = cos_ref[...], sin_ref[...]
        valid = ((i * bk + lax.broadcasted_iota(jnp.int32, (bk, 1), 0)) >= N_PAD).astype(F32)
        qr = _rope(q_ref[...], c, s)
        kr = _rope(k_ref[...], c, s) * QK_SCALE * valid
        kb = kr.astype(BF16)
        for h in range(RET_HEADS):
            hm = mask_ref[h]
            cols = slice(RET_DV * h, RET_DV * (h + 1))
            vh = v_ref[:, cols].astype(BF16)
            r_prev = r_ref[h]
            st_ref[0, h] = r_prev
            sm = _dot((qr * hm).astype(BF16), kb, NT) * w_ref[h]
            o = _dot(sm.astype(BF16), vh) + _dot((qr * (hm * wq_ref[h])).astype(BF16), r_prev.astype(BF16))
            r_ref[h] = g_blk[h] * r_prev + _dot((kr * wk_ref[h]).astype(BF16), vh, TN)
            opre_ref[:, cols] = o
            rstd = lax.rsqrt(jnp.mean(o * o, axis=-1, keepdims=True) + EPS)
            rg = rg_ref[:, cols]
            og_ref[:, cols] = (o * rstd * gain_ref[:, cols] * (rg * _sigmoid(rg))).astype(BF16)

    full = lambda shape: pl.BlockSpec(shape, lambda i: (0,) * len(shape))
    return _pcall(
        body, name=name, grid=(nb,),
        in_specs=[pl.BlockSpec((bk, RET_QK), lambda i: (i, 0)), pl.BlockSpec((bk, RET_QK), lambda i: (i, 1)),
                  pl.BlockSpec((bk, RET_V), lambda i: (i, 1)), pl.BlockSpec((bk, RET_V), lambda i: (i, 2)),
                  pl.BlockSpec((bk, RET_QK), lambda i: (i, 0)), pl.BlockSpec((bk, RET_QK), lambda i: (i, 0)),
                  full((RET_HEADS, bk, bk)), full((RET_HEADS, bk, 1)), full((RET_HEADS, bk, 1)),
                  full((RET_HEADS, 1, RET_QK)), full((1, RET_V))],
        out_specs=[pl.BlockSpec((bk, RET_V), lambda i: (i, 0)), pl.BlockSpec((bk, RET_V), lambda i: (i, 0)),
                   pl.BlockSpec((1, RET_HEADS, RET_QK, RET_DV), lambda i: (i, 0, 0, 0))],
        out_shape=[jax.ShapeDtypeStruct((t, RET_V), F32), jax.ShapeDtypeStruct((t, RET_V + FOX_W), BF16),
                   jax.ShapeDtypeStruct((nb, RET_HEADS, RET_QK, RET_DV), F32)],
        scratch_shapes=[pltpu.VMEM((RET_HEADS, RET_QK, RET_DV), F32)],
        compiler_params=_params("arbitrary"),
    )(proj, proj, proj, proj, cos, sin, w, wq, wk, mask, gain)


def _ret_bwd(proj, cos, sin, gain, dmixed, opre, states, name):
    t = proj.shape[0]
    bk = TOK_TILE
    nb = t // bk
    w, wq, wk, mask, g_blk = _ret_consts(bk)
    v0, g0 = 2 * RET_QK, 2 * RET_QK + RET_V

    def body(q_ref, k_ref, v_ref, rg_ref, cos_ref, sin_ref, w_ref, wq_ref, wk_ref, mask_ref, gain_ref,
             dog_ref, opre_ref, st_ref, dp_ref, gg_ref, dr_ref):
        step = pl.program_id(0)
        i = nb - 1 - step

        @pl.when(step == 0)
        def _():
            dr_ref[...] = jnp.zeros_like(dr_ref)
            gg_ref[...] = jnp.zeros_like(gg_ref)

        c, s = cos_ref[...], sin_ref[...]
        valid = ((i * bk + lax.broadcasted_iota(jnp.int32, (bk, 1), 0)) >= N_PAD).astype(F32)
        qr = _rope(q_ref[...], c, s)
        kr = _rope(k_ref[...], c, s) * QK_SCALE * valid
        kb = kr.astype(BF16)
        dqr = jnp.zeros((bk, RET_QK), F32)
        dkr = jnp.zeros((bk, RET_QK), F32)
        for h in range(RET_HEADS):
            hm = mask_ref[h]
            cols = slice(RET_DV * h, RET_DV * (h + 1))
            vh = v_ref[:, cols].astype(BF16)
            o = opre_ref[:, cols]
            rstd = lax.rsqrt(jnp.mean(o * o, axis=-1, keepdims=True) + EPS)
            xhat = o * rstd
            rg = rg_ref[:, cols]
            sg = _sigmoid(rg)
            gate = rg * sg
            gn = gain_ref[:, cols]
            dog = dog_ref[:, cols]
            dp_ref[:, g0 + RET_DV * h:g0 + RET_DV * (h + 1)] = (
                dog * xhat * gn * (sg * (1.0 + rg * (1.0 - sg)))).astype(BF16)
            gg_ref[:, cols] += jnp.sum(dog * xhat * gate, axis=0, keepdims=True)
            dxh = dog * gn * gate
            do = (rstd * (dxh - xhat * jnp.mean(dxh * xhat, axis=-1, keepdims=True))).astype(BF16)
            qm = (qr * hm).astype(BF16)
            qw = (qr * (hm * wq_ref[h])).astype(BF16)
            kw = (kr * wk_ref[h]).astype(BF16)
            wh = w_ref[h]
            sm = (_dot(qm, kb, NT) * wh).astype(BF16)
            ds = (_dot(do, vh, NT) * wh).astype(BF16)
            dr = dr_ref[h]
            drb = dr.astype(BF16)
            dp_ref[:, v0 + RET_DV * h:v0 + RET_DV * (h + 1)] = (_dot(sm, do, TN) + _dot(kw, drb)).astype(BF16)
            dqr = dqr + _dot(ds, kb) * hm + _dot(do, st_ref[0, h].astype(BF16), NT) * (hm * wq_ref[h])
            dkr = dkr + _dot(ds, qm, TN) + _dot(vh, drb, NT) * wk_ref[h]
            dr_ref[h] = g_blk[h] * dr + _dot(qw, do, TN)
        dp_ref[:, 0:RET_QK] = _rope_t(dqr, c, s).astype(BF16)
        dp_ref[:, RET_QK:2 * RET_QK] = _rope_t(dkr * (QK_SCALE * valid), c, s).astype(BF16)

    full = lambda shape: pl.BlockSpec(shape, lambda i: (0,) * len(shape))
    rev = lambda col: (lambda i: (nb - 1 - i, col))
    return _pcall(
        body, name=name, grid=(nb,),
        in_specs=[pl.BlockSpec((bk, RET_QK), rev(0)), pl.BlockSpec((bk, RET_QK), rev(1)),
                  pl.BlockSpec((bk, RET_V), rev(1)), pl.BlockSpec((bk, RET_V), rev(2)),
                  pl.BlockSpec((bk, RET_QK), rev(0)), pl.BlockSpec((bk, RET_QK), rev(0)),
                  full((RET_HEADS, bk, bk)), full((RET_HEADS, bk, 1)), full((RET_HEADS, bk, 1)),
                  full((RET_HEADS, 1, RET_QK)), full((1, RET_V)),
                  pl.BlockSpec((bk, RET_V), rev(0)), pl.BlockSpec((bk, RET_V), rev(0)),
                  pl.BlockSpec((1, RET_HEADS, RET_QK, RET_DV), lambda i: (nb - 1 - i, 0, 0, 0))],
        out_specs=[pl.BlockSpec((bk, g0 + RET_V), rev(0)), pl.BlockSpec((1, RET_V), lambda i: (0, 0))],
        out_shape=[jax.ShapeDtypeStruct((t, IN_PAD), BF16), jax.ShapeDtypeStruct((1, RET_V), F32)],
        scratch_shapes=[pltpu.VMEM((RET_HEADS, RET_QK, RET_DV), F32)],
        compiler_params=_params("arbitrary"),
    )(proj, proj, proj, proj, cos, sin, w, wq, wk, mask, gain, dmixed, opre, states)


def _forget_cumsum(proj, bias, name):
    t = proj.shape[0]
    rt = TOK_TILE
    nb = t // rt
    tril = jnp.asarray(np.tril(np.ones((rt, rt))), F32)

    def body(z_ref, b_ref, tril_ref, c_ref, carry_ref):
        i = pl.program_id(0)

        @pl.when(i == 0)
        def _():
            carry_ref[...] = jnp.zeros_like(carry_ref)

        z = z_ref[...] + b_ref[...]
        logf = jnp.minimum(z, 0.0) - jnp.log(1.0 + jnp.exp(-jnp.abs(z)))
        c = lax.dot_general(tril_ref[...], logf, NN, precision=lax.Precision.HIGHEST,
                            preferred_element_type=F32) + carry_ref[...]
        c_ref[...] = c
        carry_ref[...] = c[rt - 1:rt, :]

    return _pcall(
        body, name=name, grid=(nb,),
        in_specs=[pl.BlockSpec((rt, LANE), lambda i: (i, FF_COL_BLOCK)), pl.BlockSpec((1, LANE), lambda i: (0, 0)),
                  pl.BlockSpec((rt, rt), lambda i: (0, 0))],
        out_specs=pl.BlockSpec((rt, LANE), lambda i: (i, 0)),
        out_shape=jax.ShapeDtypeStruct((t, LANE), F32),
        scratch_shapes=[pltpu.VMEM((1, LANE), F32)],
        compiler_params=_params("arbitrary"),
    )(proj, bias, tril)


def _forget_cumsum_bwd(proj, bias, drs, dcs, dproj, name):
    t = proj.shape[0]
    rt = TOK_TILE
    nb = t // rt
    triu = jnp.asarray(np.triu(np.ones((rt, rt))), F32)

    def body(z_ref, b_ref, triu_ref, drs_ref, dcs_ref, dproj_in, dz_ref, gb_ref, carry_ref):
        step = pl.program_id(0)

        @pl.when(step == 0)
        def _():
            carry_ref[...] = jnp.zeros_like(carry_ref)
            gb_ref[...] = jnp.zeros_like(gb_ref)

        dlogf = lax.dot_general(triu_ref[...], drs_ref[...] - dcs_ref[...], NN, precision=lax.Precision.HIGHEST,
                                preferred_element_type=F32) + carry_ref[...]
        carry_ref[...] = dlogf[0:1, :]
        z = z_ref[...] + b_ref[...]
        is_head = lax.broadcasted_iota(jnp.int32, (rt, LANE), 1) < FOX_HEADS
        dz = jnp.where(is_head, dlogf / (1.0 + jnp.exp(z)), 0.0)
        dz_ref[...] = dz.astype(BF16)
        gb_ref[...] += jnp.sum(dz, axis=0, keepdims=True)

    return _pcall(
        body, name=name, grid=(nb,),
        in_specs=[pl.BlockSpec((rt, LANE), lambda i: (nb - 1 - i, FF_COL_BLOCK)),
                  pl.BlockSpec((1, LANE), lambda i: (0, 0)),
                  pl.BlockSpec((rt, rt), lambda i: (0, 0)),
                  pl.BlockSpec((rt, LANE), lambda i: (nb - 1 - i, 0)),
                  pl.BlockSpec((rt, LANE), lambda i: (nb - 1 - i, 0)),
                  pl.BlockSpec(memory_space=pl.ANY)],
        out_specs=[pl.BlockSpec((rt, LANE), lambda i: (nb - 1 - i, FF_COL_BLOCK)),
                   pl.BlockSpec((1, LANE), lambda i: (0, 0))],
        out_shape=[jax.ShapeDtypeStruct(dproj.shape, BF16), jax.ShapeDtypeStruct((1, LANE), F32)],
        input_output_aliases={5: 0},
        scratch_shapes=[pltpu.VMEM((1, LANE), F32)],
        compiler_params=_params("arbitrary"),
    )(proj, bias, triu, drs, dcs, dproj)


FOX_PAIRS = FOX_HEADS // 2
L_ONE_Q = FOX_DH
L_ONE_K = FOX_DH + 3
L_LSE = FOX_DH + 4


def _split3(x):
    hi = x.astype(BF16).astype(F32)
    r = x - hi
    mid = r.astype(BF16).astype(F32)
    return hi, mid, r - mid


def _head_to_low(slab, e):
    return slab if e == 0 else pltpu.roll(slab, FOX_DH, axis=1)


def _pair(a, b, low):
    return jnp.where(low, a, pltpu.roll(b, FOX_DH, axis=1))


def _fox_prep(proj, c, name):
    t = proj.shape[0]
    tq = TOK_TILE

    def body(p_ref, c_ref, qa_ref, ka_ref, va_ref):
        i = pl.program_id(0)
        lane = lax.broadcasted_iota(jnp.int32, (tq, LANE), 1)
        low = lane < FOX_DH
        live = (i * tq + lax.broadcasted_iota(jnp.int32, (tq, 1), 0)) >= N_PAD
        q_tail = jnp.where(lane < L_ONE_Q + 3, 1.0, 0.0)
        k_ones = (lane >= L_ONE_K) & (lane < L_ONE_K + 4)
        v_tail = jnp.where(lane < FOX_DH + 2, 1.0, 0.0)
        for pair in range(FOX_PAIRS):
            base = 3 * LANE * pair
            for e in range(2):
                h = 2 * pair + e
                q = _head_to_low(p_ref[:, base:base + LANE], e)
                k = _head_to_low(p_ref[:, base + LANE:base + 2 * LANE], e)
                v = _head_to_low(p_ref[:, base + 2 * LANE:base + 3 * LANE], e)
                hi, mid, lo = _split3(jnp.where(live, -c_ref[:, h:h + 1], NEG))
                ka = jnp.where(low, k, jnp.where(k_ones, 1.0, 0.0))
                ka = jnp.where(lane == L_ONE_Q, hi, jnp.where(lane == L_ONE_Q + 1, mid, jnp.where(lane == L_ONE_Q + 2, lo, ka)))
                qa_ref[h] = jnp.where(low, q * QK_SCALE, q_tail).astype(BF16)
                ka_ref[h] = ka.astype(BF16)
                va_ref[h] = jnp.where(low, v, v_tail).astype(BF16)

    out = jax.ShapeDtypeStruct((FOX_HEADS, t, LANE), BF16)
    ospec = pl.BlockSpec((FOX_HEADS, tq, LANE), lambda i: (0, i, 0))
    return _pcall(
        body, name=name, grid=(t // tq,),
        in_specs=[pl.BlockSpec((tq, 3 * FOX_W), lambda i: (i, 1)), pl.BlockSpec((tq, LANE), lambda i: (i, 0))],
        out_specs=[ospec, ospec, ospec], out_shape=[out, out, out],
        compiler_params=_params("parallel"),
    )(proj, c)


STEP_PAIRS = 2
STEP_HEADS = 2 * STEP_PAIRS
FOX_GROUPS = FOX_PAIRS // STEP_PAIRS
FWD_PAIRS = 4
FWD_HEADS = 2 * FWD_PAIRS
FWD_GROUPS = FOX_PAIRS // FWD_PAIRS


def _blockdiag(a, b):
    z = jnp.zeros_like(a)
    return jnp.concatenate([jnp.concatenate([a, z], axis=1), jnp.concatenate([z, b], axis=1)], axis=0)


def _fox_fwd(qa, ka, va, mixed, name):
    nh, nq, tq, _ = qa.shape
    t = nq * tq

    def body(qa_ref, ka_ref, va_ref, mixed_in, mixed_ref, o_ref, lse_ref):
        i = pl.program_id(1)
        lane = lax.broadcasted_iota(jnp.int32, (tq, LANE), 1)
        causal = lax.broadcasted_iota(jnp.int32, (tq, tq), 1) <= lax.broadcasted_iota(jnp.int32, (tq, tq), 0)
        qps = [jnp.concatenate([qa_ref[2 * c], qa_ref[2 * c + 1]], axis=1) for c in range(FWD_PAIRS)]

        def logits(j):
            return [_dot(qps[c], _blockdiag(ka_ref[2 * c, j], ka_ref[2 * c + 1, j]), NT) for c in range(FWD_PAIRS)]

        def update(j, scores, carry, diagonal):
            new = []
            for c in range(FWD_PAIRS):
                ms, acc = carry[c]
                ps, ms_new, alphas = [], [], []
                for e in range(2):
                    s = scores[c][:, e * tq:(e + 1) * tq]
                    if diagonal:
                        s = jnp.where(causal, s, NEG)
                    m_new = jnp.maximum(ms[e], jnp.max(s, axis=-1, keepdims=True))
                    ps.append(jnp.exp(s - m_new).astype(BF16))
                    ms_new.append(m_new)
                    alphas.append(jnp.broadcast_to(jnp.exp(ms[e] - m_new), (tq, LANE)))
                pv = _dot(jnp.concatenate(ps, axis=1), _blockdiag(va_ref[2 * c, j], va_ref[2 * c + 1, j]))
                new.append((tuple(ms_new), jnp.concatenate(alphas, axis=1) * acc + pv))
            return tuple(new)

        m0 = jnp.full((tq, 1), NEG, F32)
        init = tuple(((m0, m0), jnp.zeros((tq, 2 * LANE), F32)) for _ in range(FWD_PAIRS))
        carry = lax.fori_loop(0, i, lambda j, cr: update(j, logits(j), cr, False), init)
        o_pairs = []
        lse = jnp.zeros((tq, LANE), F32)
        for c, (ms, acc) in enumerate(update(i, logits(i), carry, True)):
            outs = []
            for e in range(2):
                half = acc[:, e * LANE:(e + 1) * LANE]
                l = half[:, FOX_DH:FOX_DH + 1]
                outs.append(half / l)
                lse = jnp.where(lane == 2 * c + e, ms[e] + jnp.log(l), lse)
            o_pairs.append(_pair(outs[0], outs[1], lane < FOX_DH))
        o_all = jnp.concatenate(o_pairs, axis=1)
        mixed_ref[...] = o_all.astype(BF16)
        o_ref[...] = o_all
        lse_ref[...] = lse

    width = FWD_PAIRS * LANE
    whole = pl.BlockSpec((FWD_HEADS, nq, tq, LANE), lambda g, i: (g, 0, 0, 0), pipeline_mode=pl.Buffered(1))
    return _pcall(
        body, name=name, grid=(FWD_GROUPS, nq),
        in_specs=[pl.BlockSpec((FWD_HEADS, None, tq, LANE), lambda g, i: (g, i, 0, 0)), whole, whole,
                  pl.BlockSpec(memory_space=pl.ANY)],
        out_specs=[pl.BlockSpec((tq, width), lambda g, i: (i, RET_V // width + g)),
                   pl.BlockSpec((tq, width), lambda g, i: (i, g)),
                   pl.BlockSpec((None, tq, LANE), lambda g, i: (g, i, 0))],
        out_shape=[jax.ShapeDtypeStruct(mixed.shape, BF16), jax.ShapeDtypeStruct((t, FOX_W), F32),
                   jax.ShapeDtypeStruct((FWD_GROUPS, t, LANE), F32)],
        input_output_aliases={3: 0},
        compiler_params=_params("parallel", "parallel"),
    )(qa, ka, va, mixed)


def _fox_prep_bwd(dmixed, o_fox, lse, qa, name):
    t = dmixed.shape[0]
    tq = TOK_TILE

    def body(dm_ref, o_ref, lse_ref, qa_ref, qab_ref, doa_ref):
        i = pl.program_id(0)
        lane = lax.broadcasted_iota(jnp.int32, (tq, LANE), 1)
        low = lane < FOX_DH
        live = (i * tq + lax.broadcasted_iota(jnp.int32, (tq, 1), 0)) >= N_PAD
        for pair in range(FOX_PAIRS):
            cols = slice(LANE * pair, LANE * (pair + 1))
            d_slab = dm_ref[:, cols]
            prod = d_slab * o_ref[:, cols]
            for e in range(2):
                h = 2 * pair + e
                nd = -jnp.sum(jnp.where(low, _head_to_low(prod, e), 0.0), axis=-1, keepdims=True)
                nd_hi = nd.astype(BF16).astype(F32)
                doa = jnp.where(low, _head_to_low(d_slab, e), 0.0)
                doa = jnp.where(lane == FOX_DH, nd_hi, jnp.where(lane == FOX_DH + 1, nd - nd_hi, doa))
                doa_ref[h] = doa.astype(BF16)
                lse_h = lse_ref[h // FWD_HEADS][:, h % FWD_HEADS:h % FWD_HEADS + 1]
                hi, mid, lo = _split3(jnp.where(live, -lse_h, 0.0))
                qab = qa_ref[h].astype(F32)
                qab = jnp.where(lane == L_LSE, hi, jnp.where(lane == L_LSE + 1, mid, jnp.where(lane == L_LSE + 2, lo, qab)))
                qab_ref[h] = qab.astype(BF16)

    out = jax.ShapeDtypeStruct((FOX_HEADS, t, LANE), BF16)
    hspec = pl.BlockSpec((FOX_HEADS, tq, LANE), lambda i: (0, i, 0))
    return _pcall(
        body, name=name, grid=(t // tq,),
        in_specs=[pl.BlockSpec((tq, FOX_W), lambda i: (i, 1)), pl.BlockSpec((tq, FOX_W), lambda i: (i, 0)),
                  pl.BlockSpec((FWD_GROUPS, tq, LANE), lambda i: (0, i, 0)), hspec],
        out_specs=[hspec, hspec], out_shape=[out, out],
        compiler_params=_params("parallel"),
    )(dmixed, o_fox, lse, qa)


def _fox_bwd(qab, doa, ka, va, dproj, name):
    nh, nq, tq, _ = qab.shape
    t = nq * tq
    slab = 3 * LANE * STEP_PAIRS
    group0 = (2 * RET_QK + 2 * RET_V) // slab

    def body(qab_ref, doa_ref, ka_ref, va_ref, dproj_in, dp_ref, drs_ref, dcs_ref, dq_ref):
        g, j = pl.program_id(0), pl.program_id(1)

        @pl.when((g == 0) & (j == 0))
        def _():
            drs_ref[...] = jnp.zeros_like(drs_ref)
            dcs_ref[...] = jnp.zeros_like(dcs_ref)

        @pl.when(j == 0)
        def _():
            dq_ref[...] = jnp.zeros_like(dq_ref)

        lane = lax.broadcasted_iota(jnp.int32, (tq, LANE), 1)
        low = lane < FOX_DH
        key_le_query = lax.broadcasted_iota(jnp.int32, (tq, tq), 0) <= lax.broadcasted_iota(jnp.int32, (tq, tq), 1)

        def by_head(c, a, b, col):
            h = STEP_HEADS * g + 2 * c
            return jnp.where(lane == h, a[:, col:col + 1], jnp.where(lane == h + 1, b[:, col:col + 1], 0.0))

        kbs = [ka_ref[h] for h in range(STEP_HEADS)]
        vbs = [va_ref[h] for h in range(STEP_HEADS)]

        def step(i, carry, diagonal):
            qbs = [qab_ref[h, i] for h in range(STEP_HEADS)]
            dobs = [doa_ref[h, i] for h in range(STEP_HEADS)]
            st = [_dot(kbs[h], qbs[h], NT) for h in range(STEP_HEADS)]
            dpt = [_dot(vbs[h], dobs[h], NT) for h in range(STEP_HEADS)]
            new = []
            for h in range(STEP_HEADS):
                p = jnp.exp(st[h])
                if diagonal:
                    p = jnp.where(key_le_query, p, 0.0)
                ds = (p * dpt[h]).astype(BF16)
                dq_ref[h, i] += _dot(ds, kbs[h], TN)
                dk, dv = carry[h]
                new.append((dk + _dot(ds, qbs[h]), dv + _dot(p.astype(BF16), dobs[h])))
            return tuple(new)

        zero = jnp.zeros((tq, LANE), F32)
        carry = step(j, tuple((zero, zero) for _ in range(STEP_HEADS)), True)
        carry = lax.fori_loop(j + 1, nq, lambda i, cr: step(i, cr, False), carry)
        rows = pl.ds(pl.multiple_of(j * tq, tq), tq)
        for c in range(STEP_PAIRS):
            (dka, dva), (dkb, dvb) = carry[2 * c], carry[2 * c + 1]
            c0 = 3 * LANE * c
            dp_ref[rows, c0 + LANE:c0 + 2 * LANE] = _pair(dka, dkb, low).astype(BF16)
            dp_ref[rows, c0 + 2 * LANE:c0 + 3 * LANE] = _pair(dva, dvb, low).astype(BF16)
            dcs_ref[rows, :] += by_head(c, dka, dkb, L_ONE_Q)

        @pl.when(j == nq - 1)
        def _():
            for c in range(STEP_PAIRS):
                for blk in range(nq):
                    r = slice(blk * tq, (blk + 1) * tq)
                    a, b = dq_ref[2 * c, blk], dq_ref[2 * c + 1, blk]
                    dp_ref[r, 3 * LANE * c:3 * LANE * c + LANE] = (_pair(a, b, low) * QK_SCALE).astype(BF16)
                    drs_ref[r, :] += by_head(c, a, b, L_ONE_K)

    whole = pl.BlockSpec((STEP_HEADS, nq, tq, LANE), lambda g, j: (g, 0, 0, 0), pipeline_mode=pl.Buffered(1))
    blk = pl.BlockSpec((STEP_HEADS, None, tq, LANE), lambda g, j: (g, j, 0, 0))
    sums = pl.BlockSpec((t, LANE), lambda g, j: (0, 0), pipeline_mode=pl.Buffered(1))
    return _pcall(
        body, name=name, grid=(FOX_GROUPS, nq),
        in_specs=[whole, whole, blk, blk, pl.BlockSpec(memory_space=pl.ANY)],
        out_specs=[pl.BlockSpec((t, slab), lambda g, j: (0, group0 + g)), sums, sums],
        out_shape=[jax.ShapeDtypeStruct(dproj.shape, BF16), jax.ShapeDtypeStruct((t, LANE), F32),
                   jax.ShapeDtypeStruct((t, LANE), F32)],
        input_output_aliases={4: 0},
        scratch_shapes=[pltpu.VMEM((STEP_HEADS, nq, tq, LANE), F32)],
        compiler_params=_params("arbitrary", "arbitrary"),
    )(qab, doa, ka, va, dproj)


HALO = 8


def _rows_ext(ref, r0, rows, t, before, after):
    lo, hi = r0 - before, r0 + rows + after
    width = ref.shape[-1]
    parts = []
    if lo < 0:
        parts.append(jnp.zeros((-lo, width), F32))
    parts.append(ref[max(lo, 0):min(hi, t), :].astype(F32))
    if hi > t:
        parts.append(jnp.zeros((hi - t, width), F32))
    return parts[0] if len(parts) == 1 else jnp.concatenate(parts, axis=0)


def _conv_taps(a_ext, r0_ext, cw_ref, cb_ref):
    n = a_ext.shape[0]
    if r0_ext < N_PAD:
        row = r0_ext + lax.broadcasted_iota(jnp.int32, (n, 1), 0)
        a_ext = jnp.where(row >= N_PAD, a_ext, 0.0)
    a1 = pltpu.roll(a_ext, 1, axis=0)
    a2 = pltpu.roll(a_ext, 2, axis=0)
    acc = cb_ref[...] + a2 * cw_ref[0:1, :] + a1 * cw_ref[1:2, :] + a_ext * cw_ref[2:3, :]
    return a_ext, a1, a2, acc


FF_COLS = 256


def _up_conv_fwd(n2, w_up_t, conv_w8, conv_b, name):
    t, d = n2.shape
    f = w_up_t.shape[1]
    rows = TOK_TILE
    starts = list(range(0, t, rows))

    def body(n_ref, wa_ref, wb_ref, cw_ref, cb_ref, up_ref, g_ref, ab_ref):
        wa, wb = wa_ref[...], wb_ref[...]

        def project(r0):
            n_rows = n_ref[r0:r0 + rows, :]
            for half, w in enumerate((wa, wb)):
                part = _dot(n_rows, w, NT)
                ab_ref[half, r0:r0 + rows, :] = part
                up_ref[half, r0:r0 + rows, :] = part.astype(BF16)

        def activate(r0):
            a_ext = _rows_ext(ab_ref.at[0], r0, rows, t, HALO, 0)
            _, _, _, acc = _conv_taps(a_ext, r0 - HALO, cw_ref, cb_ref)
            acc = acc[HALO:, :]
            g_ref[r0:r0 + rows, :] = (acc * _sigmoid(acc) * ab_ref[1, r0:r0 + rows, :]).astype(BF16)

        project(starts[0])
        for r0, r_next in zip(starts, starts[1:] + [None]):
            if r_next is not None:
                project(r_next)
            activate(r0)

    return _pcall(
        body, name=name, grid=(f // FF_COLS,),
        in_specs=[pl.BlockSpec((t, d), lambda j: (0, 0), pipeline_mode=pl.Buffered(1)),
                  pl.BlockSpec((None, FF_COLS, d), lambda j: (0, j, 0)), pl.BlockSpec((None, FF_COLS, d), lambda j: (1, j, 0)),
                  pl.BlockSpec((8, FF_COLS), lambda j: (0, j)), pl.BlockSpec((1, FF_COLS), lambda j: (0, j))],
        out_specs=[pl.BlockSpec((2, t, FF_COLS), lambda j: (0, 0, j)), pl.BlockSpec((t, FF_COLS), lambda j: (0, j))],
        out_shape=[jax.ShapeDtypeStruct((2, t, f), BF16), jax.ShapeDtypeStruct((t, f), BF16)],
        scratch_shapes=[pltpu.VMEM((2, t, FF_COLS), F32)],
        compiler_params=_params("parallel"),
    )(n2, w_up_t, w_up_t, conv_w8, conv_b)


def _dg_conv_bwd(up, conv_w8, conv_b, dh2, w_down, name):
    _, t, f = up.shape
    d = dh2.shape[1]
    rows = TOK_TILE
    starts = list(range(0, t, rows))

    def body(a_ref, b_ref, cw_ref, cb_ref, dh_ref, wd_ref, dup_ref, gcw_ref, gcb_ref, dg_ref):
        wd = wd_ref[...]

        def project(r0):
            dg_ref[r0:r0 + rows, :] = _dot(dh_ref[r0:r0 + rows, :], wd, NT)

        gw = [jnp.zeros((1, FF_COLS), F32) for _ in range(3)]
        gb = jnp.zeros((1, FF_COLS), F32)
        project(starts[0])
        for r0, r_next in zip(starts, starts[1:] + [None]):
            if r_next is not None:
                project(r_next)
            a_ext = _rows_ext(a_ref, r0, rows, t, HALO, HALO)
            b_ext = _rows_ext(b_ref, r0, rows, t, HALO, HALO)
            dg_ext = _rows_ext(dg_ref, r0, rows, t, HALO, HALO)
            a0, a1, a2, acc = _conv_taps(a_ext, r0 - HALO, cw_ref, cb_ref)
            sg = _sigmoid(acc)
            dacc = dg_ext * b_ext * (sg * (1.0 + acc * (1.0 - sg)))
            n = dacc.shape[0]
            da = (dacc * cw_ref[2:3, :] + pltpu.roll(dacc, n - 1, axis=0) * cw_ref[1:2, :]
                  + pltpu.roll(dacc, n - 2, axis=0) * cw_ref[0:1, :])
            core = slice(HALO, HALO + rows)
            da = da[core, :]
            if r0 < N_PAD:
                row = r0 + lax.broadcasted_iota(jnp.int32, (rows, 1), 0)
                da = jnp.where(row >= N_PAD, da, 0.0)
            dup_ref[0, r0:r0 + rows, :] = da.astype(BF16)
            dup_ref[1, r0:r0 + rows, :] = (dg_ext * acc * sg)[core, :].astype(BF16)
            dacc_c = dacc[core, :]
            gw[0] = gw[0] + jnp.sum(dacc_c * a2[core, :], axis=0, keepdims=True)
            gw[1] = gw[1] + jnp.sum(dacc_c * a1[core, :], axis=0, keepdims=True)
            gw[2] = gw[2] + jnp.sum(dacc_c * a0[core, :], axis=0, keepdims=True)
            gb = gb + jnp.sum(dacc_c, axis=0, keepdims=True)
        gcw_ref[...] = jnp.zeros((8, FF_COLS), F32)
        for tap in range(3):
            gcw_ref[tap:tap + 1, :] = gw[tap]
        gcb_ref[...] = gb

    return _pcall(
        body, name=name, grid=(f // FF_COLS,),
        in_specs=[pl.BlockSpec((None, t, FF_COLS), lambda j: (0, 0, j)), pl.BlockSpec((None, t, FF_COLS), lambda j: (1, 0, j)),
                  pl.BlockSpec((8, FF_COLS), lambda j: (0, j)), pl.BlockSpec((1, FF_COLS), lambda j: (0, j)),
                  pl.BlockSpec((t, d), lambda j: (0, 0), pipeline_mode=pl.Buffered(1)),
                  pl.BlockSpec((FF_COLS, d), lambda j: (j, 0))],
        out_specs=[pl.BlockSpec((2, t, FF_COLS), lambda j: (0, 0, j)), pl.BlockSpec((8, FF_COLS), lambda j: (0, j)),
                   pl.BlockSpec((1, FF_COLS), lambda j: (0, j))],
        out_shape=[jax.ShapeDtypeStruct((2, t, f), BF16), jax.ShapeDtypeStruct((8, f), F32),
                   jax.ShapeDtypeStruct((1, f), F32)],
        scratch_shapes=[pltpu.VMEM((t, FF_COLS), F32)],
        compiler_params=_params("parallel"),
    )(up, up, conv_w8, conv_b, dh2, w_down)


def _exchange(arrays, kinds, name, after=None):
    n = len(arrays)
    npeer = N_DEV - 1
    n_in = n + int(after is not None)

    def body(*refs):
        ins, outs = refs[:n], refs[n_in:n_in + n]
        send_sems, recv_sems, local_sems = refs[n_in + n:]
        x, y, c = lax.axis_index("x"), lax.axis_index("y"), lax.axis_index("c")
        me = 4 * x + 2 * y + c
        copies, locals_ = [], []
        for a in range(n):
            gather = kinds[a] == "gather"
            own = pltpu.make_async_copy(ins[a] if gather else ins[a].at[me], outs[a].at[me], local_sems.at[a])
            own.start()
            locals_.append(own)
            for d in range(1, N_DEV):
                px = 1 - x if d & 4 else x
                py = 1 - y if d & 2 else y
                pc = 1 - c if d & 1 else c
                src = ins[a] if gather else ins[a].at[4 * px + 2 * py + pc]
                cp = pltpu.make_async_remote_copy(
                    src_ref=src, dst_ref=outs[a].at[me],
                    send_sem=send_sems.at[a * npeer + d - 1], recv_sem=recv_sems.at[a * npeer + d - 1],
                    device_id=(px, py, pc), device_id_type=pl.DeviceIdType.MESH)
                cp.start()
                copies.append(cp)
        for cp in copies:
            cp.wait_recv()
        for cp in copies:
            cp.wait_send()
        for own in locals_:
            own.wait()

    out_shape = [jax.ShapeDtypeStruct((N_DEV,) + (a.shape if k == "gather" else a.shape[1:]), a.dtype)
                 for a, k in zip(arrays, kinds)]
    return _pcall(
        body, name=name,
        in_specs=[pl.BlockSpec(memory_space=pl.ANY)] * n_in,
        out_specs=[pl.BlockSpec(memory_space=pl.ANY)] * n,
        out_shape=out_shape,
        scratch_shapes=[pltpu.SemaphoreType.DMA((n * npeer,)), pltpu.SemaphoreType.DMA((n * npeer,)),
                        pltpu.SemaphoreType.DMA((n,))],
        compiler_params=pltpu.CompilerParams(has_side_effects=True),
    )(*arrays, *([] if after is None else [after]))


def _peer_copies(srcs, lands, kinds, send_sems, recv_sems):
    x, y, c = lax.axis_index("x"), lax.axis_index("y"), lax.axis_index("c")
    me = 4 * x + 2 * y + c
    copies = []
    for a in range(len(srcs)):
        for d in range(1, N_DEV):
            px = 1 - x if d & 4 else x
            py = 1 - y if d & 2 else y
            pc = 1 - c if d & 1 else c
            k = a * (N_DEV - 1) + d - 1
            copies.append(pltpu.make_async_remote_copy(
                src_ref=srcs[a] if kinds[a] == "gather" else srcs[a].at[4 * px + 2 * py + pc], dst_ref=lands[a].at[me],
                send_sem=send_sems.at[k], recv_sem=recv_sems.at[k],
                device_id=(px, py, pc), device_id_type=pl.DeviceIdType.MESH))
    return copies


def _exchange_start(arrays, kinds, name, after=None):
    n = len(arrays)
    nsem = n * (N_DEV - 1)
    hbm = pl.BlockSpec(memory_space=pltpu.HBM)
    sem = pl.BlockSpec(memory_space=pltpu.SEMAPHORE)
    land_shapes = [(N_DEV,) + (a.shape if k == "gather" else a.shape[1:]) for a, k in zip(arrays, kinds)]

    n_in = 2 * n + int(after is not None)

    def body(*refs):
        srcs, lands = refs[:n], refs[n:2 * n]
        send_sems, recv_sems = refs[n_in], refs[n_in + 1]
        token = refs[-1]
        for cp in _peer_copies(srcs, lands, kinds, send_sems, recv_sems):
            cp.start()
        token[...] = jnp.zeros_like(token)

    operands = [pltpu.with_memory_space_constraint(a, pltpu.HBM) for a in arrays]
    operands += [pltpu.with_memory_space_constraint(lax.empty(s, a.dtype), pltpu.HBM) for s, a in zip(land_shapes, arrays)]
    operands += [] if after is None else [after]
    out = _pcall(
        body, name=name,
        in_specs=[hbm] * (2 * n) + ([] if after is None else [pl.BlockSpec(memory_space=pl.ANY)]),
        out_specs=[sem, sem] + [hbm] * (2 * n) + [pl.BlockSpec(memory_space=pltpu.VMEM)],
        out_shape=[pltpu.SemaphoreType.DMA((nsem,)), pltpu.SemaphoreType.DMA((nsem,))]
        + [pltpu.HBM(a.shape, a.dtype) for a in arrays]
        + [pltpu.HBM(s, a.dtype) for s, a in zip(land_shapes, arrays)]
        + [jax.ShapeDtypeStruct((8, LANE), F32)],
        input_output_aliases={k: 2 + k for k in range(2 * n)},
        compiler_params=pltpu.CompilerParams(has_side_effects=pltpu.SideEffectType.DATAFLOW_SIDE_EFFECTING),
    )(*operands)
    return out[0], out[1], list(out[2:2 + n]), list(out[2 + n:2 + 2 * n]), out[-1]


def _exchange_wait(started, kinds, after, name):
    send_sems, recv_sems, srcs, lands, _ = started
    n = len(srcs)
    hbm = pl.BlockSpec(memory_space=pltpu.HBM)
    sem = pl.BlockSpec(memory_space=pltpu.SEMAPHORE)

    def body(*refs):
        src_refs, land_refs = refs[:n], refs[n:2 * n]
        copies = _peer_copies(src_refs, land_refs, kinds, refs[2 * n], refs[2 * n + 1])
        for cp in copies:
            cp.wait_send()
        for cp in copies:
            cp.wait_recv()

    out = _pcall(
        body, name=name,
        in_specs=[hbm] * (2 * n) + [sem, sem, pl.BlockSpec(memory_space=pl.ANY)],
        out_specs=[hbm] * (2 * n),
        out_shape=[pltpu.HBM(a.shape, a.dtype) for a in srcs + lands],
        input_output_aliases={k: k for k in range(2 * n)},
        compiler_params=pltpu.CompilerParams(has_side_effects=pltpu.SideEffectType.DATAFLOW_SIDE_EFFECTING),
    )(*srcs, *lands, send_sems, recv_sems, after)
    me = 4 * lax.axis_index("x") + 2 * lax.axis_index("y") + lax.axis_index("c")
    filled = []
    for src, land, kind in zip(out[:n], out[n:], kinds):
        own = src if kind == "gather" else lax.dynamic_index_in_dim(src, me, axis=0, keepdims=False)
        filled.append(lax.dynamic_update_slice(land, own[None], (me,) + (0,) * own.ndim))
    return filled


def _sum_slots(slots, name, rows_tile):
    nd, r, c = slots.shape

    def body(s_ref, o_ref):
        acc = s_ref[0].astype(F32)
        for p in range(1, nd):
            acc = acc + s_ref[p].astype(F32)
        o_ref[...] = acc

    return _pcall(
        body, name=name, grid=(r // rows_tile,),
        in_specs=[pl.BlockSpec((nd, rows_tile, c), lambda i: (0, i, 0))],
        out_specs=pl.BlockSpec((rows_tile, c), lambda i: (i, 0)),
        out_shape=jax.ShapeDtypeStruct((r, c), F32),
        compiler_params=_params("parallel"),
    )(slots)


def _sum_slots_small(slot_arrays, name):
    n = len(slot_arrays)

    def body(*refs):
        for s_ref, o_ref in zip(refs[:n], refs[n:]):
            acc = s_ref[0]
            for p in range(1, s_ref.shape[0]):
                acc = acc + s_ref[p]
            o_ref[...] = acc

    return _pcall(body, name=name, out_shape=[jax.ShapeDtypeStruct(a.shape[1:], F32) for a in slot_arrays])(*slot_arrays)


def _adamw_update(w_ref, g_ref, m_ref, v_ref, d_ref, nm_ref, nv_ref):
    gr = g_ref[...]
    nm = ADAM_B1 * m_ref[...] + (1.0 - ADAM_B1) * gr
    nv = ADAM_B2 * v_ref[...] + (1.0 - ADAM_B2) * (gr * gr)
    m_hat = nm / (1.0 - ADAM_B1 ** ADAM_STEP)
    v_hat = nv / (1.0 - ADAM_B2 ** ADAM_STEP)
    d_ref[...] = -ADAM_LR * (m_hat / (jnp.sqrt(v_hat) + ADAM_EPS) + ADAM_WD * w_ref[...])
    nm_ref[...] = nm
    nv_ref[...] = nv


def _adamw_small(ws, gs, ms, vs, name):
    n = len(ws)

    def body(*refs):
        ins, outs = refs[:4 * n], refs[4 * n:]
        for k in range(n):
            _adamw_update(ins[k], ins[n + k], ins[2 * n + k], ins[3 * n + k], outs[k], outs[n + k], outs[2 * n + k])

    shapes = [jax.ShapeDtypeStruct(w.shape, F32) for w in ws]
    out = _pcall(body, name=name, out_shape=shapes * 3)(*ws, *gs, *ms, *vs)
    return list(out[:n]), list(out[n:2 * n]), list(out[2 * n:])


def _adamw(w, g, m, v, name, rows_tile):
    r, c = w.shape
    body = lambda *refs: _adamw_update(*refs)
    spec = pl.BlockSpec((rows_tile, c), lambda i: (i, 0))
    shp = jax.ShapeDtypeStruct((r, c), F32)
    return _pcall(
        body, name=name, grid=(r // rows_tile,), in_specs=[spec] * 4, out_specs=[spec] * 3, out_shape=[shp] * 3,
        compiler_params=_params("parallel"),
    )(w, g, m, v)


F0 = 2 * RET_QK + 2 * RET_V


def _to_internal_rows(w_t):
    cols = w_t.shape[1]
    fox = w_t[F0:F0 + 3 * FOX_W].reshape(3, FOX_PAIRS, LANE, cols).transpose(1, 0, 2, 3).reshape(3 * FOX_W, cols)
    tail = jnp.zeros((IN_PAD - IN_WIDTH, cols), w_t.dtype)
    return jnp.concatenate([w_t[:F0], fox, w_t[F0 + 3 * FOX_W:], tail], axis=0)


def _from_internal_rows(g_t):
    cols = g_t.shape[1]
    fox = g_t[F0:F0 + 3 * FOX_W].reshape(FOX_PAIRS, 3, LANE, cols).transpose(1, 0, 2, 3).reshape(3 * FOX_W, cols)
    return jnp.concatenate([g_t[:F0], fox, g_t[F0 + 3 * FOX_W:F0 + 3 * FOX_W + FOX_HEADS]], axis=0)


def _local_step(x, target, meta, attn_g, fox_b, ret_g, ffn_g, conv_w8, conv_b, final_g,
                first_weight, late_weights, ffn_grads_ready, out_grad_ready, in_grad_ready):
    seq, d = x.shape
    t = seq + PREFIX
    tm = TOK_TILE
    nq = t // tm
    fox_b128 = jnp.pad(fox_b, ((0, 0), (0, LANE - FOX_HEADS)))

    h0, n1 = _prep_norm(x, meta, attn_g, "prep_norm")
    w_in_t = first_weight(n1)
    proj = _mm_simple(n1, w_in_t, mode="nt", tm=tm, tn=IN_PAD, tk=d, out_dtype=F32, name="mm_in")
    cos, sin = _rope_tables(t)
    o_pre, mixed, states = _ret_fwd(proj, cos, sin, ret_g, "ret_fwd")
    c = _forget_cumsum(proj, fox_b128, "forget_cumsum")
    qa, ka, va = _fox_prep(proj, c, "fox_prep")
    by_block = lambda a: a.reshape(FOX_HEADS, nq, tm, LANE)
    mixed, o_fox, lse = _fox_fwd(by_block(qa), by_block(ka), by_block(va), mixed, "fox_fwd")
    w_out, w_up_t, w_down = late_weights(o_fox)
    tile = pl.BlockSpec((tm, d), lambda i: (i, 0))
    row_vec = pl.BlockSpec((1, d), lambda i: (0, 0))
    resident = lambda shape: pl.BlockSpec(shape, lambda i: (0,) * len(shape), pipeline_mode=pl.Buffered(1))
    acts = lambda dtype: jax.ShapeDtypeStruct((t, d), dtype)
    vec = jax.ShapeDtypeStruct((1, d), F32)

    def residual_and_norm(i, acc, ins, outs):
        h = acc + ins[0][...]
        outs[0][...] = h
        outs[1][...] = (h * lax.rsqrt(jnp.mean(h * h, axis=-1, keepdims=True) + EPS) * ins[1][...]).astype(BF16)

    h1, n2 = _matmul_rows([mixed], [tile], [w_out], [resident((d, d))], [h0, ffn_g], [tile, row_vec],
                          [tile, tile], [acts(F32), acts(BF16)], residual_and_norm, mode="nn", steps=nq, name="mm_out_norm")
    nf = D_FF // 1408
    up, g = _up_conv_fwd(n2, w_up_t, conv_w8, conv_b, "up_conv_fwd")

    def residual_loss_bwd(i, acc, ins, outs):
        loss_ref, dh_ref, dhb_ref, gg_ref = outs
        part, dh, gg = _loss_tile(i, acc + ins[0][...], jnp.concatenate([ins[1][...], ins[2][...], ins[3][...]], axis=0),
                                  ins[4][...])
        _accumulate(loss_ref, i, jnp.broadcast_to(part, loss_ref.shape))
        dh_ref[...] = dh
        dhb_ref[...] = dh.astype(BF16)
        _accumulate(gg_ref, i, gg)

    loss_tile, dh2, dh2_b, g_final = _matmul_rows(
        [g], [pl.BlockSpec((tm, D_FF), lambda i: (i, 0))], [w_down], [resident((D_FF, d))],
        [h1, target, target, target, final_g], [tile] + _shifted_row_specs(d) + [row_vec],
        [pl.BlockSpec((8, LANE), lambda i: (0, 0)), tile, tile, row_vec],
        [jax.ShapeDtypeStruct((8, LANE), F32), acts(F32), acts(BF16), vec], residual_loss_bwd,
        mode="nn", steps=nq, name="mm_down_loss")

    tkw = 2112 if t % 2112 == 0 else tm
    gw_down = _mm_simple(g, dh2_b, mode="tn", tm=1408, tn=d, tk=tkw, out_dtype=BF16, name="mm_gw_down")
    dup, g_conv_w8, g_conv_b = _dg_conv_bwd(up, conv_w8, conv_b, dh2_b, w_down, "dg_conv_bwd")

    half = lambda p: pl.BlockSpec((None, tm, D_FF), lambda i: (p, i, 0))
    half_w = lambda p: pl.BlockSpec((None, D_FF, d), lambda i: (p, 0, 0), pipeline_mode=pl.Buffered(1))
    gw_up_t = _matmul(
        dup, n2, mode="tn", grid=(2 * nf, 1, t // tkw),
        a_spec=pl.BlockSpec((None, tkw, 1408), lambda i, j, k: (i // nf, k, i % nf)),
        b_spec=pl.BlockSpec((tkw, d), lambda i, j, k: (k, 0)),
        o_spec=pl.BlockSpec((1408, d), lambda i, j, k: (i, 0)),
        out_shape=jax.ShapeDtypeStruct((2 * D_FF, d), BF16), name="mm_gw_up")
    def norm_bwd_and_mixer_grad(i, acc, ins, outs):
        dh, gg = _rms_bwd_tile(acc, ins[0][...], ins[1][...], ins[2][...])
        outs[0][...] = dh
        _accumulate(outs[1], i, gg)
        outs[2][...] = _dot(dh.astype(BF16), ins[3][...], NT)

    dh1, g_ffn, dmixed = _matmul_rows(
        [dup, dup], [half(0), half(1)], [w_up_t, w_up_t], [half_w(0), half_w(1)],
        [h1, ffn_g, dh2, w_out], [tile, row_vec, tile, resident((d, d))], [tile, row_vec, tile],
        [acts(F32), vec, acts(F32)], norm_bwd_and_mixer_grad,
        mode="nn", steps=nq, name="mm_dn2_norm_bwd", after=ffn_grads_ready(gw_down, gw_up_t))
    gw_out = _mm_simple(mixed, dh1, mode="tn", tm=d, tn=d, tk=tkw, out_dtype=BF16, name="mm_gw_out")
    dproj, g_ret = _ret_bwd(proj, cos, sin, ret_g + out_grad_ready(gw_out), dmixed, o_pre, states, "ret_bwd")
    qab, doa = _fox_prep_bwd(dmixed, o_fox, lse, qa, "fox_prep_bwd")
    dproj, drs, dcs = _fox_bwd(by_block(qab), by_block(doa), by_block(ka), by_block(va), dproj, "fox_bwd")
    dproj, g_fox_b = _forget_cumsum_bwd(proj, fox_b128, drs, dcs, dproj, "forget_cumsum_bwd")
    gw_in_t = _mm_simple(dproj, n1, mode="tn", tm=640, tn=d, tk=tkw, out_dtype=BF16, name="mm_gw_in")
    sent = in_grad_ready(gw_in_t)
    def input_grads(i, acc, ins, outs):
        gx_ref, gmeta_ref, gg_ref, buf_ref, sem = outs
        dh, gg = _rms_bwd_tile(acc, ins[0][...], ins[1][...], ins[2][...])
        _accumulate(gg_ref, i, gg)
        buf_ref[...] = dh

        @pl.when(i == 0)
        def _():
            gmeta_ref[...] = dh[N_PAD:PREFIX, :]
            first = pltpu.make_async_copy(buf_ref.at[pl.ds(PREFIX, tm - PREFIX)], gx_ref.at[pl.ds(0, tm - PREFIX)], sem)
            first.start()
            first.wait()

        @pl.when(i > 0)
        def _():
            rows = pl.ds(pl.multiple_of(i * tm - PREFIX, PREFIX), tm)
            rest = pltpu.make_async_copy(buf_ref, gx_ref.at[rows], sem)
            rest.start()
            rest.wait()

    grad_x, g_meta, g_attn = _matmul_rows(
        [dproj], [pl.BlockSpec((tm, IN_PAD), lambda i: (i, 0))], [w_in_t], [resident((IN_PAD, d))],
        [h0, attn_g, dh1], [tile, row_vec, tile],
        [pl.BlockSpec(memory_space=pl.ANY), pl.BlockSpec((N_META, d), lambda i: (0, 0)), row_vec],
        [jax.ShapeDtypeStruct((seq, d), F32), jax.ShapeDtypeStruct((N_META, d), F32), vec], input_grads,
        mode="nn", steps=nq, name="mm_dn1_norm_bwd", after=sent,
        scratch=[pltpu.VMEM((tm, d), F32), pltpu.SemaphoreType.DMA(())])

    grads = dict(meta=g_meta, attn_g=g_attn, fox_b=g_fox_b, ret_g=g_ret,
                 ffn_g=g_ffn, conv_w=g_conv_w8, conv_b=g_conv_b, final_g=g_final)
    return loss_tile, grad_x, grads


def kernel(x, meta_tokens, attn_norm_g, w_in, fox_forget_b, ret_norm_g, w_out, ffn_norm_g, w_up, conv_w, conv_b, w_down, final_norm_g, loss_target, m_meta_tokens, m_attn_norm_g, m_w_in, m_fox_forget_b, m_ret_norm_g, m_w_out, m_ffn_norm_g, m_w_up, m_conv_w, m_conv_b, m_w_down, m_final_norm_g, v_meta_tokens, v_attn_norm_g, v_w_in, v_fox_forget_b, v_ret_norm_g, v_w_out, v_ffn_norm_g, v_w_up, v_conv_w, v_conv_b, v_w_down, v_final_norm_g):
    d = D_MODEL
    me = 4 * lax.axis_index("x") + 2 * lax.axis_index("y") + lax.axis_index("c")
    in_blk = IN_WIDTH // N_DEV
    in_blk_pad = 400
    up_blk = 2 * D_FF // N_DEV
    down_blk = D_FF // N_DEV
    cw_blk = D_FF // N_DEV

    w_in_loc = jnp.pad(w_in[0].T.astype(BF16), ((0, in_blk_pad - in_blk), (0, 0)))
    cw_loc = jnp.pad(conv_w[0], ((0, 5), (0, 384 - cw_blk)))
    g_meta, g_cw = _exchange([meta_tokens, cw_loc], ["gather"] * 2, "gather_small")
    first = _exchange_start([w_in_loc], ["gather"], "gather_in_start", after=g_meta)
    rest_loc = [(w_out[0] + first[-1][0:1, 0:1]).astype(BF16), w_up[0].T.astype(BF16), w_down[0].astype(BF16)]
    rest = _exchange_start(rest_loc, ["gather"] * 3, "gather_rest_start")
    meta_f = g_meta.transpose(1, 0, 2).reshape(N_META, d)
    conv_w8 = jnp.pad(g_cw[:, :3, :cw_blk].transpose(1, 0, 2).reshape(3, D_FF), ((0, 5), (0, 0)))
    pending = {}

    def first_weight(after):
        (g_in,) = _exchange_wait(first, ["gather"], after, "gather_in_wait")
        return _to_internal_rows(g_in[:, :in_blk].reshape(IN_WIDTH, d))

    def in_grad_ready(gw_in_t):
        blocks = _from_internal_rows(gw_in_t).reshape(N_DEV, in_blk, d)
        blocks = jnp.pad(blocks, ((0, 0), (0, in_blk_pad - in_blk), (0, 0)))
        pending["in"] = _exchange_start([blocks], ["scatter"], "grads_in_start")
        return pending["in"][-1][0:1, 0:1]

    def late_weights(after):
        g_out, g_up, g_down = _exchange_wait(rest, ["gather"] * 3, after, "gather_rest_wait")
        return g_out.reshape(d, d), g_up.reshape(2, D_FF, d), g_down.reshape(D_FF, d)

    def ffn_grads_ready(gw_down, gw_up_t):
        blocks = [gw_down.reshape(N_DEV, down_blk, d), gw_up_t.reshape(N_DEV, up_blk, d)]
        pending["ffn"] = _exchange_start(blocks, ["scatter"] * 2, "grads_ffn_start")
        return pending["ffn"][-1][0:1, 0:1]

    def out_grad_ready(gw_out):
        pending["out"] = _exchange_start([gw_out.reshape(N_DEV, d // N_DEV, d)], ["scatter"], "grads_out_start")
        return pending["out"][-1][0:1, 0:1]

    loss_tile, grad_x, gr = _local_step(
        x[0], loss_target[0], meta_f, attn_norm_g + rest[-1][0:1, 0:1], fox_forget_b, ret_norm_g, ffn_norm_g,
        conv_w8, conv_b, final_norm_g.reshape(1, d), first_weight, late_weights, ffn_grads_ready, out_grad_ready,
        in_grad_ready)

    r_down, r_up = _exchange_wait(pending["ffn"], ["scatter"] * 2, grad_x, "grads_ffn_wait")
    (r_out,) = _exchange_wait(pending["out"], ["scatter"], grad_x, "grads_out_wait")
    g_w_out = _sum_slots(r_out, "sum_w_out", d // N_DEV)
    g_w_up_t = _sum_slots(r_up, "sum_w_up", up_blk)
    g_w_down = _sum_slots(r_down, "sum_w_down", down_blk)
    as_t = lambda a: a[0].T
    from_t = lambda a: a.T[None]
    d_w_out, m_w_out_n, v_w_out_n = [a[None] for a in _adamw(w_out[0], g_w_out, m_w_out[0], v_w_out[0], "adamw_w_out", 128)]
    up_t = _adamw(as_t(w_up), g_w_up_t, as_t(m_w_up), as_t(v_w_up), "adamw_w_up", up_blk // 2)
    d_w_up, m_w_up_n, v_w_up_n = [from_t(a) for a in up_t]
    d_w_down, m_w_down_n, v_w_down_n = [a[None] for a in _adamw(w_down[0], g_w_down, m_w_down[0], v_w_down[0],
                                                                "adamw_w_down", down_blk)]

    small = [loss_tile, gr["attn_g"], gr["fox_b"], gr["ret_g"], gr["ffn_g"], gr["conv_b"], gr["final_g"],
             gr["meta"], gr["conv_w"]]
    r_small = _exchange(small, ["gather"] * len(small), "exchange_small", after=up_t[0])
    (loss_all, g_attn, g_fox_b128, g_ret, g_ffn, g_conv_b, g_final, g_meta_full, g_cw_full) = _sum_slots_small(
        r_small, "sum_small")
    loss = loss_all[0, 0]
    g_fox_b = g_fox_b128[:, :FOX_HEADS]
    g_meta_loc = lax.dynamic_slice(g_meta_full, (0, me * (d // N_DEV)), (N_META, d // N_DEV))
    g_cw_loc = lax.dynamic_slice(g_cw_full, (0, me * cw_blk), (3, cw_blk))

    (r_in,) = _exchange_wait(pending["in"], ["scatter"], r_small[0], "grads_in_wait")
    g_w_in_t = _sum_slots(r_in, "sum_w_in", in_blk_pad)[:in_blk]
    d_w_in, m_w_in_n, v_w_in_n = [from_t(a) for a in _adamw(as_t(w_in), g_w_in_t, as_t(m_w_in), as_t(v_w_in),
                                                            "adamw_w_in", in_blk)]
    g_w_in, g_w_up = g_w_in_t.T, g_w_up_t.T
    row = lambda a: a.reshape(1, d)
    sm_grads = [g_meta_loc, g_attn, g_fox_b, g_ret, g_ffn, g_cw_loc, g_conv_b, g_final]
    sm_w = [meta_tokens, attn_norm_g, fox_forget_b, ret_norm_g, ffn_norm_g, conv_w[0], conv_b, row(final_norm_g)]
    sm_m = [m_meta_tokens, m_attn_norm_g, m_fox_forget_b, m_ret_norm_g, m_ffn_norm_g, m_conv_w[0], m_conv_b,
            row(m_final_norm_g)]
    sm_v = [v_meta_tokens, v_attn_norm_g, v_fox_forget_b, v_ret_norm_g, v_ffn_norm_g, v_conv_w[0], v_conv_b,
            row(v_final_norm_g)]
    dl, ml, vl = [lst[:7] + [lst[7].reshape(d)] for lst in _adamw_small(sm_w, sm_grads, sm_m, sm_v, "adamw_small")]

    def by_weight(meta_, attn_, w_in_, fox_, ret_, w_out_, ffn_, w_up_, cw_, cb_, w_down_, final_):
        return (meta_, attn_, w_in_, fox_, ret_, w_out_, ffn_, w_up_, cw_[None], cb_, w_down_, final_)

    grads_out = by_weight(g_meta_loc, g_attn, g_w_in[None], g_fox_b, g_ret, g_w_out[None], g_ffn, g_w_up[None], g_cw_loc,
                          g_conv_b, g_w_down[None], g_final.reshape(d))
    delta_out = by_weight(dl[0], dl[1], d_w_in, dl[2], dl[3], d_w_out, dl[4], d_w_up, dl[5], dl[6], d_w_down, dl[7])
    m_out = by_weight(ml[0], ml[1], m_w_in_n, ml[2], ml[3], m_w_out_n, ml[4], m_w_up_n, ml[5], ml[6], m_w_down_n, ml[7])
    v_out = by_weight(vl[0], vl[1], v_w_in_n, vl[2], vl[3], v_w_out_n, vl[4], v_w_up_n, vl[5], vl[6], v_w_down_n, vl[7])
    return (loss, grad_x[None]) + grads_out + delta_out + m_out + v_out
```

```python
import numpy as np
import jax
import jax.numpy as jnp
from jax import lax
from jax.experimental import pallas as pl
from jax.experimental.pallas import tpu as pltpu

F32 = jnp.float32
BF16 = jnp.bfloat16

D_MODEL = 1024
N_META = 16
N_PAD = 112
PREFIX = 128
RET_HEADS = 4
RET_DK = 64
RET_DV = 128
FOX_HEADS = 8
FOX_DH = 64
D_FF = 2816
ROPE_BASE = 10000.0
EPS = 1e-6
NEG = -1e30
RET_QK = RET_HEADS * RET_DK
RET_V = RET_HEADS * RET_DV
FOX_W = FOX_HEADS * FOX_DH
IN_WIDTH = 2 * RET_QK + 2 * RET_V + 3 * FOX_W + FOX_HEADS
IN_PAD = 3200
FF_COL_BLOCK = (IN_WIDTH - FOX_HEADS) // 128
QK_SCALE = 0.125

ADAM_LR = 0.001
ADAM_B1 = 0.9
ADAM_B2 = 0.999
ADAM_EPS = 1e-08
ADAM_WD = 0.01
ADAM_STEP = 10

N_DEV = 8
LANE = 128
ROW_TILE = 128
TOK_TILE = 384

NN = (((1,), (0,)), ((), ()))
NT = (((1,), (1,)), ((), ()))
TN = (((0,), (0,)), ((), ()))


def _pcall(body, **kw):
    return pl.pallas_call(body, **kw)


def _params(*sem):
    return pltpu.CompilerParams(dimension_semantics=sem)


def _dot(a, b, dims=NN):
    return lax.dot_general(a, b, dims, preferred_element_type=F32)


def _sigmoid(x):
    return 0.5 * jnp.tanh(0.5 * x) + 0.5


def _matmul(a, b, *, mode, grid, a_spec, b_spec, o_spec, out_shape, name, add=None, add_spec=None, after=None):
    dims = {"nn": NN, "nt": NT, "tn": TN}[mode]
    nk = grid[2]
    has_add = add is not None
    a_list, b_list = (list(a), list(b)) if isinstance(a, (list, tuple)) else ([a], [b])
    a_specs, b_specs = (list(a_spec), list(b_spec)) if isinstance(a_spec, (list, tuple)) else ([a_spec], [b_spec])
    nt = len(a_list)
    n_in = 2 * nt + int(has_add) + int(after is not None)

    def body(*refs):
        a_refs, b_refs = refs[:nt], refs[nt:2 * nt]
        add_ref = refs[2 * nt] if has_add else None
        o_ref = refs[n_in]
        part = _dot(a_refs[0][...].astype(BF16), b_refs[0][...].astype(BF16), dims)
        for ar, br in zip(a_refs[1:], b_refs[1:]):
            part = part + _dot(ar[...].astype(BF16), br[...].astype(BF16), dims)

        def finish(acc):
            if has_add:
                acc = acc + add_ref[...]
            o_ref[...] = acc.astype(o_ref.dtype)

        if nk == 1:
            finish(part)
        else:
            acc_ref = refs[-1]
            k = pl.program_id(2)

            @pl.when(k == 0)
            def _():
                acc_ref[...] = part

            @pl.when(k > 0)
            def _():
                acc_ref[...] += part

            @pl.when(k == nk - 1)
            def _():
                finish(acc_ref[...])

    in_specs = a_specs + b_specs + ([add_spec] if has_add else [])
    args = tuple(a_list) + tuple(b_list) + ((add,) if has_add else ())
    if after is not None:
        in_specs, args = in_specs + [pl.BlockSpec(memory_space=pl.ANY)], args + (after,)
    scratch = [] if nk == 1 else [pltpu.VMEM(tuple(d for d in o_spec.block_shape if d is not None), F32)]
    return _pcall(
        body, name=name, grid=grid, in_specs=in_specs, out_specs=o_spec, out_shape=out_shape,
        scratch_shapes=scratch, compiler_params=_params("parallel", "parallel", "arbitrary"),
    )(*args)


def _mm_simple(a, b, *, mode, tm, tn, tk, out_dtype, name, add=None, after=None):
    if mode == "tn":
        K, M = a.shape
    else:
        M, K = a.shape
    N = b.shape[0] if mode == "nt" else b.shape[1]
    grid = (M // tm, N // tn, K // tk)
    resident = dict(pipeline_mode=pl.Buffered(1)) if (tn == N and tk == K) else {}
    a_spec = pl.BlockSpec((tk, tm), lambda i, j, k: (k, i)) if mode == "tn" else pl.BlockSpec((tm, tk), lambda i, j, k: (i, k))
    b_spec = (pl.BlockSpec((tn, tk), lambda i, j, k: (j, k), **resident) if mode == "nt"
              else pl.BlockSpec((tk, tn), lambda i, j, k: (k, j), **resident))
    o_spec = pl.BlockSpec((tm, tn), lambda i, j, k: (i, j))
    return _matmul(a, b, mode=mode, grid=grid, a_spec=a_spec, b_spec=b_spec, o_spec=o_spec,
                   out_shape=jax.ShapeDtypeStruct((M, N), out_dtype), name=name, add=add,
                   add_spec=o_spec if add is not None else None, after=after)


def _matmul_rows(a_list, a_specs, b_list, b_specs, extras, extra_specs, out_specs, out_shape, epilogue, *,
                 mode, steps, name, after=None, scratch=()):
    dims = {"nn": NN, "nt": NT}[mode]
    nt, ne = len(a_list), len(extras)
    n_in = 2 * nt + ne + int(after is not None)

    def body(*refs):
        acc = _dot(refs[0][...].astype(BF16), refs[nt][...].astype(BF16), dims)
        for k in range(1, nt):
            acc = acc + _dot(refs[k][...].astype(BF16), refs[nt + k][...].astype(BF16), dims)
        epilogue(pl.program_id(0), acc, refs[2 * nt:2 * nt + ne], refs[n_in:])

    in_specs = list(a_specs) + list(b_specs) + list(extra_specs)
    args = tuple(a_list) + tuple(b_list) + tuple(extras)
    if after is not None:
        in_specs, args = in_specs + [pl.BlockSpec(memory_space=pl.ANY)], args + (after,)
    return _pcall(body, name=name, grid=(steps,), in_specs=in_specs, out_specs=out_specs, out_shape=out_shape,
                  scratch_shapes=list(scratch), compiler_params=_params("arbitrary"))(*args)


def _rms_bwd_tile(dy, x, gain, dres):
    r = lax.rsqrt(jnp.mean(x * x, axis=-1, keepdims=True) + EPS)
    xhat = x * r
    u = dy * gain
    return dres + r * (u - xhat * jnp.mean(u * xhat, axis=-1, keepdims=True)), jnp.sum(dy * xhat, axis=0, keepdims=True)


def _loss_tile(i, x, tgt, gain):
    d = x.shape[-1]
    r = lax.rsqrt(jnp.mean(x * x, axis=-1, keepdims=True) + EPS)
    xhat = x * r
    counted = (i * TOK_TILE + lax.broadcasted_iota(jnp.int32, (TOK_TILE, 1), 0)) >= PREFIX
    err = jnp.where(counted, xhat * gain - tgt, 0.0)
    dy = err * (1.0 / d)
    u = dy * gain
    dh = r * (u - xhat * jnp.mean(u * xhat, axis=-1, keepdims=True))
    return 0.5 * jnp.sum(jnp.mean(err * err, axis=-1, keepdims=True)), dh, jnp.sum(dy * xhat, axis=0, keepdims=True)


def _accumulate(ref, i, part):
    @pl.when(i == 0)
    def _():
        ref[...] = part

    @pl.when(i > 0)
    def _():
        ref[...] += part


def _prep_norm(x, meta, gain, name):
    seq, d = x.shape
    t = seq + PREFIX

    def body(xa_ref, xb_ref, xc_ref, meta_ref, g_ref, h_ref, n_ref):
        i = pl.program_id(0)

        @pl.when(i == 0)
        def _():
            h_ref[0:N_PAD, :] = jnp.zeros((N_PAD, d), F32)
            h_ref[N_PAD:ROW_TILE, :] = meta_ref[...]

        @pl.when(i > 0)
        def _():
            h_ref[0:ROW_TILE, :] = xa_ref[...]

        h_ref[ROW_TILE:2 * ROW_TILE, :] = xb_ref[...]
        h_ref[2 * ROW_TILE:3 * ROW_TILE, :] = xc_ref[...]
        h = h_ref[...]
        r = lax.rsqrt(jnp.mean(h * h, axis=-1, keepdims=True) + EPS)
        n_ref[...] = (h * r * g_ref[...]).astype(BF16)

    return _pcall(
        body, name=name, grid=(t // TOK_TILE,),
        in_specs=_shifted_row_specs(d) + [pl.BlockSpec((N_META, d), lambda i: (0, 0)), pl.BlockSpec((1, d), lambda i: (0, 0))],
        out_specs=[pl.BlockSpec((TOK_TILE, d), lambda i: (i, 0)), pl.BlockSpec((TOK_TILE, d), lambda i: (i, 0))],
        out_shape=[jax.ShapeDtypeStruct((t, d), F32), jax.ShapeDtypeStruct((t, d), BF16)],
        compiler_params=_params("parallel"),
    )(x, x, x, meta, gain)


def _shifted_row_specs(d):
    blocks_per_tile = TOK_TILE // ROW_TILE
    return [pl.BlockSpec((ROW_TILE, d), lambda i, r=r: (jnp.maximum(blocks_per_tile * i + r, 0), 0)) for r in (-1, 0, 1)]


def _ret_consts(bk):
    gam = 1.0 - 2.0 ** (-5.0 - np.arange(RET_HEADS))
    n = np.arange(bk)
    same_or_earlier_chunk = (n[None, :] // 64) <= (n[:, None] // 64)
    w = gam[:, None, None] ** np.abs(n[:, None] - n[None, :])[None] * same_or_earlier_chunk[None]
    wq = gam[:, None] ** (n[None, :] + 1.0)
    wk = gam[:, None] ** (bk - 1.0 - n[None, :])
    mask = (np.arange(RET_QK)[None, :] // RET_DK) == np.arange(RET_HEADS)[:, None]
    return (jnp.asarray(w, F32), jnp.asarray(wq[:, :, None], F32), jnp.asarray(wk[:, :, None], F32),
            jnp.asarray(mask[:, None, :], F32), [float(g ** bk) for g in gam])


def _rope_tables(t):
    half = RET_DK // 2
    inv = 1.0 / (ROPE_BASE ** (jnp.arange(half, dtype=F32) / half))
    ang = jnp.arange(t).astype(F32)[:, None] * inv[None, :]
    cos, sin = jnp.cos(ang), jnp.sin(ang)
    return (jnp.tile(jnp.concatenate([cos, cos], axis=1), (1, RET_HEADS)),
            jnp.tile(jnp.concatenate([-sin, sin], axis=1), (1, RET_HEADS)))


def _swap_halves(x):
    outs = []
    for s in range(x.shape[1] // LANE):
        xs = x[:, LANE * s:LANE * (s + 1)]
        lane = lax.broadcasted_iota(jnp.int32, xs.shape, 1)
        outs.append(jnp.where((lane & 32) == 0, pltpu.roll(xs, LANE - 32, axis=1), pltpu.roll(xs, 32, axis=1)))
    return outs[0] if len(outs) == 1 else jnp.concatenate(outs, axis=1)


def _rope(x, cos, sin_signed):
    return x * cos + _swap_halves(x) * sin_signed


def _rope_t(dx, cos, sin_signed):
    return dx * cos + _swap_halves(dx * sin_signed)


def _ret_fwd(proj, cos, sin, gain, name):
    t = proj.shape[0]
    bk = TOK_TILE
    nb = t // bk
    w, wq, wk, mask, g_blk = _ret_consts(bk)

    def body(q_ref, k_ref, v_ref, rg_ref, cos_ref, sin_ref, w_ref, wq_ref, wk_ref, mask_ref, gain_ref,
             opre_ref, og_ref, st_ref, r_ref):
        i = pl.program_id(0)

        @pl.when(i == 0)
        def _():
            r_ref[...] = jnp.zeros_like(r_ref)

        c, s = cos_ref[...], sin_ref[...]
        valid = ((i * bk + lax.broadcasted_iota(jnp.int32, (bk, 1), 0)) >= N_PAD).astype(F32)
        qr = _rope(q_ref[...], c, s)
        kr = _rope(k_ref[...], c, s) * QK_SCALE * valid
        kb = kr.astype(BF16)
        for h in range(RET_HEADS):
            hm = mask_ref[h]
            cols = slice(RET_DV * h, RET_DV * (h + 1))
            vh = v_ref[:, cols].astype(BF16)
            r_prev = r_ref[h]
            st_ref[0, h] = r_prev
            sm = _dot((qr * hm).astype(BF16), kb, NT) * w_ref[h]
            o = _dot(sm.astype(BF16), vh) + _dot((qr * (hm * wq_ref[h])).astype(BF16), r_prev.astype(BF16))
            r_ref[h] = g_blk[h] * r_prev + _dot((kr * wk_ref[h]).astype(BF16), vh, TN)
            opre_ref[:, cols] = o
            rstd = lax.rsqrt(jnp.mean(o * o, axis=-1, keepdims=True) + EPS)
            rg = rg_ref[:, cols]
            og_ref[:, cols] = (o * rstd * gain_ref[:, cols] * (rg * _sigmoid(rg))).astype(BF16)

    full = lambda shape: pl.BlockSpec(shape, lambda i: (0,) * len(shape))
    return _pcall(
        body, name=name, grid=(nb,),
        in_specs=[pl.BlockSpec((bk, RET_QK), lambda i: (i, 0)), pl.BlockSpec((bk, RET_QK), lambda i: (i, 1)),
                  pl.BlockSpec((bk, RET_V), lambda i: (i, 1)), pl.BlockSpec((bk, RET_V), lambda i: (i, 2)),
                  pl.BlockSpec((bk, RET_QK), lambda i: (i, 0)), pl.BlockSpec((bk, RET_QK), lambda i: (i, 0)),
                  full((RET_HEADS, bk, bk)), full((RET_HEADS, bk, 1)), full((RET_HEADS, bk, 1)),
                  full((RET_HEADS, 1, RET_QK)), full((1, RET_V))],
        out_specs=[pl.BlockSpec((bk, RET_V), lambda i: (i, 0)), pl.BlockSpec((bk, RET_V), lambda i: (i, 0)),
                   pl.BlockSpec((1, RET_HEADS, RET_QK, RET_DV), lambda i: (i, 0, 0, 0))],
        out_shape=[jax.ShapeDtypeStruct((t, RET_V), F32), jax.ShapeDtypeStruct((t, RET_V + FOX_W), BF16),
                   jax.ShapeDtypeStruct((nb, RET_HEADS, RET_QK, RET_DV), F32)],
        scratch_shapes=[pltpu.VMEM((RET_HEADS, RET_QK, RET_DV), F32)],
        compiler_params=_params("arbitrary"),
    )(proj, proj, proj, proj, cos, sin, w, wq, wk, mask, gain)


def _ret_bwd(proj, cos, sin, gain, dmixed, opre, states, name):
    t = proj.shape[0]
    bk = TOK_TILE
    nb = t // bk
    w, wq, wk, mask, g_blk = _ret_consts(bk)
    v0, g0 = 2 * RET_QK, 2 * RET_QK + RET_V

    def body(q_ref, k_ref, v_ref, rg_ref, cos_ref, sin_ref, w_ref, wq_ref, wk_ref, mask_ref, gain_ref,
             dog_ref, opre_ref, st_ref, dp_ref, gg_ref, dr_ref):
        step = pl.program_id(0)
        i = nb - 1 - step

        @pl.when(step == 0)
        def _():
            dr_ref[...] = jnp.zeros_like(dr_ref)
            gg_ref[...] = jnp.zeros_like(gg_ref)

        c, s = cos_ref[...], sin_ref[...]
        valid = ((i * bk + lax.broadcasted_iota(jnp.int32, (bk, 1), 0)) >= N_PAD).astype(F32)
        qr = _rope(q_ref[...], c, s)
        kr = _rope(k_ref[...], c, s) * QK_SCALE * valid
        kb = kr.astype(BF16)
        dqr = jnp.zeros((bk, RET_QK), F32)
        dkr = jnp.zeros((bk, RET_QK), F32)
        for h in range(RET_HEADS):
            hm = mask_ref[h]
            cols = slice(RET_DV * h, RET_DV * (h + 1))
            vh = v_ref[:, cols].astype(BF16)
            o = opre_ref[:, cols]
            rstd = lax.rsqrt(jnp.mean(o * o, axis=-1, keepdims=True) + EPS)
            xhat = o * rstd
            rg = rg_ref[:, cols]
            sg = _sigmoid(rg)
            gate = rg * sg
            gn = gain_ref[:, cols]
            dog = dog_ref[:, cols]
            dp_ref[:, g0 + RET_DV * h:g0 + RET_DV * (h + 1)] = (
                dog * xhat * gn * (sg * (1.0 + rg * (1.0 - sg)))).astype(BF16)
            gg_ref[:, cols] += jnp.sum(dog * xhat * gate, axis=0, keepdims=True)
            dxh = dog * gn * gate
            do = (rstd * (dxh - xhat * jnp.mean(dxh * xhat, axis=-1, keepdims=True))).astype(BF16)
            qm = (qr * hm).astype(BF16)
            qw = (qr * (hm * wq_ref[h])).astype(BF16)
            kw = (kr * wk_ref[h]).astype(BF16)
            wh = w_ref[h]
            sm = (_dot(qm, kb, NT) * wh).astype(BF16)
            ds = (_dot(do, vh, NT) * wh).astype(BF16)
            dr = dr_ref[h]
            drb = dr.astype(BF16)
            dp_ref[:, v0 + RET_DV * h:v0 + RET_DV * (h + 1)] = (_dot(sm, do, TN) + _dot(kw, drb)).astype(BF16)
            dqr = dqr + _dot(ds, kb) * hm + _dot(do, st_ref[0, h].astype(BF16), NT) * (hm * wq_ref[h])
            dkr = dkr + _dot(ds, qm, TN) + _dot(vh, drb, NT) * wk_ref[h]
            dr_ref[h] = g_blk[h] * dr + _dot(qw, do, TN)
        dp_ref[:, 0:RET_QK] = _rope_t(dqr, c, s).astype(BF16)
        dp_ref[:, RET_QK:2 * RET_QK] = _rope_t(dkr * (QK_SCALE * valid), c, s).astype(BF16)

    full = lambda shape: pl.BlockSpec(shape, lambda i: (0,) * len(shape))
    rev = lambda col: (lambda i: (nb - 1 - i, col))
    return _pcall(
        body, name=name, grid=(nb,),
        in_specs=[pl.BlockSpec((bk, RET_QK), rev(0)), pl.BlockSpec((bk, RET_QK), rev(1)),
                  pl.BlockSpec((bk, RET_V), rev(1)), pl.BlockSpec((bk, RET_V), rev(2)),
                  pl.BlockSpec((bk, RET_QK), rev(0)), pl.BlockSpec((bk, RET_QK), rev(0)),
                  full((RET_HEADS, bk, bk)), full((RET_HEADS, bk, 1)), full((RET_HEADS, bk, 1)),
                  full((RET_HEADS, 1, RET_QK)), full((1, RET_V)),
                  pl.BlockSpec((bk, RET_V), rev(0)), pl.BlockSpec((bk, RET_V), rev(0)),
                  pl.BlockSpec((1, RET_HEADS, RET_QK, RET_DV), lambda i: (nb - 1 - i, 0, 0, 0))],
        out_specs=[pl.BlockSpec((bk, g0 + RET_V), rev(0)), pl.BlockSpec((1, RET_V), lambda i: (0, 0))],
        out_shape=[jax.ShapeDtypeStruct((t, IN_PAD), BF16), jax.ShapeDtypeStruct((1, RET_V), F32)],
        scratch_shapes=[pltpu.VMEM((RET_HEADS, RET_QK, RET_DV), F32)],
        compiler_params=_params("arbitrary"),
    )(proj, proj, proj, proj, cos, sin, w, wq, wk, mask, gain, dmixed, opre, states)


def _forget_cumsum(proj, bias, name):
    t = proj.shape[0]
    rt = TOK_TILE
    nb = t // rt
    tril = jnp.asarray(np.tril(np.ones((rt, rt))), F32)

    def body(z_ref, b_ref, tril_ref, c_ref, carry_ref):
        i = pl.program_id(0)

        @pl.when(i == 0)
        def _():
            carry_ref[...] = jnp.zeros_like(carry_ref)

        z = z_ref[...] + b_ref[...]
        logf = jnp.minimum(z, 0.0) - jnp.log(1.0 + jnp.exp(-jnp.abs(z)))
        c = lax.dot_general(tril_ref[...], logf, NN, precision=lax.Precision.HIGHEST,
                            preferred_element_type=F32) + carry_ref[...]
        c_ref[...] = c
        carry_ref[...] = c[rt - 1:rt, :]

    return _pcall(
        body, name=name, grid=(nb,),
        in_specs=[pl.BlockSpec((rt, LANE), lambda i: (i, FF_COL_BLOCK)), pl.BlockSpec((1, LANE), lambda i: (0, 0)),
                  pl.BlockSpec((rt, rt), lambda i: (0, 0))],
        out_specs=pl.BlockSpec((rt, LANE), lambda i: (i, 0)),
        out_shape=jax.ShapeDtypeStruct((t, LANE), F32),
        scratch_shapes=[pltpu.VMEM((1, LANE), F32)],
        compiler_params=_params("arbitrary"),
    )(proj, bias, tril)


def _forget_cumsum_bwd(proj, bias, drs, dcs, dproj, name):
    t = proj.shape[0]
    rt = TOK_TILE
    nb = t // rt
    triu = jnp.asarray(np.triu(np.ones((rt, rt))), F32)

    def body(z_ref, b_ref, triu_ref, drs_ref, dcs_ref, dproj_in, dz_ref, gb_ref, carry_ref):
        step = pl.program_id(0)

        @pl.when(step == 0)
        def _():
            carry_ref[...] = jnp.zeros_like(carry_ref)
            gb_ref[...] = jnp.zeros_like(gb_ref)

        dlogf = lax.dot_general(triu_ref[...], drs_ref[...] - dcs_ref[...], NN, precision=lax.Precision.HIGHEST,
                                preferred_element_type=F32) + carry_ref[...]
        carry_ref[...] = dlogf[0:1, :]
        z = z_ref[...] + b_ref[...]
        is_head = lax.broadcasted_iota(jnp.int32, (rt, LANE), 1) < FOX_HEADS
        dz = jnp.where(is_head, dlogf / (1.0 + jnp.exp(z)), 0.0)
        dz_ref[...] = dz.astype(BF16)
        gb_ref[...] += jnp.sum(dz, axis=0, keepdims=True)

    return _pcall(
        body, name=name, grid=(nb,),
        in_specs=[pl.BlockSpec((rt, LANE), lambda i: (nb - 1 - i, FF_COL_BLOCK)),
                  pl.BlockSpec((1, LANE), lambda i: (0, 0)),
                  pl.BlockSpec((rt, rt), lambda i: (0, 0)),
                  pl.BlockSpec((rt, LANE), lambda i: (nb - 1 - i, 0)),
                  pl.BlockSpec((rt, LANE), lambda i: (nb - 1 - i, 0)),
                  pl.BlockSpec(memory_space=pl.ANY)],
        out_specs=[pl.BlockSpec((rt, LANE), lambda i: (nb - 1 - i, FF_COL_BLOCK)),
                   pl.BlockSpec((1, LANE), lambda i: (0, 0))],
        out_shape=[jax.ShapeDtypeStruct(dproj.shape, BF16), jax.ShapeDtypeStruct((1, LANE), F32)],
        input_output_aliases={5: 0},
        scratch_shapes=[pltpu.VMEM((1, LANE), F32)],
        compiler_params=_params("arbitrary"),
    )(proj, bias, triu, drs, dcs, dproj)


FOX_PAIRS = FOX_HEADS // 2
L_ONE_Q = FOX_DH
L_ONE_K = FOX_DH + 3
L_LSE = FOX_DH + 4


def _split3(x):
    hi = x.astype(BF16).astype(F32)
    r = x - hi
    mid = r.astype(BF16).astype(F32)
    return hi, mid, r - mid


def _head_to_low(slab, e):
    return slab if e == 0 else pltpu.roll(slab, FOX_DH, axis=1)


def _pair(a, b, low):
    return jnp.where(low, a, pltpu.roll(b, FOX_DH, axis=1))


def _fox_prep(proj, c, name):
    t = proj.shape[0]
    tq = TOK_TILE

    def body(p_ref, c_ref, qa_ref, ka_ref, va_ref):
        i = pl.program_id(0)
        lane = lax.broadcasted_iota(jnp.int32, (tq, LANE), 1)
        low = lane < FOX_DH
        live = (i * tq + lax.broadcasted_iota(jnp.int32, (tq, 1), 0)) >= N_PAD
        q_tail = jnp.where(lane < L_ONE_Q + 3, 1.0, 0.0)
        k_ones = (lane >= L_ONE_K) & (lane < L_ONE_K + 4)
        v_tail = jnp.where(lane < FOX_DH + 2, 1.0, 0.0)
        for pair in range(FOX_PAIRS):
            base = 3 * LANE * pair
            for e in range(2):
                h = 2 * pair + e
                q = _head_to_low(p_ref[:, base:base + LANE], e)
                k = _head_to_low(p_ref[:, base + LANE:base + 2 * LANE], e)
                v = _head_to_low(p_ref[:, base + 2 * LANE:base + 3 * LANE], e)
                hi, mid, lo = _split3(jnp.where(live, -c_ref[:, h:h + 1], NEG))
                ka = jnp.where(low, k, jnp.where(k_ones, 1.0, 0.0))
                ka = jnp.where(lane == L_ONE_Q, hi, jnp.where(lane == L_ONE_Q + 1, mid, jnp.where(lane == L_ONE_Q + 2, lo, ka)))
                qa_ref[h] = jnp.where(low, q * QK_SCALE, q_tail).astype(BF16)
                ka_ref[h] = ka.astype(BF16)
                va_ref[h] = jnp.where(low, v, v_tail).astype(BF16)

    out = jax.ShapeDtypeStruct((FOX_HEADS, t, LANE), BF16)
    ospec = pl.BlockSpec((FOX_HEADS, tq, LANE), lambda i: (0, i, 0))
    return _pcall(
        body, name=name, grid=(t // tq,),
        in_specs=[pl.BlockSpec((tq, 3 * FOX_W), lambda i: (i, 1)), pl.BlockSpec((tq, LANE), lambda i: (i, 0))],
        out_specs=[ospec, ospec, ospec], out_shape=[out, out, out],
        compiler_params=_params("parallel"),
    )(proj, c)


STEP_PAIRS = 2
STEP_HEADS = 2 * STEP_PAIRS
FOX_GROUPS = FOX_PAIRS // STEP_PAIRS
FWD_PAIRS = 4
FWD_HEADS = 2 * FWD_PAIRS
FWD_GROUPS = FOX_PAIRS // FWD_PAIRS


def _blockdiag(a, b):
    z = jnp.zeros_like(a)
    return jnp.concatenate([jnp.concatenate([a, z], axis=1), jnp.concatenate([z, b], axis=1)], axis=0)


def _fox_fwd(qa, ka, va, mixed, name):
    nh, nq, tq, _ = qa.shape
    t = nq * tq

    def body(qa_ref, ka_ref, va_ref, mixed_in, mixed_ref, o_ref, lse_ref):
        i = pl.program_id(1)
        lane = lax.broadcasted_iota(jnp.int32, (tq, LANE), 1)
        causal = lax.broadcasted_iota(jnp.int32, (tq, tq), 1) <= lax.broadcasted_iota(jnp.int32, (tq, tq), 0)
        qps = [jnp.concatenate([qa_ref[2 * c], qa_ref[2 * c + 1]], axis=1) for c in range(FWD_PAIRS)]

        def logits(j):
            return [_dot(qps[c], _blockdiag(ka_ref[2 * c, j], ka_ref[2 * c + 1, j]), NT) for c in range(FWD_PAIRS)]

        def update(j, scores, carry, diagonal):
            new = []
            for c in range(FWD_PAIRS):
                ms, acc = carry[c]
                ps, ms_new, alphas = [], [], []
                for e in range(2):
                    s = scores[c][:, e * tq:(e + 1) * tq]
                    if diagonal:
                        s = jnp.where(causal, s, NEG)
                    m_new = jnp.maximum(ms[e], jnp.max(s, axis=-1, keepdims=True))
                    ps.append(jnp.exp(s - m_new).astype(BF16))
                    ms_new.append(m_new)
                    alphas.append(jnp.broadcast_to(jnp.exp(ms[e] - m_new), (tq, LANE)))
                pv = _dot(jnp.concatenate(ps, axis=1), _blockdiag(va_ref[2 * c, j], va_ref[2 * c + 1, j]))
                new.append((tuple(ms_new), jnp.concatenate(alphas, axis=1) * acc + pv))
            return tuple(new)

        m0 = jnp.full((tq, 1), NEG, F32)
        init = tuple(((m0, m0), jnp.zeros((tq, 2 * LANE), F32)) for _ in range(FWD_PAIRS))
        carry = lax.fori_loop(0, i, lambda j, cr: update(j, logits(j), cr, False), init)
        o_pairs = []
        lse = jnp.zeros((tq, LANE), F32)
        for c, (ms, acc) in enumerate(update(i, logits(i), carry, True)):
            outs = []
            for e in range(2):
                half = acc[:, e * LANE:(e + 1) * LANE]
                l = half[:, FOX_DH:FOX_DH + 1]
                outs.append(half / l)
                lse = jnp.where(lane == 2 * c + e, ms[e] + jnp.log(l), lse)
            o_pairs.append(_pair(outs[0], outs[1], lane < FOX_DH))
        o_all = jnp.concatenate(o_pairs, axis=1)
        mixed_ref[...] = o_all.astype(BF16)
        o_ref[...] = o_all
        lse_ref[...] = lse

    width = FWD_PAIRS * LANE
    whole = pl.BlockSpec((FWD_HEADS, nq, tq, LANE), lambda g, i: (g, 0, 0, 0), pipeline_mode=pl.Buffered(1))
    return _pcall(
        body, name=name, grid=(FWD_GROUPS, nq),
        in_specs=[pl.BlockSpec((FWD_HEADS, None, tq, LANE), lambda g, i: (g, i, 0, 0)), whole, whole,
                  pl.BlockSpec(memory_space=pl.ANY)],
        out_specs=[pl.BlockSpec((tq, width), lambda g, i: (i, RET_V // width + g)),
                   pl.BlockSpec((tq, width), lambda g, i: (i, g)),
                   pl.BlockSpec((None, tq, LANE), lambda g, i: (g, i, 0))],
        out_shape=[jax.ShapeDtypeStruct(mixed.shape, BF16), jax.ShapeDtypeStruct((t, FOX_W), F32),
                   jax.ShapeDtypeStruct((FWD_GROUPS, t, LANE), F32)],
        input_output_aliases={3: 0},
        compiler_params=_params("parallel", "parallel"),
    )(qa, ka, va, mixed)


def _fox_prep_bwd(dmixed, o_fox, lse, qa, name):
    t = dmixed.shape[0]
    tq = TOK_TILE

    def body(dm_ref, o_ref, lse_ref, qa_ref, qab_ref, doa_ref):
        i = pl.program_id(0)
        lane = lax.broadcasted_iota(jnp.int32, (tq, LANE), 1)
        low = lane < FOX_DH
        live = (i * tq + lax.broadcasted_iota(jnp.int32, (tq, 1), 0)) >= N_PAD
        for pair in range(FOX_PAIRS):
            cols = slice(LANE * pair, LANE * (pair + 1))
            d_slab = dm_ref[:, cols]
            prod = d_slab * o_ref[:, cols]
            for e in range(2):
                h = 2 * pair + e
                nd = -jnp.sum(jnp.where(low, _head_to_low(prod, e), 0.0), axis=-1, keepdims=True)
                nd_hi = nd.astype(BF16).astype(F32)
                doa = jnp.where(low, _head_to_low(d_slab, e), 0.0)
                doa = jnp.where(lane == FOX_DH, nd_hi, jnp.where(lane == FOX_DH + 1, nd - nd_hi, doa))
                doa_ref[h] = doa.astype(BF16)
                lse_h = lse_ref[h // FWD_HEADS][:, h % FWD_HEADS:h % FWD_HEADS + 1]
                hi, mid, lo = _split3(jnp.where(live, -lse_h, 0.0))
                qab = qa_ref[h].astype(F32)
                qab = jnp.where(lane == L_LSE, hi, jnp.where(lane == L_LSE + 1, mid, jnp.where(lane == L_LSE + 2, lo, qab)))
                qab_ref[h] = qab.astype(BF16)

    out = jax.ShapeDtypeStruct((FOX_HEADS, t, LANE), BF16)
    hspec = pl.BlockSpec((FOX_HEADS, tq, LANE), lambda i: (0, i, 0))
    return _pcall(
        body, name=name, grid=(t // tq,),
        in_specs=[pl.BlockSpec((tq, FOX_W), lambda i: (i, 1)), pl.BlockSpec((tq, FOX_W), lambda i: (i, 0)),
                  pl.BlockSpec((FWD_GROUPS, tq, LANE), lambda i: (0, i, 0)), hspec],
        out_specs=[hspec, hspec], out_shape=[out, out],
        compiler_params=_params("parallel"),
    )(dmixed, o_fox, lse, qa)


def _fox_bwd(qab, doa, ka, va, dproj, name):
    nh, nq, tq, _ = qab.shape
    t = nq * tq
    slab = 3 * LANE * STEP_PAIRS
    group0 = (2 * RET_QK + 2 * RET_V) // slab

    def body(qab_ref, doa_ref, ka_ref, va_ref, dproj_in, dp_ref, drs_ref, dcs_ref, dq_ref):
        g, j = pl.program_id(0), pl.program_id(1)

        @pl.when((g == 0) & (j == 0))
        def _():
            drs_ref[...] = jnp.zeros_like(drs_ref)
            dcs_ref[...] = jnp.zeros_like(dcs_ref)

        @pl.when(j == 0)
        def _():
            dq_ref[...] = jnp.zeros_like(dq_ref)

        lane = lax.broadcasted_iota(jnp.int32, (tq, LANE), 1)
        low = lane < FOX_DH
        key_le_query = lax.broadcasted_iota(jnp.int32, (tq, tq), 0) <= lax.broadcasted_iota(jnp.int32, (tq, tq), 1)

        def by_head(c, a, b, col):
            h = STEP_HEADS * g + 2 * c
            return jnp.where(lane == h, a[:, col:col + 1], jnp.where(lane == h + 1, b[:, col:col + 1], 0.0))

        kbs = [ka_ref[h] for h in range(STEP_HEADS)]
        vbs = [va_ref[h] for h in range(STEP_HEADS)]

        def step(i, carry, diagonal):
            qbs = [qab_ref[h, i] for h in range(STEP_HEADS)]
            dobs = [doa_ref[h, i] for h in range(STEP_HEADS)]
            st = [_dot(kbs[h], qbs[h], NT) for h in range(STEP_HEADS)]
            dpt = [_dot(vbs[h], dobs[h], NT) for h in range(STEP_HEADS)]
            new = []
            for h in range(STEP_HEADS):
                p = jnp.exp(st[h])
                if diagonal:
                    p = jnp.where(key_le_query, p, 0.0)
                ds = (p * dpt[h]).astype(BF16)
                dq_ref[h, i] += _dot(ds, kbs[h], TN)
                dk, dv = carry[h]
                new.append((dk + _dot(ds, qbs[h]), dv + _dot(p.astype(BF16), dobs[h])))
            return tuple(new)

        zero = jnp.zeros((tq, LANE), F32)
        carry = step(j, tuple((zero, zero) for _ in range(STEP_HEADS)), True)
        carry = lax.fori_loop(j + 1, nq, lambda i, cr: step(i, cr, False), carry)
        rows = pl.ds(pl.multiple_of(j * tq, tq), tq)
        for c in range(STEP_PAIRS):
            (dka, dva), (dkb, dvb) = carry[2 * c], carry[2 * c + 1]
            c0 = 3 * LANE * c
            dp_ref[rows, c0 + LANE:c0 + 2 * LANE] = _pair(dka, dkb, low).astype(BF16)
            dp_ref[rows, c0 + 2 * LANE:c0 + 3 * LANE] = _pair(dva, dvb, low).astype(BF16)
            dcs_ref[rows, :] += by_head(c, dka, dkb, L_ONE_Q)

        @pl.when(j == nq - 1)
        def _():
            for c in range(STEP_PAIRS):
                for blk in range(nq):
                    r = slice(blk * tq, (blk + 1) * tq)
                    a, b = dq_ref[2 * c, blk], dq_ref[2 * c + 1, blk]
                    dp_ref[r, 3 * LANE * c:3 * LANE * c + LANE] = (_pair(a, b, low) * QK_SCALE).astype(BF16)
                    drs_ref[r, :] += by_head(c, a, b, L_ONE_K)

    whole = pl.BlockSpec((STEP_HEADS, nq, tq, LANE), lambda g, j: (g, 0, 0, 0), pipeline_mode=pl.Buffered(1))
    blk = pl.BlockSpec((STEP_HEADS, None, tq, LANE), lambda g, j: (g, j, 0, 0))
    sums = pl.BlockSpec((t, LANE), lambda g, j: (0, 0), pipeline_mode=pl.Buffered(1))
    return _pcall(
        body, name=name, grid=(FOX_GROUPS, nq),
        in_specs=[whole, whole, blk, blk, pl.BlockSpec(memory_space=pl.ANY)],
        out_specs=[pl.BlockSpec((t, slab), lambda g, j: (0, group0 + g)), sums, sums],
        out_shape=[jax.ShapeDtypeStruct(dproj.shape, BF16), jax.ShapeDtypeStruct((t, LANE), F32),
                   jax.ShapeDtypeStruct((t, LANE), F32)],
        input_output_aliases={4: 0},
        scratch_shapes=[pltpu.VMEM((STEP_HEADS, nq, tq, LANE), F32)],
        compiler_params=_params("arbitrary", "arbitrary"),
    )(qab, doa, ka, va, dproj)


HALO = 8


def _rows_ext(ref, r0, rows, t, before, after):
    lo, hi = r0 - before, r0 + rows + after
    width = ref.shape[-1]
    parts = []
    if lo < 0:
        parts.append(jnp.zeros((-lo, width), F32))
    parts.append(ref[max(lo, 0):min(hi, t), :].astype(F32))
    if hi > t:
        parts.append(jnp.zeros((hi - t, width), F32))
    return parts[0] if len(parts) == 1 else jnp.concatenate(parts, axis=0)


def _conv_taps(a_ext, r0_ext, cw_ref, cb_ref):
    n = a_ext.shape[0]
    if r0_ext < N_PAD:
        row = r0_ext + lax.broadcasted_iota(jnp.int32, (n, 1), 0)
        a_ext = jnp.where(row >= N_PAD, a_ext, 0.0)
    a1 = pltpu.roll(a_ext, 1, axis=0)
    a2 = pltpu.roll(a_ext, 2, axis=0)
    acc = cb_ref[...] + a2 * cw_ref[0:1, :] + a1 * cw_ref[1:2, :] + a_ext * cw_ref[2:3, :]
    return a_ext, a1, a2, acc


FF_COLS = 256


def _up_conv_fwd(n2, w_up_t, conv_w8, conv_b, name):
    t, d = n2.shape
    f = w_up_t.shape[1]
    rows = TOK_TILE
    starts = list(range(0, t, rows))

    def body(n_ref, wa_ref, wb_ref, cw_ref, cb_ref, up_ref, g_ref):
        wa, wb = wa_ref[...], wb_ref[...]

        def project(r0):
            n_rows = n_ref[r0:r0 + rows, :]
            up_ref[0, r0:r0 + rows, :] = _dot(n_rows, wa, NT)
            up_ref[1, r0:r0 + rows, :] = _dot(n_rows, wb, NT)

        def activate(r0):
            a_ext = _rows_ext(up_ref.at[0], r0, rows, t, HALO, 0)
            _, _, _, acc = _conv_taps(a_ext, r0 - HALO, cw_ref, cb_ref)
            acc = acc[HALO:, :]
            g_ref[r0:r0 + rows, :] = (acc * _sigmoid(acc) * up_ref[1, r0:r0 + rows, :]).astype(BF16)

        project(starts[0])
        for r0, r_next in zip(starts, starts[1:] + [None]):
            if r_next is not None:
                project(r_next)
            activate(r0)

    return _pcall(
        body, name=name, grid=(f // FF_COLS,),
        in_specs=[pl.BlockSpec((t, d), lambda j: (0, 0), pipeline_mode=pl.Buffered(1)),
                  pl.BlockSpec((None, FF_COLS, d), lambda j: (0, j, 0)), pl.BlockSpec((None, FF_COLS, d), lambda j: (1, j, 0)),
                  pl.BlockSpec((8, FF_COLS), lambda j: (0, j)), pl.BlockSpec((1, FF_COLS), lambda j: (0, j))],
        out_specs=[pl.BlockSpec((2, t, FF_COLS), lambda j: (0, 0, j)), pl.BlockSpec((t, FF_COLS), lambda j: (0, j))],
        out_shape=[jax.ShapeDtypeStruct((2, t, f), F32), jax.ShapeDtypeStruct((t, f), BF16)],
        compiler_params=_params("parallel"),
    )(n2, w_up_t, w_up_t, conv_w8, conv_b)


def _dg_conv_bwd(up, conv_w8, conv_b, dh2, w_down, name):
    _, t, f = up.shape
    d = dh2.shape[1]
    rows = TOK_TILE
    starts = list(range(0, t, rows))

    def body(a_ref, b_ref, cw_ref, cb_ref, dh_ref, wd_ref, dup_ref, gcw_ref, gcb_ref, dg_ref):
        wd = wd_ref[...]

        def project(r0):
            dg_ref[r0:r0 + rows, :] = _dot(dh_ref[r0:r0 + rows, :], wd, NT)

        gw = [jnp.zeros((1, FF_COLS), F32) for _ in range(3)]
        gb = jnp.zeros((1, FF_COLS), F32)
        project(starts[0])
        for r0, r_next in zip(starts, starts[1:] + [None]):
            if r_next is not None:
                project(r_next)
            a_ext = _rows_ext(a_ref, r0, rows, t, HALO, HALO)
            b_ext = _rows_ext(b_ref, r0, rows, t, HALO, HALO)
            dg_ext = _rows_ext(dg_ref, r0, rows, t, HALO, HALO)
            a0, a1, a2, acc = _conv_taps(a_ext, r0 - HALO, cw_ref, cb_ref)
            sg = _sigmoid(acc)
            dacc = dg_ext * b_ext * (sg * (1.0 + acc * (1.0 - sg)))
            n = dacc.shape[0]
            da = (dacc * cw_ref[2:3, :] + pltpu.roll(dacc, n - 1, axis=0) * cw_ref[1:2, :]
                  + pltpu.roll(dacc, n - 2, axis=0) * cw_ref[0:1, :])
            core = slice(HALO, HALO + rows)
            da = da[core, :]
            if r0 < N_PAD:
                row = r0 + lax.broadcasted_iota(jnp.int32, (rows, 1), 0)
                da = jnp.where(row >= N_PAD, da, 0.0)
            dup_ref[0, r0:r0 + rows, :] = da.astype(BF16)
            dup_ref[1, r0:r0 + rows, :] = (dg_ext * acc * sg)[core, :].astype(BF16)
            dacc_c = dacc[core, :]
            gw[0] = gw[0] + jnp.sum(dacc_c * a2[core, :], axis=0, keepdims=True)
            gw[1] = gw[1] + jnp.sum(dacc_c * a1[core, :], axis=0, keepdims=True)
            gw[2] = gw[2] + jnp.sum(dacc_c * a0[core, :], axis=0, keepdims=True)
            gb = gb + jnp.sum(dacc_c, axis=0, keepdims=True)
        gcw_ref[...] = jnp.zeros((8, FF_COLS), F32)
        for tap in range(3):
            gcw_ref[tap:tap + 1, :] = gw[tap]
        gcb_ref[...] = gb

    return _pcall(
        body, name=name, grid=(f // FF_COLS,),
        in_specs=[pl.BlockSpec((None, t, FF_COLS), lambda j: (0, 0, j)), pl.BlockSpec((None, t, FF_COLS), lambda j: (1, 0, j)),
                  pl.BlockSpec((8, FF_COLS), lambda j: (0, j)), pl.BlockSpec((1, FF_COLS), lambda j: (0, j)),
                  pl.BlockSpec((t, d), lambda j: (0, 0), pipeline_mode=pl.Buffered(1)),
                  pl.BlockSpec((FF_COLS, d), lambda j: (j, 0))],
        out_specs=[pl.BlockSpec((2, t, FF_COLS), lambda j: (0, 0, j)), pl.BlockSpec((8, FF_COLS), lambda j: (0, j)),
                   pl.BlockSpec((1, FF_COLS), lambda j: (0, j))],
        out_shape=[jax.ShapeDtypeStruct((2, t, f), BF16), jax.ShapeDtypeStruct((8, f), F32),
                   jax.ShapeDtypeStruct((1, f), F32)],
        scratch_shapes=[pltpu.VMEM((t, FF_COLS), F32)],
        compiler_params=_params("parallel"),
    )(up, up, conv_w8, conv_b, dh2, w_down)


def _exchange(arrays, kinds, name, after=None):
    n = len(arrays)
    npeer = N_DEV - 1
    n_in = n + int(after is not None)

    def body(*refs):
        ins, outs = refs[:n], refs[n_in:n_in + n]
        send_sems, recv_sems, local_sems = refs[n_in + n:]
        x, y, c = lax.axis_index("x"), lax.axis_index("y"), lax.axis_index("c")
        me = 4 * x + 2 * y + c
        copies, locals_ = [], []
        for a in range(n):
            gather = kinds[a] == "gather"
            own = pltpu.make_async_copy(ins[a] if gather else ins[a].at[me], outs[a].at[me], local_sems.at[a])
            own.start()
            locals_.append(own)
            for d in range(1, N_DEV):
                px = 1 - x if d & 4 else x
                py = 1 - y if d & 2 else y
                pc = 1 - c if d & 1 else c
                src = ins[a] if gather else ins[a].at[4 * px + 2 * py + pc]
                cp = pltpu.make_async_remote_copy(
                    src_ref=src, dst_ref=outs[a].at[me],
                    send_sem=send_sems.at[a * npeer + d - 1], recv_sem=recv_sems.at[a * npeer + d - 1],
                    device_id=(px, py, pc), device_id_type=pl.DeviceIdType.MESH)
                cp.start()
                copies.append(cp)
        for cp in copies:
            cp.wait_recv()
        for cp in copies:
            cp.wait_send()
        for own in locals_:
            own.wait()

    out_shape = [jax.ShapeDtypeStruct((N_DEV,) + (a.shape if k == "gather" else a.shape[1:]), a.dtype)
                 for a, k in zip(arrays, kinds)]
    return _pcall(
        body, name=name,
        in_specs=[pl.BlockSpec(memory_space=pl.ANY)] * n_in,
        out_specs=[pl.BlockSpec(memory_space=pl.ANY)] * n,
        out_shape=out_shape,
        scratch_shapes=[pltpu.SemaphoreType.DMA((n * npeer,)), pltpu.SemaphoreType.DMA((n * npeer,)),
                        pltpu.SemaphoreType.DMA((n,))],
        compiler_params=pltpu.CompilerParams(has_side_effects=True),
    )(*arrays, *([] if after is None else [after]))


def _peer_copies(srcs, lands, kinds, send_sems, recv_sems):
    x, y, c = lax.axis_index("x"), lax.axis_index("y"), lax.axis_index("c")
    me = 4 * x + 2 * y + c
    copies = []
    for a in range(len(srcs)):
        for d in range(1, N_DEV):
            px = 1 - x if d & 4 else x
            py = 1 - y if d & 2 else y
            pc = 1 - c if d & 1 else c
            k = a * (N_DEV - 1) + d - 1
            copies.append(pltpu.make_async_remote_copy(
                src_ref=srcs[a] if kinds[a] == "gather" else srcs[a].at[4 * px + 2 * py + pc], dst_ref=lands[a].at[me],
                send_sem=send_sems.at[k], recv_sem=recv_sems.at[k],
                device_id=(px, py, pc), device_id_type=pl.DeviceIdType.MESH))
    return copies


def _exchange_start(arrays, kinds, name, after=None):
    n = len(arrays)
    nsem = n * (N_DEV - 1)
    hbm = pl.BlockSpec(memory_space=pltpu.HBM)
    sem = pl.BlockSpec(memory_space=pltpu.SEMAPHORE)
    land_shapes = [(N_DEV,) + (a.shape if k == "gather" else a.shape[1:]) for a, k in zip(arrays, kinds)]

    n_in = 2 * n + int(after is not None)

    def body(*refs):
        srcs, lands = refs[:n], refs[n:2 * n]
        send_sems, recv_sems = refs[n_in], refs[n_in + 1]
        token = refs[-1]
        for cp in _peer_copies(srcs, lands, kinds, send_sems, recv_sems):
            cp.start()
        token[...] = jnp.zeros_like(token)

    operands = [pltpu.with_memory_space_constraint(a, pltpu.HBM) for a in arrays]
    operands += [pltpu.with_memory_space_constraint(lax.empty(s, a.dtype), pltpu.HBM) for s, a in zip(land_shapes, arrays)]
    operands += [] if after is None else [after]
    out = _pcall(
        body, name=name,
        in_specs=[hbm] * (2 * n) + ([] if after is None else [pl.BlockSpec(memory_space=pl.ANY)]),
        out_specs=[sem, sem] + [hbm] * (2 * n) + [pl.BlockSpec(memory_space=pltpu.VMEM)],
        out_shape=[pltpu.SemaphoreType.DMA((nsem,)), pltpu.SemaphoreType.DMA((nsem,))]
        + [pltpu.HBM(a.shape, a.dtype) for a in arrays]
        + [pltpu.HBM(s, a.dtype) for s, a in zip(land_shapes, arrays)]
        + [jax.ShapeDtypeStruct((8, LANE), F32)],
        input_output_aliases={k: 2 + k for k in range(2 * n)},
        compiler_params=pltpu.CompilerParams(has_side_effects=pltpu.SideEffectType.DATAFLOW_SIDE_EFFECTING),
    )(*operands)
    return out[0], out[1], list(out[2:2 + n]), list(out[2 + n:2 + 2 * n]), out[-1]


def _exchange_wait(started, kinds, after, name):
    send_sems, recv_sems, srcs, lands, _ = started
    n = len(srcs)
    hbm = pl.BlockSpec(memory_space=pltpu.HBM)
    sem = pl.BlockSpec(memory_space=pltpu.SEMAPHORE)

    def body(*refs):
        src_refs, land_refs = refs[:n], refs[n:2 * n]
        copies = _peer_copies(src_refs, land_refs, kinds, refs[2 * n], refs[2 * n + 1])
        for cp in copies:
            cp.wait_send()
        for cp in copies:
            cp.wait_recv()

    out = _pcall(
        body, name=name,
        in_specs=[hbm] * (2 * n) + [sem, sem, pl.BlockSpec(memory_space=pl.ANY)],
        out_specs=[hbm] * (2 * n),
        out_shape=[pltpu.HBM(a.shape, a.dtype) for a in srcs + lands],
        input_output_aliases={k: k for k in range(2 * n)},
        compiler_params=pltpu.CompilerParams(has_side_effects=pltpu.SideEffectType.DATAFLOW_SIDE_EFFECTING),
    )(*srcs, *lands, send_sems, recv_sems, after)
    me = 4 * lax.axis_index("x") + 2 * lax.axis_index("y") + lax.axis_index("c")
    filled = []
    for src, land, kind in zip(out[:n], out[n:], kinds):
        own = src if kind == "gather" else lax.dynamic_index_in_dim(src, me, axis=0, keepdims=False)
        filled.append(lax.dynamic_update_slice(land, own[None], (me,) + (0,) * own.ndim))
    return filled


def _sum_slots(slots, name, rows_tile):
    nd, r, c = slots.shape

    def body(s_ref, o_ref):
        acc = s_ref[0].astype(F32)
        for p in range(1, nd):
            acc = acc + s_ref[p].astype(F32)
        o_ref[...] = acc

    return _pcall(
        body, name=name, grid=(r // rows_tile,),
        in_specs=[pl.BlockSpec((nd, rows_tile, c), lambda i: (0, i, 0))],
        out_specs=pl.BlockSpec((rows_tile, c), lambda i: (i, 0)),
        out_shape=jax.ShapeDtypeStruct((r, c), F32),
        compiler_params=_params("parallel"),
    )(slots)


def _sum_slots_small(slot_arrays, name):
    n = len(slot_arrays)

    def body(*refs):
        for s_ref, o_ref in zip(refs[:n], refs[n:]):
            acc = s_ref[0]
            for p in range(1, s_ref.shape[0]):
                acc = acc + s_ref[p]
            o_ref[...] = acc

    return _pcall(body, name=name, out_shape=[jax.ShapeDtypeStruct(a.shape[1:], F32) for a in slot_arrays])(*slot_arrays)


def _adamw_update(w_ref, g_ref, m_ref, v_ref, d_ref, nm_ref, nv_ref):
    gr = g_ref[...]
    nm = ADAM_B1 * m_ref[...] + (1.0 - ADAM_B1) * gr
    nv = ADAM_B2 * v_ref[...] + (1.0 - ADAM_B2) * (gr * gr)
    m_hat = nm / (1.0 - ADAM_B1 ** ADAM_STEP)
    v_hat = nv / (1.0 - ADAM_B2 ** ADAM_STEP)
    d_ref[...] = -ADAM_LR * (m_hat / (jnp.sqrt(v_hat) + ADAM_EPS) + ADAM_WD * w_ref[...])
    nm_ref[...] = nm
    nv_ref[...] = nv


def _adamw_small(ws, gs, ms, vs, name):
    n = len(ws)

    def body(*refs):
        ins, outs = refs[:4 * n], refs[4 * n:]
        for k in range(n):
            _adamw_update(ins[k], ins[n + k], ins[2 * n + k], ins[3 * n + k], outs[k], outs[n + k], outs[2 * n + k])

    shapes = [jax.ShapeDtypeStruct(w.shape, F32) for w in ws]
    out = _pcall(body, name=name, out_shape=shapes * 3)(*ws, *gs, *ms, *vs)
    return list(out[:n]), list(out[n:2 * n]), list(out[2 * n:])


def _adamw(w, g, m, v, name, rows_tile):
    r, c = w.shape
    body = lambda *refs: _adamw_update(*refs)
    spec = pl.BlockSpec((rows_tile, c), lambda i: (i, 0))
    shp = jax.ShapeDtypeStruct((r, c), F32)
    return _pcall(
        body, name=name, grid=(r // rows_tile,), in_specs=[spec] * 4, out_specs=[spec] * 3, out_shape=[shp] * 3,
        compiler_params=_params("parallel"),
    )(w, g, m, v)


F0 = 2 * RET_QK + 2 * RET_V


def _to_internal_rows(w_t):
    cols = w_t.shape[1]
    fox = w_t[F0:F0 + 3 * FOX_W].reshape(3, FOX_PAIRS, LANE, cols).transpose(1, 0, 2, 3).reshape(3 * FOX_W, cols)
    tail = jnp.zeros((IN_PAD - IN_WIDTH, cols), w_t.dtype)
    return jnp.concatenate([w_t[:F0], fox, w_t[F0 + 3 * FOX_W:], tail], axis=0)


def _from_internal_rows(g_t):
    cols = g_t.shape[1]
    fox = g_t[F0:F0 + 3 * FOX_W].reshape(FOX_PAIRS, 3, LANE, cols).transpose(1, 0, 2, 3).reshape(3 * FOX_W, cols)
    return jnp.concatenate([g_t[:F0], fox, g_t[F0 + 3 * FOX_W:F0 + 3 * FOX_W + FOX_HEADS]], axis=0)


def _local_step(x, target, meta, attn_g, fox_b, ret_g, ffn_g, conv_w8, conv_b, final_g,
                first_weight, late_weights, ffn_grads_ready, out_grad_ready, in_grad_ready):
    seq, d = x.shape
    t = seq + PREFIX
    tm = TOK_TILE
    nq = t // tm
    fox_b128 = jnp.pad(fox_b, ((0, 0), (0, LANE - FOX_HEADS)))

    h0, n1 = _prep_norm(x, meta, attn_g, "prep_norm")
    w_in_t = first_weight(n1)
    proj = _mm_simple(n1, w_in_t, mode="nt", tm=tm, tn=IN_PAD, tk=d, out_dtype=F32, name="mm_in")
    cos, sin = _rope_tables(t)
    o_pre, mixed, states = _ret_fwd(proj, cos, sin, ret_g, "ret_fwd")
    c = _forget_cumsum(proj, fox_b128, "forget_cumsum")
    qa, ka, va = _fox_prep(proj, c, "fox_prep")
    by_block = lambda a: a.reshape(FOX_HEADS, nq, tm, LANE)
    mixed, o_fox, lse = _fox_fwd(by_block(qa), by_block(ka), by_block(va), mixed, "fox_fwd")
    w_out, w_up_t, w_down = late_weights(o_fox)
    tile = pl.BlockSpec((tm, d), lambda i: (i, 0))
    row_vec = pl.BlockSpec((1, d), lambda i: (0, 0))
    resident = lambda shape: pl.BlockSpec(shape, lambda i: (0,) * len(shape), pipeline_mode=pl.Buffered(1))
    acts = lambda dtype: jax.ShapeDtypeStruct((t, d), dtype)
    vec = jax.ShapeDtypeStruct((1, d), F32)

    def residual_and_norm(i, acc, ins, outs):
        h = acc + ins[0][...]
        outs[0][...] = h
        outs[1][...] = (h * lax.rsqrt(jnp.mean(h * h, axis=-1, keepdims=True) + EPS) * ins[1][...]).astype(BF16)

    h1, n2 = _matmul_rows([mixed], [tile], [w_out], [resident((d, d))], [h0, ffn_g], [tile, row_vec],
                          [tile, tile], [acts(F32), acts(BF16)], residual_and_norm, mode="nn", steps=nq, name="mm_out_norm")
    nf = D_FF // 1408
    up, g = _up_conv_fwd(n2, w_up_t, conv_w8, conv_b, "up_conv_fwd")

    def residual_loss_bwd(i, acc, ins, outs):
        loss_ref, dh_ref, dhb_ref, gg_ref = outs
        part, dh, gg = _loss_tile(i, acc + ins[0][...], jnp.concatenate([ins[1][...], ins[2][...], ins[3][...]], axis=0),
                                  ins[4][...])
        _accumulate(loss_ref, i, jnp.broadcast_to(part, loss_ref.shape))
        dh_ref[...] = dh
        dhb_ref[...] = dh.astype(BF16)
        _accumulate(gg_ref, i, gg)

    loss_tile, dh2, dh2_b, g_final = _matmul_rows(
        [g], [pl.BlockSpec((tm, D_FF), lambda i: (i, 0))], [w_down], [resident((D_FF, d))],
        [h1, target, target, target, final_g], [tile] + _shifted_row_specs(d) + [row_vec],
        [pl.BlockSpec((8, LANE), lambda i: (0, 0)), tile, tile, row_vec],
        [jax.ShapeDtypeStruct((8, LANE), F32), acts(F32), acts(BF16), vec], residual_loss_bwd,
        mode="nn", steps=nq, name="mm_down_loss")

    tkw = 2112 if t % 2112 == 0 else tm
    gw_down = _mm_simple(g, dh2_b, mode="tn", tm=1408, tn=d, tk=tkw, out_dtype=BF16, name="mm_gw_down")
    dup, g_conv_w8, g_conv_b = _dg_conv_bwd(up, conv_w8, conv_b, dh2_b, w_down, "dg_conv_bwd")

    half = lambda p: pl.BlockSpec((None, tm, D_FF), lambda i: (p, i, 0))
    half_w = lambda p: pl.BlockSpec((None, D_FF, d), lambda i: (p, 0, 0), pipeline_mode=pl.Buffered(1))
    gw_up_t = _matmul(
        dup, n2, mode="tn", grid=(2 * nf, 1, t // tkw),
        a_spec=pl.BlockSpec((None, tkw, 1408), lambda i, j, k: (i // nf, k, i % nf)),
        b_spec=pl.BlockSpec((tkw, d), lambda i, j, k: (k, 0)),
        o_spec=pl.BlockSpec((1408, d), lambda i, j, k: (i, 0)),
        out_shape=jax.ShapeDtypeStruct((2 * D_FF, d), BF16), name="mm_gw_up")
    def norm_bwd_and_mixer_grad(i, acc, ins, outs):
        dh, gg = _rms_bwd_tile(acc, ins[0][...], ins[1][...], ins[2][...])
        outs[0][...] = dh
        _accumulate(outs[1], i, gg)
        outs[2][...] = _dot(dh.astype(BF16), ins[3][...], NT)

    dh1, g_ffn, dmixed = _matmul_rows(
        [dup, dup], [half(0), half(1)], [w_up_t, w_up_t], [half_w(0), half_w(1)],
        [h1, ffn_g, dh2, w_out], [tile, row_vec, tile, resident((d, d))], [tile, row_vec, tile],
        [acts(F32), vec, acts(F32)], norm_bwd_and_mixer_grad,
        mode="nn", steps=nq, name="mm_dn2_norm_bwd", after=ffn_grads_ready(gw_down, gw_up_t))
    gw_out = _mm_simple(mixed, dh1, mode="tn", tm=d, tn=d, tk=tkw, out_dtype=BF16, name="mm_gw_out")
    dproj, g_ret = _ret_bwd(proj, cos, sin, ret_g + out_grad_ready(gw_out), dmixed, o_pre, states, "ret_bwd")
    qab, doa = _fox_prep_bwd(dmixed, o_fox, lse, qa, "fox_prep_bwd")
    dproj, drs, dcs = _fox_bwd(by_block(qab), by_block(doa), by_block(ka), by_block(va), dproj, "fox_bwd")
    dproj, g_fox_b = _forget_cumsum_bwd(proj, fox_b128, drs, dcs, dproj, "forget_cumsum_bwd")
    gw_in_t = _mm_simple(dproj, n1, mode="tn", tm=640, tn=d, tk=tkw, out_dtype=BF16, name="mm_gw_in")
    sent = in_grad_ready(gw_in_t)
    def input_grads(i, acc, ins, outs):
        gx_ref, gmeta_ref, gg_ref, buf_ref, sems = outs
        dh, gg = _rms_bwd_tile(acc, ins[0][...], ins[1][...], ins[2][...])
        _accumulate(gg_ref, i, gg)
        slot = i % 2

        def first_copy():
            return pltpu.make_async_copy(buf_ref.at[0, pl.ds(PREFIX, tm - PREFIX)], gx_ref.at[pl.ds(0, tm - PREFIX)],
                                         sems.at[0])

        def tile_copy(tile, buf_slot):
            rows = pl.ds(pl.multiple_of(tile * tm - PREFIX, PREFIX), tm)
            return pltpu.make_async_copy(buf_ref.at[buf_slot], gx_ref.at[rows], sems.at[buf_slot])

        @pl.when(i == 1)
        def _():
            first_copy().wait()

        @pl.when(i >= 2)
        def _():
            tile_copy(i - 1, 1 - slot).wait()

        buf_ref[slot] = dh

        @pl.when(i == 0)
        def _():
            gmeta_ref[...] = dh[N_PAD:PREFIX, :]
            first_copy().start()

        @pl.when(i > 0)
        def _():
            tile_copy(i, slot).start()

        @pl.when(i == nq - 1)
        def _():
            tile_copy(i, slot).wait()

    grad_x, g_meta, g_attn = _matmul_rows(
        [dproj], [pl.BlockSpec((tm, IN_PAD), lambda i: (i, 0))], [w_in_t], [resident((IN_PAD, d))],
        [h0, attn_g, dh1], [tile, row_vec, tile],
        [pl.BlockSpec(memory_space=pl.ANY), pl.BlockSpec((N_META, d), lambda i: (0, 0)), row_vec],
        [jax.ShapeDtypeStruct((seq, d), F32), jax.ShapeDtypeStruct((N_META, d), F32), vec], input_grads,
        mode="nn", steps=nq, name="mm_dn1_norm_bwd", after=sent,
        scratch=[pltpu.VMEM((2, tm, d), F32), pltpu.SemaphoreType.DMA((2,))])

    grads = dict(meta=g_meta, attn_g=g_attn, fox_b=g_fox_b, ret_g=g_ret,
                 ffn_g=g_ffn, conv_w=g_conv_w8, conv_b=g_conv_b, final_g=g_final)
    return loss_tile, grad_x, grads


def kernel(x, meta_tokens, attn_norm_g, w_in, fox_forget_b, ret_norm_g, w_out, ffn_norm_g, w_up, conv_w, conv_b, w_down, final_norm_g, loss_target, m_meta_tokens, m_attn_norm_g, m_w_in, m_fox_forget_b, m_ret_norm_g, m_w_out, m_ffn_norm_g, m_w_up, m_conv_w, m_conv_b, m_w_down, m_final_norm_g, v_meta_tokens, v_attn_norm_g, v_w_in, v_fox_forget_b, v_ret_norm_g, v_w_out, v_ffn_norm_g, v_w_up, v_conv_w, v_conv_b, v_w_down, v_final_norm_g):
    d = D_MODEL
    me = 4 * lax.axis_index("x") + 2 * lax.axis_index("y") + lax.axis_index("c")
    in_blk = IN_WIDTH // N_DEV
    in_blk_pad = 400
    up_blk = 2 * D_FF // N_DEV
    down_blk = D_FF // N_DEV
    cw_blk = D_FF // N_DEV

    w_in_loc = jnp.pad(w_in[0].T.astype(BF16), ((0, in_blk_pad - in_blk), (0, 0)))
    cw_loc = jnp.pad(conv_w[0], ((0, 5), (0, 384 - cw_blk)))
    g_meta, g_cw = _exchange([meta_tokens, cw_loc], ["gather"] * 2, "gather_small")
    first = _exchange_start([w_in_loc], ["gather"], "gather_in_start", after=g_meta)
    rest_loc = [(w_out[0] + first[-1][0:1, 0:1]).astype(BF16), w_up[0].T.astype(BF16), w_down[0].astype(BF16)]
    rest = _exchange_start(rest_loc, ["gather"] * 3, "gather_rest_start")
    meta_f = g_meta.transpose(1, 0, 2).reshape(N_META, d)
    conv_w8 = jnp.pad(g_cw[:, :3, :cw_blk].transpose(1, 0, 2).reshape(3, D_FF), ((0, 5), (0, 0)))
    pending = {}

    def first_weight(after):
        (g_in,) = _exchange_wait(first, ["gather"], after, "gather_in_wait")
        return _to_internal_rows(g_in[:, :in_blk].reshape(IN_WIDTH, d))

    def in_grad_ready(gw_in_t):
        blocks = _from_internal_rows(gw_in_t).reshape(N_DEV, in_blk, d)
        blocks = jnp.pad(blocks, ((0, 0), (0, in_blk_pad - in_blk), (0, 0)))
        pending["in"] = _exchange_start([blocks], ["scatter"], "grads_in_start")
        return pending["in"][-1][0:1, 0:1]

    def late_weights(after):
        g_out, g_up, g_down = _exchange_wait(rest, ["gather"] * 3, after, "gather_rest_wait")
        return g_out.reshape(d, d), g_up.reshape(2, D_FF, d), g_down.reshape(D_FF, d)

    def ffn_grads_ready(gw_down, gw_up_t):
        blocks = [gw_down.reshape(N_DEV, down_blk, d), gw_up_t.reshape(N_DEV, up_blk, d)]
        pending["ffn"] = _exchange_start(blocks, ["scatter"] * 2, "grads_ffn_start")
        return pending["ffn"][-1][0:1, 0:1]

    def out_grad_ready(gw_out):
        pending["out"] = _exchange_start([gw_out.reshape(N_DEV, d // N_DEV, d)], ["scatter"], "grads_out_start")
        return pending["out"][-1][0:1, 0:1]

    loss_tile, grad_x, gr = _local_step(
        x[0], loss_target[0], meta_f, attn_norm_g + rest[-1][0:1, 0:1], fox_forget_b, ret_norm_g, ffn_norm_g,
        conv_w8, conv_b, final_norm_g.reshape(1, d), first_weight, late_weights, ffn_grads_ready, out_grad_ready,
        in_grad_ready)

    r_down, r_up = _exchange_wait(pending["ffn"], ["scatter"] * 2, grad_x, "grads_ffn_wait")
    (r_out,) = _exchange_wait(pending["out"], ["scatter"], grad_x, "grads_out_wait")
    g_w_out = _sum_slots(r_out, "sum_w_out", d // N_DEV)
    g_w_up_t = _sum_slots(r_up, "sum_w_up", up_blk)
    g_w_down = _sum_slots(r_down, "sum_w_down", down_blk)
    as_t = lambda a: a[0].T
    from_t = lambda a: a.T[None]
    d_w_out, m_w_out_n, v_w_out_n = [a[None] for a in _adamw(w_out[0], g_w_out, m_w_out[0], v_w_out[0], "adamw_w_out", 128)]
    up_t = _adamw(as_t(w_up), g_w_up_t, as_t(m_w_up), as_t(v_w_up), "adamw_w_up", up_blk // 2)
    d_w_up, m_w_up_n, v_w_up_n = [from_t(a) for a in up_t]
    d_w_down, m_w_down_n, v_w_down_n = [a[None] for a in _adamw(w_down[0], g_w_down, m_w_down[0], v_w_down[0],
                                                                "adamw_w_down", down_blk)]

    small = [loss_tile, gr["attn_g"], gr["fox_b"], gr["ret_g"], gr["ffn_g"], gr["conv_b"], gr["final_g"],
             gr["meta"], gr["conv_w"]]
    r_small = _exchange(small, ["gather"] * len(small), "exchange_small", after=up_t[0])
    (loss_all, g_attn, g_fox_b128, g_ret, g_ffn, g_conv_b, g_final, g_meta_full, g_cw_full) = _sum_slots_small(
        r_small, "sum_small")
    loss = loss_all[0, 0]
    g_fox_b = g_fox_b128[:, :FOX_HEADS]
    g_meta_loc = lax.dynamic_slice(g_meta_full, (0, me * (d // N_DEV)), (N_META, d // N_DEV))
    g_cw_loc = lax.dynamic_slice(g_cw_full, (0, me * cw_blk), (3, cw_blk))

    (r_in,) = _exchange_wait(pending["in"], ["scatter"], r_small[0], "grads_in_wait")
    g_w_in_t = _sum_slots(r_in, "sum_w_in", in_blk_pad)[:in_blk]
    d_w_in, m_w_in_n, v_w_in_n = [from_t(a) for a in _adamw(as_t(w_in), g_w_in_t, as_t(m_w_in), as_t(v_w_in),
                                                            "adamw_w_in", in_blk)]
    g_w_in, g_w_up = g_w_in_t.T, g_w_up_t.T
    row = lambda a: a.reshape(1, d)
    sm_grads = [g_meta_loc, g_attn, g_fox_b, g_ret, g_ffn, g_cw_loc, g_conv_b, g_final]
    sm_w = [meta_tokens, attn_norm_g, fox_forget_b, ret_norm_g, ffn_norm_g, conv_w[0], conv_b, row(final_norm_g)]
    sm_m = [m_meta_tokens, m_attn_norm_g, m_fox_forget_b, m_ret_norm_g, m_ffn_norm_g, m_conv_w[0], m_conv_b,
            row(m_final_norm_g)]
    sm_v = [v_meta_tokens, v_attn_norm_g, v_fox_forget_b, v_ret_norm_g, v_ffn_norm_g, v_conv_w[0], v_conv_b,
            row(v_final_norm_g)]
    dl, ml, vl = [lst[:7] + [lst[7].reshape(d)] for lst in _adamw_small(sm_w, sm_grads, sm_m, sm_v, "adamw_small")]

    def by_weight(meta_, attn_, w_in_, fox_, ret_, w_out_, ffn_, w_up_, cw_, cb_, w_down_, final_):
        return (meta_, attn_, w_in_, fox_, ret_, w_out_, ffn_, w_up_, cw_[None], cb_, w_down_, final_)

    grads_out = by_weight(g_meta_loc, g_attn, g_w_in[None], g_fox_b, g_ret, g_w_out[None], g_ffn, g_w_up[None], g_cw_loc,
                          g_conv_b, g_w_down[None], g_final.reshape(d))
    delta_out = by_weight(dl[0], dl[1], d_w_in, dl[2], dl[3], d_w_out, dl[4], d_w_up, dl[5], dl[6], d_w_down, dl[7])
    m_out = by_weight(ml[0], ml[1], m_w_in_n, ml[2], ml[3], m_w_out_n, ml[4], m_w_up_n, ml[5], ml[6], m_w_down_n, ml[7])
    v_out = by_weight(vl[0], vl[1], v_w_in_n, vl[2], vl[3], v_w_out_n, vl[4], v_w_up_n, vl[5], vl[6], v_w_down_n, vl[7])
    return (loss, grad_x[None]) + grads_out + delta_out + m_out + v_out
```

```python
import numpy as np
import jax
import jax.numpy as jnp
from jax import lax
from jax.experimental import pallas as pl
from jax.experimental.pallas import tpu as pltpu

F32 = jnp.float32
BF16 = jnp.bfloat16

D_MODEL = 1024
N_META = 16
N_PAD = 112
PREFIX = 128
RET_HEADS = 4
RET_DK = 64
RET_DV = 128
FOX_HEADS = 8
FOX_DH = 64
D_FF = 2816
ROPE_BASE = 10000.0
EPS = 1e-6
NEG = -1e30
RET_QK = RET_HEADS * RET_DK
RET_V = RET_HEADS * RET_DV
FOX_W = FOX_HEADS * FOX_DH
IN_WIDTH = 2 * RET_QK + 2 * RET_V + 3 * FOX_W + FOX_HEADS
IN_PAD = 3200
FF_COL_BLOCK = (IN_WIDTH - FOX_HEADS) // 128
QK_SCALE = 0.125

ADAM_LR = 0.001
ADAM_B1 = 0.9
ADAM_B2 = 0.999
ADAM_EPS = 1e-08
ADAM_WD = 0.01
ADAM_STEP = 10

N_DEV = 8
LANE = 128
ROW_TILE = 128
TOK_TILE = 384

NN = (((1,), (0,)), ((), ()))
NT = (((1,), (1,)), ((), ()))
TN = (((0,), (0,)), ((), ()))


def _pcall(body, **kw):
    return pl.pallas_call(body, **kw)


def _params(*sem):
    return pltpu.CompilerParams(dimension_semantics=sem)


def _dot(a, b, dims=NN):
    return lax.dot_general(a, b, dims, preferred_element_type=F32)


def _sigmoid(x):
    return 0.5 * jnp.tanh(0.5 * x) + 0.5


def _matmul(a, b, *, mode, grid, a_spec, b_spec, o_spec, out_shape, name, add=None, add_spec=None, after=None):
    dims = {"nn": NN, "nt": NT, "tn": TN}[mode]
    nk = grid[2]
    has_add = add is not None
    a_list, b_list = (list(a), list(b)) if isinstance(a, (list, tuple)) else ([a], [b])
    a_specs, b_specs = (list(a_spec), list(b_spec)) if isinstance(a_spec, (list, tuple)) else ([a_spec], [b_spec])
    nt = len(a_list)
    n_in = 2 * nt + int(has_add) + int(after is not None)

    def body(*refs):
        a_refs, b_refs = refs[:nt], refs[nt:2 * nt]
        add_ref = refs[2 * nt] if has_add else None
        o_ref = refs[n_in]
        part = _dot(a_refs[0][...].astype(BF16), b_refs[0][...].astype(BF16), dims)
        for ar, br in zip(a_refs[1:], b_refs[1:]):
            part = part + _dot(ar[...].astype(BF16), br[...].astype(BF16), dims)

        def finish(acc):
            if has_add:
                acc = acc + add_ref[...]
            o_ref[...] = acc.astype(o_ref.dtype)

        if nk == 1:
            finish(part)
        else:
            acc_ref = refs[-1]
            k = pl.program_id(2)

            @pl.when(k == 0)
            def _():
                acc_ref[...] = part

            @pl.when(k > 0)
            def _():
                acc_ref[...] += part

            @pl.when(k == nk - 1)
            def _():
                finish(acc_ref[...])

    in_specs = a_specs + b_specs + ([add_spec] if has_add else [])
    args = tuple(a_list) + tuple(b_list) + ((add,) if has_add else ())
    if after is not None:
        in_specs, args = in_specs + [pl.BlockSpec(memory_space=pl.ANY)], args + (after,)
    scratch = [] if nk == 1 else [pltpu.VMEM(tuple(d for d in o_spec.block_shape if d is not None), F32)]
    return _pcall(
        body, name=name, grid=grid, in_specs=in_specs, out_specs=o_spec, out_shape=out_shape,
        scratch_shapes=scratch, compiler_params=_params("parallel", "parallel", "arbitrary"),
    )(*args)


def _mm_simple(a, b, *, mode, tm, tn, tk, out_dtype, name, add=None, after=None):
    if mode == "tn":
        K, M = a.shape
    else:
        M, K = a.shape
    N = b.shape[0] if mode == "nt" else b.shape[1]
    grid = (M // tm, N // tn, K // tk)
    resident = dict(pipeline_mode=pl.Buffered(1)) if (tn == N and tk == K) else {}
    a_spec = pl.BlockSpec((tk, tm), lambda i, j, k: (k, i)) if mode == "tn" else pl.BlockSpec((tm, tk), lambda i, j, k: (i, k))
    b_spec = (pl.BlockSpec((tn, tk), lambda i, j, k: (j, k), **resident) if mode == "nt"
              else pl.BlockSpec((tk, tn), lambda i, j, k: (k, j), **resident))
    o_spec = pl.BlockSpec((tm, tn), lambda i, j, k: (i, j))
    return _matmul(a, b, mode=mode, grid=grid, a_spec=a_spec, b_spec=b_spec, o_spec=o_spec,
                   out_shape=jax.ShapeDtypeStruct((M, N), out_dtype), name=name, add=add,
                   add_spec=o_spec if add is not None else None, after=after)


def _matmul_rows(a_list, a_specs, b_list, b_specs, extras, extra_specs, out_specs, out_shape, epilogue, *,
                 mode, steps, name, after=None, scratch=()):
    dims = {"nn": NN, "nt": NT}[mode]
    nt, ne = len(a_list), len(extras)
    n_in = 2 * nt + ne + int(after is not None)

    def body(*refs):
        acc = _dot(refs[0][...].astype(BF16), refs[nt][...].astype(BF16), dims)
        for k in range(1, nt):
            acc = acc + _dot(refs[k][...].astype(BF16), refs[nt + k][...].astype(BF16), dims)
        epilogue(pl.program_id(0), acc, refs[2 * nt:2 * nt + ne], refs[n_in:])

    in_specs = list(a_specs) + list(b_specs) + list(extra_specs)
    args = tuple(a_list) + tuple(b_list) + tuple(extras)
    if after is not None:
        in_specs, args = in_specs + [pl.BlockSpec(memory_space=pl.ANY)], args + (after,)
    return _pcall(body, name=name, grid=(steps,), in_specs=in_specs, out_specs=out_specs, out_shape=out_shape,
                  scratch_shapes=list(scratch), compiler_params=_params("arbitrary"))(*args)


def _rms_bwd_tile(dy, x, gain, dres):
    r = lax.rsqrt(jnp.mean(x * x, axis=-1, keepdims=True) + EPS)
    xhat = x * r
    u = dy * gain
    return dres + r * (u - xhat * jnp.mean(u * xhat, axis=-1, keepdims=True)), jnp.sum(dy * xhat, axis=0, keepdims=True)


def _loss_tile(i, x, tgt, gain):
    d = x.shape[-1]
    r = lax.rsqrt(jnp.mean(x * x, axis=-1, keepdims=True) + EPS)
    xhat = x * r
    counted = (i * TOK_TILE + lax.broadcasted_iota(jnp.int32, (TOK_TILE, 1), 0)) >= PREFIX
    err = jnp.where(counted, xhat * gain - tgt, 0.0)
    dy = err * (1.0 / d)
    u = dy * gain
    dh = r * (u - xhat * jnp.mean(u * xhat, axis=-1, keepdims=True))
    return 0.5 * jnp.sum(jnp.mean(err * err, axis=-1, keepdims=True)), dh, jnp.sum(dy * xhat, axis=0, keepdims=True)


def _accumulate(ref, i, part):
    @pl.when(i == 0)
    def _():
        ref[...] = part

    @pl.when(i > 0)
    def _():
        ref[...] += part


def _prep_norm(x, meta, gain, name):
    seq, d = x.shape
    t = seq + PREFIX

    def body(xa_ref, xb_ref, xc_ref, meta_ref, g_ref, h_ref, n_ref):
        i = pl.program_id(0)

        @pl.when(i == 0)
        def _():
            h_ref[0:N_PAD, :] = jnp.zeros((N_PAD, d), F32)
            h_ref[N_PAD:ROW_TILE, :] = meta_ref[...]

        @pl.when(i > 0)
        def _():
            h_ref[0:ROW_TILE, :] = xa_ref[...]

        h_ref[ROW_TILE:2 * ROW_TILE, :] = xb_ref[...]
        h_ref[2 * ROW_TILE:3 * ROW_TILE, :] = xc_ref[...]
        h = h_ref[...]
        r = lax.rsqrt(jnp.mean(h * h, axis=-1, keepdims=True) + EPS)
        n_ref[...] = (h * r * g_ref[...]).astype(BF16)

    return _pcall(
        body, name=name, grid=(t // TOK_TILE,),
        in_specs=_shifted_row_specs(d) + [pl.BlockSpec((N_META, d), lambda i: (0, 0)), pl.BlockSpec((1, d), lambda i: (0, 0))],
        out_specs=[pl.BlockSpec((TOK_TILE, d), lambda i: (i, 0)), pl.BlockSpec((TOK_TILE, d), lambda i: (i, 0))],
        out_shape=[jax.ShapeDtypeStruct((t, d), F32), jax.ShapeDtypeStruct((t, d), BF16)],
        compiler_params=_params("parallel"),
    )(x, x, x, meta, gain)


def _shifted_row_specs(d):
    blocks_per_tile = TOK_TILE // ROW_TILE
    return [pl.BlockSpec((ROW_TILE, d), lambda i, r=r: (jnp.maximum(blocks_per_tile * i + r, 0), 0)) for r in (-1, 0, 1)]


def _ret_consts(bk):
    gam = 1.0 - 2.0 ** (-5.0 - np.arange(RET_HEADS))
    n = np.arange(bk)
    same_or_earlier_chunk = (n[None, :] // 64) <= (n[:, None] // 64)
    w = gam[:, None, None] ** np.abs(n[:, None] - n[None, :])[None] * same_or_earlier_chunk[None]
    wq = gam[:, None] ** (n[None, :] + 1.0)
    wk = gam[:, None] ** (bk - 1.0 - n[None, :])
    mask = (np.arange(RET_QK)[None, :] // RET_DK) == np.arange(RET_HEADS)[:, None]
    return (jnp.asarray(w, F32), jnp.asarray(wq[:, :, None], F32), jnp.asarray(wk[:, :, None], F32),
            jnp.asarray(mask[:, None, :], F32), [float(g ** bk) for g in gam])


def _rope_tables(t):
    half = RET_DK // 2
    inv = 1.0 / (ROPE_BASE ** (jnp.arange(half, dtype=F32) / half))
    ang = jnp.arange(t).astype(F32)[:, None] * inv[None, :]
    cos, sin = jnp.cos(ang), jnp.sin(ang)
    return (jnp.tile(jnp.concatenate([cos, cos], axis=1), (1, RET_HEADS)),
            jnp.tile(jnp.concatenate([-sin, sin], axis=1), (1, RET_HEADS)))


def _swap_halves(x):
    outs = []
    for s in range(x.shape[1] // LANE):
        xs = x[:, LANE * s:LANE * (s + 1)]
        lane = lax.broadcasted_iota(jnp.int32, xs.shape, 1)
        outs.append(jnp.where((lane & 32) == 0, pltpu.roll(xs, LANE - 32, axis=1), pltpu.roll(xs, 32, axis=1)))
    return outs[0] if len(outs) == 1 else jnp.concatenate(outs, axis=1)


def _rope(x, cos, sin_signed):
    return x * cos + _swap_halves(x) * sin_signed


def _rope_t(dx, cos, sin_signed):
    return dx * cos + _swap_halves(dx * sin_signed)


def _ret_fwd(proj, cos, sin, gain, name):
    t = proj.shape[0]
    bk = TOK_TILE
    nb = t // bk
    w, wq, wk, mask, g_blk = _ret_consts(bk)

    def body(q_ref, k_ref, v_ref, rg_ref, cos_ref, sin_ref, w_ref, wq_ref, wk_ref, mask_ref, gain_ref,
             opre_ref, og_ref, st_ref, r_ref):
        i = pl.program_id(0)

        @pl.when(i == 0)
        def _():
            r_ref[...] = jnp.zeros_like(r_ref)

        c, s = cos_ref[...], sin_ref[...]
        valid = ((i * bk + lax.broadcasted_iota(jnp.int32, (bk, 1), 0)) >= N_PAD).astype(F32)
        qr = _rope(q_ref[...], c, s)
        kr = _rope(k_ref[...], c, s) * QK_SCALE * valid
        kb = kr.astype(BF16)
        for h in range(RET_HEADS):
            hm = mask_ref[h]
            cols = slice(RET_DV * h, RET_DV * (h + 1))
            vh = v_ref[:, cols].astype(BF16)
            r_prev = r_ref[h]
            st_ref[0, h] = r_prev
            sm = _dot((qr * hm).astype(BF16), kb, NT) * w_ref[h]
            o = _dot(sm.astype(BF16), vh) + _dot((qr * (hm * wq_ref[h])).astype(BF16), r_prev.astype(BF16))
            r_ref[h] = g_blk[h] * r_prev + _dot((kr * wk_ref[h]).astype(BF16), vh, TN)
            opre_ref[:, cols] = o
            rstd = lax.rsqrt(jnp.mean(o * o, axis=-1, keepdims=True) + EPS)
            rg = rg_ref[:, cols]
            og_ref[:, cols] = (o * rstd * gain_ref[:, cols] * (rg * _sigmoid(rg))).astype(BF16)

    full = lambda shape: pl.BlockSpec(shape, lambda i: (0,) * len(shape))
    return _pcall(
        body, name=name, grid=(nb,),
        in_specs=[pl.BlockSpec((bk, RET_QK), lambda i: (i, 0)), pl.BlockSpec((bk, RET_QK), lambda i: (i, 1)),
                  pl.BlockSpec((bk, RET_V), lambda i: (i, 1)), pl.BlockSpec((bk, RET_V), lambda i: (i, 2)),
                  pl.BlockSpec((bk, RET_QK), lambda i: (i, 0)), pl.BlockSpec((bk, RET_QK), lambda i: (i, 0)),
                  full((RET_HEADS, bk, bk)), full((RET_HEADS, bk, 1)), full((RET_HEADS, bk, 1)),
                  full((RET_HEADS, 1, RET_QK)), full((1, RET_V))],
        out_specs=[pl.BlockSpec((bk, RET_V), lambda i: (i, 0)), pl.BlockSpec((bk, RET_V), lambda i: (i, 0)),
                   pl.BlockSpec((1, RET_HEADS, RET_QK, RET_DV), lambda i: (i, 0, 0, 0))],
        out_shape=[jax.ShapeDtypeStruct((t, RET_V), F32), jax.ShapeDtypeStruct((t, RET_V + FOX_W), BF16),
                   jax.ShapeDtypeStruct((nb, RET_HEADS, RET_QK, RET_DV), F32)],
        scratch_shapes=[pltpu.VMEM((RET_HEADS, RET_QK, RET_DV), F32)],
        compiler_params=_params("arbitrary"),
    )(proj, proj, proj, proj, cos, sin, w, wq, wk, mask, gain)


def _ret_bwd(proj, cos, sin, gain, dmixed, opre, states, name):
    t = proj.shape[0]
    bk = TOK_TILE
    nb = t // bk
    w, wq, wk, mask, g_blk = _ret_consts(bk)
    v0, g0 = 2 * RET_QK, 2 * RET_QK + RET_V

    def body(q_ref, k_ref, v_ref, rg_ref, cos_ref, sin_ref, w_ref, wq_ref, wk_ref, mask_ref, gain_ref,
             dog_ref, opre_ref, st_ref, dp_ref, gg_ref, dr_ref):
        step = pl.program_id(0)
        i = nb - 1 - step

        @pl.when(step == 0)
        def _():
            dr_ref[...] = jnp.zeros_like(dr_ref)
            gg_ref[...] = jnp.zeros_like(gg_ref)

        c, s = cos_ref[...], sin_ref[...]
        valid = ((i * bk + lax.broadcasted_iota(jnp.int32, (bk, 1), 0)) >= N_PAD).astype(F32)
        qr = _rope(q_ref[...], c, s)
        kr = _rope(k_ref[...], c, s) * QK_SCALE * valid
        kb = kr.astype(BF16)
        dqr = jnp.zeros((bk, RET_QK), F32)
        dkr = jnp.zeros((bk, RET_QK), F32)
        for h in range(RET_HEADS):
            hm = mask_ref[h]
            cols = slice(RET_DV * h, RET_DV * (h + 1))
            vh = v_ref[:, cols].astype(BF16)
            o = opre_ref[:, cols]
            rstd = lax.rsqrt(jnp.mean(o * o, axis=-1, keepdims=True) + EPS)
            xhat = o * rstd
            rg = rg_ref[:, cols]
            sg = _sigmoid(rg)
            gate = rg * sg
            gn = gain_ref[:, cols]
            dog = dog_ref[:, cols]
            dp_ref[:, g0 + RET_DV * h:g0 + RET_DV * (h + 1)] = (
                dog * xhat * gn * (sg * (1.0 + rg * (1.0 - sg)))).astype(BF16)
            gg_ref[:, cols] += jnp.sum(dog * xhat * gate, axis=0, keepdims=True)
            dxh = dog * gn * gate
            do = (rstd * (dxh - xhat * jnp.mean(dxh * xhat, axis=-1, keepdims=True))).astype(BF16)
            qm = (qr * hm).astype(BF16)
            qw = (qr * (hm * wq_ref[h])).astype(BF16)
            kw = (kr * wk_ref[h]).astype(BF16)
            wh = w_ref[h]
            sm = (_dot(qm, kb, NT) * wh).astype(BF16)
            ds = (_dot(do, vh, NT) * wh).astype(BF16)
            dr = dr_ref[h]
            drb = dr.astype(BF16)
            dp_ref[:, v0 + RET_DV * h:v0 + RET_DV * (h + 1)] = (_dot(sm, do, TN) + _dot(kw, drb)).astype(BF16)
            dqr = dqr + _dot(ds, kb) * hm + _dot(do, st_ref[0, h].astype(BF16), NT) * (hm * wq_ref[h])
            dkr = dkr + _dot(ds, qm, TN) + _dot(vh, drb, NT) * wk_ref[h]
            dr_ref[h] = g_blk[h] * dr + _dot(qw, do, TN)
        dp_ref[:, 0:RET_QK] = _rope_t(dqr, c, s).astype(BF16)
        dp_ref[:, RET_QK:2 * RET_QK] = _rope_t(dkr * (QK_SCALE * valid), c, s).astype(BF16)

    full = lambda shape: pl.BlockSpec(shape, lambda i: (0,) * len(shape))
    rev = lambda col: (lambda i: (nb - 1 - i, col))
    return _pcall(
        body, name=name, grid=(nb,),
        in_specs=[pl.BlockSpec((bk, RET_QK), rev(0)), pl.BlockSpec((bk, RET_QK), rev(1)),
                  pl.BlockSpec((bk, RET_V), rev(1)), pl.BlockSpec((bk, RET_V), rev(2)),
                  pl.BlockSpec((bk, RET_QK), rev(0)), pl.BlockSpec((bk, RET_QK), rev(0)),
                  full((RET_HEADS, bk, bk)), full((RET_HEADS, bk, 1)), full((RET_HEADS, bk, 1)),
                  full((RET_HEADS, 1, RET_QK)), full((1, RET_V)),
                  pl.BlockSpec((bk, RET_V), rev(0)), pl.BlockSpec((bk, RET_V), rev(0)),
                  pl.BlockSpec((1, RET_HEADS, RET_QK, RET_DV), lambda i: (nb - 1 - i, 0, 0, 0))],
        out_specs=[pl.BlockSpec((bk, g0 + RET_V), rev(0)), pl.BlockSpec((1, RET_V), lambda i: (0, 0))],
        out_shape=[jax.ShapeDtypeStruct((t, IN_PAD), BF16), jax.ShapeDtypeStruct((1, RET_V), F32)],
        scratch_shapes=[pltpu.VMEM((RET_HEADS, RET_QK, RET_DV), F32)],
        compiler_params=_params("arbitrary"),
    )(proj, proj, proj, proj, cos, sin, w, wq, wk, mask, gain, dmixed, opre, states)


def _forget_cumsum(proj, bias, name):
    t = proj.shape[0]
    rt = TOK_TILE
    nb = t // rt
    tril = jnp.asarray(np.tril(np.ones((rt, rt))), F32)

    def body(z_ref, b_ref, tril_ref, c_ref, carry_ref):
        i = pl.program_id(0)

        @pl.when(i == 0)
        def _():
            carry_ref[...] = jnp.zeros_like(carry_ref)

        z = z_ref[...] + b_ref[...]
        logf = jnp.minimum(z, 0.0) - jnp.log(1.0 + jnp.exp(-jnp.abs(z)))
        c = lax.dot_general(tril_ref[...], logf, NN, precision=lax.Precision.HIGHEST,
                            preferred_element_type=F32) + carry_ref[...]
        c_ref[...] = c
        carry_ref[...] = c[rt - 1:rt, :]

    return _pcall(
        body, name=name, grid=(nb,),
        in_specs=[pl.BlockSpec((rt, LANE), lambda i: (i, FF_COL_BLOCK)), pl.BlockSpec((1, LANE), lambda i: (0, 0)),
                  pl.BlockSpec((rt, rt), lambda i: (0, 0))],
        out_specs=pl.BlockSpec((rt, LANE), lambda i: (i, 0)),
        out_shape=jax.ShapeDtypeStruct((t, LANE), F32),
        scratch_shapes=[pltpu.VMEM((1, LANE), F32)],
        compiler_params=_params("arbitrary"),
    )(proj, bias, tril)


def _forget_cumsum_bwd(proj, bias, drs, dcs, dproj, name):
    t = proj.shape[0]
    rt = TOK_TILE
    nb = t // rt
    triu = jnp.asarray(np.triu(np.ones((rt, rt))), F32)

    def body(z_ref, b_ref, triu_ref, drs_ref, dcs_ref, dproj_in, dz_ref, gb_ref, carry_ref):
        step = pl.program_id(0)

        @pl.when(step == 0)
        def _():
            carry_ref[...] = jnp.zeros_like(carry_ref)
            gb_ref[...] = jnp.zeros_like(gb_ref)

        dlogf = lax.dot_general(triu_ref[...], drs_ref[...] - dcs_ref[...], NN, precision=lax.Precision.HIGHEST,
                                preferred_element_type=F32) + carry_ref[...]
        carry_ref[...] = dlogf[0:1, :]
        z = z_ref[...] + b_ref[...]
        is_head = lax.broadcasted_iota(jnp.int32, (rt, LANE), 1) < FOX_HEADS
        dz = jnp.where(is_head, dlogf / (1.0 + jnp.exp(z)), 0.0)
        dz_ref[...] = dz.astype(BF16)
        gb_ref[...] += jnp.sum(dz, axis=0, keepdims=True)

    return _pcall(
        body, name=name, grid=(nb,),
        in_specs=[pl.BlockSpec((rt, LANE), lambda i: (nb - 1 - i, FF_COL_BLOCK)),
                  pl.BlockSpec((1, LANE), lambda i: (0, 0)),
                  pl.BlockSpec((rt, rt), lambda i: (0, 0)),
                  pl.BlockSpec((rt, LANE), lambda i: (nb - 1 - i, 0)),
                  pl.BlockSpec((rt, LANE), lambda i: (nb - 1 - i, 0)),
                  pl.BlockSpec(memory_space=pl.ANY)],
        out_specs=[pl.BlockSpec((rt, LANE), lambda i: (nb - 1 - i, FF_COL_BLOCK)),
                   pl.BlockSpec((1, LANE), lambda i: (0, 0))],
        out_shape=[jax.ShapeDtypeStruct(dproj.shape, BF16), jax.ShapeDtypeStruct((1, LANE), F32)],
        input_output_aliases={5: 0},
        scratch_shapes=[pltpu.VMEM((1, LANE), F32)],
        compiler_params=_params("arbitrary"),
    )(proj, bias, triu, drs, dcs, dproj)


FOX_PAIRS = FOX_HEADS // 2
L_ONE_Q = FOX_DH
L_ONE_K = FOX_DH + 3
L_LSE = FOX_DH + 4


def _split3(x):
    hi = x.astype(BF16).astype(F32)
    r = x - hi
    mid = r.astype(BF16).astype(F32)
    return hi, mid, r - mid


def _head_to_low(slab, e):
    return slab if e == 0 else pltpu.roll(slab, FOX_DH, axis=1)


def _pair(a, b, low):
    return jnp.where(low, a, pltpu.roll(b, FOX_DH, axis=1))


def _fox_prep(proj, c, name):
    t = proj.shape[0]
    tq = TOK_TILE

    def body(p_ref, c_ref, qa_ref, ka_ref, va_ref, qt_ref, vt_ref):
        i = pl.program_id(0)
        lane = lax.broadcasted_iota(jnp.int32, (tq, LANE), 1)
        low = lane < FOX_DH
        live = (i * tq + lax.broadcasted_iota(jnp.int32, (tq, 1), 0)) >= N_PAD
        q_tail = jnp.where(lane < L_ONE_Q + 3, 1.0, 0.0)
        k_ones = (lane >= L_ONE_K) & (lane < L_ONE_K + 4)
        v_tail = jnp.where(lane < FOX_DH + 2, 1.0, 0.0)
        for pair in range(FOX_PAIRS):
            base = 3 * LANE * pair
            for e in range(2):
                h = 2 * pair + e
                q = _head_to_low(p_ref[:, base:base + LANE], e)
                k = _head_to_low(p_ref[:, base + LANE:base + 2 * LANE], e)
                v = _head_to_low(p_ref[:, base + 2 * LANE:base + 3 * LANE], e)
                hi, mid, lo = _split3(jnp.where(live, -c_ref[:, h:h + 1], NEG))
                ka = jnp.where(low, k, jnp.where(k_ones, 1.0, 0.0))
                ka = jnp.where(lane == L_ONE_Q, hi, jnp.where(lane == L_ONE_Q + 1, mid, jnp.where(lane == L_ONE_Q + 2, lo, ka)))
                qa = jnp.where(low, q * QK_SCALE, q_tail)
                va = jnp.where(low, v, v_tail)
                qa_ref[h] = qa.astype(BF16)
                ka_ref[h] = ka.astype(BF16)
                va_ref[h] = va.astype(BF16)
                qt_ref[h] = qa.T.astype(BF16)
                vt_ref[h] = va.T.astype(BF16)

    out = jax.ShapeDtypeStruct((FOX_HEADS, t, LANE), BF16)
    out_t = jax.ShapeDtypeStruct((FOX_HEADS, t // tq, LANE, tq), BF16)
    ospec = pl.BlockSpec((FOX_HEADS, tq, LANE), lambda i: (0, i, 0))
    tspec = pl.BlockSpec((FOX_HEADS, None, LANE, tq), lambda i: (0, i, 0, 0))
    return _pcall(
        body, name=name, grid=(t // tq,),
        in_specs=[pl.BlockSpec((tq, 3 * FOX_W), lambda i: (i, 1)), pl.BlockSpec((tq, LANE), lambda i: (i, 0))],
        out_specs=[ospec, ospec, ospec, tspec, tspec], out_shape=[out, out, out, out_t, out_t],
        compiler_params=_params("parallel"),
    )(proj, c)


STEP_PAIRS = 2
STEP_HEADS = 2 * STEP_PAIRS
FOX_GROUPS = FOX_PAIRS // STEP_PAIRS
FWD_PAIRS = 4
FWD_HEADS = 2 * FWD_PAIRS
FWD_GROUPS = FOX_PAIRS // FWD_PAIRS


def _blockdiag(a, b):
    z = jnp.zeros_like(a)
    return jnp.concatenate([jnp.concatenate([a, z], axis=1), jnp.concatenate([z, b], axis=1)], axis=0)


def _fox_fwd(qt, ka, vt, mixed, name):
    nh, nq, tq, _ = ka.shape
    t = nq * tq

    def body(qt_ref, ka_ref, vt_ref, mixed_in, mixed_ref, o_ref, lse_ref):
        i = pl.program_id(1)
        lane = lax.broadcasted_iota(jnp.int32, (tq, LANE), 1)
        key_le_query = lax.broadcasted_iota(jnp.int32, (tq, tq), 0) <= lax.broadcasted_iota(jnp.int32, (tq, tq), 1)
        qts = [qt_ref[h] for h in range(FWD_HEADS)]

        def logits(j):
            return [_dot(ka_ref[h, j], qts[h]) for h in range(FWD_HEADS)]

        def update(j, scores, carry, diagonal):
            new = []
            for h in range(FWD_HEADS):
                m, acc = carry[h]
                s = jnp.where(key_le_query, scores[h], NEG) if diagonal else scores[h]
                m_new = jnp.maximum(m, jnp.max(s, axis=0, keepdims=True))
                p = jnp.exp(s - m_new).astype(BF16)
                new.append((m_new, jnp.exp(m - m_new) * acc + _dot(vt_ref[h, j], p)))
            return tuple(new)

        init = tuple((jnp.full((1, tq), NEG, F32), jnp.zeros((LANE, tq), F32)) for _ in range(FWD_HEADS))
        carry = lax.fori_loop(0, i, lambda j, cr: update(j, logits(j), cr, False), init)
        outs, lse_rows = [], []
        for m, acc in update(i, logits(i), carry, True):
            l = acc[FOX_DH:FOX_DH + 1, :]
            outs.append((acc / l).T)
            lse_rows.append(m + jnp.log(l))
        lse_rows.append(jnp.zeros((LANE - FWD_HEADS, tq), F32))
        o_all = jnp.concatenate([_pair(outs[2 * c], outs[2 * c + 1], lane < FOX_DH) for c in range(FWD_PAIRS)], axis=1)
        mixed_ref[...] = o_all.astype(BF16)
        o_ref[...] = o_all
        lse_ref[...] = jnp.concatenate(lse_rows, axis=0).T

    width = FWD_PAIRS * LANE
    whole = pl.BlockSpec((FWD_HEADS, nq, tq, LANE), lambda g, i: (g, 0, 0, 0), pipeline_mode=pl.Buffered(1))
    whole_t = pl.BlockSpec((FWD_HEADS, nq, LANE, tq), lambda g, i: (g, 0, 0, 0), pipeline_mode=pl.Buffered(1))
    return _pcall(
        body, name=name, grid=(FWD_GROUPS, nq),
        in_specs=[pl.BlockSpec((FWD_HEADS, None, LANE, tq), lambda g, i: (g, i, 0, 0)), whole, whole_t,
                  pl.BlockSpec(memory_space=pl.ANY)],
        out_specs=[pl.BlockSpec((tq, width), lambda g, i: (i, RET_V // width + g)),
                   pl.BlockSpec((tq, width), lambda g, i: (i, g)),
                   pl.BlockSpec((None, tq, LANE), lambda g, i: (g, i, 0))],
        out_shape=[jax.ShapeDtypeStruct(mixed.shape, BF16), jax.ShapeDtypeStruct((t, FOX_W), F32),
                   jax.ShapeDtypeStruct((FWD_GROUPS, t, LANE), F32)],
        input_output_aliases={3: 0},
        compiler_params=_params("parallel", "parallel"),
    )(qt, ka, vt, mixed)


def _fox_prep_bwd(dmixed, o_fox, lse, qa, name):
    t = dmixed.shape[0]
    tq = TOK_TILE

    def body(dm_ref, o_ref, lse_ref, qa_ref, qab_ref, doa_ref):
        i = pl.program_id(0)
        lane = lax.broadcasted_iota(jnp.int32, (tq, LANE), 1)
        low = lane < FOX_DH
        live = (i * tq + lax.broadcasted_iota(jnp.int32, (tq, 1), 0)) >= N_PAD
        for pair in range(FOX_PAIRS):
            cols = slice(LANE * pair, LANE * (pair + 1))
            d_slab = dm_ref[:, cols]
            prod = d_slab * o_ref[:, cols]
            for e in range(2):
                h = 2 * pair + e
                nd = -jnp.sum(jnp.where(low, _head_to_low(prod, e), 0.0), axis=-1, keepdims=True)
                nd_hi = nd.astype(BF16).astype(F32)
                doa = jnp.where(low, _head_to_low(d_slab, e), 0.0)
                doa = jnp.where(lane == FOX_DH, nd_hi, jnp.where(lane == FOX_DH + 1, nd - nd_hi, doa))
                doa_ref[h] = doa.astype(BF16)
                lse_h = lse_ref[h // FWD_HEADS][:, h % FWD_HEADS:h % FWD_HEADS + 1]
                hi, mid, lo = _split3(jnp.where(live, -lse_h, 0.0))
                qab = qa_ref[h].astype(F32)
                qab = jnp.where(lane == L_LSE, hi, jnp.where(lane == L_LSE + 1, mid, jnp.where(lane == L_LSE + 2, lo, qab)))
                qab_ref[h] = qab.astype(BF16)

    out = jax.ShapeDtypeStruct((FOX_HEADS, t, LANE), BF16)
    hspec = pl.BlockSpec((FOX_HEADS, tq, LANE), lambda i: (0, i, 0))
    return _pcall(
        body, name=name, grid=(t // tq,),
        in_specs=[pl.BlockSpec((tq, FOX_W), lambda i: (i, 1)), pl.BlockSpec((tq, FOX_W), lambda i: (i, 0)),
                  pl.BlockSpec((FWD_GROUPS, tq, LANE), lambda i: (0, i, 0)), hspec],
        out_specs=[hspec, hspec], out_shape=[out, out],
        compiler_params=_params("parallel"),
    )(dmixed, o_fox, lse, qa)


def _fox_bwd(qab, doa, ka, va, dproj, name):
    nh, nq, tq, _ = qab.shape
    t = nq * tq
    slab = 3 * LANE * STEP_PAIRS
    group0 = (2 * RET_QK + 2 * RET_V) // slab

    def body(qab_ref, doa_ref, ka_ref, va_ref, dproj_in, dp_ref, drs_ref, dcs_ref, dq_ref):
        g, j = pl.program_id(0), pl.program_id(1)

        @pl.when((g == 0) & (j == 0))
        def _():
            drs_ref[...] = jnp.zeros_like(drs_ref)
            dcs_ref[...] = jnp.zeros_like(dcs_ref)

        @pl.when(j == 0)
        def _():
            dq_ref[...] = jnp.zeros_like(dq_ref)

        lane = lax.broadcasted_iota(jnp.int32, (tq, LANE), 1)
        low = lane < FOX_DH
        key_le_query = lax.broadcasted_iota(jnp.int32, (tq, tq), 0) <= lax.broadcasted_iota(jnp.int32, (tq, tq), 1)

        def by_head(c, a, b, col):
            h = STEP_HEADS * g + 2 * c
            return jnp.where(lane == h, a[:, col:col + 1], jnp.where(lane == h + 1, b[:, col:col + 1], 0.0))

        kbs = [ka_ref[h] for h in range(STEP_HEADS)]
        vbs = [va_ref[h] for h in range(STEP_HEADS)]

        def step(i, carry, diagonal):
            qbs = [qab_ref[h, i] for h in range(STEP_HEADS)]
            dobs = [doa_ref[h, i] for h in range(STEP_HEADS)]
            st = [_dot(kbs[h], qbs[h], NT) for h in range(STEP_HEADS)]
            dpt = [_dot(vbs[h], dobs[h], NT) for h in range(STEP_HEADS)]
            new = []
            for h in range(STEP_HEADS):
                p = jnp.exp(st[h])
                if diagonal:
                    p = jnp.where(key_le_query, p, 0.0)
                ds = (p * dpt[h]).astype(BF16)
                dq_ref[h, i] += _dot(ds, kbs[h], TN)
                dk, dv = carry[h]
                new.append((dk + _dot(ds, qbs[h]), dv + _dot(p.astype(BF16), dobs[h])))
            return tuple(new)

        zero = jnp.zeros((tq, LANE), F32)
        carry = step(j, tuple((zero, zero) for _ in range(STEP_HEADS)), True)
        carry = lax.fori_loop(j + 1, nq, lambda i, cr: step(i, cr, False), carry)
        rows = pl.ds(pl.multiple_of(j * tq, tq), tq)
        for c in range(STEP_PAIRS):
            (dka, dva), (dkb, dvb) = carry[2 * c], carry[2 * c + 1]
            c0 = 3 * LANE * c
            dp_ref[rows, c0 + LANE:c0 + 2 * LANE] = _pair(dka, dkb, low).astype(BF16)
            dp_ref[rows, c0 + 2 * LANE:c0 + 3 * LANE] = _pair(dva, dvb, low).astype(BF16)
            dcs_ref[rows, :] += by_head(c, dka, dkb, L_ONE_Q)

        @pl.when(j == nq - 1)
        def _():
            for c in range(STEP_PAIRS):
                for blk in range(nq):
                    r = slice(blk * tq, (blk + 1) * tq)
                    a, b = dq_ref[2 * c, blk], dq_ref[2 * c + 1, blk]
                    dp_ref[r, 3 * LANE * c:3 * LANE * c + LANE] = (_pair(a, b, low) * QK_SCALE).astype(BF16)
                    drs_ref[r, :] += by_head(c, a, b, L_ONE_K)

    whole = pl.BlockSpec((STEP_HEADS, nq, tq, LANE), lambda g, j: (g, 0, 0, 0), pipeline_mode=pl.Buffered(1))
    blk = pl.BlockSpec((STEP_HEADS, None, tq, LANE), lambda g, j: (g, j, 0, 0))
    sums = pl.BlockSpec((t, LANE), lambda g, j: (0, 0), pipeline_mode=pl.Buffered(1))
    return _pcall(
        body, name=name, grid=(FOX_GROUPS, nq),
        in_specs=[whole, whole, blk, blk, pl.BlockSpec(memory_space=pl.ANY)],
        out_specs=[pl.BlockSpec((t, slab), lambda g, j: (0, group0 + g)), sums, sums],
        out_shape=[jax.ShapeDtypeStruct(dproj.shape, BF16), jax.ShapeDtypeStruct((t, LANE), F32),
                   jax.ShapeDtypeStruct((t, LANE), F32)],
        input_output_aliases={4: 0},
        scratch_shapes=[pltpu.VMEM((STEP_HEADS, nq, tq, LANE), F32)],
        compiler_params=_params("arbitrary", "arbitrary"),
    )(qab, doa, ka, va, dproj)


HALO = 8


def _rows_ext(ref, r0, rows, t, before, after):
    lo, hi = r0 - before, r0 + rows + after
    width = ref.shape[-1]
    parts = []
    if lo < 0:
        parts.append(jnp.zeros((-lo, width), F32))
    parts.append(ref[max(lo, 0):min(hi, t), :].astype(F32))
    if hi > t:
        parts.append(jnp.zeros((hi - t, width), F32))
    return parts[0] if len(parts) == 1 else jnp.concatenate(parts, axis=0)


def _conv_taps(a_ext, r0_ext, cw_ref, cb_ref):
    n = a_ext.shape[0]
    if r0_ext < N_PAD:
        row = r0_ext + lax.broadcasted_iota(jnp.int32, (n, 1), 0)
        a_ext = jnp.where(row >= N_PAD, a_ext, 0.0)
    a1 = pltpu.roll(a_ext, 1, axis=0)
    a2 = pltpu.roll(a_ext, 2, axis=0)
    acc = cb_ref[...] + a2 * cw_ref[0:1, :] + a1 * cw_ref[1:2, :] + a_ext * cw_ref[2:3, :]
    return a_ext, a1, a2, acc


FF_COLS = 256


def _up_conv_fwd(n2, w_up_t, conv_w8, conv_b, name):
    t, d = n2.shape
    f = w_up_t.shape[1]
    rows = TOK_TILE
    starts = list(range(0, t, rows))

    def body(n_ref, wa_ref, wb_ref, cw_ref, cb_ref, up_ref, g_ref):
        wa, wb = wa_ref[...], wb_ref[...]

        def project(r0):
            n_rows = n_ref[r0:r0 + rows, :]
            up_ref[0, r0:r0 + rows, :] = _dot(n_rows, wa, NT)
            up_ref[1, r0:r0 + rows, :] = _dot(n_rows, wb, NT)

        def activate(r0):
            a_ext = _rows_ext(up_ref.at[0], r0, rows, t, HALO, 0)
            _, _, _, acc = _conv_taps(a_ext, r0 - HALO, cw_ref, cb_ref)
            acc = acc[HALO:, :]
            g_ref[r0:r0 + rows, :] = (acc * _sigmoid(acc) * up_ref[1, r0:r0 + rows, :]).astype(BF16)

        project(starts[0])
        for r0, r_next in zip(starts, starts[1:] + [None]):
            if r_next is not None:
                project(r_next)
            activate(r0)

    return _pcall(
        body, name=name, grid=(f // FF_COLS,),
        in_specs=[pl.BlockSpec((t, d), lambda j: (0, 0), pipeline_mode=pl.Buffered(1)),
                  pl.BlockSpec((None, FF_COLS, d), lambda j: (0, j, 0)), pl.BlockSpec((None, FF_COLS, d), lambda j: (1, j, 0)),
                  pl.BlockSpec((8, FF_COLS), lambda j: (0, j)), pl.BlockSpec((1, FF_COLS), lambda j: (0, j))],
        out_specs=[pl.BlockSpec((2, t, FF_COLS), lambda j: (0, 0, j)), pl.BlockSpec((t, FF_COLS), lambda j: (0, j))],
        out_shape=[jax.ShapeDtypeStruct((2, t, f), F32), jax.ShapeDtypeStruct((t, f), BF16)],
        compiler_params=_params("parallel"),
    )(n2, w_up_t, w_up_t, conv_w8, conv_b)


def _dg_conv_bwd(up, conv_w8, conv_b, dh2, w_down, name):
    _, t, f = up.shape
    d = dh2.shape[1]
    rows = TOK_TILE
    starts = list(range(0, t, rows))

    def body(a_ref, b_ref, cw_ref, cb_ref, dh_ref, wd_ref, dup_ref, gcw_ref, gcb_ref, dg_ref):
        wd = wd_ref[...]

        def project(r0):
            dg_ref[r0:r0 + rows, :] = _dot(dh_ref[r0:r0 + rows, :], wd, NT)

        gw = [jnp.zeros((1, FF_COLS), F32) for _ in range(3)]
        gb = jnp.zeros((1, FF_COLS), F32)
        project(starts[0])
        for r0, r_next in zip(starts, starts[1:] + [None]):
            if r_next is not None:
                project(r_next)
            a_ext = _rows_ext(a_ref, r0, rows, t, HALO, HALO)
            b_ext = _rows_ext(b_ref, r0, rows, t, HALO, HALO)
            dg_ext = _rows_ext(dg_ref, r0, rows, t, HALO, HALO)
            a0, a1, a2, acc = _conv_taps(a_ext, r0 - HALO, cw_ref, cb_ref)
            sg = _sigmoid(acc)
            dacc = dg_ext * b_ext * (sg * (1.0 + acc * (1.0 - sg)))
            n = dacc.shape[0]
            da = (dacc * cw_ref[2:3, :] + pltpu.roll(dacc, n - 1, axis=0) * cw_ref[1:2, :]
                  + pltpu.roll(dacc, n - 2, axis=0) * cw_ref[0:1, :])
            core = slice(HALO, HALO + rows)
            da = da[core, :]
            if r0 < N_PAD:
                row = r0 + lax.broadcasted_iota(jnp.int32, (rows, 1), 0)
                da = jnp.where(row >= N_PAD, da, 0.0)
            dup_ref[0, r0:r0 + rows, :] = da.astype(BF16)
            dup_ref[1, r0:r0 + rows, :] = (dg_ext * acc * sg)[core, :].astype(BF16)
            dacc_c = dacc[core, :]
            gw[0] = gw[0] + jnp.sum(dacc_c * a2[core, :], axis=0, keepdims=True)
            gw[1] = gw[1] + jnp.sum(dacc_c * a1[core, :], axis=0, keepdims=True)
            gw[2] = gw[2] + jnp.sum(dacc_c * a0[core, :], axis=0, keepdims=True)
            gb = gb + jnp.sum(dacc_c, axis=0, keepdims=True)
        gcw_ref[...] = jnp.zeros((8, FF_COLS), F32)
        for tap in range(3):
            gcw_ref[tap:tap + 1, :] = gw[tap]
        gcb_ref[...] = gb

    return _pcall(
        body, name=name, grid=(f // FF_COLS,),
        in_specs=[pl.BlockSpec((None, t, FF_COLS), lambda j: (0, 0, j)), pl.BlockSpec((None, t, FF_COLS), lambda j: (1, 0, j)),
                  pl.BlockSpec((8, FF_COLS), lambda j: (0, j)), pl.BlockSpec((1, FF_COLS), lambda j: (0, j)),
                  pl.BlockSpec((t, d), lambda j: (0, 0), pipeline_mode=pl.Buffered(1)),
                  pl.BlockSpec((FF_COLS, d), lambda j: (j, 0))],
        out_specs=[pl.BlockSpec((2, t, FF_COLS), lambda j: (0, 0, j)), pl.BlockSpec((8, FF_COLS), lambda j: (0, j)),
                   pl.BlockSpec((1, FF_COLS), lambda j: (0, j))],
        out_shape=[jax.ShapeDtypeStruct((2, t, f), BF16), jax.ShapeDtypeStruct((8, f), F32),
                   jax.ShapeDtypeStruct((1, f), F32)],
        scratch_shapes=[pltpu.VMEM((t, FF_COLS), F32)],
        compiler_params=_params("parallel"),
    )(up, up, conv_w8, conv_b, dh2, w_down)


def _exchange(arrays, kinds, name, after=None):
    n = len(arrays)
    npeer = N_DEV - 1
    n_in = n + int(after is not None)

    def body(*refs):
        ins, outs = refs[:n], refs[n_in:n_in + n]
        send_sems, recv_sems, local_sems = refs[n_in + n:]
        x, y, c = lax.axis_index("x"), lax.axis_index("y"), lax.axis_index("c")
        me = 4 * x + 2 * y + c
        copies, locals_ = [], []
        for a in range(n):
            gather = kinds[a] == "gather"
            own = pltpu.make_async_copy(ins[a] if gather else ins[a].at[me], outs[a].at[me], local_sems.at[a])
            own.start()
            locals_.append(own)
            for d in range(1, N_DEV):
                px = 1 - x if d & 4 else x
                py = 1 - y if d & 2 else y
                pc = 1 - c if d & 1 else c
                src = ins[a] if gather else ins[a].at[4 * px + 2 * py + pc]
                cp = pltpu.make_async_remote_copy(
                    src_ref=src, dst_ref=outs[a].at[me],
                    send_sem=send_sems.at[a * npeer + d - 1], recv_sem=recv_sems.at[a * npeer + d - 1],
                    device_id=(px, py, pc), device_id_type=pl.DeviceIdType.MESH)
                cp.start()
                copies.append(cp)
        for cp in copies:
            cp.wait_recv()
        for cp in copies:
            cp.wait_send()
        for own in locals_:
            own.wait()

    out_shape = [jax.ShapeDtypeStruct((N_DEV,) + (a.shape if k == "gather" else a.shape[1:]), a.dtype)
                 for a, k in zip(arrays, kinds)]
    return _pcall(
        body, name=name,
        in_specs=[pl.BlockSpec(memory_space=pl.ANY)] * n_in,
        out_specs=[pl.BlockSpec(memory_space=pl.ANY)] * n,
        out_shape=out_shape,
        scratch_shapes=[pltpu.SemaphoreType.DMA((n * npeer,)), pltpu.SemaphoreType.DMA((n * npeer,)),
                        pltpu.SemaphoreType.DMA((n,))],
        compiler_params=pltpu.CompilerParams(has_side_effects=True),
    )(*arrays, *([] if after is None else [after]))


def _peer_copies(srcs, lands, kinds, send_sems, recv_sems):
    x, y, c = lax.axis_index("x"), lax.axis_index("y"), lax.axis_index("c")
    me = 4 * x + 2 * y + c
    copies = []
    for a in range(len(srcs)):
        for d in range(1, N_DEV):
            px = 1 - x if d & 4 else x
            py = 1 - y if d & 2 else y
            pc = 1 - c if d & 1 else c
            k = a * (N_DEV - 1) + d - 1
            copies.append(pltpu.make_async_remote_copy(
                src_ref=srcs[a] if kinds[a] == "gather" else srcs[a].at[4 * px + 2 * py + pc], dst_ref=lands[a].at[me],
                send_sem=send_sems.at[k], recv_sem=recv_sems.at[k],
                device_id=(px, py, pc), device_id_type=pl.DeviceIdType.MESH))
    return copies


def _exchange_start(arrays, kinds, name, after=None):
    n = len(arrays)
    nsem = n * (N_DEV - 1)
    hbm = pl.BlockSpec(memory_space=pltpu.HBM)
    sem = pl.BlockSpec(memory_space=pltpu.SEMAPHORE)
    land_shapes = [(N_DEV,) + (a.shape if k == "gather" else a.shape[1:]) for a, k in zip(arrays, kinds)]

    n_in = 2 * n + int(after is not None)

    def body(*refs):
        srcs, lands = refs[:n], refs[n:2 * n]
        send_sems, recv_sems = refs[n_in], refs[n_in + 1]
        token = refs[-1]
        for cp in _peer_copies(srcs, lands, kinds, send_sems, recv_sems):
            cp.start()
        token[...] = jnp.zeros_like(token)

    operands = [pltpu.with_memory_space_constraint(a, pltpu.HBM) for a in arrays]
    operands += [pltpu.with_memory_space_constraint(lax.empty(s, a.dtype), pltpu.HBM) for s, a in zip(land_shapes, arrays)]
    operands += [] if after is None else [after]
    out = _pcall(
        body, name=name,
        in_specs=[hbm] * (2 * n) + ([] if after is None else [pl.BlockSpec(memory_space=pl.ANY)]),
        out_specs=[sem, sem] + [hbm] * (2 * n) + [pl.BlockSpec(memory_space=pltpu.VMEM)],
        out_shape=[pltpu.SemaphoreType.DMA((nsem,)), pltpu.SemaphoreType.DMA((nsem,))]
        + [pltpu.HBM(a.shape, a.dtype) for a in arrays]
        + [pltpu.HBM(s, a.dtype) for s, a in zip(land_shapes, arrays)]
        + [jax.ShapeDtypeStruct((8, LANE), F32)],
        input_output_aliases={k: 2 + k for k in range(2 * n)},
        compiler_params=pltpu.CompilerParams(has_side_effects=pltpu.SideEffectType.DATAFLOW_SIDE_EFFECTING),
    )(*operands)
    return out[0], out[1], list(out[2:2 + n]), list(out[2 + n:2 + 2 * n]), out[-1]


def _exchange_wait(started, kinds, after, name):
    send_sems, recv_sems, srcs, lands, _ = started
    n = len(srcs)
    hbm = pl.BlockSpec(memory_space=pltpu.HBM)
    sem = pl.BlockSpec(memory_space=pltpu.SEMAPHORE)

    def body(*refs):
        src_refs, land_refs = refs[:n], refs[n:2 * n]
        copies = _peer_copies(src_refs, land_refs, kinds, refs[2 * n], refs[2 * n + 1])
        for cp in copies:
            cp.wait_send()
        for cp in copies:
            cp.wait_recv()

    out = _pcall(
        body, name=name,
        in_specs=[hbm] * (2 * n) + [sem, sem, pl.BlockSpec(memory_space=pl.ANY)],
        out_specs=[hbm] * (2 * n),
        out_shape=[pltpu.HBM(a.shape, a.dtype) for a in srcs + lands],
        input_output_aliases={k: k for k in range(2 * n)},
        compiler_params=pltpu.CompilerParams(has_side_effects=pltpu.SideEffectType.DATAFLOW_SIDE_EFFECTING),
    )(*srcs, *lands, send_sems, recv_sems, after)
    me = 4 * lax.axis_index("x") + 2 * lax.axis_index("y") + lax.axis_index("c")
    filled = []
    for src, land, kind in zip(out[:n], out[n:], kinds):
        own = src if kind == "gather" else lax.dynamic_index_in_dim(src, me, axis=0, keepdims=False)
        filled.append(lax.dynamic_update_slice(land, own[None], (me,) + (0,) * own.ndim))
    return filled


def _sum_slots(slots, name, rows_tile):
    nd, r, c = slots.shape

    def body(s_ref, o_ref):
        acc = s_ref[0].astype(F32)
        for p in range(1, nd):
            acc = acc + s_ref[p].astype(F32)
        o_ref[...] = acc

    return _pcall(
        body, name=name, grid=(r // rows_tile,),
        in_specs=[pl.BlockSpec((nd, rows_tile, c), lambda i: (0, i, 0))],
        out_specs=pl.BlockSpec((rows_tile, c), lambda i: (i, 0)),
        out_shape=jax.ShapeDtypeStruct((r, c), F32),
        compiler_params=_params("parallel"),
    )(slots)


def _sum_slots_small(slot_arrays, name):
    n = len(slot_arrays)

    def body(*refs):
        for s_ref, o_ref in zip(refs[:n], refs[n:]):
            acc = s_ref[0]
            for p in range(1, s_ref.shape[0]):
                acc = acc + s_ref[p]
            o_ref[...] = acc

    return _pcall(body, name=name, out_shape=[jax.ShapeDtypeStruct(a.shape[1:], F32) for a in slot_arrays])(*slot_arrays)


def _adamw_update(w_ref, g_ref, m_ref, v_ref, d_ref, nm_ref, nv_ref):
    gr = g_ref[...]
    nm = ADAM_B1 * m_ref[...] + (1.0 - ADAM_B1) * gr
    nv = ADAM_B2 * v_ref[...] + (1.0 - ADAM_B2) * (gr * gr)
    m_hat = nm / (1.0 - ADAM_B1 ** ADAM_STEP)
    v_hat = nv / (1.0 - ADAM_B2 ** ADAM_STEP)
    d_ref[...] = -ADAM_LR * (m_hat / (jnp.sqrt(v_hat) + ADAM_EPS) + ADAM_WD * w_ref[...])
    nm_ref[...] = nm
    nv_ref[...] = nv


def _adamw_small(ws, gs, ms, vs, name):
    n = len(ws)

    def body(*refs):
        ins, outs = refs[:4 * n], refs[4 * n:]
        for k in range(n):
            _adamw_update(ins[k], ins[n + k], ins[2 * n + k], ins[3 * n + k], outs[k], outs[n + k], outs[2 * n + k])

    shapes = [jax.ShapeDtypeStruct(w.shape, F32) for w in ws]
    out = _pcall(body, name=name, out_shape=shapes * 3)(*ws, *gs, *ms, *vs)
    return list(out[:n]), list(out[n:2 * n]), list(out[2 * n:])


def _adamw(w, g, m, v, name, rows_tile):
    r, c = w.shape
    body = lambda *refs: _adamw_update(*refs)
    spec = pl.BlockSpec((rows_tile, c), lambda i: (i, 0))
    shp = jax.ShapeDtypeStruct((r, c), F32)
    return _pcall(
        body, name=name, grid=(r // rows_tile,), in_specs=[spec] * 4, out_specs=[spec] * 3, out_shape=[shp] * 3,
        compiler_params=_params("parallel"),
    )(w, g, m, v)


F0 = 2 * RET_QK + 2 * RET_V


def _to_internal_rows(w_t):
    cols = w_t.shape[1]
    fox = w_t[F0:F0 + 3 * FOX_W].reshape(3, FOX_PAIRS, LANE, cols).transpose(1, 0, 2, 3).reshape(3 * FOX_W, cols)
    tail = jnp.zeros((IN_PAD - IN_WIDTH, cols), w_t.dtype)
    return jnp.concatenate([w_t[:F0], fox, w_t[F0 + 3 * FOX_W:], tail], axis=0)


def _from_internal_rows(g_t):
    cols = g_t.shape[1]
    fox = g_t[F0:F0 + 3 * FOX_W].reshape(FOX_PAIRS, 3, LANE, cols).transpose(1, 0, 2, 3).reshape(3 * FOX_W, cols)
    return jnp.concatenate([g_t[:F0], fox, g_t[F0 + 3 * FOX_W:F0 + 3 * FOX_W + FOX_HEADS]], axis=0)


def _local_step(x, target, meta, attn_g, fox_b, ret_g, ffn_g, conv_w8, conv_b, final_g,
                first_weight, late_weights, ffn_grads_ready, out_grad_ready, in_grad_ready):
    seq, d = x.shape
    t = seq + PREFIX
    tm = TOK_TILE
    nq = t // tm
    fox_b128 = jnp.pad(fox_b, ((0, 0), (0, LANE - FOX_HEADS)))

    h0, n1 = _prep_norm(x, meta, attn_g, "prep_norm")
    w_in_t = first_weight(n1)
    proj = _mm_simple(n1, w_in_t, mode="nt", tm=tm, tn=IN_PAD, tk=d, out_dtype=F32, name="mm_in")
    cos, sin = _rope_tables(t)
    o_pre, mixed, states = _ret_fwd(proj, cos, sin, ret_g, "ret_fwd")
    c = _forget_cumsum(proj, fox_b128, "forget_cumsum")
    qa, ka, va, qt, vt = _fox_prep(proj, c, "fox_prep")
    by_block = lambda a: a.reshape(FOX_HEADS, nq, tm, LANE)
    mixed, o_fox, lse = _fox_fwd(qt, by_block(ka), vt, mixed, "fox_fwd")
    w_out, w_up_t, w_down = late_weights(o_fox)
    tile = pl.BlockSpec((tm, d), lambda i: (i, 0))
    row_vec = pl.BlockSpec((1, d), lambda i: (0, 0))
    resident = lambda shape: pl.BlockSpec(shape, lambda i: (0,) * len(shape), pipeline_mode=pl.Buffered(1))
    acts = lambda dtype: jax.ShapeDtypeStruct((t, d), dtype)
    vec = jax.ShapeDtypeStruct((1, d), F32)

    def residual_and_norm(i, acc, ins, outs):
        h = acc + ins[0][...]
        outs[0][...] = h
        outs[1][...] = (h * lax.rsqrt(jnp.mean(h * h, axis=-1, keepdims=True) + EPS) * ins[1][...]).astype(BF16)

    h1, n2 = _matmul_rows([mixed], [tile], [w_out], [resident((d, d))], [h0, ffn_g], [tile, row_vec],
                          [tile, tile], [acts(F32), acts(BF16)], residual_and_norm, mode="nn", steps=nq, name="mm_out_norm")
    nf = D_FF // 1408
    up, g = _up_conv_fwd(n2, w_up_t, conv_w8, conv_b, "up_conv_fwd")

    def residual_loss_bwd(i, acc, ins, outs):
        loss_ref, dh_ref, dhb_ref, gg_ref = outs
        part, dh, gg = _loss_tile(i, acc + ins[0][...], jnp.concatenate([ins[1][...], ins[2][...], ins[3][...]], axis=0),
                                  ins[4][...])
        _accumulate(loss_ref, i, jnp.broadcast_to(part, loss_ref.shape))
        dh_ref[...] = dh
        dhb_ref[...] = dh.astype(BF16)
        _accumulate(gg_ref, i, gg)

    loss_tile, dh2, dh2_b, g_final = _matmul_rows(
        [g], [pl.BlockSpec((tm, D_FF), lambda i: (i, 0))], [w_down], [resident((D_FF, d))],
        [h1, target, target, target, final_g], [tile] + _shifted_row_specs(d) + [row_vec],
        [pl.BlockSpec((8, LANE), lambda i: (0, 0)), tile, tile, row_vec],
        [jax.ShapeDtypeStruct((8, LANE), F32), acts(F32), acts(BF16), vec], residual_loss_bwd,
        mode="nn", steps=nq, name="mm_down_loss")

    tkw = 2112 if t % 2112 == 0 else tm
    gw_down = _mm_simple(g, dh2_b, mode="tn", tm=1408, tn=d, tk=tkw, out_dtype=BF16, name="mm_gw_down")
    dup, g_conv_w8, g_conv_b = _dg_conv_bwd(up, conv_w8, conv_b, dh2_b, w_down, "dg_conv_bwd")

    half = lambda p: pl.BlockSpec((None, tm, D_FF), lambda i: (p, i, 0))
    half_w = lambda p: pl.BlockSpec((None, D_FF, d), lambda i: (p, 0, 0), pipeline_mode=pl.Buffered(1))
    gw_up_t = _matmul(
        dup, n2, mode="tn", grid=(2 * nf, 1, t // tkw),
        a_spec=pl.BlockSpec((None, tkw, 1408), lambda i, j, k: (i // nf, k, i % nf)),
        b_spec=pl.BlockSpec((tkw, d), lambda i, j, k: (k, 0)),
        o_spec=pl.BlockSpec((1408, d), lambda i, j, k: (i, 0)),
        out_shape=jax.ShapeDtypeStruct((2 * D_FF, d), BF16), name="mm_gw_up")
    def norm_bwd_and_mixer_grad(i, acc, ins, outs):
        dh, gg = _rms_bwd_tile(acc, ins[0][...], ins[1][...], ins[2][...])
        outs[0][...] = dh
        _accumulate(outs[1], i, gg)
        outs[2][...] = _dot(dh.astype(BF16), ins[3][...], NT)

    dh1, g_ffn, dmixed = _matmul_rows(
        [dup, dup], [half(0), half(1)], [w_up_t, w_up_t], [half_w(0), half_w(1)],
        [h1, ffn_g, dh2, w_out], [tile, row_vec, tile, resident((d, d))], [tile, row_vec, tile],
        [acts(F32), vec, acts(F32)], norm_bwd_and_mixer_grad,
        mode="nn", steps=nq, name="mm_dn2_norm_bwd", after=ffn_grads_ready(gw_down, gw_up_t))
    gw_out = _mm_simple(mixed, dh1, mode="tn", tm=d, tn=d, tk=tkw, out_dtype=BF16, name="mm_gw_out")
    dproj, g_ret = _ret_bwd(proj, cos, sin, ret_g + out_grad_ready(gw_out), dmixed, o_pre, states, "ret_bwd")
    qab, doa = _fox_prep_bwd(dmixed, o_fox, lse, qa, "fox_prep_bwd")
    dproj, drs, dcs = _fox_bwd(by_block(qab), by_block(doa), by_block(ka), by_block(va), dproj, "fox_bwd")
    dproj, g_fox_b = _forget_cumsum_bwd(proj, fox_b128, drs, dcs, dproj, "forget_cumsum_bwd")
    gw_in_t = _mm_simple(dproj, n1, mode="tn", tm=640, tn=d, tk=tkw, out_dtype=BF16, name="mm_gw_in")
    sent = in_grad_ready(gw_in_t)
    def input_grads(i, acc, ins, outs):
        gx_ref, gmeta_ref, gg_ref, buf_ref, sems = outs
        dh, gg = _rms_bwd_tile(acc, ins[0][...], ins[1][...], ins[2][...])
        _accumulate(gg_ref, i, gg)
        slot = i % 2

        def first_copy():
            return pltpu.make_async_copy(buf_ref.at[0, pl.ds(PREFIX, tm - PREFIX)], gx_ref.at[pl.ds(0, tm - PREFIX)],
                                         sems.at[0])

        def tile_copy(tile, buf_slot):
            rows = pl.ds(pl.multiple_of(tile * tm - PREFIX, PREFIX), tm)
            return pltpu.make_async_copy(buf_ref.at[buf_slot], gx_ref.at[rows], sems.at[buf_slot])

        @pl.when(i == 1)
        def _():
            first_copy().wait()

        @pl.when(i >= 2)
        def _():
            tile_copy(i - 1, 1 - slot).wait()

        buf_ref[slot] = dh

        @pl.when(i == 0)
        def _():
            gmeta_ref[...] = dh[N_PAD:PREFIX, :]
            first_copy().start()

        @pl.when(i > 0)
        def _():
            tile_copy(i, slot).start()

        @pl.when(i == nq - 1)
        def _():
            tile_copy(i, slot).wait()

    grad_x, g_meta, g_attn = _matmul_rows(
        [dproj], [pl.BlockSpec((tm, IN_PAD), lambda i: (i, 0))], [w_in_t], [resident((IN_PAD, d))],
        [h0, attn_g, dh1], [tile, row_vec, tile],
        [pl.BlockSpec(memory_space=pl.ANY), pl.BlockSpec((N_META, d), lambda i: (0, 0)), row_vec],
        [jax.ShapeDtypeStruct((seq, d), F32), jax.ShapeDtypeStruct((N_META, d), F32), vec], input_grads,
        mode="nn", steps=nq, name="mm_dn1_norm_bwd", after=sent,
        scratch=[pltpu.VMEM((2, tm, d), F32), pltpu.SemaphoreType.DMA((2,))])

    grads = dict(meta=g_meta, attn_g=g_attn, fox_b=g_fox_b, ret_g=g_ret,
                 ffn_g=g_ffn, conv_w=g_conv_w8, conv_b=g_conv_b, final_g=g_final)
    return loss_tile, grad_x, grads


def kernel(x, meta_tokens, attn_norm_g, w_in, fox_forget_b, ret_norm_g, w_out, ffn_norm_g, w_up, conv_w, conv_b, w_down, final_norm_g, loss_target, m_meta_tokens, m_attn_norm_g, m_w_in, m_fox_forget_b, m_ret_norm_g, m_w_out, m_ffn_norm_g, m_w_up, m_conv_w, m_conv_b, m_w_down, m_final_norm_g, v_meta_tokens, v_attn_norm_g, v_w_in, v_fox_forget_b, v_ret_norm_g, v_w_out, v_ffn_norm_g, v_w_up, v_conv_w, v_conv_b, v_w_down, v_final_norm_g):
    d = D_MODEL
    me = 4 * lax.axis_index("x") + 2 * lax.axis_index("y") + lax.axis_index("c")
    in_blk = IN_WIDTH // N_DEV
    in_blk_pad = 400
    up_blk = 2 * D_FF // N_DEV
    down_blk = D_FF // N_DEV
    cw_blk = D_FF // N_DEV

    w_in_loc = jnp.pad(w_in[0].T.astype(BF16), ((0, in_blk_pad - in_blk), (0, 0)))
    cw_loc = jnp.pad(conv_w[0], ((0, 5), (0, 384 - cw_blk)))
    g_meta, g_cw = _exchange([meta_tokens, cw_loc], ["gather"] * 2, "gather_small")
    first = _exchange_start([w_in_loc], ["gather"], "gather_in_start", after=g_meta)
    rest_loc = [(w_out[0] + first[-1][0:1, 0:1]).astype(BF16), w_up[0].T.astype(BF16), w_down[0].astype(BF16)]
    rest = _exchange_start(rest_loc, ["gather"] * 3, "gather_rest_start")
    meta_f = g_meta.transpose(1, 0, 2).reshape(N_META, d)
    conv_w8 = jnp.pad(g_cw[:, :3, :cw_blk].transpose(1, 0, 2).reshape(3, D_FF), ((0, 5), (0, 0)))
    pending = {}

    def first_weight(after):
        (g_in,) = _exchange_wait(first, ["gather"], after, "gather_in_wait")
        return _to_internal_rows(g_in[:, :in_blk].reshape(IN_WIDTH, d))

    def in_grad_ready(gw_in_t):
        blocks = _from_internal_rows(gw_in_t).reshape(N_DEV, in_blk, d)
        blocks = jnp.pad(blocks, ((0, 0), (0, in_blk_pad - in_blk), (0, 0)))
        pending["in"] = _exchange_start([blocks], ["scatter"], "grads_in_start")
        return pending["in"][-1][0:1, 0:1]

    def late_weights(after):
        g_out, g_up, g_down = _exchange_wait(rest, ["gather"] * 3, after, "gather_rest_wait")
        return g_out.reshape(d, d), g_up.reshape(2, D_FF, d), g_down.reshape(D_FF, d)

    def ffn_grads_ready(gw_down, gw_up_t):
        blocks = [gw_down.reshape(N_DEV, down_blk, d), gw_up_t.reshape(N_DEV, up_blk, d)]
        pending["ffn"] = _exchange_start(blocks, ["scatter"] * 2, "grads_ffn_start")
        return pending["ffn"][-1][0:1, 0:1]

    def out_grad_ready(gw_out):
        pending["out"] = _exchange_start([gw_out.reshape(N_DEV, d // N_DEV, d)], ["scatter"], "grads_out_start")
        return pending["out"][-1][0:1, 0:1]

    loss_tile, grad_x, gr = _local_step(
        x[0], loss_target[0], meta_f, attn_norm_g + rest[-1][0:1, 0:1], fox_forget_b, ret_norm_g, ffn_norm_g,
        conv_w8, conv_b, final_norm_g.reshape(1, d), first_weight, late_weights, ffn_grads_ready, out_grad_ready,
        in_grad_ready)

    r_down, r_up = _exchange_wait(pending["ffn"], ["scatter"] * 2, grad_x, "grads_ffn_wait")
    (r_out,) = _exchange_wait(pending["out"], ["scatter"], grad_x, "grads_out_wait")
    g_w_out = _sum_slots(r_out, "sum_w_out", d // N_DEV)
    g_w_up_t = _sum_slots(r_up, "sum_w_up", up_blk)
    g_w_down = _sum_slots(r_down, "sum_w_down", down_blk)
    as_t = lambda a: a[0].T
    from_t = lambda a: a.T[None]
    d_w_out, m_w_out_n, v_w_out_n = [a[None] for a in _adamw(w_out[0], g_w_out, m_w_out[0], v_w_out[0], "adamw_w_out", 128)]
    up_t = _adamw(as_t(w_up), g_w_up_t, as_t(m_w_up), as_t(v_w_up), "adamw_w_up", up_blk // 2)
    d_w_up, m_w_up_n, v_w_up_n = [from_t(a) for a in up_t]
    d_w_down, m_w_down_n, v_w_down_n = [a[None] for a in _adamw(w_down[0], g_w_down, m_w_down[0], v_w_down[0],
                                                                "adamw_w_down", down_blk)]

    small = [loss_tile, gr["attn_g"], gr["fox_b"], gr["ret_g"], gr["ffn_g"], gr["conv_b"], gr["final_g"],
             gr["meta"], gr["conv_w"]]
    r_small = _exchange(small, ["gather"] * len(small), "exchange_small", after=up_t[0])
    (loss_all, g_attn, g_fox_b128, g_ret, g_ffn, g_conv_b, g_final, g_meta_full, g_cw_full) = _sum_slots_small(
        r_small, "sum_small")
    loss = loss_all[0, 0]
    g_fox_b = g_fox_b128[:, :FOX_HEADS]
    g_meta_loc = lax.dynamic_slice(g_meta_full, (0, me * (d // N_DEV)), (N_META, d // N_DEV))
    g_cw_loc = lax.dynamic_slice(g_cw_full, (0, me * cw_blk), (3, cw_blk))

    (r_in,) = _exchange_wait(pending["in"], ["scatter"], r_small[0], "grads_in_wait")
    g_w_in_t = _sum_slots(r_in, "sum_w_in", in_blk_pad)[:in_blk]
    d_w_in, m_w_in_n, v_w_in_n = [from_t(a) for a in _adamw(as_t(w_in), g_w_in_t, as_t(m_w_in), as_t(v_w_in),
                                                            "adamw_w_in", in_blk)]
    g_w_in, g_w_up = g_w_in_t.T, g_w_up_t.T
    row = lambda a: a.reshape(1, d)
    sm_grads = [g_meta_loc, g_attn, g_fox_b, g_ret, g_ffn, g_cw_loc, g_conv_b, g_final]
    sm_w = [meta_tokens, attn_norm_g, fox_forget_b, ret_norm_g, ffn_norm_g, conv_w[0], conv_b, row(final_norm_g)]
    sm_m = [m_meta_tokens, m_attn_norm_g, m_fox_forget_b, m_ret_norm_g, m_ffn_norm_g, m_conv_w[0], m_conv_b,
            row(m_final_norm_g)]
    sm_v = [v_meta_tokens, v_attn_norm_g, v_fox_forget_b, v_ret_norm_g, v_ffn_norm_g, v_conv_w[0], v_conv_b,
            row(v_final_norm_g)]
    dl, ml, vl = [lst[:7] + [lst[7].reshape(d)] for lst in _adamw_small(sm_w, sm_grads, sm_m, sm_v, "adamw_small")]

    def by_weight(meta_, attn_, w_in_, fox_, ret_, w_out_, ffn_, w_up_, cw_, cb_, w_down_, final_):
        return (meta_, attn_, w_in_, fox_, ret_, w_out_, ffn_, w_up_, cw_[None], cb_, w_down_, final_)

    grads_out = by_weight(g_meta_loc, g_attn, g_w_in[None], g_fox_b, g_ret, g_w_out[None], g_ffn, g_w_up[None], g_cw_loc,
                          g_conv_b, g_w_down[None], g_final.reshape(d))
    delta_out = by_weight(dl[0], dl[1], d_w_in, dl[2], dl[3], d_w_out, dl[4], d_w_up, dl[5], dl[6], d_w_down, dl[7])
    m_out = by_weight(ml[0], ml[1], m_w_in_n, ml[2], ml[3], m_w_out_n, ml[4], m_w_up_n, ml[5], ml[6], m_w_down_n, ml[7])
    v_out = by_weight(vl[0], vl[1], v_w_in_n, vl[2], vl[3], v_w_out_n, vl[4], v_w_up_n, vl[5], vl[6], v_w_down_n, vl[7])
    return (loss, grad_x[None]) + grads_out + delta_out + m_out + v_out
```

```python
import numpy as np
import jax
import jax.numpy as jnp
from jax import lax
from jax.experimental import pallas as pl
from jax.experimental.pallas import tpu as pltpu

F32 = jnp.float32
BF16 = jnp.bfloat16

D_MODEL = 1024
N_META = 16
N_PAD = 112
PREFIX = 128
RET_HEADS = 4
RET_DK = 64
RET_DV = 128
FOX_HEADS = 8
FOX_DH = 64
D_FF = 2816
ROPE_BASE = 10000.0
EPS = 1e-6
NEG = -1e30
RET_QK = RET_HEADS * RET_DK
RET_V = RET_HEADS * RET_DV
FOX_W = FOX_HEADS * FOX_DH
IN_WIDTH = 2 * RET_QK + 2 * RET_V + 3 * FOX_W + FOX_HEADS
IN_PAD = 3200
FF_COL_BLOCK = (IN_WIDTH - FOX_HEADS) // 128
QK_SCALE = 0.125

ADAM_LR = 0.001
ADAM_B1 = 0.9
ADAM_B2 = 0.999
ADAM_EPS = 1e-08
ADAM_WD = 0.01
ADAM_STEP = 10

N_DEV = 8
LANE = 128
ROW_TILE = 128
TOK_TILE = 384

NN = (((1,), (0,)), ((), ()))
NT = (((1,), (1,)), ((), ()))
TN = (((0,), (0,)), ((), ()))


def _pcall(body, **kw):
    return pl.pallas_call(body, **kw)


def _params(*sem):
    return pltpu.CompilerParams(dimension_semantics=sem)


def _dot(a, b, dims=NN):
    return lax.dot_general(a, b, dims, preferred_element_type=F32)


def _sigmoid(x):
    return 0.5 * jnp.tanh(0.5 * x) + 0.5


def _matmul(a, b, *, mode, grid, a_spec, b_spec, o_spec, out_shape, name, add=None, add_spec=None, after=None):
    dims = {"nn": NN, "nt": NT, "tn": TN}[mode]
    nk = grid[2]
    has_add = add is not None
    a_list, b_list = (list(a), list(b)) if isinstance(a, (list, tuple)) else ([a], [b])
    a_specs, b_specs = (list(a_spec), list(b_spec)) if isinstance(a_spec, (list, tuple)) else ([a_spec], [b_spec])
    nt = len(a_list)
    n_in = 2 * nt + int(has_add) + int(after is not None)

    def body(*refs):
        a_refs, b_refs = refs[:nt], refs[nt:2 * nt]
        add_ref = refs[2 * nt] if has_add else None
        o_ref = refs[n_in]
        part = _dot(a_refs[0][...].astype(BF16), b_refs[0][...].astype(BF16), dims)
        for ar, br in zip(a_refs[1:], b_refs[1:]):
            part = part + _dot(ar[...].astype(BF16), br[...].astype(BF16), dims)

        def finish(acc):
            if has_add:
                acc = acc + add_ref[...]
            o_ref[...] = acc.astype(o_ref.dtype)

        if nk == 1:
            finish(part)
        else:
            acc_ref = refs[-1]
            k = pl.program_id(2)

            @pl.when(k == 0)
            def _():
                acc_ref[...] = part

            @pl.when(k > 0)
            def _():
                acc_ref[...] += part

            @pl.when(k == nk - 1)
            def _():
                finish(acc_ref[...])

    in_specs = a_specs + b_specs + ([add_spec] if has_add else [])
    args = tuple(a_list) + tuple(b_list) + ((add,) if has_add else ())
    if after is not None:
        in_specs, args = in_specs + [pl.BlockSpec(memory_space=pl.ANY)], args + (after,)
    scratch = [] if nk == 1 else [pltpu.VMEM(tuple(d for d in o_spec.block_shape if d is not None), F32)]
    return _pcall(
        body, name=name, grid=grid, in_specs=in_specs, out_specs=o_spec, out_shape=out_shape,
        scratch_shapes=scratch, compiler_params=_params("parallel", "parallel", "arbitrary"),
    )(*args)


def _mm_simple(a, b, *, mode, tm, tn, tk, out_dtype, name, add=None, after=None):
    if mode == "tn":
        K, M = a.shape
    else:
        M, K = a.shape
    N = b.shape[0] if mode == "nt" else b.shape[1]
    grid = (M // tm, N // tn, K // tk)
    resident = dict(pipeline_mode=pl.Buffered(1)) if (tn == N and tk == K) else {}
    a_spec = pl.BlockSpec((tk, tm), lambda i, j, k: (k, i)) if mode == "tn" else pl.BlockSpec((tm, tk), lambda i, j, k: (i, k))
    b_spec = (pl.BlockSpec((tn, tk), lambda i, j, k: (j, k), **resident) if mode == "nt"
              else pl.BlockSpec((tk, tn), lambda i, j, k: (k, j), **resident))
    o_spec = pl.BlockSpec((tm, tn), lambda i, j, k: (i, j))
    return _matmul(a, b, mode=mode, grid=grid, a_spec=a_spec, b_spec=b_spec, o_spec=o_spec,
                   out_shape=jax.ShapeDtypeStruct((M, N), out_dtype), name=name, add=add,
                   add_spec=o_spec if add is not None else None, after=after)


def _matmul_rows(a_list, a_specs, b_list, b_specs, extras, extra_specs, out_specs, out_shape, epilogue, *,
                 mode, steps, name, after=None, scratch=()):
    dims = {"nn": NN, "nt": NT}[mode]
    nt, ne = len(a_list), len(extras)
    n_in = 2 * nt + ne + int(after is not None)

    def body(*refs):
        acc = _dot(refs[0][...].astype(BF16), refs[nt][...].astype(BF16), dims)
        for k in range(1, nt):
            acc = acc + _dot(refs[k][...].astype(BF16), refs[nt + k][...].astype(BF16), dims)
        epilogue(pl.program_id(0), acc, refs[2 * nt:2 * nt + ne], refs[n_in:])

    in_specs = list(a_specs) + list(b_specs) + list(extra_specs)
    args = tuple(a_list) + tuple(b_list) + tuple(extras)
    if after is not None:
        in_specs, args = in_specs + [pl.BlockSpec(memory_space=pl.ANY)], args + (after,)
    return _pcall(body, name=name, grid=(steps,), in_specs=in_specs, out_specs=out_specs, out_shape=out_shape,
                  scratch_shapes=list(scratch), compiler_params=_params("arbitrary"))(*args)


def _rms_bwd_tile(dy, x, gain, dres):
    r = lax.rsqrt(jnp.mean(x * x, axis=-1, keepdims=True) + EPS)
    xhat = x * r
    u = dy * gain
    return dres + r * (u - xhat * jnp.mean(u * xhat, axis=-1, keepdims=True)), jnp.sum(dy * xhat, axis=0, keepdims=True)


def _loss_tile(i, x, tgt, gain):
    d = x.shape[-1]
    r = lax.rsqrt(jnp.mean(x * x, axis=-1, keepdims=True) + EPS)
    xhat = x * r
    counted = (i * TOK_TILE + lax.broadcasted_iota(jnp.int32, (TOK_TILE, 1), 0)) >= PREFIX
    err = jnp.where(counted, xhat * gain - tgt, 0.0)
    dy = err * (1.0 / d)
    u = dy * gain
    dh = r * (u - xhat * jnp.mean(u * xhat, axis=-1, keepdims=True))
    return 0.5 * jnp.sum(jnp.mean(err * err, axis=-1, keepdims=True)), dh, jnp.sum(dy * xhat, axis=0, keepdims=True)


def _accumulate(ref, i, part):
    @pl.when(i == 0)
    def _():
        ref[...] = part

    @pl.when(i > 0)
    def _():
        ref[...] += part


def _prep_norm(x, meta, gain, name):
    seq, d = x.shape
    t = seq + PREFIX

    def body(xa_ref, xb_ref, xc_ref, meta_ref, g_ref, h_ref, n_ref):
        i = pl.program_id(0)

        @pl.when(i == 0)
        def _():
            h_ref[0:N_PAD, :] = jnp.zeros((N_PAD, d), F32)
            h_ref[N_PAD:ROW_TILE, :] = meta_ref[...]

        @pl.when(i > 0)
        def _():
            h_ref[0:ROW_TILE, :] = xa_ref[...]

        h_ref[ROW_TILE:2 * ROW_TILE, :] = xb_ref[...]
        h_ref[2 * ROW_TILE:3 * ROW_TILE, :] = xc_ref[...]
        h = h_ref[...]
        r = lax.rsqrt(jnp.mean(h * h, axis=-1, keepdims=True) + EPS)
        n_ref[...] = (h * r * g_ref[...]).astype(BF16)

    return _pcall(
        body, name=name, grid=(t // TOK_TILE,),
        in_specs=_shifted_row_specs(d) + [pl.BlockSpec((N_META, d), lambda i: (0, 0)), pl.BlockSpec((1, d), lambda i: (0, 0))],
        out_specs=[pl.BlockSpec((TOK_TILE, d), lambda i: (i, 0)), pl.BlockSpec((TOK_TILE, d), lambda i: (i, 0))],
        out_shape=[jax.ShapeDtypeStruct((t, d), F32), jax.ShapeDtypeStruct((t, d), BF16)],
        compiler_params=_params("parallel"),
    )(x, x, x, meta, gain)


def _shifted_row_specs(d):
    blocks_per_tile = TOK_TILE // ROW_TILE
    return [pl.BlockSpec((ROW_TILE, d), lambda i, r=r: (jnp.maximum(blocks_per_tile * i + r, 0), 0)) for r in (-1, 0, 1)]


def _ret_consts(bk):
    gam = 1.0 - 2.0 ** (-5.0 - np.arange(RET_HEADS))
    n = np.arange(bk)
    same_or_earlier_chunk = (n[None, :] // 64) <= (n[:, None] // 64)
    w = gam[:, None, None] ** np.abs(n[:, None] - n[None, :])[None] * same_or_earlier_chunk[None]
    wq = gam[:, None] ** (n[None, :] + 1.0)
    wk = gam[:, None] ** (bk - 1.0 - n[None, :])
    mask = (np.arange(RET_QK)[None, :] // RET_DK) == np.arange(RET_HEADS)[:, None]
    return (jnp.asarray(w, F32), jnp.asarray(wq[:, :, None], F32), jnp.asarray(wk[:, :, None], F32),
            jnp.asarray(mask[:, None, :], F32), [float(g ** bk) for g in gam])


def _rope_tables(t):
    half = RET_DK // 2
    inv = 1.0 / (ROPE_BASE ** (jnp.arange(half, dtype=F32) / half))
    ang = jnp.arange(t).astype(F32)[:, None] * inv[None, :]
    cos, sin = jnp.cos(ang), jnp.sin(ang)
    return (jnp.tile(jnp.concatenate([cos, cos], axis=1), (1, RET_HEADS)),
            jnp.tile(jnp.concatenate([-sin, sin], axis=1), (1, RET_HEADS)))


def _swap_halves(x):
    outs = []
    for s in range(x.shape[1] // LANE):
        xs = x[:, LANE * s:LANE * (s + 1)]
        lane = lax.broadcasted_iota(jnp.int32, xs.shape, 1)
        outs.append(jnp.where((lane & 32) == 0, pltpu.roll(xs, LANE - 32, axis=1), pltpu.roll(xs, 32, axis=1)))
    return outs[0] if len(outs) == 1 else jnp.concatenate(outs, axis=1)


def _rope(x, cos, sin_signed):
    return x * cos + _swap_halves(x) * sin_signed


def _rope_t(dx, cos, sin_signed):
    return dx * cos + _swap_halves(dx * sin_signed)


def _ret_fwd(proj, cos, sin, gain, name):
    t = proj.shape[0]
    bk = TOK_TILE
    nb = t // bk
    w, wq, wk, mask, g_blk = _ret_consts(bk)

    def body(q_ref, k_ref, v_ref, rg_ref, cos_ref, sin_ref, w_ref, wq_ref, wk_ref, mask_ref, gain_ref,
             opre_ref, og_ref, st_ref, r_ref):
        i = pl.program_id(0)

        @pl.when(i == 0)
        def _():
            r_ref[...] = jnp.zeros_like(r_ref)

        c, s = cos_ref[...], sin_ref[...]
        valid = ((i * bk + lax.broadcasted_iota(jnp.int32, (bk, 1), 0)) >= N_PAD).astype(F32)
        qr = _rope(q_ref[...], c, s)
        kr = _rope(k_ref[...], c, s) * QK_SCALE * valid
        kb = kr.astype(BF16)
        for h in range(RET_HEADS):
            hm = mask_ref[h]
            cols = slice(RET_DV * h, RET_DV * (h + 1))
            vh = v_ref[:, cols].astype(BF16)
            r_prev = r_ref[h]
            st_ref[0, h] = r_prev
            sm = _dot((qr * hm).astype(BF16), kb, NT) * w_ref[h]
            o = _dot(sm.astype(BF16), vh) + _dot((qr * (hm * wq_ref[h])).astype(BF16), r_prev.astype(BF16))
            r_ref[h] = g_blk[h] * r_prev + _dot((kr * wk_ref[h]).astype(BF16), vh, TN)
            opre_ref[:, cols] = o
            rstd = lax.rsqrt(jnp.mean(o * o, axis=-1, keepdims=True) + EPS)
            rg = rg_ref[:, cols]
            og_ref[:, cols] = (o * rstd * gain_ref[:, cols] * (rg * _sigmoid(rg))).astype(BF16)

    full = lambda shape: pl.BlockSpec(shape, lambda i: (0,) * len(shape))
    return _pcall(
        body, name=name, grid=(nb,),
        in_specs=[pl.BlockSpec((bk, RET_QK), lambda i: (i, 0)), pl.BlockSpec((bk, RET_QK), lambda i: (i, 1)),
                  pl.BlockSpec((bk, RET_V), lambda i: (i, 1)), pl.BlockSpec((bk, RET_V), lambda i: (i, 2)),
                  pl.BlockSpec((bk, RET_QK), lambda i: (i, 0)), pl.BlockSpec((bk, RET_QK), lambda i: (i, 0)),
                  full((RET_HEADS, bk, bk)), full((RET_HEADS, bk, 1)), full((RET_HEADS, bk, 1)),
                  full((RET_HEADS, 1, RET_QK)), full((1, RET_V))],
        out_specs=[pl.BlockSpec((bk, RET_V), lambda i: (i, 0)), pl.BlockSpec((bk, RET_V), lambda i: (i, 0)),
                   pl.BlockSpec((1, RET_HEADS, RET_QK, RET_DV), lambda i: (i, 0, 0, 0))],
        out_shape=[jax.ShapeDtypeStruct((t, RET_V), F32), jax.ShapeDtypeStruct((t, RET_V + FOX_W), BF16),
                   jax.ShapeDtypeStruct((nb, RET_HEADS, RET_QK, RET_DV), F32)],
        scratch_shapes=[pltpu.VMEM((RET_HEADS, RET_QK, RET_DV), F32)],
        compiler_params=_params("arbitrary"),
    )(proj, proj, proj, proj, cos, sin, w, wq, wk, mask, gain)


def _ret_bwd(proj, cos, sin, gain, dmixed, opre, states, name):
    t = proj.shape[0]
    bk = TOK_TILE
    nb = t // bk
    w, wq, wk, mask, g_blk = _ret_consts(bk)
    v0, g0 = 2 * RET_QK, 2 * RET_QK + RET_V

    def body(q_ref, k_ref, v_ref, rg_ref, cos_ref, sin_ref, w_ref, wq_ref, wk_ref, mask_ref, gain_ref,
             dog_ref, opre_ref, st_ref, dp_ref, gg_ref, dr_ref):
        step = pl.program_id(0)
        i = nb - 1 - step

        @pl.when(step == 0)
        def _():
            dr_ref[...] = jnp.zeros_like(dr_ref)
            gg_ref[...] = jnp.zeros_like(gg_ref)

        c, s = cos_ref[...], sin_ref[...]
        valid = ((i * bk + lax.broadcasted_iota(jnp.int32, (bk, 1), 0)) >= N_PAD).astype(F32)
        qr = _rope(q_ref[...], c, s)
        kr = _rope(k_ref[...], c, s) * QK_SCALE * valid
        kb = kr.astype(BF16)
        dqr = jnp.zeros((bk, RET_QK), F32)
        dkr = jnp.zeros((bk, RET_QK), F32)
        for h in range(RET_HEADS):
            hm = mask_ref[h]
            cols = slice(RET_DV * h, RET_DV * (h + 1))
            vh = v_ref[:, cols].astype(BF16)
            o = opre_ref[:, cols]
            rstd = lax.rsqrt(jnp.mean(o * o, axis=-1, keepdims=True) + EPS)
            xhat = o * rstd
            rg = rg_ref[:, cols]
            sg = _sigmoid(rg)
            gate = rg * sg
            gn = gain_ref[:, cols]
            dog = dog_ref[:, cols]
            dp_ref[:, g0 + RET_DV * h:g0 + RET_DV * (h + 1)] = (
                dog * xhat * gn * (sg * (1.0 + rg * (1.0 - sg)))).astype(BF16)
            gg_ref[:, cols] += jnp.sum(dog * xhat * gate, axis=0, keepdims=True)
            dxh = dog * gn * gate
            do = (rstd * (dxh - xhat * jnp.mean(dxh * xhat, axis=-1, keepdims=True))).astype(BF16)
            qm = (qr * hm).astype(BF16)
            qw = (qr * (hm * wq_ref[h])).astype(BF16)
            kw = (kr * wk_ref[h]).astype(BF16)
            wh = w_ref[h]
            sm = (_dot(qm, kb, NT) * wh).astype(BF16)
            ds = (_dot(do, vh, NT) * wh).astype(BF16)
            dr = dr_ref[h]
            drb = dr.astype(BF16)
            dp_ref[:, v0 + RET_DV * h:v0 + RET_DV * (h + 1)] = (_dot(sm, do, TN) + _dot(kw, drb)).astype(BF16)
            dqr = dqr + _dot(ds, kb) * hm + _dot(do, st_ref[0, h].astype(BF16), NT) * (hm * wq_ref[h])
            dkr = dkr + _dot(ds, qm, TN) + _dot(vh, drb, NT) * wk_ref[h]
            dr_ref[h] = g_blk[h] * dr + _dot(qw, do, TN)
        dp_ref[:, 0:RET_QK] = _rope_t(dqr, c, s).astype(BF16)
        dp_ref[:, RET_QK:2 * RET_QK] = _rope_t(dkr * (QK_SCALE * valid), c, s).astype(BF16)

    full = lambda shape: pl.BlockSpec(shape, lambda i: (0,) * len(shape))
    rev = lambda col: (lambda i: (nb - 1 - i, col))
    return _pcall(
        body, name=name, grid=(nb,),
        in_specs=[pl.BlockSpec((bk, RET_QK), rev(0)), pl.BlockSpec((bk, RET_QK), rev(1)),
                  pl.BlockSpec((bk, RET_V), rev(1)), pl.BlockSpec((bk, RET_V), rev(2)),
                  pl.BlockSpec((bk, RET_QK), rev(0)), pl.BlockSpec((bk, RET_QK), rev(0)),
                  full((RET_HEADS, bk, bk)), full((RET_HEADS, bk, 1)), full((RET_HEADS, bk, 1)),
                  full((RET_HEADS, 1, RET_QK)), full((1, RET_V)),
                  pl.BlockSpec((bk, RET_V), rev(0)), pl.BlockSpec((bk, RET_V), rev(0)),
                  pl.BlockSpec((1, RET_HEADS, RET_QK, RET_DV), lambda i: (nb - 1 - i, 0, 0, 0))],
        out_specs=[pl.BlockSpec((bk, g0 + RET_V), rev(0)), pl.BlockSpec((1, RET_V), lambda i: (0, 0))],
        out_shape=[jax.ShapeDtypeStruct((t, IN_PAD), BF16), jax.ShapeDtypeStruct((1, RET_V), F32)],
        scratch_shapes=[pltpu.VMEM((RET_HEADS, RET_QK, RET_DV), F32)],
        compiler_params=_params("arbitrary"),
    )(proj, proj, proj, proj, cos, sin, w, wq, wk, mask, gain, dmixed, opre, states)


def _forget_cumsum(proj, bias, name):
    t = proj.shape[0]
    rt = TOK_TILE
    nb = t // rt
    tril = jnp.asarray(np.tril(np.ones((rt, rt))), F32)

    def body(z_ref, b_ref, tril_ref, c_ref, carry_ref):
        i = pl.program_id(0)

        @pl.when(i == 0)
        def _():
            carry_ref[...] = jnp.zeros_like(carry_ref)

        z = z_ref[...] + b_ref[...]
        logf = jnp.minimum(z, 0.0) - jnp.log(1.0 + jnp.exp(-jnp.abs(z)))
        c = lax.dot_general(tril_ref[...], logf, NN, precision=lax.Precision.HIGHEST,
                            preferred_element_type=F32) + carry_ref[...]
        c_ref[...] = c
        carry_ref[...] = c[rt - 1:rt, :]

    return _pcall(
        body, name=name, grid=(nb,),
        in_specs=[pl.BlockSpec((rt, LANE), lambda i: (i, FF_COL_BLOCK)), pl.BlockSpec((1, LANE), lambda i: (0, 0)),
                  pl.BlockSpec((rt, rt), lambda i: (0, 0))],
        out_specs=pl.BlockSpec((rt, LANE), lambda i: (i, 0)),
        out_shape=jax.ShapeDtypeStruct((t, LANE), F32),
        scratch_shapes=[pltpu.VMEM((1, LANE), F32)],
        compiler_params=_params("arbitrary"),
    )(proj, bias, tril)


def _forget_cumsum_bwd(proj, bias, drs, dcs, dproj, name):
    t = proj.shape[0]
    rt = TOK_TILE
    nb = t // rt
    triu = jnp.asarray(np.triu(np.ones((rt, rt))), F32)

    def body(z_ref, b_ref, triu_ref, drs_ref, dcs_ref, dproj_in, dz_ref, gb_ref, carry_ref):
        step = pl.program_id(0)

        @pl.when(step == 0)
        def _():
            carry_ref[...] = jnp.zeros_like(carry_ref)
            gb_ref[...] = jnp.zeros_like(gb_ref)

        dlogf = lax.dot_general(triu_ref[...], drs_ref[...] - dcs_ref[...], NN, precision=lax.Precision.HIGHEST,
                                preferred_element_type=F32) + carry_ref[...]
        carry_ref[...] = dlogf[0:1, :]
        z = z_ref[...] + b_ref[...]
        is_head = lax.broadcasted_iota(jnp.int32, (rt, LANE), 1) < FOX_HEADS
        dz = jnp.where(is_head, dlogf / (1.0 + jnp.exp(z)), 0.0)
        dz_ref[...] = dz.astype(BF16)
        gb_ref[...] += jnp.sum(dz, axis=0, keepdims=True)

    return _pcall(
        body, name=name, grid=(nb,),
        in_specs=[pl.BlockSpec((rt, LANE), lambda i: (nb - 1 - i, FF_COL_BLOCK)),
                  pl.BlockSpec((1, LANE), lambda i: (0, 0)),
                  pl.BlockSpec((rt, rt), lambda i: (0, 0)),
                  pl.BlockSpec((rt, LANE), lambda i: (nb - 1 - i, 0)),
                  pl.BlockSpec((rt, LANE), lambda i: (nb - 1 - i, 0)),
                  pl.BlockSpec(memory_space=pl.ANY)],
        out_specs=[pl.BlockSpec((rt, LANE), lambda i: (nb - 1 - i, FF_COL_BLOCK)),
                   pl.BlockSpec((1, LANE), lambda i: (0, 0))],
        out_shape=[jax.ShapeDtypeStruct(dproj.shape, BF16), jax.ShapeDtypeStruct((1, LANE), F32)],
        input_output_aliases={5: 0},
        scratch_shapes=[pltpu.VMEM((1, LANE), F32)],
        compiler_params=_params("arbitrary"),
    )(proj, bias, triu, drs, dcs, dproj)


FOX_PAIRS = FOX_HEADS // 2
L_ONE_Q = FOX_DH
L_ONE_K = FOX_DH + 3
L_LSE = FOX_DH + 4


def _split3(x):
    hi = x.astype(BF16).astype(F32)
    r = x - hi
    mid = r.astype(BF16).astype(F32)
    return hi, mid, r - mid


def _head_to_low(slab, e):
    return slab if e == 0 else pltpu.roll(slab, FOX_DH, axis=1)


def _pair(a, b, low):
    return jnp.where(low, a, pltpu.roll(b, FOX_DH, axis=1))


def _fox_prep(proj, c, name):
    t = proj.shape[0]
    tq = TOK_TILE

    def body(p_ref, c_ref, qa_ref, ka_ref, va_ref, qt_ref, kt_ref, vt_ref):
        i = pl.program_id(0)
        lane = lax.broadcasted_iota(jnp.int32, (tq, LANE), 1)
        low = lane < FOX_DH
        live = (i * tq + lax.broadcasted_iota(jnp.int32, (tq, 1), 0)) >= N_PAD
        q_tail = jnp.where(lane < L_ONE_Q + 3, 1.0, 0.0)
        k_ones = (lane >= L_ONE_K) & (lane < L_ONE_K + 4)
        v_tail = jnp.where(lane < FOX_DH + 2, 1.0, 0.0)
        for pair in range(FOX_PAIRS):
            base = 3 * LANE * pair
            for e in range(2):
                h = 2 * pair + e
                q = _head_to_low(p_ref[:, base:base + LANE], e)
                k = _head_to_low(p_ref[:, base + LANE:base + 2 * LANE], e)
                v = _head_to_low(p_ref[:, base + 2 * LANE:base + 3 * LANE], e)
                hi, mid, lo = _split3(jnp.where(live, -c_ref[:, h:h + 1], NEG))
                ka = jnp.where(low, k, jnp.where(k_ones, 1.0, 0.0))
                ka = jnp.where(lane == L_ONE_Q, hi, jnp.where(lane == L_ONE_Q + 1, mid, jnp.where(lane == L_ONE_Q + 2, lo, ka)))
                qa = jnp.where(low, q * QK_SCALE, q_tail)
                va = jnp.where(low, v, v_tail)
                qa_ref[h] = qa.astype(BF16)
                ka_ref[h] = ka.astype(BF16)
                va_ref[h] = va.astype(BF16)
                qt_ref[h] = qa.T.astype(BF16)
                kt_ref[h] = ka.T.astype(BF16)
                vt_ref[h] = va.T.astype(BF16)

    out = jax.ShapeDtypeStruct((FOX_HEADS, t, LANE), BF16)
    out_t = jax.ShapeDtypeStruct((FOX_HEADS, t // tq, LANE, tq), BF16)
    ospec = pl.BlockSpec((FOX_HEADS, tq, LANE), lambda i: (0, i, 0))
    tspec = pl.BlockSpec((FOX_HEADS, None, LANE, tq), lambda i: (0, i, 0, 0))
    return _pcall(
        body, name=name, grid=(t // tq,),
        in_specs=[pl.BlockSpec((tq, 3 * FOX_W), lambda i: (i, 1)), pl.BlockSpec((tq, LANE), lambda i: (i, 0))],
        out_specs=[ospec, ospec, ospec, tspec, tspec, tspec], out_shape=[out, out, out, out_t, out_t, out_t],
        compiler_params=_params("parallel"),
    )(proj, c)


STEP_PAIRS = 2
STEP_HEADS = 2 * STEP_PAIRS
FOX_GROUPS = FOX_PAIRS // STEP_PAIRS
FWD_PAIRS = 4
FWD_HEADS = 2 * FWD_PAIRS
FWD_GROUPS = FOX_PAIRS // FWD_PAIRS


def _blockdiag(a, b):
    z = jnp.zeros_like(a)
    return jnp.concatenate([jnp.concatenate([a, z], axis=1), jnp.concatenate([z, b], axis=1)], axis=0)


def _fox_fwd(qt, ka, vt, mixed, name):
    nh, nq, tq, _ = ka.shape
    t = nq * tq

    def body(qt_ref, ka_ref, vt_ref, mixed_in, mixed_ref, o_ref, lse_ref):
        i = pl.program_id(1)
        lane = lax.broadcasted_iota(jnp.int32, (tq, LANE), 1)
        key_le_query = lax.broadcasted_iota(jnp.int32, (tq, tq), 0) <= lax.broadcasted_iota(jnp.int32, (tq, tq), 1)
        qts = [qt_ref[h] for h in range(FWD_HEADS)]

        def logits(j):
            return [_dot(ka_ref[h, j], qts[h]) for h in range(FWD_HEADS)]

        def update(j, scores, carry, diagonal):
            new = []
            for h in range(FWD_HEADS):
                m, acc = carry[h]
                s = jnp.where(key_le_query, scores[h], NEG) if diagonal else scores[h]
                m_new = jnp.maximum(m, jnp.max(s, axis=0, keepdims=True))
                p = jnp.exp(s - m_new).astype(BF16)
                new.append((m_new, jnp.exp(m - m_new) * acc + _dot(vt_ref[h, j], p)))
            return tuple(new)

        init = tuple((jnp.full((1, tq), NEG, F32), jnp.zeros((LANE, tq), F32)) for _ in range(FWD_HEADS))
        carry = lax.fori_loop(0, i, lambda j, cr: update(j, logits(j), cr, False), init)
        outs, lse_rows = [], []
        for m, acc in update(i, logits(i), carry, True):
            l = acc[FOX_DH:FOX_DH + 1, :]
            outs.append((acc / l).T)
            lse_rows.append(m + jnp.log(l))
        lse_rows.append(jnp.zeros((LANE - FWD_HEADS, tq), F32))
        o_all = jnp.concatenate([_pair(outs[2 * c], outs[2 * c + 1], lane < FOX_DH) for c in range(FWD_PAIRS)], axis=1)
        mixed_ref[...] = o_all.astype(BF16)
        o_ref[...] = o_all
        lse_ref[...] = jnp.concatenate(lse_rows, axis=0).T

    width = FWD_PAIRS * LANE
    whole = pl.BlockSpec((FWD_HEADS, nq, tq, LANE), lambda g, i: (g, 0, 0, 0), pipeline_mode=pl.Buffered(1))
    whole_t = pl.BlockSpec((FWD_HEADS, nq, LANE, tq), lambda g, i: (g, 0, 0, 0), pipeline_mode=pl.Buffered(1))
    return _pcall(
        body, name=name, grid=(FWD_GROUPS, nq),
        in_specs=[pl.BlockSpec((FWD_HEADS, None, LANE, tq), lambda g, i: (g, i, 0, 0)), whole, whole_t,
                  pl.BlockSpec(memory_space=pl.ANY)],
        out_specs=[pl.BlockSpec((tq, width), lambda g, i: (i, RET_V // width + g)),
                   pl.BlockSpec((tq, width), lambda g, i: (i, g)),
                   pl.BlockSpec((None, tq, LANE), lambda g, i: (g, i, 0))],
        out_shape=[jax.ShapeDtypeStruct(mixed.shape, BF16), jax.ShapeDtypeStruct((t, FOX_W), F32),
                   jax.ShapeDtypeStruct((FWD_GROUPS, t, LANE), F32)],
        input_output_aliases={3: 0},
        compiler_params=_params("parallel", "parallel"),
    )(qt, ka, vt, mixed)


def _fox_prep_bwd(dmixed, o_fox, lse, qa, name):
    t = dmixed.shape[0]
    tq = TOK_TILE

    def body(dm_ref, o_ref, lse_ref, qa_ref, qab_ref, doa_ref, qbt_ref, dot_ref):
        i = pl.program_id(0)
        lane = lax.broadcasted_iota(jnp.int32, (tq, LANE), 1)
        low = lane < FOX_DH
        live = (i * tq + lax.broadcasted_iota(jnp.int32, (tq, 1), 0)) >= N_PAD
        for pair in range(FOX_PAIRS):
            cols = slice(LANE * pair, LANE * (pair + 1))
            d_slab = dm_ref[:, cols]
            prod = d_slab * o_ref[:, cols]
            for e in range(2):
                h = 2 * pair + e
                nd = -jnp.sum(jnp.where(low, _head_to_low(prod, e), 0.0), axis=-1, keepdims=True)
                nd_hi = nd.astype(BF16).astype(F32)
                doa = jnp.where(low, _head_to_low(d_slab, e), 0.0)
                doa = jnp.where(lane == FOX_DH, nd_hi, jnp.where(lane == FOX_DH + 1, nd - nd_hi, doa))
                doa_ref[h] = doa.astype(BF16)
                dot_ref[h] = doa.T.astype(BF16)
                lse_h = lse_ref[h // FWD_HEADS][:, h % FWD_HEADS:h % FWD_HEADS + 1]
                hi, mid, lo = _split3(jnp.where(live, -lse_h, 0.0))
                qab = qa_ref[h].astype(F32)
                qab = jnp.where(lane == L_LSE, hi, jnp.where(lane == L_LSE + 1, mid, jnp.where(lane == L_LSE + 2, lo, qab)))
                qab_ref[h] = qab.astype(BF16)
                qbt_ref[h] = qab.T.astype(BF16)

    out = jax.ShapeDtypeStruct((FOX_HEADS, t, LANE), BF16)
    out_t = jax.ShapeDtypeStruct((FOX_HEADS, t // tq, LANE, tq), BF16)
    hspec = pl.BlockSpec((FOX_HEADS, tq, LANE), lambda i: (0, i, 0))
    tspec = pl.BlockSpec((FOX_HEADS, None, LANE, tq), lambda i: (0, i, 0, 0))
    return _pcall(
        body, name=name, grid=(t // tq,),
        in_specs=[pl.BlockSpec((tq, FOX_W), lambda i: (i, 1)), pl.BlockSpec((tq, FOX_W), lambda i: (i, 0)),
                  pl.BlockSpec((FWD_GROUPS, tq, LANE), lambda i: (0, i, 0)), hspec],
        out_specs=[hspec, hspec, tspec, tspec], out_shape=[out, out, out_t, out_t],
        compiler_params=_params("parallel"),
    )(dmixed, o_fox, lse, qa)


def _fox_bwd(qab, doa, qbt, dot_, ka, va, kt, dproj, name):
    nh, nq, tq, _ = qab.shape
    t = nq * tq
    slab = 3 * LANE * STEP_PAIRS
    group0 = (2 * RET_QK + 2 * RET_V) // slab

    def body(qab_ref, doa_ref, qbt_ref, dot_ref, ka_ref, va_ref, kt_ref, dproj_in, dp_ref, drs_ref, dcs_ref, dq_ref):
        g, j = pl.program_id(0), pl.program_id(1)

        @pl.when((g == 0) & (j == 0))
        def _():
            drs_ref[...] = jnp.zeros_like(drs_ref)
            dcs_ref[...] = jnp.zeros_like(dcs_ref)

        @pl.when(j == 0)
        def _():
            dq_ref[...] = jnp.zeros_like(dq_ref)

        lane = lax.broadcasted_iota(jnp.int32, (tq, LANE), 1)
        low = lane < FOX_DH
        key_le_query = lax.broadcasted_iota(jnp.int32, (tq, tq), 0) <= lax.broadcasted_iota(jnp.int32, (tq, tq), 1)

        def by_head(c, a, b, col):
            h = STEP_HEADS * g + 2 * c
            return jnp.where(lane == h, a[:, col:col + 1], jnp.where(lane == h + 1, b[:, col:col + 1], 0.0))

        kbs = [ka_ref[h] for h in range(STEP_HEADS)]
        vbs = [va_ref[h] for h in range(STEP_HEADS)]
        kts = [kt_ref[h] for h in range(STEP_HEADS)]

        def step(i, carry, diagonal):
            st = [_dot(kbs[h], qbt_ref[h, i]) for h in range(STEP_HEADS)]
            dpt = [_dot(vbs[h], dot_ref[h, i]) for h in range(STEP_HEADS)]
            new = []
            for h in range(STEP_HEADS):
                p = jnp.exp(st[h])
                if diagonal:
                    p = jnp.where(key_le_query, p, 0.0)
                ds = (p * dpt[h]).astype(BF16)
                dq_ref[h, i] += _dot(kts[h], ds)
                dk, dv = carry[h]
                new.append((dk + _dot(ds, qab_ref[h, i]), dv + _dot(p.astype(BF16), doa_ref[h, i])))
            return tuple(new)

        zero = jnp.zeros((tq, LANE), F32)
        carry = step(j, tuple((zero, zero) for _ in range(STEP_HEADS)), True)
        carry = lax.fori_loop(j + 1, nq, lambda i, cr: step(i, cr, False), carry)
        rows = pl.ds(pl.multiple_of(j * tq, tq), tq)
        for c in range(STEP_PAIRS):
            (dka, dva), (dkb, dvb) = carry[2 * c], carry[2 * c + 1]
            c0 = 3 * LANE * c
            dp_ref[rows, c0 + LANE:c0 + 2 * LANE] = _pair(dka, dkb, low).astype(BF16)
            dp_ref[rows, c0 + 2 * LANE:c0 + 3 * LANE] = _pair(dva, dvb, low).astype(BF16)
            dcs_ref[rows, :] += by_head(c, dka, dkb, L_ONE_Q)

        @pl.when(j == nq - 1)
        def _():
            for c in range(STEP_PAIRS):
                for blk in range(nq):
                    r = slice(blk * tq, (blk + 1) * tq)
                    a, b = dq_ref[2 * c, blk].T, dq_ref[2 * c + 1, blk].T
                    dp_ref[r, 3 * LANE * c:3 * LANE * c + LANE] = (_pair(a, b, low) * QK_SCALE).astype(BF16)
                    drs_ref[r, :] += by_head(c, a, b, L_ONE_K)

    whole = pl.BlockSpec((STEP_HEADS, nq, tq, LANE), lambda g, j: (g, 0, 0, 0), pipeline_mode=pl.Buffered(1))
    whole_t = pl.BlockSpec((STEP_HEADS, nq, LANE, tq), lambda g, j: (g, 0, 0, 0), pipeline_mode=pl.Buffered(1))
    blk = pl.BlockSpec((STEP_HEADS, None, tq, LANE), lambda g, j: (g, j, 0, 0))
    blk_t = pl.BlockSpec((STEP_HEADS, None, LANE, tq), lambda g, j: (g, j, 0, 0))
    sums = pl.BlockSpec((t, LANE), lambda g, j: (0, 0), pipeline_mode=pl.Buffered(1))
    return _pcall(
        body, name=name, grid=(FOX_GROUPS, nq),
        in_specs=[whole, whole, whole_t, whole_t, blk, blk, blk_t, pl.BlockSpec(memory_space=pl.ANY)],
        out_specs=[pl.BlockSpec((t, slab), lambda g, j: (0, group0 + g)), sums, sums],
        out_shape=[jax.ShapeDtypeStruct(dproj.shape, BF16), jax.ShapeDtypeStruct((t, LANE), F32),
                   jax.ShapeDtypeStruct((t, LANE), F32)],
        input_output_aliases={7: 0},
        scratch_shapes=[pltpu.VMEM((STEP_HEADS, nq, LANE, tq), F32)],
        compiler_params=_params("arbitrary", "arbitrary"),
    )(qab, doa, qbt, dot_, ka, va, kt, dproj)


HALO = 8


def _rows_ext(ref, r0, rows, t, before, after):
    lo, hi = r0 - before, r0 + rows + after
    width = ref.shape[-1]
    parts = []
    if lo < 0:
        parts.append(jnp.zeros((-lo, width), F32))
    parts.append(ref[max(lo, 0):min(hi, t), :].astype(F32))
    if hi > t:
        parts.append(jnp.zeros((hi - t, width), F32))
    return parts[0] if len(parts) == 1 else jnp.concatenate(parts, axis=0)


def _conv_taps(a_ext, r0_ext, cw_ref, cb_ref):
    n = a_ext.shape[0]
    if r0_ext < N_PAD:
        row = r0_ext + lax.broadcasted_iota(jnp.int32, (n, 1), 0)
        a_ext = jnp.where(row >= N_PAD, a_ext, 0.0)
    a1 = pltpu.roll(a_ext, 1, axis=0)
    a2 = pltpu.roll(a_ext, 2, axis=0)
    acc = cb_ref[...] + a2 * cw_ref[0:1, :] + a1 * cw_ref[1:2, :] + a_ext * cw_ref[2:3, :]
    return a_ext, a1, a2, acc


FF_COLS = 256


def _up_conv_fwd(n2, w_up_t, conv_w8, conv_b, name):
    t, d = n2.shape
    f = w_up_t.shape[1]
    rows = TOK_TILE
    starts = list(range(0, t, rows))

    def body(n_ref, wa_ref, wb_ref, cw_ref, cb_ref, up_ref, g_ref):
        wa, wb = wa_ref[...], wb_ref[...]

        def project(r0):
            n_rows = n_ref[r0:r0 + rows, :]
            up_ref[0, r0:r0 + rows, :] = _dot(n_rows, wa, NT)
            up_ref[1, r0:r0 + rows, :] = _dot(n_rows, wb, NT)

        def activate(r0):
            a_ext = _rows_ext(up_ref.at[0], r0, rows, t, HALO, 0)
            _, _, _, acc = _conv_taps(a_ext, r0 - HALO, cw_ref, cb_ref)
            acc = acc[HALO:, :]
            g_ref[r0:r0 + rows, :] = (acc * _sigmoid(acc) * up_ref[1, r0:r0 + rows, :]).astype(BF16)

        project(starts[0])
        for r0, r_next in zip(starts, starts[1:] + [None]):
            if r_next is not None:
                project(r_next)
            activate(r0)

    return _pcall(
        body, name=name, grid=(f // FF_COLS,),
        in_specs=[pl.BlockSpec((t, d), lambda j: (0, 0), pipeline_mode=pl.Buffered(1)),
                  pl.BlockSpec((None, FF_COLS, d), lambda j: (0, j, 0)), pl.BlockSpec((None, FF_COLS, d), lambda j: (1, j, 0)),
                  pl.BlockSpec((8, FF_COLS), lambda j: (0, j)), pl.BlockSpec((1, FF_COLS), lambda j: (0, j))],
        out_specs=[pl.BlockSpec((2, t, FF_COLS), lambda j: (0, 0, j)), pl.BlockSpec((t, FF_COLS), lambda j: (0, j))],
        out_shape=[jax.ShapeDtypeStruct((2, t, f), F32), jax.ShapeDtypeStruct((t, f), BF16)],
        compiler_params=_params("parallel"),
    )(n2, w_up_t, w_up_t, conv_w8, conv_b)


def _dg_conv_bwd(up, conv_w8, conv_b, dh2, w_down, name):
    _, t, f = up.shape
    d = dh2.shape[1]
    rows = TOK_TILE
    starts = list(range(0, t, rows))

    def body(a_ref, b_ref, cw_ref, cb_ref, dh_ref, wd_ref, dup_ref, gcw_ref, gcb_ref, dg_ref):
        wd = wd_ref[...]

        def project(r0):
            dg_ref[r0:r0 + rows, :] = _dot(dh_ref[r0:r0 + rows, :], wd, NT)

        gw = [jnp.zeros((1, FF_COLS), F32) for _ in range(3)]
        gb = jnp.zeros((1, FF_COLS), F32)
        project(starts[0])
        for r0, r_next in zip(starts, starts[1:] + [None]):
            if r_next is not None:
                project(r_next)
            a_ext = _rows_ext(a_ref, r0, rows, t, HALO, HALO)
            b_ext = _rows_ext(b_ref, r0, rows, t, HALO, HALO)
            dg_ext = _rows_ext(dg_ref, r0, rows, t, HALO, HALO)
            a0, a1, a2, acc = _conv_taps(a_ext, r0 - HALO, cw_ref, cb_ref)
            sg = _sigmoid(acc)
            dacc = dg_ext * b_ext * (sg * (1.0 + acc * (1.0 - sg)))
            n = dacc.shape[0]
            da = (dacc * cw_ref[2:3, :] + pltpu.roll(dacc, n - 1, axis=0) * cw_ref[1:2, :]
                  + pltpu.roll(dacc, n - 2, axis=0) * cw_ref[0:1, :])
            core = slice(HALO, HALO + rows)
            da = da[core, :]
            if r0 < N_PAD:
                row = r0 + lax.broadcasted_iota(jnp.int32, (rows, 1), 0)
                da = jnp.where(row >= N_PAD, da, 0.0)
            dup_ref[0, r0:r0 + rows, :] = da.astype(BF16)
            dup_ref[1, r0:r0 + rows, :] = (dg_ext * acc * sg)[core, :].astype(BF16)
            dacc_c = dacc[core, :]
            gw[0] = gw[0] + jnp.sum(dacc_c * a2[core, :], axis=0, keepdims=True)
            gw[1] = gw[1] + jnp.sum(dacc_c * a1[core, :], axis=0, keepdims=True)
            gw[2] = gw[2] + jnp.sum(dacc_c * a0[core, :], axis=0, keepdims=True)
            gb = gb + jnp.sum(dacc_c, axis=0, keepdims=True)
        gcw_ref[...] = jnp.zeros((8, FF_COLS), F32)
        for tap in range(3):
            gcw_ref[tap:tap + 1, :] = gw[tap]
        gcb_ref[...] = gb

    return _pcall(
        body, name=name, grid=(f // FF_COLS,),
        in_specs=[pl.BlockSpec((None, t, FF_COLS), lambda j: (0, 0, j)), pl.BlockSpec((None, t, FF_COLS), lambda j: (1, 0, j)),
                  pl.BlockSpec((8, FF_COLS), lambda j: (0, j)), pl.BlockSpec((1, FF_COLS), lambda j: (0, j)),
                  pl.BlockSpec((t, d), lambda j: (0, 0), pipeline_mode=pl.Buffered(1)),
                  pl.BlockSpec((FF_COLS, d), lambda j: (j, 0))],
        out_specs=[pl.BlockSpec((2, t, FF_COLS), lambda j: (0, 0, j)), pl.BlockSpec((8, FF_COLS), lambda j: (0, j)),
                   pl.BlockSpec((1, FF_COLS), lambda j: (0, j))],
        out_shape=[jax.ShapeDtypeStruct((2, t, f), BF16), jax.ShapeDtypeStruct((8, f), F32),
                   jax.ShapeDtypeStruct((1, f), F32)],
        scratch_shapes=[pltpu.VMEM((t, FF_COLS), F32)],
        compiler_params=_params("parallel"),
    )(up, up, conv_w8, conv_b, dh2, w_down)


def _exchange(arrays, kinds, name, after=None):
    n = len(arrays)
    npeer = N_DEV - 1
    n_in = n + int(after is not None)

    def body(*refs):
        ins, outs = refs[:n], refs[n_in:n_in + n]
        send_sems, recv_sems, local_sems = refs[n_in + n:]
        x, y, c = lax.axis_index("x"), lax.axis_index("y"), lax.axis_index("c")
        me = 4 * x + 2 * y + c
        copies, locals_ = [], []
        for a in range(n):
            gather = kinds[a] == "gather"
            own = pltpu.make_async_copy(ins[a] if gather else ins[a].at[me], outs[a].at[me], local_sems.at[a])
            own.start()
            locals_.append(own)
            for d in range(1, N_DEV):
                px = 1 - x if d & 4 else x
                py = 1 - y if d & 2 else y
                pc = 1 - c if d & 1 else c
                src = ins[a] if gather else ins[a].at[4 * px + 2 * py + pc]
                cp = pltpu.make_async_remote_copy(
                    src_ref=src, dst_ref=outs[a].at[me],
                    send_sem=send_sems.at[a * npeer + d - 1], recv_sem=recv_sems.at[a * npeer + d - 1],
                    device_id=(px, py, pc), device_id_type=pl.DeviceIdType.MESH)
                cp.start()
                copies.append(cp)
        for cp in copies:
            cp.wait_recv()
        for cp in copies:
            cp.wait_send()
        for own in locals_:
            own.wait()

    out_shape = [jax.ShapeDtypeStruct((N_DEV,) + (a.shape if k == "gather" else a.shape[1:]), a.dtype)
                 for a, k in zip(arrays, kinds)]
    return _pcall(
        body, name=name,
        in_specs=[pl.BlockSpec(memory_space=pl.ANY)] * n_in,
        out_specs=[pl.BlockSpec(memory_space=pl.ANY)] * n,
        out_shape=out_shape,
        scratch_shapes=[pltpu.SemaphoreType.DMA((n * npeer,)), pltpu.SemaphoreType.DMA((n * npeer,)),
                        pltpu.SemaphoreType.DMA((n,))],
        compiler_params=pltpu.CompilerParams(has_side_effects=True),
    )(*arrays, *([] if after is None else [after]))


def _peer_copies(srcs, lands, kinds, send_sems, recv_sems):
    x, y, c = lax.axis_index("x"), lax.axis_index("y"), lax.axis_index("c")
    me = 4 * x + 2 * y + c
    copies = []
    for a in range(len(srcs)):
        for d in range(1, N_DEV):
            px = 1 - x if d & 4 else x
            py = 1 - y if d & 2 else y
            pc = 1 - c if d & 1 else c
            k = a * (N_DEV - 1) + d - 1
            copies.append(pltpu.make_async_remote_copy(
                src_ref=srcs[a] if kinds[a] == "gather" else srcs[a].at[4 * px + 2 * py + pc], dst_ref=lands[a].at[me],
                send_sem=send_sems.at[k], recv_sem=recv_sems.at[k],
                device_id=(px, py, pc), device_id_type=pl.DeviceIdType.MESH))
    return copies


def _exchange_start(arrays, kinds, name, after=None):
    n = len(arrays)
    nsem = n * (N_DEV - 1)
    hbm = pl.BlockSpec(memory_space=pltpu.HBM)
    sem = pl.BlockSpec(memory_space=pltpu.SEMAPHORE)
    land_shapes = [(N_DEV,) + (a.shape if k == "gather" else a.shape[1:]) for a, k in zip(arrays, kinds)]

    n_in = 2 * n + int(after is not None)

    def body(*refs):
        srcs, lands = refs[:n], refs[n:2 * n]
        send_sems, recv_sems = refs[n_in], refs[n_in + 1]
        token = refs[-1]
        for cp in _peer_copies(srcs, lands, kinds, send_sems, recv_sems):
            cp.start()
        token[...] = jnp.zeros_like(token)

    operands = [pltpu.with_memory_space_constraint(a, pltpu.HBM) for a in arrays]
    operands += [pltpu.with_memory_space_constraint(lax.empty(s, a.dtype), pltpu.HBM) for s, a in zip(land_shapes, arrays)]
    operands += [] if after is None else [after]
    out = _pcall(
        body, name=name,
        in_specs=[hbm] * (2 * n) + ([] if after is None else [pl.BlockSpec(memory_space=pl.ANY)]),
        out_specs=[sem, sem] + [hbm] * (2 * n) + [pl.BlockSpec(memory_space=pltpu.VMEM)],
        out_shape=[pltpu.SemaphoreType.DMA((nsem,)), pltpu.SemaphoreType.DMA((nsem,))]
        + [pltpu.HBM(a.shape, a.dtype) for a in arrays]
        + [pltpu.HBM(s, a.dtype) for s, a in zip(land_shapes, arrays)]
        + [jax.ShapeDtypeStruct((8, LANE), F32)],
        input_output_aliases={k: 2 + k for k in range(2 * n)},
        compiler_params=pltpu.CompilerParams(has_side_effects=pltpu.SideEffectType.DATAFLOW_SIDE_EFFECTING),
    )(*operands)
    return out[0], out[1], list(out[2:2 + n]), list(out[2 + n:2 + 2 * n]), out[-1]


def _exchange_wait(started, kinds, after, name):
    send_sems, recv_sems, srcs, lands, _ = started
    n = len(srcs)
    hbm = pl.BlockSpec(memory_space=pltpu.HBM)
    sem = pl.BlockSpec(memory_space=pltpu.SEMAPHORE)

    def body(*refs):
        src_refs, land_refs = refs[:n], refs[n:2 * n]
        copies = _peer_copies(src_refs, land_refs, kinds, refs[2 * n], refs[2 * n + 1])
        for cp in copies:
            cp.wait_send()
        for cp in copies:
            cp.wait_recv()

    out = _pcall(
        body, name=name,
        in_specs=[hbm] * (2 * n) + [sem, sem, pl.BlockSpec(memory_space=pl.ANY)],
        out_specs=[hbm] * (2 * n),
        out_shape=[pltpu.HBM(a.shape, a.dtype) for a in srcs + lands],
        input_output_aliases={k: k for k in range(2 * n)},
        compiler_params=pltpu.CompilerParams(has_side_effects=pltpu.SideEffectType.DATAFLOW_SIDE_EFFECTING),
    )(*srcs, *lands, send_sems, recv_sems, after)
    me = 4 * lax.axis_index("x") + 2 * lax.axis_index("y") + lax.axis_index("c")
    filled = []
    for src, land, kind in zip(out[:n], out[n:], kinds):
        own = src if kind == "gather" else lax.dynamic_index_in_dim(src, me, axis=0, keepdims=False)
        filled.append(lax.dynamic_update_slice(land, own[None], (me,) + (0,) * own.ndim))
    return filled


def _sum_slots(slots, name, rows_tile):
    nd, r, c = slots.shape

    def body(s_ref, o_ref):
        acc = s_ref[0].astype(F32)
        for p in range(1, nd):
            acc = acc + s_ref[p].astype(F32)
        o_ref[...] = acc

    return _pcall(
        body, name=name, grid=(r // rows_tile,),
        in_specs=[pl.BlockSpec((nd, rows_tile, c), lambda i: (0, i, 0))],
        out_specs=pl.BlockSpec((rows_tile, c), lambda i: (i, 0)),
        out_shape=jax.ShapeDtypeStruct((r, c), F32),
        compiler_params=_params("parallel"),
    )(slots)


def _sum_slots_small(slot_arrays, name):
    n = len(slot_arrays)

    def body(*refs):
        for s_ref, o_ref in zip(refs[:n], refs[n:]):
            acc = s_ref[0]
            for p in range(1, s_ref.shape[0]):
                acc = acc + s_ref[p]
            o_ref[...] = acc

    return _pcall(body, name=name, out_shape=[jax.ShapeDtypeStruct(a.shape[1:], F32) for a in slot_arrays])(*slot_arrays)


def _adamw_update(w_ref, g_ref, m_ref, v_ref, d_ref, nm_ref, nv_ref):
    gr = g_ref[...]
    nm = ADAM_B1 * m_ref[...] + (1.0 - ADAM_B1) * gr
    nv = ADAM_B2 * v_ref[...] + (1.0 - ADAM_B2) * (gr * gr)
    m_hat = nm / (1.0 - ADAM_B1 ** ADAM_STEP)
    v_hat = nv / (1.0 - ADAM_B2 ** ADAM_STEP)
    d_ref[...] = -ADAM_LR * (m_hat / (jnp.sqrt(v_hat) + ADAM_EPS) + ADAM_WD * w_ref[...])
    nm_ref[...] = nm
    nv_ref[...] = nv


def _adamw_small(ws, gs, ms, vs, name):
    n = len(ws)

    def body(*refs):
        ins, outs = refs[:4 * n], refs[4 * n:]
        for k in range(n):
            _adamw_update(ins[k], ins[n + k], ins[2 * n + k], ins[3 * n + k], outs[k], outs[n + k], outs[2 * n + k])

    shapes = [jax.ShapeDtypeStruct(w.shape, F32) for w in ws]
    out = _pcall(body, name=name, out_shape=shapes * 3)(*ws, *gs, *ms, *vs)
    return list(out[:n]), list(out[n:2 * n]), list(out[2 * n:])


def _adamw(w, g, m, v, name, rows_tile):
    r, c = w.shape
    body = lambda *refs: _adamw_update(*refs)
    spec = pl.BlockSpec((rows_tile, c), lambda i: (i, 0))
    shp = jax.ShapeDtypeStruct((r, c), F32)
    return _pcall(
        body, name=name, grid=(r // rows_tile,), in_specs=[spec] * 4, out_specs=[spec] * 3, out_shape=[shp] * 3,
        compiler_params=_params("parallel"),
    )(w, g, m, v)


F0 = 2 * RET_QK + 2 * RET_V


def _to_internal_rows(w_t):
    cols = w_t.shape[1]
    fox = w_t[F0:F0 + 3 * FOX_W].reshape(3, FOX_PAIRS, LANE, cols).transpose(1, 0, 2, 3).reshape(3 * FOX_W, cols)
    tail = jnp.zeros((IN_PAD - IN_WIDTH, cols), w_t.dtype)
    return jnp.concatenate([w_t[:F0], fox, w_t[F0 + 3 * FOX_W:], tail], axis=0)


def _from_internal_rows(g_t):
    cols = g_t.shape[1]
    fox = g_t[F0:F0 + 3 * FOX_W].reshape(FOX_PAIRS, 3, LANE, cols).transpose(1, 0, 2, 3).reshape(3 * FOX_W, cols)
    return jnp.concatenate([g_t[:F0], fox, g_t[F0 + 3 * FOX_W:F0 + 3 * FOX_W + FOX_HEADS]], axis=0)


def _local_step(x, target, meta, attn_g, fox_b, ret_g, ffn_g, conv_w8, conv_b, final_g,
                first_weight, late_weights, ffn_grads_ready, out_grad_ready, in_grad_ready):
    seq, d = x.shape
    t = seq + PREFIX
    tm = TOK_TILE
    nq = t // tm
    fox_b128 = jnp.pad(fox_b, ((0, 0), (0, LANE - FOX_HEADS)))

    h0, n1 = _prep_norm(x, meta, attn_g, "prep_norm")
    w_in_t = first_weight(n1)
    proj = _mm_simple(n1, w_in_t, mode="nt", tm=tm, tn=IN_PAD, tk=d, out_dtype=F32, name="mm_in")
    cos, sin = _rope_tables(t)
    o_pre, mixed, states = _ret_fwd(proj, cos, sin, ret_g, "ret_fwd")
    c = _forget_cumsum(proj, fox_b128, "forget_cumsum")
    qa, ka, va, qt, kt, vt = _fox_prep(proj, c, "fox_prep")
    by_block = lambda a: a.reshape(FOX_HEADS, nq, tm, LANE)
    mixed, o_fox, lse = _fox_fwd(qt, by_block(ka), vt, mixed, "fox_fwd")
    w_out, w_up_t, w_down = late_weights(o_fox)
    tile = pl.BlockSpec((tm, d), lambda i: (i, 0))
    row_vec = pl.BlockSpec((1, d), lambda i: (0, 0))
    resident = lambda shape: pl.BlockSpec(shape, lambda i: (0,) * len(shape), pipeline_mode=pl.Buffered(1))
    acts = lambda dtype: jax.ShapeDtypeStruct((t, d), dtype)
    vec = jax.ShapeDtypeStruct((1, d), F32)

    def residual_and_norm(i, acc, ins, outs):
        h = acc + ins[0][...]
        outs[0][...] = h
        outs[1][...] = (h * lax.rsqrt(jnp.mean(h * h, axis=-1, keepdims=True) + EPS) * ins[1][...]).astype(BF16)

    h1, n2 = _matmul_rows([mixed], [tile], [w_out], [resident((d, d))], [h0, ffn_g], [tile, row_vec],
                          [tile, tile], [acts(F32), acts(BF16)], residual_and_norm, mode="nn", steps=nq, name="mm_out_norm")
    nf = D_FF // 1408
    up, g = _up_conv_fwd(n2, w_up_t, conv_w8, conv_b, "up_conv_fwd")

    def residual_loss_bwd(i, acc, ins, outs):
        loss_ref, dh_ref, dhb_ref, gg_ref = outs
        part, dh, gg = _loss_tile(i, acc + ins[0][...], jnp.concatenate([ins[1][...], ins[2][...], ins[3][...]], axis=0),
                                  ins[4][...])
        _accumulate(loss_ref, i, jnp.broadcast_to(part, loss_ref.shape))
        dh_ref[...] = dh
        dhb_ref[...] = dh.astype(BF16)
        _accumulate(gg_ref, i, gg)

    loss_tile, dh2, dh2_b, g_final = _matmul_rows(
        [g], [pl.BlockSpec((tm, D_FF), lambda i: (i, 0))], [w_down], [resident((D_FF, d))],
        [h1, target, target, target, final_g], [tile] + _shifted_row_specs(d) + [row_vec],
        [pl.BlockSpec((8, LANE), lambda i: (0, 0)), tile, tile, row_vec],
        [jax.ShapeDtypeStruct((8, LANE), F32), acts(F32), acts(BF16), vec], residual_loss_bwd,
        mode="nn", steps=nq, name="mm_down_loss")

    tkw = 2112 if t % 2112 == 0 else tm
    gw_down = _mm_simple(g, dh2_b, mode="tn", tm=1408, tn=d, tk=tkw, out_dtype=BF16, name="mm_gw_down")
    dup, g_conv_w8, g_conv_b = _dg_conv_bwd(up, conv_w8, conv_b, dh2_b, w_down, "dg_conv_bwd")

    half = lambda p: pl.BlockSpec((None, tm, D_FF), lambda i: (p, i, 0))
    half_w = lambda p: pl.BlockSpec((None, D_FF, d), lambda i: (p, 0, 0), pipeline_mode=pl.Buffered(1))
    gw_up_t = _matmul(
        dup, n2, mode="tn", grid=(2 * nf, 1, t // tkw),
        a_spec=pl.BlockSpec((None, tkw, 1408), lambda i, j, k: (i // nf, k, i % nf)),
        b_spec=pl.BlockSpec((tkw, d), lambda i, j, k: (k, 0)),
        o_spec=pl.BlockSpec((1408, d), lambda i, j, k: (i, 0)),
        out_shape=jax.ShapeDtypeStruct((2 * D_FF, d), BF16), name="mm_gw_up")
    def norm_bwd_and_mixer_grad(i, acc, ins, outs):
        dh, gg = _rms_bwd_tile(acc, ins[0][...], ins[1][...], ins[2][...])
        outs[0][...] = dh
        _accumulate(outs[1], i, gg)
        outs[2][...] = _dot(dh.astype(BF16), ins[3][...], NT)

    dh1, g_ffn, dmixed = _matmul_rows(
        [dup, dup], [half(0), half(1)], [w_up_t, w_up_t], [half_w(0), half_w(1)],
        [h1, ffn_g, dh2, w_out], [tile, row_vec, tile, resident((d, d))], [tile, row_vec, tile],
        [acts(F32), vec, acts(F32)], norm_bwd_and_mixer_grad,
        mode="nn", steps=nq, name="mm_dn2_norm_bwd", after=ffn_grads_ready(gw_down, gw_up_t))
    gw_out = _mm_simple(mixed, dh1, mode="tn", tm=d, tn=d, tk=tkw, out_dtype=BF16, name="mm_gw_out")
    dproj, g_ret = _ret_bwd(proj, cos, sin, ret_g + out_grad_ready(gw_out), dmixed, o_pre, states, "ret_bwd")
    qab, doa, qbt, dot_ = _fox_prep_bwd(dmixed, o_fox, lse, qa, "fox_prep_bwd")
    dproj, drs, dcs = _fox_bwd(by_block(qab), by_block(doa), qbt, dot_, by_block(ka), by_block(va), kt, dproj, "fox_bwd")
    dproj, g_fox_b = _forget_cumsum_bwd(proj, fox_b128, drs, dcs, dproj, "forget_cumsum_bwd")
    gw_in_t = _mm_simple(dproj, n1, mode="tn", tm=640, tn=d, tk=tkw, out_dtype=BF16, name="mm_gw_in")
    sent = in_grad_ready(gw_in_t)
    def input_grads(i, acc, ins, outs):
        gx_ref, gmeta_ref, gg_ref, buf_ref, sems = outs
        dh, gg = _rms_bwd_tile(acc, ins[0][...], ins[1][...], ins[2][...])
        _accumulate(gg_ref, i, gg)
        slot = i % 2

        def first_copy():
            return pltpu.make_async_copy(buf_ref.at[0, pl.ds(PREFIX, tm - PREFIX)], gx_ref.at[pl.ds(0, tm - PREFIX)],
                                         sems.at[0])

        def tile_copy(tile, buf_slot):
            rows = pl.ds(pl.multiple_of(tile * tm - PREFIX, PREFIX), tm)
            return pltpu.make_async_copy(buf_ref.at[buf_slot], gx_ref.at[rows], sems.at[buf_slot])

        @pl.when(i == 1)
        def _():
            first_copy().wait()

        @pl.when(i >= 2)
        def _():
            tile_copy(i - 1, 1 - slot).wait()

        buf_ref[slot] = dh

        @pl.when(i == 0)
        def _():
            gmeta_ref[...] = dh[N_PAD:PREFIX, :]
            first_copy().start()

        @pl.when(i > 0)
        def _():
            tile_copy(i, slot).start()

        @pl.when(i == nq - 1)
        def _():
            tile_copy(i, slot).wait()

    grad_x, g_meta, g_attn = _matmul_rows(
        [dproj], [pl.BlockSpec((tm, IN_PAD), lambda i: (i, 0))], [w_in_t], [resident((IN_PAD, d))],
        [h0, attn_g, dh1], [tile, row_vec, tile],
        [pl.BlockSpec(memory_space=pl.ANY), pl.BlockSpec((N_META, d), lambda i: (0, 0)), row_vec],
        [jax.ShapeDtypeStruct((seq, d), F32), jax.ShapeDtypeStruct((N_META, d), F32), vec], input_grads,
        mode="nn", steps=nq, name="mm_dn1_norm_bwd", after=sent,
        scratch=[pltpu.VMEM((2, tm, d), F32), pltpu.SemaphoreType.DMA((2,))])

    grads = dict(meta=g_meta, attn_g=g_attn, fox_b=g_fox_b, ret_g=g_ret,
                 ffn_g=g_ffn, conv_w=g_conv_w8, conv_b=g_conv_b, final_g=g_final)
    return loss_tile, grad_x, grads


def kernel(x, meta_tokens, attn_norm_g, w_in, fox_forget_b, ret_norm_g, w_out, ffn_norm_g, w_up, conv_w, conv_b, w_down, final_norm_g, loss_target, m_meta_tokens, m_attn_norm_g, m_w_in, m_fox_forget_b, m_ret_norm_g, m_w_out, m_ffn_norm_g, m_w_up, m_conv_w, m_conv_b, m_w_down, m_final_norm_g, v_meta_tokens, v_attn_norm_g, v_w_in, v_fox_forget_b, v_ret_norm_g, v_w_out, v_ffn_norm_g, v_w_up, v_conv_w, v_conv_b, v_w_down, v_final_norm_g):
    d = D_MODEL
    me = 4 * lax.axis_index("x") + 2 * lax.axis_index("y") + lax.axis_index("c")
    in_blk = IN_WIDTH // N_DEV
    in_blk_pad = 400
    up_blk = 2 * D_FF // N_DEV
    down_blk = D_FF // N_DEV
    cw_blk = D_FF // N_DEV

    w_in_loc = jnp.pad(w_in[0].T.astype(BF16), ((0, in_blk_pad - in_blk), (0, 0)))
    cw_loc = jnp.pad(conv_w[0], ((0, 5), (0, 384 - cw_blk)))
    g_meta, g_cw = _exchange([meta_tokens, cw_loc], ["gather"] * 2, "gather_small")
    first = _exchange_start([w_in_loc], ["gather"], "gather_in_start", after=g_meta)
    rest_loc = [(w_out[0] + first[-1][0:1, 0:1]).astype(BF16), w_up[0].T.astype(BF16), w_down[0].astype(BF16)]
    rest = _exchange_start(rest_loc, ["gather"] * 3, "gather_rest_start")
    meta_f = g_meta.transpose(1, 0, 2).reshape(N_META, d)
    conv_w8 = jnp.pad(g_cw[:, :3, :cw_blk].transpose(1, 0, 2).reshape(3, D_FF), ((0, 5), (0, 0)))
    pending = {}

    def first_weight(after):
        (g_in,) = _exchange_wait(first, ["gather"], after, "gather_in_wait")
        return _to_internal_rows(g_in[:, :in_blk].reshape(IN_WIDTH, d))

    def in_grad_ready(gw_in_t):
        blocks = _from_internal_rows(gw_in_t).reshape(N_DEV, in_blk, d)
        blocks = jnp.pad(blocks, ((0, 0), (0, in_blk_pad - in_blk), (0, 0)))
        pending["in"] = _exchange_start([blocks], ["scatter"], "grads_in_start")
        return pending["in"][-1][0:1, 0:1]

    def late_weights(after):
        g_out, g_up, g_down = _exchange_wait(rest, ["gather"] * 3, after, "gather_rest_wait")
        return g_out.reshape(d, d), g_up.reshape(2, D_FF, d), g_down.reshape(D_FF, d)

    def ffn_grads_ready(gw_down, gw_up_t):
        blocks = [gw_down.reshape(N_DEV, down_blk, d), gw_up_t.reshape(N_DEV, up_blk, d)]
        pending["ffn"] = _exchange_start(blocks, ["scatter"] * 2, "grads_ffn_start")
        return pending["ffn"][-1][0:1, 0:1]

    def out_grad_ready(gw_out):
        pending["out"] = _exchange_start([gw_out.reshape(N_DEV, d // N_DEV, d)], ["scatter"], "grads_out_start")
        return pending["out"][-1][0:1, 0:1]

    loss_tile, grad_x, gr = _local_step(
        x[0], loss_target[0], meta_f, attn_norm_g + rest[-1][0:1, 0:1], fox_forget_b, ret_norm_g, ffn_norm_g,
        conv_w8, conv_b, final_norm_g.reshape(1, d), first_weight, late_weights, ffn_grads_ready, out_grad_ready,
        in_grad_ready)

    r_down, r_up = _exchange_wait(pending["ffn"], ["scatter"] * 2, grad_x, "grads_ffn_wait")
    (r_out,) = _exchange_wait(pending["out"], ["scatter"], grad_x, "grads_out_wait")
    g_w_out = _sum_slots(r_out, "sum_w_out", d // N_DEV)
    g_w_up_t = _sum_slots(r_up, "sum_w_up", up_blk)
    g_w_down = _sum_slots(r_down, "sum_w_down", down_blk)
    as_t = lambda a: a[0].T
    from_t = lambda a: a.T[None]
    d_w_out, m_w_out_n, v_w_out_n = [a[None] for a in _adamw(w_out[0], g_w_out, m_w_out[0], v_w_out[0], "adamw_w_out", 128)]
    up_t = _adamw(as_t(w_up), g_w_up_t, as_t(m_w_up), as_t(v_w_up), "adamw_w_up", up_blk // 2)
    d_w_up, m_w_up_n, v_w_up_n = [from_t(a) for a in up_t]
    d_w_down, m_w_down_n, v_w_down_n = [a[None] for a in _adamw(w_down[0], g_w_down, m_w_down[0], v_w_down[0],
                                                                "adamw_w_down", down_blk)]

    small = [loss_tile, gr["attn_g"], gr["fox_b"], gr["ret_g"], gr["ffn_g"], gr["conv_b"], gr["final_g"],
             gr["meta"], gr["conv_w"]]
    r_small = _exchange(small, ["gather"] * len(small), "exchange_small", after=up_t[0])
    (loss_all, g_attn, g_fox_b128, g_ret, g_ffn, g_conv_b, g_final, g_meta_full, g_cw_full) = _sum_slots_small(
        r_small, "sum_small")
    loss = loss_all[0, 0]
    g_fox_b = g_fox_b128[:, :FOX_HEADS]
    g_meta_loc = lax.dynamic_slice(g_meta_full, (0, me * (d // N_DEV)), (N_META, d // N_DEV))
    g_cw_loc = lax.dynamic_slice(g_cw_full, (0, me * cw_blk), (3, cw_blk))

    (r_in,) = _exchange_wait(pending["in"], ["scatter"], r_small[0], "grads_in_wait")
    g_w_in_t = _sum_slots(r_in, "sum_w_in", in_blk_pad)[:in_blk]
    d_w_in, m_w_in_n, v_w_in_n = [from_t(a) for a in _adamw(as_t(w_in), g_w_in_t, as_t(m_w_in), as_t(v_w_in),
                                                            "adamw_w_in", in_blk)]
    g_w_in, g_w_up = g_w_in_t.T, g_w_up_t.T
    row = lambda a: a.reshape(1, d)
    sm_grads = [g_meta_loc, g_attn, g_fox_b, g_ret, g_ffn, g_cw_loc, g_conv_b, g_final]
    sm_w = [meta_tokens, attn_norm_g, fox_forget_b, ret_norm_g, ffn_norm_g, conv_w[0], conv_b, row(final_norm_g)]
    sm_m = [m_meta_tokens, m_attn_norm_g, m_fox_forget_b, m_ret_norm_g, m_ffn_norm_g, m_conv_w[0], m_conv_b,
            row(m_final_norm_g)]
    sm_v = [v_meta_tokens, v_attn_norm_g, v_fox_forget_b, v_ret_norm_g, v_ffn_norm_g, v_conv_w[0], v_conv_b,
            row(v_final_norm_g)]
    dl, ml, vl = [lst[:7] + [lst[7].reshape(d)] for lst in _adamw_small(sm_w, sm_grads, sm_m, sm_v, "adamw_small")]

    def by_weight(meta_, attn_, w_in_, fox_, ret_, w_out_, ffn_, w_up_, cw_, cb_, w_down_, final_):
        return (meta_, attn_, w_in_, fox_, ret_, w_out_, ffn_, w_up_, cw_[None], cb_, w_down_, final_)

    grads_out = by_weight(g_meta_loc, g_attn, g_w_in[None], g_fox_b, g_ret, g_w_out[None], g_ffn, g_w_up[None], g_cw_loc,
                          g_conv_b, g_w_down[None], g_final.reshape(d))
    delta_out = by_weight(dl[0], dl[1], d_w_in, dl[2], dl[3], d_w_out, dl[4], d_w_up, dl[5], dl[6], d_w_down, dl[7])
    m_out = by_weight(ml[0], ml[1], m_w_in_n, ml[2], ml[3], m_w_out_n, ml[4], m_w_up_n, ml[5], ml[6], m_w_down_n, ml[7])
    v_out = by_weight(vl[0], vl[1], v_w_in_n, vl[2], vl[3], v_w_out_n, vl[4], v_w_up_n, vl[5], vl[6], v_w_down_n, vl[7])
    return (loss, grad_x[None]) + grads_out + delta_out + m_out + v_out
```

```python
import numpy as np
import jax
import jax.numpy as jnp
from jax import lax
from jax.experimental import pallas as pl
from jax.experimental.pallas import tpu as pltpu

F32 = jnp.float32
BF16 = jnp.bfloat16

D_MODEL = 1024
N_META = 16
N_PAD = 112
PREFIX = 128
RET_HEADS = 4
RET_DK = 64
RET_DV = 128
FOX_HEADS = 8
FOX_DH = 64
D_FF = 2816
ROPE_BASE = 10000.0
EPS = 1e-6
NEG = -1e30
RET_QK = RET_HEADS * RET_DK
RET_V = RET_HEADS * RET_DV
FOX_W = FOX_HEADS * FOX_DH
IN_WIDTH = 2 * RET_QK + 2 * RET_V + 3 * FOX_W + FOX_HEADS
IN_PAD = 3200
FF_COL_BLOCK = (IN_WIDTH - FOX_HEADS) // 128
QK_SCALE = 0.125

ADAM_LR = 0.001
ADAM_B1 = 0.9
ADAM_B2 = 0.999
ADAM_EPS = 1e-08
ADAM_WD = 0.01
ADAM_STEP = 10

N_DEV = 8
LANE = 128
ROW_TILE = 128
TOK_TILE = 384

NN = (((1,), (0,)), ((), ()))
NT = (((1,), (1,)), ((), ()))
TN = (((0,), (0,)), ((), ()))


def _pcall(body, **kw):
    return pl.pallas_call(body, **kw)


def _params(*sem):
    return pltpu.CompilerParams(dimension_semantics=sem)


def _dot(a, b, dims=NN):
    return lax.dot_general(a, b, dims, preferred_element_type=F32)


def _sigmoid(x):
    return 0.5 * jnp.tanh(0.5 * x) + 0.5


def _matmul(a, b, *, mode, grid, a_spec, b_spec, o_spec, out_shape, name, add=None, add_spec=None, after=None):
    dims = {"nn": NN, "nt": NT, "tn": TN}[mode]
    nk = grid[2]
    has_add = add is not None
    a_list, b_list = (list(a), list(b)) if isinstance(a, (list, tuple)) else ([a], [b])
    a_specs, b_specs = (list(a_spec), list(b_spec)) if isinstance(a_spec, (list, tuple)) else ([a_spec], [b_spec])
    nt = len(a_list)
    n_in = 2 * nt + int(has_add) + int(after is not None)

    def body(*refs):
        a_refs, b_refs = refs[:nt], refs[nt:2 * nt]
        add_ref = refs[2 * nt] if has_add else None
        o_ref = refs[n_in]
        part = _dot(a_refs[0][...].astype(BF16), b_refs[0][...].astype(BF16), dims)
        for ar, br in zip(a_refs[1:], b_refs[1:]):
            part = part + _dot(ar[...].astype(BF16), br[...].astype(BF16), dims)

        def finish(acc):
            if has_add:
                acc = acc + add_ref[...]
            o_ref[...] = acc.astype(o_ref.dtype)

        if nk == 1:
            finish(part)
        else:
            acc_ref = refs[-1]
            k = pl.program_id(2)

            @pl.when(k == 0)
            def _():
                acc_ref[...] = part

            @pl.when(k > 0)
            def _():
                acc_ref[...] += part

            @pl.when(k == nk - 1)
            def _():
                finish(acc_ref[...])

    in_specs = a_specs + b_specs + ([add_spec] if has_add else [])
    args = tuple(a_list) + tuple(b_list) + ((add,) if has_add else ())
    if after is not None:
        in_specs, args = in_specs + [pl.BlockSpec(memory_space=pl.ANY)], args + (after,)
    scratch = [] if nk == 1 else [pltpu.VMEM(tuple(d for d in o_spec.block_shape if d is not None), F32)]
    return _pcall(
        body, name=name, grid=grid, in_specs=in_specs, out_specs=o_spec, out_shape=out_shape,
        scratch_shapes=scratch, compiler_params=_params("parallel", "parallel", "arbitrary"),
    )(*args)


def _mm_simple(a, b, *, mode, tm, tn, tk, out_dtype, name, add=None, after=None):
    if mode == "tn":
        K, M = a.shape
    else:
        M, K = a.shape
    N = b.shape[0] if mode == "nt" else b.shape[1]
    grid = (M // tm, N // tn, K // tk)
    resident = dict(pipeline_mode=pl.Buffered(1)) if (tn == N and tk == K) else {}
    a_spec = pl.BlockSpec((tk, tm), lambda i, j, k: (k, i)) if mode == "tn" else pl.BlockSpec((tm, tk), lambda i, j, k: (i, k))
    b_spec = (pl.BlockSpec((tn, tk), lambda i, j, k: (j, k), **resident) if mode == "nt"
              else pl.BlockSpec((tk, tn), lambda i, j, k: (k, j), **resident))
    o_spec = pl.BlockSpec((tm, tn), lambda i, j, k: (i, j))
    return _matmul(a, b, mode=mode, grid=grid, a_spec=a_spec, b_spec=b_spec, o_spec=o_spec,
                   out_shape=jax.ShapeDtypeStruct((M, N), out_dtype), name=name, add=add,
                   add_spec=o_spec if add is not None else None, after=after)


def _matmul_rows(a_list, a_specs, b_list, b_specs, extras, extra_specs, out_specs, out_shape, epilogue, *,
                 mode, steps, name, after=None, scratch=()):
    dims = {"nn": NN, "nt": NT}[mode]
    nt, ne = len(a_list), len(extras)
    n_in = 2 * nt + ne + int(after is not None)

    def body(*refs):
        acc = _dot(refs[0][...].astype(BF16), refs[nt][...].astype(BF16), dims)
        for k in range(1, nt):
            acc = acc + _dot(refs[k][...].astype(BF16), refs[nt + k][...].astype(BF16), dims)
        epilogue(pl.program_id(0), acc, refs[2 * nt:2 * nt + ne], refs[n_in:])

    in_specs = list(a_specs) + list(b_specs) + list(extra_specs)
    args = tuple(a_list) + tuple(b_list) + tuple(extras)
    if after is not None:
        in_specs, args = in_specs + [pl.BlockSpec(memory_space=pl.ANY)], args + (after,)
    return _pcall(body, name=name, grid=(steps,), in_specs=in_specs, out_specs=out_specs, out_shape=out_shape,
                  scratch_shapes=list(scratch), compiler_params=_params("arbitrary"))(*args)


def _rms_bwd_tile(dy, x, gain, dres):
    r = lax.rsqrt(jnp.mean(x * x, axis=-1, keepdims=True) + EPS)
    xhat = x * r
    u = dy * gain
    return dres + r * (u - xhat * jnp.mean(u * xhat, axis=-1, keepdims=True)), jnp.sum(dy * xhat, axis=0, keepdims=True)


def _loss_tile(i, x, tgt, gain):
    d = x.shape[-1]
    r = lax.rsqrt(jnp.mean(x * x, axis=-1, keepdims=True) + EPS)
    xhat = x * r
    counted = (i * TOK_TILE + lax.broadcasted_iota(jnp.int32, (TOK_TILE, 1), 0)) >= PREFIX
    err = jnp.where(counted, xhat * gain - tgt, 0.0)
    dy = err * (1.0 / d)
    u = dy * gain
    dh = r * (u - xhat * jnp.mean(u * xhat, axis=-1, keepdims=True))
    return 0.5 * jnp.sum(jnp.mean(err * err, axis=-1, keepdims=True)), dh, jnp.sum(dy * xhat, axis=0, keepdims=True)


def _accumulate(ref, i, part):
    @pl.when(i == 0)
    def _():
        ref[...] = part

    @pl.when(i > 0)
    def _():
        ref[...] += part


def _prep_norm(x, meta, gain, name):
    seq, d = x.shape
    t = seq + PREFIX

    def body(xa_ref, xb_ref, xc_ref, meta_ref, g_ref, h_ref, n_ref):
        i = pl.program_id(0)

        @pl.when(i == 0)
        def _():
            h_ref[0:N_PAD, :] = jnp.zeros((N_PAD, d), F32)
            h_ref[N_PAD:ROW_TILE, :] = meta_ref[...]

        @pl.when(i > 0)
        def _():
            h_ref[0:ROW_TILE, :] = xa_ref[...]

        h_ref[ROW_TILE:2 * ROW_TILE, :] = xb_ref[...]
        h_ref[2 * ROW_TILE:3 * ROW_TILE, :] = xc_ref[...]
        h = h_ref[...]
        r = lax.rsqrt(jnp.mean(h * h, axis=-1, keepdims=True) + EPS)
        n_ref[...] = (h * r * g_ref[...]).astype(BF16)

    return _pcall(
        body, name=name, grid=(t // TOK_TILE,),
        in_specs=_shifted_row_specs(d) + [pl.BlockSpec((N_META, d), lambda i: (0, 0)), pl.BlockSpec((1, d), lambda i: (0, 0))],
        out_specs=[pl.BlockSpec((TOK_TILE, d), lambda i: (i, 0)), pl.BlockSpec((TOK_TILE, d), lambda i: (i, 0))],
        out_shape=[jax.ShapeDtypeStruct((t, d), F32), jax.ShapeDtypeStruct((t, d), BF16)],
        compiler_params=_params("parallel"),
    )(x, x, x, meta, gain)


def _shifted_row_specs(d):
    blocks_per_tile = TOK_TILE // ROW_TILE
    return [pl.BlockSpec((ROW_TILE, d), lambda i, r=r: (jnp.maximum(blocks_per_tile * i + r, 0), 0)) for r in (-1, 0, 1)]


def _ret_consts(bk):
    gam = 1.0 - 2.0 ** (-5.0 - np.arange(RET_HEADS))
    n = np.arange(bk)
    same_or_earlier_chunk = (n[None, :] // 64) <= (n[:, None] // 64)
    w = gam[:, None, None] ** np.abs(n[:, None] - n[None, :])[None] * same_or_earlier_chunk[None]
    wq = gam[:, None] ** (n[None, :] + 1.0)
    wk = gam[:, None] ** (bk - 1.0 - n[None, :])
    mask = (np.arange(RET_QK)[None, :] // RET_DK) == np.arange(RET_HEADS)[:, None]
    return (jnp.asarray(w, F32), jnp.asarray(wq[:, :, None], F32), jnp.asarray(wk[:, :, None], F32),
            jnp.asarray(mask[:, None, :], F32), [float(g ** bk) for g in gam])


def _rope_tables(t):
    half = RET_DK // 2
    inv = 1.0 / (ROPE_BASE ** (jnp.arange(half, dtype=F32) / half))
    ang = jnp.arange(t).astype(F32)[:, None] * inv[None, :]
    cos, sin = jnp.cos(ang), jnp.sin(ang)
    return (jnp.tile(jnp.concatenate([cos, cos], axis=1), (1, RET_HEADS)),
            jnp.tile(jnp.concatenate([-sin, sin], axis=1), (1, RET_HEADS)))


def _swap_halves(x):
    outs = []
    for s in range(x.shape[1] // LANE):
        xs = x[:, LANE * s:LANE * (s + 1)]
        lane = lax.broadcasted_iota(jnp.int32, xs.shape, 1)
        outs.append(jnp.where((lane & 32) == 0, pltpu.roll(xs, LANE - 32, axis=1), pltpu.roll(xs, 32, axis=1)))
    return outs[0] if len(outs) == 1 else jnp.concatenate(outs, axis=1)


def _rope(x, cos, sin_signed):
    return x * cos + _swap_halves(x) * sin_signed


def _rope_t(dx, cos, sin_signed):
    return dx * cos + _swap_halves(dx * sin_signed)


def _ret_fwd(proj, cos, sin, gain, name):
    t = proj.shape[0]
    bk = TOK_TILE
    nb = t // bk
    w, wq, wk, mask, g_blk = _ret_consts(bk)

    def body(q_ref, k_ref, v_ref, rg_ref, cos_ref, sin_ref, w_ref, wq_ref, wk_ref, mask_ref, gain_ref,
             opre_ref, og_ref, st_ref, r_ref):
        i = pl.program_id(0)

        @pl.when(i == 0)
        def _():
            r_ref[...] = jnp.zeros_like(r_ref)

        c, s = cos_ref[...], sin_ref[...]
        valid = ((i * bk + lax.broadcasted_iota(jnp.int32, (bk, 1), 0)) >= N_PAD).astype(F32)
        qr = _rope(q_ref[...], c, s)
        kr = _rope(k_ref[...], c, s) * QK_SCALE * valid
        kb = kr.astype(BF16)
        for h in range(RET_HEADS):
            hm = mask_ref[h]
            cols = slice(RET_DV * h, RET_DV * (h + 1))
            vh = v_ref[:, cols].astype(BF16)
            r_prev = r_ref[h]
            st_ref[0, h] = r_prev
            sm = _dot((qr * hm).astype(BF16), kb, NT) * w_ref[h]
            o = _dot(sm.astype(BF16), vh) + _dot((qr * (hm * wq_ref[h])).astype(BF16), r_prev.astype(BF16))
            r_ref[h] = g_blk[h] * r_prev + _dot((kr * wk_ref[h]).astype(BF16), vh, TN)
            opre_ref[:, cols] = o
            rstd = lax.rsqrt(jnp.mean(o * o, axis=-1, keepdims=True) + EPS)
            rg = rg_ref[:, cols]
            og_ref[:, cols] = (o * rstd * gain_ref[:, cols] * (rg * _sigmoid(rg))).astype(BF16)

    full = lambda shape: pl.BlockSpec(shape, lambda i: (0,) * len(shape))
    return _pcall(
        body, name=name, grid=(nb,),
        in_specs=[pl.BlockSpec((bk, RET_QK), lambda i: (i, 0)), pl.BlockSpec((bk, RET_QK), lambda i: (i, 1)),
                  pl.BlockSpec((bk, RET_V), lambda i: (i, 1)), pl.BlockSpec((bk, RET_V), lambda i: (i, 2)),
                  pl.BlockSpec((bk, RET_QK), lambda i: (i, 0)), pl.BlockSpec((bk, RET_QK), lambda i: (i, 0)),
                  full((RET_HEADS, bk, bk)), full((RET_HEADS, bk, 1)), full((RET_HEADS, bk, 1)),
                  full((RET_HEADS, 1, RET_QK)), full((1, RET_V))],
        out_specs=[pl.BlockSpec((bk, RET_V), lambda i: (i, 0)), pl.BlockSpec((bk, RET_V), lambda i: (i, 0)),
                   pl.BlockSpec((1, RET_HEADS, RET_QK, RET_DV), lambda i: (i, 0, 0, 0))],
        out_shape=[jax.ShapeDtypeStruct((t, RET_V), F32), jax.ShapeDtypeStruct((t, RET_V + FOX_W), BF16),
                   jax.ShapeDtypeStruct((nb, RET_HEADS, RET_QK, RET_DV), F32)],
        scratch_shapes=[pltpu.VMEM((RET_HEADS, RET_QK, RET_DV), F32)],
        compiler_params=_params("arbitrary"),
    )(proj, proj, proj, proj, cos, sin, w, wq, wk, mask, gain)


def _ret_bwd(proj, cos, sin, gain, dmixed, opre, states, name):
    t = proj.shape[0]
    bk = TOK_TILE
    nb = t // bk
    w, wq, wk, mask, g_blk = _ret_consts(bk)
    v0, g0 = 2 * RET_QK, 2 * RET_QK + RET_V

    def body(q_ref, k_ref, v_ref, rg_ref, cos_ref, sin_ref, w_ref, wq_ref, wk_ref, mask_ref, gain_ref,
             dog_ref, opre_ref, st_ref, dp_ref, gg_ref, dr_ref):
        step = pl.program_id(0)
        i = nb - 1 - step

        @pl.when(step == 0)
        def _():
            dr_ref[...] = jnp.zeros_like(dr_ref)
            gg_ref[...] = jnp.zeros_like(gg_ref)

        c, s = cos_ref[...], sin_ref[...]
        valid = ((i * bk + lax.broadcasted_iota(jnp.int32, (bk, 1), 0)) >= N_PAD).astype(F32)
        qr = _rope(q_ref[...], c, s)
        kr = _rope(k_ref[...], c, s) * QK_SCALE * valid
        kb = kr.astype(BF16)
        dqr = jnp.zeros((bk, RET_QK), F32)
        dkr = jnp.zeros((bk, RET_QK), F32)
        for h in range(RET_HEADS):
            hm = mask_ref[h]
            cols = slice(RET_DV * h, RET_DV * (h + 1))
            vh = v_ref[:, cols].astype(BF16)
            o = opre_ref[:, cols]
            rstd = lax.rsqrt(jnp.mean(o * o, axis=-1, keepdims=True) + EPS)
            xhat = o * rstd
            rg = rg_ref[:, cols]
            sg = _sigmoid(rg)
            gate = rg * sg
            gn = gain_ref[:, cols]
            dog = dog_ref[:, cols]
            dp_ref[:, g0 + RET_DV * h:g0 + RET_DV * (h + 1)] = (
                dog * xhat * gn * (sg * (1.0 + rg * (1.0 - sg)))).astype(BF16)
            gg_ref[:, cols] += jnp.sum(dog * xhat * gate, axis=0, keepdims=True)
            dxh = dog * gn * gate
            do = (rstd * (dxh - xhat * jnp.mean(dxh * xhat, axis=-1, keepdims=True))).astype(BF16)
            qm = (qr * hm).astype(BF16)
            qw = (qr * (hm * wq_ref[h])).astype(BF16)
            kw = (kr * wk_ref[h]).astype(BF16)
            wh = w_ref[h]
            sm = (_dot(qm, kb, NT) * wh).astype(BF16)
            ds = (_dot(do, vh, NT) * wh).astype(BF16)
            dr = dr_ref[h]
            drb = dr.astype(BF16)
            dp_ref[:, v0 + RET_DV * h:v0 + RET_DV * (h + 1)] = (_dot(sm, do, TN) + _dot(kw, drb)).astype(BF16)
            dqr = dqr + _dot(ds, kb) * hm + _dot(do, st_ref[0, h].astype(BF16), NT) * (hm * wq_ref[h])
            dkr = dkr + _dot(ds, qm, TN) + _dot(vh, drb, NT) * wk_ref[h]
            dr_ref[h] = g_blk[h] * dr + _dot(qw, do, TN)
        dp_ref[:, 0:RET_QK] = _rope_t(dqr, c, s).astype(BF16)
        dp_ref[:, RET_QK:2 * RET_QK] = _rope_t(dkr * (QK_SCALE * valid), c, s).astype(BF16)

    full = lambda shape: pl.BlockSpec(shape, lambda i: (0,) * len(shape))
    rev = lambda col: (lambda i: (nb - 1 - i, col))
    return _pcall(
        body, name=name, grid=(nb,),
        in_specs=[pl.BlockSpec((bk, RET_QK), rev(0)), pl.BlockSpec((bk, RET_QK), rev(1)),
                  pl.BlockSpec((bk, RET_V), rev(1)), pl.BlockSpec((bk, RET_V), rev(2)),
                  pl.BlockSpec((bk, RET_QK), rev(0)), pl.BlockSpec((bk, RET_QK), rev(0)),
                  full((RET_HEADS, bk, bk)), full((RET_HEADS, bk, 1)), full((RET_HEADS, bk, 1)),
                  full((RET_HEADS, 1, RET_QK)), full((1, RET_V)),
                  pl.BlockSpec((bk, RET_V), rev(0)), pl.BlockSpec((bk, RET_V), rev(0)),
                  pl.BlockSpec((1, RET_HEADS, RET_QK, RET_DV), lambda i: (nb - 1 - i, 0, 0, 0))],
        out_specs=[pl.BlockSpec((bk, g0 + RET_V), rev(0)), pl.BlockSpec((1, RET_V), lambda i: (0, 0))],
        out_shape=[jax.ShapeDtypeStruct((t, IN_PAD), BF16), jax.ShapeDtypeStruct((1, RET_V), F32)],
        scratch_shapes=[pltpu.VMEM((RET_HEADS, RET_QK, RET_DV), F32)],
        compiler_params=_params("arbitrary"),
    )(proj, proj, proj, proj, cos, sin, w, wq, wk, mask, gain, dmixed, opre, states)


def _forget_cumsum(proj, bias, name):
    t = proj.shape[0]
    rt = TOK_TILE
    nb = t // rt
    tril = jnp.asarray(np.tril(np.ones((rt, rt))), F32)

    def body(z_ref, b_ref, tril_ref, c_ref, carry_ref):
        i = pl.program_id(0)

        @pl.when(i == 0)
        def _():
            carry_ref[...] = jnp.zeros_like(carry_ref)

        z = z_ref[...] + b_ref[...]
        logf = jnp.minimum(z, 0.0) - jnp.log(1.0 + jnp.exp(-jnp.abs(z)))
        c = lax.dot_general(tril_ref[...], logf, NN, precision=lax.Precision.HIGHEST,
                            preferred_element_type=F32) + carry_ref[...]
        c_ref[...] = c
        carry_ref[...] = c[rt - 1:rt, :]

    return _pcall(
        body, name=name, grid=(nb,),
        in_specs=[pl.BlockSpec((rt, LANE), lambda i: (i, FF_COL_BLOCK)), pl.BlockSpec((1, LANE), lambda i: (0, 0)),
                  pl.BlockSpec((rt, rt), lambda i: (0, 0))],
        out_specs=pl.BlockSpec((rt, LANE), lambda i: (i, 0)),
        out_shape=jax.ShapeDtypeStruct((t, LANE), F32),
        scratch_shapes=[pltpu.VMEM((1, LANE), F32)],
        compiler_params=_params("arbitrary"),
    )(proj, bias, tril)


def _forget_cumsum_bwd(proj, bias, drs, dcs, dproj, name):
    t = proj.shape[0]
    rt = TOK_TILE
    nb = t // rt
    triu = jnp.asarray(np.triu(np.ones((rt, rt))), F32)

    def body(z_ref, b_ref, triu_ref, drs_ref, dcs_ref, dproj_in, dz_ref, gb_ref, carry_ref):
        step = pl.program_id(0)

        @pl.when(step == 0)
        def _():
            carry_ref[...] = jnp.zeros_like(carry_ref)
            gb_ref[...] = jnp.zeros_like(gb_ref)

        dlogf = lax.dot_general(triu_ref[...], drs_ref[...] - dcs_ref[...], NN, precision=lax.Precision.HIGHEST,
                                preferred_element_type=F32) + carry_ref[...]
        carry_ref[...] = dlogf[0:1, :]
        z = z_ref[...] + b_ref[...]
        is_head = lax.broadcasted_iota(jnp.int32, (rt, LANE), 1) < FOX_HEADS
        dz = jnp.where(is_head, dlogf / (1.0 + jnp.exp(z)), 0.0)
        dz_ref[...] = dz.astype(BF16)
        gb_ref[...] += jnp.sum(dz, axis=0, keepdims=True)

    return _pcall(
        body, name=name, grid=(nb,),
        in_specs=[pl.BlockSpec((rt, LANE), lambda i: (nb - 1 - i, FF_COL_BLOCK)),
                  pl.BlockSpec((1, LANE), lambda i: (0, 0)),
                  pl.BlockSpec((rt, rt), lambda i: (0, 0)),
                  pl.BlockSpec((rt, LANE), lambda i: (nb - 1 - i, 0)),
                  pl.BlockSpec((rt, LANE), lambda i: (nb - 1 - i, 0)),
                  pl.BlockSpec(memory_space=pl.ANY)],
        out_specs=[pl.BlockSpec((rt, LANE), lambda i: (nb - 1 - i, FF_COL_BLOCK)),
                   pl.BlockSpec((1, LANE), lambda i: (0, 0))],
        out_shape=[jax.ShapeDtypeStruct(dproj.shape, BF16), jax.ShapeDtypeStruct((1, LANE), F32)],
        input_output_aliases={5: 0},
        scratch_shapes=[pltpu.VMEM((1, LANE), F32)],
        compiler_params=_params("arbitrary"),
    )(proj, bias, triu, drs, dcs, dproj)


FOX_PAIRS = FOX_HEADS // 2
L_ONE_Q = FOX_DH
L_ONE_K = FOX_DH + 3
L_LSE = FOX_DH + 4


def _split3(x):
    hi = x.astype(BF16).astype(F32)
    r = x - hi
    mid = r.astype(BF16).astype(F32)
    return hi, mid, r - mid


def _head_to_low(slab, e):
    return slab if e == 0 else pltpu.roll(slab, FOX_DH, axis=1)


def _pair(a, b, low):
    return jnp.where(low, a, pltpu.roll(b, FOX_DH, axis=1))


def _fox_prep(proj, c, name):
    t = proj.shape[0]
    tq = TOK_TILE

    def body(p_ref, c_ref, qa_ref, ka_ref, va_ref, qt_ref, vt_ref):
        i = pl.program_id(0)
        lane = lax.broadcasted_iota(jnp.int32, (tq, LANE), 1)
        low = lane < FOX_DH
        live = (i * tq + lax.broadcasted_iota(jnp.int32, (tq, 1), 0)) >= N_PAD
        q_tail = jnp.where(lane < L_ONE_Q + 3, 1.0, 0.0)
        k_ones = (lane >= L_ONE_K) & (lane < L_ONE_K + 4)
        v_tail = jnp.where(lane < FOX_DH + 2, 1.0, 0.0)
        for pair in range(FOX_PAIRS):
            base = 3 * LANE * pair
            for e in range(2):
                h = 2 * pair + e
                q = _head_to_low(p_ref[:, base:base + LANE], e)
                k = _head_to_low(p_ref[:, base + LANE:base + 2 * LANE], e)
                v = _head_to_low(p_ref[:, base + 2 * LANE:base + 3 * LANE], e)
                hi, mid, lo = _split3(jnp.where(live, -c_ref[:, h:h + 1], NEG))
                ka = jnp.where(low, k, jnp.where(k_ones, 1.0, 0.0))
                ka = jnp.where(lane == L_ONE_Q, hi, jnp.where(lane == L_ONE_Q + 1, mid, jnp.where(lane == L_ONE_Q + 2, lo, ka)))
                qa = jnp.where(low, q * QK_SCALE, q_tail)
                va = jnp.where(low, v, v_tail)
                qa_ref[h] = qa.astype(BF16)
                ka_ref[h] = ka.astype(BF16)
                va_ref[h] = va.astype(BF16)
                qt_ref[h] = qa.T.astype(BF16)
                vt_ref[h] = va.T.astype(BF16)

    out = jax.ShapeDtypeStruct((FOX_HEADS, t, LANE), BF16)
    out_t = jax.ShapeDtypeStruct((FOX_HEADS, t // tq, LANE, tq), BF16)
    ospec = pl.BlockSpec((FOX_HEADS, tq, LANE), lambda i: (0, i, 0))
    tspec = pl.BlockSpec((FOX_HEADS, None, LANE, tq), lambda i: (0, i, 0, 0))
    return _pcall(
        body, name=name, grid=(t // tq,),
        in_specs=[pl.BlockSpec((tq, 3 * FOX_W), lambda i: (i, 1)), pl.BlockSpec((tq, LANE), lambda i: (i, 0))],
        out_specs=[ospec, ospec, ospec, tspec, tspec], out_shape=[out, out, out, out_t, out_t],
        compiler_params=_params("parallel"),
    )(proj, c)


STEP_PAIRS = 2
STEP_HEADS = 2 * STEP_PAIRS
FOX_GROUPS = FOX_PAIRS // STEP_PAIRS
FWD_PAIRS = 4
FWD_HEADS = 2 * FWD_PAIRS
FWD_GROUPS = FOX_PAIRS // FWD_PAIRS


def _blockdiag(a, b):
    z = jnp.zeros_like(a)
    return jnp.concatenate([jnp.concatenate([a, z], axis=1), jnp.concatenate([z, b], axis=1)], axis=0)


def _fox_fwd(qt, ka, vt, mixed, name):
    nh, nq, tq, _ = ka.shape
    t = nq * tq

    def body(qt_ref, ka_ref, vt_ref, mixed_in, mixed_ref, o_ref, lse_ref):
        i = pl.program_id(1)
        lane = lax.broadcasted_iota(jnp.int32, (tq, LANE), 1)
        key_le_query = lax.broadcasted_iota(jnp.int32, (tq, tq), 0) <= lax.broadcasted_iota(jnp.int32, (tq, tq), 1)
        qts = [qt_ref[h] for h in range(FWD_HEADS)]

        def logits(j):
            return [_dot(ka_ref[h, j], qts[h]) for h in range(FWD_HEADS)]

        def update(j, scores, carry, diagonal):
            new = []
            for h in range(FWD_HEADS):
                m, acc = carry[h]
                s = jnp.where(key_le_query, scores[h], NEG) if diagonal else scores[h]
                m_new = jnp.maximum(m, jnp.max(s, axis=0, keepdims=True))
                p = jnp.exp(s - m_new).astype(BF16)
                new.append((m_new, jnp.exp(m - m_new) * acc + _dot(vt_ref[h, j], p)))
            return tuple(new)

        init = tuple((jnp.full((1, tq), NEG, F32), jnp.zeros((LANE, tq), F32)) for _ in range(FWD_HEADS))
        carry = lax.fori_loop(0, i, lambda j, cr: update(j, logits(j), cr, False), init)
        outs, lse_rows = [], []
        for m, acc in update(i, logits(i), carry, True):
            l = acc[FOX_DH:FOX_DH + 1, :]
            outs.append((acc / l).T)
            lse_rows.append(m + jnp.log(l))
        lse_rows.append(jnp.zeros((LANE - FWD_HEADS, tq), F32))
        o_all = jnp.concatenate([_pair(outs[2 * c], outs[2 * c + 1], lane < FOX_DH) for c in range(FWD_PAIRS)], axis=1)
        mixed_ref[...] = o_all.astype(BF16)
        o_ref[...] = o_all
        lse_ref[...] = jnp.concatenate(lse_rows, axis=0).T

    width = FWD_PAIRS * LANE
    whole = pl.BlockSpec((FWD_HEADS, nq, tq, LANE), lambda g, i: (g, 0, 0, 0), pipeline_mode=pl.Buffered(1))
    whole_t = pl.BlockSpec((FWD_HEADS, nq, LANE, tq), lambda g, i: (g, 0, 0, 0), pipeline_mode=pl.Buffered(1))
    return _pcall(
        body, name=name, grid=(FWD_GROUPS, nq),
        in_specs=[pl.BlockSpec((FWD_HEADS, None, LANE, tq), lambda g, i: (g, i, 0, 0)), whole, whole_t,
                  pl.BlockSpec(memory_space=pl.ANY)],
        out_specs=[pl.BlockSpec((tq, width), lambda g, i: (i, RET_V // width + g)),
                   pl.BlockSpec((tq, width), lambda g, i: (i, g)),
                   pl.BlockSpec((None, tq, LANE), lambda g, i: (g, i, 0))],
        out_shape=[jax.ShapeDtypeStruct(mixed.shape, BF16), jax.ShapeDtypeStruct((t, FOX_W), F32),
                   jax.ShapeDtypeStruct((FWD_GROUPS, t, LANE), F32)],
        input_output_aliases={3: 0},
        compiler_params=_params("parallel", "parallel"),
    )(qt, ka, vt, mixed)


def _fox_prep_bwd(dmixed, o_fox, lse, qa, name):
    t = dmixed.shape[0]
    tq = TOK_TILE

    def body(dm_ref, o_ref, lse_ref, qa_ref, qab_ref, doa_ref):
        i = pl.program_id(0)
        lane = lax.broadcasted_iota(jnp.int32, (tq, LANE), 1)
        low = lane < FOX_DH
        live = (i * tq + lax.broadcasted_iota(jnp.int32, (tq, 1), 0)) >= N_PAD
        for pair in range(FOX_PAIRS):
            cols = slice(LANE * pair, LANE * (pair + 1))
            d_slab = dm_ref[:, cols]
            prod = d_slab * o_ref[:, cols]
            for e in range(2):
                h = 2 * pair + e
                nd = -jnp.sum(jnp.where(low, _head_to_low(prod, e), 0.0), axis=-1, keepdims=True)
                nd_hi = nd.astype(BF16).astype(F32)
                doa = jnp.where(low, _head_to_low(d_slab, e), 0.0)
                doa = jnp.where(lane == FOX_DH, nd_hi, jnp.where(lane == FOX_DH + 1, nd - nd_hi, doa))
                doa_ref[h] = doa.astype(BF16)
                lse_h = lse_ref[h // FWD_HEADS][:, h % FWD_HEADS:h % FWD_HEADS + 1]
                hi, mid, lo = _split3(jnp.where(live, -lse_h, 0.0))
                qab = qa_ref[h].astype(F32)
                qab = jnp.where(lane == L_LSE, hi, jnp.where(lane == L_LSE + 1, mid, jnp.where(lane == L_LSE + 2, lo, qab)))
                qab_ref[h] = qab.astype(BF16)

    out = jax.ShapeDtypeStruct((FOX_HEADS, t, LANE), BF16)
    hspec = pl.BlockSpec((FOX_HEADS, tq, LANE), lambda i: (0, i, 0))
    return _pcall(
        body, name=name, grid=(t // tq,),
        in_specs=[pl.BlockSpec((tq, FOX_W), lambda i: (i, 1)), pl.BlockSpec((tq, FOX_W), lambda i: (i, 0)),
                  pl.BlockSpec((FWD_GROUPS, tq, LANE), lambda i: (0, i, 0)), hspec],
        out_specs=[hspec, hspec], out_shape=[out, out],
        compiler_params=_params("parallel"),
    )(dmixed, o_fox, lse, qa)


def _fox_bwd(qab, doa, ka, va, dproj, name):
    nh, nq, tq, _ = qab.shape
    t = nq * tq
    slab = 3 * LANE * STEP_PAIRS
    group0 = (2 * RET_QK + 2 * RET_V) // slab

    def body(qab_ref, doa_ref, ka_ref, va_ref, dproj_in, dp_ref, drs_ref, dcs_ref, dq_ref):
        g, j = pl.program_id(0), pl.program_id(1)

        @pl.when((g == 0) & (j == 0))
        def _():
            drs_ref[...] = jnp.zeros_like(drs_ref)
            dcs_ref[...] = jnp.zeros_like(dcs_ref)

        @pl.when(j == 0)
        def _():
            dq_ref[...] = jnp.zeros_like(dq_ref)

        lane = lax.broadcasted_iota(jnp.int32, (tq, LANE), 1)
        low = lane < FOX_DH
        key_le_query = lax.broadcasted_iota(jnp.int32, (tq, tq), 0) <= lax.broadcasted_iota(jnp.int32, (tq, tq), 1)

        def by_head(c, a, b, col):
            h = STEP_HEADS * g + 2 * c
            return jnp.where(lane == h, a[:, col:col + 1], jnp.where(lane == h + 1, b[:, col:col + 1], 0.0))

        kbs = [ka_ref[h] for h in range(STEP_HEADS)]
        vbs = [va_ref[h] for h in range(STEP_HEADS)]

        def step(i, carry, diagonal):
            qbs = [qab_ref[h, i] for h in range(STEP_HEADS)]
            dobs = [doa_ref[h, i] for h in range(STEP_HEADS)]
            st = [_dot(kbs[h], qbs[h], NT) for h in range(STEP_HEADS)]
            dpt = [_dot(vbs[h], dobs[h], NT) for h in range(STEP_HEADS)]
            new = []
            for h in range(STEP_HEADS):
                p = jnp.exp(st[h])
                if diagonal:
                    p = jnp.where(key_le_query, p, 0.0)
                ds = (p * dpt[h]).astype(BF16)
                dq_ref[h, i] += _dot(ds, kbs[h], TN)
                dk, dv = carry[h]
                new.append((dk + _dot(ds, qbs[h]), dv + _dot(p.astype(BF16), dobs[h])))
            return tuple(new)

        zero = jnp.zeros((tq, LANE), F32)
        carry = step(j, tuple((zero, zero) for _ in range(STEP_HEADS)), True)
        carry = lax.fori_loop(j + 1, nq, lambda i, cr: step(i, cr, False), carry)
        rows = pl.ds(pl.multiple_of(j * tq, tq), tq)
        for c in range(STEP_PAIRS):
            (dka, dva), (dkb, dvb) = carry[2 * c], carry[2 * c + 1]
            c0 = 3 * LANE * c
            dp_ref[rows, c0 + LANE:c0 + 2 * LANE] = _pair(dka, dkb, low).astype(BF16)
            dp_ref[rows, c0 + 2 * LANE:c0 + 3 * LANE] = _pair(dva, dvb, low).astype(BF16)
            dcs_ref[rows, :] += by_head(c, dka, dkb, L_ONE_Q)

        @pl.when(j == nq - 1)
        def _():
            for c in range(STEP_PAIRS):
                for blk in range(nq):
                    r = slice(blk * tq, (blk + 1) * tq)
                    a, b = dq_ref[2 * c, blk], dq_ref[2 * c + 1, blk]
                    dp_ref[r, 3 * LANE * c:3 * LANE * c + LANE] = (_pair(a, b, low) * QK_SCALE).astype(BF16)
                    drs_ref[r, :] += by_head(c, a, b, L_ONE_K)

    whole = pl.BlockSpec((STEP_HEADS, nq, tq, LANE), lambda g, j: (g, 0, 0, 0), pipeline_mode=pl.Buffered(1))
    blk = pl.BlockSpec((STEP_HEADS, None, tq, LANE), lambda g, j: (g, j, 0, 0))
    sums = pl.BlockSpec((t, LANE), lambda g, j: (0, 0), pipeline_mode=pl.Buffered(1))
    return _pcall(
        body, name=name, grid=(FOX_GROUPS, nq),
        in_specs=[whole, whole, blk, blk, pl.BlockSpec(memory_space=pl.ANY)],
        out_specs=[pl.BlockSpec((t, slab), lambda g, j: (0, group0 + g)), sums, sums],
        out_shape=[jax.ShapeDtypeStruct(dproj.shape, BF16), jax.ShapeDtypeStruct((t, LANE), F32),
                   jax.ShapeDtypeStruct((t, LANE), F32)],
        input_output_aliases={4: 0},
        scratch_shapes=[pltpu.VMEM((STEP_HEADS, nq, tq, LANE), F32)],
        compiler_params=_params("arbitrary", "arbitrary"),
    )(qab, doa, ka, va, dproj)


HALO = 8


def _rows_ext(ref, r0, rows, t, before, after):
    lo, hi = r0 - before, r0 + rows + after
    width = ref.shape[-1]
    parts = []
    if lo < 0:
        parts.append(jnp.zeros((-lo, width), F32))
    parts.append(ref[max(lo, 0):min(hi, t), :].astype(F32))
    if hi > t:
        parts.append(jnp.zeros((hi - t, width), F32))
    return parts[0] if len(parts) == 1 else jnp.concatenate(parts, axis=0)


def _conv_taps(a_ext, r0_ext, cw_ref, cb_ref):
    n = a_ext.shape[0]
    if r0_ext < N_PAD:
        row = r0_ext + lax.broadcasted_iota(jnp.int32, (n, 1), 0)
        a_ext = jnp.where(row >= N_PAD, a_ext, 0.0)
    a1 = pltpu.roll(a_ext, 1, axis=0)
    a2 = pltpu.roll(a_ext, 2, axis=0)
    acc = cb_ref[...] + a2 * cw_ref[0:1, :] + a1 * cw_ref[1:2, :] + a_ext * cw_ref[2:3, :]
    return a_ext, a1, a2, acc


FF_COLS = 256


def _up_conv_fwd(n2, w_up_t, conv_w8, conv_b, name):
    t, d = n2.shape
    f = w_up_t.shape[1]
    rows = TOK_TILE
    starts = list(range(0, t, rows))

    def body(n_ref, wa_ref, wb_ref, cw_ref, cb_ref, up_ref, g_ref):
        wa, wb = wa_ref[...], wb_ref[...]

        def project(r0):
            n_rows = n_ref[r0:r0 + rows, :]
            up_ref[0, r0:r0 + rows, :] = _dot(n_rows, wa, NT)
            up_ref[1, r0:r0 + rows, :] = _dot(n_rows, wb, NT)

        def activate(r0):
            a_ext = _rows_ext(up_ref.at[0], r0, rows, t, HALO, 0)
            _, _, _, acc = _conv_taps(a_ext, r0 - HALO, cw_ref, cb_ref)
            acc = acc[HALO:, :]
            g_ref[r0:r0 + rows, :] = (acc * _sigmoid(acc) * up_ref[1, r0:r0 + rows, :]).astype(BF16)

        project(starts[0])
        for r0, r_next in zip(starts, starts[1:] + [None]):
            if r_next is not None:
                project(r_next)
            activate(r0)

    return _pcall(
        body, name=name, grid=(f // FF_COLS,),
        in_specs=[pl.BlockSpec((t, d), lambda j: (0, 0), pipeline_mode=pl.Buffered(1)),
                  pl.BlockSpec((None, FF_COLS, d), lambda j: (0, j, 0)), pl.BlockSpec((None, FF_COLS, d), lambda j: (1, j, 0)),
                  pl.BlockSpec((8, FF_COLS), lambda j: (0, j)), pl.BlockSpec((1, FF_COLS), lambda j: (0, j))],
        out_specs=[pl.BlockSpec((2, t, FF_COLS), lambda j: (0, 0, j)), pl.BlockSpec((t, FF_COLS), lambda j: (0, j))],
        out_shape=[jax.ShapeDtypeStruct((2, t, f), F32), jax.ShapeDtypeStruct((t, f), BF16)],
        compiler_params=_params("parallel"),
    )(n2, w_up_t, w_up_t, conv_w8, conv_b)


def _dg_conv_bwd(up, conv_w8, conv_b, dh2, w_down, name):
    _, t, f = up.shape
    d = dh2.shape[1]
    rows = TOK_TILE
    starts = list(range(0, t, rows))

    def body(a_ref, b_ref, cw_ref, cb_ref, dh_ref, wd_ref, dup_ref, gcw_ref, gcb_ref, dg_ref):
        wd = wd_ref[...]

        def project(r0):
            dg_ref[r0:r0 + rows, :] = _dot(dh_ref[r0:r0 + rows, :], wd, NT)

        gw = [jnp.zeros((1, FF_COLS), F32) for _ in range(3)]
        gb = jnp.zeros((1, FF_COLS), F32)
        project(starts[0])
        for r0, r_next in zip(starts, starts[1:] + [None]):
            if r_next is not None:
                project(r_next)
            a_ext = _rows_ext(a_ref, r0, rows, t, HALO, HALO)
            b_ext = _rows_ext(b_ref, r0, rows, t, HALO, HALO)
            dg_ext = _rows_ext(dg_ref, r0, rows, t, HALO, HALO)
            a0, a1, a2, acc = _conv_taps(a_ext, r0 - HALO, cw_ref, cb_ref)
            sg = _sigmoid(acc)
            dacc = dg_ext * b_ext * (sg * (1.0 + acc * (1.0 - sg)))
            n = dacc.shape[0]
            da = (dacc * cw_ref[2:3, :] + pltpu.roll(dacc, n - 1, axis=0) * cw_ref[1:2, :]
                  + pltpu.roll(dacc, n - 2, axis=0) * cw_ref[0:1, :])
            core = slice(HALO, HALO + rows)
            da = da[core, :]
            if r0 < N_PAD:
                row = r0 + lax.broadcasted_iota(jnp.int32, (rows, 1), 0)
                da = jnp.where(row >= N_PAD, da, 0.0)
            dup_ref[0, r0:r0 + rows, :] = da.astype(BF16)
            dup_ref[1, r0:r0 + rows, :] = (dg_ext * acc * sg)[core, :].astype(BF16)
            dacc_c = dacc[core, :]
            gw[0] = gw[0] + jnp.sum(dacc_c * a2[core, :], axis=0, keepdims=True)
            gw[1] = gw[1] + jnp.sum(dacc_c * a1[core, :], axis=0, keepdims=True)
            gw[2] = gw[2] + jnp.sum(dacc_c * a0[core, :], axis=0, keepdims=True)
            gb = gb + jnp.sum(dacc_c, axis=0, keepdims=True)
        gcw_ref[...] = jnp.zeros((8, FF_COLS), F32)
        for tap in range(3):
            gcw_ref[tap:tap + 1, :] = gw[tap]
        gcb_ref[...] = gb

    return _pcall(
        body, name=name, grid=(f // FF_COLS,),
        in_specs=[pl.BlockSpec((None, t, FF_COLS), lambda j: (0, 0, j)), pl.BlockSpec((None, t, FF_COLS), lambda j: (1, 0, j)),
                  pl.BlockSpec((8, FF_COLS), lambda j: (0, j)), pl.BlockSpec((1, FF_COLS), lambda j: (0, j)),
                  pl.BlockSpec((t, d), lambda j: (0, 0), pipeline_mode=pl.Buffered(1)),
                  pl.BlockSpec((FF_COLS, d), lambda j: (j, 0))],
        out_specs=[pl.BlockSpec((2, t, FF_COLS), lambda j: (0, 0, j)), pl.BlockSpec((8, FF_COLS), lambda j: (0, j)),
                   pl.BlockSpec((1, FF_COLS), lambda j: (0, j))],
        out_shape=[jax.ShapeDtypeStruct((2, t, f), BF16), jax.ShapeDtypeStruct((8, f), F32),
                   jax.ShapeDtypeStruct((1, f), F32)],
        scratch_shapes=[pltpu.VMEM((t, FF_COLS), F32)],
        compiler_params=_params("parallel"),
    )(up, up, conv_w8, conv_b, dh2, w_down)


def _exchange(arrays, kinds, name, after=None):
    n = len(arrays)
    npeer = N_DEV - 1
    n_in = n + int(after is not None)

    def body(*refs):
        ins, outs = refs[:n], refs[n_in:n_in + n]
        send_sems, recv_sems, local_sems = refs[n_in + n:]
        x, y, c = lax.axis_index("x"), lax.axis_index("y"), lax.axis_index("c")
        me = 4 * x + 2 * y + c
        copies, locals_ = [], []
        for a in range(n):
            gather = kinds[a] == "gather"
            own = pltpu.make_async_copy(ins[a] if gather else ins[a].at[me], outs[a].at[me], local_sems.at[a])
            own.start()
            locals_.append(own)
            for d in range(1, N_DEV):
                px = 1 - x if d & 4 else x
                py = 1 - y if d & 2 else y
                pc = 1 - c if d & 1 else c
                src = ins[a] if gather else ins[a].at[4 * px + 2 * py + pc]
                cp = pltpu.make_async_remote_copy(
                    src_ref=src, dst_ref=outs[a].at[me],
                    send_sem=send_sems.at[a * npeer + d - 1], recv_sem=recv_sems.at[a * npeer + d - 1],
                    device_id=(px, py, pc), device_id_type=pl.DeviceIdType.MESH)
                cp.start()
                copies.append(cp)
        for cp in copies:
            cp.wait_recv()
        for cp in copies:
            cp.wait_send()
        for own in locals_:
            own.wait()

    out_shape = [jax.ShapeDtypeStruct((N_DEV,) + (a.shape if k == "gather" else a.shape[1:]), a.dtype)
                 for a, k in zip(arrays, kinds)]
    return _pcall(
        body, name=name,
        in_specs=[pl.BlockSpec(memory_space=pl.ANY)] * n_in,
        out_specs=[pl.BlockSpec(memory_space=pl.ANY)] * n,
        out_shape=out_shape,
        scratch_shapes=[pltpu.SemaphoreType.DMA((n * npeer,)), pltpu.SemaphoreType.DMA((n * npeer,)),
                        pltpu.SemaphoreType.DMA((n,))],
        compiler_params=pltpu.CompilerParams(has_side_effects=True),
    )(*arrays, *([] if after is None else [after]))


def _peer_copies(srcs, lands, kinds, send_sems, recv_sems):
    x, y, c = lax.axis_index("x"), lax.axis_index("y"), lax.axis_index("c")
    me = 4 * x + 2 * y + c
    copies = []
    for a in range(len(srcs)):
        for d in range(1, N_DEV):
            px = 1 - x if d & 4 else x
            py = 1 - y if d & 2 else y
            pc = 1 - c if d & 1 else c
            k = a * (N_DEV - 1) + d - 1
            copies.append(pltpu.make_async_remote_copy(
                src_ref=srcs[a] if kinds[a] == "gather" else srcs[a].at[4 * px + 2 * py + pc], dst_ref=lands[a].at[me],
                send_sem=send_sems.at[k], recv_sem=recv_sems.at[k],
                device_id=(px, py, pc), device_id_type=pl.DeviceIdType.MESH))
    return copies


def _exchange_start(arrays, kinds, name, after=None):
    n = len(arrays)
    nsem = n * (N_DEV - 1)
    hbm = pl.BlockSpec(memory_space=pltpu.HBM)
    sem = pl.BlockSpec(memory_space=pltpu.SEMAPHORE)
    land_shapes = [(N_DEV,) + (a.shape if k == "gather" else a.shape[1:]) for a, k in zip(arrays, kinds)]

    n_in = 2 * n + int(after is not None)

    def body(*refs):
        srcs, lands = refs[:n], refs[n:2 * n]
        send_sems, recv_sems = refs[n_in], refs[n_in + 1]
        token = refs[-1]
        for cp in _peer_copies(srcs, lands, kinds, send_sems, recv_sems):
            cp.start()
        token[...] = jnp.zeros_like(token)

    operands = [pltpu.with_memory_space_constraint(a, pltpu.HBM) for a in arrays]
    operands += [pltpu.with_memory_space_constraint(lax.empty(s, a.dtype), pltpu.HBM) for s, a in zip(land_shapes, arrays)]
    operands += [] if after is None else [after]
    out = _pcall(
        body, name=name,
        in_specs=[hbm] * (2 * n) + ([] if after is None else [pl.BlockSpec(memory_space=pl.ANY)]),
        out_specs=[sem, sem] + [hbm] * (2 * n) + [pl.BlockSpec(memory_space=pltpu.VMEM)],
        out_shape=[pltpu.SemaphoreType.DMA((nsem,)), pltpu.SemaphoreType.DMA((nsem,))]
        + [pltpu.HBM(a.shape, a.dtype) for a in arrays]
        + [pltpu.HBM(s, a.dtype) for s, a in zip(land_shapes, arrays)]
        + [jax.ShapeDtypeStruct((8, LANE), F32)],
        input_output_aliases={k: 2 + k for k in range(2 * n)},
        compiler_params=pltpu.CompilerParams(has_side_effects=pltpu.SideEffectType.DATAFLOW_SIDE_EFFECTING),
    )(*operands)
    return out[0], out[1], list(out[2:2 + n]), list(out[2 + n:2 + 2 * n]), out[-1]


def _exchange_wait(started, kinds, after, name, fill_own=True):
    send_sems, recv_sems, srcs, lands, _ = started
    n = len(srcs)
    hbm = pl.BlockSpec(memory_space=pltpu.HBM)
    sem = pl.BlockSpec(memory_space=pltpu.SEMAPHORE)

    def body(*refs):
        src_refs, land_refs = refs[:n], refs[n:2 * n]
        copies = _peer_copies(src_refs, land_refs, kinds, refs[2 * n], refs[2 * n + 1])
        for cp in copies:
            cp.wait_send()
        for cp in copies:
            cp.wait_recv()

    out = _pcall(
        body, name=name,
        in_specs=[hbm] * (2 * n) + [sem, sem, pl.BlockSpec(memory_space=pl.ANY)],
        out_specs=[hbm] * (2 * n),
        out_shape=[pltpu.HBM(a.shape, a.dtype) for a in srcs + lands],
        input_output_aliases={k: k for k in range(2 * n)},
        compiler_params=pltpu.CompilerParams(has_side_effects=pltpu.SideEffectType.DATAFLOW_SIDE_EFFECTING),
    )(*srcs, *lands, send_sems, recv_sems, after)
    if not fill_own:
        return list(out[:n]), list(out[n:])
    me = 4 * lax.axis_index("x") + 2 * lax.axis_index("y") + lax.axis_index("c")
    filled = []
    for src, land, kind in zip(out[:n], out[n:], kinds):
        own = src if kind == "gather" else lax.dynamic_index_in_dim(src, me, axis=0, keepdims=False)
        filled.append(lax.dynamic_update_slice(land, own[None], (me,) + (0,) * own.ndim))
    return filled


def _sum_slots(slots, name, rows_tile):
    nd, r, c = slots.shape

    def body(s_ref, o_ref):
        acc = s_ref[0].astype(F32)
        for p in range(1, nd):
            acc = acc + s_ref[p].astype(F32)
        o_ref[...] = acc

    return _pcall(
        body, name=name, grid=(r // rows_tile,),
        in_specs=[pl.BlockSpec((nd, rows_tile, c), lambda i: (0, i, 0))],
        out_specs=pl.BlockSpec((rows_tile, c), lambda i: (i, 0)),
        out_shape=jax.ShapeDtypeStruct((r, c), F32),
        compiler_params=_params("parallel"),
    )(slots)


def _sum_slots_small(slot_arrays, own_arrays, name):
    n = len(slot_arrays)

    def body(*refs):
        me = 4 * lax.axis_index("x") + 2 * lax.axis_index("y") + lax.axis_index("c")
        for s_ref, own_ref, o_ref in zip(refs[:n], refs[n:2 * n], refs[2 * n:]):
            acc = jnp.where(me == 0, own_ref[...], s_ref[0])
            for p in range(1, s_ref.shape[0]):
                acc = acc + jnp.where(me == p, own_ref[...], s_ref[p])
            o_ref[...] = acc

    return _pcall(body, name=name, out_shape=[jax.ShapeDtypeStruct(a.shape[1:], F32) for a in slot_arrays])(
        *slot_arrays, *own_arrays)


def _adamw_update(w_ref, g_ref, m_ref, v_ref, d_ref, nm_ref, nv_ref):
    gr = g_ref[...]
    nm = ADAM_B1 * m_ref[...] + (1.0 - ADAM_B1) * gr
    nv = ADAM_B2 * v_ref[...] + (1.0 - ADAM_B2) * (gr * gr)
    m_hat = nm / (1.0 - ADAM_B1 ** ADAM_STEP)
    v_hat = nv / (1.0 - ADAM_B2 ** ADAM_STEP)
    d_ref[...] = -ADAM_LR * (m_hat / (jnp.sqrt(v_hat) + ADAM_EPS) + ADAM_WD * w_ref[...])
    nm_ref[...] = nm
    nv_ref[...] = nv


def _adamw_small(ws, gs, ms, vs, name):
    n = len(ws)

    def body(*refs):
        ins, outs = refs[:4 * n], refs[4 * n:]
        for k in range(n):
            _adamw_update(ins[k], ins[n + k], ins[2 * n + k], ins[3 * n + k], outs[k], outs[n + k], outs[2 * n + k])

    shapes = [jax.ShapeDtypeStruct(w.shape, F32) for w in ws]
    out = _pcall(body, name=name, out_shape=shapes * 3)(*ws, *gs, *ms, *vs)
    return list(out[:n]), list(out[n:2 * n]), list(out[2 * n:])


def _adamw(w, g, m, v, name, rows_tile):
    r, c = w.shape
    body = lambda *refs: _adamw_update(*refs)
    spec = pl.BlockSpec((rows_tile, c), lambda i: (i, 0))
    shp = jax.ShapeDtypeStruct((r, c), F32)
    return _pcall(
        body, name=name, grid=(r // rows_tile,), in_specs=[spec] * 4, out_specs=[spec] * 3, out_shape=[shp] * 3,
        compiler_params=_params("parallel"),
    )(w, g, m, v)


F0 = 2 * RET_QK + 2 * RET_V


def _to_internal_rows(w_t):
    cols = w_t.shape[1]
    fox = w_t[F0:F0 + 3 * FOX_W].reshape(3, FOX_PAIRS, LANE, cols).transpose(1, 0, 2, 3).reshape(3 * FOX_W, cols)
    tail = jnp.zeros((IN_PAD - IN_WIDTH, cols), w_t.dtype)
    return jnp.concatenate([w_t[:F0], fox, w_t[F0 + 3 * FOX_W:], tail], axis=0)


def _from_internal_rows(g_t):
    cols = g_t.shape[1]
    fox = g_t[F0:F0 + 3 * FOX_W].reshape(FOX_PAIRS, 3, LANE, cols).transpose(1, 0, 2, 3).reshape(3 * FOX_W, cols)
    return jnp.concatenate([g_t[:F0], fox, g_t[F0 + 3 * FOX_W:F0 + 3 * FOX_W + FOX_HEADS]], axis=0)


def _local_step(x, target, meta, attn_g, fox_b, ret_g, ffn_g, conv_w8, conv_b, final_g,
                first_weight, late_weights, ffn_grads_ready, out_grad_ready, in_grad_ready):
    seq, d = x.shape
    t = seq + PREFIX
    tm = TOK_TILE
    nq = t // tm
    fox_b128 = jnp.pad(fox_b, ((0, 0), (0, LANE - FOX_HEADS)))

    h0, n1 = _prep_norm(x, meta, attn_g, "prep_norm")
    w_in_t = first_weight(n1)
    proj = _mm_simple(n1, w_in_t, mode="nt", tm=tm, tn=IN_PAD, tk=d, out_dtype=F32, name="mm_in")
    cos, sin = _rope_tables(t)
    o_pre, mixed, states = _ret_fwd(proj, cos, sin, ret_g, "ret_fwd")
    c = _forget_cumsum(proj, fox_b128, "forget_cumsum")
    qa, ka, va, qt, vt = _fox_prep(proj, c, "fox_prep")
    by_block = lambda a: a.reshape(FOX_HEADS, nq, tm, LANE)
    mixed, o_fox, lse = _fox_fwd(qt, by_block(ka), vt, mixed, "fox_fwd")
    w_out, w_up_t, w_down = late_weights(o_fox)
    tile = pl.BlockSpec((tm, d), lambda i: (i, 0))
    row_vec = pl.BlockSpec((1, d), lambda i: (0, 0))
    resident = lambda shape: pl.BlockSpec(shape, lambda i: (0,) * len(shape), pipeline_mode=pl.Buffered(1))
    acts = lambda dtype: jax.ShapeDtypeStruct((t, d), dtype)
    vec = jax.ShapeDtypeStruct((1, d), F32)

    def residual_and_norm(i, acc, ins, outs):
        h = acc + ins[0][...]
        outs[0][...] = h
        outs[1][...] = (h * lax.rsqrt(jnp.mean(h * h, axis=-1, keepdims=True) + EPS) * ins[1][...]).astype(BF16)

    h1, n2 = _matmul_rows([mixed], [tile], [w_out], [resident((d, d))], [h0, ffn_g], [tile, row_vec],
                          [tile, tile], [acts(F32), acts(BF16)], residual_and_norm, mode="nn", steps=nq, name="mm_out_norm")
    nf = D_FF // 1408
    up, g = _up_conv_fwd(n2, w_up_t, conv_w8, conv_b, "up_conv_fwd")

    def residual_loss_bwd(i, acc, ins, outs):
        loss_ref, dh_ref, dhb_ref, gg_ref = outs
        part, dh, gg = _loss_tile(i, acc + ins[0][...], jnp.concatenate([ins[1][...], ins[2][...], ins[3][...]], axis=0),
                                  ins[4][...])
        _accumulate(loss_ref, i, jnp.broadcast_to(part, loss_ref.shape))
        dh_ref[...] = dh
        dhb_ref[...] = dh.astype(BF16)
        _accumulate(gg_ref, i, gg)

    loss_tile, dh2, dh2_b, g_final = _matmul_rows(
        [g], [pl.BlockSpec((tm, D_FF), lambda i: (i, 0))], [w_down], [resident((D_FF, d))],
        [h1, target, target, target, final_g], [tile] + _shifted_row_specs(d) + [row_vec],
        [pl.BlockSpec((8, LANE), lambda i: (0, 0)), tile, tile, row_vec],
        [jax.ShapeDtypeStruct((8, LANE), F32), acts(F32), acts(BF16), vec], residual_loss_bwd,
        mode="nn", steps=nq, name="mm_down_loss")

    tkw = 2112 if t % 2112 == 0 else tm
    gw_down = _mm_simple(g, dh2_b, mode="tn", tm=1408, tn=d, tk=tkw, out_dtype=BF16, name="mm_gw_down")
    dup, g_conv_w8, g_conv_b = _dg_conv_bwd(up, conv_w8, conv_b, dh2_b, w_down, "dg_conv_bwd")

    half = lambda p: pl.BlockSpec((None, tm, D_FF), lambda i: (p, i, 0))
    half_w = lambda p: pl.BlockSpec((None, D_FF, d), lambda i: (p, 0, 0), pipeline_mode=pl.Buffered(1))
    gw_up_t = _matmul(
        dup, n2, mode="tn", grid=(2 * nf, 1, t // tkw),
        a_spec=pl.BlockSpec((None, tkw, 1408), lambda i, j, k: (i // nf, k, i % nf)),
        b_spec=pl.BlockSpec((tkw, d), lambda i, j, k: (k, 0)),
        o_spec=pl.BlockSpec((1408, d), lambda i, j, k: (i, 0)),
        out_shape=jax.ShapeDtypeStruct((2 * D_FF, d), BF16), name="mm_gw_up")
    def norm_bwd_and_mixer_grad(i, acc, ins, outs):
        dh, gg = _rms_bwd_tile(acc, ins[0][...], ins[1][...], ins[2][...])
        outs[0][...] = dh
        _accumulate(outs[1], i, gg)
        outs[2][...] = _dot(dh.astype(BF16), ins[3][...], NT)

    dh1, g_ffn, dmixed = _matmul_rows(
        [dup, dup], [half(0), half(1)], [w_up_t, w_up_t], [half_w(0), half_w(1)],
        [h1, ffn_g, dh2, w_out], [tile, row_vec, tile, resident((d, d))], [tile, row_vec, tile],
        [acts(F32), vec, acts(F32)], norm_bwd_and_mixer_grad,
        mode="nn", steps=nq, name="mm_dn2_norm_bwd", after=ffn_grads_ready(gw_down, gw_up_t))
    gw_out = _mm_simple(mixed, dh1, mode="tn", tm=d, tn=d, tk=tkw, out_dtype=BF16, name="mm_gw_out")
    dproj, g_ret = _ret_bwd(proj, cos, sin, ret_g + out_grad_ready(gw_out), dmixed, o_pre, states, "ret_bwd")
    qab, doa = _fox_prep_bwd(dmixed, o_fox, lse, qa, "fox_prep_bwd")
    dproj, drs, dcs = _fox_bwd(by_block(qab), by_block(doa), by_block(ka), by_block(va), dproj, "fox_bwd")
    dproj, g_fox_b = _forget_cumsum_bwd(proj, fox_b128, drs, dcs, dproj, "forget_cumsum_bwd")
    gw_in_t = _mm_simple(dproj, n1, mode="tn", tm=640, tn=d, tk=tkw, out_dtype=BF16, name="mm_gw_in")
    sent = in_grad_ready(gw_in_t)
    def input_grads(i, acc, ins, outs):
        gx_ref, gmeta_ref, gg_ref, buf_ref, sems = outs
        dh, gg = _rms_bwd_tile(acc, ins[0][...], ins[1][...], ins[2][...])
        _accumulate(gg_ref, i, gg)
        slot = i % 2

        def first_copy():
            return pltpu.make_async_copy(buf_ref.at[0, pl.ds(PREFIX, tm - PREFIX)], gx_ref.at[pl.ds(0, tm - PREFIX)],
                                         sems.at[0])

        def tile_copy(tile, buf_slot):
            rows = pl.ds(pl.multiple_of(tile * tm - PREFIX, PREFIX), tm)
            return pltpu.make_async_copy(buf_ref.at[buf_slot], gx_ref.at[rows], sems.at[buf_slot])

        @pl.when(i == 1)
        def _():
            first_copy().wait()

        @pl.when(i >= 2)
        def _():
            tile_copy(i - 1, 1 - slot).wait()

        buf_ref[slot] = dh

        @pl.when(i == 0)
        def _():
            gmeta_ref[...] = dh[N_PAD:PREFIX, :]
            first_copy().start()

        @pl.when(i > 0)
        def _():
            tile_copy(i, slot).start()

        @pl.when(i == nq - 1)
        def _():
            tile_copy(i, slot).wait()

    grad_x, g_meta, g_attn = _matmul_rows(
        [dproj], [pl.BlockSpec((tm, IN_PAD), lambda i: (i, 0))], [w_in_t], [resident((IN_PAD, d))],
        [h0, attn_g, dh1], [tile, row_vec, tile],
        [pl.BlockSpec(memory_space=pl.ANY), pl.BlockSpec((N_META, d), lambda i: (0, 0)), row_vec],
        [jax.ShapeDtypeStruct((seq, d), F32), jax.ShapeDtypeStruct((N_META, d), F32), vec], input_grads,
        mode="nn", steps=nq, name="mm_dn1_norm_bwd", after=sent,
        scratch=[pltpu.VMEM((2, tm, d), F32), pltpu.SemaphoreType.DMA((2,))])

    grads = dict(meta=g_meta, attn_g=g_attn, fox_b=g_fox_b, ret_g=g_ret,
                 ffn_g=g_ffn, conv_w=g_conv_w8, conv_b=g_conv_b, final_g=g_final)
    return loss_tile, grad_x, grads


def kernel(x, meta_tokens, attn_norm_g, w_in, fox_forget_b, ret_norm_g, w_out, ffn_norm_g, w_up, conv_w, conv_b, w_down, final_norm_g, loss_target, m_meta_tokens, m_attn_norm_g, m_w_in, m_fox_forget_b, m_ret_norm_g, m_w_out, m_ffn_norm_g, m_w_up, m_conv_w, m_conv_b, m_w_down, m_final_norm_g, v_meta_tokens, v_attn_norm_g, v_w_in, v_fox_forget_b, v_ret_norm_g, v_w_out, v_ffn_norm_g, v_w_up, v_conv_w, v_conv_b, v_w_down, v_final_norm_g):
    d = D_MODEL
    me = 4 * lax.axis_index("x") + 2 * lax.axis_index("y") + lax.axis_index("c")
    in_blk = IN_WIDTH // N_DEV
    in_blk_pad = 400
    up_blk = 2 * D_FF // N_DEV
    down_blk = D_FF // N_DEV
    cw_blk = D_FF // N_DEV

    w_in_loc = jnp.pad(w_in[0].T.astype(BF16), ((0, in_blk_pad - in_blk), (0, 0)))
    cw_loc = jnp.pad(conv_w[0], ((0, 5), (0, 384 - cw_blk)))
    g_meta, g_cw = _exchange([meta_tokens, cw_loc], ["gather"] * 2, "gather_small")
    first = _exchange_start([w_in_loc], ["gather"], "gather_in_start", after=g_meta)
    rest_loc = [(w_out[0] + first[-1][0:1, 0:1]).astype(BF16), w_up[0].T.astype(BF16), w_down[0].astype(BF16)]
    rest = _exchange_start(rest_loc, ["gather"] * 3, "gather_rest_start")
    meta_f = g_meta.transpose(1, 0, 2).reshape(N_META, d)
    conv_w8 = jnp.pad(g_cw[:, :3, :cw_blk].transpose(1, 0, 2).reshape(3, D_FF), ((0, 5), (0, 0)))
    pending = {}

    def first_weight(after):
        (g_in,) = _exchange_wait(first, ["gather"], after, "gather_in_wait")
        return _to_internal_rows(g_in[:, :in_blk].reshape(IN_WIDTH, d))

    def in_grad_ready(gw_in_t):
        blocks = _from_internal_rows(gw_in_t).reshape(N_DEV, in_blk, d)
        blocks = jnp.pad(blocks, ((0, 0), (0, in_blk_pad - in_blk), (0, 0)))
        pending["in"] = _exchange_start([blocks], ["scatter"], "grads_in_start")
        return pending["in"][-1][0:1, 0:1]

    def late_weights(after):
        g_out, g_up, g_down = _exchange_wait(rest, ["gather"] * 3, after, "gather_rest_wait")
        return g_out.reshape(d, d), g_up.reshape(2, D_FF, d), g_down.reshape(D_FF, d)

    def ffn_grads_ready(gw_down, gw_up_t):
        blocks = [gw_down.reshape(N_DEV, down_blk, d), gw_up_t.reshape(N_DEV, up_blk, d)]
        pending["ffn"] = _exchange_start(blocks, ["scatter"] * 2, "grads_ffn_start")
        return pending["ffn"][-1][0:1, 0:1]

    def out_grad_ready(gw_out):
        pending["out"] = _exchange_start([gw_out.reshape(N_DEV, d // N_DEV, d)], ["scatter"], "grads_out_start")
        return pending["out"][-1][0:1, 0:1]

    loss_tile, grad_x, gr = _local_step(
        x[0], loss_target[0], meta_f, attn_norm_g + rest[-1][0:1, 0:1], fox_forget_b, ret_norm_g, ffn_norm_g,
        conv_w8, conv_b, final_norm_g.reshape(1, d), first_weight, late_weights, ffn_grads_ready, out_grad_ready,
        in_grad_ready)

    small = [loss_tile, gr["attn_g"], gr["fox_b"], gr["ret_g"], gr["ffn_g"], gr["conv_b"], gr["final_g"],
             gr["meta"], gr["conv_w"]]
    small_kinds = ["gather"] * len(small)
    small_started = _exchange_start(small, small_kinds, "grads_small_start")

    r_down, r_up = _exchange_wait(pending["ffn"], ["scatter"] * 2, small_started[-1], "grads_ffn_wait")
    (r_out,) = _exchange_wait(pending["out"], ["scatter"], small_started[-1], "grads_out_wait")
    g_w_out = _sum_slots(r_out, "sum_w_out", d // N_DEV)
    g_w_up_t = _sum_slots(r_up, "sum_w_up", up_blk)
    g_w_down = _sum_slots(r_down, "sum_w_down", down_blk)
    as_t = lambda a: a[0].T
    from_t = lambda a: a.T[None]
    d_w_out, m_w_out_n, v_w_out_n = [a[None] for a in _adamw(w_out[0], g_w_out, m_w_out[0], v_w_out[0], "adamw_w_out", 128)]
    up_t = _adamw(as_t(w_up), g_w_up_t, as_t(m_w_up), as_t(v_w_up), "adamw_w_up", up_blk // 2)
    d_w_up, m_w_up_n, v_w_up_n = [from_t(a) for a in up_t]
    d_w_down, m_w_down_n, v_w_down_n = [a[None] for a in _adamw(w_down[0], g_w_down, m_w_down[0], v_w_down[0],
                                                                "adamw_w_down", down_blk)]

    own_small, r_small = _exchange_wait(small_started, small_kinds, up_t[0], "grads_small_wait", fill_own=False)
    (loss_all, g_attn, g_fox_b128, g_ret, g_ffn, g_conv_b, g_final, g_meta_full, g_cw_full) = _sum_slots_small(
        r_small, own_small, "sum_small")
    loss = loss_all[0, 0]
    g_fox_b = g_fox_b128[:, :FOX_HEADS]
    g_meta_loc = lax.dynamic_slice(g_meta_full, (0, me * (d // N_DEV)), (N_META, d // N_DEV))
    g_cw_loc = lax.dynamic_slice(g_cw_full, (0, me * cw_blk), (3, cw_blk))

    (r_in,) = _exchange_wait(pending["in"], ["scatter"], r_small[0], "grads_in_wait")
    g_w_in_t = _sum_slots(r_in, "sum_w_in", in_blk_pad)[:in_blk]
    d_w_in, m_w_in_n, v_w_in_n = [from_t(a) for a in _adamw(as_t(w_in), g_w_in_t, as_t(m_w_in), as_t(v_w_in),
                                                            "adamw_w_in", in_blk)]
    g_w_in, g_w_up = g_w_in_t.T, g_w_up_t.T
    row = lambda a: a.reshape(1, d)
    sm_grads = [g_meta_loc, g_attn, g_fox_b, g_ret, g_ffn, g_cw_loc, g_conv_b, g_final]
    sm_w = [meta_tokens, attn_norm_g, fox_forget_b, ret_norm_g, ffn_norm_g, conv_w[0], conv_b, row(final_norm_g)]
    sm_m = [m_meta_tokens, m_attn_norm_g, m_fox_forget_b, m_ret_norm_g, m_ffn_norm_g, m_conv_w[0], m_conv_b,
            row(m_final_norm_g)]
    sm_v = [v_meta_tokens, v_attn_norm_g, v_fox_forget_b, v_ret_norm_g, v_ffn_norm_g, v_conv_w[0], v_conv_b,
            row(v_final_norm_g)]
    dl, ml, vl = [lst[:7] + [lst[7].reshape(d)] for lst in _adamw_small(sm_w, sm_grads, sm_m, sm_v, "adamw_small")]

    def by_weight(meta_, attn_, w_in_, fox_, ret_, w_out_, ffn_, w_up_, cw_, cb_, w_down_, final_):
        return (meta_, attn_, w_in_, fox_, ret_, w_out_, ffn_, w_up_, cw_[None], cb_, w_down_, final_)

    grads_out = by_weight(g_meta_loc, g_attn, g_w_in[None], g_fox_b, g_ret, g_w_out[None], g_ffn, g_w_up[None], g_cw_loc,
                          g_conv_b, g_w_down[None], g_final.reshape(d))
    delta_out = by_weight(dl[0], dl[1], d_w_in, dl[2], dl[3], d_w_out, dl[4], d_w_up, dl[5], dl[6], d_w_down, dl[7])
    m_out = by_weight(ml[0], ml[1], m_w_in_n, ml[2], ml[3], m_w_out_n, ml[4], m_w_up_n, ml[5], ml[6], m_w_down_n, ml[7])
    v_out = by_weight(vl[0], vl[1], v_w_in_n, vl[2], vl[3], v_w_out_n, vl[4], v_w_up_n, vl[5], vl[6], v_w_down_n, vl[7])
    return (loss, grad_x[None]) + grads_out + delta_out + m_out + v_out
```

```python
import numpy as np
import jax
import jax.numpy as jnp
from jax import lax
from jax.experimental import pallas as pl
from jax.experimental.pallas import tpu as pltpu

F32 = jnp.float32
BF16 = jnp.bfloat16

D_MODEL = 1024
N_META = 16
N_PAD = 112
PREFIX = 128
RET_HEADS = 4
RET_DK = 64
RET_DV = 128
FOX_HEADS = 8
FOX_DH = 64
D_FF = 2816
ROPE_BASE = 10000.0
EPS = 1e-6
NEG = -1e30
RET_QK = RET_HEADS * RET_DK
RET_V = RET_HEADS * RET_DV
FOX_W = FOX_HEADS * FOX_DH
IN_WIDTH = 2 * RET_QK + 2 * RET_V + 3 * FOX_W + FOX_HEADS
IN_PAD = 3200
FF_COL_BLOCK = (IN_WIDTH - FOX_HEADS) // 128
QK_SCALE = 0.125

ADAM_LR = 0.001
ADAM_B1 = 0.9
ADAM_B2 = 0.999
ADAM_EPS = 1e-08
ADAM_WD = 0.01
ADAM_STEP = 10

N_DEV = 8
LANE = 128
ROW_TILE = 128
TOK_TILE = 384

NN = (((1,), (0,)), ((), ()))
NT = (((1,), (1,)), ((), ()))
TN = (((0,), (0,)), ((), ()))


def _pcall(body, **kw):
    return pl.pallas_call(body, **kw)


def _params(*sem):
    return pltpu.CompilerParams(dimension_semantics=sem)


def _dot(a, b, dims=NN):
    return lax.dot_general(a, b, dims, preferred_element_type=F32)


def _sigmoid(x):
    return 0.5 * jnp.tanh(0.5 * x) + 0.5


def _matmul(a, b, *, mode, grid, a_spec, b_spec, o_spec, out_shape, name, add=None, add_spec=None, after=None):
    dims = {"nn": NN, "nt": NT, "tn": TN}[mode]
    nk = grid[2]
    has_add = add is not None
    a_list, b_list = (list(a), list(b)) if isinstance(a, (list, tuple)) else ([a], [b])
    a_specs, b_specs = (list(a_spec), list(b_spec)) if isinstance(a_spec, (list, tuple)) else ([a_spec], [b_spec])
    nt = len(a_list)
    n_in = 2 * nt + int(has_add) + int(after is not None)

    def body(*refs):
        a_refs, b_refs = refs[:nt], refs[nt:2 * nt]
        add_ref = refs[2 * nt] if has_add else None
        o_ref = refs[n_in]
        part = _dot(a_refs[0][...].astype(BF16), b_refs[0][...].astype(BF16), dims)
        for ar, br in zip(a_refs[1:], b_refs[1:]):
            part = part + _dot(ar[...].astype(BF16), br[...].astype(BF16), dims)

        def finish(acc):
            if has_add:
                acc = acc + add_ref[...]
            o_ref[...] = acc.astype(o_ref.dtype)

        if nk == 1:
            finish(part)
        else:
            acc_ref = refs[-1]
            k = pl.program_id(2)

            @pl.when(k == 0)
            def _():
                acc_ref[...] = part

            @pl.when(k > 0)
            def _():
                acc_ref[...] += part

            @pl.when(k == nk - 1)
            def _():
                finish(acc_ref[...])

    in_specs = a_specs + b_specs + ([add_spec] if has_add else [])
    args = tuple(a_list) + tuple(b_list) + ((add,) if has_add else ())
    if after is not None:
        in_specs, args = in_specs + [pl.BlockSpec(memory_space=pl.ANY)], args + (after,)
    scratch = [] if nk == 1 else [pltpu.VMEM(tuple(d for d in o_spec.block_shape if d is not None), F32)]
    return _pcall(
        body, name=name, grid=grid, in_specs=in_specs, out_specs=o_spec, out_shape=out_shape,
        scratch_shapes=scratch, compiler_params=_params("parallel", "parallel", "arbitrary"),
    )(*args)


def _mm_simple(a, b, *, mode, tm, tn, tk, out_dtype, name, add=None, after=None):
    if mode == "tn":
        K, M = a.shape
    else:
        M, K = a.shape
    N = b.shape[0] if mode == "nt" else b.shape[1]
    grid = (M // tm, N // tn, K // tk)
    resident = dict(pipeline_mode=pl.Buffered(1)) if (tn == N and tk == K) else {}
    a_spec = pl.BlockSpec((tk, tm), lambda i, j, k: (k, i)) if mode == "tn" else pl.BlockSpec((tm, tk), lambda i, j, k: (i, k))
    b_spec = (pl.BlockSpec((tn, tk), lambda i, j, k: (j, k), **resident) if mode == "nt"
              else pl.BlockSpec((tk, tn), lambda i, j, k: (k, j), **resident))
    o_spec = pl.BlockSpec((tm, tn), lambda i, j, k: (i, j))
    return _matmul(a, b, mode=mode, grid=grid, a_spec=a_spec, b_spec=b_spec, o_spec=o_spec,
                   out_shape=jax.ShapeDtypeStruct((M, N), out_dtype), name=name, add=add,
                   add_spec=o_spec if add is not None else None, after=after)


def _matmul_rows(a_list, a_specs, b_list, b_specs, extras, extra_specs, out_specs, out_shape, epilogue, *,
                 mode, steps, name, after=None, scratch=()):
    dims = {"nn": NN, "nt": NT}[mode]
    nt, ne = len(a_list), len(extras)
    n_in = 2 * nt + ne + int(after is not None)

    def body(*refs):
        acc = _dot(refs[0][...].astype(BF16), refs[nt][...].astype(BF16), dims)
        for k in range(1, nt):
            acc = acc + _dot(refs[k][...].astype(BF16), refs[nt + k][...].astype(BF16), dims)
        epilogue(pl.program_id(0), acc, refs[2 * nt:2 * nt + ne], refs[n_in:])

    in_specs = list(a_specs) + list(b_specs) + list(extra_specs)
    args = tuple(a_list) + tuple(b_list) + tuple(extras)
    if after is not None:
        in_specs, args = in_specs + [pl.BlockSpec(memory_space=pl.ANY)], args + (after,)
    return _pcall(body, name=name, grid=(steps,), in_specs=in_specs, out_specs=out_specs, out_shape=out_shape,
                  scratch_shapes=list(scratch), compiler_params=_params("arbitrary"))(*args)


def _rms_bwd_tile(dy, x, gain, dres):
    r = lax.rsqrt(jnp.mean(x * x, axis=-1, keepdims=True) + EPS)
    xhat = x * r
    u = dy * gain
    return dres + r * (u - xhat * jnp.mean(u * xhat, axis=-1, keepdims=True)), jnp.sum(dy * xhat, axis=0, keepdims=True)


def _loss_tile(i, x, tgt, gain):
    d = x.shape[-1]
    r = lax.rsqrt(jnp.mean(x * x, axis=-1, keepdims=True) + EPS)
    xhat = x * r
    counted = (i * TOK_TILE + lax.broadcasted_iota(jnp.int32, (TOK_TILE, 1), 0)) >= PREFIX
    err = jnp.where(counted, xhat * gain - tgt, 0.0)
    dy = err * (1.0 / d)
    u = dy * gain
    dh = r * (u - xhat * jnp.mean(u * xhat, axis=-1, keepdims=True))
    return 0.5 * jnp.sum(jnp.mean(err * err, axis=-1, keepdims=True)), dh, jnp.sum(dy * xhat, axis=0, keepdims=True)


def _accumulate(ref, i, part):
    @pl.when(i == 0)
    def _():
        ref[...] = part

    @pl.when(i > 0)
    def _():
        ref[...] += part


def _prep_norm(x, meta, gain, name):
    seq, d = x.shape
    t = seq + PREFIX

    def body(xa_ref, xb_ref, xc_ref, meta_ref, g_ref, h_ref, n_ref):
        i = pl.program_id(0)

        @pl.when(i == 0)
        def _():
            h_ref[0:N_PAD, :] = jnp.zeros((N_PAD, d), F32)
            h_ref[N_PAD:ROW_TILE, :] = meta_ref[...]

        @pl.when(i > 0)
        def _():
            h_ref[0:ROW_TILE, :] = xa_ref[...]

        h_ref[ROW_TILE:2 * ROW_TILE, :] = xb_ref[...]
        h_ref[2 * ROW_TILE:3 * ROW_TILE, :] = xc_ref[...]
        h = h_ref[...]
        r = lax.rsqrt(jnp.mean(h * h, axis=-1, keepdims=True) + EPS)
        n_ref[...] = (h * r * g_ref[...]).astype(BF16)

    return _pcall(
        body, name=name, grid=(t // TOK_TILE,),
        in_specs=_shifted_row_specs(d) + [pl.BlockSpec((N_META, d), lambda i: (0, 0)), pl.BlockSpec((1, d), lambda i: (0, 0))],
        out_specs=[pl.BlockSpec((TOK_TILE, d), lambda i: (i, 0)), pl.BlockSpec((TOK_TILE, d), lambda i: (i, 0))],
        out_shape=[jax.ShapeDtypeStruct((t, d), F32), jax.ShapeDtypeStruct((t, d), BF16)],
        compiler_params=_params("parallel"),
    )(x, x, x, meta, gain)


def _shifted_row_specs(d):
    blocks_per_tile = TOK_TILE // ROW_TILE
    return [pl.BlockSpec((ROW_TILE, d), lambda i, r=r: (jnp.maximum(blocks_per_tile * i + r, 0), 0)) for r in (-1, 0, 1)]


def _ret_consts(bk):
    gam = 1.0 - 2.0 ** (-5.0 - np.arange(RET_HEADS))
    n = np.arange(bk)
    same_or_earlier_chunk = (n[None, :] // 64) <= (n[:, None] // 64)
    w = gam[:, None, None] ** np.abs(n[:, None] - n[None, :])[None] * same_or_earlier_chunk[None]
    wq = gam[:, None] ** (n[None, :] + 1.0)
    wk = gam[:, None] ** (bk - 1.0 - n[None, :])
    mask = (np.arange(RET_QK)[None, :] // RET_DK) == np.arange(RET_HEADS)[:, None]
    return (jnp.asarray(w, F32), jnp.asarray(wq[:, :, None], F32), jnp.asarray(wk[:, :, None], F32),
            jnp.asarray(mask[:, None, :], F32), [float(g ** bk) for g in gam])


def _rope_tables(t):
    half = RET_DK // 2
    inv = 1.0 / (ROPE_BASE ** (jnp.arange(half, dtype=F32) / half))
    ang = jnp.arange(t).astype(F32)[:, None] * inv[None, :]
    cos, sin = jnp.cos(ang), jnp.sin(ang)
    return (jnp.tile(jnp.concatenate([cos, cos], axis=1), (1, RET_HEADS)),
            jnp.tile(jnp.concatenate([-sin, sin], axis=1), (1, RET_HEADS)))


def _swap_halves(x):
    outs = []
    for s in range(x.shape[1] // LANE):
        xs = x[:, LANE * s:LANE * (s + 1)]
        lane = lax.broadcasted_iota(jnp.int32, xs.shape, 1)
        outs.append(jnp.where((lane & 32) == 0, pltpu.roll(xs, LANE - 32, axis=1), pltpu.roll(xs, 32, axis=1)))
    return outs[0] if len(outs) == 1 else jnp.concatenate(outs, axis=1)


def _rope(x, cos, sin_signed):
    return x * cos + _swap_halves(x) * sin_signed


def _rope_t(dx, cos, sin_signed):
    return dx * cos + _swap_halves(dx * sin_signed)


def _ret_fwd(proj, cos, sin, gain, name):
    t = proj.shape[0]
    bk = TOK_TILE
    nb = t // bk
    w, wq, wk, mask, g_blk = _ret_consts(bk)

    def body(q_ref, k_ref, v_ref, rg_ref, cos_ref, sin_ref, w_ref, wq_ref, wk_ref, mask_ref, gain_ref,
             opre_ref, og_ref, st_ref, r_ref):
        i = pl.program_id(0)

        @pl.when(i == 0)
        def _():
            r_ref[...] = jnp.zeros_like(r_ref)

        c, s = cos_ref[...], sin_ref[...]
        valid = ((i * bk + lax.broadcasted_iota(jnp.int32, (bk, 1), 0)) >= N_PAD).astype(F32)
        qr = _rope(q_ref[...], c, s)
        kr = _rope(k_ref[...], c, s) * QK_SCALE * valid
        kb = kr.astype(BF16)
        for h in range(RET_HEADS):
            hm = mask_ref[h]
            cols = slice(RET_DV * h, RET_DV * (h + 1))
            vh = v_ref[:, cols].astype(BF16)
            r_prev = r_ref[h]
            st_ref[0, h] = r_prev
            sm = _dot((qr * hm).astype(BF16), kb, NT) * w_ref[h]
            o = _dot(sm.astype(BF16), vh) + _dot((qr * (hm * wq_ref[h])).astype(BF16), r_prev.astype(BF16))
            r_ref[h] = g_blk[h] * r_prev + _dot((kr * wk_ref[h]).astype(BF16), vh, TN)
            opre_ref[:, cols] = o
            rstd = lax.rsqrt(jnp.mean(o * o, axis=-1, keepdims=True) + EPS)
            rg = rg_ref[:, cols]
            og_ref[:, cols] = (o * rstd * gain_ref[:, cols] * (rg * _sigmoid(rg))).astype(BF16)

    full = lambda shape: pl.BlockSpec(shape, lambda i: (0,) * len(shape))
    return _pcall(
        body, name=name, grid=(nb,),
        in_specs=[pl.BlockSpec((bk, RET_QK), lambda i: (i, 0)), pl.BlockSpec((bk, RET_QK), lambda i: (i, 1)),
                  pl.BlockSpec((bk, RET_V), lambda i: (i, 1)), pl.BlockSpec((bk, RET_V), lambda i: (i, 2)),
                  pl.BlockSpec((bk, RET_QK), lambda i: (i, 0)), pl.BlockSpec((bk, RET_QK), lambda i: (i, 0)),
                  full((RET_HEADS, bk, bk)), full((RET_HEADS, bk, 1)), full((RET_HEADS, bk, 1)),
                  full((RET_HEADS, 1, RET_QK)), full((1, RET_V))],
        out_specs=[pl.BlockSpec((bk, RET_V), lambda i: (i, 0)), pl.BlockSpec((bk, RET_V), lambda i: (i, 0)),
                   pl.BlockSpec((1, RET_HEADS, RET_QK, RET_DV), lambda i: (i, 0, 0, 0))],
        out_shape=[jax.ShapeDtypeStruct((t, RET_V), F32), jax.ShapeDtypeStruct((t, RET_V + FOX_W), BF16),
                   jax.ShapeDtypeStruct((nb, RET_HEADS, RET_QK, RET_DV), F32)],
        scratch_shapes=[pltpu.VMEM((RET_HEADS, RET_QK, RET_DV), F32)],
        compiler_params=_params("arbitrary"),
    )(proj, proj, proj, proj, cos, sin, w, wq, wk, mask, gain)


def _ret_bwd(proj, cos, sin, gain, dmixed, opre, states, name):
    t = proj.shape[0]
    bk = TOK_TILE
    nb = t // bk
    w, wq, wk, mask, g_blk = _ret_consts(bk)
    v0, g0 = 2 * RET_QK, 2 * RET_QK + RET_V

    def body(q_ref, k_ref, v_ref, rg_ref, cos_ref, sin_ref, w_ref, wq_ref, wk_ref, mask_ref, gain_ref,
             dog_ref, opre_ref, st_ref, dp_ref, gg_ref, dr_ref):
        step = pl.program_id(0)
        i = nb - 1 - step

        @pl.when(step == 0)
        def _():
            dr_ref[...] = jnp.zeros_like(dr_ref)
            gg_ref[...] = jnp.zeros_like(gg_ref)

        c, s = cos_ref[...], sin_ref[...]
        valid = ((i * bk + lax.broadcasted_iota(jnp.int32, (bk, 1), 0)) >= N_PAD).astype(F32)
        qr = _rope(q_ref[...], c, s)
        kr = _rope(k_ref[...], c, s) * QK_SCALE * valid
        kb = kr.astype(BF16)
        dqr = jnp.zeros((bk, RET_QK), F32)
        dkr = jnp.zeros((bk, RET_QK), F32)
        for h in range(RET_HEADS):
            hm = mask_ref[h]
            cols = slice(RET_DV * h, RET_DV * (h + 1))
            vh = v_ref[:, cols].astype(BF16)
            o = opre_ref[:, cols]
            rstd = lax.rsqrt(jnp.mean(o * o, axis=-1, keepdims=True) + EPS)
            xhat = o * rstd
            rg = rg_ref[:, cols]
            sg = _sigmoid(rg)
            gate = rg * sg
            gn = gain_ref[:, cols]
            dog = dog_ref[:, cols]
            dp_ref[:, g0 + RET_DV * h:g0 + RET_DV * (h + 1)] = (
                dog * xhat * gn * (sg * (1.0 + rg * (1.0 - sg)))).astype(BF16)
            gg_ref[:, cols] += jnp.sum(dog * xhat * gate, axis=0, keepdims=True)
            dxh = dog * gn * gate
            do = (rstd * (dxh - xhat * jnp.mean(dxh * xhat, axis=-1, keepdims=True))).astype(BF16)
            qm = (qr * hm).astype(BF16)
            qw = (qr * (hm * wq_ref[h])).astype(BF16)
            kw = (kr * wk_ref[h]).astype(BF16)
            wh = w_ref[h]
            sm = (_dot(qm, kb, NT) * wh).astype(BF16)
            ds = (_dot(do, vh, NT) * wh).astype(BF16)
            dr = dr_ref[h]
            drb = dr.astype(BF16)
            dp_ref[:, v0 + RET_DV * h:v0 + RET_DV * (h + 1)] = (_dot(sm, do, TN) + _dot(kw, drb)).astype(BF16)
            dqr = dqr + _dot(ds, kb) * hm + _dot(do, st_ref[0, h].astype(BF16), NT) * (hm * wq_ref[h])
            dkr = dkr + _dot(ds, qm, TN) + _dot(vh, drb, NT) * wk_ref[h]
            dr_ref[h] = g_blk[h] * dr + _dot(qw, do, TN)
        dp_ref[:, 0:RET_QK] = _rope_t(dqr, c, s).astype(BF16)
        dp_ref[:, RET_QK:2 * RET_QK] = _rope_t(dkr * (QK_SCALE * valid), c, s).astype(BF16)

    full = lambda shape: pl.BlockSpec(shape, lambda i: (0,) * len(shape))
    rev = lambda col: (lambda i: (nb - 1 - i, col))
    return _pcall(
        body, name=name, grid=(nb,),
        in_specs=[pl.BlockSpec((bk, RET_QK), rev(0)), pl.BlockSpec((bk, RET_QK), rev(1)),
                  pl.BlockSpec((bk, RET_V), rev(1)), pl.BlockSpec((bk, RET_V), rev(2)),
                  pl.BlockSpec((bk, RET_QK), rev(0)), pl.BlockSpec((bk, RET_QK), rev(0)),
                  full((RET_HEADS, bk, bk)), full((RET_HEADS, bk, 1)), full((RET_HEADS, bk, 1)),
                  full((RET_HEADS, 1, RET_QK)), full((1, RET_V)),
                  pl.BlockSpec((bk, RET_V), rev(0)), pl.BlockSpec((bk, RET_V), rev(0)),
                  pl.BlockSpec((1, RET_HEADS, RET_QK, RET_DV), lambda i: (nb - 1 - i, 0, 0, 0))],
        out_specs=[pl.BlockSpec((bk, g0 + RET_V), rev(0)), pl.BlockSpec((1, RET_V), lambda i: (0, 0))],
        out_shape=[jax.ShapeDtypeStruct((t, IN_PAD), BF16), jax.ShapeDtypeStruct((1, RET_V), F32)],
        scratch_shapes=[pltpu.VMEM((RET_HEADS, RET_QK, RET_DV), F32)],
        compiler_params=_params("arbitrary"),
    )(proj, proj, proj, proj, cos, sin, w, wq, wk, mask, gain, dmixed, opre, states)


def _forget_cumsum(proj, bias, name):
    t = proj.shape[0]
    rt = TOK_TILE
    nb = t // rt
    tril = jnp.asarray(np.tril(np.ones((rt, rt))), F32)

    def body(z_ref, b_ref, tril_ref, c_ref, carry_ref):
        i = pl.program_id(0)

        @pl.when(i == 0)
        def _():
            carry_ref[...] = jnp.zeros_like(carry_ref)

        z = z_ref[...] + b_ref[...]
        logf = jnp.minimum(z, 0.0) - jnp.log(1.0 + jnp.exp(-jnp.abs(z)))
        c = lax.dot_general(tril_ref[...], logf, NN, precision=lax.Precision.HIGHEST,
                            preferred_element_type=F32) + carry_ref[...]
        c_ref[...] = c
        carry_ref[...] = c[rt - 1:rt, :]

    return _pcall(
        body, name=name, grid=(nb,),
        in_specs=[pl.BlockSpec((rt, LANE), lambda i: (i, FF_COL_BLOCK)), pl.BlockSpec((1, LANE), lambda i: (0, 0)),
                  pl.BlockSpec((rt, rt), lambda i: (0, 0))],
        out_specs=pl.BlockSpec((rt, LANE), lambda i: (i, 0)),
        out_shape=jax.ShapeDtypeStruct((t, LANE), F32),
        scratch_shapes=[pltpu.VMEM((1, LANE), F32)],
        compiler_params=_params("arbitrary"),
    )(proj, bias, tril)


def _forget_cumsum_bwd(proj, bias, drs, dcs, dproj, name):
    t = proj.shape[0]
    rt = TOK_TILE
    nb = t // rt
    triu = jnp.asarray(np.triu(np.ones((rt, rt))), F32)

    def body(z_ref, b_ref, triu_ref, drs_ref, dcs_ref, dproj_in, dz_ref, gb_ref, carry_ref):
        step = pl.program_id(0)

        @pl.when(step == 0)
        def _():
            carry_ref[...] = jnp.zeros_like(carry_ref)
            gb_ref[...] = jnp.zeros_like(gb_ref)

        dlogf = lax.dot_general(triu_ref[...], drs_ref[...] - dcs_ref[...], NN, precision=lax.Precision.HIGHEST,
                                preferred_element_type=F32) + carry_ref[...]
        carry_ref[...] = dlogf[0:1, :]
        z = z_ref[...] + b_ref[...]
        is_head = lax.broadcasted_iota(jnp.int32, (rt, LANE), 1) < FOX_HEADS
        dz = jnp.where(is_head, dlogf / (1.0 + jnp.exp(z)), 0.0)
        dz_ref[...] = dz.astype(BF16)
        gb_ref[...] += jnp.sum(dz, axis=0, keepdims=True)

    return _pcall(
        body, name=name, grid=(nb,),
        in_specs=[pl.BlockSpec((rt, LANE), lambda i: (nb - 1 - i, FF_COL_BLOCK)),
                  pl.BlockSpec((1, LANE), lambda i: (0, 0)),
                  pl.BlockSpec((rt, rt), lambda i: (0, 0)),
                  pl.BlockSpec((rt, LANE), lambda i: (nb - 1 - i, 0)),
                  pl.BlockSpec((rt, LANE), lambda i: (nb - 1 - i, 0)),
                  pl.BlockSpec(memory_space=pl.ANY)],
        out_specs=[pl.BlockSpec((rt, LANE), lambda i: (nb - 1 - i, FF_COL_BLOCK)),
                   pl.BlockSpec((1, LANE), lambda i: (0, 0))],
        out_shape=[jax.ShapeDtypeStruct(dproj.shape, BF16), jax.ShapeDtypeStruct((1, LANE), F32)],
        input_output_aliases={5: 0},
        scratch_shapes=[pltpu.VMEM((1, LANE), F32)],
        compiler_params=_params("arbitrary"),
    )(proj, bias, triu, drs, dcs, dproj)


FOX_PAIRS = FOX_HEADS // 2
L_ONE_Q = FOX_DH
L_ONE_K = FOX_DH + 3
L_LSE = FOX_DH + 4


def _split3(x):
    hi = x.astype(BF16).astype(F32)
    r = x - hi
    mid = r.astype(BF16).astype(F32)
    return hi, mid, r - mid


def _head_to_low(slab, e):
    return slab if e == 0 else pltpu.roll(slab, FOX_DH, axis=1)


def _pair(a, b, low):
    return jnp.where(low, a, pltpu.roll(b, FOX_DH, axis=1))


def _fox_prep(proj, c, name):
    t = proj.shape[0]
    tq = TOK_TILE

    def body(p_ref, c_ref, qa_ref, ka_ref, va_ref, qt_ref, vt_ref):
        i = pl.program_id(0)
        lane = lax.broadcasted_iota(jnp.int32, (tq, LANE), 1)
        low = lane < FOX_DH
        live = (i * tq + lax.broadcasted_iota(jnp.int32, (tq, 1), 0)) >= N_PAD
        q_tail = jnp.where(lane < L_ONE_Q + 3, 1.0, 0.0)
        k_ones = (lane >= L_ONE_K) & (lane < L_ONE_K + 4)
        v_tail = jnp.where(lane < FOX_DH + 2, 1.0, 0.0)
        bias_parts = _split3(jnp.where(live, -c_ref[...], NEG))
        for pair in range(FOX_PAIRS):
            base = 3 * LANE * pair
            for e in range(2):
                h = 2 * pair + e
                q = _head_to_low(p_ref[:, base:base + LANE], e)
                k = _head_to_low(p_ref[:, base + LANE:base + 2 * LANE], e)
                v = _head_to_low(p_ref[:, base + 2 * LANE:base + 3 * LANE], e)
                hi, mid, lo = [part[:, h:h + 1] for part in bias_parts]
                ka = jnp.where(low, k, jnp.where(k_ones, 1.0, 0.0))
                ka = jnp.where(lane == L_ONE_Q, hi, jnp.where(lane == L_ONE_Q + 1, mid, jnp.where(lane == L_ONE_Q + 2, lo, ka)))
                qa = jnp.where(low, q * QK_SCALE, q_tail)
                va = jnp.where(low, v, v_tail)
                qa_ref[h] = qa.astype(BF16)
                ka_ref[h] = ka.astype(BF16)
                va_ref[h] = va.astype(BF16)
                qt_ref[h] = qa.T.astype(BF16)
                vt_ref[h] = va.T.astype(BF16)

    out = jax.ShapeDtypeStruct((FOX_HEADS, t, LANE), BF16)
    out_t = jax.ShapeDtypeStruct((FOX_HEADS, t // tq, LANE, tq), BF16)
    ospec = pl.BlockSpec((FOX_HEADS, tq, LANE), lambda i: (0, i, 0))
    tspec = pl.BlockSpec((FOX_HEADS, None, LANE, tq), lambda i: (0, i, 0, 0))
    return _pcall(
        body, name=name, grid=(t // tq,),
        in_specs=[pl.BlockSpec((tq, 3 * FOX_W), lambda i: (i, 1)), pl.BlockSpec((tq, LANE), lambda i: (i, 0))],
        out_specs=[ospec, ospec, ospec, tspec, tspec], out_shape=[out, out, out, out_t, out_t],
        compiler_params=_params("parallel"),
    )(proj, c)


STEP_PAIRS = 2
STEP_HEADS = 2 * STEP_PAIRS
FOX_GROUPS = FOX_PAIRS // STEP_PAIRS
FWD_PAIRS = 4
FWD_HEADS = 2 * FWD_PAIRS
FWD_GROUPS = FOX_PAIRS // FWD_PAIRS


def _blockdiag(a, b):
    z = jnp.zeros_like(a)
    return jnp.concatenate([jnp.concatenate([a, z], axis=1), jnp.concatenate([z, b], axis=1)], axis=0)


def _fox_fwd(qt, ka, vt, mixed, name):
    nh, nq, tq, _ = ka.shape
    t = nq * tq

    def body(qt_ref, ka_ref, vt_ref, mixed_in, mixed_ref, o_ref, lse_ref):
        i = pl.program_id(1)
        lane = lax.broadcasted_iota(jnp.int32, (tq, LANE), 1)
        key_le_query = lax.broadcasted_iota(jnp.int32, (tq, tq), 0) <= lax.broadcasted_iota(jnp.int32, (tq, tq), 1)
        qts = [qt_ref[h] for h in range(FWD_HEADS)]

        def logits(j):
            return [_dot(ka_ref[h, j], qts[h]) for h in range(FWD_HEADS)]

        def update(j, scores, carry, diagonal):
            new = []
            for h in range(FWD_HEADS):
                m, acc = carry[h]
                s = jnp.where(key_le_query, scores[h], NEG) if diagonal else scores[h]
                m_new = jnp.maximum(m, jnp.max(s, axis=0, keepdims=True))
                p = jnp.exp(s - m_new).astype(BF16)
                new.append((m_new, jnp.exp(m - m_new) * acc + _dot(vt_ref[h, j], p)))
            return tuple(new)

        init = tuple((jnp.full((1, tq), NEG, F32), jnp.zeros((LANE, tq), F32)) for _ in range(FWD_HEADS))
        carry = lax.fori_loop(0, i, lambda j, cr: update(j, logits(j), cr, False), init)
        outs, lse_rows = [], []
        for m, acc in update(i, logits(i), carry, True):
            l = acc[FOX_DH:FOX_DH + 1, :]
            outs.append((acc / l).T)
            lse_rows.append(m + jnp.log(l))
        lse_rows.append(jnp.zeros((LANE - FWD_HEADS, tq), F32))
        o_all = jnp.concatenate([_pair(outs[2 * c], outs[2 * c + 1], lane < FOX_DH) for c in range(FWD_PAIRS)], axis=1)
        mixed_ref[...] = o_all.astype(BF16)
        o_ref[...] = o_all
        lse_ref[...] = jnp.concatenate(lse_rows, axis=0).T

    width = FWD_PAIRS * LANE
    whole = pl.BlockSpec((FWD_HEADS, nq, tq, LANE), lambda g, i: (g, 0, 0, 0), pipeline_mode=pl.Buffered(1))
    whole_t = pl.BlockSpec((FWD_HEADS, nq, LANE, tq), lambda g, i: (g, 0, 0, 0), pipeline_mode=pl.Buffered(1))
    return _pcall(
        body, name=name, grid=(FWD_GROUPS, nq),
        in_specs=[pl.BlockSpec((FWD_HEADS, None, LANE, tq), lambda g, i: (g, i, 0, 0)), whole, whole_t,
                  pl.BlockSpec(memory_space=pl.ANY)],
        out_specs=[pl.BlockSpec((tq, width), lambda g, i: (i, RET_V // width + g)),
                   pl.BlockSpec((tq, width), lambda g, i: (i, g)),
                   pl.BlockSpec((None, tq, LANE), lambda g, i: (g, i, 0))],
        out_shape=[jax.ShapeDtypeStruct(mixed.shape, BF16), jax.ShapeDtypeStruct((t, FOX_W), F32),
                   jax.ShapeDtypeStruct((FWD_GROUPS, t, LANE), F32)],
        input_output_aliases={3: 0},
        compiler_params=_params("parallel", "parallel"),
    )(qt, ka, vt, mixed)


def _fox_prep_bwd(dmixed, o_fox, lse, qa, name):
    t = dmixed.shape[0]
    tq = TOK_TILE

    def body(dm_ref, o_ref, lse_ref, qa_ref, qab_ref, doa_ref):
        i = pl.program_id(0)
        lane = lax.broadcasted_iota(jnp.int32, (tq, LANE), 1)
        low = lane < FOX_DH
        live = (i * tq + lax.broadcasted_iota(jnp.int32, (tq, 1), 0)) >= N_PAD
        lse_parts = [_split3(jnp.where(live, -lse_ref[grp], 0.0)) for grp in range(FWD_GROUPS)]
        for pair in range(FOX_PAIRS):
            cols = slice(LANE * pair, LANE * (pair + 1))
            d_slab = dm_ref[:, cols]
            prod = d_slab * o_ref[:, cols]
            for e in range(2):
                h = 2 * pair + e
                nd = -jnp.sum(jnp.where(low, _head_to_low(prod, e), 0.0), axis=-1, keepdims=True)
                nd_hi = nd.astype(BF16).astype(F32)
                doa = jnp.where(low, _head_to_low(d_slab, e), 0.0)
                doa = jnp.where(lane == FOX_DH, nd_hi, jnp.where(lane == FOX_DH + 1, nd - nd_hi, doa))
                doa_ref[h] = doa.astype(BF16)
                lane_h = h % FWD_HEADS
                hi, mid, lo = [part[:, lane_h:lane_h + 1] for part in lse_parts[h // FWD_HEADS]]
                qab = qa_ref[h].astype(F32)
                qab = jnp.where(lane == L_LSE, hi, jnp.where(lane == L_LSE + 1, mid, jnp.where(lane == L_LSE + 2, lo, qab)))
                qab_ref[h] = qab.astype(BF16)

    out = jax.ShapeDtypeStruct((FOX_HEADS, t, LANE), BF16)
    hspec = pl.BlockSpec((FOX_HEADS, tq, LANE), lambda i: (0, i, 0))
    return _pcall(
        body, name=name, grid=(t // tq,),
        in_specs=[pl.BlockSpec((tq, FOX_W), lambda i: (i, 1)), pl.BlockSpec((tq, FOX_W), lambda i: (i, 0)),
                  pl.BlockSpec((FWD_GROUPS, tq, LANE), lambda i: (0, i, 0)), hspec],
        out_specs=[hspec, hspec], out_shape=[out, out],
        compiler_params=_params("parallel"),
    )(dmixed, o_fox, lse, qa)


def _fox_bwd(qab, doa, ka, va, dproj, name):
    nh, nq, tq, _ = qab.shape
    t = nq * tq
    slab = 3 * LANE * STEP_PAIRS
    group0 = (2 * RET_QK + 2 * RET_V) // slab

    def body(qab_ref, doa_ref, ka_ref, va_ref, dproj_in, dp_ref, drs_ref, dcs_ref, dq_ref):
        g, j = pl.program_id(0), pl.program_id(1)

        @pl.when((g == 0) & (j == 0))
        def _():
            drs_ref[...] = jnp.zeros_like(drs_ref)
            dcs_ref[...] = jnp.zeros_like(dcs_ref)

        @pl.when(j == 0)
        def _():
            dq_ref[...] = jnp.zeros_like(dq_ref)

        lane = lax.broadcasted_iota(jnp.int32, (tq, LANE), 1)
        low = lane < FOX_DH
        key_le_query = lax.broadcasted_iota(jnp.int32, (tq, tq), 0) <= lax.broadcasted_iota(jnp.int32, (tq, tq), 1)

        def by_head(c, a, b, col):
            h = STEP_HEADS * g + 2 * c
            return jnp.where(lane == h, a[:, col:col + 1], jnp.where(lane == h + 1, b[:, col:col + 1], 0.0))

        kbs = [ka_ref[h] for h in range(STEP_HEADS)]
        vbs = [va_ref[h] for h in range(STEP_HEADS)]

        def step(i, carry, diagonal):
            qbs = [qab_ref[h, i] for h in range(STEP_HEADS)]
            dobs = [doa_ref[h, i] for h in range(STEP_HEADS)]
            st = [_dot(kbs[h], qbs[h], NT) for h in range(STEP_HEADS)]
            dpt = [_dot(vbs[h], dobs[h], NT) for h in range(STEP_HEADS)]
            new = []
            for h in range(STEP_HEADS):
                p = jnp.exp(st[h])
                if diagonal:
                    p = jnp.where(key_le_query, p, 0.0)
                ds = (p * dpt[h]).astype(BF16)
                dq_ref[h, i] += _dot(ds, kbs[h], TN)
                dk, dv = carry[h]
                new.append((dk + _dot(ds, qbs[h]), dv + _dot(p.astype(BF16), dobs[h])))
            return tuple(new)

        zero = jnp.zeros((tq, LANE), F32)
        carry = step(j, tuple((zero, zero) for _ in range(STEP_HEADS)), True)
        carry = lax.fori_loop(j + 1, nq, lambda i, cr: step(i, cr, False), carry)
        rows = pl.ds(pl.multiple_of(j * tq, tq), tq)
        for c in range(STEP_PAIRS):
            (dka, dva), (dkb, dvb) = carry[2 * c], carry[2 * c + 1]
            c0 = 3 * LANE * c
            dp_ref[rows, c0 + LANE:c0 + 2 * LANE] = _pair(dka, dkb, low).astype(BF16)
            dp_ref[rows, c0 + 2 * LANE:c0 + 3 * LANE] = _pair(dva, dvb, low).astype(BF16)
            dcs_ref[rows, :] += by_head(c, dka, dkb, L_ONE_Q)

        @pl.when(j == nq - 1)
        def _():
            for c in range(STEP_PAIRS):
                for blk in range(nq):
                    r = slice(blk * tq, (blk + 1) * tq)
                    a, b = dq_ref[2 * c, blk], dq_ref[2 * c + 1, blk]
                    dp_ref[r, 3 * LANE * c:3 * LANE * c + LANE] = (_pair(a, b, low) * QK_SCALE).astype(BF16)
                    drs_ref[r, :] += by_head(c, a, b, L_ONE_K)

    whole = pl.BlockSpec((STEP_HEADS, nq, tq, LANE), lambda g, j: (g, 0, 0, 0), pipeline_mode=pl.Buffered(1))
    blk = pl.BlockSpec((STEP_HEADS, None, tq, LANE), lambda g, j: (g, j, 0, 0))
    sums = pl.BlockSpec((t, LANE), lambda g, j: (0, 0), pipeline_mode=pl.Buffered(1))
    return _pcall(
        body, name=name, grid=(FOX_GROUPS, nq),
        in_specs=[whole, whole, blk, blk, pl.BlockSpec(memory_space=pl.ANY)],
        out_specs=[pl.BlockSpec((t, slab), lambda g, j: (0, group0 + g)), sums, sums],
        out_shape=[jax.ShapeDtypeStruct(dproj.shape, BF16), jax.ShapeDtypeStruct((t, LANE), F32),
                   jax.ShapeDtypeStruct((t, LANE), F32)],
        input_output_aliases={4: 0},
        scratch_shapes=[pltpu.VMEM((STEP_HEADS, nq, tq, LANE), F32)],
        compiler_params=_params("arbitrary", "arbitrary"),
    )(qab, doa, ka, va, dproj)


HALO = 8


def _rows_ext(ref, r0, rows, t, before, after):
    lo, hi = r0 - before, r0 + rows + after
    width = ref.shape[-1]
    parts = []
    if lo < 0:
        parts.append(jnp.zeros((-lo, width), F32))
    parts.append(ref[max(lo, 0):min(hi, t), :].astype(F32))
    if hi > t:
        parts.append(jnp.zeros((hi - t, width), F32))
    return parts[0] if len(parts) == 1 else jnp.concatenate(parts, axis=0)


def _conv_taps(a_ext, r0_ext, cw_ref, cb_ref):
    n = a_ext.shape[0]
    if r0_ext < N_PAD:
        row = r0_ext + lax.broadcasted_iota(jnp.int32, (n, 1), 0)
        a_ext = jnp.where(row >= N_PAD, a_ext, 0.0)
    a1 = pltpu.roll(a_ext, 1, axis=0)
    a2 = pltpu.roll(a_ext, 2, axis=0)
    acc = cb_ref[...] + a2 * cw_ref[0:1, :] + a1 * cw_ref[1:2, :] + a_ext * cw_ref[2:3, :]
    return a_ext, a1, a2, acc


FF_COLS = 256


def _up_conv_fwd(n2, w_up_t, conv_w8, conv_b, name):
    t, d = n2.shape
    f = w_up_t.shape[1]
    rows = TOK_TILE
    starts = list(range(0, t, rows))

    def body(n_ref, wa_ref, wb_ref, cw_ref, cb_ref, up_ref, g_ref):
        wa, wb = wa_ref[...], wb_ref[...]

        def project(r0):
            n_rows = n_ref[r0:r0 + rows, :]
            up_ref[0, r0:r0 + rows, :] = _dot(n_rows, wa, NT)
            up_ref[1, r0:r0 + rows, :] = _dot(n_rows, wb, NT)

        def activate(r0):
            a_ext = _rows_ext(up_ref.at[0], r0, rows, t, HALO, 0)
            _, _, _, acc = _conv_taps(a_ext, r0 - HALO, cw_ref, cb_ref)
            acc = acc[HALO:, :]
            g_ref[r0:r0 + rows, :] = (acc * _sigmoid(acc) * up_ref[1, r0:r0 + rows, :]).astype(BF16)

        project(starts[0])
        for r0, r_next in zip(starts, starts[1:] + [None]):
            if r_next is not None:
                project(r_next)
            activate(r0)

    return _pcall(
        body, name=name, grid=(f // FF_COLS,),
        in_specs=[pl.BlockSpec((t, d), lambda j: (0, 0), pipeline_mode=pl.Buffered(1)),
                  pl.BlockSpec((None, FF_COLS, d), lambda j: (0, j, 0)), pl.BlockSpec((None, FF_COLS, d), lambda j: (1, j, 0)),
                  pl.BlockSpec((8, FF_COLS), lambda j: (0, j)), pl.BlockSpec((1, FF_COLS), lambda j: (0, j))],
        out_specs=[pl.BlockSpec((2, t, FF_COLS), lambda j: (0, 0, j)), pl.BlockSpec((t, FF_COLS), lambda j: (0, j))],
        out_shape=[jax.ShapeDtypeStruct((2, t, f), F32), jax.ShapeDtypeStruct((t, f), BF16)],
        compiler_params=_params("parallel"),
    )(n2, w_up_t, w_up_t, conv_w8, conv_b)


def _dg_conv_bwd(up, conv_w8, conv_b, dh2, w_down, name):
    _, t, f = up.shape
    d = dh2.shape[1]
    rows = TOK_TILE
    starts = list(range(0, t, rows))

    def body(a_ref, b_ref, cw_ref, cb_ref, dh_ref, wd_ref, dup_ref, gcw_ref, gcb_ref, dg_ref):
        wd = wd_ref[...]

        def project(r0):
            dg_ref[r0:r0 + rows, :] = _dot(dh_ref[r0:r0 + rows, :], wd, NT)

        gw = [jnp.zeros((1, FF_COLS), F32) for _ in range(3)]
        gb = jnp.zeros((1, FF_COLS), F32)
        project(starts[0])
        for r0, r_next in zip(starts, starts[1:] + [None]):
            if r_next is not None:
                project(r_next)
            a_ext = _rows_ext(a_ref, r0, rows, t, HALO, HALO)
            b_ext = _rows_ext(b_ref, r0, rows, t, HALO, HALO)
            dg_ext = _rows_ext(dg_ref, r0, rows, t, HALO, HALO)
            a0, a1, a2, acc = _conv_taps(a_ext, r0 - HALO, cw_ref, cb_ref)
            sg = _sigmoid(acc)
            dacc = dg_ext * b_ext * (sg * (1.0 + acc * (1.0 - sg)))
            n = dacc.shape[0]
            da = (dacc * cw_ref[2:3, :] + pltpu.roll(dacc, n - 1, axis=0) * cw_ref[1:2, :]
                  + pltpu.roll(dacc, n - 2, axis=0) * cw_ref[0:1, :])
            core = slice(HALO, HALO + rows)
            da = da[core, :]
            if r0 < N_PAD:
                row = r0 + lax.broadcasted_iota(jnp.int32, (rows, 1), 0)
                da = jnp.where(row >= N_PAD, da, 0.0)
            dup_ref[0, r0:r0 + rows, :] = da.astype(BF16)
            dup_ref[1, r0:r0 + rows, :] = (dg_ext * acc * sg)[core, :].astype(BF16)
            dacc_c = dacc[core, :]
            gw[0] = gw[0] + jnp.sum(dacc_c * a2[core, :], axis=0, keepdims=True)
            gw[1] = gw[1] + jnp.sum(dacc_c * a1[core, :], axis=0, keepdims=True)
            gw[2] = gw[2] + jnp.sum(dacc_c * a0[core, :], axis=0, keepdims=True)
            gb = gb + jnp.sum(dacc_c, axis=0, keepdims=True)
        gcw_ref[...] = jnp.zeros((8, FF_COLS), F32)
        for tap in range(3):
            gcw_ref[tap:tap + 1, :] = gw[tap]
        gcb_ref[...] = gb

    return _pcall(
        body, name=name, grid=(f // FF_COLS,),
        in_specs=[pl.BlockSpec((None, t, FF_COLS), lambda j: (0, 0, j)), pl.BlockSpec((None, t, FF_COLS), lambda j: (1, 0, j)),
                  pl.BlockSpec((8, FF_COLS), lambda j: (0, j)), pl.BlockSpec((1, FF_COLS), lambda j: (0, j)),
                  pl.BlockSpec((t, d), lambda j: (0, 0), pipeline_mode=pl.Buffered(1)),
                  pl.BlockSpec((FF_COLS, d), lambda j: (j, 0))],
        out_specs=[pl.BlockSpec((2, t, FF_COLS), lambda j: (0, 0, j)), pl.BlockSpec((8, FF_COLS), lambda j: (0, j)),
                   pl.BlockSpec((1, FF_COLS), lambda j: (0, j))],
        out_shape=[jax.ShapeDtypeStruct((2, t, f), BF16), jax.ShapeDtypeStruct((8, f), F32),
                   jax.ShapeDtypeStruct((1, f), F32)],
        scratch_shapes=[pltpu.VMEM((t, FF_COLS), F32)],
        compiler_params=_params("parallel"),
    )(up, up, conv_w8, conv_b, dh2, w_down)


def _exchange(arrays, kinds, name, after=None):
    n = len(arrays)
    npeer = N_DEV - 1
    n_in = n + int(after is not None)

    def body(*refs):
        ins, outs = refs[:n], refs[n_in:n_in + n]
        send_sems, recv_sems, local_sems = refs[n_in + n:]
        x, y, c = lax.axis_index("x"), lax.axis_index("y"), lax.axis_index("c")
        me = 4 * x + 2 * y + c
        copies, locals_ = [], []
        for a in range(n):
            gather = kinds[a] == "gather"
            own = pltpu.make_async_copy(ins[a] if gather else ins[a].at[me], outs[a].at[me], local_sems.at[a])
            own.start()
            locals_.append(own)
            for d in range(1, N_DEV):
                px = 1 - x if d & 4 else x
                py = 1 - y if d & 2 else y
                pc = 1 - c if d & 1 else c
                src = ins[a] if gather else ins[a].at[4 * px + 2 * py + pc]
                cp = pltpu.make_async_remote_copy(
                    src_ref=src, dst_ref=outs[a].at[me],
                    send_sem=send_sems.at[a * npeer + d - 1], recv_sem=recv_sems.at[a * npeer + d - 1],
                    device_id=(px, py, pc), device_id_type=pl.DeviceIdType.MESH)
                cp.start()
                copies.append(cp)
        for cp in copies:
            cp.wait_recv()
        for cp in copies:
            cp.wait_send()
        for own in locals_:
            own.wait()

    out_shape = [jax.ShapeDtypeStruct((N_DEV,) + (a.shape if k == "gather" else a.shape[1:]), a.dtype)
                 for a, k in zip(arrays, kinds)]
    return _pcall(
        body, name=name,
        in_specs=[pl.BlockSpec(memory_space=pl.ANY)] * n_in,
        out_specs=[pl.BlockSpec(memory_space=pl.ANY)] * n,
        out_shape=out_shape,
        scratch_shapes=[pltpu.SemaphoreType.DMA((n * npeer,)), pltpu.SemaphoreType.DMA((n * npeer,)),
                        pltpu.SemaphoreType.DMA((n,))],
        compiler_params=pltpu.CompilerParams(has_side_effects=True),
    )(*arrays, *([] if after is None else [after]))


def _peer_copies(srcs, lands, kinds, send_sems, recv_sems):
    x, y, c = lax.axis_index("x"), lax.axis_index("y"), lax.axis_index("c")
    me = 4 * x + 2 * y + c
    copies = []
    for a in range(len(srcs)):
        for d in range(1, N_DEV):
            px = 1 - x if d & 4 else x
            py = 1 - y if d & 2 else y
            pc = 1 - c if d & 1 else c
            k = a * (N_DEV - 1) + d - 1
            copies.append(pltpu.make_async_remote_copy(
                src_ref=srcs[a] if kinds[a] == "gather" else srcs[a].at[4 * px + 2 * py + pc], dst_ref=lands[a].at[me],
                send_sem=send_sems.at[k], recv_sem=recv_sems.at[k],
                device_id=(px, py, pc), device_id_type=pl.DeviceIdType.MESH))
    return copies


def _exchange_start(arrays, kinds, name, after=None):
    n = len(arrays)
    nsem = n * (N_DEV - 1)
    hbm = pl.BlockSpec(memory_space=pltpu.HBM)
    sem = pl.BlockSpec(memory_space=pltpu.SEMAPHORE)
    land_shapes = [(N_DEV,) + (a.shape if k == "gather" else a.shape[1:]) for a, k in zip(arrays, kinds)]

    n_in = 2 * n + int(after is not None)

    def body(*refs):
        srcs, lands = refs[:n], refs[n:2 * n]
        send_sems, recv_sems = refs[n_in], refs[n_in + 1]
        token = refs[-1]
        for cp in _peer_copies(srcs, lands, kinds, send_sems, recv_sems):
            cp.start()
        token[...] = jnp.zeros_like(token)

    operands = [pltpu.with_memory_space_constraint(a, pltpu.HBM) for a in arrays]
    operands += [pltpu.with_memory_space_constraint(lax.empty(s, a.dtype), pltpu.HBM) for s, a in zip(land_shapes, arrays)]
    operands += [] if after is None else [after]
    out = _pcall(
        body, name=name,
        in_specs=[hbm] * (2 * n) + ([] if after is None else [pl.BlockSpec(memory_space=pl.ANY)]),
        out_specs=[sem, sem] + [hbm] * (2 * n) + [pl.BlockSpec(memory_space=pltpu.VMEM)],
        out_shape=[pltpu.SemaphoreType.DMA((nsem,)), pltpu.SemaphoreType.DMA((nsem,))]
        + [pltpu.HBM(a.shape, a.dtype) for a in arrays]
        + [pltpu.HBM(s, a.dtype) for s, a in zip(land_shapes, arrays)]
        + [jax.ShapeDtypeStruct((8, LANE), F32)],
        input_output_aliases={k: 2 + k for k in range(2 * n)},
        compiler_params=pltpu.CompilerParams(has_side_effects=pltpu.SideEffectType.DATAFLOW_SIDE_EFFECTING),
    )(*operands)
    return out[0], out[1], list(out[2:2 + n]), list(out[2 + n:2 + 2 * n]), out[-1]


def _exchange_wait(started, kinds, after, name, fill_own=True):
    send_sems, recv_sems, srcs, lands, _ = started
    n = len(srcs)
    hbm = pl.BlockSpec(memory_space=pltpu.HBM)
    sem = pl.BlockSpec(memory_space=pltpu.SEMAPHORE)

    def body(*refs):
        src_refs, land_refs = refs[:n], refs[n:2 * n]
        copies = _peer_copies(src_refs, land_refs, kinds, refs[2 * n], refs[2 * n + 1])
        for cp in copies:
            cp.wait_send()
        for cp in copies:
            cp.wait_recv()

    out = _pcall(
        body, name=name,
        in_specs=[hbm] * (2 * n) + [sem, sem, pl.BlockSpec(memory_space=pl.ANY)],
        out_specs=[hbm] * (2 * n),
        out_shape=[pltpu.HBM(a.shape, a.dtype) for a in srcs + lands],
        input_output_aliases={k: k for k in range(2 * n)},
        compiler_params=pltpu.CompilerParams(has_side_effects=pltpu.SideEffectType.DATAFLOW_SIDE_EFFECTING),
    )(*srcs, *lands, send_sems, recv_sems, after)
    if not fill_own:
        return list(out[:n]), list(out[n:])
    me = 4 * lax.axis_index("x") + 2 * lax.axis_index("y") + lax.axis_index("c")
    filled = []
    for src, land, kind in zip(out[:n], out[n:], kinds):
        own = src if kind == "gather" else lax.dynamic_index_in_dim(src, me, axis=0, keepdims=False)
        filled.append(lax.dynamic_update_slice(land, own[None], (me,) + (0,) * own.ndim))
    return filled


def _sum_slots(slots, name, rows_tile):
    nd, r, c = slots.shape

    def body(s_ref, o_ref):
        acc = s_ref[0].astype(F32)
        for p in range(1, nd):
            acc = acc + s_ref[p].astype(F32)
        o_ref[...] = acc

    return _pcall(
        body, name=name, grid=(r // rows_tile,),
        in_specs=[pl.BlockSpec((nd, rows_tile, c), lambda i: (0, i, 0))],
        out_specs=pl.BlockSpec((rows_tile, c), lambda i: (i, 0)),
        out_shape=jax.ShapeDtypeStruct((r, c), F32),
        compiler_params=_params("parallel"),
    )(slots)


def _sum_slots_small(slot_arrays, own_arrays, name):
    n = len(slot_arrays)

    def body(*refs):
        me = 4 * lax.axis_index("x") + 2 * lax.axis_index("y") + lax.axis_index("c")
        for s_ref, own_ref, o_ref in zip(refs[:n], refs[n:2 * n], refs[2 * n:]):
            acc = jnp.where(me == 0, own_ref[...], s_ref[0])
            for p in range(1, s_ref.shape[0]):
                acc = acc + jnp.where(me == p, own_ref[...], s_ref[p])
            o_ref[...] = acc

    return _pcall(body, name=name, out_shape=[jax.ShapeDtypeStruct(a.shape[1:], F32) for a in slot_arrays])(
        *slot_arrays, *own_arrays)


def _adamw_update(w_ref, g_ref, m_ref, v_ref, d_ref, nm_ref, nv_ref):
    gr = g_ref[...]
    nm = ADAM_B1 * m_ref[...] + (1.0 - ADAM_B1) * gr
    nv = ADAM_B2 * v_ref[...] + (1.0 - ADAM_B2) * (gr * gr)
    m_hat = nm / (1.0 - ADAM_B1 ** ADAM_STEP)
    v_hat = nv / (1.0 - ADAM_B2 ** ADAM_STEP)
    d_ref[...] = -ADAM_LR * (m_hat / (jnp.sqrt(v_hat) + ADAM_EPS) + ADAM_WD * w_ref[...])
    nm_ref[...] = nm
    nv_ref[...] = nv


def _adamw_small(ws, gs, ms, vs, name):
    n = len(ws)

    def body(*refs):
        ins, outs = refs[:4 * n], refs[4 * n:]
        for k in range(n):
            _adamw_update(ins[k], ins[n + k], ins[2 * n + k], ins[3 * n + k], outs[k], outs[n + k], outs[2 * n + k])

    shapes = [jax.ShapeDtypeStruct(w.shape, F32) for w in ws]
    out = _pcall(body, name=name, out_shape=shapes * 3)(*ws, *gs, *ms, *vs)
    return list(out[:n]), list(out[n:2 * n]), list(out[2 * n:])


def _adamw(w, g, m, v, name, rows_tile):
    r, c = w.shape
    body = lambda *refs: _adamw_update(*refs)
    spec = pl.BlockSpec((rows_tile, c), lambda i: (i, 0))
    shp = jax.ShapeDtypeStruct((r, c), F32)
    return _pcall(
        body, name=name, grid=(r // rows_tile,), in_specs=[spec] * 4, out_specs=[spec] * 3, out_shape=[shp] * 3,
        compiler_params=_params("parallel"),
    )(w, g, m, v)


F0 = 2 * RET_QK + 2 * RET_V


def _to_internal_rows(w_t):
    cols = w_t.shape[1]
    fox = w_t[F0:F0 + 3 * FOX_W].reshape(3, FOX_PAIRS, LANE, cols).transpose(1, 0, 2, 3).reshape(3 * FOX_W, cols)
    tail = jnp.zeros((IN_PAD - IN_WIDTH, cols), w_t.dtype)
    return jnp.concatenate([w_t[:F0], fox, w_t[F0 + 3 * FOX_W:], tail], axis=0)


def _from_internal_rows(g_t):
    cols = g_t.shape[1]
    fox = g_t[F0:F0 + 3 * FOX_W].reshape(FOX_PAIRS, 3, LANE, cols).transpose(1, 0, 2, 3).reshape(3 * FOX_W, cols)
    return jnp.concatenate([g_t[:F0], fox, g_t[F0 + 3 * FOX_W:F0 + 3 * FOX_W + FOX_HEADS]], axis=0)


def _local_step(x, target, meta, attn_g, fox_b, ret_g, ffn_g, conv_w8, conv_b, final_g,
                first_weight, late_weights, ffn_grads_ready, out_grad_ready, in_grad_ready):
    seq, d = x.shape
    t = seq + PREFIX
    tm = TOK_TILE
    nq = t // tm
    fox_b128 = jnp.pad(fox_b, ((0, 0), (0, LANE - FOX_HEADS)))

    h0, n1 = _prep_norm(x, meta, attn_g, "prep_norm")
    w_in_t = first_weight(n1)
    proj = _mm_simple(n1, w_in_t, mode="nt", tm=tm, tn=IN_PAD, tk=d, out_dtype=F32, name="mm_in")
    cos, sin = _rope_tables(t)
    o_pre, mixed, states = _ret_fwd(proj, cos, sin, ret_g, "ret_fwd")
    c = _forget_cumsum(proj, fox_b128, "forget_cumsum")
    qa, ka, va, qt, vt = _fox_prep(proj, c, "fox_prep")
    by_block = lambda a: a.reshape(FOX_HEADS, nq, tm, LANE)
    mixed, o_fox, lse = _fox_fwd(qt, by_block(ka), vt, mixed, "fox_fwd")
    w_out, w_up_t, w_down = late_weights(o_fox)
    tile = pl.BlockSpec((tm, d), lambda i: (i, 0))
    row_vec = pl.BlockSpec((1, d), lambda i: (0, 0))
    resident = lambda shape: pl.BlockSpec(shape, lambda i: (0,) * len(shape), pipeline_mode=pl.Buffered(1))
    acts = lambda dtype: jax.ShapeDtypeStruct((t, d), dtype)
    vec = jax.ShapeDtypeStruct((1, d), F32)

    def residual_and_norm(i, acc, ins, outs):
        h = acc + ins[0][...]
        outs[0][...] = h
        outs[1][...] = (h * lax.rsqrt(jnp.mean(h * h, axis=-1, keepdims=True) + EPS) * ins[1][...]).astype(BF16)

    h1, n2 = _matmul_rows([mixed], [tile], [w_out], [resident((d, d))], [h0, ffn_g], [tile, row_vec],
                          [tile, tile], [acts(F32), acts(BF16)], residual_and_norm, mode="nn", steps=nq, name="mm_out_norm")
    nf = D_FF // 1408
    up, g = _up_conv_fwd(n2, w_up_t, conv_w8, conv_b, "up_conv_fwd")

    def residual_loss_bwd(i, acc, ins, outs):
        loss_ref, dh_ref, dhb_ref, gg_ref = outs
        part, dh, gg = _loss_tile(i, acc + ins[0][...], jnp.concatenate([ins[1][...], ins[2][...], ins[3][...]], axis=0),
                                  ins[4][...])
        _accumulate(loss_ref, i, jnp.broadcast_to(part, loss_ref.shape))
        dh_ref[...] = dh
        dhb_ref[...] = dh.astype(BF16)
        _accumulate(gg_ref, i, gg)

    loss_tile, dh2, dh2_b, g_final = _matmul_rows(
        [g], [pl.BlockSpec((tm, D_FF), lambda i: (i, 0))], [w_down], [resident((D_FF, d))],
        [h1, target, target, target, final_g], [tile] + _shifted_row_specs(d) + [row_vec],
        [pl.BlockSpec((8, LANE), lambda i: (0, 0)), tile, tile, row_vec],
        [jax.ShapeDtypeStruct((8, LANE), F32), acts(F32), acts(BF16), vec], residual_loss_bwd,
        mode="nn", steps=nq, name="mm_down_loss")

    tkw = 2112 if t % 2112 == 0 else tm
    gw_down = _mm_simple(g, dh2_b, mode="tn", tm=1408, tn=d, tk=tkw, out_dtype=BF16, name="mm_gw_down")
    dup, g_conv_w8, g_conv_b = _dg_conv_bwd(up, conv_w8, conv_b, dh2_b, w_down, "dg_conv_bwd")

    half = lambda p: pl.BlockSpec((None, tm, D_FF), lambda i: (p, i, 0))
    half_w = lambda p: pl.BlockSpec((None, D_FF, d), lambda i: (p, 0, 0), pipeline_mode=pl.Buffered(1))
    gw_up_t = _matmul(
        dup, n2, mode="tn", grid=(2 * nf, 1, t // tkw),
        a_spec=pl.BlockSpec((None, tkw, 1408), lambda i, j, k: (i // nf, k, i % nf)),
        b_spec=pl.BlockSpec((tkw, d), lambda i, j, k: (k, 0)),
        o_spec=pl.BlockSpec((1408, d), lambda i, j, k: (i, 0)),
        out_shape=jax.ShapeDtypeStruct((2 * D_FF, d), BF16), name="mm_gw_up")
    def norm_bwd_and_mixer_grad(i, acc, ins, outs):
        dh, gg = _rms_bwd_tile(acc, ins[0][...], ins[1][...], ins[2][...])
        outs[0][...] = dh
        _accumulate(outs[1], i, gg)
        outs[2][...] = _dot(dh.astype(BF16), ins[3][...], NT)

    dh1, g_ffn, dmixed = _matmul_rows(
        [dup, dup], [half(0), half(1)], [w_up_t, w_up_t], [half_w(0), half_w(1)],
        [h1, ffn_g, dh2, w_out], [tile, row_vec, tile, resident((d, d))], [tile, row_vec, tile],
        [acts(F32), vec, acts(F32)], norm_bwd_and_mixer_grad,
        mode="nn", steps=nq, name="mm_dn2_norm_bwd", after=ffn_grads_ready(gw_down, gw_up_t))
    gw_out = _mm_simple(mixed, dh1, mode="tn", tm=d, tn=d, tk=tkw, out_dtype=BF16, name="mm_gw_out")
    dproj, g_ret = _ret_bwd(proj, cos, sin, ret_g + out_grad_ready(gw_out), dmixed, o_pre, states, "ret_bwd")
    qab, doa = _fox_prep_bwd(dmixed, o_fox, lse, qa, "fox_prep_bwd")
    dproj, drs, dcs = _fox_bwd(by_block(qab), by_block(doa), by_block(ka), by_block(va), dproj, "fox_bwd")
    dproj, g_fox_b = _forget_cumsum_bwd(proj, fox_b128, drs, dcs, dproj, "forget_cumsum_bwd")
    gw_in_t = _mm_simple(dproj, n1, mode="tn", tm=640, tn=d, tk=tkw, out_dtype=BF16, name="mm_gw_in")
    sent = in_grad_ready(gw_in_t)
    def input_grads(i, acc, ins, outs):
        gx_ref, gmeta_ref, gg_ref, buf_ref, sems = outs
        dh, gg = _rms_bwd_tile(acc, ins[0][...], ins[1][...], ins[2][...])
        _accumulate(gg_ref, i, gg)
        slot = i % 2

        def first_copy():
            return pltpu.make_async_copy(buf_ref.at[0, pl.ds(PREFIX, tm - PREFIX)], gx_ref.at[pl.ds(0, tm - PREFIX)],
                                         sems.at[0])

        def tile_copy(tile, buf_slot):
            rows = pl.ds(pl.multiple_of(tile * tm - PREFIX, PREFIX), tm)
            return pltpu.make_async_copy(buf_ref.at[buf_slot], gx_ref.at[rows], sems.at[buf_slot])

        @pl.when(i == 1)
        def _():
            first_copy().wait()

        @pl.when(i >= 2)
        def _():
            tile_copy(i - 1, 1 - slot).wait()

        buf_ref[slot] = dh

        @pl.when(i == 0)
        def _():
            gmeta_ref[...] = dh[N_PAD:PREFIX, :]
            first_copy().start()

        @pl.when(i > 0)
        def _():
            tile_copy(i, slot).start()

        @pl.when(i == nq - 1)
        def _():
            tile_copy(i, slot).wait()

    grad_x, g_meta, g_attn = _matmul_rows(
        [dproj], [pl.BlockSpec((tm, IN_PAD), lambda i: (i, 0))], [w_in_t], [resident((IN_PAD, d))],
        [h0, attn_g, dh1], [tile, row_vec, tile],
        [pl.BlockSpec(memory_space=pl.ANY), pl.BlockSpec((N_META, d), lambda i: (0, 0)), row_vec],
        [jax.ShapeDtypeStruct((seq, d), F32), jax.ShapeDtypeStruct((N_META, d), F32), vec], input_grads,
        mode="nn", steps=nq, name="mm_dn1_norm_bwd", after=sent,
        scratch=[pltpu.VMEM((2, tm, d), F32), pltpu.SemaphoreType.DMA((2,))])

    grads = dict(meta=g_meta, attn_g=g_attn, fox_b=g_fox_b, ret_g=g_ret,
                 ffn_g=g_ffn, conv_w=g_conv_w8, conv_b=g_conv_b, final_g=g_final)
    return loss_tile, grad_x, grads


def kernel(x, meta_tokens, attn_norm_g, w_in, fox_forget_b, ret_norm_g, w_out, ffn_norm_g, w_up, conv_w, conv_b, w_down, final_norm_g, loss_target, m_meta_tokens, m_attn_norm_g, m_w_in, m_fox_forget_b, m_ret_norm_g, m_w_out, m_ffn_norm_g, m_w_up, m_conv_w, m_conv_b, m_w_down, m_final_norm_g, v_meta_tokens, v_attn_norm_g, v_w_in, v_fox_forget_b, v_ret_norm_g, v_w_out, v_ffn_norm_g, v_w_up, v_conv_w, v_conv_b, v_w_down, v_final_norm_g):
    d = D_MODEL
    me = 4 * lax.axis_index("x") + 2 * lax.axis_index("y") + lax.axis_index("c")
    in_blk = IN_WIDTH // N_DEV
    in_blk_pad = 400
    up_blk = 2 * D_FF // N_DEV
    down_blk = D_FF // N_DEV
    cw_blk = D_FF // N_DEV

    w_in_loc = jnp.pad(w_in[0].T.astype(BF16), ((0, in_blk_pad - in_blk), (0, 0)))
    cw_loc = jnp.pad(conv_w[0], ((0, 5), (0, 384 - cw_blk)))
    g_meta, g_cw = _exchange([meta_tokens, cw_loc], ["gather"] * 2, "gather_small")
    first = _exchange_start([w_in_loc], ["gather"], "gather_in_start", after=g_meta)
    rest_loc = [(w_out[0] + first[-1][0:1, 0:1]).astype(BF16), w_up[0].T.astype(BF16), w_down[0].astype(BF16)]
    rest = _exchange_start(rest_loc, ["gather"] * 3, "gather_rest_start")
    meta_f = g_meta.transpose(1, 0, 2).reshape(N_META, d)
    conv_w8 = jnp.pad(g_cw[:, :3, :cw_blk].transpose(1, 0, 2).reshape(3, D_FF), ((0, 5), (0, 0)))
    pending = {}

    def first_weight(after):
        (g_in,) = _exchange_wait(first, ["gather"], after, "gather_in_wait")
        return _to_internal_rows(g_in[:, :in_blk].reshape(IN_WIDTH, d))

    def in_grad_ready(gw_in_t):
        blocks = _from_internal_rows(gw_in_t).reshape(N_DEV, in_blk, d)
        blocks = jnp.pad(blocks, ((0, 0), (0, in_blk_pad - in_blk), (0, 0)))
        pending["in"] = _exchange_start([blocks], ["scatter"], "grads_in_start")
        return pending["in"][-1][0:1, 0:1]

    def late_weights(after):
        g_out, g_up, g_down = _exchange_wait(rest, ["gather"] * 3, after, "gather_rest_wait")
        return g_out.reshape(d, d), g_up.reshape(2, D_FF, d), g_down.reshape(D_FF, d)

    def ffn_grads_ready(gw_down, gw_up_t):
        blocks = [gw_down.reshape(N_DEV, down_blk, d), gw_up_t.reshape(N_DEV, up_blk, d)]
        pending["ffn"] = _exchange_start(blocks, ["scatter"] * 2, "grads_ffn_start")
        return pending["ffn"][-1][0:1, 0:1]

    def out_grad_ready(gw_out):
        pending["out"] = _exchange_start([gw_out.reshape(N_DEV, d // N_DEV, d)], ["scatter"], "grads_out_start")
        return pending["out"][-1][0:1, 0:1]

    loss_tile, grad_x, gr = _local_step(
        x[0], loss_target[0], meta_f, attn_norm_g + rest[-1][0:1, 0:1], fox_forget_b, ret_norm_g, ffn_norm_g,
        conv_w8, conv_b, final_norm_g.reshape(1, d), first_weight, late_weights, ffn_grads_ready, out_grad_ready,
        in_grad_ready)

    small = [loss_tile, gr["attn_g"], gr["fox_b"], gr["ret_g"], gr["ffn_g"], gr["conv_b"], gr["final_g"],
             gr["meta"], gr["conv_w"]]
    small_kinds = ["gather"] * len(small)
    small_started = _exchange_start(small, small_kinds, "grads_small_start")

    r_down, r_up = _exchange_wait(pending["ffn"], ["scatter"] * 2, small_started[-1], "grads_ffn_wait")
    (r_out,) = _exchange_wait(pending["out"], ["scatter"], small_started[-1], "grads_out_wait")
    g_w_out = _sum_slots(r_out, "sum_w_out", d // N_DEV)
    g_w_up_t = _sum_slots(r_up, "sum_w_up", up_blk)
    g_w_down = _sum_slots(r_down, "sum_w_down", down_blk)
    as_t = lambda a: a[0].T
    from_t = lambda a: a.T[None]
    d_w_out, m_w_out_n, v_w_out_n = [a[None] for a in _adamw(w_out[0], g_w_out, m_w_out[0], v_w_out[0], "adamw_w_out", 128)]
    up_t = _adamw(as_t(w_up), g_w_up_t, as_t(m_w_up), as_t(v_w_up), "adamw_w_up", up_blk // 2)
    d_w_up, m_w_up_n, v_w_up_n = [from_t(a) for a in up_t]
    d_w_down, m_w_down_n, v_w_down_n = [a[None] for a in _adamw(w_down[0], g_w_down, m_w_down[0], v_w_down[0],
                                                                "adamw_w_down", down_blk)]

    own_small, r_small = _exchange_wait(small_started, small_kinds, up_t[0], "grads_small_wait", fill_own=False)
    (loss_all, g_attn, g_fox_b128, g_ret, g_ffn, g_conv_b, g_final, g_meta_full, g_cw_full) = _sum_slots_small(
        r_small, own_small, "sum_small")
    loss = loss_all[0, 0]
    g_fox_b = g_fox_b128[:, :FOX_HEADS]
    g_meta_loc = lax.dynamic_slice(g_meta_full, (0, me * (d // N_DEV)), (N_META, d // N_DEV))
    g_cw_loc = lax.dynamic_slice(g_cw_full, (0, me * cw_blk), (3, cw_blk))

    (r_in,) = _exchange_wait(pending["in"], ["scatter"], r_small[0], "grads_in_wait")
    g_w_in_t = _sum_slots(r_in, "sum_w_in", in_blk_pad)[:in_blk]
    d_w_in, m_w_in_n, v_w_in_n = [from_t(a) for a in _adamw(as_t(w_in), g_w_in_t, as_t(m_w_in), as_t(v_w_in),
                                                            "adamw_w_in", in_blk)]
    g_w_in, g_w_up = g_w_in_t.T, g_w_up_t.T
    row = lambda a: a.reshape(1, d)
    sm_grads = [g_meta_loc, g_attn, g_fox_b, g_ret, g_ffn, g_cw_loc, g_conv_b, g_final]
    sm_w = [meta_tokens, attn_norm_g, fox_forget_b, ret_norm_g, ffn_norm_g, conv_w[0], conv_b, row(final_norm_g)]
    sm_m = [m_meta_tokens, m_attn_norm_g, m_fox_forget_b, m_ret_norm_g, m_ffn_norm_g, m_conv_w[0], m_conv_b,
            row(m_final_norm_g)]
    sm_v = [v_meta_tokens, v_attn_norm_g, v_fox_forget_b, v_ret_norm_g, v_ffn_norm_g, v_conv_w[0], v_conv_b,
            row(v_final_norm_g)]
    dl, ml, vl = [lst[:7] + [lst[7].reshape(d)] for lst in _adamw_small(sm_w, sm_grads, sm_m, sm_v, "adamw_small")]

    def by_weight(meta_, attn_, w_in_, fox_, ret_, w_out_, ffn_, w_up_, cw_, cb_, w_down_, final_):
        return (meta_, attn_, w_in_, fox_, ret_, w_out_, ffn_, w_up_, cw_[None], cb_, w_down_, final_)

    grads_out = by_weight(g_meta_loc, g_attn, g_w_in[None], g_fox_b, g_ret, g_w_out[None], g_ffn, g_w_up[None], g_cw_loc,
                          g_conv_b, g_w_down[None], g_final.reshape(d))
    delta_out = by_weight(dl[0], dl[1], d_w_in, dl[2], dl[3], d_w_out, dl[4], d_w_up, dl[5], dl[6], d_w_down, dl[7])
    m_out = by_weight(ml[0], ml[1], m_w_in_n, ml[2], ml[3], m_w_out_n, ml[4], m_w_up_n, ml[5], ml[6], m_w_down_n, ml[7])
    v_out = by_weight(vl[0], vl[1], v_w_in_n, vl[2], vl[3], v_w_out_n, vl[4], v_w_up_n, vl[5], vl[6], v_w_down_n, vl[7])
    return (loss, grad_x[None]) + grads_out + delta_out + m_out + v_out
```

```python
import numpy as np
import jax
import jax.numpy as jnp
from jax import lax
from jax.experimental import pallas as pl
from jax.experimental.pallas import tpu as pltpu

F32 = jnp.float32
BF16 = jnp.bfloat16

D_MODEL = 1024
N_META = 16
N_PAD = 112
PREFIX = 128
RET_HEADS = 4
RET_DK = 64
RET_DV = 128
FOX_HEADS = 8
FOX_DH = 64
D_FF = 2816
ROPE_BASE = 10000.0
EPS = 1e-6
NEG = -1e30
RET_QK = RET_HEADS * RET_DK
RET_V = RET_HEADS * RET_DV
FOX_W = FOX_HEADS * FOX_DH
IN_WIDTH = 2 * RET_QK + 2 * RET_V + 3 * FOX_W + FOX_HEADS
IN_PAD = 3200
FF_COL_BLOCK = (IN_WIDTH - FOX_HEADS) // 128
QK_SCALE = 0.125

ADAM_LR = 0.001
ADAM_B1 = 0.9
ADAM_B2 = 0.999
ADAM_EPS = 1e-08
ADAM_WD = 0.01
ADAM_STEP = 10

N_DEV = 8
LANE = 128
ROW_TILE = 128
TOK_TILE = 384

NN = (((1,), (0,)), ((), ()))
NT = (((1,), (1,)), ((), ()))
TN = (((0,), (0,)), ((), ()))


def _pcall(body, **kw):
    return pl.pallas_call(body, **kw)


def _params(*sem):
    return pltpu.CompilerParams(dimension_semantics=sem)


def _dot(a, b, dims=NN):
    return lax.dot_general(a, b, dims, preferred_element_type=F32)


def _sigmoid(x):
    return 0.5 * jnp.tanh(0.5 * x) + 0.5


def _matmul(a, b, *, mode, grid, a_spec, b_spec, o_spec, out_shape, name, add=None, add_spec=None, after=None):
    dims = {"nn": NN, "nt": NT, "tn": TN}[mode]
    nk = grid[2]
    has_add = add is not None
    a_list, b_list = (list(a), list(b)) if isinstance(a, (list, tuple)) else ([a], [b])
    a_specs, b_specs = (list(a_spec), list(b_spec)) if isinstance(a_spec, (list, tuple)) else ([a_spec], [b_spec])
    nt = len(a_list)
    n_in = 2 * nt + int(has_add) + int(after is not None)

    def body(*refs):
        a_refs, b_refs = refs[:nt], refs[nt:2 * nt]
        add_ref = refs[2 * nt] if has_add else None
        o_ref = refs[n_in]
        part = _dot(a_refs[0][...].astype(BF16), b_refs[0][...].astype(BF16), dims)
        for ar, br in zip(a_refs[1:], b_refs[1:]):
            part = part + _dot(ar[...].astype(BF16), br[...].astype(BF16), dims)

        def finish(acc):
            if has_add:
                acc = acc + add_ref[...]
            o_ref[...] = acc.astype(o_ref.dtype)

        if nk == 1:
            finish(part)
        else:
            acc_ref = refs[-1]
            k = pl.program_id(2)

            @pl.when(k == 0)
            def _():
                acc_ref[...] = part

            @pl.when(k > 0)
            def _():
                acc_ref[...] += part

            @pl.when(k == nk - 1)
            def _():
                finish(acc_ref[...])

    in_specs = a_specs + b_specs + ([add_spec] if has_add else [])
    args = tuple(a_list) + tuple(b_list) + ((add,) if has_add else ())
    if after is not None:
        in_specs, args = in_specs + [pl.BlockSpec(memory_space=pl.ANY)], args + (after,)
    scratch = [] if nk == 1 else [pltpu.VMEM(tuple(d for d in o_spec.block_shape if d is not None), F32)]
    return _pcall(
        body, name=name, grid=grid, in_specs=in_specs, out_specs=o_spec, out_shape=out_shape,
        scratch_shapes=scratch, compiler_params=_params("parallel", "parallel", "arbitrary"),
    )(*args)


def _mm_simple(a, b, *, mode, tm, tn, tk, out_dtype, name, add=None, after=None):
    if mode == "tn":
        K, M = a.shape
    else:
        M, K = a.shape
    N = b.shape[0] if mode == "nt" else b.shape[1]
    grid = (M // tm, N // tn, K // tk)
    resident = dict(pipeline_mode=pl.Buffered(1)) if (tn == N and tk == K) else {}
    a_spec = pl.BlockSpec((tk, tm), lambda i, j, k: (k, i)) if mode == "tn" else pl.BlockSpec((tm, tk), lambda i, j, k: (i, k))
    b_spec = (pl.BlockSpec((tn, tk), lambda i, j, k: (j, k), **resident) if mode == "nt"
              else pl.BlockSpec((tk, tn), lambda i, j, k: (k, j), **resident))
    o_spec = pl.BlockSpec((tm, tn), lambda i, j, k: (i, j))
    return _matmul(a, b, mode=mode, grid=grid, a_spec=a_spec, b_spec=b_spec, o_spec=o_spec,
                   out_shape=jax.ShapeDtypeStruct((M, N), out_dtype), name=name, add=add,
                   add_spec=o_spec if add is not None else None, after=after)


def _matmul_rows(a_list, a_specs, b_list, b_specs, extras, extra_specs, out_specs, out_shape, epilogue, *,
                 mode, steps, name, after=None, scratch=()):
    dims = {"nn": NN, "nt": NT}[mode]
    nt, ne = len(a_list), len(extras)
    n_in = 2 * nt + ne + int(after is not None)

    def body(*refs):
        acc = _dot(refs[0][...].astype(BF16), refs[nt][...].astype(BF16), dims)
        for k in range(1, nt):
            acc = acc + _dot(refs[k][...].astype(BF16), refs[nt + k][...].astype(BF16), dims)
        epilogue(pl.program_id(0), acc, refs[2 * nt:2 * nt + ne], refs[n_in:])

    in_specs = list(a_specs) + list(b_specs) + list(extra_specs)
    args = tuple(a_list) + tuple(b_list) + tuple(extras)
    if after is not None:
        in_specs, args = in_specs + [pl.BlockSpec(memory_space=pl.ANY)], args + (after,)
    return _pcall(body, name=name, grid=(steps,), in_specs=in_specs, out_specs=out_specs, out_shape=out_shape,
                  scratch_shapes=list(scratch), compiler_params=_params("arbitrary"))(*args)


def _rms_bwd_tile(dy, x, gain, dres):
    r = lax.rsqrt(jnp.mean(x * x, axis=-1, keepdims=True) + EPS)
    xhat = x * r
    u = dy * gain
    return dres + r * (u - xhat * jnp.mean(u * xhat, axis=-1, keepdims=True)), jnp.sum(dy * xhat, axis=0, keepdims=True)


def _loss_tile(i, x, tgt, gain):
    d = x.shape[-1]
    r = lax.rsqrt(jnp.mean(x * x, axis=-1, keepdims=True) + EPS)
    xhat = x * r
    counted = (i * TOK_TILE + lax.broadcasted_iota(jnp.int32, (TOK_TILE, 1), 0)) >= PREFIX
    err = jnp.where(counted, xhat * gain - tgt, 0.0)
    dy = err * (1.0 / d)
    u = dy * gain
    dh = r * (u - xhat * jnp.mean(u * xhat, axis=-1, keepdims=True))
    return 0.5 * jnp.sum(jnp.mean(err * err, axis=-1, keepdims=True)), dh, jnp.sum(dy * xhat, axis=0, keepdims=True)


def _accumulate(ref, i, part):
    @pl.when(i == 0)
    def _():
        ref[...] = part

    @pl.when(i > 0)
    def _():
        ref[...] += part


def _prep_norm(x, meta, gain, name):
    seq, d = x.shape
    t = seq + PREFIX

    def body(xa_ref, xb_ref, xc_ref, meta_ref, g_ref, h_ref, n_ref):
        i = pl.program_id(0)

        @pl.when(i == 0)
        def _():
            h_ref[0:N_PAD, :] = jnp.zeros((N_PAD, d), F32)
            h_ref[N_PAD:ROW_TILE, :] = meta_ref[...]

        @pl.when(i > 0)
        def _():
            h_ref[0:ROW_TILE, :] = xa_ref[...]

        h_ref[ROW_TILE:2 * ROW_TILE, :] = xb_ref[...]
        h_ref[2 * ROW_TILE:3 * ROW_TILE, :] = xc_ref[...]
        h = h_ref[...]
        r = lax.rsqrt(jnp.mean(h * h, axis=-1, keepdims=True) + EPS)
        n_ref[...] = (h * r * g_ref[...]).astype(BF16)

    return _pcall(
        body, name=name, grid=(t // TOK_TILE,),
        in_specs=_shifted_row_specs(d) + [pl.BlockSpec((N_META, d), lambda i: (0, 0)), pl.BlockSpec((1, d), lambda i: (0, 0))],
        out_specs=[pl.BlockSpec((TOK_TILE, d), lambda i: (i, 0)), pl.BlockSpec((TOK_TILE, d), lambda i: (i, 0))],
        out_shape=[jax.ShapeDtypeStruct((t, d), F32), jax.ShapeDtypeStruct((t, d), BF16)],
        compiler_params=_params("parallel"),
    )(x, x, x, meta, gain)


def _shifted_row_specs(d):
    blocks_per_tile = TOK_TILE // ROW_TILE
    return [pl.BlockSpec((ROW_TILE, d), lambda i, r=r: (jnp.maximum(blocks_per_tile * i + r, 0), 0)) for r in (-1, 0, 1)]


def _ret_consts(bk):
    gam = 1.0 - 2.0 ** (-5.0 - np.arange(RET_HEADS))
    n = np.arange(bk)
    same_or_earlier_chunk = (n[None, :] // 64) <= (n[:, None] // 64)
    w = gam[:, None, None] ** np.abs(n[:, None] - n[None, :])[None] * same_or_earlier_chunk[None]
    wq = gam[:, None] ** (n[None, :] + 1.0)
    wk = gam[:, None] ** (bk - 1.0 - n[None, :])
    mask = (np.arange(RET_QK)[None, :] // RET_DK) == np.arange(RET_HEADS)[:, None]
    return (jnp.asarray(w, F32), jnp.asarray(wq[:, :, None], F32), jnp.asarray(wk[:, :, None], F32),
            jnp.asarray(mask[:, None, :], F32), [float(g ** bk) for g in gam])


def _rope_tables(t):
    half = RET_DK // 2
    inv = 1.0 / (ROPE_BASE ** (jnp.arange(half, dtype=F32) / half))
    ang = jnp.arange(t).astype(F32)[:, None] * inv[None, :]
    cos, sin = jnp.cos(ang), jnp.sin(ang)
    return (jnp.tile(jnp.concatenate([cos, cos], axis=1), (1, RET_HEADS)),
            jnp.tile(jnp.concatenate([-sin, sin], axis=1), (1, RET_HEADS)))


def _swap_halves(x):
    outs = []
    for s in range(x.shape[1] // LANE):
        xs = x[:, LANE * s:LANE * (s + 1)]
        lane = lax.broadcasted_iota(jnp.int32, xs.shape, 1)
        outs.append(jnp.where((lane & 32) == 0, pltpu.roll(xs, LANE - 32, axis=1), pltpu.roll(xs, 32, axis=1)))
    return outs[0] if len(outs) == 1 else jnp.concatenate(outs, axis=1)


def _rope(x, cos, sin_signed):
    return x * cos + _swap_halves(x) * sin_signed


def _rope_t(dx, cos, sin_signed):
    return dx * cos + _swap_halves(dx * sin_signed)


def _ret_fwd(proj, cos, sin, gain, name):
    t = proj.shape[0]
    bk = TOK_TILE
    nb = t // bk
    w, wq, wk, mask, g_blk = _ret_consts(bk)

    def body(q_ref, k_ref, v_ref, rg_ref, cos_ref, sin_ref, w_ref, wq_ref, wk_ref, mask_ref, gain_ref,
             opre_ref, og_ref, st_ref, r_ref):
        i = pl.program_id(0)

        @pl.when(i == 0)
        def _():
            r_ref[...] = jnp.zeros_like(r_ref)

        c, s = cos_ref[...], sin_ref[...]
        valid = ((i * bk + lax.broadcasted_iota(jnp.int32, (bk, 1), 0)) >= N_PAD).astype(F32)
        qr = _rope(q_ref[...], c, s)
        kr = _rope(k_ref[...], c, s) * QK_SCALE * valid
        kb = kr.astype(BF16)
        for h in range(RET_HEADS):
            hm = mask_ref[h]
            cols = slice(RET_DV * h, RET_DV * (h + 1))
            vh = v_ref[:, cols].astype(BF16)
            r_prev = r_ref[h]
            st_ref[0, h] = r_prev
            sm = _dot((qr * hm).astype(BF16), kb, NT) * w_ref[h]
            o = _dot(sm.astype(BF16), vh) + _dot((qr * (hm * wq_ref[h])).astype(BF16), r_prev.astype(BF16))
            r_ref[h] = g_blk[h] * r_prev + _dot((kr * wk_ref[h]).astype(BF16), vh, TN)
            opre_ref[:, cols] = o
            rstd = lax.rsqrt(jnp.mean(o * o, axis=-1, keepdims=True) + EPS)
            rg = rg_ref[:, cols]
            og_ref[:, cols] = (o * rstd * gain_ref[:, cols] * (rg * _sigmoid(rg))).astype(BF16)

    full = lambda shape: pl.BlockSpec(shape, lambda i: (0,) * len(shape))
    return _pcall(
        body, name=name, grid=(nb,),
        in_specs=[pl.BlockSpec((bk, RET_QK), lambda i: (i, 0)), pl.BlockSpec((bk, RET_QK), lambda i: (i, 1)),
                  pl.BlockSpec((bk, RET_V), lambda i: (i, 1)), pl.BlockSpec((bk, RET_V), lambda i: (i, 2)),
                  pl.BlockSpec((bk, RET_QK), lambda i: (i, 0)), pl.BlockSpec((bk, RET_QK), lambda i: (i, 0)),
                  full((RET_HEADS, bk, bk)), full((RET_HEADS, bk, 1)), full((RET_HEADS, bk, 1)),
                  full((RET_HEADS, 1, RET_QK)), full((1, RET_V))],
        out_specs=[pl.BlockSpec((bk, RET_V), lambda i: (i, 0)), pl.BlockSpec((bk, RET_V), lambda i: (i, 0)),
                   pl.BlockSpec((1, RET_HEADS, RET_QK, RET_DV), lambda i: (i, 0, 0, 0))],
        out_shape=[jax.ShapeDtypeStruct((t, RET_V), F32), jax.ShapeDtypeStruct((t, RET_V + FOX_W), BF16),
                   jax.ShapeDtypeStruct((nb, RET_HEADS, RET_QK, RET_DV), F32)],
        scratch_shapes=[pltpu.VMEM((RET_HEADS, RET_QK, RET_DV), F32)],
        compiler_params=_params("arbitrary"),
    )(proj, proj, proj, proj, cos, sin, w, wq, wk, mask, gain)


def _ret_bwd(proj, cos, sin, gain, dmixed, opre, states, name):
    t = proj.shape[0]
    bk = TOK_TILE
    nb = t // bk
    w, wq, wk, mask, g_blk = _ret_consts(bk)
    v0, g0 = 2 * RET_QK, 2 * RET_QK + RET_V

    def body(q_ref, k_ref, v_ref, rg_ref, cos_ref, sin_ref, w_ref, wq_ref, wk_ref, mask_ref, gain_ref,
             dog_ref, opre_ref, st_ref, dp_ref, gg_ref, dr_ref):
        step = pl.program_id(0)
        i = nb - 1 - step

        @pl.when(step == 0)
        def _():
            dr_ref[...] = jnp.zeros_like(dr_ref)
            gg_ref[...] = jnp.zeros_like(gg_ref)

        c, s = cos_ref[...], sin_ref[...]
        valid = ((i * bk + lax.broadcasted_iota(jnp.int32, (bk, 1), 0)) >= N_PAD).astype(F32)
        qr = _rope(q_ref[...], c, s)
        kr = _rope(k_ref[...], c, s) * QK_SCALE * valid
        kb = kr.astype(BF16)
        dqr = jnp.zeros((bk, RET_QK), F32)
        dkr = jnp.zeros((bk, RET_QK), F32)
        for h in range(RET_HEADS):
            hm = mask_ref[h]
            cols = slice(RET_DV * h, RET_DV * (h + 1))
            vh = v_ref[:, cols].astype(BF16)
            o = opre_ref[:, cols]
            rstd = lax.rsqrt(jnp.mean(o * o, axis=-1, keepdims=True) + EPS)
            xhat = o * rstd
            rg = rg_ref[:, cols]
            sg = _sigmoid(rg)
            gate = rg * sg
            gn = gain_ref[:, cols]
            dog = dog_ref[:, cols]
            dp_ref[:, g0 + RET_DV * h:g0 + RET_DV * (h + 1)] = (
                dog * xhat * gn * (sg * (1.0 + rg * (1.0 - sg)))).astype(BF16)
            gg_ref[:, cols] += jnp.sum(dog * xhat * gate, axis=0, keepdims=True)
            dxh = dog * gn * gate
            do = (rstd * (dxh - xhat * jnp.mean(dxh * xhat, axis=-1, keepdims=True))).astype(BF16)
            qm = (qr * hm).astype(BF16)
            qw = (qr * (hm * wq_ref[h])).astype(BF16)
            kw = (kr * wk_ref[h]).astype(BF16)
            wh = w_ref[h]
            sm = (_dot(qm, kb, NT) * wh).astype(BF16)
            ds = (_dot(do, vh, NT) * wh).astype(BF16)
            dr = dr_ref[h]
            drb = dr.astype(BF16)
            dp_ref[:, v0 + RET_DV * h:v0 + RET_DV * (h + 1)] = (_dot(sm, do, TN) + _dot(kw, drb)).astype(BF16)
            dqr = dqr + _dot(ds, kb) * hm + _dot(do, st_ref[0, h].astype(BF16), NT) * (hm * wq_ref[h])
            dkr = dkr + _dot(ds, qm, TN) + _dot(vh, drb, NT) * wk_ref[h]
            dr_ref[h] = g_blk[h] * dr + _dot(qw, do, TN)
        dp_ref[:, 0:RET_QK] = _rope_t(dqr, c, s).astype(BF16)
        dp_ref[:, RET_QK:2 * RET_QK] = _rope_t(dkr * (QK_SCALE * valid), c, s).astype(BF16)

    full = lambda shape: pl.BlockSpec(shape, lambda i: (0,) * len(shape))
    rev = lambda col: (lambda i: (nb - 1 - i, col))
    return _pcall(
        body, name=name, grid=(nb,),
        in_specs=[pl.BlockSpec((bk, RET_QK), rev(0)), pl.BlockSpec((bk, RET_QK), rev(1)),
                  pl.BlockSpec((bk, RET_V), rev(1)), pl.BlockSpec((bk, RET_V), rev(2)),
                  pl.BlockSpec((bk, RET_QK), rev(0)), pl.BlockSpec((bk, RET_QK), rev(0)),
                  full((RET_HEADS, bk, bk)), full((RET_HEADS, bk, 1)), full((RET_HEADS, bk, 1)),
                  full((RET_HEADS, 1, RET_QK)), full((1, RET_V)),
                  pl.BlockSpec((bk, RET_V), rev(0)), pl.BlockSpec((bk, RET_V), rev(0)),
                  pl.BlockSpec((1, RET_HEADS, RET_QK, RET_DV), lambda i: (nb - 1 - i, 0, 0, 0))],
        out_specs=[pl.BlockSpec((bk, g0 + RET_V), rev(0)), pl.BlockSpec((1, RET_V), lambda i: (0, 0))],
        out_shape=[jax.ShapeDtypeStruct((t, IN_PAD), BF16), jax.ShapeDtypeStruct((1, RET_V), F32)],
        scratch_shapes=[pltpu.VMEM((RET_HEADS, RET_QK, RET_DV), F32)],
        compiler_params=_params("arbitrary"),
    )(proj, proj, proj, proj, cos, sin, w, wq, wk, mask, gain, dmixed, opre, states)


def _forget_cumsum(proj, bias, name):
    t = proj.shape[0]
    rt = TOK_TILE
    nb = t // rt
    tril = jnp.asarray(np.tril(np.ones((rt, rt))), F32)

    def body(z_ref, b_ref, tril_ref, c_ref, carry_ref):
        i = pl.program_id(0)

        @pl.when(i == 0)
        def _():
            carry_ref[...] = jnp.zeros_like(carry_ref)

        z = z_ref[...] + b_ref[...]
        logf = jnp.minimum(z, 0.0) - jnp.log(1.0 + jnp.exp(-jnp.abs(z)))
        c = lax.dot_general(tril_ref[...], logf, NN, precision=lax.Precision.HIGHEST,
                            preferred_element_type=F32) + carry_ref[...]
        c_ref[...] = c
        carry_ref[...] = c[rt - 1:rt, :]

    return _pcall(
        body, name=name, grid=(nb,),
        in_specs=[pl.BlockSpec((rt, LANE), lambda i: (i, FF_COL_BLOCK)), pl.BlockSpec((1, LANE), lambda i: (0, 0)),
                  pl.BlockSpec((rt, rt), lambda i: (0, 0))],
        out_specs=pl.BlockSpec((rt, LANE), lambda i: (i, 0)),
        out_shape=jax.ShapeDtypeStruct((t, LANE), F32),
        scratch_shapes=[pltpu.VMEM((1, LANE), F32)],
        compiler_params=_params("arbitrary"),
    )(proj, bias, tril)


def _forget_cumsum_bwd(proj, bias, drs, dcs, dproj, name):
    t = proj.shape[0]
    rt = TOK_TILE
    nb = t // rt
    triu = jnp.asarray(np.triu(np.ones((rt, rt))), F32)

    def body(z_ref, b_ref, triu_ref, drs_ref, dcs_ref, dproj_in, dz_ref, gb_ref, carry_ref):
        step = pl.program_id(0)

        @pl.when(step == 0)
        def _():
            carry_ref[...] = jnp.zeros_like(carry_ref)
            gb_ref[...] = jnp.zeros_like(gb_ref)

        dlogf = lax.dot_general(triu_ref[...], drs_ref[...] - dcs_ref[...], NN, precision=lax.Precision.HIGHEST,
                                preferred_element_type=F32) + carry_ref[...]
        carry_ref[...] = dlogf[0:1, :]
        z = z_ref[...] + b_ref[...]
        is_head = lax.broadcasted_iota(jnp.int32, (rt, LANE), 1) < FOX_HEADS
        dz = jnp.where(is_head, dlogf / (1.0 + jnp.exp(z)), 0.0)
        dz_ref[...] = dz.astype(BF16)
        gb_ref[...] += jnp.sum(dz, axis=0, keepdims=True)

    return _pcall(
        body, name=name, grid=(nb,),
        in_specs=[pl.BlockSpec((rt, LANE), lambda i: (nb - 1 - i, FF_COL_BLOCK)),
                  pl.BlockSpec((1, LANE), lambda i: (0, 0)),
                  pl.BlockSpec((rt, rt), lambda i: (0, 0)),
                  pl.BlockSpec((rt, LANE), lambda i: (nb - 1 - i, 0)),
                  pl.BlockSpec((rt, LANE), lambda i: (nb - 1 - i, 0)),
                  pl.BlockSpec(memory_space=pl.ANY)],
        out_specs=[pl.BlockSpec((rt, LANE), lambda i: (nb - 1 - i, FF_COL_BLOCK)),
                   pl.BlockSpec((1, LANE), lambda i: (0, 0))],
        out_shape=[jax.ShapeDtypeStruct(dproj.shape, BF16), jax.ShapeDtypeStruct((1, LANE), F32)],
        input_output_aliases={5: 0},
        scratch_shapes=[pltpu.VMEM((1, LANE), F32)],
        compiler_params=_params("arbitrary"),
    )(proj, bias, triu, drs, dcs, dproj)


FOX_PAIRS = FOX_HEADS // 2
L_ONE_Q = FOX_DH
L_ONE_K = FOX_DH + 3
L_LSE = FOX_DH + 4


def _split3(x):
    hi = x.astype(BF16).astype(F32)
    r = x - hi
    mid = r.astype(BF16).astype(F32)
    return hi, mid, r - mid


def _head_to_low(slab, e):
    return slab if e == 0 else pltpu.roll(slab, FOX_DH, axis=1)


def _pair(a, b, low):
    return jnp.where(low, a, pltpu.roll(b, FOX_DH, axis=1))


def _fox_prep(proj, c, name):
    t = proj.shape[0]
    tq = TOK_TILE

    def body(p_ref, c_ref, qa_ref, ka_ref, va_ref, qt_ref, vt_ref):
        i = pl.program_id(0)
        lane = lax.broadcasted_iota(jnp.int32, (tq, LANE), 1)
        low = lane < FOX_DH
        live = (i * tq + lax.broadcasted_iota(jnp.int32, (tq, 1), 0)) >= N_PAD
        q_tail = jnp.where(lane < L_ONE_Q + 3, 1.0, 0.0)
        k_ones = (lane >= L_ONE_K) & (lane < L_ONE_K + 4)
        v_tail = jnp.where(lane < FOX_DH + 2, 1.0, 0.0)
        bias_parts = _split3(jnp.where(live, -c_ref[...], NEG))
        for pair in range(FOX_PAIRS):
            base = 3 * LANE * pair
            for e in range(2):
                h = 2 * pair + e
                q = _head_to_low(p_ref[:, base:base + LANE], e)
                k = _head_to_low(p_ref[:, base + LANE:base + 2 * LANE], e)
                v = _head_to_low(p_ref[:, base + 2 * LANE:base + 3 * LANE], e)
                hi, mid, lo = [part[:, h:h + 1] for part in bias_parts]
                ka = jnp.where(low, k, jnp.where(k_ones, 1.0, 0.0))
                ka = jnp.where(lane == L_ONE_Q, hi, jnp.where(lane == L_ONE_Q + 1, mid, jnp.where(lane == L_ONE_Q + 2, lo, ka)))
                qa = jnp.where(low, q * QK_SCALE, q_tail)
                va = jnp.where(low, v, v_tail)
                qa_ref[h] = qa.astype(BF16)
                ka_ref[h] = ka.astype(BF16)
                va_ref[h] = va.astype(BF16)
                qt_ref[h] = qa.T.astype(BF16)
                vt_ref[h] = va.T.astype(BF16)

    out = jax.ShapeDtypeStruct((FOX_HEADS, t, LANE), BF16)
    out_t = jax.ShapeDtypeStruct((FOX_HEADS, t // tq, LANE, tq), BF16)
    ospec = pl.BlockSpec((FOX_HEADS, tq, LANE), lambda i: (0, i, 0))
    tspec = pl.BlockSpec((FOX_HEADS, None, LANE, tq), lambda i: (0, i, 0, 0))
    return _pcall(
        body, name=name, grid=(t // tq,),
        in_specs=[pl.BlockSpec((tq, 3 * FOX_W), lambda i: (i, 1)), pl.BlockSpec((tq, LANE), lambda i: (i, 0))],
        out_specs=[ospec, ospec, ospec, tspec, tspec], out_shape=[out, out, out, out_t, out_t],
        compiler_params=_params("parallel"),
    )(proj, c)


STEP_PAIRS = 2
STEP_HEADS = 2 * STEP_PAIRS
FOX_GROUPS = FOX_PAIRS // STEP_PAIRS
FWD_PAIRS = 4
FWD_HEADS = 2 * FWD_PAIRS
FWD_GROUPS = FOX_PAIRS // FWD_PAIRS


def _blockdiag(a, b):
    z = jnp.zeros_like(a)
    return jnp.concatenate([jnp.concatenate([a, z], axis=1), jnp.concatenate([z, b], axis=1)], axis=0)


def _fox_fwd(qt, ka, vt, mixed, name):
    nh, nq, tq, _ = ka.shape
    t = nq * tq

    def body(qt_ref, ka_ref, vt_ref, mixed_in, mixed_ref, o_ref, lse_ref):
        i = pl.program_id(1)
        lane = lax.broadcasted_iota(jnp.int32, (tq, LANE), 1)
        key_le_query = lax.broadcasted_iota(jnp.int32, (tq, tq), 0) <= lax.broadcasted_iota(jnp.int32, (tq, tq), 1)
        qts = [qt_ref[h] for h in range(FWD_HEADS)]

        def logits(j):
            return [_dot(ka_ref[h, j], qts[h]) for h in range(FWD_HEADS)]

        def update(j, scores, carry, diagonal):
            new = []
            for h in range(FWD_HEADS):
                m, acc = carry[h]
                s = jnp.where(key_le_query, scores[h], NEG) if diagonal else scores[h]
                m_new = jnp.maximum(m, jnp.max(s, axis=0, keepdims=True))
                p = jnp.exp(s - m_new).astype(BF16)
                new.append((m_new, jnp.exp(m - m_new) * acc + _dot(vt_ref[h, j], p)))
            return tuple(new)

        init = tuple((jnp.full((1, tq), NEG, F32), jnp.zeros((LANE, tq), F32)) for _ in range(FWD_HEADS))
        carry = lax.fori_loop(0, i, lambda j, cr: update(j, logits(j), cr, False), init)
        outs, lse_rows = [], []
        for m, acc in update(i, logits(i), carry, True):
            l = acc[FOX_DH:FOX_DH + 1, :]
            outs.append((acc / l).T)
            lse_rows.append(m + jnp.log(l))
        lse_rows.append(jnp.zeros((LANE - FWD_HEADS, tq), F32))
        o_all = jnp.concatenate([_pair(outs[2 * c], outs[2 * c + 1], lane < FOX_DH) for c in range(FWD_PAIRS)], axis=1)
        mixed_ref[...] = o_all.astype(BF16)
        o_ref[...] = o_all
        lse_ref[...] = jnp.concatenate(lse_rows, axis=0).T

    width = FWD_PAIRS * LANE
    whole = pl.BlockSpec((FWD_HEADS, nq, tq, LANE), lambda g, i: (g, 0, 0, 0), pipeline_mode=pl.Buffered(1))
    whole_t = pl.BlockSpec((FWD_HEADS, nq, LANE, tq), lambda g, i: (g, 0, 0, 0), pipeline_mode=pl.Buffered(1))
    return _pcall(
        body, name=name, grid=(FWD_GROUPS, nq),
        in_specs=[pl.BlockSpec((FWD_HEADS, None, LANE, tq), lambda g, i: (g, i, 0, 0)), whole, whole_t,
                  pl.BlockSpec(memory_space=pl.ANY)],
        out_specs=[pl.BlockSpec((tq, width), lambda g, i: (i, RET_V // width + g)),
                   pl.BlockSpec((tq, width), lambda g, i: (i, g)),
                   pl.BlockSpec((None, tq, LANE), lambda g, i: (g, i, 0))],
        out_shape=[jax.ShapeDtypeStruct(mixed.shape, BF16), jax.ShapeDtypeStruct((t, FOX_W), F32),
                   jax.ShapeDtypeStruct((FWD_GROUPS, t, LANE), F32)],
        input_output_aliases={3: 0},
        compiler_params=_params("parallel", "parallel"),
    )(qt, ka, vt, mixed)


def _fox_prep_bwd(dmixed, o_fox, lse, qa, name):
    t = dmixed.shape[0]
    tq = TOK_TILE

    def body(dm_ref, o_ref, lse_ref, qa_ref, qab_ref, doa_ref):
        i = pl.program_id(0)
        lane = lax.broadcasted_iota(jnp.int32, (tq, LANE), 1)
        low = lane < FOX_DH
        live = (i * tq + lax.broadcasted_iota(jnp.int32, (tq, 1), 0)) >= N_PAD
        lse_parts = [_split3(jnp.where(live, -lse_ref[grp], 0.0)) for grp in range(FWD_GROUPS)]
        for pair in range(FOX_PAIRS):
            cols = slice(LANE * pair, LANE * (pair + 1))
            d_slab = dm_ref[:, cols]
            prod = d_slab * o_ref[:, cols]
            for e in range(2):
                h = 2 * pair + e
                nd = -jnp.sum(jnp.where(low, _head_to_low(prod, e), 0.0), axis=-1, keepdims=True)
                nd_hi = nd.astype(BF16).astype(F32)
                doa = jnp.where(low, _head_to_low(d_slab, e), 0.0)
                doa = jnp.where(lane == FOX_DH, nd_hi, jnp.where(lane == FOX_DH + 1, nd - nd_hi, doa))
                doa_ref[h] = doa.astype(BF16)
                lane_h = h % FWD_HEADS
                hi, mid, lo = [part[:, lane_h:lane_h + 1] for part in lse_parts[h // FWD_HEADS]]
                qab = qa_ref[h].astype(F32)
                qab = jnp.where(lane == L_LSE, hi, jnp.where(lane == L_LSE + 1, mid, jnp.where(lane == L_LSE + 2, lo, qab)))
                qab_ref[h] = qab.astype(BF16)

    out = jax.ShapeDtypeStruct((FOX_HEADS, t, LANE), BF16)
    hspec = pl.BlockSpec((FOX_HEADS, tq, LANE), lambda i: (0, i, 0))
    return _pcall(
        body, name=name, grid=(t // tq,),
        in_specs=[pl.BlockSpec((tq, FOX_W), lambda i: (i, 1)), pl.BlockSpec((tq, FOX_W), lambda i: (i, 0)),
                  pl.BlockSpec((FWD_GROUPS, tq, LANE), lambda i: (0, i, 0)), hspec],
        out_specs=[hspec, hspec], out_shape=[out, out],
        compiler_params=_params("parallel"),
    )(dmixed, o_fox, lse, qa)


def _fox_bwd(qab, doa, ka, va, dproj, name):
    nh, nq, tq, _ = qab.shape
    t = nq * tq
    slab = 3 * LANE * STEP_PAIRS
    group0 = (2 * RET_QK + 2 * RET_V) // slab

    def body(qab_ref, doa_ref, ka_ref, va_ref, dproj_in, dp_ref, drs_ref, dcs_ref, dq_ref):
        g, j = pl.program_id(0), pl.program_id(1)

        @pl.when((g == 0) & (j == 0))
        def _():
            drs_ref[...] = jnp.zeros_like(drs_ref)
            dcs_ref[...] = jnp.zeros_like(dcs_ref)

        @pl.when(j == 0)
        def _():
            dq_ref[...] = jnp.zeros_like(dq_ref)

        lane = lax.broadcasted_iota(jnp.int32, (tq, LANE), 1)
        low = lane < FOX_DH
        key_le_query = lax.broadcasted_iota(jnp.int32, (tq, tq), 0) <= lax.broadcasted_iota(jnp.int32, (tq, tq), 1)

        def by_head(c, a, b, col):
            h = STEP_HEADS * g + 2 * c
            return jnp.where(lane == h, a[:, col:col + 1], jnp.where(lane == h + 1, b[:, col:col + 1], 0.0))

        kbs = [ka_ref[h] for h in range(STEP_HEADS)]
        vbs = [va_ref[h] for h in range(STEP_HEADS)]

        def step(i, carry, diagonal):
            qbs = [qab_ref[h, i] for h in range(STEP_HEADS)]
            dobs = [doa_ref[h, i] for h in range(STEP_HEADS)]
            st = [_dot(kbs[h], qbs[h], NT) for h in range(STEP_HEADS)]
            dpt = [_dot(vbs[h], dobs[h], NT) for h in range(STEP_HEADS)]
            new = []
            for h in range(STEP_HEADS):
                p = jnp.exp(st[h])
                if diagonal:
                    p = jnp.where(key_le_query, p, 0.0)
                ds = (p * dpt[h]).astype(BF16)
                dq_ref[h, i] += _dot(ds, kbs[h], TN)
                dk, dv = carry[h]
                new.append((dk + _dot(ds, qbs[h]), dv + _dot(p.astype(BF16), dobs[h])))
            return tuple(new)

        zero = jnp.zeros((tq, LANE), F32)
        carry = step(j, tuple((zero, zero) for _ in range(STEP_HEADS)), True)
        carry = lax.fori_loop(j + 1, nq, lambda i, cr: step(i, cr, False), carry)
        rows = pl.ds(pl.multiple_of(j * tq, tq), tq)
        for c in range(STEP_PAIRS):
            (dka, dva), (dkb, dvb) = carry[2 * c], carry[2 * c + 1]
            c0 = 3 * LANE * c
            dp_ref[rows, c0 + LANE:c0 + 2 * LANE] = _pair(dka, dkb, low).astype(BF16)
            dp_ref[rows, c0 + 2 * LANE:c0 + 3 * LANE] = _pair(dva, dvb, low).astype(BF16)
            dcs_ref[rows, :] += by_head(c, dka, dkb, L_ONE_Q)

        @pl.when(j == nq - 1)
        def _():
            for c in range(STEP_PAIRS):
                for blk in range(nq):
                    r = slice(blk * tq, (blk + 1) * tq)
                    a, b = dq_ref[2 * c, blk], dq_ref[2 * c + 1, blk]
                    dp_ref[r, 3 * LANE * c:3 * LANE * c + LANE] = (_pair(a, b, low) * QK_SCALE).astype(BF16)
                    drs_ref[r, :] += by_head(c, a, b, L_ONE_K)

    whole = pl.BlockSpec((STEP_HEADS, nq, tq, LANE), lambda g, j: (g, 0, 0, 0), pipeline_mode=pl.Buffered(1))
    blk = pl.BlockSpec((STEP_HEADS, None, tq, LANE), lambda g, j: (g, j, 0, 0))
    sums = pl.BlockSpec((t, LANE), lambda g, j: (0, 0), pipeline_mode=pl.Buffered(1))
    return _pcall(
        body, name=name, grid=(FOX_GROUPS, nq),
        in_specs=[whole, whole, blk, blk, pl.BlockSpec(memory_space=pl.ANY)],
        out_specs=[pl.BlockSpec((t, slab), lambda g, j: (0, group0 + g)), sums, sums],
        out_shape=[jax.ShapeDtypeStruct(dproj.shape, BF16), jax.ShapeDtypeStruct((t, LANE), F32),
                   jax.ShapeDtypeStruct((t, LANE), F32)],
        input_output_aliases={4: 0},
        scratch_shapes=[pltpu.VMEM((STEP_HEADS, nq, tq, LANE), F32)],
        compiler_params=_params("arbitrary", "arbitrary"),
    )(qab, doa, ka, va, dproj)


HALO = 8


def _rows_ext(ref, r0, rows, t, before, after):
    lo, hi = r0 - before, r0 + rows + after
    width = ref.shape[-1]
    parts = []
    if lo < 0:
        parts.append(jnp.zeros((-lo, width), F32))
    parts.append(ref[max(lo, 0):min(hi, t), :].astype(F32))
    if hi > t:
        parts.append(jnp.zeros((hi - t, width), F32))
    return parts[0] if len(parts) == 1 else jnp.concatenate(parts, axis=0)


def _conv_taps(a_ext, r0_ext, cw_ref, cb_ref):
    n = a_ext.shape[0]
    if r0_ext < N_PAD:
        row = r0_ext + lax.broadcasted_iota(jnp.int32, (n, 1), 0)
        a_ext = jnp.where(row >= N_PAD, a_ext, 0.0)
    a1 = pltpu.roll(a_ext, 1, axis=0)
    a2 = pltpu.roll(a_ext, 2, axis=0)
    acc = cb_ref[...] + a2 * cw_ref[0:1, :] + a1 * cw_ref[1:2, :] + a_ext * cw_ref[2:3, :]
    return a_ext, a1, a2, acc


FF_COLS = 256


def _up_conv_fwd(n2, w_up_t, conv_w8, conv_b, name):
    t, d = n2.shape
    f = w_up_t.shape[1]
    rows = TOK_TILE
    starts = list(range(0, t, rows))

    def body(n_ref, wa_ref, wb_ref, cw_ref, cb_ref, up_ref, g_ref):
        wa, wb = wa_ref[...], wb_ref[...]

        def project(r0):
            n_rows = n_ref[r0:r0 + rows, :]
            up_ref[0, r0:r0 + rows, :] = _dot(n_rows, wa, NT)
            up_ref[1, r0:r0 + rows, :] = _dot(n_rows, wb, NT)

        def activate(r0):
            a_ext = _rows_ext(up_ref.at[0], r0, rows, t, HALO, 0)
            _, _, _, acc = _conv_taps(a_ext, r0 - HALO, cw_ref, cb_ref)
            acc = acc[HALO:, :]
            g_ref[r0:r0 + rows, :] = (acc * _sigmoid(acc) * up_ref[1, r0:r0 + rows, :]).astype(BF16)

        project(starts[0])
        for r0, r_next in zip(starts, starts[1:] + [None]):
            if r_next is not None:
                project(r_next)
            activate(r0)

    return _pcall(
        body, name=name, grid=(f // FF_COLS,),
        in_specs=[pl.BlockSpec((t, d), lambda j: (0, 0), pipeline_mode=pl.Buffered(1)),
                  pl.BlockSpec((None, FF_COLS, d), lambda j: (0, j, 0)), pl.BlockSpec((None, FF_COLS, d), lambda j: (1, j, 0)),
                  pl.BlockSpec((8, FF_COLS), lambda j: (0, j)), pl.BlockSpec((1, FF_COLS), lambda j: (0, j))],
        out_specs=[pl.BlockSpec((2, t, FF_COLS), lambda j: (0, 0, j)), pl.BlockSpec((t, FF_COLS), lambda j: (0, j))],
        out_shape=[jax.ShapeDtypeStruct((2, t, f), F32), jax.ShapeDtypeStruct((t, f), BF16)],
        compiler_params=_params("parallel"),
    )(n2, w_up_t, w_up_t, conv_w8, conv_b)


def _dg_conv_bwd(up, conv_w8, conv_b, dh2, w_down, name):
    _, t, f = up.shape
    d = dh2.shape[1]
    rows = TOK_TILE
    starts = list(range(0, t, rows))

    def body(a_ref, b_ref, cw_ref, cb_ref, dh_ref, wd_ref, dup_ref, gcw_ref, gcb_ref, dg_ref):
        wd = wd_ref[...]

        def project(r0):
            dg_ref[r0:r0 + rows, :] = _dot(dh_ref[r0:r0 + rows, :], wd, NT)

        gw = [jnp.zeros((1, FF_COLS), F32) for _ in range(3)]
        gb = jnp.zeros((1, FF_COLS), F32)
        project(starts[0])
        for r0, r_next in zip(starts, starts[1:] + [None]):
            if r_next is not None:
                project(r_next)
            a_ext = _rows_ext(a_ref, r0, rows, t, HALO, HALO)
            b_ext = _rows_ext(b_ref, r0, rows, t, HALO, HALO)
            dg_ext = _rows_ext(dg_ref, r0, rows, t, HALO, HALO)
            a0, a1, a2, acc = _conv_taps(a_ext, r0 - HALO, cw_ref, cb_ref)
            sg = _sigmoid(acc)
            dacc = dg_ext * b_ext * (sg * (1.0 + acc * (1.0 - sg)))
            n = dacc.shape[0]
            da = (dacc * cw_ref[2:3, :] + pltpu.roll(dacc, n - 1, axis=0) * cw_ref[1:2, :]
                  + pltpu.roll(dacc, n - 2, axis=0) * cw_ref[0:1, :])
            core = slice(HALO, HALO + rows)
            da = da[core, :]
            if r0 < N_PAD:
                row = r0 + lax.broadcasted_iota(jnp.int32, (rows, 1), 0)
                da = jnp.where(row >= N_PAD, da, 0.0)
            dup_ref[0, r0:r0 + rows, :] = da.astype(BF16)
            dup_ref[1, r0:r0 + rows, :] = (dg_ext * acc * sg)[core, :].astype(BF16)
            dacc_c = dacc[core, :]
            gw[0] = gw[0] + jnp.sum(dacc_c * a2[core, :], axis=0, keepdims=True)
            gw[1] = gw[1] + jnp.sum(dacc_c * a1[core, :], axis=0, keepdims=True)
            gw[2] = gw[2] + jnp.sum(dacc_c * a0[core, :], axis=0, keepdims=True)
            gb = gb + jnp.sum(dacc_c, axis=0, keepdims=True)
        gcw_ref[...] = jnp.zeros((8, FF_COLS), F32)
        for tap in range(3):
            gcw_ref[tap:tap + 1, :] = gw[tap]
        gcb_ref[...] = gb

    return _pcall(
        body, name=name, grid=(f // FF_COLS,),
        in_specs=[pl.BlockSpec((None, t, FF_COLS), lambda j: (0, 0, j)), pl.BlockSpec((None, t, FF_COLS), lambda j: (1, 0, j)),
                  pl.BlockSpec((8, FF_COLS), lambda j: (0, j)), pl.BlockSpec((1, FF_COLS), lambda j: (0, j)),
                  pl.BlockSpec((t, d), lambda j: (0, 0), pipeline_mode=pl.Buffered(1)),
                  pl.BlockSpec((FF_COLS, d), lambda j: (j, 0))],
        out_specs=[pl.BlockSpec((2, t, FF_COLS), lambda j: (0, 0, j)), pl.BlockSpec((8, FF_COLS), lambda j: (0, j)),
                   pl.BlockSpec((1, FF_COLS), lambda j: (0, j))],
        out_shape=[jax.ShapeDtypeStruct((2, t, f), BF16), jax.ShapeDtypeStruct((8, f), F32),
                   jax.ShapeDtypeStruct((1, f), F32)],
        scratch_shapes=[pltpu.VMEM((t, FF_COLS), F32)],
        compiler_params=_params("parallel"),
    )(up, up, conv_w8, conv_b, dh2, w_down)


def _exchange(arrays, kinds, name, after=None):
    n = len(arrays)
    npeer = N_DEV - 1
    n_in = n + int(after is not None)

    def body(*refs):
        ins, outs = refs[:n], refs[n_in:n_in + n]
        send_sems, recv_sems, local_sems = refs[n_in + n:]
        x, y, c = lax.axis_index("x"), lax.axis_index("y"), lax.axis_index("c")
        me = 4 * x + 2 * y + c
        copies, locals_ = [], []
        for a in range(n):
            gather = kinds[a] == "gather"
            own = pltpu.make_async_copy(ins[a] if gather else ins[a].at[me], outs[a].at[me], local_sems.at[a])
            own.start()
            locals_.append(own)
            for d in range(1, N_DEV):
                px = 1 - x if d & 4 else x
                py = 1 - y if d & 2 else y
                pc = 1 - c if d & 1 else c
                src = ins[a] if gather else ins[a].at[4 * px + 2 * py + pc]
                cp = pltpu.make_async_remote_copy(
                    src_ref=src, dst_ref=outs[a].at[me],
                    send_sem=send_sems.at[a * npeer + d - 1], recv_sem=recv_sems.at[a * npeer + d - 1],
                    device_id=(px, py, pc), device_id_type=pl.DeviceIdType.MESH)
                cp.start()
                copies.append(cp)
        for cp in copies:
            cp.wait_recv()
        for cp in copies:
            cp.wait_send()
        for own in locals_:
            own.wait()

    out_shape = [jax.ShapeDtypeStruct((N_DEV,) + (a.shape if k == "gather" else a.shape[1:]), a.dtype)
                 for a, k in zip(arrays, kinds)]
    return _pcall(
        body, name=name,
        in_specs=[pl.BlockSpec(memory_space=pl.ANY)] * n_in,
        out_specs=[pl.BlockSpec(memory_space=pl.ANY)] * n,
        out_shape=out_shape,
        scratch_shapes=[pltpu.SemaphoreType.DMA((n * npeer,)), pltpu.SemaphoreType.DMA((n * npeer,)),
                        pltpu.SemaphoreType.DMA((n,))],
        compiler_params=pltpu.CompilerParams(has_side_effects=True),
    )(*arrays, *([] if after is None else [after]))


def _peer_copies(srcs, lands, kinds, send_sems, recv_sems):
    x, y, c = lax.axis_index("x"), lax.axis_index("y"), lax.axis_index("c")
    me = 4 * x + 2 * y + c
    copies = []
    for a in range(len(srcs)):
        for d in range(1, N_DEV):
            px = 1 - x if d & 4 else x
            py = 1 - y if d & 2 else y
            pc = 1 - c if d & 1 else c
            k = a * (N_DEV - 1) + d - 1
            copies.append(pltpu.make_async_remote_copy(
                src_ref=srcs[a] if kinds[a] == "gather" else srcs[a].at[4 * px + 2 * py + pc], dst_ref=lands[a].at[me],
                send_sem=send_sems.at[k], recv_sem=recv_sems.at[k],
                device_id=(px, py, pc), device_id_type=pl.DeviceIdType.MESH))
    return copies


def _exchange_start(arrays, kinds, name, after=None):
    n = len(arrays)
    nsem = n * (N_DEV - 1)
    hbm = pl.BlockSpec(memory_space=pltpu.HBM)
    sem = pl.BlockSpec(memory_space=pltpu.SEMAPHORE)
    land_shapes = [(N_DEV,) + (a.shape if k == "gather" else a.shape[1:]) for a, k in zip(arrays, kinds)]

    n_in = 2 * n + int(after is not None)

    def body(*refs):
        srcs, lands = refs[:n], refs[n:2 * n]
        send_sems, recv_sems = refs[n_in], refs[n_in + 1]
        token = refs[-1]
        for cp in _peer_copies(srcs, lands, kinds, send_sems, recv_sems):
            cp.start()
        token[...] = jnp.zeros_like(token)

    operands = [pltpu.with_memory_space_constraint(a, pltpu.HBM) for a in arrays]
    operands += [pltpu.with_memory_space_constraint(lax.empty(s, a.dtype), pltpu.HBM) for s, a in zip(land_shapes, arrays)]
    operands += [] if after is None else [after]
    out = _pcall(
        body, name=name,
        in_specs=[hbm] * (2 * n) + ([] if after is None else [pl.BlockSpec(memory_space=pl.ANY)]),
        out_specs=[sem, sem] + [hbm] * (2 * n) + [pl.BlockSpec(memory_space=pltpu.VMEM)],
        out_shape=[pltpu.SemaphoreType.DMA((nsem,)), pltpu.SemaphoreType.DMA((nsem,))]
        + [pltpu.HBM(a.shape, a.dtype) for a in arrays]
        + [pltpu.HBM(s, a.dtype) for s, a in zip(land_shapes, arrays)]
        + [jax.ShapeDtypeStruct((8, LANE), F32)],
        input_output_aliases={k: 2 + k for k in range(2 * n)},
        compiler_params=pltpu.CompilerParams(has_side_effects=pltpu.SideEffectType.DATAFLOW_SIDE_EFFECTING),
    )(*operands)
    return out[0], out[1], list(out[2:2 + n]), list(out[2 + n:2 + 2 * n]), out[-1]


def _exchange_wait(started, kinds, after, name, fill_own=True):
    send_sems, recv_sems, srcs, lands, _ = started
    n = len(srcs)
    hbm = pl.BlockSpec(memory_space=pltpu.HBM)
    sem = pl.BlockSpec(memory_space=pltpu.SEMAPHORE)

    def body(*refs):
        src_refs, land_refs = refs[:n], refs[n:2 * n]
        copies = _peer_copies(src_refs, land_refs, kinds, refs[2 * n], refs[2 * n + 1])
        for cp in copies:
            cp.wait_send()
        for cp in copies:
            cp.wait_recv()

    out = _pcall(
        body, name=name,
        in_specs=[hbm] * (2 * n) + [sem, sem, pl.BlockSpec(memory_space=pl.ANY)],
        out_specs=[hbm] * (2 * n),
        out_shape=[pltpu.HBM(a.shape, a.dtype) for a in srcs + lands],
        input_output_aliases={k: k for k in range(2 * n)},
        compiler_params=pltpu.CompilerParams(has_side_effects=pltpu.SideEffectType.DATAFLOW_SIDE_EFFECTING),
    )(*srcs, *lands, send_sems, recv_sems, after)
    if not fill_own:
        return list(out[:n]), list(out[n:])
    me = 4 * lax.axis_index("x") + 2 * lax.axis_index("y") + lax.axis_index("c")
    filled = []
    for src, land, kind in zip(out[:n], out[n:], kinds):
        own = src if kind == "gather" else lax.dynamic_index_in_dim(src, me, axis=0, keepdims=False)
        filled.append(lax.dynamic_update_slice(land, own[None], (me,) + (0,) * own.ndim))
    return filled


def _sum_slots(slots, blocks, name, rows_tile):
    nd, r, c = slots.shape

    def body(s_ref, b_ref, o_ref):
        me = 4 * lax.axis_index("x") + 2 * lax.axis_index("y") + lax.axis_index("c")
        acc = jnp.where(me == 0, b_ref[0], s_ref[0]).astype(F32)
        for p in range(1, nd):
            acc = acc + jnp.where(me == p, b_ref[p], s_ref[p]).astype(F32)
        o_ref[...] = acc

    spec = pl.BlockSpec((nd, rows_tile, c), lambda i: (0, i, 0))
    return _pcall(
        body, name=name, grid=(r // rows_tile,), in_specs=[spec, spec],
        out_specs=pl.BlockSpec((rows_tile, c), lambda i: (i, 0)),
        out_shape=jax.ShapeDtypeStruct((r, c), F32),
        compiler_params=_params("parallel"),
    )(slots, blocks)


def _sum_slots_small(slot_arrays, own_arrays, name):
    n = len(slot_arrays)

    def body(*refs):
        me = 4 * lax.axis_index("x") + 2 * lax.axis_index("y") + lax.axis_index("c")
        for s_ref, own_ref, o_ref in zip(refs[:n], refs[n:2 * n], refs[2 * n:]):
            acc = jnp.where(me == 0, own_ref[...], s_ref[0])
            for p in range(1, s_ref.shape[0]):
                acc = acc + jnp.where(me == p, own_ref[...], s_ref[p])
            o_ref[...] = acc

    return _pcall(body, name=name, out_shape=[jax.ShapeDtypeStruct(a.shape[1:], F32) for a in slot_arrays])(
        *slot_arrays, *own_arrays)


def _adamw_update(w_ref, g_ref, m_ref, v_ref, d_ref, nm_ref, nv_ref):
    gr = g_ref[...]
    nm = ADAM_B1 * m_ref[...] + (1.0 - ADAM_B1) * gr
    nv = ADAM_B2 * v_ref[...] + (1.0 - ADAM_B2) * (gr * gr)
    m_hat = nm / (1.0 - ADAM_B1 ** ADAM_STEP)
    v_hat = nv / (1.0 - ADAM_B2 ** ADAM_STEP)
    d_ref[...] = -ADAM_LR * (m_hat / (jnp.sqrt(v_hat) + ADAM_EPS) + ADAM_WD * w_ref[...])
    nm_ref[...] = nm
    nv_ref[...] = nv


def _adamw_small(ws, gs, ms, vs, name):
    n = len(ws)

    def body(*refs):
        ins, outs = refs[:4 * n], refs[4 * n:]
        for k in range(n):
            _adamw_update(ins[k], ins[n + k], ins[2 * n + k], ins[3 * n + k], outs[k], outs[n + k], outs[2 * n + k])

    shapes = [jax.ShapeDtypeStruct(w.shape, F32) for w in ws]
    out = _pcall(body, name=name, out_shape=shapes * 3)(*ws, *gs, *ms, *vs)
    return list(out[:n]), list(out[n:2 * n]), list(out[2 * n:])


def _adamw(w, g, m, v, name, rows_tile):
    r, c = w.shape
    body = lambda *refs: _adamw_update(*refs)
    spec = pl.BlockSpec((rows_tile, c), lambda i: (i, 0))
    shp = jax.ShapeDtypeStruct((r, c), F32)
    return _pcall(
        body, name=name, grid=(r // rows_tile,), in_specs=[spec] * 4, out_specs=[spec] * 3, out_shape=[shp] * 3,
        compiler_params=_params("parallel"),
    )(w, g, m, v)


F0 = 2 * RET_QK + 2 * RET_V


def _to_internal_rows(w_t):
    cols = w_t.shape[1]
    fox = w_t[F0:F0 + 3 * FOX_W].reshape(3, FOX_PAIRS, LANE, cols).transpose(1, 0, 2, 3).reshape(3 * FOX_W, cols)
    tail = jnp.zeros((IN_PAD - IN_WIDTH, cols), w_t.dtype)
    return jnp.concatenate([w_t[:F0], fox, w_t[F0 + 3 * FOX_W:], tail], axis=0)


def _from_internal_rows(g_t):
    cols = g_t.shape[1]
    fox = g_t[F0:F0 + 3 * FOX_W].reshape(FOX_PAIRS, 3, LANE, cols).transpose(1, 0, 2, 3).reshape(3 * FOX_W, cols)
    return jnp.concatenate([g_t[:F0], fox, g_t[F0 + 3 * FOX_W:F0 + 3 * FOX_W + FOX_HEADS]], axis=0)


def _local_step(x, target, meta, attn_g, fox_b, ret_g, ffn_g, conv_w8, conv_b, final_g,
                first_weight, late_weights, ffn_grads_ready, out_grad_ready, in_grad_ready):
    seq, d = x.shape
    t = seq + PREFIX
    tm = TOK_TILE
    nq = t // tm
    fox_b128 = jnp.pad(fox_b, ((0, 0), (0, LANE - FOX_HEADS)))

    h0, n1 = _prep_norm(x, meta, attn_g, "prep_norm")
    w_in_t = first_weight(n1)
    proj = _mm_simple(n1, w_in_t, mode="nt", tm=tm, tn=IN_PAD, tk=d, out_dtype=F32, name="mm_in")
    cos, sin = _rope_tables(t)
    o_pre, mixed, states = _ret_fwd(proj, cos, sin, ret_g, "ret_fwd")
    c = _forget_cumsum(proj, fox_b128, "forget_cumsum")
    qa, ka, va, qt, vt = _fox_prep(proj, c, "fox_prep")
    by_block = lambda a: a.reshape(FOX_HEADS, nq, tm, LANE)
    mixed, o_fox, lse = _fox_fwd(qt, by_block(ka), vt, mixed, "fox_fwd")
    w_out, w_up_t, w_down = late_weights(o_fox)
    tile = pl.BlockSpec((tm, d), lambda i: (i, 0))
    row_vec = pl.BlockSpec((1, d), lambda i: (0, 0))
    resident = lambda shape: pl.BlockSpec(shape, lambda i: (0,) * len(shape), pipeline_mode=pl.Buffered(1))
    acts = lambda dtype: jax.ShapeDtypeStruct((t, d), dtype)
    vec = jax.ShapeDtypeStruct((1, d), F32)

    def residual_and_norm(i, acc, ins, outs):
        h = acc + ins[0][...]
        outs[0][...] = h
        outs[1][...] = (h * lax.rsqrt(jnp.mean(h * h, axis=-1, keepdims=True) + EPS) * ins[1][...]).astype(BF16)

    h1, n2 = _matmul_rows([mixed], [tile], [w_out], [resident((d, d))], [h0, ffn_g], [tile, row_vec],
                          [tile, tile], [acts(F32), acts(BF16)], residual_and_norm, mode="nn", steps=nq, name="mm_out_norm")
    nf = D_FF // 1408
    up, g = _up_conv_fwd(n2, w_up_t, conv_w8, conv_b, "up_conv_fwd")

    def residual_loss_bwd(i, acc, ins, outs):
        loss_ref, dh_ref, dhb_ref, gg_ref = outs
        part, dh, gg = _loss_tile(i, acc + ins[0][...], jnp.concatenate([ins[1][...], ins[2][...], ins[3][...]], axis=0),
                                  ins[4][...])
        _accumulate(loss_ref, i, jnp.broadcast_to(part, loss_ref.shape))
        dh_ref[...] = dh
        dhb_ref[...] = dh.astype(BF16)
        _accumulate(gg_ref, i, gg)

    loss_tile, dh2, dh2_b, g_final = _matmul_rows(
        [g], [pl.BlockSpec((tm, D_FF), lambda i: (i, 0))], [w_down], [resident((D_FF, d))],
        [h1, target, target, target, final_g], [tile] + _shifted_row_specs(d) + [row_vec],
        [pl.BlockSpec((8, LANE), lambda i: (0, 0)), tile, tile, row_vec],
        [jax.ShapeDtypeStruct((8, LANE), F32), acts(F32), acts(BF16), vec], residual_loss_bwd,
        mode="nn", steps=nq, name="mm_down_loss")

    tkw = 2112 if t % 2112 == 0 else tm
    gw_down = _mm_simple(g, dh2_b, mode="tn", tm=1408, tn=d, tk=tkw, out_dtype=BF16, name="mm_gw_down")
    dup, g_conv_w8, g_conv_b = _dg_conv_bwd(up, conv_w8, conv_b, dh2_b, w_down, "dg_conv_bwd")

    half = lambda p: pl.BlockSpec((None, tm, D_FF), lambda i: (p, i, 0))
    half_w = lambda p: pl.BlockSpec((None, D_FF, d), lambda i: (p, 0, 0), pipeline_mode=pl.Buffered(1))
    gw_up_t = _matmul(
        dup, n2, mode="tn", grid=(2 * nf, 1, t // tkw),
        a_spec=pl.BlockSpec((None, tkw, 1408), lambda i, j, k: (i // nf, k, i % nf)),
        b_spec=pl.BlockSpec((tkw, d), lambda i, j, k: (k, 0)),
        o_spec=pl.BlockSpec((1408, d), lambda i, j, k: (i, 0)),
        out_shape=jax.ShapeDtypeStruct((2 * D_FF, d), BF16), name="mm_gw_up")
    def norm_bwd_and_mixer_grad(i, acc, ins, outs):
        dh, gg = _rms_bwd_tile(acc, ins[0][...], ins[1][...], ins[2][...])
        outs[0][...] = dh
        _accumulate(outs[1], i, gg)
        outs[2][...] = _dot(dh.astype(BF16), ins[3][...], NT)

    dh1, g_ffn, dmixed = _matmul_rows(
        [dup, dup], [half(0), half(1)], [w_up_t, w_up_t], [half_w(0), half_w(1)],
        [h1, ffn_g, dh2, w_out], [tile, row_vec, tile, resident((d, d))], [tile, row_vec, tile],
        [acts(F32), vec, acts(F32)], norm_bwd_and_mixer_grad,
        mode="nn", steps=nq, name="mm_dn2_norm_bwd", after=ffn_grads_ready(gw_down, gw_up_t))
    gw_out = _mm_simple(mixed, dh1, mode="tn", tm=d, tn=d, tk=tkw, out_dtype=BF16, name="mm_gw_out")
    dproj, g_ret = _ret_bwd(proj, cos, sin, ret_g + out_grad_ready(gw_out), dmixed, o_pre, states, "ret_bwd")
    qab, doa = _fox_prep_bwd(dmixed, o_fox, lse, qa, "fox_prep_bwd")
    dproj, drs, dcs = _fox_bwd(by_block(qab), by_block(doa), by_block(ka), by_block(va), dproj, "fox_bwd")
    dproj, g_fox_b = _forget_cumsum_bwd(proj, fox_b128, drs, dcs, dproj, "forget_cumsum_bwd")
    gw_in_t = _mm_simple(dproj, n1, mode="tn", tm=640, tn=d, tk=tkw, out_dtype=BF16, name="mm_gw_in")
    sent = in_grad_ready(gw_in_t)
    def input_grads(i, acc, ins, outs):
        gx_ref, gmeta_ref, gg_ref, buf_ref, sems = outs
        dh, gg = _rms_bwd_tile(acc, ins[0][...], ins[1][...], ins[2][...])
        _accumulate(gg_ref, i, gg)
        slot = i % 2

        def first_copy():
            return pltpu.make_async_copy(buf_ref.at[0, pl.ds(PREFIX, tm - PREFIX)], gx_ref.at[pl.ds(0, tm - PREFIX)],
                                         sems.at[0])

        def tile_copy(tile, buf_slot):
            rows = pl.ds(pl.multiple_of(tile * tm - PREFIX, PREFIX), tm)
            return pltpu.make_async_copy(buf_ref.at[buf_slot], gx_ref.at[rows], sems.at[buf_slot])

        @pl.when(i == 1)
        def _():
            first_copy().wait()

        @pl.when(i >= 2)
        def _():
            tile_copy(i - 1, 1 - slot).wait()

        buf_ref[slot] = dh

        @pl.when(i == 0)
        def _():
            gmeta_ref[...] = dh[N_PAD:PREFIX, :]
            first_copy().start()

        @pl.when(i > 0)
        def _():
            tile_copy(i, slot).start()

        @pl.when(i == nq - 1)
        def _():
            tile_copy(i, slot).wait()

    grad_x, g_meta, g_attn = _matmul_rows(
        [dproj], [pl.BlockSpec((tm, IN_PAD), lambda i: (i, 0))], [w_in_t], [resident((IN_PAD, d))],
        [h0, attn_g, dh1], [tile, row_vec, tile],
        [pl.BlockSpec(memory_space=pl.ANY), pl.BlockSpec((N_META, d), lambda i: (0, 0)), row_vec],
        [jax.ShapeDtypeStruct((seq, d), F32), jax.ShapeDtypeStruct((N_META, d), F32), vec], input_grads,
        mode="nn", steps=nq, name="mm_dn1_norm_bwd", after=sent,
        scratch=[pltpu.VMEM((2, tm, d), F32), pltpu.SemaphoreType.DMA((2,))])

    grads = dict(meta=g_meta, attn_g=g_attn, fox_b=g_fox_b, ret_g=g_ret,
                 ffn_g=g_ffn, conv_w=g_conv_w8, conv_b=g_conv_b, final_g=g_final)
    return loss_tile, grad_x, grads


def kernel(x, meta_tokens, attn_norm_g, w_in, fox_forget_b, ret_norm_g, w_out, ffn_norm_g, w_up, conv_w, conv_b, w_down, final_norm_g, loss_target, m_meta_tokens, m_attn_norm_g, m_w_in, m_fox_forget_b, m_ret_norm_g, m_w_out, m_ffn_norm_g, m_w_up, m_conv_w, m_conv_b, m_w_down, m_final_norm_g, v_meta_tokens, v_attn_norm_g, v_w_in, v_fox_forget_b, v_ret_norm_g, v_w_out, v_ffn_norm_g, v_w_up, v_conv_w, v_conv_b, v_w_down, v_final_norm_g):
    d = D_MODEL
    me = 4 * lax.axis_index("x") + 2 * lax.axis_index("y") + lax.axis_index("c")
    in_blk = IN_WIDTH // N_DEV
    in_blk_pad = 400
    up_blk = 2 * D_FF // N_DEV
    down_blk = D_FF // N_DEV
    cw_blk = D_FF // N_DEV

    w_in_loc = jnp.pad(w_in[0].T.astype(BF16), ((0, in_blk_pad - in_blk), (0, 0)))
    cw_loc = jnp.pad(conv_w[0], ((0, 5), (0, 384 - cw_blk)))
    g_meta, g_cw = _exchange([meta_tokens, cw_loc], ["gather"] * 2, "gather_small")
    first = _exchange_start([w_in_loc], ["gather"], "gather_in_start", after=g_meta)
    rest_loc = [(w_out[0] + first[-1][0:1, 0:1]).astype(BF16), w_up[0].T.astype(BF16), w_down[0].astype(BF16)]
    rest = _exchange_start(rest_loc, ["gather"] * 3, "gather_rest_start")
    meta_f = g_meta.transpose(1, 0, 2).reshape(N_META, d)
    conv_w8 = jnp.pad(g_cw[:, :3, :cw_blk].transpose(1, 0, 2).reshape(3, D_FF), ((0, 5), (0, 0)))
    pending = {}

    def first_weight(after):
        (g_in,) = _exchange_wait(first, ["gather"], after, "gather_in_wait")
        return _to_internal_rows(g_in[:, :in_blk].reshape(IN_WIDTH, d))

    def in_grad_ready(gw_in_t):
        blocks = _from_internal_rows(gw_in_t).reshape(N_DEV, in_blk, d)
        blocks = jnp.pad(blocks, ((0, 0), (0, in_blk_pad - in_blk), (0, 0)))
        pending["in"] = _exchange_start([blocks], ["scatter"], "grads_in_start")
        return pending["in"][-1][0:1, 0:1]

    def late_weights(after):
        g_out, g_up, g_down = _exchange_wait(rest, ["gather"] * 3, after, "gather_rest_wait")
        return g_out.reshape(d, d), g_up.reshape(2, D_FF, d), g_down.reshape(D_FF, d)

    def ffn_grads_ready(gw_down, gw_up_t):
        blocks = [gw_down.reshape(N_DEV, down_blk, d), gw_up_t.reshape(N_DEV, up_blk, d)]
        pending["ffn"] = _exchange_start(blocks, ["scatter"] * 2, "grads_ffn_start")
        return pending["ffn"][-1][0:1, 0:1]

    def out_grad_ready(gw_out):
        pending["out"] = _exchange_start([gw_out.reshape(N_DEV, d // N_DEV, d)], ["scatter"], "grads_out_start")
        return pending["out"][-1][0:1, 0:1]

    loss_tile, grad_x, gr = _local_step(
        x[0], loss_target[0], meta_f, attn_norm_g + rest[-1][0:1, 0:1], fox_forget_b, ret_norm_g, ffn_norm_g,
        conv_w8, conv_b, final_norm_g.reshape(1, d), first_weight, late_weights, ffn_grads_ready, out_grad_ready,
        in_grad_ready)

    small = [loss_tile, gr["attn_g"], gr["fox_b"], gr["ret_g"], gr["ffn_g"], gr["conv_b"], gr["final_g"],
             gr["meta"], gr["conv_w"]]
    small_kinds = ["gather"] * len(small)
    small_started = _exchange_start(small, small_kinds, "grads_small_start")

    (s_down, s_up), (r_down, r_up) = _exchange_wait(pending["ffn"], ["scatter"] * 2, small_started[-1], "grads_ffn_wait",
                                                    fill_own=False)
    (s_out,), (r_out,) = _exchange_wait(pending["out"], ["scatter"], small_started[-1], "grads_out_wait", fill_own=False)
    g_w_out = _sum_slots(r_out, s_out, "sum_w_out", d // N_DEV)
    g_w_up_t = _sum_slots(r_up, s_up, "sum_w_up", up_blk // 2)
    g_w_down = _sum_slots(r_down, s_down, "sum_w_down", down_blk)
    as_t = lambda a: a[0].T
    from_t = lambda a: a.T[None]
    d_w_out, m_w_out_n, v_w_out_n = [a[None] for a in _adamw(w_out[0], g_w_out, m_w_out[0], v_w_out[0], "adamw_w_out", 128)]
    up_t = _adamw(as_t(w_up), g_w_up_t, as_t(m_w_up), as_t(v_w_up), "adamw_w_up", up_blk // 2)
    d_w_up, m_w_up_n, v_w_up_n = [from_t(a) for a in up_t]
    d_w_down, m_w_down_n, v_w_down_n = [a[None] for a in _adamw(w_down[0], g_w_down, m_w_down[0], v_w_down[0],
                                                                "adamw_w_down", down_blk)]

    own_small, r_small = _exchange_wait(small_started, small_kinds, up_t[0], "grads_small_wait", fill_own=False)
    (loss_all, g_attn, g_fox_b128, g_ret, g_ffn, g_conv_b, g_final, g_meta_full, g_cw_full) = _sum_slots_small(
        r_small, own_small, "sum_small")
    loss = loss_all[0, 0]
    g_fox_b = g_fox_b128[:, :FOX_HEADS]
    g_meta_loc = lax.dynamic_slice(g_meta_full, (0, me * (d // N_DEV)), (N_META, d // N_DEV))
    g_cw_loc = lax.dynamic_slice(g_cw_full, (0, me * cw_blk), (3, cw_blk))

    (s_in,), (r_in,) = _exchange_wait(pending["in"], ["scatter"], r_small[0], "grads_in_wait", fill_own=False)
    g_w_in_t = _sum_slots(r_in, s_in, "sum_w_in", in_blk_pad)[:in_blk]
    d_w_in, m_w_in_n, v_w_in_n = [from_t(a) for a in _adamw(as_t(w_in), g_w_in_t, as_t(m_w_in), as_t(v_w_in),
                                                            "adamw_w_in", in_blk)]
    g_w_in, g_w_up = g_w_in_t.T, g_w_up_t.T
    row = lambda a: a.reshape(1, d)
    sm_grads = [g_meta_loc, g_attn, g_fox_b, g_ret, g_ffn, g_cw_loc, g_conv_b, g_final]
    sm_w = [meta_tokens, attn_norm_g, fox_forget_b, ret_norm_g, ffn_norm_g, conv_w[0], conv_b, row(final_norm_g)]
    sm_m = [m_meta_tokens, m_attn_norm_g, m_fox_forget_b, m_ret_norm_g, m_ffn_norm_g, m_conv_w[0], m_conv_b,
            row(m_final_norm_g)]
    sm_v = [v_meta_tokens, v_attn_norm_g, v_fox_forget_b, v_ret_norm_g, v_ffn_norm_g, v_conv_w[0], v_conv_b,
            row(v_final_norm_g)]
    dl, ml, vl = [lst[:7] + [lst[7].reshape(d)] for lst in _adamw_small(sm_w, sm_grads, sm_m, sm_v, "adamw_small")]

    def by_weight(meta_, attn_, w_in_, fox_, ret_, w_out_, ffn_, w_up_, cw_, cb_, w_down_, final_):
        return (meta_, attn_, w_in_, fox_, ret_, w_out_, ffn_, w_up_, cw_[None], cb_, w_down_, final_)

    grads_out = by_weight(g_meta_loc, g_attn, g_w_in[None], g_fox_b, g_ret, g_w_out[None], g_ffn, g_w_up[None], g_cw_loc,
                          g_conv_b, g_w_down[None], g_final.reshape(d))
    delta_out = by_weight(dl[0], dl[1], d_w_in, dl[2], dl[3], d_w_out, dl[4], d_w_up, dl[5], dl[6], d_w_down, dl[7])
    m_out = by_weight(ml[0], ml[1], m_w_in_n, ml[2], ml[3], m_w_out_n, ml[4], m_w_up_n, ml[5], ml[6], m_w_down_n, ml[7])
    v_out = by_weight(vl[0], vl[1], v_w_in_n, vl[2], vl[3], v_w_out_n, vl[4], v_w_up_n, vl[5], vl[6], v_w_down_n, vl[7])
    return (loss, grad_x[None]) + grads_out + delta_out + m_out + v_out
```

```python
import numpy as np
import jax
import jax.numpy as jnp
from jax import lax
from jax.experimental import pallas as pl
from jax.experimental.pallas import tpu as pltpu

F32 = jnp.float32
BF16 = jnp.bfloat16

D_MODEL = 1024
N_META = 16
N_PAD = 112
PREFIX = 128
RET_HEADS = 4
RET_DK = 64
RET_DV = 128
FOX_HEADS = 8
FOX_DH = 64
D_FF = 2816
ROPE_BASE = 10000.0
EPS = 1e-6
NEG = -1e30
RET_QK = RET_HEADS * RET_DK
RET_V = RET_HEADS * RET_DV
FOX_W = FOX_HEADS * FOX_DH
IN_WIDTH = 2 * RET_QK + 2 * RET_V + 3 * FOX_W + FOX_HEADS
IN_PAD = 3200
FF_COL_BLOCK = (IN_WIDTH - FOX_HEADS) // 128
QK_SCALE = 0.125

ADAM_LR = 0.001
ADAM_B1 = 0.9
ADAM_B2 = 0.999
ADAM_EPS = 1e-08
ADAM_WD = 0.01
ADAM_STEP = 10

N_DEV = 8
LANE = 128
ROW_TILE = 128
TOK_TILE = 384

NN = (((1,), (0,)), ((), ()))
NT = (((1,), (1,)), ((), ()))
TN = (((0,), (0,)), ((), ()))


def _pcall(body, **kw):
    return pl.pallas_call(body, **kw)


def _params(*sem):
    return pltpu.CompilerParams(dimension_semantics=sem)


def _dot(a, b, dims=NN):
    return lax.dot_general(a, b, dims, preferred_element_type=F32)


def _sigmoid(x):
    return 0.5 * jnp.tanh(0.5 * x) + 0.5


def _matmul(a, b, *, mode, grid, a_spec, b_spec, o_spec, out_shape, name, add=None, add_spec=None, after=None):
    dims = {"nn": NN, "nt": NT, "tn": TN}[mode]
    nk = grid[2]
    has_add = add is not None
    a_list, b_list = (list(a), list(b)) if isinstance(a, (list, tuple)) else ([a], [b])
    a_specs, b_specs = (list(a_spec), list(b_spec)) if isinstance(a_spec, (list, tuple)) else ([a_spec], [b_spec])
    nt = len(a_list)
    n_in = 2 * nt + int(has_add) + int(after is not None)

    def body(*refs):
        a_refs, b_refs = refs[:nt], refs[nt:2 * nt]
        add_ref = refs[2 * nt] if has_add else None
        o_ref = refs[n_in]
        part = _dot(a_refs[0][...].astype(BF16), b_refs[0][...].astype(BF16), dims)
        for ar, br in zip(a_refs[1:], b_refs[1:]):
            part = part + _dot(ar[...].astype(BF16), br[...].astype(BF16), dims)

        def finish(acc):
            if has_add:
                acc = acc + add_ref[...]
            o_ref[...] = acc.astype(o_ref.dtype)

        if nk == 1:
            finish(part)
        else:
            acc_ref = refs[-1]
            k = pl.program_id(2)

            @pl.when(k == 0)
            def _():
                acc_ref[...] = part

            @pl.when(k > 0)
            def _():
                acc_ref[...] += part

            @pl.when(k == nk - 1)
            def _():
                finish(acc_ref[...])

    in_specs = a_specs + b_specs + ([add_spec] if has_add else [])
    args = tuple(a_list) + tuple(b_list) + ((add,) if has_add else ())
    if after is not None:
        in_specs, args = in_specs + [pl.BlockSpec(memory_space=pl.ANY)], args + (after,)
    scratch = [] if nk == 1 else [pltpu.VMEM(tuple(d for d in o_spec.block_shape if d is not None), F32)]
    return _pcall(
        body, name=name, grid=grid, in_specs=in_specs, out_specs=o_spec, out_shape=out_shape,
        scratch_shapes=scratch, compiler_params=_params("parallel", "parallel", "arbitrary"),
    )(*args)


def _mm_simple(a, b, *, mode, tm, tn, tk, out_dtype, name, add=None, after=None):
    if mode == "tn":
        K, M = a.shape
    else:
        M, K = a.shape
    N = b.shape[0] if mode == "nt" else b.shape[1]
    grid = (M // tm, N // tn, K // tk)
    resident = dict(pipeline_mode=pl.Buffered(1)) if (tn == N and tk == K) else {}
    a_spec = pl.BlockSpec((tk, tm), lambda i, j, k: (k, i)) if mode == "tn" else pl.BlockSpec((tm, tk), lambda i, j, k: (i, k))
    b_spec = (pl.BlockSpec((tn, tk), lambda i, j, k: (j, k), **resident) if mode == "nt"
              else pl.BlockSpec((tk, tn), lambda i, j, k: (k, j), **resident))
    o_spec = pl.BlockSpec((tm, tn), lambda i, j, k: (i, j))
    return _matmul(a, b, mode=mode, grid=grid, a_spec=a_spec, b_spec=b_spec, o_spec=o_spec,
                   out_shape=jax.ShapeDtypeStruct((M, N), out_dtype), name=name, add=add,
                   add_spec=o_spec if add is not None else None, after=after)


def _matmul_rows(a_list, a_specs, b_list, b_specs, extras, extra_specs, out_specs, out_shape, epilogue, *,
                 mode, steps, name, after=None, scratch=()):
    dims = {"nn": NN, "nt": NT}[mode]
    nt, ne = len(a_list), len(extras)
    n_in = 2 * nt + ne + int(after is not None)

    def body(*refs):
        acc = _dot(refs[0][...].astype(BF16), refs[nt][...].astype(BF16), dims)
        for k in range(1, nt):
            acc = acc + _dot(refs[k][...].astype(BF16), refs[nt + k][...].astype(BF16), dims)
        epilogue(pl.program_id(0), acc, refs[2 * nt:2 * nt + ne], refs[n_in:])

    in_specs = list(a_specs) + list(b_specs) + list(extra_specs)
    args = tuple(a_list) + tuple(b_list) + tuple(extras)
    if after is not None:
        in_specs, args = in_specs + [pl.BlockSpec(memory_space=pl.ANY)], args + (after,)
    return _pcall(body, name=name, grid=(steps,), in_specs=in_specs, out_specs=out_specs, out_shape=out_shape,
                  scratch_shapes=list(scratch), compiler_params=_params("arbitrary"))(*args)


def _rms_bwd_tile(dy, x, gain, dres):
    r = lax.rsqrt(jnp.mean(x * x, axis=-1, keepdims=True) + EPS)
    xhat = x * r
    u = dy * gain
    return dres + r * (u - xhat * jnp.mean(u * xhat, axis=-1, keepdims=True)), jnp.sum(dy * xhat, axis=0, keepdims=True)


def _loss_tile(i, x, tgt, gain):
    d = x.shape[-1]
    r = lax.rsqrt(jnp.mean(x * x, axis=-1, keepdims=True) + EPS)
    xhat = x * r
    counted = (i * TOK_TILE + lax.broadcasted_iota(jnp.int32, (TOK_TILE, 1), 0)) >= PREFIX
    err = jnp.where(counted, xhat * gain - tgt, 0.0)
    dy = err * (1.0 / d)
    u = dy * gain
    dh = r * (u - xhat * jnp.mean(u * xhat, axis=-1, keepdims=True))
    return 0.5 * jnp.sum(jnp.mean(err * err, axis=-1, keepdims=True)), dh, jnp.sum(dy * xhat, axis=0, keepdims=True)


def _accumulate(ref, i, part):
    @pl.when(i == 0)
    def _():
        ref[...] = part

    @pl.when(i > 0)
    def _():
        ref[...] += part


def _prep_norm(x, meta, gain, name):
    seq, d = x.shape
    t = seq + PREFIX

    def body(xa_ref, xb_ref, xc_ref, meta_ref, g_ref, h_ref, n_ref):
        i = pl.program_id(0)

        @pl.when(i == 0)
        def _():
            h_ref[0:N_PAD, :] = jnp.zeros((N_PAD, d), F32)
            h_ref[N_PAD:ROW_TILE, :] = meta_ref[...]

        @pl.when(i > 0)
        def _():
            h_ref[0:ROW_TILE, :] = xa_ref[...]

        h_ref[ROW_TILE:2 * ROW_TILE, :] = xb_ref[...]
        h_ref[2 * ROW_TILE:3 * ROW_TILE, :] = xc_ref[...]
        h = h_ref[...]
        r = lax.rsqrt(jnp.mean(h * h, axis=-1, keepdims=True) + EPS)
        n_ref[...] = (h * r * g_ref[...]).astype(BF16)

    return _pcall(
        body, name=name, grid=(t // TOK_TILE,),
        in_specs=_shifted_row_specs(d) + [pl.BlockSpec((N_META, d), lambda i: (0, 0)), pl.BlockSpec((1, d), lambda i: (0, 0))],
        out_specs=[pl.BlockSpec((TOK_TILE, d), lambda i: (i, 0)), pl.BlockSpec((TOK_TILE, d), lambda i: (i, 0))],
        out_shape=[jax.ShapeDtypeStruct((t, d), F32), jax.ShapeDtypeStruct((t, d), BF16)],
        compiler_params=_params("parallel"),
    )(x, x, x, meta, gain)


def _shifted_row_specs(d):
    blocks_per_tile = TOK_TILE // ROW_TILE
    return [pl.BlockSpec((ROW_TILE, d), lambda i, r=r: (jnp.maximum(blocks_per_tile * i + r, 0), 0)) for r in (-1, 0, 1)]


def _ret_consts(bk):
    gam = 1.0 - 2.0 ** (-5.0 - np.arange(RET_HEADS))
    n = np.arange(bk)
    same_or_earlier_chunk = (n[None, :] // 64) <= (n[:, None] // 64)
    w = gam[:, None, None] ** np.abs(n[:, None] - n[None, :])[None] * same_or_earlier_chunk[None]
    wq = gam[:, None] ** (n[None, :] + 1.0)
    wk = gam[:, None] ** (bk - 1.0 - n[None, :])
    mask = (np.arange(RET_QK)[None, :] // RET_DK) == np.arange(RET_HEADS)[:, None]
    return (jnp.asarray(w, F32), jnp.asarray(wq[:, :, None], F32), jnp.asarray(wk[:, :, None], F32),
            jnp.asarray(mask[:, None, :], F32), [float(g ** bk) for g in gam])


def _rope_tables(t):
    half = RET_DK // 2
    inv = 1.0 / (ROPE_BASE ** (jnp.arange(half, dtype=F32) / half))
    ang = jnp.arange(t).astype(F32)[:, None] * inv[None, :]
    cos, sin = jnp.cos(ang), jnp.sin(ang)
    return (jnp.tile(jnp.concatenate([cos, cos], axis=1), (1, RET_HEADS)),
            jnp.tile(jnp.concatenate([-sin, sin], axis=1), (1, RET_HEADS)))


def _swap_halves(x):
    outs = []
    for s in range(x.shape[1] // LANE):
        xs = x[:, LANE * s:LANE * (s + 1)]
        lane = lax.broadcasted_iota(jnp.int32, xs.shape, 1)
        outs.append(jnp.where((lane & 32) == 0, pltpu.roll(xs, LANE - 32, axis=1), pltpu.roll(xs, 32, axis=1)))
    return outs[0] if len(outs) == 1 else jnp.concatenate(outs, axis=1)


def _rope(x, cos, sin_signed):
    return x * cos + _swap_halves(x) * sin_signed


def _rope_t(dx, cos, sin_signed):
    return dx * cos + _swap_halves(dx * sin_signed)


def _ret_fwd(proj, cos, sin, gain, name):
    t = proj.shape[0]
    bk = TOK_TILE
    nb = t // bk
    w, wq, wk, mask, g_blk = _ret_consts(bk)

    def body(q_ref, k_ref, v_ref, rg_ref, cos_ref, sin_ref, w_ref, wq_ref, wk_ref, mask_ref, gain_ref,
             opre_ref, og_ref, st_ref, r_ref):
        i = pl.program_id(0)

        @pl.when(i == 0)
        def _():
            r_ref[...] = jnp.zeros_like(r_ref)

        c, s = cos_ref[...], sin_ref[...]
        valid = ((i * bk + lax.broadcasted_iota(jnp.int32, (bk, 1), 0)) >= N_PAD).astype(F32)
        qr = _rope(q_ref[...], c, s)
        kr = _rope(k_ref[...], c, s) * QK_SCALE * valid
        kb = kr.astype(BF16)
        for h in range(RET_HEADS):
            hm = mask_ref[h]
            cols = slice(RET_DV * h, RET_DV * (h + 1))
            vh = v_ref[:, cols].astype(BF16)
            r_prev = r_ref[h]
            st_ref[0, h] = r_prev
            sm = _dot((qr * hm).astype(BF16), kb, NT) * w_ref[h]
            o = _dot(sm.astype(BF16), vh) + _dot((qr * (hm * wq_ref[h])).astype(BF16), r_prev.astype(BF16))
            r_ref[h] = g_blk[h] * r_prev + _dot((kr * wk_ref[h]).astype(BF16), vh, TN)
            opre_ref[:, cols] = o
            rstd = lax.rsqrt(jnp.mean(o * o, axis=-1, keepdims=True) + EPS)
            rg = rg_ref[:, cols]
            og_ref[:, cols] = (o * rstd * gain_ref[:, cols] * (rg * _sigmoid(rg))).astype(BF16)

    full = lambda shape: pl.BlockSpec(shape, lambda i: (0,) * len(shape))
    return _pcall(
        body, name=name, grid=(nb,),
        in_specs=[pl.BlockSpec((bk, RET_QK), lambda i: (i, 0)), pl.BlockSpec((bk, RET_QK), lambda i: (i, 1)),
                  pl.BlockSpec((bk, RET_V), lambda i: (i, 1)), pl.BlockSpec((bk, RET_V), lambda i: (i, 2)),
                  pl.BlockSpec((bk, RET_QK), lambda i: (i, 0)), pl.BlockSpec((bk, RET_QK), lambda i: (i, 0)),
                  full((RET_HEADS, bk, bk)), full((RET_HEADS, bk, 1)), full((RET_HEADS, bk, 1)),
                  full((RET_HEADS, 1, RET_QK)), full((1, RET_V))],
        out_specs=[pl.BlockSpec((bk, RET_V), lambda i: (i, 0)), pl.BlockSpec((bk, RET_V), lambda i: (i, 0)),
                   pl.BlockSpec((1, RET_HEADS, RET_QK, RET_DV), lambda i: (i, 0, 0, 0))],
        out_shape=[jax.ShapeDtypeStruct((t, RET_V), F32), jax.ShapeDtypeStruct((t, RET_V + FOX_W), BF16),
                   jax.ShapeDtypeStruct((nb, RET_HEADS, RET_QK, RET_DV), F32)],
        scratch_shapes=[pltpu.VMEM((RET_HEADS, RET_QK, RET_DV), F32)],
        compiler_params=_params("arbitrary"),
    )(proj, proj, proj, proj, cos, sin, w, wq, wk, mask, gain)


def _ret_bwd(proj, cos, sin, gain, dmixed, opre, states, name):
    t = proj.shape[0]
    bk = TOK_TILE
    nb = t // bk
    w, wq, wk, mask, g_blk = _ret_consts(bk)
    v0, g0 = 2 * RET_QK, 2 * RET_QK + RET_V

    def body(q_ref, k_ref, v_ref, rg_ref, cos_ref, sin_ref, w_ref, wq_ref, wk_ref, mask_ref, gain_ref,
             dog_ref, opre_ref, st_ref, dp_ref, gg_ref, dr_ref):
        step = pl.program_id(0)
        i = nb - 1 - step

        @pl.when(step == 0)
        def _():
            dr_ref[...] = jnp.zeros_like(dr_ref)
            gg_ref[...] = jnp.zeros_like(gg_ref)

        c, s = cos_ref[...], sin_ref[...]
        valid = ((i * bk + lax.broadcasted_iota(jnp.int32, (bk, 1), 0)) >= N_PAD).astype(F32)
        qr = _rope(q_ref[...], c, s)
        kr = _rope(k_ref[...], c, s) * QK_SCALE * valid
        kb = kr.astype(BF16)
        dqr = jnp.zeros((bk, RET_QK), F32)
        dkr = jnp.zeros((bk, RET_QK), F32)
        for h in range(RET_HEADS):
            hm = mask_ref[h]
            cols = slice(RET_DV * h, RET_DV * (h + 1))
            vh = v_ref[:, cols].astype(BF16)
            o = opre_ref[:, cols]
            rstd = lax.rsqrt(jnp.mean(o * o, axis=-1, keepdims=True) + EPS)
            xhat = o * rstd
            rg = rg_ref[:, cols]
            sg = _sigmoid(rg)
            gate = rg * sg
            gn = gain_ref[:, cols]
            dog = dog_ref[:, cols]
            dp_ref[:, g0 + RET_DV * h:g0 + RET_DV * (h + 1)] = (
                dog * xhat * gn * (sg * (1.0 + rg * (1.0 - sg)))).astype(BF16)
            gg_ref[:, cols] += jnp.sum(dog * xhat * gate, axis=0, keepdims=True)
            dxh = dog * gn * gate
            do = (rstd * (dxh - xhat * jnp.mean(dxh * xhat, axis=-1, keepdims=True))).astype(BF16)
            qm = (qr * hm).astype(BF16)
            qw = (qr * (hm * wq_ref[h])).astype(BF16)
            kw = (kr * wk_ref[h]).astype(BF16)
            wh = w_ref[h]
            sm = (_dot(qm, kb, NT) * wh).astype(BF16)
            ds = (_dot(do, vh, NT) * wh).astype(BF16)
            dr = dr_ref[h]
            drb = dr.astype(BF16)
            dp_ref[:, v0 + RET_DV * h:v0 + RET_DV * (h + 1)] = (_dot(sm, do, TN) + _dot(kw, drb)).astype(BF16)
            dqr = dqr + _dot(ds, kb) * hm + _dot(do, st_ref[0, h].astype(BF16), NT) * (hm * wq_ref[h])
            dkr = dkr + _dot(ds, qm, TN) + _dot(vh, drb, NT) * wk_ref[h]
            dr_ref[h] = g_blk[h] * dr + _dot(qw, do, TN)
        dp_ref[:, 0:RET_QK] = _rope_t(dqr, c, s).astype(BF16)
        dp_ref[:, RET_QK:2 * RET_QK] = _rope_t(dkr * (QK_SCALE * valid), c, s).astype(BF16)

    full = lambda shape: pl.BlockSpec(shape, lambda i: (0,) * len(shape))
    rev = lambda col: (lambda i: (nb - 1 - i, col))
    return _pcall(
        body, name=name, grid=(nb,),
        in_specs=[pl.BlockSpec((bk, RET_QK), rev(0)), pl.BlockSpec((bk, RET_QK), rev(1)),
                  pl.BlockSpec((bk, RET_V), rev(1)), pl.BlockSpec((bk, RET_V), rev(2)),
                  pl.BlockSpec((bk, RET_QK), rev(0)), pl.BlockSpec((bk, RET_QK), rev(0)),
                  full((RET_HEADS, bk, bk)), full((RET_HEADS, bk, 1)), full((RET_HEADS, bk, 1)),
                  full((RET_HEADS, 1, RET_QK)), full((1, RET_V)),
                  pl.BlockSpec((bk, RET_V), rev(0)), pl.BlockSpec((bk, RET_V), rev(0)),
                  pl.BlockSpec((1, RET_HEADS, RET_QK, RET_DV), lambda i: (nb - 1 - i, 0, 0, 0))],
        out_specs=[pl.BlockSpec((bk, g0 + RET_V), rev(0)), pl.BlockSpec((1, RET_V), lambda i: (0, 0))],
        out_shape=[jax.ShapeDtypeStruct((t, IN_PAD), BF16), jax.ShapeDtypeStruct((1, RET_V), F32)],
        scratch_shapes=[pltpu.VMEM((RET_HEADS, RET_QK, RET_DV), F32)],
        compiler_params=_params("arbitrary"),
    )(proj, proj, proj, proj, cos, sin, w, wq, wk, mask, gain, dmixed, opre, states)


def _forget_cumsum(proj, bias, name):
    t = proj.shape[0]
    rt = TOK_TILE
    nb = t // rt
    tril = jnp.asarray(np.tril(np.ones((rt, rt))), F32)

    def body(z_ref, b_ref, tril_ref, c_ref, carry_ref):
        i = pl.program_id(0)

        @pl.when(i == 0)
        def _():
            carry_ref[...] = jnp.zeros_like(carry_ref)

        z = z_ref[...] + b_ref[...]
        logf = jnp.minimum(z, 0.0) - jnp.log(1.0 + jnp.exp(-jnp.abs(z)))
        c = lax.dot_general(tril_ref[...], logf, NN, precision=lax.Precision.HIGHEST,
                            preferred_element_type=F32) + carry_ref[...]
        c_ref[...] = c
        carry_ref[...] = c[rt - 1:rt, :]

    return _pcall(
        body, name=name, grid=(nb,),
        in_specs=[pl.BlockSpec((rt, LANE), lambda i: (i, FF_COL_BLOCK)), pl.BlockSpec((1, LANE), lambda i: (0, 0)),
                  pl.BlockSpec((rt, rt), lambda i: (0, 0))],
        out_specs=pl.BlockSpec((rt, LANE), lambda i: (i, 0)),
        out_shape=jax.ShapeDtypeStruct((t, LANE), F32),
        scratch_shapes=[pltpu.VMEM((1, LANE), F32)],
        compiler_params=_params("arbitrary"),
    )(proj, bias, tril)


def _forget_cumsum_bwd(proj, bias, drs, dcs, dproj, name):
    t = proj.shape[0]
    rt = TOK_TILE
    nb = t // rt
    triu = jnp.asarray(np.triu(np.ones((rt, rt))), F32)

    def body(z_ref, b_ref, triu_ref, drs_ref, dcs_ref, dproj_in, dz_ref, gb_ref, carry_ref):
        step = pl.program_id(0)

        @pl.when(step == 0)
        def _():
            carry_ref[...] = jnp.zeros_like(carry_ref)
            gb_ref[...] = jnp.zeros_like(gb_ref)

        dlogf = lax.dot_general(triu_ref[...], drs_ref[...] - dcs_ref[...], NN, precision=lax.Precision.HIGHEST,
                                preferred_element_type=F32) + carry_ref[...]
        carry_ref[...] = dlogf[0:1, :]
        z = z_ref[...] + b_ref[...]
        is_head = lax.broadcasted_iota(jnp.int32, (rt, LANE), 1) < FOX_HEADS
        dz = jnp.where(is_head, dlogf / (1.0 + jnp.exp(z)), 0.0)
        dz_ref[...] = dz.astype(BF16)
        gb_ref[...] += jnp.sum(dz, axis=0, keepdims=True)

    return _pcall(
        body, name=name, grid=(nb,),
        in_specs=[pl.BlockSpec((rt, LANE), lambda i: (nb - 1 - i, FF_COL_BLOCK)),
                  pl.BlockSpec((1, LANE), lambda i: (0, 0)),
                  pl.BlockSpec((rt, rt), lambda i: (0, 0)),
                  pl.BlockSpec((rt, LANE), lambda i: (nb - 1 - i, 0)),
                  pl.BlockSpec((rt, LANE), lambda i: (nb - 1 - i, 0)),
                  pl.BlockSpec(memory_space=pl.ANY)],
        out_specs=[pl.BlockSpec((rt, LANE), lambda i: (nb - 1 - i, FF_COL_BLOCK)),
                   pl.BlockSpec((1, LANE), lambda i: (0, 0))],
        out_shape=[jax.ShapeDtypeStruct(dproj.shape, BF16), jax.ShapeDtypeStruct((1, LANE), F32)],
        input_output_aliases={5: 0},
        scratch_shapes=[pltpu.VMEM((1, LANE), F32)],
        compiler_params=_params("arbitrary"),
    )(proj, bias, triu, drs, dcs, dproj)


FOX_PAIRS = FOX_HEADS // 2
L_ONE_Q = FOX_DH
L_ONE_K = FOX_DH + 3
L_LSE = FOX_DH + 4


def _split3(x):
    hi = x.astype(BF16).astype(F32)
    r = x - hi
    mid = r.astype(BF16).astype(F32)
    return hi, mid, r - mid


def _head_to_low(slab, e):
    return slab if e == 0 else pltpu.roll(slab, FOX_DH, axis=1)


def _pair(a, b, low):
    return jnp.where(low, a, pltpu.roll(b, FOX_DH, axis=1))


def _fox_prep(proj, c, name):
    t = proj.shape[0]
    tq = TOK_TILE

    def body(p_ref, c_ref, qa_ref, ka_ref, va_ref, qt_ref, vt_ref):
        i = pl.program_id(0)
        lane = lax.broadcasted_iota(jnp.int32, (tq, LANE), 1)
        low = lane < FOX_DH
        live = (i * tq + lax.broadcasted_iota(jnp.int32, (tq, 1), 0)) >= N_PAD
        q_tail = jnp.where(lane < L_ONE_Q + 3, 1.0, 0.0)
        k_ones = (lane >= L_ONE_K) & (lane < L_ONE_K + 4)
        v_tail = jnp.where(lane < FOX_DH + 2, 1.0, 0.0)
        bias_parts = _split3(jnp.where(live, -c_ref[...], NEG))
        for pair in range(FOX_PAIRS):
            base = 3 * LANE * pair
            for e in range(2):
                h = 2 * pair + e
                q = _head_to_low(p_ref[:, base:base + LANE], e)
                k = _head_to_low(p_ref[:, base + LANE:base + 2 * LANE], e)
                v = _head_to_low(p_ref[:, base + 2 * LANE:base + 3 * LANE], e)
                hi, mid, lo = [part[:, h:h + 1] for part in bias_parts]
                ka = jnp.where(low, k, jnp.where(k_ones, 1.0, 0.0))
                ka = jnp.where(lane == L_ONE_Q, hi, jnp.where(lane == L_ONE_Q + 1, mid, jnp.where(lane == L_ONE_Q + 2, lo, ka)))
                qa = jnp.where(low, q * QK_SCALE, q_tail)
                va = jnp.where(low, v, v_tail)
                qa_ref[h] = qa.astype(BF16)
                ka_ref[h] = ka.astype(BF16)
                va_ref[h] = va.astype(BF16)
                qt_ref[h] = qa.T.astype(BF16)
                vt_ref[h] = va.T.astype(BF16)

    out = jax.ShapeDtypeStruct((FOX_HEADS, t, LANE), BF16)
    out_t = jax.ShapeDtypeStruct((FOX_HEADS, t // tq, LANE, tq), BF16)
    ospec = pl.BlockSpec((FOX_HEADS, tq, LANE), lambda i: (0, i, 0))
    tspec = pl.BlockSpec((FOX_HEADS, None, LANE, tq), lambda i: (0, i, 0, 0))
    return _pcall(
        body, name=name, grid=(t // tq,),
        in_specs=[pl.BlockSpec((tq, 3 * FOX_W), lambda i: (i, 1)), pl.BlockSpec((tq, LANE), lambda i: (i, 0))],
        out_specs=[ospec, ospec, ospec, tspec, tspec], out_shape=[out, out, out, out_t, out_t],
        compiler_params=_params("parallel"),
    )(proj, c)


STEP_PAIRS = 2
STEP_HEADS = 2 * STEP_PAIRS
FOX_GROUPS = FOX_PAIRS // STEP_PAIRS
FWD_PAIRS = 4
FWD_HEADS = 2 * FWD_PAIRS
FWD_GROUPS = FOX_PAIRS // FWD_PAIRS


def _blockdiag(a, b):
    z = jnp.zeros_like(a)
    return jnp.concatenate([jnp.concatenate([a, z], axis=1), jnp.concatenate([z, b], axis=1)], axis=0)


def _fox_fwd(qt, ka, vt, mixed, name):
    nh, nq, tq, _ = ka.shape
    t = nq * tq

    def body(qt_ref, ka_ref, vt_ref, mixed_in, mixed_ref, o_ref, lse_ref):
        i = pl.program_id(1)
        lane = lax.broadcasted_iota(jnp.int32, (tq, LANE), 1)
        key_le_query = lax.broadcasted_iota(jnp.int32, (tq, tq), 0) <= lax.broadcasted_iota(jnp.int32, (tq, tq), 1)

        def logits(j):
            return [_dot(ka_ref[h, j], qt_ref[h]) for h in range(FWD_HEADS)]

        def update(j, scores, carry, diagonal):
            new = []
            for h in range(FWD_HEADS):
                m, acc = carry[h]
                s = jnp.where(key_le_query, scores[h], NEG) if diagonal else scores[h]
                m_new = jnp.maximum(m, jnp.max(s, axis=0, keepdims=True))
                p = jnp.exp(s - m_new).astype(BF16)
                new.append((m_new, jnp.exp(m - m_new) * acc + _dot(vt_ref[h, j], p)))
            return tuple(new)

        init = tuple((jnp.full((1, tq), NEG, F32), jnp.zeros((LANE, tq), F32)) for _ in range(FWD_HEADS))
        carry = lax.fori_loop(0, i, lambda j, cr: update(j, logits(j), cr, False), init)
        outs, lse_rows = [], []
        for m, acc in update(i, logits(i), carry, True):
            l = acc[FOX_DH:FOX_DH + 1, :]
            outs.append((acc / l).T)
            lse_rows.append(m + jnp.log(l))
        lse_rows.append(jnp.zeros((LANE - FWD_HEADS, tq), F32))
        o_all = jnp.concatenate([_pair(outs[2 * c], outs[2 * c + 1], lane < FOX_DH) for c in range(FWD_PAIRS)], axis=1)
        mixed_ref[...] = o_all.astype(BF16)
        o_ref[...] = o_all
        lse_ref[...] = jnp.concatenate(lse_rows, axis=0).T

    width = FWD_PAIRS * LANE
    whole = pl.BlockSpec((FWD_HEADS, nq, tq, LANE), lambda g, i: (g, 0, 0, 0), pipeline_mode=pl.Buffered(1))
    whole_t = pl.BlockSpec((FWD_HEADS, nq, LANE, tq), lambda g, i: (g, 0, 0, 0), pipeline_mode=pl.Buffered(1))
    return _pcall(
        body, name=name, grid=(FWD_GROUPS, nq),
        in_specs=[pl.BlockSpec((FWD_HEADS, None, LANE, tq), lambda g, i: (g, i, 0, 0)), whole, whole_t,
                  pl.BlockSpec(memory_space=pl.ANY)],
        out_specs=[pl.BlockSpec((tq, width), lambda g, i: (i, RET_V // width + g)),
                   pl.BlockSpec((tq, width), lambda g, i: (i, g)),
                   pl.BlockSpec((None, tq, LANE), lambda g, i: (g, i, 0))],
        out_shape=[jax.ShapeDtypeStruct(mixed.shape, BF16), jax.ShapeDtypeStruct((t, FOX_W), F32),
                   jax.ShapeDtypeStruct((FWD_GROUPS, t, LANE), F32)],
        input_output_aliases={3: 0},
        compiler_params=_params("parallel", "parallel"),
    )(qt, ka, vt, mixed)


def _fox_prep_bwd(dmixed, o_fox, lse, qa, name):
    t = dmixed.shape[0]
    tq = TOK_TILE

    def body(dm_ref, o_ref, lse_ref, qa_ref, qab_ref, doa_ref):
        i = pl.program_id(0)
        lane = lax.broadcasted_iota(jnp.int32, (tq, LANE), 1)
        low = lane < FOX_DH
        live = (i * tq + lax.broadcasted_iota(jnp.int32, (tq, 1), 0)) >= N_PAD
        lse_parts = [_split3(jnp.where(live, -lse_ref[grp], 0.0)) for grp in range(FWD_GROUPS)]
        for pair in range(FOX_PAIRS):
            cols = slice(LANE * pair, LANE * (pair + 1))
            d_slab = dm_ref[:, cols]
            prod = d_slab * o_ref[:, cols]
            for e in range(2):
                h = 2 * pair + e
                nd = -jnp.sum(jnp.where(low, _head_to_low(prod, e), 0.0), axis=-1, keepdims=True)
                nd_hi = nd.astype(BF16).astype(F32)
                doa = jnp.where(low, _head_to_low(d_slab, e), 0.0)
                doa = jnp.where(lane == FOX_DH, nd_hi, jnp.where(lane == FOX_DH + 1, nd - nd_hi, doa))
                doa_ref[h] = doa.astype(BF16)
                lane_h = h % FWD_HEADS
                hi, mid, lo = [part[:, lane_h:lane_h + 1] for part in lse_parts[h // FWD_HEADS]]
                qab = qa_ref[h].astype(F32)
                qab = jnp.where(lane == L_LSE, hi, jnp.where(lane == L_LSE + 1, mid, jnp.where(lane == L_LSE + 2, lo, qab)))
                qab_ref[h] = qab.astype(BF16)

    out = jax.ShapeDtypeStruct((FOX_HEADS, t, LANE), BF16)
    hspec = pl.BlockSpec((FOX_HEADS, tq, LANE), lambda i: (0, i, 0))
    return _pcall(
        body, name=name, grid=(t // tq,),
        in_specs=[pl.BlockSpec((tq, FOX_W), lambda i: (i, 1)), pl.BlockSpec((tq, FOX_W), lambda i: (i, 0)),
                  pl.BlockSpec((FWD_GROUPS, tq, LANE), lambda i: (0, i, 0)), hspec],
        out_specs=[hspec, hspec], out_shape=[out, out],
        compiler_params=_params("parallel"),
    )(dmixed, o_fox, lse, qa)


def _fox_bwd(qab, doa, ka, va, dproj, name):
    nh, nq, tq, _ = qab.shape
    t = nq * tq
    slab = 3 * LANE * STEP_PAIRS
    group0 = (2 * RET_QK + 2 * RET_V) // slab

    def body(qab_ref, doa_ref, ka_ref, va_ref, dproj_in, dp_ref, drs_ref, dcs_ref, dq_ref):
        g, j = pl.program_id(0), pl.program_id(1)

        @pl.when((g == 0) & (j == 0))
        def _():
            drs_ref[...] = jnp.zeros_like(drs_ref)
            dcs_ref[...] = jnp.zeros_like(dcs_ref)

        @pl.when(j == 0)
        def _():
            dq_ref[...] = jnp.zeros_like(dq_ref)

        lane = lax.broadcasted_iota(jnp.int32, (tq, LANE), 1)
        low = lane < FOX_DH
        key_le_query = lax.broadcasted_iota(jnp.int32, (tq, tq), 0) <= lax.broadcasted_iota(jnp.int32, (tq, tq), 1)

        def by_head(c, a, b, col):
            h = STEP_HEADS * g + 2 * c
            return jnp.where(lane == h, a[:, col:col + 1], jnp.where(lane == h + 1, b[:, col:col + 1], 0.0))


        def step(i, carry, diagonal):
            st = [_dot(ka_ref[h], qab_ref[h, i], NT) for h in range(STEP_HEADS)]
            dpt = [_dot(va_ref[h], doa_ref[h, i], NT) for h in range(STEP_HEADS)]
            new = []
            for h in range(STEP_HEADS):
                p = jnp.exp(st[h])
                if diagonal:
                    p = jnp.where(key_le_query, p, 0.0)
                ds = (p * dpt[h]).astype(BF16)
                dq_ref[h, i] += _dot(ds, ka_ref[h], TN)
                dk, dv = carry[h]
                new.append((dk + _dot(ds, qab_ref[h, i]), dv + _dot(p.astype(BF16), doa_ref[h, i])))
            return tuple(new)

        zero = jnp.zeros((tq, LANE), F32)
        carry = step(j, tuple((zero, zero) for _ in range(STEP_HEADS)), True)
        carry = lax.fori_loop(j + 1, nq, lambda i, cr: step(i, cr, False), carry)
        rows = pl.ds(pl.multiple_of(j * tq, tq), tq)
        for c in range(STEP_PAIRS):
            (dka, dva), (dkb, dvb) = carry[2 * c], carry[2 * c + 1]
            c0 = 3 * LANE * c
            dp_ref[rows, c0 + LANE:c0 + 2 * LANE] = _pair(dka, dkb, low).astype(BF16)
            dp_ref[rows, c0 + 2 * LANE:c0 + 3 * LANE] = _pair(dva, dvb, low).astype(BF16)
            dcs_ref[rows, :] += by_head(c, dka, dkb, L_ONE_Q)

        @pl.when(j == nq - 1)
        def _():
            for c in range(STEP_PAIRS):
                for blk in range(nq):
                    r = slice(blk * tq, (blk + 1) * tq)
                    a, b = dq_ref[2 * c, blk], dq_ref[2 * c + 1, blk]
                    dp_ref[r, 3 * LANE * c:3 * LANE * c + LANE] = (_pair(a, b, low) * QK_SCALE).astype(BF16)
                    drs_ref[r, :] += by_head(c, a, b, L_ONE_K)

    whole = pl.BlockSpec((STEP_HEADS, nq, tq, LANE), lambda g, j: (g, 0, 0, 0), pipeline_mode=pl.Buffered(1))
    blk = pl.BlockSpec((STEP_HEADS, None, tq, LANE), lambda g, j: (g, j, 0, 0))
    sums = pl.BlockSpec((t, LANE), lambda g, j: (0, 0), pipeline_mode=pl.Buffered(1))
    return _pcall(
        body, name=name, grid=(FOX_GROUPS, nq),
        in_specs=[whole, whole, blk, blk, pl.BlockSpec(memory_space=pl.ANY)],
        out_specs=[pl.BlockSpec((t, slab), lambda g, j: (0, group0 + g)), sums, sums],
        out_shape=[jax.ShapeDtypeStruct(dproj.shape, BF16), jax.ShapeDtypeStruct((t, LANE), F32),
                   jax.ShapeDtypeStruct((t, LANE), F32)],
        input_output_aliases={4: 0},
        scratch_shapes=[pltpu.VMEM((STEP_HEADS, nq, tq, LANE), F32)],
        compiler_params=_params("arbitrary", "arbitrary"),
    )(qab, doa, ka, va, dproj)


HALO = 8


def _rows_ext(ref, r0, rows, t, before, after):
    lo, hi = r0 - before, r0 + rows + after
    width = ref.shape[-1]
    parts = []
    if lo < 0:
        parts.append(jnp.zeros((-lo, width), F32))
    parts.append(ref[max(lo, 0):min(hi, t), :].astype(F32))
    if hi > t:
        parts.append(jnp.zeros((hi - t, width), F32))
    return parts[0] if len(parts) == 1 else jnp.concatenate(parts, axis=0)


def _conv_taps(a_ext, r0_ext, cw_ref, cb_ref):
    n = a_ext.shape[0]
    if r0_ext < N_PAD:
        row = r0_ext + lax.broadcasted_iota(jnp.int32, (n, 1), 0)
        a_ext = jnp.where(row >= N_PAD, a_ext, 0.0)
    a1 = pltpu.roll(a_ext, 1, axis=0)
    a2 = pltpu.roll(a_ext, 2, axis=0)
    acc = cb_ref[...] + a2 * cw_ref[0:1, :] + a1 * cw_ref[1:2, :] + a_ext * cw_ref[2:3, :]
    return a_ext, a1, a2, acc


FF_COLS = 256


def _up_conv_fwd(n2, w_up_t, conv_w8, conv_b, name):
    t, d = n2.shape
    f = w_up_t.shape[1]
    rows = TOK_TILE
    starts = list(range(0, t, rows))

    def body(n_ref, wa_ref, wb_ref, cw_ref, cb_ref, up_ref, g_ref):
        wa, wb = wa_ref[...], wb_ref[...]

        def project(r0):
            n_rows = n_ref[r0:r0 + rows, :]
            up_ref[0, r0:r0 + rows, :] = _dot(n_rows, wa, NT)
            up_ref[1, r0:r0 + rows, :] = _dot(n_rows, wb, NT)

        def activate(r0):
            a_ext = _rows_ext(up_ref.at[0], r0, rows, t, HALO, 0)
            _, _, _, acc = _conv_taps(a_ext, r0 - HALO, cw_ref, cb_ref)
            acc = acc[HALO:, :]
            g_ref[r0:r0 + rows, :] = (acc * _sigmoid(acc) * up_ref[1, r0:r0 + rows, :]).astype(BF16)

        project(starts[0])
        for r0, r_next in zip(starts, starts[1:] + [None]):
            if r_next is not None:
                project(r_next)
            activate(r0)

    return _pcall(
        body, name=name, grid=(f // FF_COLS,),
        in_specs=[pl.BlockSpec((t, d), lambda j: (0, 0), pipeline_mode=pl.Buffered(1)),
                  pl.BlockSpec((None, FF_COLS, d), lambda j: (0, j, 0)), pl.BlockSpec((None, FF_COLS, d), lambda j: (1, j, 0)),
                  pl.BlockSpec((8, FF_COLS), lambda j: (0, j)), pl.BlockSpec((1, FF_COLS), lambda j: (0, j))],
        out_specs=[pl.BlockSpec((2, t, FF_COLS), lambda j: (0, 0, j)), pl.BlockSpec((t, FF_COLS), lambda j: (0, j))],
        out_shape=[jax.ShapeDtypeStruct((2, t, f), F32), jax.ShapeDtypeStruct((t, f), BF16)],
        compiler_params=_params("parallel"),
    )(n2, w_up_t, w_up_t, conv_w8, conv_b)


def _dg_conv_bwd(up, conv_w8, conv_b, dh2, w_down, name):
    _, t, f = up.shape
    d = dh2.shape[1]
    rows = TOK_TILE
    starts = list(range(0, t, rows))

    def body(a_ref, b_ref, cw_ref, cb_ref, dh_ref, wd_ref, dup_ref, gcw_ref, gcb_ref, dg_ref):
        wd = wd_ref[...]

        def project(r0):
            dg_ref[r0:r0 + rows, :] = _dot(dh_ref[r0:r0 + rows, :], wd, NT)

        gw = [jnp.zeros((1, FF_COLS), F32) for _ in range(3)]
        gb = jnp.zeros((1, FF_COLS), F32)
        project(starts[0])
        for r0, r_next in zip(starts, starts[1:] + [None]):
            if r_next is not None:
                project(r_next)
            a_ext = _rows_ext(a_ref, r0, rows, t, HALO, HALO)
            b_ext = _rows_ext(b_ref, r0, rows, t, HALO, HALO)
            dg_ext = _rows_ext(dg_ref, r0, rows, t, HALO, HALO)
            a0, a1, a2, acc = _conv_taps(a_ext, r0 - HALO, cw_ref, cb_ref)
            sg = _sigmoid(acc)
            dacc = dg_ext * b_ext * (sg * (1.0 + acc * (1.0 - sg)))
            n = dacc.shape[0]
            da = (dacc * cw_ref[2:3, :] + pltpu.roll(dacc, n - 1, axis=0) * cw_ref[1:2, :]
                  + pltpu.roll(dacc, n - 2, axis=0) * cw_ref[0:1, :])
            core = slice(HALO, HALO + rows)
            da = da[core, :]
            if r0 < N_PAD:
                row = r0 + lax.broadcasted_iota(jnp.int32, (rows, 1), 0)
                da = jnp.where(row >= N_PAD, da, 0.0)
            dup_ref[0, r0:r0 + rows, :] = da.astype(BF16)
            dup_ref[1, r0:r0 + rows, :] = (dg_ext * acc * sg)[core, :].astype(BF16)
            dacc_c = dacc[core, :]
            gw[0] = gw[0] + jnp.sum(dacc_c * a2[core, :], axis=0, keepdims=True)
            gw[1] = gw[1] + jnp.sum(dacc_c * a1[core, :], axis=0, keepdims=True)
            gw[2] = gw[2] + jnp.sum(dacc_c * a0[core, :], axis=0, keepdims=True)
            gb = gb + jnp.sum(dacc_c, axis=0, keepdims=True)
        gcw_ref[...] = jnp.zeros((8, FF_COLS), F32)
        for tap in range(3):
            gcw_ref[tap:tap + 1, :] = gw[tap]
        gcb_ref[...] = gb

    return _pcall(
        body, name=name, grid=(f // FF_COLS,),
        in_specs=[pl.BlockSpec((None, t, FF_COLS), lambda j: (0, 0, j)), pl.BlockSpec((None, t, FF_COLS), lambda j: (1, 0, j)),
                  pl.BlockSpec((8, FF_COLS), lambda j: (0, j)), pl.BlockSpec((1, FF_COLS), lambda j: (0, j)),
                  pl.BlockSpec((t, d), lambda j: (0, 0), pipeline_mode=pl.Buffered(1)),
                  pl.BlockSpec((FF_COLS, d), lambda j: (j, 0))],
        out_specs=[pl.BlockSpec((2, t, FF_COLS), lambda j: (0, 0, j)), pl.BlockSpec((8, FF_COLS), lambda j: (0, j)),
                   pl.BlockSpec((1, FF_COLS), lambda j: (0, j))],
        out_shape=[jax.ShapeDtypeStruct((2, t, f), BF16), jax.ShapeDtypeStruct((8, f), F32),
                   jax.ShapeDtypeStruct((1, f), F32)],
        scratch_shapes=[pltpu.VMEM((t, FF_COLS), F32)],
        compiler_params=_params("parallel"),
    )(up, up, conv_w8, conv_b, dh2, w_down)


def _exchange(arrays, kinds, name, after=None):
    n = len(arrays)
    npeer = N_DEV - 1
    n_in = n + int(after is not None)

    def body(*refs):
        ins, outs = refs[:n], refs[n_in:n_in + n]
        send_sems, recv_sems, local_sems = refs[n_in + n:]
        x, y, c = lax.axis_index("x"), lax.axis_index("y"), lax.axis_index("c")
        me = 4 * x + 2 * y + c
        copies, locals_ = [], []
        for a in range(n):
            gather = kinds[a] == "gather"
            own = pltpu.make_async_copy(ins[a] if gather else ins[a].at[me], outs[a].at[me], local_sems.at[a])
            own.start()
            locals_.append(own)
            for d in range(1, N_DEV):
                px = 1 - x if d & 4 else x
                py = 1 - y if d & 2 else y
                pc = 1 - c if d & 1 else c
                src = ins[a] if gather else ins[a].at[4 * px + 2 * py + pc]
                cp = pltpu.make_async_remote_copy(
                    src_ref=src, dst_ref=outs[a].at[me],
                    send_sem=send_sems.at[a * npeer + d - 1], recv_sem=recv_sems.at[a * npeer + d - 1],
                    device_id=(px, py, pc), device_id_type=pl.DeviceIdType.MESH)
                cp.start()
                copies.append(cp)
        for cp in copies:
            cp.wait_recv()
        for cp in copies:
            cp.wait_send()
        for own in locals_:
            own.wait()

    out_shape = [jax.ShapeDtypeStruct((N_DEV,) + (a.shape if k == "gather" else a.shape[1:]), a.dtype)
                 for a, k in zip(arrays, kinds)]
    return _pcall(
        body, name=name,
        in_specs=[pl.BlockSpec(memory_space=pl.ANY)] * n_in,
        out_specs=[pl.BlockSpec(memory_space=pl.ANY)] * n,
        out_shape=out_shape,
        scratch_shapes=[pltpu.SemaphoreType.DMA((n * npeer,)), pltpu.SemaphoreType.DMA((n * npeer,)),
                        pltpu.SemaphoreType.DMA((n,))],
        compiler_params=pltpu.CompilerParams(has_side_effects=True),
    )(*arrays, *([] if after is None else [after]))


def _peer_copies(srcs, lands, kinds, send_sems, recv_sems):
    x, y, c = lax.axis_index("x"), lax.axis_index("y"), lax.axis_index("c")
    me = 4 * x + 2 * y + c
    copies = []
    for a in range(len(srcs)):
        for d in range(1, N_DEV):
            px = 1 - x if d & 4 else x
            py = 1 - y if d & 2 else y
            pc = 1 - c if d & 1 else c
            k = a * (N_DEV - 1) + d - 1
            copies.append(pltpu.make_async_remote_copy(
                src_ref=srcs[a] if kinds[a] == "gather" else srcs[a].at[4 * px + 2 * py + pc], dst_ref=lands[a].at[me],
                send_sem=send_sems.at[k], recv_sem=recv_sems.at[k],
                device_id=(px, py, pc), device_id_type=pl.DeviceIdType.MESH))
    return copies


def _exchange_start(arrays, kinds, name, after=None):
    n = len(arrays)
    nsem = n * (N_DEV - 1)
    hbm = pl.BlockSpec(memory_space=pltpu.HBM)
    sem = pl.BlockSpec(memory_space=pltpu.SEMAPHORE)
    land_shapes = [(N_DEV,) + (a.shape if k == "gather" else a.shape[1:]) for a, k in zip(arrays, kinds)]

    n_in = 2 * n + int(after is not None)

    def body(*refs):
        srcs, lands = refs[:n], refs[n:2 * n]
        send_sems, recv_sems = refs[n_in], refs[n_in + 1]
        token = refs[-1]
        for cp in _peer_copies(srcs, lands, kinds, send_sems, recv_sems):
            cp.start()
        token[...] = jnp.zeros_like(token)

    operands = [pltpu.with_memory_space_constraint(a, pltpu.HBM) for a in arrays]
    operands += [pltpu.with_memory_space_constraint(lax.empty(s, a.dtype), pltpu.HBM) for s, a in zip(land_shapes, arrays)]
    operands += [] if after is None else [after]
    out = _pcall(
        body, name=name,
        in_specs=[hbm] * (2 * n) + ([] if after is None else [pl.BlockSpec(memory_space=pl.ANY)]),
        out_specs=[sem, sem] + [hbm] * (2 * n) + [pl.BlockSpec(memory_space=pltpu.VMEM)],
        out_shape=[pltpu.SemaphoreType.DMA((nsem,)), pltpu.SemaphoreType.DMA((nsem,))]
        + [pltpu.HBM(a.shape, a.dtype) for a in arrays]
        + [pltpu.HBM(s, a.dtype) for s, a in zip(land_shapes, arrays)]
        + [jax.ShapeDtypeStruct((8, LANE), F32)],
        input_output_aliases={k: 2 + k for k in range(2 * n)},
        compiler_params=pltpu.CompilerParams(has_side_effects=pltpu.SideEffectType.DATAFLOW_SIDE_EFFECTING),
    )(*operands)
    return out[0], out[1], list(out[2:2 + n]), list(out[2 + n:2 + 2 * n]), out[-1]


def _exchange_wait(started, kinds, after, name, fill_own=True):
    send_sems, recv_sems, srcs, lands, _ = started
    n = len(srcs)
    hbm = pl.BlockSpec(memory_space=pltpu.HBM)
    sem = pl.BlockSpec(memory_space=pltpu.SEMAPHORE)

    def body(*refs):
        src_refs, land_refs = refs[:n], refs[n:2 * n]
        copies = _peer_copies(src_refs, land_refs, kinds, refs[2 * n], refs[2 * n + 1])
        for cp in copies:
            cp.wait_send()
        for cp in copies:
            cp.wait_recv()

    out = _pcall(
        body, name=name,
        in_specs=[hbm] * (2 * n) + [sem, sem, pl.BlockSpec(memory_space=pl.ANY)],
        out_specs=[hbm] * (2 * n),
        out_shape=[pltpu.HBM(a.shape, a.dtype) for a in srcs + lands],
        input_output_aliases={k: k for k in range(2 * n)},
        compiler_params=pltpu.CompilerParams(has_side_effects=pltpu.SideEffectType.DATAFLOW_SIDE_EFFECTING),
    )(*srcs, *lands, send_sems, recv_sems, after)
    if not fill_own:
        return list(out[:n]), list(out[n:])
    me = 4 * lax.axis_index("x") + 2 * lax.axis_index("y") + lax.axis_index("c")
    filled = []
    for src, land, kind in zip(out[:n], out[n:], kinds):
        own = src if kind == "gather" else lax.dynamic_index_in_dim(src, me, axis=0, keepdims=False)
        filled.append(lax.dynamic_update_slice(land, own[None], (me,) + (0,) * own.ndim))
    return filled


def _sum_slots(slots, name, rows_tile):
    nd, r, c = slots.shape

    def body(s_ref, o_ref):
        acc = s_ref[0].astype(F32)
        for p in range(1, nd):
            acc = acc + s_ref[p].astype(F32)
        o_ref[...] = acc

    return _pcall(
        body, name=name, grid=(r // rows_tile,),
        in_specs=[pl.BlockSpec((nd, rows_tile, c), lambda i: (0, i, 0))],
        out_specs=pl.BlockSpec((rows_tile, c), lambda i: (i, 0)),
        out_shape=jax.ShapeDtypeStruct((r, c), F32),
        compiler_params=_params("parallel"),
    )(slots)


def _sum_slots_small(slot_arrays, own_arrays, name):
    n = len(slot_arrays)

    def body(*refs):
        me = 4 * lax.axis_index("x") + 2 * lax.axis_index("y") + lax.axis_index("c")
        for s_ref, own_ref, o_ref in zip(refs[:n], refs[n:2 * n], refs[2 * n:]):
            acc = jnp.where(me == 0, own_ref[...], s_ref[0])
            for p in range(1, s_ref.shape[0]):
                acc = acc + jnp.where(me == p, own_ref[...], s_ref[p])
            o_ref[...] = acc

    return _pcall(body, name=name, out_shape=[jax.ShapeDtypeStruct(a.shape[1:], F32) for a in slot_arrays])(
        *slot_arrays, *own_arrays)


def _adamw_update(w_ref, g_ref, m_ref, v_ref, d_ref, nm_ref, nv_ref):
    gr = g_ref[...]
    nm = ADAM_B1 * m_ref[...] + (1.0 - ADAM_B1) * gr
    nv = ADAM_B2 * v_ref[...] + (1.0 - ADAM_B2) * (gr * gr)
    m_hat = nm / (1.0 - ADAM_B1 ** ADAM_STEP)
    v_hat = nv / (1.0 - ADAM_B2 ** ADAM_STEP)
    d_ref[...] = -ADAM_LR * (m_hat / (jnp.sqrt(v_hat) + ADAM_EPS) + ADAM_WD * w_ref[...])
    nm_ref[...] = nm
    nv_ref[...] = nv


def _adamw_small(ws, gs, ms, vs, name):
    n = len(ws)

    def body(*refs):
        ins, outs = refs[:4 * n], refs[4 * n:]
        for k in range(n):
            _adamw_update(ins[k], ins[n + k], ins[2 * n + k], ins[3 * n + k], outs[k], outs[n + k], outs[2 * n + k])

    shapes = [jax.ShapeDtypeStruct(w.shape, F32) for w in ws]
    out = _pcall(body, name=name, out_shape=shapes * 3)(*ws, *gs, *ms, *vs)
    return list(out[:n]), list(out[n:2 * n]), list(out[2 * n:])


def _adamw(w, g, m, v, name, rows_tile):
    r, c = w.shape
    body = lambda *refs: _adamw_update(*refs)
    spec = pl.BlockSpec((rows_tile, c), lambda i: (i, 0))
    shp = jax.ShapeDtypeStruct((r, c), F32)
    return _pcall(
        body, name=name, grid=(r // rows_tile,), in_specs=[spec] * 4, out_specs=[spec] * 3, out_shape=[shp] * 3,
        compiler_params=_params("parallel"),
    )(w, g, m, v)


F0 = 2 * RET_QK + 2 * RET_V


def _to_internal_rows(w_t):
    cols = w_t.shape[1]
    fox = w_t[F0:F0 + 3 * FOX_W].reshape(3, FOX_PAIRS, LANE, cols).transpose(1, 0, 2, 3).reshape(3 * FOX_W, cols)
    tail = jnp.zeros((IN_PAD - IN_WIDTH, cols), w_t.dtype)
    return jnp.concatenate([w_t[:F0], fox, w_t[F0 + 3 * FOX_W:], tail], axis=0)


def _from_internal_rows(g_t):
    cols = g_t.shape[1]
    fox = g_t[F0:F0 + 3 * FOX_W].reshape(FOX_PAIRS, 3, LANE, cols).transpose(1, 0, 2, 3).reshape(3 * FOX_W, cols)
    return jnp.concatenate([g_t[:F0], fox, g_t[F0 + 3 * FOX_W:F0 + 3 * FOX_W + FOX_HEADS]], axis=0)


def _local_step(x, target, meta, attn_g, fox_b, ret_g, ffn_g, conv_w8, conv_b, final_g,
                first_weight, late_weights, ffn_grads_ready, out_grad_ready, in_grad_ready):
    seq, d = x.shape
    t = seq + PREFIX
    tm = TOK_TILE
    nq = t // tm
    fox_b128 = jnp.pad(fox_b, ((0, 0), (0, LANE - FOX_HEADS)))

    h0, n1 = _prep_norm(x, meta, attn_g, "prep_norm")
    w_in_t = first_weight(n1)
    proj = _mm_simple(n1, w_in_t, mode="nt", tm=tm, tn=IN_PAD, tk=d, out_dtype=F32, name="mm_in")
    cos, sin = _rope_tables(t)
    o_pre, mixed, states = _ret_fwd(proj, cos, sin, ret_g, "ret_fwd")
    c = _forget_cumsum(proj, fox_b128, "forget_cumsum")
    qa, ka, va, qt, vt = _fox_prep(proj, c, "fox_prep")
    by_block = lambda a: a.reshape(FOX_HEADS, nq, tm, LANE)
    mixed, o_fox, lse = _fox_fwd(qt, by_block(ka), vt, mixed, "fox_fwd")
    w_out, w_up_t, w_down = late_weights(o_fox)
    tile = pl.BlockSpec((tm, d), lambda i: (i, 0))
    row_vec = pl.BlockSpec((1, d), lambda i: (0, 0))
    resident = lambda shape: pl.BlockSpec(shape, lambda i: (0,) * len(shape), pipeline_mode=pl.Buffered(1))
    acts = lambda dtype: jax.ShapeDtypeStruct((t, d), dtype)
    vec = jax.ShapeDtypeStruct((1, d), F32)

    def residual_and_norm(i, acc, ins, outs):
        h = acc + ins[0][...]
        outs[0][...] = h
        outs[1][...] = (h * lax.rsqrt(jnp.mean(h * h, axis=-1, keepdims=True) + EPS) * ins[1][...]).astype(BF16)

    h1, n2 = _matmul_rows([mixed], [tile], [w_out], [resident((d, d))], [h0, ffn_g], [tile, row_vec],
                          [tile, tile], [acts(F32), acts(BF16)], residual_and_norm, mode="nn", steps=nq, name="mm_out_norm")
    nf = D_FF // 1408
    up, g = _up_conv_fwd(n2, w_up_t, conv_w8, conv_b, "up_conv_fwd")

    def residual_loss_bwd(i, acc, ins, outs):
        loss_ref, dh_ref, dhb_ref, gg_ref = outs
        part, dh, gg = _loss_tile(i, acc + ins[0][...], jnp.concatenate([ins[1][...], ins[2][...], ins[3][...]], axis=0),
                                  ins[4][...])
        _accumulate(loss_ref, i, jnp.broadcast_to(part, loss_ref.shape))
        dh_ref[...] = dh
        dhb_ref[...] = dh.astype(BF16)
        _accumulate(gg_ref, i, gg)

    loss_tile, dh2, dh2_b, g_final = _matmul_rows(
        [g], [pl.BlockSpec((tm, D_FF), lambda i: (i, 0))], [w_down], [resident((D_FF, d))],
        [h1, target, target, target, final_g], [tile] + _shifted_row_specs(d) + [row_vec],
        [pl.BlockSpec((8, LANE), lambda i: (0, 0)), tile, tile, row_vec],
        [jax.ShapeDtypeStruct((8, LANE), F32), acts(F32), acts(BF16), vec], residual_loss_bwd,
        mode="nn", steps=nq, name="mm_down_loss")

    tkw = 2112 if t % 2112 == 0 else tm
    gw_down = _mm_simple(g, dh2_b, mode="tn", tm=1408, tn=d, tk=tkw, out_dtype=BF16, name="mm_gw_down")
    dup, g_conv_w8, g_conv_b = _dg_conv_bwd(up, conv_w8, conv_b, dh2_b, w_down, "dg_conv_bwd")

    half = lambda p: pl.BlockSpec((None, tm, D_FF), lambda i: (p, i, 0))
    half_w = lambda p: pl.BlockSpec((None, D_FF, d), lambda i: (p, 0, 0), pipeline_mode=pl.Buffered(1))
    gw_up_t = _matmul(
        dup, n2, mode="tn", grid=(2 * nf, 1, t // tkw),
        a_spec=pl.BlockSpec((None, tkw, 1408), lambda i, j, k: (i // nf, k, i % nf)),
        b_spec=pl.BlockSpec((tkw, d), lambda i, j, k: (k, 0)),
        o_spec=pl.BlockSpec((1408, d), lambda i, j, k: (i, 0)),
        out_shape=jax.ShapeDtypeStruct((2 * D_FF, d), BF16), name="mm_gw_up")
    def norm_bwd_and_mixer_grad(i, acc, ins, outs):
        dh, gg = _rms_bwd_tile(acc, ins[0][...], ins[1][...], ins[2][...])
        outs[0][...] = dh
        _accumulate(outs[1], i, gg)
        outs[2][...] = _dot(dh.astype(BF16), ins[3][...], NT)

    dh1, g_ffn, dmixed = _matmul_rows(
        [dup, dup], [half(0), half(1)], [w_up_t, w_up_t], [half_w(0), half_w(1)],
        [h1, ffn_g, dh2, w_out], [tile, row_vec, tile, resident((d, d))], [tile, row_vec, tile],
        [acts(F32), vec, acts(F32)], norm_bwd_and_mixer_grad,
        mode="nn", steps=nq, name="mm_dn2_norm_bwd", after=ffn_grads_ready(gw_down, gw_up_t))
    gw_out = _mm_simple(mixed, dh1, mode="tn", tm=d, tn=d, tk=tkw, out_dtype=BF16, name="mm_gw_out")
    dproj, g_ret = _ret_bwd(proj, cos, sin, ret_g + out_grad_ready(gw_out), dmixed, o_pre, states, "ret_bwd")
    qab, doa = _fox_prep_bwd(dmixed, o_fox, lse, qa, "fox_prep_bwd")
    dproj, drs, dcs = _fox_bwd(by_block(qab), by_block(doa), by_block(ka), by_block(va), dproj, "fox_bwd")
    dproj, g_fox_b = _forget_cumsum_bwd(proj, fox_b128, drs, dcs, dproj, "forget_cumsum_bwd")
    gw_in_t = _mm_simple(dproj, n1, mode="tn", tm=640, tn=d, tk=tkw, out_dtype=BF16, name="mm_gw_in")
    sent = in_grad_ready(gw_in_t)
    def input_grads(i, acc, ins, outs):
        gx_ref, gmeta_ref, gg_ref, buf_ref, sems = outs
        dh, gg = _rms_bwd_tile(acc, ins[0][...], ins[1][...], ins[2][...])
        _accumulate(gg_ref, i, gg)
        slot = i % 2

        def first_copy():
            return pltpu.make_async_copy(buf_ref.at[0, pl.ds(PREFIX, tm - PREFIX)], gx_ref.at[pl.ds(0, tm - PREFIX)],
                                         sems.at[0])

        def tile_copy(tile, buf_slot):
            rows = pl.ds(pl.multiple_of(tile * tm - PREFIX, PREFIX), tm)
            return pltpu.make_async_copy(buf_ref.at[buf_slot], gx_ref.at[rows], sems.at[buf_slot])

        @pl.when(i == 1)
        def _():
            first_copy().wait()

        @pl.when(i >= 2)
        def _():
            tile_copy(i - 1, 1 - slot).wait()

        buf_ref[slot] = dh

        @pl.when(i == 0)
        def _():
            gmeta_ref[...] = dh[N_PAD:PREFIX, :]
            first_copy().start()

        @pl.when(i > 0)
        def _():
            tile_copy(i, slot).start()

        @pl.when(i == nq - 1)
        def _():
            tile_copy(i, slot).wait()

    grad_x, g_meta, g_attn = _matmul_rows(
        [dproj], [pl.BlockSpec((tm, IN_PAD), lambda i: (i, 0))], [w_in_t], [resident((IN_PAD, d))],
        [h0, attn_g, dh1], [tile, row_vec, tile],
        [pl.BlockSpec(memory_space=pl.ANY), pl.BlockSpec((N_META, d), lambda i: (0, 0)), row_vec],
        [jax.ShapeDtypeStruct((seq, d), F32), jax.ShapeDtypeStruct((N_META, d), F32), vec], input_grads,
        mode="nn", steps=nq, name="mm_dn1_norm_bwd", after=sent,
        scratch=[pltpu.VMEM((2, tm, d), F32), pltpu.SemaphoreType.DMA((2,))])

    grads = dict(meta=g_meta, attn_g=g_attn, fox_b=g_fox_b, ret_g=g_ret,
                 ffn_g=g_ffn, conv_w=g_conv_w8, conv_b=g_conv_b, final_g=g_final)
    return loss_tile, grad_x, grads


def kernel(x, meta_tokens, attn_norm_g, w_in, fox_forget_b, ret_norm_g, w_out, ffn_norm_g, w_up, conv_w, conv_b, w_down, final_norm_g, loss_target, m_meta_tokens, m_attn_norm_g, m_w_in, m_fox_forget_b, m_ret_norm_g, m_w_out, m_ffn_norm_g, m_w_up, m_conv_w, m_conv_b, m_w_down, m_final_norm_g, v_meta_tokens, v_attn_norm_g, v_w_in, v_fox_forget_b, v_ret_norm_g, v_w_out, v_ffn_norm_g, v_w_up, v_conv_w, v_conv_b, v_w_down, v_final_norm_g):
    d = D_MODEL
    me = 4 * lax.axis_index("x") + 2 * lax.axis_index("y") + lax.axis_index("c")
    in_blk = IN_WIDTH // N_DEV
    in_blk_pad = 400
    up_blk = 2 * D_FF // N_DEV
    down_blk = D_FF // N_DEV
    cw_blk = D_FF // N_DEV

    w_in_loc = jnp.pad(w_in[0].T.astype(BF16), ((0, in_blk_pad - in_blk), (0, 0)))
    cw_loc = jnp.pad(conv_w[0], ((0, 5), (0, 384 - cw_blk)))
    g_meta, g_cw = _exchange([meta_tokens, cw_loc], ["gather"] * 2, "gather_small")
    first = _exchange_start([w_in_loc], ["gather"], "gather_in_start", after=g_meta)
    rest_loc = [(w_out[0] + first[-1][0:1, 0:1]).astype(BF16), w_up[0].T.astype(BF16), w_down[0].astype(BF16)]
    rest = _exchange_start(rest_loc, ["gather"] * 3, "gather_rest_start")
    meta_f = g_meta.transpose(1, 0, 2).reshape(N_META, d)
    conv_w8 = jnp.pad(g_cw[:, :3, :cw_blk].transpose(1, 0, 2).reshape(3, D_FF), ((0, 5), (0, 0)))
    pending = {}

    def first_weight(after):
        (g_in,) = _exchange_wait(first, ["gather"], after, "gather_in_wait")
        return _to_internal_rows(g_in[:, :in_blk].reshape(IN_WIDTH, d))

    def in_grad_ready(gw_in_t):
        blocks = _from_internal_rows(gw_in_t).reshape(N_DEV, in_blk, d)
        blocks = jnp.pad(blocks, ((0, 0), (0, in_blk_pad - in_blk), (0, 0)))
        pending["in"] = _exchange_start([blocks], ["scatter"], "grads_in_start")
        return pending["in"][-1][0:1, 0:1]

    def late_weights(after):
        g_out, g_up, g_down = _exchange_wait(rest, ["gather"] * 3, after, "gather_rest_wait")
        return g_out.reshape(d, d), g_up.reshape(2, D_FF, d), g_down.reshape(D_FF, d)

    def ffn_grads_ready(gw_down, gw_up_t):
        blocks = [gw_down.reshape(N_DEV, down_blk, d), gw_up_t.reshape(N_DEV, up_blk, d)]
        pending["ffn"] = _exchange_start(blocks, ["scatter"] * 2, "grads_ffn_start")
        return pending["ffn"][-1][0:1, 0:1]

    def out_grad_ready(gw_out):
        pending["out"] = _exchange_start([gw_out.reshape(N_DEV, d // N_DEV, d)], ["scatter"], "grads_out_start")
        return pending["out"][-1][0:1, 0:1]

    loss_tile, grad_x, gr = _local_step(
        x[0], loss_target[0], meta_f, attn_norm_g + rest[-1][0:1, 0:1], fox_forget_b, ret_norm_g, ffn_norm_g,
        conv_w8, conv_b, final_norm_g.reshape(1, d), first_weight, late_weights, ffn_grads_ready, out_grad_ready,
        in_grad_ready)

    small = [loss_tile, gr["attn_g"], gr["fox_b"], gr["ret_g"], gr["ffn_g"], gr["conv_b"], gr["final_g"],
             gr["meta"], gr["conv_w"]]
    small_kinds = ["gather"] * len(small)
    small_started = _exchange_start(small, small_kinds, "grads_small_start")

    r_down, r_up = _exchange_wait(pending["ffn"], ["scatter"] * 2, small_started[-1], "grads_ffn_wait")
    (r_out,) = _exchange_wait(pending["out"], ["scatter"], small_started[-1], "grads_out_wait")
    g_w_out = _sum_slots(r_out, "sum_w_out", d // N_DEV)
    g_w_up_t = _sum_slots(r_up, "sum_w_up", up_blk)
    g_w_down = _sum_slots(r_down, "sum_w_down", down_blk)
    as_t = lambda a: a[0].T
    from_t = lambda a: a.T[None]
    d_w_out, m_w_out_n, v_w_out_n = [a[None] for a in _adamw(w_out[0], g_w_out, m_w_out[0], v_w_out[0], "adamw_w_out", 128)]
    up_t = _adamw(as_t(w_up), g_w_up_t, as_t(m_w_up), as_t(v_w_up), "adamw_w_up", up_blk // 2)
    d_w_up, m_w_up_n, v_w_up_n = [from_t(a) for a in up_t]
    d_w_down, m_w_down_n, v_w_down_n = [a[None] for a in _adamw(w_down[0], g_w_down, m_w_down[0], v_w_down[0],
                                                                "adamw_w_down", down_blk)]

    own_small, r_small = _exchange_wait(small_started, small_kinds, up_t[0], "grads_small_wait", fill_own=False)
    (loss_all, g_attn, g_fox_b128, g_ret, g_ffn, g_conv_b, g_final, g_meta_full, g_cw_full) = _sum_slots_small(
        r_small, own_small, "sum_small")
    loss = loss_all[0, 0]
    g_fox_b = g_fox_b128[:, :FOX_HEADS]
    g_meta_loc = lax.dynamic_slice(g_meta_full, (0, me * (d // N_DEV)), (N_META, d // N_DEV))
    g_cw_loc = lax.dynamic_slice(g_cw_full, (0, me * cw_blk), (3, cw_blk))

    (r_in,) = _exchange_wait(pending["in"], ["scatter"], r_small[0], "grads_in_wait")
    g_w_in_t = _sum_slots(r_in, "sum_w_in", in_blk_pad)[:in_blk]
    d_w_in, m_w_in_n, v_w_in_n = [from_t(a) for a in _adamw(as_t(w_in), g_w_in_t, as_t(m_w_in), as_t(v_w_in),
                                                            "adamw_w_in", in_blk)]
    g_w_in, g_w_up = g_w_in_t.T, g_w_up_t.T
    row = lambda a: a.reshape(1, d)
    sm_grads = [g_meta_loc, g_attn, g_fox_b, g_ret, g_ffn, g_cw_loc, g_conv_b, g_final]
    sm_w = [meta_tokens, attn_norm_g, fox_forget_b, ret_norm_g, ffn_norm_g, conv_w[0], conv_b, row(final_norm_g)]
    sm_m = [m_meta_tokens, m_attn_norm_g, m_fox_forget_b, m_ret_norm_g, m_ffn_norm_g, m_conv_w[0], m_conv_b,
            row(m_final_norm_g)]
    sm_v = [v_meta_tokens, v_attn_norm_g, v_fox_forget_b, v_ret_norm_g, v_ffn_norm_g, v_conv_w[0], v_conv_b,
            row(v_final_norm_g)]
    dl, ml, vl = [lst[:7] + [lst[7].reshape(d)] for lst in _adamw_small(sm_w, sm_grads, sm_m, sm_v, "adamw_small")]

    def by_weight(meta_, attn_, w_in_, fox_, ret_, w_out_, ffn_, w_up_, cw_, cb_, w_down_, final_):
        return (meta_, attn_, w_in_, fox_, ret_, w_out_, ffn_, w_up_, cw_[None], cb_, w_down_, final_)

    grads_out = by_weight(g_meta_loc, g_attn, g_w_in[None], g_fox_b, g_ret, g_w_out[None], g_ffn, g_w_up[None], g_cw_loc,
                          g_conv_b, g_w_down[None], g_final.reshape(d))
    delta_out = by_weight(dl[0], dl[1], d_w_in, dl[2], dl[3], d_w_out, dl[4], d_w_up, dl[5], dl[6], d_w_down, dl[7])
    m_out = by_weight(ml[0], ml[1], m_w_in_n, ml[2], ml[3], m_w_out_n, ml[4], m_w_up_n, ml[5], ml[6], m_w_down_n, ml[7])
    v_out = by_weight(vl[0], vl[1], v_w_in_n, vl[2], vl[3], v_w_out_n, vl[4], v_w_up_n, vl[5], vl[6], v_w_down_n, vl[7])
    return (loss, grad_x[None]) + grads_out + delta_out + m_out + v_out
```

```python
import numpy as np
import jax
import jax.numpy as jnp
from jax import lax
from jax.experimental import pallas as pl
from jax.experimental.pallas import tpu as pltpu

F32 = jnp.float32
BF16 = jnp.bfloat16

D_MODEL = 1024
N_META = 16
N_PAD = 112
PREFIX = 128
RET_HEADS = 4
RET_DK = 64
RET_DV = 128
FOX_HEADS = 8
FOX_DH = 64
D_FF = 2816
ROPE_BASE = 10000.0
EPS = 1e-6
NEG = -1e30
RET_QK = RET_HEADS * RET_DK
RET_V = RET_HEADS * RET_DV
FOX_W = FOX_HEADS * FOX_DH
IN_WIDTH = 2 * RET_QK + 2 * RET_V + 3 * FOX_W + FOX_HEADS
IN_PAD = 3200
FF_COL_BLOCK = (IN_WIDTH - FOX_HEADS) // 128
QK_SCALE = 0.125

ADAM_LR = 0.001
ADAM_B1 = 0.9
ADAM_B2 = 0.999
ADAM_EPS = 1e-08
ADAM_WD = 0.01
ADAM_STEP = 10

N_DEV = 8
LANE = 128
ROW_TILE = 128
TOK_TILE = 384

NN = (((1,), (0,)), ((), ()))
NT = (((1,), (1,)), ((), ()))
TN = (((0,), (0,)), ((), ()))


def _pcall(body, **kw):
    return pl.pallas_call(body, **kw)


def _params(*sem):
    return pltpu.CompilerParams(dimension_semantics=sem)


def _dot(a, b, dims=NN):
    return lax.dot_general(a, b, dims, preferred_element_type=F32)


def _sigmoid(x):
    return 0.5 * jnp.tanh(0.5 * x) + 0.5


def _matmul(a, b, *, mode, grid, a_spec, b_spec, o_spec, out_shape, name, add=None, add_spec=None, after=None):
    dims = {"nn": NN, "nt": NT, "tn": TN}[mode]
    nk = grid[2]
    has_add = add is not None
    a_list, b_list = (list(a), list(b)) if isinstance(a, (list, tuple)) else ([a], [b])
    a_specs, b_specs = (list(a_spec), list(b_spec)) if isinstance(a_spec, (list, tuple)) else ([a_spec], [b_spec])
    nt = len(a_list)
    n_in = 2 * nt + int(has_add) + int(after is not None)

    def body(*refs):
        a_refs, b_refs = refs[:nt], refs[nt:2 * nt]
        add_ref = refs[2 * nt] if has_add else None
        o_ref = refs[n_in]
        part = _dot(a_refs[0][...].astype(BF16), b_refs[0][...].astype(BF16), dims)
        for ar, br in zip(a_refs[1:], b_refs[1:]):
            part = part + _dot(ar[...].astype(BF16), br[...].astype(BF16), dims)

        def finish(acc):
            if has_add:
                acc = acc + add_ref[...]
            o_ref[...] = acc.astype(o_ref.dtype)

        if nk == 1:
            finish(part)
        else:
            acc_ref = refs[-1]
            k = pl.program_id(2)

            @pl.when(k == 0)
            def _():
                acc_ref[...] = part

            @pl.when(k > 0)
            def _():
                acc_ref[...] += part

            @pl.when(k == nk - 1)
            def _():
                finish(acc_ref[...])

    in_specs = a_specs + b_specs + ([add_spec] if has_add else [])
    args = tuple(a_list) + tuple(b_list) + ((add,) if has_add else ())
    if after is not None:
        in_specs, args = in_specs + [pl.BlockSpec(memory_space=pl.ANY)], args + (after,)
    scratch = [] if nk == 1 else [pltpu.VMEM(tuple(d for d in o_spec.block_shape if d is not None), F32)]
    return _pcall(
        body, name=name, grid=grid, in_specs=in_specs, out_specs=o_spec, out_shape=out_shape,
        scratch_shapes=scratch, compiler_params=_params("parallel", "parallel", "arbitrary"),
    )(*args)


def _mm_simple(a, b, *, mode, tm, tn, tk, out_dtype, name, add=None, after=None):
    if mode == "tn":
        K, M = a.shape
    else:
        M, K = a.shape
    N = b.shape[0] if mode == "nt" else b.shape[1]
    grid = (M // tm, N // tn, K // tk)
    resident = dict(pipeline_mode=pl.Buffered(1)) if (tn == N and tk == K) else {}
    a_spec = pl.BlockSpec((tk, tm), lambda i, j, k: (k, i)) if mode == "tn" else pl.BlockSpec((tm, tk), lambda i, j, k: (i, k))
    b_spec = (pl.BlockSpec((tn, tk), lambda i, j, k: (j, k), **resident) if mode == "nt"
              else pl.BlockSpec((tk, tn), lambda i, j, k: (k, j), **resident))
    o_spec = pl.BlockSpec((tm, tn), lambda i, j, k: (i, j))
    return _matmul(a, b, mode=mode, grid=grid, a_spec=a_spec, b_spec=b_spec, o_spec=o_spec,
                   out_shape=jax.ShapeDtypeStruct((M, N), out_dtype), name=name, add=add,
                   add_spec=o_spec if add is not None else None, after=after)


def _matmul_rows(a_list, a_specs, b_list, b_specs, extras, extra_specs, out_specs, out_shape, epilogue, *,
                 mode, steps, name, after=None, scratch=()):
    dims = {"nn": NN, "nt": NT}[mode]
    nt, ne = len(a_list), len(extras)
    n_in = 2 * nt + ne + int(after is not None)

    def body(*refs):
        acc = _dot(refs[0][...].astype(BF16), refs[nt][...].astype(BF16), dims)
        for k in range(1, nt):
            acc = acc + _dot(refs[k][...].astype(BF16), refs[nt + k][...].astype(BF16), dims)
        epilogue(pl.program_id(0), acc, refs[2 * nt:2 * nt + ne], refs[n_in:])

    in_specs = list(a_specs) + list(b_specs) + list(extra_specs)
    args = tuple(a_list) + tuple(b_list) + tuple(extras)
    if after is not None:
        in_specs, args = in_specs + [pl.BlockSpec(memory_space=pl.ANY)], args + (after,)
    return _pcall(body, name=name, grid=(steps,), in_specs=in_specs, out_specs=out_specs, out_shape=out_shape,
                  scratch_shapes=list(scratch), compiler_params=_params("arbitrary"))(*args)


def _rms_bwd_tile(dy, x, gain, dres):
    r = lax.rsqrt(jnp.mean(x * x, axis=-1, keepdims=True) + EPS)
    xhat = x * r
    u = dy * gain
    return dres + r * (u - xhat * jnp.mean(u * xhat, axis=-1, keepdims=True)), jnp.sum(dy * xhat, axis=0, keepdims=True)


def _loss_tile(i, x, tgt, gain):
    d = x.shape[-1]
    r = lax.rsqrt(jnp.mean(x * x, axis=-1, keepdims=True) + EPS)
    xhat = x * r
    counted = (i * TOK_TILE + lax.broadcasted_iota(jnp.int32, (TOK_TILE, 1), 0)) >= PREFIX
    err = jnp.where(counted, xhat * gain - tgt, 0.0)
    dy = err * (1.0 / d)
    u = dy * gain
    dh = r * (u - xhat * jnp.mean(u * xhat, axis=-1, keepdims=True))
    return 0.5 * jnp.sum(jnp.mean(err * err, axis=-1, keepdims=True)), dh, jnp.sum(dy * xhat, axis=0, keepdims=True)


def _accumulate(ref, i, part):
    @pl.when(i == 0)
    def _():
        ref[...] = part

    @pl.when(i > 0)
    def _():
        ref[...] += part


def _prep_norm(x, meta, gain, name):
    seq, d = x.shape
    t = seq + PREFIX

    def body(xa_ref, xb_ref, xc_ref, meta_ref, g_ref, h_ref, n_ref):
        i = pl.program_id(0)

        @pl.when(i == 0)
        def _():
            h_ref[0:N_PAD, :] = jnp.zeros((N_PAD, d), F32)
            h_ref[N_PAD:ROW_TILE, :] = meta_ref[...]

        @pl.when(i > 0)
        def _():
            h_ref[0:ROW_TILE, :] = xa_ref[...]

        h_ref[ROW_TILE:2 * ROW_TILE, :] = xb_ref[...]
        h_ref[2 * ROW_TILE:3 * ROW_TILE, :] = xc_ref[...]
        h = h_ref[...]
        r = lax.rsqrt(jnp.mean(h * h, axis=-1, keepdims=True) + EPS)
        n_ref[...] = (h * r * g_ref[...]).astype(BF16)

    return _pcall(
        body, name=name, grid=(t // TOK_TILE,),
        in_specs=_shifted_row_specs(d) + [pl.BlockSpec((N_META, d), lambda i: (0, 0)), pl.BlockSpec((1, d), lambda i: (0, 0))],
        out_specs=[pl.BlockSpec((TOK_TILE, d), lambda i: (i, 0)), pl.BlockSpec((TOK_TILE, d), lambda i: (i, 0))],
        out_shape=[jax.ShapeDtypeStruct((t, d), F32), jax.ShapeDtypeStruct((t, d), BF16)],
        compiler_params=_params("parallel"),
    )(x, x, x, meta, gain)


def _shifted_row_specs(d):
    blocks_per_tile = TOK_TILE // ROW_TILE
    return [pl.BlockSpec((ROW_TILE, d), lambda i, r=r: (jnp.maximum(blocks_per_tile * i + r, 0), 0)) for r in (-1, 0, 1)]


def _ret_consts(bk):
    gam = 1.0 - 2.0 ** (-5.0 - np.arange(RET_HEADS))
    n = np.arange(bk)
    same_or_earlier_chunk = (n[None, :] // 64) <= (n[:, None] // 64)
    w = gam[:, None, None] ** np.abs(n[:, None] - n[None, :])[None] * same_or_earlier_chunk[None]
    wq = gam[:, None] ** (n[None, :] + 1.0)
    wk = gam[:, None] ** (bk - 1.0 - n[None, :])
    mask = (np.arange(RET_QK)[None, :] // RET_DK) == np.arange(RET_HEADS)[:, None]
    return (jnp.asarray(w, F32), jnp.asarray(wq[:, :, None], F32), jnp.asarray(wk[:, :, None], F32),
            jnp.asarray(mask[:, None, :], F32), [float(g ** bk) for g in gam])


def _rope_tables(t):
    half = RET_DK // 2
    inv = 1.0 / (ROPE_BASE ** (jnp.arange(half, dtype=F32) / half))
    ang = jnp.arange(t).astype(F32)[:, None] * inv[None, :]
    cos, sin = jnp.cos(ang), jnp.sin(ang)
    return (jnp.tile(jnp.concatenate([cos, cos], axis=1), (1, RET_HEADS)),
            jnp.tile(jnp.concatenate([-sin, sin], axis=1), (1, RET_HEADS)))


def _swap_halves(x):
    outs = []
    for s in range(x.shape[1] // LANE):
        xs = x[:, LANE * s:LANE * (s + 1)]
        lane = lax.broadcasted_iota(jnp.int32, xs.shape, 1)
        outs.append(jnp.where((lane & 32) == 0, pltpu.roll(xs, LANE - 32, axis=1), pltpu.roll(xs, 32, axis=1)))
    return outs[0] if len(outs) == 1 else jnp.concatenate(outs, axis=1)


def _rope(x, cos, sin_signed):
    return x * cos + _swap_halves(x) * sin_signed


def _rope_t(dx, cos, sin_signed):
    return dx * cos + _swap_halves(dx * sin_signed)


def _ret_fwd(proj, cos, sin, gain, name):
    t = proj.shape[0]
    bk = TOK_TILE
    nb = t // bk
    w, wq, wk, mask, g_blk = _ret_consts(bk)

    def body(q_ref, k_ref, v_ref, rg_ref, cos_ref, sin_ref, w_ref, wq_ref, wk_ref, mask_ref, gain_ref,
             opre_ref, og_ref, st_ref, r_ref):
        i = pl.program_id(0)

        @pl.when(i == 0)
        def _():
            r_ref[...] = jnp.zeros_like(r_ref)

        c, s = cos_ref[...], sin_ref[...]
        valid = ((i * bk + lax.broadcasted_iota(jnp.int32, (bk, 1), 0)) >= N_PAD).astype(F32)
        qr = _rope(q_ref[...], c, s)
        kr = _rope(k_ref[...], c, s) * QK_SCALE * valid
        kb = kr.astype(BF16)
        for h in range(RET_HEADS):
            hm = mask_ref[h]
            cols = slice(RET_DV * h, RET_DV * (h + 1))
            vh = v_ref[:, cols].astype(BF16)
            r_prev = r_ref[h]
            st_ref[0, h] = r_prev
            sm = _dot((qr * hm).astype(BF16), kb, NT) * w_ref[h]
            o = _dot(sm.astype(BF16), vh) + _dot((qr * (hm * wq_ref[h])).astype(BF16), r_prev.astype(BF16))
            r_ref[h] = g_blk[h] * r_prev + _dot((kr * wk_ref[h]).astype(BF16), vh, TN)
            opre_ref[:, cols] = o
            rstd = lax.rsqrt(jnp.mean(o * o, axis=-1, keepdims=True) + EPS)
            rg = rg_ref[:, cols]
            og_ref[:, cols] = (o * rstd * gain_ref[:, cols] * (rg * _sigmoid(rg))).astype(BF16)

    full = lambda shape: pl.BlockSpec(shape, lambda i: (0,) * len(shape))
    return _pcall(
        body, name=name, grid=(nb,),
        in_specs=[pl.BlockSpec((bk, RET_QK), lambda i: (i, 0)), pl.BlockSpec((bk, RET_QK), lambda i: (i, 1)),
                  pl.BlockSpec((bk, RET_V), lambda i: (i, 1)), pl.BlockSpec((bk, RET_V), lambda i: (i, 2)),
                  pl.BlockSpec((bk, RET_QK), lambda i: (i, 0)), pl.BlockSpec((bk, RET_QK), lambda i: (i, 0)),
                  full((RET_HEADS, bk, bk)), full((RET_HEADS, bk, 1)), full((RET_HEADS, bk, 1)),
                  full((RET_HEADS, 1, RET_QK)), full((1, RET_V))],
        out_specs=[pl.BlockSpec((bk, RET_V), lambda i: (i, 0)), pl.BlockSpec((bk, RET_V), lambda i: (i, 0)),
                   pl.BlockSpec((1, RET_HEADS, RET_QK, RET_DV), lambda i: (i, 0, 0, 0))],
        out_shape=[jax.ShapeDtypeStruct((t, RET_V), F32), jax.ShapeDtypeStruct((t, RET_V + FOX_W), BF16),
                   jax.ShapeDtypeStruct((nb, RET_HEADS, RET_QK, RET_DV), F32)],
        scratch_shapes=[pltpu.VMEM((RET_HEADS, RET_QK, RET_DV), F32)],
        compiler_params=_params("arbitrary"),
    )(proj, proj, proj, proj, cos, sin, w, wq, wk, mask, gain)


def _ret_bwd(proj, cos, sin, gain, dmixed, opre, states, name):
    t = proj.shape[0]
    bk = TOK_TILE
    nb = t // bk
    w, wq, wk, mask, g_blk = _ret_consts(bk)
    v0, g0 = 2 * RET_QK, 2 * RET_QK + RET_V

    def body(q_ref, k_ref, v_ref, rg_ref, cos_ref, sin_ref, w_ref, wq_ref, wk_ref, mask_ref, gain_ref,
             dog_ref, opre_ref, st_ref, dp_ref, gg_ref, dr_ref):
        step = pl.program_id(0)
        i = nb - 1 - step

        @pl.when(step == 0)
        def _():
            dr_ref[...] = jnp.zeros_like(dr_ref)
            gg_ref[...] = jnp.zeros_like(gg_ref)

        c, s = cos_ref[...], sin_ref[...]
        valid = ((i * bk + lax.broadcasted_iota(jnp.int32, (bk, 1), 0)) >= N_PAD).astype(F32)
        qr = _rope(q_ref[...], c, s)
        kr = _rope(k_ref[...], c, s) * QK_SCALE * valid
        kb = kr.astype(BF16)
        dqr = jnp.zeros((bk, RET_QK), F32)
        dkr = jnp.zeros((bk, RET_QK), F32)
        for h in range(RET_HEADS):
            hm = mask_ref[h]
            cols = slice(RET_DV * h, RET_DV * (h + 1))
            vh = v_ref[:, cols].astype(BF16)
            o = opre_ref[:, cols]
            rstd = lax.rsqrt(jnp.mean(o * o, axis=-1, keepdims=True) + EPS)
            xhat = o * rstd
            rg = rg_ref[:, cols]
            sg = _sigmoid(rg)
            gate = rg * sg
            gn = gain_ref[:, cols]
            dog = dog_ref[:, cols]
            dp_ref[:, g0 + RET_DV * h:g0 + RET_DV * (h + 1)] = (
                dog * xhat * gn * (sg * (1.0 + rg * (1.0 - sg)))).astype(BF16)
            gg_ref[:, cols] += jnp.sum(dog * xhat * gate, axis=0, keepdims=True)
            dxh = dog * gn * gate
            do = (rstd * (dxh - xhat * jnp.mean(dxh * xhat, axis=-1, keepdims=True))).astype(BF16)
            qm = (qr * hm).astype(BF16)
            qw = (qr * (hm * wq_ref[h])).astype(BF16)
            kw = (kr * wk_ref[h]).astype(BF16)
            wh = w_ref[h]
            sm = (_dot(qm, kb, NT) * wh).astype(BF16)
            ds = (_dot(do, vh, NT) * wh).astype(BF16)
            dr = dr_ref[h]
            drb = dr.astype(BF16)
            dp_ref[:, v0 + RET_DV * h:v0 + RET_DV * (h + 1)] = (_dot(sm, do, TN) + _dot(kw, drb)).astype(BF16)
            dqr = dqr + _dot(ds, kb) * hm + _dot(do, st_ref[0, h].astype(BF16), NT) * (hm * wq_ref[h])
            dkr = dkr + _dot(ds, qm, TN) + _dot(vh, drb, NT) * wk_ref[h]
            dr_ref[h] = g_blk[h] * dr + _dot(qw, do, TN)
        dp_ref[:, 0:RET_QK] = _rope_t(dqr, c, s).astype(BF16)
        dp_ref[:, RET_QK:2 * RET_QK] = _rope_t(dkr * (QK_SCALE * valid), c, s).astype(BF16)

    full = lambda shape: pl.BlockSpec(shape, lambda i: (0,) * len(shape))
    rev = lambda col: (lambda i: (nb - 1 - i, col))
    return _pcall(
        body, name=name, grid=(nb,),
        in_specs=[pl.BlockSpec((bk, RET_QK), rev(0)), pl.BlockSpec((bk, RET_QK), rev(1)),
                  pl.BlockSpec((bk, RET_V), rev(1)), pl.BlockSpec((bk, RET_V), rev(2)),
                  pl.BlockSpec((bk, RET_QK), rev(0)), pl.BlockSpec((bk, RET_QK), rev(0)),
                  full((RET_HEADS, bk, bk)), full((RET_HEADS, bk, 1)), full((RET_HEADS, bk, 1)),
                  full((RET_HEADS, 1, RET_QK)), full((1, RET_V)),
                  pl.BlockSpec((bk, RET_V), rev(0)), pl.BlockSpec((bk, RET_V), rev(0)),
                  pl.BlockSpec((1, RET_HEADS, RET_QK, RET_DV), lambda i: (nb - 1 - i, 0, 0, 0))],
        out_specs=[pl.BlockSpec((bk, g0 + RET_V), rev(0)), pl.BlockSpec((1, RET_V), lambda i: (0, 0))],
        out_shape=[jax.ShapeDtypeStruct((t, IN_PAD), BF16), jax.ShapeDtypeStruct((1, RET_V), F32)],
        scratch_shapes=[pltpu.VMEM((RET_HEADS, RET_QK, RET_DV), F32)],
        compiler_params=_params("arbitrary"),
    )(proj, proj, proj, proj, cos, sin, w, wq, wk, mask, gain, dmixed, opre, states)


def _forget_cumsum(proj, bias, name):
    t = proj.shape[0]
    rt = TOK_TILE
    nb = t // rt
    tril = jnp.asarray(np.tril(np.ones((rt, rt))), F32)

    def body(z_ref, b_ref, tril_ref, c_ref, carry_ref):
        i = pl.program_id(0)

        @pl.when(i == 0)
        def _():
            carry_ref[...] = jnp.zeros_like(carry_ref)

        z = z_ref[...] + b_ref[...]
        logf = jnp.minimum(z, 0.0) - jnp.log(1.0 + jnp.exp(-jnp.abs(z)))
        c = lax.dot_general(tril_ref[...], logf, NN, precision=lax.Precision.HIGHEST,
                            preferred_element_type=F32) + carry_ref[...]
        c_ref[...] = c
        carry_ref[...] = c[rt - 1:rt, :]

    return _pcall(
        body, name=name, grid=(nb,),
        in_specs=[pl.BlockSpec((rt, LANE), lambda i: (i, FF_COL_BLOCK)), pl.BlockSpec((1, LANE), lambda i: (0, 0)),
                  pl.BlockSpec((rt, rt), lambda i: (0, 0))],
        out_specs=pl.BlockSpec((rt, LANE), lambda i: (i, 0)),
        out_shape=jax.ShapeDtypeStruct((t, LANE), F32),
        scratch_shapes=[pltpu.VMEM((1, LANE), F32)],
        compiler_params=_params("arbitrary"),
    )(proj, bias, tril)


def _forget_cumsum_bwd(proj, bias, drs, dcs, dproj, name):
    t = proj.shape[0]
    rt = TOK_TILE
    nb = t // rt
    triu = jnp.asarray(np.triu(np.ones((rt, rt))), F32)

    def body(z_ref, b_ref, triu_ref, drs_ref, dcs_ref, dproj_in, dz_ref, gb_ref, carry_ref):
        step = pl.program_id(0)

        @pl.when(step == 0)
        def _():
            carry_ref[...] = jnp.zeros_like(carry_ref)
            gb_ref[...] = jnp.zeros_like(gb_ref)

        dlogf = lax.dot_general(triu_ref[...], drs_ref[...] - dcs_ref[...], NN, precision=lax.Precision.HIGHEST,
                                preferred_element_type=F32) + carry_ref[...]
        carry_ref[...] = dlogf[0:1, :]
        z = z_ref[...] + b_ref[...]
        is_head = lax.broadcasted_iota(jnp.int32, (rt, LANE), 1) < FOX_HEADS
        dz = jnp.where(is_head, dlogf / (1.0 + jnp.exp(z)), 0.0)
        dz_ref[...] = dz.astype(BF16)
        gb_ref[...] += jnp.sum(dz, axis=0, keepdims=True)

    return _pcall(
        body, name=name, grid=(nb,),
        in_specs=[pl.BlockSpec((rt, LANE), lambda i: (nb - 1 - i, FF_COL_BLOCK)),
                  pl.BlockSpec((1, LANE), lambda i: (0, 0)),
                  pl.BlockSpec((rt, rt), lambda i: (0, 0)),
                  pl.BlockSpec((rt, LANE), lambda i: (nb - 1 - i, 0)),
                  pl.BlockSpec((rt, LANE), lambda i: (nb - 1 - i, 0)),
                  pl.BlockSpec(memory_space=pl.ANY)],
        out_specs=[pl.BlockSpec((rt, LANE), lambda i: (nb - 1 - i, FF_COL_BLOCK)),
                   pl.BlockSpec((1, LANE), lambda i: (0, 0))],
        out_shape=[jax.ShapeDtypeStruct(dproj.shape, BF16), jax.ShapeDtypeStruct((1, LANE), F32)],
        input_output_aliases={5: 0},
        scratch_shapes=[pltpu.VMEM((1, LANE), F32)],
        compiler_params=_params("arbitrary"),
    )(proj, bias, triu, drs, dcs, dproj)


FOX_PAIRS = FOX_HEADS // 2
L_ONE_Q = FOX_DH
L_ONE_K = FOX_DH + 3
L_LSE = FOX_DH + 4


def _split3(x):
    hi = x.astype(BF16).astype(F32)
    r = x - hi
    mid = r.astype(BF16).astype(F32)
    return hi, mid, r - mid


def _head_to_low(slab, e):
    return slab if e == 0 else pltpu.roll(slab, FOX_DH, axis=1)


def _pair(a, b, low):
    return jnp.where(low, a, pltpu.roll(b, FOX_DH, axis=1))


def _fox_prep(proj, c, name):
    t = proj.shape[0]
    tq = TOK_TILE

    def body(p_ref, c_ref, qa_ref, ka_ref, va_ref, qt_ref, vt_ref):
        i = pl.program_id(0)
        lane = lax.broadcasted_iota(jnp.int32, (tq, LANE), 1)
        low = lane < FOX_DH
        live = (i * tq + lax.broadcasted_iota(jnp.int32, (tq, 1), 0)) >= N_PAD
        q_tail = jnp.where(lane < L_ONE_Q + 3, 1.0, 0.0)
        k_ones = (lane >= L_ONE_K) & (lane < L_ONE_K + 4)
        v_tail = jnp.where(lane < FOX_DH + 2, 1.0, 0.0)
        bias_parts = _split3(jnp.where(live, -c_ref[...], NEG))
        for pair in range(FOX_PAIRS):
            base = 3 * LANE * pair
            for e in range(2):
                h = 2 * pair + e
                q = _head_to_low(p_ref[:, base:base + LANE], e)
                k = _head_to_low(p_ref[:, base + LANE:base + 2 * LANE], e)
                v = _head_to_low(p_ref[:, base + 2 * LANE:base + 3 * LANE], e)
                hi, mid, lo = [part[:, h:h + 1] for part in bias_parts]
                ka = jnp.where(low, k, jnp.where(k_ones, 1.0, 0.0))
                ka = jnp.where(lane == L_ONE_Q, hi, jnp.where(lane == L_ONE_Q + 1, mid, jnp.where(lane == L_ONE_Q + 2, lo, ka)))
                qa = jnp.where(low, q * QK_SCALE, q_tail)
                va = jnp.where(low, v, v_tail)
                qa_ref[h] = qa.astype(BF16)
                ka_ref[h] = ka.astype(BF16)
                va_ref[h] = va.astype(BF16)
                qt_ref[h] = qa.T.astype(BF16)
                vt_ref[h] = va.T.astype(BF16)

    out = jax.ShapeDtypeStruct((FOX_HEADS, t, LANE), BF16)
    out_t = jax.ShapeDtypeStruct((FOX_HEADS, t // tq, LANE, tq), BF16)
    ospec = pl.BlockSpec((FOX_HEADS, tq, LANE), lambda i: (0, i, 0))
    tspec = pl.BlockSpec((FOX_HEADS, None, LANE, tq), lambda i: (0, i, 0, 0))
    return _pcall(
        body, name=name, grid=(t // tq,),
        in_specs=[pl.BlockSpec((tq, 3 * FOX_W), lambda i: (i, 1)), pl.BlockSpec((tq, LANE), lambda i: (i, 0))],
        out_specs=[ospec, ospec, ospec, tspec, tspec], out_shape=[out, out, out, out_t, out_t],
        compiler_params=_params("parallel"),
    )(proj, c)


STEP_PAIRS = 2
STEP_HEADS = 2 * STEP_PAIRS
FOX_GROUPS = FOX_PAIRS // STEP_PAIRS
FWD_PAIRS = 4
FWD_HEADS = 2 * FWD_PAIRS
FWD_GROUPS = FOX_PAIRS // FWD_PAIRS


def _blockdiag(a, b):
    z = jnp.zeros_like(a)
    return jnp.concatenate([jnp.concatenate([a, z], axis=1), jnp.concatenate([z, b], axis=1)], axis=0)


def _fox_fwd(qt, ka, vt, mixed, name):
    nh, nq, tq, _ = ka.shape
    t = nq * tq

    def body(qt_ref, ka_ref, vt_ref, mixed_in, mixed_ref, o_ref, lse_ref):
        i = pl.program_id(1)
        lane = lax.broadcasted_iota(jnp.int32, (tq, LANE), 1)
        key_le_query = lax.broadcasted_iota(jnp.int32, (tq, tq), 0) <= lax.broadcasted_iota(jnp.int32, (tq, tq), 1)

        def logits(j):
            return [_dot(ka_ref[h, j], qt_ref[h]) for h in range(FWD_HEADS)]

        def update(j, scores, carry, diagonal):
            new = []
            for h in range(FWD_HEADS):
                m, acc = carry[h]
                s = jnp.where(key_le_query, scores[h], NEG) if diagonal else scores[h]
                m_new = jnp.maximum(m, jnp.max(s, axis=0, keepdims=True))
                p = jnp.exp(s - m_new).astype(BF16)
                new.append((m_new, jnp.exp(m - m_new) * acc + _dot(vt_ref[h, j], p)))
            return tuple(new)

        init = tuple((jnp.full((1, tq), NEG, F32), jnp.zeros((LANE, tq), F32)) for _ in range(FWD_HEADS))
        carry = lax.fori_loop(0, i, lambda j, cr: update(j, logits(j), cr, False), init)
        outs, lse_rows = [], []
        for m, acc in update(i, logits(i), carry, True):
            l = acc[FOX_DH:FOX_DH + 1, :]
            outs.append((acc / l).T)
            lse_rows.append(m + jnp.log(l))
        lse_rows.append(jnp.zeros((LANE - FWD_HEADS, tq), F32))
        o_all = jnp.concatenate([_pair(outs[2 * c], outs[2 * c + 1], lane < FOX_DH) for c in range(FWD_PAIRS)], axis=1)
        mixed_ref[...] = o_all.astype(BF16)
        o_ref[...] = o_all
        lse_ref[...] = jnp.concatenate(lse_rows, axis=0).T

    width = FWD_PAIRS * LANE
    whole = pl.BlockSpec((FWD_HEADS, nq, tq, LANE), lambda g, i: (g, 0, 0, 0), pipeline_mode=pl.Buffered(1))
    whole_t = pl.BlockSpec((FWD_HEADS, nq, LANE, tq), lambda g, i: (g, 0, 0, 0), pipeline_mode=pl.Buffered(1))
    return _pcall(
        body, name=name, grid=(FWD_GROUPS, nq),
        in_specs=[pl.BlockSpec((FWD_HEADS, None, LANE, tq), lambda g, i: (g, i, 0, 0)), whole, whole_t,
                  pl.BlockSpec(memory_space=pl.ANY)],
        out_specs=[pl.BlockSpec((tq, width), lambda g, i: (i, RET_V // width + g)),
                   pl.BlockSpec((tq, width), lambda g, i: (i, g)),
                   pl.BlockSpec((None, tq, LANE), lambda g, i: (g, i, 0))],
        out_shape=[jax.ShapeDtypeStruct(mixed.shape, BF16), jax.ShapeDtypeStruct((t, FOX_W), F32),
                   jax.ShapeDtypeStruct((FWD_GROUPS, t, LANE), F32)],
        input_output_aliases={3: 0},
        compiler_params=_params("parallel", "parallel"),
    )(qt, ka, vt, mixed)


def _fox_prep_bwd(dmixed, o_fox, lse, qa, name):
    t = dmixed.shape[0]
    tq = TOK_TILE

    def body(dm_ref, o_ref, lse_ref, qa_ref, qab_ref, doa_ref):
        i = pl.program_id(0)
        lane = lax.broadcasted_iota(jnp.int32, (tq, LANE), 1)
        low = lane < FOX_DH
        live = (i * tq + lax.broadcasted_iota(jnp.int32, (tq, 1), 0)) >= N_PAD
        lse_parts = [_split3(jnp.where(live, -lse_ref[grp], 0.0)) for grp in range(FWD_GROUPS)]
        for pair in range(FOX_PAIRS):
            cols = slice(LANE * pair, LANE * (pair + 1))
            d_slab = dm_ref[:, cols]
            prod = d_slab * o_ref[:, cols]
            for e in range(2):
                h = 2 * pair + e
                nd = -jnp.sum(jnp.where(low, _head_to_low(prod, e), 0.0), axis=-1, keepdims=True)
                nd_hi = nd.astype(BF16).astype(F32)
                doa = jnp.where(low, _head_to_low(d_slab, e), 0.0)
                doa = jnp.where(lane == FOX_DH, nd_hi, jnp.where(lane == FOX_DH + 1, nd - nd_hi, doa))
                doa_ref[h] = doa.astype(BF16)
                lane_h = h % FWD_HEADS
                hi, mid, lo = [part[:, lane_h:lane_h + 1] for part in lse_parts[h // FWD_HEADS]]
                qab = qa_ref[h].astype(F32)
                qab = jnp.where(lane == L_LSE, hi, jnp.where(lane == L_LSE + 1, mid, jnp.where(lane == L_LSE + 2, lo, qab)))
                qab_ref[h] = qab.astype(BF16)

    out = jax.ShapeDtypeStruct((FOX_HEADS, t, LANE), BF16)
    hspec = pl.BlockSpec((FOX_HEADS, tq, LANE), lambda i: (0, i, 0))
    return _pcall(
        body, name=name, grid=(t // tq,),
        in_specs=[pl.BlockSpec((tq, FOX_W), lambda i: (i, 1)), pl.BlockSpec((tq, FOX_W), lambda i: (i, 0)),
                  pl.BlockSpec((FWD_GROUPS, tq, LANE), lambda i: (0, i, 0)), hspec],
        out_specs=[hspec, hspec], out_shape=[out, out],
        compiler_params=_params("parallel"),
    )(dmixed, o_fox, lse, qa)


def _fox_bwd(qab, doa, ka, va, dproj, name):
    nh, nq, tq, _ = qab.shape
    t = nq * tq
    slab = 3 * LANE * STEP_PAIRS
    group0 = (2 * RET_QK + 2 * RET_V) // slab

    def body(qab_ref, doa_ref, ka_ref, va_ref, dproj_in, dp_ref, drs_ref, dcs_ref, dq_ref):
        g, j = pl.program_id(0), pl.program_id(1)

        @pl.when((g == 0) & (j == 0))
        def _():
            drs_ref[...] = jnp.zeros_like(drs_ref)
            dcs_ref[...] = jnp.zeros_like(dcs_ref)

        @pl.when(j == 0)
        def _():
            dq_ref[...] = jnp.zeros_like(dq_ref)

        lane = lax.broadcasted_iota(jnp.int32, (tq, LANE), 1)
        low = lane < FOX_DH
        key_le_query = lax.broadcasted_iota(jnp.int32, (tq, tq), 0) <= lax.broadcasted_iota(jnp.int32, (tq, tq), 1)

        def by_head(c, a, b, col):
            h = STEP_HEADS * g + 2 * c
            return jnp.where(lane == h, a[:, col:col + 1], jnp.where(lane == h + 1, b[:, col:col + 1], 0.0))


        def step(i, carry, diagonal):
            st = [_dot(ka_ref[h], qab_ref[h, i], NT) for h in range(STEP_HEADS)]
            dpt = [_dot(va_ref[h], doa_ref[h, i], NT) for h in range(STEP_HEADS)]
            new = []
            for h in range(STEP_HEADS):
                p = jnp.exp(st[h])
                if diagonal:
                    p = jnp.where(key_le_query, p, 0.0)
                ds = (p * dpt[h]).astype(BF16)
                dq_ref[h, i] += _dot(ds, ka_ref[h], TN)
                dk, dv = carry[h]
                new.append((dk + _dot(ds, qab_ref[h, i]), dv + _dot(p.astype(BF16), doa_ref[h, i])))
            return tuple(new)

        zero = jnp.zeros((tq, LANE), F32)
        carry = step(j, tuple((zero, zero) for _ in range(STEP_HEADS)), True)
        carry = lax.fori_loop(j + 1, nq, lambda i, cr: step(i, cr, False), carry)
        rows = pl.ds(pl.multiple_of(j * tq, tq), tq)
        for c in range(STEP_PAIRS):
            (dka, dva), (dkb, dvb) = carry[2 * c], carry[2 * c + 1]
            c0 = 3 * LANE * c
            dp_ref[rows, c0 + LANE:c0 + 2 * LANE] = _pair(dka, dkb, low).astype(BF16)
            dp_ref[rows, c0 + 2 * LANE:c0 + 3 * LANE] = _pair(dva, dvb, low).astype(BF16)
            dcs_ref[rows, :] += by_head(c, dka, dkb, L_ONE_Q)

        @pl.when(j == nq - 1)
        def _():
            for c in range(STEP_PAIRS):
                for blk in range(nq):
                    r = slice(blk * tq, (blk + 1) * tq)
                    a, b = dq_ref[2 * c, blk], dq_ref[2 * c + 1, blk]
                    dp_ref[r, 3 * LANE * c:3 * LANE * c + LANE] = (_pair(a, b, low) * QK_SCALE).astype(BF16)
                    drs_ref[r, :] += by_head(c, a, b, L_ONE_K)

    whole = pl.BlockSpec((STEP_HEADS, nq, tq, LANE), lambda g, j: (g, 0, 0, 0), pipeline_mode=pl.Buffered(1))
    blk = pl.BlockSpec((STEP_HEADS, None, tq, LANE), lambda g, j: (g, j, 0, 0))
    sums = pl.BlockSpec((t, LANE), lambda g, j: (0, 0), pipeline_mode=pl.Buffered(1))
    return _pcall(
        body, name=name, grid=(FOX_GROUPS, nq),
        in_specs=[whole, whole, blk, blk, pl.BlockSpec(memory_space=pl.ANY)],
        out_specs=[pl.BlockSpec((t, slab), lambda g, j: (0, group0 + g)), sums, sums],
        out_shape=[jax.ShapeDtypeStruct(dproj.shape, BF16), jax.ShapeDtypeStruct((t, LANE), F32),
                   jax.ShapeDtypeStruct((t, LANE), F32)],
        input_output_aliases={4: 0},
        scratch_shapes=[pltpu.VMEM((STEP_HEADS, nq, tq, LANE), F32)],
        compiler_params=_params("arbitrary", "arbitrary"),
    )(qab, doa, ka, va, dproj)


HALO = 8


def _rows_ext(ref, r0, rows, t, before, after):
    lo, hi = r0 - before, r0 + rows + after
    width = ref.shape[-1]
    parts = []
    if lo < 0:
        parts.append(jnp.zeros((-lo, width), F32))
    parts.append(ref[max(lo, 0):min(hi, t), :].astype(F32))
    if hi > t:
        parts.append(jnp.zeros((hi - t, width), F32))
    return parts[0] if len(parts) == 1 else jnp.concatenate(parts, axis=0)


def _conv_taps(a_ext, r0_ext, cw_ref, cb_ref):
    n = a_ext.shape[0]
    if r0_ext < N_PAD:
        row = r0_ext + lax.broadcasted_iota(jnp.int32, (n, 1), 0)
        a_ext = jnp.where(row >= N_PAD, a_ext, 0.0)
    a1 = pltpu.roll(a_ext, 1, axis=0)
    a2 = pltpu.roll(a_ext, 2, axis=0)
    acc = cb_ref[...] + a2 * cw_ref[0:1, :] + a1 * cw_ref[1:2, :] + a_ext * cw_ref[2:3, :]
    return a_ext, a1, a2, acc


FF_COLS = 256


def _up_conv_fwd(n2, w_up_t, conv_w8, conv_b, name):
    t, d = n2.shape
    f = w_up_t.shape[1]
    rows = TOK_TILE
    starts = list(range(0, t, rows))

    def body(n_ref, wa_ref, wb_ref, cw_ref, cb_ref, up_ref, g_ref):
        def project(r0):
            n_rows = n_ref[r0:r0 + rows, :]
            up_ref[0, r0:r0 + rows, :] = _dot(n_rows, wa_ref[...], NT)
            up_ref[1, r0:r0 + rows, :] = _dot(n_rows, wb_ref[...], NT)

        def activate(r0):
            a_ext = _rows_ext(up_ref.at[0], r0, rows, t, HALO, 0)
            _, _, _, acc = _conv_taps(a_ext, r0 - HALO, cw_ref, cb_ref)
            acc = acc[HALO:, :]
            g_ref[r0:r0 + rows, :] = (acc * _sigmoid(acc) * up_ref[1, r0:r0 + rows, :]).astype(BF16)

        project(starts[0])
        for r0, r_next in zip(starts, starts[1:] + [None]):
            if r_next is not None:
                project(r_next)
            activate(r0)

    return _pcall(
        body, name=name, grid=(f // FF_COLS,),
        in_specs=[pl.BlockSpec((t, d), lambda j: (0, 0), pipeline_mode=pl.Buffered(1)),
                  pl.BlockSpec((None, FF_COLS, d), lambda j: (0, j, 0)), pl.BlockSpec((None, FF_COLS, d), lambda j: (1, j, 0)),
                  pl.BlockSpec((8, FF_COLS), lambda j: (0, j)), pl.BlockSpec((1, FF_COLS), lambda j: (0, j))],
        out_specs=[pl.BlockSpec((2, t, FF_COLS), lambda j: (0, 0, j)), pl.BlockSpec((t, FF_COLS), lambda j: (0, j))],
        out_shape=[jax.ShapeDtypeStruct((2, t, f), F32), jax.ShapeDtypeStruct((t, f), BF16)],
        compiler_params=_params("parallel"),
    )(n2, w_up_t, w_up_t, conv_w8, conv_b)


def _dg_conv_bwd(up, conv_w8, conv_b, dh2, w_down, name):
    _, t, f = up.shape
    d = dh2.shape[1]
    rows = TOK_TILE
    starts = list(range(0, t, rows))

    def body(a_ref, b_ref, cw_ref, cb_ref, dh_ref, wd_ref, dup_ref, gcw_ref, gcb_ref, dg_ref):
        def project(r0):
            dg_ref[r0:r0 + rows, :] = _dot(dh_ref[r0:r0 + rows, :], wd_ref[...], NT)

        gw = [jnp.zeros((1, FF_COLS), F32) for _ in range(3)]
        gb = jnp.zeros((1, FF_COLS), F32)
        project(starts[0])
        for r0, r_next in zip(starts, starts[1:] + [None]):
            if r_next is not None:
                project(r_next)
            a_ext = _rows_ext(a_ref, r0, rows, t, HALO, HALO)
            b_ext = _rows_ext(b_ref, r0, rows, t, HALO, HALO)
            dg_ext = _rows_ext(dg_ref, r0, rows, t, HALO, HALO)
            a0, a1, a2, acc = _conv_taps(a_ext, r0 - HALO, cw_ref, cb_ref)
            sg = _sigmoid(acc)
            dacc = dg_ext * b_ext * (sg * (1.0 + acc * (1.0 - sg)))
            n = dacc.shape[0]
            da = (dacc * cw_ref[2:3, :] + pltpu.roll(dacc, n - 1, axis=0) * cw_ref[1:2, :]
                  + pltpu.roll(dacc, n - 2, axis=0) * cw_ref[0:1, :])
            core = slice(HALO, HALO + rows)
            da = da[core, :]
            if r0 < N_PAD:
                row = r0 + lax.broadcasted_iota(jnp.int32, (rows, 1), 0)
                da = jnp.where(row >= N_PAD, da, 0.0)
            dup_ref[0, r0:r0 + rows, :] = da.astype(BF16)
            dup_ref[1, r0:r0 + rows, :] = (dg_ext * acc * sg)[core, :].astype(BF16)
            dacc_c = dacc[core, :]
            gw[0] = gw[0] + jnp.sum(dacc_c * a2[core, :], axis=0, keepdims=True)
            gw[1] = gw[1] + jnp.sum(dacc_c * a1[core, :], axis=0, keepdims=True)
            gw[2] = gw[2] + jnp.sum(dacc_c * a0[core, :], axis=0, keepdims=True)
            gb = gb + jnp.sum(dacc_c, axis=0, keepdims=True)
        gcw_ref[...] = jnp.zeros((8, FF_COLS), F32)
        for tap in range(3):
            gcw_ref[tap:tap + 1, :] = gw[tap]
        gcb_ref[...] = gb

    return _pcall(
        body, name=name, grid=(f // FF_COLS,),
        in_specs=[pl.BlockSpec((None, t, FF_COLS), lambda j: (0, 0, j)), pl.BlockSpec((None, t, FF_COLS), lambda j: (1, 0, j)),
                  pl.BlockSpec((8, FF_COLS), lambda j: (0, j)), pl.BlockSpec((1, FF_COLS), lambda j: (0, j)),
                  pl.BlockSpec((t, d), lambda j: (0, 0), pipeline_mode=pl.Buffered(1)),
                  pl.BlockSpec((FF_COLS, d), lambda j: (j, 0))],
        out_specs=[pl.BlockSpec((2, t, FF_COLS), lambda j: (0, 0, j)), pl.BlockSpec((8, FF_COLS), lambda j: (0, j)),
                   pl.BlockSpec((1, FF_COLS), lambda j: (0, j))],
        out_shape=[jax.ShapeDtypeStruct((2, t, f), BF16), jax.ShapeDtypeStruct((8, f), F32),
                   jax.ShapeDtypeStruct((1, f), F32)],
        scratch_shapes=[pltpu.VMEM((t, FF_COLS), F32)],
        compiler_params=_params("parallel"),
    )(up, up, conv_w8, conv_b, dh2, w_down)


def _exchange(arrays, kinds, name, after=None):
    n = len(arrays)
    npeer = N_DEV - 1
    n_in = n + int(after is not None)

    def body(*refs):
        ins, outs = refs[:n], refs[n_in:n_in + n]
        send_sems, recv_sems, local_sems = refs[n_in + n:]
        x, y, c = lax.axis_index("x"), lax.axis_index("y"), lax.axis_index("c")
        me = 4 * x + 2 * y + c
        copies, locals_ = [], []
        for a in range(n):
            gather = kinds[a] == "gather"
            own = pltpu.make_async_copy(ins[a] if gather else ins[a].at[me], outs[a].at[me], local_sems.at[a])
            own.start()
            locals_.append(own)
            for d in range(1, N_DEV):
                px = 1 - x if d & 4 else x
                py = 1 - y if d & 2 else y
                pc = 1 - c if d & 1 else c
                src = ins[a] if gather else ins[a].at[4 * px + 2 * py + pc]
                cp = pltpu.make_async_remote_copy(
                    src_ref=src, dst_ref=outs[a].at[me],
                    send_sem=send_sems.at[a * npeer + d - 1], recv_sem=recv_sems.at[a * npeer + d - 1],
                    device_id=(px, py, pc), device_id_type=pl.DeviceIdType.MESH)
                cp.start()
                copies.append(cp)
        for cp in copies:
            cp.wait_recv()
        for cp in copies:
            cp.wait_send()
        for own in locals_:
            own.wait()

    out_shape = [jax.ShapeDtypeStruct((N_DEV,) + (a.shape if k == "gather" else a.shape[1:]), a.dtype)
                 for a, k in zip(arrays, kinds)]
    return _pcall(
        body, name=name,
        in_specs=[pl.BlockSpec(memory_space=pl.ANY)] * n_in,
        out_specs=[pl.BlockSpec(memory_space=pl.ANY)] * n,
        out_shape=out_shape,
        scratch_shapes=[pltpu.SemaphoreType.DMA((n * npeer,)), pltpu.SemaphoreType.DMA((n * npeer,)),
                        pltpu.SemaphoreType.DMA((n,))],
        compiler_params=pltpu.CompilerParams(has_side_effects=True),
    )(*arrays, *([] if after is None else [after]))


ALL_PEERS = tuple(range(1, N_DEV))
SAME_CORE_AND_SIBLING = (1, 2, 4, 6)
OTHER_CHIPS = (2, 4, 6)


def _peer_copies(srcs, lands, kinds, send_sems, recv_sems, relations=ALL_PEERS):
    x, y, c = lax.axis_index("x"), lax.axis_index("y"), lax.axis_index("c")
    me = 4 * x + 2 * y + c
    copies = []
    for a in range(len(srcs)):
        for d in relations:
            px = 1 - x if d & 4 else x
            py = 1 - y if d & 2 else y
            pc = 1 - c if d & 1 else c
            peer = 4 * px + 2 * py + pc
            k = a * (N_DEV - 1) + d - 1
            if kinds[a] == "forward":
                src, dst, target = lands[a].at[peer], lands[a].at[peer], (x, y, 1 - c)
            else:
                src, dst, target = (srcs[a] if kinds[a] == "gather" else srcs[a].at[peer]), lands[a].at[me], (px, py, pc)
            copies.append(pltpu.make_async_remote_copy(
                src_ref=src, dst_ref=dst, send_sem=send_sems.at[k], recv_sem=recv_sems.at[k],
                device_id=target, device_id_type=pl.DeviceIdType.MESH))
    return copies


def _exchange_start(arrays, kinds, name, after=None, relations=ALL_PEERS, lands=None):
    n = len(arrays)
    nsem = n * (N_DEV - 1)
    hbm = pl.BlockSpec(memory_space=pltpu.HBM)
    sem = pl.BlockSpec(memory_space=pltpu.SEMAPHORE)
    land_shapes = ([l.shape for l in lands] if lands is not None else
                   [(N_DEV,) + (a.shape if k == "gather" else a.shape[1:]) for a, k in zip(arrays, kinds)])

    n_in = 2 * n + int(after is not None)

    def body(*refs):
        srcs, land_refs = refs[:n], refs[n:2 * n]
        send_sems, recv_sems = refs[n_in], refs[n_in + 1]
        token = refs[-1]
        for cp in _peer_copies(srcs, land_refs, kinds, send_sems, recv_sems, relations):
            cp.start()
        token[...] = jnp.zeros_like(token)

    operands = [pltpu.with_memory_space_constraint(a, pltpu.HBM) for a in arrays]
    operands += (list(lands) if lands is not None else
                 [pltpu.with_memory_space_constraint(lax.empty(s, a.dtype), pltpu.HBM) for s, a in zip(land_shapes, arrays)])
    operands += [] if after is None else [after]
    out = _pcall(
        body, name=name,
        in_specs=[hbm] * (2 * n) + ([] if after is None else [pl.BlockSpec(memory_space=pl.ANY)]),
        out_specs=[sem, sem] + [hbm] * (2 * n) + [pl.BlockSpec(memory_space=pltpu.VMEM)],
        out_shape=[pltpu.SemaphoreType.DMA((nsem,)), pltpu.SemaphoreType.DMA((nsem,))]
        + [pltpu.HBM(a.shape, a.dtype) for a in arrays]
        + [pltpu.HBM(s, a.dtype) for s, a in zip(land_shapes, arrays)]
        + [jax.ShapeDtypeStruct((8, LANE), F32)],
        input_output_aliases={k: 2 + k for k in range(2 * n)},
        compiler_params=pltpu.CompilerParams(has_side_effects=pltpu.SideEffectType.DATAFLOW_SIDE_EFFECTING),
    )(*operands)
    return out[0], out[1], list(out[2:2 + n]), list(out[2 + n:2 + 2 * n]), out[-1]


def _exchange_wait(started, kinds, after, name, fill_own=True, relations=ALL_PEERS):
    send_sems, recv_sems, srcs, lands, _ = started
    n = len(srcs)
    hbm = pl.BlockSpec(memory_space=pltpu.HBM)
    sem = pl.BlockSpec(memory_space=pltpu.SEMAPHORE)

    def body(*refs):
        src_refs, land_refs = refs[:n], refs[n:2 * n]
        copies = _peer_copies(src_refs, land_refs, kinds, refs[2 * n], refs[2 * n + 1], relations)
        for cp in copies:
            cp.wait_send()
        for cp in copies:
            cp.wait_recv()

    out = _pcall(
        body, name=name,
        in_specs=[hbm] * (2 * n) + [sem, sem, pl.BlockSpec(memory_space=pl.ANY)],
        out_specs=[hbm] * (2 * n),
        out_shape=[pltpu.HBM(a.shape, a.dtype) for a in srcs + lands],
        input_output_aliases={k: k for k in range(2 * n)},
        compiler_params=pltpu.CompilerParams(has_side_effects=pltpu.SideEffectType.DATAFLOW_SIDE_EFFECTING),
    )(*srcs, *lands, send_sems, recv_sems, after)
    if not fill_own:
        return list(out[:n]), list(out[n:])
    me = 4 * lax.axis_index("x") + 2 * lax.axis_index("y") + lax.axis_index("c")
    filled = []
    for src, land, kind in zip(out[:n], out[n:], kinds):
        own = lax.dynamic_index_in_dim(src, me, axis=0, keepdims=False) if kind == "scatter" else src
        filled.append(lax.dynamic_update_slice(land, own[None], (me,) + (0,) * own.ndim))
    return filled


def _sum_slots(slots, name, rows_tile):
    nd, r, c = slots.shape

    def body(s_ref, o_ref):
        acc = s_ref[0].astype(F32)
        for p in range(1, nd):
            acc = acc + s_ref[p].astype(F32)
        o_ref[...] = acc

    return _pcall(
        body, name=name, grid=(r // rows_tile,),
        in_specs=[pl.BlockSpec((nd, rows_tile, c), lambda i: (0, i, 0))],
        out_specs=pl.BlockSpec((rows_tile, c), lambda i: (i, 0)),
        out_shape=jax.ShapeDtypeStruct((r, c), F32),
        compiler_params=_params("parallel"),
    )(slots)


def _sum_slots_small(slot_arrays, own_arrays, name):
    n = len(slot_arrays)

    def body(*refs):
        me = 4 * lax.axis_index("x") + 2 * lax.axis_index("y") + lax.axis_index("c")
        for s_ref, own_ref, o_ref in zip(refs[:n], refs[n:2 * n], refs[2 * n:]):
            acc = jnp.where(me == 0, own_ref[...], s_ref[0])
            for p in range(1, s_ref.shape[0]):
                acc = acc + jnp.where(me == p, own_ref[...], s_ref[p])
            o_ref[...] = acc

    return _pcall(body, name=name, out_shape=[jax.ShapeDtypeStruct(a.shape[1:], F32) for a in slot_arrays])(
        *slot_arrays, *own_arrays)


def _adamw_update(w_ref, g_ref, m_ref, v_ref, d_ref, nm_ref, nv_ref):
    gr = g_ref[...]
    nm = ADAM_B1 * m_ref[...] + (1.0 - ADAM_B1) * gr
    nv = ADAM_B2 * v_ref[...] + (1.0 - ADAM_B2) * (gr * gr)
    m_hat = nm / (1.0 - ADAM_B1 ** ADAM_STEP)
    v_hat = nv / (1.0 - ADAM_B2 ** ADAM_STEP)
    d_ref[...] = -ADAM_LR * (m_hat / (jnp.sqrt(v_hat) + ADAM_EPS) + ADAM_WD * w_ref[...])
    nm_ref[...] = nm
    nv_ref[...] = nv


def _adamw_small(ws, gs, ms, vs, name):
    n = len(ws)

    def body(*refs):
        ins, outs = refs[:4 * n], refs[4 * n:]
        for k in range(n):
            _adamw_update(ins[k], ins[n + k], ins[2 * n + k], ins[3 * n + k], outs[k], outs[n + k], outs[2 * n + k])

    shapes = [jax.ShapeDtypeStruct(w.shape, F32) for w in ws]
    out = _pcall(body, name=name, out_shape=shapes * 3)(*ws, *gs, *ms, *vs)
    return list(out[:n]), list(out[n:2 * n]), list(out[2 * n:])


def _adamw(w, g, m, v, name, rows_tile):
    r, c = w.shape
    body = lambda *refs: _adamw_update(*refs)
    spec = pl.BlockSpec((rows_tile, c), lambda i: (i, 0))
    shp = jax.ShapeDtypeStruct((r, c), F32)
    return _pcall(
        body, name=name, grid=(r // rows_tile,), in_specs=[spec] * 4, out_specs=[spec] * 3, out_shape=[shp] * 3,
        compiler_params=_params("parallel"),
    )(w, g, m, v)


F0 = 2 * RET_QK + 2 * RET_V


def _to_internal_rows(w_t):
    cols = w_t.shape[1]
    fox = w_t[F0:F0 + 3 * FOX_W].reshape(3, FOX_PAIRS, LANE, cols).transpose(1, 0, 2, 3).reshape(3 * FOX_W, cols)
    tail = jnp.zeros((IN_PAD - IN_WIDTH, cols), w_t.dtype)
    return jnp.concatenate([w_t[:F0], fox, w_t[F0 + 3 * FOX_W:], tail], axis=0)


def _from_internal_rows(g_t):
    cols = g_t.shape[1]
    fox = g_t[F0:F0 + 3 * FOX_W].reshape(FOX_PAIRS, 3, LANE, cols).transpose(1, 0, 2, 3).reshape(3 * FOX_W, cols)
    return jnp.concatenate([g_t[:F0], fox, g_t[F0 + 3 * FOX_W:F0 + 3 * FOX_W + FOX_HEADS]], axis=0)


def _local_step(x, target, meta, attn_g, fox_b, ret_g, ffn_g, conv_w8, conv_b, final_g,
                first_weight, late_weights, ffn_grads_ready, out_grad_ready, in_grad_ready):
    seq, d = x.shape
    t = seq + PREFIX
    tm = TOK_TILE
    nq = t // tm
    fox_b128 = jnp.pad(fox_b, ((0, 0), (0, LANE - FOX_HEADS)))

    h0, n1 = _prep_norm(x, meta, attn_g, "prep_norm")
    w_in_t = first_weight(n1)
    proj = _mm_simple(n1, w_in_t, mode="nt", tm=tm, tn=IN_PAD, tk=d, out_dtype=F32, name="mm_in")
    cos, sin = _rope_tables(t)
    o_pre, mixed, states = _ret_fwd(proj, cos, sin, ret_g, "ret_fwd")
    c = _forget_cumsum(proj, fox_b128, "forget_cumsum")
    qa, ka, va, qt, vt = _fox_prep(proj, c, "fox_prep")
    by_block = lambda a: a.reshape(FOX_HEADS, nq, tm, LANE)
    mixed, o_fox, lse = _fox_fwd(qt, by_block(ka), vt, mixed, "fox_fwd")
    w_out, w_up_t, w_down = late_weights(o_fox)
    tile = pl.BlockSpec((tm, d), lambda i: (i, 0))
    row_vec = pl.BlockSpec((1, d), lambda i: (0, 0))
    resident = lambda shape: pl.BlockSpec(shape, lambda i: (0,) * len(shape), pipeline_mode=pl.Buffered(1))
    acts = lambda dtype: jax.ShapeDtypeStruct((t, d), dtype)
    vec = jax.ShapeDtypeStruct((1, d), F32)

    def residual_and_norm(i, acc, ins, outs):
        h = acc + ins[0][...]
        outs[0][...] = h
        outs[1][...] = (h * lax.rsqrt(jnp.mean(h * h, axis=-1, keepdims=True) + EPS) * ins[1][...]).astype(BF16)

    h1, n2 = _matmul_rows([mixed], [tile], [w_out], [resident((d, d))], [h0, ffn_g], [tile, row_vec],
                          [tile, tile], [acts(F32), acts(BF16)], residual_and_norm, mode="nn", steps=nq, name="mm_out_norm")
    nf = D_FF // 1408
    up, g = _up_conv_fwd(n2, w_up_t, conv_w8, conv_b, "up_conv_fwd")

    def residual_loss_bwd(i, acc, ins, outs):
        loss_ref, dh_ref, dhb_ref, gg_ref = outs
        part, dh, gg = _loss_tile(i, acc + ins[0][...], jnp.concatenate([ins[1][...], ins[2][...], ins[3][...]], axis=0),
                                  ins[4][...])
        _accumulate(loss_ref, i, jnp.broadcast_to(part, loss_ref.shape))
        dh_ref[...] = dh
        dhb_ref[...] = dh.astype(BF16)
        _accumulate(gg_ref, i, gg)

    loss_tile, dh2, dh2_b, g_final = _matmul_rows(
        [g], [pl.BlockSpec((tm, D_FF), lambda i: (i, 0))], [w_down], [resident((D_FF, d))],
        [h1, target, target, target, final_g], [tile] + _shifted_row_specs(d) + [row_vec],
        [pl.BlockSpec((8, LANE), lambda i: (0, 0)), tile, tile, row_vec],
        [jax.ShapeDtypeStruct((8, LANE), F32), acts(F32), acts(BF16), vec], residual_loss_bwd,
        mode="nn", steps=nq, name="mm_down_loss")

    tkw = 2112 if t % 2112 == 0 else tm
    gw_down = _mm_simple(g, dh2_b, mode="tn", tm=1408, tn=d, tk=tkw, out_dtype=BF16, name="mm_gw_down")
    dup, g_conv_w8, g_conv_b = _dg_conv_bwd(up, conv_w8, conv_b, dh2_b, w_down, "dg_conv_bwd")

    half = lambda p: pl.BlockSpec((None, tm, D_FF), lambda i: (p, i, 0))
    half_w = lambda p: pl.BlockSpec((None, D_FF, d), lambda i: (p, 0, 0), pipeline_mode=pl.Buffered(1))
    gw_up_t = _matmul(
        dup, n2, mode="tn", grid=(2 * nf, 1, t // tkw),
        a_spec=pl.BlockSpec((None, tkw, 1408), lambda i, j, k: (i // nf, k, i % nf)),
        b_spec=pl.BlockSpec((tkw, d), lambda i, j, k: (k, 0)),
        o_spec=pl.BlockSpec((1408, d), lambda i, j, k: (i, 0)),
        out_shape=jax.ShapeDtypeStruct((2 * D_FF, d), BF16), name="mm_gw_up")
    def norm_bwd_and_mixer_grad(i, acc, ins, outs):
        dh, gg = _rms_bwd_tile(acc, ins[0][...], ins[1][...], ins[2][...])
        outs[0][...] = dh
        _accumulate(outs[1], i, gg)
        outs[2][...] = _dot(dh.astype(BF16), ins[3][...], NT)

    dh1, g_ffn, dmixed = _matmul_rows(
        [dup, dup], [half(0), half(1)], [w_up_t, w_up_t], [half_w(0), half_w(1)],
        [h1, ffn_g, dh2, w_out], [tile, row_vec, tile, resident((d, d))], [tile, row_vec, tile],
        [acts(F32), vec, acts(F32)], norm_bwd_and_mixer_grad,
        mode="nn", steps=nq, name="mm_dn2_norm_bwd", after=ffn_grads_ready(gw_down, gw_up_t))
    gw_out = _mm_simple(mixed, dh1, mode="tn", tm=d, tn=d, tk=tkw, out_dtype=BF16, name="mm_gw_out")
    dproj, g_ret = _ret_bwd(proj, cos, sin, ret_g + out_grad_ready(gw_out), dmixed, o_pre, states, "ret_bwd")
    qab, doa = _fox_prep_bwd(dmixed, o_fox, lse, qa, "fox_prep_bwd")
    dproj, drs, dcs = _fox_bwd(by_block(qab), by_block(doa), by_block(ka), by_block(va), dproj, "fox_bwd")
    dproj, g_fox_b = _forget_cumsum_bwd(proj, fox_b128, drs, dcs, dproj, "forget_cumsum_bwd")
    gw_in_t = _mm_simple(dproj, n1, mode="tn", tm=640, tn=d, tk=tkw, out_dtype=BF16, name="mm_gw_in")
    sent = in_grad_ready(gw_in_t)
    def input_grads(i, acc, ins, outs):
        gx_ref, gmeta_ref, gg_ref, buf_ref, sems = outs
        dh, gg = _rms_bwd_tile(acc, ins[0][...], ins[1][...], ins[2][...])
        _accumulate(gg_ref, i, gg)
        slot = i % 2

        def first_copy():
            return pltpu.make_async_copy(buf_ref.at[0, pl.ds(PREFIX, tm - PREFIX)], gx_ref.at[pl.ds(0, tm - PREFIX)],
                                         sems.at[0])

        def tile_copy(tile, buf_slot):
            rows = pl.ds(pl.multiple_of(tile * tm - PREFIX, PREFIX), tm)
            return pltpu.make_async_copy(buf_ref.at[buf_slot], gx_ref.at[rows], sems.at[buf_slot])

        @pl.when(i == 1)
        def _():
            first_copy().wait()

        @pl.when(i >= 2)
        def _():
            tile_copy(i - 1, 1 - slot).wait()

        buf_ref[slot] = dh

        @pl.when(i == 0)
        def _():
            gmeta_ref[...] = dh[N_PAD:PREFIX, :]
            first_copy().start()

        @pl.when(i > 0)
        def _():
            tile_copy(i, slot).start()

        @pl.when(i == nq - 1)
        def _():
            tile_copy(i, slot).wait()

    grad_x, g_meta, g_attn = _matmul_rows(
        [dproj], [pl.BlockSpec((tm, IN_PAD), lambda i: (i, 0))], [w_in_t], [resident((IN_PAD, d))],
        [h0, attn_g, dh1], [tile, row_vec, tile],
        [pl.BlockSpec(memory_space=pl.ANY), pl.BlockSpec((N_META, d), lambda i: (0, 0)), row_vec],
        [jax.ShapeDtypeStruct((seq, d), F32), jax.ShapeDtypeStruct((N_META, d), F32), vec], input_grads,
        mode="nn", steps=nq, name="mm_dn1_norm_bwd", after=sent,
        scratch=[pltpu.VMEM((2, tm, d), F32), pltpu.SemaphoreType.DMA((2,))])

    grads = dict(meta=g_meta, attn_g=g_attn, fox_b=g_fox_b, ret_g=g_ret,
                 ffn_g=g_ffn, conv_w=g_conv_w8, conv_b=g_conv_b, final_g=g_final)
    return loss_tile, grad_x, grads


def kernel(x, meta_tokens, attn_norm_g, w_in, fox_forget_b, ret_norm_g, w_out, ffn_norm_g, w_up, conv_w, conv_b, w_down, final_norm_g, loss_target, m_meta_tokens, m_attn_norm_g, m_w_in, m_fox_forget_b, m_ret_norm_g, m_w_out, m_ffn_norm_g, m_w_up, m_conv_w, m_conv_b, m_w_down, m_final_norm_g, v_meta_tokens, v_attn_norm_g, v_w_in, v_fox_forget_b, v_ret_norm_g, v_w_out, v_ffn_norm_g, v_w_up, v_conv_w, v_conv_b, v_w_down, v_final_norm_g):
    d = D_MODEL
    me = 4 * lax.axis_index("x") + 2 * lax.axis_index("y") + lax.axis_index("c")
    in_blk = IN_WIDTH // N_DEV
    in_blk_pad = 400
    up_blk = 2 * D_FF // N_DEV
    down_blk = D_FF // N_DEV
    cw_blk = D_FF // N_DEV

    w_in_loc = jnp.pad(w_in[0].T.astype(BF16), ((0, in_blk_pad - in_blk), (0, 0)))
    cw_loc = jnp.pad(conv_w[0], ((0, 5), (0, 384 - cw_blk)))
    g_meta, g_cw = _exchange([meta_tokens, cw_loc], ["gather"] * 2, "gather_small")
    first = _exchange_start([w_in_loc], ["gather"], "gather_in_start", after=g_meta, relations=SAME_CORE_AND_SIBLING)
    rest_loc = [(w_out[0] + first[-1][0:1, 0:1]).astype(BF16), w_up[0].T.astype(BF16), w_down[0].astype(BF16)]
    rest = _exchange_start(rest_loc, ["gather"] * 3, "gather_rest_start")
    meta_f = g_meta.transpose(1, 0, 2).reshape(N_META, d)
    conv_w8 = jnp.pad(g_cw[:, :3, :cw_blk].transpose(1, 0, 2).reshape(3, D_FF), ((0, 5), (0, 0)))
    pending = {}

    def first_weight(after):
        own_in, landed = _exchange_wait(first, ["gather"], after, "gather_in_wait", fill_own=False,
                                        relations=SAME_CORE_AND_SIBLING)
        onward = _exchange_start(own_in, ["forward"], "gather_in_forward_start", relations=OTHER_CHIPS, lands=landed)
        (g_in,) = _exchange_wait(onward, ["forward"], onward[-1], "gather_in_forward_wait", relations=OTHER_CHIPS)
        return _to_internal_rows(g_in[:, :in_blk].reshape(IN_WIDTH, d))

    def in_grad_ready(gw_in_t):
        blocks = _from_internal_rows(gw_in_t).reshape(N_DEV, in_blk, d)
        blocks = jnp.pad(blocks, ((0, 0), (0, in_blk_pad - in_blk), (0, 0)))
        pending["in"] = _exchange_start([blocks], ["scatter"], "grads_in_start")
        return pending["in"][-1][0:1, 0:1]

    def late_weights(after):
        g_out, g_up, g_down = _exchange_wait(rest, ["gather"] * 3, after, "gather_rest_wait")
        return g_out.reshape(d, d), g_up.reshape(2, D_FF, d), g_down.reshape(D_FF, d)

    def ffn_grads_ready(gw_down, gw_up_t):
        blocks = [gw_down.reshape(N_DEV, down_blk, d), gw_up_t.reshape(N_DEV, up_blk, d)]
        pending["ffn"] = _exchange_start(blocks, ["scatter"] * 2, "grads_ffn_start")
        return pending["ffn"][-1][0:1, 0:1]

    def out_grad_ready(gw_out):
        pending["out"] = _exchange_start([gw_out.reshape(N_DEV, d // N_DEV, d)], ["scatter"], "grads_out_start")
        return pending["out"][-1][0:1, 0:1]

    loss_tile, grad_x, gr = _local_step(
        x[0], loss_target[0], meta_f, attn_norm_g + rest[-1][0:1, 0:1], fox_forget_b, ret_norm_g, ffn_norm_g,
        conv_w8, conv_b, final_norm_g.reshape(1, d), first_weight, late_weights, ffn_grads_ready, out_grad_ready,
        in_grad_ready)

    small = [loss_tile, gr["attn_g"], gr["fox_b"], gr["ret_g"], gr["ffn_g"], gr["conv_b"], gr["final_g"],
             gr["meta"], gr["conv_w"]]
    small_kinds = ["gather"] * len(small)
    small_started = _exchange_start(small, small_kinds, "grads_small_start")

    r_down, r_up = _exchange_wait(pending["ffn"], ["scatter"] * 2, small_started[-1], "grads_ffn_wait")
    (r_out,) = _exchange_wait(pending["out"], ["scatter"], small_started[-1], "grads_out_wait")
    g_w_out = _sum_slots(r_out, "sum_w_out", d // N_DEV)
    g_w_up_t = _sum_slots(r_up, "sum_w_up", up_blk)
    g_w_down = _sum_slots(r_down, "sum_w_down", down_blk)
    as_t = lambda a: a[0].T
    from_t = lambda a: a.T[None]
    d_w_out, m_w_out_n, v_w_out_n = [a[None] for a in _adamw(w_out[0], g_w_out, m_w_out[0], v_w_out[0], "adamw_w_out", 128)]
    up_t = _adamw(as_t(w_up), g_w_up_t, as_t(m_w_up), as_t(v_w_up), "adamw_w_up", up_blk // 2)
    d_w_up, m_w_up_n, v_w_up_n = [from_t(a) for a in up_t]
    d_w_down, m_w_down_n, v_w_down_n = [a[None] for a in _adamw(w_down[0], g_w_down, m_w_down[0], v_w_down[0],
                                                                "adamw_w_down", down_blk)]

    own_small, r_small = _exchange_wait(small_started, small_kinds, up_t[0], "grads_small_wait", fill_own=False)
    (loss_all, g_attn, g_fox_b128, g_ret, g_ffn, g_conv_b, g_final, g_meta_full, g_cw_full) = _sum_slots_small(
        r_small, own_small, "sum_small")
    loss = loss_all[0, 0]
    g_fox_b = g_fox_b128[:, :FOX_HEADS]
    g_meta_loc = lax.dynamic_slice(g_meta_full, (0, me * (d // N_DEV)), (N_META, d // N_DEV))
    g_cw_loc = lax.dynamic_slice(g_cw_full, (0, me * cw_blk), (3, cw_blk))

    (r_in,) = _exchange_wait(pending["in"], ["scatter"], r_small[0], "grads_in_wait")
    g_w_in_t = _sum_slots(r_in, "sum_w_in", in_blk_pad)[:in_blk]
    d_w_in, m_w_in_n, v_w_in_n = [from_t(a) for a in _adamw(as_t(w_in), g_w_in_t, as_t(m_w_in), as_t(v_w_in),
                                                            "adamw_w_in", in_blk)]
    g_w_in, g_w_up = g_w_in_t.T, g_w_up_t.T
    row = lambda a: a.reshape(1, d)
    sm_grads = [g_meta_loc, g_attn, g_fox_b, g_ret, g_ffn, g_cw_loc, g_conv_b, g_final]
    sm_w = [meta_tokens, attn_norm_g, fox_forget_b, ret_norm_g, ffn_norm_g, conv_w[0], conv_b, row(final_norm_g)]
    sm_m = [m_meta_tokens, m_attn_norm_g, m_fox_forget_b, m_ret_norm_g, m_ffn_norm_g, m_conv_w[0], m_conv_b,
            row(m_final_norm_g)]
    sm_v = [v_meta_tokens, v_attn_norm_g, v_fox_forget_b, v_ret_norm_g, v_ffn_norm_g, v_conv_w[0], v_conv_b,
            row(v_final_norm_g)]
    dl, ml, vl = [lst[:7] + [lst[7].reshape(d)] for lst in _adamw_small(sm_w, sm_grads, sm_m, sm_v, "adamw_small")]

    def by_weight(meta_, attn_, w_in_, fox_, ret_, w_out_, ffn_, w_up_, cw_, cb_, w_down_, final_):
        return (meta_, attn_, w_in_, fox_, ret_, w_out_, ffn_, w_up_, cw_[None], cb_, w_down_, final_)

    grads_out = by_weight(g_meta_loc, g_attn, g_w_in[None], g_fox_b, g_ret, g_w_out[None], g_ffn, g_w_up[None], g_cw_loc,
                          g_conv_b, g_w_down[None], g_final.reshape(d))
    delta_out = by_weight(dl[0], dl[1], d_w_in, dl[2], dl[3], d_w_out, dl[4], d_w_up, dl[5], dl[6], d_w_down, dl[7])
    m_out = by_weight(ml[0], ml[1], m_w_in_n, ml[2], ml[3], m_w_out_n, ml[4], m_w_up_n, ml[5], ml[6], m_w_down_n, ml[7])
    v_out = by_weight(vl[0], vl[1], v_w_in_n, vl[2], vl[3], v_w_out_n, vl[4], v_w_up_n, vl[5], vl[6], v_w_down_n, vl[7])
    return (loss, grad_x[None]) + grads_out + delta_out + m_out + v_out
```

```python
import numpy as np
import jax
import jax.numpy as jnp
from jax import lax
from jax.experimental import pallas as pl
from jax.experimental.pallas import tpu as pltpu

F32 = jnp.float32
BF16 = jnp.bfloat16

D_MODEL = 1024
N_META = 16
N_PAD = 112
PREFIX = 128
RET_HEADS = 4
RET_DK = 64
RET_DV = 128
FOX_HEADS = 8
FOX_DH = 64
D_FF = 2816
ROPE_BASE = 10000.0
EPS = 1e-6
NEG = -1e30
RET_QK = RET_HEADS * RET_DK
RET_V = RET_HEADS * RET_DV
FOX_W = FOX_HEADS * FOX_DH
IN_WIDTH = 2 * RET_QK + 2 * RET_V + 3 * FOX_W + FOX_HEADS
IN_PAD = 3200
FF_COL_BLOCK = (IN_WIDTH - FOX_HEADS) // 128
QK_SCALE = 0.125

ADAM_LR = 0.001
ADAM_B1 = 0.9
ADAM_B2 = 0.999
ADAM_EPS = 1e-08
ADAM_WD = 0.01
ADAM_STEP = 10

N_DEV = 8
LANE = 128
ROW_TILE = 128
TOK_TILE = 384

NN = (((1,), (0,)), ((), ()))
NT = (((1,), (1,)), ((), ()))
TN = (((0,), (0,)), ((), ()))


def _pcall(body, **kw):
    return pl.pallas_call(body, **kw)


def _params(*sem):
    return pltpu.CompilerParams(dimension_semantics=sem)


def _dot(a, b, dims=NN):
    return lax.dot_general(a, b, dims, preferred_element_type=F32)


def _sigmoid(x):
    return 0.5 * jnp.tanh(0.5 * x) + 0.5


def _matmul(a, b, *, mode, grid, a_spec, b_spec, o_spec, out_shape, name, add=None, add_spec=None, after=None):
    dims = {"nn": NN, "nt": NT, "tn": TN}[mode]
    nk = grid[2]
    has_add = add is not None
    a_list, b_list = (list(a), list(b)) if isinstance(a, (list, tuple)) else ([a], [b])
    a_specs, b_specs = (list(a_spec), list(b_spec)) if isinstance(a_spec, (list, tuple)) else ([a_spec], [b_spec])
    nt = len(a_list)
    n_in = 2 * nt + int(has_add) + int(after is not None)

    def body(*refs):
        a_refs, b_refs = refs[:nt], refs[nt:2 * nt]
        add_ref = refs[2 * nt] if has_add else None
        o_ref = refs[n_in]
        part = _dot(a_refs[0][...].astype(BF16), b_refs[0][...].astype(BF16), dims)
        for ar, br in zip(a_refs[1:], b_refs[1:]):
            part = part + _dot(ar[...].astype(BF16), br[...].astype(BF16), dims)

        def finish(acc):
            if has_add:
                acc = acc + add_ref[...]
            o_ref[...] = acc.astype(o_ref.dtype)

        if nk == 1:
            finish(part)
        else:
            acc_ref = refs[-1]
            k = pl.program_id(2)

            @pl.when(k == 0)
            def _():
                acc_ref[...] = part

            @pl.when(k > 0)
            def _():
                acc_ref[...] += part

            @pl.when(k == nk - 1)
            def _():
                finish(acc_ref[...])

    in_specs = a_specs + b_specs + ([add_spec] if has_add else [])
    args = tuple(a_list) + tuple(b_list) + ((add,) if has_add else ())
    if after is not None:
        in_specs, args = in_specs + [pl.BlockSpec(memory_space=pl.ANY)], args + (after,)
    scratch = [] if nk == 1 else [pltpu.VMEM(tuple(d for d in o_spec.block_shape if d is not None), F32)]
    return _pcall(
        body, name=name, grid=grid, in_specs=in_specs, out_specs=o_spec, out_shape=out_shape,
        scratch_shapes=scratch, compiler_params=_params("parallel", "parallel", "arbitrary"),
    )(*args)


def _mm_simple(a, b, *, mode, tm, tn, tk, out_dtype, name, add=None, after=None):
    if mode == "tn":
        K, M = a.shape
    else:
        M, K = a.shape
    N = b.shape[0] if mode == "nt" else b.shape[1]
    grid = (M // tm, N // tn, K // tk)
    resident = dict(pipeline_mode=pl.Buffered(1)) if (tn == N and tk == K) else {}
    a_spec = pl.BlockSpec((tk, tm), lambda i, j, k: (k, i)) if mode == "tn" else pl.BlockSpec((tm, tk), lambda i, j, k: (i, k))
    b_spec = (pl.BlockSpec((tn, tk), lambda i, j, k: (j, k), **resident) if mode == "nt"
              else pl.BlockSpec((tk, tn), lambda i, j, k: (k, j), **resident))
    o_spec = pl.BlockSpec((tm, tn), lambda i, j, k: (i, j))
    return _matmul(a, b, mode=mode, grid=grid, a_spec=a_spec, b_spec=b_spec, o_spec=o_spec,
                   out_shape=jax.ShapeDtypeStruct((M, N), out_dtype), name=name, add=add,
                   add_spec=o_spec if add is not None else None, after=after)


def _matmul_rows(a_list, a_specs, b_list, b_specs, extras, extra_specs, out_specs, out_shape, epilogue, *,
                 mode, steps, name, after=None, scratch=()):
    dims = {"nn": NN, "nt": NT}[mode]
    nt, ne = len(a_list), len(extras)
    n_in = 2 * nt + ne + int(after is not None)

    def body(*refs):
        acc = _dot(refs[0][...].astype(BF16), refs[nt][...].astype(BF16), dims)
        for k in range(1, nt):
            acc = acc + _dot(refs[k][...].astype(BF16), refs[nt + k][...].astype(BF16), dims)
        epilogue(pl.program_id(0), acc, refs[2 * nt:2 * nt + ne], refs[n_in:])

    in_specs = list(a_specs) + list(b_specs) + list(extra_specs)
    args = tuple(a_list) + tuple(b_list) + tuple(extras)
    if after is not None:
        in_specs, args = in_specs + [pl.BlockSpec(memory_space=pl.ANY)], args + (after,)
    return _pcall(body, name=name, grid=(steps,), in_specs=in_specs, out_specs=out_specs, out_shape=out_shape,
                  scratch_shapes=list(scratch), compiler_params=_params("arbitrary"))(*args)


def _rms_bwd_tile(dy, x, gain, dres):
    r = lax.rsqrt(jnp.mean(x * x, axis=-1, keepdims=True) + EPS)
    xhat = x * r
    u = dy * gain
    return dres + r * (u - xhat * jnp.mean(u * xhat, axis=-1, keepdims=True)), jnp.sum(dy * xhat, axis=0, keepdims=True)


def _loss_tile(i, x, tgt, gain):
    d = x.shape[-1]
    r = lax.rsqrt(jnp.mean(x * x, axis=-1, keepdims=True) + EPS)
    xhat = x * r
    counted = (i * TOK_TILE + lax.broadcasted_iota(jnp.int32, (TOK_TILE, 1), 0)) >= PREFIX
    err = jnp.where(counted, xhat * gain - tgt, 0.0)
    dy = err * (1.0 / d)
    u = dy * gain
    dh = r * (u - xhat * jnp.mean(u * xhat, axis=-1, keepdims=True))
    return 0.5 * jnp.sum(jnp.mean(err * err, axis=-1, keepdims=True)), dh, jnp.sum(dy * xhat, axis=0, keepdims=True)


def _accumulate(ref, i, part):
    @pl.when(i == 0)
    def _():
        ref[...] = part

    @pl.when(i > 0)
    def _():
        ref[...] += part


def _prep_norm(x, meta, gain, name):
    seq, d = x.shape
    t = seq + PREFIX

    def body(xa_ref, xb_ref, xc_ref, meta_ref, g_ref, h_ref, n_ref):
        i = pl.program_id(0)

        @pl.when(i == 0)
        def _():
            h_ref[0:N_PAD, :] = jnp.zeros((N_PAD, d), F32)
            h_ref[N_PAD:ROW_TILE, :] = meta_ref[...]

        @pl.when(i > 0)
        def _():
            h_ref[0:ROW_TILE, :] = xa_ref[...]

        h_ref[ROW_TILE:2 * ROW_TILE, :] = xb_ref[...]
        h_ref[2 * ROW_TILE:3 * ROW_TILE, :] = xc_ref[...]
        h = h_ref[...]
        r = lax.rsqrt(jnp.mean(h * h, axis=-1, keepdims=True) + EPS)
        n_ref[...] = (h * r * g_ref[...]).astype(BF16)

    return _pcall(
        body, name=name, grid=(t // TOK_TILE,),
        in_specs=_shifted_row_specs(d) + [pl.BlockSpec((N_META, d), lambda i: (0, 0)), pl.BlockSpec((1, d), lambda i: (0, 0))],
        out_specs=[pl.BlockSpec((TOK_TILE, d), lambda i: (i, 0)), pl.BlockSpec((TOK_TILE, d), lambda i: (i, 0))],
        out_shape=[jax.ShapeDtypeStruct((t, d), F32), jax.ShapeDtypeStruct((t, d), BF16)],
        compiler_params=_params("parallel"),
    )(x, x, x, meta, gain)


def _shifted_row_specs(d):
    blocks_per_tile = TOK_TILE // ROW_TILE
    return [pl.BlockSpec((ROW_TILE, d), lambda i, r=r: (jnp.maximum(blocks_per_tile * i + r, 0), 0)) for r in (-1, 0, 1)]


def _ret_consts(bk):
    gam = 1.0 - 2.0 ** (-5.0 - np.arange(RET_HEADS))
    n = np.arange(bk)
    same_or_earlier_chunk = (n[None, :] // 64) <= (n[:, None] // 64)
    w = gam[:, None, None] ** np.abs(n[:, None] - n[None, :])[None] * same_or_earlier_chunk[None]
    wq = gam[:, None] ** (n[None, :] + 1.0)
    wk = gam[:, None] ** (bk - 1.0 - n[None, :])
    mask = (np.arange(RET_QK)[None, :] // RET_DK) == np.arange(RET_HEADS)[:, None]
    return (jnp.asarray(w, F32), jnp.asarray(wq[:, :, None], F32), jnp.asarray(wk[:, :, None], F32),
            jnp.asarray(mask[:, None, :], F32), [float(g ** bk) for g in gam])


def _rope_tables(t):
    half = RET_DK // 2
    inv = 1.0 / (ROPE_BASE ** (jnp.arange(half, dtype=F32) / half))
    ang = jnp.arange(t).astype(F32)[:, None] * inv[None, :]
    cos, sin = jnp.cos(ang), jnp.sin(ang)
    return (jnp.tile(jnp.concatenate([cos, cos], axis=1), (1, RET_HEADS)),
            jnp.tile(jnp.concatenate([-sin, sin], axis=1), (1, RET_HEADS)))


def _swap_halves(x):
    outs = []
    for s in range(x.shape[1] // LANE):
        xs = x[:, LANE * s:LANE * (s + 1)]
        lane = lax.broadcasted_iota(jnp.int32, xs.shape, 1)
        outs.append(jnp.where((lane & 32) == 0, pltpu.roll(xs, LANE - 32, axis=1), pltpu.roll(xs, 32, axis=1)))
    return outs[0] if len(outs) == 1 else jnp.concatenate(outs, axis=1)


def _rope(x, cos, sin_signed):
    return x * cos + _swap_halves(x) * sin_signed


def _rope_t(dx, cos, sin_signed):
    return dx * cos + _swap_halves(dx * sin_signed)


def _ret_fwd(proj, cos, sin, gain, name):
    t = proj.shape[0]
    bk = TOK_TILE
    nb = t // bk
    w, wq, wk, mask, g_blk = _ret_consts(bk)

    def body(q_ref, k_ref, v_ref, rg_ref, cos_ref, sin_ref, w_ref, wq_ref, wk_ref, mask_ref, gain_ref,
             opre_ref, og_ref, st_ref, r_ref):
        i = pl.program_id(0)

        @pl.when(i == 0)
        def _():
            r_ref[...] = jnp.zeros_like(r_ref)

        c, s = cos_ref[...], sin_ref[...]
        valid = ((i * bk + lax.broadcasted_iota(jnp.int32, (bk, 1), 0)) >= N_PAD).astype(F32)
        qr = _rope(q_ref[...], c, s)
        kr = _rope(k_ref[...], c, s) * QK_SCALE * valid
        kb = kr.astype(BF16)
        for h in range(RET_HEADS):
            hm = mask_ref[h]
            cols = slice(RET_DV * h, RET_DV * (h + 1))
            vh = v_ref[:, cols].astype(BF16)
            r_prev = r_ref[h]
            st_ref[0, h] = r_prev
            sm = _dot((qr * hm).astype(BF16), kb, NT) * w_ref[h]
            o = _dot(sm.astype(BF16), vh) + _dot((qr * (hm * wq_ref[h])).astype(BF16), r_prev.astype(BF16))
            r_ref[h] = g_blk[h] * r_prev + _dot((kr * wk_ref[h]).astype(BF16), vh, TN)
            opre_ref[:, cols] = o
            rstd = lax.rsqrt(jnp.mean(o * o, axis=-1, keepdims=True) + EPS)
            rg = rg_ref[:, cols]
            og_ref[:, cols] = (o * rstd * gain_ref[:, cols] * (rg * _sigmoid(rg))).astype(BF16)

    full = lambda shape: pl.BlockSpec(shape, lambda i: (0,) * len(shape))
    return _pcall(
        body, name=name, grid=(nb,),
        in_specs=[pl.BlockSpec((bk, RET_QK), lambda i: (i, 0)), pl.BlockSpec((bk, RET_QK), lambda i: (i, 1)),
                  pl.BlockSpec((bk, RET_V), lambda i: (i, 1)), pl.BlockSpec((bk, RET_V), lambda i: (i, 2)),
                  pl.BlockSpec((bk, RET_QK), lambda i: (i, 0)), pl.BlockSpec((bk, RET_QK), lambda i: (i, 0)),
                  full((RET_HEADS, bk, bk)), full((RET_HEADS, bk, 1)), full((RET_HEADS, bk, 1)),
                  full((RET_HEADS, 1, RET_QK)), full((1, RET_V))],
        out_specs=[pl.BlockSpec((bk, RET_V), lambda i: (i, 0)), pl.BlockSpec((bk, RET_V), lambda i: (i, 0)),
                   pl.BlockSpec((1, RET_HEADS, RET_QK, RET_DV), lambda i: (i, 0, 0, 0))],
        out_shape=[jax.ShapeDtypeStruct((t, RET_V), F32), jax.ShapeDtypeStruct((t, RET_V + FOX_W), BF16),
                   jax.ShapeDtypeStruct((nb, RET_HEADS, RET_QK, RET_DV), F32)],
        scratch_shapes=[pltpu.VMEM((RET_HEADS, RET_QK, RET_DV), F32)],
        compiler_params=_params("arbitrary"),
    )(proj, proj, proj, proj, cos, sin, w, wq, wk, mask, gain)


def _ret_bwd(proj, cos, sin, gain, dmixed, opre, states, name):
    t = proj.shape[0]
    bk = TOK_TILE
    nb = t // bk
    w, wq, wk, mask, g_blk = _ret_consts(bk)
    v0, g0 = 2 * RET_QK, 2 * RET_QK + RET_V

    def body(q_ref, k_ref, v_ref, rg_ref, cos_ref, sin_ref, w_ref, wq_ref, wk_ref, mask_ref, gain_ref,
             dog_ref, opre_ref, st_ref, dp_ref, gg_ref, dr_ref):
        step = pl.program_id(0)
        i = nb - 1 - step

        @pl.when(step == 0)
        def _():
            dr_ref[...] = jnp.zeros_like(dr_ref)
            gg_ref[...] = jnp.zeros_like(gg_ref)

        c, s = cos_ref[...], sin_ref[...]
        valid = ((i * bk + lax.broadcasted_iota(jnp.int32, (bk, 1), 0)) >= N_PAD).astype(F32)
        qr = _rope(q_ref[...], c, s)
        kr = _rope(k_ref[...], c, s) * QK_SCALE * valid
        kb = kr.astype(BF16)
        dqr = jnp.zeros((bk, RET_QK), F32)
        dkr = jnp.zeros((bk, RET_QK), F32)
        for h in range(RET_HEADS):
            hm = mask_ref[h]
            cols = slice(RET_DV * h, RET_DV * (h + 1))
            vh = v_ref[:, cols].astype(BF16)
            o = opre_ref[:, cols]
            rstd = lax.rsqrt(jnp.mean(o * o, axis=-1, keepdims=True) + EPS)
            xhat = o * rstd
            rg = rg_ref[:, cols]
            sg = _sigmoid(rg)
            gate = rg * sg
            gn = gain_ref[:, cols]
            dog = dog_ref[:, cols]
            dp_ref[:, g0 + RET_DV * h:g0 + RET_DV * (h + 1)] = (
                dog * xhat * gn * (sg * (1.0 + rg * (1.0 - sg)))).astype(BF16)
            gg_ref[:, cols] += jnp.sum(dog * xhat * gate, axis=0, keepdims=True)
            dxh = dog * gn * gate
            do = (rstd * (dxh - xhat * jnp.mean(dxh * xhat, axis=-1, keepdims=True))).astype(BF16)
            qm = (qr * hm).astype(BF16)
            qw = (qr * (hm * wq_ref[h])).astype(BF16)
            kw = (kr * wk_ref[h]).astype(BF16)
            wh = w_ref[h]
            sm = (_dot(qm, kb, NT) * wh).astype(BF16)
            ds = (_dot(do, vh, NT) * wh).astype(BF16)
            dr = dr_ref[h]
            drb = dr.astype(BF16)
            dp_ref[:, v0 + RET_DV * h:v0 + RET_DV * (h + 1)] = (_dot(sm, do, TN) + _dot(kw, drb)).astype(BF16)
            dqr = dqr + _dot(ds, kb) * hm + _dot(do, st_ref[0, h].astype(BF16), NT) * (hm * wq_ref[h])
            dkr = dkr + _dot(ds, qm, TN) + _dot(vh, drb, NT) * wk_ref[h]
            dr_ref[h] = g_blk[h] * dr + _dot(qw, do, TN)
        dp_ref[:, 0:RET_QK] = _rope_t(dqr, c, s).astype(BF16)
        dp_ref[:, RET_QK:2 * RET_QK] = _rope_t(dkr * (QK_SCALE * valid), c, s).astype(BF16)

    full = lambda shape: pl.BlockSpec(shape, lambda i: (0,) * len(shape))
    rev = lambda col: (lambda i: (nb - 1 - i, col))
    return _pcall(
        body, name=name, grid=(nb,),
        in_specs=[pl.BlockSpec((bk, RET_QK), rev(0)), pl.BlockSpec((bk, RET_QK), rev(1)),
                  pl.BlockSpec((bk, RET_V), rev(1)), pl.BlockSpec((bk, RET_V), rev(2)),
                  pl.BlockSpec((bk, RET_QK), rev(0)), pl.BlockSpec((bk, RET_QK), rev(0)),
                  full((RET_HEADS, bk, bk)), full((RET_HEADS, bk, 1)), full((RET_HEADS, bk, 1)),
                  full((RET_HEADS, 1, RET_QK)), full((1, RET_V)),
                  pl.BlockSpec((bk, RET_V), rev(0)), pl.BlockSpec((bk, RET_V), rev(0)),
                  pl.BlockSpec((1, RET_HEADS, RET_QK, RET_DV), lambda i: (nb - 1 - i, 0, 0, 0))],
        out_specs=[pl.BlockSpec((bk, g0 + RET_V), rev(0)), pl.BlockSpec((1, RET_V), lambda i: (0, 0))],
        out_shape=[jax.ShapeDtypeStruct((t, IN_PAD), BF16), jax.ShapeDtypeStruct((1, RET_V), F32)],
        scratch_shapes=[pltpu.VMEM((RET_HEADS, RET_QK, RET_DV), F32)],
        compiler_params=_params("arbitrary"),
    )(proj, proj, proj, proj, cos, sin, w, wq, wk, mask, gain, dmixed, opre, states)


def _forget_cumsum(proj, bias, name):
    t = proj.shape[0]
    rt = TOK_TILE
    nb = t // rt
    tril = jnp.asarray(np.tril(np.ones((rt, rt))), F32)

    def body(z_ref, b_ref, tril_ref, c_ref, carry_ref):
        i = pl.program_id(0)

        @pl.when(i == 0)
        def _():
            carry_ref[...] = jnp.zeros_like(carry_ref)

        z = z_ref[...] + b_ref[...]
        logf = jnp.minimum(z, 0.0) - jnp.log(1.0 + jnp.exp(-jnp.abs(z)))
        c = lax.dot_general(tril_ref[...], logf, NN, precision=lax.Precision.HIGHEST,
                            preferred_element_type=F32) + carry_ref[...]
        c_ref[...] = c
        carry_ref[...] = c[rt - 1:rt, :]

    return _pcall(
        body, name=name, grid=(nb,),
        in_specs=[pl.BlockSpec((rt, LANE), lambda i: (i, FF_COL_BLOCK)), pl.BlockSpec((1, LANE), lambda i: (0, 0)),
                  pl.BlockSpec((rt, rt), lambda i: (0, 0))],
        out_specs=pl.BlockSpec((rt, LANE), lambda i: (i, 0)),
        out_shape=jax.ShapeDtypeStruct((t, LANE), F32),
        scratch_shapes=[pltpu.VMEM((1, LANE), F32)],
        compiler_params=_params("arbitrary"),
    )(proj, bias, tril)


def _forget_cumsum_bwd(proj, bias, drs, dcs, dproj, name):
    t = proj.shape[0]
    rt = TOK_TILE
    nb = t // rt
    triu = jnp.asarray(np.triu(np.ones((rt, rt))), F32)

    def body(z_ref, b_ref, triu_ref, drs_ref, dcs_ref, dproj_in, dz_ref, gb_ref, carry_ref):
        step = pl.program_id(0)

        @pl.when(step == 0)
        def _():
            carry_ref[...] = jnp.zeros_like(carry_ref)
            gb_ref[...] = jnp.zeros_like(gb_ref)

        dlogf = lax.dot_general(triu_ref[...], drs_ref[...] - dcs_ref[...], NN, precision=lax.Precision.HIGHEST,
                                preferred_element_type=F32) + carry_ref[...]
        carry_ref[...] = dlogf[0:1, :]
        z = z_ref[...] + b_ref[...]
        is_head = lax.broadcasted_iota(jnp.int32, (rt, LANE), 1) < FOX_HEADS
        dz = jnp.where(is_head, dlogf / (1.0 + jnp.exp(z)), 0.0)
        dz_ref[...] = dz.astype(BF16)
        gb_ref[...] += jnp.sum(dz, axis=0, keepdims=True)

    return _pcall(
        body, name=name, grid=(nb,),
        in_specs=[pl.BlockSpec((rt, LANE), lambda i: (nb - 1 - i, FF_COL_BLOCK)),
                  pl.BlockSpec((1, LANE), lambda i: (0, 0)),
                  pl.BlockSpec((rt, rt), lambda i: (0, 0)),
                  pl.BlockSpec((rt, LANE), lambda i: (nb - 1 - i, 0)),
                  pl.BlockSpec((rt, LANE), lambda i: (nb - 1 - i, 0)),
                  pl.BlockSpec(memory_space=pl.ANY)],
        out_specs=[pl.BlockSpec((rt, LANE), lambda i: (nb - 1 - i, FF_COL_BLOCK)),
                   pl.BlockSpec((1, LANE), lambda i: (0, 0))],
        out_shape=[jax.ShapeDtypeStruct(dproj.shape, BF16), jax.ShapeDtypeStruct((1, LANE), F32)],
        input_output_aliases={5: 0},
        scratch_shapes=[pltpu.VMEM((1, LANE), F32)],
        compiler_params=_params("arbitrary"),
    )(proj, bias, triu, drs, dcs, dproj)


FOX_PAIRS = FOX_HEADS // 2
L_ONE_Q = FOX_DH
L_ONE_K = FOX_DH + 3
L_LSE = FOX_DH + 4


def _split3(x):
    hi = x.astype(BF16).astype(F32)
    r = x - hi
    mid = r.astype(BF16).astype(F32)
    return hi, mid, r - mid


def _head_to_low(slab, e):
    return slab if e == 0 else pltpu.roll(slab, FOX_DH, axis=1)


def _pair(a, b, low):
    return jnp.where(low, a, pltpu.roll(b, FOX_DH, axis=1))


def _fox_prep(proj, c, name):
    t = proj.shape[0]
    tq = TOK_TILE

    def body(p_ref, c_ref, qa_ref, ka_ref, va_ref, qt_ref, vt_ref):
        i = pl.program_id(0)
        lane = lax.broadcasted_iota(jnp.int32, (tq, LANE), 1)
        low = lane < FOX_DH
        live = (i * tq + lax.broadcasted_iota(jnp.int32, (tq, 1), 0)) >= N_PAD
        q_tail = jnp.where(lane < L_ONE_Q + 3, 1.0, 0.0)
        k_ones = (lane >= L_ONE_K) & (lane < L_ONE_K + 4)
        v_tail = jnp.where(lane < FOX_DH + 2, 1.0, 0.0)
        bias_parts = _split3(jnp.where(live, -c_ref[...], NEG))
        for pair in range(FOX_PAIRS):
            base = 3 * LANE * pair
            for e in range(2):
                h = 2 * pair + e
                q = _head_to_low(p_ref[:, base:base + LANE], e)
                k = _head_to_low(p_ref[:, base + LANE:base + 2 * LANE], e)
                v = _head_to_low(p_ref[:, base + 2 * LANE:base + 3 * LANE], e)
                hi, mid, lo = [part[:, h:h + 1] for part in bias_parts]
                ka = jnp.where(low, k, jnp.where(k_ones, 1.0, 0.0))
                ka = jnp.where(lane == L_ONE_Q, hi, jnp.where(lane == L_ONE_Q + 1, mid, jnp.where(lane == L_ONE_Q + 2, lo, ka)))
                qa = jnp.where(low, q * QK_SCALE, q_tail)
                va = jnp.where(low, v, v_tail)
                qa_ref[h] = qa.astype(BF16)
                ka_ref[h] = ka.astype(BF16)
                va_ref[h] = va.astype(BF16)
                qt_ref[h] = qa.T.astype(BF16)
                vt_ref[h] = va.T.astype(BF16)

    out = jax.ShapeDtypeStruct((FOX_HEADS, t, LANE), BF16)
    out_t = jax.ShapeDtypeStruct((FOX_HEADS, t // tq, LANE, tq), BF16)
    ospec = pl.BlockSpec((FOX_HEADS, tq, LANE), lambda i: (0, i, 0))
    tspec = pl.BlockSpec((FOX_HEADS, None, LANE, tq), lambda i: (0, i, 0, 0))
    return _pcall(
        body, name=name, grid=(t // tq,),
        in_specs=[pl.BlockSpec((tq, 3 * FOX_W), lambda i: (i, 1)), pl.BlockSpec((tq, LANE), lambda i: (i, 0))],
        out_specs=[ospec, ospec, ospec, tspec, tspec], out_shape=[out, out, out, out_t, out_t],
        compiler_params=_params("parallel"),
    )(proj, c)


STEP_PAIRS = 2
STEP_HEADS = 2 * STEP_PAIRS
FOX_GROUPS = FOX_PAIRS // STEP_PAIRS
FWD_PAIRS = 4
FWD_HEADS = 2 * FWD_PAIRS
FWD_GROUPS = FOX_PAIRS // FWD_PAIRS


def _blockdiag(a, b):
    z = jnp.zeros_like(a)
    return jnp.concatenate([jnp.concatenate([a, z], axis=1), jnp.concatenate([z, b], axis=1)], axis=0)


def _fox_fwd(qt, ka, vt, mixed, name):
    nh, nq, tq, _ = ka.shape
    t = nq * tq

    def body(qt_ref, ka_ref, vt_ref, mixed_in, mixed_ref, o_ref, lse_ref):
        i = pl.program_id(1)
        lane = lax.broadcasted_iota(jnp.int32, (tq, LANE), 1)
        key_le_query = lax.broadcasted_iota(jnp.int32, (tq, tq), 0) <= lax.broadcasted_iota(jnp.int32, (tq, tq), 1)

        def logits(j):
            return [_dot(ka_ref[h, j], qt_ref[h]) for h in range(FWD_HEADS)]

        def update(j, scores, carry, diagonal):
            new = []
            for h in range(FWD_HEADS):
                m, acc = carry[h]
                s = jnp.where(key_le_query, scores[h], NEG) if diagonal else scores[h]
                m_new = jnp.maximum(m, jnp.max(s, axis=0, keepdims=True))
                p = jnp.exp(s - m_new).astype(BF16)
                new.append((m_new, jnp.exp(m - m_new) * acc + _dot(vt_ref[h, j], p)))
            return tuple(new)

        init = tuple((jnp.full((1, tq), NEG, F32), jnp.zeros((LANE, tq), F32)) for _ in range(FWD_HEADS))
        carry = lax.fori_loop(0, i, lambda j, cr: update(j, logits(j), cr, False), init)
        outs, lse_rows = [], []
        for m, acc in update(i, logits(i), carry, True):
            l = acc[FOX_DH:FOX_DH + 1, :]
            outs.append((acc / l).T)
            lse_rows.append(m + jnp.log(l))
        lse_rows.append(jnp.zeros((LANE - FWD_HEADS, tq), F32))
        o_all = jnp.concatenate([_pair(outs[2 * c], outs[2 * c + 1], lane < FOX_DH) for c in range(FWD_PAIRS)], axis=1)
        mixed_ref[...] = o_all.astype(BF16)
        o_ref[...] = o_all
        lse_ref[...] = jnp.concatenate(lse_rows, axis=0).T

    width = FWD_PAIRS * LANE
    whole = pl.BlockSpec((FWD_HEADS, nq, tq, LANE), lambda g, i: (g, 0, 0, 0), pipeline_mode=pl.Buffered(1))
    whole_t = pl.BlockSpec((FWD_HEADS, nq, LANE, tq), lambda g, i: (g, 0, 0, 0), pipeline_mode=pl.Buffered(1))
    return _pcall(
        body, name=name, grid=(FWD_GROUPS, nq),
        in_specs=[pl.BlockSpec((FWD_HEADS, None, LANE, tq), lambda g, i: (g, i, 0, 0)), whole, whole_t,
                  pl.BlockSpec(memory_space=pl.ANY)],
        out_specs=[pl.BlockSpec((tq, width), lambda g, i: (i, RET_V // width + g)),
                   pl.BlockSpec((tq, width), lambda g, i: (i, g)),
                   pl.BlockSpec((None, tq, LANE), lambda g, i: (g, i, 0))],
        out_shape=[jax.ShapeDtypeStruct(mixed.shape, BF16), jax.ShapeDtypeStruct((t, FOX_W), F32),
                   jax.ShapeDtypeStruct((FWD_GROUPS, t, LANE), F32)],
        input_output_aliases={3: 0},
        compiler_params=_params("parallel", "parallel"),
    )(qt, ka, vt, mixed)


def _fox_prep_bwd(dmixed, o_fox, lse, qa, name):
    t = dmixed.shape[0]
    tq = TOK_TILE

    def body(dm_ref, o_ref, lse_ref, qa_ref, qab_ref, doa_ref):
        i = pl.program_id(0)
        lane = lax.broadcasted_iota(jnp.int32, (tq, LANE), 1)
        low = lane < FOX_DH
        live = (i * tq + lax.broadcasted_iota(jnp.int32, (tq, 1), 0)) >= N_PAD
        lse_parts = [_split3(jnp.where(live, -lse_ref[grp], 0.0)) for grp in range(FWD_GROUPS)]
        for pair in range(FOX_PAIRS):
            cols = slice(LANE * pair, LANE * (pair + 1))
            d_slab = dm_ref[:, cols]
            prod = d_slab * o_ref[:, cols]
            for e in range(2):
                h = 2 * pair + e
                nd = -jnp.sum(jnp.where(low, _head_to_low(prod, e), 0.0), axis=-1, keepdims=True)
                nd_hi = nd.astype(BF16).astype(F32)
                doa = jnp.where(low, _head_to_low(d_slab, e), 0.0)
                doa = jnp.where(lane == FOX_DH, nd_hi, jnp.where(lane == FOX_DH + 1, nd - nd_hi, doa))
                doa_ref[h] = doa.astype(BF16)
                lane_h = h % FWD_HEADS
                hi, mid, lo = [part[:, lane_h:lane_h + 1] for part in lse_parts[h // FWD_HEADS]]
                qab = qa_ref[h].astype(F32)
                qab = jnp.where(lane == L_LSE, hi, jnp.where(lane == L_LSE + 1, mid, jnp.where(lane == L_LSE + 2, lo, qab)))
                qab_ref[h] = qab.astype(BF16)

    out = jax.ShapeDtypeStruct((FOX_HEADS, t, LANE), BF16)
    hspec = pl.BlockSpec((FOX_HEADS, tq, LANE), lambda i: (0, i, 0))
    return _pcall(
        body, name=name, grid=(t // tq,),
        in_specs=[pl.BlockSpec((tq, FOX_W), lambda i: (i, 1)), pl.BlockSpec((tq, FOX_W), lambda i: (i, 0)),
                  pl.BlockSpec((FWD_GROUPS, tq, LANE), lambda i: (0, i, 0)), hspec],
        out_specs=[hspec, hspec], out_shape=[out, out],
        compiler_params=_params("parallel"),
    )(dmixed, o_fox, lse, qa)


def _fox_bwd(qab, doa, ka, va, dproj, name):
    nh, nq, tq, _ = qab.shape
    t = nq * tq
    slab = 3 * LANE * STEP_PAIRS
    group0 = (2 * RET_QK + 2 * RET_V) // slab

    def body(qab_ref, doa_ref, ka_ref, va_ref, dproj_in, dp_ref, drs_ref, dcs_ref, dq_ref):
        g, j = pl.program_id(0), pl.program_id(1)

        @pl.when((g == 0) & (j == 0))
        def _():
            drs_ref[...] = jnp.zeros_like(drs_ref)
            dcs_ref[...] = jnp.zeros_like(dcs_ref)

        @pl.when(j == 0)
        def _():
            dq_ref[...] = jnp.zeros_like(dq_ref)

        lane = lax.broadcasted_iota(jnp.int32, (tq, LANE), 1)
        low = lane < FOX_DH
        key_le_query = lax.broadcasted_iota(jnp.int32, (tq, tq), 0) <= lax.broadcasted_iota(jnp.int32, (tq, tq), 1)

        def by_head(c, a, b, col):
            h = STEP_HEADS * g + 2 * c
            return jnp.where(lane == h, a[:, col:col + 1], jnp.where(lane == h + 1, b[:, col:col + 1], 0.0))


        def step(i, carry, diagonal):
            st = [_dot(ka_ref[h], qab_ref[h, i], NT) for h in range(STEP_HEADS)]
            dpt = [_dot(va_ref[h], doa_ref[h, i], NT) for h in range(STEP_HEADS)]
            new = []
            for h in range(STEP_HEADS):
                p = jnp.exp(st[h])
                if diagonal:
                    p = jnp.where(key_le_query, p, 0.0)
                ds = (p * dpt[h]).astype(BF16)
                dq_ref[h, i] += _dot(ds, ka_ref[h], TN)
                dk, dv = carry[h]
                new.append((dk + _dot(ds, qab_ref[h, i]), dv + _dot(p.astype(BF16), doa_ref[h, i])))
            return tuple(new)

        zero = jnp.zeros((tq, LANE), F32)
        carry = step(j, tuple((zero, zero) for _ in range(STEP_HEADS)), True)
        carry = lax.fori_loop(j + 1, nq, lambda i, cr: step(i, cr, False), carry)
        rows = pl.ds(pl.multiple_of(j * tq, tq), tq)
        for c in range(STEP_PAIRS):
            (dka, dva), (dkb, dvb) = carry[2 * c], carry[2 * c + 1]
            c0 = 3 * LANE * c
            dp_ref[rows, c0 + LANE:c0 + 2 * LANE] = _pair(dka, dkb, low).astype(BF16)
            dp_ref[rows, c0 + 2 * LANE:c0 + 3 * LANE] = _pair(dva, dvb, low).astype(BF16)
            dcs_ref[rows, :] += by_head(c, dka, dkb, L_ONE_Q)

        @pl.when(j == nq - 1)
        def _():
            for c in range(STEP_PAIRS):
                for blk in range(nq):
                    r = slice(blk * tq, (blk + 1) * tq)
                    a, b = dq_ref[2 * c, blk], dq_ref[2 * c + 1, blk]
                    dp_ref[r, 3 * LANE * c:3 * LANE * c + LANE] = (_pair(a, b, low) * QK_SCALE).astype(BF16)
                    drs_ref[r, :] += by_head(c, a, b, L_ONE_K)

    whole = pl.BlockSpec((STEP_HEADS, nq, tq, LANE), lambda g, j: (g, 0, 0, 0), pipeline_mode=pl.Buffered(1))
    blk = pl.BlockSpec((STEP_HEADS, None, tq, LANE), lambda g, j: (g, j, 0, 0))
    sums = pl.BlockSpec((t, LANE), lambda g, j: (0, 0), pipeline_mode=pl.Buffered(1))
    return _pcall(
        body, name=name, grid=(FOX_GROUPS, nq),
        in_specs=[whole, whole, blk, blk, pl.BlockSpec(memory_space=pl.ANY)],
        out_specs=[pl.BlockSpec((t, slab), lambda g, j: (0, group0 + g)), sums, sums],
        out_shape=[jax.ShapeDtypeStruct(dproj.shape, BF16), jax.ShapeDtypeStruct((t, LANE), F32),
                   jax.ShapeDtypeStruct((t, LANE), F32)],
        input_output_aliases={4: 0},
        scratch_shapes=[pltpu.VMEM((STEP_HEADS, nq, tq, LANE), F32)],
        compiler_params=_params("arbitrary", "arbitrary"),
    )(qab, doa, ka, va, dproj)


HALO = 8


def _rows_ext(ref, r0, rows, t, before, after):
    lo, hi = r0 - before, r0 + rows + after
    width = ref.shape[-1]
    parts = []
    if lo < 0:
        parts.append(jnp.zeros((-lo, width), F32))
    parts.append(ref[max(lo, 0):min(hi, t), :].astype(F32))
    if hi > t:
        parts.append(jnp.zeros((hi - t, width), F32))
    return parts[0] if len(parts) == 1 else jnp.concatenate(parts, axis=0)


def _conv_taps(a_ext, r0_ext, cw_ref, cb_ref):
    n = a_ext.shape[0]
    if r0_ext < N_PAD:
        row = r0_ext + lax.broadcasted_iota(jnp.int32, (n, 1), 0)
        a_ext = jnp.where(row >= N_PAD, a_ext, 0.0)
    a1 = pltpu.roll(a_ext, 1, axis=0)
    a2 = pltpu.roll(a_ext, 2, axis=0)
    acc = cb_ref[...] + a2 * cw_ref[0:1, :] + a1 * cw_ref[1:2, :] + a_ext * cw_ref[2:3, :]
    return a_ext, a1, a2, acc


FF_COLS = 256


def _up_conv_fwd(n2, w_up_t, conv_w8, conv_b, name):
    t, d = n2.shape
    f = w_up_t.shape[1]
    rows = TOK_TILE
    starts = list(range(0, t, rows))

    def body(n_ref, wa_ref, wb_ref, cw_ref, cb_ref, up_ref, g_ref):
        def project(r0):
            n_rows = n_ref[r0:r0 + rows, :]
            up_ref[0, r0:r0 + rows, :] = _dot(n_rows, wa_ref[...], NT)
            up_ref[1, r0:r0 + rows, :] = _dot(n_rows, wb_ref[...], NT)

        def activate(r0):
            a_ext = _rows_ext(up_ref.at[0], r0, rows, t, HALO, 0)
            _, _, _, acc = _conv_taps(a_ext, r0 - HALO, cw_ref, cb_ref)
            acc = acc[HALO:, :]
            g_ref[r0:r0 + rows, :] = (acc * _sigmoid(acc) * up_ref[1, r0:r0 + rows, :]).astype(BF16)

        project(starts[0])
        for r0, r_next in zip(starts, starts[1:] + [None]):
            if r_next is not None:
                project(r_next)
            activate(r0)

    return _pcall(
        body, name=name, grid=(f // FF_COLS,),
        in_specs=[pl.BlockSpec((t, d), lambda j: (0, 0), pipeline_mode=pl.Buffered(1)),
                  pl.BlockSpec((None, FF_COLS, d), lambda j: (0, j, 0)), pl.BlockSpec((None, FF_COLS, d), lambda j: (1, j, 0)),
                  pl.BlockSpec((8, FF_COLS), lambda j: (0, j)), pl.BlockSpec((1, FF_COLS), lambda j: (0, j))],
        out_specs=[pl.BlockSpec((2, t, FF_COLS), lambda j: (0, 0, j)), pl.BlockSpec((t, FF_COLS), lambda j: (0, j))],
        out_shape=[jax.ShapeDtypeStruct((2, t, f), F32), jax.ShapeDtypeStruct((t, f), BF16)],
        compiler_params=_params("parallel"),
    )(n2, w_up_t, w_up_t, conv_w8, conv_b)


def _dg_conv_bwd(up, conv_w8, conv_b, dh2, w_down, name):
    _, t, f = up.shape
    d = dh2.shape[1]
    rows = TOK_TILE
    starts = list(range(0, t, rows))

    def body(a_ref, b_ref, cw_ref, cb_ref, dh_ref, wd_ref, dup_ref, gcw_ref, gcb_ref, dg_ref):
        def project(r0):
            dg_ref[r0:r0 + rows, :] = _dot(dh_ref[r0:r0 + rows, :], wd_ref[...], NT)

        gw = [jnp.zeros((1, FF_COLS), F32) for _ in range(3)]
        gb = jnp.zeros((1, FF_COLS), F32)
        project(starts[0])
        for r0, r_next in zip(starts, starts[1:] + [None]):
            if r_next is not None:
                project(r_next)
            a_ext = _rows_ext(a_ref, r0, rows, t, HALO, HALO)
            b_ext = _rows_ext(b_ref, r0, rows, t, HALO, HALO)
            dg_ext = _rows_ext(dg_ref, r0, rows, t, HALO, HALO)
            a0, a1, a2, acc = _conv_taps(a_ext, r0 - HALO, cw_ref, cb_ref)
            sg = _sigmoid(acc)
            dacc = dg_ext * b_ext * (sg * (1.0 + acc * (1.0 - sg)))
            n = dacc.shape[0]
            da = (dacc * cw_ref[2:3, :] + pltpu.roll(dacc, n - 1, axis=0) * cw_ref[1:2, :]
                  + pltpu.roll(dacc, n - 2, axis=0) * cw_ref[0:1, :])
            core = slice(HALO, HALO + rows)
            da = da[core, :]
            if r0 < N_PAD:
                row = r0 + lax.broadcasted_iota(jnp.int32, (rows, 1), 0)
                da = jnp.where(row >= N_PAD, da, 0.0)
            dup_ref[0, r0:r0 + rows, :] = da.astype(BF16)
            dup_ref[1, r0:r0 + rows, :] = (dg_ext * acc * sg)[core, :].astype(BF16)
            dacc_c = dacc[core, :]
            gw[0] = gw[0] + jnp.sum(dacc_c * a2[core, :], axis=0, keepdims=True)
            gw[1] = gw[1] + jnp.sum(dacc_c * a1[core, :], axis=0, keepdims=True)
            gw[2] = gw[2] + jnp.sum(dacc_c * a0[core, :], axis=0, keepdims=True)
            gb = gb + jnp.sum(dacc_c, axis=0, keepdims=True)
        gcw_ref[...] = jnp.zeros((8, FF_COLS), F32)
        for tap in range(3):
            gcw_ref[tap:tap + 1, :] = gw[tap]
        gcb_ref[...] = gb

    return _pcall(
        body, name=name, grid=(f // FF_COLS,),
        in_specs=[pl.BlockSpec((None, t, FF_COLS), lambda j: (0, 0, j)), pl.BlockSpec((None, t, FF_COLS), lambda j: (1, 0, j)),
                  pl.BlockSpec((8, FF_COLS), lambda j: (0, j)), pl.BlockSpec((1, FF_COLS), lambda j: (0, j)),
                  pl.BlockSpec((t, d), lambda j: (0, 0), pipeline_mode=pl.Buffered(1)),
                  pl.BlockSpec((FF_COLS, d), lambda j: (j, 0))],
        out_specs=[pl.BlockSpec((2, t, FF_COLS), lambda j: (0, 0, j)), pl.BlockSpec((8, FF_COLS), lambda j: (0, j)),
                   pl.BlockSpec((1, FF_COLS), lambda j: (0, j))],
        out_shape=[jax.ShapeDtypeStruct((2, t, f), BF16), jax.ShapeDtypeStruct((8, f), F32),
                   jax.ShapeDtypeStruct((1, f), F32)],
        scratch_shapes=[pltpu.VMEM((t, FF_COLS), F32)],
        compiler_params=_params("parallel"),
    )(up, up, conv_w8, conv_b, dh2, w_down)


def _exchange(arrays, kinds, name, after=None):
    n = len(arrays)
    npeer = N_DEV - 1
    n_in = n + int(after is not None)

    def body(*refs):
        ins, outs = refs[:n], refs[n_in:n_in + n]
        send_sems, recv_sems, local_sems = refs[n_in + n:]
        x, y, c = lax.axis_index("x"), lax.axis_index("y"), lax.axis_index("c")
        me = 4 * x + 2 * y + c
        copies, locals_ = [], []
        for a in range(n):
            gather = kinds[a] == "gather"
            own = pltpu.make_async_copy(ins[a] if gather else ins[a].at[me], outs[a].at[me], local_sems.at[a])
            own.start()
            locals_.append(own)
            for d in range(1, N_DEV):
                px = 1 - x if d & 4 else x
                py = 1 - y if d & 2 else y
                pc = 1 - c if d & 1 else c
                src = ins[a] if gather else ins[a].at[4 * px + 2 * py + pc]
                cp = pltpu.make_async_remote_copy(
                    src_ref=src, dst_ref=outs[a].at[me],
                    send_sem=send_sems.at[a * npeer + d - 1], recv_sem=recv_sems.at[a * npeer + d - 1],
                    device_id=(px, py, pc), device_id_type=pl.DeviceIdType.MESH)
                cp.start()
                copies.append(cp)
        for cp in copies:
            cp.wait_recv()
        for cp in copies:
            cp.wait_send()
        for own in locals_:
            own.wait()

    out_shape = [jax.ShapeDtypeStruct((N_DEV,) + (a.shape if k == "gather" else a.shape[1:]), a.dtype)
                 for a, k in zip(arrays, kinds)]
    return _pcall(
        body, name=name,
        in_specs=[pl.BlockSpec(memory_space=pl.ANY)] * n_in,
        out_specs=[pl.BlockSpec(memory_space=pl.ANY)] * n,
        out_shape=out_shape,
        scratch_shapes=[pltpu.SemaphoreType.DMA((n * npeer,)), pltpu.SemaphoreType.DMA((n * npeer,)),
                        pltpu.SemaphoreType.DMA((n,))],
        compiler_params=pltpu.CompilerParams(has_side_effects=True),
    )(*arrays, *([] if after is None else [after]))


ALL_PEERS = tuple(range(1, N_DEV))
SAME_CORE_AND_SIBLING = (1, 2, 4, 6)
OTHER_CHIPS = (2, 4, 6)


def _peer_copies(srcs, lands, kinds, send_sems, recv_sems, relations=ALL_PEERS):
    x, y, c = lax.axis_index("x"), lax.axis_index("y"), lax.axis_index("c")
    me = 4 * x + 2 * y + c
    copies = []
    for a in range(len(srcs)):
        for d in relations:
            px = 1 - x if d & 4 else x
            py = 1 - y if d & 2 else y
            pc = 1 - c if d & 1 else c
            peer = 4 * px + 2 * py + pc
            k = a * (N_DEV - 1) + d - 1
            if kinds[a] == "forward":
                src, dst, target = lands[a].at[peer], lands[a].at[peer], (x, y, 1 - c)
            else:
                src, dst, target = (srcs[a] if kinds[a] == "gather" else srcs[a].at[peer]), lands[a].at[me], (px, py, pc)
            copies.append(pltpu.make_async_remote_copy(
                src_ref=src, dst_ref=dst, send_sem=send_sems.at[k], recv_sem=recv_sems.at[k],
                device_id=target, device_id_type=pl.DeviceIdType.MESH))
    return copies


def _exchange_start(arrays, kinds, name, after=None, relations=ALL_PEERS, lands=None):
    n = len(arrays)
    nsem = n * (N_DEV - 1)
    hbm = pl.BlockSpec(memory_space=pltpu.HBM)
    sem = pl.BlockSpec(memory_space=pltpu.SEMAPHORE)
    land_shapes = ([l.shape for l in lands] if lands is not None else
                   [(N_DEV,) + (a.shape if k == "gather" else a.shape[1:]) for a, k in zip(arrays, kinds)])

    n_in = 2 * n + int(after is not None)

    def body(*refs):
        srcs, land_refs = refs[:n], refs[n:2 * n]
        send_sems, recv_sems = refs[n_in], refs[n_in + 1]
        token = refs[-1]
        for cp in _peer_copies(srcs, land_refs, kinds, send_sems, recv_sems, relations):
            cp.start()
        token[...] = jnp.zeros_like(token)

    operands = [pltpu.with_memory_space_constraint(a, pltpu.HBM) for a in arrays]
    operands += (list(lands) if lands is not None else
                 [pltpu.with_memory_space_constraint(lax.empty(s, a.dtype), pltpu.HBM) for s, a in zip(land_shapes, arrays)])
    operands += [] if after is None else [after]
    out = _pcall(
        body, name=name,
        in_specs=[hbm] * (2 * n) + ([] if after is None else [pl.BlockSpec(memory_space=pl.ANY)]),
        out_specs=[sem, sem] + [hbm] * (2 * n) + [pl.BlockSpec(memory_space=pltpu.VMEM)],
        out_shape=[pltpu.SemaphoreType.DMA((nsem,)), pltpu.SemaphoreType.DMA((nsem,))]
        + [pltpu.HBM(a.shape, a.dtype) for a in arrays]
        + [pltpu.HBM(s, a.dtype) for s, a in zip(land_shapes, arrays)]
        + [jax.ShapeDtypeStruct((8, LANE), F32)],
        input_output_aliases={k: 2 + k for k in range(2 * n)},
        compiler_params=pltpu.CompilerParams(has_side_effects=pltpu.SideEffectType.DATAFLOW_SIDE_EFFECTING),
    )(*operands)
    return out[0], out[1], list(out[2:2 + n]), list(out[2 + n:2 + 2 * n]), out[-1]


def _exchange_wait(started, kinds, after, name, fill_own=True, relations=ALL_PEERS):
    send_sems, recv_sems, srcs, lands, _ = started
    n = len(srcs)
    hbm = pl.BlockSpec(memory_space=pltpu.HBM)
    sem = pl.BlockSpec(memory_space=pltpu.SEMAPHORE)

    def body(*refs):
        src_refs, land_refs = refs[:n], refs[n:2 * n]
        copies = _peer_copies(src_refs, land_refs, kinds, refs[2 * n], refs[2 * n + 1], relations)
        for cp in copies:
            cp.wait_send()
        for cp in copies:
            cp.wait_recv()

    out = _pcall(
        body, name=name,
        in_specs=[hbm] * (2 * n) + [sem, sem, pl.BlockSpec(memory_space=pl.ANY)],
        out_specs=[hbm] * (2 * n),
        out_shape=[pltpu.HBM(a.shape, a.dtype) for a in srcs + lands],
        input_output_aliases={k: k for k in range(2 * n)},
        compiler_params=pltpu.CompilerParams(has_side_effects=pltpu.SideEffectType.DATAFLOW_SIDE_EFFECTING),
    )(*srcs, *lands, send_sems, recv_sems, after)
    if not fill_own:
        return list(out[:n]), list(out[n:])
    me = 4 * lax.axis_index("x") + 2 * lax.axis_index("y") + lax.axis_index("c")
    filled = []
    for src, land, kind in zip(out[:n], out[n:], kinds):
        own = lax.dynamic_index_in_dim(src, me, axis=0, keepdims=False) if kind == "scatter" else src
        filled.append(lax.dynamic_update_slice(land, own[None], (me,) + (0,) * own.ndim))
    return filled


def _sum_slots(slots, name, rows_tile):
    nd, r, c = slots.shape

    def body(s_ref, o_ref):
        acc = s_ref[0].astype(F32)
        for p in range(1, nd):
            acc = acc + s_ref[p].astype(F32)
        o_ref[...] = acc

    return _pcall(
        body, name=name, grid=(r // rows_tile,),
        in_specs=[pl.BlockSpec((nd, rows_tile, c), lambda i: (0, i, 0))],
        out_specs=pl.BlockSpec((rows_tile, c), lambda i: (i, 0)),
        out_shape=jax.ShapeDtypeStruct((r, c), F32),
        compiler_params=_params("parallel"),
    )(slots)


def _sum_slots_small(slot_arrays, own_arrays, name):
    n = len(slot_arrays)

    def body(*refs):
        me = 4 * lax.axis_index("x") + 2 * lax.axis_index("y") + lax.axis_index("c")
        for s_ref, own_ref, o_ref in zip(refs[:n], refs[n:2 * n], refs[2 * n:]):
            acc = jnp.where(me == 0, own_ref[...], s_ref[0])
            for p in range(1, s_ref.shape[0]):
                acc = acc + jnp.where(me == p, own_ref[...], s_ref[p])
            o_ref[...] = acc

    return _pcall(body, name=name, out_shape=[jax.ShapeDtypeStruct(a.shape[1:], F32) for a in slot_arrays])(
        *slot_arrays, *own_arrays)


def _adamw_update(w_ref, g_ref, m_ref, v_ref, d_ref, nm_ref, nv_ref):
    gr = g_ref[...]
    nm = ADAM_B1 * m_ref[...] + (1.0 - ADAM_B1) * gr
    nv = ADAM_B2 * v_ref[...] + (1.0 - ADAM_B2) * (gr * gr)
    m_hat = nm / (1.0 - ADAM_B1 ** ADAM_STEP)
    v_hat = nv / (1.0 - ADAM_B2 ** ADAM_STEP)
    d_ref[...] = -ADAM_LR * (m_hat / (jnp.sqrt(v_hat) + ADAM_EPS) + ADAM_WD * w_ref[...])
    nm_ref[...] = nm
    nv_ref[...] = nv


def _adamw_small(ws, gs, ms, vs, name):
    n = len(ws)

    def body(*refs):
        ins, outs = refs[:4 * n], refs[4 * n:]
        for k in range(n):
            _adamw_update(ins[k], ins[n + k], ins[2 * n + k], ins[3 * n + k], outs[k], outs[n + k], outs[2 * n + k])

    shapes = [jax.ShapeDtypeStruct(w.shape, F32) for w in ws]
    out = _pcall(body, name=name, out_shape=shapes * 3)(*ws, *gs, *ms, *vs)
    return list(out[:n]), list(out[n:2 * n]), list(out[2 * n:])


def _adamw(w, g, m, v, name, rows_tile):
    r, c = w.shape
    body = lambda *refs: _adamw_update(*refs)
    spec = pl.BlockSpec((rows_tile, c), lambda i: (i, 0))
    shp = jax.ShapeDtypeStruct((r, c), F32)
    return _pcall(
        body, name=name, grid=(r // rows_tile,), in_specs=[spec] * 4, out_specs=[spec] * 3, out_shape=[shp] * 3,
        compiler_params=_params("parallel"),
    )(w, g, m, v)


F0 = 2 * RET_QK + 2 * RET_V


def _to_internal_rows(w_t):
    cols = w_t.shape[1]
    fox = w_t[F0:F0 + 3 * FOX_W].reshape(3, FOX_PAIRS, LANE, cols).transpose(1, 0, 2, 3).reshape(3 * FOX_W, cols)
    tail = jnp.zeros((IN_PAD - IN_WIDTH, cols), w_t.dtype)
    return jnp.concatenate([w_t[:F0], fox, w_t[F0 + 3 * FOX_W:], tail], axis=0)


def _from_internal_rows(g_t):
    cols = g_t.shape[1]
    fox = g_t[F0:F0 + 3 * FOX_W].reshape(FOX_PAIRS, 3, LANE, cols).transpose(1, 0, 2, 3).reshape(3 * FOX_W, cols)
    return jnp.concatenate([g_t[:F0], fox, g_t[F0 + 3 * FOX_W:F0 + 3 * FOX_W + FOX_HEADS]], axis=0)


IN_BLOCK = IN_WIDTH // N_DEV
IN_BLOCK_PAD = 400
BF16_ROWS = 16


def _slot_row_of_internal():
    rows = np.arange(IN_WIDTH)
    fox = rows[F0:F0 + 3 * FOX_W].reshape(3, FOX_PAIRS, LANE).transpose(1, 0, 2).reshape(-1)
    original = np.concatenate([rows[:F0], fox, rows[F0 + 3 * FOX_W:]])
    slot_rows = original // IN_BLOCK * IN_BLOCK_PAD + original % IN_BLOCK
    return np.concatenate([slot_rows, np.full(IN_PAD - IN_WIDTH, -1)])


def _internal_row_of_slot():
    forward = _slot_row_of_internal()
    back = np.full(N_DEV * IN_BLOCK_PAD, -1)
    back[forward[forward >= 0]] = np.nonzero(forward >= 0)[0]
    return back


def _row_runs(src_of_dst):
    tiles = []
    for t0 in range(0, len(src_of_dst), LANE):
        runs = []
        for o in range(LANE):
            s = int(src_of_dst[t0 + o])
            if s < 0:
                continue
            if runs and runs[-1][0] + runs[-1][2] == o and runs[-1][1] + runs[-1][2] == s:
                runs[-1][2] += 1
            else:
                runs.append([o, s, 1])
        tiles.append(runs)
    return tiles


def _move_rows(src, src_of_dst, name):
    n_src, cols = src.shape
    tiles = _row_runs(src_of_dst)

    def body(s_ref, o_ref):
        for t, runs in enumerate(tiles):
            rows = pl.ds(t * LANE, LANE)
            if not runs:
                o_ref[rows, :] = jnp.zeros((LANE, cols), o_ref.dtype)
                continue
            if len(runs) == 1 and runs[0][0] == 0 and runs[0][2] == LANE and runs[0][1] % BF16_ROWS == 0:
                o_ref[rows, :] = s_ref[pl.ds(runs[0][1], LANE), :]
                continue
            acc = None
            for o0, s0, n in runs:
                w0 = s0 // BF16_ROWS * BF16_ROWS
                width = -(-(s0 - w0 + n) // LANE) * LANE
                w0 = min(w0, n_src - width)
                i = lax.broadcasted_iota(jnp.int32, (LANE, width), 0)
                j = lax.broadcasted_iota(jnp.int32, (LANE, width), 1)
                pick = ((j - i == s0 - w0 - o0) & (i >= o0) & (i < o0 + n)).astype(src.dtype)
                part = _dot(pick, s_ref[pl.ds(w0, width), :])
                acc = part if acc is None else acc + part
            o_ref[rows, :] = acc.astype(o_ref.dtype)

    return _pcall(body, name=name, out_shape=jax.ShapeDtypeStruct((len(src_of_dst), cols), src.dtype))(src)


def _local_step(x, target, meta, attn_g, fox_b, ret_g, ffn_g, conv_w8, conv_b, final_g,
                first_weight, late_weights, ffn_grads_ready, out_grad_ready, in_grad_ready):
    seq, d = x.shape
    t = seq + PREFIX
    tm = TOK_TILE
    nq = t // tm
    fox_b128 = jnp.pad(fox_b, ((0, 0), (0, LANE - FOX_HEADS)))

    h0, n1 = _prep_norm(x, meta, attn_g, "prep_norm")
    w_in_t = first_weight(n1)
    proj = _mm_simple(n1, w_in_t, mode="nt", tm=tm, tn=IN_PAD, tk=d, out_dtype=F32, name="mm_in")
    cos, sin = _rope_tables(t)
    o_pre, mixed, states = _ret_fwd(proj, cos, sin, ret_g, "ret_fwd")
    c = _forget_cumsum(proj, fox_b128, "forget_cumsum")
    qa, ka, va, qt, vt = _fox_prep(proj, c, "fox_prep")
    by_block = lambda a: a.reshape(FOX_HEADS, nq, tm, LANE)
    mixed, o_fox, lse = _fox_fwd(qt, by_block(ka), vt, mixed, "fox_fwd")
    w_out, w_up_t, w_down = late_weights(o_fox)
    tile = pl.BlockSpec((tm, d), lambda i: (i, 0))
    row_vec = pl.BlockSpec((1, d), lambda i: (0, 0))
    resident = lambda shape: pl.BlockSpec(shape, lambda i: (0,) * len(shape), pipeline_mode=pl.Buffered(1))
    acts = lambda dtype: jax.ShapeDtypeStruct((t, d), dtype)
    vec = jax.ShapeDtypeStruct((1, d), F32)

    def residual_and_norm(i, acc, ins, outs):
        h = acc + ins[0][...]
        outs[0][...] = h
        outs[1][...] = (h * lax.rsqrt(jnp.mean(h * h, axis=-1, keepdims=True) + EPS) * ins[1][...]).astype(BF16)

    h1, n2 = _matmul_rows([mixed], [tile], [w_out], [resident((d, d))], [h0, ffn_g], [tile, row_vec],
                          [tile, tile], [acts(F32), acts(BF16)], residual_and_norm, mode="nn", steps=nq, name="mm_out_norm")
    nf = D_FF // 1408
    up, g = _up_conv_fwd(n2, w_up_t, conv_w8, conv_b, "up_conv_fwd")

    def residual_loss_bwd(i, acc, ins, outs):
        loss_ref, dh_ref, dhb_ref, gg_ref = outs
        part, dh, gg = _loss_tile(i, acc + ins[0][...], jnp.concatenate([ins[1][...], ins[2][...], ins[3][...]], axis=0),
                                  ins[4][...])
        _accumulate(loss_ref, i, jnp.broadcast_to(part, loss_ref.shape))
        dh_ref[...] = dh
        dhb_ref[...] = dh.astype(BF16)
        _accumulate(gg_ref, i, gg)

    loss_tile, dh2, dh2_b, g_final = _matmul_rows(
        [g], [pl.BlockSpec((tm, D_FF), lambda i: (i, 0))], [w_down], [resident((D_FF, d))],
        [h1, target, target, target, final_g], [tile] + _shifted_row_specs(d) + [row_vec],
        [pl.BlockSpec((8, LANE), lambda i: (0, 0)), tile, tile, row_vec],
        [jax.ShapeDtypeStruct((8, LANE), F32), acts(F32), acts(BF16), vec], residual_loss_bwd,
        mode="nn", steps=nq, name="mm_down_loss")

    tkw = 2112 if t % 2112 == 0 else tm
    gw_down = _mm_simple(g, dh2_b, mode="tn", tm=1408, tn=d, tk=tkw, out_dtype=BF16, name="mm_gw_down")
    dup, g_conv_w8, g_conv_b = _dg_conv_bwd(up, conv_w8, conv_b, dh2_b, w_down, "dg_conv_bwd")

    half = lambda p: pl.BlockSpec((None, tm, D_FF), lambda i: (p, i, 0))
    half_w = lambda p: pl.BlockSpec((None, D_FF, d), lambda i: (p, 0, 0), pipeline_mode=pl.Buffered(1))
    gw_up_t = _matmul(
        dup, n2, mode="tn", grid=(2 * nf, 1, t // tkw),
        a_spec=pl.BlockSpec((None, tkw, 1408), lambda i, j, k: (i // nf, k, i % nf)),
        b_spec=pl.BlockSpec((tkw, d), lambda i, j, k: (k, 0)),
        o_spec=pl.BlockSpec((1408, d), lambda i, j, k: (i, 0)),
        out_shape=jax.ShapeDtypeStruct((2 * D_FF, d), BF16), name="mm_gw_up")
    def norm_bwd_and_mixer_grad(i, acc, ins, outs):
        dh, gg = _rms_bwd_tile(acc, ins[0][...], ins[1][...], ins[2][...])
        outs[0][...] = dh
        _accumulate(outs[1], i, gg)
        outs[2][...] = _dot(dh.astype(BF16), ins[3][...], NT)

    dh1, g_ffn, dmixed = _matmul_rows(
        [dup, dup], [half(0), half(1)], [w_up_t, w_up_t], [half_w(0), half_w(1)],
        [h1, ffn_g, dh2, w_out], [tile, row_vec, tile, resident((d, d))], [tile, row_vec, tile],
        [acts(F32), vec, acts(F32)], norm_bwd_and_mixer_grad,
        mode="nn", steps=nq, name="mm_dn2_norm_bwd", after=ffn_grads_ready(gw_down, gw_up_t))
    gw_out = _mm_simple(mixed, dh1, mode="tn", tm=d, tn=d, tk=tkw, out_dtype=BF16, name="mm_gw_out")
    dproj, g_ret = _ret_bwd(proj, cos, sin, ret_g + out_grad_ready(gw_out), dmixed, o_pre, states, "ret_bwd")
    qab, doa = _fox_prep_bwd(dmixed, o_fox, lse, qa, "fox_prep_bwd")
    dproj, drs, dcs = _fox_bwd(by_block(qab), by_block(doa), by_block(ka), by_block(va), dproj, "fox_bwd")
    dproj, g_fox_b = _forget_cumsum_bwd(proj, fox_b128, drs, dcs, dproj, "forget_cumsum_bwd")
    gw_in_t = _mm_simple(dproj, n1, mode="tn", tm=640, tn=d, tk=tkw, out_dtype=BF16, name="mm_gw_in")
    sent = in_grad_ready(gw_in_t)
    def input_grads(i, acc, ins, outs):
        gx_ref, gmeta_ref, gg_ref, buf_ref, sems = outs
        dh, gg = _rms_bwd_tile(acc, ins[0][...], ins[1][...], ins[2][...])
        _accumulate(gg_ref, i, gg)
        slot = i % 2

        def first_copy():
            return pltpu.make_async_copy(buf_ref.at[0, pl.ds(PREFIX, tm - PREFIX)], gx_ref.at[pl.ds(0, tm - PREFIX)],
                                         sems.at[0])

        def tile_copy(tile, buf_slot):
            rows = pl.ds(pl.multiple_of(tile * tm - PREFIX, PREFIX), tm)
            return pltpu.make_async_copy(buf_ref.at[buf_slot], gx_ref.at[rows], sems.at[buf_slot])

        @pl.when(i == 1)
        def _():
            first_copy().wait()

        @pl.when(i >= 2)
        def _():
            tile_copy(i - 1, 1 - slot).wait()

        buf_ref[slot] = dh

        @pl.when(i == 0)
        def _():
            gmeta_ref[...] = dh[N_PAD:PREFIX, :]
            first_copy().start()

        @pl.when(i > 0)
        def _():
            tile_copy(i, slot).start()

        @pl.when(i == nq - 1)
        def _():
            tile_copy(i, slot).wait()

    grad_x, g_meta, g_attn = _matmul_rows(
        [dproj], [pl.BlockSpec((tm, IN_PAD), lambda i: (i, 0))], [w_in_t], [resident((IN_PAD, d))],
        [h0, attn_g, dh1], [tile, row_vec, tile],
        [pl.BlockSpec(memory_space=pl.ANY), pl.BlockSpec((N_META, d), lambda i: (0, 0)), row_vec],
        [jax.ShapeDtypeStruct((seq, d), F32), jax.ShapeDtypeStruct((N_META, d), F32), vec], input_grads,
        mode="nn", steps=nq, name="mm_dn1_norm_bwd", after=sent,
        scratch=[pltpu.VMEM((2, tm, d), F32), pltpu.SemaphoreType.DMA((2,))])

    grads = dict(meta=g_meta, attn_g=g_attn, fox_b=g_fox_b, ret_g=g_ret,
                 ffn_g=g_ffn, conv_w=g_conv_w8, conv_b=g_conv_b, final_g=g_final)
    return loss_tile, grad_x, grads


def kernel(x, meta_tokens, attn_norm_g, w_in, fox_forget_b, ret_norm_g, w_out, ffn_norm_g, w_up, conv_w, conv_b, w_down, final_norm_g, loss_target, m_meta_tokens, m_attn_norm_g, m_w_in, m_fox_forget_b, m_ret_norm_g, m_w_out, m_ffn_norm_g, m_w_up, m_conv_w, m_conv_b, m_w_down, m_final_norm_g, v_meta_tokens, v_attn_norm_g, v_w_in, v_fox_forget_b, v_ret_norm_g, v_w_out, v_ffn_norm_g, v_w_up, v_conv_w, v_conv_b, v_w_down, v_final_norm_g):
    d = D_MODEL
    me = 4 * lax.axis_index("x") + 2 * lax.axis_index("y") + lax.axis_index("c")
    in_blk, in_blk_pad = IN_BLOCK, IN_BLOCK_PAD
    up_blk = 2 * D_FF // N_DEV
    down_blk = D_FF // N_DEV
    cw_blk = D_FF // N_DEV

    w_in_loc = jnp.pad(w_in[0].T.astype(BF16), ((0, in_blk_pad - in_blk), (0, 0)))
    cw_loc = jnp.pad(conv_w[0], ((0, 5), (0, 384 - cw_blk)))
    g_meta, g_cw = _exchange([meta_tokens, cw_loc], ["gather"] * 2, "gather_small")
    first = _exchange_start([w_in_loc], ["gather"], "gather_in_start", after=g_meta, relations=SAME_CORE_AND_SIBLING)
    rest_loc = [(w_out[0] + first[-1][0:1, 0:1]).astype(BF16), w_up[0].T.astype(BF16), w_down[0].astype(BF16)]
    rest = _exchange_start(rest_loc, ["gather"] * 3, "gather_rest_start")
    meta_f = g_meta.transpose(1, 0, 2).reshape(N_META, d)
    conv_w8 = jnp.pad(g_cw[:, :3, :cw_blk].transpose(1, 0, 2).reshape(3, D_FF), ((0, 5), (0, 0)))
    pending = {}

    def first_weight(after):
        own_in, landed = _exchange_wait(first, ["gather"], after, "gather_in_wait", fill_own=False,
                                        relations=SAME_CORE_AND_SIBLING)
        onward = _exchange_start(own_in, ["forward"], "gather_in_forward_start", relations=OTHER_CHIPS, lands=landed)
        (g_in,) = _exchange_wait(onward, ["forward"], onward[-1], "gather_in_forward_wait", relations=OTHER_CHIPS)
        return _move_rows(g_in.reshape(IN_PAD, d), _slot_row_of_internal(), "w_in_rows")

    def in_grad_ready(gw_in_t):
        blocks = _move_rows(gw_in_t, _internal_row_of_slot(), "gw_in_rows").reshape(N_DEV, in_blk_pad, d)
        pending["in"] = _exchange_start([blocks], ["scatter"], "grads_in_start")
        return pending["in"][-1][0:1, 0:1]

    def late_weights(after):
        g_out, g_up, g_down = _exchange_wait(rest, ["gather"] * 3, after, "gather_rest_wait")
        return g_out.reshape(d, d), g_up.reshape(2, D_FF, d), g_down.reshape(D_FF, d)

    def ffn_grads_ready(gw_down, gw_up_t):
        blocks = [gw_down.reshape(N_DEV, down_blk, d), gw_up_t.reshape(N_DEV, up_blk, d)]
        pending["ffn"] = _exchange_start(blocks, ["scatter"] * 2, "grads_ffn_start")
        return pending["ffn"][-1][0:1, 0:1]

    def out_grad_ready(gw_out):
        pending["out"] = _exchange_start([gw_out.reshape(N_DEV, d // N_DEV, d)], ["scatter"], "grads_out_start")
        return pending["out"][-1][0:1, 0:1]

    loss_tile, grad_x, gr = _local_step(
        x[0], loss_target[0], meta_f, attn_norm_g + rest[-1][0:1, 0:1], fox_forget_b, ret_norm_g, ffn_norm_g,
        conv_w8, conv_b, final_norm_g.reshape(1, d), first_weight, late_weights, ffn_grads_ready, out_grad_ready,
        in_grad_ready)

    small = [loss_tile, gr["attn_g"], gr["fox_b"], gr["ret_g"], gr["ffn_g"], gr["conv_b"], gr["final_g"],
             gr["meta"], gr["conv_w"]]
    small_kinds = ["gather"] * len(small)
    small_started = _exchange_start(small, small_kinds, "grads_small_start")

    r_down, r_up = _exchange_wait(pending["ffn"], ["scatter"] * 2, small_started[-1], "grads_ffn_wait")
    (r_out,) = _exchange_wait(pending["out"], ["scatter"], small_started[-1], "grads_out_wait")
    g_w_out = _sum_slots(r_out, "sum_w_out", d // N_DEV)
    g_w_up_t = _sum_slots(r_up, "sum_w_up", up_blk)
    g_w_down = _sum_slots(r_down, "sum_w_down", down_blk)
    as_t = lambda a: a[0].T
    from_t = lambda a: a.T[None]
    d_w_out, m_w_out_n, v_w_out_n = [a[None] for a in _adamw(w_out[0], g_w_out, m_w_out[0], v_w_out[0], "adamw_w_out", 128)]
    up_t = _adamw(as_t(w_up), g_w_up_t, as_t(m_w_up), as_t(v_w_up), "adamw_w_up", up_blk // 2)
    d_w_up, m_w_up_n, v_w_up_n = [from_t(a) for a in up_t]
    d_w_down, m_w_down_n, v_w_down_n = [a[None] for a in _adamw(w_down[0], g_w_down, m_w_down[0], v_w_down[0],
                                                                "adamw_w_down", down_blk)]

    own_small, r_small = _exchange_wait(small_started, small_kinds, up_t[0], "grads_small_wait", fill_own=False)
    (loss_all, g_attn, g_fox_b128, g_ret, g_ffn, g_conv_b, g_final, g_meta_full, g_cw_full) = _sum_slots_small(
        r_small, own_small, "sum_small")
    loss = loss_all[0, 0]
    g_fox_b = g_fox_b128[:, :FOX_HEADS]
    g_meta_loc = lax.dynamic_slice(g_meta_full, (0, me * (d // N_DEV)), (N_META, d // N_DEV))
    g_cw_loc = lax.dynamic_slice(g_cw_full, (0, me * cw_blk), (3, cw_blk))

    (r_in,) = _exchange_wait(pending["in"], ["scatter"], r_small[0], "grads_in_wait")
    g_w_in_t = _sum_slots(r_in, "sum_w_in", in_blk_pad)[:in_blk]
    d_w_in, m_w_in_n, v_w_in_n = [from_t(a) for a in _adamw(as_t(w_in), g_w_in_t, as_t(m_w_in), as_t(v_w_in),
                                                            "adamw_w_in", in_blk)]
    g_w_in, g_w_up = g_w_in_t.T, g_w_up_t.T
    row = lambda a: a.reshape(1, d)
    sm_grads = [g_meta_loc, g_attn, g_fox_b, g_ret, g_ffn, g_cw_loc, g_conv_b, g_final]
    sm_w = [meta_tokens, attn_norm_g, fox_forget_b, ret_norm_g, ffn_norm_g, conv_w[0], conv_b, row(final_norm_g)]
    sm_m = [m_meta_tokens, m_attn_norm_g, m_fox_forget_b, m_ret_norm_g, m_ffn_norm_g, m_conv_w[0], m_conv_b,
            row(m_final_norm_g)]
    sm_v = [v_meta_tokens, v_attn_norm_g, v_fox_forget_b, v_ret_norm_g, v_ffn_norm_g, v_conv_w[0], v_conv_b,
            row(v_final_norm_g)]
    dl, ml, vl = [lst[:7] + [lst[7].reshape(d)] for lst in _adamw_small(sm_w, sm_grads, sm_m, sm_v, "adamw_small")]

    def by_weight(meta_, attn_, w_in_, fox_, ret_, w_out_, ffn_, w_up_, cw_, cb_, w_down_, final_):
        return (meta_, attn_, w_in_, fox_, ret_, w_out_, ffn_, w_up_, cw_[None], cb_, w_down_, final_)

    grads_out = by_weight(g_meta_loc, g_attn, g_w_in[None], g_fox_b, g_ret, g_w_out[None], g_ffn, g_w_up[None], g_cw_loc,
                          g_conv_b, g_w_down[None], g_final.reshape(d))
    delta_out = by_weight(dl[0], dl[1], d_w_in, dl[2], dl[3], d_w_out, dl[4], d_w_up, dl[5], dl[6], d_w_down, dl[7])
    m_out = by_weight(ml[0], ml[1], m_w_in_n, ml[2], ml[3], m_w_out_n, ml[4], m_w_up_n, ml[5], ml[6], m_w_down_n, ml[7])
    v_out = by_weight(vl[0], vl[1], v_w_in_n, vl[2], vl[3], v_w_out_n, vl[4], v_w_up_n, vl[5], vl[6], v_w_down_n, vl[7])
    return (loss, grad_x[None]) + grads_out + delta_out + m_out + v_out
```

```python
import numpy as np
import jax
import jax.numpy as jnp
from jax import lax
from jax.experimental import pallas as pl
from jax.experimental.pallas import tpu as pltpu

F32 = jnp.float32
BF16 = jnp.bfloat16

D_MODEL = 1024
N_META = 16
N_PAD = 112
PREFIX = 128
RET_HEADS = 4
RET_DK = 64
RET_DV = 128
FOX_HEADS = 8
FOX_DH = 64
D_FF = 2816
ROPE_BASE = 10000.0
EPS = 1e-6
NEG = -1e30
RET_QK = RET_HEADS * RET_DK
RET_V = RET_HEADS * RET_DV
FOX_W = FOX_HEADS * FOX_DH
IN_WIDTH = 2 * RET_QK + 2 * RET_V + 3 * FOX_W + FOX_HEADS
IN_PAD = 3200
FF_COL_BLOCK = (IN_WIDTH - FOX_HEADS) // 128
QK_SCALE = 0.125

ADAM_LR = 0.001
ADAM_B1 = 0.9
ADAM_B2 = 0.999
ADAM_EPS = 1e-08
ADAM_WD = 0.01
ADAM_STEP = 10

N_DEV = 8
LANE = 128
ROW_TILE = 128
TOK_TILE = 384

NN = (((1,), (0,)), ((), ()))
NT = (((1,), (1,)), ((), ()))
TN = (((0,), (0,)), ((), ()))


def _pcall(body, **kw):
    return pl.pallas_call(body, **kw)


def _params(*sem):
    return pltpu.CompilerParams(dimension_semantics=sem)


def _dot(a, b, dims=NN):
    return lax.dot_general(a, b, dims, preferred_element_type=F32)


def _sigmoid(x):
    return 0.5 * jnp.tanh(0.5 * x) + 0.5


def _matmul(a, b, *, mode, grid, a_spec, b_spec, o_spec, out_shape, name, add=None, add_spec=None, after=None):
    dims = {"nn": NN, "nt": NT, "tn": TN}[mode]
    nk = grid[2]
    has_add = add is not None
    a_list, b_list = (list(a), list(b)) if isinstance(a, (list, tuple)) else ([a], [b])
    a_specs, b_specs = (list(a_spec), list(b_spec)) if isinstance(a_spec, (list, tuple)) else ([a_spec], [b_spec])
    nt = len(a_list)
    n_in = 2 * nt + int(has_add) + int(after is not None)

    def body(*refs):
        a_refs, b_refs = refs[:nt], refs[nt:2 * nt]
        add_ref = refs[2 * nt] if has_add else None
        o_ref = refs[n_in]
        part = _dot(a_refs[0][...].astype(BF16), b_refs[0][...].astype(BF16), dims)
        for ar, br in zip(a_refs[1:], b_refs[1:]):
            part = part + _dot(ar[...].astype(BF16), br[...].astype(BF16), dims)

        def finish(acc):
            if has_add:
                acc = acc + add_ref[...]
            o_ref[...] = acc.astype(o_ref.dtype)

        if nk == 1:
            finish(part)
        else:
            acc_ref = refs[-1]
            k = pl.program_id(2)

            @pl.when(k == 0)
            def _():
                acc_ref[...] = part

            @pl.when(k > 0)
            def _():
                acc_ref[...] += part

            @pl.when(k == nk - 1)
            def _():
                finish(acc_ref[...])

    in_specs = a_specs + b_specs + ([add_spec] if has_add else [])
    args = tuple(a_list) + tuple(b_list) + ((add,) if has_add else ())
    if after is not None:
        in_specs, args = in_specs + [pl.BlockSpec(memory_space=pl.ANY)], args + (after,)
    scratch = [] if nk == 1 else [pltpu.VMEM(tuple(d for d in o_spec.block_shape if d is not None), F32)]
    return _pcall(
        body, name=name, grid=grid, in_specs=in_specs, out_specs=o_spec, out_shape=out_shape,
        scratch_shapes=scratch, compiler_params=_params("parallel", "parallel", "arbitrary"),
    )(*args)


def _mm_simple(a, b, *, mode, tm, tn, tk, out_dtype, name, add=None, after=None):
    if mode == "tn":
        K, M = a.shape
    else:
        M, K = a.shape
    N = b.shape[0] if mode == "nt" else b.shape[1]
    grid = (M // tm, N // tn, K // tk)
    resident = dict(pipeline_mode=pl.Buffered(1)) if (tn == N and tk == K) else {}
    a_spec = pl.BlockSpec((tk, tm), lambda i, j, k: (k, i)) if mode == "tn" else pl.BlockSpec((tm, tk), lambda i, j, k: (i, k))
    b_spec = (pl.BlockSpec((tn, tk), lambda i, j, k: (j, k), **resident) if mode == "nt"
              else pl.BlockSpec((tk, tn), lambda i, j, k: (k, j), **resident))
    o_spec = pl.BlockSpec((tm, tn), lambda i, j, k: (i, j))
    return _matmul(a, b, mode=mode, grid=grid, a_spec=a_spec, b_spec=b_spec, o_spec=o_spec,
                   out_shape=jax.ShapeDtypeStruct((M, N), out_dtype), name=name, add=add,
                   add_spec=o_spec if add is not None else None, after=after)


def _matmul_rows(a_list, a_specs, b_list, b_specs, extras, extra_specs, out_specs, out_shape, epilogue, *,
                 mode, steps, name, after=None, scratch=()):
    dims = {"nn": NN, "nt": NT}[mode]
    nt, ne = len(a_list), len(extras)
    n_in = 2 * nt + ne + int(after is not None)

    def body(*refs):
        acc = _dot(refs[0][...].astype(BF16), refs[nt][...].astype(BF16), dims)
        for k in range(1, nt):
            acc = acc + _dot(refs[k][...].astype(BF16), refs[nt + k][...].astype(BF16), dims)
        epilogue(pl.program_id(0), acc, refs[2 * nt:2 * nt + ne], refs[n_in:])

    in_specs = list(a_specs) + list(b_specs) + list(extra_specs)
    args = tuple(a_list) + tuple(b_list) + tuple(extras)
    if after is not None:
        in_specs, args = in_specs + [pl.BlockSpec(memory_space=pl.ANY)], args + (after,)
    return _pcall(body, name=name, grid=(steps,), in_specs=in_specs, out_specs=out_specs, out_shape=out_shape,
                  scratch_shapes=list(scratch), compiler_params=_params("arbitrary"))(*args)


def _rms_bwd_tile(dy, x, gain, dres):
    r = lax.rsqrt(jnp.mean(x * x, axis=-1, keepdims=True) + EPS)
    xhat = x * r
    u = dy * gain
    return dres + r * (u - xhat * jnp.mean(u * xhat, axis=-1, keepdims=True)), jnp.sum(dy * xhat, axis=0, keepdims=True)


def _loss_tile(i, x, tgt, gain):
    d = x.shape[-1]
    r = lax.rsqrt(jnp.mean(x * x, axis=-1, keepdims=True) + EPS)
    xhat = x * r
    counted = (i * TOK_TILE + lax.broadcasted_iota(jnp.int32, (TOK_TILE, 1), 0)) >= PREFIX
    err = jnp.where(counted, xhat * gain - tgt, 0.0)
    dy = err * (1.0 / d)
    u = dy * gain
    dh = r * (u - xhat * jnp.mean(u * xhat, axis=-1, keepdims=True))
    return 0.5 * jnp.sum(jnp.mean(err * err, axis=-1, keepdims=True)), dh, jnp.sum(dy * xhat, axis=0, keepdims=True)


def _accumulate(ref, i, part):
    @pl.when(i == 0)
    def _():
        ref[...] = part

    @pl.when(i > 0)
    def _():
        ref[...] += part


def _prep_norm(x, meta, gain, name):
    seq, d = x.shape
    t = seq + PREFIX

    def body(xa_ref, xb_ref, xc_ref, meta_ref, g_ref, h_ref, n_ref):
        i = pl.program_id(0)

        @pl.when(i == 0)
        def _():
            h_ref[0:N_PAD, :] = jnp.zeros((N_PAD, d), F32)
            h_ref[N_PAD:ROW_TILE, :] = meta_ref[...]

        @pl.when(i > 0)
        def _():
            h_ref[0:ROW_TILE, :] = xa_ref[...]

        h_ref[ROW_TILE:2 * ROW_TILE, :] = xb_ref[...]
        h_ref[2 * ROW_TILE:3 * ROW_TILE, :] = xc_ref[...]
        h = h_ref[...]
        r = lax.rsqrt(jnp.mean(h * h, axis=-1, keepdims=True) + EPS)
        n_ref[...] = (h * r * g_ref[...]).astype(BF16)

    return _pcall(
        body, name=name, grid=(t // TOK_TILE,),
        in_specs=_shifted_row_specs(d) + [pl.BlockSpec((N_META, d), lambda i: (0, 0)), pl.BlockSpec((1, d), lambda i: (0, 0))],
        out_specs=[pl.BlockSpec((TOK_TILE, d), lambda i: (i, 0)), pl.BlockSpec((TOK_TILE, d), lambda i: (i, 0))],
        out_shape=[jax.ShapeDtypeStruct((t, d), F32), jax.ShapeDtypeStruct((t, d), BF16)],
        compiler_params=_params("parallel"),
    )(x, x, x, meta, gain)


def _shifted_row_specs(d):
    blocks_per_tile = TOK_TILE // ROW_TILE
    return [pl.BlockSpec((ROW_TILE, d), lambda i, r=r: (jnp.maximum(blocks_per_tile * i + r, 0), 0)) for r in (-1, 0, 1)]


def _ret_consts(bk):
    gam = 1.0 - 2.0 ** (-5.0 - np.arange(RET_HEADS))
    n = np.arange(bk)
    same_or_earlier_chunk = (n[None, :] // 64) <= (n[:, None] // 64)
    w = gam[:, None, None] ** np.abs(n[:, None] - n[None, :])[None] * same_or_earlier_chunk[None]
    wq = gam[:, None] ** (n[None, :] + 1.0)
    wk = gam[:, None] ** (bk - 1.0 - n[None, :])
    mask = (np.arange(RET_QK)[None, :] // RET_DK) == np.arange(RET_HEADS)[:, None]
    return (jnp.asarray(w, F32), jnp.asarray(wq[:, :, None], F32), jnp.asarray(wk[:, :, None], F32),
            jnp.asarray(mask[:, None, :], F32), [float(g ** bk) for g in gam])


def _rope_tables(t):
    half = RET_DK // 2
    inv = 1.0 / (ROPE_BASE ** (jnp.arange(half, dtype=F32) / half))
    ang = jnp.arange(t).astype(F32)[:, None] * jnp.tile(inv, 2 * RET_HEADS)[None, :]
    sign = np.tile(np.concatenate([-np.ones(half), np.ones(half)]), RET_HEADS).astype(np.float32)
    return jnp.cos(ang), jnp.sin(ang) * sign[None, :]


def _swap_halves(x):
    outs = []
    for s in range(x.shape[1] // LANE):
        xs = x[:, LANE * s:LANE * (s + 1)]
        lane = lax.broadcasted_iota(jnp.int32, xs.shape, 1)
        outs.append(jnp.where((lane & 32) == 0, pltpu.roll(xs, LANE - 32, axis=1), pltpu.roll(xs, 32, axis=1)))
    return outs[0] if len(outs) == 1 else jnp.concatenate(outs, axis=1)


def _rope(x, cos, sin_signed):
    return x * cos + _swap_halves(x) * sin_signed


def _rope_t(dx, cos, sin_signed):
    return dx * cos + _swap_halves(dx * sin_signed)


def _ret_fwd(proj, cos, sin, gain, name):
    t = proj.shape[0]
    bk = TOK_TILE
    nb = t // bk
    w, wq, wk, mask, g_blk = _ret_consts(bk)

    def body(q_ref, k_ref, v_ref, rg_ref, cos_ref, sin_ref, w_ref, wq_ref, wk_ref, mask_ref, gain_ref,
             opre_ref, og_ref, st_ref, r_ref):
        i = pl.program_id(0)

        @pl.when(i == 0)
        def _():
            r_ref[...] = jnp.zeros_like(r_ref)

        c, s = cos_ref[...], sin_ref[...]
        valid = ((i * bk + lax.broadcasted_iota(jnp.int32, (bk, 1), 0)) >= N_PAD).astype(F32)
        qr = _rope(q_ref[...], c, s)
        kr = _rope(k_ref[...], c, s) * QK_SCALE * valid
        kb = kr.astype(BF16)
        for h in range(RET_HEADS):
            hm = mask_ref[h]
            cols = slice(RET_DV * h, RET_DV * (h + 1))
            vh = v_ref[:, cols].astype(BF16)
            r_prev = r_ref[h]
            st_ref[0, h] = r_prev
            sm = _dot((qr * hm).astype(BF16), kb, NT) * w_ref[h]
            o = _dot(sm.astype(BF16), vh) + _dot((qr * (hm * wq_ref[h])).astype(BF16), r_prev.astype(BF16))
            r_ref[h] = g_blk[h] * r_prev + _dot((kr * wk_ref[h]).astype(BF16), vh, TN)
            opre_ref[:, cols] = o
            rstd = lax.rsqrt(jnp.mean(o * o, axis=-1, keepdims=True) + EPS)
            rg = rg_ref[:, cols]
            og_ref[:, cols] = (o * rstd * gain_ref[:, cols] * (rg * _sigmoid(rg))).astype(BF16)

    full = lambda shape: pl.BlockSpec(shape, lambda i: (0,) * len(shape))
    return _pcall(
        body, name=name, grid=(nb,),
        in_specs=[pl.BlockSpec((bk, RET_QK), lambda i: (i, 0)), pl.BlockSpec((bk, RET_QK), lambda i: (i, 1)),
                  pl.BlockSpec((bk, RET_V), lambda i: (i, 1)), pl.BlockSpec((bk, RET_V), lambda i: (i, 2)),
                  pl.BlockSpec((bk, RET_QK), lambda i: (i, 0)), pl.BlockSpec((bk, RET_QK), lambda i: (i, 0)),
                  full((RET_HEADS, bk, bk)), full((RET_HEADS, bk, 1)), full((RET_HEADS, bk, 1)),
                  full((RET_HEADS, 1, RET_QK)), full((1, RET_V))],
        out_specs=[pl.BlockSpec((bk, RET_V), lambda i: (i, 0)), pl.BlockSpec((bk, RET_V), lambda i: (i, 0)),
                   pl.BlockSpec((1, RET_HEADS, RET_QK, RET_DV), lambda i: (i, 0, 0, 0))],
        out_shape=[jax.ShapeDtypeStruct((t, RET_V), F32), jax.ShapeDtypeStruct((t, RET_V + FOX_W), BF16),
                   jax.ShapeDtypeStruct((nb, RET_HEADS, RET_QK, RET_DV), F32)],
        scratch_shapes=[pltpu.VMEM((RET_HEADS, RET_QK, RET_DV), F32)],
        compiler_params=_params("arbitrary"),
    )(proj, proj, proj, proj, cos, sin, w, wq, wk, mask, gain)


def _ret_bwd(proj, cos, sin, gain, dmixed, opre, states, name):
    t = proj.shape[0]
    bk = TOK_TILE
    nb = t // bk
    w, wq, wk, mask, g_blk = _ret_consts(bk)
    v0, g0 = 2 * RET_QK, 2 * RET_QK + RET_V

    def body(q_ref, k_ref, v_ref, rg_ref, cos_ref, sin_ref, w_ref, wq_ref, wk_ref, mask_ref, gain_ref,
             dog_ref, opre_ref, st_ref, dp_ref, gg_ref, dr_ref):
        step = pl.program_id(0)
        i = nb - 1 - step

        @pl.when(step == 0)
        def _():
            dr_ref[...] = jnp.zeros_like(dr_ref)
            gg_ref[...] = jnp.zeros_like(gg_ref)

        c, s = cos_ref[...], sin_ref[...]
        valid = ((i * bk + lax.broadcasted_iota(jnp.int32, (bk, 1), 0)) >= N_PAD).astype(F32)
        qr = _rope(q_ref[...], c, s)
        kr = _rope(k_ref[...], c, s) * QK_SCALE * valid
        kb = kr.astype(BF16)
        dqr = jnp.zeros((bk, RET_QK), F32)
        dkr = jnp.zeros((bk, RET_QK), F32)
        for h in range(RET_HEADS):
            hm = mask_ref[h]
            cols = slice(RET_DV * h, RET_DV * (h + 1))
            vh = v_ref[:, cols].astype(BF16)
            o = opre_ref[:, cols]
            rstd = lax.rsqrt(jnp.mean(o * o, axis=-1, keepdims=True) + EPS)
            xhat = o * rstd
            rg = rg_ref[:, cols]
            sg = _sigmoid(rg)
            gate = rg * sg
            gn = gain_ref[:, cols]
            dog = dog_ref[:, cols]
            dp_ref[:, g0 + RET_DV * h:g0 + RET_DV * (h + 1)] = (
                dog * xhat * gn * (sg * (1.0 + rg * (1.0 - sg)))).astype(BF16)
            gg_ref[:, cols] += jnp.sum(dog * xhat * gate, axis=0, keepdims=True)
            dxh = dog * gn * gate
            do = (rstd * (dxh - xhat * jnp.mean(dxh * xhat, axis=-1, keepdims=True))).astype(BF16)
            qm = (qr * hm).astype(BF16)
            qw = (qr * (hm * wq_ref[h])).astype(BF16)
            kw = (kr * wk_ref[h]).astype(BF16)
            wh = w_ref[h]
            sm = (_dot(qm, kb, NT) * wh).astype(BF16)
            ds = (_dot(do, vh, NT) * wh).astype(BF16)
            dr = dr_ref[h]
            drb = dr.astype(BF16)
            dp_ref[:, v0 + RET_DV * h:v0 + RET_DV * (h + 1)] = (_dot(sm, do, TN) + _dot(kw, drb)).astype(BF16)
            dqr = dqr + _dot(ds, kb) * hm + _dot(do, st_ref[0, h].astype(BF16), NT) * (hm * wq_ref[h])
            dkr = dkr + _dot(ds, qm, TN) + _dot(vh, drb, NT) * wk_ref[h]
            dr_ref[h] = g_blk[h] * dr + _dot(qw, do, TN)
        dp_ref[:, 0:RET_QK] = _rope_t(dqr, c, s).astype(BF16)
        dp_ref[:, RET_QK:2 * RET_QK] = _rope_t(dkr * (QK_SCALE * valid), c, s).astype(BF16)

    full = lambda shape: pl.BlockSpec(shape, lambda i: (0,) * len(shape))
    rev = lambda col: (lambda i: (nb - 1 - i, col))
    return _pcall(
        body, name=name, grid=(nb,),
        in_specs=[pl.BlockSpec((bk, RET_QK), rev(0)), pl.BlockSpec((bk, RET_QK), rev(1)),
                  pl.BlockSpec((bk, RET_V), rev(1)), pl.BlockSpec((bk, RET_V), rev(2)),
                  pl.BlockSpec((bk, RET_QK), rev(0)), pl.BlockSpec((bk, RET_QK), rev(0)),
                  full((RET_HEADS, bk, bk)), full((RET_HEADS, bk, 1)), full((RET_HEADS, bk, 1)),
                  full((RET_HEADS, 1, RET_QK)), full((1, RET_V)),
                  pl.BlockSpec((bk, RET_V), rev(0)), pl.BlockSpec((bk, RET_V), rev(0)),
                  pl.BlockSpec((1, RET_HEADS, RET_QK, RET_DV), lambda i: (nb - 1 - i, 0, 0, 0))],
        out_specs=[pl.BlockSpec((bk, g0 + RET_V), rev(0)), pl.BlockSpec((1, RET_V), lambda i: (0, 0))],
        out_shape=[jax.ShapeDtypeStruct((t, IN_PAD), BF16), jax.ShapeDtypeStruct((1, RET_V), F32)],
        scratch_shapes=[pltpu.VMEM((RET_HEADS, RET_QK, RET_DV), F32)],
        compiler_params=_params("arbitrary"),
    )(proj, proj, proj, proj, cos, sin, w, wq, wk, mask, gain, dmixed, opre, states)


def _forget_cumsum(proj, bias, name):
    t = proj.shape[0]
    rt = TOK_TILE
    nb = t // rt
    tril = jnp.asarray(np.tril(np.ones((rt, rt))), F32)

    def body(z_ref, b_ref, tril_ref, c_ref, carry_ref):
        i = pl.program_id(0)

        @pl.when(i == 0)
        def _():
            carry_ref[...] = jnp.zeros_like(carry_ref)

        z = z_ref[...] + b_ref[...]
        logf = jnp.minimum(z, 0.0) - jnp.log(1.0 + jnp.exp(-jnp.abs(z)))
        c = lax.dot_general(tril_ref[...], logf, NN, precision=lax.Precision.HIGHEST,
                            preferred_element_type=F32) + carry_ref[...]
        c_ref[...] = c
        carry_ref[...] = c[rt - 1:rt, :]

    return _pcall(
        body, name=name, grid=(nb,),
        in_specs=[pl.BlockSpec((rt, LANE), lambda i: (i, FF_COL_BLOCK)), pl.BlockSpec((1, LANE), lambda i: (0, 0)),
                  pl.BlockSpec((rt, rt), lambda i: (0, 0))],
        out_specs=pl.BlockSpec((rt, LANE), lambda i: (i, 0)),
        out_shape=jax.ShapeDtypeStruct((t, LANE), F32),
        scratch_shapes=[pltpu.VMEM((1, LANE), F32)],
        compiler_params=_params("arbitrary"),
    )(proj, bias, tril)


def _forget_cumsum_bwd(proj, bias, drs, dcs, dproj, name):
    t = proj.shape[0]
    rt = TOK_TILE
    nb = t // rt
    triu = jnp.asarray(np.triu(np.ones((rt, rt))), F32)

    def body(z_ref, b_ref, triu_ref, drs_ref, dcs_ref, dproj_in, dz_ref, gb_ref, carry_ref):
        step = pl.program_id(0)

        @pl.when(step == 0)
        def _():
            carry_ref[...] = jnp.zeros_like(carry_ref)
            gb_ref[...] = jnp.zeros_like(gb_ref)

        dlogf = lax.dot_general(triu_ref[...], drs_ref[...] - dcs_ref[...], NN, precision=lax.Precision.HIGHEST,
                                preferred_element_type=F32) + carry_ref[...]
        carry_ref[...] = dlogf[0:1, :]
        z = z_ref[...] + b_ref[...]
        is_head = lax.broadcasted_iota(jnp.int32, (rt, LANE), 1) < FOX_HEADS
        dz = jnp.where(is_head, dlogf / (1.0 + jnp.exp(z)), 0.0)
        dz_ref[...] = dz.astype(BF16)
        gb_ref[...] += jnp.sum(dz, axis=0, keepdims=True)

    return _pcall(
        body, name=name, grid=(nb,),
        in_specs=[pl.BlockSpec((rt, LANE), lambda i: (nb - 1 - i, FF_COL_BLOCK)),
                  pl.BlockSpec((1, LANE), lambda i: (0, 0)),
                  pl.BlockSpec((rt, rt), lambda i: (0, 0)),
                  pl.BlockSpec((rt, LANE), lambda i: (nb - 1 - i, 0)),
                  pl.BlockSpec((rt, LANE), lambda i: (nb - 1 - i, 0)),
                  pl.BlockSpec(memory_space=pl.ANY)],
        out_specs=[pl.BlockSpec((rt, LANE), lambda i: (nb - 1 - i, FF_COL_BLOCK)),
                   pl.BlockSpec((1, LANE), lambda i: (0, 0))],
        out_shape=[jax.ShapeDtypeStruct(dproj.shape, BF16), jax.ShapeDtypeStruct((1, LANE), F32)],
        input_output_aliases={5: 0},
        scratch_shapes=[pltpu.VMEM((1, LANE), F32)],
        compiler_params=_params("arbitrary"),
    )(proj, bias, triu, drs, dcs, dproj)


FOX_PAIRS = FOX_HEADS // 2
L_ONE_Q = FOX_DH
L_ONE_K = FOX_DH + 3
L_LSE = FOX_DH + 4


def _split3(x):
    hi = x.astype(BF16).astype(F32)
    r = x - hi
    mid = r.astype(BF16).astype(F32)
    return hi, mid, r - mid


def _head_to_low(slab, e):
    return slab if e == 0 else pltpu.roll(slab, FOX_DH, axis=1)


def _pair(a, b, low):
    return jnp.where(low, a, pltpu.roll(b, FOX_DH, axis=1))


def _fox_prep(proj, c, name):
    t = proj.shape[0]
    tq = TOK_TILE

    def body(p_ref, c_ref, qa_ref, ka_ref, va_ref, qt_ref, vt_ref):
        i = pl.program_id(0)
        lane = lax.broadcasted_iota(jnp.int32, (tq, LANE), 1)
        low = lane < FOX_DH
        live = (i * tq + lax.broadcasted_iota(jnp.int32, (tq, 1), 0)) >= N_PAD
        q_tail = jnp.where(lane < L_ONE_Q + 3, 1.0, 0.0)
        k_ones = (lane >= L_ONE_K) & (lane < L_ONE_K + 4)
        v_tail = jnp.where(lane < FOX_DH + 2, 1.0, 0.0)
        bias_parts = _split3(jnp.where(live, -c_ref[...], NEG))
        for pair in range(FOX_PAIRS):
            base = 3 * LANE * pair
            for e in range(2):
                h = 2 * pair + e
                q = _head_to_low(p_ref[:, base:base + LANE], e)
                k = _head_to_low(p_ref[:, base + LANE:base + 2 * LANE], e)
                v = _head_to_low(p_ref[:, base + 2 * LANE:base + 3 * LANE], e)
                hi, mid, lo = [part[:, h:h + 1] for part in bias_parts]
                ka = jnp.where(low, k, jnp.where(k_ones, 1.0, 0.0))
                ka = jnp.where(lane == L_ONE_Q, hi, jnp.where(lane == L_ONE_Q + 1, mid, jnp.where(lane == L_ONE_Q + 2, lo, ka)))
                qa = jnp.where(low, q * QK_SCALE, q_tail)
                va = jnp.where(low, v, v_tail)
                qa_ref[h] = qa.astype(BF16)
                ka_ref[h] = ka.astype(BF16)
                va_ref[h] = va.astype(BF16)
                qt_ref[h] = qa.T.astype(BF16)
                vt_ref[h] = va.T.astype(BF16)

    out = jax.ShapeDtypeStruct((FOX_HEADS, t, LANE), BF16)
    out_t = jax.ShapeDtypeStruct((FOX_HEADS, t // tq, LANE, tq), BF16)
    ospec = pl.BlockSpec((FOX_HEADS, tq, LANE), lambda i: (0, i, 0))
    tspec = pl.BlockSpec((FOX_HEADS, None, LANE, tq), lambda i: (0, i, 0, 0))
    return _pcall(
        body, name=name, grid=(t // tq,),
        in_specs=[pl.BlockSpec((tq, 3 * FOX_W), lambda i: (i, 1)), pl.BlockSpec((tq, LANE), lambda i: (i, 0))],
        out_specs=[ospec, ospec, ospec, tspec, tspec], out_shape=[out, out, out, out_t, out_t],
        compiler_params=_params("parallel"),
    )(proj, c)


STEP_PAIRS = 2
STEP_HEADS = 2 * STEP_PAIRS
FOX_GROUPS = FOX_PAIRS // STEP_PAIRS
FWD_PAIRS = 4
FWD_HEADS = 2 * FWD_PAIRS
FWD_GROUPS = FOX_PAIRS // FWD_PAIRS


def _blockdiag(a, b):
    z = jnp.zeros_like(a)
    return jnp.concatenate([jnp.concatenate([a, z], axis=1), jnp.concatenate([z, b], axis=1)], axis=0)


def _fox_fwd(qt, ka, vt, mixed, name):
    nh, nq, tq, _ = ka.shape
    t = nq * tq

    def body(qt_ref, ka_ref, vt_ref, mixed_in, mixed_ref, o_ref, lse_ref):
        i = pl.program_id(1)
        lane = lax.broadcasted_iota(jnp.int32, (tq, LANE), 1)
        key_le_query = lax.broadcasted_iota(jnp.int32, (tq, tq), 0) <= lax.broadcasted_iota(jnp.int32, (tq, tq), 1)

        def logits(j):
            return [_dot(ka_ref[h, j], qt_ref[h]) for h in range(FWD_HEADS)]

        def update(j, scores, carry, diagonal):
            new = []
            for h in range(FWD_HEADS):
                m, acc = carry[h]
                s = jnp.where(key_le_query, scores[h], NEG) if diagonal else scores[h]
                m_new = jnp.maximum(m, jnp.max(s, axis=0, keepdims=True))
                p = jnp.exp(s - m_new).astype(BF16)
                new.append((m_new, jnp.exp(m - m_new) * acc + _dot(vt_ref[h, j], p)))
            return tuple(new)

        init = tuple((jnp.full((1, tq), NEG, F32), jnp.zeros((LANE, tq), F32)) for _ in range(FWD_HEADS))
        carry = lax.fori_loop(0, i, lambda j, cr: update(j, logits(j), cr, False), init)
        outs, lse_rows = [], []
        for m, acc in update(i, logits(i), carry, True):
            l = acc[FOX_DH:FOX_DH + 1, :]
            outs.append((acc / l).T)
            lse_rows.append(m + jnp.log(l))
        lse_rows.append(jnp.zeros((LANE - FWD_HEADS, tq), F32))
        o_all = jnp.concatenate([_pair(outs[2 * c], outs[2 * c + 1], lane < FOX_DH) for c in range(FWD_PAIRS)], axis=1)
        mixed_ref[...] = o_all.astype(BF16)
        o_ref[...] = o_all
        lse_ref[...] = jnp.concatenate(lse_rows, axis=0).T

    width = FWD_PAIRS * LANE
    whole = pl.BlockSpec((FWD_HEADS, nq, tq, LANE), lambda g, i: (g, 0, 0, 0), pipeline_mode=pl.Buffered(1))
    whole_t = pl.BlockSpec((FWD_HEADS, nq, LANE, tq), lambda g, i: (g, 0, 0, 0), pipeline_mode=pl.Buffered(1))
    return _pcall(
        body, name=name, grid=(FWD_GROUPS, nq),
        in_specs=[pl.BlockSpec((FWD_HEADS, None, LANE, tq), lambda g, i: (g, i, 0, 0)), whole, whole_t,
                  pl.BlockSpec(memory_space=pl.ANY)],
        out_specs=[pl.BlockSpec((tq, width), lambda g, i: (i, RET_V // width + g)),
                   pl.BlockSpec((tq, width), lambda g, i: (i, g)),
                   pl.BlockSpec((None, tq, LANE), lambda g, i: (g, i, 0))],
        out_shape=[jax.ShapeDtypeStruct(mixed.shape, BF16), jax.ShapeDtypeStruct((t, FOX_W), F32),
                   jax.ShapeDtypeStruct((FWD_GROUPS, t, LANE), F32)],
        input_output_aliases={3: 0},
        compiler_params=_params("parallel", "parallel"),
    )(qt, ka, vt, mixed)


def _fox_prep_bwd(dmixed, o_fox, lse, qa, name):
    t = dmixed.shape[0]
    tq = TOK_TILE

    def body(dm_ref, o_ref, lse_ref, qa_ref, qab_ref, doa_ref):
        i = pl.program_id(0)
        lane = lax.broadcasted_iota(jnp.int32, (tq, LANE), 1)
        low = lane < FOX_DH
        live = (i * tq + lax.broadcasted_iota(jnp.int32, (tq, 1), 0)) >= N_PAD
        lse_parts = [_split3(jnp.where(live, -lse_ref[grp], 0.0)) for grp in range(FWD_GROUPS)]
        for pair in range(FOX_PAIRS):
            cols = slice(LANE * pair, LANE * (pair + 1))
            d_slab = dm_ref[:, cols]
            prod = d_slab * o_ref[:, cols]
            for e in range(2):
                h = 2 * pair + e
                nd = -jnp.sum(jnp.where(low, _head_to_low(prod, e), 0.0), axis=-1, keepdims=True)
                nd_hi = nd.astype(BF16).astype(F32)
                doa = jnp.where(low, _head_to_low(d_slab, e), 0.0)
                doa = jnp.where(lane == FOX_DH, nd_hi, jnp.where(lane == FOX_DH + 1, nd - nd_hi, doa))
                doa_ref[h] = doa.astype(BF16)
                lane_h = h % FWD_HEADS
                hi, mid, lo = [part[:, lane_h:lane_h + 1] for part in lse_parts[h // FWD_HEADS]]
                qab = qa_ref[h].astype(F32)
                qab = jnp.where(lane == L_LSE, hi, jnp.where(lane == L_LSE + 1, mid, jnp.where(lane == L_LSE + 2, lo, qab)))
                qab_ref[h] = qab.astype(BF16)

    out = jax.ShapeDtypeStruct((FOX_HEADS, t, LANE), BF16)
    hspec = pl.BlockSpec((FOX_HEADS, tq, LANE), lambda i: (0, i, 0))
    return _pcall(
        body, name=name, grid=(t // tq,),
        in_specs=[pl.BlockSpec((tq, FOX_W), lambda i: (i, 1)), pl.BlockSpec((tq, FOX_W), lambda i: (i, 0)),
                  pl.BlockSpec((FWD_GROUPS, tq, LANE), lambda i: (0, i, 0)), hspec],
        out_specs=[hspec, hspec], out_shape=[out, out],
        compiler_params=_params("parallel"),
    )(dmixed, o_fox, lse, qa)


def _fox_bwd(qab, doa, ka, va, dproj, name):
    nh, nq, tq, _ = qab.shape
    t = nq * tq
    slab = 3 * LANE * STEP_PAIRS
    group0 = (2 * RET_QK + 2 * RET_V) // slab

    def body(qab_ref, doa_ref, ka_ref, va_ref, dproj_in, dp_ref, drs_ref, dcs_ref, dq_ref):
        g, j = pl.program_id(0), pl.program_id(1)

        @pl.when((g == 0) & (j == 0))
        def _():
            drs_ref[...] = jnp.zeros_like(drs_ref)
            dcs_ref[...] = jnp.zeros_like(dcs_ref)

        @pl.when(j == 0)
        def _():
            dq_ref[...] = jnp.zeros_like(dq_ref)

        lane = lax.broadcasted_iota(jnp.int32, (tq, LANE), 1)
        low = lane < FOX_DH
        key_le_query = lax.broadcasted_iota(jnp.int32, (tq, tq), 0) <= lax.broadcasted_iota(jnp.int32, (tq, tq), 1)

        def by_head(c, a, b, col):
            h = STEP_HEADS * g + 2 * c
            return jnp.where(lane == h, a[:, col:col + 1], jnp.where(lane == h + 1, b[:, col:col + 1], 0.0))


        def step(i, carry, diagonal):
            st = [_dot(ka_ref[h], qab_ref[h, i], NT) for h in range(STEP_HEADS)]
            dpt = [_dot(va_ref[h], doa_ref[h, i], NT) for h in range(STEP_HEADS)]
            new = []
            for h in range(STEP_HEADS):
                p = jnp.exp(st[h])
                if diagonal:
                    p = jnp.where(key_le_query, p, 0.0)
                ds = (p * dpt[h]).astype(BF16)
                dq_ref[h, i] += _dot(ds, ka_ref[h], TN)
                dk, dv = carry[h]
                new.append((dk + _dot(ds, qab_ref[h, i]), dv + _dot(p.astype(BF16), doa_ref[h, i])))
            return tuple(new)

        zero = jnp.zeros((tq, LANE), F32)
        carry = step(j, tuple((zero, zero) for _ in range(STEP_HEADS)), True)
        carry = lax.fori_loop(j + 1, nq, lambda i, cr: step(i, cr, False), carry)
        rows = pl.ds(pl.multiple_of(j * tq, tq), tq)
        for c in range(STEP_PAIRS):
            (dka, dva), (dkb, dvb) = carry[2 * c], carry[2 * c + 1]
            c0 = 3 * LANE * c
            dp_ref[rows, c0 + LANE:c0 + 2 * LANE] = _pair(dka, dkb, low).astype(BF16)
            dp_ref[rows, c0 + 2 * LANE:c0 + 3 * LANE] = _pair(dva, dvb, low).astype(BF16)
            dcs_ref[rows, :] += by_head(c, dka, dkb, L_ONE_Q)

        @pl.when(j == nq - 1)
        def _():
            for c in range(STEP_PAIRS):
                for blk in range(nq):
                    r = slice(blk * tq, (blk + 1) * tq)
                    a, b = dq_ref[2 * c, blk], dq_ref[2 * c + 1, blk]
                    dp_ref[r, 3 * LANE * c:3 * LANE * c + LANE] = (_pair(a, b, low) * QK_SCALE).astype(BF16)
                    drs_ref[r, :] += by_head(c, a, b, L_ONE_K)

    whole = pl.BlockSpec((STEP_HEADS, nq, tq, LANE), lambda g, j: (g, 0, 0, 0), pipeline_mode=pl.Buffered(1))
    blk = pl.BlockSpec((STEP_HEADS, None, tq, LANE), lambda g, j: (g, j, 0, 0))
    sums = pl.BlockSpec((t, LANE), lambda g, j: (0, 0), pipeline_mode=pl.Buffered(1))
    return _pcall(
        body, name=name, grid=(FOX_GROUPS, nq),
        in_specs=[whole, whole, blk, blk, pl.BlockSpec(memory_space=pl.ANY)],
        out_specs=[pl.BlockSpec((t, slab), lambda g, j: (0, group0 + g)), sums, sums],
        out_shape=[jax.ShapeDtypeStruct(dproj.shape, BF16), jax.ShapeDtypeStruct((t, LANE), F32),
                   jax.ShapeDtypeStruct((t, LANE), F32)],
        input_output_aliases={4: 0},
        scratch_shapes=[pltpu.VMEM((STEP_HEADS, nq, tq, LANE), F32)],
        compiler_params=_params("arbitrary", "arbitrary"),
    )(qab, doa, ka, va, dproj)


HALO = 8


def _rows_ext(ref, r0, rows, t, before, after):
    lo, hi = r0 - before, r0 + rows + after
    width = ref.shape[-1]
    parts = []
    if lo < 0:
        parts.append(jnp.zeros((-lo, width), F32))
    parts.append(ref[max(lo, 0):min(hi, t), :].astype(F32))
    if hi > t:
        parts.append(jnp.zeros((hi - t, width), F32))
    return parts[0] if len(parts) == 1 else jnp.concatenate(parts, axis=0)


def _conv_taps(a_ext, r0_ext, cw_ref, cb_ref):
    n = a_ext.shape[0]
    if r0_ext < N_PAD:
        row = r0_ext + lax.broadcasted_iota(jnp.int32, (n, 1), 0)
        a_ext = jnp.where(row >= N_PAD, a_ext, 0.0)
    a1 = pltpu.roll(a_ext, 1, axis=0)
    a2 = pltpu.roll(a_ext, 2, axis=0)
    acc = cb_ref[...] + a2 * cw_ref[0:1, :] + a1 * cw_ref[1:2, :] + a_ext * cw_ref[2:3, :]
    return a_ext, a1, a2, acc


FF_COLS = 256


def _up_conv_fwd(n2, w_up_t, conv_w8, conv_b, name):
    t, d = n2.shape
    f = w_up_t.shape[1]
    rows = TOK_TILE
    starts = list(range(0, t, rows))

    def body(n_ref, wa_ref, wb_ref, cw_ref, cb_ref, up_ref, g_ref):
        def project(r0):
            n_rows = n_ref[r0:r0 + rows, :]
            up_ref[0, r0:r0 + rows, :] = _dot(n_rows, wa_ref[...], NT)
            up_ref[1, r0:r0 + rows, :] = _dot(n_rows, wb_ref[...], NT)

        def activate(r0):
            a_ext = _rows_ext(up_ref.at[0], r0, rows, t, HALO, 0)
            _, _, _, acc = _conv_taps(a_ext, r0 - HALO, cw_ref, cb_ref)
            acc = acc[HALO:, :]
            g_ref[r0:r0 + rows, :] = (acc * _sigmoid(acc) * up_ref[1, r0:r0 + rows, :]).astype(BF16)

        project(starts[0])
        for r0, r_next in zip(starts, starts[1:] + [None]):
            if r_next is not None:
                project(r_next)
            activate(r0)

    return _pcall(
        body, name=name, grid=(f // FF_COLS,),
        in_specs=[pl.BlockSpec((t, d), lambda j: (0, 0), pipeline_mode=pl.Buffered(1)),
                  pl.BlockSpec((None, FF_COLS, d), lambda j: (0, j, 0)), pl.BlockSpec((None, FF_COLS, d), lambda j: (1, j, 0)),
                  pl.BlockSpec((8, FF_COLS), lambda j: (0, j)), pl.BlockSpec((1, FF_COLS), lambda j: (0, j))],
        out_specs=[pl.BlockSpec((2, t, FF_COLS), lambda j: (0, 0, j)), pl.BlockSpec((t, FF_COLS), lambda j: (0, j))],
        out_shape=[jax.ShapeDtypeStruct((2, t, f), F32), jax.ShapeDtypeStruct((t, f), BF16)],
        compiler_params=_params("parallel"),
    )(n2, w_up_t, w_up_t, conv_w8, conv_b)


def _dg_conv_bwd(up, conv_w8, conv_b, dh2, w_down, name):
    _, t, f = up.shape
    d = dh2.shape[1]
    rows = TOK_TILE
    starts = list(range(0, t, rows))

    def body(a_ref, b_ref, cw_ref, cb_ref, dh_ref, wd_ref, dup_ref, gcw_ref, gcb_ref, dg_ref):
        def project(r0):
            dg_ref[r0:r0 + rows, :] = _dot(dh_ref[r0:r0 + rows, :], wd_ref[...], NT)

        gw = [jnp.zeros((1, FF_COLS), F32) for _ in range(3)]
        gb = jnp.zeros((1, FF_COLS), F32)
        project(starts[0])
        for r0, r_next in zip(starts, starts[1:] + [None]):
            if r_next is not None:
                project(r_next)
            a_ext = _rows_ext(a_ref, r0, rows, t, HALO, HALO)
            b_ext = _rows_ext(b_ref, r0, rows, t, HALO, HALO)
            dg_ext = _rows_ext(dg_ref, r0, rows, t, HALO, HALO)
            a0, a1, a2, acc = _conv_taps(a_ext, r0 - HALO, cw_ref, cb_ref)
            sg = _sigmoid(acc)
            dacc = dg_ext * b_ext * (sg * (1.0 + acc * (1.0 - sg)))
            n = dacc.shape[0]
            da = (dacc * cw_ref[2:3, :] + pltpu.roll(dacc, n - 1, axis=0) * cw_ref[1:2, :]
                  + pltpu.roll(dacc, n - 2, axis=0) * cw_ref[0:1, :])
            core = slice(HALO, HALO + rows)
            da = da[core, :]
            if r0 < N_PAD:
                row = r0 + lax.broadcasted_iota(jnp.int32, (rows, 1), 0)
                da = jnp.where(row >= N_PAD, da, 0.0)
            dup_ref[0, r0:r0 + rows, :] = da.astype(BF16)
            dup_ref[1, r0:r0 + rows, :] = (dg_ext * acc * sg)[core, :].astype(BF16)
            dacc_c = dacc[core, :]
            gw[0] = gw[0] + jnp.sum(dacc_c * a2[core, :], axis=0, keepdims=True)
            gw[1] = gw[1] + jnp.sum(dacc_c * a1[core, :], axis=0, keepdims=True)
            gw[2] = gw[2] + jnp.sum(dacc_c * a0[core, :], axis=0, keepdims=True)
            gb = gb + jnp.sum(dacc_c, axis=0, keepdims=True)
        gcw_ref[...] = jnp.zeros((8, FF_COLS), F32)
        for tap in range(3):
            gcw_ref[tap:tap + 1, :] = gw[tap]
        gcb_ref[...] = gb

    return _pcall(
        body, name=name, grid=(f // FF_COLS,),
        in_specs=[pl.BlockSpec((None, t, FF_COLS), lambda j: (0, 0, j)), pl.BlockSpec((None, t, FF_COLS), lambda j: (1, 0, j)),
                  pl.BlockSpec((8, FF_COLS), lambda j: (0, j)), pl.BlockSpec((1, FF_COLS), lambda j: (0, j)),
                  pl.BlockSpec((t, d), lambda j: (0, 0), pipeline_mode=pl.Buffered(1)),
                  pl.BlockSpec((FF_COLS, d), lambda j: (j, 0))],
        out_specs=[pl.BlockSpec((2, t, FF_COLS), lambda j: (0, 0, j)), pl.BlockSpec((8, FF_COLS), lambda j: (0, j)),
                   pl.BlockSpec((1, FF_COLS), lambda j: (0, j))],
        out_shape=[jax.ShapeDtypeStruct((2, t, f), BF16), jax.ShapeDtypeStruct((8, f), F32),
                   jax.ShapeDtypeStruct((1, f), F32)],
        scratch_shapes=[pltpu.VMEM((t, FF_COLS), F32)],
        compiler_params=_params("parallel"),
    )(up, up, conv_w8, conv_b, dh2, w_down)


def _exchange(arrays, kinds, name, after=None):
    n = len(arrays)
    npeer = N_DEV - 1
    n_in = n + int(after is not None)

    def body(*refs):
        ins, outs = refs[:n], refs[n_in:n_in + n]
        send_sems, recv_sems, local_sems = refs[n_in + n:]
        x, y, c = lax.axis_index("x"), lax.axis_index("y"), lax.axis_index("c")
        me = 4 * x + 2 * y + c
        copies, locals_ = [], []
        for a in range(n):
            gather = kinds[a] == "gather"
            own = pltpu.make_async_copy(ins[a] if gather else ins[a].at[me], outs[a].at[me], local_sems.at[a])
            own.start()
            locals_.append(own)
            for d in range(1, N_DEV):
                px = 1 - x if d & 4 else x
                py = 1 - y if d & 2 else y
                pc = 1 - c if d & 1 else c
                src = ins[a] if gather else ins[a].at[4 * px + 2 * py + pc]
                cp = pltpu.make_async_remote_copy(
                    src_ref=src, dst_ref=outs[a].at[me],
                    send_sem=send_sems.at[a * npeer + d - 1], recv_sem=recv_sems.at[a * npeer + d - 1],
                    device_id=(px, py, pc), device_id_type=pl.DeviceIdType.MESH)
                cp.start()
                copies.append(cp)
        for cp in copies:
            cp.wait_recv()
        for cp in copies:
            cp.wait_send()
        for own in locals_:
            own.wait()

    out_shape = [jax.ShapeDtypeStruct((N_DEV,) + (a.shape if k == "gather" else a.shape[1:]), a.dtype)
                 for a, k in zip(arrays, kinds)]
    return _pcall(
        body, name=name,
        in_specs=[pl.BlockSpec(memory_space=pl.ANY)] * n_in,
        out_specs=[pl.BlockSpec(memory_space=pl.ANY)] * n,
        out_shape=out_shape,
        scratch_shapes=[pltpu.SemaphoreType.DMA((n * npeer,)), pltpu.SemaphoreType.DMA((n * npeer,)),
                        pltpu.SemaphoreType.DMA((n,))],
        compiler_params=pltpu.CompilerParams(has_side_effects=True),
    )(*arrays, *([] if after is None else [after]))


ALL_PEERS = tuple(range(1, N_DEV))
SAME_CORE_AND_SIBLING = (1, 2, 4, 6)
OTHER_CHIPS = (2, 4, 6)


def _peer_copies(srcs, lands, kinds, send_sems, recv_sems, relations=ALL_PEERS):
    x, y, c = lax.axis_index("x"), lax.axis_index("y"), lax.axis_index("c")
    me = 4 * x + 2 * y + c
    copies = []
    for a in range(len(srcs)):
        for d in relations:
            px = 1 - x if d & 4 else x
            py = 1 - y if d & 2 else y
            pc = 1 - c if d & 1 else c
            peer = 4 * px + 2 * py + pc
            k = a * (N_DEV - 1) + d - 1
            if kinds[a] == "forward":
                src, dst, target = lands[a].at[peer], lands[a].at[peer], (x, y, 1 - c)
            else:
                src, dst, target = (srcs[a] if kinds[a] == "gather" else srcs[a].at[peer]), lands[a].at[me], (px, py, pc)
            copies.append(pltpu.make_async_remote_copy(
                src_ref=src, dst_ref=dst, send_sem=send_sems.at[k], recv_sem=recv_sems.at[k],
                device_id=target, device_id_type=pl.DeviceIdType.MESH))
    return copies


def _own_copies(srcs, lands, kinds, sems):
    me = 4 * lax.axis_index("x") + 2 * lax.axis_index("y") + lax.axis_index("c")
    first = len(srcs) * (N_DEV - 1)
    return [pltpu.make_async_copy(srcs[a].at[me] if kinds[a] == "scatter" else srcs[a], lands[a].at[me], sems.at[first + a])
            for a in range(len(srcs))]


def _exchange_start(arrays, kinds, name, after=None, relations=ALL_PEERS, lands=None, own=True):
    n = len(arrays)
    nsem = n * (N_DEV - 1) + n
    hbm = pl.BlockSpec(memory_space=pltpu.HBM)
    sem = pl.BlockSpec(memory_space=pltpu.SEMAPHORE)
    land_shapes = ([l.shape for l in lands] if lands is not None else
                   [(N_DEV,) + (a.shape if k == "gather" else a.shape[1:]) for a, k in zip(arrays, kinds)])

    n_in = 2 * n + int(after is not None)

    def body(*refs):
        srcs, land_refs = refs[:n], refs[n:2 * n]
        send_sems, recv_sems = refs[n_in], refs[n_in + 1]
        token = refs[-1]
        for cp in _peer_copies(srcs, land_refs, kinds, send_sems, recv_sems, relations):
            cp.start()
        for cp in _own_copies(srcs, land_refs, kinds, send_sems) if own else []:
            cp.start()
        token[...] = jnp.zeros_like(token)

    operands = [pltpu.with_memory_space_constraint(a, pltpu.HBM) for a in arrays]
    operands += (list(lands) if lands is not None else
                 [pltpu.with_memory_space_constraint(lax.empty(s, a.dtype), pltpu.HBM) for s, a in zip(land_shapes, arrays)])
    operands += [] if after is None else [after]
    out = _pcall(
        body, name=name,
        in_specs=[hbm] * (2 * n) + ([] if after is None else [pl.BlockSpec(memory_space=pl.ANY)]),
        out_specs=[sem, sem] + [hbm] * (2 * n) + [pl.BlockSpec(memory_space=pltpu.VMEM)],
        out_shape=[pltpu.SemaphoreType.DMA((nsem,)), pltpu.SemaphoreType.DMA((nsem,))]
        + [pltpu.HBM(a.shape, a.dtype) for a in arrays]
        + [pltpu.HBM(s, a.dtype) for s, a in zip(land_shapes, arrays)]
        + [jax.ShapeDtypeStruct((8, LANE), F32)],
        input_output_aliases={k: 2 + k for k in range(2 * n)},
        compiler_params=pltpu.CompilerParams(has_side_effects=pltpu.SideEffectType.DATAFLOW_SIDE_EFFECTING),
    )(*operands)
    return out[0], out[1], list(out[2:2 + n]), list(out[2 + n:2 + 2 * n]), out[-1]


def _exchange_wait(started, kinds, after, name, own=True, relations=ALL_PEERS, with_sources=False):
    send_sems, recv_sems, srcs, lands, _ = started
    n = len(srcs)
    hbm = pl.BlockSpec(memory_space=pltpu.HBM)
    sem = pl.BlockSpec(memory_space=pltpu.SEMAPHORE)

    def body(*refs):
        src_refs, land_refs = refs[:n], refs[n:2 * n]
        copies = _peer_copies(src_refs, land_refs, kinds, refs[2 * n], refs[2 * n + 1], relations)
        for cp in copies:
            cp.wait_send()
        for cp in copies:
            cp.wait_recv()
        for cp in _own_copies(src_refs, land_refs, kinds, refs[2 * n]) if own else []:
            cp.wait()

    out = _pcall(
        body, name=name,
        in_specs=[hbm] * (2 * n) + [sem, sem, pl.BlockSpec(memory_space=pl.ANY)],
        out_specs=[hbm] * (2 * n),
        out_shape=[pltpu.HBM(a.shape, a.dtype) for a in srcs + lands],
        input_output_aliases={k: k for k in range(2 * n)},
        compiler_params=pltpu.CompilerParams(has_side_effects=pltpu.SideEffectType.DATAFLOW_SIDE_EFFECTING),
    )(*srcs, *lands, send_sems, recv_sems, after)
    return (list(out[:n]), list(out[n:])) if with_sources else list(out[n:])


def _sum_slots(slots, name, rows_tile):
    nd, r, c = slots.shape

    def body(s_ref, o_ref):
        acc = s_ref[0].astype(F32)
        for p in range(1, nd):
            acc = acc + s_ref[p].astype(F32)
        o_ref[...] = acc

    return _pcall(
        body, name=name, grid=(r // rows_tile,),
        in_specs=[pl.BlockSpec((nd, rows_tile, c), lambda i: (0, i, 0))],
        out_specs=pl.BlockSpec((rows_tile, c), lambda i: (i, 0)),
        out_shape=jax.ShapeDtypeStruct((r, c), F32),
        compiler_params=_params("parallel"),
    )(slots)


def _sum_slots_small(slot_arrays, own_arrays, name):
    n = len(slot_arrays)

    def body(*refs):
        me = 4 * lax.axis_index("x") + 2 * lax.axis_index("y") + lax.axis_index("c")
        for s_ref, own_ref, o_ref in zip(refs[:n], refs[n:2 * n], refs[2 * n:]):
            acc = jnp.where(me == 0, own_ref[...], s_ref[0])
            for p in range(1, s_ref.shape[0]):
                acc = acc + jnp.where(me == p, own_ref[...], s_ref[p])
            o_ref[...] = acc

    return _pcall(body, name=name, out_shape=[jax.ShapeDtypeStruct(a.shape[1:], F32) for a in slot_arrays])(
        *slot_arrays, *own_arrays)


def _adamw_update(w_ref, g_ref, m_ref, v_ref, d_ref, nm_ref, nv_ref):
    gr = g_ref[...]
    nm = ADAM_B1 * m_ref[...] + (1.0 - ADAM_B1) * gr
    nv = ADAM_B2 * v_ref[...] + (1.0 - ADAM_B2) * (gr * gr)
    m_hat = nm / (1.0 - ADAM_B1 ** ADAM_STEP)
    v_hat = nv / (1.0 - ADAM_B2 ** ADAM_STEP)
    d_ref[...] = -ADAM_LR * (m_hat / (jnp.sqrt(v_hat) + ADAM_EPS) + ADAM_WD * w_ref[...])
    nm_ref[...] = nm
    nv_ref[...] = nv


def _adamw_small(ws, gs, ms, vs, name):
    n = len(ws)

    def body(*refs):
        ins, outs = refs[:4 * n], refs[4 * n:]
        for k in range(n):
            _adamw_update(ins[k], ins[n + k], ins[2 * n + k], ins[3 * n + k], outs[k], outs[n + k], outs[2 * n + k])

    shapes = [jax.ShapeDtypeStruct(w.shape, F32) for w in ws]
    out = _pcall(body, name=name, out_shape=shapes * 3)(*ws, *gs, *ms, *vs)
    return list(out[:n]), list(out[n:2 * n]), list(out[2 * n:])


def _adamw(w, g, m, v, name, rows_tile):
    r, c = w.shape
    body = lambda *refs: _adamw_update(*refs)
    spec = pl.BlockSpec((rows_tile, c), lambda i: (i, 0))
    shp = jax.ShapeDtypeStruct((r, c), F32)
    return _pcall(
        body, name=name, grid=(r // rows_tile,), in_specs=[spec] * 4, out_specs=[spec] * 3, out_shape=[shp] * 3,
        compiler_params=_params("parallel"),
    )(w, g, m, v)


F0 = 2 * RET_QK + 2 * RET_V


def _to_internal_rows(w_t):
    cols = w_t.shape[1]
    fox = w_t[F0:F0 + 3 * FOX_W].reshape(3, FOX_PAIRS, LANE, cols).transpose(1, 0, 2, 3).reshape(3 * FOX_W, cols)
    tail = jnp.zeros((IN_PAD - IN_WIDTH, cols), w_t.dtype)
    return jnp.concatenate([w_t[:F0], fox, w_t[F0 + 3 * FOX_W:], tail], axis=0)


def _from_internal_rows(g_t):
    cols = g_t.shape[1]
    fox = g_t[F0:F0 + 3 * FOX_W].reshape(FOX_PAIRS, 3, LANE, cols).transpose(1, 0, 2, 3).reshape(3 * FOX_W, cols)
    return jnp.concatenate([g_t[:F0], fox, g_t[F0 + 3 * FOX_W:F0 + 3 * FOX_W + FOX_HEADS]], axis=0)


IN_BLOCK = IN_WIDTH // N_DEV
IN_BLOCK_PAD = 400
BF16_ROWS = 16


def _slot_row_of_internal():
    rows = np.arange(IN_WIDTH)
    fox = rows[F0:F0 + 3 * FOX_W].reshape(3, FOX_PAIRS, LANE).transpose(1, 0, 2).reshape(-1)
    original = np.concatenate([rows[:F0], fox, rows[F0 + 3 * FOX_W:]])
    slot_rows = original // IN_BLOCK * IN_BLOCK_PAD + original % IN_BLOCK
    return np.concatenate([slot_rows, np.full(IN_PAD - IN_WIDTH, -1)])


def _internal_row_of_slot():
    forward = _slot_row_of_internal()
    back = np.full(N_DEV * IN_BLOCK_PAD, -1)
    back[forward[forward >= 0]] = np.nonzero(forward >= 0)[0]
    return back


def _row_runs(src_of_dst):
    tiles = []
    for t0 in range(0, len(src_of_dst), LANE):
        runs = []
        for o in range(LANE):
            s = int(src_of_dst[t0 + o])
            if s < 0:
                continue
            if runs and runs[-1][0] + runs[-1][2] == o and runs[-1][1] + runs[-1][2] == s:
                runs[-1][2] += 1
            else:
                runs.append([o, s, 1])
        tiles.append(runs)
    return tiles


def _move_rows(src, src_of_dst, name):
    n_src, cols = src.shape
    tiles = _row_runs(src_of_dst)

    def body(s_ref, o_ref):
        for t, runs in enumerate(tiles):
            rows = pl.ds(t * LANE, LANE)
            if not runs:
                o_ref[rows, :] = jnp.zeros((LANE, cols), o_ref.dtype)
                continue
            if len(runs) == 1 and runs[0][0] == 0 and runs[0][2] == LANE and runs[0][1] % BF16_ROWS == 0:
                o_ref[rows, :] = s_ref[pl.ds(runs[0][1], LANE), :]
                continue
            acc = None
            for o0, s0, n in runs:
                w0 = s0 // BF16_ROWS * BF16_ROWS
                width = -(-(s0 - w0 + n) // LANE) * LANE
                w0 = min(w0, n_src - width)
                i = lax.broadcasted_iota(jnp.int32, (LANE, width), 0)
                j = lax.broadcasted_iota(jnp.int32, (LANE, width), 1)
                pick = ((j - i == s0 - w0 - o0) & (i >= o0) & (i < o0 + n)).astype(src.dtype)
                part = _dot(pick, s_ref[pl.ds(w0, width), :])
                acc = part if acc is None else acc + part
            o_ref[rows, :] = acc.astype(o_ref.dtype)

    return _pcall(body, name=name, out_shape=jax.ShapeDtypeStruct((len(src_of_dst), cols), src.dtype))(src)


def _local_step(x, target, meta, attn_g, fox_b, ret_g, ffn_g, conv_w8, conv_b, final_g,
                first_weight, late_weights, ffn_grads_ready, out_grad_ready, in_grad_ready):
    seq, d = x.shape
    t = seq + PREFIX
    tm = TOK_TILE
    nq = t // tm
    fox_b128 = jnp.pad(fox_b, ((0, 0), (0, LANE - FOX_HEADS)))

    h0, n1 = _prep_norm(x, meta, attn_g, "prep_norm")
    w_in_t = first_weight(n1)
    proj = _mm_simple(n1, w_in_t, mode="nt", tm=tm, tn=IN_PAD, tk=d, out_dtype=F32, name="mm_in")
    cos, sin = _rope_tables(t)
    o_pre, mixed, states = _ret_fwd(proj, cos, sin, ret_g, "ret_fwd")
    c = _forget_cumsum(proj, fox_b128, "forget_cumsum")
    qa, ka, va, qt, vt = _fox_prep(proj, c, "fox_prep")
    by_block = lambda a: a.reshape(FOX_HEADS, nq, tm, LANE)
    mixed, o_fox, lse = _fox_fwd(qt, by_block(ka), vt, mixed, "fox_fwd")
    w_out, w_up_t, w_down = late_weights(o_fox)
    tile = pl.BlockSpec((tm, d), lambda i: (i, 0))
    row_vec = pl.BlockSpec((1, d), lambda i: (0, 0))
    resident = lambda shape: pl.BlockSpec(shape, lambda i: (0,) * len(shape), pipeline_mode=pl.Buffered(1))
    acts = lambda dtype: jax.ShapeDtypeStruct((t, d), dtype)
    vec = jax.ShapeDtypeStruct((1, d), F32)

    def residual_and_norm(i, acc, ins, outs):
        h = acc + ins[0][...]
        outs[0][...] = h
        outs[1][...] = (h * lax.rsqrt(jnp.mean(h * h, axis=-1, keepdims=True) + EPS) * ins[1][...]).astype(BF16)

    h1, n2 = _matmul_rows([mixed], [tile], [w_out], [resident((d, d))], [h0, ffn_g], [tile, row_vec],
                          [tile, tile], [acts(F32), acts(BF16)], residual_and_norm, mode="nn", steps=nq, name="mm_out_norm")
    nf = D_FF // 1408
    up, g = _up_conv_fwd(n2, w_up_t, conv_w8, conv_b, "up_conv_fwd")

    def residual_loss_bwd(i, acc, ins, outs):
        loss_ref, dh_ref, dhb_ref, gg_ref = outs
        part, dh, gg = _loss_tile(i, acc + ins[0][...], jnp.concatenate([ins[1][...], ins[2][...], ins[3][...]], axis=0),
                                  ins[4][...])
        _accumulate(loss_ref, i, jnp.broadcast_to(part, loss_ref.shape))
        dh_ref[...] = dh
        dhb_ref[...] = dh.astype(BF16)
        _accumulate(gg_ref, i, gg)

    loss_tile, dh2, dh2_b, g_final = _matmul_rows(
        [g], [pl.BlockSpec((tm, D_FF), lambda i: (i, 0))], [w_down], [resident((D_FF, d))],
        [h1, target, target, target, final_g], [tile] + _shifted_row_specs(d) + [row_vec],
        [pl.BlockSpec((8, LANE), lambda i: (0, 0)), tile, tile, row_vec],
        [jax.ShapeDtypeStruct((8, LANE), F32), acts(F32), acts(BF16), vec], residual_loss_bwd,
        mode="nn", steps=nq, name="mm_down_loss")

    tkw = 2112 if t % 2112 == 0 else tm
    gw_down = _mm_simple(g, dh2_b, mode="tn", tm=1408, tn=d, tk=tkw, out_dtype=BF16, name="mm_gw_down")
    dup, g_conv_w8, g_conv_b = _dg_conv_bwd(up, conv_w8, conv_b, dh2_b, w_down, "dg_conv_bwd")

    half = lambda p: pl.BlockSpec((None, tm, D_FF), lambda i: (p, i, 0))
    half_w = lambda p: pl.BlockSpec((None, D_FF, d), lambda i: (p, 0, 0), pipeline_mode=pl.Buffered(1))
    gw_up_t = _matmul(
        dup, n2, mode="tn", grid=(2 * nf, 1, t // tkw),
        a_spec=pl.BlockSpec((None, tkw, 1408), lambda i, j, k: (i // nf, k, i % nf)),
        b_spec=pl.BlockSpec((tkw, d), lambda i, j, k: (k, 0)),
        o_spec=pl.BlockSpec((1408, d), lambda i, j, k: (i, 0)),
        out_shape=jax.ShapeDtypeStruct((2 * D_FF, d), BF16), name="mm_gw_up")
    def norm_bwd_and_mixer_grad(i, acc, ins, outs):
        dh, gg = _rms_bwd_tile(acc, ins[0][...], ins[1][...], ins[2][...])
        outs[0][...] = dh
        _accumulate(outs[1], i, gg)
        outs[2][...] = _dot(dh.astype(BF16), ins[3][...], NT)

    dh1, g_ffn, dmixed = _matmul_rows(
        [dup, dup], [half(0), half(1)], [w_up_t, w_up_t], [half_w(0), half_w(1)],
        [h1, ffn_g, dh2, w_out], [tile, row_vec, tile, resident((d, d))], [tile, row_vec, tile],
        [acts(F32), vec, acts(F32)], norm_bwd_and_mixer_grad,
        mode="nn", steps=nq, name="mm_dn2_norm_bwd", after=ffn_grads_ready(gw_down, gw_up_t))
    gw_out = _mm_simple(mixed, dh1, mode="tn", tm=d, tn=d, tk=tkw, out_dtype=BF16, name="mm_gw_out")
    dproj, g_ret = _ret_bwd(proj, cos, sin, ret_g + out_grad_ready(gw_out), dmixed, o_pre, states, "ret_bwd")
    qab, doa = _fox_prep_bwd(dmixed, o_fox, lse, qa, "fox_prep_bwd")
    dproj, drs, dcs = _fox_bwd(by_block(qab), by_block(doa), by_block(ka), by_block(va), dproj, "fox_bwd")
    dproj, g_fox_b = _forget_cumsum_bwd(proj, fox_b128, drs, dcs, dproj, "forget_cumsum_bwd")
    gw_in_t = _mm_simple(dproj, n1, mode="tn", tm=640, tn=d, tk=tkw, out_dtype=BF16, name="mm_gw_in")
    sent = in_grad_ready(gw_in_t)
    def input_grads(i, acc, ins, outs):
        gx_ref, gmeta_ref, gg_ref, buf_ref, sems = outs
        dh, gg = _rms_bwd_tile(acc, ins[0][...], ins[1][...], ins[2][...])
        _accumulate(gg_ref, i, gg)
        slot = i % 2

        def first_copy():
            return pltpu.make_async_copy(buf_ref.at[0, pl.ds(PREFIX, tm - PREFIX)], gx_ref.at[pl.ds(0, tm - PREFIX)],
                                         sems.at[0])

        def tile_copy(tile, buf_slot):
            rows = pl.ds(pl.multiple_of(tile * tm - PREFIX, PREFIX), tm)
            return pltpu.make_async_copy(buf_ref.at[buf_slot], gx_ref.at[rows], sems.at[buf_slot])

        @pl.when(i == 1)
        def _():
            first_copy().wait()

        @pl.when(i >= 2)
        def _():
            tile_copy(i - 1, 1 - slot).wait()

        buf_ref[slot] = dh

        @pl.when(i == 0)
        def _():
            gmeta_ref[...] = dh[N_PAD:PREFIX, :]
            first_copy().start()

        @pl.when(i > 0)
        def _():
            tile_copy(i, slot).start()

        @pl.when(i == nq - 1)
        def _():
            tile_copy(i, slot).wait()

    grad_x, g_meta, g_attn = _matmul_rows(
        [dproj], [pl.BlockSpec((tm, IN_PAD), lambda i: (i, 0))], [w_in_t], [resident((IN_PAD, d))],
        [h0, attn_g, dh1], [tile, row_vec, tile],
        [pl.BlockSpec(memory_space=pl.ANY), pl.BlockSpec((N_META, d), lambda i: (0, 0)), row_vec],
        [jax.ShapeDtypeStruct((seq, d), F32), jax.ShapeDtypeStruct((N_META, d), F32), vec], input_grads,
        mode="nn", steps=nq, name="mm_dn1_norm_bwd", after=sent,
        scratch=[pltpu.VMEM((2, tm, d), F32), pltpu.SemaphoreType.DMA((2,))])

    grads = dict(meta=g_meta, attn_g=g_attn, fox_b=g_fox_b, ret_g=g_ret,
                 ffn_g=g_ffn, conv_w=g_conv_w8, conv_b=g_conv_b, final_g=g_final)
    return loss_tile, grad_x, grads


def kernel(x, meta_tokens, attn_norm_g, w_in, fox_forget_b, ret_norm_g, w_out, ffn_norm_g, w_up, conv_w, conv_b, w_down, final_norm_g, loss_target, m_meta_tokens, m_attn_norm_g, m_w_in, m_fox_forget_b, m_ret_norm_g, m_w_out, m_ffn_norm_g, m_w_up, m_conv_w, m_conv_b, m_w_down, m_final_norm_g, v_meta_tokens, v_attn_norm_g, v_w_in, v_fox_forget_b, v_ret_norm_g, v_w_out, v_ffn_norm_g, v_w_up, v_conv_w, v_conv_b, v_w_down, v_final_norm_g):
    d = D_MODEL
    me = 4 * lax.axis_index("x") + 2 * lax.axis_index("y") + lax.axis_index("c")
    in_blk, in_blk_pad = IN_BLOCK, IN_BLOCK_PAD
    up_blk = 2 * D_FF // N_DEV
    down_blk = D_FF // N_DEV
    cw_blk = D_FF // N_DEV

    w_in_loc = jnp.pad(w_in[0].T.astype(BF16), ((0, in_blk_pad - in_blk), (0, 0)))
    cw_loc = jnp.pad(conv_w[0], ((0, 5), (0, 384 - cw_blk)))
    g_meta, g_cw = _exchange([meta_tokens, cw_loc], ["gather"] * 2, "gather_small")
    first = _exchange_start([w_in_loc], ["gather"], "gather_in_start", after=g_meta, relations=SAME_CORE_AND_SIBLING)
    rest_loc = [(w_out[0] + first[-1][0:1, 0:1]).astype(BF16), w_up[0].T.astype(BF16), w_down[0].astype(BF16)]
    rest = _exchange_start(rest_loc, ["gather"] * 3, "gather_rest_start")
    meta_f = g_meta.transpose(1, 0, 2).reshape(N_META, d)
    conv_w8 = jnp.pad(g_cw[:, :3, :cw_blk].transpose(1, 0, 2).reshape(3, D_FF), ((0, 5), (0, 0)))
    pending = {}

    def first_weight(after):
        own_in, landed = _exchange_wait(first, ["gather"], after, "gather_in_wait", relations=SAME_CORE_AND_SIBLING,
                                        with_sources=True)
        onward = _exchange_start(own_in, ["forward"], "gather_in_forward_start", relations=OTHER_CHIPS, lands=landed,
                                 own=False)
        (g_in,) = _exchange_wait(onward, ["forward"], onward[-1], "gather_in_forward_wait", own=False,
                                 relations=OTHER_CHIPS)
        return _move_rows(g_in.reshape(IN_PAD, d), _slot_row_of_internal(), "w_in_rows")

    def in_grad_ready(gw_in_t):
        blocks = _move_rows(gw_in_t, _internal_row_of_slot(), "gw_in_rows").reshape(N_DEV, in_blk_pad, d)
        pending["in"] = _exchange_start([blocks], ["scatter"], "grads_in_start")
        return pending["in"][-1][0:1, 0:1]

    def late_weights(after):
        g_out, g_up, g_down = _exchange_wait(rest, ["gather"] * 3, after, "gather_rest_wait")
        return g_out.reshape(d, d), g_up.reshape(2, D_FF, d), g_down.reshape(D_FF, d)

    def ffn_grads_ready(gw_down, gw_up_t):
        blocks = [gw_down.reshape(N_DEV, down_blk, d), gw_up_t.reshape(N_DEV, up_blk, d)]
        pending["ffn"] = _exchange_start(blocks, ["scatter"] * 2, "grads_ffn_start")
        return pending["ffn"][-1][0:1, 0:1]

    def out_grad_ready(gw_out):
        pending["out"] = _exchange_start([gw_out.reshape(N_DEV, d // N_DEV, d)], ["scatter"], "grads_out_start")
        return pending["out"][-1][0:1, 0:1]

    loss_tile, grad_x, gr = _local_step(
        x[0], loss_target[0], meta_f, attn_norm_g + rest[-1][0:1, 0:1], fox_forget_b, ret_norm_g, ffn_norm_g,
        conv_w8, conv_b, final_norm_g.reshape(1, d), first_weight, late_weights, ffn_grads_ready, out_grad_ready,
        in_grad_ready)

    small = [loss_tile, gr["attn_g"], gr["fox_b"], gr["ret_g"], gr["ffn_g"], gr["conv_b"], gr["final_g"],
             gr["meta"], gr["conv_w"]]
    small_kinds = ["gather"] * len(small)
    small_started = _exchange_start(small, small_kinds, "grads_small_start", own=False)

    r_down, r_up = _exchange_wait(pending["ffn"], ["scatter"] * 2, small_started[-1], "grads_ffn_wait")
    (r_out,) = _exchange_wait(pending["out"], ["scatter"], small_started[-1], "grads_out_wait")
    g_w_out = _sum_slots(r_out, "sum_w_out", d // N_DEV)
    g_w_up_t = _sum_slots(r_up, "sum_w_up", up_blk)
    g_w_down = _sum_slots(r_down, "sum_w_down", down_blk)
    as_t = lambda a: a[0].T
    from_t = lambda a: a.T[None]
    d_w_out, m_w_out_n, v_w_out_n = [a[None] for a in _adamw(w_out[0], g_w_out, m_w_out[0], v_w_out[0], "adamw_w_out", 128)]
    up_t = _adamw(as_t(w_up), g_w_up_t, as_t(m_w_up), as_t(v_w_up), "adamw_w_up", up_blk // 2)
    d_w_up, m_w_up_n, v_w_up_n = [from_t(a) for a in up_t]
    d_w_down, m_w_down_n, v_w_down_n = [a[None] for a in _adamw(w_down[0], g_w_down, m_w_down[0], v_w_down[0],
                                                                "adamw_w_down", down_blk)]

    own_small, r_small = _exchange_wait(small_started, small_kinds, up_t[0], "grads_small_wait", own=False,
                                        with_sources=True)
    (loss_all, g_attn, g_fox_b128, g_ret, g_ffn, g_conv_b, g_final, g_meta_full, g_cw_full) = _sum_slots_small(
        r_small, own_small, "sum_small")
    loss = loss_all[0, 0]
    g_fox_b = g_fox_b128[:, :FOX_HEADS]
    g_meta_loc = lax.dynamic_slice(g_meta_full, (0, me * (d // N_DEV)), (N_META, d // N_DEV))
    g_cw_loc = lax.dynamic_slice(g_cw_full, (0, me * cw_blk), (3, cw_blk))

    (r_in,) = _exchange_wait(pending["in"], ["scatter"], r_small[0], "grads_in_wait")
    g_w_in_t = _sum_slots(r_in, "sum_w_in", in_blk_pad)[:in_blk]
    d_w_in, m_w_in_n, v_w_in_n = [from_t(a) for a in _adamw(as_t(w_in), g_w_in_t, as_t(m_w_in), as_t(v_w_in),
                                                            "adamw_w_in", in_blk)]
    g_w_in, g_w_up = g_w_in_t.T, g_w_up_t.T
    row = lambda a: a.reshape(1, d)
    sm_grads = [g_meta_loc, g_attn, g_fox_b, g_ret, g_ffn, g_cw_loc, g_conv_b, g_final]
    sm_w = [meta_tokens, attn_norm_g, fox_forget_b, ret_norm_g, ffn_norm_g, conv_w[0], conv_b, row(final_norm_g)]
    sm_m = [m_meta_tokens, m_attn_norm_g, m_fox_forget_b, m_ret_norm_g, m_ffn_norm_g, m_conv_w[0], m_conv_b,
            row(m_final_norm_g)]
    sm_v = [v_meta_tokens, v_attn_norm_g, v_fox_forget_b, v_ret_norm_g, v_ffn_norm_g, v_conv_w[0], v_conv_b,
            row(v_final_norm_g)]
    dl, ml, vl = [lst[:7] + [lst[7].reshape(d)] for lst in _adamw_small(sm_w, sm_grads, sm_m, sm_v, "adamw_small")]

    def by_weight(meta_, attn_, w_in_, fox_, ret_, w_out_, ffn_, w_up_, cw_, cb_, w_down_, final_):
        return (meta_, attn_, w_in_, fox_, ret_, w_out_, ffn_, w_up_, cw_[None], cb_, w_down_, final_)

    grads_out = by_weight(g_meta_loc, g_attn, g_w_in[None], g_fox_b, g_ret, g_w_out[None], g_ffn, g_w_up[None], g_cw_loc,
                          g_conv_b, g_w_down[None], g_final.reshape(d))
    delta_out = by_weight(dl[0], dl[1], d_w_in, dl[2], dl[3], d_w_out, dl[4], d_w_up, dl[5], dl[6], d_w_down, dl[7])
    m_out = by_weight(ml[0], ml[1], m_w_in_n, ml[2], ml[3], m_w_out_n, ml[4], m_w_up_n, ml[5], ml[6], m_w_down_n, ml[7])
    v_out = by_weight(vl[0], vl[1], v_w_in_n, vl[2], vl[3], v_w_out_n, vl[4], v_w_up_n, vl[5], vl[6], v_w_down_n, vl[7])
    return (loss, grad_x[None]) + grads_out + delta_out + m_out + v_out
```

```python
import numpy as np
import jax
import jax.numpy as jnp
from jax import lax
from jax.experimental import pallas as pl
from jax.experimental.pallas import tpu as pltpu

F32 = jnp.float32
BF16 = jnp.bfloat16

D_MODEL = 1024
N_META = 16
N_PAD = 112
PREFIX = 128
RET_HEADS = 4
RET_DK = 64
RET_DV = 128
FOX_HEADS = 8
FOX_DH = 64
D_FF = 2816
ROPE_BASE = 10000.0
EPS = 1e-6
NEG = -1e30
RET_QK = RET_HEADS * RET_DK
RET_V = RET_HEADS * RET_DV
FOX_W = FOX_HEADS * FOX_DH
IN_WIDTH = 2 * RET_QK + 2 * RET_V + 3 * FOX_W + FOX_HEADS
IN_PAD = 3200
FF_COL_BLOCK = (IN_WIDTH - FOX_HEADS) // 128
QK_SCALE = 0.125

ADAM_LR = 0.001
ADAM_B1 = 0.9
ADAM_B2 = 0.999
ADAM_EPS = 1e-08
ADAM_WD = 0.01
ADAM_STEP = 10

N_DEV = 8
LANE = 128
ROW_TILE = 128
TOK_TILE = 384

NN = (((1,), (0,)), ((), ()))
NT = (((1,), (1,)), ((), ()))
TN = (((0,), (0,)), ((), ()))


def _pcall(body, **kw):
    return pl.pallas_call(body, **kw)


def _params(*sem):
    return pltpu.CompilerParams(dimension_semantics=sem)


def _dot(a, b, dims=NN):
    return lax.dot_general(a, b, dims, preferred_element_type=F32)


def _sigmoid(x):
    return 0.5 * jnp.tanh(0.5 * x) + 0.5


def _matmul(a, b, *, mode, grid, a_spec, b_spec, o_spec, out_shape, name, add=None, add_spec=None, after=None):
    dims = {"nn": NN, "nt": NT, "tn": TN}[mode]
    nk = grid[2]
    has_add = add is not None
    a_list, b_list = (list(a), list(b)) if isinstance(a, (list, tuple)) else ([a], [b])
    a_specs, b_specs = (list(a_spec), list(b_spec)) if isinstance(a_spec, (list, tuple)) else ([a_spec], [b_spec])
    nt = len(a_list)
    n_in = 2 * nt + int(has_add) + int(after is not None)

    def body(*refs):
        a_refs, b_refs = refs[:nt], refs[nt:2 * nt]
        add_ref = refs[2 * nt] if has_add else None
        o_ref = refs[n_in]
        part = _dot(a_refs[0][...].astype(BF16), b_refs[0][...].astype(BF16), dims)
        for ar, br in zip(a_refs[1:], b_refs[1:]):
            part = part + _dot(ar[...].astype(BF16), br[...].astype(BF16), dims)

        def finish(acc):
            if has_add:
                acc = acc + add_ref[...]
            o_ref[...] = acc.astype(o_ref.dtype)

        if nk == 1:
            finish(part)
        else:
            acc_ref = refs[-1]
            k = pl.program_id(2)

            @pl.when(k == 0)
            def _():
                acc_ref[...] = part

            @pl.when(k > 0)
            def _():
                acc_ref[...] += part

            @pl.when(k == nk - 1)
            def _():
                finish(acc_ref[...])

    in_specs = a_specs + b_specs + ([add_spec] if has_add else [])
    args = tuple(a_list) + tuple(b_list) + ((add,) if has_add else ())
    if after is not None:
        in_specs, args = in_specs + [pl.BlockSpec(memory_space=pl.ANY)], args + (after,)
    scratch = [] if nk == 1 else [pltpu.VMEM(tuple(d for d in o_spec.block_shape if d is not None), F32)]
    return _pcall(
        body, name=name, grid=grid, in_specs=in_specs, out_specs=o_spec, out_shape=out_shape,
        scratch_shapes=scratch, compiler_params=_params("parallel", "parallel", "arbitrary"),
    )(*args)


def _mm_simple(a, b, *, mode, tm, tn, tk, out_dtype, name, add=None, after=None):
    if mode == "tn":
        K, M = a.shape
    else:
        M, K = a.shape
    N = b.shape[0] if mode == "nt" else b.shape[1]
    grid = (M // tm, N // tn, K // tk)
    resident = dict(pipeline_mode=pl.Buffered(1)) if (tn == N and tk == K) else {}
    a_spec = pl.BlockSpec((tk, tm), lambda i, j, k: (k, i)) if mode == "tn" else pl.BlockSpec((tm, tk), lambda i, j, k: (i, k))
    b_spec = (pl.BlockSpec((tn, tk), lambda i, j, k: (j, k), **resident) if mode == "nt"
              else pl.BlockSpec((tk, tn), lambda i, j, k: (k, j), **resident))
    o_spec = pl.BlockSpec((tm, tn), lambda i, j, k: (i, j))
    return _matmul(a, b, mode=mode, grid=grid, a_spec=a_spec, b_spec=b_spec, o_spec=o_spec,
                   out_shape=jax.ShapeDtypeStruct((M, N), out_dtype), name=name, add=add,
                   add_spec=o_spec if add is not None else None, after=after)


def _matmul_rows(a_list, a_specs, b_list, b_specs, extras, extra_specs, out_specs, out_shape, epilogue, *,
                 mode, steps, name, after=None, scratch=()):
    dims = {"nn": NN, "nt": NT}[mode]
    nt, ne = len(a_list), len(extras)
    n_in = 2 * nt + ne + int(after is not None)

    def body(*refs):
        acc = _dot(refs[0][...].astype(BF16), refs[nt][...].astype(BF16), dims)
        for k in range(1, nt):
            acc = acc + _dot(refs[k][...].astype(BF16), refs[nt + k][...].astype(BF16), dims)
        epilogue(pl.program_id(0), acc, refs[2 * nt:2 * nt + ne], refs[n_in:])

    in_specs = list(a_specs) + list(b_specs) + list(extra_specs)
    args = tuple(a_list) + tuple(b_list) + tuple(extras)
    if after is not None:
        in_specs, args = in_specs + [pl.BlockSpec(memory_space=pl.ANY)], args + (after,)
    return _pcall(body, name=name, grid=(steps,), in_specs=in_specs, out_specs=out_specs, out_shape=out_shape,
                  scratch_shapes=list(scratch), compiler_params=_params("arbitrary"))(*args)


def _rms_bwd_tile(dy, x, gain, dres):
    r = lax.rsqrt(jnp.mean(x * x, axis=-1, keepdims=True) + EPS)
    xhat = x * r
    u = dy * gain
    return dres + r * (u - xhat * jnp.mean(u * xhat, axis=-1, keepdims=True)), jnp.sum(dy * xhat, axis=0, keepdims=True)


def _loss_tile(i, x, tgt, gain):
    d = x.shape[-1]
    r = lax.rsqrt(jnp.mean(x * x, axis=-1, keepdims=True) + EPS)
    xhat = x * r
    counted = (i * TOK_TILE + lax.broadcasted_iota(jnp.int32, (TOK_TILE, 1), 0)) >= PREFIX
    err = jnp.where(counted, xhat * gain - tgt, 0.0)
    dy = err * (1.0 / d)
    u = dy * gain
    dh = r * (u - xhat * jnp.mean(u * xhat, axis=-1, keepdims=True))
    return 0.5 * jnp.sum(jnp.mean(err * err, axis=-1, keepdims=True)), dh, jnp.sum(dy * xhat, axis=0, keepdims=True)


def _accumulate(ref, i, part):
    @pl.when(i == 0)
    def _():
        ref[...] = part

    @pl.when(i > 0)
    def _():
        ref[...] += part


def _prep_norm(x, meta, gain, name):
    seq, d = x.shape
    t = seq + PREFIX

    def body(xa_ref, xb_ref, xc_ref, meta_ref, g_ref, h_ref, n_ref):
        i = pl.program_id(0)

        @pl.when(i == 0)
        def _():
            h_ref[0:N_PAD, :] = jnp.zeros((N_PAD, d), F32)
            h_ref[N_PAD:ROW_TILE, :] = meta_ref[...]

        @pl.when(i > 0)
        def _():
            h_ref[0:ROW_TILE, :] = xa_ref[...]

        h_ref[ROW_TILE:2 * ROW_TILE, :] = xb_ref[...]
        h_ref[2 * ROW_TILE:3 * ROW_TILE, :] = xc_ref[...]
        h = h_ref[...]
        r = lax.rsqrt(jnp.mean(h * h, axis=-1, keepdims=True) + EPS)
        n_ref[...] = (h * r * g_ref[...]).astype(BF16)

    return _pcall(
        body, name=name, grid=(t // TOK_TILE,),
        in_specs=_shifted_row_specs(d) + [pl.BlockSpec((N_META, d), lambda i: (0, 0)), pl.BlockSpec((1, d), lambda i: (0, 0))],
        out_specs=[pl.BlockSpec((TOK_TILE, d), lambda i: (i, 0)), pl.BlockSpec((TOK_TILE, d), lambda i: (i, 0))],
        out_shape=[jax.ShapeDtypeStruct((t, d), F32), jax.ShapeDtypeStruct((t, d), BF16)],
        compiler_params=_params("parallel"),
    )(x, x, x, meta, gain)


def _shifted_row_specs(d):
    blocks_per_tile = TOK_TILE // ROW_TILE
    return [pl.BlockSpec((ROW_TILE, d), lambda i, r=r: (jnp.maximum(blocks_per_tile * i + r, 0), 0)) for r in (-1, 0, 1)]


def _ret_consts(bk):
    gam = 1.0 - 2.0 ** (-5.0 - np.arange(RET_HEADS))
    n = np.arange(bk)
    same_or_earlier_chunk = (n[None, :] // 64) <= (n[:, None] // 64)
    w = gam[:, None, None] ** np.abs(n[:, None] - n[None, :])[None] * same_or_earlier_chunk[None]
    wq = gam[:, None] ** (n[None, :] + 1.0)
    wk = gam[:, None] ** (bk - 1.0 - n[None, :])
    mask = (np.arange(RET_QK)[None, :] // RET_DK) == np.arange(RET_HEADS)[:, None]
    return (jnp.asarray(w, F32), jnp.asarray(wq[:, :, None], F32), jnp.asarray(wk[:, :, None], F32),
            jnp.asarray(mask[:, None, :], F32), [float(g ** bk) for g in gam])


def _rope_tables(t):
    half = RET_DK // 2
    inv = 1.0 / (ROPE_BASE ** (jnp.arange(half, dtype=F32) / half))
    ang = jnp.arange(t).astype(F32)[:, None] * inv[None, :]
    cos, sin = jnp.cos(ang), jnp.sin(ang)
    return (jnp.tile(jnp.concatenate([cos, cos], axis=1), (1, RET_HEADS)),
            jnp.tile(jnp.concatenate([-sin, sin], axis=1), (1, RET_HEADS)))


def _swap_halves(x):
    outs = []
    for s in range(x.shape[1] // LANE):
        xs = x[:, LANE * s:LANE * (s + 1)]
        lane = lax.broadcasted_iota(jnp.int32, xs.shape, 1)
        outs.append(jnp.where((lane & 32) == 0, pltpu.roll(xs, LANE - 32, axis=1), pltpu.roll(xs, 32, axis=1)))
    return outs[0] if len(outs) == 1 else jnp.concatenate(outs, axis=1)


def _rope(x, cos, sin_signed):
    return x * cos + _swap_halves(x) * sin_signed


def _rope_t(dx, cos, sin_signed):
    return dx * cos + _swap_halves(dx * sin_signed)


def _ret_fwd(proj, cos, sin, gain, name):
    t = proj.shape[0]
    bk = TOK_TILE
    nb = t // bk
    w, wq, wk, mask, g_blk = _ret_consts(bk)

    def body(q_ref, k_ref, v_ref, rg_ref, cos_ref, sin_ref, w_ref, wq_ref, wk_ref, mask_ref, gain_ref,
             opre_ref, og_ref, st_ref, r_ref):
        i = pl.program_id(0)

        @pl.when(i == 0)
        def _():
            r_ref[...] = jnp.zeros_like(r_ref)

        c, s = cos_ref[...], sin_ref[...]
        valid = ((i * bk + lax.broadcasted_iota(jnp.int32, (bk, 1), 0)) >= N_PAD).astype(F32)
        qr = _rope(q_ref[...], c, s)
        kr = _rope(k_ref[...], c, s) * QK_SCALE * valid
        kb = kr.astype(BF16)
        for h in range(RET_HEADS):
            hm = mask_ref[h]
            cols = slice(RET_DV * h, RET_DV * (h + 1))
            vh = v_ref[:, cols].astype(BF16)
            r_prev = r_ref[h]
            st_ref[0, h] = r_prev
            sm = _dot((qr * hm).astype(BF16), kb, NT) * w_ref[h]
            o = _dot(sm.astype(BF16), vh) + _dot((qr * (hm * wq_ref[h])).astype(BF16), r_prev.astype(BF16))
            r_ref[h] = g_blk[h] * r_prev + _dot((kr * wk_ref[h]).astype(BF16), vh, TN)
            opre_ref[:, cols] = o
            rstd = lax.rsqrt(jnp.mean(o * o, axis=-1, keepdims=True) + EPS)
            rg = rg_ref[:, cols]
            og_ref[:, cols] = (o * rstd * gain_ref[:, cols] * (rg * _sigmoid(rg))).astype(BF16)

    full = lambda shape: pl.BlockSpec(shape, lambda i: (0,) * len(shape))
    return _pcall(
        body, name=name, grid=(nb,),
        in_specs=[pl.BlockSpec((bk, RET_QK), lambda i: (i, 0)), pl.BlockSpec((bk, RET_QK), lambda i: (i, 1)),
                  pl.BlockSpec((bk, RET_V), lambda i: (i, 1)), pl.BlockSpec((bk, RET_V), lambda i: (i, 2)),
                  pl.BlockSpec((bk, RET_QK), lambda i: (i, 0)), pl.BlockSpec((bk, RET_QK), lambda i: (i, 0)),
                  full((RET_HEADS, bk, bk)), full((RET_HEADS, bk, 1)), full((RET_HEADS, bk, 1)),
                  full((RET_HEADS, 1, RET_QK)), full((1, RET_V))],
        out_specs=[pl.BlockSpec((bk, RET_V), lambda i: (i, 0)), pl.BlockSpec((bk, RET_V), lambda i: (i, 0)),
                   pl.BlockSpec((1, RET_HEADS, RET_QK, RET_DV), lambda i: (i, 0, 0, 0))],
        out_shape=[jax.ShapeDtypeStruct((t, RET_V), F32), jax.ShapeDtypeStruct((t, RET_V + FOX_W), BF16),
                   jax.ShapeDtypeStruct((nb, RET_HEADS, RET_QK, RET_DV), F32)],
        scratch_shapes=[pltpu.VMEM((RET_HEADS, RET_QK, RET_DV), F32)],
        compiler_params=_params("arbitrary"),
    )(proj, proj, proj, proj, cos, sin, w, wq, wk, mask, gain)


def _ret_bwd(proj, cos, sin, gain, dmixed, opre, states, name):
    t = proj.shape[0]
    bk = TOK_TILE
    nb = t // bk
    w, wq, wk, mask, g_blk = _ret_consts(bk)
    v0, g0 = 2 * RET_QK, 2 * RET_QK + RET_V

    def body(q_ref, k_ref, v_ref, rg_ref, cos_ref, sin_ref, w_ref, wq_ref, wk_ref, mask_ref, gain_ref,
             dog_ref, opre_ref, st_ref, dp_ref, gg_ref, dr_ref):
        step = pl.program_id(0)
        i = nb - 1 - step

        @pl.when(step == 0)
        def _():
            dr_ref[...] = jnp.zeros_like(dr_ref)
            gg_ref[...] = jnp.zeros_like(gg_ref)

        c, s = cos_ref[...], sin_ref[...]
        valid = ((i * bk + lax.broadcasted_iota(jnp.int32, (bk, 1), 0)) >= N_PAD).astype(F32)
        qr = _rope(q_ref[...], c, s)
        kr = _rope(k_ref[...], c, s) * QK_SCALE * valid
        kb = kr.astype(BF16)
        dqr = jnp.zeros((bk, RET_QK), F32)
        dkr = jnp.zeros((bk, RET_QK), F32)
        for h in range(RET_HEADS):
            hm = mask_ref[h]
            cols = slice(RET_DV * h, RET_DV * (h + 1))
            vh = v_ref[:, cols].astype(BF16)
            o = opre_ref[:, cols]
            rstd = lax.rsqrt(jnp.mean(o * o, axis=-1, keepdims=True) + EPS)
            xhat = o * rstd
            rg = rg_ref[:, cols]
            sg = _sigmoid(rg)
            gate = rg * sg
            gn = gain_ref[:, cols]
            dog = dog_ref[:, cols]
            dp_ref[:, g0 + RET_DV * h:g0 + RET_DV * (h + 1)] = (
                dog * xhat * gn * (sg * (1.0 + rg * (1.0 - sg)))).astype(BF16)
            gg_ref[:, cols] += jnp.sum(dog * xhat * gate, axis=0, keepdims=True)
            dxh = dog * gn * gate
            do = (rstd * (dxh - xhat * jnp.mean(dxh * xhat, axis=-1, keepdims=True))).astype(BF16)
            qm = (qr * hm).astype(BF16)
            qw = (qr * (hm * wq_ref[h])).astype(BF16)
            kw = (kr * wk_ref[h]).astype(BF16)
            wh = w_ref[h]
            sm = (_dot(qm, kb, NT) * wh).astype(BF16)
            ds = (_dot(do, vh, NT) * wh).astype(BF16)
            dr = dr_ref[h]
            drb = dr.astype(BF16)
            dp_ref[:, v0 + RET_DV * h:v0 + RET_DV * (h + 1)] = (_dot(sm, do, TN) + _dot(kw, drb)).astype(BF16)
            dqr = dqr + _dot(ds, kb) * hm + _dot(do, st_ref[0, h].astype(BF16), NT) * (hm * wq_ref[h])
            dkr = dkr + _dot(ds, qm, TN) + _dot(vh, drb, NT) * wk_ref[h]
            dr_ref[h] = g_blk[h] * dr + _dot(qw, do, TN)
        dp_ref[:, 0:RET_QK] = _rope_t(dqr, c, s).astype(BF16)
        dp_ref[:, RET_QK:2 * RET_QK] = _rope_t(dkr * (QK_SCALE * valid), c, s).astype(BF16)

    full = lambda shape: pl.BlockSpec(shape, lambda i: (0,) * len(shape))
    rev = lambda col: (lambda i: (nb - 1 - i, col))
    return _pcall(
        body, name=name, grid=(nb,),
        in_specs=[pl.BlockSpec((bk, RET_QK), rev(0)), pl.BlockSpec((bk, RET_QK), rev(1)),
                  pl.BlockSpec((bk, RET_V), rev(1)), pl.BlockSpec((bk, RET_V), rev(2)),
                  pl.BlockSpec((bk, RET_QK), rev(0)), pl.BlockSpec((bk, RET_QK), rev(0)),
                  full((RET_HEADS, bk, bk)), full((RET_HEADS, bk, 1)), full((RET_HEADS, bk, 1)),
                  full((RET_HEADS, 1, RET_QK)), full((1, RET_V)),
                  pl.BlockSpec((bk, RET_V), rev(0)), pl.BlockSpec((bk, RET_V), rev(0)),
                  pl.BlockSpec((1, RET_HEADS, RET_QK, RET_DV), lambda i: (nb - 1 - i, 0, 0, 0))],
        out_specs=[pl.BlockSpec((bk, g0 + RET_V), rev(0)), pl.BlockSpec((1, RET_V), lambda i: (0, 0))],
        out_shape=[jax.ShapeDtypeStruct((t, IN_PAD), BF16), jax.ShapeDtypeStruct((1, RET_V), F32)],
        scratch_shapes=[pltpu.VMEM((RET_HEADS, RET_QK, RET_DV), F32)],
        compiler_params=_params("arbitrary"),
    )(proj, proj, proj, proj, cos, sin, w, wq, wk, mask, gain, dmixed, opre, states)


def _forget_cumsum(proj, bias, name):
    t = proj.shape[0]
    rt = TOK_TILE
    nb = t // rt
    tril = jnp.asarray(np.tril(np.ones((rt, rt))), F32)

    def body(z_ref, b_ref, tril_ref, c_ref, carry_ref):
        i = pl.program_id(0)

        @pl.when(i == 0)
        def _():
            carry_ref[...] = jnp.zeros_like(carry_ref)

        z = z_ref[...] + b_ref[...]
        logf = jnp.minimum(z, 0.0) - jnp.log(1.0 + jnp.exp(-jnp.abs(z)))
        c = lax.dot_general(tril_ref[...], logf, NN, precision=lax.Precision.HIGHEST,
                            preferred_element_type=F32) + carry_ref[...]
        c_ref[...] = c
        carry_ref[...] = c[rt - 1:rt, :]

    return _pcall(
        body, name=name, grid=(nb,),
        in_specs=[pl.BlockSpec((rt, LANE), lambda i: (i, FF_COL_BLOCK)), pl.BlockSpec((1, LANE), lambda i: (0, 0)),
                  pl.BlockSpec((rt, rt), lambda i: (0, 0))],
        out_specs=pl.BlockSpec((rt, LANE), lambda i: (i, 0)),
        out_shape=jax.ShapeDtypeStruct((t, LANE), F32),
        scratch_shapes=[pltpu.VMEM((1, LANE), F32)],
        compiler_params=_params("arbitrary"),
    )(proj, bias, tril)


def _forget_cumsum_bwd(proj, bias, drs, dcs, dproj, name):
    t = proj.shape[0]
    rt = TOK_TILE
    nb = t // rt
    triu = jnp.asarray(np.triu(np.ones((rt, rt))), F32)

    def body(z_ref, b_ref, triu_ref, drs_ref, dcs_ref, dproj_in, dz_ref, gb_ref, carry_ref):
        step = pl.program_id(0)

        @pl.when(step == 0)
        def _():
            carry_ref[...] = jnp.zeros_like(carry_ref)
            gb_ref[...] = jnp.zeros_like(gb_ref)

        dlogf = lax.dot_general(triu_ref[...], drs_ref[...] - dcs_ref[...], NN, precision=lax.Precision.HIGHEST,
                                preferred_element_type=F32) + carry_ref[...]
        carry_ref[...] = dlogf[0:1, :]
        z = z_ref[...] + b_ref[...]
        is_head = lax.broadcasted_iota(jnp.int32, (rt, LANE), 1) < FOX_HEADS
        dz = jnp.where(is_head, dlogf / (1.0 + jnp.exp(z)), 0.0)
        dz_ref[...] = dz.astype(BF16)
        gb_ref[...] += jnp.sum(dz, axis=0, keepdims=True)

    return _pcall(
        body, name=name, grid=(nb,),
        in_specs=[pl.BlockSpec((rt, LANE), lambda i: (nb - 1 - i, FF_COL_BLOCK)),
                  pl.BlockSpec((1, LANE), lambda i: (0, 0)),
                  pl.BlockSpec((rt, rt), lambda i: (0, 0)),
                  pl.BlockSpec((rt, LANE), lambda i: (nb - 1 - i, 0)),
                  pl.BlockSpec((rt, LANE), lambda i: (nb - 1 - i, 0)),
                  pl.BlockSpec(memory_space=pl.ANY)],
        out_specs=[pl.BlockSpec((rt, LANE), lambda i: (nb - 1 - i, FF_COL_BLOCK)),
                   pl.BlockSpec((1, LANE), lambda i: (0, 0))],
        out_shape=[jax.ShapeDtypeStruct(dproj.shape, BF16), jax.ShapeDtypeStruct((1, LANE), F32)],
        input_output_aliases={5: 0},
        scratch_shapes=[pltpu.VMEM((1, LANE), F32)],
        compiler_params=_params("arbitrary"),
    )(proj, bias, triu, drs, dcs, dproj)


FOX_PAIRS = FOX_HEADS // 2
L_ONE_Q = FOX_DH
L_ONE_K = FOX_DH + 3
L_LSE = FOX_DH + 4


def _split3(x):
    hi = x.astype(BF16).astype(F32)
    r = x - hi
    mid = r.astype(BF16).astype(F32)
    return hi, mid, r - mid


def _head_to_low(slab, e):
    return slab if e == 0 else pltpu.roll(slab, FOX_DH, axis=1)


def _pair(a, b, low):
    return jnp.where(low, a, pltpu.roll(b, FOX_DH, axis=1))


def _fox_prep(proj, c, name):
    t = proj.shape[0]
    tq = TOK_TILE

    def body(p_ref, c_ref, qa_ref, ka_ref, va_ref, qt_ref, vt_ref):
        i = pl.program_id(0)
        lane = lax.broadcasted_iota(jnp.int32, (tq, LANE), 1)
        low = lane < FOX_DH
        live = (i * tq + lax.broadcasted_iota(jnp.int32, (tq, 1), 0)) >= N_PAD
        q_tail = jnp.where(lane < L_ONE_Q + 3, 1.0, 0.0)
        k_ones = (lane >= L_ONE_K) & (lane < L_ONE_K + 4)
        v_tail = jnp.where(lane < FOX_DH + 2, 1.0, 0.0)
        bias_parts = _split3(jnp.where(live, -c_ref[...], NEG))
        for pair in range(FOX_PAIRS):
            base = 3 * LANE * pair
            for e in range(2):
                h = 2 * pair + e
                q = _head_to_low(p_ref[:, base:base + LANE], e)
                k = _head_to_low(p_ref[:, base + LANE:base + 2 * LANE], e)
                v = _head_to_low(p_ref[:, base + 2 * LANE:base + 3 * LANE], e)
                hi, mid, lo = [part[:, h:h + 1] for part in bias_parts]
                ka = jnp.where(low, k, jnp.where(k_ones, 1.0, 0.0))
                ka = jnp.where(lane == L_ONE_Q, hi, jnp.where(lane == L_ONE_Q + 1, mid, jnp.where(lane == L_ONE_Q + 2, lo, ka)))
                qa = jnp.where(low, q * QK_SCALE, q_tail)
                va = jnp.where(low, v, v_tail)
                qa_ref[h] = qa.astype(BF16)
                ka_ref[h] = ka.astype(BF16)
                va_ref[h] = va.astype(BF16)
                qt_ref[h] = qa.T.astype(BF16)
                vt_ref[h] = va.T.astype(BF16)

    out = jax.ShapeDtypeStruct((FOX_HEADS, t, LANE), BF16)
    out_t = jax.ShapeDtypeStruct((FOX_HEADS, t // tq, LANE, tq), BF16)
    ospec = pl.BlockSpec((FOX_HEADS, tq, LANE), lambda i: (0, i, 0))
    tspec = pl.BlockSpec((FOX_HEADS, None, LANE, tq), lambda i: (0, i, 0, 0))
    return _pcall(
        body, name=name, grid=(t // tq,),
        in_specs=[pl.BlockSpec((tq, 3 * FOX_W), lambda i: (i, 1)), pl.BlockSpec((tq, LANE), lambda i: (i, 0))],
        out_specs=[ospec, ospec, ospec, tspec, tspec], out_shape=[out, out, out, out_t, out_t],
        compiler_params=_params("parallel"),
    )(proj, c)


STEP_PAIRS = 2
STEP_HEADS = 2 * STEP_PAIRS
FOX_GROUPS = FOX_PAIRS // STEP_PAIRS
FWD_PAIRS = 4
FWD_HEADS = 2 * FWD_PAIRS
FWD_GROUPS = FOX_PAIRS // FWD_PAIRS


def _fox_fwd(qt, ka, vt, mixed, name):
    nh, nq, tq, _ = ka.shape
    t = nq * tq

    def body(qt_ref, ka_ref, vt_ref, mixed_in, mixed_ref, o_ref, lse_ref):
        i = pl.program_id(1)
        lane = lax.broadcasted_iota(jnp.int32, (tq, LANE), 1)
        key_le_query = lax.broadcasted_iota(jnp.int32, (tq, tq), 0) <= lax.broadcasted_iota(jnp.int32, (tq, tq), 1)

        def logits(j):
            return [_dot(ka_ref[h, j], qt_ref[h]) for h in range(FWD_HEADS)]

        def update(j, scores, carry, diagonal):
            new = []
            for h in range(FWD_HEADS):
                m, acc = carry[h]
                s = jnp.where(key_le_query, scores[h], NEG) if diagonal else scores[h]
                m_new = jnp.maximum(m, jnp.max(s, axis=0, keepdims=True))
                p = jnp.exp(s - m_new).astype(BF16)
                new.append((m_new, jnp.exp(m - m_new) * acc + _dot(vt_ref[h, j], p)))
            return tuple(new)

        init = tuple((jnp.full((1, tq), NEG, F32), jnp.zeros((LANE, tq), F32)) for _ in range(FWD_HEADS))
        carry = lax.fori_loop(0, i, lambda j, cr: update(j, logits(j), cr, False), init)
        outs, lse_rows = [], []
        for m, acc in update(i, logits(i), carry, True):
            l = acc[FOX_DH:FOX_DH + 1, :]
            outs.append((acc / l).T)
            lse_rows.append(m + jnp.log(l))
        lse_rows.append(jnp.zeros((LANE - FWD_HEADS, tq), F32))
        o_all = jnp.concatenate([_pair(outs[2 * c], outs[2 * c + 1], lane < FOX_DH) for c in range(FWD_PAIRS)], axis=1)
        mixed_ref[...] = o_all.astype(BF16)
        o_ref[...] = o_all
        lse_ref[...] = jnp.concatenate(lse_rows, axis=0).T

    width = FWD_PAIRS * LANE
    whole = pl.BlockSpec((FWD_HEADS, nq, tq, LANE), lambda g, i: (g, 0, 0, 0), pipeline_mode=pl.Buffered(1))
    whole_t = pl.BlockSpec((FWD_HEADS, nq, LANE, tq), lambda g, i: (g, 0, 0, 0), pipeline_mode=pl.Buffered(1))
    return _pcall(
        body, name=name, grid=(FWD_GROUPS, nq),
        in_specs=[pl.BlockSpec((FWD_HEADS, None, LANE, tq), lambda g, i: (g, i, 0, 0)), whole, whole_t,
                  pl.BlockSpec(memory_space=pl.ANY)],
        out_specs=[pl.BlockSpec((tq, width), lambda g, i: (i, RET_V // width + g)),
                   pl.BlockSpec((tq, width), lambda g, i: (i, g)),
                   pl.BlockSpec((None, tq, LANE), lambda g, i: (g, i, 0))],
        out_shape=[jax.ShapeDtypeStruct(mixed.shape, BF16), jax.ShapeDtypeStruct((t, FOX_W), F32),
                   jax.ShapeDtypeStruct((FWD_GROUPS, t, LANE), F32)],
        input_output_aliases={3: 0},
        compiler_params=_params("parallel", "parallel"),
    )(qt, ka, vt, mixed)


def _fox_prep_bwd(dmixed, o_fox, lse, qa, name):
    t = dmixed.shape[0]
    tq = TOK_TILE

    def body(dm_ref, o_ref, lse_ref, qa_ref, qab_ref, doa_ref):
        i = pl.program_id(0)
        lane = lax.broadcasted_iota(jnp.int32, (tq, LANE), 1)
        low = lane < FOX_DH
        live = (i * tq + lax.broadcasted_iota(jnp.int32, (tq, 1), 0)) >= N_PAD
        lse_parts = [_split3(jnp.where(live, -lse_ref[grp], 0.0)) for grp in range(FWD_GROUPS)]
        for pair in range(FOX_PAIRS):
            cols = slice(LANE * pair, LANE * (pair + 1))
            d_slab = dm_ref[:, cols]
            prod = d_slab * o_ref[:, cols]
            for e in range(2):
                h = 2 * pair + e
                nd = -jnp.sum(jnp.where(low, _head_to_low(prod, e), 0.0), axis=-1, keepdims=True)
                nd_hi = nd.astype(BF16).astype(F32)
                doa = jnp.where(low, _head_to_low(d_slab, e), 0.0)
                doa = jnp.where(lane == FOX_DH, nd_hi, jnp.where(lane == FOX_DH + 1, nd - nd_hi, doa))
                doa_ref[h] = doa.astype(BF16)
                lane_h = h % FWD_HEADS
                hi, mid, lo = [part[:, lane_h:lane_h + 1] for part in lse_parts[h // FWD_HEADS]]
                qab = qa_ref[h].astype(F32)
                qab = jnp.where(lane == L_LSE, hi, jnp.where(lane == L_LSE + 1, mid, jnp.where(lane == L_LSE + 2, lo, qab)))
                qab_ref[h] = qab.astype(BF16)

    out = jax.ShapeDtypeStruct((FOX_HEADS, t, LANE), BF16)
    hspec = pl.BlockSpec((FOX_HEADS, tq, LANE), lambda i: (0, i, 0))
    return _pcall(
        body, name=name, grid=(t // tq,),
        in_specs=[pl.BlockSpec((tq, FOX_W), lambda i: (i, 1)), pl.BlockSpec((tq, FOX_W), lambda i: (i, 0)),
                  pl.BlockSpec((FWD_GROUPS, tq, LANE), lambda i: (0, i, 0)), hspec],
        out_specs=[hspec, hspec], out_shape=[out, out],
        compiler_params=_params("parallel"),
    )(dmixed, o_fox, lse, qa)


def _fox_bwd(qab, doa, ka, va, dproj, name):
    nh, nq, tq, _ = qab.shape
    t = nq * tq
    slab = 3 * LANE * STEP_PAIRS
    group0 = (2 * RET_QK + 2 * RET_V) // slab

    def body(qab_ref, doa_ref, ka_ref, va_ref, dproj_in, dp_ref, drs_ref, dcs_ref, dq_ref):
        g, j = pl.program_id(0), pl.program_id(1)

        @pl.when((g == 0) & (j == 0))
        def _():
            drs_ref[...] = jnp.zeros_like(drs_ref)
            dcs_ref[...] = jnp.zeros_like(dcs_ref)

        @pl.when(j == 0)
        def _():
            dq_ref[...] = jnp.zeros_like(dq_ref)

        lane = lax.broadcasted_iota(jnp.int32, (tq, LANE), 1)
        low = lane < FOX_DH
        key_le_query = lax.broadcasted_iota(jnp.int32, (tq, tq), 0) <= lax.broadcasted_iota(jnp.int32, (tq, tq), 1)

        def by_head(c, a, b, col):
            h = STEP_HEADS * g + 2 * c
            return jnp.where(lane == h, a[:, col:col + 1], jnp.where(lane == h + 1, b[:, col:col + 1], 0.0))


        def step(i, carry, diagonal):
            st = [_dot(ka_ref[h], qab_ref[h, i], NT) for h in range(STEP_HEADS)]
            dpt = [_dot(va_ref[h], doa_ref[h, i], NT) for h in range(STEP_HEADS)]
            new = []
            for h in range(STEP_HEADS):
                p = jnp.exp(st[h])
                if diagonal:
                    p = jnp.where(key_le_query, p, 0.0)
                ds = (p * dpt[h]).astype(BF16)
                dq_ref[h, i] += _dot(ds, ka_ref[h], TN)
                dk, dv = carry[h]
                new.append((dk + _dot(ds, qab_ref[h, i]), dv + _dot(p.astype(BF16), doa_ref[h, i])))
            return tuple(new)

        zero = jnp.zeros((tq, LANE), F32)
        carry = step(j, tuple((zero, zero) for _ in range(STEP_HEADS)), True)
        carry = lax.fori_loop(j + 1, nq, lambda i, cr: step(i, cr, False), carry)
        rows = pl.ds(pl.multiple_of(j * tq, tq), tq)
        for c in range(STEP_PAIRS):
            (dka, dva), (dkb, dvb) = carry[2 * c], carry[2 * c + 1]
            c0 = 3 * LANE * c
            dp_ref[rows, c0 + LANE:c0 + 2 * LANE] = _pair(dka, dkb, low).astype(BF16)
            dp_ref[rows, c0 + 2 * LANE:c0 + 3 * LANE] = _pair(dva, dvb, low).astype(BF16)
            dcs_ref[rows, :] += by_head(c, dka, dkb, L_ONE_Q)

        @pl.when(j == nq - 1)
        def _():
            for c in range(STEP_PAIRS):
                for blk in range(nq):
                    r = slice(blk * tq, (blk + 1) * tq)
                    a, b = dq_ref[2 * c, blk], dq_ref[2 * c + 1, blk]
                    dp_ref[r, 3 * LANE * c:3 * LANE * c + LANE] = (_pair(a, b, low) * QK_SCALE).astype(BF16)
                    drs_ref[r, :] += by_head(c, a, b, L_ONE_K)

    whole = pl.BlockSpec((STEP_HEADS, nq, tq, LANE), lambda g, j: (g, 0, 0, 0), pipeline_mode=pl.Buffered(1))
    blk = pl.BlockSpec((STEP_HEADS, None, tq, LANE), lambda g, j: (g, j, 0, 0))
    sums = pl.BlockSpec((t, LANE), lambda g, j: (0, 0), pipeline_mode=pl.Buffered(1))
    return _pcall(
        body, name=name, grid=(FOX_GROUPS, nq),
        in_specs=[whole, whole, blk, blk, pl.BlockSpec(memory_space=pl.ANY)],
        out_specs=[pl.BlockSpec((t, slab), lambda g, j: (0, group0 + g)), sums, sums],
        out_shape=[jax.ShapeDtypeStruct(dproj.shape, BF16), jax.ShapeDtypeStruct((t, LANE), F32),
                   jax.ShapeDtypeStruct((t, LANE), F32)],
        input_output_aliases={4: 0},
        scratch_shapes=[pltpu.VMEM((STEP_HEADS, nq, tq, LANE), F32)],
        compiler_params=_params("arbitrary", "arbitrary"),
    )(qab, doa, ka, va, dproj)


HALO = 8


def _rows_ext(ref, r0, rows, t, before, after):
    lo, hi = r0 - before, r0 + rows + after
    width = ref.shape[-1]
    parts = []
    if lo < 0:
        parts.append(jnp.zeros((-lo, width), F32))
    parts.append(ref[max(lo, 0):min(hi, t), :].astype(F32))
    if hi > t:
        parts.append(jnp.zeros((hi - t, width), F32))
    return parts[0] if len(parts) == 1 else jnp.concatenate(parts, axis=0)


def _conv_taps(a_ext, r0_ext, cw_ref, cb_ref):
    n = a_ext.shape[0]
    if r0_ext < N_PAD:
        row = r0_ext + lax.broadcasted_iota(jnp.int32, (n, 1), 0)
        a_ext = jnp.where(row >= N_PAD, a_ext, 0.0)
    a1 = pltpu.roll(a_ext, 1, axis=0)
    a2 = pltpu.roll(a_ext, 2, axis=0)
    acc = cb_ref[...] + a2 * cw_ref[0:1, :] + a1 * cw_ref[1:2, :] + a_ext * cw_ref[2:3, :]
    return a_ext, a1, a2, acc


FF_COLS = 256


def _up_conv_fwd(n2, w_up_t, conv_w8, conv_b, name):
    t, d = n2.shape
    f = w_up_t.shape[1]
    rows = TOK_TILE
    starts = list(range(0, t, rows))

    def body(n_ref, wa_ref, wb_ref, cw_ref, cb_ref, up_ref, g_ref):
        def project(r0):
            n_rows = n_ref[r0:r0 + rows, :]
            up_ref[0, r0:r0 + rows, :] = _dot(n_rows, wa_ref[...], NT)
            up_ref[1, r0:r0 + rows, :] = _dot(n_rows, wb_ref[...], NT)

        def activate(r0):
            a_ext = _rows_ext(up_ref.at[0], r0, rows, t, HALO, 0)
            _, _, _, acc = _conv_taps(a_ext, r0 - HALO, cw_ref, cb_ref)
            acc = acc[HALO:, :]
            g_ref[r0:r0 + rows, :] = (acc * _sigmoid(acc) * up_ref[1, r0:r0 + rows, :]).astype(BF16)

        project(starts[0])
        for r0, r_next in zip(starts, starts[1:] + [None]):
            if r_next is not None:
                project(r_next)
            activate(r0)

    return _pcall(
        body, name=name, grid=(f // FF_COLS,),
        in_specs=[pl.BlockSpec((t, d), lambda j: (0, 0), pipeline_mode=pl.Buffered(1)),
                  pl.BlockSpec((None, FF_COLS, d), lambda j: (0, j, 0)), pl.BlockSpec((None, FF_COLS, d), lambda j: (1, j, 0)),
                  pl.BlockSpec((8, FF_COLS), lambda j: (0, j)), pl.BlockSpec((1, FF_COLS), lambda j: (0, j))],
        out_specs=[pl.BlockSpec((2, t, FF_COLS), lambda j: (0, 0, j)), pl.BlockSpec((t, FF_COLS), lambda j: (0, j))],
        out_shape=[jax.ShapeDtypeStruct((2, t, f), F32), jax.ShapeDtypeStruct((t, f), BF16)],
        compiler_params=_params("parallel"),
    )(n2, w_up_t, w_up_t, conv_w8, conv_b)


def _dg_conv_bwd(up, conv_w8, conv_b, dh2, w_down, name):
    _, t, f = up.shape
    d = dh2.shape[1]
    rows = TOK_TILE
    starts = list(range(0, t, rows))

    def body(a_ref, b_ref, cw_ref, cb_ref, dh_ref, wd_ref, dup_ref, gcw_ref, gcb_ref, dg_ref):
        def project(r0):
            dg_ref[r0:r0 + rows, :] = _dot(dh_ref[r0:r0 + rows, :], wd_ref[...], NT)

        gw = [jnp.zeros((1, FF_COLS), F32) for _ in range(3)]
        gb = jnp.zeros((1, FF_COLS), F32)
        project(starts[0])
        for r0, r_next in zip(starts, starts[1:] + [None]):
            if r_next is not None:
                project(r_next)
            a_ext = _rows_ext(a_ref, r0, rows, t, HALO, HALO)
            b_ext = _rows_ext(b_ref, r0, rows, t, HALO, HALO)
            dg_ext = _rows_ext(dg_ref, r0, rows, t, HALO, HALO)
            a0, a1, a2, acc = _conv_taps(a_ext, r0 - HALO, cw_ref, cb_ref)
            sg = _sigmoid(acc)
            dacc = dg_ext * b_ext * (sg * (1.0 + acc * (1.0 - sg)))
            n = dacc.shape[0]
            da = (dacc * cw_ref[2:3, :] + pltpu.roll(dacc, n - 1, axis=0) * cw_ref[1:2, :]
                  + pltpu.roll(dacc, n - 2, axis=0) * cw_ref[0:1, :])
            core = slice(HALO, HALO + rows)
            da = da[core, :]
            if r0 < N_PAD:
                row = r0 + lax.broadcasted_iota(jnp.int32, (rows, 1), 0)
                da = jnp.where(row >= N_PAD, da, 0.0)
            dup_ref[0, r0:r0 + rows, :] = da.astype(BF16)
            dup_ref[1, r0:r0 + rows, :] = (dg_ext * acc * sg)[core, :].astype(BF16)
            dacc_c = dacc[core, :]
            gw[0] = gw[0] + jnp.sum(dacc_c * a2[core, :], axis=0, keepdims=True)
            gw[1] = gw[1] + jnp.sum(dacc_c * a1[core, :], axis=0, keepdims=True)
            gw[2] = gw[2] + jnp.sum(dacc_c * a0[core, :], axis=0, keepdims=True)
            gb = gb + jnp.sum(dacc_c, axis=0, keepdims=True)
        gcw_ref[...] = jnp.zeros((8, FF_COLS), F32)
        for tap in range(3):
            gcw_ref[tap:tap + 1, :] = gw[tap]
        gcb_ref[...] = gb

    return _pcall(
        body, name=name, grid=(f // FF_COLS,),
        in_specs=[pl.BlockSpec((None, t, FF_COLS), lambda j: (0, 0, j)), pl.BlockSpec((None, t, FF_COLS), lambda j: (1, 0, j)),
                  pl.BlockSpec((8, FF_COLS), lambda j: (0, j)), pl.BlockSpec((1, FF_COLS), lambda j: (0, j)),
                  pl.BlockSpec((t, d), lambda j: (0, 0), pipeline_mode=pl.Buffered(1)),
                  pl.BlockSpec((FF_COLS, d), lambda j: (j, 0))],
        out_specs=[pl.BlockSpec((2, t, FF_COLS), lambda j: (0, 0, j)), pl.BlockSpec((8, FF_COLS), lambda j: (0, j)),
                   pl.BlockSpec((1, FF_COLS), lambda j: (0, j))],
        out_shape=[jax.ShapeDtypeStruct((2, t, f), BF16), jax.ShapeDtypeStruct((8, f), F32),
                   jax.ShapeDtypeStruct((1, f), F32)],
        scratch_shapes=[pltpu.VMEM((t, FF_COLS), F32)],
        compiler_params=_params("parallel"),
    )(up, up, conv_w8, conv_b, dh2, w_down)


def _exchange(arrays, kinds, name, after=None):
    n = len(arrays)
    npeer = N_DEV - 1
    n_in = n + int(after is not None)

    def body(*refs):
        ins, outs = refs[:n], refs[n_in:n_in + n]
        send_sems, recv_sems, local_sems = refs[n_in + n:]
        x, y, c = lax.axis_index("x"), lax.axis_index("y"), lax.axis_index("c")
        me = 4 * x + 2 * y + c
        copies, locals_ = [], []
        for a in range(n):
            gather = kinds[a] == "gather"
            own = pltpu.make_async_copy(ins[a] if gather else ins[a].at[me], outs[a].at[me], local_sems.at[a])
            own.start()
            locals_.append(own)
            for d in range(1, N_DEV):
                px = 1 - x if d & 4 else x
                py = 1 - y if d & 2 else y
                pc = 1 - c if d & 1 else c
                src = ins[a] if gather else ins[a].at[4 * px + 2 * py + pc]
                cp = pltpu.make_async_remote_copy(
                    src_ref=src, dst_ref=outs[a].at[me],
                    send_sem=send_sems.at[a * npeer + d - 1], recv_sem=recv_sems.at[a * npeer + d - 1],
                    device_id=(px, py, pc), device_id_type=pl.DeviceIdType.MESH)
                cp.start()
                copies.append(cp)
        for cp in copies:
            cp.wait_recv()
        for cp in copies:
            cp.wait_send()
        for own in locals_:
            own.wait()

    out_shape = [jax.ShapeDtypeStruct((N_DEV,) + (a.shape if k == "gather" else a.shape[1:]), a.dtype)
                 for a, k in zip(arrays, kinds)]
    return _pcall(
        body, name=name,
        in_specs=[pl.BlockSpec(memory_space=pl.ANY)] * n_in,
        out_specs=[pl.BlockSpec(memory_space=pl.ANY)] * n,
        out_shape=out_shape,
        scratch_shapes=[pltpu.SemaphoreType.DMA((n * npeer,)), pltpu.SemaphoreType.DMA((n * npeer,)),
                        pltpu.SemaphoreType.DMA((n,))],
        compiler_params=pltpu.CompilerParams(has_side_effects=True),
    )(*arrays, *([] if after is None else [after]))


ALL_PEERS = tuple(range(1, N_DEV))
SAME_CORE_AND_SIBLING = (1, 2, 4, 6)
OTHER_CHIPS = (2, 4, 6)


def _peer_copies(srcs, lands, kinds, send_sems, recv_sems, relations=ALL_PEERS):
    x, y, c = lax.axis_index("x"), lax.axis_index("y"), lax.axis_index("c")
    me = 4 * x + 2 * y + c
    copies = []
    for a in range(len(srcs)):
        for d in relations:
            px = 1 - x if d & 4 else x
            py = 1 - y if d & 2 else y
            pc = 1 - c if d & 1 else c
            peer = 4 * px + 2 * py + pc
            k = a * (N_DEV - 1) + d - 1
            if kinds[a] == "forward":
                src, dst, target = lands[a].at[peer], lands[a].at[peer], (x, y, 1 - c)
            else:
                src, dst, target = (srcs[a] if kinds[a] == "gather" else srcs[a].at[peer]), lands[a].at[me], (px, py, pc)
            copies.append(pltpu.make_async_remote_copy(
                src_ref=src, dst_ref=dst, send_sem=send_sems.at[k], recv_sem=recv_sems.at[k],
                device_id=target, device_id_type=pl.DeviceIdType.MESH))
    return copies


def _own_copies(srcs, lands, kinds, sems):
    me = 4 * lax.axis_index("x") + 2 * lax.axis_index("y") + lax.axis_index("c")
    first = len(srcs) * (N_DEV - 1)
    return [pltpu.make_async_copy(srcs[a].at[me] if kinds[a] == "scatter" else srcs[a], lands[a].at[me], sems.at[first + a])
            for a in range(len(srcs))]


def _exchange_start(arrays, kinds, name, after=None, relations=ALL_PEERS, lands=None, own=True):
    n = len(arrays)
    nsem = n * (N_DEV - 1) + n
    hbm = pl.BlockSpec(memory_space=pltpu.HBM)
    sem = pl.BlockSpec(memory_space=pltpu.SEMAPHORE)
    land_shapes = ([l.shape for l in lands] if lands is not None else
                   [(N_DEV,) + (a.shape if k == "gather" else a.shape[1:]) for a, k in zip(arrays, kinds)])

    n_in = 2 * n + int(after is not None)

    def body(*refs):
        srcs, land_refs = refs[:n], refs[n:2 * n]
        send_sems, recv_sems = refs[n_in], refs[n_in + 1]
        token = refs[-1]
        for cp in _peer_copies(srcs, land_refs, kinds, send_sems, recv_sems, relations):
            cp.start()
        for cp in _own_copies(srcs, land_refs, kinds, send_sems) if own else []:
            cp.start()
        token[...] = jnp.zeros_like(token)

    operands = [pltpu.with_memory_space_constraint(a, pltpu.HBM) for a in arrays]
    operands += (list(lands) if lands is not None else
                 [pltpu.with_memory_space_constraint(lax.empty(s, a.dtype), pltpu.HBM) for s, a in zip(land_shapes, arrays)])
    operands += [] if after is None else [after]
    out = _pcall(
        body, name=name,
        in_specs=[hbm] * (2 * n) + ([] if after is None else [pl.BlockSpec(memory_space=pl.ANY)]),
        out_specs=[sem, sem] + [hbm] * (2 * n) + [pl.BlockSpec(memory_space=pltpu.VMEM)],
        out_shape=[pltpu.SemaphoreType.DMA((nsem,)), pltpu.SemaphoreType.DMA((nsem,))]
        + [pltpu.HBM(a.shape, a.dtype) for a in arrays]
        + [pltpu.HBM(s, a.dtype) for s, a in zip(land_shapes, arrays)]
        + [jax.ShapeDtypeStruct((8, LANE), F32)],
        input_output_aliases={k: 2 + k for k in range(2 * n)},
        compiler_params=pltpu.CompilerParams(has_side_effects=pltpu.SideEffectType.DATAFLOW_SIDE_EFFECTING),
    )(*operands)
    return out[0], out[1], list(out[2:2 + n]), list(out[2 + n:2 + 2 * n]), out[-1]


def _exchange_wait(started, kinds, after, name, own=True, relations=ALL_PEERS, with_sources=False):
    send_sems, recv_sems, srcs, lands, _ = started
    n = len(srcs)
    hbm = pl.BlockSpec(memory_space=pltpu.HBM)
    sem = pl.BlockSpec(memory_space=pltpu.SEMAPHORE)

    def body(*refs):
        src_refs, land_refs = refs[:n], refs[n:2 * n]
        copies = _peer_copies(src_refs, land_refs, kinds, refs[2 * n], refs[2 * n + 1], relations)
        for cp in copies:
            cp.wait_send()
        for cp in copies:
            cp.wait_recv()
        for cp in _own_copies(src_refs, land_refs, kinds, refs[2 * n]) if own else []:
            cp.wait()

    out = _pcall(
        body, name=name,
        in_specs=[hbm] * (2 * n) + [sem, sem, pl.BlockSpec(memory_space=pl.ANY)],
        out_specs=[hbm] * (2 * n),
        out_shape=[pltpu.HBM(a.shape, a.dtype) for a in srcs + lands],
        input_output_aliases={k: k for k in range(2 * n)},
        compiler_params=pltpu.CompilerParams(has_side_effects=pltpu.SideEffectType.DATAFLOW_SIDE_EFFECTING),
    )(*srcs, *lands, send_sems, recv_sems, after)
    return (list(out[:n]), list(out[n:])) if with_sources else list(out[n:])


def _sum_slots(slots, name, rows_tile):
    nd, r, c = slots.shape

    def body(s_ref, o_ref):
        acc = s_ref[0].astype(F32)
        for p in range(1, nd):
            acc = acc + s_ref[p].astype(F32)
        o_ref[...] = acc

    return _pcall(
        body, name=name, grid=(r // rows_tile,),
        in_specs=[pl.BlockSpec((nd, rows_tile, c), lambda i: (0, i, 0))],
        out_specs=pl.BlockSpec((rows_tile, c), lambda i: (i, 0)),
        out_shape=jax.ShapeDtypeStruct((r, c), F32),
        compiler_params=_params("parallel"),
    )(slots)


def _sum_slots_small(slot_arrays, own_arrays, name):
    n = len(slot_arrays)

    def body(*refs):
        me = 4 * lax.axis_index("x") + 2 * lax.axis_index("y") + lax.axis_index("c")
        for s_ref, own_ref, o_ref in zip(refs[:n], refs[n:2 * n], refs[2 * n:]):
            acc = jnp.where(me == 0, own_ref[...], s_ref[0])
            for p in range(1, s_ref.shape[0]):
                acc = acc + jnp.where(me == p, own_ref[...], s_ref[p])
            o_ref[...] = acc

    return _pcall(body, name=name, out_shape=[jax.ShapeDtypeStruct(a.shape[1:], F32) for a in slot_arrays])(
        *slot_arrays, *own_arrays)


def _adamw_update(w_ref, g_ref, m_ref, v_ref, d_ref, nm_ref, nv_ref):
    gr = g_ref[...]
    nm = ADAM_B1 * m_ref[...] + (1.0 - ADAM_B1) * gr
    nv = ADAM_B2 * v_ref[...] + (1.0 - ADAM_B2) * (gr * gr)
    m_hat = nm / (1.0 - ADAM_B1 ** ADAM_STEP)
    v_hat = nv / (1.0 - ADAM_B2 ** ADAM_STEP)
    d_ref[...] = -ADAM_LR * (m_hat / (jnp.sqrt(v_hat) + ADAM_EPS) + ADAM_WD * w_ref[...])
    nm_ref[...] = nm
    nv_ref[...] = nv


def _adamw_small(ws, gs, ms, vs, name):
    n = len(ws)

    def body(*refs):
        ins, outs = refs[:4 * n], refs[4 * n:]
        for k in range(n):
            _adamw_update(ins[k], ins[n + k], ins[2 * n + k], ins[3 * n + k], outs[k], outs[n + k], outs[2 * n + k])

    shapes = [jax.ShapeDtypeStruct(w.shape, F32) for w in ws]
    out = _pcall(body, name=name, out_shape=shapes * 3)(*ws, *gs, *ms, *vs)
    return list(out[:n]), list(out[n:2 * n]), list(out[2 * n:])


def _adamw(w, g, m, v, name, rows_tile):
    r, c = w.shape
    body = lambda *refs: _adamw_update(*refs)
    spec = pl.BlockSpec((rows_tile, c), lambda i: (i, 0))
    shp = jax.ShapeDtypeStruct((r, c), F32)
    return _pcall(
        body, name=name, grid=(r // rows_tile,), in_specs=[spec] * 4, out_specs=[spec] * 3, out_shape=[shp] * 3,
        compiler_params=_params("parallel"),
    )(w, g, m, v)


F0 = 2 * RET_QK + 2 * RET_V


def _to_internal_rows(w_t):
    cols = w_t.shape[1]
    fox = w_t[F0:F0 + 3 * FOX_W].reshape(3, FOX_PAIRS, LANE, cols).transpose(1, 0, 2, 3).reshape(3 * FOX_W, cols)
    tail = jnp.zeros((IN_PAD - IN_WIDTH, cols), w_t.dtype)
    return jnp.concatenate([w_t[:F0], fox, w_t[F0 + 3 * FOX_W:], tail], axis=0)


def _from_internal_rows(g_t):
    cols = g_t.shape[1]
    fox = g_t[F0:F0 + 3 * FOX_W].reshape(FOX_PAIRS, 3, LANE, cols).transpose(1, 0, 2, 3).reshape(3 * FOX_W, cols)
    return jnp.concatenate([g_t[:F0], fox, g_t[F0 + 3 * FOX_W:F0 + 3 * FOX_W + FOX_HEADS]], axis=0)


IN_BLOCK = IN_WIDTH // N_DEV
IN_BLOCK_PAD = 400
BF16_ROWS = 16


def _slot_row_of_internal():
    rows = np.arange(IN_WIDTH)
    fox = rows[F0:F0 + 3 * FOX_W].reshape(3, FOX_PAIRS, LANE).transpose(1, 0, 2).reshape(-1)
    original = np.concatenate([rows[:F0], fox, rows[F0 + 3 * FOX_W:]])
    slot_rows = original // IN_BLOCK * IN_BLOCK_PAD + original % IN_BLOCK
    return np.concatenate([slot_rows, np.full(IN_PAD - IN_WIDTH, -1)])


def _internal_row_of_slot():
    forward = _slot_row_of_internal()
    back = np.full(N_DEV * IN_BLOCK_PAD, -1)
    back[forward[forward >= 0]] = np.nonzero(forward >= 0)[0]
    return back


def _row_runs(src_of_dst):
    tiles = []
    for t0 in range(0, len(src_of_dst), LANE):
        runs = []
        for o in range(LANE):
            s = int(src_of_dst[t0 + o])
            if s < 0:
                continue
            if runs and runs[-1][0] + runs[-1][2] == o and runs[-1][1] + runs[-1][2] == s:
                runs[-1][2] += 1
            else:
                runs.append([o, s, 1])
        tiles.append(runs)
    return tiles


def _move_rows(src, src_of_dst, name):
    n_src, cols = src.shape
    tiles = _row_runs(src_of_dst)

    def body(s_ref, o_ref):
        for t, runs in enumerate(tiles):
            rows = pl.ds(t * LANE, LANE)
            if not runs:
                o_ref[rows, :] = jnp.zeros((LANE, cols), o_ref.dtype)
                continue
            if len(runs) == 1 and runs[0][0] == 0 and runs[0][2] == LANE and runs[0][1] % BF16_ROWS == 0:
                o_ref[rows, :] = s_ref[pl.ds(runs[0][1], LANE), :]
                continue
            acc = None
            for o0, s0, n in runs:
                w0 = s0 // BF16_ROWS * BF16_ROWS
                width = -(-(s0 - w0 + n) // LANE) * LANE
                w0 = min(w0, n_src - width)
                i = lax.broadcasted_iota(jnp.int32, (LANE, width), 0)
                j = lax.broadcasted_iota(jnp.int32, (LANE, width), 1)
                pick = ((j - i == s0 - w0 - o0) & (i >= o0) & (i < o0 + n)).astype(src.dtype)
                part = _dot(pick, s_ref[pl.ds(w0, width), :])
                acc = part if acc is None else acc + part
            o_ref[rows, :] = acc.astype(o_ref.dtype)

    return _pcall(body, name=name, out_shape=jax.ShapeDtypeStruct((len(src_of_dst), cols), src.dtype))(src)


def _local_step(x, target, meta, attn_g, fox_b, ret_g, ffn_g, conv_w8, conv_b, final_g,
                first_weight, late_weights, ffn_grads_ready, out_grad_ready, in_grad_ready):
    seq, d = x.shape
    t = seq + PREFIX
    tm = TOK_TILE
    nq = t // tm
    fox_b128 = jnp.pad(fox_b, ((0, 0), (0, LANE - FOX_HEADS)))

    h0, n1 = _prep_norm(x, meta, attn_g, "prep_norm")
    w_in_t = first_weight(n1)
    proj = _mm_simple(n1, w_in_t, mode="nt", tm=tm, tn=IN_PAD, tk=d, out_dtype=F32, name="mm_in")
    cos, sin = _rope_tables(t)
    o_pre, mixed, states = _ret_fwd(proj, cos, sin, ret_g, "ret_fwd")
    c = _forget_cumsum(proj, fox_b128, "forget_cumsum")
    qa, ka, va, qt, vt = _fox_prep(proj, c, "fox_prep")
    by_block = lambda a: a.reshape(FOX_HEADS, nq, tm, LANE)
    mixed, o_fox, lse = _fox_fwd(qt, by_block(ka), vt, mixed, "fox_fwd")
    w_out, w_up_t, w_down = late_weights(o_fox)
    tile = pl.BlockSpec((tm, d), lambda i: (i, 0))
    row_vec = pl.BlockSpec((1, d), lambda i: (0, 0))
    resident = lambda shape: pl.BlockSpec(shape, lambda i: (0,) * len(shape), pipeline_mode=pl.Buffered(1))
    acts = lambda dtype: jax.ShapeDtypeStruct((t, d), dtype)
    vec = jax.ShapeDtypeStruct((1, d), F32)

    def residual_and_norm(i, acc, ins, outs):
        h = acc + ins[0][...]
        outs[0][...] = h
        outs[1][...] = (h * lax.rsqrt(jnp.mean(h * h, axis=-1, keepdims=True) + EPS) * ins[1][...]).astype(BF16)

    h1, n2 = _matmul_rows([mixed], [tile], [w_out], [resident((d, d))], [h0, ffn_g], [tile, row_vec],
                          [tile, tile], [acts(F32), acts(BF16)], residual_and_norm, mode="nn", steps=nq, name="mm_out_norm")
    nf = D_FF // 1408
    up, g = _up_conv_fwd(n2, w_up_t, conv_w8, conv_b, "up_conv_fwd")

    def residual_loss_bwd(i, acc, ins, outs):
        loss_ref, dh_ref, dhb_ref, gg_ref = outs
        part, dh, gg = _loss_tile(i, acc + ins[0][...], jnp.concatenate([ins[1][...], ins[2][...], ins[3][...]], axis=0),
                                  ins[4][...])
        _accumulate(loss_ref, i, jnp.broadcast_to(part, loss_ref.shape))
        dh_ref[...] = dh
        dhb_ref[...] = dh.astype(BF16)
        _accumulate(gg_ref, i, gg)

    loss_tile, dh2, dh2_b, g_final = _matmul_rows(
        [g], [pl.BlockSpec((tm, D_FF), lambda i: (i, 0))], [w_down], [resident((D_FF, d))],
        [h1, target, target, target, final_g], [tile] + _shifted_row_specs(d) + [row_vec],
        [pl.BlockSpec((8, LANE), lambda i: (0, 0)), tile, tile, row_vec],
        [jax.ShapeDtypeStruct((8, LANE), F32), acts(F32), acts(BF16), vec], residual_loss_bwd,
        mode="nn", steps=nq, name="mm_down_loss")

    tkw = 2112 if t % 2112 == 0 else tm
    gw_down = _mm_simple(g, dh2_b, mode="tn", tm=1408, tn=d, tk=tkw, out_dtype=BF16, name="mm_gw_down")
    dup, g_conv_w8, g_conv_b = _dg_conv_bwd(up, conv_w8, conv_b, dh2_b, w_down, "dg_conv_bwd")

    half = lambda p: pl.BlockSpec((None, tm, D_FF), lambda i: (p, i, 0))
    half_w = lambda p: pl.BlockSpec((None, D_FF, d), lambda i: (p, 0, 0), pipeline_mode=pl.Buffered(1))
    gw_up_t = _matmul(
        dup, n2, mode="tn", grid=(2 * nf, 1, t // tkw),
        a_spec=pl.BlockSpec((None, tkw, 1408), lambda i, j, k: (i // nf, k, i % nf)),
        b_spec=pl.BlockSpec((tkw, d), lambda i, j, k: (k, 0)),
        o_spec=pl.BlockSpec((1408, d), lambda i, j, k: (i, 0)),
        out_shape=jax.ShapeDtypeStruct((2 * D_FF, d), BF16), name="mm_gw_up")
    def norm_bwd_and_mixer_grad(i, acc, ins, outs):
        dh, gg = _rms_bwd_tile(acc, ins[0][...], ins[1][...], ins[2][...])
        outs[0][...] = dh
        _accumulate(outs[1], i, gg)
        outs[2][...] = _dot(dh.astype(BF16), ins[3][...], NT)

    dh1, g_ffn, dmixed = _matmul_rows(
        [dup, dup], [half(0), half(1)], [w_up_t, w_up_t], [half_w(0), half_w(1)],
        [h1, ffn_g, dh2, w_out], [tile, row_vec, tile, resident((d, d))], [tile, row_vec, tile],
        [acts(F32), vec, acts(F32)], norm_bwd_and_mixer_grad,
        mode="nn", steps=nq, name="mm_dn2_norm_bwd", after=ffn_grads_ready(gw_down, gw_up_t))
    gw_out = _mm_simple(mixed, dh1, mode="tn", tm=d, tn=d, tk=tkw, out_dtype=BF16, name="mm_gw_out")
    dproj, g_ret = _ret_bwd(proj, cos, sin, ret_g + out_grad_ready(gw_out), dmixed, o_pre, states, "ret_bwd")
    qab, doa = _fox_prep_bwd(dmixed, o_fox, lse, qa, "fox_prep_bwd")
    dproj, drs, dcs = _fox_bwd(by_block(qab), by_block(doa), by_block(ka), by_block(va), dproj, "fox_bwd")
    dproj, g_fox_b = _forget_cumsum_bwd(proj, fox_b128, drs, dcs, dproj, "forget_cumsum_bwd")
    gw_in_t = _mm_simple(dproj, n1, mode="tn", tm=640, tn=d, tk=tkw, out_dtype=BF16, name="mm_gw_in")
    sent = in_grad_ready(gw_in_t)
    def input_grads(i, acc, ins, outs):
        gx_ref, gmeta_ref, gg_ref, buf_ref, sems = outs
        dh, gg = _rms_bwd_tile(acc, ins[0][...], ins[1][...], ins[2][...])
        _accumulate(gg_ref, i, gg)
        slot = i % 2

        def first_copy():
            return pltpu.make_async_copy(buf_ref.at[0, pl.ds(PREFIX, tm - PREFIX)], gx_ref.at[pl.ds(0, tm - PREFIX)],
                                         sems.at[0])

        def tile_copy(tile, buf_slot):
            rows = pl.ds(pl.multiple_of(tile * tm - PREFIX, PREFIX), tm)
            return pltpu.make_async_copy(buf_ref.at[buf_slot], gx_ref.at[rows], sems.at[buf_slot])

        @pl.when(i == 1)
        def _():
            first_copy().wait()

        @pl.when(i >= 2)
        def _():
            tile_copy(i - 1, 1 - slot).wait()

        buf_ref[slot] = dh

        @pl.when(i == 0)
        def _():
            gmeta_ref[...] = dh[N_PAD:PREFIX, :]
            first_copy().start()

        @pl.when(i > 0)
        def _():
            tile_copy(i, slot).start()

        @pl.when(i == nq - 1)
        def _():
            tile_copy(i, slot).wait()

    grad_x, g_meta, g_attn = _matmul_rows(
        [dproj], [pl.BlockSpec((tm, IN_PAD), lambda i: (i, 0))], [w_in_t], [resident((IN_PAD, d))],
        [h0, attn_g, dh1], [tile, row_vec, tile],
        [pl.BlockSpec(memory_space=pl.ANY), pl.BlockSpec((N_META, d), lambda i: (0, 0)), row_vec],
        [jax.ShapeDtypeStruct((seq, d), F32), jax.ShapeDtypeStruct((N_META, d), F32), vec], input_grads,
        mode="nn", steps=nq, name="mm_dn1_norm_bwd", after=sent,
        scratch=[pltpu.VMEM((2, tm, d), F32), pltpu.SemaphoreType.DMA((2,))])

    grads = dict(meta=g_meta, attn_g=g_attn, fox_b=g_fox_b, ret_g=g_ret,
                 ffn_g=g_ffn, conv_w=g_conv_w8, conv_b=g_conv_b, final_g=g_final)
    return loss_tile, grad_x, grads


def kernel(x, meta_tokens, attn_norm_g, w_in, fox_forget_b, ret_norm_g, w_out, ffn_norm_g, w_up, conv_w, conv_b, w_down, final_norm_g, loss_target, m_meta_tokens, m_attn_norm_g, m_w_in, m_fox_forget_b, m_ret_norm_g, m_w_out, m_ffn_norm_g, m_w_up, m_conv_w, m_conv_b, m_w_down, m_final_norm_g, v_meta_tokens, v_attn_norm_g, v_w_in, v_fox_forget_b, v_ret_norm_g, v_w_out, v_ffn_norm_g, v_w_up, v_conv_w, v_conv_b, v_w_down, v_final_norm_g):
    d = D_MODEL
    me = 4 * lax.axis_index("x") + 2 * lax.axis_index("y") + lax.axis_index("c")
    in_blk, in_blk_pad = IN_BLOCK, IN_BLOCK_PAD
    up_blk = 2 * D_FF // N_DEV
    down_blk = D_FF // N_DEV
    cw_blk = D_FF // N_DEV

    w_in_loc = jnp.pad(w_in[0].T.astype(BF16), ((0, in_blk_pad - in_blk), (0, 0)))
    cw_loc = jnp.pad(conv_w[0], ((0, 5), (0, 384 - cw_blk)))
    g_meta, g_cw = _exchange([meta_tokens, cw_loc], ["gather"] * 2, "gather_small")
    first = _exchange_start([w_in_loc], ["gather"], "gather_in_start", after=g_meta, relations=SAME_CORE_AND_SIBLING)
    rest_loc = [(w_out[0] + first[-1][0:1, 0:1]).astype(BF16), w_up[0].T.astype(BF16), w_down[0].astype(BF16)]
    rest = _exchange_start(rest_loc, ["gather"] * 3, "gather_rest_start")
    meta_f = g_meta.transpose(1, 0, 2).reshape(N_META, d)
    conv_w8 = jnp.pad(g_cw[:, :3, :cw_blk].transpose(1, 0, 2).reshape(3, D_FF), ((0, 5), (0, 0)))
    pending = {}

    def first_weight(after):
        own_in, landed = _exchange_wait(first, ["gather"], after, "gather_in_wait", relations=SAME_CORE_AND_SIBLING,
                                        with_sources=True)
        onward = _exchange_start(own_in, ["forward"], "gather_in_forward_start", relations=OTHER_CHIPS, lands=landed,
                                 own=False)
        (g_in,) = _exchange_wait(onward, ["forward"], onward[-1], "gather_in_forward_wait", own=False,
                                 relations=OTHER_CHIPS)
        return _move_rows(g_in.reshape(IN_PAD, d), _slot_row_of_internal(), "w_in_rows")

    def in_grad_ready(gw_in_t):
        blocks = _move_rows(gw_in_t, _internal_row_of_slot(), "gw_in_rows").reshape(N_DEV, in_blk_pad, d)
        pending["in"] = _exchange_start([blocks], ["scatter"], "grads_in_start")
        return pending["in"][-1][0:1, 0:1]

    def late_weights(after):
        g_out, g_up, g_down = _exchange_wait(rest, ["gather"] * 3, after, "gather_rest_wait")
        return g_out.reshape(d, d), g_up.reshape(2, D_FF, d), g_down.reshape(D_FF, d)

    def ffn_grads_ready(gw_down, gw_up_t):
        blocks = [gw_down.reshape(N_DEV, down_blk, d), gw_up_t.reshape(N_DEV, up_blk, d)]
        pending["ffn"] = _exchange_start(blocks, ["scatter"] * 2, "grads_ffn_start")
        return pending["ffn"][-1][0:1, 0:1]

    def out_grad_ready(gw_out):
        pending["out"] = _exchange_start([gw_out.reshape(N_DEV, d // N_DEV, d)], ["scatter"], "grads_out_start")
        return pending["out"][-1][0:1, 0:1]

    loss_tile, grad_x, gr = _local_step(
        x[0], loss_target[0], meta_f, attn_norm_g + rest[-1][0:1, 0:1], fox_forget_b, ret_norm_g, ffn_norm_g,
        conv_w8, conv_b, final_norm_g.reshape(1, d), first_weight, late_weights, ffn_grads_ready, out_grad_ready,
        in_grad_ready)

    small = [loss_tile, gr["attn_g"], gr["fox_b"], gr["ret_g"], gr["ffn_g"], gr["conv_b"], gr["final_g"],
             gr["meta"], gr["conv_w"]]
    small_kinds = ["gather"] * len(small)
    small_started = _exchange_start(small, small_kinds, "grads_small_start", own=False)

    r_down, r_up = _exchange_wait(pending["ffn"], ["scatter"] * 2, small_started[-1], "grads_ffn_wait")
    (r_out,) = _exchange_wait(pending["out"], ["scatter"], small_started[-1], "grads_out_wait")
    g_w_out = _sum_slots(r_out, "sum_w_out", d // N_DEV)
    g_w_up_t = _sum_slots(r_up, "sum_w_up", up_blk)
    g_w_down = _sum_slots(r_down, "sum_w_down", down_blk)
    as_t = lambda a: a[0].T
    from_t = lambda a: a.T[None]
    d_w_out, m_w_out_n, v_w_out_n = [a[None] for a in _adamw(w_out[0], g_w_out, m_w_out[0], v_w_out[0], "adamw_w_out", 128)]
    up_t = _adamw(as_t(w_up), g_w_up_t, as_t(m_w_up), as_t(v_w_up), "adamw_w_up", up_blk // 2)
    d_w_up, m_w_up_n, v_w_up_n = [from_t(a) for a in up_t]
    d_w_down, m_w_down_n, v_w_down_n = [a[None] for a in _adamw(w_down[0], g_w_down, m_w_down[0], v_w_down[0],
                                                                "adamw_w_down", down_blk)]

    own_small, r_small = _exchange_wait(small_started, small_kinds, up_t[0], "grads_small_wait", own=False,
                                        with_sources=True)
    (loss_all, g_attn, g_fox_b128, g_ret, g_ffn, g_conv_b, g_final, g_meta_full, g_cw_full) = _sum_slots_small(
        r_small, own_small, "sum_small")
    loss = loss_all[0, 0]
    g_fox_b = g_fox_b128[:, :FOX_HEADS]
    g_meta_loc = lax.dynamic_slice(g_meta_full, (0, me * (d // N_DEV)), (N_META, d // N_DEV))
    g_cw_loc = lax.dynamic_slice(g_cw_full, (0, me * cw_blk), (3, cw_blk))

    (r_in,) = _exchange_wait(pending["in"], ["scatter"], r_small[0], "grads_in_wait")
    g_w_in_t = _sum_slots(r_in, "sum_w_in", in_blk_pad)[:in_blk]
    d_w_in, m_w_in_n, v_w_in_n = [from_t(a) for a in _adamw(as_t(w_in), g_w_in_t, as_t(m_w_in), as_t(v_w_in),
                                                            "adamw_w_in", in_blk)]
    g_w_in, g_w_up = g_w_in_t.T, g_w_up_t.T
    row = lambda a: a.reshape(1, d)
    sm_grads = [g_meta_loc, g_attn, g_fox_b, g_ret, g_ffn, g_cw_loc, g_conv_b, g_final]
    sm_w = [meta_tokens, attn_norm_g, fox_forget_b, ret_norm_g, ffn_norm_g, conv_w[0], conv_b, row(final_norm_g)]
    sm_m = [m_meta_tokens, m_attn_norm_g, m_fox_forget_b, m_ret_norm_g, m_ffn_norm_g, m_conv_w[0], m_conv_b,
            row(m_final_norm_g)]
    sm_v = [v_meta_tokens, v_attn_norm_g, v_fox_forget_b, v_ret_norm_g, v_ffn_norm_g, v_conv_w[0], v_conv_b,
            row(v_final_norm_g)]
    dl, ml, vl = [lst[:7] + [lst[7].reshape(d)] for lst in _adamw_small(sm_w, sm_grads, sm_m, sm_v, "adamw_small")]

    def by_weight(meta_, attn_, w_in_, fox_, ret_, w_out_, ffn_, w_up_, cw_, cb_, w_down_, final_):
        return (meta_, attn_, w_in_, fox_, ret_, w_out_, ffn_, w_up_, cw_[None], cb_, w_down_, final_)

    grads_out = by_weight(g_meta_loc, g_attn, g_w_in[None], g_fox_b, g_ret, g_w_out[None], g_ffn, g_w_up[None], g_cw_loc,
                          g_conv_b, g_w_down[None], g_final.reshape(d))
    delta_out = by_weight(dl[0], dl[1], d_w_in, dl[2], dl[3], d_w_out, dl[4], d_w_up, dl[5], dl[6], d_w_down, dl[7])
    m_out = by_weight(ml[0], ml[1], m_w_in_n, ml[2], ml[3], m_w_out_n, ml[4], m_w_up_n, ml[5], ml[6], m_w_down_n, ml[7])
    v_out = by_weight(vl[0], vl[1], v_w_in_n, vl[2], vl[3], v_w_out_n, vl[4], v_w_up_n, vl[5], vl[6], v_w_down_n, vl[7])
    return (loss, grad_x[None]) + grads_out + delta_out + m_out + v_out
```

```python
import numpy as np
import jax
import jax.numpy as jnp
from jax import lax
from jax.experimental import pallas as pl
from jax.experimental.pallas import tpu as pltpu

F32 = jnp.float32
BF16 = jnp.bfloat16

D_MODEL = 1024
N_META = 16
N_PAD = 112
PREFIX = 128
RET_HEADS = 4
RET_DK = 64
RET_DV = 128
FOX_HEADS = 8
FOX_DH = 64
D_FF = 2816
ROPE_BASE = 10000.0
EPS = 1e-6
NEG = -1e30
RET_QK = RET_HEADS * RET_DK
RET_V = RET_HEADS * RET_DV
FOX_W = FOX_HEADS * FOX_DH
IN_WIDTH = 2 * RET_QK + 2 * RET_V + 3 * FOX_W + FOX_HEADS
IN_PAD = 3200
FF_COL_BLOCK = (IN_WIDTH - FOX_HEADS) // 128
QK_SCALE = 0.125

ADAM_LR = 0.001
ADAM_B1 = 0.9
ADAM_B2 = 0.999
ADAM_EPS = 1e-08
ADAM_WD = 0.01
ADAM_STEP = 10

N_DEV = 8
LANE = 128
ROW_TILE = 128
TOK_TILE = 384

NN = (((1,), (0,)), ((), ()))
NT = (((1,), (1,)), ((), ()))
TN = (((0,), (0,)), ((), ()))


def _pcall(body, **kw):
    return pl.pallas_call(body, **kw)


def _params(*sem):
    return pltpu.CompilerParams(dimension_semantics=sem)


def _dot(a, b, dims=NN):
    return lax.dot_general(a, b, dims, preferred_element_type=F32)


def _sigmoid(x):
    return 0.5 * jnp.tanh(0.5 * x) + 0.5


def _matmul(a, b, *, mode, grid, a_spec, b_spec, o_spec, out_shape, name, add=None, add_spec=None, after=None):
    dims = {"nn": NN, "nt": NT, "tn": TN}[mode]
    nk = grid[2]
    has_add = add is not None
    a_list, b_list = (list(a), list(b)) if isinstance(a, (list, tuple)) else ([a], [b])
    a_specs, b_specs = (list(a_spec), list(b_spec)) if isinstance(a_spec, (list, tuple)) else ([a_spec], [b_spec])
    nt = len(a_list)
    n_in = 2 * nt + int(has_add) + int(after is not None)

    def body(*refs):
        a_refs, b_refs = refs[:nt], refs[nt:2 * nt]
        add_ref = refs[2 * nt] if has_add else None
        o_ref = refs[n_in]
        part = _dot(a_refs[0][...].astype(BF16), b_refs[0][...].astype(BF16), dims)
        for ar, br in zip(a_refs[1:], b_refs[1:]):
            part = part + _dot(ar[...].astype(BF16), br[...].astype(BF16), dims)

        def finish(acc):
            if has_add:
                acc = acc + add_ref[...]
            o_ref[...] = acc.astype(o_ref.dtype)

        if nk == 1:
            finish(part)
        else:
            acc_ref = refs[-1]
            k = pl.program_id(2)

            @pl.when(k == 0)
            def _():
                acc_ref[...] = part

            @pl.when(k > 0)
            def _():
                acc_ref[...] += part

            @pl.when(k == nk - 1)
            def _():
                finish(acc_ref[...])

    in_specs = a_specs + b_specs + ([add_spec] if has_add else [])
    args = tuple(a_list) + tuple(b_list) + ((add,) if has_add else ())
    if after is not None:
        in_specs, args = in_specs + [pl.BlockSpec(memory_space=pl.ANY)], args + (after,)
    scratch = [] if nk == 1 else [pltpu.VMEM(tuple(d for d in o_spec.block_shape if d is not None), F32)]
    return _pcall(
        body, name=name, grid=grid, in_specs=in_specs, out_specs=o_spec, out_shape=out_shape,
        scratch_shapes=scratch, compiler_params=_params("parallel", "parallel", "arbitrary"),
    )(*args)


def _mm_simple(a, b, *, mode, tm, tn, tk, out_dtype, name, add=None, after=None):
    if mode == "tn":
        K, M = a.shape
    else:
        M, K = a.shape
    N = b.shape[0] if mode == "nt" else b.shape[1]
    grid = (M // tm, N // tn, K // tk)
    resident = dict(pipeline_mode=pl.Buffered(1)) if (tn == N and tk == K) else {}
    a_spec = pl.BlockSpec((tk, tm), lambda i, j, k: (k, i)) if mode == "tn" else pl.BlockSpec((tm, tk), lambda i, j, k: (i, k))
    b_spec = (pl.BlockSpec((tn, tk), lambda i, j, k: (j, k), **resident) if mode == "nt"
              else pl.BlockSpec((tk, tn), lambda i, j, k: (k, j), **resident))
    o_spec = pl.BlockSpec((tm, tn), lambda i, j, k: (i, j))
    return _matmul(a, b, mode=mode, grid=grid, a_spec=a_spec, b_spec=b_spec, o_spec=o_spec,
                   out_shape=jax.ShapeDtypeStruct((M, N), out_dtype), name=name, add=add,
                   add_spec=o_spec if add is not None else None, after=after)


def _matmul_rows(a_list, a_specs, b_list, b_specs, extras, extra_specs, out_specs, out_shape, epilogue, *,
                 mode, steps, name, after=None, scratch=()):
    dims = {"nn": NN, "nt": NT}[mode]
    nt, ne = len(a_list), len(extras)
    n_in = 2 * nt + ne + int(after is not None)

    def body(*refs):
        acc = _dot(refs[0][...].astype(BF16), refs[nt][...].astype(BF16), dims)
        for k in range(1, nt):
            acc = acc + _dot(refs[k][...].astype(BF16), refs[nt + k][...].astype(BF16), dims)
        epilogue(pl.program_id(0), acc, refs[2 * nt:2 * nt + ne], refs[n_in:])

    in_specs = list(a_specs) + list(b_specs) + list(extra_specs)
    args = tuple(a_list) + tuple(b_list) + tuple(extras)
    if after is not None:
        in_specs, args = in_specs + [pl.BlockSpec(memory_space=pl.ANY)], args + (after,)
    return _pcall(body, name=name, grid=(steps,), in_specs=in_specs, out_specs=out_specs, out_shape=out_shape,
                  scratch_shapes=list(scratch), compiler_params=_params("arbitrary"))(*args)


def _rms_bwd_tile(dy, x, gain, dres):
    r = lax.rsqrt(jnp.mean(x * x, axis=-1, keepdims=True) + EPS)
    xhat = x * r
    u = dy * gain
    return dres + r * (u - xhat * jnp.mean(u * xhat, axis=-1, keepdims=True)), jnp.sum(dy * xhat, axis=0, keepdims=True)


def _loss_tile(i, x, tgt, gain):
    d = x.shape[-1]
    r = lax.rsqrt(jnp.mean(x * x, axis=-1, keepdims=True) + EPS)
    xhat = x * r
    counted = (i * TOK_TILE + lax.broadcasted_iota(jnp.int32, (TOK_TILE, 1), 0)) >= PREFIX
    err = jnp.where(counted, xhat * gain - tgt, 0.0)
    dy = err * (1.0 / d)
    u = dy * gain
    dh = r * (u - xhat * jnp.mean(u * xhat, axis=-1, keepdims=True))
    return 0.5 * jnp.sum(jnp.mean(err * err, axis=-1, keepdims=True)), dh, jnp.sum(dy * xhat, axis=0, keepdims=True)


def _accumulate(ref, i, part):
    @pl.when(i == 0)
    def _():
        ref[...] = part

    @pl.when(i > 0)
    def _():
        ref[...] += part


def _prep_norm(x, meta, gain, name):
    seq, d = x.shape
    t = seq + PREFIX

    def body(xa_ref, xb_ref, xc_ref, meta_ref, g_ref, h_ref, n_ref):
        i = pl.program_id(0)

        @pl.when(i == 0)
        def _():
            h_ref[0:N_PAD, :] = jnp.zeros((N_PAD, d), F32)
            h_ref[N_PAD:ROW_TILE, :] = meta_ref[...]

        @pl.when(i > 0)
        def _():
            h_ref[0:ROW_TILE, :] = xa_ref[...]

        h_ref[ROW_TILE:2 * ROW_TILE, :] = xb_ref[...]
        h_ref[2 * ROW_TILE:3 * ROW_TILE, :] = xc_ref[...]
        h = h_ref[...]
        r = lax.rsqrt(jnp.mean(h * h, axis=-1, keepdims=True) + EPS)
        n_ref[...] = (h * r * g_ref[...]).astype(BF16)

    return _pcall(
        body, name=name, grid=(t // TOK_TILE,),
        in_specs=_shifted_row_specs(d) + [pl.BlockSpec((N_META, d), lambda i: (0, 0)), pl.BlockSpec((1, d), lambda i: (0, 0))],
        out_specs=[pl.BlockSpec((TOK_TILE, d), lambda i: (i, 0)), pl.BlockSpec((TOK_TILE, d), lambda i: (i, 0))],
        out_shape=[jax.ShapeDtypeStruct((t, d), F32), jax.ShapeDtypeStruct((t, d), BF16)],
        compiler_params=_params("parallel"),
    )(x, x, x, meta, gain)


def _shifted_row_specs(d):
    blocks_per_tile = TOK_TILE // ROW_TILE
    return [pl.BlockSpec((ROW_TILE, d), lambda i, r=r: (jnp.maximum(blocks_per_tile * i + r, 0), 0)) for r in (-1, 0, 1)]


def _ret_consts(bk):
    gam = 1.0 - 2.0 ** (-5.0 - np.arange(RET_HEADS))
    n = np.arange(bk)
    same_or_earlier_chunk = (n[None, :] // 64) <= (n[:, None] // 64)
    w = gam[:, None, None] ** np.abs(n[:, None] - n[None, :])[None] * same_or_earlier_chunk[None]
    wq = gam[:, None] ** (n[None, :] + 1.0)
    wk = gam[:, None] ** (bk - 1.0 - n[None, :])
    mask = (np.arange(RET_QK)[None, :] // RET_DK) == np.arange(RET_HEADS)[:, None]
    return (jnp.asarray(w, F32), jnp.asarray(wq[:, :, None], F32), jnp.asarray(wk[:, :, None], F32),
            jnp.asarray(mask[:, None, :], F32), [float(g ** bk) for g in gam])


def _rope_tables(t):
    half = RET_DK // 2
    inv = 1.0 / (ROPE_BASE ** (jnp.arange(half, dtype=F32) / half))
    ang = jnp.arange(t).astype(F32)[:, None] * inv[None, :]
    cos, sin = jnp.cos(ang), jnp.sin(ang)
    return (jnp.tile(jnp.concatenate([cos, cos], axis=1), (1, RET_HEADS)),
            jnp.tile(jnp.concatenate([-sin, sin], axis=1), (1, RET_HEADS)))


def _swap_halves(x):
    outs = []
    for s in range(x.shape[1] // LANE):
        xs = x[:, LANE * s:LANE * (s + 1)]
        lane = lax.broadcasted_iota(jnp.int32, xs.shape, 1)
        outs.append(jnp.where((lane & 32) == 0, pltpu.roll(xs, LANE - 32, axis=1), pltpu.roll(xs, 32, axis=1)))
    return outs[0] if len(outs) == 1 else jnp.concatenate(outs, axis=1)


def _rope(x, cos, sin_signed):
    return x * cos + _swap_halves(x) * sin_signed


def _rope_t(dx, cos, sin_signed):
    return dx * cos + _swap_halves(dx * sin_signed)


def _ret_fwd(proj, cos, sin, gain, name):
    t = proj.shape[0]
    bk = TOK_TILE
    nb = t // bk
    w, wq, wk, mask, g_blk = _ret_consts(bk)

    def body(q_ref, k_ref, v_ref, rg_ref, cos_ref, sin_ref, w_ref, wq_ref, wk_ref, mask_ref, gain_ref,
             opre_ref, og_ref, st_ref, r_ref):
        i = pl.program_id(0)

        @pl.when(i == 0)
        def _():
            r_ref[...] = jnp.zeros_like(r_ref)

        c, s = cos_ref[...], sin_ref[...]
        valid = ((i * bk + lax.broadcasted_iota(jnp.int32, (bk, 1), 0)) >= N_PAD).astype(F32)
        qr = _rope(q_ref[...], c, s)
        kr = _rope(k_ref[...], c, s) * QK_SCALE * valid
        kb = kr.astype(BF16)
        for h in range(RET_HEADS):
            hm = mask_ref[h]
            cols = slice(RET_DV * h, RET_DV * (h + 1))
            vh = v_ref[:, cols].astype(BF16)
            r_prev = r_ref[h]
            st_ref[0, h] = r_prev
            sm = _dot((qr * hm).astype(BF16), kb, NT) * w_ref[h]
            o = _dot(sm.astype(BF16), vh) + _dot((qr * (hm * wq_ref[h])).astype(BF16), r_prev.astype(BF16))
            r_ref[h] = g_blk[h] * r_prev + _dot((kr * wk_ref[h]).astype(BF16), vh, TN)
            opre_ref[:, cols] = o
            rstd = lax.rsqrt(jnp.mean(o * o, axis=-1, keepdims=True) + EPS)
            rg = rg_ref[:, cols]
            og_ref[:, cols] = (o * rstd * gain_ref[:, cols] * (rg * _sigmoid(rg))).astype(BF16)

    full = lambda shape: pl.BlockSpec(shape, lambda i: (0,) * len(shape))
    return _pcall(
        body, name=name, grid=(nb,),
        in_specs=[pl.BlockSpec((bk, RET_QK), lambda i: (i, 0)), pl.BlockSpec((bk, RET_QK), lambda i: (i, 1)),
                  pl.BlockSpec((bk, RET_V), lambda i: (i, 1)), pl.BlockSpec((bk, RET_V), lambda i: (i, 2)),
                  pl.BlockSpec((bk, RET_QK), lambda i: (i, 0)), pl.BlockSpec((bk, RET_QK), lambda i: (i, 0)),
                  full((RET_HEADS, bk, bk)), full((RET_HEADS, bk, 1)), full((RET_HEADS, bk, 1)),
                  full((RET_HEADS, 1, RET_QK)), full((1, RET_V))],
        out_specs=[pl.BlockSpec((bk, RET_V), lambda i: (i, 0)), pl.BlockSpec((bk, RET_V), lambda i: (i, 0)),
                   pl.BlockSpec((1, RET_HEADS, RET_QK, RET_DV), lambda i: (i, 0, 0, 0))],
        out_shape=[jax.ShapeDtypeStruct((t, RET_V), F32), jax.ShapeDtypeStruct((t, RET_V + FOX_W), BF16),
                   jax.ShapeDtypeStruct((nb, RET_HEADS, RET_QK, RET_DV), F32)],
        scratch_shapes=[pltpu.VMEM((RET_HEADS, RET_QK, RET_DV), F32)],
        compiler_params=_params("arbitrary"),
    )(proj, proj, proj, proj, cos, sin, w, wq, wk, mask, gain)


def _ret_bwd(proj, cos, sin, gain, dmixed, opre, states, name):
    t = proj.shape[0]
    bk = TOK_TILE
    nb = t // bk
    w, wq, wk, mask, g_blk = _ret_consts(bk)
    v0, g0 = 2 * RET_QK, 2 * RET_QK + RET_V

    def body(q_ref, k_ref, v_ref, rg_ref, cos_ref, sin_ref, w_ref, wq_ref, wk_ref, mask_ref, gain_ref,
             dog_ref, opre_ref, st_ref, dp_ref, gg_ref, dr_ref):
        step = pl.program_id(0)
        i = nb - 1 - step

        @pl.when(step == 0)
        def _():
            dr_ref[...] = jnp.zeros_like(dr_ref)
            gg_ref[...] = jnp.zeros_like(gg_ref)

        c, s = cos_ref[...], sin_ref[...]
        valid = ((i * bk + lax.broadcasted_iota(jnp.int32, (bk, 1), 0)) >= N_PAD).astype(F32)
        qr = _rope(q_ref[...], c, s)
        kr = _rope(k_ref[...], c, s) * QK_SCALE * valid
        kb = kr.astype(BF16)
        dqr = jnp.zeros((bk, RET_QK), F32)
        dkr = jnp.zeros((bk, RET_QK), F32)
        for h in range(RET_HEADS):
            hm = mask_ref[h]
            cols = slice(RET_DV * h, RET_DV * (h + 1))
            vh = v_ref[:, cols].astype(BF16)
            o = opre_ref[:, cols]
            rstd = lax.rsqrt(jnp.mean(o * o, axis=-1, keepdims=True) + EPS)
            xhat = o * rstd
            rg = rg_ref[:, cols]
            sg = _sigmoid(rg)
            gate = rg * sg
            gn = gain_ref[:, cols]
            dog = dog_ref[:, cols]
            dp_ref[:, g0 + RET_DV * h:g0 + RET_DV * (h + 1)] = (
                dog * xhat * gn * (sg * (1.0 + rg * (1.0 - sg)))).astype(BF16)
            gg_ref[:, cols] += jnp.sum(dog * xhat * gate, axis=0, keepdims=True)
            dxh = dog * gn * gate
            do = (rstd * (dxh - xhat * jnp.mean(dxh * xhat, axis=-1, keepdims=True))).astype(BF16)
            qm = (qr * hm).astype(BF16)
            qw = (qr * (hm * wq_ref[h])).astype(BF16)
            kw = (kr * wk_ref[h]).astype(BF16)
            wh = w_ref[h]
            sm = (_dot(qm, kb, NT) * wh).astype(BF16)
            ds = (_dot(do, vh, NT) * wh).astype(BF16)
            dr = dr_ref[h]
            drb = dr.astype(BF16)
            dp_ref[:, v0 + RET_DV * h:v0 + RET_DV * (h + 1)] = (_dot(sm, do, TN) + _dot(kw, drb)).astype(BF16)
            dqr = dqr + _dot(ds, kb) * hm + _dot(do, st_ref[0, h].astype(BF16), NT) * (hm * wq_ref[h])
            dkr = dkr + _dot(ds, qm, TN) + _dot(vh, drb, NT) * wk_ref[h]
            dr_ref[h] = g_blk[h] * dr + _dot(qw, do, TN)
        dp_ref[:, 0:RET_QK] = _rope_t(dqr, c, s).astype(BF16)
        dp_ref[:, RET_QK:2 * RET_QK] = _rope_t(dkr * (QK_SCALE * valid), c, s).astype(BF16)

    full = lambda shape: pl.BlockSpec(shape, lambda i: (0,) * len(shape))
    rev = lambda col: (lambda i: (nb - 1 - i, col))
    return _pcall(
        body, name=name, grid=(nb,),
        in_specs=[pl.BlockSpec((bk, RET_QK), rev(0)), pl.BlockSpec((bk, RET_QK), rev(1)),
                  pl.BlockSpec((bk, RET_V), rev(1)), pl.BlockSpec((bk, RET_V), rev(2)),
                  pl.BlockSpec((bk, RET_QK), rev(0)), pl.BlockSpec((bk, RET_QK), rev(0)),
                  full((RET_HEADS, bk, bk)), full((RET_HEADS, bk, 1)), full((RET_HEADS, bk, 1)),
                  full((RET_HEADS, 1, RET_QK)), full((1, RET_V)),
                  pl.BlockSpec((bk, RET_V), rev(0)), pl.BlockSpec((bk, RET_V), rev(0)),
                  pl.BlockSpec((1, RET_HEADS, RET_QK, RET_DV), lambda i: (nb - 1 - i, 0, 0, 0))],
        out_specs=[pl.BlockSpec((bk, g0 + RET_V), rev(0)), pl.BlockSpec((1, RET_V), lambda i: (0, 0))],
        out_shape=[jax.ShapeDtypeStruct((t, IN_PAD), BF16), jax.ShapeDtypeStruct((1, RET_V), F32)],
        scratch_shapes=[pltpu.VMEM((RET_HEADS, RET_QK, RET_DV), F32)],
        compiler_params=_params("arbitrary"),
    )(proj, proj, proj, proj, cos, sin, w, wq, wk, mask, gain, dmixed, opre, states)


def _forget_cumsum(proj, bias, name):
    t = proj.shape[0]
    rt = TOK_TILE
    nb = t // rt
    tril = jnp.asarray(np.tril(np.ones((rt, rt))), F32)

    def body(z_ref, b_ref, tril_ref, c_ref, carry_ref):
        i = pl.program_id(0)

        @pl.when(i == 0)
        def _():
            carry_ref[...] = jnp.zeros_like(carry_ref)

        z = z_ref[...] + b_ref[...]
        logf = jnp.minimum(z, 0.0) - jnp.log(1.0 + jnp.exp(-jnp.abs(z)))
        c = lax.dot_general(tril_ref[...], logf, NN, precision=lax.Precision.HIGHEST,
                            preferred_element_type=F32) + carry_ref[...]
        c_ref[...] = c
        carry_ref[...] = c[rt - 1:rt, :]

    return _pcall(
        body, name=name, grid=(nb,),
        in_specs=[pl.BlockSpec((rt, LANE), lambda i: (i, FF_COL_BLOCK)), pl.BlockSpec((1, LANE), lambda i: (0, 0)),
                  pl.BlockSpec((rt, rt), lambda i: (0, 0))],
        out_specs=pl.BlockSpec((rt, LANE), lambda i: (i, 0)),
        out_shape=jax.ShapeDtypeStruct((t, LANE), F32),
        scratch_shapes=[pltpu.VMEM((1, LANE), F32)],
        compiler_params=_params("arbitrary"),
    )(proj, bias, tril)


def _forget_cumsum_bwd(proj, bias, drs, dcs, dproj, name):
    t = proj.shape[0]
    rt = TOK_TILE
    nb = t // rt
    triu = jnp.asarray(np.triu(np.ones((rt, rt))), F32)

    def body(z_ref, b_ref, triu_ref, drs_ref, dcs_ref, dproj_in, dz_ref, gb_ref, carry_ref):
        step = pl.program_id(0)

        @pl.when(step == 0)
        def _():
            carry_ref[...] = jnp.zeros_like(carry_ref)
            gb_ref[...] = jnp.zeros_like(gb_ref)

        dlogf = lax.dot_general(triu_ref[...], drs_ref[...] - dcs_ref[...], NN, precision=lax.Precision.HIGHEST,
                                preferred_element_type=F32) + carry_ref[...]
        carry_ref[...] = dlogf[0:1, :]
        z = z_ref[...] + b_ref[...]
        is_head = lax.broadcasted_iota(jnp.int32, (rt, LANE), 1) < FOX_HEADS
        dz = jnp.where(is_head, dlogf / (1.0 + jnp.exp(z)), 0.0)
        dz_ref[...] = dz.astype(BF16)
        gb_ref[...] += jnp.sum(dz, axis=0, keepdims=True)

    return _pcall(
        body, name=name, grid=(nb,),
        in_specs=[pl.BlockSpec((rt, LANE), lambda i: (nb - 1 - i, FF_COL_BLOCK)),
                  pl.BlockSpec((1, LANE), lambda i: (0, 0)),
                  pl.BlockSpec((rt, rt), lambda i: (0, 0)),
                  pl.BlockSpec((rt, LANE), lambda i: (nb - 1 - i, 0)),
                  pl.BlockSpec((rt, LANE), lambda i: (nb - 1 - i, 0)),
                  pl.BlockSpec(memory_space=pl.ANY)],
        out_specs=[pl.BlockSpec((rt, LANE), lambda i: (nb - 1 - i, FF_COL_BLOCK)),
                   pl.BlockSpec((1, LANE), lambda i: (0, 0))],
        out_shape=[jax.ShapeDtypeStruct(dproj.shape, BF16), jax.ShapeDtypeStruct((1, LANE), F32)],
        input_output_aliases={5: 0},
        scratch_shapes=[pltpu.VMEM((1, LANE), F32)],
        compiler_params=_params("arbitrary"),
    )(proj, bias, triu, drs, dcs, dproj)


FOX_PAIRS = FOX_HEADS // 2
L_ONE_Q = FOX_DH
L_ONE_K = FOX_DH + 3
L_LSE = FOX_DH + 4


def _split3(x):
    hi = x.astype(BF16).astype(F32)
    r = x - hi
    mid = r.astype(BF16).astype(F32)
    return hi, mid, r - mid


def _head_to_low(slab, e):
    return slab if e == 0 else pltpu.roll(slab, FOX_DH, axis=1)


def _pair(a, b, low):
    return jnp.where(low, a, pltpu.roll(b, FOX_DH, axis=1))


def _fox_prep(proj, c, name):
    t = proj.shape[0]
    tq = TOK_TILE

    def body(p_ref, c_ref, qa_ref, ka_ref, va_ref, qt_ref, vt_ref):
        i = pl.program_id(0)
        lane = lax.broadcasted_iota(jnp.int32, (tq, LANE), 1)
        low = lane < FOX_DH
        live = (i * tq + lax.broadcasted_iota(jnp.int32, (tq, 1), 0)) >= N_PAD
        q_tail = jnp.where(lane < L_ONE_Q + 3, 1.0, 0.0)
        k_ones = (lane >= L_ONE_K) & (lane < L_ONE_K + 4)
        v_tail = jnp.where(lane < FOX_DH + 2, 1.0, 0.0)
        bias_parts = _split3(jnp.where(live, -c_ref[...], NEG))
        for pair in range(FOX_PAIRS):
            base = 3 * LANE * pair
            for e in range(2):
                h = 2 * pair + e
                q = _head_to_low(p_ref[:, base:base + LANE], e)
                k = _head_to_low(p_ref[:, base + LANE:base + 2 * LANE], e)
                v = _head_to_low(p_ref[:, base + 2 * LANE:base + 3 * LANE], e)
                hi, mid, lo = [part[:, h:h + 1] for part in bias_parts]
                ka = jnp.where(low, k, jnp.where(k_ones, 1.0, 0.0))
                ka = jnp.where(lane == L_ONE_Q, hi, jnp.where(lane == L_ONE_Q + 1, mid, jnp.where(lane == L_ONE_Q + 2, lo, ka)))
                qa = jnp.where(low, q * QK_SCALE, q_tail)
                va = jnp.where(low, v, v_tail)
                qa_ref[h] = qa.astype(BF16)
                ka_ref[h] = ka.astype(BF16)
                va_ref[h] = va.astype(BF16)
                qt_ref[h] = qa.T.astype(BF16)
                vt_ref[h] = va.T.astype(BF16)

    out = jax.ShapeDtypeStruct((FOX_HEADS, t, LANE), BF16)
    out_t = jax.ShapeDtypeStruct((FOX_HEADS, t // tq, LANE, tq), BF16)
    ospec = pl.BlockSpec((FOX_HEADS, tq, LANE), lambda i: (0, i, 0))
    tspec = pl.BlockSpec((FOX_HEADS, None, LANE, tq), lambda i: (0, i, 0, 0))
    return _pcall(
        body, name=name, grid=(t // tq,),
        in_specs=[pl.BlockSpec((tq, 3 * FOX_W), lambda i: (i, 1)), pl.BlockSpec((tq, LANE), lambda i: (i, 0))],
        out_specs=[ospec, ospec, ospec, tspec, tspec], out_shape=[out, out, out, out_t, out_t],
        compiler_params=_params("parallel"),
    )(proj, c)


STEP_PAIRS = 2
STEP_HEADS = 2 * STEP_PAIRS
FOX_GROUPS = FOX_PAIRS // STEP_PAIRS
FWD_PAIRS = 4
FWD_HEADS = 2 * FWD_PAIRS
FWD_GROUPS = FOX_PAIRS // FWD_PAIRS


def _fox_fwd(qt, ka, vt, mixed, name):
    nh, nq, tq, _ = ka.shape
    t = nq * tq

    def body(qt_ref, ka_ref, vt_ref, mixed_in, mixed_ref, o_ref, lse_ref):
        i = pl.program_id(1)
        lane = lax.broadcasted_iota(jnp.int32, (tq, LANE), 1)
        key_le_query = lax.broadcasted_iota(jnp.int32, (tq, tq), 0) <= lax.broadcasted_iota(jnp.int32, (tq, tq), 1)

        def logits(j):
            return [_dot(ka_ref[h, j], qt_ref[h]) for h in range(FWD_HEADS)]

        def update(j, scores, carry, diagonal):
            new = []
            for h in range(FWD_HEADS):
                m, acc = carry[h]
                s = jnp.where(key_le_query, scores[h], NEG) if diagonal else scores[h]
                m_new = jnp.maximum(m, jnp.max(s, axis=0, keepdims=True))
                p = jnp.exp(s - m_new).astype(BF16)
                new.append((m_new, jnp.exp(m - m_new) * acc + _dot(vt_ref[h, j], p)))
            return tuple(new)

        init = tuple((jnp.full((1, tq), NEG, F32), jnp.zeros((LANE, tq), F32)) for _ in range(FWD_HEADS))
        carry = lax.fori_loop(0, i, lambda j, cr: update(j, logits(j), cr, False), init)
        outs, lse_rows = [], []
        for m, acc in update(i, logits(i), carry, True):
            l = acc[FOX_DH:FOX_DH + 1, :]
            outs.append((acc / l).T)
            lse_rows.append(m + jnp.log(l))
        lse_rows.append(jnp.zeros((LANE - FWD_HEADS, tq), F32))
        o_all = jnp.concatenate([_pair(outs[2 * c], outs[2 * c + 1], lane < FOX_DH) for c in range(FWD_PAIRS)], axis=1)
        mixed_ref[...] = o_all.astype(BF16)
        o_ref[...] = o_all
        lse_ref[...] = jnp.concatenate(lse_rows, axis=0).T

    width = FWD_PAIRS * LANE
    whole = pl.BlockSpec((FWD_HEADS, nq, tq, LANE), lambda g, i: (g, 0, 0, 0), pipeline_mode=pl.Buffered(1))
    whole_t = pl.BlockSpec((FWD_HEADS, nq, LANE, tq), lambda g, i: (g, 0, 0, 0), pipeline_mode=pl.Buffered(1))
    return _pcall(
        body, name=name, grid=(FWD_GROUPS, nq),
        in_specs=[pl.BlockSpec((FWD_HEADS, None, LANE, tq), lambda g, i: (g, i, 0, 0)), whole, whole_t,
                  pl.BlockSpec(memory_space=pl.ANY)],
        out_specs=[pl.BlockSpec((tq, width), lambda g, i: (i, RET_V // width + g)),
                   pl.BlockSpec((tq, width), lambda g, i: (i, g)),
                   pl.BlockSpec((None, tq, LANE), lambda g, i: (g, i, 0))],
        out_shape=[jax.ShapeDtypeStruct(mixed.shape, BF16), jax.ShapeDtypeStruct((t, FOX_W), F32),
                   jax.ShapeDtypeStruct((FWD_GROUPS, t, LANE), F32)],
        input_output_aliases={3: 0},
        compiler_params=_params("parallel", "parallel"),
    )(qt, ka, vt, mixed)


def _fox_prep_bwd(dmixed, o_fox, lse, qa, name):
    t = dmixed.shape[0]
    tq = TOK_TILE

    def body(dm_ref, o_ref, lse_ref, qa_ref, qab_ref, doa_ref):
        i = pl.program_id(0)
        lane = lax.broadcasted_iota(jnp.int32, (tq, LANE), 1)
        low = lane < FOX_DH
        live = (i * tq + lax.broadcasted_iota(jnp.int32, (tq, 1), 0)) >= N_PAD
        lse_parts = [_split3(jnp.where(live, -lse_ref[grp], 0.0)) for grp in range(FWD_GROUPS)]
        for pair in range(FOX_PAIRS):
            cols = slice(LANE * pair, LANE * (pair + 1))
            d_slab = dm_ref[:, cols]
            prod = d_slab * o_ref[:, cols]
            for e in range(2):
                h = 2 * pair + e
                nd = -jnp.sum(jnp.where(low, _head_to_low(prod, e), 0.0), axis=-1, keepdims=True)
                nd_hi = nd.astype(BF16).astype(F32)
                doa = jnp.where(low, _head_to_low(d_slab, e), 0.0)
                doa = jnp.where(lane == FOX_DH, nd_hi, jnp.where(lane == FOX_DH + 1, nd - nd_hi, doa))
                doa_ref[h] = doa.astype(BF16)
                lane_h = h % FWD_HEADS
                hi, mid, lo = [part[:, lane_h:lane_h + 1] for part in lse_parts[h // FWD_HEADS]]
                qab = qa_ref[h].astype(F32)
                qab = jnp.where(lane == L_LSE, hi, jnp.where(lane == L_LSE + 1, mid, jnp.where(lane == L_LSE + 2, lo, qab)))
                qab_ref[h] = qab.astype(BF16)

    out = jax.ShapeDtypeStruct((FOX_HEADS, t, LANE), BF16)
    hspec = pl.BlockSpec((FOX_HEADS, tq, LANE), lambda i: (0, i, 0))
    return _pcall(
        body, name=name, grid=(t // tq,),
        in_specs=[pl.BlockSpec((tq, FOX_W), lambda i: (i, 1)), pl.BlockSpec((tq, FOX_W), lambda i: (i, 0)),
                  pl.BlockSpec((FWD_GROUPS, tq, LANE), lambda i: (0, i, 0)), hspec],
        out_specs=[hspec, hspec], out_shape=[out, out],
        compiler_params=_params("parallel"),
    )(dmixed, o_fox, lse, qa)


def _fox_bwd(qab, doa, ka, va, dproj, name):
    nh, nq, tq, _ = qab.shape
    t = nq * tq
    slab = 3 * LANE * STEP_PAIRS
    group0 = (2 * RET_QK + 2 * RET_V) // slab

    def body(qab_ref, doa_ref, ka_ref, va_ref, dproj_in, dp_ref, drs_ref, dcs_ref, dq_ref):
        g, j = pl.program_id(0), pl.program_id(1)

        @pl.when((g == 0) & (j == 0))
        def _():
            drs_ref[...] = jnp.zeros_like(drs_ref)
            dcs_ref[...] = jnp.zeros_like(dcs_ref)

        @pl.when(j == 0)
        def _():
            dq_ref[...] = jnp.zeros_like(dq_ref)

        lane = lax.broadcasted_iota(jnp.int32, (tq, LANE), 1)
        low = lane < FOX_DH
        key_le_query = lax.broadcasted_iota(jnp.int32, (tq, tq), 0) <= lax.broadcasted_iota(jnp.int32, (tq, tq), 1)

        def by_head(c, a, b, col):
            h = STEP_HEADS * g + 2 * c
            return jnp.where(lane == h, a[:, col:col + 1], jnp.where(lane == h + 1, b[:, col:col + 1], 0.0))


        def step(i, carry, diagonal):
            st = [_dot(ka_ref[h], qab_ref[h, i], NT) for h in range(STEP_HEADS)]
            dpt = [_dot(va_ref[h], doa_ref[h, i], NT) for h in range(STEP_HEADS)]
            new = []
            for h in range(STEP_HEADS):
                p = jnp.exp(st[h])
                if diagonal:
                    p = jnp.where(key_le_query, p, 0.0)
                ds = (p * dpt[h]).astype(BF16)
                dq_ref[h, i] += _dot(ds, ka_ref[h], TN)
                dk, dv = carry[h]
                new.append((dk + _dot(ds, qab_ref[h, i]), dv + _dot(p.astype(BF16), doa_ref[h, i])))
            return tuple(new)

        zero = jnp.zeros((tq, LANE), F32)
        carry = step(j, tuple((zero, zero) for _ in range(STEP_HEADS)), True)
        carry = lax.fori_loop(j + 1, nq, lambda i, cr: step(i, cr, False), carry)
        rows = pl.ds(pl.multiple_of(j * tq, tq), tq)
        for c in range(STEP_PAIRS):
            (dka, dva), (dkb, dvb) = carry[2 * c], carry[2 * c + 1]
            c0 = 3 * LANE * c
            dp_ref[rows, c0 + LANE:c0 + 2 * LANE] = _pair(dka, dkb, low).astype(BF16)
            dp_ref[rows, c0 + 2 * LANE:c0 + 3 * LANE] = _pair(dva, dvb, low).astype(BF16)
            dcs_ref[rows, :] += by_head(c, dka, dkb, L_ONE_Q)

        @pl.when(j == nq - 1)
        def _():
            for c in range(STEP_PAIRS):
                for blk in range(nq):
                    r = slice(blk * tq, (blk + 1) * tq)
                    a, b = dq_ref[2 * c, blk], dq_ref[2 * c + 1, blk]
                    dp_ref[r, 3 * LANE * c:3 * LANE * c + LANE] = (_pair(a, b, low) * QK_SCALE).astype(BF16)
                    drs_ref[r, :] += by_head(c, a, b, L_ONE_K)

    whole = pl.BlockSpec((STEP_HEADS, nq, tq, LANE), lambda g, j: (g, 0, 0, 0), pipeline_mode=pl.Buffered(1))
    blk = pl.BlockSpec((STEP_HEADS, None, tq, LANE), lambda g, j: (g, j, 0, 0))
    sums = pl.BlockSpec((t, LANE), lambda g, j: (0, 0), pipeline_mode=pl.Buffered(1))
    return _pcall(
        body, name=name, grid=(FOX_GROUPS, nq),
        in_specs=[whole, whole, blk, blk, pl.BlockSpec(memory_space=pl.ANY)],
        out_specs=[pl.BlockSpec((t, slab), lambda g, j: (0, group0 + g)), sums, sums],
        out_shape=[jax.ShapeDtypeStruct(dproj.shape, BF16), jax.ShapeDtypeStruct((t, LANE), F32),
                   jax.ShapeDtypeStruct((t, LANE), F32)],
        input_output_aliases={4: 0},
        scratch_shapes=[pltpu.VMEM((STEP_HEADS, nq, tq, LANE), F32)],
        compiler_params=_params("arbitrary", "arbitrary"),
    )(qab, doa, ka, va, dproj)


HALO = 8


def _rows_ext(ref, r0, rows, t, before, after):
    lo, hi = r0 - before, r0 + rows + after
    width = ref.shape[-1]
    parts = []
    if lo < 0:
        parts.append(jnp.zeros((-lo, width), F32))
    parts.append(ref[max(lo, 0):min(hi, t), :].astype(F32))
    if hi > t:
        parts.append(jnp.zeros((hi - t, width), F32))
    return parts[0] if len(parts) == 1 else jnp.concatenate(parts, axis=0)


def _conv_taps(a_ext, r0_ext, cw_ref, cb_ref):
    n = a_ext.shape[0]
    if r0_ext < N_PAD:
        row = r0_ext + lax.broadcasted_iota(jnp.int32, (n, 1), 0)
        a_ext = jnp.where(row >= N_PAD, a_ext, 0.0)
    a1 = pltpu.roll(a_ext, 1, axis=0)
    a2 = pltpu.roll(a_ext, 2, axis=0)
    acc = cb_ref[...] + a2 * cw_ref[0:1, :] + a1 * cw_ref[1:2, :] + a_ext * cw_ref[2:3, :]
    return a_ext, a1, a2, acc


FF_COLS = 256


def _up_conv_fwd(n2, w_up_t, conv_w8, conv_b, name):
    t, d = n2.shape
    f = w_up_t.shape[1]
    rows = TOK_TILE
    starts = list(range(0, t, rows))

    def body(n_ref, wa_ref, wb_ref, cw_ref, cb_ref, up_ref, g_ref):
        def project(r0):
            n_rows = n_ref[r0:r0 + rows, :]
            up_ref[0, r0:r0 + rows, :] = _dot(n_rows, wa_ref[...], NT)
            up_ref[1, r0:r0 + rows, :] = _dot(n_rows, wb_ref[...], NT)

        def activate(r0):
            a_ext = _rows_ext(up_ref.at[0], r0, rows, t, HALO, 0)
            _, _, _, acc = _conv_taps(a_ext, r0 - HALO, cw_ref, cb_ref)
            acc = acc[HALO:, :]
            g_ref[r0:r0 + rows, :] = (acc * _sigmoid(acc) * up_ref[1, r0:r0 + rows, :]).astype(BF16)

        project(starts[0])
        for r0, r_next in zip(starts, starts[1:] + [None]):
            if r_next is not None:
                project(r_next)
            activate(r0)

    return _pcall(
        body, name=name, grid=(f // FF_COLS,),
        in_specs=[pl.BlockSpec((t, d), lambda j: (0, 0), pipeline_mode=pl.Buffered(1)),
                  pl.BlockSpec((None, FF_COLS, d), lambda j: (0, j, 0)), pl.BlockSpec((None, FF_COLS, d), lambda j: (1, j, 0)),
                  pl.BlockSpec((8, FF_COLS), lambda j: (0, j)), pl.BlockSpec((1, FF_COLS), lambda j: (0, j))],
        out_specs=[pl.BlockSpec((2, t, FF_COLS), lambda j: (0, 0, j)), pl.BlockSpec((t, FF_COLS), lambda j: (0, j))],
        out_shape=[jax.ShapeDtypeStruct((2, t, f), F32), jax.ShapeDtypeStruct((t, f), BF16)],
        compiler_params=_params("parallel"),
    )(n2, w_up_t, w_up_t, conv_w8, conv_b)


def _dg_conv_bwd(up, conv_w8, conv_b, dh2, w_down, name):
    _, t, f = up.shape
    d = dh2.shape[1]
    rows = TOK_TILE
    starts = list(range(0, t, rows))

    def body(a_ref, b_ref, cw_ref, cb_ref, dh_ref, wd_ref, dup_ref, gcw_ref, gcb_ref, dg_ref):
        def project(r0):
            dg_ref[r0:r0 + rows, :] = _dot(dh_ref[r0:r0 + rows, :], wd_ref[...], NT)

        gw = [jnp.zeros((1, FF_COLS), F32) for _ in range(3)]
        gb = jnp.zeros((1, FF_COLS), F32)
        project(starts[0])
        for r0, r_next in zip(starts, starts[1:] + [None]):
            if r_next is not None:
                project(r_next)
            a_ext = _rows_ext(a_ref, r0, rows, t, HALO, HALO)
            b_ext = _rows_ext(b_ref, r0, rows, t, HALO, HALO)
            dg_ext = _rows_ext(dg_ref, r0, rows, t, HALO, HALO)
            a0, a1, a2, acc = _conv_taps(a_ext, r0 - HALO, cw_ref, cb_ref)
            sg = _sigmoid(acc)
            dacc = dg_ext * b_ext * (sg * (1.0 + acc * (1.0 - sg)))
            n = dacc.shape[0]
            da = (dacc * cw_ref[2:3, :] + pltpu.roll(dacc, n - 1, axis=0) * cw_ref[1:2, :]
                  + pltpu.roll(dacc, n - 2, axis=0) * cw_ref[0:1, :])
            core = slice(HALO, HALO + rows)
            da = da[core, :]
            if r0 < N_PAD:
                row = r0 + lax.broadcasted_iota(jnp.int32, (rows, 1), 0)
                da = jnp.where(row >= N_PAD, da, 0.0)
            dup_ref[0, r0:r0 + rows, :] = da.astype(BF16)
            dup_ref[1, r0:r0 + rows, :] = (dg_ext * acc * sg)[core, :].astype(BF16)
            dacc_c = dacc[core, :]
            gw[0] = gw[0] + jnp.sum(dacc_c * a2[core, :], axis=0, keepdims=True)
            gw[1] = gw[1] + jnp.sum(dacc_c * a1[core, :], axis=0, keepdims=True)
            gw[2] = gw[2] + jnp.sum(dacc_c * a0[core, :], axis=0, keepdims=True)
            gb = gb + jnp.sum(dacc_c, axis=0, keepdims=True)
        gcw_ref[...] = jnp.zeros((8, FF_COLS), F32)
        for tap in range(3):
            gcw_ref[tap:tap + 1, :] = gw[tap]
        gcb_ref[...] = gb

    return _pcall(
        body, name=name, grid=(f // FF_COLS,),
        in_specs=[pl.BlockSpec((None, t, FF_COLS), lambda j: (0, 0, j)), pl.BlockSpec((None, t, FF_COLS), lambda j: (1, 0, j)),
                  pl.BlockSpec((8, FF_COLS), lambda j: (0, j)), pl.BlockSpec((1, FF_COLS), lambda j: (0, j)),
                  pl.BlockSpec((t, d), lambda j: (0, 0), pipeline_mode=pl.Buffered(1)),
                  pl.BlockSpec((FF_COLS, d), lambda j: (j, 0))],
        out_specs=[pl.BlockSpec((2, t, FF_COLS), lambda j: (0, 0, j)), pl.BlockSpec((8, FF_COLS), lambda j: (0, j)),
                   pl.BlockSpec((1, FF_COLS), lambda j: (0, j))],
        out_shape=[jax.ShapeDtypeStruct((2, t, f), BF16), jax.ShapeDtypeStruct((8, f), F32),
                   jax.ShapeDtypeStruct((1, f), F32)],
        scratch_shapes=[pltpu.VMEM((t, FF_COLS), F32)],
        compiler_params=_params("parallel"),
    )(up, up, conv_w8, conv_b, dh2, w_down)


def _exchange(arrays, kinds, name, after=None):
    n = len(arrays)
    npeer = N_DEV - 1
    n_in = n + int(after is not None)

    def body(*refs):
        ins, outs = refs[:n], refs[n_in:n_in + n]
        send_sems, recv_sems, local_sems = refs[n_in + n:]
        x, y, c = lax.axis_index("x"), lax.axis_index("y"), lax.axis_index("c")
        me = 4 * x + 2 * y + c
        copies, locals_ = [], []
        for a in range(n):
            gather = kinds[a] == "gather"
            own = pltpu.make_async_copy(ins[a] if gather else ins[a].at[me], outs[a].at[me], local_sems.at[a])
            own.start()
            locals_.append(own)
            for d in range(1, N_DEV):
                px = 1 - x if d & 4 else x
                py = 1 - y if d & 2 else y
                pc = 1 - c if d & 1 else c
                src = ins[a] if gather else ins[a].at[4 * px + 2 * py + pc]
                cp = pltpu.make_async_remote_copy(
                    src_ref=src, dst_ref=outs[a].at[me],
                    send_sem=send_sems.at[a * npeer + d - 1], recv_sem=recv_sems.at[a * npeer + d - 1],
                    device_id=(px, py, pc), device_id_type=pl.DeviceIdType.MESH)
                cp.start()
                copies.append(cp)
        for cp in copies:
            cp.wait_recv()
        for cp in copies:
            cp.wait_send()
        for own in locals_:
            own.wait()

    out_shape = [jax.ShapeDtypeStruct((N_DEV,) + (a.shape if k == "gather" else a.shape[1:]), a.dtype)
                 for a, k in zip(arrays, kinds)]
    return _pcall(
        body, name=name,
        in_specs=[pl.BlockSpec(memory_space=pl.ANY)] * n_in,
        out_specs=[pl.BlockSpec(memory_space=pl.ANY)] * n,
        out_shape=out_shape,
        scratch_shapes=[pltpu.SemaphoreType.DMA((n * npeer,)), pltpu.SemaphoreType.DMA((n * npeer,)),
                        pltpu.SemaphoreType.DMA((n,))],
        compiler_params=pltpu.CompilerParams(has_side_effects=True),
    )(*arrays, *([] if after is None else [after]))


ALL_PEERS = tuple(range(1, N_DEV))
SAME_CORE_AND_SIBLING = (1, 2, 4, 6)
OTHER_CHIPS = (2, 4, 6)


def _peer_copies(srcs, lands, kinds, send_sems, recv_sems, relations=ALL_PEERS):
    x, y, c = lax.axis_index("x"), lax.axis_index("y"), lax.axis_index("c")
    me = 4 * x + 2 * y + c
    copies = []
    for a in range(len(srcs)):
        for d in relations:
            px = 1 - x if d & 4 else x
            py = 1 - y if d & 2 else y
            pc = 1 - c if d & 1 else c
            peer = 4 * px + 2 * py + pc
            k = a * (N_DEV - 1) + d - 1
            if kinds[a] == "forward":
                src, dst, target = lands[a].at[peer], lands[a].at[peer], (x, y, 1 - c)
            else:
                src, dst, target = (srcs[a] if kinds[a] == "gather" else srcs[a].at[peer]), lands[a].at[me], (px, py, pc)
            copies.append(pltpu.make_async_remote_copy(
                src_ref=src, dst_ref=dst, send_sem=send_sems.at[k], recv_sem=recv_sems.at[k],
                device_id=target, device_id_type=pl.DeviceIdType.MESH))
    return copies


def _own_copies(srcs, lands, kinds, sems):
    me = 4 * lax.axis_index("x") + 2 * lax.axis_index("y") + lax.axis_index("c")
    first = len(srcs) * (N_DEV - 1)
    return [pltpu.make_async_copy(srcs[a].at[me] if kinds[a] == "scatter" else srcs[a], lands[a].at[me], sems.at[first + a])
            for a in range(len(srcs))]


def _exchange_start(arrays, kinds, name, after=None, relations=ALL_PEERS, lands=None, own=True):
    n = len(arrays)
    nsem = n * (N_DEV - 1) + n
    hbm = pl.BlockSpec(memory_space=pltpu.HBM)
    sem = pl.BlockSpec(memory_space=pltpu.SEMAPHORE)
    land_shapes = ([l.shape for l in lands] if lands is not None else
                   [(N_DEV,) + (a.shape if k == "gather" else a.shape[1:]) for a, k in zip(arrays, kinds)])

    n_in = 2 * n + int(after is not None)

    def body(*refs):
        srcs, land_refs = refs[:n], refs[n:2 * n]
        send_sems, recv_sems = refs[n_in], refs[n_in + 1]
        token = refs[-1]
        for cp in _peer_copies(srcs, land_refs, kinds, send_sems, recv_sems, relations):
            cp.start()
        for cp in _own_copies(srcs, land_refs, kinds, send_sems) if own else []:
            cp.start()
        token[...] = jnp.zeros_like(token)

    operands = [pltpu.with_memory_space_constraint(a, pltpu.HBM) for a in arrays]
    operands += (list(lands) if lands is not None else
                 [pltpu.with_memory_space_constraint(lax.empty(s, a.dtype), pltpu.HBM) for s, a in zip(land_shapes, arrays)])
    operands += [] if after is None else [after]
    out = _pcall(
        body, name=name,
        in_specs=[hbm] * (2 * n) + ([] if after is None else [pl.BlockSpec(memory_space=pl.ANY)]),
        out_specs=[sem, sem] + [hbm] * (2 * n) + [pl.BlockSpec(memory_space=pltpu.VMEM)],
        out_shape=[pltpu.SemaphoreType.DMA((nsem,)), pltpu.SemaphoreType.DMA((nsem,))]
        + [pltpu.HBM(a.shape, a.dtype) for a in arrays]
        + [pltpu.HBM(s, a.dtype) for s, a in zip(land_shapes, arrays)]
        + [jax.ShapeDtypeStruct((8, LANE), F32)],
        input_output_aliases={k: 2 + k for k in range(2 * n)},
        compiler_params=pltpu.CompilerParams(has_side_effects=pltpu.SideEffectType.DATAFLOW_SIDE_EFFECTING),
    )(*operands)
    return out[0], out[1], list(out[2:2 + n]), list(out[2 + n:2 + 2 * n]), out[-1]


def _exchange_wait(started, kinds, after, name, own=True, relations=ALL_PEERS, with_sources=False):
    send_sems, recv_sems, srcs, lands, _ = started
    n = len(srcs)
    hbm = pl.BlockSpec(memory_space=pltpu.HBM)
    sem = pl.BlockSpec(memory_space=pltpu.SEMAPHORE)

    def body(*refs):
        src_refs, land_refs = refs[:n], refs[n:2 * n]
        copies = _peer_copies(src_refs, land_refs, kinds, refs[2 * n], refs[2 * n + 1], relations)
        for cp in copies:
            cp.wait_send()
        for cp in copies:
            cp.wait_recv()
        for cp in _own_copies(src_refs, land_refs, kinds, refs[2 * n]) if own else []:
            cp.wait()

    out = _pcall(
        body, name=name,
        in_specs=[hbm] * (2 * n) + [sem, sem, pl.BlockSpec(memory_space=pl.ANY)],
        out_specs=[hbm] * (2 * n),
        out_shape=[pltpu.HBM(a.shape, a.dtype) for a in srcs + lands],
        input_output_aliases={k: k for k in range(2 * n)},
        compiler_params=pltpu.CompilerParams(has_side_effects=pltpu.SideEffectType.DATAFLOW_SIDE_EFFECTING),
    )(*srcs, *lands, send_sems, recv_sems, after)
    return (list(out[:n]), list(out[n:])) if with_sources else list(out[n:])


def _sum_slots(slots, name, rows_tile):
    nd, r, c = slots.shape

    def body(s_ref, o_ref):
        acc = s_ref[0].astype(F32)
        for p in range(1, nd):
            acc = acc + s_ref[p].astype(F32)
        o_ref[...] = acc

    return _pcall(
        body, name=name, grid=(r // rows_tile,),
        in_specs=[pl.BlockSpec((nd, rows_tile, c), lambda i: (0, i, 0))],
        out_specs=pl.BlockSpec((rows_tile, c), lambda i: (i, 0)),
        out_shape=jax.ShapeDtypeStruct((r, c), F32),
        compiler_params=_params("parallel"),
    )(slots)


def _sum_slots_small(slot_arrays, own_arrays, name):
    n = len(slot_arrays)

    def body(*refs):
        me = 4 * lax.axis_index("x") + 2 * lax.axis_index("y") + lax.axis_index("c")
        for s_ref, own_ref, o_ref in zip(refs[:n], refs[n:2 * n], refs[2 * n:]):
            acc = jnp.where(me == 0, own_ref[...], s_ref[0])
            for p in range(1, s_ref.shape[0]):
                acc = acc + jnp.where(me == p, own_ref[...], s_ref[p])
            o_ref[...] = acc

    return _pcall(body, name=name, out_shape=[jax.ShapeDtypeStruct(a.shape[1:], F32) for a in slot_arrays])(
        *slot_arrays, *own_arrays)


def _adamw_values(w, gr, m, v):
    nm = ADAM_B1 * m + (1.0 - ADAM_B1) * gr
    nv = ADAM_B2 * v + (1.0 - ADAM_B2) * (gr * gr)
    m_hat = nm / (1.0 - ADAM_B1 ** ADAM_STEP)
    v_hat = nv / (1.0 - ADAM_B2 ** ADAM_STEP)
    return -ADAM_LR * (m_hat / (jnp.sqrt(v_hat) + ADAM_EPS) + ADAM_WD * w), nm, nv


def _adamw_update(w_ref, g_ref, m_ref, v_ref, d_ref, nm_ref, nv_ref):
    d_ref[...], nm_ref[...], nv_ref[...] = _adamw_values(w_ref[...], g_ref[...], m_ref[...], v_ref[...])


def _adamw_from_slots(w, slots, m, v, name, cols_tile=256):
    rows, cols = w.shape
    nd, rows_pad, _ = slots.shape

    def body(w_ref, s_ref, m_ref, v_ref, g_ref, d_ref, nm_ref, nv_ref):
        gr = s_ref[0, 0:rows, :].astype(F32)
        for p in range(1, nd):
            gr = gr + s_ref[p, 0:rows, :].astype(F32)
        g_ref[...] = gr
        d_ref[...], nm_ref[...], nv_ref[...] = _adamw_values(w_ref[...], gr, m_ref[...], v_ref[...])

    spec = pl.BlockSpec((rows, cols_tile), lambda i: (0, i))
    slot_spec = pl.BlockSpec((nd, rows_pad, cols_tile), lambda i: (0, 0, i))
    return _pcall(
        body, name=name, grid=(cols // cols_tile,), in_specs=[spec, slot_spec, spec, spec], out_specs=[spec] * 4,
        out_shape=[jax.ShapeDtypeStruct((rows, cols), F32)] * 4, compiler_params=_params("parallel"),
    )(w, slots, m, v)


def _adamw_small(ws, gs, ms, vs, name):
    n = len(ws)

    def body(*refs):
        ins, outs = refs[:4 * n], refs[4 * n:]
        for k in range(n):
            _adamw_update(ins[k], ins[n + k], ins[2 * n + k], ins[3 * n + k], outs[k], outs[n + k], outs[2 * n + k])

    shapes = [jax.ShapeDtypeStruct(w.shape, F32) for w in ws]
    out = _pcall(body, name=name, out_shape=shapes * 3)(*ws, *gs, *ms, *vs)
    return list(out[:n]), list(out[n:2 * n]), list(out[2 * n:])


def _adamw(w, g, m, v, name, rows_tile):
    r, c = w.shape
    body = lambda *refs: _adamw_update(*refs)
    spec = pl.BlockSpec((rows_tile, c), lambda i: (i, 0))
    shp = jax.ShapeDtypeStruct((r, c), F32)
    return _pcall(
        body, name=name, grid=(r // rows_tile,), in_specs=[spec] * 4, out_specs=[spec] * 3, out_shape=[shp] * 3,
        compiler_params=_params("parallel"),
    )(w, g, m, v)


F0 = 2 * RET_QK + 2 * RET_V


def _to_internal_rows(w_t):
    cols = w_t.shape[1]
    fox = w_t[F0:F0 + 3 * FOX_W].reshape(3, FOX_PAIRS, LANE, cols).transpose(1, 0, 2, 3).reshape(3 * FOX_W, cols)
    tail = jnp.zeros((IN_PAD - IN_WIDTH, cols), w_t.dtype)
    return jnp.concatenate([w_t[:F0], fox, w_t[F0 + 3 * FOX_W:], tail], axis=0)


def _from_internal_rows(g_t):
    cols = g_t.shape[1]
    fox = g_t[F0:F0 + 3 * FOX_W].reshape(FOX_PAIRS, 3, LANE, cols).transpose(1, 0, 2, 3).reshape(3 * FOX_W, cols)
    return jnp.concatenate([g_t[:F0], fox, g_t[F0 + 3 * FOX_W:F0 + 3 * FOX_W + FOX_HEADS]], axis=0)


IN_BLOCK = IN_WIDTH // N_DEV
IN_BLOCK_PAD = 400
BF16_ROWS = 16


def _slot_row_of_internal():
    rows = np.arange(IN_WIDTH)
    fox = rows[F0:F0 + 3 * FOX_W].reshape(3, FOX_PAIRS, LANE).transpose(1, 0, 2).reshape(-1)
    original = np.concatenate([rows[:F0], fox, rows[F0 + 3 * FOX_W:]])
    slot_rows = original // IN_BLOCK * IN_BLOCK_PAD + original % IN_BLOCK
    return np.concatenate([slot_rows, np.full(IN_PAD - IN_WIDTH, -1)])


def _internal_row_of_slot():
    forward = _slot_row_of_internal()
    back = np.full(N_DEV * IN_BLOCK_PAD, -1)
    back[forward[forward >= 0]] = np.nonzero(forward >= 0)[0]
    return back


def _row_runs(src_of_dst):
    tiles = []
    for t0 in range(0, len(src_of_dst), LANE):
        runs = []
        for o in range(LANE):
            s = int(src_of_dst[t0 + o])
            if s < 0:
                continue
            if runs and runs[-1][0] + runs[-1][2] == o and runs[-1][1] + runs[-1][2] == s:
                runs[-1][2] += 1
            else:
                runs.append([o, s, 1])
        tiles.append(runs)
    return tiles


def _move_rows(src, src_of_dst, name):
    n_src, cols = src.shape
    tiles = _row_runs(src_of_dst)

    def body(s_ref, o_ref):
        for t, runs in enumerate(tiles):
            rows = pl.ds(t * LANE, LANE)
            if not runs:
                o_ref[rows, :] = jnp.zeros((LANE, cols), o_ref.dtype)
                continue
            if len(runs) == 1 and runs[0][0] == 0 and runs[0][2] == LANE and runs[0][1] % BF16_ROWS == 0:
                o_ref[rows, :] = s_ref[pl.ds(runs[0][1], LANE), :]
                continue
            acc = None
            for o0, s0, n in runs:
                w0 = s0 // BF16_ROWS * BF16_ROWS
                width = -(-(s0 - w0 + n) // LANE) * LANE
                w0 = min(w0, n_src - width)
                i = lax.broadcasted_iota(jnp.int32, (LANE, width), 0)
                j = lax.broadcasted_iota(jnp.int32, (LANE, width), 1)
                pick = ((j - i == s0 - w0 - o0) & (i >= o0) & (i < o0 + n)).astype(src.dtype)
                part = _dot(pick, s_ref[pl.ds(w0, width), :])
                acc = part if acc is None else acc + part
            o_ref[rows, :] = acc.astype(o_ref.dtype)

    return _pcall(body, name=name, out_shape=jax.ShapeDtypeStruct((len(src_of_dst), cols), src.dtype))(src)


def _local_step(x, target, meta, attn_g, fox_b, ret_g, ffn_g, conv_w8, conv_b, final_g,
                first_weight, late_weights, ffn_grads_ready, out_grad_ready, in_grad_ready):
    seq, d = x.shape
    t = seq + PREFIX
    tm = TOK_TILE
    nq = t // tm
    fox_b128 = jnp.pad(fox_b, ((0, 0), (0, LANE - FOX_HEADS)))

    h0, n1 = _prep_norm(x, meta, attn_g, "prep_norm")
    w_in_t = first_weight(n1)
    proj = _mm_simple(n1, w_in_t, mode="nt", tm=tm, tn=IN_PAD, tk=d, out_dtype=F32, name="mm_in")
    cos, sin = _rope_tables(t)
    o_pre, mixed, states = _ret_fwd(proj, cos, sin, ret_g, "ret_fwd")
    c = _forget_cumsum(proj, fox_b128, "forget_cumsum")
    qa, ka, va, qt, vt = _fox_prep(proj, c, "fox_prep")
    by_block = lambda a: a.reshape(FOX_HEADS, nq, tm, LANE)
    mixed, o_fox, lse = _fox_fwd(qt, by_block(ka), vt, mixed, "fox_fwd")
    w_out, w_up_t, w_down = late_weights(o_fox)
    tile = pl.BlockSpec((tm, d), lambda i: (i, 0))
    row_vec = pl.BlockSpec((1, d), lambda i: (0, 0))
    resident = lambda shape: pl.BlockSpec(shape, lambda i: (0,) * len(shape), pipeline_mode=pl.Buffered(1))
    acts = lambda dtype: jax.ShapeDtypeStruct((t, d), dtype)
    vec = jax.ShapeDtypeStruct((1, d), F32)

    def residual_and_norm(i, acc, ins, outs):
        h = acc + ins[0][...]
        outs[0][...] = h
        outs[1][...] = (h * lax.rsqrt(jnp.mean(h * h, axis=-1, keepdims=True) + EPS) * ins[1][...]).astype(BF16)

    h1, n2 = _matmul_rows([mixed], [tile], [w_out], [resident((d, d))], [h0, ffn_g], [tile, row_vec],
                          [tile, tile], [acts(F32), acts(BF16)], residual_and_norm, mode="nn", steps=nq, name="mm_out_norm")
    nf = D_FF // 1408
    up, g = _up_conv_fwd(n2, w_up_t, conv_w8, conv_b, "up_conv_fwd")

    def residual_loss_bwd(i, acc, ins, outs):
        loss_ref, dh_ref, dhb_ref, gg_ref = outs
        part, dh, gg = _loss_tile(i, acc + ins[0][...], jnp.concatenate([ins[1][...], ins[2][...], ins[3][...]], axis=0),
                                  ins[4][...])
        _accumulate(loss_ref, i, jnp.broadcast_to(part, loss_ref.shape))
        dh_ref[...] = dh
        dhb_ref[...] = dh.astype(BF16)
        _accumulate(gg_ref, i, gg)

    loss_tile, dh2, dh2_b, g_final = _matmul_rows(
        [g], [pl.BlockSpec((tm, D_FF), lambda i: (i, 0))], [w_down], [resident((D_FF, d))],
        [h1, target, target, target, final_g], [tile] + _shifted_row_specs(d) + [row_vec],
        [pl.BlockSpec((8, LANE), lambda i: (0, 0)), tile, tile, row_vec],
        [jax.ShapeDtypeStruct((8, LANE), F32), acts(F32), acts(BF16), vec], residual_loss_bwd,
        mode="nn", steps=nq, name="mm_down_loss")

    tkw = 2112 if t % 2112 == 0 else tm
    gw_down = _mm_simple(g, dh2_b, mode="tn", tm=1408, tn=d, tk=tkw, out_dtype=BF16, name="mm_gw_down")
    dup, g_conv_w8, g_conv_b = _dg_conv_bwd(up, conv_w8, conv_b, dh2_b, w_down, "dg_conv_bwd")

    half = lambda p: pl.BlockSpec((None, tm, D_FF), lambda i: (p, i, 0))
    half_w = lambda p: pl.BlockSpec((None, D_FF, d), lambda i: (p, 0, 0), pipeline_mode=pl.Buffered(1))
    gw_up_t = _matmul(
        dup, n2, mode="tn", grid=(2 * nf, 1, t // tkw),
        a_spec=pl.BlockSpec((None, tkw, 1408), lambda i, j, k: (i // nf, k, i % nf)),
        b_spec=pl.BlockSpec((tkw, d), lambda i, j, k: (k, 0)),
        o_spec=pl.BlockSpec((1408, d), lambda i, j, k: (i, 0)),
        out_shape=jax.ShapeDtypeStruct((2 * D_FF, d), BF16), name="mm_gw_up")
    def norm_bwd_and_mixer_grad(i, acc, ins, outs):
        dh, gg = _rms_bwd_tile(acc, ins[0][...], ins[1][...], ins[2][...])
        outs[0][...] = dh
        _accumulate(outs[1], i, gg)
        outs[2][...] = _dot(dh.astype(BF16), ins[3][...], NT)

    dh1, g_ffn, dmixed = _matmul_rows(
        [dup, dup], [half(0), half(1)], [w_up_t, w_up_t], [half_w(0), half_w(1)],
        [h1, ffn_g, dh2, w_out], [tile, row_vec, tile, resident((d, d))], [tile, row_vec, tile],
        [acts(F32), vec, acts(F32)], norm_bwd_and_mixer_grad,
        mode="nn", steps=nq, name="mm_dn2_norm_bwd", after=ffn_grads_ready(gw_down, gw_up_t))
    gw_out = _mm_simple(mixed, dh1, mode="tn", tm=d, tn=d, tk=tkw, out_dtype=BF16, name="mm_gw_out")
    dproj, g_ret = _ret_bwd(proj, cos, sin, ret_g + out_grad_ready(gw_out), dmixed, o_pre, states, "ret_bwd")
    qab, doa = _fox_prep_bwd(dmixed, o_fox, lse, qa, "fox_prep_bwd")
    dproj, drs, dcs = _fox_bwd(by_block(qab), by_block(doa), by_block(ka), by_block(va), dproj, "fox_bwd")
    dproj, g_fox_b = _forget_cumsum_bwd(proj, fox_b128, drs, dcs, dproj, "forget_cumsum_bwd")
    gw_in_t = _mm_simple(dproj, n1, mode="tn", tm=640, tn=d, tk=tkw, out_dtype=BF16, name="mm_gw_in")
    sent = in_grad_ready(gw_in_t)
    def input_grads(i, acc, ins, outs):
        gx_ref, gmeta_ref, gg_ref, buf_ref, sems = outs
        dh, gg = _rms_bwd_tile(acc, ins[0][...], ins[1][...], ins[2][...])
        _accumulate(gg_ref, i, gg)
        slot = i % 2

        def first_copy():
            return pltpu.make_async_copy(buf_ref.at[0, pl.ds(PREFIX, tm - PREFIX)], gx_ref.at[pl.ds(0, tm - PREFIX)],
                                         sems.at[0])

        def tile_copy(tile, buf_slot):
            rows = pl.ds(pl.multiple_of(tile * tm - PREFIX, PREFIX), tm)
            return pltpu.make_async_copy(buf_ref.at[buf_slot], gx_ref.at[rows], sems.at[buf_slot])

        @pl.when(i == 1)
        def _():
            first_copy().wait()

        @pl.when(i >= 2)
        def _():
            tile_copy(i - 1, 1 - slot).wait()

        buf_ref[slot] = dh

        @pl.when(i == 0)
        def _():
            gmeta_ref[...] = dh[N_PAD:PREFIX, :]
            first_copy().start()

        @pl.when(i > 0)
        def _():
            tile_copy(i, slot).start()

        @pl.when(i == nq - 1)
        def _():
            tile_copy(i, slot).wait()

    grad_x, g_meta, g_attn = _matmul_rows(
        [dproj], [pl.BlockSpec((tm, IN_PAD), lambda i: (i, 0))], [w_in_t], [resident((IN_PAD, d))],
        [h0, attn_g, dh1], [tile, row_vec, tile],
        [pl.BlockSpec(memory_space=pl.ANY), pl.BlockSpec((N_META, d), lambda i: (0, 0)), row_vec],
        [jax.ShapeDtypeStruct((seq, d), F32), jax.ShapeDtypeStruct((N_META, d), F32), vec], input_grads,
        mode="nn", steps=nq, name="mm_dn1_norm_bwd", after=sent,
        scratch=[pltpu.VMEM((2, tm, d), F32), pltpu.SemaphoreType.DMA((2,))])

    grads = dict(meta=g_meta, attn_g=g_attn, fox_b=g_fox_b, ret_g=g_ret,
                 ffn_g=g_ffn, conv_w=g_conv_w8, conv_b=g_conv_b, final_g=g_final)
    return loss_tile, grad_x, grads


def kernel(x, meta_tokens, attn_norm_g, w_in, fox_forget_b, ret_norm_g, w_out, ffn_norm_g, w_up, conv_w, conv_b, w_down, final_norm_g, loss_target, m_meta_tokens, m_attn_norm_g, m_w_in, m_fox_forget_b, m_ret_norm_g, m_w_out, m_ffn_norm_g, m_w_up, m_conv_w, m_conv_b, m_w_down, m_final_norm_g, v_meta_tokens, v_attn_norm_g, v_w_in, v_fox_forget_b, v_ret_norm_g, v_w_out, v_ffn_norm_g, v_w_up, v_conv_w, v_conv_b, v_w_down, v_final_norm_g):
    d = D_MODEL
    me = 4 * lax.axis_index("x") + 2 * lax.axis_index("y") + lax.axis_index("c")
    in_blk, in_blk_pad = IN_BLOCK, IN_BLOCK_PAD
    up_blk = 2 * D_FF // N_DEV
    down_blk = D_FF // N_DEV
    cw_blk = D_FF // N_DEV

    w_in_loc = jnp.pad(w_in[0].T.astype(BF16), ((0, in_blk_pad - in_blk), (0, 0)))
    cw_loc = jnp.pad(conv_w[0], ((0, 5), (0, 384 - cw_blk)))
    g_meta, g_cw = _exchange([meta_tokens, cw_loc], ["gather"] * 2, "gather_small")
    first = _exchange_start([w_in_loc], ["gather"], "gather_in_start", after=g_meta, relations=SAME_CORE_AND_SIBLING)
    rest_loc = [(w_out[0] + first[-1][0:1, 0:1]).astype(BF16), w_up[0].T.astype(BF16), w_down[0].astype(BF16)]
    rest = _exchange_start(rest_loc, ["gather"] * 3, "gather_rest_start")
    meta_f = g_meta.transpose(1, 0, 2).reshape(N_META, d)
    conv_w8 = jnp.pad(g_cw[:, :3, :cw_blk].transpose(1, 0, 2).reshape(3, D_FF), ((0, 5), (0, 0)))
    pending = {}

    def first_weight(after):
        own_in, landed = _exchange_wait(first, ["gather"], after, "gather_in_wait", relations=SAME_CORE_AND_SIBLING,
                                        with_sources=True)
        onward = _exchange_start(own_in, ["forward"], "gather_in_forward_start", relations=OTHER_CHIPS, lands=landed,
                                 own=False)
        (g_in,) = _exchange_wait(onward, ["forward"], onward[-1], "gather_in_forward_wait", own=False,
                                 relations=OTHER_CHIPS)
        return _move_rows(g_in.reshape(IN_PAD, d), _slot_row_of_internal(), "w_in_rows")

    def in_grad_ready(gw_in_t):
        blocks = _move_rows(gw_in_t, _internal_row_of_slot(), "gw_in_rows").reshape(N_DEV, in_blk_pad, d)
        pending["in"] = _exchange_start([blocks], ["scatter"], "grads_in_start")
        return pending["in"][-1][0:1, 0:1]

    def late_weights(after):
        g_out, g_up, g_down = _exchange_wait(rest, ["gather"] * 3, after, "gather_rest_wait")
        return g_out.reshape(d, d), g_up.reshape(2, D_FF, d), g_down.reshape(D_FF, d)

    def ffn_grads_ready(gw_down, gw_up_t):
        blocks = [gw_down.reshape(N_DEV, down_blk, d), gw_up_t.reshape(N_DEV, up_blk, d)]
        pending["ffn"] = _exchange_start(blocks, ["scatter"] * 2, "grads_ffn_start")
        return pending["ffn"][-1][0:1, 0:1]

    def out_grad_ready(gw_out):
        pending["out"] = _exchange_start([gw_out.reshape(N_DEV, d // N_DEV, d)], ["scatter"], "grads_out_start")
        return pending["out"][-1][0:1, 0:1]

    loss_tile, grad_x, gr = _local_step(
        x[0], loss_target[0], meta_f, attn_norm_g + rest[-1][0:1, 0:1], fox_forget_b, ret_norm_g, ffn_norm_g,
        conv_w8, conv_b, final_norm_g.reshape(1, d), first_weight, late_weights, ffn_grads_ready, out_grad_ready,
        in_grad_ready)

    small = [loss_tile, gr["attn_g"], gr["fox_b"], gr["ret_g"], gr["ffn_g"], gr["conv_b"], gr["final_g"],
             gr["meta"], gr["conv_w"]]
    small_kinds = ["gather"] * len(small)
    small_started = _exchange_start(small, small_kinds, "grads_small_start", own=False)

    r_down, r_up = _exchange_wait(pending["ffn"], ["scatter"] * 2, small_started[-1], "grads_ffn_wait")
    (r_out,) = _exchange_wait(pending["out"], ["scatter"], small_started[-1], "grads_out_wait")
    g_w_out = _sum_slots(r_out, "sum_w_out", d // N_DEV)
    g_w_up_t = _sum_slots(r_up, "sum_w_up", up_blk)
    g_w_down = _sum_slots(r_down, "sum_w_down", down_blk)
    as_t = lambda a: a[0].T
    from_t = lambda a: a.T[None]
    d_w_out, m_w_out_n, v_w_out_n = [a[None] for a in _adamw(w_out[0], g_w_out, m_w_out[0], v_w_out[0], "adamw_w_out", 128)]
    up_t = _adamw(as_t(w_up), g_w_up_t, as_t(m_w_up), as_t(v_w_up), "adamw_w_up", up_blk // 2)
    d_w_up, m_w_up_n, v_w_up_n = [from_t(a) for a in up_t]
    d_w_down, m_w_down_n, v_w_down_n = [a[None] for a in _adamw(w_down[0], g_w_down, m_w_down[0], v_w_down[0],
                                                                "adamw_w_down", down_blk)]

    own_small, r_small = _exchange_wait(small_started, small_kinds, up_t[0], "grads_small_wait", own=False,
                                        with_sources=True)
    (loss_all, g_attn, g_fox_b128, g_ret, g_ffn, g_conv_b, g_final, g_meta_full, g_cw_full) = _sum_slots_small(
        r_small, own_small, "sum_small")
    loss = loss_all[0, 0]
    g_fox_b = g_fox_b128[:, :FOX_HEADS]
    g_meta_loc = lax.dynamic_slice(g_meta_full, (0, me * (d // N_DEV)), (N_META, d // N_DEV))
    g_cw_loc = lax.dynamic_slice(g_cw_full, (0, me * cw_blk), (3, cw_blk))

    (r_in,) = _exchange_wait(pending["in"], ["scatter"], r_small[0], "grads_in_wait")
    g_w_in, d_w_in, m_w_in_n, v_w_in_n = [from_t(a) for a in _adamw_from_slots(
        as_t(w_in), r_in, as_t(m_w_in), as_t(v_w_in), "adamw_w_in")]
    g_w_in, g_w_up = g_w_in[0], g_w_up_t.T
    row = lambda a: a.reshape(1, d)
    sm_grads = [g_meta_loc, g_attn, g_fox_b, g_ret, g_ffn, g_cw_loc, g_conv_b, g_final]
    sm_w = [meta_tokens, attn_norm_g, fox_forget_b, ret_norm_g, ffn_norm_g, conv_w[0], conv_b, row(final_norm_g)]
    sm_m = [m_meta_tokens, m_attn_norm_g, m_fox_forget_b, m_ret_norm_g, m_ffn_norm_g, m_conv_w[0], m_conv_b,
            row(m_final_norm_g)]
    sm_v = [v_meta_tokens, v_attn_norm_g, v_fox_forget_b, v_ret_norm_g, v_ffn_norm_g, v_conv_w[0], v_conv_b,
            row(v_final_norm_g)]
    dl, ml, vl = [lst[:7] + [lst[7].reshape(d)] for lst in _adamw_small(sm_w, sm_grads, sm_m, sm_v, "adamw_small")]

    def by_weight(meta_, attn_, w_in_, fox_, ret_, w_out_, ffn_, w_up_, cw_, cb_, w_down_, final_):
        return (meta_, attn_, w_in_, fox_, ret_, w_out_, ffn_, w_up_, cw_[None], cb_, w_down_, final_)

    grads_out = by_weight(g_meta_loc, g_attn, g_w_in[None], g_fox_b, g_ret, g_w_out[None], g_ffn, g_w_up[None], g_cw_loc,
                          g_conv_b, g_w_down[None], g_final.reshape(d))
    delta_out = by_weight(dl[0], dl[1], d_w_in, dl[2], dl[3], d_w_out, dl[4], d_w_up, dl[5], dl[6], d_w_down, dl[7])
    m_out = by_weight(ml[0], ml[1], m_w_in_n, ml[2], ml[3], m_w_out_n, ml[4], m_w_up_n, ml[5], ml[6], m_w_down_n, ml[7])
    v_out = by_weight(vl[0], vl[1], v_w_in_n, vl[2], vl[3], v_w_out_n, vl[4], v_w_up_n, vl[5], vl[6], v_w_down_n, vl[7])
    return (loss, grad_x[None]) + grads_out + delta_out + m_out + v_out
```

```python
import numpy as np
import jax
import jax.numpy as jnp
from jax import lax
from jax.experimental import pallas as pl
from jax.experimental.pallas import tpu as pltpu

F32 = jnp.float32
BF16 = jnp.bfloat16

D_MODEL = 1024
N_META = 16
N_PAD = 112
PREFIX = 128
RET_HEADS = 4
RET_DK = 64
RET_DV = 128
FOX_HEADS = 8
FOX_DH = 64
D_FF = 2816
ROPE_BASE = 10000.0
EPS = 1e-6
NEG = -1e30
RET_QK = RET_HEADS * RET_DK
RET_V = RET_HEADS * RET_DV
FOX_W = FOX_HEADS * FOX_DH
IN_WIDTH = 2 * RET_QK + 2 * RET_V + 3 * FOX_W + FOX_HEADS
IN_PAD = 3200
FF_COL_BLOCK = (IN_WIDTH - FOX_HEADS) // 128
QK_SCALE = 0.125

ADAM_LR = 0.001
ADAM_B1 = 0.9
ADAM_B2 = 0.999
ADAM_EPS = 1e-08
ADAM_WD = 0.01
ADAM_STEP = 10

N_DEV = 8
LANE = 128
ROW_TILE = 128
TOK_TILE = 384

NN = (((1,), (0,)), ((), ()))
NT = (((1,), (1,)), ((), ()))
TN = (((0,), (0,)), ((), ()))


def _pcall(body, **kw):
    return pl.pallas_call(body, **kw)


def _params(*sem):
    return pltpu.CompilerParams(dimension_semantics=sem)


def _dot(a, b, dims=NN):
    return lax.dot_general(a, b, dims, preferred_element_type=F32)


def _sigmoid(x):
    return 0.5 * jnp.tanh(0.5 * x) + 0.5


def _matmul(a, b, *, mode, grid, a_spec, b_spec, o_spec, out_shape, name, add=None, add_spec=None, after=None):
    dims = {"nn": NN, "nt": NT, "tn": TN}[mode]
    nk = grid[2]
    has_add = add is not None
    a_list, b_list = (list(a), list(b)) if isinstance(a, (list, tuple)) else ([a], [b])
    a_specs, b_specs = (list(a_spec), list(b_spec)) if isinstance(a_spec, (list, tuple)) else ([a_spec], [b_spec])
    nt = len(a_list)
    n_in = 2 * nt + int(has_add) + int(after is not None)

    def body(*refs):
        a_refs, b_refs = refs[:nt], refs[nt:2 * nt]
        add_ref = refs[2 * nt] if has_add else None
        o_ref = refs[n_in]
        part = _dot(a_refs[0][...].astype(BF16), b_refs[0][...].astype(BF16), dims)
        for ar, br in zip(a_refs[1:], b_refs[1:]):
            part = part + _dot(ar[...].astype(BF16), br[...].astype(BF16), dims)

        def finish(acc):
            if has_add:
                acc = acc + add_ref[...]
            o_ref[...] = acc.astype(o_ref.dtype)

        if nk == 1:
            finish(part)
        else:
            acc_ref = refs[-1]
            k = pl.program_id(2)

            @pl.when(k == 0)
            def _():
                acc_ref[...] = part

            @pl.when(k > 0)
            def _():
                acc_ref[...] += part

            @pl.when(k == nk - 1)
            def _():
                finish(acc_ref[...])

    in_specs = a_specs + b_specs + ([add_spec] if has_add else [])
    args = tuple(a_list) + tuple(b_list) + ((add,) if has_add else ())
    if after is not None:
        in_specs, args = in_specs + [pl.BlockSpec(memory_space=pl.ANY)], args + (after,)
    scratch = [] if nk == 1 else [pltpu.VMEM(tuple(d for d in o_spec.block_shape if d is not None), F32)]
    return _pcall(
        body, name=name, grid=grid, in_specs=in_specs, out_specs=o_spec, out_shape=out_shape,
        scratch_shapes=scratch, compiler_params=_params("parallel", "parallel", "arbitrary"),
    )(*args)


def _mm_simple(a, b, *, mode, tm, tn, tk, out_dtype, name, add=None, after=None):
    if mode == "tn":
        K, M = a.shape
    else:
        M, K = a.shape
    N = b.shape[0] if mode == "nt" else b.shape[1]
    grid = (M // tm, N // tn, K // tk)
    resident = dict(pipeline_mode=pl.Buffered(1)) if (tn == N and tk == K) else {}
    a_spec = pl.BlockSpec((tk, tm), lambda i, j, k: (k, i)) if mode == "tn" else pl.BlockSpec((tm, tk), lambda i, j, k: (i, k))
    b_spec = (pl.BlockSpec((tn, tk), lambda i, j, k: (j, k), **resident) if mode == "nt"
              else pl.BlockSpec((tk, tn), lambda i, j, k: (k, j), **resident))
    o_spec = pl.BlockSpec((tm, tn), lambda i, j, k: (i, j))
    return _matmul(a, b, mode=mode, grid=grid, a_spec=a_spec, b_spec=b_spec, o_spec=o_spec,
                   out_shape=jax.ShapeDtypeStruct((M, N), out_dtype), name=name, add=add,
                   add_spec=o_spec if add is not None else None, after=after)


def _matmul_rows(a_list, a_specs, b_list, b_specs, extras, extra_specs, out_specs, out_shape, epilogue, *,
                 mode, steps, name, after=None, scratch=()):
    dims = {"nn": NN, "nt": NT}[mode]
    nt, ne = len(a_list), len(extras)
    n_in = 2 * nt + ne + int(after is not None)

    def body(*refs):
        acc = _dot(refs[0][...].astype(BF16), refs[nt][...].astype(BF16), dims)
        for k in range(1, nt):
            acc = acc + _dot(refs[k][...].astype(BF16), refs[nt + k][...].astype(BF16), dims)
        epilogue(pl.program_id(0), acc, refs[2 * nt:2 * nt + ne], refs[n_in:])

    in_specs = list(a_specs) + list(b_specs) + list(extra_specs)
    args = tuple(a_list) + tuple(b_list) + tuple(extras)
    if after is not None:
        in_specs, args = in_specs + [pl.BlockSpec(memory_space=pl.ANY)], args + (after,)
    return _pcall(body, name=name, grid=(steps,), in_specs=in_specs, out_specs=out_specs, out_shape=out_shape,
                  scratch_shapes=list(scratch), compiler_params=_params("arbitrary"))(*args)


def _rms_bwd_tile(dy, x, gain, dres):
    r = lax.rsqrt(jnp.mean(x * x, axis=-1, keepdims=True) + EPS)
    xhat = x * r
    u = dy * gain
    return dres + r * (u - xhat * jnp.mean(u * xhat, axis=-1, keepdims=True)), jnp.sum(dy * xhat, axis=0, keepdims=True)


def _loss_tile(i, x, tgt, gain):
    d = x.shape[-1]
    r = lax.rsqrt(jnp.mean(x * x, axis=-1, keepdims=True) + EPS)
    xhat = x * r
    counted = (i * TOK_TILE + lax.broadcasted_iota(jnp.int32, (TOK_TILE, 1), 0)) >= PREFIX
    err = jnp.where(counted, xhat * gain - tgt, 0.0)
    dy = err * (1.0 / d)
    u = dy * gain
    dh = r * (u - xhat * jnp.mean(u * xhat, axis=-1, keepdims=True))
    return 0.5 * jnp.sum(jnp.mean(err * err, axis=-1, keepdims=True)), dh, jnp.sum(dy * xhat, axis=0, keepdims=True)


def _accumulate(ref, i, part):
    @pl.when(i == 0)
    def _():
        ref[...] = part

    @pl.when(i > 0)
    def _():
        ref[...] += part


def _prep_norm(x, meta, gain, name):
    seq, d = x.shape
    t = seq + PREFIX

    def body(xa_ref, xb_ref, xc_ref, meta_ref, g_ref, h_ref, n_ref):
        i = pl.program_id(0)

        @pl.when(i == 0)
        def _():
            h_ref[0:N_PAD, :] = jnp.zeros((N_PAD, d), F32)
            h_ref[N_PAD:ROW_TILE, :] = meta_ref[...]

        @pl.when(i > 0)
        def _():
            h_ref[0:ROW_TILE, :] = xa_ref[...]

        h_ref[ROW_TILE:2 * ROW_TILE, :] = xb_ref[...]
        h_ref[2 * ROW_TILE:3 * ROW_TILE, :] = xc_ref[...]
        h = h_ref[...]
        r = lax.rsqrt(jnp.mean(h * h, axis=-1, keepdims=True) + EPS)
        n_ref[...] = (h * r * g_ref[...]).astype(BF16)

    return _pcall(
        body, name=name, grid=(t // TOK_TILE,),
        in_specs=_shifted_row_specs(d) + [pl.BlockSpec((N_META, d), lambda i: (0, 0)), pl.BlockSpec((1, d), lambda i: (0, 0))],
        out_specs=[pl.BlockSpec((TOK_TILE, d), lambda i: (i, 0)), pl.BlockSpec((TOK_TILE, d), lambda i: (i, 0))],
        out_shape=[jax.ShapeDtypeStruct((t, d), F32), jax.ShapeDtypeStruct((t, d), BF16)],
        compiler_params=_params("parallel"),
    )(x, x, x, meta, gain)


def _shifted_row_specs(d):
    blocks_per_tile = TOK_TILE // ROW_TILE
    return [pl.BlockSpec((ROW_TILE, d), lambda i, r=r: (jnp.maximum(blocks_per_tile * i + r, 0), 0)) for r in (-1, 0, 1)]


def _ret_consts(bk):
    gam = 1.0 - 2.0 ** (-5.0 - np.arange(RET_HEADS))
    n = np.arange(bk)
    same_or_earlier_chunk = (n[None, :] // 64) <= (n[:, None] // 64)
    w = gam[:, None, None] ** np.abs(n[:, None] - n[None, :])[None] * same_or_earlier_chunk[None]
    wq = gam[:, None] ** (n[None, :] + 1.0)
    wk = gam[:, None] ** (bk - 1.0 - n[None, :])
    mask = (np.arange(RET_QK)[None, :] // RET_DK) == np.arange(RET_HEADS)[:, None]
    return (jnp.asarray(w, F32), jnp.asarray(wq[:, :, None], F32), jnp.asarray(wk[:, :, None], F32),
            jnp.asarray(mask[:, None, :], F32), [float(g ** bk) for g in gam])


def _rope_tables(t):
    half = RET_DK // 2
    inv = 1.0 / (ROPE_BASE ** (jnp.arange(half, dtype=F32) / half))
    ang = jnp.arange(t).astype(F32)[:, None] * inv[None, :]
    cos, sin = jnp.cos(ang), jnp.sin(ang)
    return (jnp.tile(jnp.concatenate([cos, cos], axis=1), (1, RET_HEADS)),
            jnp.tile(jnp.concatenate([-sin, sin], axis=1), (1, RET_HEADS)))


def _swap_halves(x):
    outs = []
    for s in range(x.shape[1] // LANE):
        xs = x[:, LANE * s:LANE * (s + 1)]
        lane = lax.broadcasted_iota(jnp.int32, xs.shape, 1)
        outs.append(jnp.where((lane & 32) == 0, pltpu.roll(xs, LANE - 32, axis=1), pltpu.roll(xs, 32, axis=1)))
    return outs[0] if len(outs) == 1 else jnp.concatenate(outs, axis=1)


def _rope(x, cos, sin_signed):
    return x * cos + _swap_halves(x) * sin_signed


def _rope_t(dx, cos, sin_signed):
    return dx * cos + _swap_halves(dx * sin_signed)


def _ret_fwd(proj, cos, sin, gain, name):
    t = proj.shape[0]
    bk = TOK_TILE
    nb = t // bk
    w, wq, wk, mask, g_blk = _ret_consts(bk)

    def body(q_ref, k_ref, v_ref, rg_ref, cos_ref, sin_ref, w_ref, wq_ref, wk_ref, mask_ref, gain_ref,
             opre_ref, og_ref, st_ref, r_ref):
        i = pl.program_id(0)

        @pl.when(i == 0)
        def _():
            r_ref[...] = jnp.zeros_like(r_ref)

        c, s = cos_ref[...], sin_ref[...]
        valid = ((i * bk + lax.broadcasted_iota(jnp.int32, (bk, 1), 0)) >= N_PAD).astype(F32)
        qr = _rope(q_ref[...], c, s)
        kr = _rope(k_ref[...], c, s) * QK_SCALE * valid
        kb = kr.astype(BF16)
        for h in range(RET_HEADS):
            hm = mask_ref[h]
            cols = slice(RET_DV * h, RET_DV * (h + 1))
            vh = v_ref[:, cols].astype(BF16)
            r_prev = r_ref[h]
            st_ref[0, h] = r_prev
            sm = _dot((qr * hm).astype(BF16), kb, NT) * w_ref[h]
            o = _dot(sm.astype(BF16), vh) + _dot((qr * (hm * wq_ref[h])).astype(BF16), r_prev.astype(BF16))
            r_ref[h] = g_blk[h] * r_prev + _dot((kr * wk_ref[h]).astype(BF16), vh, TN)
            opre_ref[:, cols] = o
            rstd = lax.rsqrt(jnp.mean(o * o, axis=-1, keepdims=True) + EPS)
            rg = rg_ref[:, cols]
            og_ref[:, cols] = (o * rstd * gain_ref[:, cols] * (rg * _sigmoid(rg))).astype(BF16)

    full = lambda shape: pl.BlockSpec(shape, lambda i: (0,) * len(shape))
    return _pcall(
        body, name=name, grid=(nb,),
        in_specs=[pl.BlockSpec((bk, RET_QK), lambda i: (i, 0)), pl.BlockSpec((bk, RET_QK), lambda i: (i, 1)),
                  pl.BlockSpec((bk, RET_V), lambda i: (i, 1)), pl.BlockSpec((bk, RET_V), lambda i: (i, 2)),
                  pl.BlockSpec((bk, RET_QK), lambda i: (i, 0)), pl.BlockSpec((bk, RET_QK), lambda i: (i, 0)),
                  full((RET_HEADS, bk, bk)), full((RET_HEADS, bk, 1)), full((RET_HEADS, bk, 1)),
                  full((RET_HEADS, 1, RET_QK)), full((1, RET_V))],
        out_specs=[pl.BlockSpec((bk, RET_V), lambda i: (i, 0)), pl.BlockSpec((bk, RET_V), lambda i: (i, 0)),
                   pl.BlockSpec((1, RET_HEADS, RET_QK, RET_DV), lambda i: (i, 0, 0, 0))],
        out_shape=[jax.ShapeDtypeStruct((t, RET_V), F32), jax.ShapeDtypeStruct((t, RET_V + FOX_W), BF16),
                   jax.ShapeDtypeStruct((nb, RET_HEADS, RET_QK, RET_DV), F32)],
        scratch_shapes=[pltpu.VMEM((RET_HEADS, RET_QK, RET_DV), F32)],
        compiler_params=_params("arbitrary"),
    )(proj, proj, proj, proj, cos, sin, w, wq, wk, mask, gain)


def _ret_bwd(proj, cos, sin, gain, dmixed, opre, states, name):
    t = proj.shape[0]
    bk = TOK_TILE
    nb = t // bk
    w, wq, wk, mask, g_blk = _ret_consts(bk)
    v0, g0 = 2 * RET_QK, 2 * RET_QK + RET_V

    def body(q_ref, k_ref, v_ref, rg_ref, cos_ref, sin_ref, w_ref, wq_ref, wk_ref, mask_ref, gain_ref,
             dog_ref, opre_ref, st_ref, dp_ref, gg_ref, dr_ref):
        step = pl.program_id(0)
        i = nb - 1 - step

        @pl.when(step == 0)
        def _():
            dr_ref[...] = jnp.zeros_like(dr_ref)
            gg_ref[...] = jnp.zeros_like(gg_ref)

        c, s = cos_ref[...], sin_ref[...]
        valid = ((i * bk + lax.broadcasted_iota(jnp.int32, (bk, 1), 0)) >= N_PAD).astype(F32)
        qr = _rope(q_ref[...], c, s)
        kr = _rope(k_ref[...], c, s) * QK_SCALE * valid
        kb = kr.astype(BF16)
        dqr = jnp.zeros((bk, RET_QK), F32)
        dkr = jnp.zeros((bk, RET_QK), F32)
        for h in range(RET_HEADS):
            hm = mask_ref[h]
            cols = slice(RET_DV * h, RET_DV * (h + 1))
            vh = v_ref[:, cols].astype(BF16)
            o = opre_ref[:, cols]
            rstd = lax.rsqrt(jnp.mean(o * o, axis=-1, keepdims=True) + EPS)
            xhat = o * rstd
            rg = rg_ref[:, cols]
            sg = _sigmoid(rg)
            gate = rg * sg
            gn = gain_ref[:, cols]
            dog = dog_ref[:, cols]
            dp_ref[:, g0 + RET_DV * h:g0 + RET_DV * (h + 1)] = (
                dog * xhat * gn * (sg * (1.0 + rg * (1.0 - sg)))).astype(BF16)
            gg_ref[:, cols] += jnp.sum(dog * xhat * gate, axis=0, keepdims=True)
            dxh = dog * gn * gate
            do = (rstd * (dxh - xhat * jnp.mean(dxh * xhat, axis=-1, keepdims=True))).astype(BF16)
            qm = (qr * hm).astype(BF16)
            qw = (qr * (hm * wq_ref[h])).astype(BF16)
            kw = (kr * wk_ref[h]).astype(BF16)
            wh = w_ref[h]
            sm = (_dot(qm, kb, NT) * wh).astype(BF16)
            ds = (_dot(do, vh, NT) * wh).astype(BF16)
            dr = dr_ref[h]
            drb = dr.astype(BF16)
            dp_ref[:, v0 + RET_DV * h:v0 + RET_DV * (h + 1)] = (_dot(sm, do, TN) + _dot(kw, drb)).astype(BF16)
            dqr = dqr + _dot(ds, kb) * hm + _dot(do, st_ref[0, h].astype(BF16), NT) * (hm * wq_ref[h])
            dkr = dkr + _dot(ds, qm, TN) + _dot(vh, drb, NT) * wk_ref[h]
            dr_ref[h] = g_blk[h] * dr + _dot(qw, do, TN)
        dp_ref[:, 0:RET_QK] = _rope_t(dqr, c, s).astype(BF16)
        dp_ref[:, RET_QK:2 * RET_QK] = _rope_t(dkr * (QK_SCALE * valid), c, s).astype(BF16)

    full = lambda shape: pl.BlockSpec(shape, lambda i: (0,) * len(shape))
    rev = lambda col: (lambda i: (nb - 1 - i, col))
    return _pcall(
        body, name=name, grid=(nb,),
        in_specs=[pl.BlockSpec((bk, RET_QK), rev(0)), pl.BlockSpec((bk, RET_QK), rev(1)),
                  pl.BlockSpec((bk, RET_V), rev(1)), pl.BlockSpec((bk, RET_V), rev(2)),
                  pl.BlockSpec((bk, RET_QK), rev(0)), pl.BlockSpec((bk, RET_QK), rev(0)),
                  full((RET_HEADS, bk, bk)), full((RET_HEADS, bk, 1)), full((RET_HEADS, bk, 1)),
                  full((RET_HEADS, 1, RET_QK)), full((1, RET_V)),
                  pl.BlockSpec((bk, RET_V), rev(0)), pl.BlockSpec((bk, RET_V), rev(0)),
                  pl.BlockSpec((1, RET_HEADS, RET_QK, RET_DV), lambda i: (nb - 1 - i, 0, 0, 0))],
        out_specs=[pl.BlockSpec((bk, g0 + RET_V), rev(0)), pl.BlockSpec((1, RET_V), lambda i: (0, 0))],
        out_shape=[jax.ShapeDtypeStruct((t, IN_PAD), BF16), jax.ShapeDtypeStruct((1, RET_V), F32)],
        scratch_shapes=[pltpu.VMEM((RET_HEADS, RET_QK, RET_DV), F32)],
        compiler_params=_params("arbitrary"),
    )(proj, proj, proj, proj, cos, sin, w, wq, wk, mask, gain, dmixed, opre, states)


def _forget_cumsum(proj, bias, name):
    t = proj.shape[0]
    rt = TOK_TILE
    nb = t // rt
    tril = jnp.asarray(np.tril(np.ones((rt, rt))), F32)

    def body(z_ref, b_ref, tril_ref, c_ref, carry_ref):
        i = pl.program_id(0)

        @pl.when(i == 0)
        def _():
            carry_ref[...] = jnp.zeros_like(carry_ref)

        z = z_ref[...] + b_ref[...]
        logf = jnp.minimum(z, 0.0) - jnp.log(1.0 + jnp.exp(-jnp.abs(z)))
        c = lax.dot_general(tril_ref[...], logf, NN, precision=lax.Precision.HIGHEST,
                            preferred_element_type=F32) + carry_ref[...]
        c_ref[...] = c
        carry_ref[...] = c[rt - 1:rt, :]

    return _pcall(
        body, name=name, grid=(nb,),
        in_specs=[pl.BlockSpec((rt, LANE), lambda i: (i, FF_COL_BLOCK)), pl.BlockSpec((1, LANE), lambda i: (0, 0)),
                  pl.BlockSpec((rt, rt), lambda i: (0, 0))],
        out_specs=pl.BlockSpec((rt, LANE), lambda i: (i, 0)),
        out_shape=jax.ShapeDtypeStruct((t, LANE), F32),
        scratch_shapes=[pltpu.VMEM((1, LANE), F32)],
        compiler_params=_params("arbitrary"),
    )(proj, bias, tril)


def _forget_cumsum_bwd(proj, bias, drs, dcs, dproj, name):
    t = proj.shape[0]
    rt = TOK_TILE
    nb = t // rt
    triu = jnp.asarray(np.triu(np.ones((rt, rt))), F32)

    def body(z_ref, b_ref, triu_ref, drs_ref, dcs_ref, dproj_in, dz_ref, gb_ref, carry_ref):
        step = pl.program_id(0)

        @pl.when(step == 0)
        def _():
            carry_ref[...] = jnp.zeros_like(carry_ref)
            gb_ref[...] = jnp.zeros_like(gb_ref)

        dlogf = lax.dot_general(triu_ref[...], drs_ref[...] - dcs_ref[...], NN, precision=lax.Precision.HIGHEST,
                                preferred_element_type=F32) + carry_ref[...]
        carry_ref[...] = dlogf[0:1, :]
        z = z_ref[...] + b_ref[...]
        is_head = lax.broadcasted_iota(jnp.int32, (rt, LANE), 1) < FOX_HEADS
        dz = jnp.where(is_head, dlogf / (1.0 + jnp.exp(z)), 0.0)
        dz_ref[...] = dz.astype(BF16)
        gb_ref[...] += jnp.sum(dz, axis=0, keepdims=True)

    return _pcall(
        body, name=name, grid=(nb,),
        in_specs=[pl.BlockSpec((rt, LANE), lambda i: (nb - 1 - i, FF_COL_BLOCK)),
                  pl.BlockSpec((1, LANE), lambda i: (0, 0)),
                  pl.BlockSpec((rt, rt), lambda i: (0, 0)),
                  pl.BlockSpec((rt, LANE), lambda i: (nb - 1 - i, 0)),
                  pl.BlockSpec((rt, LANE), lambda i: (nb - 1 - i, 0)),
                  pl.BlockSpec(memory_space=pl.ANY)],
        out_specs=[pl.BlockSpec((rt, LANE), lambda i: (nb - 1 - i, FF_COL_BLOCK)),
                   pl.BlockSpec((1, LANE), lambda i: (0, 0))],
        out_shape=[jax.ShapeDtypeStruct(dproj.shape, BF16), jax.ShapeDtypeStruct((1, LANE), F32)],
        input_output_aliases={5: 0},
        scratch_shapes=[pltpu.VMEM((1, LANE), F32)],
        compiler_params=_params("arbitrary"),
    )(proj, bias, triu, drs, dcs, dproj)


FOX_PAIRS = FOX_HEADS // 2
L_ONE_Q = FOX_DH
L_ONE_K = FOX_DH + 3
L_LSE = FOX_DH + 4


def _split3(x):
    hi = x.astype(BF16).astype(F32)
    r = x - hi
    mid = r.astype(BF16).astype(F32)
    return hi, mid, r - mid


def _head_to_low(slab, e):
    return slab if e == 0 else pltpu.roll(slab, FOX_DH, axis=1)


def _pair(a, b, low):
    return jnp.where(low, a, pltpu.roll(b, FOX_DH, axis=1))


def _fox_prep(proj, c, name):
    t = proj.shape[0]
    tq = TOK_TILE

    def body(p_ref, c_ref, qa_ref, ka_ref, va_ref, qt_ref, vt_ref):
        i = pl.program_id(0)
        lane = lax.broadcasted_iota(jnp.int32, (tq, LANE), 1)
        low = lane < FOX_DH
        live = (i * tq + lax.broadcasted_iota(jnp.int32, (tq, 1), 0)) >= N_PAD
        q_tail = jnp.where(lane < L_ONE_Q + 3, 1.0, 0.0)
        k_ones = (lane >= L_ONE_K) & (lane < L_ONE_K + 4)
        v_tail = jnp.where(lane < FOX_DH + 2, 1.0, 0.0)
        bias_parts = _split3(jnp.where(live, -c_ref[...], NEG))
        for pair in range(FOX_PAIRS):
            base = 3 * LANE * pair
            for e in range(2):
                h = 2 * pair + e
                q = _head_to_low(p_ref[:, base:base + LANE], e)
                k = _head_to_low(p_ref[:, base + LANE:base + 2 * LANE], e)
                v = _head_to_low(p_ref[:, base + 2 * LANE:base + 3 * LANE], e)
                hi, mid, lo = [part[:, h:h + 1] for part in bias_parts]
                ka = jnp.where(low, k, jnp.where(k_ones, 1.0, 0.0))
                ka = jnp.where(lane == L_ONE_Q, hi, jnp.where(lane == L_ONE_Q + 1, mid, jnp.where(lane == L_ONE_Q + 2, lo, ka)))
                qa = jnp.where(low, q * QK_SCALE, q_tail)
                va = jnp.where(low, v, v_tail)
                qa_ref[h] = qa.astype(BF16)
                ka_ref[h] = ka.astype(BF16)
                va_ref[h] = va.astype(BF16)
                qt_ref[h] = qa.T.astype(BF16)
                vt_ref[h] = va.T.astype(BF16)

    out = jax.ShapeDtypeStruct((FOX_HEADS, t, LANE), BF16)
    out_t = jax.ShapeDtypeStruct((FOX_HEADS, t // tq, LANE, tq), BF16)
    ospec = pl.BlockSpec((FOX_HEADS, tq, LANE), lambda i: (0, i, 0))
    tspec = pl.BlockSpec((FOX_HEADS, None, LANE, tq), lambda i: (0, i, 0, 0))
    return _pcall(
        body, name=name, grid=(t // tq,),
        in_specs=[pl.BlockSpec((tq, 3 * FOX_W), lambda i: (i, 1)), pl.BlockSpec((tq, LANE), lambda i: (i, 0))],
        out_specs=[ospec, ospec, ospec, tspec, tspec], out_shape=[out, out, out, out_t, out_t],
        compiler_params=_params("parallel"),
    )(proj, c)


STEP_PAIRS = 2
STEP_HEADS = 2 * STEP_PAIRS
FOX_GROUPS = FOX_PAIRS // STEP_PAIRS
FWD_PAIRS = 4
FWD_HEADS = 2 * FWD_PAIRS
FWD_GROUPS = FOX_PAIRS // FWD_PAIRS


def _fox_fwd(qt, ka, vt, mixed, name):
    nh, nq, tq, _ = ka.shape
    t = nq * tq

    def body(qt_ref, ka_ref, vt_ref, mixed_in, mixed_ref, o_ref, lse_ref):
        i = pl.program_id(1)
        lane = lax.broadcasted_iota(jnp.int32, (tq, LANE), 1)
        key_le_query = lax.broadcasted_iota(jnp.int32, (tq, tq), 0) <= lax.broadcasted_iota(jnp.int32, (tq, tq), 1)

        def logits(j):
            return [_dot(ka_ref[h, j], qt_ref[h]) for h in range(FWD_HEADS)]

        def update(j, scores, carry, diagonal):
            new = []
            for h in range(FWD_HEADS):
                m, acc = carry[h]
                s = jnp.where(key_le_query, scores[h], NEG) if diagonal else scores[h]
                m_new = jnp.maximum(m, jnp.max(s, axis=0, keepdims=True))
                p = jnp.exp(s - m_new).astype(BF16)
                new.append((m_new, jnp.exp(m - m_new) * acc + _dot(vt_ref[h, j], p)))
            return tuple(new)

        init = tuple((jnp.full((1, tq), NEG, F32), jnp.zeros((LANE, tq), F32)) for _ in range(FWD_HEADS))
        carry = lax.fori_loop(0, i, lambda j, cr: update(j, logits(j), cr, False), init)
        outs, lse_rows = [], []
        for m, acc in update(i, logits(i), carry, True):
            l = acc[FOX_DH:FOX_DH + 1, :]
            outs.append((acc / l).T)
            lse_rows.append(m + jnp.log(l))
        lse_rows.append(jnp.zeros((LANE - FWD_HEADS, tq), F32))
        o_all = jnp.concatenate([_pair(outs[2 * c], outs[2 * c + 1], lane < FOX_DH) for c in range(FWD_PAIRS)], axis=1)
        mixed_ref[...] = o_all.astype(BF16)
        o_ref[...] = o_all
        lse_ref[...] = jnp.concatenate(lse_rows, axis=0).T

    width = FWD_PAIRS * LANE
    whole = pl.BlockSpec((FWD_HEADS, nq, tq, LANE), lambda g, i: (g, 0, 0, 0), pipeline_mode=pl.Buffered(1))
    whole_t = pl.BlockSpec((FWD_HEADS, nq, LANE, tq), lambda g, i: (g, 0, 0, 0), pipeline_mode=pl.Buffered(1))
    return _pcall(
        body, name=name, grid=(FWD_GROUPS, nq),
        in_specs=[pl.BlockSpec((FWD_HEADS, None, LANE, tq), lambda g, i: (g, i, 0, 0)), whole, whole_t,
                  pl.BlockSpec(memory_space=pl.ANY)],
        out_specs=[pl.BlockSpec((tq, width), lambda g, i: (i, RET_V // width + g)),
                   pl.BlockSpec((tq, width), lambda g, i: (i, g)),
                   pl.BlockSpec((None, tq, LANE), lambda g, i: (g, i, 0))],
        out_shape=[jax.ShapeDtypeStruct(mixed.shape, BF16), jax.ShapeDtypeStruct((t, FOX_W), F32),
                   jax.ShapeDtypeStruct((FWD_GROUPS, t, LANE), F32)],
        input_output_aliases={3: 0},
        compiler_params=_params("parallel", "parallel"),
    )(qt, ka, vt, mixed)


def _fox_prep_bwd(dmixed, o_fox, lse, qa, name):
    t = dmixed.shape[0]
    tq = TOK_TILE

    def body(dm_ref, o_ref, lse_ref, qa_ref, qab_ref, doa_ref):
        i = pl.program_id(0)
        lane = lax.broadcasted_iota(jnp.int32, (tq, LANE), 1)
        low = lane < FOX_DH
        live = (i * tq + lax.broadcasted_iota(jnp.int32, (tq, 1), 0)) >= N_PAD
        lse_parts = [_split3(jnp.where(live, -lse_ref[grp], 0.0)) for grp in range(FWD_GROUPS)]
        for pair in range(FOX_PAIRS):
            cols = slice(LANE * pair, LANE * (pair + 1))
            d_slab = dm_ref[:, cols]
            prod = d_slab * o_ref[:, cols]
            for e in range(2):
                h = 2 * pair + e
                nd = -jnp.sum(jnp.where(low, _head_to_low(prod, e), 0.0), axis=-1, keepdims=True)
                nd_hi = nd.astype(BF16).astype(F32)
                doa = jnp.where(low, _head_to_low(d_slab, e), 0.0)
                doa = jnp.where(lane == FOX_DH, nd_hi, jnp.where(lane == FOX_DH + 1, nd - nd_hi, doa))
                doa_ref[h] = doa.astype(BF16)
                lane_h = h % FWD_HEADS
                hi, mid, lo = [part[:, lane_h:lane_h + 1] for part in lse_parts[h // FWD_HEADS]]
                qab = qa_ref[h].astype(F32)
                qab = jnp.where(lane == L_LSE, hi, jnp.where(lane == L_LSE + 1, mid, jnp.where(lane == L_LSE + 2, lo, qab)))
                qab_ref[h] = qab.astype(BF16)

    out = jax.ShapeDtypeStruct((FOX_HEADS, t, LANE), BF16)
    hspec = pl.BlockSpec((FOX_HEADS, tq, LANE), lambda i: (0, i, 0))
    return _pcall(
        body, name=name, grid=(t // tq,),
        in_specs=[pl.BlockSpec((tq, FOX_W), lambda i: (i, 1)), pl.BlockSpec((tq, FOX_W), lambda i: (i, 0)),
                  pl.BlockSpec((FWD_GROUPS, tq, LANE), lambda i: (0, i, 0)), hspec],
        out_specs=[hspec, hspec], out_shape=[out, out],
        compiler_params=_params("parallel"),
    )(dmixed, o_fox, lse, qa)


def _fox_bwd(qab, doa, ka, va, dproj, name):
    nh, nq, tq, _ = qab.shape
    t = nq * tq
    slab = 3 * LANE * STEP_PAIRS
    group0 = (2 * RET_QK + 2 * RET_V) // slab

    def body(qab_ref, doa_ref, ka_ref, va_ref, dproj_in, dp_ref, drs_ref, dcs_ref, dq_ref):
        g, j = pl.program_id(0), pl.program_id(1)

        @pl.when((g == 0) & (j == 0))
        def _():
            drs_ref[...] = jnp.zeros_like(drs_ref)
            dcs_ref[...] = jnp.zeros_like(dcs_ref)

        @pl.when(j == 0)
        def _():
            dq_ref[...] = jnp.zeros_like(dq_ref)

        lane = lax.broadcasted_iota(jnp.int32, (tq, LANE), 1)
        low = lane < FOX_DH
        key_le_query = lax.broadcasted_iota(jnp.int32, (tq, tq), 0) <= lax.broadcasted_iota(jnp.int32, (tq, tq), 1)

        def by_head(c, a, b, col):
            h = STEP_HEADS * g + 2 * c
            return jnp.where(lane == h, a[:, col:col + 1], jnp.where(lane == h + 1, b[:, col:col + 1], 0.0))


        def step(i, carry, diagonal):
            st = [_dot(ka_ref[h], qab_ref[h, i], NT) for h in range(STEP_HEADS)]
            dpt = [_dot(va_ref[h], doa_ref[h, i], NT) for h in range(STEP_HEADS)]
            new = []
            for h in range(STEP_HEADS):
                p = jnp.exp(st[h])
                if diagonal:
                    p = jnp.where(key_le_query, p, 0.0)
                ds = (p * dpt[h]).astype(BF16)
                dq_ref[h, i] += _dot(ds, ka_ref[h], TN)
                dk, dv = carry[h]
                new.append((dk + _dot(ds, qab_ref[h, i]), dv + _dot(p.astype(BF16), doa_ref[h, i])))
            return tuple(new)

        zero = jnp.zeros((tq, LANE), F32)
        carry = step(j, tuple((zero, zero) for _ in range(STEP_HEADS)), True)
        carry = lax.fori_loop(j + 1, nq, lambda i, cr: step(i, cr, False), carry)
        rows = pl.ds(pl.multiple_of(j * tq, tq), tq)
        for c in range(STEP_PAIRS):
            (dka, dva), (dkb, dvb) = carry[2 * c], carry[2 * c + 1]
            c0 = 3 * LANE * c
            dp_ref[rows, c0 + LANE:c0 + 2 * LANE] = _pair(dka, dkb, low).astype(BF16)
            dp_ref[rows, c0 + 2 * LANE:c0 + 3 * LANE] = _pair(dva, dvb, low).astype(BF16)
            dcs_ref[rows, :] += by_head(c, dka, dkb, L_ONE_Q)

        @pl.when(j == nq - 1)
        def _():
            for c in range(STEP_PAIRS):
                for blk in range(nq):
                    r = slice(blk * tq, (blk + 1) * tq)
                    a, b = dq_ref[2 * c, blk], dq_ref[2 * c + 1, blk]
                    dp_ref[r, 3 * LANE * c:3 * LANE * c + LANE] = (_pair(a, b, low) * QK_SCALE).astype(BF16)
                    drs_ref[r, :] += by_head(c, a, b, L_ONE_K)

    whole = pl.BlockSpec((STEP_HEADS, nq, tq, LANE), lambda g, j: (g, 0, 0, 0), pipeline_mode=pl.Buffered(1))
    blk = pl.BlockSpec((STEP_HEADS, None, tq, LANE), lambda g, j: (g, j, 0, 0))
    sums = pl.BlockSpec((t, LANE), lambda g, j: (0, 0), pipeline_mode=pl.Buffered(1))
    return _pcall(
        body, name=name, grid=(FOX_GROUPS, nq),
        in_specs=[whole, whole, blk, blk, pl.BlockSpec(memory_space=pl.ANY)],
        out_specs=[pl.BlockSpec((t, slab), lambda g, j: (0, group0 + g)), sums, sums],
        out_shape=[jax.ShapeDtypeStruct(dproj.shape, BF16), jax.ShapeDtypeStruct((t, LANE), F32),
                   jax.ShapeDtypeStruct((t, LANE), F32)],
        input_output_aliases={4: 0},
        scratch_shapes=[pltpu.VMEM((STEP_HEADS, nq, tq, LANE), F32)],
        compiler_params=_params("arbitrary", "arbitrary"),
    )(qab, doa, ka, va, dproj)


HALO = 8


def _rows_ext(ref, r0, rows, t, before, after):
    lo, hi = r0 - before, r0 + rows + after
    width = ref.shape[-1]
    parts = []
    if lo < 0:
        parts.append(jnp.zeros((-lo, width), F32))
    parts.append(ref[max(lo, 0):min(hi, t), :].astype(F32))
    if hi > t:
        parts.append(jnp.zeros((hi - t, width), F32))
    return parts[0] if len(parts) == 1 else jnp.concatenate(parts, axis=0)


def _conv_taps(a_ext, r0_ext, cw_ref, cb_ref):
    n = a_ext.shape[0]
    if r0_ext < N_PAD:
        row = r0_ext + lax.broadcasted_iota(jnp.int32, (n, 1), 0)
        a_ext = jnp.where(row >= N_PAD, a_ext, 0.0)
    a1 = pltpu.roll(a_ext, 1, axis=0)
    a2 = pltpu.roll(a_ext, 2, axis=0)
    acc = cb_ref[...] + a2 * cw_ref[0:1, :] + a1 * cw_ref[1:2, :] + a_ext * cw_ref[2:3, :]
    return a_ext, a1, a2, acc


FF_COLS = 256


def _up_conv_fwd(n2, w_up_t, conv_w8, conv_b, name):
    t, d = n2.shape
    f = w_up_t.shape[1]
    rows = TOK_TILE
    starts = list(range(0, t, rows))

    def body(n_ref, wa_ref, wb_ref, cw_ref, cb_ref, up_ref, g_ref):
        def project(r0):
            n_rows = n_ref[r0:r0 + rows, :]
            up_ref[0, r0:r0 + rows, :] = _dot(n_rows, wa_ref[...], NT)
            up_ref[1, r0:r0 + rows, :] = _dot(n_rows, wb_ref[...], NT)

        def activate(r0):
            a_ext = _rows_ext(up_ref.at[0], r0, rows, t, HALO, 0)
            _, _, _, acc = _conv_taps(a_ext, r0 - HALO, cw_ref, cb_ref)
            acc = acc[HALO:, :]
            g_ref[r0:r0 + rows, :] = (acc * _sigmoid(acc) * up_ref[1, r0:r0 + rows, :]).astype(BF16)

        project(starts[0])
        for r0, r_next in zip(starts, starts[1:] + [None]):
            if r_next is not None:
                project(r_next)
            activate(r0)

    return _pcall(
        body, name=name, grid=(f // FF_COLS,),
        in_specs=[pl.BlockSpec((t, d), lambda j: (0, 0), pipeline_mode=pl.Buffered(1)),
                  pl.BlockSpec((None, FF_COLS, d), lambda j: (0, j, 0)), pl.BlockSpec((None, FF_COLS, d), lambda j: (1, j, 0)),
                  pl.BlockSpec((8, FF_COLS), lambda j: (0, j)), pl.BlockSpec((1, FF_COLS), lambda j: (0, j))],
        out_specs=[pl.BlockSpec((2, t, FF_COLS), lambda j: (0, 0, j)), pl.BlockSpec((t, FF_COLS), lambda j: (0, j))],
        out_shape=[jax.ShapeDtypeStruct((2, t, f), F32), jax.ShapeDtypeStruct((t, f), BF16)],
        compiler_params=_params("parallel"),
    )(n2, w_up_t, w_up_t, conv_w8, conv_b)


def _dg_conv_bwd(up, conv_w8, conv_b, dh2, w_down, name):
    _, t, f = up.shape
    d = dh2.shape[1]
    rows = TOK_TILE
    starts = list(range(0, t, rows))

    def body(a_ref, b_ref, cw_ref, cb_ref, dh_ref, wd_ref, dup_ref, gcw_ref, gcb_ref, dg_ref):
        def project(r0):
            dg_ref[r0:r0 + rows, :] = _dot(dh_ref[r0:r0 + rows, :], wd_ref[...], NT)

        gw = [jnp.zeros((1, FF_COLS), F32) for _ in range(3)]
        gb = jnp.zeros((1, FF_COLS), F32)
        project(starts[0])
        for r0, r_next in zip(starts, starts[1:] + [None]):
            if r_next is not None:
                project(r_next)
            a_ext = _rows_ext(a_ref, r0, rows, t, HALO, HALO)
            b_ext = _rows_ext(b_ref, r0, rows, t, HALO, HALO)
            dg_ext = _rows_ext(dg_ref, r0, rows, t, HALO, HALO)
            a0, a1, a2, acc = _conv_taps(a_ext, r0 - HALO, cw_ref, cb_ref)
            sg = _sigmoid(acc)
            dacc = dg_ext * b_ext * (sg * (1.0 + acc * (1.0 - sg)))
            n = dacc.shape[0]
            da = (dacc * cw_ref[2:3, :] + pltpu.roll(dacc, n - 1, axis=0) * cw_ref[1:2, :]
                  + pltpu.roll(dacc, n - 2, axis=0) * cw_ref[0:1, :])
            core = slice(HALO, HALO + rows)
            da = da[core, :]
            if r0 < N_PAD:
                row = r0 + lax.broadcasted_iota(jnp.int32, (rows, 1), 0)
                da = jnp.where(row >= N_PAD, da, 0.0)
            dup_ref[0, r0:r0 + rows, :] = da.astype(BF16)
            dup_ref[1, r0:r0 + rows, :] = (dg_ext * acc * sg)[core, :].astype(BF16)
            dacc_c = dacc[core, :]
            gw[0] = gw[0] + jnp.sum(dacc_c * a2[core, :], axis=0, keepdims=True)
            gw[1] = gw[1] + jnp.sum(dacc_c * a1[core, :], axis=0, keepdims=True)
            gw[2] = gw[2] + jnp.sum(dacc_c * a0[core, :], axis=0, keepdims=True)
            gb = gb + jnp.sum(dacc_c, axis=0, keepdims=True)
        gcw_ref[...] = jnp.zeros((8, FF_COLS), F32)
        for tap in range(3):
            gcw_ref[tap:tap + 1, :] = gw[tap]
        gcb_ref[...] = gb

    return _pcall(
        body, name=name, grid=(f // FF_COLS,),
        in_specs=[pl.BlockSpec((None, t, FF_COLS), lambda j: (0, 0, j)), pl.BlockSpec((None, t, FF_COLS), lambda j: (1, 0, j)),
                  pl.BlockSpec((8, FF_COLS), lambda j: (0, j)), pl.BlockSpec((1, FF_COLS), lambda j: (0, j)),
                  pl.BlockSpec((t, d), lambda j: (0, 0), pipeline_mode=pl.Buffered(1)),
                  pl.BlockSpec((FF_COLS, d), lambda j: (j, 0))],
        out_specs=[pl.BlockSpec((2, t, FF_COLS), lambda j: (0, 0, j)), pl.BlockSpec((8, FF_COLS), lambda j: (0, j)),
                   pl.BlockSpec((1, FF_COLS), lambda j: (0, j))],
        out_shape=[jax.ShapeDtypeStruct((2, t, f), BF16), jax.ShapeDtypeStruct((8, f), F32),
                   jax.ShapeDtypeStruct((1, f), F32)],
        scratch_shapes=[pltpu.VMEM((t, FF_COLS), F32)],
        compiler_params=_params("parallel"),
    )(up, up, conv_w8, conv_b, dh2, w_down)


def _exchange(arrays, kinds, name, after=None):
    n = len(arrays)
    npeer = N_DEV - 1
    n_in = n + int(after is not None)

    def body(*refs):
        ins, outs = refs[:n], refs[n_in:n_in + n]
        send_sems, recv_sems, local_sems = refs[n_in + n:]
        x, y, c = lax.axis_index("x"), lax.axis_index("y"), lax.axis_index("c")
        me = 4 * x + 2 * y + c
        copies, locals_ = [], []
        for a in range(n):
            gather = kinds[a] == "gather"
            own = pltpu.make_async_copy(ins[a] if gather else ins[a].at[me], outs[a].at[me], local_sems.at[a])
            own.start()
            locals_.append(own)
            for d in range(1, N_DEV):
                px = 1 - x if d & 4 else x
                py = 1 - y if d & 2 else y
                pc = 1 - c if d & 1 else c
                src = ins[a] if gather else ins[a].at[4 * px + 2 * py + pc]
                cp = pltpu.make_async_remote_copy(
                    src_ref=src, dst_ref=outs[a].at[me],
                    send_sem=send_sems.at[a * npeer + d - 1], recv_sem=recv_sems.at[a * npeer + d - 1],
                    device_id=(px, py, pc), device_id_type=pl.DeviceIdType.MESH)
                cp.start()
                copies.append(cp)
        for cp in copies:
            cp.wait_recv()
        for cp in copies:
            cp.wait_send()
        for own in locals_:
            own.wait()

    out_shape = [jax.ShapeDtypeStruct((N_DEV,) + (a.shape if k == "gather" else a.shape[1:]), a.dtype)
                 for a, k in zip(arrays, kinds)]
    return _pcall(
        body, name=name,
        in_specs=[pl.BlockSpec(memory_space=pl.ANY)] * n_in,
        out_specs=[pl.BlockSpec(memory_space=pl.ANY)] * n,
        out_shape=out_shape,
        scratch_shapes=[pltpu.SemaphoreType.DMA((n * npeer,)), pltpu.SemaphoreType.DMA((n * npeer,)),
                        pltpu.SemaphoreType.DMA((n,))],
        compiler_params=pltpu.CompilerParams(has_side_effects=True),
    )(*arrays, *([] if after is None else [after]))


ALL_PEERS = tuple(range(1, N_DEV))
SAME_CORE_AND_SIBLING = (1, 2, 4, 6)
OTHER_CHIPS = (2, 4, 6)


def _peer_copies(srcs, lands, kinds, send_sems, recv_sems, relations=ALL_PEERS):
    x, y, c = lax.axis_index("x"), lax.axis_index("y"), lax.axis_index("c")
    me = 4 * x + 2 * y + c
    copies = []
    for a in range(len(srcs)):
        for d in relations:
            px = 1 - x if d & 4 else x
            py = 1 - y if d & 2 else y
            pc = 1 - c if d & 1 else c
            peer = 4 * px + 2 * py + pc
            k = a * (N_DEV - 1) + d - 1
            if kinds[a] == "forward":
                src, dst, target = lands[a].at[peer], lands[a].at[peer], (x, y, 1 - c)
            else:
                src, dst, target = (srcs[a] if kinds[a] == "gather" else srcs[a].at[peer]), lands[a].at[me], (px, py, pc)
            copies.append(pltpu.make_async_remote_copy(
                src_ref=src, dst_ref=dst, send_sem=send_sems.at[k], recv_sem=recv_sems.at[k],
                device_id=target, device_id_type=pl.DeviceIdType.MESH))
    return copies


def _own_copies(srcs, lands, kinds, sems):
    me = 4 * lax.axis_index("x") + 2 * lax.axis_index("y") + lax.axis_index("c")
    first = len(srcs) * (N_DEV - 1)
    return [pltpu.make_async_copy(srcs[a].at[me] if kinds[a] == "scatter" else srcs[a], lands[a].at[me], sems.at[first + a])
            for a in range(len(srcs))]


def _exchange_start(arrays, kinds, name, after=None, relations=ALL_PEERS, lands=None, own=True):
    n = len(arrays)
    nsem = n * (N_DEV - 1) + n
    hbm = pl.BlockSpec(memory_space=pltpu.HBM)
    sem = pl.BlockSpec(memory_space=pltpu.SEMAPHORE)
    land_shapes = ([l.shape for l in lands] if lands is not None else
                   [(N_DEV,) + (a.shape if k == "gather" else a.shape[1:]) for a, k in zip(arrays, kinds)])

    n_in = 2 * n + int(after is not None)

    def body(*refs):
        srcs, land_refs = refs[:n], refs[n:2 * n]
        send_sems, recv_sems = refs[n_in], refs[n_in + 1]
        token = refs[-1]
        for cp in _peer_copies(srcs, land_refs, kinds, send_sems, recv_sems, relations):
            cp.start()
        for cp in _own_copies(srcs, land_refs, kinds, send_sems) if own else []:
            cp.start()
        token[...] = jnp.zeros_like(token)

    operands = [pltpu.with_memory_space_constraint(a, pltpu.HBM) for a in arrays]
    operands += (list(lands) if lands is not None else
                 [pltpu.with_memory_space_constraint(lax.empty(s, a.dtype), pltpu.HBM) for s, a in zip(land_shapes, arrays)])
    operands += [] if after is None else [after]
    out = _pcall(
        body, name=name,
        in_specs=[hbm] * (2 * n) + ([] if after is None else [pl.BlockSpec(memory_space=pl.ANY)]),
        out_specs=[sem, sem] + [hbm] * (2 * n) + [pl.BlockSpec(memory_space=pltpu.VMEM)],
        out_shape=[pltpu.SemaphoreType.DMA((nsem,)), pltpu.SemaphoreType.DMA((nsem,))]
        + [pltpu.HBM(a.shape, a.dtype) for a in arrays]
        + [pltpu.HBM(s, a.dtype) for s, a in zip(land_shapes, arrays)]
        + [jax.ShapeDtypeStruct((8, LANE), F32)],
        input_output_aliases={k: 2 + k for k in range(2 * n)},
        compiler_params=pltpu.CompilerParams(has_side_effects=pltpu.SideEffectType.DATAFLOW_SIDE_EFFECTING),
    )(*operands)
    return out[0], out[1], list(out[2:2 + n]), list(out[2 + n:2 + 2 * n]), out[-1]


def _exchange_wait(started, kinds, after, name, own=True, relations=ALL_PEERS, with_sources=False):
    send_sems, recv_sems, srcs, lands, _ = started
    n = len(srcs)
    hbm = pl.BlockSpec(memory_space=pltpu.HBM)
    sem = pl.BlockSpec(memory_space=pltpu.SEMAPHORE)

    def body(*refs):
        src_refs, land_refs = refs[:n], refs[n:2 * n]
        copies = _peer_copies(src_refs, land_refs, kinds, refs[2 * n], refs[2 * n + 1], relations)
        for cp in copies:
            cp.wait_send()
        for cp in copies:
            cp.wait_recv()
        for cp in _own_copies(src_refs, land_refs, kinds, refs[2 * n]) if own else []:
            cp.wait()

    out = _pcall(
        body, name=name,
        in_specs=[hbm] * (2 * n) + [sem, sem, pl.BlockSpec(memory_space=pl.ANY)],
        out_specs=[hbm] * (2 * n),
        out_shape=[pltpu.HBM(a.shape, a.dtype) for a in srcs + lands],
        input_output_aliases={k: k for k in range(2 * n)},
        compiler_params=pltpu.CompilerParams(has_side_effects=pltpu.SideEffectType.DATAFLOW_SIDE_EFFECTING),
    )(*srcs, *lands, send_sems, recv_sems, after)
    return (list(out[:n]), list(out[n:])) if with_sources else list(out[n:])


def _sum_slots(slots, name, rows_tile):
    nd, r, c = slots.shape

    def body(s_ref, o_ref):
        acc = s_ref[0].astype(F32)
        for p in range(1, nd):
            acc = acc + s_ref[p].astype(F32)
        o_ref[...] = acc

    return _pcall(
        body, name=name, grid=(r // rows_tile,),
        in_specs=[pl.BlockSpec((nd, rows_tile, c), lambda i: (0, i, 0))],
        out_specs=pl.BlockSpec((rows_tile, c), lambda i: (i, 0)),
        out_shape=jax.ShapeDtypeStruct((r, c), F32),
        compiler_params=_params("parallel"),
    )(slots)


def _sum_slots_small(slot_arrays, own_arrays, name):
    n = len(slot_arrays)

    def body(*refs):
        me = 4 * lax.axis_index("x") + 2 * lax.axis_index("y") + lax.axis_index("c")
        for s_ref, own_ref, o_ref in zip(refs[:n], refs[n:2 * n], refs[2 * n:]):
            acc = jnp.where(me == 0, own_ref[...], s_ref[0])
            for p in range(1, s_ref.shape[0]):
                acc = acc + jnp.where(me == p, own_ref[...], s_ref[p])
            o_ref[...] = acc

    return _pcall(body, name=name, out_shape=[jax.ShapeDtypeStruct(a.shape[1:], F32) for a in slot_arrays])(
        *slot_arrays, *own_arrays)


def _adamw_values(w, gr, m, v):
    nm = ADAM_B1 * m + (1.0 - ADAM_B1) * gr
    nv = ADAM_B2 * v + (1.0 - ADAM_B2) * (gr * gr)
    m_hat = nm / (1.0 - ADAM_B1 ** ADAM_STEP)
    v_hat = nv / (1.0 - ADAM_B2 ** ADAM_STEP)
    return -ADAM_LR * (m_hat / (jnp.sqrt(v_hat) + ADAM_EPS) + ADAM_WD * w), nm, nv


def _adamw_update(w_ref, g_ref, m_ref, v_ref, d_ref, nm_ref, nv_ref):
    d_ref[...], nm_ref[...], nv_ref[...] = _adamw_values(w_ref[...], g_ref[...], m_ref[...], v_ref[...])


def _adamw_from_slots(w, slots, m, v, name, cols_tile=256):
    rows, cols = w.shape
    nd, rows_pad, _ = slots.shape

    def body(w_ref, s_ref, m_ref, v_ref, g_ref, d_ref, nm_ref, nv_ref):
        gr = s_ref[0, 0:rows, :].astype(F32)
        for p in range(1, nd):
            gr = gr + s_ref[p, 0:rows, :].astype(F32)
        g_ref[...] = gr
        d_ref[...], nm_ref[...], nv_ref[...] = _adamw_values(w_ref[...], gr, m_ref[...], v_ref[...])

    spec = pl.BlockSpec((rows, cols_tile), lambda i: (0, i))
    slot_spec = pl.BlockSpec((nd, rows_pad, cols_tile), lambda i: (0, 0, i))
    return _pcall(
        body, name=name, grid=(cols // cols_tile,), in_specs=[spec, slot_spec, spec, spec], out_specs=[spec] * 4,
        out_shape=[jax.ShapeDtypeStruct((rows, cols), F32)] * 4, compiler_params=_params("parallel"),
    )(w, slots, m, v)


def _adamw_small(ws, gs, ms, vs, name):
    n = len(ws)

    def body(*refs):
        ins, outs = refs[:4 * n], refs[4 * n:]
        for k in range(n):
            _adamw_update(ins[k], ins[n + k], ins[2 * n + k], ins[3 * n + k], outs[k], outs[n + k], outs[2 * n + k])

    shapes = [jax.ShapeDtypeStruct(w.shape, F32) for w in ws]
    out = _pcall(body, name=name, out_shape=shapes * 3)(*ws, *gs, *ms, *vs)
    return list(out[:n]), list(out[n:2 * n]), list(out[2 * n:])


def _adamw(w, g, m, v, name, rows_tile):
    r, c = w.shape
    body = lambda *refs: _adamw_update(*refs)
    spec = pl.BlockSpec((rows_tile, c), lambda i: (i, 0))
    shp = jax.ShapeDtypeStruct((r, c), F32)
    return _pcall(
        body, name=name, grid=(r // rows_tile,), in_specs=[spec] * 4, out_specs=[spec] * 3, out_shape=[shp] * 3,
        compiler_params=_params("parallel"),
    )(w, g, m, v)


F0 = 2 * RET_QK + 2 * RET_V


def _to_internal_rows(w_t):
    cols = w_t.shape[1]
    fox = w_t[F0:F0 + 3 * FOX_W].reshape(3, FOX_PAIRS, LANE, cols).transpose(1, 0, 2, 3).reshape(3 * FOX_W, cols)
    tail = jnp.zeros((IN_PAD - IN_WIDTH, cols), w_t.dtype)
    return jnp.concatenate([w_t[:F0], fox, w_t[F0 + 3 * FOX_W:], tail], axis=0)


def _from_internal_rows(g_t):
    cols = g_t.shape[1]
    fox = g_t[F0:F0 + 3 * FOX_W].reshape(FOX_PAIRS, 3, LANE, cols).transpose(1, 0, 2, 3).reshape(3 * FOX_W, cols)
    return jnp.concatenate([g_t[:F0], fox, g_t[F0 + 3 * FOX_W:F0 + 3 * FOX_W + FOX_HEADS]], axis=0)


IN_BLOCK = IN_WIDTH // N_DEV
IN_BLOCK_PAD = 400
BF16_ROWS = 16


def _slot_row_of_internal():
    rows = np.arange(IN_WIDTH)
    fox = rows[F0:F0 + 3 * FOX_W].reshape(3, FOX_PAIRS, LANE).transpose(1, 0, 2).reshape(-1)
    original = np.concatenate([rows[:F0], fox, rows[F0 + 3 * FOX_W:]])
    slot_rows = original // IN_BLOCK * IN_BLOCK_PAD + original % IN_BLOCK
    return np.concatenate([slot_rows, np.full(IN_PAD - IN_WIDTH, -1)])


def _internal_row_of_slot():
    forward = _slot_row_of_internal()
    back = np.full(N_DEV * IN_BLOCK_PAD, -1)
    back[forward[forward >= 0]] = np.nonzero(forward >= 0)[0]
    return back


def _row_runs(src_of_dst):
    tiles = []
    for t0 in range(0, len(src_of_dst), LANE):
        runs = []
        for o in range(LANE):
            s = int(src_of_dst[t0 + o])
            if s < 0:
                continue
            if runs and runs[-1][0] + runs[-1][2] == o and runs[-1][1] + runs[-1][2] == s:
                runs[-1][2] += 1
            else:
                runs.append([o, s, 1])
        tiles.append(runs)
    return tiles


def _move_rows(src, src_of_dst, name, cols_tile=256):
    n_src, cols = src.shape
    tiles = _row_runs(src_of_dst)

    def body(s_ref, o_ref):
        for t, runs in enumerate(tiles):
            rows = pl.ds(t * LANE, LANE)
            if not runs:
                o_ref[rows, :] = jnp.zeros((LANE, cols_tile), o_ref.dtype)
                continue
            if len(runs) == 1 and runs[0][0] == 0 and runs[0][2] == LANE and runs[0][1] % BF16_ROWS == 0:
                o_ref[rows, :] = s_ref[pl.ds(runs[0][1], LANE), :]
                continue
            acc = None
            for o0, s0, n in runs:
                w0 = s0 // BF16_ROWS * BF16_ROWS
                width = -(-(s0 - w0 + n) // LANE) * LANE
                w0 = min(w0, n_src - width)
                i = lax.broadcasted_iota(jnp.int32, (LANE, width), 0)
                j = lax.broadcasted_iota(jnp.int32, (LANE, width), 1)
                pick = ((j - i == s0 - w0 - o0) & (i >= o0) & (i < o0 + n)).astype(src.dtype)
                part = _dot(pick, s_ref[pl.ds(w0, width), :])
                acc = part if acc is None else acc + part
            o_ref[rows, :] = acc.astype(o_ref.dtype)

    return _pcall(
        body, name=name, grid=(cols // cols_tile,),
        in_specs=[pl.BlockSpec((n_src, cols_tile), lambda i: (0, i))],
        out_specs=pl.BlockSpec((len(src_of_dst), cols_tile), lambda i: (0, i)),
        out_shape=jax.ShapeDtypeStruct((len(src_of_dst), cols), src.dtype), compiler_params=_params("parallel"),
    )(src)


def _local_step(x, target, meta, attn_g, fox_b, ret_g, ffn_g, conv_w8, conv_b, final_g,
                first_weight, late_weights, ffn_grads_ready, out_grad_ready, in_grad_ready):
    seq, d = x.shape
    t = seq + PREFIX
    tm = TOK_TILE
    nq = t // tm
    fox_b128 = jnp.pad(fox_b, ((0, 0), (0, LANE - FOX_HEADS)))

    h0, n1 = _prep_norm(x, meta, attn_g, "prep_norm")
    w_in_t = first_weight(n1)
    proj = _mm_simple(n1, w_in_t, mode="nt", tm=tm, tn=IN_PAD, tk=d, out_dtype=F32, name="mm_in")
    cos, sin = _rope_tables(t)
    o_pre, mixed, states = _ret_fwd(proj, cos, sin, ret_g, "ret_fwd")
    c = _forget_cumsum(proj, fox_b128, "forget_cumsum")
    qa, ka, va, qt, vt = _fox_prep(proj, c, "fox_prep")
    by_block = lambda a: a.reshape(FOX_HEADS, nq, tm, LANE)
    mixed, o_fox, lse = _fox_fwd(qt, by_block(ka), vt, mixed, "fox_fwd")
    w_out, w_up_t, w_down = late_weights(o_fox)
    tile = pl.BlockSpec((tm, d), lambda i: (i, 0))
    row_vec = pl.BlockSpec((1, d), lambda i: (0, 0))
    resident = lambda shape: pl.BlockSpec(shape, lambda i: (0,) * len(shape), pipeline_mode=pl.Buffered(1))
    acts = lambda dtype: jax.ShapeDtypeStruct((t, d), dtype)
    vec = jax.ShapeDtypeStruct((1, d), F32)

    def residual_and_norm(i, acc, ins, outs):
        h = acc + ins[0][...]
        outs[0][...] = h
        outs[1][...] = (h * lax.rsqrt(jnp.mean(h * h, axis=-1, keepdims=True) + EPS) * ins[1][...]).astype(BF16)

    h1, n2 = _matmul_rows([mixed], [tile], [w_out], [resident((d, d))], [h0, ffn_g], [tile, row_vec],
                          [tile, tile], [acts(F32), acts(BF16)], residual_and_norm, mode="nn", steps=nq, name="mm_out_norm")
    nf = D_FF // 1408
    up, g = _up_conv_fwd(n2, w_up_t, conv_w8, conv_b, "up_conv_fwd")

    def residual_loss_bwd(i, acc, ins, outs):
        loss_ref, dh_ref, dhb_ref, gg_ref = outs
        part, dh, gg = _loss_tile(i, acc + ins[0][...], jnp.concatenate([ins[1][...], ins[2][...], ins[3][...]], axis=0),
                                  ins[4][...])
        _accumulate(loss_ref, i, jnp.broadcast_to(part, loss_ref.shape))
        dh_ref[...] = dh
        dhb_ref[...] = dh.astype(BF16)
        _accumulate(gg_ref, i, gg)

    loss_tile, dh2, dh2_b, g_final = _matmul_rows(
        [g], [pl.BlockSpec((tm, D_FF), lambda i: (i, 0))], [w_down], [resident((D_FF, d))],
        [h1, target, target, target, final_g], [tile] + _shifted_row_specs(d) + [row_vec],
        [pl.BlockSpec((8, LANE), lambda i: (0, 0)), tile, tile, row_vec],
        [jax.ShapeDtypeStruct((8, LANE), F32), acts(F32), acts(BF16), vec], residual_loss_bwd,
        mode="nn", steps=nq, name="mm_down_loss")

    tkw = 2112 if t % 2112 == 0 else tm
    gw_down = _mm_simple(g, dh2_b, mode="tn", tm=1408, tn=d, tk=tkw, out_dtype=BF16, name="mm_gw_down")
    dup, g_conv_w8, g_conv_b = _dg_conv_bwd(up, conv_w8, conv_b, dh2_b, w_down, "dg_conv_bwd")

    half = lambda p: pl.BlockSpec((None, tm, D_FF), lambda i: (p, i, 0))
    half_w = lambda p: pl.BlockSpec((None, D_FF, d), lambda i: (p, 0, 0), pipeline_mode=pl.Buffered(1))
    gw_up_t = _matmul(
        dup, n2, mode="tn", grid=(2 * nf, 1, t // tkw),
        a_spec=pl.BlockSpec((None, tkw, 1408), lambda i, j, k: (i // nf, k, i % nf)),
        b_spec=pl.BlockSpec((tkw, d), lambda i, j, k: (k, 0)),
        o_spec=pl.BlockSpec((1408, d), lambda i, j, k: (i, 0)),
        out_shape=jax.ShapeDtypeStruct((2 * D_FF, d), BF16), name="mm_gw_up")
    def norm_bwd_and_mixer_grad(i, acc, ins, outs):
        dh, gg = _rms_bwd_tile(acc, ins[0][...], ins[1][...], ins[2][...])
        outs[0][...] = dh
        _accumulate(outs[1], i, gg)
        outs[2][...] = _dot(dh.astype(BF16), ins[3][...], NT)

    dh1, g_ffn, dmixed = _matmul_rows(
        [dup, dup], [half(0), half(1)], [w_up_t, w_up_t], [half_w(0), half_w(1)],
        [h1, ffn_g, dh2, w_out], [tile, row_vec, tile, resident((d, d))], [tile, row_vec, tile],
        [acts(F32), vec, acts(F32)], norm_bwd_and_mixer_grad,
        mode="nn", steps=nq, name="mm_dn2_norm_bwd", after=ffn_grads_ready(gw_down, gw_up_t))
    gw_out = _mm_simple(mixed, dh1, mode="tn", tm=d, tn=d, tk=tkw, out_dtype=BF16, name="mm_gw_out")
    dproj, g_ret = _ret_bwd(proj, cos, sin, ret_g + out_grad_ready(gw_out), dmixed, o_pre, states, "ret_bwd")
    qab, doa = _fox_prep_bwd(dmixed, o_fox, lse, qa, "fox_prep_bwd")
    dproj, drs, dcs = _fox_bwd(by_block(qab), by_block(doa), by_block(ka), by_block(va), dproj, "fox_bwd")
    dproj, g_fox_b = _forget_cumsum_bwd(proj, fox_b128, drs, dcs, dproj, "forget_cumsum_bwd")
    gw_in_t = _mm_simple(dproj, n1, mode="tn", tm=640, tn=d, tk=tkw, out_dtype=BF16, name="mm_gw_in")
    sent = in_grad_ready(gw_in_t)
    def input_grads(i, acc, ins, outs):
        gx_ref, gmeta_ref, gg_ref, buf_ref, sems = outs
        dh, gg = _rms_bwd_tile(acc, ins[0][...], ins[1][...], ins[2][...])
        _accumulate(gg_ref, i, gg)
        slot = i % 2

        def first_copy():
            return pltpu.make_async_copy(buf_ref.at[0, pl.ds(PREFIX, tm - PREFIX)], gx_ref.at[pl.ds(0, tm - PREFIX)],
                                         sems.at[0])

        def tile_copy(tile, buf_slot):
            rows = pl.ds(pl.multiple_of(tile * tm - PREFIX, PREFIX), tm)
            return pltpu.make_async_copy(buf_ref.at[buf_slot], gx_ref.at[rows], sems.at[buf_slot])

        @pl.when(i == 1)
        def _():
            first_copy().wait()

        @pl.when(i >= 2)
        def _():
            tile_copy(i - 1, 1 - slot).wait()

        buf_ref[slot] = dh

        @pl.when(i == 0)
        def _():
            gmeta_ref[...] = dh[N_PAD:PREFIX, :]
            first_copy().start()

        @pl.when(i > 0)
        def _():
            tile_copy(i, slot).start()

        @pl.when(i == nq - 1)
        def _():
            tile_copy(i, slot).wait()

    grad_x, g_meta, g_attn = _matmul_rows(
        [dproj], [pl.BlockSpec((tm, IN_PAD), lambda i: (i, 0))], [w_in_t], [resident((IN_PAD, d))],
        [h0, attn_g, dh1], [tile, row_vec, tile],
        [pl.BlockSpec(memory_space=pl.ANY), pl.BlockSpec((N_META, d), lambda i: (0, 0)), row_vec],
        [jax.ShapeDtypeStruct((seq, d), F32), jax.ShapeDtypeStruct((N_META, d), F32), vec], input_grads,
        mode="nn", steps=nq, name="mm_dn1_norm_bwd", after=sent,
        scratch=[pltpu.VMEM((2, tm, d), F32), pltpu.SemaphoreType.DMA((2,))])

    grads = dict(meta=g_meta, attn_g=g_attn, fox_b=g_fox_b, ret_g=g_ret,
                 ffn_g=g_ffn, conv_w=g_conv_w8, conv_b=g_conv_b, final_g=g_final)
    return loss_tile, grad_x, grads


def kernel(x, meta_tokens, attn_norm_g, w_in, fox_forget_b, ret_norm_g, w_out, ffn_norm_g, w_up, conv_w, conv_b, w_down, final_norm_g, loss_target, m_meta_tokens, m_attn_norm_g, m_w_in, m_fox_forget_b, m_ret_norm_g, m_w_out, m_ffn_norm_g, m_w_up, m_conv_w, m_conv_b, m_w_down, m_final_norm_g, v_meta_tokens, v_attn_norm_g, v_w_in, v_fox_forget_b, v_ret_norm_g, v_w_out, v_ffn_norm_g, v_w_up, v_conv_w, v_conv_b, v_w_down, v_final_norm_g):
    d = D_MODEL
    me = 4 * lax.axis_index("x") + 2 * lax.axis_index("y") + lax.axis_index("c")
    in_blk, in_blk_pad = IN_BLOCK, IN_BLOCK_PAD
    up_blk = 2 * D_FF // N_DEV
    down_blk = D_FF // N_DEV
    cw_blk = D_FF // N_DEV

    w_in_loc = jnp.pad(w_in[0].T.astype(BF16), ((0, in_blk_pad - in_blk), (0, 0)))
    cw_loc = jnp.pad(conv_w[0], ((0, 5), (0, 384 - cw_blk)))
    g_meta, g_cw = _exchange([meta_tokens, cw_loc], ["gather"] * 2, "gather_small")
    first = _exchange_start([w_in_loc], ["gather"], "gather_in_start", after=g_meta, relations=SAME_CORE_AND_SIBLING)
    rest_loc = [(w_out[0] + first[-1][0:1, 0:1]).astype(BF16), w_up[0].T.astype(BF16), w_down[0].astype(BF16)]
    rest = _exchange_start(rest_loc, ["gather"] * 3, "gather_rest_start")
    meta_f = g_meta.transpose(1, 0, 2).reshape(N_META, d)
    conv_w8 = jnp.pad(g_cw[:, :3, :cw_blk].transpose(1, 0, 2).reshape(3, D_FF), ((0, 5), (0, 0)))
    pending = {}

    def first_weight(after):
        own_in, landed = _exchange_wait(first, ["gather"], after, "gather_in_wait", relations=SAME_CORE_AND_SIBLING,
                                        with_sources=True)
        onward = _exchange_start(own_in, ["forward"], "gather_in_forward_start", relations=OTHER_CHIPS, lands=landed,
                                 own=False)
        (g_in,) = _exchange_wait(onward, ["forward"], onward[-1], "gather_in_forward_wait", own=False,
                                 relations=OTHER_CHIPS)
        return _move_rows(g_in.reshape(IN_PAD, d), _slot_row_of_internal(), "w_in_rows")

    def in_grad_ready(gw_in_t):
        blocks = _move_rows(gw_in_t, _internal_row_of_slot(), "gw_in_rows").reshape(N_DEV, in_blk_pad, d)
        pending["in"] = _exchange_start([blocks], ["scatter"], "grads_in_start")
        return pending["in"][-1][0:1, 0:1]

    def late_weights(after):
        g_out, g_up, g_down = _exchange_wait(rest, ["gather"] * 3, after, "gather_rest_wait")
        return g_out.reshape(d, d), g_up.reshape(2, D_FF, d), g_down.reshape(D_FF, d)

    def ffn_grads_ready(gw_down, gw_up_t):
        blocks = [gw_down.reshape(N_DEV, down_blk, d), gw_up_t.reshape(N_DEV, up_blk, d)]
        pending["ffn"] = _exchange_start(blocks, ["scatter"] * 2, "grads_ffn_start")
        return pending["ffn"][-1][0:1, 0:1]

    def out_grad_ready(gw_out):
        pending["out"] = _exchange_start([gw_out.reshape(N_DEV, d // N_DEV, d)], ["scatter"], "grads_out_start")
        return pending["out"][-1][0:1, 0:1]

    loss_tile, grad_x, gr = _local_step(
        x[0], loss_target[0], meta_f, attn_norm_g + rest[-1][0:1, 0:1], fox_forget_b, ret_norm_g, ffn_norm_g,
        conv_w8, conv_b, final_norm_g.reshape(1, d), first_weight, late_weights, ffn_grads_ready, out_grad_ready,
        in_grad_ready)

    small = [loss_tile, gr["attn_g"], gr["fox_b"], gr["ret_g"], gr["ffn_g"], gr["conv_b"], gr["final_g"],
             gr["meta"], gr["conv_w"]]
    small_kinds = ["gather"] * len(small)
    small_started = _exchange_start(small, small_kinds, "grads_small_start", own=False)

    r_down, r_up = _exchange_wait(pending["ffn"], ["scatter"] * 2, small_started[-1], "grads_ffn_wait")
    (r_out,) = _exchange_wait(pending["out"], ["scatter"], small_started[-1], "grads_out_wait")
    g_w_out = _sum_slots(r_out, "sum_w_out", d // N_DEV)
    g_w_up_t = _sum_slots(r_up, "sum_w_up", up_blk)
    g_w_down = _sum_slots(r_down, "sum_w_down", down_blk)
    as_t = lambda a: a[0].T
    from_t = lambda a: a.T[None]
    d_w_out, m_w_out_n, v_w_out_n = [a[None] for a in _adamw(w_out[0], g_w_out, m_w_out[0], v_w_out[0], "adamw_w_out", 128)]
    up_t = _adamw(as_t(w_up), g_w_up_t, as_t(m_w_up), as_t(v_w_up), "adamw_w_up", up_blk // 2)
    d_w_up, m_w_up_n, v_w_up_n = [from_t(a) for a in up_t]
    d_w_down, m_w_down_n, v_w_down_n = [a[None] for a in _adamw(w_down[0], g_w_down, m_w_down[0], v_w_down[0],
                                                                "adamw_w_down", down_blk)]

    own_small, r_small = _exchange_wait(small_started, small_kinds, up_t[0], "grads_small_wait", own=False,
                                        with_sources=True)
    (loss_all, g_attn, g_fox_b128, g_ret, g_ffn, g_conv_b, g_final, g_meta_full, g_cw_full) = _sum_slots_small(
        r_small, own_small, "sum_small")
    loss = loss_all[0, 0]
    g_fox_b = g_fox_b128[:, :FOX_HEADS]
    g_meta_loc = lax.dynamic_slice(g_meta_full, (0, me * (d // N_DEV)), (N_META, d // N_DEV))
    g_cw_loc = lax.dynamic_slice(g_cw_full, (0, me * cw_blk), (3, cw_blk))

    (r_in,) = _exchange_wait(pending["in"], ["scatter"], r_small[0], "grads_in_wait")
    g_w_in, d_w_in, m_w_in_n, v_w_in_n = [from_t(a) for a in _adamw_from_slots(
        as_t(w_in), r_in, as_t(m_w_in), as_t(v_w_in), "adamw_w_in")]
    g_w_in, g_w_up = g_w_in[0], g_w_up_t.T
    row = lambda a: a.reshape(1, d)
    sm_grads = [g_meta_loc, g_attn, g_fox_b, g_ret, g_ffn, g_cw_loc, g_conv_b, g_final]
    sm_w = [meta_tokens, attn_norm_g, fox_forget_b, ret_norm_g, ffn_norm_g, conv_w[0], conv_b, row(final_norm_g)]
    sm_m = [m_meta_tokens, m_attn_norm_g, m_fox_forget_b, m_ret_norm_g, m_ffn_norm_g, m_conv_w[0], m_conv_b,
            row(m_final_norm_g)]
    sm_v = [v_meta_tokens, v_attn_norm_g, v_fox_forget_b, v_ret_norm_g, v_ffn_norm_g, v_conv_w[0], v_conv_b,
            row(v_final_norm_g)]
    dl, ml, vl = [lst[:7] + [lst[7].reshape(d)] for lst in _adamw_small(sm_w, sm_grads, sm_m, sm_v, "adamw_small")]

    def by_weight(meta_, attn_, w_in_, fox_, ret_, w_out_, ffn_, w_up_, cw_, cb_, w_down_, final_):
        return (meta_, attn_, w_in_, fox_, ret_, w_out_, ffn_, w_up_, cw_[None], cb_, w_down_, final_)

    grads_out = by_weight(g_meta_loc, g_attn, g_w_in[None], g_fox_b, g_ret, g_w_out[None], g_ffn, g_w_up[None], g_cw_loc,
                          g_conv_b, g_w_down[None], g_final.reshape(d))
    delta_out = by_weight(dl[0], dl[1], d_w_in, dl[2], dl[3], d_w_out, dl[4], d_w_up, dl[5], dl[6], d_w_down, dl[7])
    m_out = by_weight(ml[0], ml[1], m_w_in_n, ml[2], ml[3], m_w_out_n, ml[4], m_w_up_n, ml[5], ml[6], m_w_down_n, ml[7])
    v_out = by_weight(vl[0], vl[1], v_w_in_n, vl[2], vl[3], v_w_out_n, vl[4], v_w_up_n, vl[5], vl[6], v_w_down_n, vl[7])
    return (loss, grad_x[None]) + grads_out + delta_out + m_out + v_out
```

```python
import numpy as np
import jax
import jax.numpy as jnp
from jax import lax
from jax.experimental import pallas as pl
from jax.experimental.pallas import tpu as pltpu

F32 = jnp.float32
BF16 = jnp.bfloat16

D_MODEL = 1024
N_META = 16
N_PAD = 112
PREFIX = 128
RET_HEADS = 4
RET_DK = 64
RET_DV = 128
FOX_HEADS = 8
FOX_DH = 64
D_FF = 2816
ROPE_BASE = 10000.0
EPS = 1e-6
NEG = -1e30
RET_QK = RET_HEADS * RET_DK
RET_V = RET_HEADS * RET_DV
FOX_W = FOX_HEADS * FOX_DH
IN_WIDTH = 2 * RET_QK + 2 * RET_V + 3 * FOX_W + FOX_HEADS
IN_PAD = 3200
FF_COL_BLOCK = (IN_WIDTH - FOX_HEADS) // 128
QK_SCALE = 0.125

ADAM_LR = 0.001
ADAM_B1 = 0.9
ADAM_B2 = 0.999
ADAM_EPS = 1e-08
ADAM_WD = 0.01
ADAM_STEP = 10

N_DEV = 8
LANE = 128
ROW_TILE = 128
TOK_TILE = 384

NN = (((1,), (0,)), ((), ()))
NT = (((1,), (1,)), ((), ()))
TN = (((0,), (0,)), ((), ()))


def _pcall(body, **kw):
    return pl.pallas_call(body, **kw)


def _params(*sem):
    return pltpu.CompilerParams(dimension_semantics=sem)


def _dot(a, b, dims=NN):
    return lax.dot_general(a, b, dims, preferred_element_type=F32)


def _sigmoid(x):
    return 0.5 * jnp.tanh(0.5 * x) + 0.5


def _matmul(a, b, *, mode, grid, a_spec, b_spec, o_spec, out_shape, name, add=None, add_spec=None, after=None):
    dims = {"nn": NN, "nt": NT, "tn": TN}[mode]
    nk = grid[2]
    has_add = add is not None
    a_list, b_list = (list(a), list(b)) if isinstance(a, (list, tuple)) else ([a], [b])
    a_specs, b_specs = (list(a_spec), list(b_spec)) if isinstance(a_spec, (list, tuple)) else ([a_spec], [b_spec])
    nt = len(a_list)
    n_in = 2 * nt + int(has_add) + int(after is not None)

    def body(*refs):
        a_refs, b_refs = refs[:nt], refs[nt:2 * nt]
        add_ref = refs[2 * nt] if has_add else None
        o_ref = refs[n_in]
        part = _dot(a_refs[0][...].astype(BF16), b_refs[0][...].astype(BF16), dims)
        for ar, br in zip(a_refs[1:], b_refs[1:]):
            part = part + _dot(ar[...].astype(BF16), br[...].astype(BF16), dims)

        def finish(acc):
            if has_add:
                acc = acc + add_ref[...]
            o_ref[...] = acc.astype(o_ref.dtype)

        if nk == 1:
            finish(part)
        else:
            acc_ref = refs[-1]
            k = pl.program_id(2)

            @pl.when(k == 0)
            def _():
                acc_ref[...] = part

            @pl.when(k > 0)
            def _():
                acc_ref[...] += part

            @pl.when(k == nk - 1)
            def _():
                finish(acc_ref[...])

    in_specs = a_specs + b_specs + ([add_spec] if has_add else [])
    args = tuple(a_list) + tuple(b_list) + ((add,) if has_add else ())
    if after is not None:
        in_specs, args = in_specs + [pl.BlockSpec(memory_space=pl.ANY)], args + (after,)
    scratch = [] if nk == 1 else [pltpu.VMEM(tuple(d for d in o_spec.block_shape if d is not None), F32)]
    return _pcall(
        body, name=name, grid=grid, in_specs=in_specs, out_specs=o_spec, out_shape=out_shape,
        scratch_shapes=scratch, compiler_params=_params("parallel", "parallel", "arbitrary"),
    )(*args)


def _mm_simple(a, b, *, mode, tm, tn, tk, out_dtype, name, add=None, after=None, m_tiles=None, m_first=0):
    if mode == "tn":
        K, M = a.shape
    else:
        M, K = a.shape
    M = M if m_tiles is None else m_tiles * tm
    N = b.shape[0] if mode == "nt" else b.shape[1]
    grid = (M // tm, N // tn, K // tk)
    resident = dict(pipeline_mode=pl.Buffered(1)) if (tn == N and tk == K) else {}
    a_spec = (pl.BlockSpec((tk, tm), lambda i, j, k: (k, i + m_first)) if mode == "tn"
              else pl.BlockSpec((tm, tk), lambda i, j, k: (i + m_first, k)))
    b_spec = (pl.BlockSpec((tn, tk), lambda i, j, k: (j, k), **resident) if mode == "nt"
              else pl.BlockSpec((tk, tn), lambda i, j, k: (k, j), **resident))
    o_spec = pl.BlockSpec((tm, tn), lambda i, j, k: (i, j))
    return _matmul(a, b, mode=mode, grid=grid, a_spec=a_spec, b_spec=b_spec, o_spec=o_spec,
                   out_shape=jax.ShapeDtypeStruct((M, N), out_dtype), name=name, add=add,
                   add_spec=o_spec if add is not None else None, after=after)


def _matmul_rows(a_list, a_specs, b_list, b_specs, extras, extra_specs, out_specs, out_shape, epilogue, *,
                 mode, steps, name, after=None, scratch=()):
    dims = {"nn": NN, "nt": NT}[mode]
    nt, ne = len(a_list), len(extras)
    n_in = 2 * nt + ne + int(after is not None)

    def body(*refs):
        acc = _dot(refs[0][...].astype(BF16), refs[nt][...].astype(BF16), dims)
        for k in range(1, nt):
            acc = acc + _dot(refs[k][...].astype(BF16), refs[nt + k][...].astype(BF16), dims)
        epilogue(pl.program_id(0), acc, refs[2 * nt:2 * nt + ne], refs[n_in:])

    in_specs = list(a_specs) + list(b_specs) + list(extra_specs)
    args = tuple(a_list) + tuple(b_list) + tuple(extras)
    if after is not None:
        in_specs, args = in_specs + [pl.BlockSpec(memory_space=pl.ANY)], args + (after,)
    return _pcall(body, name=name, grid=(steps,), in_specs=in_specs, out_specs=out_specs, out_shape=out_shape,
                  scratch_shapes=list(scratch), compiler_params=_params("arbitrary"))(*args)


def _rms_bwd_tile(dy, x, gain, dres):
    r = lax.rsqrt(jnp.mean(x * x, axis=-1, keepdims=True) + EPS)
    xhat = x * r
    u = dy * gain
    return dres + r * (u - xhat * jnp.mean(u * xhat, axis=-1, keepdims=True)), jnp.sum(dy * xhat, axis=0, keepdims=True)


def _loss_tile(i, x, tgt, gain):
    d = x.shape[-1]
    r = lax.rsqrt(jnp.mean(x * x, axis=-1, keepdims=True) + EPS)
    xhat = x * r
    counted = (i * TOK_TILE + lax.broadcasted_iota(jnp.int32, (TOK_TILE, 1), 0)) >= PREFIX
    err = jnp.where(counted, xhat * gain - tgt, 0.0)
    dy = err * (1.0 / d)
    u = dy * gain
    dh = r * (u - xhat * jnp.mean(u * xhat, axis=-1, keepdims=True))
    return 0.5 * jnp.sum(jnp.mean(err * err, axis=-1, keepdims=True)), dh, jnp.sum(dy * xhat, axis=0, keepdims=True)


def _accumulate(ref, i, part):
    @pl.when(i == 0)
    def _():
        ref[...] = part

    @pl.when(i > 0)
    def _():
        ref[...] += part


def _prep_norm(x, meta, gain, name):
    seq, d = x.shape
    t = seq + PREFIX

    def body(xa_ref, xb_ref, xc_ref, meta_ref, g_ref, h_ref, n_ref):
        i = pl.program_id(0)

        @pl.when(i == 0)
        def _():
            h_ref[0:N_PAD, :] = jnp.zeros((N_PAD, d), F32)
            h_ref[N_PAD:ROW_TILE, :] = meta_ref[...]

        @pl.when(i > 0)
        def _():
            h_ref[0:ROW_TILE, :] = xa_ref[...]

        h_ref[ROW_TILE:2 * ROW_TILE, :] = xb_ref[...]
        h_ref[2 * ROW_TILE:3 * ROW_TILE, :] = xc_ref[...]
        h = h_ref[...]
        r = lax.rsqrt(jnp.mean(h * h, axis=-1, keepdims=True) + EPS)
        n_ref[...] = (h * r * g_ref[...]).astype(BF16)

    return _pcall(
        body, name=name, grid=(t // TOK_TILE,),
        in_specs=_shifted_row_specs(d) + [pl.BlockSpec((N_META, d), lambda i: (0, 0)), pl.BlockSpec((1, d), lambda i: (0, 0))],
        out_specs=[pl.BlockSpec((TOK_TILE, d), lambda i: (i, 0)), pl.BlockSpec((TOK_TILE, d), lambda i: (i, 0))],
        out_shape=[jax.ShapeDtypeStruct((t, d), F32), jax.ShapeDtypeStruct((t, d), BF16)],
        compiler_params=_params("parallel"),
    )(x, x, x, meta, gain)


def _shifted_row_specs(d):
    blocks_per_tile = TOK_TILE // ROW_TILE
    return [pl.BlockSpec((ROW_TILE, d), lambda i, r=r: (jnp.maximum(blocks_per_tile * i + r, 0), 0)) for r in (-1, 0, 1)]


def _ret_consts(bk):
    gam = 1.0 - 2.0 ** (-5.0 - np.arange(RET_HEADS))
    n = np.arange(bk)
    same_or_earlier_chunk = (n[None, :] // 64) <= (n[:, None] // 64)
    w = gam[:, None, None] ** np.abs(n[:, None] - n[None, :])[None] * same_or_earlier_chunk[None]
    wq = gam[:, None] ** (n[None, :] + 1.0)
    wk = gam[:, None] ** (bk - 1.0 - n[None, :])
    mask = (np.arange(RET_QK)[None, :] // RET_DK) == np.arange(RET_HEADS)[:, None]
    return (jnp.asarray(w, F32), jnp.asarray(wq[:, :, None], F32), jnp.asarray(wk[:, :, None], F32),
            jnp.asarray(mask[:, None, :], F32), [float(g ** bk) for g in gam])


def _rope_tables(t):
    half = RET_DK // 2
    inv = 1.0 / (ROPE_BASE ** (jnp.arange(half, dtype=F32) / half))
    ang = jnp.arange(t).astype(F32)[:, None] * inv[None, :]
    cos, sin = jnp.cos(ang), jnp.sin(ang)
    return (jnp.tile(jnp.concatenate([cos, cos], axis=1), (1, RET_HEADS)),
            jnp.tile(jnp.concatenate([-sin, sin], axis=1), (1, RET_HEADS)))


def _swap_halves(x):
    outs = []
    for s in range(x.shape[1] // LANE):
        xs = x[:, LANE * s:LANE * (s + 1)]
        lane = lax.broadcasted_iota(jnp.int32, xs.shape, 1)
        outs.append(jnp.where((lane & 32) == 0, pltpu.roll(xs, LANE - 32, axis=1), pltpu.roll(xs, 32, axis=1)))
    return outs[0] if len(outs) == 1 else jnp.concatenate(outs, axis=1)


def _rope(x, cos, sin_signed):
    return x * cos + _swap_halves(x) * sin_signed


def _rope_t(dx, cos, sin_signed):
    return dx * cos + _swap_halves(dx * sin_signed)


def _ret_fwd(proj, cos, sin, gain, name):
    t = proj.shape[0]
    bk = TOK_TILE
    nb = t // bk
    w, wq, wk, mask, g_blk = _ret_consts(bk)

    def body(q_ref, k_ref, v_ref, rg_ref, cos_ref, sin_ref, w_ref, wq_ref, wk_ref, mask_ref, gain_ref,
             opre_ref, og_ref, st_ref, r_ref):
        i = pl.program_id(0)

        @pl.when(i == 0)
        def _():
            r_ref[...] = jnp.zeros_like(r_ref)

        c, s = cos_ref[...], sin_ref[...]
        valid = ((i * bk + lax.broadcasted_iota(jnp.int32, (bk, 1), 0)) >= N_PAD).astype(F32)
        qr = _rope(q_ref[...], c, s)
        kr = _rope(k_ref[...], c, s) * QK_SCALE * valid
        kb = kr.astype(BF16)
        for h in range(RET_HEADS):
            hm = mask_ref[h]
            cols = slice(RET_DV * h, RET_DV * (h + 1))
            vh = v_ref[:, cols].astype(BF16)
            r_prev = r_ref[h]
            st_ref[0, h] = r_prev
            sm = _dot((qr * hm).astype(BF16), kb, NT) * w_ref[h]
            o = _dot(sm.astype(BF16), vh) + _dot((qr * (hm * wq_ref[h])).astype(BF16), r_prev.astype(BF16))
            r_ref[h] = g_blk[h] * r_prev + _dot((kr * wk_ref[h]).astype(BF16), vh, TN)
            opre_ref[:, cols] = o
            rstd = lax.rsqrt(jnp.mean(o * o, axis=-1, keepdims=True) + EPS)
            rg = rg_ref[:, cols]
            og_ref[:, cols] = (o * rstd * gain_ref[:, cols] * (rg * _sigmoid(rg))).astype(BF16)

    full = lambda shape: pl.BlockSpec(shape, lambda i: (0,) * len(shape))
    return _pcall(
        body, name=name, grid=(nb,),
        in_specs=[pl.BlockSpec((bk, RET_QK), lambda i: (i, 0)), pl.BlockSpec((bk, RET_QK), lambda i: (i, 1)),
                  pl.BlockSpec((bk, RET_V), lambda i: (i, 1)), pl.BlockSpec((bk, RET_V), lambda i: (i, 2)),
                  pl.BlockSpec((bk, RET_QK), lambda i: (i, 0)), pl.BlockSpec((bk, RET_QK), lambda i: (i, 0)),
                  full((RET_HEADS, bk, bk)), full((RET_HEADS, bk, 1)), full((RET_HEADS, bk, 1)),
                  full((RET_HEADS, 1, RET_QK)), full((1, RET_V))],
        out_specs=[pl.BlockSpec((bk, RET_V), lambda i: (i, 0)), pl.BlockSpec((bk, RET_V), lambda i: (i, 0)),
                   pl.BlockSpec((1, RET_HEADS, RET_QK, RET_DV), lambda i: (i, 0, 0, 0))],
        out_shape=[jax.ShapeDtypeStruct((t, RET_V), F32), jax.ShapeDtypeStruct((t, RET_V + FOX_W), BF16),
                   jax.ShapeDtypeStruct((nb, RET_HEADS, RET_QK, RET_DV), F32)],
        scratch_shapes=[pltpu.VMEM((RET_HEADS, RET_QK, RET_DV), F32)],
        compiler_params=_params("arbitrary"),
    )(proj, proj, proj, proj, cos, sin, w, wq, wk, mask, gain)


def _ret_bwd(proj, cos, sin, gain, dmixed, opre, states, name):
    t = proj.shape[0]
    bk = TOK_TILE
    nb = t // bk
    w, wq, wk, mask, g_blk = _ret_consts(bk)
    v0, g0 = 2 * RET_QK, 2 * RET_QK + RET_V

    def body(q_ref, k_ref, v_ref, rg_ref, cos_ref, sin_ref, w_ref, wq_ref, wk_ref, mask_ref, gain_ref,
             dog_ref, opre_ref, st_ref, dp_ref, gg_ref, dr_ref):
        step = pl.program_id(0)
        i = nb - 1 - step

        @pl.when(step == 0)
        def _():
            dr_ref[...] = jnp.zeros_like(dr_ref)
            gg_ref[...] = jnp.zeros_like(gg_ref)

        c, s = cos_ref[...], sin_ref[...]
        valid = ((i * bk + lax.broadcasted_iota(jnp.int32, (bk, 1), 0)) >= N_PAD).astype(F32)
        qr = _rope(q_ref[...], c, s)
        kr = _rope(k_ref[...], c, s) * QK_SCALE * valid
        kb = kr.astype(BF16)
        dqr = jnp.zeros((bk, RET_QK), F32)
        dkr = jnp.zeros((bk, RET_QK), F32)
        for h in range(RET_HEADS):
            hm = mask_ref[h]
            cols = slice(RET_DV * h, RET_DV * (h + 1))
            vh = v_ref[:, cols].astype(BF16)
            o = opre_ref[:, cols]
            rstd = lax.rsqrt(jnp.mean(o * o, axis=-1, keepdims=True) + EPS)
            xhat = o * rstd
            rg = rg_ref[:, cols]
            sg = _sigmoid(rg)
            gate = rg * sg
            gn = gain_ref[:, cols]
            dog = dog_ref[:, cols]
            dp_ref[:, g0 + RET_DV * h:g0 + RET_DV * (h + 1)] = (
                dog * xhat * gn * (sg * (1.0 + rg * (1.0 - sg)))).astype(BF16)
            gg_ref[:, cols] += jnp.sum(dog * xhat * gate, axis=0, keepdims=True)
            dxh = dog * gn * gate
            do = (rstd * (dxh - xhat * jnp.mean(dxh * xhat, axis=-1, keepdims=True))).astype(BF16)
            qm = (qr * hm).astype(BF16)
            qw = (qr * (hm * wq_ref[h])).astype(BF16)
            kw = (kr * wk_ref[h]).astype(BF16)
            wh = w_ref[h]
            sm = (_dot(qm, kb, NT) * wh).astype(BF16)
            ds = (_dot(do, vh, NT) * wh).astype(BF16)
            dr = dr_ref[h]
            drb = dr.astype(BF16)
            dp_ref[:, v0 + RET_DV * h:v0 + RET_DV * (h + 1)] = (_dot(sm, do, TN) + _dot(kw, drb)).astype(BF16)
            dqr = dqr + _dot(ds, kb) * hm + _dot(do, st_ref[0, h].astype(BF16), NT) * (hm * wq_ref[h])
            dkr = dkr + _dot(ds, qm, TN) + _dot(vh, drb, NT) * wk_ref[h]
            dr_ref[h] = g_blk[h] * dr + _dot(qw, do, TN)
        dp_ref[:, 0:RET_QK] = _rope_t(dqr, c, s).astype(BF16)
        dp_ref[:, RET_QK:2 * RET_QK] = _rope_t(dkr * (QK_SCALE * valid), c, s).astype(BF16)

    full = lambda shape: pl.BlockSpec(shape, lambda i: (0,) * len(shape))
    rev = lambda col: (lambda i: (nb - 1 - i, col))
    return _pcall(
        body, name=name, grid=(nb,),
        in_specs=[pl.BlockSpec((bk, RET_QK), rev(0)), pl.BlockSpec((bk, RET_QK), rev(1)),
                  pl.BlockSpec((bk, RET_V), rev(1)), pl.BlockSpec((bk, RET_V), rev(2)),
                  pl.BlockSpec((bk, RET_QK), rev(0)), pl.BlockSpec((bk, RET_QK), rev(0)),
                  full((RET_HEADS, bk, bk)), full((RET_HEADS, bk, 1)), full((RET_HEADS, bk, 1)),
                  full((RET_HEADS, 1, RET_QK)), full((1, RET_V)),
                  pl.BlockSpec((bk, RET_V), rev(0)), pl.BlockSpec((bk, RET_V), rev(0)),
                  pl.BlockSpec((1, RET_HEADS, RET_QK, RET_DV), lambda i: (nb - 1 - i, 0, 0, 0))],
        out_specs=[pl.BlockSpec((bk, g0 + RET_V), rev(0)), pl.BlockSpec((1, RET_V), lambda i: (0, 0))],
        out_shape=[jax.ShapeDtypeStruct((t, IN_PAD), BF16), jax.ShapeDtypeStruct((1, RET_V), F32)],
        scratch_shapes=[pltpu.VMEM((RET_HEADS, RET_QK, RET_DV), F32)],
        compiler_params=_params("arbitrary"),
    )(proj, proj, proj, proj, cos, sin, w, wq, wk, mask, gain, dmixed, opre, states)


def _forget_cumsum(proj, bias, name):
    t = proj.shape[0]
    rt = TOK_TILE
    nb = t // rt
    tril = jnp.asarray(np.tril(np.ones((rt, rt))), F32)

    def body(z_ref, b_ref, tril_ref, c_ref, carry_ref):
        i = pl.program_id(0)

        @pl.when(i == 0)
        def _():
            carry_ref[...] = jnp.zeros_like(carry_ref)

        z = z_ref[...] + b_ref[...]
        logf = jnp.minimum(z, 0.0) - jnp.log(1.0 + jnp.exp(-jnp.abs(z)))
        c = lax.dot_general(tril_ref[...], logf, NN, precision=lax.Precision.HIGHEST,
                            preferred_element_type=F32) + carry_ref[...]
        c_ref[...] = c
        carry_ref[...] = c[rt - 1:rt, :]

    return _pcall(
        body, name=name, grid=(nb,),
        in_specs=[pl.BlockSpec((rt, LANE), lambda i: (i, FF_COL_BLOCK)), pl.BlockSpec((1, LANE), lambda i: (0, 0)),
                  pl.BlockSpec((rt, rt), lambda i: (0, 0))],
        out_specs=pl.BlockSpec((rt, LANE), lambda i: (i, 0)),
        out_shape=jax.ShapeDtypeStruct((t, LANE), F32),
        scratch_shapes=[pltpu.VMEM((1, LANE), F32)],
        compiler_params=_params("arbitrary"),
    )(proj, bias, tril)


def _forget_cumsum_bwd(proj, bias, drs, dcs, dproj, name):
    t = proj.shape[0]
    rt = TOK_TILE
    nb = t // rt
    triu = jnp.asarray(np.triu(np.ones((rt, rt))), F32)

    def body(z_ref, b_ref, triu_ref, drs_ref, dcs_ref, dproj_in, dz_ref, gb_ref, carry_ref):
        step = pl.program_id(0)

        @pl.when(step == 0)
        def _():
            carry_ref[...] = jnp.zeros_like(carry_ref)
            gb_ref[...] = jnp.zeros_like(gb_ref)

        dlogf = lax.dot_general(triu_ref[...], drs_ref[...] - dcs_ref[...], NN, precision=lax.Precision.HIGHEST,
                                preferred_element_type=F32) + carry_ref[...]
        carry_ref[...] = dlogf[0:1, :]
        z = z_ref[...] + b_ref[...]
        is_head = lax.broadcasted_iota(jnp.int32, (rt, LANE), 1) < FOX_HEADS
        dz = jnp.where(is_head, dlogf / (1.0 + jnp.exp(z)), 0.0)
        dz_ref[...] = dz.astype(BF16)
        gb_ref[...] += jnp.sum(dz, axis=0, keepdims=True)

    return _pcall(
        body, name=name, grid=(nb,),
        in_specs=[pl.BlockSpec((rt, LANE), lambda i: (nb - 1 - i, FF_COL_BLOCK)),
                  pl.BlockSpec((1, LANE), lambda i: (0, 0)),
                  pl.BlockSpec((rt, rt), lambda i: (0, 0)),
                  pl.BlockSpec((rt, LANE), lambda i: (nb - 1 - i, 0)),
                  pl.BlockSpec((rt, LANE), lambda i: (nb - 1 - i, 0)),
                  pl.BlockSpec(memory_space=pl.ANY)],
        out_specs=[pl.BlockSpec((rt, LANE), lambda i: (nb - 1 - i, FF_COL_BLOCK)),
                   pl.BlockSpec((1, LANE), lambda i: (0, 0))],
        out_shape=[jax.ShapeDtypeStruct(dproj.shape, BF16), jax.ShapeDtypeStruct((1, LANE), F32)],
        input_output_aliases={5: 0},
        scratch_shapes=[pltpu.VMEM((1, LANE), F32)],
        compiler_params=_params("arbitrary"),
    )(proj, bias, triu, drs, dcs, dproj)


FOX_PAIRS = FOX_HEADS // 2
L_ONE_Q = FOX_DH
L_ONE_K = FOX_DH + 3
L_LSE = FOX_DH + 4


def _split3(x):
    hi = x.astype(BF16).astype(F32)
    r = x - hi
    mid = r.astype(BF16).astype(F32)
    return hi, mid, r - mid


def _head_to_low(slab, e):
    return slab if e == 0 else pltpu.roll(slab, FOX_DH, axis=1)


def _pair(a, b, low):
    return jnp.where(low, a, pltpu.roll(b, FOX_DH, axis=1))


def _fox_prep(proj, c, name):
    t = proj.shape[0]
    tq = TOK_TILE

    def body(p_ref, c_ref, qa_ref, ka_ref, va_ref, qt_ref, vt_ref):
        i = pl.program_id(0)
        lane = lax.broadcasted_iota(jnp.int32, (tq, LANE), 1)
        low = lane < FOX_DH
        live = (i * tq + lax.broadcasted_iota(jnp.int32, (tq, 1), 0)) >= N_PAD
        q_tail = jnp.where(lane < L_ONE_Q + 3, 1.0, 0.0)
        k_ones = (lane >= L_ONE_K) & (lane < L_ONE_K + 4)
        v_tail = jnp.where(lane < FOX_DH + 2, 1.0, 0.0)
        bias_parts = _split3(jnp.where(live, -c_ref[...], NEG))
        for pair in range(FOX_PAIRS):
            base = 3 * LANE * pair
            for e in range(2):
                h = 2 * pair + e
                q = _head_to_low(p_ref[:, base:base + LANE], e)
                k = _head_to_low(p_ref[:, base + LANE:base + 2 * LANE], e)
                v = _head_to_low(p_ref[:, base + 2 * LANE:base + 3 * LANE], e)
                hi, mid, lo = [part[:, h:h + 1] for part in bias_parts]
                ka = jnp.where(low, k, jnp.where(k_ones, 1.0, 0.0))
                ka = jnp.where(lane == L_ONE_Q, hi, jnp.where(lane == L_ONE_Q + 1, mid, jnp.where(lane == L_ONE_Q + 2, lo, ka)))
                qa = jnp.where(low, q * QK_SCALE, q_tail)
                va = jnp.where(low, v, v_tail)
                qa_ref[h] = qa.astype(BF16)
                ka_ref[h] = ka.astype(BF16)
                va_ref[h] = va.astype(BF16)
                qt_ref[h] = qa.T.astype(BF16)
                vt_ref[h] = va.T.astype(BF16)

    out = jax.ShapeDtypeStruct((FOX_HEADS, t, LANE), BF16)
    out_t = jax.ShapeDtypeStruct((FOX_HEADS, t // tq, LANE, tq), BF16)
    ospec = pl.BlockSpec((FOX_HEADS, tq, LANE), lambda i: (0, i, 0))
    tspec = pl.BlockSpec((FOX_HEADS, None, LANE, tq), lambda i: (0, i, 0, 0))
    return _pcall(
        body, name=name, grid=(t // tq,),
        in_specs=[pl.BlockSpec((tq, 3 * FOX_W), lambda i: (i, 1)), pl.BlockSpec((tq, LANE), lambda i: (i, 0))],
        out_specs=[ospec, ospec, ospec, tspec, tspec], out_shape=[out, out, out, out_t, out_t],
        compiler_params=_params("parallel"),
    )(proj, c)


STEP_PAIRS = 2
STEP_HEADS = 2 * STEP_PAIRS
FOX_GROUPS = FOX_PAIRS // STEP_PAIRS
FWD_PAIRS = 4
FWD_HEADS = 2 * FWD_PAIRS
FWD_GROUPS = FOX_PAIRS // FWD_PAIRS


def _fox_fwd(qt, ka, vt, mixed, name):
    nh, nq, tq, _ = ka.shape
    t = nq * tq

    def body(qt_ref, ka_ref, vt_ref, mixed_in, mixed_ref, o_ref, lse_ref):
        i = pl.program_id(1)
        lane = lax.broadcasted_iota(jnp.int32, (tq, LANE), 1)
        key_le_query = lax.broadcasted_iota(jnp.int32, (tq, tq), 0) <= lax.broadcasted_iota(jnp.int32, (tq, tq), 1)

        def logits(j):
            return [_dot(ka_ref[h, j], qt_ref[h]) for h in range(FWD_HEADS)]

        def update(j, scores, carry, diagonal):
            new = []
            for h in range(FWD_HEADS):
                m, acc = carry[h]
                s = jnp.where(key_le_query, scores[h], NEG) if diagonal else scores[h]
                m_new = jnp.maximum(m, jnp.max(s, axis=0, keepdims=True))
                p = jnp.exp(s - m_new).astype(BF16)
                new.append((m_new, jnp.exp(m - m_new) * acc + _dot(vt_ref[h, j], p)))
            return tuple(new)

        init = tuple((jnp.full((1, tq), NEG, F32), jnp.zeros((LANE, tq), F32)) for _ in range(FWD_HEADS))
        carry = lax.fori_loop(0, i, lambda j, cr: update(j, logits(j), cr, False), init)
        outs, lse_rows = [], []
        for m, acc in update(i, logits(i), carry, True):
            l = acc[FOX_DH:FOX_DH + 1, :]
            outs.append((acc / l).T)
            lse_rows.append(m + jnp.log(l))
        lse_rows.append(jnp.zeros((LANE - FWD_HEADS, tq), F32))
        o_all = jnp.concatenate([_pair(outs[2 * c], outs[2 * c + 1], lane < FOX_DH) for c in range(FWD_PAIRS)], axis=1)
        mixed_ref[...] = o_all.astype(BF16)
        o_ref[...] = o_all
        lse_ref[...] = jnp.concatenate(lse_rows, axis=0).T

    width = FWD_PAIRS * LANE
    whole = pl.BlockSpec((FWD_HEADS, nq, tq, LANE), lambda g, i: (g, 0, 0, 0), pipeline_mode=pl.Buffered(1))
    whole_t = pl.BlockSpec((FWD_HEADS, nq, LANE, tq), lambda g, i: (g, 0, 0, 0), pipeline_mode=pl.Buffered(1))
    return _pcall(
        body, name=name, grid=(FWD_GROUPS, nq),
        in_specs=[pl.BlockSpec((FWD_HEADS, None, LANE, tq), lambda g, i: (g, i, 0, 0)), whole, whole_t,
                  pl.BlockSpec(memory_space=pl.ANY)],
        out_specs=[pl.BlockSpec((tq, width), lambda g, i: (i, RET_V // width + g)),
                   pl.BlockSpec((tq, width), lambda g, i: (i, g)),
                   pl.BlockSpec((None, tq, LANE), lambda g, i: (g, i, 0))],
        out_shape=[jax.ShapeDtypeStruct(mixed.shape, BF16), jax.ShapeDtypeStruct((t, FOX_W), F32),
                   jax.ShapeDtypeStruct((FWD_GROUPS, t, LANE), F32)],
        input_output_aliases={3: 0},
        compiler_params=_params("parallel", "parallel"),
    )(qt, ka, vt, mixed)


def _fox_prep_bwd(dmixed, o_fox, lse, qa, after, name):
    t = dmixed.shape[0]
    tq = TOK_TILE

    def body(dm_ref, o_ref, lse_ref, qa_ref, _, qab_ref, doa_ref):
        i = pl.program_id(0)
        lane = lax.broadcasted_iota(jnp.int32, (tq, LANE), 1)
        low = lane < FOX_DH
        live = (i * tq + lax.broadcasted_iota(jnp.int32, (tq, 1), 0)) >= N_PAD
        lse_parts = [_split3(jnp.where(live, -lse_ref[grp], 0.0)) for grp in range(FWD_GROUPS)]
        for pair in range(FOX_PAIRS):
            cols = slice(LANE * pair, LANE * (pair + 1))
            d_slab = dm_ref[:, cols]
            prod = d_slab * o_ref[:, cols]
            for e in range(2):
                h = 2 * pair + e
                nd = -jnp.sum(jnp.where(low, _head_to_low(prod, e), 0.0), axis=-1, keepdims=True)
                nd_hi = nd.astype(BF16).astype(F32)
                doa = jnp.where(low, _head_to_low(d_slab, e), 0.0)
                doa = jnp.where(lane == FOX_DH, nd_hi, jnp.where(lane == FOX_DH + 1, nd - nd_hi, doa))
                doa_ref[h] = doa.astype(BF16)
                lane_h = h % FWD_HEADS
                hi, mid, lo = [part[:, lane_h:lane_h + 1] for part in lse_parts[h // FWD_HEADS]]
                qab = qa_ref[h].astype(F32)
                qab = jnp.where(lane == L_LSE, hi, jnp.where(lane == L_LSE + 1, mid, jnp.where(lane == L_LSE + 2, lo, qab)))
                qab_ref[h] = qab.astype(BF16)

    out = jax.ShapeDtypeStruct((FOX_HEADS, t, LANE), BF16)
    hspec = pl.BlockSpec((FOX_HEADS, tq, LANE), lambda i: (0, i, 0))
    return _pcall(
        body, name=name, grid=(t // tq,),
        in_specs=[pl.BlockSpec((tq, FOX_W), lambda i: (i, 1)), pl.BlockSpec((tq, FOX_W), lambda i: (i, 0)),
                  pl.BlockSpec((FWD_GROUPS, tq, LANE), lambda i: (0, i, 0)), hspec,
                  pl.BlockSpec(memory_space=pl.ANY)],
        out_specs=[hspec, hspec], out_shape=[out, out],
        compiler_params=_params("parallel"),
    )(dmixed, o_fox, lse, qa, after)


def _fox_bwd(qab, doa, ka, va, dproj, name):
    nh, nq, tq, _ = qab.shape
    t = nq * tq
    slab = 3 * LANE * STEP_PAIRS
    group0 = (2 * RET_QK + 2 * RET_V) // slab

    def body(qab_ref, doa_ref, ka_ref, va_ref, dproj_in, dp_ref, drs_ref, dcs_ref, dq_ref):
        g, j = pl.program_id(0), pl.program_id(1)

        @pl.when((g == 0) & (j == 0))
        def _():
            drs_ref[...] = jnp.zeros_like(drs_ref)
            dcs_ref[...] = jnp.zeros_like(dcs_ref)

        @pl.when(j == 0)
        def _():
            dq_ref[...] = jnp.zeros_like(dq_ref)

        lane = lax.broadcasted_iota(jnp.int32, (tq, LANE), 1)
        low = lane < FOX_DH
        key_le_query = lax.broadcasted_iota(jnp.int32, (tq, tq), 0) <= lax.broadcasted_iota(jnp.int32, (tq, tq), 1)

        def by_head(c, a, b, col):
            h = STEP_HEADS * g + 2 * c
            return jnp.where(lane == h, a[:, col:col + 1], jnp.where(lane == h + 1, b[:, col:col + 1], 0.0))


        def step(i, carry, diagonal):
            st = [_dot(ka_ref[h], qab_ref[h, i], NT) for h in range(STEP_HEADS)]
            dpt = [_dot(va_ref[h], doa_ref[h, i], NT) for h in range(STEP_HEADS)]
            new = []
            for h in range(STEP_HEADS):
                p = jnp.exp(st[h])
                if diagonal:
                    p = jnp.where(key_le_query, p, 0.0)
                ds = (p * dpt[h]).astype(BF16)
                dq_ref[h, i] += _dot(ds, ka_ref[h], TN)
                dk, dv = carry[h]
                new.append((dk + _dot(ds, qab_ref[h, i]), dv + _dot(p.astype(BF16), doa_ref[h, i])))
            return tuple(new)

        zero = jnp.zeros((tq, LANE), F32)
        carry = step(j, tuple((zero, zero) for _ in range(STEP_HEADS)), True)
        carry = lax.fori_loop(j + 1, nq, lambda i, cr: step(i, cr, False), carry)
        rows = pl.ds(pl.multiple_of(j * tq, tq), tq)
        for c in range(STEP_PAIRS):
            (dka, dva), (dkb, dvb) = carry[2 * c], carry[2 * c + 1]
            c0 = 3 * LANE * c
            dp_ref[rows, c0 + LANE:c0 + 2 * LANE] = _pair(dka, dkb, low).astype(BF16)
            dp_ref[rows, c0 + 2 * LANE:c0 + 3 * LANE] = _pair(dva, dvb, low).astype(BF16)
            dcs_ref[rows, :] += by_head(c, dka, dkb, L_ONE_Q)

        @pl.when(j == nq - 1)
        def _():
            for c in range(STEP_PAIRS):
                for blk in range(nq):
                    r = slice(blk * tq, (blk + 1) * tq)
                    a, b = dq_ref[2 * c, blk], dq_ref[2 * c + 1, blk]
                    dp_ref[r, 3 * LANE * c:3 * LANE * c + LANE] = (_pair(a, b, low) * QK_SCALE).astype(BF16)
                    drs_ref[r, :] += by_head(c, a, b, L_ONE_K)

    whole = pl.BlockSpec((STEP_HEADS, nq, tq, LANE), lambda g, j: (g, 0, 0, 0), pipeline_mode=pl.Buffered(1))
    blk = pl.BlockSpec((STEP_HEADS, None, tq, LANE), lambda g, j: (g, j, 0, 0))
    sums = pl.BlockSpec((t, LANE), lambda g, j: (0, 0), pipeline_mode=pl.Buffered(1))
    return _pcall(
        body, name=name, grid=(FOX_GROUPS, nq),
        in_specs=[whole, whole, blk, blk, pl.BlockSpec(memory_space=pl.ANY)],
        out_specs=[pl.BlockSpec((t, slab), lambda g, j: (0, group0 + g)), sums, sums],
        out_shape=[jax.ShapeDtypeStruct(dproj.shape, BF16), jax.ShapeDtypeStruct((t, LANE), F32),
                   jax.ShapeDtypeStruct((t, LANE), F32)],
        input_output_aliases={4: 0},
        scratch_shapes=[pltpu.VMEM((STEP_HEADS, nq, tq, LANE), F32)],
        compiler_params=_params("arbitrary", "arbitrary"),
    )(qab, doa, ka, va, dproj)


HALO = 8


def _rows_ext(ref, r0, rows, t, before, after):
    lo, hi = r0 - before, r0 + rows + after
    width = ref.shape[-1]
    parts = []
    if lo < 0:
        parts.append(jnp.zeros((-lo, width), F32))
    parts.append(ref[max(lo, 0):min(hi, t), :].astype(F32))
    if hi > t:
        parts.append(jnp.zeros((hi - t, width), F32))
    return parts[0] if len(parts) == 1 else jnp.concatenate(parts, axis=0)


def _conv_taps(a_ext, r0_ext, cw_ref, cb_ref):
    n = a_ext.shape[0]
    if r0_ext < N_PAD:
        row = r0_ext + lax.broadcasted_iota(jnp.int32, (n, 1), 0)
        a_ext = jnp.where(row >= N_PAD, a_ext, 0.0)
    a1 = pltpu.roll(a_ext, 1, axis=0)
    a2 = pltpu.roll(a_ext, 2, axis=0)
    acc = cb_ref[...] + a2 * cw_ref[0:1, :] + a1 * cw_ref[1:2, :] + a_ext * cw_ref[2:3, :]
    return a_ext, a1, a2, acc


FF_COLS = 256


def _up_conv_fwd(n2, w_up_t, conv_w8, conv_b, name):
    t, d = n2.shape
    f = w_up_t.shape[1]
    rows = TOK_TILE
    starts = list(range(0, t, rows))

    def body(n_ref, wa_ref, wb_ref, cw_ref, cb_ref, up_ref, g_ref):
        def project(r0):
            n_rows = n_ref[r0:r0 + rows, :]
            up_ref[0, r0:r0 + rows, :] = _dot(n_rows, wa_ref[...], NT)
            up_ref[1, r0:r0 + rows, :] = _dot(n_rows, wb_ref[...], NT)

        def activate(r0):
            a_ext = _rows_ext(up_ref.at[0], r0, rows, t, HALO, 0)
            _, _, _, acc = _conv_taps(a_ext, r0 - HALO, cw_ref, cb_ref)
            acc = acc[HALO:, :]
            g_ref[r0:r0 + rows, :] = (acc * _sigmoid(acc) * up_ref[1, r0:r0 + rows, :]).astype(BF16)

        project(starts[0])
        for r0, r_next in zip(starts, starts[1:] + [None]):
            if r_next is not None:
                project(r_next)
            activate(r0)

    return _pcall(
        body, name=name, grid=(f // FF_COLS,),
        in_specs=[pl.BlockSpec((t, d), lambda j: (0, 0), pipeline_mode=pl.Buffered(1)),
                  pl.BlockSpec((None, FF_COLS, d), lambda j: (0, j, 0)), pl.BlockSpec((None, FF_COLS, d), lambda j: (1, j, 0)),
                  pl.BlockSpec((8, FF_COLS), lambda j: (0, j)), pl.BlockSpec((1, FF_COLS), lambda j: (0, j))],
        out_specs=[pl.BlockSpec((2, t, FF_COLS), lambda j: (0, 0, j)), pl.BlockSpec((t, FF_COLS), lambda j: (0, j))],
        out_shape=[jax.ShapeDtypeStruct((2, t, f), F32), jax.ShapeDtypeStruct((t, f), BF16)],
        compiler_params=_params("parallel"),
    )(n2, w_up_t, w_up_t, conv_w8, conv_b)


def _dg_conv_bwd(up, conv_w8, conv_b, dh2, w_down, name):
    _, t, f = up.shape
    d = dh2.shape[1]
    rows = TOK_TILE
    starts = list(range(0, t, rows))

    def body(a_ref, b_ref, cw_ref, cb_ref, dh_ref, wd_ref, dup_ref, gcw_ref, gcb_ref, dg_ref):
        def project(r0):
            dg_ref[r0:r0 + rows, :] = _dot(dh_ref[r0:r0 + rows, :], wd_ref[...], NT)

        gw = [jnp.zeros((1, FF_COLS), F32) for _ in range(3)]
        gb = jnp.zeros((1, FF_COLS), F32)
        project(starts[0])
        for r0, r_next in zip(starts, starts[1:] + [None]):
            if r_next is not None:
                project(r_next)
            a_ext = _rows_ext(a_ref, r0, rows, t, HALO, HALO)
            b_ext = _rows_ext(b_ref, r0, rows, t, HALO, HALO)
            dg_ext = _rows_ext(dg_ref, r0, rows, t, HALO, HALO)
            a0, a1, a2, acc = _conv_taps(a_ext, r0 - HALO, cw_ref, cb_ref)
            sg = _sigmoid(acc)
            dacc = dg_ext * b_ext * (sg * (1.0 + acc * (1.0 - sg)))
            n = dacc.shape[0]
            da = (dacc * cw_ref[2:3, :] + pltpu.roll(dacc, n - 1, axis=0) * cw_ref[1:2, :]
                  + pltpu.roll(dacc, n - 2, axis=0) * cw_ref[0:1, :])
            core = slice(HALO, HALO + rows)
            da = da[core, :]
            if r0 < N_PAD:
                row = r0 + lax.broadcasted_iota(jnp.int32, (rows, 1), 0)
                da = jnp.where(row >= N_PAD, da, 0.0)
            dup_ref[0, r0:r0 + rows, :] = da.astype(BF16)
            dup_ref[1, r0:r0 + rows, :] = (dg_ext * acc * sg)[core, :].astype(BF16)
            dacc_c = dacc[core, :]
            gw[0] = gw[0] + jnp.sum(dacc_c * a2[core, :], axis=0, keepdims=True)
            gw[1] = gw[1] + jnp.sum(dacc_c * a1[core, :], axis=0, keepdims=True)
            gw[2] = gw[2] + jnp.sum(dacc_c * a0[core, :], axis=0, keepdims=True)
            gb = gb + jnp.sum(dacc_c, axis=0, keepdims=True)
        gcw_ref[...] = jnp.zeros((8, FF_COLS), F32)
        for tap in range(3):
            gcw_ref[tap:tap + 1, :] = gw[tap]
        gcb_ref[...] = gb

    return _pcall(
        body, name=name, grid=(f // FF_COLS,),
        in_specs=[pl.BlockSpec((None, t, FF_COLS), lambda j: (0, 0, j)), pl.BlockSpec((None, t, FF_COLS), lambda j: (1, 0, j)),
                  pl.BlockSpec((8, FF_COLS), lambda j: (0, j)), pl.BlockSpec((1, FF_COLS), lambda j: (0, j)),
                  pl.BlockSpec((t, d), lambda j: (0, 0), pipeline_mode=pl.Buffered(1)),
                  pl.BlockSpec((FF_COLS, d), lambda j: (j, 0))],
        out_specs=[pl.BlockSpec((2, t, FF_COLS), lambda j: (0, 0, j)), pl.BlockSpec((8, FF_COLS), lambda j: (0, j)),
                   pl.BlockSpec((1, FF_COLS), lambda j: (0, j))],
        out_shape=[jax.ShapeDtypeStruct((2, t, f), BF16), jax.ShapeDtypeStruct((8, f), F32),
                   jax.ShapeDtypeStruct((1, f), F32)],
        scratch_shapes=[pltpu.VMEM((t, FF_COLS), F32)],
        compiler_params=_params("parallel"),
    )(up, up, conv_w8, conv_b, dh2, w_down)


def _exchange(arrays, kinds, name, after=None):
    n = len(arrays)
    npeer = N_DEV - 1
    n_in = n + int(after is not None)

    def body(*refs):
        ins, outs = refs[:n], refs[n_in:n_in + n]
        send_sems, recv_sems, local_sems = refs[n_in + n:]
        x, y, c = lax.axis_index("x"), lax.axis_index("y"), lax.axis_index("c")
        me = 4 * x + 2 * y + c
        copies, locals_ = [], []
        for a in range(n):
            gather = kinds[a] == "gather"
            own = pltpu.make_async_copy(ins[a] if gather else ins[a].at[me], outs[a].at[me], local_sems.at[a])
            own.start()
            locals_.append(own)
            for d in range(1, N_DEV):
                px = 1 - x if d & 4 else x
                py = 1 - y if d & 2 else y
                pc = 1 - c if d & 1 else c
                src = ins[a] if gather else ins[a].at[4 * px + 2 * py + pc]
                cp = pltpu.make_async_remote_copy(
                    src_ref=src, dst_ref=outs[a].at[me],
                    send_sem=send_sems.at[a * npeer + d - 1], recv_sem=recv_sems.at[a * npeer + d - 1],
                    device_id=(px, py, pc), device_id_type=pl.DeviceIdType.MESH)
                cp.start()
                copies.append(cp)
        for cp in copies:
            cp.wait_recv()
        for cp in copies:
            cp.wait_send()
        for own in locals_:
            own.wait()

    out_shape = [jax.ShapeDtypeStruct((N_DEV,) + (a.shape if k == "gather" else a.shape[1:]), a.dtype)
                 for a, k in zip(arrays, kinds)]
    return _pcall(
        body, name=name,
        in_specs=[pl.BlockSpec(memory_space=pl.ANY)] * n_in,
        out_specs=[pl.BlockSpec(memory_space=pl.ANY)] * n,
        out_shape=out_shape,
        scratch_shapes=[pltpu.SemaphoreType.DMA((n * npeer,)), pltpu.SemaphoreType.DMA((n * npeer,)),
                        pltpu.SemaphoreType.DMA((n,))],
        compiler_params=pltpu.CompilerParams(has_side_effects=True),
    )(*arrays, *([] if after is None else [after]))


ALL_PEERS = tuple(range(1, N_DEV))
SAME_CORE_AND_SIBLING = (1, 2, 4, 6)
OTHER_CHIPS = (2, 4, 6)


def _taking_part(index, slots):
    return None if slots is None else (index >= slots[0]) & (index < slots[1])


def _do_if(condition, action):
    if condition is None:
        action()
    else:
        pl.when(condition)(action)


def _block_of(src, index, slots):
    return src.at[index] if slots is None else src.at[jnp.clip(index - slots[0], 0, slots[1] - slots[0] - 1)]


def _peer_copies(srcs, lands, kinds, send_sems, recv_sems, relations=ALL_PEERS, slots=None):
    x, y, c = lax.axis_index("x"), lax.axis_index("y"), lax.axis_index("c")
    me = 4 * x + 2 * y + c
    copies = []
    for a in range(len(srcs)):
        for d in relations:
            px = 1 - x if d & 4 else x
            py = 1 - y if d & 2 else y
            pc = 1 - c if d & 1 else c
            peer = 4 * px + 2 * py + pc
            k = a * (N_DEV - 1) + d - 1
            if kinds[a] == "forward":
                src, dst, target = lands[a].at[peer], lands[a].at[peer], (x, y, 1 - c)
            else:
                src = srcs[a] if kinds[a] == "gather" else _block_of(srcs[a], peer, slots)
                dst, target = lands[a].at[me], (px, py, pc)
            copies.append((pltpu.make_async_remote_copy(
                src_ref=src, dst_ref=dst, send_sem=send_sems.at[k], recv_sem=recv_sems.at[k],
                device_id=target, device_id_type=pl.DeviceIdType.MESH), peer))
    return copies


def _own_copies(srcs, lands, kinds, sems, slots=None):
    me = 4 * lax.axis_index("x") + 2 * lax.axis_index("y") + lax.axis_index("c")
    first = len(srcs) * (N_DEV - 1)
    return [pltpu.make_async_copy(_block_of(srcs[a], me, slots) if kinds[a] == "scatter" else srcs[a], lands[a].at[me],
                                  sems.at[first + a])
            for a in range(len(srcs))]


def _exchange_start(arrays, kinds, name, after=None, relations=ALL_PEERS, lands=None, own=True, slots=None):
    n = len(arrays)
    nsem = n * (N_DEV - 1) + n
    hbm = pl.BlockSpec(memory_space=pltpu.HBM)
    sem = pl.BlockSpec(memory_space=pltpu.SEMAPHORE)
    land_shapes = ([l.shape for l in lands] if lands is not None else
                   [(N_DEV,) + (a.shape if k == "gather" else a.shape[1:]) for a, k in zip(arrays, kinds)])

    n_in = 2 * n + int(after is not None)

    def body(*refs):
        srcs, land_refs = refs[:n], refs[n:2 * n]
        send_sems, recv_sems = refs[n_in], refs[n_in + 1]
        token = refs[-1]
        me = 4 * lax.axis_index("x") + 2 * lax.axis_index("y") + lax.axis_index("c")
        for cp, peer in _peer_copies(srcs, land_refs, kinds, send_sems, recv_sems, relations, slots):
            _do_if(_taking_part(peer, slots), cp.start)
        for cp in _own_copies(srcs, land_refs, kinds, send_sems, slots) if own else []:
            _do_if(_taking_part(me, slots), cp.start)
        token[...] = jnp.zeros_like(token)

    operands = [pltpu.with_memory_space_constraint(a, pltpu.HBM) for a in arrays]
    operands += (list(lands) if lands is not None else
                 [pltpu.with_memory_space_constraint(lax.empty(s, a.dtype), pltpu.HBM) for s, a in zip(land_shapes, arrays)])
    operands += [] if after is None else [after]
    out = _pcall(
        body, name=name,
        in_specs=[hbm] * (2 * n) + ([] if after is None else [pl.BlockSpec(memory_space=pl.ANY)]),
        out_specs=[sem, sem] + [hbm] * (2 * n) + [pl.BlockSpec(memory_space=pltpu.VMEM)],
        out_shape=[pltpu.SemaphoreType.DMA((nsem,)), pltpu.SemaphoreType.DMA((nsem,))]
        + [pltpu.HBM(a.shape, a.dtype) for a in arrays]
        + [pltpu.HBM(s, a.dtype) for s, a in zip(land_shapes, arrays)]
        + [jax.ShapeDtypeStruct((8, LANE), F32)],
        input_output_aliases={k: 2 + k for k in range(2 * n)},
        compiler_params=pltpu.CompilerParams(has_side_effects=pltpu.SideEffectType.DATAFLOW_SIDE_EFFECTING),
    )(*operands)
    return out[0], out[1], list(out[2:2 + n]), list(out[2 + n:2 + 2 * n]), out[-1]


def _exchange_wait(started, kinds, after, name, own=True, relations=ALL_PEERS, with_sources=False, slots=None):
    send_sems, recv_sems, srcs, lands, _ = started
    n = len(srcs)
    hbm = pl.BlockSpec(memory_space=pltpu.HBM)
    sem = pl.BlockSpec(memory_space=pltpu.SEMAPHORE)

    def body(*refs):
        src_refs, land_refs = refs[:n], refs[n:2 * n]
        me = 4 * lax.axis_index("x") + 2 * lax.axis_index("y") + lax.axis_index("c")
        copies = _peer_copies(src_refs, land_refs, kinds, refs[2 * n], refs[2 * n + 1], relations, slots)
        for cp, peer in copies:
            _do_if(_taking_part(peer, slots), cp.wait_send)
        for cp, _ in copies:
            _do_if(_taking_part(me, slots), cp.wait_recv)
        for cp in _own_copies(src_refs, land_refs, kinds, refs[2 * n], slots) if own else []:
            _do_if(_taking_part(me, slots), cp.wait)

    out = _pcall(
        body, name=name,
        in_specs=[hbm] * (2 * n) + [sem, sem, pl.BlockSpec(memory_space=pl.ANY)],
        out_specs=[hbm] * (2 * n),
        out_shape=[pltpu.HBM(a.shape, a.dtype) for a in srcs + lands],
        input_output_aliases={k: k for k in range(2 * n)},
        compiler_params=pltpu.CompilerParams(has_side_effects=pltpu.SideEffectType.DATAFLOW_SIDE_EFFECTING),
    )(*srcs, *lands, send_sems, recv_sems, after)
    return (list(out[:n]), list(out[n:])) if with_sources else list(out[n:])


def _sum_slots(slots, name, rows_tile):
    nd, r, c = slots.shape

    def body(s_ref, o_ref):
        acc = s_ref[0].astype(F32)
        for p in range(1, nd):
            acc = acc + s_ref[p].astype(F32)
        o_ref[...] = acc

    return _pcall(
        body, name=name, grid=(r // rows_tile,),
        in_specs=[pl.BlockSpec((nd, rows_tile, c), lambda i: (0, i, 0))],
        out_specs=pl.BlockSpec((rows_tile, c), lambda i: (i, 0)),
        out_shape=jax.ShapeDtypeStruct((r, c), F32),
        compiler_params=_params("parallel"),
    )(slots)


def _sum_slots_small(slot_arrays, own_arrays, name):
    n = len(slot_arrays)

    def body(*refs):
        me = 4 * lax.axis_index("x") + 2 * lax.axis_index("y") + lax.axis_index("c")
        for s_ref, own_ref, o_ref in zip(refs[:n], refs[n:2 * n], refs[2 * n:]):
            acc = jnp.where(me == 0, own_ref[...], s_ref[0])
            for p in range(1, s_ref.shape[0]):
                acc = acc + jnp.where(me == p, own_ref[...], s_ref[p])
            o_ref[...] = acc

    return _pcall(body, name=name, out_shape=[jax.ShapeDtypeStruct(a.shape[1:], F32) for a in slot_arrays])(
        *slot_arrays, *own_arrays)


def _adamw_values(w, gr, m, v):
    nm = ADAM_B1 * m + (1.0 - ADAM_B1) * gr
    nv = ADAM_B2 * v + (1.0 - ADAM_B2) * (gr * gr)
    m_hat = nm / (1.0 - ADAM_B1 ** ADAM_STEP)
    v_hat = nv / (1.0 - ADAM_B2 ** ADAM_STEP)
    return -ADAM_LR * (m_hat / (jnp.sqrt(v_hat) + ADAM_EPS) + ADAM_WD * w), nm, nv


def _adamw_update(w_ref, g_ref, m_ref, v_ref, d_ref, nm_ref, nv_ref):
    d_ref[...], nm_ref[...], nv_ref[...] = _adamw_values(w_ref[...], g_ref[...], m_ref[...], v_ref[...])


def _adamw_from_slots(w, slots, m, v, name, cols_tile=2 * LANE):
    rows, cols = w.shape
    nd, rows_pad, _ = slots.shape

    def body(w_ref, s_ref, m_ref, v_ref, g_ref, d_ref, nm_ref, nv_ref):
        gr = s_ref[0, 0:rows, :].astype(F32)
        for p in range(1, nd):
            gr = gr + s_ref[p, 0:rows, :].astype(F32)
        g_ref[...] = gr
        d_ref[...], nm_ref[...], nv_ref[...] = _adamw_values(w_ref[...], gr, m_ref[...], v_ref[...])

    spec = pl.BlockSpec((rows, cols_tile), lambda i: (0, i))
    slot_spec = pl.BlockSpec((nd, rows_pad, cols_tile), lambda i: (0, 0, i))
    return _pcall(
        body, name=name, grid=(cols // cols_tile,), in_specs=[spec, slot_spec, spec, spec], out_specs=[spec] * 4,
        out_shape=[jax.ShapeDtypeStruct((rows, cols), F32)] * 4, compiler_params=_params("parallel"),
    )(w, slots, m, v)


def _adamw_small(ws, gs, ms, vs, name):
    n = len(ws)

    def body(*refs):
        ins, outs = refs[:4 * n], refs[4 * n:]
        for k in range(n):
            _adamw_update(ins[k], ins[n + k], ins[2 * n + k], ins[3 * n + k], outs[k], outs[n + k], outs[2 * n + k])

    shapes = [jax.ShapeDtypeStruct(w.shape, F32) for w in ws]
    out = _pcall(body, name=name, out_shape=shapes * 3)(*ws, *gs, *ms, *vs)
    return list(out[:n]), list(out[n:2 * n]), list(out[2 * n:])


def _adamw(w, g, m, v, name, rows_tile):
    r, c = w.shape
    body = lambda *refs: _adamw_update(*refs)
    spec = pl.BlockSpec((rows_tile, c), lambda i: (i, 0))
    shp = jax.ShapeDtypeStruct((r, c), F32)
    return _pcall(
        body, name=name, grid=(r // rows_tile,), in_specs=[spec] * 4, out_specs=[spec] * 3, out_shape=[shp] * 3,
        compiler_params=_params("parallel"),
    )(w, g, m, v)


F0 = 2 * RET_QK + 2 * RET_V


def _to_internal_rows(w_t):
    cols = w_t.shape[1]
    fox = w_t[F0:F0 + 3 * FOX_W].reshape(3, FOX_PAIRS, LANE, cols).transpose(1, 0, 2, 3).reshape(3 * FOX_W, cols)
    tail = jnp.zeros((IN_PAD - IN_WIDTH, cols), w_t.dtype)
    return jnp.concatenate([w_t[:F0], fox, w_t[F0 + 3 * FOX_W:], tail], axis=0)


def _from_internal_rows(g_t):
    cols = g_t.shape[1]
    fox = g_t[F0:F0 + 3 * FOX_W].reshape(FOX_PAIRS, 3, LANE, cols).transpose(1, 0, 2, 3).reshape(3 * FOX_W, cols)
    return jnp.concatenate([g_t[:F0], fox, g_t[F0 + 3 * FOX_W:F0 + 3 * FOX_W + FOX_HEADS]], axis=0)


IN_BLOCK = IN_WIDTH // N_DEV
IN_BLOCK_PAD = 400
BF16_ROWS = 16
FIRST_SLOTS = F0 // IN_BLOCK
GW_IN_TILE = 640
GW_IN_FIRST_TILES = -(-FIRST_SLOTS * IN_BLOCK // GW_IN_TILE)


def _slot_row_of_internal():
    rows = np.arange(IN_WIDTH)
    fox = rows[F0:F0 + 3 * FOX_W].reshape(3, FOX_PAIRS, LANE).transpose(1, 0, 2).reshape(-1)
    original = np.concatenate([rows[:F0], fox, rows[F0 + 3 * FOX_W:]])
    slot_rows = original // IN_BLOCK * IN_BLOCK_PAD + original % IN_BLOCK
    return np.concatenate([slot_rows, np.full(IN_PAD - IN_WIDTH, -1)])


def _internal_row_of_slot():
    forward = _slot_row_of_internal()
    back = np.full(N_DEV * IN_BLOCK_PAD, -1)
    back[forward[forward >= 0]] = np.nonzero(forward >= 0)[0]
    return back


def _row_runs(src_of_dst, starts):
    tiles = []
    for t0 in range(0, len(src_of_dst), LANE):
        runs = []
        for o in range(min(LANE, len(src_of_dst) - t0)):
            s = int(src_of_dst[t0 + o])
            if s < 0:
                continue
            if runs and runs[-1][0] + runs[-1][2] == o and runs[-1][1] + runs[-1][2] == s and s not in starts:
                runs[-1][2] += 1
            else:
                runs.append([o, s, 1])
        tiles.append(runs)
    return tiles


def _move_rows(srcs, src_of_dst, name):
    cols, dtype = srcs[0].shape[1], srcs[0].dtype
    starts = [int(v) for v in np.cumsum([0] + [s.shape[0] for s in srcs[:-1]])]
    n_dst = len(src_of_dst)
    tiles = _row_runs(src_of_dst, starts)

    def body(*refs):
        o_ref = refs[-1]
        for t, runs in enumerate(tiles):
            size = min(LANE, n_dst - t * LANE)
            rows = pl.ds(t * LANE, size)
            if not runs:
                o_ref[rows, :] = jnp.zeros((size, cols), o_ref.dtype)
                continue
            acc = None
            for o0, s0, n in runs:
                k = max(a for a in range(len(srcs)) if starts[a] <= s0)
                s_ref, s0, n_src = refs[k], s0 - starts[k], srcs[k].shape[0]
                if (o0, n) == (0, LANE) and s0 % BF16_ROWS == 0:
                    acc = s_ref[pl.ds(s0, LANE), :]
                    continue
                w0 = s0 // BF16_ROWS * BF16_ROWS
                width = -(-(s0 - w0 + n) // LANE) * LANE
                w0 = min(w0, n_src - width)
                i = lax.broadcasted_iota(jnp.int32, (LANE, width), 0)
                j = lax.broadcasted_iota(jnp.int32, (LANE, width), 1)
                pick = ((j - i == s0 - w0 - o0) & (i >= o0) & (i < o0 + n)).astype(dtype)
                part = _dot(pick, s_ref[pl.ds(w0, width), :])
                acc = part if acc is None else acc + part
            o_ref[rows, :] = acc[:size].astype(o_ref.dtype)

    return _pcall(body, name=name, out_shape=jax.ShapeDtypeStruct((n_dst, cols), dtype))(*srcs)


def _local_step(x, target, meta, attn_g, fox_b, ret_g, ffn_g, conv_w8, conv_b, final_g,
                first_weight, late_weights, ffn_grads_ready, out_grad_ready, in_grad_first_ready, in_grad_ready):
    seq, d = x.shape
    t = seq + PREFIX
    tm = TOK_TILE
    nq = t // tm
    fox_b128 = jnp.pad(fox_b, ((0, 0), (0, LANE - FOX_HEADS)))

    h0, n1 = _prep_norm(x, meta, attn_g, "prep_norm")
    w_in_t = first_weight(n1)
    proj = _mm_simple(n1, w_in_t, mode="nt", tm=tm, tn=IN_PAD, tk=d, out_dtype=F32, name="mm_in")
    cos, sin = _rope_tables(t)
    o_pre, mixed, states = _ret_fwd(proj, cos, sin, ret_g, "ret_fwd")
    c = _forget_cumsum(proj, fox_b128, "forget_cumsum")
    qa, ka, va, qt, vt = _fox_prep(proj, c, "fox_prep")
    by_block = lambda a: a.reshape(FOX_HEADS, nq, tm, LANE)
    mixed, o_fox, lse = _fox_fwd(qt, by_block(ka), vt, mixed, "fox_fwd")
    w_out, w_up_t, w_down = late_weights(o_fox)
    tile = pl.BlockSpec((tm, d), lambda i: (i, 0))
    row_vec = pl.BlockSpec((1, d), lambda i: (0, 0))
    resident = lambda shape: pl.BlockSpec(shape, lambda i: (0,) * len(shape), pipeline_mode=pl.Buffered(1))
    acts = lambda dtype: jax.ShapeDtypeStruct((t, d), dtype)
    vec = jax.ShapeDtypeStruct((1, d), F32)

    def residual_and_norm(i, acc, ins, outs):
        h = acc + ins[0][...]
        outs[0][...] = h
        outs[1][...] = (h * lax.rsqrt(jnp.mean(h * h, axis=-1, keepdims=True) + EPS) * ins[1][...]).astype(BF16)

    h1, n2 = _matmul_rows([mixed], [tile], [w_out], [resident((d, d))], [h0, ffn_g], [tile, row_vec],
                          [tile, tile], [acts(F32), acts(BF16)], residual_and_norm, mode="nn", steps=nq, name="mm_out_norm")
    nf = D_FF // 1408
    up, g = _up_conv_fwd(n2, w_up_t, conv_w8, conv_b, "up_conv_fwd")

    def residual_loss_bwd(i, acc, ins, outs):
        loss_ref, dh_ref, dhb_ref, gg_ref = outs
        part, dh, gg = _loss_tile(i, acc + ins[0][...], jnp.concatenate([ins[1][...], ins[2][...], ins[3][...]], axis=0),
                                  ins[4][...])
        _accumulate(loss_ref, i, jnp.broadcast_to(part, loss_ref.shape))
        dh_ref[...] = dh
        dhb_ref[...] = dh.astype(BF16)
        _accumulate(gg_ref, i, gg)

    loss_tile, dh2, dh2_b, g_final = _matmul_rows(
        [g], [pl.BlockSpec((tm, D_FF), lambda i: (i, 0))], [w_down], [resident((D_FF, d))],
        [h1, target, target, target, final_g], [tile] + _shifted_row_specs(d) + [row_vec],
        [pl.BlockSpec((8, LANE), lambda i: (0, 0)), tile, tile, row_vec],
        [jax.ShapeDtypeStruct((8, LANE), F32), acts(F32), acts(BF16), vec], residual_loss_bwd,
        mode="nn", steps=nq, name="mm_down_loss")

    tkw = 2112 if t % 2112 == 0 else tm
    gw_down = _mm_simple(g, dh2_b, mode="tn", tm=1408, tn=d, tk=tkw, out_dtype=BF16, name="mm_gw_down")
    dup, g_conv_w8, g_conv_b = _dg_conv_bwd(up, conv_w8, conv_b, dh2_b, w_down, "dg_conv_bwd")

    half = lambda p: pl.BlockSpec((None, tm, D_FF), lambda i: (p, i, 0))
    half_w = lambda p: pl.BlockSpec((None, D_FF, d), lambda i: (p, 0, 0), pipeline_mode=pl.Buffered(1))
    gw_up_t = _matmul(
        dup, n2, mode="tn", grid=(2 * nf, 1, t // tkw),
        a_spec=pl.BlockSpec((None, tkw, 1408), lambda i, j, k: (i // nf, k, i % nf)),
        b_spec=pl.BlockSpec((tkw, d), lambda i, j, k: (k, 0)),
        o_spec=pl.BlockSpec((1408, d), lambda i, j, k: (i, 0)),
        out_shape=jax.ShapeDtypeStruct((2 * D_FF, d), BF16), name="mm_gw_up")
    def norm_bwd_and_mixer_grad(i, acc, ins, outs):
        dh, gg = _rms_bwd_tile(acc, ins[0][...], ins[1][...], ins[2][...])
        outs[0][...] = dh
        _accumulate(outs[1], i, gg)
        outs[2][...] = _dot(dh.astype(BF16), ins[3][...], NT)

    dh1, g_ffn, dmixed = _matmul_rows(
        [dup, dup], [half(0), half(1)], [w_up_t, w_up_t], [half_w(0), half_w(1)],
        [h1, ffn_g, dh2, w_out], [tile, row_vec, tile, resident((d, d))], [tile, row_vec, tile],
        [acts(F32), vec, acts(F32)], norm_bwd_and_mixer_grad,
        mode="nn", steps=nq, name="mm_dn2_norm_bwd", after=ffn_grads_ready(gw_down, gw_up_t))
    gw_out = _mm_simple(mixed, dh1, mode="tn", tm=d, tn=d, tk=tkw, out_dtype=BF16, name="mm_gw_out")
    dproj, g_ret = _ret_bwd(proj, cos, sin, ret_g + out_grad_ready(gw_out), dmixed, o_pre, states, "ret_bwd")
    gw_in_first = _mm_simple(dproj, n1, mode="tn", tm=GW_IN_TILE, tn=d, tk=tkw, out_dtype=BF16, name="mm_gw_in_first",
                             m_tiles=GW_IN_FIRST_TILES)
    qab, doa = _fox_prep_bwd(dmixed, o_fox, lse, qa, in_grad_first_ready(gw_in_first), "fox_prep_bwd")
    dproj, drs, dcs = _fox_bwd(by_block(qab), by_block(doa), by_block(ka), by_block(va), dproj, "fox_bwd")
    dproj, g_fox_b = _forget_cumsum_bwd(proj, fox_b128, drs, dcs, dproj, "forget_cumsum_bwd")
    gw_in_rest = _mm_simple(dproj, n1, mode="tn", tm=GW_IN_TILE, tn=d, tk=tkw, out_dtype=BF16, name="mm_gw_in",
                            m_tiles=IN_PAD // GW_IN_TILE - GW_IN_FIRST_TILES, m_first=GW_IN_FIRST_TILES)
    sent = in_grad_ready(gw_in_first, gw_in_rest)
    def input_grads(i, acc, ins, outs):
        gx_ref, gmeta_ref, gg_ref, buf_ref, sems = outs
        dh, gg = _rms_bwd_tile(acc, ins[0][...], ins[1][...], ins[2][...])
        _accumulate(gg_ref, i, gg)
        slot = i % 2

        def first_copy():
            return pltpu.make_async_copy(buf_ref.at[0, pl.ds(PREFIX, tm - PREFIX)], gx_ref.at[pl.ds(0, tm - PREFIX)],
                                         sems.at[0])

        def tile_copy(tile, buf_slot):
            rows = pl.ds(pl.multiple_of(tile * tm - PREFIX, PREFIX), tm)
            return pltpu.make_async_copy(buf_ref.at[buf_slot], gx_ref.at[rows], sems.at[buf_slot])

        @pl.when(i == 1)
        def _():
            first_copy().wait()

        @pl.when(i >= 2)
        def _():
            tile_copy(i - 1, 1 - slot).wait()

        buf_ref[slot] = dh

        @pl.when(i == 0)
        def _():
            gmeta_ref[...] = dh[N_PAD:PREFIX, :]
            first_copy().start()

        @pl.when(i > 0)
        def _():
            tile_copy(i, slot).start()

        @pl.when(i == nq - 1)
        def _():
            tile_copy(i, slot).wait()

    grad_x, g_meta, g_attn = _matmul_rows(
        [dproj], [pl.BlockSpec((tm, IN_PAD), lambda i: (i, 0))], [w_in_t], [resident((IN_PAD, d))],
        [h0, attn_g, dh1], [tile, row_vec, tile],
        [pl.BlockSpec(memory_space=pl.ANY), pl.BlockSpec((N_META, d), lambda i: (0, 0)), row_vec],
        [jax.ShapeDtypeStruct((seq, d), F32), jax.ShapeDtypeStruct((N_META, d), F32), vec], input_grads,
        mode="nn", steps=nq, name="mm_dn1_norm_bwd", after=sent,
        scratch=[pltpu.VMEM((2, tm, d), F32), pltpu.SemaphoreType.DMA((2,))])

    grads = dict(meta=g_meta, attn_g=g_attn, fox_b=g_fox_b, ret_g=g_ret,
                 ffn_g=g_ffn, conv_w=g_conv_w8, conv_b=g_conv_b, final_g=g_final)
    return loss_tile, grad_x, grads


def kernel(x, meta_tokens, attn_norm_g, w_in, fox_forget_b, ret_norm_g, w_out, ffn_norm_g, w_up, conv_w, conv_b, w_down, final_norm_g, loss_target, m_meta_tokens, m_attn_norm_g, m_w_in, m_fox_forget_b, m_ret_norm_g, m_w_out, m_ffn_norm_g, m_w_up, m_conv_w, m_conv_b, m_w_down, m_final_norm_g, v_meta_tokens, v_attn_norm_g, v_w_in, v_fox_forget_b, v_ret_norm_g, v_w_out, v_ffn_norm_g, v_w_up, v_conv_w, v_conv_b, v_w_down, v_final_norm_g):
    d = D_MODEL
    me = 4 * lax.axis_index("x") + 2 * lax.axis_index("y") + lax.axis_index("c")
    in_blk, in_blk_pad = IN_BLOCK, IN_BLOCK_PAD
    up_blk = 2 * D_FF // N_DEV
    down_blk = D_FF // N_DEV
    cw_blk = D_FF // N_DEV

    w_in_loc = jnp.pad(w_in[0].T.astype(BF16), ((0, in_blk_pad - in_blk), (0, 0)))
    cw_loc = jnp.pad(conv_w[0], ((0, 5), (0, 384 - cw_blk)))
    g_meta, g_cw = _exchange([meta_tokens, cw_loc], ["gather"] * 2, "gather_small")
    first = _exchange_start([w_in_loc], ["gather"], "gather_in_start", after=g_meta, relations=SAME_CORE_AND_SIBLING)
    rest_loc = [(w_out[0] + first[-1][0:1, 0:1]).astype(BF16), w_up[0].T.astype(BF16), w_down[0].astype(BF16)]
    rest = _exchange_start(rest_loc, ["gather"] * 3, "gather_rest_start")
    meta_f = g_meta.transpose(1, 0, 2).reshape(N_META, d)
    conv_w8 = jnp.pad(g_cw[:, :3, :cw_blk].transpose(1, 0, 2).reshape(3, D_FF), ((0, 5), (0, 0)))
    pending = {}

    def first_weight(after):
        own_in, landed = _exchange_wait(first, ["gather"], after, "gather_in_wait", relations=SAME_CORE_AND_SIBLING,
                                        with_sources=True)
        onward = _exchange_start(own_in, ["forward"], "gather_in_forward_start", relations=OTHER_CHIPS, lands=landed,
                                 own=False)
        (g_in,) = _exchange_wait(onward, ["forward"], onward[-1], "gather_in_forward_wait", own=False,
                                 relations=OTHER_CHIPS)
        return _move_rows([g_in.reshape(IN_PAD, d)], _slot_row_of_internal(), "w_in_rows")

    first_rows = FIRST_SLOTS * in_blk_pad
    first_slots, other_slots = (0, FIRST_SLOTS), (FIRST_SLOTS, N_DEV)

    def in_grad_first_ready(gw_first):
        blocks = _move_rows([gw_first], _internal_row_of_slot()[:first_rows], "gw_in_first_rows")
        pending["in_first"] = _exchange_start([blocks.reshape(FIRST_SLOTS, in_blk_pad, d)], ["scatter"],
                                              "grads_in_first_start", slots=first_slots)
        return pending["in_first"][-1]

    def in_grad_ready(gw_first, gw_rest):
        blocks = _move_rows([gw_first, gw_rest], _internal_row_of_slot()[first_rows:], "gw_in_rows")
        landed = _exchange_wait(pending["in_first"], ["scatter"], blocks, "grads_in_first_wait", slots=first_slots)
        pending["in"] = _exchange_start([blocks.reshape(N_DEV - FIRST_SLOTS, in_blk_pad, d)], ["scatter"],
                                        "grads_in_start", lands=landed, slots=other_slots)
        return pending["in"][-1][0:1, 0:1]

    def late_weights(after):
        g_out, g_up, g_down = _exchange_wait(rest, ["gather"] * 3, after, "gather_rest_wait")
        return g_out.reshape(d, d), g_up.reshape(2, D_FF, d), g_down.reshape(D_FF, d)

    def ffn_grads_ready(gw_down, gw_up_t):
        blocks = [gw_down.reshape(N_DEV, down_blk, d), gw_up_t.reshape(N_DEV, up_blk, d)]
        pending["ffn"] = _exchange_start(blocks, ["scatter"] * 2, "grads_ffn_start")
        return pending["ffn"][-1][0:1, 0:1]

    def out_grad_ready(gw_out):
        pending["out"] = _exchange_start([gw_out.reshape(N_DEV, d // N_DEV, d)], ["scatter"], "grads_out_start")
        return pending["out"][-1][0:1, 0:1]

    loss_tile, grad_x, gr = _local_step(
        x[0], loss_target[0], meta_f, attn_norm_g + rest[-1][0:1, 0:1], fox_forget_b, ret_norm_g, ffn_norm_g,
        conv_w8, conv_b, final_norm_g.reshape(1, d), first_weight, late_weights, ffn_grads_ready, out_grad_ready,
        in_grad_first_ready, in_grad_ready)

    small = [loss_tile, gr["attn_g"], gr["fox_b"], gr["ret_g"], gr["ffn_g"], gr["conv_b"], gr["final_g"],
             gr["meta"], gr["conv_w"]]
    small_kinds = ["gather"] * len(small)
    small_started = _exchange_start(small, small_kinds, "grads_small_start", own=False)

    r_down, r_up = _exchange_wait(pending["ffn"], ["scatter"] * 2, small_started[-1], "grads_ffn_wait")
    (r_out,) = _exchange_wait(pending["out"], ["scatter"], small_started[-1], "grads_out_wait")
    g_w_out = _sum_slots(r_out, "sum_w_out", d // N_DEV)
    g_w_up_t = _sum_slots(r_up, "sum_w_up", up_blk)
    g_w_down = _sum_slots(r_down, "sum_w_down", down_blk)
    as_t = lambda a: a[0].T
    from_t = lambda a: a.T[None]
    d_w_out, m_w_out_n, v_w_out_n = [a[None] for a in _adamw(w_out[0], g_w_out, m_w_out[0], v_w_out[0], "adamw_w_out", 128)]
    up_t = _adamw(as_t(w_up), g_w_up_t, as_t(m_w_up), as_t(v_w_up), "adamw_w_up", up_blk // 2)
    d_w_up, m_w_up_n, v_w_up_n = [from_t(a) for a in up_t]
    d_w_down, m_w_down_n, v_w_down_n = [a[None] for a in _adamw(w_down[0], g_w_down, m_w_down[0], v_w_down[0],
                                                                "adamw_w_down", down_blk)]

    own_small, r_small = _exchange_wait(small_started, small_kinds, up_t[0], "grads_small_wait", own=False,
                                        with_sources=True)
    (loss_all, g_attn, g_fox_b128, g_ret, g_ffn, g_conv_b, g_final, g_meta_full, g_cw_full) = _sum_slots_small(
        r_small, own_small, "sum_small")
    loss = loss_all[0, 0]
    g_fox_b = g_fox_b128[:, :FOX_HEADS]
    g_meta_loc = lax.dynamic_slice(g_meta_full, (0, me * (d // N_DEV)), (N_META, d // N_DEV))
    g_cw_loc = lax.dynamic_slice(g_cw_full, (0, me * cw_blk), (3, cw_blk))

    (r_in,) = _exchange_wait(pending["in"], ["scatter"], r_small[0], "grads_in_wait", slots=other_slots)
    g_w_in, d_w_in, m_w_in_n, v_w_in_n = [from_t(a) for a in _adamw_from_slots(
        as_t(w_in), r_in, as_t(m_w_in), as_t(v_w_in), "adamw_w_in")]
    g_w_in, g_w_up = g_w_in[0], g_w_up_t.T
    row = lambda a: a.reshape(1, d)
    sm_grads = [g_meta_loc, g_attn, g_fox_b, g_ret, g_ffn, g_cw_loc, g_conv_b, g_final]
    sm_w = [meta_tokens, attn_norm_g, fox_forget_b, ret_norm_g, ffn_norm_g, conv_w[0], conv_b, row(final_norm_g)]
    sm_m = [m_meta_tokens, m_attn_norm_g, m_fox_forget_b, m_ret_norm_g, m_ffn_norm_g, m_conv_w[0], m_conv_b,
            row(m_final_norm_g)]
    sm_v = [v_meta_tokens, v_attn_norm_g, v_fox_forget_b, v_ret_norm_g, v_ffn_norm_g, v_conv_w[0], v_conv_b,
            row(v_final_norm_g)]
    dl, ml, vl = [lst[:7] + [lst[7].reshape(d)] for lst in _adamw_small(sm_w, sm_grads, sm_m, sm_v, "adamw_small")]

    def by_weight(meta_, attn_, w_in_, fox_, ret_, w_out_, ffn_, w_up_, cw_, cb_, w_down_, final_):
        return (meta_, attn_, w_in_, fox_, ret_, w_out_, ffn_, w_up_, cw_[None], cb_, w_down_, final_)

    grads_out = by_weight(g_meta_loc, g_attn, g_w_in[None], g_fox_b, g_ret, g_w_out[None], g_ffn, g_w_up[None], g_cw_loc,
                          g_conv_b, g_w_down[None], g_final.reshape(d))
    delta_out = by_weight(dl[0], dl[1], d_w_in, dl[2], dl[3], d_w_out, dl[4], d_w_up, dl[5], dl[6], d_w_down, dl[7])
    m_out = by_weight(ml[0], ml[1], m_w_in_n, ml[2], ml[3], m_w_out_n, ml[4], m_w_up_n, ml[5], ml[6], m_w_down_n, ml[7])
    v_out = by_weight(vl[0], vl[1], v_w_in_n, vl[2], vl[3], v_w_out_n, vl[4], v_w_up_n, vl[5], vl[6], v_w_down_n, vl[7])
    return (loss, grad_x[None]) + grads_out + delta_out + m_out + v_out
```

```python
import numpy as np
import jax
import jax.numpy as jnp
from jax import lax
from jax.experimental import pallas as pl
from jax.experimental.pallas import tpu as pltpu

F32 = jnp.float32
BF16 = jnp.bfloat16

D_MODEL = 1024
N_META = 16
N_PAD = 112
PREFIX = 128
RET_HEADS = 4
RET_DK = 64
RET_DV = 128
FOX_HEADS = 8
FOX_DH = 64
D_FF = 2816
ROPE_BASE = 10000.0
EPS = 1e-6
NEG = -1e30
RET_QK = RET_HEADS * RET_DK
RET_V = RET_HEADS * RET_DV
FOX_W = FOX_HEADS * FOX_DH
IN_WIDTH = 2 * RET_QK + 2 * RET_V + 3 * FOX_W + FOX_HEADS
IN_PAD = 3200
FF_COL_BLOCK = (IN_WIDTH - FOX_HEADS) // 128
QK_SCALE = 0.125

ADAM_LR = 0.001
ADAM_B1 = 0.9
ADAM_B2 = 0.999
ADAM_EPS = 1e-08
ADAM_WD = 0.01
ADAM_STEP = 10

N_DEV = 8
LANE = 128
ROW_TILE = 128
TOK_TILE = 384

NN = (((1,), (0,)), ((), ()))
NT = (((1,), (1,)), ((), ()))
TN = (((0,), (0,)), ((), ()))


def _pcall(body, **kw):
    return pl.pallas_call(body, **kw)


def _params(*sem):
    return pltpu.CompilerParams(dimension_semantics=sem)


def _dot(a, b, dims=NN):
    return lax.dot_general(a, b, dims, preferred_element_type=F32)


def _sigmoid(x):
    return 0.5 * jnp.tanh(0.5 * x) + 0.5


def _matmul(a, b, *, mode, grid, a_spec, b_spec, o_spec, out_shape, name, add=None, add_spec=None, after=None):
    dims = {"nn": NN, "nt": NT, "tn": TN}[mode]
    nk = grid[2]
    has_add = add is not None
    a_list, b_list = (list(a), list(b)) if isinstance(a, (list, tuple)) else ([a], [b])
    a_specs, b_specs = (list(a_spec), list(b_spec)) if isinstance(a_spec, (list, tuple)) else ([a_spec], [b_spec])
    nt = len(a_list)
    n_in = 2 * nt + int(has_add) + int(after is not None)

    def body(*refs):
        a_refs, b_refs = refs[:nt], refs[nt:2 * nt]
        add_ref = refs[2 * nt] if has_add else None
        o_ref = refs[n_in]
        part = _dot(a_refs[0][...].astype(BF16), b_refs[0][...].astype(BF16), dims)
        for ar, br in zip(a_refs[1:], b_refs[1:]):
            part = part + _dot(ar[...].astype(BF16), br[...].astype(BF16), dims)

        def finish(acc):
            if has_add:
                acc = acc + add_ref[...]
            o_ref[...] = acc.astype(o_ref.dtype)

        if nk == 1:
            finish(part)
        else:
            acc_ref = refs[-1]
            k = pl.program_id(2)

            @pl.when(k == 0)
            def _():
                acc_ref[...] = part

            @pl.when(k > 0)
            def _():
                acc_ref[...] += part

            @pl.when(k == nk - 1)
            def _():
                finish(acc_ref[...])

    in_specs = a_specs + b_specs + ([add_spec] if has_add else [])
    args = tuple(a_list) + tuple(b_list) + ((add,) if has_add else ())
    if after is not None:
        in_specs, args = in_specs + [pl.BlockSpec(memory_space=pl.ANY)], args + (after,)
    scratch = [] if nk == 1 else [pltpu.VMEM(tuple(d for d in o_spec.block_shape if d is not None), F32)]
    return _pcall(
        body, name=name, grid=grid, in_specs=in_specs, out_specs=o_spec, out_shape=out_shape,
        scratch_shapes=scratch, compiler_params=_params("parallel", "parallel", "arbitrary"),
    )(*args)


def _mm_simple(a, b, *, mode, tm, tn, tk, out_dtype, name, add=None, after=None):
    if mode == "tn":
        K, M = a.shape
    else:
        M, K = a.shape
    N = b.shape[0] if mode == "nt" else b.shape[1]
    grid = (M // tm, N // tn, K // tk)
    resident = dict(pipeline_mode=pl.Buffered(1)) if (tn == N and tk == K) else {}
    a_spec = pl.BlockSpec((tk, tm), lambda i, j, k: (k, i)) if mode == "tn" else pl.BlockSpec((tm, tk), lambda i, j, k: (i, k))
    b_spec = (pl.BlockSpec((tn, tk), lambda i, j, k: (j, k), **resident) if mode == "nt"
              else pl.BlockSpec((tk, tn), lambda i, j, k: (k, j), **resident))
    o_spec = pl.BlockSpec((tm, tn), lambda i, j, k: (i, j))
    return _matmul(a, b, mode=mode, grid=grid, a_spec=a_spec, b_spec=b_spec, o_spec=o_spec,
                   out_shape=jax.ShapeDtypeStruct((M, N), out_dtype), name=name, add=add,
                   add_spec=o_spec if add is not None else None, after=after)


def _matmul_rows(a_list, a_specs, b_list, b_specs, extras, extra_specs, out_specs, out_shape, epilogue, *,
                 mode, steps, name, after=None, scratch=()):
    dims = {"nn": NN, "nt": NT}[mode]
    nt, ne = len(a_list), len(extras)
    n_in = 2 * nt + ne + int(after is not None)

    def body(*refs):
        acc = _dot(refs[0][...].astype(BF16), refs[nt][...].astype(BF16), dims)
        for k in range(1, nt):
            acc = acc + _dot(refs[k][...].astype(BF16), refs[nt + k][...].astype(BF16), dims)
        epilogue(pl.program_id(0), acc, refs[2 * nt:2 * nt + ne], refs[n_in:])

    in_specs = list(a_specs) + list(b_specs) + list(extra_specs)
    args = tuple(a_list) + tuple(b_list) + tuple(extras)
    if after is not None:
        in_specs, args = in_specs + [pl.BlockSpec(memory_space=pl.ANY)], args + (after,)
    return _pcall(body, name=name, grid=(steps,), in_specs=in_specs, out_specs=out_specs, out_shape=out_shape,
                  scratch_shapes=list(scratch), compiler_params=_params("arbitrary"))(*args)


def _rms_bwd_tile(dy, x, gain, dres):
    r = lax.rsqrt(jnp.mean(x * x, axis=-1, keepdims=True) + EPS)
    xhat = x * r
    u = dy * gain
    return dres + r * (u - xhat * jnp.mean(u * xhat, axis=-1, keepdims=True)), jnp.sum(dy * xhat, axis=0, keepdims=True)


def _loss_tile(i, x, tgt, gain):
    d = x.shape[-1]
    r = lax.rsqrt(jnp.mean(x * x, axis=-1, keepdims=True) + EPS)
    xhat = x * r
    counted = (i * TOK_TILE + lax.broadcasted_iota(jnp.int32, (TOK_TILE, 1), 0)) >= PREFIX
    err = jnp.where(counted, xhat * gain - tgt, 0.0)
    dy = err * (1.0 / d)
    u = dy * gain
    dh = r * (u - xhat * jnp.mean(u * xhat, axis=-1, keepdims=True))
    return 0.5 * jnp.sum(jnp.mean(err * err, axis=-1, keepdims=True)), dh, jnp.sum(dy * xhat, axis=0, keepdims=True)


def _accumulate(ref, i, part):
    @pl.when(i == 0)
    def _():
        ref[...] = part

    @pl.when(i > 0)
    def _():
        ref[...] += part


def _prep_norm(x, meta, gain, name):
    seq, d = x.shape
    t = seq + PREFIX

    def body(xa_ref, xb_ref, xc_ref, meta_ref, g_ref, h_ref, n_ref):
        i = pl.program_id(0)

        @pl.when(i == 0)
        def _():
            h_ref[0:N_PAD, :] = jnp.zeros((N_PAD, d), F32)
            h_ref[N_PAD:ROW_TILE, :] = meta_ref[...]

        @pl.when(i > 0)
        def _():
            h_ref[0:ROW_TILE, :] = xa_ref[...]

        h_ref[ROW_TILE:2 * ROW_TILE, :] = xb_ref[...]
        h_ref[2 * ROW_TILE:3 * ROW_TILE, :] = xc_ref[...]
        h = h_ref[...]
        r = lax.rsqrt(jnp.mean(h * h, axis=-1, keepdims=True) + EPS)
        n_ref[...] = (h * r * g_ref[...]).astype(BF16)

    return _pcall(
        body, name=name, grid=(t // TOK_TILE,),
        in_specs=_shifted_row_specs(d) + [pl.BlockSpec((N_META, d), lambda i: (0, 0)), pl.BlockSpec((1, d), lambda i: (0, 0))],
        out_specs=[pl.BlockSpec((TOK_TILE, d), lambda i: (i, 0)), pl.BlockSpec((TOK_TILE, d), lambda i: (i, 0))],
        out_shape=[jax.ShapeDtypeStruct((t, d), F32), jax.ShapeDtypeStruct((t, d), BF16)],
        compiler_params=_params("parallel"),
    )(x, x, x, meta, gain)


def _shifted_row_specs(d):
    blocks_per_tile = TOK_TILE // ROW_TILE
    return [pl.BlockSpec((ROW_TILE, d), lambda i, r=r: (jnp.maximum(blocks_per_tile * i + r, 0), 0)) for r in (-1, 0, 1)]


def _ret_consts(bk):
    gam = 1.0 - 2.0 ** (-5.0 - np.arange(RET_HEADS))
    n = np.arange(bk)
    same_or_earlier_chunk = (n[None, :] // 64) <= (n[:, None] // 64)
    w = gam[:, None, None] ** np.abs(n[:, None] - n[None, :])[None] * same_or_earlier_chunk[None]
    wq = gam[:, None] ** (n[None, :] + 1.0)
    wk = gam[:, None] ** (bk - 1.0 - n[None, :])
    mask = (np.arange(RET_QK)[None, :] // RET_DK) == np.arange(RET_HEADS)[:, None]
    return (jnp.asarray(w, F32), jnp.asarray(wq[:, :, None], F32), jnp.asarray(wk[:, :, None], F32),
            jnp.asarray(mask[:, None, :], F32), [float(g ** bk) for g in gam])


def _rope_tables(t):
    half = RET_DK // 2
    inv = 1.0 / (ROPE_BASE ** (jnp.arange(half, dtype=F32) / half))
    ang = jnp.arange(t).astype(F32)[:, None] * inv[None, :]
    cos, sin = jnp.cos(ang), jnp.sin(ang)
    return (jnp.tile(jnp.concatenate([cos, cos], axis=1), (1, RET_HEADS)),
            jnp.tile(jnp.concatenate([-sin, sin], axis=1), (1, RET_HEADS)))


def _swap_halves(x):
    outs = []
    for s in range(x.shape[1] // LANE):
        xs = x[:, LANE * s:LANE * (s + 1)]
        lane = lax.broadcasted_iota(jnp.int32, xs.shape, 1)
        outs.append(jnp.where((lane & 32) == 0, pltpu.roll(xs, LANE - 32, axis=1), pltpu.roll(xs, 32, axis=1)))
    return outs[0] if len(outs) == 1 else jnp.concatenate(outs, axis=1)


def _rope(x, cos, sin_signed):
    return x * cos + _swap_halves(x) * sin_signed


def _rope_t(dx, cos, sin_signed):
    return dx * cos + _swap_halves(dx * sin_signed)


def _ret_fwd(proj, cos, sin, gain, name):
    t = proj.shape[0]
    bk = TOK_TILE
    nb = t // bk
    w, wq, wk, mask, g_blk = _ret_consts(bk)

    def body(q_ref, k_ref, v_ref, rg_ref, cos_ref, sin_ref, w_ref, wq_ref, wk_ref, mask_ref, gain_ref,
             opre_ref, og_ref, st_ref, r_ref):
        i = pl.program_id(0)

        @pl.when(i == 0)
        def _():
            r_ref[...] = jnp.zeros_like(r_ref)

        c, s = cos_ref[...], sin_ref[...]
        valid = ((i * bk + lax.broadcasted_iota(jnp.int32, (bk, 1), 0)) >= N_PAD).astype(F32)
        qr = _rope(q_ref[...], c, s)
        kr = _rope(k_ref[...], c, s) * QK_SCALE * valid
        kb = kr.astype(BF16)
        for h in range(RET_HEADS):
            hm = mask_ref[h]
            cols = slice(RET_DV * h, RET_DV * (h + 1))
            vh = v_ref[:, cols].astype(BF16)
            r_prev = r_ref[h]
            st_ref[0, h] = r_prev
            sm = _dot((qr * hm).astype(BF16), kb, NT) * w_ref[h]
            o = _dot(sm.astype(BF16), vh) + _dot((qr * (hm * wq_ref[h])).astype(BF16), r_prev.astype(BF16))
            r_ref[h] = g_blk[h] * r_prev + _dot((kr * wk_ref[h]).astype(BF16), vh, TN)
            opre_ref[:, cols] = o
            rstd = lax.rsqrt(jnp.mean(o * o, axis=-1, keepdims=True) + EPS)
            rg = rg_ref[:, cols]
            og_ref[:, cols] = (o * rstd * gain_ref[:, cols] * (rg * _sigmoid(rg))).astype(BF16)

    full = lambda shape: pl.BlockSpec(shape, lambda i: (0,) * len(shape))
    return _pcall(
        body, name=name, grid=(nb,),
        in_specs=[pl.BlockSpec((bk, RET_QK), lambda i: (i, 0)), pl.BlockSpec((bk, RET_QK), lambda i: (i, 1)),
                  pl.BlockSpec((bk, RET_V), lambda i: (i, 1)), pl.BlockSpec((bk, RET_V), lambda i: (i, 2)),
                  pl.BlockSpec((bk, RET_QK), lambda i: (i, 0)), pl.BlockSpec((bk, RET_QK), lambda i: (i, 0)),
                  full((RET_HEADS, bk, bk)), full((RET_HEADS, bk, 1)), full((RET_HEADS, bk, 1)),
                  full((RET_HEADS, 1, RET_QK)), full((1, RET_V))],
        out_specs=[pl.BlockSpec((bk, RET_V), lambda i: (i, 0)), pl.BlockSpec((bk, RET_V), lambda i: (i, 0)),
                   pl.BlockSpec((1, RET_HEADS, RET_QK, RET_DV), lambda i: (i, 0, 0, 0))],
        out_shape=[jax.ShapeDtypeStruct((t, RET_V), F32), jax.ShapeDtypeStruct((t, RET_V + FOX_W), BF16),
                   jax.ShapeDtypeStruct((nb, RET_HEADS, RET_QK, RET_DV), F32)],
        scratch_shapes=[pltpu.VMEM((RET_HEADS, RET_QK, RET_DV), F32)],
        compiler_params=_params("arbitrary"),
    )(proj, proj, proj, proj, cos, sin, w, wq, wk, mask, gain)


def _ret_bwd(proj, cos, sin, gain, dmixed, opre, states, name):
    t = proj.shape[0]
    bk = TOK_TILE
    nb = t // bk
    w, wq, wk, mask, g_blk = _ret_consts(bk)
    v0, g0 = 2 * RET_QK, 2 * RET_QK + RET_V

    def body(q_ref, k_ref, v_ref, rg_ref, cos_ref, sin_ref, w_ref, wq_ref, wk_ref, mask_ref, gain_ref,
             dog_ref, opre_ref, st_ref, dp_ref, gg_ref, dr_ref):
        step = pl.program_id(0)
        i = nb - 1 - step

        @pl.when(step == 0)
        def _():
            dr_ref[...] = jnp.zeros_like(dr_ref)
            gg_ref[...] = jnp.zeros_like(gg_ref)

        c, s = cos_ref[...], sin_ref[...]
        valid = ((i * bk + lax.broadcasted_iota(jnp.int32, (bk, 1), 0)) >= N_PAD).astype(F32)
        qr = _rope(q_ref[...], c, s)
        kr = _rope(k_ref[...], c, s) * QK_SCALE * valid
        kb = kr.astype(BF16)
        dqr = jnp.zeros((bk, RET_QK), F32)
        dkr = jnp.zeros((bk, RET_QK), F32)
        for h in range(RET_HEADS):
            hm = mask_ref[h]
            cols = slice(RET_DV * h, RET_DV * (h + 1))
            vh = v_ref[:, cols].astype(BF16)
            o = opre_ref[:, cols]
            rstd = lax.rsqrt(jnp.mean(o * o, axis=-1, keepdims=True) + EPS)
            xhat = o * rstd
            rg = rg_ref[:, cols]
            sg = _sigmoid(rg)
            gate = rg * sg
            gn = gain_ref[:, cols]
            dog = dog_ref[:, cols]
            dp_ref[:, g0 + RET_DV * h:g0 + RET_DV * (h + 1)] = (
                dog * xhat * gn * (sg * (1.0 + rg * (1.0 - sg)))).astype(BF16)
            gg_ref[:, cols] += jnp.sum(dog * xhat * gate, axis=0, keepdims=True)
            dxh = dog * gn * gate
            do = (rstd * (dxh - xhat * jnp.mean(dxh * xhat, axis=-1, keepdims=True))).astype(BF16)
            qm = (qr * hm).astype(BF16)
            qw = (qr * (hm * wq_ref[h])).astype(BF16)
            kw = (kr * wk_ref[h]).astype(BF16)
            wh = w_ref[h]
            sm = (_dot(qm, kb, NT) * wh).astype(BF16)
            ds = (_dot(do, vh, NT) * wh).astype(BF16)
            dr = dr_ref[h]
            drb = dr.astype(BF16)
            dp_ref[:, v0 + RET_DV * h:v0 + RET_DV * (h + 1)] = (_dot(sm, do, TN) + _dot(kw, drb)).astype(BF16)
            dqr = dqr + _dot(ds, kb) * hm + _dot(do, st_ref[0, h].astype(BF16), NT) * (hm * wq_ref[h])
            dkr = dkr + _dot(ds, qm, TN) + _dot(vh, drb, NT) * wk_ref[h]
            dr_ref[h] = g_blk[h] * dr + _dot(qw, do, TN)
        dp_ref[:, 0:RET_QK] = _rope_t(dqr, c, s).astype(BF16)
        dp_ref[:, RET_QK:2 * RET_QK] = _rope_t(dkr * (QK_SCALE * valid), c, s).astype(BF16)

    full = lambda shape: pl.BlockSpec(shape, lambda i: (0,) * len(shape))
    rev = lambda col: (lambda i: (nb - 1 - i, col))
    return _pcall(
        body, name=name, grid=(nb,),
        in_specs=[pl.BlockSpec((bk, RET_QK), rev(0)), pl.BlockSpec((bk, RET_QK), rev(1)),
                  pl.BlockSpec((bk, RET_V), rev(1)), pl.BlockSpec((bk, RET_V), rev(2)),
                  pl.BlockSpec((bk, RET_QK), rev(0)), pl.BlockSpec((bk, RET_QK), rev(0)),
                  full((RET_HEADS, bk, bk)), full((RET_HEADS, bk, 1)), full((RET_HEADS, bk, 1)),
                  full((RET_HEADS, 1, RET_QK)), full((1, RET_V)),
                  pl.BlockSpec((bk, RET_V), rev(0)), pl.BlockSpec((bk, RET_V), rev(0)),
                  pl.BlockSpec((1, RET_HEADS, RET_QK, RET_DV), lambda i: (nb - 1 - i, 0, 0, 0))],
        out_specs=[pl.BlockSpec((bk, g0 + RET_V), rev(0)), pl.BlockSpec((1, RET_V), lambda i: (0, 0))],
        out_shape=[jax.ShapeDtypeStruct((t, IN_PAD), BF16), jax.ShapeDtypeStruct((1, RET_V), F32)],
        scratch_shapes=[pltpu.VMEM((RET_HEADS, RET_QK, RET_DV), F32)],
        compiler_params=_params("arbitrary"),
    )(proj, proj, proj, proj, cos, sin, w, wq, wk, mask, gain, dmixed, opre, states)


def _forget_cumsum(proj, bias, name):
    t = proj.shape[0]
    rt = TOK_TILE
    nb = t // rt
    tril = jnp.asarray(np.tril(np.ones((rt, rt))), F32)

    def body(z_ref, b_ref, tril_ref, c_ref, carry_ref):
        i = pl.program_id(0)

        @pl.when(i == 0)
        def _():
            carry_ref[...] = jnp.zeros_like(carry_ref)

        z = z_ref[...] + b_ref[...]
        logf = jnp.minimum(z, 0.0) - jnp.log(1.0 + jnp.exp(-jnp.abs(z)))
        c = lax.dot_general(tril_ref[...], logf, NN, precision=lax.Precision.HIGHEST,
                            preferred_element_type=F32) + carry_ref[...]
        c_ref[...] = c
        carry_ref[...] = c[rt - 1:rt, :]

    return _pcall(
        body, name=name, grid=(nb,),
        in_specs=[pl.BlockSpec((rt, LANE), lambda i: (i, FF_COL_BLOCK)), pl.BlockSpec((1, LANE), lambda i: (0, 0)),
                  pl.BlockSpec((rt, rt), lambda i: (0, 0))],
        out_specs=pl.BlockSpec((rt, LANE), lambda i: (i, 0)),
        out_shape=jax.ShapeDtypeStruct((t, LANE), F32),
        scratch_shapes=[pltpu.VMEM((1, LANE), F32)],
        compiler_params=_params("arbitrary"),
    )(proj, bias, tril)


def _forget_cumsum_bwd(proj, bias, drs, dcs, dproj, name):
    t = proj.shape[0]
    rt = TOK_TILE
    nb = t // rt
    triu = jnp.asarray(np.triu(np.ones((rt, rt))), F32)

    def body(z_ref, b_ref, triu_ref, drs_ref, dcs_ref, dproj_in, dz_ref, gb_ref, carry_ref):
        step = pl.program_id(0)

        @pl.when(step == 0)
        def _():
            carry_ref[...] = jnp.zeros_like(carry_ref)
            gb_ref[...] = jnp.zeros_like(gb_ref)

        dlogf = lax.dot_general(triu_ref[...], drs_ref[...] - dcs_ref[...], NN, precision=lax.Precision.HIGHEST,
                                preferred_element_type=F32) + carry_ref[...]
        carry_ref[...] = dlogf[0:1, :]
        z = z_ref[...] + b_ref[...]
        is_head = lax.broadcasted_iota(jnp.int32, (rt, LANE), 1) < FOX_HEADS
        dz = jnp.where(is_head, dlogf / (1.0 + jnp.exp(z)), 0.0)
        dz_ref[...] = dz.astype(BF16)
        gb_ref[...] += jnp.sum(dz, axis=0, keepdims=True)

    return _pcall(
        body, name=name, grid=(nb,),
        in_specs=[pl.BlockSpec((rt, LANE), lambda i: (nb - 1 - i, FF_COL_BLOCK)),
                  pl.BlockSpec((1, LANE), lambda i: (0, 0)),
                  pl.BlockSpec((rt, rt), lambda i: (0, 0)),
                  pl.BlockSpec((rt, LANE), lambda i: (nb - 1 - i, 0)),
                  pl.BlockSpec((rt, LANE), lambda i: (nb - 1 - i, 0)),
                  pl.BlockSpec(memory_space=pl.ANY)],
        out_specs=[pl.BlockSpec((rt, LANE), lambda i: (nb - 1 - i, FF_COL_BLOCK)),
                   pl.BlockSpec((1, LANE), lambda i: (0, 0))],
        out_shape=[jax.ShapeDtypeStruct(dproj.shape, BF16), jax.ShapeDtypeStruct((1, LANE), F32)],
        input_output_aliases={5: 0},
        scratch_shapes=[pltpu.VMEM((1, LANE), F32)],
        compiler_params=_params("arbitrary"),
    )(proj, bias, triu, drs, dcs, dproj)


FOX_PAIRS = FOX_HEADS // 2
L_ONE_Q = FOX_DH
L_ONE_K = FOX_DH + 3
L_LSE = FOX_DH + 4


def _split3(x):
    hi = x.astype(BF16).astype(F32)
    r = x - hi
    mid = r.astype(BF16).astype(F32)
    return hi, mid, r - mid


def _head_to_low(slab, e):
    return slab if e == 0 else pltpu.roll(slab, FOX_DH, axis=1)


def _pair(a, b, low):
    return jnp.where(low, a, pltpu.roll(b, FOX_DH, axis=1))


def _fox_prep(proj, c, name):
    t = proj.shape[0]
    tq = TOK_TILE

    def body(p_ref, c_ref, qa_ref, ka_ref, va_ref, qt_ref, vt_ref):
        i = pl.program_id(0)
        lane = lax.broadcasted_iota(jnp.int32, (tq, LANE), 1)
        low = lane < FOX_DH
        live = (i * tq + lax.broadcasted_iota(jnp.int32, (tq, 1), 0)) >= N_PAD
        q_tail = jnp.where(lane < L_ONE_Q + 3, 1.0, 0.0)
        k_ones = (lane >= L_ONE_K) & (lane < L_ONE_K + 4)
        v_tail = jnp.where(lane < FOX_DH + 2, 1.0, 0.0)
        bias_parts = _split3(jnp.where(live, -c_ref[...], NEG))
        for pair in range(FOX_PAIRS):
            base = 3 * LANE * pair
            for e in range(2):
                h = 2 * pair + e
                q = _head_to_low(p_ref[:, base:base + LANE], e)
                k = _head_to_low(p_ref[:, base + LANE:base + 2 * LANE], e)
                v = _head_to_low(p_ref[:, base + 2 * LANE:base + 3 * LANE], e)
                hi, mid, lo = [part[:, h:h + 1] for part in bias_parts]
                ka = jnp.where(low, k, jnp.where(k_ones, 1.0, 0.0))
                ka = jnp.where(lane == L_ONE_Q, hi, jnp.where(lane == L_ONE_Q + 1, mid, jnp.where(lane == L_ONE_Q + 2, lo, ka)))
                qa = jnp.where(low, q * QK_SCALE, q_tail)
                va = jnp.where(low, v, v_tail)
                qa_ref[h] = qa.astype(BF16)
                ka_ref[h] = ka.astype(BF16)
                va_ref[h] = va.astype(BF16)
                qt_ref[h] = qa.T.astype(BF16)
                vt_ref[h] = va.T.astype(BF16)

    out = jax.ShapeDtypeStruct((FOX_HEADS, t, LANE), BF16)
    out_t = jax.ShapeDtypeStruct((FOX_HEADS, t // tq, LANE, tq), BF16)
    ospec = pl.BlockSpec((FOX_HEADS, tq, LANE), lambda i: (0, i, 0))
    tspec = pl.BlockSpec((FOX_HEADS, None, LANE, tq), lambda i: (0, i, 0, 0))
    return _pcall(
        body, name=name, grid=(t // tq,),
        in_specs=[pl.BlockSpec((tq, 3 * FOX_W), lambda i: (i, 1)), pl.BlockSpec((tq, LANE), lambda i: (i, 0))],
        out_specs=[ospec, ospec, ospec, tspec, tspec], out_shape=[out, out, out, out_t, out_t],
        compiler_params=_params("parallel"),
    )(proj, c)


STEP_PAIRS = 2
STEP_HEADS = 2 * STEP_PAIRS
FOX_GROUPS = FOX_PAIRS // STEP_PAIRS
FWD_PAIRS = 4
FWD_HEADS = 2 * FWD_PAIRS
FWD_GROUPS = FOX_PAIRS // FWD_PAIRS


def _fox_fwd(qt, ka, vt, mixed, name):
    nh, nq, tq, _ = ka.shape
    t = nq * tq

    def body(qt_ref, ka_ref, vt_ref, mixed_in, mixed_ref, o_ref, lse_ref):
        i = pl.program_id(1)
        lane = lax.broadcasted_iota(jnp.int32, (tq, LANE), 1)
        key_le_query = lax.broadcasted_iota(jnp.int32, (tq, tq), 0) <= lax.broadcasted_iota(jnp.int32, (tq, tq), 1)

        def logits(j):
            return [_dot(ka_ref[h, j], qt_ref[h]) for h in range(FWD_HEADS)]

        def update(j, scores, carry, diagonal):
            new = []
            for h in range(FWD_HEADS):
                m, acc = carry[h]
                s = jnp.where(key_le_query, scores[h], NEG) if diagonal else scores[h]
                m_new = jnp.maximum(m, jnp.max(s, axis=0, keepdims=True))
                p = jnp.exp(s - m_new).astype(BF16)
                new.append((m_new, jnp.exp(m - m_new) * acc + _dot(vt_ref[h, j], p)))
            return tuple(new)

        init = tuple((jnp.full((1, tq), NEG, F32), jnp.zeros((LANE, tq), F32)) for _ in range(FWD_HEADS))
        carry = lax.fori_loop(0, i, lambda j, cr: update(j, logits(j), cr, False), init)
        outs, lse_rows = [], []
        for m, acc in update(i, logits(i), carry, True):
            l = acc[FOX_DH:FOX_DH + 1, :]
            outs.append((acc / l).T)
            lse_rows.append(m + jnp.log(l))
        lse_rows.append(jnp.zeros((LANE - FWD_HEADS, tq), F32))
        o_all = jnp.concatenate([_pair(outs[2 * c], outs[2 * c + 1], lane < FOX_DH) for c in range(FWD_PAIRS)], axis=1)
        mixed_ref[...] = o_all.astype(BF16)
        o_ref[...] = o_all
        lse_ref[...] = jnp.concatenate(lse_rows, axis=0).T

    width = FWD_PAIRS * LANE
    whole = pl.BlockSpec((FWD_HEADS, nq, tq, LANE), lambda g, i: (g, 0, 0, 0), pipeline_mode=pl.Buffered(1))
    whole_t = pl.BlockSpec((FWD_HEADS, nq, LANE, tq), lambda g, i: (g, 0, 0, 0), pipeline_mode=pl.Buffered(1))
    return _pcall(
        body, name=name, grid=(FWD_GROUPS, nq),
        in_specs=[pl.BlockSpec((FWD_HEADS, None, LANE, tq), lambda g, i: (g, i, 0, 0)), whole, whole_t,
                  pl.BlockSpec(memory_space=pl.ANY)],
        out_specs=[pl.BlockSpec((tq, width), lambda g, i: (i, RET_V // width + g)),
                   pl.BlockSpec((tq, width), lambda g, i: (i, g)),
                   pl.BlockSpec((None, tq, LANE), lambda g, i: (g, i, 0))],
        out_shape=[jax.ShapeDtypeStruct(mixed.shape, BF16), jax.ShapeDtypeStruct((t, FOX_W), F32),
                   jax.ShapeDtypeStruct((FWD_GROUPS, t, LANE), F32)],
        input_output_aliases={3: 0},
        compiler_params=_params("parallel", "parallel"),
    )(qt, ka, vt, mixed)


def _fox_prep_bwd(dmixed, o_fox, lse, qa, name):
    t = dmixed.shape[0]
    tq = TOK_TILE

    def body(dm_ref, o_ref, lse_ref, qa_ref, qab_ref, doa_ref):
        i = pl.program_id(0)
        lane = lax.broadcasted_iota(jnp.int32, (tq, LANE), 1)
        low = lane < FOX_DH
        live = (i * tq + lax.broadcasted_iota(jnp.int32, (tq, 1), 0)) >= N_PAD
        lse_parts = [_split3(jnp.where(live, -lse_ref[grp], 0.0)) for grp in range(FWD_GROUPS)]
        for pair in range(FOX_PAIRS):
            cols = slice(LANE * pair, LANE * (pair + 1))
            d_slab = dm_ref[:, cols]
            prod = d_slab * o_ref[:, cols]
            for e in range(2):
                h = 2 * pair + e
                nd = -jnp.sum(jnp.where(low, _head_to_low(prod, e), 0.0), axis=-1, keepdims=True)
                nd_hi = nd.astype(BF16).astype(F32)
                doa = jnp.where(low, _head_to_low(d_slab, e), 0.0)
                doa = jnp.where(lane == FOX_DH, nd_hi, jnp.where(lane == FOX_DH + 1, nd - nd_hi, doa))
                doa_ref[h] = doa.astype(BF16)
                lane_h = h % FWD_HEADS
                hi, mid, lo = [part[:, lane_h:lane_h + 1] for part in lse_parts[h // FWD_HEADS]]
                qab = qa_ref[h].astype(F32)
                qab = jnp.where(lane == L_LSE, hi, jnp.where(lane == L_LSE + 1, mid, jnp.where(lane == L_LSE + 2, lo, qab)))
                qab_ref[h] = qab.astype(BF16)

    out = jax.ShapeDtypeStruct((FOX_HEADS, t, LANE), BF16)
    hspec = pl.BlockSpec((FOX_HEADS, tq, LANE), lambda i: (0, i, 0))
    return _pcall(
        body, name=name, grid=(t // tq,),
        in_specs=[pl.BlockSpec((tq, FOX_W), lambda i: (i, 1)), pl.BlockSpec((tq, FOX_W), lambda i: (i, 0)),
                  pl.BlockSpec((FWD_GROUPS, tq, LANE), lambda i: (0, i, 0)), hspec],
        out_specs=[hspec, hspec], out_shape=[out, out],
        compiler_params=_params("parallel"),
    )(dmixed, o_fox, lse, qa)


def _fox_bwd(qab, doa, ka, va, dproj, name):
    nh, nq, tq, _ = qab.shape
    t = nq * tq
    slab = 3 * LANE * STEP_PAIRS
    group0 = (2 * RET_QK + 2 * RET_V) // slab

    def body(qab_ref, doa_ref, ka_ref, va_ref, dproj_in, dp_ref, drs_ref, dcs_ref, dq_ref):
        g, j = pl.program_id(0), pl.program_id(1)

        @pl.when((g == 0) & (j == 0))
        def _():
            drs_ref[...] = jnp.zeros_like(drs_ref)
            dcs_ref[...] = jnp.zeros_like(dcs_ref)

        @pl.when(j == 0)
        def _():
            dq_ref[...] = jnp.zeros_like(dq_ref)

        lane = lax.broadcasted_iota(jnp.int32, (tq, LANE), 1)
        low = lane < FOX_DH
        key_le_query = lax.broadcasted_iota(jnp.int32, (tq, tq), 0) <= lax.broadcasted_iota(jnp.int32, (tq, tq), 1)

        def by_head(c, a, b, col):
            h = STEP_HEADS * g + 2 * c
            return jnp.where(lane == h, a[:, col:col + 1], jnp.where(lane == h + 1, b[:, col:col + 1], 0.0))


        def step(i, carry, diagonal):
            st = [_dot(ka_ref[h], qab_ref[h, i], NT) for h in range(STEP_HEADS)]
            dpt = [_dot(va_ref[h], doa_ref[h, i], NT) for h in range(STEP_HEADS)]
            new = []
            for h in range(STEP_HEADS):
                p = jnp.exp(st[h])
                if diagonal:
                    p = jnp.where(key_le_query, p, 0.0)
                ds = (p * dpt[h]).astype(BF16)
                dq_ref[h, i] += _dot(ds, ka_ref[h], TN)
                dk, dv = carry[h]
                new.append((dk + _dot(ds, qab_ref[h, i]), dv + _dot(p.astype(BF16), doa_ref[h, i])))
            return tuple(new)

        zero = jnp.zeros((tq, LANE), F32)
        carry = step(j, tuple((zero, zero) for _ in range(STEP_HEADS)), True)
        carry = lax.fori_loop(j + 1, nq, lambda i, cr: step(i, cr, False), carry)
        rows = pl.ds(pl.multiple_of(j * tq, tq), tq)
        for c in range(STEP_PAIRS):
            (dka, dva), (dkb, dvb) = carry[2 * c], carry[2 * c + 1]
            c0 = 3 * LANE * c
            dp_ref[rows, c0 + LANE:c0 + 2 * LANE] = _pair(dka, dkb, low).astype(BF16)
            dp_ref[rows, c0 + 2 * LANE:c0 + 3 * LANE] = _pair(dva, dvb, low).astype(BF16)
            dcs_ref[rows, :] += by_head(c, dka, dkb, L_ONE_Q)

        @pl.when(j == nq - 1)
        def _():
            for c in range(STEP_PAIRS):
                for blk in range(nq):
                    r = slice(blk * tq, (blk + 1) * tq)
                    a, b = dq_ref[2 * c, blk], dq_ref[2 * c + 1, blk]
                    dp_ref[r, 3 * LANE * c:3 * LANE * c + LANE] = (_pair(a, b, low) * QK_SCALE).astype(BF16)
                    drs_ref[r, :] += by_head(c, a, b, L_ONE_K)

    whole = pl.BlockSpec((STEP_HEADS, nq, tq, LANE), lambda g, j: (g, 0, 0, 0), pipeline_mode=pl.Buffered(1))
    blk = pl.BlockSpec((STEP_HEADS, None, tq, LANE), lambda g, j: (g, j, 0, 0))
    sums = pl.BlockSpec((t, LANE), lambda g, j: (0, 0), pipeline_mode=pl.Buffered(1))
    return _pcall(
        body, name=name, grid=(FOX_GROUPS, nq),
        in_specs=[whole, whole, blk, blk, pl.BlockSpec(memory_space=pl.ANY)],
        out_specs=[pl.BlockSpec((t, slab), lambda g, j: (0, group0 + g)), sums, sums],
        out_shape=[jax.ShapeDtypeStruct(dproj.shape, BF16), jax.ShapeDtypeStruct((t, LANE), F32),
                   jax.ShapeDtypeStruct((t, LANE), F32)],
        input_output_aliases={4: 0},
        scratch_shapes=[pltpu.VMEM((STEP_HEADS, nq, tq, LANE), F32)],
        compiler_params=_params("arbitrary", "arbitrary"),
    )(qab, doa, ka, va, dproj)


HALO = 8


def _rows_ext(ref, r0, rows, t, before, after):
    lo, hi = r0 - before, r0 + rows + after
    width = ref.shape[-1]
    parts = []
    if lo < 0:
        parts.append(jnp.zeros((-lo, width), F32))
    parts.append(ref[max(lo, 0):min(hi, t), :].astype(F32))
    if hi > t:
        parts.append(jnp.zeros((hi - t, width), F32))
    return parts[0] if len(parts) == 1 else jnp.concatenate(parts, axis=0)


def _conv_taps(a_ext, r0_ext, cw_ref, cb_ref):
    n = a_ext.shape[0]
    if r0_ext < N_PAD:
        row = r0_ext + lax.broadcasted_iota(jnp.int32, (n, 1), 0)
        a_ext = jnp.where(row >= N_PAD, a_ext, 0.0)
    a1 = pltpu.roll(a_ext, 1, axis=0)
    a2 = pltpu.roll(a_ext, 2, axis=0)
    acc = cb_ref[...] + a2 * cw_ref[0:1, :] + a1 * cw_ref[1:2, :] + a_ext * cw_ref[2:3, :]
    return a_ext, a1, a2, acc


FF_COLS = 256


def _up_conv_fwd(n2, w_up_t, conv_w8, conv_b, name):
    t, d = n2.shape
    f = w_up_t.shape[1]
    rows = TOK_TILE
    starts = list(range(0, t, rows))

    def body(n_ref, wa_ref, wb_ref, cw_ref, cb_ref, up_ref, g_ref):
        def project(r0):
            n_rows = n_ref[r0:r0 + rows, :]
            up_ref[0, r0:r0 + rows, :] = _dot(n_rows, wa_ref[...], NT)
            up_ref[1, r0:r0 + rows, :] = _dot(n_rows, wb_ref[...], NT)

        def activate(r0):
            a_ext = _rows_ext(up_ref.at[0], r0, rows, t, HALO, 0)
            _, _, _, acc = _conv_taps(a_ext, r0 - HALO, cw_ref, cb_ref)
            acc = acc[HALO:, :]
            g_ref[r0:r0 + rows, :] = (acc * _sigmoid(acc) * up_ref[1, r0:r0 + rows, :]).astype(BF16)

        project(starts[0])
        for r0, r_next in zip(starts, starts[1:] + [None]):
            if r_next is not None:
                project(r_next)
            activate(r0)

    return _pcall(
        body, name=name, grid=(f // FF_COLS,),
        in_specs=[pl.BlockSpec((t, d), lambda j: (0, 0), pipeline_mode=pl.Buffered(1)),
                  pl.BlockSpec((None, FF_COLS, d), lambda j: (0, j, 0)), pl.BlockSpec((None, FF_COLS, d), lambda j: (1, j, 0)),
                  pl.BlockSpec((8, FF_COLS), lambda j: (0, j)), pl.BlockSpec((1, FF_COLS), lambda j: (0, j))],
        out_specs=[pl.BlockSpec((2, t, FF_COLS), lambda j: (0, 0, j)), pl.BlockSpec((t, FF_COLS), lambda j: (0, j))],
        out_shape=[jax.ShapeDtypeStruct((2, t, f), F32), jax.ShapeDtypeStruct((t, f), BF16)],
        compiler_params=_params("parallel"),
    )(n2, w_up_t, w_up_t, conv_w8, conv_b)


def _dg_conv_bwd(up, conv_w8, conv_b, dh2, w_down, name):
    _, t, f = up.shape
    d = dh2.shape[1]
    rows = TOK_TILE
    starts = list(range(0, t, rows))

    def body(a_ref, b_ref, cw_ref, cb_ref, dh_ref, wd_ref, dup_ref, gcw_ref, gcb_ref, dg_ref):
        def project(r0):
            dg_ref[r0:r0 + rows, :] = _dot(dh_ref[r0:r0 + rows, :], wd_ref[...], NT)

        gw = [jnp.zeros((1, FF_COLS), F32) for _ in range(3)]
        gb = jnp.zeros((1, FF_COLS), F32)
        project(starts[0])
        for r0, r_next in zip(starts, starts[1:] + [None]):
            if r_next is not None:
                project(r_next)
            a_ext = _rows_ext(a_ref, r0, rows, t, HALO, HALO)
            b_ext = _rows_ext(b_ref, r0, rows, t, HALO, HALO)
            dg_ext = _rows_ext(dg_ref, r0, rows, t, HALO, HALO)
            a0, a1, a2, acc = _conv_taps(a_ext, r0 - HALO, cw_ref, cb_ref)
            sg = _sigmoid(acc)
            dacc = dg_ext * b_ext * (sg * (1.0 + acc * (1.0 - sg)))
            n = dacc.shape[0]
            da = (dacc * cw_ref[2:3, :] + pltpu.roll(dacc, n - 1, axis=0) * cw_ref[1:2, :]
                  + pltpu.roll(dacc, n - 2, axis=0) * cw_ref[0:1, :])
            core = slice(HALO, HALO + rows)
            da = da[core, :]
            if r0 < N_PAD:
                row = r0 + lax.broadcasted_iota(jnp.int32, (rows, 1), 0)
                da = jnp.where(row >= N_PAD, da, 0.0)
            dup_ref[0, r0:r0 + rows, :] = da.astype(BF16)
            dup_ref[1, r0:r0 + rows, :] = (dg_ext * acc * sg)[core, :].astype(BF16)
            dacc_c = dacc[core, :]
            gw[0] = gw[0] + jnp.sum(dacc_c * a2[core, :], axis=0, keepdims=True)
            gw[1] = gw[1] + jnp.sum(dacc_c * a1[core, :], axis=0, keepdims=True)
            gw[2] = gw[2] + jnp.sum(dacc_c * a0[core, :], axis=0, keepdims=True)
            gb = gb + jnp.sum(dacc_c, axis=0, keepdims=True)
        gcw_ref[...] = jnp.zeros((8, FF_COLS), F32)
        for tap in range(3):
            gcw_ref[tap:tap + 1, :] = gw[tap]
        gcb_ref[...] = gb

    return _pcall(
        body, name=name, grid=(f // FF_COLS,),
        in_specs=[pl.BlockSpec((None, t, FF_COLS), lambda j: (0, 0, j)), pl.BlockSpec((None, t, FF_COLS), lambda j: (1, 0, j)),
                  pl.BlockSpec((8, FF_COLS), lambda j: (0, j)), pl.BlockSpec((1, FF_COLS), lambda j: (0, j)),
                  pl.BlockSpec((t, d), lambda j: (0, 0), pipeline_mode=pl.Buffered(1)),
                  pl.BlockSpec((FF_COLS, d), lambda j: (j, 0))],
        out_specs=[pl.BlockSpec((2, t, FF_COLS), lambda j: (0, 0, j)), pl.BlockSpec((8, FF_COLS), lambda j: (0, j)),
                   pl.BlockSpec((1, FF_COLS), lambda j: (0, j))],
        out_shape=[jax.ShapeDtypeStruct((2, t, f), BF16), jax.ShapeDtypeStruct((8, f), F32),
                   jax.ShapeDtypeStruct((1, f), F32)],
        scratch_shapes=[pltpu.VMEM((t, FF_COLS), F32)],
        compiler_params=_params("parallel"),
    )(up, up, conv_w8, conv_b, dh2, w_down)


def _exchange(arrays, kinds, name, after=None):
    n = len(arrays)
    npeer = N_DEV - 1
    n_in = n + int(after is not None)

    def body(*refs):
        ins, outs = refs[:n], refs[n_in:n_in + n]
        send_sems, recv_sems, local_sems = refs[n_in + n:]
        x, y, c = lax.axis_index("x"), lax.axis_index("y"), lax.axis_index("c")
        me = 4 * x + 2 * y + c
        copies, locals_ = [], []
        for a in range(n):
            gather = kinds[a] == "gather"
            own = pltpu.make_async_copy(ins[a] if gather else ins[a].at[me], outs[a].at[me], local_sems.at[a])
            own.start()
            locals_.append(own)
            for d in range(1, N_DEV):
                px = 1 - x if d & 4 else x
                py = 1 - y if d & 2 else y
                pc = 1 - c if d & 1 else c
                src = ins[a] if gather else ins[a].at[4 * px + 2 * py + pc]
                cp = pltpu.make_async_remote_copy(
                    src_ref=src, dst_ref=outs[a].at[me],
                    send_sem=send_sems.at[a * npeer + d - 1], recv_sem=recv_sems.at[a * npeer + d - 1],
                    device_id=(px, py, pc), device_id_type=pl.DeviceIdType.MESH)
                cp.start()
                copies.append(cp)
        for cp in copies:
            cp.wait_recv()
        for cp in copies:
            cp.wait_send()
        for own in locals_:
            own.wait()

    out_shape = [jax.ShapeDtypeStruct((N_DEV,) + (a.shape if k == "gather" else a.shape[1:]), a.dtype)
                 for a, k in zip(arrays, kinds)]
    return _pcall(
        body, name=name,
        in_specs=[pl.BlockSpec(memory_space=pl.ANY)] * n_in,
        out_specs=[pl.BlockSpec(memory_space=pl.ANY)] * n,
        out_shape=out_shape,
        scratch_shapes=[pltpu.SemaphoreType.DMA((n * npeer,)), pltpu.SemaphoreType.DMA((n * npeer,)),
                        pltpu.SemaphoreType.DMA((n,))],
        compiler_params=pltpu.CompilerParams(has_side_effects=True),
    )(*arrays, *([] if after is None else [after]))


ALL_PEERS = tuple(range(1, N_DEV))
SAME_CORE_AND_SIBLING = (1, 2, 4, 6)
OTHER_CHIPS = (2, 4, 6)


def _peer_copies(srcs, lands, kinds, send_sems, recv_sems, relations=ALL_PEERS):
    x, y, c = lax.axis_index("x"), lax.axis_index("y"), lax.axis_index("c")
    me = 4 * x + 2 * y + c
    copies = []
    for a in range(len(srcs)):
        for d in relations:
            px = 1 - x if d & 4 else x
            py = 1 - y if d & 2 else y
            pc = 1 - c if d & 1 else c
            peer = 4 * px + 2 * py + pc
            k = a * (N_DEV - 1) + d - 1
            if kinds[a] == "forward":
                src, dst, target = lands[a].at[peer], lands[a].at[peer], (x, y, 1 - c)
            else:
                src, dst, target = (srcs[a] if kinds[a] == "gather" else srcs[a].at[peer]), lands[a].at[me], (px, py, pc)
            copies.append(pltpu.make_async_remote_copy(
                src_ref=src, dst_ref=dst, send_sem=send_sems.at[k], recv_sem=recv_sems.at[k],
                device_id=target, device_id_type=pl.DeviceIdType.MESH))
    return copies


def _own_copies(srcs, lands, kinds, sems):
    me = 4 * lax.axis_index("x") + 2 * lax.axis_index("y") + lax.axis_index("c")
    first = len(srcs) * (N_DEV - 1)
    return [pltpu.make_async_copy(srcs[a].at[me] if kinds[a] == "scatter" else srcs[a], lands[a].at[me], sems.at[first + a])
            for a in range(len(srcs))]


def _exchange_start(arrays, kinds, name, after=None, relations=ALL_PEERS, lands=None, own=True):
    n = len(arrays)
    nsem = n * (N_DEV - 1) + n
    hbm = pl.BlockSpec(memory_space=pltpu.HBM)
    sem = pl.BlockSpec(memory_space=pltpu.SEMAPHORE)
    land_shapes = ([l.shape for l in lands] if lands is not None else
                   [(N_DEV,) + (a.shape if k == "gather" else a.shape[1:]) for a, k in zip(arrays, kinds)])

    n_in = 2 * n + int(after is not None)

    def body(*refs):
        srcs, land_refs = refs[:n], refs[n:2 * n]
        send_sems, recv_sems = refs[n_in], refs[n_in + 1]
        token = refs[-1]
        for cp in _peer_copies(srcs, land_refs, kinds, send_sems, recv_sems, relations):
            cp.start()
        for cp in _own_copies(srcs, land_refs, kinds, send_sems) if own else []:
            cp.start()
        token[...] = jnp.zeros_like(token)

    operands = [pltpu.with_memory_space_constraint(a, pltpu.HBM) for a in arrays]
    operands += (list(lands) if lands is not None else
                 [pltpu.with_memory_space_constraint(lax.empty(s, a.dtype), pltpu.HBM) for s, a in zip(land_shapes, arrays)])
    operands += [] if after is None else [after]
    out = _pcall(
        body, name=name,
        in_specs=[hbm] * (2 * n) + ([] if after is None else [pl.BlockSpec(memory_space=pl.ANY)]),
        out_specs=[sem, sem] + [hbm] * (2 * n) + [pl.BlockSpec(memory_space=pltpu.VMEM)],
        out_shape=[pltpu.SemaphoreType.DMA((nsem,)), pltpu.SemaphoreType.DMA((nsem,))]
        + [pltpu.HBM(a.shape, a.dtype) for a in arrays]
        + [pltpu.HBM(s, a.dtype) for s, a in zip(land_shapes, arrays)]
        + [jax.ShapeDtypeStruct((8, LANE), F32)],
        input_output_aliases={k: 2 + k for k in range(2 * n)},
        compiler_params=pltpu.CompilerParams(has_side_effects=pltpu.SideEffectType.DATAFLOW_SIDE_EFFECTING),
    )(*operands)
    return out[0], out[1], list(out[2:2 + n]), list(out[2 + n:2 + 2 * n]), out[-1]


def _exchange_wait(started, kinds, after, name, own=True, relations=ALL_PEERS, with_sources=False):
    send_sems, recv_sems, srcs, lands, _ = started
    n = len(srcs)
    hbm = pl.BlockSpec(memory_space=pltpu.HBM)
    sem = pl.BlockSpec(memory_space=pltpu.SEMAPHORE)

    def body(*refs):
        src_refs, land_refs = refs[:n], refs[n:2 * n]
        copies = _peer_copies(src_refs, land_refs, kinds, refs[2 * n], refs[2 * n + 1], relations)
        for cp in copies:
            cp.wait_send()
        for cp in copies:
            cp.wait_recv()
        for cp in _own_copies(src_refs, land_refs, kinds, refs[2 * n]) if own else []:
            cp.wait()

    out = _pcall(
        body, name=name,
        in_specs=[hbm] * (2 * n) + [sem, sem, pl.BlockSpec(memory_space=pl.ANY)],
        out_specs=[hbm] * (2 * n),
        out_shape=[pltpu.HBM(a.shape, a.dtype) for a in srcs + lands],
        input_output_aliases={k: k for k in range(2 * n)},
        compiler_params=pltpu.CompilerParams(has_side_effects=pltpu.SideEffectType.DATAFLOW_SIDE_EFFECTING),
    )(*srcs, *lands, send_sems, recv_sems, after)
    return (list(out[:n]), list(out[n:])) if with_sources else list(out[n:])


def _sum_slots(slots, name, rows_tile):
    nd, r, c = slots.shape

    def body(s_ref, o_ref):
        acc = s_ref[0].astype(F32)
        for p in range(1, nd):
            acc = acc + s_ref[p].astype(F32)
        o_ref[...] = acc

    return _pcall(
        body, name=name, grid=(r // rows_tile,),
        in_specs=[pl.BlockSpec((nd, rows_tile, c), lambda i: (0, i, 0))],
        out_specs=pl.BlockSpec((rows_tile, c), lambda i: (i, 0)),
        out_shape=jax.ShapeDtypeStruct((r, c), F32),
        compiler_params=_params("parallel"),
    )(slots)


def _sum_slots_small(slot_arrays, own_arrays, name):
    n = len(slot_arrays)

    def body(*refs):
        me = 4 * lax.axis_index("x") + 2 * lax.axis_index("y") + lax.axis_index("c")
        for s_ref, own_ref, o_ref in zip(refs[:n], refs[n:2 * n], refs[2 * n:]):
            acc = jnp.where(me == 0, own_ref[...], s_ref[0])
            for p in range(1, s_ref.shape[0]):
                acc = acc + jnp.where(me == p, own_ref[...], s_ref[p])
            o_ref[...] = acc

    return _pcall(body, name=name, out_shape=[jax.ShapeDtypeStruct(a.shape[1:], F32) for a in slot_arrays])(
        *slot_arrays, *own_arrays)


def _adamw_values(w, gr, m, v):
    nm = ADAM_B1 * m + (1.0 - ADAM_B1) * gr
    nv = ADAM_B2 * v + (1.0 - ADAM_B2) * (gr * gr)
    m_hat = nm / (1.0 - ADAM_B1 ** ADAM_STEP)
    v_hat = nv / (1.0 - ADAM_B2 ** ADAM_STEP)
    return -ADAM_LR * (m_hat / (jnp.sqrt(v_hat) + ADAM_EPS) + ADAM_WD * w), nm, nv


def _adamw_update(w_ref, g_ref, m_ref, v_ref, d_ref, nm_ref, nv_ref):
    d_ref[...], nm_ref[...], nv_ref[...] = _adamw_values(w_ref[...], g_ref[...], m_ref[...], v_ref[...])


def _adamw_from_slots(w, slots, m, v, name, cols_tile=2 * LANE):
    rows, cols = w.shape
    nd, rows_pad, _ = slots.shape

    def body(w_ref, s_ref, m_ref, v_ref, g_ref, d_ref, nm_ref, nv_ref):
        gr = s_ref[0, 0:rows, :].astype(F32)
        for p in range(1, nd):
            gr = gr + s_ref[p, 0:rows, :].astype(F32)
        g_ref[...] = gr
        d_ref[...], nm_ref[...], nv_ref[...] = _adamw_values(w_ref[...], gr, m_ref[...], v_ref[...])

    spec = pl.BlockSpec((rows, cols_tile), lambda i: (0, i))
    slot_spec = pl.BlockSpec((nd, rows_pad, cols_tile), lambda i: (0, 0, i))
    return _pcall(
        body, name=name, grid=(cols // cols_tile,), in_specs=[spec, slot_spec, spec, spec], out_specs=[spec] * 4,
        out_shape=[jax.ShapeDtypeStruct((rows, cols), F32)] * 4, compiler_params=_params("parallel"),
    )(w, slots, m, v)


def _adamw_small(ws, gs, ms, vs, name):
    n = len(ws)

    def body(*refs):
        ins, outs = refs[:4 * n], refs[4 * n:]
        for k in range(n):
            _adamw_update(ins[k], ins[n + k], ins[2 * n + k], ins[3 * n + k], outs[k], outs[n + k], outs[2 * n + k])

    shapes = [jax.ShapeDtypeStruct(w.shape, F32) for w in ws]
    out = _pcall(body, name=name, out_shape=shapes * 3)(*ws, *gs, *ms, *vs)
    return list(out[:n]), list(out[n:2 * n]), list(out[2 * n:])


def _adamw(w, g, m, v, name, rows_tile):
    r, c = w.shape
    body = lambda *refs: _adamw_update(*refs)
    spec = pl.BlockSpec((rows_tile, c), lambda i: (i, 0))
    shp = jax.ShapeDtypeStruct((r, c), F32)
    return _pcall(
        body, name=name, grid=(r // rows_tile,), in_specs=[spec] * 4, out_specs=[spec] * 3, out_shape=[shp] * 3,
        compiler_params=_params("parallel"),
    )(w, g, m, v)


F0 = 2 * RET_QK + 2 * RET_V


def _to_internal_rows(w_t):
    cols = w_t.shape[1]
    fox = w_t[F0:F0 + 3 * FOX_W].reshape(3, FOX_PAIRS, LANE, cols).transpose(1, 0, 2, 3).reshape(3 * FOX_W, cols)
    tail = jnp.zeros((IN_PAD - IN_WIDTH, cols), w_t.dtype)
    return jnp.concatenate([w_t[:F0], fox, w_t[F0 + 3 * FOX_W:], tail], axis=0)


def _from_internal_rows(g_t):
    cols = g_t.shape[1]
    fox = g_t[F0:F0 + 3 * FOX_W].reshape(FOX_PAIRS, 3, LANE, cols).transpose(1, 0, 2, 3).reshape(3 * FOX_W, cols)
    return jnp.concatenate([g_t[:F0], fox, g_t[F0 + 3 * FOX_W:F0 + 3 * FOX_W + FOX_HEADS]], axis=0)


IN_BLOCK = IN_WIDTH // N_DEV
IN_BLOCK_PAD = 400
BF16_ROWS = 16


def _slot_row_of_internal():
    rows = np.arange(IN_WIDTH)
    fox = rows[F0:F0 + 3 * FOX_W].reshape(3, FOX_PAIRS, LANE).transpose(1, 0, 2).reshape(-1)
    original = np.concatenate([rows[:F0], fox, rows[F0 + 3 * FOX_W:]])
    slot_rows = original // IN_BLOCK * IN_BLOCK_PAD + original % IN_BLOCK
    return np.concatenate([slot_rows, np.full(IN_PAD - IN_WIDTH, -1)])


def _internal_row_of_slot():
    forward = _slot_row_of_internal()
    back = np.full(N_DEV * IN_BLOCK_PAD, -1)
    back[forward[forward >= 0]] = np.nonzero(forward >= 0)[0]
    return back


def _row_runs(src_of_dst):
    tiles = []
    for t0 in range(0, len(src_of_dst), LANE):
        runs = []
        for o in range(LANE):
            s = int(src_of_dst[t0 + o])
            if s < 0:
                continue
            if runs and runs[-1][0] + runs[-1][2] == o and runs[-1][1] + runs[-1][2] == s:
                runs[-1][2] += 1
            else:
                runs.append([o, s, 1])
        tiles.append(runs)
    return tiles


def _move_rows(src, src_of_dst, name):
    n_src, cols = src.shape
    tiles = _row_runs(src_of_dst)

    def body(s_ref, o_ref):
        for t, runs in enumerate(tiles):
            rows = pl.ds(t * LANE, LANE)
            if not runs:
                o_ref[rows, :] = jnp.zeros((LANE, cols), o_ref.dtype)
                continue
            if len(runs) == 1 and runs[0][0] == 0 and runs[0][2] == LANE and runs[0][1] % BF16_ROWS == 0:
                o_ref[rows, :] = s_ref[pl.ds(runs[0][1], LANE), :]
                continue
            acc = None
            for o0, s0, n in runs:
                w0 = s0 // BF16_ROWS * BF16_ROWS
                width = -(-(s0 - w0 + n) // LANE) * LANE
                w0 = min(w0, n_src - width)
                i = lax.broadcasted_iota(jnp.int32, (LANE, width), 0)
                j = lax.broadcasted_iota(jnp.int32, (LANE, width), 1)
                pick = ((j - i == s0 - w0 - o0) & (i >= o0) & (i < o0 + n)).astype(src.dtype)
                part = _dot(pick, s_ref[pl.ds(w0, width), :])
                acc = part if acc is None else acc + part
            o_ref[rows, :] = acc.astype(o_ref.dtype)

    return _pcall(body, name=name, out_shape=jax.ShapeDtypeStruct((len(src_of_dst), cols), src.dtype))(src)


def _local_step(x, target, meta, attn_g, fox_b, ret_g, ffn_g, conv_w8, conv_b, final_g,
                first_weight, late_weights, ffn_grads_ready, out_grad_ready, in_grad_ready):
    seq, d = x.shape
    t = seq + PREFIX
    tm = TOK_TILE
    nq = t // tm
    fox_b128 = jnp.pad(fox_b, ((0, 0), (0, LANE - FOX_HEADS)))

    h0, n1 = _prep_norm(x, meta, attn_g, "prep_norm")
    w_in_t = first_weight(n1)
    proj = _mm_simple(n1, w_in_t, mode="nt", tm=tm, tn=IN_PAD, tk=d, out_dtype=F32, name="mm_in")
    cos, sin = _rope_tables(t)
    o_pre, mixed, states = _ret_fwd(proj, cos, sin, ret_g, "ret_fwd")
    c = _forget_cumsum(proj, fox_b128, "forget_cumsum")
    qa, ka, va, qt, vt = _fox_prep(proj, c, "fox_prep")
    by_block = lambda a: a.reshape(FOX_HEADS, nq, tm, LANE)
    mixed, o_fox, lse = _fox_fwd(qt, by_block(ka), vt, mixed, "fox_fwd")
    w_out, w_up_t, w_down = late_weights(o_fox)
    tile = pl.BlockSpec((tm, d), lambda i: (i, 0))
    row_vec = pl.BlockSpec((1, d), lambda i: (0, 0))
    resident = lambda shape: pl.BlockSpec(shape, lambda i: (0,) * len(shape), pipeline_mode=pl.Buffered(1))
    acts = lambda dtype: jax.ShapeDtypeStruct((t, d), dtype)
    vec = jax.ShapeDtypeStruct((1, d), F32)

    def residual_and_norm(i, acc, ins, outs):
        h = acc + ins[0][...]
        outs[0][...] = h
        outs[1][...] = (h * lax.rsqrt(jnp.mean(h * h, axis=-1, keepdims=True) + EPS) * ins[1][...]).astype(BF16)

    h1, n2 = _matmul_rows([mixed], [tile], [w_out], [resident((d, d))], [h0, ffn_g], [tile, row_vec],
                          [tile, tile], [acts(F32), acts(BF16)], residual_and_norm, mode="nn", steps=nq, name="mm_out_norm")
    nf = D_FF // 1408
    up, g = _up_conv_fwd(n2, w_up_t, conv_w8, conv_b, "up_conv_fwd")

    def residual_loss_bwd(i, acc, ins, outs):
        loss_ref, dh_ref, dhb_ref, gg_ref = outs
        part, dh, gg = _loss_tile(i, acc + ins[0][...], jnp.concatenate([ins[1][...], ins[2][...], ins[3][...]], axis=0),
                                  ins[4][...])
        _accumulate(loss_ref, i, jnp.broadcast_to(part, loss_ref.shape))
        dh_ref[...] = dh
        dhb_ref[...] = dh.astype(BF16)
        _accumulate(gg_ref, i, gg)

    loss_tile, dh2, dh2_b, g_final = _matmul_rows(
        [g], [pl.BlockSpec((tm, D_FF), lambda i: (i, 0))], [w_down], [resident((D_FF, d))],
        [h1, target, target, target, final_g], [tile] + _shifted_row_specs(d) + [row_vec],
        [pl.BlockSpec((8, LANE), lambda i: (0, 0)), tile, tile, row_vec],
        [jax.ShapeDtypeStruct((8, LANE), F32), acts(F32), acts(BF16), vec], residual_loss_bwd,
        mode="nn", steps=nq, name="mm_down_loss")

    tkw = 2112 if t % 2112 == 0 else tm
    gw_down = _mm_simple(g, dh2_b, mode="tn", tm=1408, tn=d, tk=tkw, out_dtype=BF16, name="mm_gw_down")
    dup, g_conv_w8, g_conv_b = _dg_conv_bwd(up, conv_w8, conv_b, dh2_b, w_down, "dg_conv_bwd")

    half = lambda p: pl.BlockSpec((None, tm, D_FF), lambda i: (p, i, 0))
    half_w = lambda p: pl.BlockSpec((None, D_FF, d), lambda i: (p, 0, 0), pipeline_mode=pl.Buffered(1))
    gw_up_t = _matmul(
        dup, n2, mode="tn", grid=(2 * nf, 1, t // tkw),
        a_spec=pl.BlockSpec((None, tkw, 1408), lambda i, j, k: (i // nf, k, i % nf)),
        b_spec=pl.BlockSpec((tkw, d), lambda i, j, k: (k, 0)),
        o_spec=pl.BlockSpec((1408, d), lambda i, j, k: (i, 0)),
        out_shape=jax.ShapeDtypeStruct((2 * D_FF, d), BF16), name="mm_gw_up")
    def norm_bwd_and_mixer_grad(i, acc, ins, outs):
        dh, gg = _rms_bwd_tile(acc, ins[0][...], ins[1][...], ins[2][...])
        outs[0][...] = dh
        _accumulate(outs[1], i, gg)
        outs[2][...] = _dot(dh.astype(BF16), ins[3][...], NT)

    dh1, g_ffn, dmixed = _matmul_rows(
        [dup, dup], [half(0), half(1)], [w_up_t, w_up_t], [half_w(0), half_w(1)],
        [h1, ffn_g, dh2, w_out], [tile, row_vec, tile, resident((d, d))], [tile, row_vec, tile],
        [acts(F32), vec, acts(F32)], norm_bwd_and_mixer_grad,
        mode="nn", steps=nq, name="mm_dn2_norm_bwd", after=ffn_grads_ready(gw_down, gw_up_t))
    gw_out = _mm_simple(mixed, dh1, mode="tn", tm=d, tn=d, tk=tkw, out_dtype=BF16, name="mm_gw_out")
    dproj, g_ret = _ret_bwd(proj, cos, sin, ret_g + out_grad_ready(gw_out), dmixed, o_pre, states, "ret_bwd")
    qab, doa = _fox_prep_bwd(dmixed, o_fox, lse, qa, "fox_prep_bwd")
    dproj, drs, dcs = _fox_bwd(by_block(qab), by_block(doa), by_block(ka), by_block(va), dproj, "fox_bwd")
    dproj, g_fox_b = _forget_cumsum_bwd(proj, fox_b128, drs, dcs, dproj, "forget_cumsum_bwd")
    gw_in_t = _mm_simple(dproj, n1, mode="tn", tm=640, tn=d, tk=tkw, out_dtype=BF16, name="mm_gw_in")
    sent = in_grad_ready(gw_in_t)
    def input_grads(i, acc, ins, outs):
        gx_ref, gmeta_ref, gg_ref, buf_ref, sems = outs
        dh, gg = _rms_bwd_tile(acc, ins[0][...], ins[1][...], ins[2][...])
        _accumulate(gg_ref, i, gg)
        slot = i % 2

        def first_copy():
            return pltpu.make_async_copy(buf_ref.at[0, pl.ds(PREFIX, tm - PREFIX)], gx_ref.at[pl.ds(0, tm - PREFIX)],
                                         sems.at[0])

        def tile_copy(tile, buf_slot):
            rows = pl.ds(pl.multiple_of(tile * tm - PREFIX, PREFIX), tm)
            return pltpu.make_async_copy(buf_ref.at[buf_slot], gx_ref.at[rows], sems.at[buf_slot])

        @pl.when(i == 1)
        def _():
            first_copy().wait()

        @pl.when(i >= 2)
        def _():
            tile_copy(i - 1, 1 - slot).wait()

        buf_ref[slot] = dh

        @pl.when(i == 0)
        def _():
            gmeta_ref[...] = dh[N_PAD:PREFIX, :]
            first_copy().start()

        @pl.when(i > 0)
        def _():
            tile_copy(i, slot).start()

        @pl.when(i == nq - 1)
        def _():
            tile_copy(i, slot).wait()

    grad_x, g_meta, g_attn = _matmul_rows(
        [dproj], [pl.BlockSpec((tm, IN_PAD), lambda i: (i, 0))], [w_in_t], [resident((IN_PAD, d))],
        [h0, attn_g, dh1], [tile, row_vec, tile],
        [pl.BlockSpec(memory_space=pl.ANY), pl.BlockSpec((N_META, d), lambda i: (0, 0)), row_vec],
        [jax.ShapeDtypeStruct((seq, d), F32), jax.ShapeDtypeStruct((N_META, d), F32), vec], input_grads,
        mode="nn", steps=nq, name="mm_dn1_norm_bwd", after=sent,
        scratch=[pltpu.VMEM((2, tm, d), F32), pltpu.SemaphoreType.DMA((2,))])

    grads = dict(meta=g_meta, attn_g=g_attn, fox_b=g_fox_b, ret_g=g_ret,
                 ffn_g=g_ffn, conv_w=g_conv_w8, conv_b=g_conv_b, final_g=g_final)
    return loss_tile, grad_x, grads


def kernel(x, meta_tokens, attn_norm_g, w_in, fox_forget_b, ret_norm_g, w_out, ffn_norm_g, w_up, conv_w, conv_b, w_down, final_norm_g, loss_target, m_meta_tokens, m_attn_norm_g, m_w_in, m_fox_forget_b, m_ret_norm_g, m_w_out, m_ffn_norm_g, m_w_up, m_conv_w, m_conv_b, m_w_down, m_final_norm_g, v_meta_tokens, v_attn_norm_g, v_w_in, v_fox_forget_b, v_ret_norm_g, v_w_out, v_ffn_norm_g, v_w_up, v_conv_w, v_conv_b, v_w_down, v_final_norm_g):
    d = D_MODEL
    me = 4 * lax.axis_index("x") + 2 * lax.axis_index("y") + lax.axis_index("c")
    in_blk, in_blk_pad = IN_BLOCK, IN_BLOCK_PAD
    up_blk = 2 * D_FF // N_DEV
    down_blk = D_FF // N_DEV
    cw_blk = D_FF // N_DEV

    w_in_loc = jnp.pad(w_in[0].T.astype(BF16), ((0, in_blk_pad - in_blk), (0, 0)))
    cw_loc = jnp.pad(conv_w[0], ((0, 5), (0, 384 - cw_blk)))
    g_meta, g_cw = _exchange([meta_tokens, cw_loc], ["gather"] * 2, "gather_small")
    first = _exchange_start([w_in_loc], ["gather"], "gather_in_start", after=g_meta, relations=SAME_CORE_AND_SIBLING)
    rest_loc = [(w_out[0] + first[-1][0:1, 0:1]).astype(BF16), w_up[0].T.astype(BF16), w_down[0].astype(BF16)]
    rest = _exchange_start(rest_loc, ["gather"] * 3, "gather_rest_start")
    meta_f = g_meta.transpose(1, 0, 2).reshape(N_META, d)
    conv_w8 = jnp.pad(g_cw[:, :3, :cw_blk].transpose(1, 0, 2).reshape(3, D_FF), ((0, 5), (0, 0)))
    pending = {}

    def first_weight(after):
        own_in, landed = _exchange_wait(first, ["gather"], after, "gather_in_wait", relations=SAME_CORE_AND_SIBLING,
                                        with_sources=True)
        onward = _exchange_start(own_in, ["forward"], "gather_in_forward_start", relations=OTHER_CHIPS, lands=landed,
                                 own=False)
        (g_in,) = _exchange_wait(onward, ["forward"], onward[-1], "gather_in_forward_wait", own=False,
                                 relations=OTHER_CHIPS)
        return _move_rows(g_in.reshape(IN_PAD, d), _slot_row_of_internal(), "w_in_rows")

    def in_grad_ready(gw_in_t):
        blocks = _move_rows(gw_in_t, _internal_row_of_slot(), "gw_in_rows").reshape(N_DEV, in_blk_pad, d)
        pending["in"] = _exchange_start([blocks], ["scatter"], "grads_in_start")
        return pending["in"][-1][0:1, 0:1]

    def late_weights(after):
        g_out, g_up, g_down = _exchange_wait(rest, ["gather"] * 3, after, "gather_rest_wait")
        return g_out.reshape(d, d), g_up.reshape(2, D_FF, d), g_down.reshape(D_FF, d)

    def ffn_grads_ready(gw_down, gw_up_t):
        pending["ffn_blocks"] = [gw_down.reshape(N_DEV, down_blk, d), gw_up_t.reshape(N_DEV, up_blk, d)]
        return gw_down[0:1, 0:1]

    def out_grad_ready(gw_out):
        blocks = pending["ffn_blocks"] + [gw_out.reshape(N_DEV, d // N_DEV, d)]
        pending["mid"] = _exchange_start(blocks, ["scatter"] * 3, "grads_mid_start")
        return pending["mid"][-1][0:1, 0:1]

    loss_tile, grad_x, gr = _local_step(
        x[0], loss_target[0], meta_f, attn_norm_g + rest[-1][0:1, 0:1], fox_forget_b, ret_norm_g, ffn_norm_g,
        conv_w8, conv_b, final_norm_g.reshape(1, d), first_weight, late_weights, ffn_grads_ready, out_grad_ready,
        in_grad_ready)

    small = [loss_tile, gr["attn_g"], gr["fox_b"], gr["ret_g"], gr["ffn_g"], gr["conv_b"], gr["final_g"],
             gr["meta"], gr["conv_w"]]
    small_kinds = ["gather"] * len(small)
    small_started = _exchange_start(small, small_kinds, "grads_small_start", own=False)

    r_down, r_up, r_out = _exchange_wait(pending["mid"], ["scatter"] * 3, small_started[-1], "grads_mid_wait")
    g_w_out = _sum_slots(r_out, "sum_w_out", d // N_DEV)
    g_w_up_t = _sum_slots(r_up, "sum_w_up", up_blk)
    g_w_down = _sum_slots(r_down, "sum_w_down", down_blk)
    as_t = lambda a: a[0].T
    from_t = lambda a: a.T[None]
    d_w_out, m_w_out_n, v_w_out_n = [a[None] for a in _adamw(w_out[0], g_w_out, m_w_out[0], v_w_out[0], "adamw_w_out", 128)]
    up_t = _adamw(as_t(w_up), g_w_up_t, as_t(m_w_up), as_t(v_w_up), "adamw_w_up", up_blk // 2)
    d_w_up, m_w_up_n, v_w_up_n = [from_t(a) for a in up_t]
    d_w_down, m_w_down_n, v_w_down_n = [a[None] for a in _adamw(w_down[0], g_w_down, m_w_down[0], v_w_down[0],
                                                                "adamw_w_down", down_blk)]

    own_small, r_small = _exchange_wait(small_started, small_kinds, up_t[0], "grads_small_wait", own=False,
                                        with_sources=True)
    (loss_all, g_attn, g_fox_b128, g_ret, g_ffn, g_conv_b, g_final, g_meta_full, g_cw_full) = _sum_slots_small(
        r_small, own_small, "sum_small")
    loss = loss_all[0, 0]
    g_fox_b = g_fox_b128[:, :FOX_HEADS]
    g_meta_loc = lax.dynamic_slice(g_meta_full, (0, me * (d // N_DEV)), (N_META, d // N_DEV))
    g_cw_loc = lax.dynamic_slice(g_cw_full, (0, me * cw_blk), (3, cw_blk))

    (r_in,) = _exchange_wait(pending["in"], ["scatter"], r_small[0], "grads_in_wait")
    g_w_in, d_w_in, m_w_in_n, v_w_in_n = [from_t(a) for a in _adamw_from_slots(
        as_t(w_in), r_in, as_t(m_w_in), as_t(v_w_in), "adamw_w_in")]
    g_w_in, g_w_up = g_w_in[0], g_w_up_t.T
    row = lambda a: a.reshape(1, d)
    sm_grads = [g_meta_loc, g_attn, g_fox_b, g_ret, g_ffn, g_cw_loc, g_conv_b, g_final]
    sm_w = [meta_tokens, attn_norm_g, fox_forget_b, ret_norm_g, ffn_norm_g, conv_w[0], conv_b, row(final_norm_g)]
    sm_m = [m_meta_tokens, m_attn_norm_g, m_fox_forget_b, m_ret_norm_g, m_ffn_norm_g, m_conv_w[0], m_conv_b,
            row(m_final_norm_g)]
    sm_v = [v_meta_tokens, v_attn_norm_g, v_fox_forget_b, v_ret_norm_g, v_ffn_norm_g, v_conv_w[0], v_conv_b,
            row(v_final_norm_g)]
    dl, ml, vl = [lst[:7] + [lst[7].reshape(d)] for lst in _adamw_small(sm_w, sm_grads, sm_m, sm_v, "adamw_small")]

    def by_weight(meta_, attn_, w_in_, fox_, ret_, w_out_, ffn_, w_up_, cw_, cb_, w_down_, final_):
        return (meta_, attn_, w_in_, fox_, ret_, w_out_, ffn_, w_up_, cw_[None], cb_, w_down_, final_)

    grads_out = by_weight(g_meta_loc, g_attn, g_w_in[None], g_fox_b, g_ret, g_w_out[None], g_ffn, g_w_up[None], g_cw_loc,
                          g_conv_b, g_w_down[None], g_final.reshape(d))
    delta_out = by_weight(dl[0], dl[1], d_w_in, dl[2], dl[3], d_w_out, dl[4], d_w_up, dl[5], dl[6], d_w_down, dl[7])
    m_out = by_weight(ml[0], ml[1], m_w_in_n, ml[2], ml[3], m_w_out_n, ml[4], m_w_up_n, ml[5], ml[6], m_w_down_n, ml[7])
    v_out = by_weight(vl[0], vl[1], v_w_in_n, vl[2], vl[3], v_w_out_n, vl[4], v_w_up_n, vl[5], vl[6], v_w_down_n, vl[7])
    return (loss, grad_x[None]) + grads_out + delta_out + m_out + v_out
```

```python
import numpy as np
import jax
import jax.numpy as jnp
from jax import lax
from jax.experimental import pallas as pl
from jax.experimental.pallas import tpu as pltpu

F32 = jnp.float32
BF16 = jnp.bfloat16

D_MODEL = 1024
N_META = 16
N_PAD = 112
PREFIX = 128
RET_HEADS = 4
RET_DK = 64
RET_DV = 128
FOX_HEADS = 8
FOX_DH = 64
D_FF = 2816
ROPE_BASE = 10000.0
EPS = 1e-6
NEG = -1e30
RET_QK = RET_HEADS * RET_DK
RET_V = RET_HEADS * RET_DV
FOX_W = FOX_HEADS * FOX_DH
IN_WIDTH = 2 * RET_QK + 2 * RET_V + 3 * FOX_W + FOX_HEADS
IN_PAD = 3200
FF_COL_BLOCK = (IN_WIDTH - FOX_HEADS) // 128
QK_SCALE = 0.125

ADAM_LR = 0.001
ADAM_B1 = 0.9
ADAM_B2 = 0.999
ADAM_EPS = 1e-08
ADAM_WD = 0.01
ADAM_STEP = 10

N_DEV = 8
LANE = 128
ROW_TILE = 128
TOK_TILE = 384

NN = (((1,), (0,)), ((), ()))
NT = (((1,), (1,)), ((), ()))
TN = (((0,), (0,)), ((), ()))


def _pcall(body, **kw):
    return pl.pallas_call(body, **kw)


def _params(*sem):
    return pltpu.CompilerParams(dimension_semantics=sem)


def _dot(a, b, dims=NN):
    return lax.dot_general(a, b, dims, preferred_element_type=F32)


def _sigmoid(x):
    return 0.5 * jnp.tanh(0.5 * x) + 0.5


def _matmul(a, b, *, mode, grid, a_spec, b_spec, o_spec, out_shape, name, add=None, add_spec=None, after=None):
    dims = {"nn": NN, "nt": NT, "tn": TN}[mode]
    nk = grid[2]
    has_add = add is not None
    a_list, b_list = (list(a), list(b)) if isinstance(a, (list, tuple)) else ([a], [b])
    a_specs, b_specs = (list(a_spec), list(b_spec)) if isinstance(a_spec, (list, tuple)) else ([a_spec], [b_spec])
    nt = len(a_list)
    n_in = 2 * nt + int(has_add) + int(after is not None)

    def body(*refs):
        a_refs, b_refs = refs[:nt], refs[nt:2 * nt]
        add_ref = refs[2 * nt] if has_add else None
        o_ref = refs[n_in]
        part = _dot(a_refs[0][...].astype(BF16), b_refs[0][...].astype(BF16), dims)
        for ar, br in zip(a_refs[1:], b_refs[1:]):
            part = part + _dot(ar[...].astype(BF16), br[...].astype(BF16), dims)

        def finish(acc):
            if has_add:
                acc = acc + add_ref[...]
            o_ref[...] = acc.astype(o_ref.dtype)

        if nk == 1:
            finish(part)
        else:
            acc_ref = refs[-1]
            k = pl.program_id(2)

            @pl.when(k == 0)
            def _():
                acc_ref[...] = part

            @pl.when(k > 0)
            def _():
                acc_ref[...] += part

            @pl.when(k == nk - 1)
            def _():
                finish(acc_ref[...])

    in_specs = a_specs + b_specs + ([add_spec] if has_add else [])
    args = tuple(a_list) + tuple(b_list) + ((add,) if has_add else ())
    if after is not None:
        in_specs, args = in_specs + [pl.BlockSpec(memory_space=pl.ANY)], args + (after,)
    scratch = [] if nk == 1 else [pltpu.VMEM(tuple(d for d in o_spec.block_shape if d is not None), F32)]
    return _pcall(
        body, name=name, grid=grid, in_specs=in_specs, out_specs=o_spec, out_shape=out_shape,
        scratch_shapes=scratch, compiler_params=_params("parallel", "parallel", "arbitrary"),
    )(*args)


def _mm_simple(a, b, *, mode, tm, tn, tk, out_dtype, name, add=None, after=None):
    if mode == "tn":
        K, M = a.shape
    else:
        M, K = a.shape
    N = b.shape[0] if mode == "nt" else b.shape[1]
    grid = (M // tm, N // tn, K // tk)
    resident = dict(pipeline_mode=pl.Buffered(1)) if (tn == N and tk == K) else {}
    a_spec = pl.BlockSpec((tk, tm), lambda i, j, k: (k, i)) if mode == "tn" else pl.BlockSpec((tm, tk), lambda i, j, k: (i, k))
    b_spec = (pl.BlockSpec((tn, tk), lambda i, j, k: (j, k), **resident) if mode == "nt"
              else pl.BlockSpec((tk, tn), lambda i, j, k: (k, j), **resident))
    o_spec = pl.BlockSpec((tm, tn), lambda i, j, k: (i, j))
    return _matmul(a, b, mode=mode, grid=grid, a_spec=a_spec, b_spec=b_spec, o_spec=o_spec,
                   out_shape=jax.ShapeDtypeStruct((M, N), out_dtype), name=name, add=add,
                   add_spec=o_spec if add is not None else None, after=after)


def _matmul_rows(a_list, a_specs, b_list, b_specs, extras, extra_specs, out_specs, out_shape, epilogue, *,
                 mode, steps, name, after=None, scratch=()):
    dims = {"nn": NN, "nt": NT}[mode]
    nt, ne = len(a_list), len(extras)
    n_in = 2 * nt + ne + int(after is not None)

    def body(*refs):
        acc = _dot(refs[0][...].astype(BF16), refs[nt][...].astype(BF16), dims)
        for k in range(1, nt):
            acc = acc + _dot(refs[k][...].astype(BF16), refs[nt + k][...].astype(BF16), dims)
        epilogue(pl.program_id(0), acc, refs[2 * nt:2 * nt + ne], refs[n_in:])

    in_specs = list(a_specs) + list(b_specs) + list(extra_specs)
    args = tuple(a_list) + tuple(b_list) + tuple(extras)
    if after is not None:
        in_specs, args = in_specs + [pl.BlockSpec(memory_space=pl.ANY)], args + (after,)
    return _pcall(body, name=name, grid=(steps,), in_specs=in_specs, out_specs=out_specs, out_shape=out_shape,
                  scratch_shapes=list(scratch), compiler_params=_params("arbitrary"))(*args)


def _rms_bwd_tile(dy, x, gain, dres):
    r = lax.rsqrt(jnp.mean(x * x, axis=-1, keepdims=True) + EPS)
    xhat = x * r
    u = dy * gain
    return dres + r * (u - xhat * jnp.mean(u * xhat, axis=-1, keepdims=True)), jnp.sum(dy * xhat, axis=0, keepdims=True)


def _loss_tile(i, x, tgt, gain):
    d = x.shape[-1]
    r = lax.rsqrt(jnp.mean(x * x, axis=-1, keepdims=True) + EPS)
    xhat = x * r
    counted = (i * TOK_TILE + lax.broadcasted_iota(jnp.int32, (TOK_TILE, 1), 0)) >= PREFIX
    err = jnp.where(counted, xhat * gain - tgt, 0.0)
    dy = err * (1.0 / d)
    u = dy * gain
    dh = r * (u - xhat * jnp.mean(u * xhat, axis=-1, keepdims=True))
    return 0.5 * jnp.sum(jnp.mean(err * err, axis=-1, keepdims=True)), dh, jnp.sum(dy * xhat, axis=0, keepdims=True)


def _accumulate(ref, i, part):
    @pl.when(i == 0)
    def _():
        ref[...] = part

    @pl.when(i > 0)
    def _():
        ref[...] += part


def _prep_norm(x, meta, gain, name):
    seq, d = x.shape
    t = seq + PREFIX

    def body(xa_ref, xb_ref, xc_ref, meta_ref, g_ref, h_ref, n_ref):
        i = pl.program_id(0)

        @pl.when(i == 0)
        def _():
            h_ref[0:N_PAD, :] = jnp.zeros((N_PAD, d), F32)
            h_ref[N_PAD:ROW_TILE, :] = meta_ref[...]

        @pl.when(i > 0)
        def _():
            h_ref[0:ROW_TILE, :] = xa_ref[...]

        h_ref[ROW_TILE:2 * ROW_TILE, :] = xb_ref[...]
        h_ref[2 * ROW_TILE:3 * ROW_TILE, :] = xc_ref[...]
        h = h_ref[...]
        r = lax.rsqrt(jnp.mean(h * h, axis=-1, keepdims=True) + EPS)
        n_ref[...] = (h * r * g_ref[...]).astype(BF16)

    return _pcall(
        body, name=name, grid=(t // TOK_TILE,),
        in_specs=_shifted_row_specs(d) + [pl.BlockSpec((N_META, d), lambda i: (0, 0)), pl.BlockSpec((1, d), lambda i: (0, 0))],
        out_specs=[pl.BlockSpec((TOK_TILE, d), lambda i: (i, 0)), pl.BlockSpec((TOK_TILE, d), lambda i: (i, 0))],
        out_shape=[jax.ShapeDtypeStruct((t, d), F32), jax.ShapeDtypeStruct((t, d), BF16)],
        compiler_params=_params("parallel"),
    )(x, x, x, meta, gain)


def _shifted_row_specs(d):
    blocks_per_tile = TOK_TILE // ROW_TILE
    return [pl.BlockSpec((ROW_TILE, d), lambda i, r=r: (jnp.maximum(blocks_per_tile * i + r, 0), 0)) for r in (-1, 0, 1)]


def _ret_consts(bk):
    gam = 1.0 - 2.0 ** (-5.0 - np.arange(RET_HEADS))
    n = np.arange(bk)
    same_or_earlier_chunk = (n[None, :] // 64) <= (n[:, None] // 64)
    w = gam[:, None, None] ** np.abs(n[:, None] - n[None, :])[None] * same_or_earlier_chunk[None]
    wq = gam[:, None] ** (n[None, :] + 1.0)
    wk = gam[:, None] ** (bk - 1.0 - n[None, :])
    mask = (np.arange(RET_QK)[None, :] // RET_DK) == np.arange(RET_HEADS)[:, None]
    return (jnp.asarray(w, F32), jnp.asarray(wq[:, :, None], F32), jnp.asarray(wk[:, :, None], F32),
            jnp.asarray(mask[:, None, :], F32), [float(g ** bk) for g in gam])


def _rope_tables(t):
    half = RET_DK // 2
    inv = 1.0 / (ROPE_BASE ** (jnp.arange(half, dtype=F32) / half))
    ang = jnp.arange(t).astype(F32)[:, None] * inv[None, :]
    cos, sin = jnp.cos(ang), jnp.sin(ang)
    return (jnp.tile(jnp.concatenate([cos, cos], axis=1), (1, RET_HEADS)),
            jnp.tile(jnp.concatenate([-sin, sin], axis=1), (1, RET_HEADS)))


def _swap_halves(x):
    outs = []
    for s in range(x.shape[1] // LANE):
        xs = x[:, LANE * s:LANE * (s + 1)]
        lane = lax.broadcasted_iota(jnp.int32, xs.shape, 1)
        outs.append(jnp.where((lane & 32) == 0, pltpu.roll(xs, LANE - 32, axis=1), pltpu.roll(xs, 32, axis=1)))
    return outs[0] if len(outs) == 1 else jnp.concatenate(outs, axis=1)


def _rope(x, cos, sin_signed):
    return x * cos + _swap_halves(x) * sin_signed


def _rope_t(dx, cos, sin_signed):
    return dx * cos + _swap_halves(dx * sin_signed)


def _ret_fwd(proj, cos, sin, gain, name):
    t = proj.shape[0]
    bk = TOK_TILE
    nb = t // bk
    w, wq, wk, mask, g_blk = _ret_consts(bk)

    def body(q_ref, k_ref, v_ref, rg_ref, cos_ref, sin_ref, w_ref, wq_ref, wk_ref, mask_ref, gain_ref,
             opre_ref, og_ref, st_ref, r_ref):
        i = pl.program_id(0)

        @pl.when(i == 0)
        def _():
            r_ref[...] = jnp.zeros_like(r_ref)

        c, s = cos_ref[...], sin_ref[...]
        valid = ((i * bk + lax.broadcasted_iota(jnp.int32, (bk, 1), 0)) >= N_PAD).astype(F32)
        qr = _rope(q_ref[...], c, s)
        kr = _rope(k_ref[...], c, s) * QK_SCALE * valid
        kb = kr.astype(BF16)
        for h in range(RET_HEADS):
            hm = mask_ref[h]
            cols = slice(RET_DV * h, RET_DV * (h + 1))
            vh = v_ref[:, cols].astype(BF16)
            r_prev = r_ref[h]
            st_ref[0, h] = r_prev
            sm = _dot((qr * hm).astype(BF16), kb, NT) * w_ref[h]
            o = _dot(sm.astype(BF16), vh) + _dot((qr * (hm * wq_ref[h])).astype(BF16), r_prev.astype(BF16))
            r_ref[h] = g_blk[h] * r_prev + _dot((kr * wk_ref[h]).astype(BF16), vh, TN)
            opre_ref[:, cols] = o
            rstd = lax.rsqrt(jnp.mean(o * o, axis=-1, keepdims=True) + EPS)
            rg = rg_ref[:, cols]
            og_ref[:, cols] = (o * rstd * gain_ref[:, cols] * (rg * _sigmoid(rg))).astype(BF16)

    full = lambda shape: pl.BlockSpec(shape, lambda i: (0,) * len(shape))
    return _pcall(
        body, name=name, grid=(nb,),
        in_specs=[pl.BlockSpec((bk, RET_QK), lambda i: (i, 0)), pl.BlockSpec((bk, RET_QK), lambda i: (i, 1)),
                  pl.BlockSpec((bk, RET_V), lambda i: (i, 1)), pl.BlockSpec((bk, RET_V), lambda i: (i, 2)),
                  pl.BlockSpec((bk, RET_QK), lambda i: (i, 0)), pl.BlockSpec((bk, RET_QK), lambda i: (i, 0)),
                  full((RET_HEADS, bk, bk)), full((RET_HEADS, bk, 1)), full((RET_HEADS, bk, 1)),
                  full((RET_HEADS, 1, RET_QK)), full((1, RET_V))],
        out_specs=[pl.BlockSpec((bk, RET_V), lambda i: (i, 0)), pl.BlockSpec((bk, RET_V), lambda i: (i, 0)),
                   pl.BlockSpec((1, RET_HEADS, RET_QK, RET_DV), lambda i: (i, 0, 0, 0))],
        out_shape=[jax.ShapeDtypeStruct((t, RET_V), F32), jax.ShapeDtypeStruct((t, RET_V + FOX_W), BF16),
                   jax.ShapeDtypeStruct((nb, RET_HEADS, RET_QK, RET_DV), F32)],
        scratch_shapes=[pltpu.VMEM((RET_HEADS, RET_QK, RET_DV), F32)],
        compiler_params=_params("arbitrary"),
    )(proj, proj, proj, proj, cos, sin, w, wq, wk, mask, gain)


def _ret_bwd(proj, cos, sin, gain, dmixed, opre, states, name):
    t = proj.shape[0]
    bk = TOK_TILE
    nb = t // bk
    w, wq, wk, mask, g_blk = _ret_consts(bk)
    v0, g0 = 2 * RET_QK, 2 * RET_QK + RET_V

    def body(q_ref, k_ref, v_ref, rg_ref, cos_ref, sin_ref, w_ref, wq_ref, wk_ref, mask_ref, gain_ref,
             dog_ref, opre_ref, st_ref, dp_ref, gg_ref, dr_ref):
        step = pl.program_id(0)
        i = nb - 1 - step

        @pl.when(step == 0)
        def _():
            dr_ref[...] = jnp.zeros_like(dr_ref)
            gg_ref[...] = jnp.zeros_like(gg_ref)

        c, s = cos_ref[...], sin_ref[...]
        valid = ((i * bk + lax.broadcasted_iota(jnp.int32, (bk, 1), 0)) >= N_PAD).astype(F32)
        qr = _rope(q_ref[...], c, s)
        kr = _rope(k_ref[...], c, s) * QK_SCALE * valid
        kb = kr.astype(BF16)
        dqr = jnp.zeros((bk, RET_QK), F32)
        dkr = jnp.zeros((bk, RET_QK), F32)
        for h in range(RET_HEADS):
            hm = mask_ref[h]
            cols = slice(RET_DV * h, RET_DV * (h + 1))
            vh = v_ref[:, cols].astype(BF16)
            o = opre_ref[:, cols]
            rstd = lax.rsqrt(jnp.mean(o * o, axis=-1, keepdims=True) + EPS)
            xhat = o * rstd
            rg = rg_ref[:, cols]
            sg = _sigmoid(rg)
            gate = rg * sg
            gn = gain_ref[:, cols]
            dog = dog_ref[:, cols]
            dp_ref[:, g0 + RET_DV * h:g0 + RET_DV * (h + 1)] = (
                dog * xhat * gn * (sg * (1.0 + rg * (1.0 - sg)))).astype(BF16)
            gg_ref[:, cols] += jnp.sum(dog * xhat * gate, axis=0, keepdims=True)
            dxh = dog * gn * gate
            do = (rstd * (dxh - xhat * jnp.mean(dxh * xhat, axis=-1, keepdims=True))).astype(BF16)
            qm = (qr * hm).astype(BF16)
            qw = (qr * (hm * wq_ref[h])).astype(BF16)
            kw = (kr * wk_ref[h]).astype(BF16)
            wh = w_ref[h]
            sm = (_dot(qm, kb, NT) * wh).astype(BF16)
            ds = (_dot(do, vh, NT) * wh).astype(BF16)
            dr = dr_ref[h]
            drb = dr.astype(BF16)
            dp_ref[:, v0 + RET_DV * h:v0 + RET_DV * (h + 1)] = (_dot(sm, do, TN) + _dot(kw, drb)).astype(BF16)
            dqr = dqr + _dot(ds, kb) * hm + _dot(do, st_ref[0, h].astype(BF16), NT) * (hm * wq_ref[h])
            dkr = dkr + _dot(ds, qm, TN) + _dot(vh, drb, NT) * wk_ref[h]
            dr_ref[h] = g_blk[h] * dr + _dot(qw, do, TN)
        dp_ref[:, 0:RET_QK] = _rope_t(dqr, c, s).astype(BF16)
        dp_ref[:, RET_QK:2 * RET_QK] = _rope_t(dkr * (QK_SCALE * valid), c, s).astype(BF16)

    full = lambda shape: pl.BlockSpec(shape, lambda i: (0,) * len(shape))
    rev = lambda col: (lambda i: (nb - 1 - i, col))
    return _pcall(
        body, name=name, grid=(nb,),
        in_specs=[pl.BlockSpec((bk, RET_QK), rev(0)), pl.BlockSpec((bk, RET_QK), rev(1)),
                  pl.BlockSpec((bk, RET_V), rev(1)), pl.BlockSpec((bk, RET_V), rev(2)),
                  pl.BlockSpec((bk, RET_QK), rev(0)), pl.BlockSpec((bk, RET_QK), rev(0)),
                  full((RET_HEADS, bk, bk)), full((RET_HEADS, bk, 1)), full((RET_HEADS, bk, 1)),
                  full((RET_HEADS, 1, RET_QK)), full((1, RET_V)),
                  pl.BlockSpec((bk, RET_V), rev(0)), pl.BlockSpec((bk, RET_V), rev(0)),
                  pl.BlockSpec((1, RET_HEADS, RET_QK, RET_DV), lambda i: (nb - 1 - i, 0, 0, 0))],
        out_specs=[pl.BlockSpec((bk, g0 + RET_V), rev(0)), pl.BlockSpec((1, RET_V), lambda i: (0, 0))],
        out_shape=[jax.ShapeDtypeStruct((t, IN_PAD), BF16), jax.ShapeDtypeStruct((1, RET_V), F32)],
        scratch_shapes=[pltpu.VMEM((RET_HEADS, RET_QK, RET_DV), F32)],
        compiler_params=_params("arbitrary"),
    )(proj, proj, proj, proj, cos, sin, w, wq, wk, mask, gain, dmixed, opre, states)


def _forget_cumsum(proj, bias, name):
    t = proj.shape[0]
    rt = TOK_TILE
    nb = t // rt
    tril = jnp.asarray(np.tril(np.ones((rt, rt))), F32)

    def body(z_ref, b_ref, tril_ref, c_ref, carry_ref):
        i = pl.program_id(0)

        @pl.when(i == 0)
        def _():
            carry_ref[...] = jnp.zeros_like(carry_ref)

        z = z_ref[...] + b_ref[...]
        logf = jnp.minimum(z, 0.0) - jnp.log(1.0 + jnp.exp(-jnp.abs(z)))
        c = lax.dot_general(tril_ref[...], logf, NN, precision=lax.Precision.HIGHEST,
                            preferred_element_type=F32) + carry_ref[...]
        c_ref[...] = c
        carry_ref[...] = c[rt - 1:rt, :]

    return _pcall(
        body, name=name, grid=(nb,),
        in_specs=[pl.BlockSpec((rt, LANE), lambda i: (i, FF_COL_BLOCK)), pl.BlockSpec((1, LANE), lambda i: (0, 0)),
                  pl.BlockSpec((rt, rt), lambda i: (0, 0))],
        out_specs=pl.BlockSpec((rt, LANE), lambda i: (i, 0)),
        out_shape=jax.ShapeDtypeStruct((t, LANE), F32),
        scratch_shapes=[pltpu.VMEM((1, LANE), F32)],
        compiler_params=_params("arbitrary"),
    )(proj, bias, tril)


def _forget_cumsum_bwd(proj, bias, drs, dcs, dproj, name):
    t = proj.shape[0]
    rt = TOK_TILE
    nb = t // rt
    triu = jnp.asarray(np.triu(np.ones((rt, rt))), F32)

    def body(z_ref, b_ref, triu_ref, drs_ref, dcs_ref, dproj_in, dz_ref, gb_ref, carry_ref):
        step = pl.program_id(0)

        @pl.when(step == 0)
        def _():
            carry_ref[...] = jnp.zeros_like(carry_ref)
            gb_ref[...] = jnp.zeros_like(gb_ref)

        dlogf = lax.dot_general(triu_ref[...], drs_ref[...] - dcs_ref[...], NN, precision=lax.Precision.HIGHEST,
                                preferred_element_type=F32) + carry_ref[...]
        carry_ref[...] = dlogf[0:1, :]
        z = z_ref[...] + b_ref[...]
        is_head = lax.broadcasted_iota(jnp.int32, (rt, LANE), 1) < FOX_HEADS
        dz = jnp.where(is_head, dlogf / (1.0 + jnp.exp(z)), 0.0)
        dz_ref[...] = dz.astype(BF16)
        gb_ref[...] += jnp.sum(dz, axis=0, keepdims=True)

    return _pcall(
        body, name=name, grid=(nb,),
        in_specs=[pl.BlockSpec((rt, LANE), lambda i: (nb - 1 - i, FF_COL_BLOCK)),
                  pl.BlockSpec((1, LANE), lambda i: (0, 0)),
                  pl.BlockSpec((rt, rt), lambda i: (0, 0)),
                  pl.BlockSpec((rt, LANE), lambda i: (nb - 1 - i, 0)),
                  pl.BlockSpec((rt, LANE), lambda i: (nb - 1 - i, 0)),
                  pl.BlockSpec(memory_space=pl.ANY)],
        out_specs=[pl.BlockSpec((rt, LANE), lambda i: (nb - 1 - i, FF_COL_BLOCK)),
                   pl.BlockSpec((1, LANE), lambda i: (0, 0))],
        out_shape=[jax.ShapeDtypeStruct(dproj.shape, BF16), jax.ShapeDtypeStruct((1, LANE), F32)],
        input_output_aliases={5: 0},
        scratch_shapes=[pltpu.VMEM((1, LANE), F32)],
        compiler_params=_params("arbitrary"),
    )(proj, bias, triu, drs, dcs, dproj)


FOX_PAIRS = FOX_HEADS // 2
L_ONE_Q = FOX_DH
L_ONE_K = FOX_DH + 3
L_LSE = FOX_DH + 4


def _split3(x):
    hi = x.astype(BF16).astype(F32)
    r = x - hi
    mid = r.astype(BF16).astype(F32)
    return hi, mid, r - mid


def _head_to_low(slab, e):
    return slab if e == 0 else pltpu.roll(slab, FOX_DH, axis=1)


def _pair(a, b, low):
    return jnp.where(low, a, pltpu.roll(b, FOX_DH, axis=1))


def _fox_prep(proj, c, name):
    t = proj.shape[0]
    tq = TOK_TILE

    def body(p_ref, c_ref, qa_ref, ka_ref, va_ref, qt_ref, vt_ref):
        i = pl.program_id(0)
        lane = lax.broadcasted_iota(jnp.int32, (tq, LANE), 1)
        low = lane < FOX_DH
        live = (i * tq + lax.broadcasted_iota(jnp.int32, (tq, 1), 0)) >= N_PAD
        q_tail = jnp.where(lane < L_ONE_Q + 3, 1.0, 0.0)
        k_ones = (lane >= L_ONE_K) & (lane < L_ONE_K + 4)
        v_tail = jnp.where(lane < FOX_DH + 2, 1.0, 0.0)
        bias_parts = _split3(jnp.where(live, -c_ref[...], NEG))
        for pair in range(FOX_PAIRS):
            base = 3 * LANE * pair
            for e in range(2):
                h = 2 * pair + e
                q = _head_to_low(p_ref[:, base:base + LANE], e)
                k = _head_to_low(p_ref[:, base + LANE:base + 2 * LANE], e)
                v = _head_to_low(p_ref[:, base + 2 * LANE:base + 3 * LANE], e)
                hi, mid, lo = [part[:, h:h + 1] for part in bias_parts]
                ka = jnp.where(low, k, jnp.where(k_ones, 1.0, 0.0))
                ka = jnp.where(lane == L_ONE_Q, hi, jnp.where(lane == L_ONE_Q + 1, mid, jnp.where(lane == L_ONE_Q + 2, lo, ka)))
                qa = jnp.where(low, q * QK_SCALE, q_tail)
                va = jnp.where(low, v, v_tail)
                qa_ref[h] = qa.astype(BF16)
                ka_ref[h] = ka.astype(BF16)
                va_ref[h] = va.astype(BF16)
                qt_ref[h] = qa.T.astype(BF16)
                vt_ref[h] = va.T.astype(BF16)

    out = jax.ShapeDtypeStruct((FOX_HEADS, t, LANE), BF16)
    out_t = jax.ShapeDtypeStruct((FOX_HEADS, t // tq, LANE, tq), BF16)
    ospec = pl.BlockSpec((FOX_HEADS, tq, LANE), lambda i: (0, i, 0))
    tspec = pl.BlockSpec((FOX_HEADS, None, LANE, tq), lambda i: (0, i, 0, 0))
    return _pcall(
        body, name=name, grid=(t // tq,),
        in_specs=[pl.BlockSpec((tq, 3 * FOX_W), lambda i: (i, 1)), pl.BlockSpec((tq, LANE), lambda i: (i, 0))],
        out_specs=[ospec, ospec, ospec, tspec, tspec], out_shape=[out, out, out, out_t, out_t],
        compiler_params=_params("parallel"),
    )(proj, c)


STEP_PAIRS = 2
STEP_HEADS = 2 * STEP_PAIRS
FOX_GROUPS = FOX_PAIRS // STEP_PAIRS
FWD_PAIRS = 4
FWD_HEADS = 2 * FWD_PAIRS
FWD_GROUPS = FOX_PAIRS // FWD_PAIRS


def _fox_fwd(qt, ka, vt, mixed, name):
    nh, nq, tq, _ = ka.shape
    t = nq * tq

    def body(qt_ref, ka_ref, vt_ref, mixed_in, mixed_ref, o_ref, lse_ref):
        i = pl.program_id(1)
        lane = lax.broadcasted_iota(jnp.int32, (tq, LANE), 1)
        key_le_query = lax.broadcasted_iota(jnp.int32, (tq, tq), 0) <= lax.broadcasted_iota(jnp.int32, (tq, tq), 1)

        def logits(j):
            return [_dot(ka_ref[h, j], qt_ref[h]) for h in range(FWD_HEADS)]

        def update(j, scores, carry, diagonal):
            new = []
            for h in range(FWD_HEADS):
                m, acc = carry[h]
                s = jnp.where(key_le_query, scores[h], NEG) if diagonal else scores[h]
                m_new = jnp.maximum(m, jnp.max(s, axis=0, keepdims=True))
                p = jnp.exp(s - m_new).astype(BF16)
                new.append((m_new, jnp.exp(m - m_new) * acc + _dot(vt_ref[h, j], p)))
            return tuple(new)

        init = tuple((jnp.full((1, tq), NEG, F32), jnp.zeros((LANE, tq), F32)) for _ in range(FWD_HEADS))
        carry = lax.fori_loop(0, i, lambda j, cr: update(j, logits(j), cr, False), init)
        outs, lse_rows = [], []
        for m, acc in update(i, logits(i), carry, True):
            l = acc[FOX_DH:FOX_DH + 1, :]
            outs.append((acc / l).T)
            lse_rows.append(m + jnp.log(l))
        lse_rows.append(jnp.zeros((LANE - FWD_HEADS, tq), F32))
        o_all = jnp.concatenate([_pair(outs[2 * c], outs[2 * c + 1], lane < FOX_DH) for c in range(FWD_PAIRS)], axis=1)
        mixed_ref[...] = o_all.astype(BF16)
        o_ref[...] = o_all
        lse_ref[...] = jnp.concatenate(lse_rows, axis=0).T

    width = FWD_PAIRS * LANE
    whole = pl.BlockSpec((FWD_HEADS, nq, tq, LANE), lambda g, i: (g, 0, 0, 0), pipeline_mode=pl.Buffered(1))
    whole_t = pl.BlockSpec((FWD_HEADS, nq, LANE, tq), lambda g, i: (g, 0, 0, 0), pipeline_mode=pl.Buffered(1))
    return _pcall(
        body, name=name, grid=(FWD_GROUPS, nq),
        in_specs=[pl.BlockSpec((FWD_HEADS, None, LANE, tq), lambda g, i: (g, i, 0, 0)), whole, whole_t,
                  pl.BlockSpec(memory_space=pl.ANY)],
        out_specs=[pl.BlockSpec((tq, width), lambda g, i: (i, RET_V // width + g)),
                   pl.BlockSpec((tq, width), lambda g, i: (i, g)),
                   pl.BlockSpec((None, tq, LANE), lambda g, i: (g, i, 0))],
        out_shape=[jax.ShapeDtypeStruct(mixed.shape, BF16), jax.ShapeDtypeStruct((t, FOX_W), F32),
                   jax.ShapeDtypeStruct((FWD_GROUPS, t, LANE), F32)],
        input_output_aliases={3: 0},
        compiler_params=_params("parallel", "parallel"),
    )(qt, ka, vt, mixed)


def _fox_prep_bwd(dmixed, o_fox, lse, qa, name):
    t = dmixed.shape[0]
    tq = TOK_TILE

    def body(dm_ref, o_ref, lse_ref, qa_ref, qab_ref, doa_ref):
        i = pl.program_id(0)
        lane = lax.broadcasted_iota(jnp.int32, (tq, LANE), 1)
        low = lane < FOX_DH
        live = (i * tq + lax.broadcasted_iota(jnp.int32, (tq, 1), 0)) >= N_PAD
        lse_parts = [_split3(jnp.where(live, -lse_ref[grp], 0.0)) for grp in range(FWD_GROUPS)]
        for pair in range(FOX_PAIRS):
            cols = slice(LANE * pair, LANE * (pair + 1))
            d_slab = dm_ref[:, cols]
            prod = d_slab * o_ref[:, cols]
            for e in range(2):
                h = 2 * pair + e
                nd = -jnp.sum(jnp.where(low, _head_to_low(prod, e), 0.0), axis=-1, keepdims=True)
                nd_hi = nd.astype(BF16).astype(F32)
                doa = jnp.where(low, _head_to_low(d_slab, e), 0.0)
                doa = jnp.where(lane == FOX_DH, nd_hi, jnp.where(lane == FOX_DH + 1, nd - nd_hi, doa))
                doa_ref[h] = doa.astype(BF16)
                lane_h = h % FWD_HEADS
                hi, mid, lo = [part[:, lane_h:lane_h + 1] for part in lse_parts[h // FWD_HEADS]]
                qab = qa_ref[h].astype(F32)
                qab = jnp.where(lane == L_LSE, hi, jnp.where(lane == L_LSE + 1, mid, jnp.where(lane == L_LSE + 2, lo, qab)))
                qab_ref[h] = qab.astype(BF16)

    out = jax.ShapeDtypeStruct((FOX_HEADS, t, LANE), BF16)
    hspec = pl.BlockSpec((FOX_HEADS, tq, LANE), lambda i: (0, i, 0))
    return _pcall(
        body, name=name, grid=(t // tq,),
        in_specs=[pl.BlockSpec((tq, FOX_W), lambda i: (i, 1)), pl.BlockSpec((tq, FOX_W), lambda i: (i, 0)),
                  pl.BlockSpec((FWD_GROUPS, tq, LANE), lambda i: (0, i, 0)), hspec],
        out_specs=[hspec, hspec], out_shape=[out, out],
        compiler_params=_params("parallel"),
    )(dmixed, o_fox, lse, qa)


def _fox_bwd(qab, doa, ka, va, dproj, name):
    nh, nq, tq, _ = qab.shape
    t = nq * tq
    slab = 3 * LANE * STEP_PAIRS
    group0 = (2 * RET_QK + 2 * RET_V) // slab

    def body(qab_ref, doa_ref, ka_ref, va_ref, dproj_in, dp_ref, drs_ref, dcs_ref, dq_ref):
        g, j = pl.program_id(0), pl.program_id(1)

        @pl.when((g == 0) & (j == 0))
        def _():
            drs_ref[...] = jnp.zeros_like(drs_ref)
            dcs_ref[...] = jnp.zeros_like(dcs_ref)

        @pl.when(j == 0)
        def _():
            dq_ref[...] = jnp.zeros_like(dq_ref)

        lane = lax.broadcasted_iota(jnp.int32, (tq, LANE), 1)
        low = lane < FOX_DH
        key_le_query = lax.broadcasted_iota(jnp.int32, (tq, tq), 0) <= lax.broadcasted_iota(jnp.int32, (tq, tq), 1)

        def by_head(c, a, b, col):
            h = STEP_HEADS * g + 2 * c
            return jnp.where(lane == h, a[:, col:col + 1], jnp.where(lane == h + 1, b[:, col:col + 1], 0.0))


        def step(i, carry, diagonal):
            st = [_dot(ka_ref[h], qab_ref[h, i], NT) for h in range(STEP_HEADS)]
            dpt = [_dot(va_ref[h], doa_ref[h, i], NT) for h in range(STEP_HEADS)]
            new = []
            for h in range(STEP_HEADS):
                p = jnp.exp(st[h])
                if diagonal:
                    p = jnp.where(key_le_query, p, 0.0)
                ds = (p * dpt[h]).astype(BF16)
                dq_ref[h, i] += _dot(ds, ka_ref[h], TN)
                dk, dv = carry[h]
                new.append((dk + _dot(ds, qab_ref[h, i]), dv + _dot(p.astype(BF16), doa_ref[h, i])))
            return tuple(new)

        zero = jnp.zeros((tq, LANE), F32)
        carry = step(j, tuple((zero, zero) for _ in range(STEP_HEADS)), True)
        carry = lax.fori_loop(j + 1, nq, lambda i, cr: step(i, cr, False), carry)
        rows = pl.ds(pl.multiple_of(j * tq, tq), tq)
        for c in range(STEP_PAIRS):
            (dka, dva), (dkb, dvb) = carry[2 * c], carry[2 * c + 1]
            c0 = 3 * LANE * c
            dp_ref[rows, c0 + LANE:c0 + 2 * LANE] = _pair(dka, dkb, low).astype(BF16)
            dp_ref[rows, c0 + 2 * LANE:c0 + 3 * LANE] = _pair(dva, dvb, low).astype(BF16)
            dcs_ref[rows, :] += by_head(c, dka, dkb, L_ONE_Q)

        @pl.when(j == nq - 1)
        def _():
            for c in range(STEP_PAIRS):
                for blk in range(nq):
                    r = slice(blk * tq, (blk + 1) * tq)
                    a, b = dq_ref[2 * c, blk], dq_ref[2 * c + 1, blk]
                    dp_ref[r, 3 * LANE * c:3 * LANE * c + LANE] = (_pair(a, b, low) * QK_SCALE).astype(BF16)
                    drs_ref[r, :] += by_head(c, a, b, L_ONE_K)

    whole = pl.BlockSpec((STEP_HEADS, nq, tq, LANE), lambda g, j: (g, 0, 0, 0), pipeline_mode=pl.Buffered(1))
    blk = pl.BlockSpec((STEP_HEADS, None, tq, LANE), lambda g, j: (g, j, 0, 0))
    sums = pl.BlockSpec((t, LANE), lambda g, j: (0, 0), pipeline_mode=pl.Buffered(1))
    return _pcall(
        body, name=name, grid=(FOX_GROUPS, nq),
        in_specs=[whole, whole, blk, blk, pl.BlockSpec(memory_space=pl.ANY)],
        out_specs=[pl.BlockSpec((t, slab), lambda g, j: (0, group0 + g)), sums, sums],
        out_shape=[jax.ShapeDtypeStruct(dproj.shape, BF16), jax.ShapeDtypeStruct((t, LANE), F32),
                   jax.ShapeDtypeStruct((t, LANE), F32)],
        input_output_aliases={4: 0},
        scratch_shapes=[pltpu.VMEM((STEP_HEADS, nq, tq, LANE), F32)],
        compiler_params=_params("arbitrary", "arbitrary"),
    )(qab, doa, ka, va, dproj)


HALO = 8


def _rows_ext(ref, r0, rows, t, before, after):
    lo, hi = r0 - before, r0 + rows + after
    width = ref.shape[-1]
    parts = []
    if lo < 0:
        parts.append(jnp.zeros((-lo, width), F32))
    parts.append(ref[max(lo, 0):min(hi, t), :].astype(F32))
    if hi > t:
        parts.append(jnp.zeros((hi - t, width), F32))
    return parts[0] if len(parts) == 1 else jnp.concatenate(parts, axis=0)


def _conv_taps(a_ext, r0_ext, cw_ref, cb_ref):
    n = a_ext.shape[0]
    if r0_ext < N_PAD:
        row = r0_ext + lax.broadcasted_iota(jnp.int32, (n, 1), 0)
        a_ext = jnp.where(row >= N_PAD, a_ext, 0.0)
    a1 = pltpu.roll(a_ext, 1, axis=0)
    a2 = pltpu.roll(a_ext, 2, axis=0)
    acc = cb_ref[...] + a2 * cw_ref[0:1, :] + a1 * cw_ref[1:2, :] + a_ext * cw_ref[2:3, :]
    return a_ext, a1, a2, acc


FF_COLS = 256


def _up_conv_fwd(n2, w_up_t, conv_w8, conv_b, name):
    t, d = n2.shape
    f = w_up_t.shape[1]
    rows = TOK_TILE
    starts = list(range(0, t, rows))

    def body(n_ref, wa_ref, wb_ref, cw_ref, cb_ref, up_ref, g_ref):
        def project(r0):
            n_rows = n_ref[r0:r0 + rows, :]
            up_ref[0, r0:r0 + rows, :] = _dot(n_rows, wa_ref[...], NT)
            up_ref[1, r0:r0 + rows, :] = _dot(n_rows, wb_ref[...], NT)

        def activate(r0):
            a_ext = _rows_ext(up_ref.at[0], r0, rows, t, HALO, 0)
            _, _, _, acc = _conv_taps(a_ext, r0 - HALO, cw_ref, cb_ref)
            acc = acc[HALO:, :]
            g_ref[r0:r0 + rows, :] = (acc * _sigmoid(acc) * up_ref[1, r0:r0 + rows, :]).astype(BF16)

        project(starts[0])
        for r0, r_next in zip(starts, starts[1:] + [None]):
            if r_next is not None:
                project(r_next)
            activate(r0)

    return _pcall(
        body, name=name, grid=(f // FF_COLS,),
        in_specs=[pl.BlockSpec((t, d), lambda j: (0, 0), pipeline_mode=pl.Buffered(1)),
                  pl.BlockSpec((None, FF_COLS, d), lambda j: (0, j, 0)), pl.BlockSpec((None, FF_COLS, d), lambda j: (1, j, 0)),
                  pl.BlockSpec((8, FF_COLS), lambda j: (0, j)), pl.BlockSpec((1, FF_COLS), lambda j: (0, j))],
        out_specs=[pl.BlockSpec((2, t, FF_COLS), lambda j: (0, 0, j)), pl.BlockSpec((t, FF_COLS), lambda j: (0, j))],
        out_shape=[jax.ShapeDtypeStruct((2, t, f), F32), jax.ShapeDtypeStruct((t, f), BF16)],
        compiler_params=_params("parallel"),
    )(n2, w_up_t, w_up_t, conv_w8, conv_b)


def _dg_conv_bwd(up, conv_w8, conv_b, dh2, w_down, name):
    _, t, f = up.shape
    d = dh2.shape[1]
    rows = TOK_TILE
    starts = list(range(0, t, rows))

    def body(a_ref, b_ref, cw_ref, cb_ref, dh_ref, wd_ref, dup_ref, gcw_ref, gcb_ref, dg_ref):
        def project(r0):
            dg_ref[r0:r0 + rows, :] = _dot(dh_ref[r0:r0 + rows, :], wd_ref[...], NT)

        gw = [jnp.zeros((1, FF_COLS), F32) for _ in range(3)]
        gb = jnp.zeros((1, FF_COLS), F32)
        project(starts[0])
        for r0, r_next in zip(starts, starts[1:] + [None]):
            if r_next is not None:
                project(r_next)
            a_ext = _rows_ext(a_ref, r0, rows, t, HALO, HALO)
            b_ext = _rows_ext(b_ref, r0, rows, t, HALO, HALO)
            dg_ext = _rows_ext(dg_ref, r0, rows, t, HALO, HALO)
            a0, a1, a2, acc = _conv_taps(a_ext, r0 - HALO, cw_ref, cb_ref)
            sg = _sigmoid(acc)
            dacc = dg_ext * b_ext * (sg * (1.0 + acc * (1.0 - sg)))
            n = dacc.shape[0]
            da = (dacc * cw_ref[2:3, :] + pltpu.roll(dacc, n - 1, axis=0) * cw_ref[1:2, :]
                  + pltpu.roll(dacc, n - 2, axis=0) * cw_ref[0:1, :])
            core = slice(HALO, HALO + rows)
            da = da[core, :]
            if r0 < N_PAD:
                row = r0 + lax.broadcasted_iota(jnp.int32, (rows, 1), 0)
                da = jnp.where(row >= N_PAD, da, 0.0)
            dup_ref[0, r0:r0 + rows, :] = da.astype(BF16)
            dup_ref[1, r0:r0 + rows, :] = (dg_ext * acc * sg)[core, :].astype(BF16)
            dacc_c = dacc[core, :]
            gw[0] = gw[0] + jnp.sum(dacc_c * a2[core, :], axis=0, keepdims=True)
            gw[1] = gw[1] + jnp.sum(dacc_c * a1[core, :], axis=0, keepdims=True)
            gw[2] = gw[2] + jnp.sum(dacc_c * a0[core, :], axis=0, keepdims=True)
            gb = gb + jnp.sum(dacc_c, axis=0, keepdims=True)
        gcw_ref[...] = jnp.zeros((8, FF_COLS), F32)
        for tap in range(3):
            gcw_ref[tap:tap + 1, :] = gw[tap]
        gcb_ref[...] = gb

    return _pcall(
        body, name=name, grid=(f // FF_COLS,),
        in_specs=[pl.BlockSpec((None, t, FF_COLS), lambda j: (0, 0, j)), pl.BlockSpec((None, t, FF_COLS), lambda j: (1, 0, j)),
                  pl.BlockSpec((8, FF_COLS), lambda j: (0, j)), pl.BlockSpec((1, FF_COLS), lambda j: (0, j)),
                  pl.BlockSpec((t, d), lambda j: (0, 0), pipeline_mode=pl.Buffered(1)),
                  pl.BlockSpec((FF_COLS, d), lambda j: (j, 0))],
        out_specs=[pl.BlockSpec((2, t, FF_COLS), lambda j: (0, 0, j)), pl.BlockSpec((8, FF_COLS), lambda j: (0, j)),
                   pl.BlockSpec((1, FF_COLS), lambda j: (0, j))],
        out_shape=[jax.ShapeDtypeStruct((2, t, f), BF16), jax.ShapeDtypeStruct((8, f), F32),
                   jax.ShapeDtypeStruct((1, f), F32)],
        scratch_shapes=[pltpu.VMEM((t, FF_COLS), F32)],
        compiler_params=_params("parallel"),
    )(up, up, conv_w8, conv_b, dh2, w_down)


def _exchange(arrays, kinds, name, after=None):
    n = len(arrays)
    npeer = N_DEV - 1
    n_in = n + int(after is not None)

    def body(*refs):
        ins, outs = refs[:n], refs[n_in:n_in + n]
        send_sems, recv_sems, local_sems = refs[n_in + n:]
        x, y, c = lax.axis_index("x"), lax.axis_index("y"), lax.axis_index("c")
        me = 4 * x + 2 * y + c
        copies, locals_ = [], []
        for a in range(n):
            gather = kinds[a] == "gather"
            own = pltpu.make_async_copy(ins[a] if gather else ins[a].at[me], outs[a].at[me], local_sems.at[a])
            own.start()
            locals_.append(own)
            for d in range(1, N_DEV):
                px = 1 - x if d & 4 else x
                py = 1 - y if d & 2 else y
                pc = 1 - c if d & 1 else c
                src = ins[a] if gather else ins[a].at[4 * px + 2 * py + pc]
                cp = pltpu.make_async_remote_copy(
                    src_ref=src, dst_ref=outs[a].at[me],
                    send_sem=send_sems.at[a * npeer + d - 1], recv_sem=recv_sems.at[a * npeer + d - 1],
                    device_id=(px, py, pc), device_id_type=pl.DeviceIdType.MESH)
                cp.start()
                copies.append(cp)
        for cp in copies:
            cp.wait_recv()
        for cp in copies:
            cp.wait_send()
        for own in locals_:
            own.wait()

    out_shape = [jax.ShapeDtypeStruct((N_DEV,) + (a.shape if k == "gather" else a.shape[1:]), a.dtype)
                 for a, k in zip(arrays, kinds)]
    return _pcall(
        body, name=name,
        in_specs=[pl.BlockSpec(memory_space=pl.ANY)] * n_in,
        out_specs=[pl.BlockSpec(memory_space=pl.ANY)] * n,
        out_shape=out_shape,
        scratch_shapes=[pltpu.SemaphoreType.DMA((n * npeer,)), pltpu.SemaphoreType.DMA((n * npeer,)),
                        pltpu.SemaphoreType.DMA((n,))],
        compiler_params=pltpu.CompilerParams(has_side_effects=True),
    )(*arrays, *([] if after is None else [after]))


ALL_PEERS = tuple(range(1, N_DEV))
SAME_CORE_AND_SIBLING = (1, 2, 4, 6)
OTHER_CHIPS = (2, 4, 6)


def _peer_copies(srcs, lands, kinds, send_sems, recv_sems, relations=ALL_PEERS):
    x, y, c = lax.axis_index("x"), lax.axis_index("y"), lax.axis_index("c")
    me = 4 * x + 2 * y + c
    copies = []
    for a in range(len(srcs)):
        for d in relations:
            px = 1 - x if d & 4 else x
            py = 1 - y if d & 2 else y
            pc = 1 - c if d & 1 else c
            peer = 4 * px + 2 * py + pc
            k = a * (N_DEV - 1) + d - 1
            if kinds[a] == "forward":
                src, dst, target = lands[a].at[peer], lands[a].at[peer], (x, y, 1 - c)
            else:
                src, dst, target = (srcs[a] if kinds[a] == "gather" else srcs[a].at[peer]), lands[a].at[me], (px, py, pc)
            copies.append(pltpu.make_async_remote_copy(
                src_ref=src, dst_ref=dst, send_sem=send_sems.at[k], recv_sem=recv_sems.at[k],
                device_id=target, device_id_type=pl.DeviceIdType.MESH))
    return copies


def _own_copies(srcs, lands, kinds, sems):
    me = 4 * lax.axis_index("x") + 2 * lax.axis_index("y") + lax.axis_index("c")
    first = len(srcs) * (N_DEV - 1)
    return [pltpu.make_async_copy(srcs[a].at[me] if kinds[a] == "scatter" else srcs[a], lands[a].at[me], sems.at[first + a])
            for a in range(len(srcs))]


def _exchange_start(arrays, kinds, name, after=None, relations=ALL_PEERS, lands=None, own=True):
    n = len(arrays)
    nsem = n * (N_DEV - 1) + n
    hbm = pl.BlockSpec(memory_space=pltpu.HBM)
    sem = pl.BlockSpec(memory_space=pltpu.SEMAPHORE)
    land_shapes = ([l.shape for l in lands] if lands is not None else
                   [(N_DEV,) + (a.shape if k == "gather" else a.shape[1:]) for a, k in zip(arrays, kinds)])

    n_in = 2 * n + int(after is not None)

    def body(*refs):
        srcs, land_refs = refs[:n], refs[n:2 * n]
        send_sems, recv_sems = refs[n_in], refs[n_in + 1]
        token = refs[-1]
        for cp in _peer_copies(srcs, land_refs, kinds, send_sems, recv_sems, relations):
            cp.start()
        for cp in _own_copies(srcs, land_refs, kinds, send_sems) if own else []:
            cp.start()
        token[...] = jnp.zeros_like(token)

    operands = [pltpu.with_memory_space_constraint(a, pltpu.HBM) for a in arrays]
    operands += (list(lands) if lands is not None else
                 [pltpu.with_memory_space_constraint(lax.empty(s, a.dtype), pltpu.HBM) for s, a in zip(land_shapes, arrays)])
    operands += [] if after is None else [after]
    out = _pcall(
        body, name=name,
        in_specs=[hbm] * (2 * n) + ([] if after is None else [pl.BlockSpec(memory_space=pl.ANY)]),
        out_specs=[sem, sem] + [hbm] * (2 * n) + [pl.BlockSpec(memory_space=pltpu.VMEM)],
        out_shape=[pltpu.SemaphoreType.DMA((nsem,)), pltpu.SemaphoreType.DMA((nsem,))]
        + [pltpu.HBM(a.shape, a.dtype) for a in arrays]
        + [pltpu.HBM(s, a.dtype) for s, a in zip(land_shapes, arrays)]
        + [jax.ShapeDtypeStruct((8, LANE), F32)],
        input_output_aliases={k: 2 + k for k in range(2 * n)},
        compiler_params=pltpu.CompilerParams(has_side_effects=pltpu.SideEffectType.DATAFLOW_SIDE_EFFECTING),
    )(*operands)
    return out[0], out[1], list(out[2:2 + n]), list(out[2 + n:2 + 2 * n]), out[-1]


def _exchange_wait(started, kinds, after, name, own=True, relations=ALL_PEERS, with_sources=False):
    send_sems, recv_sems, srcs, lands, _ = started
    n = len(srcs)
    hbm = pl.BlockSpec(memory_space=pltpu.HBM)
    sem = pl.BlockSpec(memory_space=pltpu.SEMAPHORE)

    def body(*refs):
        src_refs, land_refs = refs[:n], refs[n:2 * n]
        copies = _peer_copies(src_refs, land_refs, kinds, refs[2 * n], refs[2 * n + 1], relations)
        for cp in copies:
            cp.wait_send()
        for cp in copies:
            cp.wait_recv()
        for cp in _own_copies(src_refs, land_refs, kinds, refs[2 * n]) if own else []:
            cp.wait()

    out = _pcall(
        body, name=name,
        in_specs=[hbm] * (2 * n) + [sem, sem, pl.BlockSpec(memory_space=pl.ANY)],
        out_specs=[hbm] * (2 * n),
        out_shape=[pltpu.HBM(a.shape, a.dtype) for a in srcs + lands],
        input_output_aliases={k: k for k in range(2 * n)},
        compiler_params=pltpu.CompilerParams(has_side_effects=pltpu.SideEffectType.DATAFLOW_SIDE_EFFECTING),
    )(*srcs, *lands, send_sems, recv_sems, after)
    return (list(out[:n]), list(out[n:])) if with_sources else list(out[n:])


def _sum_slots(slots, name, rows_tile):
    nd, r, c = slots.shape

    def body(s_ref, o_ref):
        acc = s_ref[0].astype(F32)
        for p in range(1, nd):
            acc = acc + s_ref[p].astype(F32)
        o_ref[...] = acc

    return _pcall(
        body, name=name, grid=(r // rows_tile,),
        in_specs=[pl.BlockSpec((nd, rows_tile, c), lambda i: (0, i, 0))],
        out_specs=pl.BlockSpec((rows_tile, c), lambda i: (i, 0)),
        out_shape=jax.ShapeDtypeStruct((r, c), F32),
        compiler_params=_params("parallel"),
    )(slots)


def _sum_slots_small(slot_arrays, own_arrays, name):
    n = len(slot_arrays)

    def body(*refs):
        me = 4 * lax.axis_index("x") + 2 * lax.axis_index("y") + lax.axis_index("c")
        for s_ref, own_ref, o_ref in zip(refs[:n], refs[n:2 * n], refs[2 * n:]):
            acc = jnp.where(me == 0, own_ref[...], s_ref[0])
            for p in range(1, s_ref.shape[0]):
                acc = acc + jnp.where(me == p, own_ref[...], s_ref[p])
            o_ref[...] = acc

    return _pcall(body, name=name, out_shape=[jax.ShapeDtypeStruct(a.shape[1:], F32) for a in slot_arrays])(
        *slot_arrays, *own_arrays)


def _adamw_values(w, gr, m, v):
    nm = ADAM_B1 * m + (1.0 - ADAM_B1) * gr
    nv = ADAM_B2 * v + (1.0 - ADAM_B2) * (gr * gr)
    m_hat = nm / (1.0 - ADAM_B1 ** ADAM_STEP)
    v_hat = nv / (1.0 - ADAM_B2 ** ADAM_STEP)
    return -ADAM_LR * (m_hat / (jnp.sqrt(v_hat) + ADAM_EPS) + ADAM_WD * w), nm, nv


def _adamw_update(w_ref, g_ref, m_ref, v_ref, d_ref, nm_ref, nv_ref):
    d_ref[...], nm_ref[...], nv_ref[...] = _adamw_values(w_ref[...], g_ref[...], m_ref[...], v_ref[...])


def _adamw_from_slots(w, slots, m, v, name, cols_tile=2 * LANE):
    rows, cols = w.shape
    nd, rows_pad, _ = slots.shape

    def body(w_ref, s_ref, m_ref, v_ref, g_ref, d_ref, nm_ref, nv_ref):
        gr = s_ref[0, 0:rows, :].astype(F32)
        for p in range(1, nd):
            gr = gr + s_ref[p, 0:rows, :].astype(F32)
        g_ref[...] = gr
        d_ref[...], nm_ref[...], nv_ref[...] = _adamw_values(w_ref[...], gr, m_ref[...], v_ref[...])

    spec = pl.BlockSpec((rows, cols_tile), lambda i: (0, i))
    slot_spec = pl.BlockSpec((nd, rows_pad, cols_tile), lambda i: (0, 0, i))
    return _pcall(
        body, name=name, grid=(cols // cols_tile,), in_specs=[spec, slot_spec, spec, spec], out_specs=[spec] * 4,
        out_shape=[jax.ShapeDtypeStruct((rows, cols), F32)] * 4, compiler_params=_params("parallel"),
    )(w, slots, m, v)


def _adamw_small(ws, gs, ms, vs, name):
    n = len(ws)

    def body(*refs):
        ins, outs = refs[:4 * n], refs[4 * n:]
        for k in range(n):
            _adamw_update(ins[k], ins[n + k], ins[2 * n + k], ins[3 * n + k], outs[k], outs[n + k], outs[2 * n + k])

    shapes = [jax.ShapeDtypeStruct(w.shape, F32) for w in ws]
    out = _pcall(body, name=name, out_shape=shapes * 3)(*ws, *gs, *ms, *vs)
    return list(out[:n]), list(out[n:2 * n]), list(out[2 * n:])


def _adamw(w, g, m, v, name, rows_tile):
    r, c = w.shape
    body = lambda *refs: _adamw_update(*refs)
    spec = pl.BlockSpec((rows_tile, c), lambda i: (i, 0))
    shp = jax.ShapeDtypeStruct((r, c), F32)
    return _pcall(
        body, name=name, grid=(r // rows_tile,), in_specs=[spec] * 4, out_specs=[spec] * 3, out_shape=[shp] * 3,
        compiler_params=_params("parallel"),
    )(w, g, m, v)


F0 = 2 * RET_QK + 2 * RET_V


def _to_internal_rows(w_t):
    cols = w_t.shape[1]
    fox = w_t[F0:F0 + 3 * FOX_W].reshape(3, FOX_PAIRS, LANE, cols).transpose(1, 0, 2, 3).reshape(3 * FOX_W, cols)
    tail = jnp.zeros((IN_PAD - IN_WIDTH, cols), w_t.dtype)
    return jnp.concatenate([w_t[:F0], fox, w_t[F0 + 3 * FOX_W:], tail], axis=0)


def _from_internal_rows(g_t):
    cols = g_t.shape[1]
    fox = g_t[F0:F0 + 3 * FOX_W].reshape(FOX_PAIRS, 3, LANE, cols).transpose(1, 0, 2, 3).reshape(3 * FOX_W, cols)
    return jnp.concatenate([g_t[:F0], fox, g_t[F0 + 3 * FOX_W:F0 + 3 * FOX_W + FOX_HEADS]], axis=0)


IN_BLOCK = IN_WIDTH // N_DEV
IN_BLOCK_PAD = 400
BF16_ROWS = 16


def _slot_row_of_internal():
    rows = np.arange(IN_WIDTH)
    fox = rows[F0:F0 + 3 * FOX_W].reshape(3, FOX_PAIRS, LANE).transpose(1, 0, 2).reshape(-1)
    original = np.concatenate([rows[:F0], fox, rows[F0 + 3 * FOX_W:]])
    slot_rows = original // IN_BLOCK * IN_BLOCK_PAD + original % IN_BLOCK
    return np.concatenate([slot_rows, np.full(IN_PAD - IN_WIDTH, -1)])


def _internal_row_of_slot():
    forward = _slot_row_of_internal()
    back = np.full(N_DEV * IN_BLOCK_PAD, -1)
    back[forward[forward >= 0]] = np.nonzero(forward >= 0)[0]
    return back


def _row_runs(src_of_dst):
    tiles = []
    for t0 in range(0, len(src_of_dst), LANE):
        runs = []
        for o in range(LANE):
            s = int(src_of_dst[t0 + o])
            if s < 0:
                continue
            if runs and runs[-1][0] + runs[-1][2] == o and runs[-1][1] + runs[-1][2] == s:
                runs[-1][2] += 1
            else:
                runs.append([o, s, 1])
        tiles.append(runs)
    return tiles


def _move_rows(src, src_of_dst, name):
    n_src, cols = src.shape
    tiles = _row_runs(src_of_dst)

    def body(s_ref, o_ref):
        for t, runs in enumerate(tiles):
            rows = pl.ds(t * LANE, LANE)
            if not runs:
                o_ref[rows, :] = jnp.zeros((LANE, cols), o_ref.dtype)
                continue
            if len(runs) == 1 and runs[0][0] == 0 and runs[0][2] == LANE and runs[0][1] % BF16_ROWS == 0:
                o_ref[rows, :] = s_ref[pl.ds(runs[0][1], LANE), :]
                continue
            acc = None
            for o0, s0, n in runs:
                w0 = s0 // BF16_ROWS * BF16_ROWS
                width = -(-(s0 - w0 + n) // LANE) * LANE
                w0 = min(w0, n_src - width)
                i = lax.broadcasted_iota(jnp.int32, (LANE, width), 0)
                j = lax.broadcasted_iota(jnp.int32, (LANE, width), 1)
                pick = ((j - i == s0 - w0 - o0) & (i >= o0) & (i < o0 + n)).astype(src.dtype)
                part = _dot(pick, s_ref[pl.ds(w0, width), :])
                acc = part if acc is None else acc + part
            o_ref[rows, :] = acc.astype(o_ref.dtype)

    return _pcall(body, name=name, out_shape=jax.ShapeDtypeStruct((len(src_of_dst), cols), src.dtype))(src)


def _local_step(x, target, meta, attn_g, fox_b, ret_g, ffn_g, conv_w8, conv_b, final_g,
                first_weight, late_weights, ffn_grads_ready, out_grad_ready, in_grad_ready):
    seq, d = x.shape
    t = seq + PREFIX
    tm = TOK_TILE
    nq = t // tm
    fox_b128 = jnp.pad(fox_b, ((0, 0), (0, LANE - FOX_HEADS)))

    h0, n1 = _prep_norm(x, meta, attn_g, "prep_norm")
    w_in_t = first_weight(n1)
    proj = _mm_simple(n1, w_in_t, mode="nt", tm=tm, tn=IN_PAD, tk=d, out_dtype=F32, name="mm_in")
    cos, sin = _rope_tables(t)
    o_pre, mixed, states = _ret_fwd(proj, cos, sin, ret_g, "ret_fwd")
    c = _forget_cumsum(proj, fox_b128, "forget_cumsum")
    qa, ka, va, qt, vt = _fox_prep(proj, c, "fox_prep")
    by_block = lambda a: a.reshape(FOX_HEADS, nq, tm, LANE)
    mixed, o_fox, lse = _fox_fwd(qt, by_block(ka), vt, mixed, "fox_fwd")
    w_out, w_up_t, w_down = late_weights(o_fox)
    tile = pl.BlockSpec((tm, d), lambda i: (i, 0))
    row_vec = pl.BlockSpec((1, d), lambda i: (0, 0))
    resident = lambda shape: pl.BlockSpec(shape, lambda i: (0,) * len(shape), pipeline_mode=pl.Buffered(1))
    acts = lambda dtype: jax.ShapeDtypeStruct((t, d), dtype)
    vec = jax.ShapeDtypeStruct((1, d), F32)

    def residual_and_norm(i, acc, ins, outs):
        h = acc + ins[0][...]
        outs[0][...] = h
        outs[1][...] = (h * lax.rsqrt(jnp.mean(h * h, axis=-1, keepdims=True) + EPS) * ins[1][...]).astype(BF16)

    h1, n2 = _matmul_rows([mixed], [tile], [w_out], [resident((d, d))], [h0, ffn_g], [tile, row_vec],
                          [tile, tile], [acts(F32), acts(BF16)], residual_and_norm, mode="nn", steps=nq, name="mm_out_norm")
    nf = D_FF // 1408
    up, g = _up_conv_fwd(n2, w_up_t, conv_w8, conv_b, "up_conv_fwd")

    def residual_loss_bwd(i, acc, ins, outs):
        loss_ref, dh_ref, dhb_ref, gg_ref = outs
        part, dh, gg = _loss_tile(i, acc + ins[0][...], jnp.concatenate([ins[1][...], ins[2][...], ins[3][...]], axis=0),
                                  ins[4][...])
        _accumulate(loss_ref, i, jnp.broadcast_to(part, loss_ref.shape))
        dh_ref[...] = dh
        dhb_ref[...] = dh.astype(BF16)
        _accumulate(gg_ref, i, gg)

    loss_tile, dh2, dh2_b, g_final = _matmul_rows(
        [g], [pl.BlockSpec((tm, D_FF), lambda i: (i, 0))], [w_down], [resident((D_FF, d))],
        [h1, target, target, target, final_g], [tile] + _shifted_row_specs(d) + [row_vec],
        [pl.BlockSpec((8, LANE), lambda i: (0, 0)), tile, tile, row_vec],
        [jax.ShapeDtypeStruct((8, LANE), F32), acts(F32), acts(BF16), vec], residual_loss_bwd,
        mode="nn", steps=nq, name="mm_down_loss")

    tkw = 2112 if t % 2112 == 0 else tm
    gw_down = _mm_simple(g, dh2_b, mode="tn", tm=1408, tn=d, tk=tkw, out_dtype=BF16, name="mm_gw_down")
    dup, g_conv_w8, g_conv_b = _dg_conv_bwd(up, conv_w8, conv_b, dh2_b, w_down, "dg_conv_bwd")

    half = lambda p: pl.BlockSpec((None, tm, D_FF), lambda i: (p, i, 0))
    half_w = lambda p: pl.BlockSpec((None, D_FF, d), lambda i: (p, 0, 0), pipeline_mode=pl.Buffered(1))
    gw_up_t = _matmul(
        dup, n2, mode="tn", grid=(2 * nf, 1, t // tkw),
        a_spec=pl.BlockSpec((None, tkw, 1408), lambda i, j, k: (i // nf, k, i % nf)),
        b_spec=pl.BlockSpec((tkw, d), lambda i, j, k: (k, 0)),
        o_spec=pl.BlockSpec((1408, d), lambda i, j, k: (i, 0)),
        out_shape=jax.ShapeDtypeStruct((2 * D_FF, d), BF16), name="mm_gw_up")
    def norm_bwd_and_mixer_grad(i, acc, ins, outs):
        dh, gg = _rms_bwd_tile(acc, ins[0][...], ins[1][...], ins[2][...])
        outs[0][...] = dh
        _accumulate(outs[1], i, gg)
        outs[2][...] = _dot(dh.astype(BF16), ins[3][...], NT)

    dh1, g_ffn, dmixed = _matmul_rows(
        [dup, dup], [half(0), half(1)], [w_up_t, w_up_t], [half_w(0), half_w(1)],
        [h1, ffn_g, dh2, w_out], [tile, row_vec, tile, resident((d, d))], [tile, row_vec, tile],
        [acts(F32), vec, acts(F32)], norm_bwd_and_mixer_grad,
        mode="nn", steps=nq, name="mm_dn2_norm_bwd", after=ffn_grads_ready(gw_down, gw_up_t))
    gw_out = _mm_simple(mixed, dh1, mode="tn", tm=d, tn=d, tk=tkw, out_dtype=BF16, name="mm_gw_out")
    dproj, g_ret = _ret_bwd(proj, cos, sin, ret_g + out_grad_ready(gw_out), dmixed, o_pre, states, "ret_bwd")
    qab, doa = _fox_prep_bwd(dmixed, o_fox, lse, qa, "fox_prep_bwd")
    dproj, drs, dcs = _fox_bwd(by_block(qab), by_block(doa), by_block(ka), by_block(va), dproj, "fox_bwd")
    dproj, g_fox_b = _forget_cumsum_bwd(proj, fox_b128, drs, dcs, dproj, "forget_cumsum_bwd")
    gw_in_t = _mm_simple(dproj, n1, mode="tn", tm=640, tn=d, tk=t, out_dtype=BF16, name="mm_gw_in")
    sent = in_grad_ready(gw_in_t)
    def input_grads(i, acc, ins, outs):
        gx_ref, gmeta_ref, gg_ref, buf_ref, sems = outs
        dh, gg = _rms_bwd_tile(acc, ins[0][...], ins[1][...], ins[2][...])
        _accumulate(gg_ref, i, gg)
        slot = i % 2

        def first_copy():
            return pltpu.make_async_copy(buf_ref.at[0, pl.ds(PREFIX, tm - PREFIX)], gx_ref.at[pl.ds(0, tm - PREFIX)],
                                         sems.at[0])

        def tile_copy(tile, buf_slot):
            rows = pl.ds(pl.multiple_of(tile * tm - PREFIX, PREFIX), tm)
            return pltpu.make_async_copy(buf_ref.at[buf_slot], gx_ref.at[rows], sems.at[buf_slot])

        @pl.when(i == 1)
        def _():
            first_copy().wait()

        @pl.when(i >= 2)
        def _():
            tile_copy(i - 1, 1 - slot).wait()

        buf_ref[slot] = dh

        @pl.when(i == 0)
        def _():
            gmeta_ref[...] = dh[N_PAD:PREFIX, :]
            first_copy().start()

        @pl.when(i > 0)
        def _():
            tile_copy(i, slot).start()

        @pl.when(i == nq - 1)
        def _():
            tile_copy(i, slot).wait()

    grad_x, g_meta, g_attn = _matmul_rows(
        [dproj], [pl.BlockSpec((tm, IN_PAD), lambda i: (i, 0))], [w_in_t], [resident((IN_PAD, d))],
        [h0, attn_g, dh1], [tile, row_vec, tile],
        [pl.BlockSpec(memory_space=pl.ANY), pl.BlockSpec((N_META, d), lambda i: (0, 0)), row_vec],
        [jax.ShapeDtypeStruct((seq, d), F32), jax.ShapeDtypeStruct((N_META, d), F32), vec], input_grads,
        mode="nn", steps=nq, name="mm_dn1_norm_bwd", after=sent,
        scratch=[pltpu.VMEM((2, tm, d), F32), pltpu.SemaphoreType.DMA((2,))])

    grads = dict(meta=g_meta, attn_g=g_attn, fox_b=g_fox_b, ret_g=g_ret,
                 ffn_g=g_ffn, conv_w=g_conv_w8, conv_b=g_conv_b, final_g=g_final)
    return loss_tile, grad_x, grads


def kernel(x, meta_tokens, attn_norm_g, w_in, fox_forget_b, ret_norm_g, w_out, ffn_norm_g, w_up, conv_w, conv_b, w_down, final_norm_g, loss_target, m_meta_tokens, m_attn_norm_g, m_w_in, m_fox_forget_b, m_ret_norm_g, m_w_out, m_ffn_norm_g, m_w_up, m_conv_w, m_conv_b, m_w_down, m_final_norm_g, v_meta_tokens, v_attn_norm_g, v_w_in, v_fox_forget_b, v_ret_norm_g, v_w_out, v_ffn_norm_g, v_w_up, v_conv_w, v_conv_b, v_w_down, v_final_norm_g):
    d = D_MODEL
    me = 4 * lax.axis_index("x") + 2 * lax.axis_index("y") + lax.axis_index("c")
    in_blk, in_blk_pad = IN_BLOCK, IN_BLOCK_PAD
    up_blk = 2 * D_FF // N_DEV
    down_blk = D_FF // N_DEV
    cw_blk = D_FF // N_DEV

    w_in_loc = jnp.pad(w_in[0].T.astype(BF16), ((0, in_blk_pad - in_blk), (0, 0)))
    cw_loc = jnp.pad(conv_w[0], ((0, 5), (0, 384 - cw_blk)))
    g_meta, g_cw = _exchange([meta_tokens, cw_loc], ["gather"] * 2, "gather_small")
    first = _exchange_start([w_in_loc], ["gather"], "gather_in_start", after=g_meta, relations=SAME_CORE_AND_SIBLING)
    rest_loc = [(w_out[0] + first[-1][0:1, 0:1]).astype(BF16), w_up[0].T.astype(BF16), w_down[0].astype(BF16)]
    rest = _exchange_start(rest_loc, ["gather"] * 3, "gather_rest_start")
    meta_f = g_meta.transpose(1, 0, 2).reshape(N_META, d)
    conv_w8 = jnp.pad(g_cw[:, :3, :cw_blk].transpose(1, 0, 2).reshape(3, D_FF), ((0, 5), (0, 0)))
    pending = {}

    def first_weight(after):
        own_in, landed = _exchange_wait(first, ["gather"], after, "gather_in_wait", relations=SAME_CORE_AND_SIBLING,
                                        with_sources=True)
        onward = _exchange_start(own_in, ["forward"], "gather_in_forward_start", relations=OTHER_CHIPS, lands=landed,
                                 own=False)
        (g_in,) = _exchange_wait(onward, ["forward"], onward[-1], "gather_in_forward_wait", own=False,
                                 relations=OTHER_CHIPS)
        return _move_rows(g_in.reshape(IN_PAD, d), _slot_row_of_internal(), "w_in_rows")

    def in_grad_ready(gw_in_t):
        blocks = _move_rows(gw_in_t, _internal_row_of_slot(), "gw_in_rows").reshape(N_DEV, in_blk_pad, d)
        pending["in"] = _exchange_start([blocks], ["scatter"], "grads_in_start")
        return pending["in"][-1][0:1, 0:1]

    def late_weights(after):
        g_out, g_up, g_down = _exchange_wait(rest, ["gather"] * 3, after, "gather_rest_wait")
        return g_out.reshape(d, d), g_up.reshape(2, D_FF, d), g_down.reshape(D_FF, d)

    def ffn_grads_ready(gw_down, gw_up_t):
        pending["ffn_blocks"] = [gw_down.reshape(N_DEV, down_blk, d), gw_up_t.reshape(N_DEV, up_blk, d)]
        return gw_down[0:1, 0:1]

    def out_grad_ready(gw_out):
        blocks = pending["ffn_blocks"] + [gw_out.reshape(N_DEV, d // N_DEV, d)]
        pending["mid"] = _exchange_start(blocks, ["scatter"] * 3, "grads_mid_start")
        return pending["mid"][-1][0:1, 0:1]

    loss_tile, grad_x, gr = _local_step(
        x[0], loss_target[0], meta_f, attn_norm_g + rest[-1][0:1, 0:1], fox_forget_b, ret_norm_g, ffn_norm_g,
        conv_w8, conv_b, final_norm_g.reshape(1, d), first_weight, late_weights, ffn_grads_ready, out_grad_ready,
        in_grad_ready)

    small = [loss_tile, gr["attn_g"], gr["fox_b"], gr["ret_g"], gr["ffn_g"], gr["conv_b"], gr["final_g"],
             gr["meta"], gr["conv_w"]]
    small_kinds = ["gather"] * len(small)
    small_started = _exchange_start(small, small_kinds, "grads_small_start", own=False)

    r_down, r_up, r_out = _exchange_wait(pending["mid"], ["scatter"] * 3, small_started[-1], "grads_mid_wait")
    g_w_out = _sum_slots(r_out, "sum_w_out", d // N_DEV)
    g_w_up_t = _sum_slots(r_up, "sum_w_up", up_blk)
    g_w_down = _sum_slots(r_down, "sum_w_down", down_blk)
    as_t = lambda a: a[0].T
    from_t = lambda a: a.T[None]
    d_w_out, m_w_out_n, v_w_out_n = [a[None] for a in _adamw(w_out[0], g_w_out, m_w_out[0], v_w_out[0], "adamw_w_out", 128)]
    up_t = _adamw(as_t(w_up), g_w_up_t, as_t(m_w_up), as_t(v_w_up), "adamw_w_up", up_blk // 2)
    d_w_up, m_w_up_n, v_w_up_n = [from_t(a) for a in up_t]
    d_w_down, m_w_down_n, v_w_down_n = [a[None] for a in _adamw(w_down[0], g_w_down, m_w_down[0], v_w_down[0],
                                                                "adamw_w_down", down_blk)]

    own_small, r_small = _exchange_wait(small_started, small_kinds, up_t[0], "grads_small_wait", own=False,
                                        with_sources=True)
    (loss_all, g_attn, g_fox_b128, g_ret, g_ffn, g_conv_b, g_final, g_meta_full, g_cw_full) = _sum_slots_small(
        r_small, own_small, "sum_small")
    loss = loss_all[0, 0]
    g_fox_b = g_fox_b128[:, :FOX_HEADS]
    g_meta_loc = lax.dynamic_slice(g_meta_full, (0, me * (d // N_DEV)), (N_META, d // N_DEV))
    g_cw_loc = lax.dynamic_slice(g_cw_full, (0, me * cw_blk), (3, cw_blk))

    (r_in,) = _exchange_wait(pending["in"], ["scatter"], r_small[0], "grads_in_wait")
    g_w_in, d_w_in, m_w_in_n, v_w_in_n = [from_t(a) for a in _adamw_from_slots(
        as_t(w_in), r_in, as_t(m_w_in), as_t(v_w_in), "adamw_w_in")]
    g_w_in, g_w_up = g_w_in[0], g_w_up_t.T
    row = lambda a: a.reshape(1, d)
    sm_grads = [g_meta_loc, g_attn, g_fox_b, g_ret, g_ffn, g_cw_loc, g_conv_b, g_final]
    sm_w = [meta_tokens, attn_norm_g, fox_forget_b, ret_norm_g, ffn_norm_g, conv_w[0], conv_b, row(final_norm_g)]
    sm_m = [m_meta_tokens, m_attn_norm_g, m_fox_forget_b, m_ret_norm_g, m_ffn_norm_g, m_conv_w[0], m_conv_b,
            row(m_final_norm_g)]
    sm_v = [v_meta_tokens, v_attn_norm_g, v_fox_forget_b, v_ret_norm_g, v_ffn_norm_g, v_conv_w[0], v_conv_b,
            row(v_final_norm_g)]
    dl, ml, vl = [lst[:7] + [lst[7].reshape(d)] for lst in _adamw_small(sm_w, sm_grads, sm_m, sm_v, "adamw_small")]

    def by_weight(meta_, attn_, w_in_, fox_, ret_, w_out_, ffn_, w_up_, cw_, cb_, w_down_, final_):
        return (meta_, attn_, w_in_, fox_, ret_, w_out_, ffn_, w_up_, cw_[None], cb_, w_down_, final_)

    grads_out = by_weight(g_meta_loc, g_attn, g_w_in[None], g_fox_b, g_ret, g_w_out[None], g_ffn, g_w_up[None], g_cw_loc,
                          g_conv_b, g_w_down[None], g_final.reshape(d))
    delta_out = by_weight(dl[0], dl[1], d_w_in, dl[2], dl[3], d_w_out, dl[4], d_w_up, dl[5], dl[6], d_w_down, dl[7])
    m_out = by_weight(ml[0], ml[1], m_w_in_n, ml[2], ml[3], m_w_out_n, ml[4], m_w_up_n, ml[5], ml[6], m_w_down_n, ml[7])
    v_out = by_weight(vl[0], vl[1], v_w_in_n, vl[2], vl[3], v_w_out_n, vl[4], v_w_up_n, vl[5], vl[6], v_w_down_n, vl[7])
    return (loss, grad_x[None]) + grads_out + delta_out + m_out + v_out
```

```python
import numpy as np
import jax
import jax.numpy as jnp
from jax import lax
from jax.experimental import pallas as pl
from jax.experimental.pallas import tpu as pltpu

F32 = jnp.float32
BF16 = jnp.bfloat16

D_MODEL = 1024
N_META = 16
N_PAD = 112
PREFIX = 128
RET_HEADS = 4
RET_DK = 64
RET_DV = 128
FOX_HEADS = 8
FOX_DH = 64
D_FF = 2816
ROPE_BASE = 10000.0
EPS = 1e-6
NEG = -1e30
RET_QK = RET_HEADS * RET_DK
RET_V = RET_HEADS * RET_DV
FOX_W = FOX_HEADS * FOX_DH
IN_WIDTH = 2 * RET_QK + 2 * RET_V + 3 * FOX_W + FOX_HEADS
IN_PAD = 3200
FF_COL_BLOCK = (IN_WIDTH - FOX_HEADS) // 128
QK_SCALE = 0.125

ADAM_LR = 0.001
ADAM_B1 = 0.9
ADAM_B2 = 0.999
ADAM_EPS = 1e-08
ADAM_WD = 0.01
ADAM_STEP = 10

N_DEV = 8
LANE = 128
ROW_TILE = 128
TOK_TILE = 384

NN = (((1,), (0,)), ((), ()))
NT = (((1,), (1,)), ((), ()))
TN = (((0,), (0,)), ((), ()))


def _pcall(body, **kw):
    return pl.pallas_call(body, **kw)


def _params(*sem):
    return pltpu.CompilerParams(dimension_semantics=sem)


def _dot(a, b, dims=NN):
    return lax.dot_general(a, b, dims, preferred_element_type=F32)


def _sigmoid(x):
    return 0.5 * jnp.tanh(0.5 * x) + 0.5


def _matmul(a, b, *, mode, grid, a_spec, b_spec, o_spec, out_shape, name, add=None, add_spec=None, after=None):
    dims = {"nn": NN, "nt": NT, "tn": TN}[mode]
    nk = grid[2]
    has_add = add is not None
    a_list, b_list = (list(a), list(b)) if isinstance(a, (list, tuple)) else ([a], [b])
    a_specs, b_specs = (list(a_spec), list(b_spec)) if isinstance(a_spec, (list, tuple)) else ([a_spec], [b_spec])
    nt = len(a_list)
    n_in = 2 * nt + int(has_add) + int(after is not None)

    def body(*refs):
        a_refs, b_refs = refs[:nt], refs[nt:2 * nt]
        add_ref = refs[2 * nt] if has_add else None
        o_ref = refs[n_in]
        part = _dot(a_refs[0][...].astype(BF16), b_refs[0][...].astype(BF16), dims)
        for ar, br in zip(a_refs[1:], b_refs[1:]):
            part = part + _dot(ar[...].astype(BF16), br[...].astype(BF16), dims)

        def finish(acc):
            if has_add:
                acc = acc + add_ref[...]
            o_ref[...] = acc.astype(o_ref.dtype)

        if nk == 1:
            finish(part)
        else:
            acc_ref = refs[-1]
            k = pl.program_id(2)

            @pl.when(k == 0)
            def _():
                acc_ref[...] = part

            @pl.when(k > 0)
            def _():
                acc_ref[...] += part

            @pl.when(k == nk - 1)
            def _():
                finish(acc_ref[...])

    in_specs = a_specs + b_specs + ([add_spec] if has_add else [])
    args = tuple(a_list) + tuple(b_list) + ((add,) if has_add else ())
    if after is not None:
        in_specs, args = in_specs + [pl.BlockSpec(memory_space=pl.ANY)], args + (after,)
    scratch = [] if nk == 1 else [pltpu.VMEM(tuple(d for d in o_spec.block_shape if d is not None), F32)]
    return _pcall(
        body, name=name, grid=grid, in_specs=in_specs, out_specs=o_spec, out_shape=out_shape,
        scratch_shapes=scratch, compiler_params=_params("parallel", "parallel", "arbitrary"),
    )(*args)


def _mm_simple(a, b, *, mode, tm, tn, tk, out_dtype, name, add=None, after=None):
    if mode == "tn":
        K, M = a.shape
    else:
        M, K = a.shape
    N = b.shape[0] if mode == "nt" else b.shape[1]
    grid = (M // tm, N // tn, K // tk)
    resident = dict(pipeline_mode=pl.Buffered(1)) if (tn == N and tk == K) else {}
    a_spec = pl.BlockSpec((tk, tm), lambda i, j, k: (k, i)) if mode == "tn" else pl.BlockSpec((tm, tk), lambda i, j, k: (i, k))
    b_spec = (pl.BlockSpec((tn, tk), lambda i, j, k: (j, k), **resident) if mode == "nt"
              else pl.BlockSpec((tk, tn), lambda i, j, k: (k, j), **resident))
    o_spec = pl.BlockSpec((tm, tn), lambda i, j, k: (i, j))
    return _matmul(a, b, mode=mode, grid=grid, a_spec=a_spec, b_spec=b_spec, o_spec=o_spec,
                   out_shape=jax.ShapeDtypeStruct((M, N), out_dtype), name=name, add=add,
                   add_spec=o_spec if add is not None else None, after=after)


def _matmul_rows(a_list, a_specs, b_list, b_specs, extras, extra_specs, out_specs, out_shape, epilogue, *,
                 mode, steps, name, after=None, scratch=()):
    dims = {"nn": NN, "nt": NT}[mode]
    nt, ne = len(a_list), len(extras)
    n_in = 2 * nt + ne + int(after is not None)

    def body(*refs):
        acc = _dot(refs[0][...].astype(BF16), refs[nt][...].astype(BF16), dims)
        for k in range(1, nt):
            acc = acc + _dot(refs[k][...].astype(BF16), refs[nt + k][...].astype(BF16), dims)
        epilogue(pl.program_id(0), acc, refs[2 * nt:2 * nt + ne], refs[n_in:])

    in_specs = list(a_specs) + list(b_specs) + list(extra_specs)
    args = tuple(a_list) + tuple(b_list) + tuple(extras)
    if after is not None:
        in_specs, args = in_specs + [pl.BlockSpec(memory_space=pl.ANY)], args + (after,)
    return _pcall(body, name=name, grid=(steps,), in_specs=in_specs, out_specs=out_specs, out_shape=out_shape,
                  scratch_shapes=list(scratch), compiler_params=_params("arbitrary"))(*args)


def _rms_bwd_tile(dy, x, gain, dres):
    r = lax.rsqrt(jnp.mean(x * x, axis=-1, keepdims=True) + EPS)
    xhat = x * r
    u = dy * gain
    return dres + r * (u - xhat * jnp.mean(u * xhat, axis=-1, keepdims=True)), jnp.sum(dy * xhat, axis=0, keepdims=True)


def _loss_tile(i, x, tgt, gain):
    d = x.shape[-1]
    r = lax.rsqrt(jnp.mean(x * x, axis=-1, keepdims=True) + EPS)
    xhat = x * r
    counted = (i * TOK_TILE + lax.broadcasted_iota(jnp.int32, (TOK_TILE, 1), 0)) >= PREFIX
    err = jnp.where(counted, xhat * gain - tgt, 0.0)
    dy = err * (1.0 / d)
    u = dy * gain
    dh = r * (u - xhat * jnp.mean(u * xhat, axis=-1, keepdims=True))
    return 0.5 * jnp.sum(jnp.mean(err * err, axis=-1, keepdims=True)), dh, jnp.sum(dy * xhat, axis=0, keepdims=True)


def _accumulate(ref, i, part):
    @pl.when(i == 0)
    def _():
        ref[...] = part

    @pl.when(i > 0)
    def _():
        ref[...] += part


def _prep_norm(x, meta, gain, name):
    seq, d = x.shape
    t = seq + PREFIX

    def body(xa_ref, xb_ref, xc_ref, meta_ref, g_ref, h_ref, n_ref):
        i = pl.program_id(0)

        @pl.when(i == 0)
        def _():
            h_ref[0:N_PAD, :] = jnp.zeros((N_PAD, d), F32)
            h_ref[N_PAD:ROW_TILE, :] = meta_ref[...]

        @pl.when(i > 0)
        def _():
            h_ref[0:ROW_TILE, :] = xa_ref[...]

        h_ref[ROW_TILE:2 * ROW_TILE, :] = xb_ref[...]
        h_ref[2 * ROW_TILE:3 * ROW_TILE, :] = xc_ref[...]
        h = h_ref[...]
        r = lax.rsqrt(jnp.mean(h * h, axis=-1, keepdims=True) + EPS)
        n_ref[...] = (h * r * g_ref[...]).astype(BF16)

    return _pcall(
        body, name=name, grid=(t // TOK_TILE,),
        in_specs=_shifted_row_specs(d) + [pl.BlockSpec((N_META, d), lambda i: (0, 0)), pl.BlockSpec((1, d), lambda i: (0, 0))],
        out_specs=[pl.BlockSpec((TOK_TILE, d), lambda i: (i, 0)), pl.BlockSpec((TOK_TILE, d), lambda i: (i, 0))],
        out_shape=[jax.ShapeDtypeStruct((t, d), F32), jax.ShapeDtypeStruct((t, d), BF16)],
        compiler_params=_params("parallel"),
    )(x, x, x, meta, gain)


def _shifted_row_specs(d):
    blocks_per_tile = TOK_TILE // ROW_TILE
    return [pl.BlockSpec((ROW_TILE, d), lambda i, r=r: (jnp.maximum(blocks_per_tile * i + r, 0), 0)) for r in (-1, 0, 1)]


def _ret_consts(bk):
    gam = 1.0 - 2.0 ** (-5.0 - np.arange(RET_HEADS))
    n = np.arange(bk)
    same_or_earlier_chunk = (n[None, :] // 64) <= (n[:, None] // 64)
    w = gam[:, None, None] ** np.abs(n[:, None] - n[None, :])[None] * same_or_earlier_chunk[None]
    wq = gam[:, None] ** (n[None, :] + 1.0)
    wk = gam[:, None] ** (bk - 1.0 - n[None, :])
    mask = (np.arange(RET_QK)[None, :] // RET_DK) == np.arange(RET_HEADS)[:, None]
    return (jnp.asarray(w, F32), jnp.asarray(wq[:, :, None], F32), jnp.asarray(wk[:, :, None], F32),
            jnp.asarray(mask[:, None, :], F32), [float(g ** bk) for g in gam])


def _rope_tables(t):
    half = RET_DK // 2
    inv = 1.0 / (ROPE_BASE ** (jnp.arange(half, dtype=F32) / half))
    ang = jnp.arange(t).astype(F32)[:, None] * inv[None, :]
    cos, sin = jnp.cos(ang), jnp.sin(ang)
    return (jnp.tile(jnp.concatenate([cos, cos], axis=1), (1, RET_HEADS)),
            jnp.tile(jnp.concatenate([-sin, sin], axis=1), (1, RET_HEADS)))


def _swap_halves(x):
    outs = []
    for s in range(x.shape[1] // LANE):
        xs = x[:, LANE * s:LANE * (s + 1)]
        lane = lax.broadcasted_iota(jnp.int32, xs.shape, 1)
        outs.append(jnp.where((lane & 32) == 0, pltpu.roll(xs, LANE - 32, axis=1), pltpu.roll(xs, 32, axis=1)))
    return outs[0] if len(outs) == 1 else jnp.concatenate(outs, axis=1)


def _rope(x, cos, sin_signed):
    return x * cos + _swap_halves(x) * sin_signed


def _rope_t(dx, cos, sin_signed):
    return dx * cos + _swap_halves(dx * sin_signed)


def _ret_fwd(proj, cos, sin, gain, name):
    t = proj.shape[0]
    bk = TOK_TILE
    nb = t // bk
    w, wq, wk, mask, g_blk = _ret_consts(bk)

    def body(q_ref, k_ref, v_ref, rg_ref, cos_ref, sin_ref, w_ref, wq_ref, wk_ref, mask_ref, gain_ref,
             opre_ref, og_ref, st_ref, r_ref):
        i = pl.program_id(0)

        @pl.when(i == 0)
        def _():
            r_ref[...] = jnp.zeros_like(r_ref)

        c, s = cos_ref[...], sin_ref[...]
        valid = ((i * bk + lax.broadcasted_iota(jnp.int32, (bk, 1), 0)) >= N_PAD).astype(F32)
        qr = _rope(q_ref[...], c, s)
        kr = _rope(k_ref[...], c, s) * QK_SCALE * valid
        kb = kr.astype(BF16)
        for h in range(RET_HEADS):
            hm = mask_ref[h]
            cols = slice(RET_DV * h, RET_DV * (h + 1))
            vh = v_ref[:, cols].astype(BF16)
            r_prev = r_ref[h]
            st_ref[0, h] = r_prev
            sm = _dot((qr * hm).astype(BF16), kb, NT) * w_ref[h]
            o = _dot(sm.astype(BF16), vh) + _dot((qr * (hm * wq_ref[h])).astype(BF16), r_prev.astype(BF16))
            r_ref[h] = g_blk[h] * r_prev + _dot((kr * wk_ref[h]).astype(BF16), vh, TN)
            opre_ref[:, cols] = o
            rstd = lax.rsqrt(jnp.mean(o * o, axis=-1, keepdims=True) + EPS)
            rg = rg_ref[:, cols]
            og_ref[:, cols] = (o * rstd * gain_ref[:, cols] * (rg * _sigmoid(rg))).astype(BF16)

    full = lambda shape: pl.BlockSpec(shape, lambda i: (0,) * len(shape))
    return _pcall(
        body, name=name, grid=(nb,),
        in_specs=[pl.BlockSpec((bk, RET_QK), lambda i: (i, 0)), pl.BlockSpec((bk, RET_QK), lambda i: (i, 1)),
                  pl.BlockSpec((bk, RET_V), lambda i: (i, 1)), pl.BlockSpec((bk, RET_V), lambda i: (i, 2)),
                  pl.BlockSpec((bk, RET_QK), lambda i: (i, 0)), pl.BlockSpec((bk, RET_QK), lambda i: (i, 0)),
                  full((RET_HEADS, bk, bk)), full((RET_HEADS, bk, 1)), full((RET_HEADS, bk, 1)),
                  full((RET_HEADS, 1, RET_QK)), full((1, RET_V))],
        out_specs=[pl.BlockSpec((bk, RET_V), lambda i: (i, 0)), pl.BlockSpec((bk, RET_V), lambda i: (i, 0)),
                   pl.BlockSpec((1, RET_HEADS, RET_QK, RET_DV), lambda i: (i, 0, 0, 0))],
        out_shape=[jax.ShapeDtypeStruct((t, RET_V), F32), jax.ShapeDtypeStruct((t, RET_V + FOX_W), BF16),
                   jax.ShapeDtypeStruct((nb, RET_HEADS, RET_QK, RET_DV), F32)],
        scratch_shapes=[pltpu.VMEM((RET_HEADS, RET_QK, RET_DV), F32)],
        compiler_params=_params("arbitrary"),
    )(proj, proj, proj, proj, cos, sin, w, wq, wk, mask, gain)


def _ret_bwd(proj, cos, sin, gain, dmixed, opre, states, name):
    t = proj.shape[0]
    bk = TOK_TILE
    nb = t // bk
    w, wq, wk, mask, g_blk = _ret_consts(bk)
    v0, g0 = 2 * RET_QK, 2 * RET_QK + RET_V

    def body(q_ref, k_ref, v_ref, rg_ref, cos_ref, sin_ref, w_ref, wq_ref, wk_ref, mask_ref, gain_ref,
             dog_ref, opre_ref, st_ref, dp_ref, gg_ref, dr_ref):
        step = pl.program_id(0)
        i = nb - 1 - step

        @pl.when(step == 0)
        def _():
            dr_ref[...] = jnp.zeros_like(dr_ref)
            gg_ref[...] = jnp.zeros_like(gg_ref)

        c, s = cos_ref[...], sin_ref[...]
        valid = ((i * bk + lax.broadcasted_iota(jnp.int32, (bk, 1), 0)) >= N_PAD).astype(F32)
        qr = _rope(q_ref[...], c, s)
        kr = _rope(k_ref[...], c, s) * QK_SCALE * valid
        kb = kr.astype(BF16)
        dqr = jnp.zeros((bk, RET_QK), F32)
        dkr = jnp.zeros((bk, RET_QK), F32)
        for h in range(RET_HEADS):
            hm = mask_ref[h]
            cols = slice(RET_DV * h, RET_DV * (h + 1))
            vh = v_ref[:, cols].astype(BF16)
            o = opre_ref[:, cols]
            rstd = lax.rsqrt(jnp.mean(o * o, axis=-1, keepdims=True) + EPS)
            xhat = o * rstd
            rg = rg_ref[:, cols]
            sg = _sigmoid(rg)
            gate = rg * sg
            gn = gain_ref[:, cols]
            dog = dog_ref[:, cols]
            dp_ref[:, g0 + RET_DV * h:g0 + RET_DV * (h + 1)] = (
                dog * xhat * gn * (sg * (1.0 + rg * (1.0 - sg)))).astype(BF16)
            gg_ref[:, cols] += jnp.sum(dog * xhat * gate, axis=0, keepdims=True)
            dxh = dog * gn * gate
            do = (rstd * (dxh - xhat * jnp.mean(dxh * xhat, axis=-1, keepdims=True))).astype(BF16)
            qm = (qr * hm).astype(BF16)
            qw = (qr * (hm * wq_ref[h])).astype(BF16)
            kw = (kr * wk_ref[h]).astype(BF16)
            wh = w_ref[h]
            sm = (_dot(qm, kb, NT) * wh).astype(BF16)
            ds = (_dot(do, vh, NT) * wh).astype(BF16)
            dr = dr_ref[h]
            drb = dr.astype(BF16)
            dp_ref[:, v0 + RET_DV * h:v0 + RET_DV * (h + 1)] = (_dot(sm, do, TN) + _dot(kw, drb)).astype(BF16)
            dqr = dqr + _dot(ds, kb) * hm + _dot(do, st_ref[0, h].astype(BF16), NT) * (hm * wq_ref[h])
            dkr = dkr + _dot(ds, qm, TN) + _dot(vh, drb, NT) * wk_ref[h]
            dr_ref[h] = g_blk[h] * dr + _dot(qw, do, TN)
        dp_ref[:, 0:RET_QK] = _rope_t(dqr, c, s).astype(BF16)
        dp_ref[:, RET_QK:2 * RET_QK] = _rope_t(dkr * (QK_SCALE * valid), c, s).astype(BF16)

    full = lambda shape: pl.BlockSpec(shape, lambda i: (0,) * len(shape))
    rev = lambda col: (lambda i: (nb - 1 - i, col))
    return _pcall(
        body, name=name, grid=(nb,),
        in_specs=[pl.BlockSpec((bk, RET_QK), rev(0)), pl.BlockSpec((bk, RET_QK), rev(1)),
                  pl.BlockSpec((bk, RET_V), rev(1)), pl.BlockSpec((bk, RET_V), rev(2)),
                  pl.BlockSpec((bk, RET_QK), rev(0)), pl.BlockSpec((bk, RET_QK), rev(0)),
                  full((RET_HEADS, bk, bk)), full((RET_HEADS, bk, 1)), full((RET_HEADS, bk, 1)),
                  full((RET_HEADS, 1, RET_QK)), full((1, RET_V)),
                  pl.BlockSpec((bk, RET_V), rev(0)), pl.BlockSpec((bk, RET_V), rev(0)),
                  pl.BlockSpec((1, RET_HEADS, RET_QK, RET_DV), lambda i: (nb - 1 - i, 0, 0, 0))],
        out_specs=[pl.BlockSpec((bk, g0 + RET_V), rev(0)), pl.BlockSpec((1, RET_V), lambda i: (0, 0))],
        out_shape=[jax.ShapeDtypeStruct((t, IN_PAD), BF16), jax.ShapeDtypeStruct((1, RET_V), F32)],
        scratch_shapes=[pltpu.VMEM((RET_HEADS, RET_QK, RET_DV), F32)],
        compiler_params=_params("arbitrary"),
    )(proj, proj, proj, proj, cos, sin, w, wq, wk, mask, gain, dmixed, opre, states)


def _forget_cumsum(proj, bias, name):
    t = proj.shape[0]
    rt = TOK_TILE
    nb = t // rt
    tril = jnp.asarray(np.tril(np.ones((rt, rt))), F32)

    def body(z_ref, b_ref, tril_ref, c_ref, carry_ref):
        i = pl.program_id(0)

        @pl.when(i == 0)
        def _():
            carry_ref[...] = jnp.zeros_like(carry_ref)

        z = z_ref[...] + b_ref[...]
        logf = jnp.minimum(z, 0.0) - jnp.log(1.0 + jnp.exp(-jnp.abs(z)))
        c = lax.dot_general(tril_ref[...], logf, NN, precision=lax.Precision.HIGHEST,
                            preferred_element_type=F32) + carry_ref[...]
        c_ref[...] = c
        carry_ref[...] = c[rt - 1:rt, :]

    return _pcall(
        body, name=name, grid=(nb,),
        in_specs=[pl.BlockSpec((rt, LANE), lambda i: (i, FF_COL_BLOCK)), pl.BlockSpec((1, LANE), lambda i: (0, 0)),
                  pl.BlockSpec((rt, rt), lambda i: (0, 0))],
        out_specs=pl.BlockSpec((rt, LANE), lambda i: (i, 0)),
        out_shape=jax.ShapeDtypeStruct((t, LANE), F32),
        scratch_shapes=[pltpu.VMEM((1, LANE), F32)],
        compiler_params=_params("arbitrary"),
    )(proj, bias, tril)


def _forget_cumsum_bwd(proj, bias, drs, dcs, dproj, name):
    t = proj.shape[0]
    rt = TOK_TILE
    nb = t // rt
    triu = jnp.asarray(np.triu(np.ones((rt, rt))), F32)

    def body(z_ref, b_ref, triu_ref, drs_ref, dcs_ref, dproj_in, dz_ref, gb_ref, carry_ref):
        step = pl.program_id(0)

        @pl.when(step == 0)
        def _():
            carry_ref[...] = jnp.zeros_like(carry_ref)
            gb_ref[...] = jnp.zeros_like(gb_ref)

        dlogf = lax.dot_general(triu_ref[...], drs_ref[...] - dcs_ref[...], NN, precision=lax.Precision.HIGHEST,
                                preferred_element_type=F32) + carry_ref[...]
        carry_ref[...] = dlogf[0:1, :]
        z = z_ref[...] + b_ref[...]
        is_head = lax.broadcasted_iota(jnp.int32, (rt, LANE), 1) < FOX_HEADS
        dz = jnp.where(is_head, dlogf / (1.0 + jnp.exp(z)), 0.0)
        dz_ref[...] = dz.astype(BF16)
        gb_ref[...] += jnp.sum(dz, axis=0, keepdims=True)

    return _pcall(
        body, name=name, grid=(nb,),
        in_specs=[pl.BlockSpec((rt, LANE), lambda i: (nb - 1 - i, FF_COL_BLOCK)),
                  pl.BlockSpec((1, LANE), lambda i: (0, 0)),
                  pl.BlockSpec((rt, rt), lambda i: (0, 0)),
                  pl.BlockSpec((rt, LANE), lambda i: (nb - 1 - i, 0)),
                  pl.BlockSpec((rt, LANE), lambda i: (nb - 1 - i, 0)),
                  pl.BlockSpec(memory_space=pl.ANY)],
        out_specs=[pl.BlockSpec((rt, LANE), lambda i: (nb - 1 - i, FF_COL_BLOCK)),
                   pl.BlockSpec((1, LANE), lambda i: (0, 0))],
        out_shape=[jax.ShapeDtypeStruct(dproj.shape, BF16), jax.ShapeDtypeStruct((1, LANE), F32)],
        input_output_aliases={5: 0},
        scratch_shapes=[pltpu.VMEM((1, LANE), F32)],
        compiler_params=_params("arbitrary"),
    )(proj, bias, triu, drs, dcs, dproj)


FOX_PAIRS = FOX_HEADS // 2
L_ONE_Q = FOX_DH
L_ONE_K = FOX_DH + 3
L_LSE = FOX_DH + 4


def _split3(x):
    hi = x.astype(BF16).astype(F32)
    r = x - hi
    mid = r.astype(BF16).astype(F32)
    return hi, mid, r - mid


def _head_to_low(slab, e):
    return slab if e == 0 else pltpu.roll(slab, FOX_DH, axis=1)


def _pair(a, b, low):
    return jnp.where(low, a, pltpu.roll(b, FOX_DH, axis=1))


def _fox_prep(proj, c, name):
    t = proj.shape[0]
    tq = TOK_TILE

    def body(p_ref, c_ref, qa_ref, ka_ref, va_ref, qt_ref, vt_ref):
        i = pl.program_id(0)
        lane = lax.broadcasted_iota(jnp.int32, (tq, LANE), 1)
        low = lane < FOX_DH
        live = (i * tq + lax.broadcasted_iota(jnp.int32, (tq, 1), 0)) >= N_PAD
        q_tail = jnp.where(lane < L_ONE_Q + 3, 1.0, 0.0)
        k_ones = (lane >= L_ONE_K) & (lane < L_ONE_K + 4)
        v_tail = jnp.where(lane < FOX_DH + 2, 1.0, 0.0)
        bias_parts = _split3(jnp.where(live, -c_ref[...], NEG))
        for pair in range(FOX_PAIRS):
            base = 3 * LANE * pair
            for e in range(2):
                h = 2 * pair + e
                q = _head_to_low(p_ref[:, base:base + LANE], e)
                k = _head_to_low(p_ref[:, base + LANE:base + 2 * LANE], e)
                v = _head_to_low(p_ref[:, base + 2 * LANE:base + 3 * LANE], e)
                hi, mid, lo = [part[:, h:h + 1] for part in bias_parts]
                ka = jnp.where(low, k, jnp.where(k_ones, 1.0, 0.0))
                ka = jnp.where(lane == L_ONE_Q, hi, jnp.where(lane == L_ONE_Q + 1, mid, jnp.where(lane == L_ONE_Q + 2, lo, ka)))
                qa = jnp.where(low, q * QK_SCALE, q_tail)
                va = jnp.where(low, v, v_tail)
                qa_ref[h] = qa.astype(BF16)
                ka_ref[h] = ka.astype(BF16)
                va_ref[h] = va.astype(BF16)
                qt_ref[h] = qa.T.astype(BF16)
                vt_ref[h] = va.T.astype(BF16)

    out = jax.ShapeDtypeStruct((FOX_HEADS, t, LANE), BF16)
    out_t = jax.ShapeDtypeStruct((FOX_HEADS, t // tq, LANE, tq), BF16)
    ospec = pl.BlockSpec((FOX_HEADS, tq, LANE), lambda i: (0, i, 0))
    tspec = pl.BlockSpec((FOX_HEADS, None, LANE, tq), lambda i: (0, i, 0, 0))
    return _pcall(
        body, name=name, grid=(t // tq,),
        in_specs=[pl.BlockSpec((tq, 3 * FOX_W), lambda i: (i, 1)), pl.BlockSpec((tq, LANE), lambda i: (i, 0))],
        out_specs=[ospec, ospec, ospec, tspec, tspec], out_shape=[out, out, out, out_t, out_t],
        compiler_params=_params("parallel"),
    )(proj, c)


STEP_PAIRS = 2
STEP_HEADS = 2 * STEP_PAIRS
FOX_GROUPS = FOX_PAIRS // STEP_PAIRS
FWD_PAIRS = 4
FWD_HEADS = 2 * FWD_PAIRS
FWD_GROUPS = FOX_PAIRS // FWD_PAIRS


def _fox_fwd(qt, ka, vt, mixed, name):
    nh, nq, tq, _ = ka.shape
    t = nq * tq

    def body(qt_ref, ka_ref, vt_ref, mixed_in, mixed_ref, o_ref, lse_ref):
        i = pl.program_id(1)
        lane = lax.broadcasted_iota(jnp.int32, (tq, LANE), 1)
        key_le_query = lax.broadcasted_iota(jnp.int32, (tq, tq), 0) <= lax.broadcasted_iota(jnp.int32, (tq, tq), 1)

        def logits(j):
            return [_dot(ka_ref[h, j], qt_ref[h]) for h in range(FWD_HEADS)]

        def update(j, scores, carry, diagonal):
            new = []
            for h in range(FWD_HEADS):
                m, acc = carry[h]
                s = jnp.where(key_le_query, scores[h], NEG) if diagonal else scores[h]
                m_new = jnp.maximum(m, jnp.max(s, axis=0, keepdims=True))
                p = jnp.exp(s - m_new).astype(BF16)
                new.append((m_new, jnp.exp(m - m_new) * acc + _dot(vt_ref[h, j], p)))
            return tuple(new)

        init = tuple((jnp.full((1, tq), NEG, F32), jnp.zeros((LANE, tq), F32)) for _ in range(FWD_HEADS))
        carry = lax.fori_loop(0, i, lambda j, cr: update(j, logits(j), cr, False), init)
        outs, lse_rows = [], []
        for m, acc in update(i, logits(i), carry, True):
            l = acc[FOX_DH:FOX_DH + 1, :]
            outs.append((acc / l).T)
            lse_rows.append(m + jnp.log(l))
        lse_rows.append(jnp.zeros((LANE - FWD_HEADS, tq), F32))
        o_all = jnp.concatenate([_pair(outs[2 * c], outs[2 * c + 1], lane < FOX_DH) for c in range(FWD_PAIRS)], axis=1)
        mixed_ref[...] = o_all.astype(BF16)
        o_ref[...] = o_all
        lse_ref[...] = jnp.concatenate(lse_rows, axis=0).T

    width = FWD_PAIRS * LANE
    whole = pl.BlockSpec((FWD_HEADS, nq, tq, LANE), lambda g, i: (g, 0, 0, 0), pipeline_mode=pl.Buffered(1))
    whole_t = pl.BlockSpec((FWD_HEADS, nq, LANE, tq), lambda g, i: (g, 0, 0, 0), pipeline_mode=pl.Buffered(1))
    return _pcall(
        body, name=name, grid=(FWD_GROUPS, nq),
        in_specs=[pl.BlockSpec((FWD_HEADS, None, LANE, tq), lambda g, i: (g, i, 0, 0)), whole, whole_t,
                  pl.BlockSpec(memory_space=pl.ANY)],
        out_specs=[pl.BlockSpec((tq, width), lambda g, i: (i, RET_V // width + g)),
                   pl.BlockSpec((tq, width), lambda g, i: (i, g)),
                   pl.BlockSpec((None, tq, LANE), lambda g, i: (g, i, 0))],
        out_shape=[jax.ShapeDtypeStruct(mixed.shape, BF16), jax.ShapeDtypeStruct((t, FOX_W), F32),
                   jax.ShapeDtypeStruct((FWD_GROUPS, t, LANE), F32)],
        input_output_aliases={3: 0},
        compiler_params=_params("parallel", "parallel"),
    )(qt, ka, vt, mixed)


def _fox_prep_bwd(dmixed, o_fox, lse, qa, name):
    t = dmixed.shape[0]
    tq = TOK_TILE

    def body(dm_ref, o_ref, lse_ref, qa_ref, qab_ref, doa_ref):
        i = pl.program_id(0)
        lane = lax.broadcasted_iota(jnp.int32, (tq, LANE), 1)
        low = lane < FOX_DH
        live = (i * tq + lax.broadcasted_iota(jnp.int32, (tq, 1), 0)) >= N_PAD
        lse_parts = [_split3(jnp.where(live, -lse_ref[grp], 0.0)) for grp in range(FWD_GROUPS)]
        for pair in range(FOX_PAIRS):
            cols = slice(LANE * pair, LANE * (pair + 1))
            d_slab = dm_ref[:, cols]
            prod = d_slab * o_ref[:, cols]
            for e in range(2):
                h = 2 * pair + e
                nd = -jnp.sum(jnp.where(low, _head_to_low(prod, e), 0.0), axis=-1, keepdims=True)
                nd_hi = nd.astype(BF16).astype(F32)
                doa = jnp.where(low, _head_to_low(d_slab, e), 0.0)
                doa = jnp.where(lane == FOX_DH, nd_hi, jnp.where(lane == FOX_DH + 1, nd - nd_hi, doa))
                doa_ref[h] = doa.astype(BF16)
                lane_h = h % FWD_HEADS
                hi, mid, lo = [part[:, lane_h:lane_h + 1] for part in lse_parts[h // FWD_HEADS]]
                qab = qa_ref[h].astype(F32)
                qab = jnp.where(lane == L_LSE, hi, jnp.where(lane == L_LSE + 1, mid, jnp.where(lane == L_LSE + 2, lo, qab)))
                qab_ref[h] = qab.astype(BF16)

    out = jax.ShapeDtypeStruct((FOX_HEADS, t, LANE), BF16)
    hspec = pl.BlockSpec((FOX_HEADS, tq, LANE), lambda i: (0, i, 0))
    return _pcall(
        body, name=name, grid=(t // tq,),
        in_specs=[pl.BlockSpec((tq, FOX_W), lambda i: (i, 1)), pl.BlockSpec((tq, FOX_W), lambda i: (i, 0)),
                  pl.BlockSpec((FWD_GROUPS, tq, LANE), lambda i: (0, i, 0)), hspec],
        out_specs=[hspec, hspec], out_shape=[out, out],
        compiler_params=_params("parallel"),
    )(dmixed, o_fox, lse, qa)


def _fox_bwd(qab, doa, ka, va, dproj, name):
    nh, nq, tq, _ = qab.shape
    t = nq * tq
    slab = 3 * LANE * STEP_PAIRS
    group0 = (2 * RET_QK + 2 * RET_V) // slab

    def body(qab_ref, doa_ref, ka_ref, va_ref, dproj_in, dp_ref, drs_ref, dcs_ref, dq_ref):
        g, j = pl.program_id(0), pl.program_id(1)

        @pl.when((g == 0) & (j == 0))
        def _():
            drs_ref[...] = jnp.zeros_like(drs_ref)
            dcs_ref[...] = jnp.zeros_like(dcs_ref)

        @pl.when(j == 0)
        def _():
            dq_ref[...] = jnp.zeros_like(dq_ref)

        lane = lax.broadcasted_iota(jnp.int32, (tq, LANE), 1)
        low = lane < FOX_DH
        key_le_query = lax.broadcasted_iota(jnp.int32, (tq, tq), 0) <= lax.broadcasted_iota(jnp.int32, (tq, tq), 1)

        def by_head(c, a, b, col):
            h = STEP_HEADS * g + 2 * c
            return jnp.where(lane == h, a[:, col:col + 1], jnp.where(lane == h + 1, b[:, col:col + 1], 0.0))


        def step(i, carry, diagonal):
            st = [_dot(ka_ref[h], qab_ref[h, i], NT) for h in range(STEP_HEADS)]
            dpt = [_dot(va_ref[h], doa_ref[h, i], NT) for h in range(STEP_HEADS)]
            new = []
            for h in range(STEP_HEADS):
                p = jnp.exp(st[h])
                if diagonal:
                    p = jnp.where(key_le_query, p, 0.0)
                ds = (p * dpt[h]).astype(BF16)
                dq_ref[h, i] += _dot(ds, ka_ref[h], TN)
                dk, dv = carry[h]
                new.append((dk + _dot(ds, qab_ref[h, i]), dv + _dot(p.astype(BF16), doa_ref[h, i])))
            return tuple(new)

        zero = jnp.zeros((tq, LANE), F32)
        carry = step(j, tuple((zero, zero) for _ in range(STEP_HEADS)), True)
        carry = lax.fori_loop(j + 1, nq, lambda i, cr: step(i, cr, False), carry)
        rows = pl.ds(pl.multiple_of(j * tq, tq), tq)
        for c in range(STEP_PAIRS):
            (dka, dva), (dkb, dvb) = carry[2 * c], carry[2 * c + 1]
            c0 = 3 * LANE * c
            dp_ref[rows, c0 + LANE:c0 + 2 * LANE] = _pair(dka, dkb, low).astype(BF16)
            dp_ref[rows, c0 + 2 * LANE:c0 + 3 * LANE] = _pair(dva, dvb, low).astype(BF16)
            dcs_ref[rows, :] += by_head(c, dka, dkb, L_ONE_Q)

        @pl.when(j == nq - 1)
        def _():
            for c in range(STEP_PAIRS):
                for blk in range(nq):
                    r = slice(blk * tq, (blk + 1) * tq)
                    a, b = dq_ref[2 * c, blk], dq_ref[2 * c + 1, blk]
                    dp_ref[r, 3 * LANE * c:3 * LANE * c + LANE] = (_pair(a, b, low) * QK_SCALE).astype(BF16)
                    drs_ref[r, :] += by_head(c, a, b, L_ONE_K)

    whole = pl.BlockSpec((STEP_HEADS, nq, tq, LANE), lambda g, j: (g, 0, 0, 0), pipeline_mode=pl.Buffered(1))
    blk = pl.BlockSpec((STEP_HEADS, None, tq, LANE), lambda g, j: (g, j, 0, 0))
    sums = pl.BlockSpec((t, LANE), lambda g, j: (0, 0), pipeline_mode=pl.Buffered(1))
    return _pcall(
        body, name=name, grid=(FOX_GROUPS, nq),
        in_specs=[whole, whole, blk, blk, pl.BlockSpec(memory_space=pl.ANY)],
        out_specs=[pl.BlockSpec((t, slab), lambda g, j: (0, group0 + g)), sums, sums],
        out_shape=[jax.ShapeDtypeStruct(dproj.shape, BF16), jax.ShapeDtypeStruct((t, LANE), F32),
                   jax.ShapeDtypeStruct((t, LANE), F32)],
        input_output_aliases={4: 0},
        scratch_shapes=[pltpu.VMEM((STEP_HEADS, nq, tq, LANE), F32)],
        compiler_params=_params("arbitrary", "arbitrary"),
    )(qab, doa, ka, va, dproj)


HALO = 8


def _rows_ext(ref, r0, rows, t, before, after):
    lo, hi = r0 - before, r0 + rows + after
    width = ref.shape[-1]
    parts = []
    if lo < 0:
        parts.append(jnp.zeros((-lo, width), F32))
    parts.append(ref[max(lo, 0):min(hi, t), :].astype(F32))
    if hi > t:
        parts.append(jnp.zeros((hi - t, width), F32))
    return parts[0] if len(parts) == 1 else jnp.concatenate(parts, axis=0)


def _conv_taps(a_ext, r0_ext, cw_ref, cb_ref):
    n = a_ext.shape[0]
    if r0_ext < N_PAD:
        row = r0_ext + lax.broadcasted_iota(jnp.int32, (n, 1), 0)
        a_ext = jnp.where(row >= N_PAD, a_ext, 0.0)
    a1 = pltpu.roll(a_ext, 1, axis=0)
    a2 = pltpu.roll(a_ext, 2, axis=0)
    acc = cb_ref[...] + a2 * cw_ref[0:1, :] + a1 * cw_ref[1:2, :] + a_ext * cw_ref[2:3, :]
    return a_ext, a1, a2, acc


FF_COLS = 256


def _up_conv_fwd(n2, w_up_t, conv_w8, conv_b, name):
    t, d = n2.shape
    f = w_up_t.shape[1]
    rows = TOK_TILE
    starts = list(range(0, t, rows))

    def body(n_ref, wa_ref, wb_ref, cw_ref, cb_ref, up_ref, g_ref):
        def project(r0):
            n_rows = n_ref[r0:r0 + rows, :]
            up_ref[0, r0:r0 + rows, :] = _dot(n_rows, wa_ref[...], NT)
            up_ref[1, r0:r0 + rows, :] = _dot(n_rows, wb_ref[...], NT)

        def activate(r0):
            a_ext = _rows_ext(up_ref.at[0], r0, rows, t, HALO, 0)
            _, _, _, acc = _conv_taps(a_ext, r0 - HALO, cw_ref, cb_ref)
            acc = acc[HALO:, :]
            g_ref[r0:r0 + rows, :] = (acc * _sigmoid(acc) * up_ref[1, r0:r0 + rows, :]).astype(BF16)

        project(starts[0])
        for r0, r_next in zip(starts, starts[1:] + [None]):
            if r_next is not None:
                project(r_next)
            activate(r0)

    return _pcall(
        body, name=name, grid=(f // FF_COLS,),
        in_specs=[pl.BlockSpec((t, d), lambda j: (0, 0), pipeline_mode=pl.Buffered(1)),
                  pl.BlockSpec((None, FF_COLS, d), lambda j: (0, j, 0)), pl.BlockSpec((None, FF_COLS, d), lambda j: (1, j, 0)),
                  pl.BlockSpec((8, FF_COLS), lambda j: (0, j)), pl.BlockSpec((1, FF_COLS), lambda j: (0, j))],
        out_specs=[pl.BlockSpec((2, t, FF_COLS), lambda j: (0, 0, j)), pl.BlockSpec((t, FF_COLS), lambda j: (0, j))],
        out_shape=[jax.ShapeDtypeStruct((2, t, f), F32), jax.ShapeDtypeStruct((t, f), BF16)],
        compiler_params=_params("parallel"),
    )(n2, w_up_t, w_up_t, conv_w8, conv_b)


def _dg_conv_bwd(up, conv_w8, conv_b, dh2, w_down, name):
    _, t, f = up.shape
    d = dh2.shape[1]
    rows = TOK_TILE
    starts = list(range(0, t, rows))

    def body(a_ref, b_ref, cw_ref, cb_ref, dh_ref, wd_ref, dup_ref, gcw_ref, gcb_ref, dg_ref):
        def project(r0):
            dg_ref[r0:r0 + rows, :] = _dot(dh_ref[r0:r0 + rows, :], wd_ref[...], NT)

        gw = [jnp.zeros((1, FF_COLS), F32) for _ in range(3)]
        gb = jnp.zeros((1, FF_COLS), F32)
        project(starts[0])
        for r0, r_next in zip(starts, starts[1:] + [None]):
            if r_next is not None:
                project(r_next)
            a_ext = _rows_ext(a_ref, r0, rows, t, HALO, HALO)
            b_ext = _rows_ext(b_ref, r0, rows, t, HALO, HALO)
            dg_ext = _rows_ext(dg_ref, r0, rows, t, HALO, HALO)
            a0, a1, a2, acc = _conv_taps(a_ext, r0 - HALO, cw_ref, cb_ref)
            sg = _sigmoid(acc)
            dacc = dg_ext * b_ext * (sg * (1.0 + acc * (1.0 - sg)))
            n = dacc.shape[0]
            da = (dacc * cw_ref[2:3, :] + pltpu.roll(dacc, n - 1, axis=0) * cw_ref[1:2, :]
                  + pltpu.roll(dacc, n - 2, axis=0) * cw_ref[0:1, :])
            core = slice(HALO, HALO + rows)
            da = da[core, :]
            if r0 < N_PAD:
                row = r0 + lax.broadcasted_iota(jnp.int32, (rows, 1), 0)
                da = jnp.where(row >= N_PAD, da, 0.0)
            dup_ref[0, r0:r0 + rows, :] = da.astype(BF16)
            dup_ref[1, r0:r0 + rows, :] = (dg_ext * acc * sg)[core, :].astype(BF16)
            dacc_c = dacc[core, :]
            gw[0] = gw[0] + jnp.sum(dacc_c * a2[core, :], axis=0, keepdims=True)
            gw[1] = gw[1] + jnp.sum(dacc_c * a1[core, :], axis=0, keepdims=True)
            gw[2] = gw[2] + jnp.sum(dacc_c * a0[core, :], axis=0, keepdims=True)
            gb = gb + jnp.sum(dacc_c, axis=0, keepdims=True)
        gcw_ref[...] = jnp.zeros((8, FF_COLS), F32)
        for tap in range(3):
            gcw_ref[tap:tap + 1, :] = gw[tap]
        gcb_ref[...] = gb

    return _pcall(
        body, name=name, grid=(f // FF_COLS,),
        in_specs=[pl.BlockSpec((None, t, FF_COLS), lambda j: (0, 0, j)), pl.BlockSpec((None, t, FF_COLS), lambda j: (1, 0, j)),
                  pl.BlockSpec((8, FF_COLS), lambda j: (0, j)), pl.BlockSpec((1, FF_COLS), lambda j: (0, j)),
                  pl.BlockSpec((t, d), lambda j: (0, 0), pipeline_mode=pl.Buffered(1)),
                  pl.BlockSpec((FF_COLS, d), lambda j: (j, 0))],
        out_specs=[pl.BlockSpec((2, t, FF_COLS), lambda j: (0, 0, j)), pl.BlockSpec((8, FF_COLS), lambda j: (0, j)),
                   pl.BlockSpec((1, FF_COLS), lambda j: (0, j))],
        out_shape=[jax.ShapeDtypeStruct((2, t, f), BF16), jax.ShapeDtypeStruct((8, f), F32),
                   jax.ShapeDtypeStruct((1, f), F32)],
        scratch_shapes=[pltpu.VMEM((t, FF_COLS), F32)],
        compiler_params=_params("parallel"),
    )(up, up, conv_w8, conv_b, dh2, w_down)


def _exchange(arrays, kinds, name, after=None):
    n = len(arrays)
    npeer = N_DEV - 1
    n_in = n + int(after is not None)

    def body(*refs):
        ins, outs = refs[:n], refs[n_in:n_in + n]
        send_sems, recv_sems, local_sems = refs[n_in + n:]
        x, y, c = lax.axis_index("x"), lax.axis_index("y"), lax.axis_index("c")
        me = 4 * x + 2 * y + c
        copies, locals_ = [], []
        for a in range(n):
            gather = kinds[a] == "gather"
            own = pltpu.make_async_copy(ins[a] if gather else ins[a].at[me], outs[a].at[me], local_sems.at[a])
            own.start()
            locals_.append(own)
            for d in range(1, N_DEV):
                px = 1 - x if d & 4 else x
                py = 1 - y if d & 2 else y
                pc = 1 - c if d & 1 else c
                src = ins[a] if gather else ins[a].at[4 * px + 2 * py + pc]
                cp = pltpu.make_async_remote_copy(
                    src_ref=src, dst_ref=outs[a].at[me],
                    send_sem=send_sems.at[a * npeer + d - 1], recv_sem=recv_sems.at[a * npeer + d - 1],
                    device_id=(px, py, pc), device_id_type=pl.DeviceIdType.MESH)
                cp.start()
                copies.append(cp)
        for cp in copies:
            cp.wait_recv()
        for cp in copies:
            cp.wait_send()
        for own in locals_:
            own.wait()

    out_shape = [jax.ShapeDtypeStruct((N_DEV,) + (a.shape if k == "gather" else a.shape[1:]), a.dtype)
                 for a, k in zip(arrays, kinds)]
    return _pcall(
        body, name=name,
        in_specs=[pl.BlockSpec(memory_space=pl.ANY)] * n_in,
        out_specs=[pl.BlockSpec(memory_space=pl.ANY)] * n,
        out_shape=out_shape,
        scratch_shapes=[pltpu.SemaphoreType.DMA((n * npeer,)), pltpu.SemaphoreType.DMA((n * npeer,)),
                        pltpu.SemaphoreType.DMA((n,))],
        compiler_params=pltpu.CompilerParams(has_side_effects=True),
    )(*arrays, *([] if after is None else [after]))


ALL_PEERS = tuple(range(1, N_DEV))
SAME_CORE_AND_SIBLING = (1, 2, 4, 6)
OTHER_CHIPS = (2, 4, 6)


def _peer_copies(srcs, lands, kinds, send_sems, recv_sems, relations=ALL_PEERS):
    x, y, c = lax.axis_index("x"), lax.axis_index("y"), lax.axis_index("c")
    me = 4 * x + 2 * y + c
    copies = []
    for a in range(len(srcs)):
        for d in relations:
            px = 1 - x if d & 4 else x
            py = 1 - y if d & 2 else y
            pc = 1 - c if d & 1 else c
            peer = 4 * px + 2 * py + pc
            k = a * (N_DEV - 1) + d - 1
            if kinds[a] == "forward":
                src, dst, target = lands[a].at[peer], lands[a].at[peer], (x, y, 1 - c)
            else:
                src, dst, target = (srcs[a] if kinds[a] == "gather" else srcs[a].at[peer]), lands[a].at[me], (px, py, pc)
            copies.append(pltpu.make_async_remote_copy(
                src_ref=src, dst_ref=dst, send_sem=send_sems.at[k], recv_sem=recv_sems.at[k],
                device_id=target, device_id_type=pl.DeviceIdType.MESH))
    return copies


def _own_copies(srcs, lands, kinds, sems):
    me = 4 * lax.axis_index("x") + 2 * lax.axis_index("y") + lax.axis_index("c")
    first = len(srcs) * (N_DEV - 1)
    return [pltpu.make_async_copy(srcs[a].at[me] if kinds[a] == "scatter" else srcs[a], lands[a].at[me], sems.at[first + a])
            for a in range(len(srcs))]


def _exchange_start(arrays, kinds, name, after=None, relations=ALL_PEERS, lands=None, own=True):
    n = len(arrays)
    nsem = n * (N_DEV - 1) + n
    hbm = pl.BlockSpec(memory_space=pltpu.HBM)
    sem = pl.BlockSpec(memory_space=pltpu.SEMAPHORE)
    land_shapes = ([l.shape for l in lands] if lands is not None else
                   [(N_DEV,) + (a.shape if k == "gather" else a.shape[1:]) for a, k in zip(arrays, kinds)])

    n_in = 2 * n + int(after is not None)

    def body(*refs):
        srcs, land_refs = refs[:n], refs[n:2 * n]
        send_sems, recv_sems = refs[n_in], refs[n_in + 1]
        token = refs[-1]
        for cp in _peer_copies(srcs, land_refs, kinds, send_sems, recv_sems, relations):
            cp.start()
        for cp in _own_copies(srcs, land_refs, kinds, send_sems) if own else []:
            cp.start()
        token[...] = jnp.zeros_like(token)

    operands = [pltpu.with_memory_space_constraint(a, pltpu.HBM) for a in arrays]
    operands += (list(lands) if lands is not None else
                 [pltpu.with_memory_space_constraint(lax.empty(s, a.dtype), pltpu.HBM) for s, a in zip(land_shapes, arrays)])
    operands += [] if after is None else [after]
    out = _pcall(
        body, name=name,
        in_specs=[hbm] * (2 * n) + ([] if after is None else [pl.BlockSpec(memory_space=pl.ANY)]),
        out_specs=[sem, sem] + [hbm] * (2 * n) + [pl.BlockSpec(memory_space=pltpu.VMEM)],
        out_shape=[pltpu.SemaphoreType.DMA((nsem,)), pltpu.SemaphoreType.DMA((nsem,))]
        + [pltpu.HBM(a.shape, a.dtype) for a in arrays]
        + [pltpu.HBM(s, a.dtype) for s, a in zip(land_shapes, arrays)]
        + [jax.ShapeDtypeStruct((8, LANE), F32)],
        input_output_aliases={k: 2 + k for k in range(2 * n)},
        compiler_params=pltpu.CompilerParams(has_side_effects=pltpu.SideEffectType.DATAFLOW_SIDE_EFFECTING),
    )(*operands)
    return out[0], out[1], list(out[2:2 + n]), list(out[2 + n:2 + 2 * n]), out[-1]


def _exchange_wait(started, kinds, after, name, own=True, relations=ALL_PEERS, with_sources=False):
    send_sems, recv_sems, srcs, lands, _ = started
    n = len(srcs)
    hbm = pl.BlockSpec(memory_space=pltpu.HBM)
    sem = pl.BlockSpec(memory_space=pltpu.SEMAPHORE)

    def body(*refs):
        src_refs, land_refs = refs[:n], refs[n:2 * n]
        copies = _peer_copies(src_refs, land_refs, kinds, refs[2 * n], refs[2 * n + 1], relations)
        for cp in copies:
            cp.wait_send()
        for cp in copies:
            cp.wait_recv()
        for cp in _own_copies(src_refs, land_refs, kinds, refs[2 * n]) if own else []:
            cp.wait()

    out = _pcall(
        body, name=name,
        in_specs=[hbm] * (2 * n) + [sem, sem, pl.BlockSpec(memory_space=pl.ANY)],
        out_specs=[hbm] * (2 * n),
        out_shape=[pltpu.HBM(a.shape, a.dtype) for a in srcs + lands],
        input_output_aliases={k: k for k in range(2 * n)},
        compiler_params=pltpu.CompilerParams(has_side_effects=pltpu.SideEffectType.DATAFLOW_SIDE_EFFECTING),
    )(*srcs, *lands, send_sems, recv_sems, after)
    return (list(out[:n]), list(out[n:])) if with_sources else list(out[n:])


def _sum_slots(slots, name, rows_tile):
    nd, r, c = slots.shape

    def body(s_ref, o_ref):
        acc = s_ref[0].astype(F32)
        for p in range(1, nd):
            acc = acc + s_ref[p].astype(F32)
        o_ref[...] = acc

    return _pcall(
        body, name=name, grid=(r // rows_tile,),
        in_specs=[pl.BlockSpec((nd, rows_tile, c), lambda i: (0, i, 0))],
        out_specs=pl.BlockSpec((rows_tile, c), lambda i: (i, 0)),
        out_shape=jax.ShapeDtypeStruct((r, c), F32),
        compiler_params=_params("parallel"),
    )(slots)


def _sum_slots_small(slot_arrays, own_arrays, name):
    n = len(slot_arrays)

    def body(*refs):
        me = 4 * lax.axis_index("x") + 2 * lax.axis_index("y") + lax.axis_index("c")
        for s_ref, own_ref, o_ref in zip(refs[:n], refs[n:2 * n], refs[2 * n:]):
            acc = jnp.where(me == 0, own_ref[...], s_ref[0])
            for p in range(1, s_ref.shape[0]):
                acc = acc + jnp.where(me == p, own_ref[...], s_ref[p])
            o_ref[...] = acc

    return _pcall(body, name=name, out_shape=[jax.ShapeDtypeStruct(a.shape[1:], F32) for a in slot_arrays])(
        *slot_arrays, *own_arrays)


def _adamw_values(w, gr, m, v):
    nm = ADAM_B1 * m + (1.0 - ADAM_B1) * gr
    nv = ADAM_B2 * v + (1.0 - ADAM_B2) * (gr * gr)
    m_hat = nm / (1.0 - ADAM_B1 ** ADAM_STEP)
    v_hat = nv / (1.0 - ADAM_B2 ** ADAM_STEP)
    return -ADAM_LR * (m_hat / (jnp.sqrt(v_hat) + ADAM_EPS) + ADAM_WD * w), nm, nv


def _adamw_update(w_ref, g_ref, m_ref, v_ref, d_ref, nm_ref, nv_ref):
    d_ref[...], nm_ref[...], nv_ref[...] = _adamw_values(w_ref[...], g_ref[...], m_ref[...], v_ref[...])


def _adamw_from_slots(w, slots, m, v, name, cols_tile=2 * LANE):
    rows, cols = w.shape
    nd, rows_pad, _ = slots.shape

    def body(w_ref, s_ref, m_ref, v_ref, g_ref, d_ref, nm_ref, nv_ref):
        gr = s_ref[0, 0:rows, :].astype(F32)
        for p in range(1, nd):
            gr = gr + s_ref[p, 0:rows, :].astype(F32)
        g_ref[...] = gr
        d_ref[...], nm_ref[...], nv_ref[...] = _adamw_values(w_ref[...], gr, m_ref[...], v_ref[...])

    spec = pl.BlockSpec((rows, cols_tile), lambda i: (0, i))
    slot_spec = pl.BlockSpec((nd, rows_pad, cols_tile), lambda i: (0, 0, i))
    return _pcall(
        body, name=name, grid=(cols // cols_tile,), in_specs=[spec, slot_spec, spec, spec], out_specs=[spec] * 4,
        out_shape=[jax.ShapeDtypeStruct((rows, cols), F32)] * 4, compiler_params=_params("parallel"),
    )(w, slots, m, v)


def _adamw_small(ws, gs, ms, vs, name):
    n = len(ws)

    def body(*refs):
        ins, outs = refs[:4 * n], refs[4 * n:]
        for k in range(n):
            _adamw_update(ins[k], ins[n + k], ins[2 * n + k], ins[3 * n + k], outs[k], outs[n + k], outs[2 * n + k])

    shapes = [jax.ShapeDtypeStruct(w.shape, F32) for w in ws]
    out = _pcall(body, name=name, out_shape=shapes * 3)(*ws, *gs, *ms, *vs)
    return list(out[:n]), list(out[n:2 * n]), list(out[2 * n:])


def _adamw(w, g, m, v, name, rows_tile):
    r, c = w.shape
    body = lambda *refs: _adamw_update(*refs)
    spec = pl.BlockSpec((rows_tile, c), lambda i: (i, 0))
    shp = jax.ShapeDtypeStruct((r, c), F32)
    return _pcall(
        body, name=name, grid=(r // rows_tile,), in_specs=[spec] * 4, out_specs=[spec] * 3, out_shape=[shp] * 3,
        compiler_params=_params("parallel"),
    )(w, g, m, v)


F0 = 2 * RET_QK + 2 * RET_V


def _to_internal_rows(w_t):
    cols = w_t.shape[1]
    fox = w_t[F0:F0 + 3 * FOX_W].reshape(3, FOX_PAIRS, LANE, cols).transpose(1, 0, 2, 3).reshape(3 * FOX_W, cols)
    tail = jnp.zeros((IN_PAD - IN_WIDTH, cols), w_t.dtype)
    return jnp.concatenate([w_t[:F0], fox, w_t[F0 + 3 * FOX_W:], tail], axis=0)


def _from_internal_rows(g_t):
    cols = g_t.shape[1]
    fox = g_t[F0:F0 + 3 * FOX_W].reshape(FOX_PAIRS, 3, LANE, cols).transpose(1, 0, 2, 3).reshape(3 * FOX_W, cols)
    return jnp.concatenate([g_t[:F0], fox, g_t[F0 + 3 * FOX_W:F0 + 3 * FOX_W + FOX_HEADS]], axis=0)


IN_BLOCK = IN_WIDTH // N_DEV
IN_BLOCK_PAD = 400
BF16_ROWS = 16


def _slot_row_of_internal():
    rows = np.arange(IN_WIDTH)
    fox = rows[F0:F0 + 3 * FOX_W].reshape(3, FOX_PAIRS, LANE).transpose(1, 0, 2).reshape(-1)
    original = np.concatenate([rows[:F0], fox, rows[F0 + 3 * FOX_W:]])
    slot_rows = original // IN_BLOCK * IN_BLOCK_PAD + original % IN_BLOCK
    return np.concatenate([slot_rows, np.full(IN_PAD - IN_WIDTH, -1)])


def _internal_row_of_slot():
    forward = _slot_row_of_internal()
    back = np.full(N_DEV * IN_BLOCK_PAD, -1)
    back[forward[forward >= 0]] = np.nonzero(forward >= 0)[0]
    return back


def _row_runs(src_of_dst):
    tiles = []
    for t0 in range(0, len(src_of_dst), LANE):
        runs = []
        for o in range(LANE):
            s = int(src_of_dst[t0 + o])
            if s < 0:
                continue
            if runs and runs[-1][0] + runs[-1][2] == o and runs[-1][1] + runs[-1][2] == s:
                runs[-1][2] += 1
            else:
                runs.append([o, s, 1])
        tiles.append(runs)
    return tiles


def _move_rows(src, src_of_dst, name):
    n_src, cols = src.shape
    tiles = _row_runs(src_of_dst)

    def body(s_ref, o_ref):
        for t, runs in enumerate(tiles):
            rows = pl.ds(t * LANE, LANE)
            if not runs:
                o_ref[rows, :] = jnp.zeros((LANE, cols), o_ref.dtype)
                continue
            if len(runs) == 1 and runs[0][0] == 0 and runs[0][2] == LANE and runs[0][1] % BF16_ROWS == 0:
                o_ref[rows, :] = s_ref[pl.ds(runs[0][1], LANE), :]
                continue
            acc = None
            for o0, s0, n in runs:
                w0 = s0 // BF16_ROWS * BF16_ROWS
                width = -(-(s0 - w0 + n) // LANE) * LANE
                w0 = min(w0, n_src - width)
                i = lax.broadcasted_iota(jnp.int32, (LANE, width), 0)
                j = lax.broadcasted_iota(jnp.int32, (LANE, width), 1)
                pick = ((j - i == s0 - w0 - o0) & (i >= o0) & (i < o0 + n)).astype(src.dtype)
                part = _dot(pick, s_ref[pl.ds(w0, width), :])
                acc = part if acc is None else acc + part
            o_ref[rows, :] = acc.astype(o_ref.dtype)

    return _pcall(body, name=name, out_shape=jax.ShapeDtypeStruct((len(src_of_dst), cols), src.dtype))(src)


def _local_step(x, target, meta, attn_g, fox_b, ret_g, ffn_g, conv_w8, conv_b, final_g,
                first_weight, late_weights, ffn_grads_ready, out_grad_ready, in_grad_ready):
    seq, d = x.shape
    t = seq + PREFIX
    tm = TOK_TILE
    nq = t // tm
    fox_b128 = jnp.pad(fox_b, ((0, 0), (0, LANE - FOX_HEADS)))

    h0, n1 = _prep_norm(x, meta, attn_g, "prep_norm")
    w_in_t = first_weight(n1)
    proj = _mm_simple(n1, w_in_t, mode="nt", tm=tm, tn=IN_PAD, tk=d, out_dtype=F32, name="mm_in")
    cos, sin = _rope_tables(t)
    o_pre, mixed, states = _ret_fwd(proj, cos, sin, ret_g, "ret_fwd")
    c = _forget_cumsum(proj, fox_b128, "forget_cumsum")
    qa, ka, va, qt, vt = _fox_prep(proj, c, "fox_prep")
    by_block = lambda a: a.reshape(FOX_HEADS, nq, tm, LANE)
    mixed, o_fox, lse = _fox_fwd(qt, by_block(ka), vt, mixed, "fox_fwd")
    w_out, w_up_t, w_down = late_weights(o_fox)
    tile = pl.BlockSpec((tm, d), lambda i: (i, 0))
    row_vec = pl.BlockSpec((1, d), lambda i: (0, 0))
    resident = lambda shape: pl.BlockSpec(shape, lambda i: (0,) * len(shape), pipeline_mode=pl.Buffered(1))
    acts = lambda dtype: jax.ShapeDtypeStruct((t, d), dtype)
    vec = jax.ShapeDtypeStruct((1, d), F32)

    def residual_and_norm(i, acc, ins, outs):
        h = acc + ins[0][...]
        outs[0][...] = h
        outs[1][...] = (h * lax.rsqrt(jnp.mean(h * h, axis=-1, keepdims=True) + EPS) * ins[1][...]).astype(BF16)

    h1, n2 = _matmul_rows([mixed], [tile], [w_out], [resident((d, d))], [h0, ffn_g], [tile, row_vec],
                          [tile, tile], [acts(F32), acts(BF16)], residual_and_norm, mode="nn", steps=nq, name="mm_out_norm")
    nf = D_FF // 1408
    up, g = _up_conv_fwd(n2, w_up_t, conv_w8, conv_b, "up_conv_fwd")

    def residual_loss_bwd(i, acc, ins, outs):
        loss_ref, dh_ref, dhb_ref, gg_ref = outs
        part, dh, gg = _loss_tile(i, acc + ins[0][...], jnp.concatenate([ins[1][...], ins[2][...], ins[3][...]], axis=0),
                                  ins[4][...])
        _accumulate(loss_ref, i, jnp.broadcast_to(part, loss_ref.shape))
        dh_ref[...] = dh
        dhb_ref[...] = dh.astype(BF16)
        _accumulate(gg_ref, i, gg)

    loss_tile, dh2, dh2_b, g_final = _matmul_rows(
        [g], [pl.BlockSpec((tm, D_FF), lambda i: (i, 0))], [w_down], [resident((D_FF, d))],
        [h1, target, target, target, final_g], [tile] + _shifted_row_specs(d) + [row_vec],
        [pl.BlockSpec((8, LANE), lambda i: (0, 0)), tile, tile, row_vec],
        [jax.ShapeDtypeStruct((8, LANE), F32), acts(F32), acts(BF16), vec], residual_loss_bwd,
        mode="nn", steps=nq, name="mm_down_loss")

    tkw = 2112 if t % 2112 == 0 else tm
    gw_down = _mm_simple(g, dh2_b, mode="tn", tm=1408, tn=d, tk=t, out_dtype=BF16, name="mm_gw_down")
    dup, g_conv_w8, g_conv_b = _dg_conv_bwd(up, conv_w8, conv_b, dh2_b, w_down, "dg_conv_bwd")

    half = lambda p: pl.BlockSpec((None, tm, D_FF), lambda i: (p, i, 0))
    half_w = lambda p: pl.BlockSpec((None, D_FF, d), lambda i: (p, 0, 0), pipeline_mode=pl.Buffered(1))
    gw_up_t = _matmul(
        dup, n2, mode="tn", grid=(2 * nf, 1, t // tkw),
        a_spec=pl.BlockSpec((None, tkw, 1408), lambda i, j, k: (i // nf, k, i % nf)),
        b_spec=pl.BlockSpec((tkw, d), lambda i, j, k: (k, 0)),
        o_spec=pl.BlockSpec((1408, d), lambda i, j, k: (i, 0)),
        out_shape=jax.ShapeDtypeStruct((2 * D_FF, d), BF16), name="mm_gw_up")
    def norm_bwd_and_mixer_grad(i, acc, ins, outs):
        dh, gg = _rms_bwd_tile(acc, ins[0][...], ins[1][...], ins[2][...])
        outs[0][...] = dh
        _accumulate(outs[1], i, gg)
        outs[2][...] = _dot(dh.astype(BF16), ins[3][...], NT)

    dh1, g_ffn, dmixed = _matmul_rows(
        [dup, dup], [half(0), half(1)], [w_up_t, w_up_t], [half_w(0), half_w(1)],
        [h1, ffn_g, dh2, w_out], [tile, row_vec, tile, resident((d, d))], [tile, row_vec, tile],
        [acts(F32), vec, acts(F32)], norm_bwd_and_mixer_grad,
        mode="nn", steps=nq, name="mm_dn2_norm_bwd", after=ffn_grads_ready(gw_down, gw_up_t))
    gw_out = _mm_simple(mixed, dh1, mode="tn", tm=d, tn=d, tk=tkw, out_dtype=BF16, name="mm_gw_out")
    dproj, g_ret = _ret_bwd(proj, cos, sin, ret_g + out_grad_ready(gw_out), dmixed, o_pre, states, "ret_bwd")
    qab, doa = _fox_prep_bwd(dmixed, o_fox, lse, qa, "fox_prep_bwd")
    dproj, drs, dcs = _fox_bwd(by_block(qab), by_block(doa), by_block(ka), by_block(va), dproj, "fox_bwd")
    dproj, g_fox_b = _forget_cumsum_bwd(proj, fox_b128, drs, dcs, dproj, "forget_cumsum_bwd")
    gw_in_t = _mm_simple(dproj, n1, mode="tn", tm=640, tn=d, tk=t, out_dtype=BF16, name="mm_gw_in")
    sent = in_grad_ready(gw_in_t)
    def input_grads(i, acc, ins, outs):
        gx_ref, gmeta_ref, gg_ref, buf_ref, sems = outs
        dh, gg = _rms_bwd_tile(acc, ins[0][...], ins[1][...], ins[2][...])
        _accumulate(gg_ref, i, gg)
        slot = i % 2

        def first_copy():
            return pltpu.make_async_copy(buf_ref.at[0, pl.ds(PREFIX, tm - PREFIX)], gx_ref.at[pl.ds(0, tm - PREFIX)],
                                         sems.at[0])

        def tile_copy(tile, buf_slot):
            rows = pl.ds(pl.multiple_of(tile * tm - PREFIX, PREFIX), tm)
            return pltpu.make_async_copy(buf_ref.at[buf_slot], gx_ref.at[rows], sems.at[buf_slot])

        @pl.when(i == 1)
        def _():
            first_copy().wait()

        @pl.when(i >= 2)
        def _():
            tile_copy(i - 1, 1 - slot).wait()

        buf_ref[slot] = dh

        @pl.when(i == 0)
        def _():
            gmeta_ref[...] = dh[N_PAD:PREFIX, :]
            first_copy().start()

        @pl.when(i > 0)
        def _():
            tile_copy(i, slot).start()

        @pl.when(i == nq - 1)
        def _():
            tile_copy(i, slot).wait()

    grad_x, g_meta, g_attn = _matmul_rows(
        [dproj], [pl.BlockSpec((tm, IN_PAD), lambda i: (i, 0))], [w_in_t], [resident((IN_PAD, d))],
        [h0, attn_g, dh1], [tile, row_vec, tile],
        [pl.BlockSpec(memory_space=pl.ANY), pl.BlockSpec((N_META, d), lambda i: (0, 0)), row_vec],
        [jax.ShapeDtypeStruct((seq, d), F32), jax.ShapeDtypeStruct((N_META, d), F32), vec], input_grads,
        mode="nn", steps=nq, name="mm_dn1_norm_bwd", after=sent,
        scratch=[pltpu.VMEM((2, tm, d), F32), pltpu.SemaphoreType.DMA((2,))])

    grads = dict(meta=g_meta, attn_g=g_attn, fox_b=g_fox_b, ret_g=g_ret,
                 ffn_g=g_ffn, conv_w=g_conv_w8, conv_b=g_conv_b, final_g=g_final)
    return loss_tile, grad_x, grads


def kernel(x, meta_tokens, attn_norm_g, w_in, fox_forget_b, ret_norm_g, w_out, ffn_norm_g, w_up, conv_w, conv_b, w_down, final_norm_g, loss_target, m_meta_tokens, m_attn_norm_g, m_w_in, m_fox_forget_b, m_ret_norm_g, m_w_out, m_ffn_norm_g, m_w_up, m_conv_w, m_conv_b, m_w_down, m_final_norm_g, v_meta_tokens, v_attn_norm_g, v_w_in, v_fox_forget_b, v_ret_norm_g, v_w_out, v_ffn_norm_g, v_w_up, v_conv_w, v_conv_b, v_w_down, v_final_norm_g):
    d = D_MODEL
    me = 4 * lax.axis_index("x") + 2 * lax.axis_index("y") + lax.axis_index("c")
    in_blk, in_blk_pad = IN_BLOCK, IN_BLOCK_PAD
    up_blk = 2 * D_FF // N_DEV
    down_blk = D_FF // N_DEV
    cw_blk = D_FF // N_DEV

    w_in_loc = jnp.pad(w_in[0].T.astype(BF16), ((0, in_blk_pad - in_blk), (0, 0)))
    cw_loc = jnp.pad(conv_w[0], ((0, 5), (0, 384 - cw_blk)))
    g_meta, g_cw = _exchange([meta_tokens, cw_loc], ["gather"] * 2, "gather_small")
    first = _exchange_start([w_in_loc], ["gather"], "gather_in_start", after=g_meta, relations=SAME_CORE_AND_SIBLING)
    rest_loc = [(w_out[0] + first[-1][0:1, 0:1]).astype(BF16), w_up[0].T.astype(BF16), w_down[0].astype(BF16)]
    rest = _exchange_start(rest_loc, ["gather"] * 3, "gather_rest_start")
    meta_f = g_meta.transpose(1, 0, 2).reshape(N_META, d)
    conv_w8 = jnp.pad(g_cw[:, :3, :cw_blk].transpose(1, 0, 2).reshape(3, D_FF), ((0, 5), (0, 0)))
    pending = {}

    def first_weight(after):
        own_in, landed = _exchange_wait(first, ["gather"], after, "gather_in_wait", relations=SAME_CORE_AND_SIBLING,
                                        with_sources=True)
        onward = _exchange_start(own_in, ["forward"], "gather_in_forward_start", relations=OTHER_CHIPS, lands=landed,
                                 own=False)
        (g_in,) = _exchange_wait(onward, ["forward"], onward[-1], "gather_in_forward_wait", own=False,
                                 relations=OTHER_CHIPS)
        return _move_rows(g_in.reshape(IN_PAD, d), _slot_row_of_internal(), "w_in_rows")

    def in_grad_ready(gw_in_t):
        blocks = _move_rows(gw_in_t, _internal_row_of_slot(), "gw_in_rows").reshape(N_DEV, in_blk_pad, d)
        pending["in"] = _exchange_start([blocks], ["scatter"], "grads_in_start")
        return pending["in"][-1][0:1, 0:1]

    def late_weights(after):
        g_out, g_up, g_down = _exchange_wait(rest, ["gather"] * 3, after, "gather_rest_wait")
        return g_out.reshape(d, d), g_up.reshape(2, D_FF, d), g_down.reshape(D_FF, d)

    def ffn_grads_ready(gw_down, gw_up_t):
        pending["ffn_blocks"] = [gw_down.reshape(N_DEV, down_blk, d), gw_up_t.reshape(N_DEV, up_blk, d)]
        return gw_down[0:1, 0:1]

    def out_grad_ready(gw_out):
        blocks = pending["ffn_blocks"] + [gw_out.reshape(N_DEV, d // N_DEV, d)]
        pending["mid"] = _exchange_start(blocks, ["scatter"] * 3, "grads_mid_start")
        return pending["mid"][-1][0:1, 0:1]

    loss_tile, grad_x, gr = _local_step(
        x[0], loss_target[0], meta_f, attn_norm_g + rest[-1][0:1, 0:1], fox_forget_b, ret_norm_g, ffn_norm_g,
        conv_w8, conv_b, final_norm_g.reshape(1, d), first_weight, late_weights, ffn_grads_ready, out_grad_ready,
        in_grad_ready)

    small = [loss_tile, gr["attn_g"], gr["fox_b"], gr["ret_g"], gr["ffn_g"], gr["conv_b"], gr["final_g"],
             gr["meta"], gr["conv_w"]]
    small_kinds = ["gather"] * len(small)
    small_started = _exchange_start(small, small_kinds, "grads_small_start", own=False)

    r_down, r_up, r_out = _exchange_wait(pending["mid"], ["scatter"] * 3, small_started[-1], "grads_mid_wait")
    g_w_out = _sum_slots(r_out, "sum_w_out", d // N_DEV)
    g_w_up_t = _sum_slots(r_up, "sum_w_up", up_blk)
    g_w_down = _sum_slots(r_down, "sum_w_down", down_blk)
    as_t = lambda a: a[0].T
    from_t = lambda a: a.T[None]
    d_w_out, m_w_out_n, v_w_out_n = [a[None] for a in _adamw(w_out[0], g_w_out, m_w_out[0], v_w_out[0], "adamw_w_out", 128)]
    up_t = _adamw(as_t(w_up), g_w_up_t, as_t(m_w_up), as_t(v_w_up), "adamw_w_up", up_blk // 2)
    d_w_up, m_w_up_n, v_w_up_n = [from_t(a) for a in up_t]
    d_w_down, m_w_down_n, v_w_down_n = [a[None] for a in _adamw(w_down[0], g_w_down, m_w_down[0], v_w_down[0],
                                                                "adamw_w_down", down_blk)]

    own_small, r_small = _exchange_wait(small_started, small_kinds, up_t[0], "grads_small_wait", own=False,
                                        with_sources=True)
    (loss_all, g_attn, g_fox_b128, g_ret, g_ffn, g_conv_b, g_final, g_meta_full, g_cw_full) = _sum_slots_small(
        r_small, own_small, "sum_small")
    loss = loss_all[0, 0]
    g_fox_b = g_fox_b128[:, :FOX_HEADS]
    g_meta_loc = lax.dynamic_slice(g_meta_full, (0, me * (d // N_DEV)), (N_META, d // N_DEV))
    g_cw_loc = lax.dynamic_slice(g_cw_full, (0, me * cw_blk), (3, cw_blk))

    (r_in,) = _exchange_wait(pending["in"], ["scatter"], r_small[0], "grads_in_wait")
    g_w_in, d_w_in, m_w_in_n, v_w_in_n = [from_t(a) for a in _adamw_from_slots(
        as_t(w_in), r_in, as_t(m_w_in), as_t(v_w_in), "adamw_w_in")]
    g_w_in, g_w_up = g_w_in[0], g_w_up_t.T
    row = lambda a: a.reshape(1, d)
    sm_grads = [g_meta_loc, g_attn, g_fox_b, g_ret, g_ffn, g_cw_loc, g_conv_b, g_final]
    sm_w = [meta_tokens, attn_norm_g, fox_forget_b, ret_norm_g, ffn_norm_g, conv_w[0], conv_b, row(final_norm_g)]
    sm_m = [m_meta_tokens, m_attn_norm_g, m_fox_forget_b, m_ret_norm_g, m_ffn_norm_g, m_conv_w[0], m_conv_b,
            row(m_final_norm_g)]
    sm_v = [v_meta_tokens, v_attn_norm_g, v_fox_forget_b, v_ret_norm_g, v_ffn_norm_g, v_conv_w[0], v_conv_b,
            row(v_final_norm_g)]
    dl, ml, vl = [lst[:7] + [lst[7].reshape(d)] for lst in _adamw_small(sm_w, sm_grads, sm_m, sm_v, "adamw_small")]

    def by_weight(meta_, attn_, w_in_, fox_, ret_, w_out_, ffn_, w_up_, cw_, cb_, w_down_, final_):
        return (meta_, attn_, w_in_, fox_, ret_, w_out_, ffn_, w_up_, cw_[None], cb_, w_down_, final_)

    grads_out = by_weight(g_meta_loc, g_attn, g_w_in[None], g_fox_b, g_ret, g_w_out[None], g_ffn, g_w_up[None], g_cw_loc,
                          g_conv_b, g_w_down[None], g_final.reshape(d))
    delta_out = by_weight(dl[0], dl[1], d_w_in, dl[2], dl[3], d_w_out, dl[4], d_w_up, dl[5], dl[6], d_w_down, dl[7])
    m_out = by_weight(ml[0], ml[1], m_w_in_n, ml[2], ml[3], m_w_out_n, ml[4], m_w_up_n, ml[5], ml[6], m_w_down_n, ml[7])
    v_out = by_weight(vl[0], vl[1], v_w_in_n, vl[2], vl[3], v_w_out_n, vl[4], v_w_up_n, vl[5], vl[6], v_w_down_n, vl[7])
    return (loss, grad_x[None]) + grads_out + delta_out + m_out + v_out
```
